```python
import jax, jax.numpy as jnp
from jax import lax
import numpy as np

D_MODEL = 1024
BATCH = 16
SEQ = 2048
DEPTH = 1

POOL_WIDTH = D_MODEL // 2
POOL_WINDOWS = (2, 4, 8, 16)
N_POOL_GROUPS = len(POOL_WINDOWS)
POOL_GROUP = POOL_WIDTH // N_POOL_GROUPS
SSD_HEAD_DIM = 64
SSD_INNER = D_MODEL
SSD_HEADS = SSD_INNER // SSD_HEAD_DIM
SSD_GROUPS = 2
SSD_HPG = SSD_HEADS // SSD_GROUPS
SSD_STATE = 128
CONV_WIDTH = 4
CHUNK = 128
CONV_CH = SSD_INNER + 2 * SSD_GROUPS * SSD_STATE
MIX_WIDTH = POOL_WIDTH + SSD_INNER
OFF_POOL = 0
OFF_Z = OFF_POOL + POOL_WIDTH
OFF_XBC = OFF_Z + SSD_INNER
OFF_DT = OFF_XBC + CONV_CH
IN_WIDTH = OFF_DT + SSD_HEADS
D_FF = 4 * D_MODEL
N_MOD = 6
EPS = 1e-5

kernel_name = "hybrid_pool_ssd_adaln_block"


def rms_norm(x, g):
    x32 = x.astype(jnp.float32)
    y = x32 * lax.rsqrt(jnp.mean(x32 * x32, axis=-1, keepdims=True) + EPS)
    return (y * g.astype(jnp.float32)).astype(x.dtype)


def pool_mixer(u, w_pool, pool_scale):
    Bsz, S, _ = u.shape
    u32 = u.astype(jnp.float32)
    cs = jnp.cumsum(u32, axis=1)
    t = jnp.arange(1, S + 1, dtype=jnp.float32)[None, :, None]
    outs = []
    for gi, w in enumerate(POOL_WINDOWS):
        sl = slice(gi * POOL_GROUP, (gi + 1) * POOL_GROUP)
        cs_g = cs[..., sl]
        prev = jnp.pad(cs_g, ((0, 0), (w, 0), (0, 0)))[:, :S]
        mean = (cs_g - prev) / jnp.minimum(t, float(w))
        outs.append(mean - u32[..., sl])
    p = jnp.stack(outs, axis=2).astype(u.dtype)
    y = jnp.einsum('bsgc,gcd->bsgd', p, w_pool).reshape(Bsz, S, POOL_WIDTH)
    return y * pool_scale


def causal_depthwise_conv(u, w, b):
    K = w.shape[0]
    S = u.shape[1]
    up = jnp.pad(u, ((0, 0), (K - 1, 0), (0, 0)))
    out = b + up[:, 0:S] * w[0]
    for k in range(1, K):
        out = out + up[:, k:k + S] * w[k]
    return out


def ssd_chunked(xs, dt, a, bm, cm):
    Bsz, S, G, R, P = xs.shape
    N = bm.shape[-1]
    nc = S // CHUNK
    xdt = (xs.astype(jnp.float32) * dt[..., None]).reshape(Bsz, nc, CHUNK, G, R, P)
    da = (dt * a).reshape(Bsz, nc, CHUNK, G, R)
    bc = bm.astype(jnp.float32).reshape(Bsz, nc, CHUNK, G, N)
    cc = cm.astype(jnp.float32).reshape(Bsz, nc, CHUNK, G, N)
    a_cum = jnp.cumsum(da, axis=2)
    causal = jnp.tril(jnp.ones((CHUNK, CHUNK), dtype=bool))[None, None, :, :, None, None]
    seg = a_cum[:, :, :, None] - a_cum[:, :, None, :]
    decay_in = jnp.exp(jnp.where(causal, seg, -jnp.inf))
    scores = jnp.einsum('bclgn,bcsgn->bclsg', cc, bc)
    y_diag = jnp.einsum('bclsgr,bcsgrp->bclgrp', scores[..., None] * decay_in, xdt)
    decay_out = jnp.exp(a_cum[:, :, -1:] - a_cum)
    states = jnp.einsum('bclgn,bclgr,bclgrp->bcgrpn', bc, decay_out, xdt)
    chunk_decay = jnp.exp(a_cum[:, :, -1])

    def step(h, inp):
        st, dec = inp
        return h * dec[..., None, None] + st, h

    h0 = jnp.zeros((Bsz, G, R, P, N), jnp.float32)
    _, prev = lax.scan(step, h0, (jnp.moveaxis(states, 1, 0), jnp.moveaxis(chunk_decay, 1, 0)))
    prev = jnp.moveaxis(prev, 0, 1)
    y_off = jnp.einsum('bclgn,bcgrpn,bclgr->bclgrp', cc, prev, jnp.exp(a_cum))
    return (y_diag + y_off).reshape(Bsz, S, G, R, P)


def ssd_mixer(z, u_xbc, u_dt, conv_w, conv_b, dt_bias, a_log, d_skip, g_ssd):
    Bsz, S, _ = z.shape
    xbc = jax.nn.silu(causal_depthwise_conv(u_xbc, conv_w, conv_b))
    GN = SSD_GROUPS * SSD_STATE
    xs = xbc[..., :SSD_INNER].reshape(Bsz, S, SSD_GROUPS, SSD_HPG, SSD_HEAD_DIM)
    bm = xbc[..., SSD_INNER:SSD_INNER + GN].reshape(Bsz, S, SSD_GROUPS, SSD_STATE)
    cm = xbc[..., SSD_INNER + GN:].reshape(Bsz, S, SSD_GROUPS, SSD_STATE)
    dt = jax.nn.softplus(u_dt.astype(jnp.float32) + dt_bias.astype(jnp.float32))
    dt = dt.reshape(Bsz, S, SSD_GROUPS, SSD_HPG)
    a = -jnp.exp(a_log.astype(jnp.float32)).reshape(SSD_GROUPS, SSD_HPG)
    y = ssd_chunked(xs, dt, a, bm, cm)
    y = y + d_skip.astype(jnp.float32).reshape(SSD_GROUPS, SSD_HPG)[:, :, None] * xs.astype(jnp.float32)
    y = y.reshape(Bsz, S, SSD_INNER) * jax.nn.silu(z.astype(jnp.float32))
    yg = y.reshape(Bsz, S, SSD_GROUPS, SSD_INNER // SSD_GROUPS)
    yg = yg * lax.rsqrt(jnp.mean(yg * yg, axis=-1, keepdims=True) + EPS)
    y = yg.reshape(Bsz, S, SSD_INNER) * g_ssd.astype(jnp.float32)
    return y.astype(z.dtype)


def _fwd_setup_inputs(seed: int = 0) -> dict:
    key = jax.random.key(seed)
    ks = jax.random.split(key, 20)
    f32 = jnp.float32
    L = DEPTH
    x = jax.random.normal(ks[0], (BATCH, SEQ, D_MODEL), f32)
    c = jax.random.normal(ks[1], (BATCH, D_MODEL), f32)
    w_ada = jax.random.normal(ks[2], (L, D_MODEL, N_MOD * D_MODEL), f32) * D_MODEL ** -0.5
    b_ada = jax.random.normal(ks[3], (L, N_MOD * D_MODEL), f32) * 0.02
    g_mix = 1.0 + 0.02 * jax.random.normal(ks[4], (L, D_MODEL), f32)
    w_in = jax.random.normal(ks[5], (L, D_MODEL, IN_WIDTH), f32) * D_MODEL ** -0.5
    conv_w = jax.random.normal(ks[6], (L, CONV_WIDTH, CONV_CH), f32) * CONV_WIDTH ** -0.5
    conv_b = jax.random.normal(ks[7], (L, CONV_CH), f32) * 0.02
    dt0 = jnp.exp(jax.random.uniform(ks[8], (L, SSD_HEADS), f32, np.log(1e-3), np.log(1e-1)))
    dt_bias = dt0 + jnp.log(-jnp.expm1(-dt0))
    a_log = jnp.log(jax.random.uniform(ks[9], (L, SSD_HEADS), f32, 1.0, 16.0))
    d_skip = 1.0 + 0.1 * jax.random.normal(ks[10], (L, SSD_HEADS), f32)
    g_ssd = 1.0 + 0.02 * jax.random.normal(ks[11], (L, SSD_INNER), f32)
    w_pool = jax.random.normal(ks[12], (L, N_POOL_GROUPS, POOL_GROUP, POOL_GROUP), f32) * POOL_GROUP ** -0.5
    pool_scale = 1.0 + 0.1 * jax.random.normal(ks[13], (L, POOL_WIDTH), f32)
    w_out = jax.random.normal(ks[14], (L, MIX_WIDTH, D_MODEL), f32) * MIX_WIDTH ** -0.5
    g_mlp = 1.0 + 0.02 * jax.random.normal(ks[15], (L, D_MODEL), f32)
    w_up = jax.random.normal(ks[16], (L, D_MODEL, D_FF), f32) * D_MODEL ** -0.5
    w_down = jax.random.normal(ks[17], (L, D_FF, D_MODEL), f32) * D_FF ** -0.5
    g_final = 1.0 + 0.02 * jax.random.normal(ks[18], (D_MODEL,), f32)
    return {"x": x, "c": c, "w_ada": w_ada, "b_ada": b_ada, "g_mix": g_mix, "w_in": w_in,
            "conv_w": conv_w, "conv_b": conv_b, "dt_bias": dt_bias, "a_log": a_log,
            "d_skip": d_skip, "g_ssd": g_ssd, "w_pool": w_pool, "pool_scale": pool_scale,
            "w_out": w_out, "g_mlp": g_mlp, "w_up": w_up, "w_down": w_down, "g_final": g_final}


def _fwd_reference(x, c, w_ada, b_ada, g_mix, w_in, conv_w, conv_b, dt_bias, a_log, d_skip, g_ssd,
              w_pool, pool_scale, w_out, g_mlp, w_up, w_down, g_final):
    h = x
    c_act = jax.nn.silu(c)
    for layer in range(DEPTH):
        mod = jnp.einsum('bd,de->be', c_act, w_ada[layer]) + b_ada[layer]
        shift_m, scale_m, gate_m, shift_f, scale_f, gate_f = jnp.split(mod[:, None, :], N_MOD, axis=-1)

        u = rms_norm(h, g_mix[layer]) * (1.0 + scale_m) + shift_m
        proj = jnp.einsum('bsd,de->bse', u, w_in[layer])
        u_pool = proj[..., OFF_POOL:OFF_Z]
        z = proj[..., OFF_Z:OFF_XBC]
        u_xbc = proj[..., OFF_XBC:OFF_DT]
        u_dt = proj[..., OFF_DT:]
        y_pool = pool_mixer(u_pool, w_pool[layer], pool_scale[layer])
        y_ssd = ssd_mixer(z, u_xbc, u_dt, conv_w[layer], conv_b[layer], dt_bias[layer],
                          a_log[layer], d_skip[layer], g_ssd[layer])
        y_mix = jnp.concatenate([y_pool.astype(h.dtype), y_ssd.astype(h.dtype)], axis=-1)
        h = h + gate_m * jnp.einsum('bse,ed->bsd', y_mix, w_out[layer])

        u = rms_norm(h, g_mlp[layer]) * (1.0 + scale_f) + shift_f
        f = jnp.square(jax.nn.relu(jnp.einsum('bsd,df->bsf', u, w_up[layer])))
        h = h + gate_f * jnp.einsum('bsf,fd->bsd', f, w_down[layer])
    return rms_norm(h, g_final)


import jax as _jax
import jax.numpy as _jnp

TWIN_FORMAT = 'train_step'
FWD_PARAMS = ['x', 'c', 'w_ada', 'b_ada', 'g_mix', 'w_in', 'conv_w', 'conv_b', 'dt_bias', 'a_log', 'd_skip', 'g_ssd', 'w_pool', 'pool_scale', 'w_out', 'g_mlp', 'w_up', 'w_down', 'g_final']
TWIN_WEIGHTS = ['w_ada', 'b_ada', 'g_mix', 'w_in', 'conv_w', 'conv_b', 'dt_bias', 'a_log', 'd_skip', 'g_ssd', 'w_pool', 'pool_scale', 'w_out', 'g_mlp', 'w_up', 'w_down', 'g_final']
TWIN_DIFF_INPUT = 'x'
TWIN_INPUTS = ['x', 'c', 'w_ada', 'b_ada', 'g_mix', 'w_in', 'conv_w', 'conv_b', 'dt_bias', 'a_log', 'd_skip', 'g_ssd', 'w_pool', 'pool_scale', 'w_out', 'g_mlp', 'w_up', 'w_down', 'g_final', 'loss_target', 'm_w_ada', 'm_b_ada', 'm_g_mix', 'm_w_in', 'm_conv_w', 'm_conv_b', 'm_dt_bias', 'm_a_log', 'm_d_skip', 'm_g_ssd', 'm_w_pool', 'm_pool_scale', 'm_w_out', 'm_g_mlp', 'm_w_up', 'm_w_down', 'm_g_final', 'v_w_ada', 'v_b_ada', 'v_g_mix', 'v_w_in', 'v_conv_w', 'v_conv_b', 'v_dt_bias', 'v_a_log', 'v_d_skip', 'v_g_ssd', 'v_w_pool', 'v_pool_scale', 'v_w_out', 'v_g_mlp', 'v_w_up', 'v_w_down', 'v_g_final']
TWIN_OUTPUTS = ['loss', 'grad_x', 'grad_w_ada', 'grad_b_ada', 'grad_g_mix', 'grad_w_in', 'grad_conv_w', 'grad_conv_b', 'grad_dt_bias', 'grad_a_log', 'grad_d_skip', 'grad_g_ssd', 'grad_w_pool', 'grad_pool_scale', 'grad_w_out', 'grad_g_mlp', 'grad_w_up', 'grad_w_down', 'grad_g_final', 'delta_w_ada', 'delta_b_ada', 'delta_g_mix', 'delta_w_in', 'delta_conv_w', 'delta_conv_b', 'delta_dt_bias', 'delta_a_log', 'delta_d_skip', 'delta_g_ssd', 'delta_w_pool', 'delta_pool_scale', 'delta_w_out', 'delta_g_mlp', 'delta_w_up', 'delta_w_down', 'delta_g_final', 'new_m_w_ada', 'new_m_b_ada', 'new_m_g_mix', 'new_m_w_in', 'new_m_conv_w', 'new_m_conv_b', 'new_m_dt_bias', 'new_m_a_log', 'new_m_d_skip', 'new_m_g_ssd', 'new_m_w_pool', 'new_m_pool_scale', 'new_m_w_out', 'new_m_g_mlp', 'new_m_w_up', 'new_m_w_down', 'new_m_g_final', 'new_v_w_ada', 'new_v_b_ada', 'new_v_g_mix', 'new_v_w_in', 'new_v_conv_w', 'new_v_conv_b', 'new_v_dt_bias', 'new_v_a_log', 'new_v_d_skip', 'new_v_g_ssd', 'new_v_w_pool', 'new_v_pool_scale', 'new_v_w_out', 'new_v_g_mlp', 'new_v_w_up', 'new_v_w_down', 'new_v_g_final']
TWIN_LEAF_KINDS = {'loss': 'loss', 'grad_x': 'grad_x', 'grad_w_ada': 'grad_w', 'grad_b_ada': 'grad_w', 'grad_g_mix': 'grad_w', 'grad_w_in': 'grad_w', 'grad_conv_w': 'grad_w', 'grad_conv_b': 'grad_w', 'grad_dt_bias': 'grad_w', 'grad_a_log': 'grad_w', 'grad_d_skip': 'grad_w', 'grad_g_ssd': 'grad_w', 'grad_w_pool': 'grad_w', 'grad_pool_scale': 'grad_w', 'grad_w_out': 'grad_w', 'grad_g_mlp': 'grad_w', 'grad_w_up': 'grad_w', 'grad_w_down': 'grad_w', 'grad_g_final': 'grad_w', 'delta_w_ada': 'delta_w', 'delta_b_ada': 'delta_w', 'delta_g_mix': 'delta_w', 'delta_w_in': 'delta_w', 'delta_conv_w': 'delta_w', 'delta_conv_b': 'delta_w', 'delta_dt_bias': 'delta_w', 'delta_a_log': 'delta_w', 'delta_d_skip': 'delta_w', 'delta_g_ssd': 'delta_w', 'delta_w_pool': 'delta_w', 'delta_pool_scale': 'delta_w', 'delta_w_out': 'delta_w', 'delta_g_mlp': 'delta_w', 'delta_w_up': 'delta_w', 'delta_w_down': 'delta_w', 'delta_g_final': 'delta_w', 'new_m_w_ada': 'new_m', 'new_m_b_ada': 'new_m', 'new_m_g_mix': 'new_m', 'new_m_w_in': 'new_m', 'new_m_conv_w': 'new_m', 'new_m_conv_b': 'new_m', 'new_m_dt_bias': 'new_m', 'new_m_a_log': 'new_m', 'new_m_d_skip': 'new_m', 'new_m_g_ssd': 'new_m', 'new_m_w_pool': 'new_m', 'new_m_pool_scale': 'new_m', 'new_m_w_out': 'new_m', 'new_m_g_mlp': 'new_m', 'new_m_w_up': 'new_m', 'new_m_w_down': 'new_m', 'new_m_g_final': 'new_m', 'new_v_w_ada': 'new_v', 'new_v_b_ada': 'new_v', 'new_v_g_mix': 'new_v', 'new_v_w_in': 'new_v', 'new_v_conv_w': 'new_v', 'new_v_conv_b': 'new_v', 'new_v_dt_bias': 'new_v', 'new_v_a_log': 'new_v', 'new_v_d_skip': 'new_v', 'new_v_g_ssd': 'new_v', 'new_v_w_pool': 'new_v', 'new_v_pool_scale': 'new_v', 'new_v_w_out': 'new_v', 'new_v_g_mlp': 'new_v', 'new_v_w_up': 'new_v', 'new_v_w_down': 'new_v', 'new_v_g_final': 'new_v'}


def _forward(args):
    return _fwd_reference(*[args[k] for k in FWD_PARAMS])


def _output_shape():
    out = _jax.eval_shape(lambda: _forward(_fwd_setup_inputs(0)))
    return out.shape, out.dtype

N_MICROBATCH = 1
ADAM_LR = 0.001
ADAM_B1 = 0.9
ADAM_B2 = 0.999
ADAM_EPS = 1e-08
ADAM_WD = 0.01
ADAM_STEP = 10
PER_EXAMPLE_BATCH_AXIS = {'x': 0, 'c': 0, 'loss_target': 0}
SHARED_INPUTS = []
_WEIGHT_DTYPES = {'w_ada': _jnp.float32, 'b_ada': _jnp.float32, 'g_mix': _jnp.float32, 'w_in': _jnp.float32, 'conv_w': _jnp.float32, 'conv_b': _jnp.float32, 'dt_bias': _jnp.float32, 'a_log': _jnp.float32, 'd_skip': _jnp.float32, 'g_ssd': _jnp.float32, 'w_pool': _jnp.float32, 'pool_scale': _jnp.float32, 'w_out': _jnp.float32, 'g_mlp': _jnp.float32, 'w_up': _jnp.float32, 'w_down': _jnp.float32, 'g_final': _jnp.float32}
MOMENT_SCALE = {'w_ada': 1.308310e-01, 'b_ada': 2.350970e-01, 'g_mix': 1.196952e-01, 'w_in': 7.678317e-02, 'conv_w': 7.070313e-02, 'conv_b': 7.355286e-02, 'dt_bias': 5.343848e-01, 'a_log': 5.046265e-01, 'd_skip': 2.414982e-01, 'g_ssd': 7.914968e-02, 'w_pool': 8.050383e-02, 'pool_scale': 8.460023e-02, 'w_out': 9.907042e-02, 'g_mlp': 1.359657e-01, 'w_up': 8.498272e-02, 'w_down': 1.807494e-01, 'g_final': 3.529796e+01}


def _to_microbatches(a, axis):
    t = _jnp.moveaxis(a, axis, 0)
    t = t.reshape((N_MICROBATCH, t.shape[0] // N_MICROBATCH) + t.shape[1:])
    return _jnp.moveaxis(t, 1, axis + 1)


def setup_inputs(seed: int = 0) -> dict:
    inp = _fwd_setup_inputs(seed)
    key = _jax.random.fold_in(_jax.random.key(seed), 7919)
    shape, _ = _output_shape()
    out = dict(inp)
    out["loss_target"] = _jax.random.normal(_jax.random.fold_in(key, 0), shape, _jnp.float32)
    for i, name in enumerate(TWIN_WEIGHTS):
        w = inp[name].astype(_jnp.float32)
        if MOMENT_SCALE is None:
            s = _jnp.sqrt(_jnp.mean(_jnp.square(w)) + 1e-30)
        else:
            s = MOMENT_SCALE[name]
        km, kv = _jax.random.split(_jax.random.fold_in(key, i + 1))
        out[name] = w
        out["m_" + name] = s * _jax.random.normal(km, w.shape, _jnp.float32)
        out["v_" + name] = (s * s) * _jax.random.uniform(kv, w.shape, _jnp.float32, 0.5, 1.5)
    if N_MICROBATCH > 1:
        for name, axis in PER_EXAMPLE_BATCH_AXIS.items():
            out[name] = _to_microbatches(out[name], axis)
    return {'x': out['x'], 'c': out['c'], 'w_ada': out['w_ada'], 'b_ada': out['b_ada'], 'g_mix': out['g_mix'], 'w_in': out['w_in'], 'conv_w': out['conv_w'], 'conv_b': out['conv_b'], 'dt_bias': out['dt_bias'], 'a_log': out['a_log'], 'd_skip': out['d_skip'], 'g_ssd': out['g_ssd'], 'w_pool': out['w_pool'], 'pool_scale': out['pool_scale'], 'w_out': out['w_out'], 'g_mlp': out['g_mlp'], 'w_up': out['w_up'], 'w_down': out['w_down'], 'g_final': out['g_final'], 'loss_target': out['loss_target'], 'm_w_ada': out['m_w_ada'], 'm_b_ada': out['m_b_ada'], 'm_g_mix': out['m_g_mix'], 'm_w_in': out['m_w_in'], 'm_conv_w': out['m_conv_w'], 'm_conv_b': out['m_conv_b'], 'm_dt_bias': out['m_dt_bias'], 'm_a_log': out['m_a_log'], 'm_d_skip': out['m_d_skip'], 'm_g_ssd': out['m_g_ssd'], 'm_w_pool': out['m_w_pool'], 'm_pool_scale': out['m_pool_scale'], 'm_w_out': out['m_w_out'], 'm_g_mlp': out['m_g_mlp'], 'm_w_up': out['m_w_up'], 'm_w_down': out['m_w_down'], 'm_g_final': out['m_g_final'], 'v_w_ada': out['v_w_ada'], 'v_b_ada': out['v_b_ada'], 'v_g_mix': out['v_g_mix'], 'v_w_in': out['v_w_in'], 'v_conv_w': out['v_conv_w'], 'v_conv_b': out['v_conv_b'], 'v_dt_bias': out['v_dt_bias'], 'v_a_log': out['v_a_log'], 'v_d_skip': out['v_d_skip'], 'v_g_ssd': out['v_g_ssd'], 'v_w_pool': out['v_w_pool'], 'v_pool_scale': out['v_pool_scale'], 'v_w_out': out['v_w_out'], 'v_g_mlp': out['v_g_mlp'], 'v_w_up': out['v_w_up'], 'v_w_down': out['v_w_down'], 'v_g_final': out['v_g_final']}


def _loss(weights, diff, rest, loss_target):
    with _jax.named_scope("forward"):
        args = {**rest, TWIN_DIFF_INPUT: diff, **{k: w.astype(_WEIGHT_DTYPES[k]) for k, w in weights.items()}}
        y = _forward(args)
    with _jax.named_scope("loss_head"):
        err = _jnp.square(y.astype(_jnp.float32) - loss_target)
        return 0.5 * _jnp.sum(_jnp.mean(err, axis=-1)) if err.ndim else 0.5 * err


def _adamw(w, g, m, v):
    m = ADAM_B1 * m + (1.0 - ADAM_B1) * g
    v = ADAM_B2 * v + (1.0 - ADAM_B2) * _jnp.square(g)
    m_hat = m / (1.0 - ADAM_B1 ** ADAM_STEP)
    v_hat = v / (1.0 - ADAM_B2 ** ADAM_STEP)
    delta = -ADAM_LR * (m_hat / (_jnp.sqrt(v_hat) + ADAM_EPS) + ADAM_WD * w)
    return delta, m, v


def reference(x, c, w_ada, b_ada, g_mix, w_in, conv_w, conv_b, dt_bias, a_log, d_skip, g_ssd, w_pool, pool_scale, w_out, g_mlp, w_up, w_down, g_final, loss_target, m_w_ada, m_b_ada, m_g_mix, m_w_in, m_conv_w, m_conv_b, m_dt_bias, m_a_log, m_d_skip, m_g_ssd, m_w_pool, m_pool_scale, m_w_out, m_g_mlp, m_w_up, m_w_down, m_g_final, v_w_ada, v_b_ada, v_g_mix, v_w_in, v_conv_w, v_conv_b, v_dt_bias, v_a_log, v_d_skip, v_g_ssd, v_w_pool, v_pool_scale, v_w_out, v_g_mlp, v_w_up, v_w_down, v_g_final):
    given = dict(x=x, c=c, w_ada=w_ada, b_ada=b_ada, g_mix=g_mix, w_in=w_in, conv_w=conv_w, conv_b=conv_b, dt_bias=dt_bias, a_log=a_log, d_skip=d_skip, g_ssd=g_ssd, w_pool=w_pool, pool_scale=pool_scale, w_out=w_out, g_mlp=g_mlp, w_up=w_up, w_down=w_down, g_final=g_final, loss_target=loss_target, m_w_ada=m_w_ada, m_b_ada=m_b_ada, m_g_mix=m_g_mix, m_w_in=m_w_in, m_conv_w=m_conv_w, m_conv_b=m_conv_b, m_dt_bias=m_dt_bias, m_a_log=m_a_log, m_d_skip=m_d_skip, m_g_ssd=m_g_ssd, m_w_pool=m_w_pool, m_pool_scale=m_pool_scale, m_w_out=m_w_out, m_g_mlp=m_g_mlp, m_w_up=m_w_up, m_w_down=m_w_down, m_g_final=m_g_final, v_w_ada=v_w_ada, v_b_ada=v_b_ada, v_g_mix=v_g_mix, v_w_in=v_w_in, v_conv_w=v_conv_w, v_conv_b=v_conv_b, v_dt_bias=v_dt_bias, v_a_log=v_a_log, v_d_skip=v_d_skip, v_g_ssd=v_g_ssd, v_w_pool=v_w_pool, v_pool_scale=v_pool_scale, v_w_out=v_w_out, v_g_mlp=v_g_mlp, v_w_up=v_w_up, v_w_down=v_w_down, v_g_final=v_g_final)
    weights = {n: given[n] for n in TWIN_WEIGHTS}
    shared = {n: given[n] for n in SHARED_INPUTS}
    per_example = {n: given[n] for n in ['x', 'c']}
    grad_fn = _jax.value_and_grad(_loss, argnums=(0, 1))

    def one_microbatch(ex, loss_target):
        ex = dict(ex)
        diff = ex.pop(TWIN_DIFF_INPUT)
        return grad_fn(weights, diff, {**shared, **ex}, loss_target)

    if N_MICROBATCH == 1:
        loss, (grad_w, grad_x) = one_microbatch(per_example, given["loss_target"])
    else:
        def body(carry, xs):
            loss_sum, grad_sum = carry
            l_k, (gw_k, gx_k) = one_microbatch(xs[0], xs[1])
            with _jax.named_scope("update"):
                return (loss_sum + l_k, _jax.tree.map(_jnp.add, grad_sum, gw_k)), gx_k

        init = (_jnp.zeros((), _jnp.float32), _jax.tree.map(_jnp.zeros_like, weights))
        (loss, grad_w), grad_x = _jax.lax.scan(body, init, (per_example, given["loss_target"]))
    with _jax.named_scope("update"):
        delta_w, new_m, new_v = {}, {}, {}
        for n in TWIN_WEIGHTS:
            delta_w[n], new_m[n], new_v[n] = _adamw(weights[n], grad_w[n], given["m_" + n], given["v_" + n])
    return (loss, grad_x, *[grad_w[n] for n in TWIN_WEIGHTS], *[delta_w[n] for n in TWIN_WEIGHTS],
            *[new_m[n] for n in TWIN_WEIGHTS], *[new_v[n] for n in TWIN_WEIGHTS])
```

```python
import functools

import jax
import jax.numpy as jnp
from jax import lax
from jax.experimental import pallas as pl
from jax.experimental.pallas import tpu as pltpu

F32 = jnp.float32
BF = jnp.bfloat16
MESH = pl.DeviceIdType.MESH

EPS = 1e-5
D_MODEL = 1024
POOL_WIDTH = 512
POOL_WINDOWS = (2, 4, 8, 16)
POOL_GROUP = 128
SSD_INNER = 1024
SSD_HEADS = 16
SSD_HEAD_DIM = 64
SSD_STATE = 128
GROUP_W = 512
CHUNK = 128
CONV_CH = 1536
OFF_Z = 512
OFF_XBC = 1536
OFF_DT = 3072
IN_WIDTH = 3088
IN_PAD = 3200
D_FF = 4096
N_MOD = 6
N_CHIPS = 4
N_DEV = 8
HALO = 16
CONV_HALO = 8

ADAM_LR = 0.001
ADAM_B1 = 0.9
ADAM_B2 = 0.999
ADAM_EPS = 1e-08
ADAM_WD = 0.01
ADAM_STEP = 10

VMEM_BYTES_V7X = 64 * 1024 * 1024


def _cp(semantics=None, vmem_mb=48, **kw):
    args = dict(vmem_limit_bytes=vmem_mb * 1024 * 1024, **kw)
    if semantics is not None:
        args["dimension_semantics"] = semantics
    return pltpu.CompilerParams(**args)


def _nn(a, b):
    return jnp.dot(a, b, preferred_element_type=F32)


def _nt(a, b):
    return lax.dot_general(a, b, (((1,), (1,)), ((), ())), preferred_element_type=F32)


def _tn(a, b):
    return lax.dot_general(a, b, (((0,), (0,)), ((), ())), preferred_element_type=F32)


def _split3(v):
    hi = v.astype(BF)
    r1 = v - hi.astype(F32)
    mid = r1.astype(BF)
    lo = (r1 - mid.astype(F32)).astype(BF)
    return hi, mid, lo


def _exact_nn(v, m01):
    hi, mid, lo = _split3(v)
    return _nn(hi, m01) + _nn(mid, m01) + _nn(lo, m01)


def _exact_nn_left(m01, v):
    hi, mid, lo = _split3(v)
    return _nn(m01, hi) + _nn(m01, mid) + _nn(m01, lo)


def _exact_nt_left(m01, v):
    hi, mid, lo = _split3(v)
    return _nt(m01, hi) + _nt(m01, mid) + _nt(m01, lo)


def _sigmoid(v):
    return 1.0 / (1.0 + jnp.exp(-v))


def _iota(shape, dim):
    return lax.broadcasted_iota(jnp.int32, shape, dim)


def _head_expand_matrix(heads, width):
    return (_iota((heads, width), 1) // SSD_HEAD_DIM == _iota((heads, width), 0)).astype(BF)


def _head_reduce_matrix(width, heads):
    return (_iota((width, heads), 0) // SSD_HEAD_DIM == _iota((width, heads), 1)).astype(BF)


def _mesh_pos():
    return lax.axis_index("x"), lax.axis_index("y"), lax.axis_index("c")


def _flip(v, bit):
    return v + bit - 2 * bit * v


def _all_gather_small(arrays):
    n = len(arrays)

    def body(*refs):
        in_refs, out_refs = refs[:n], refs[n:2 * n]
        send_sems, recv_sems, local_sems = refs[2 * n:]
        x, y, c = _mesh_pos()
        me = 4 * x + 2 * y + c
        local = []
        for a in range(n):
            cp = pltpu.make_async_copy(in_refs[a], out_refs[a].at[me], local_sems.at[a])
            cp.start()
            local.append(cp)
        sends = []
        for k in range(1, N_DEV):
            peer = (_flip(x, (k >> 2) & 1), _flip(y, (k >> 1) & 1), _flip(c, k & 1))
            for a in range(n):
                cp = pltpu.make_async_remote_copy(
                    src_ref=in_refs[a], dst_ref=out_refs[a].at[me],
                    send_sem=send_sems.at[a, k], recv_sem=recv_sems.at[a, k],
                    device_id=peer, device_id_type=MESH)
                cp.start()
                sends.append(cp)
        for k in range(1, N_DEV):
            px, py, pc = _flip(x, (k >> 2) & 1), _flip(y, (k >> 1) & 1), _flip(c, k & 1)
            src = 4 * px + 2 * py + pc
            for a in range(n):
                pltpu.make_async_remote_copy(
                    src_ref=in_refs[a], dst_ref=out_refs[a].at[src],
                    send_sem=send_sems.at[a, k], recv_sem=recv_sems.at[a, k],
                    device_id=(px, py, pc), device_id_type=MESH).wait_recv()
        for cp in sends:
            cp.wait_send()
        for cp in local:
            cp.wait()

    vm = pl.BlockSpec(memory_space=pltpu.VMEM)
    return pl.pallas_call(
        body, name="all_gather_small",
        out_shape=[jax.ShapeDtypeStruct((N_DEV,) + a.shape, a.dtype) for a in arrays],
        in_specs=[vm] * n, out_specs=[vm] * n,
        scratch_shapes=[pltpu.SemaphoreType.DMA((n, N_DEV)), pltpu.SemaphoreType.DMA((n, N_DEV)),
                        pltpu.SemaphoreType.DMA((n,))],
        compiler_params=_cp(vmem_mb=32),
    )(*arrays)


def _weight_all_gather(shards):
    n = len(shards)
    any_spec = pl.BlockSpec(memory_space=pl.ANY)

    def body(*refs):
        in_refs, out_refs = refs[:n], refs[n:2 * n]
        send_sems, recv_sems, fsend_sems, frecv_sems, local_sems = refs[2 * n:]
        x, y, c = _mesh_pos()
        chip = 2 * x + y
        local, sends = [], []
        for a in range(n):
            cp = pltpu.make_async_copy(in_refs[a], out_refs[a].at[chip], local_sems.at[a])
            cp.start()
            local.append(cp)

        def half_rows(a, which):
            hr = shards[a].shape[0] // 2
            return pl.ds(pl.multiple_of(which * hr, 16), hr)

        for j in range(1, N_CHIPS):
            px, py = _flip(x, (j >> 1) & 1), _flip(y, j & 1)
            for a in range(n):
                cp = pltpu.make_async_remote_copy(
                    src_ref=in_refs[a].at[half_rows(a, c)], dst_ref=out_refs[a].at[chip, half_rows(a, c)],
                    send_sem=send_sems.at[a, j], recv_sem=recv_sems.at[a, j],
                    device_id=(px, py, c), device_id_type=MESH)
                cp.start()
                sends.append(cp)
        for j in range(1, N_CHIPS):
            px, py = _flip(x, (j >> 1) & 1), _flip(y, j & 1)
            src = 2 * px + py
            for a in range(n):
                landed = out_refs[a].at[src, half_rows(a, c)]
                pltpu.make_async_remote_copy(
                    src_ref=landed, dst_ref=landed, send_sem=send_sems.at[a, j], recv_sem=recv_sems.at[a, j],
                    device_id=(px, py, c), device_id_type=MESH).wait_recv()
                cp = pltpu.make_async_remote_copy(
                    src_ref=landed, dst_ref=landed, send_sem=fsend_sems.at[a, j], recv_sem=frecv_sems.at[a, j],
                    device_id=(x, y, 1 - c), device_id_type=MESH)
                cp.start()
                sends.append(cp)
        for j in range(1, N_CHIPS):
            px, py = _flip(x, (j >> 1) & 1), _flip(y, j & 1)
            src = 2 * px + py
            for a in range(n):
                other = out_refs[a].at[src, half_rows(a, 1 - c)]
                pltpu.make_async_remote_copy(
                    src_ref=other, dst_ref=other, send_sem=fsend_sems.at[a, j], recv_sem=frecv_sems.at[a, j],
                    device_id=(x, y, 1 - c), device_id_type=MESH).wait_recv()
        for cp in sends:
            cp.wait_send()
        for cp in local:
            cp.wait()

    return pl.pallas_call(
        body, name="weight_all_gather",
        out_shape=[jax.ShapeDtypeStruct((N_CHIPS,) + s.shape, s.dtype) for s in shards],
        in_specs=[any_spec] * n, out_specs=[any_spec] * n,
        scratch_shapes=[pltpu.SemaphoreType.DMA((n, N_CHIPS))] * 4 + [pltpu.SemaphoreType.DMA((n,))],
        compiler_params=_cp(vmem_mb=16),
    )(*shards)


def _grad_to_sibling(grads):
    n = len(grads)
    any_spec = pl.BlockSpec(memory_space=pl.ANY)

    def body(*refs):
        in_refs, own_refs, got_refs = refs[:n], refs[n:2 * n], refs[2 * n:3 * n]
        send_sems, recv_sems, local_sems = refs[3 * n:]
        x, y, c = _mesh_pos()
        copies, local = [], []
        for a in range(n):
            hr = grads[a].shape[1] // 2
            mine = pl.ds(pl.multiple_of(c * hr, 16), hr)
            theirs = pl.ds(pl.multiple_of((1 - c) * hr, 16), hr)
            lc = pltpu.make_async_copy(in_refs[a].at[:, mine, :], own_refs[a], local_sems.at[a])
            lc.start()
            local.append(lc)
            cp = pltpu.make_async_remote_copy(
                src_ref=in_refs[a].at[:, theirs, :], dst_ref=got_refs[a],
                send_sem=send_sems.at[a], recv_sem=recv_sems.at[a],
                device_id=(x, y, 1 - c), device_id_type=MESH)
            cp.start()
            copies.append(cp)
        for cp in copies:
            cp.wait_recv()
        for cp in copies:
            cp.wait_send()
        for lc in local:
            lc.wait()

    half = [jax.ShapeDtypeStruct((g.shape[0], g.shape[1] // 2, g.shape[2]), g.dtype) for g in grads]
    outs = pl.pallas_call(
        body, name="grad_to_sibling", out_shape=half + half,
        in_specs=[any_spec] * n, out_specs=[any_spec] * (2 * n),
        scratch_shapes=[pltpu.SemaphoreType.DMA((n,))] * 3,
        compiler_params=_cp(vmem_mb=16),
    )(*grads)
    return outs[:n], outs[n:]


def _grad_to_chips(parts):
    n = len(parts)
    any_spec = pl.BlockSpec(memory_space=pl.ANY)

    def body(*refs):
        in_refs, out_refs = refs[:n], refs[n:2 * n]
        send_sems, recv_sems, local_sems = refs[2 * n:]
        x, y, c = _mesh_pos()
        chip = 2 * x + y
        sends, local = [], []
        for a in range(n):
            lc = pltpu.make_async_copy(in_refs[a].at[chip], out_refs[a].at[chip], local_sems.at[a])
            lc.start()
            local.append(lc)
        for j in range(1, N_CHIPS):
            px, py = _flip(x, (j >> 1) & 1), _flip(y, j & 1)
            dst_chip = 2 * px + py
            for a in range(n):
                cp = pltpu.make_async_remote_copy(
                    src_ref=in_refs[a].at[dst_chip], dst_ref=out_refs[a].at[chip],
                    send_sem=send_sems.at[a, j], recv_sem=recv_sems.at[a, j],
                    device_id=(px, py, c), device_id_type=MESH)
                cp.start()
                sends.append(cp)
        for j in range(1, N_CHIPS):
            px, py = _flip(x, (j >> 1) & 1), _flip(y, j & 1)
            src = 2 * px + py
            for a in range(n):
                pltpu.make_async_remote_copy(
                    src_ref=in_refs[a].at[src], dst_ref=out_refs[a].at[src],
                    send_sem=send_sems.at[a, j], recv_sem=recv_sems.at[a, j],
                    device_id=(px, py, c), device_id_type=MESH).wait_recv()
        for cp in sends:
            cp.wait_send()
        for lc in local:
            lc.wait()

    return pl.pallas_call(
        body, name="grad_to_chips",
        out_shape=[jax.ShapeDtypeStruct(p.shape, p.dtype) for p in parts],
        in_specs=[any_spec] * n, out_specs=[any_spec] * n,
        scratch_shapes=[pltpu.SemaphoreType.DMA((n, N_CHIPS))] * 2 + [pltpu.SemaphoreType.DMA((n,))],
        compiler_params=_cp(vmem_mb=16),
    )(*parts)


def _halves_exchange(halves):
    n = len(halves)
    any_spec = pl.BlockSpec(memory_space=pl.ANY)

    def body(*refs):
        in_refs, out_refs = refs[:n], refs[n:2 * n]
        send_sems, recv_sems, local_sems = refs[2 * n:]
        x, y, c = _mesh_pos()
        copies, local = [], []
        for a in range(n):
            hr = halves[a].shape[0]
            mine = pl.ds(pl.multiple_of(c * hr, 8), hr)
            lc = pltpu.make_async_copy(in_refs[a], out_refs[a].at[mine], local_sems.at[a])
            lc.start()
            local.append(lc)
            cp = pltpu.make_async_remote_copy(
                src_ref=in_refs[a], dst_ref=out_refs[a].at[mine],
                send_sem=send_sems.at[a], recv_sem=recv_sems.at[a],
                device_id=(x, y, 1 - c), device_id_type=MESH)
            cp.start()
            copies.append(cp)
        for a in range(n):
            hr = halves[a].shape[0]
            theirs = pl.ds(pl.multiple_of((1 - c) * hr, 8), hr)
            pltpu.make_async_remote_copy(
                src_ref=in_refs[a], dst_ref=out_refs[a].at[theirs],
                send_sem=send_sems.at[a], recv_sem=recv_sems.at[a],
                device_id=(x, y, 1 - c), device_id_type=MESH).wait_recv()
        for cp in copies:
            cp.wait_send()
        for lc in local:
            lc.wait()

    return pl.pallas_call(
        body, name="halves_exchange",
        out_shape=[jax.ShapeDtypeStruct((2 * h.shape[0], h.shape[1]), h.dtype) for h in halves],
        in_specs=[any_spec] * n, out_specs=[any_spec] * n,
        scratch_shapes=[pltpu.SemaphoreType.DMA((n,))] * 3,
        compiler_params=_cp(vmem_mb=16),
    )(*halves)


def _add_pair(a, b):
    k, h, c = a.shape

    def body(a_ref, b_ref, o_ref):
        o_ref[...] = (a_ref[...].astype(F32) + b_ref[...].astype(F32)).astype(o_ref.dtype)

    spec = pl.BlockSpec((1, h, c), lambda i: (i, 0, 0))
    return pl.pallas_call(body, name="add_pair", grid=(k,), in_specs=[spec, spec], out_specs=spec,
                          out_shape=jax.ShapeDtypeStruct(a.shape, BF), compiler_params=_cp(("parallel",), 32))(a, b)


def _sum_slots(p):
    k, h, c = p.shape
    th = h // 2

    def body(p_ref, o_ref):
        acc = p_ref[0].astype(F32)
        for s in range(1, k):
            acc = acc + p_ref[s].astype(F32)
        o_ref[...] = acc

    return pl.pallas_call(body, name="sum_slots", grid=(2,),
                          in_specs=[pl.BlockSpec((k, th, c), lambda i: (0, i, 0))],
                          out_specs=pl.BlockSpec((th, c), lambda i: (i, 0)),
                          out_shape=jax.ShapeDtypeStruct((h, c), F32), compiler_params=_cp(("parallel",), 32))(p)


def _adam_math(w, g, m, v):
    m2 = ADAM_B1 * m + (1.0 - ADAM_B1) * g
    v2 = ADAM_B2 * v + (1.0 - ADAM_B2) * (g * g)
    m_hat = m2 / (1.0 - ADAM_B1 ** ADAM_STEP)
    v_hat = v2 / (1.0 - ADAM_B2 ** ADAM_STEP)
    delta = -ADAM_LR * (m_hat / (jnp.sqrt(v_hat) + ADAM_EPS) + ADAM_WD * w)
    return delta, m2, v2


def _adam_big(g, w, m, v):
    r, c = w.shape
    tr = r // 4

    def body(g_ref, w_ref, m_ref, v_ref, d_ref, m2_ref, v2_ref):
        d, m2, v2 = _adam_math(w_ref[...], g_ref[...], m_ref[...], v_ref[...])
        d_ref[...] = d
        m2_ref[...] = m2
        v2_ref[...] = v2

    spec = pl.BlockSpec((tr, c), lambda i: (i, 0))
    sh = jax.ShapeDtypeStruct((r, c), F32)
    return pl.pallas_call(body, name="adam_big", grid=(4,), in_specs=[spec] * 4, out_specs=[spec] * 3,
                          out_shape=[sh, sh, sh], compiler_params=_cp(("parallel",), 32))(g, w, m, v)


def _adam_ada(c_act_t, dmod_cols, w, m, v):
    r, c = w.shape
    tc = 512

    def body(ct_ref, dm_ref, w_ref, m_ref, v_ref, g_ref, d_ref, m2_ref, v2_ref):
        g = _nn(ct_ref[...], dm_ref[...].astype(BF))
        d, m2, v2 = _adam_math(w_ref[...], g, m_ref[...], v_ref[...])
        g_ref[...] = g
        d_ref[...] = d
        m2_ref[...] = m2
        v2_ref[...] = v2

    spec = pl.BlockSpec((r, tc), lambda i: (0, i))
    sh = jax.ShapeDtypeStruct((r, c), F32)
    return pl.pallas_call(
        body, name="adam_ada", grid=(c // tc,),
        in_specs=[pl.BlockSpec(c_act_t.shape, lambda i: (0, 0)), pl.BlockSpec((dmod_cols.shape[0], tc), lambda i: (0, i)),
                  spec, spec, spec],
        out_specs=[spec] * 4, out_shape=[sh] * 4, compiler_params=_cp(("parallel",), 48))(c_act_t, dmod_cols, w, m, v)


def _ada_mod(c_all, w_shard, b_shard):
    nb, d = c_all.shape
    cols = w_shard.shape[1]
    tc = 512

    def body(c_ref, w_ref, b_ref, mod_ref, act_ref):
        cv = c_ref[...]
        act = cv * _sigmoid(cv)
        act_ref[...] = act
        mod_ref[...] = _nn(act.astype(BF), w_ref[...].astype(BF)) + b_ref[...]

    return pl.pallas_call(
        body, name="ada_mod", grid=(cols // tc,),
        in_specs=[pl.BlockSpec((nb, d), lambda i: (0, 0)), pl.BlockSpec((d, tc), lambda i: (0, i)),
                  pl.BlockSpec((1, tc), lambda i: (0, i))],
        out_specs=[pl.BlockSpec((nb, tc), lambda i: (0, i)), pl.BlockSpec((nb, d), lambda i: (0, 0))],
        out_shape=[jax.ShapeDtypeStruct((nb, cols), F32), jax.ShapeDtypeStruct((nb, d), F32)],
        compiler_params=_cp(("arbitrary",), 32))(c_all, w_shard, b_shard)


def _in_proj(x, mod3, g_mix, w_in_p, seq):
    t, d = x.shape
    tm = 256
    tps = seq // tm

    def body(x_ref, mod_ref, g_ref, w_ref, proj_ref, u1_ref):
        xv = x_ref[...]
        r = lax.rsqrt(jnp.mean(xv * xv, -1, keepdims=True) + EPS)
        u = (xv * r * g_ref[...]) * (1.0 + mod_ref[0, 1:2, :]) + mod_ref[0, 0:1, :]
        ub = u.astype(BF)
        u1_ref[...] = ub
        proj_ref[...] = _nn(ub, w_ref[...])

    return pl.pallas_call(
        body, name="in_proj", grid=(t // tm,),
        in_specs=[pl.BlockSpec((tm, d), lambda i: (i, 0)), pl.BlockSpec((1, N_MOD, d), lambda i: (i // tps, 0, 0)),
                  pl.BlockSpec((1, d), lambda i: (0, 0)), pl.BlockSpec((d, IN_PAD), lambda i: (0, 0))],
        out_specs=[pl.BlockSpec((tm, IN_PAD), lambda i: (i, 0)), pl.BlockSpec((tm, d), lambda i: (i, 0))],
        out_shape=[jax.ShapeDtypeStruct((t, IN_PAD), F32), jax.ShapeDtypeStruct((t, d), BF)],
        compiler_params=_cp(("parallel",), 48))(x, mod3, g_mix, w_in_p)


def _pool_tile(seq):
    return min(512, seq)


def _pool_fwd(proj, w_pool, pool_scale, nb, seq):
    ts = _pool_tile(seq)
    nt = seq // ts

    def body(u_ref, halo_ref, wp_ref, ps_ref, yp_ref, p_ref):
        i = pl.program_id(1)
        halo = jnp.where(i == 0, 0.0, halo_ref[...])
        u = u_ref[...]
        ext = jnp.concatenate([halo, u], 0)
        tpos = i * ts + _iota((ts, 1), 0)
        for g, w in enumerate(POOL_WINDOWS):
            gs = slice(g * POOL_GROUP, (g + 1) * POOL_GROUP)
            s = ext[:, gs]
            sh = 1
            while sh < w:
                s = s + pltpu.roll(s, sh, 0)
                sh *= 2
            cnt = jnp.minimum(tpos + 1, w).astype(F32)
            pb = (s[HALO:] / cnt - u[:, gs]).astype(BF)
            p_ref[:, gs] = pb
            yp_ref[:, gs] = (_nn(pb, wp_ref[g].astype(BF)) * ps_ref[:, gs]).astype(BF)

    hb = ts // HALO
    return pl.pallas_call(
        body, name="pool_fwd", grid=(nb, nt),
        in_specs=[pl.BlockSpec((ts, POOL_WIDTH), lambda b, i: (b * nt + i, 0)),
                  pl.BlockSpec((HALO, POOL_WIDTH), lambda b, i: (jnp.maximum((b * nt + i) * hb - 1, 0), 0)),
                  pl.BlockSpec((4, POOL_GROUP, POOL_GROUP), lambda b, i: (0, 0, 0)),
                  pl.BlockSpec((1, POOL_WIDTH), lambda b, i: (0, 0))],
        out_specs=[pl.BlockSpec((ts, POOL_WIDTH), lambda b, i: (b * nt + i, 0))] * 2,
        out_shape=[jax.ShapeDtypeStruct((nb * seq, POOL_WIDTH), BF)] * 2,
        compiler_params=_cp(("parallel", "parallel"), 32))(proj, proj, w_pool, pool_scale)


def _chunk_recompute(uxbc, halo, udt, cw, cb, dtb, alog, first):
    halo = jnp.where(first, 0.0, halo)
    ext = jnp.concatenate([halo, uxbc], 0)
    pre = cb + uxbc * cw[3:4]
    for k in (2, 1, 0):
        pre = pre + pltpu.roll(ext, 3 - k, 0)[CONV_HALO:] * cw[k:k + 1]
    sg = _sigmoid(pre)
    xbc = pre * sg
    dtp = udt[:, :SSD_HEADS] + dtb
    dt = jnp.maximum(dtp, 0.0) + jnp.log(1.0 + jnp.exp(-jnp.abs(dtp)))
    a = -jnp.exp(alog)
    da = dt * a
    tril = (_iota((CHUNK, CHUNK), 0) >= _iota((CHUNK, CHUNK), 1))
    acum = _exact_nn_left(tril.astype(BF), da)
    eye = (_iota((SSD_HEADS, SSD_HEADS), 0) == _iota((SSD_HEADS, SSD_HEADS), 1)).astype(BF)
    acum_t = _exact_nt_left(eye, acum)
    expand = _head_expand_matrix(SSD_HEADS, SSD_INNER)
    acum_e = _exact_nn(acum, expand)
    dt_e = _exact_nn(dt, expand)
    last_e = acum_e[CHUNK - 1:CHUNK]
    return dict(pre=pre, sg=sg, xbc=xbc, dtp=dtp, dt=dt, a=a, acum=acum, acum_t=acum_t, tril=tril,
                dt_e=dt_e, e_a=jnp.exp(acum_e), d_out=jnp.exp(last_e - acum_e), c_dec=jnp.exp(last_e))


def _head_decay(r, h):
    seg = r["acum"][:, h:h + 1] - r["acum_t"][h:h + 1, :]
    return jnp.where(r["tril"], jnp.exp(jnp.minimum(seg, 0.0)), 0.0)


def _ssd_specs(nb, seq, reverse):
    nc = seq // CHUNK
    per = seq // CONV_HALO

    def cidx(c):
        return (nc - 1 - c) if reverse else c

    def row(b, c):
        return b * nc + cidx(c)

    specs = [
        pl.BlockSpec((CHUNK, CONV_CH), lambda b, c: (row(b, c), 1)),
        pl.BlockSpec((CONV_HALO, CONV_CH),
                     lambda b, c: (jnp.maximum(b * per + cidx(c) * (CHUNK // CONV_HALO) - 1, 0), 1)),
        pl.BlockSpec((CHUNK, GROUP_W), lambda b, c: (row(b, c), 1)),
        pl.BlockSpec((CHUNK, GROUP_W), lambda b, c: (row(b, c), 2)),
        pl.BlockSpec((CHUNK, 128), lambda b, c: (row(b, c), OFF_DT // 128)),
    ]
    return specs, row, cidx, nc


def _const_spec(shape):
    return pl.BlockSpec(shape, lambda b, c: (0,) * len(shape))


def _ssd_fwd(proj, conv_w, conv_b, dt_bias, a_log, dskip_e, g_ssd, nb, seq):
    specs, row, cidx, nc = _ssd_specs(nb, seq, reverse=False)

    def body(uxbc_ref, halo_ref, z0_ref, z1_ref, udt_ref, cw_ref, cb_ref, dtb_ref, alog_ref, dsk_ref, gs_ref,
             yssd_ref, yssm_ref, hprev_ref, h_ref, yd_ref):
        c = pl.program_id(1)

        @pl.when(c == 0)
        def _():
            h_ref[...] = jnp.zeros_like(h_ref)

        r = _chunk_recompute(uxbc_ref[...], halo_ref[...], udt_ref[...], cw_ref[...], cb_ref[...],
                             dtb_ref[...], alog_ref[...], c == 0)
        xbc = r["xbc"]
        xs = xbc[:, :SSD_INNER]
        xdt = xs * r["dt_e"]
        xdt_b = xdt.astype(BF)
        xdo_b = (xdt * r["d_out"]).astype(BF)
        hprev_ref[0, 0] = h_ref[...]
        for g in range(2):
            gs = slice(g * GROUP_W, (g + 1) * GROUP_W)
            bg = xbc[:, SSD_INNER + g * SSD_STATE:SSD_INNER + (g + 1) * SSD_STATE].astype(BF)
            cg = xbc[:, SSD_INNER + (2 + g) * SSD_STATE:SSD_INNER + (3 + g) * SSD_STATE].astype(BF)
            scores = _nt(cg, bg)
            hg = h_ref[g]
            y_off = _nn(cg, hg.astype(BF)) * r["e_a"][:, gs]
            for hh in range(8):
                h = g * 8 + hh
                m = (scores * _head_decay(r, h)).astype(BF)
                yd_ref[:, h * SSD_HEAD_DIM:(h + 1) * SSD_HEAD_DIM] = _nn(m, xdt_b[:, h * SSD_HEAD_DIM:(h + 1) * SSD_HEAD_DIM])
            h_ref[g] = hg * r["c_dec"][:, gs] + _tn(bg, xdo_b[:, gs])
            y = yd_ref[:, gs] + y_off + dsk_ref[:, gs] * xs[:, gs]
            yssm_ref[:, gs] = y
            zg = (z0_ref if g == 0 else z1_ref)[...]
            yg = y * (zg * _sigmoid(zg))
            rg = lax.rsqrt(jnp.mean(yg * yg, -1, keepdims=True) + EPS)
            yssd_ref[:, gs] = (yg * rg * gs_ref[:, gs]).astype(BF)

    t = nb * seq
    return pl.pallas_call(
        body, name="ssd_fwd", grid=(nb, nc),
        in_specs=specs + [_const_spec((4, CONV_CH)), _const_spec((1, CONV_CH)), _const_spec((1, SSD_HEADS)),
                          _const_spec((1, SSD_HEADS)), _const_spec((1, SSD_INNER)), _const_spec((1, SSD_INNER))],
        out_specs=[pl.BlockSpec((CHUNK, SSD_INNER), lambda b, c: (row(b, c), 0)),
                   pl.BlockSpec((CHUNK, SSD_INNER), lambda b, c: (row(b, c), 0)),
                   pl.BlockSpec((1, 1, 2, SSD_STATE, GROUP_W), lambda b, c: (b, c, 0, 0, 0))],
        out_shape=[jax.ShapeDtypeStruct((t, SSD_INNER), BF), jax.ShapeDtypeStruct((t, SSD_INNER), F32),
                   jax.ShapeDtypeStruct((nb, nc, 2, SSD_STATE, GROUP_W), F32)],
        scratch_shapes=[pltpu.VMEM((2, SSD_STATE, GROUP_W), F32), pltpu.VMEM((CHUNK, SSD_INNER), F32)],
        compiler_params=_cp(("arbitrary", "arbitrary"), 48),
    )(proj, proj, proj, proj, proj, conv_w, conv_b, dt_bias, a_log, dskip_e, g_ssd)


def _out_proj(y_pool, y_ssd, w_out, x, mod3, g_mlp, seq):
    t, d = x.shape
    tm = 512
    tps = seq // tm if seq >= tm else 1
    tm = min(tm, seq)

    def body(yp_ref, ys_ref, w_ref, x_ref, mod_ref, g_ref, h1_ref, o_ref, u2_ref):
        o = _nn(yp_ref[...], w_ref[0:POOL_WIDTH, :]) + _nn(ys_ref[...], w_ref[POOL_WIDTH:, :])
        o_ref[...] = o.astype(BF)
        h1 = x_ref[...] + mod_ref[0, 2:3, :] * o
        h1_ref[...] = h1
        r = lax.rsqrt(jnp.mean(h1 * h1, -1, keepdims=True) + EPS)
        u2_ref[...] = ((h1 * r * g_ref[...]) * (1.0 + mod_ref[0, 4:5, :]) + mod_ref[0, 3:4, :]).astype(BF)

    row = lambda i: (i, 0)
    return pl.pallas_call(
        body, name="out_proj", grid=(t // tm,),
        in_specs=[pl.BlockSpec((tm, POOL_WIDTH), row), pl.BlockSpec((tm, SSD_INNER), row),
                  pl.BlockSpec(w_out.shape, lambda i: (0, 0)), pl.BlockSpec((tm, d), row),
                  pl.BlockSpec((1, N_MOD, d), lambda i: (i // tps, 0, 0)), pl.BlockSpec((1, d), lambda i: (0, 0))],
        out_specs=[pl.BlockSpec((tm, d), row)] * 3,
        out_shape=[jax.ShapeDtypeStruct((t, d), F32), jax.ShapeDtypeStruct((t, d), BF), jax.ShapeDtypeStruct((t, d), BF)],
        compiler_params=_cp(("parallel",), 48))(y_pool, y_ssd, w_out, x, mod3, g_mlp)


def _mlp_up(u2, w_up4):
    t, d = u2.shape
    tm = min(512, t)
    nk, _, cols = w_up4.shape

    def body(u_ref, w_ref, a_ref):
        a_ref[...] = _nn(u_ref[...], w_ref[0]).astype(BF)

    return pl.pallas_call(
        body, name="mlp_up", grid=(nk, t // tm),
        in_specs=[pl.BlockSpec((tm, d), lambda k, i: (i, 0)), pl.BlockSpec((1, d, cols), lambda k, i: (k, 0, 0))],
        out_specs=pl.BlockSpec((tm, cols), lambda k, i: (i, k)),
        out_shape=jax.ShapeDtypeStruct((t, nk * cols), BF),
        compiler_params=_cp(("parallel", "parallel"), 32))(u2, w_up4)


def _mlp_down_loss(a_up, w_down, h1, mod3, g_final, target, seq):
    t, d = h1.shape
    nb = t // seq
    tm = min(256, seq)
    tps = seq // tm

    def body(a_ref, w_ref, h1_ref, mod_ref, g_ref, tg_ref, ddn_ref, dh2_ref, sq_ref, gg_ref, dgf_ref):
        i = pl.program_id(0)
        f = jnp.square(jnp.maximum(a_ref[...], 0))
        dn = _nn(f, w_ref[...])
        gate = mod_ref[0, 5:6, :]
        h2 = h1_ref[...] + gate * dn
        r = lax.rsqrt(jnp.mean(h2 * h2, -1, keepdims=True) + EPS)
        hh = h2 * r
        err = hh * g_ref[...] - tg_ref[...]
        dy = err * (1.0 / d)
        dhat = dy * g_ref[...]
        dh2 = r * (dhat - hh * jnp.mean(dhat * hh, -1, keepdims=True))
        dh2_ref[...] = dh2
        ddn_ref[...] = (dh2 * gate).astype(BF)

        @pl.when(i == 0)
        def _():
            sq_ref[...] = jnp.zeros_like(sq_ref)
            gg_ref[...] = jnp.zeros_like(gg_ref)

        @pl.when(i % tps == 0)
        def _():
            dgf_ref[...] = jnp.zeros_like(dgf_ref)

        sq_ref[...] += jnp.sum(err * err, 0, keepdims=True)
        gg_ref[...] += jnp.sum(dy * hh, 0, keepdims=True)
        dgf_ref[0] += jnp.sum(dh2 * dn, 0, keepdims=True)

    row = lambda i: (i, 0)
    vec = pl.BlockSpec((1, d), lambda i: (0, 0))
    return pl.pallas_call(
        body, name="mlp_down_loss", grid=(t // tm,),
        in_specs=[pl.BlockSpec((tm, D_FF), row), pl.BlockSpec(w_down.shape, lambda i: (0, 0)), pl.BlockSpec((tm, d), row),
                  pl.BlockSpec((1, N_MOD, d), lambda i: (i // tps, 0, 0)), vec, pl.BlockSpec((tm, d), row)],
        out_specs=[pl.BlockSpec((tm, d), row), pl.BlockSpec((tm, d), row), vec, vec,
                   pl.BlockSpec((1, 1, d), lambda i: (i // tps, 0, 0))],
        out_shape=[jax.ShapeDtypeStruct((t, d), BF), jax.ShapeDtypeStruct((t, d), F32), jax.ShapeDtypeStruct((1, d), F32),
                   jax.ShapeDtypeStruct((1, d), F32), jax.ShapeDtypeStruct((nb, 1, d), F32)],
        compiler_params=_cp(("arbitrary",), 56))(a_up, w_down, h1, mod3, g_final, target)


def _tn_matmul(a, b, tk, tn, name, square_relu=False, out3=False):
    t, kdim = a.shape
    ndim = b.shape[1]

    def body(a_ref, b_ref, o_ref):
        av = a_ref[...]
        if square_relu:
            av = jnp.square(jnp.maximum(av, 0))
        res = _tn(av, b_ref[...]).astype(BF)
        if out3:
            o_ref[0] = res
        else:
            o_ref[...] = res

    if out3:
        out_spec = pl.BlockSpec((1, tk, tn), lambda j, i: (j, i, 0))
        out_shape = jax.ShapeDtypeStruct((ndim // tn, kdim, tn), BF)
    else:
        out_spec = pl.BlockSpec((tk, tn), lambda j, i: (i, j))
        out_shape = jax.ShapeDtypeStruct((kdim, ndim), BF)
    return pl.pallas_call(
        body, name=name, grid=(ndim // tn, kdim // tk),
        in_specs=[pl.BlockSpec((t, tk), lambda j, i: (0, i)), pl.BlockSpec((t, tn), lambda j, i: (0, j))],
        out_specs=out_spec, out_shape=out_shape,
        compiler_params=_cp(("parallel", "parallel"), 56))(a, b)


def _mlp_down_bwd(d_dn, w_down4, a_up):
    t, d = d_dn.shape
    tm = min(512, t)
    nk, rows, _ = w_down4.shape

    def body(g_ref, w_ref, a_ref, o_ref):
        df = _nt(g_ref[...], w_ref[0])
        o_ref[...] = (df * (2.0 * jnp.maximum(a_ref[...], 0).astype(F32))).astype(BF)

    return pl.pallas_call(
        body, name="mlp_down_bwd", grid=(nk, t // tm),
        in_specs=[pl.BlockSpec((tm, d), lambda k, i: (i, 0)), pl.BlockSpec((1, rows, d), lambda k, i: (k, 0, 0)),
                  pl.BlockSpec((tm, rows), lambda k, i: (i, k))],
        out_specs=pl.BlockSpec((tm, rows), lambda k, i: (i, k)),
        out_shape=jax.ShapeDtypeStruct((t, nk * rows), BF),
        compiler_params=_cp(("parallel", "parallel"), 32))(d_dn, w_down4, a_up)


def _mlp_up_bwd(d_a, w_up4, h1, dh2, o, mod3, g_mlp, seq):
    t, d = h1.shape
    nb = t // seq
    tm = min(256, seq)
    tps = seq // tm
    nk = w_up4.shape[0]
    cols = w_up4.shape[2]

    def body(da_ref, w_ref, h1_ref, dh2_ref, o_ref, mod_ref, g_ref, dh1_ref, do_ref, acc_ref, gg_ref):
        i = pl.program_id(0)
        du = _nt(da_ref[:, 0:cols], w_ref[0])
        for k in range(1, nk):
            du = du + _nt(da_ref[:, k * cols:(k + 1) * cols], w_ref[k])
        h1 = h1_ref[...]
        r = lax.rsqrt(jnp.mean(h1 * h1, -1, keepdims=True) + EPS)
        hh = h1 * r
        n2 = hh * g_ref[...]
        dn2 = du * (1.0 + mod_ref[0, 4:5, :])
        dhat = dn2 * g_ref[...]
        dh1 = dh2_ref[...] + r * (dhat - hh * jnp.mean(dhat * hh, -1, keepdims=True))
        dh1_ref[...] = dh1
        do_ref[...] = (dh1 * mod_ref[0, 2:3, :]).astype(BF)

        @pl.when(i == 0)
        def _():
            gg_ref[...] = jnp.zeros_like(gg_ref)

        @pl.when(i % tps == 0)
        def _():
            acc_ref[...] = jnp.zeros_like(acc_ref)

        gg_ref[...] += jnp.sum(dn2 * hh, 0, keepdims=True)
        acc_ref[0, 0:1, :] += jnp.sum(du, 0, keepdims=True)
        acc_ref[0, 1:2, :] += jnp.sum(du * n2, 0, keepdims=True)
        acc_ref[0, 2:3, :] += jnp.sum(dh1 * o_ref[...].astype(F32), 0, keepdims=True)

    row = lambda i: (i, 0)
    vec = pl.BlockSpec((1, d), lambda i: (0, 0))
    return pl.pallas_call(
        body, name="mlp_up_bwd", grid=(t // tm,),
        in_specs=[pl.BlockSpec((tm, D_FF), row), pl.BlockSpec(w_up4.shape, lambda i: (0, 0, 0)), pl.BlockSpec((tm, d), row),
                  pl.BlockSpec((tm, d), row), pl.BlockSpec((tm, d), row),
                  pl.BlockSpec((1, N_MOD, d), lambda i: (i // tps, 0, 0)), vec],
        out_specs=[pl.BlockSpec((tm, d), row), pl.BlockSpec((tm, d), row),
                   pl.BlockSpec((1, 8, d), lambda i: (i // tps, 0, 0)), vec],
        out_shape=[jax.ShapeDtypeStruct((t, d), F32), jax.ShapeDtypeStruct((t, d), BF),
                   jax.ShapeDtypeStruct((nb, 8, d), F32), jax.ShapeDtypeStruct((1, d), F32)],
        compiler_params=_cp(("arbitrary",), 56))(d_a, w_up4, h1, dh2, o, mod3, g_mlp)


def _out_proj_bwd(d_o, w_out):
    t, d = d_o.shape
    tm = min(512, t)

    def body(g_ref, w_ref, dp_ref, ds_ref):
        gv = g_ref[...]
        dp_ref[...] = _nt(gv, w_ref[0:POOL_WIDTH, :])
        ds_ref[...] = _nt(gv, w_ref[POOL_WIDTH:, :])

    row = lambda i: (i, 0)
    return pl.pallas_call(
        body, name="out_proj_bwd", grid=(t // tm,),
        in_specs=[pl.BlockSpec((tm, d), row), pl.BlockSpec(w_out.shape, lambda i: (0, 0))],
        out_specs=[pl.BlockSpec((tm, POOL_WIDTH), row), pl.BlockSpec((tm, SSD_INNER), row)],
        out_shape=[jax.ShapeDtypeStruct((t, POOL_WIDTH), F32), jax.ShapeDtypeStruct((t, SSD_INNER), F32)],
        compiler_params=_cp(("parallel",), 32))(d_o, w_out)


def _pool_bwd(d_ypool, p, w_pool, pool_scale, nb, seq):
    ts = _pool_tile(seq)
    nt = seq // ts
    hb = ts // HALO
    last_block = nb * seq // HALO - 1

    def body(dy_ref, halo_ref, p_ref, wp_ref, ps_ref, du_ref, gw_ref, gs_ref):
        b = pl.program_id(0)
        i = pl.program_id(1)

        @pl.when((b == 0) & (i == 0))
        def _():
            gw_ref[...] = jnp.zeros_like(gw_ref)
            gs_ref[...] = jnp.zeros_like(gs_ref)

        halo = jnp.where(i == nt - 1, 0.0, halo_ref[...])
        dy = dy_ref[...]
        ext = jnp.concatenate([dy, halo], 0)
        tpos = i * ts + _iota((ts + HALO, 1), 0)
        n_ext = ts + HALO
        for g, w in enumerate(POOL_WINDOWS):
            gs = slice(g * POOL_GROUP, (g + 1) * POOL_GROUP)
            wg = wp_ref[g].astype(BF)
            pg = p_ref[:, gs]
            pw = _nn(pg, wg)
            gs_ref[:, gs] += jnp.sum(dy[:, gs] * pw, 0, keepdims=True)
            dpw = (ext[:, gs] * ps_ref[:, gs]).astype(BF)
            gw_ref[g] += _tn(pg, dpw[:ts])
            dp = _nt(dpw, wg)
            cnt = jnp.minimum(tpos + 1, w).astype(F32)
            s = dp / cnt
            sh = 1
            while sh < w:
                s = s + pltpu.roll(s, n_ext - sh, 0)
                sh *= 2
            du_ref[:, gs] = (s[:ts] - dp[:ts]).astype(BF)

    return pl.pallas_call(
        body, name="pool_bwd", grid=(nb, nt),
        in_specs=[pl.BlockSpec((ts, POOL_WIDTH), lambda b, i: (b * nt + i, 0)),
                  pl.BlockSpec((HALO, POOL_WIDTH), lambda b, i: (jnp.minimum((b * nt + i + 1) * hb, last_block), 0)),
                  pl.BlockSpec((ts, POOL_WIDTH), lambda b, i: (b * nt + i, 0)),
                  pl.BlockSpec((4, POOL_GROUP, POOL_GROUP), lambda b, i: (0, 0, 0)),
                  pl.BlockSpec((1, POOL_WIDTH), lambda b, i: (0, 0))],
        out_specs=[pl.BlockSpec((ts, POOL_WIDTH), lambda b, i: (b * nt + i, 0)),
                   pl.BlockSpec((4, POOL_GROUP, POOL_GROUP), lambda b, i: (0, 0, 0)),
                   pl.BlockSpec((1, POOL_WIDTH), lambda b, i: (0, 0))],
        out_shape=[jax.ShapeDtypeStruct((nb * seq, POOL_WIDTH), BF), jax.ShapeDtypeStruct((4, POOL_GROUP, POOL_GROUP), F32),
                   jax.ShapeDtypeStruct((1, POOL_WIDTH), F32)],
        compiler_params=_cp(("arbitrary", "arbitrary"), 32))(d_ypool, d_ypool, p, w_pool, pool_scale)


def _ssd_bwd(proj, d_yssd, yssm, h_prev, conv_w, conv_b, dt_bias, a_log, dskip_e, g_ssd, nb, seq):
    specs, row, cidx, nc = _ssd_specs(nb, seq, reverse=True)

    def body(uxbc_ref, halo_ref, z0_ref, z1_ref, udt_ref, dys_ref, yssm_ref, hprev_ref,
             cw_ref, cb_ref, dtb_ref, alog_ref, dsk_ref, gs_ref,
             dz_ref, dpre_ref, dudt_ref, ggs_ref, gdsk_ref, ga_ref, gdtb_ref,
             g_ref, dxdt_ref, dyv_ref):
        b = pl.program_id(0)
        c = pl.program_id(1)

        @pl.when(c == 0)
        def _():
            g_ref[...] = jnp.zeros_like(g_ref)

        @pl.when((b == 0) & (c == 0))
        def _():
            ggs_ref[...] = jnp.zeros_like(ggs_ref)
            gdsk_ref[...] = jnp.zeros_like(gdsk_ref)
            ga_ref[...] = jnp.zeros_like(ga_ref)
            gdtb_ref[...] = jnp.zeros_like(gdtb_ref)

        r = _chunk_recompute(uxbc_ref[...], halo_ref[...], udt_ref[...], cw_ref[...], cb_ref[...],
                             dtb_ref[...], alog_ref[...], c == nc - 1)
        xbc = r["xbc"]
        xs = xbc[:, :SSD_INNER]
        dt_e = r["dt_e"]
        xdt = xs * dt_e
        xdt_b = xdt.astype(BF)
        reduce_m = _head_reduce_matrix(GROUP_W, 8)
        onehot16 = lambda h: (_iota((1, SSD_HEADS), 1) == h).astype(F32)
        onecol16 = lambda h: (_iota((SSD_HEADS, 1), 0) == h).astype(F32)

        d_acum = jnp.zeros((CHUNK, SSD_HEADS), F32)
        d_acum_t = jnp.zeros((SSD_HEADS, CHUNK), F32)
        d_alast = jnp.zeros((1, SSD_HEADS), F32)
        place8 = lambda g: (_iota((8, SSD_HEADS), 1) == _iota((8, SSD_HEADS), 0) + 8 * g).astype(BF)
        d_b, d_c = [], []
        for g in range(2):
            gs = slice(g * GROUP_W, (g + 1) * GROUP_W)
            zg = (z0_ref if g == 0 else z1_ref)[...]
            sz = _sigmoid(zg)
            silu_z = zg * sz
            ys = yssm_ref[:, gs]
            yg = ys * silu_z
            rg = lax.rsqrt(jnp.mean(yg * yg, -1, keepdims=True) + EPS)
            yh = yg * rg
            dys = dys_ref[:, gs]
            ggs_ref[:, gs] += jnp.sum(dys * yh, 0, keepdims=True)
            dyh = dys * gs_ref[:, gs]
            dyg = rg * (dyh - yh * jnp.mean(dyh * yh, -1, keepdims=True))
            dy = dyg * silu_z
            dz_ref[:, gs] = (dyg * ys * (sz * (1.0 + zg * (1.0 - sz)))).astype(BF)
            gdsk_ref[:, gs] += jnp.sum(dy * xs[:, gs], 0, keepdims=True)
            dyv_ref[:, gs] = dy
            dy_b = dy.astype(BF)

            bg = xbc[:, SSD_INNER + g * SSD_STATE:SSD_INNER + (g + 1) * SSD_STATE].astype(BF)
            cg = xbc[:, SSD_INNER + (2 + g) * SSD_STATE:SSD_INNER + (3 + g) * SSD_STATE].astype(BF)
            scores = _nt(cg, bg)
            hg = hprev_ref[0, 0, g]
            hg_b = hg.astype(BF)
            gg = g_ref[g]
            gg_b = gg.astype(BF)
            e_a = r["e_a"][:, gs]
            d_out = r["d_out"][:, gs]
            c_dec = r["c_dec"][:, gs]
            zc = _nn(cg, hg_b)
            wv = e_a * dy
            wv_b = wv.astype(BF)
            da_g = _exact_nn(wv * zc, reduce_m)
            dcg = _nt(wv_b, hg_b)
            d_hprev = _tn(cg, wv_b)
            vg = _nn(bg, gg_b)
            dxdt_g = d_out * vg
            dd_out = _exact_nn(xdt[:, gs] * vg, reduce_m)
            dbg = _nt((xdt[:, gs] * d_out).astype(BF), gg_b)
            dcd = _exact_nn(jnp.sum(gg * hg, 0, keepdims=True), reduce_m)
            d_out8 = jnp.exp(r["acum"][CHUNK - 1:CHUNK, 8 * g:8 * g + 8] - r["acum"][:, 8 * g:8 * g + 8])
            c_dec8 = jnp.exp(r["acum"][CHUNK - 1:CHUNK, 8 * g:8 * g + 8])
            t8 = dd_out * d_out8
            d_alast = d_alast + _exact_nn(jnp.sum(t8, 0, keepdims=True) + dcd * c_dec8, place8(g))
            d_acum = d_acum + _exact_nn(da_g - t8, place8(g))
            dsc = jnp.zeros((CHUNK, CHUNK), F32)
            for hh in range(8):
                h = g * 8 + hh
                hs = slice(h * SSD_HEAD_DIM, (h + 1) * SSD_HEAD_DIM)
                lam = _head_decay(r, h)
                m = scores * lam
                dyh_b = dy_b[:, hh * SSD_HEAD_DIM:(hh + 1) * SSD_HEAD_DIM]
                dm = _nt(dyh_b, xdt_b[:, hs])
                tm_ = dm * m
                d_acum = d_acum + jnp.sum(tm_, 1, keepdims=True) * onehot16(h)
                d_acum_t = d_acum_t + onecol16(h) * jnp.sum(tm_, 0, keepdims=True)
                dsc = dsc + dm * lam
                dxdt_ref[:, hs] = _tn(m.astype(BF), dyh_b) + dxdt_g[:, hh * SSD_HEAD_DIM:(hh + 1) * SSD_HEAD_DIM]
            dsc_b = dsc.astype(BF)
            d_c.append(dcg + _nn(dsc_b, bg))
            d_b.append(dbg + _tn(dsc_b, cg))
            g_ref[g] = d_hprev + c_dec * gg

        eye = (_iota((CHUNK, CHUNK), 0) == _iota((CHUNK, CHUNK), 1)).astype(BF)
        d_acum = d_acum - _exact_nt_left(eye, d_acum_t)
        is_last = (_iota((CHUNK, 1), 0) == CHUNK - 1).astype(F32)
        d_acum = d_acum + is_last * d_alast
        triu = (_iota((CHUNK, CHUNK), 0) <= _iota((CHUNK, CHUNK), 1)).astype(BF)
        d_da = _exact_nn_left(triu, d_acum)
        dt = r["dt"]
        ga_ref[...] += jnp.sum(d_da * dt, 0, keepdims=True)
        dxdt = dxdt_ref[...]
        reduce16 = _head_reduce_matrix(SSD_INNER, SSD_HEADS)
        d_dt = d_da * r["a"] + _exact_nn(dxdt * xs, reduce16)
        d_udt = d_dt * _sigmoid(r["dtp"])
        gdtb_ref[...] += jnp.sum(d_udt, 0, keepdims=True)
        dudt_ref[...] = jnp.zeros_like(dudt_ref)
        dudt_ref[:, 0:SSD_HEADS] = d_udt.astype(BF)
        pre, sg = r["pre"], r["sg"]
        dsilu = sg * (1.0 + pre * (1.0 - sg))
        dpre_ref[:, 0:SSD_INNER] = (dsk_ref[...] * dyv_ref[...] + dxdt * dt_e) * dsilu[:, 0:SSD_INNER]
        for g in range(2):
            bs = slice(SSD_INNER + g * SSD_STATE, SSD_INNER + (g + 1) * SSD_STATE)
            cs = slice(SSD_INNER + (2 + g) * SSD_STATE, SSD_INNER + (3 + g) * SSD_STATE)
            dpre_ref[:, bs] = d_b[g] * dsilu[:, bs]
            dpre_ref[:, cs] = d_c[g] * dsilu[:, cs]

    t = nb * seq
    vec = _const_spec((1, SSD_INNER))
    small = _const_spec((1, SSD_HEADS))
    return pl.pallas_call(
        body, name="ssd_bwd", grid=(nb, nc),
        in_specs=specs + [pl.BlockSpec((CHUNK, SSD_INNER), lambda b, c: (row(b, c), 0)),
                          pl.BlockSpec((CHUNK, SSD_INNER), lambda b, c: (row(b, c), 0)),
                          pl.BlockSpec((1, 1, 2, SSD_STATE, GROUP_W), lambda b, c: (b, cidx(c), 0, 0, 0)),
                          _const_spec((4, CONV_CH)), _const_spec((1, CONV_CH)), small, small, vec, vec],
        out_specs=[pl.BlockSpec((CHUNK, SSD_INNER), lambda b, c: (row(b, c), 0)),
                   pl.BlockSpec((CHUNK, CONV_CH), lambda b, c: (row(b, c), 0)),
                   pl.BlockSpec((CHUNK, 128), lambda b, c: (row(b, c), 0)),
                   vec, vec, small, small],
        out_shape=[jax.ShapeDtypeStruct((t, SSD_INNER), BF), jax.ShapeDtypeStruct((t, CONV_CH), F32),
                   jax.ShapeDtypeStruct((t, 128), BF), jax.ShapeDtypeStruct((1, SSD_INNER), F32),
                   jax.ShapeDtypeStruct((1, SSD_INNER), F32), jax.ShapeDtypeStruct((1, SSD_HEADS), F32),
                   jax.ShapeDtypeStruct((1, SSD_HEADS), F32)],
        scratch_shapes=[pltpu.VMEM((2, SSD_STATE, GROUP_W), F32), pltpu.VMEM((CHUNK, SSD_INNER), F32),
                        pltpu.VMEM((CHUNK, SSD_INNER), F32)],
        compiler_params=_cp(("arbitrary", "arbitrary"), 48),
    )(proj, proj, proj, proj, proj, d_yssd, yssm, h_prev, conv_w, conv_b, dt_bias, a_log, dskip_e, g_ssd)


def _conv_bwd(d_pre, proj, conv_w, nb, seq):
    ts = _pool_tile(seq)
    nt = seq // ts
    hb = ts // CONV_HALO
    last_block = nb * seq // CONV_HALO - 1
    n_ext = ts + CONV_HALO

    def body(dp_ref, dnext_ref, u_ref, uprev_ref, cw_ref, du_ref, gw_ref, gb_ref):
        b = pl.program_id(0)
        i = pl.program_id(1)

        @pl.when((b == 0) & (i == 0))
        def _():
            gw_ref[...] = jnp.zeros_like(gw_ref)
            gb_ref[...] = jnp.zeros_like(gb_ref)

        dp = dp_ref[...]
        ext_d = jnp.concatenate([dp, jnp.where(i == nt - 1, 0.0, dnext_ref[...])], 0)
        ext_u = jnp.concatenate([jnp.where(i == 0, 0.0, uprev_ref[...]), u_ref[...]], 0)
        cw = cw_ref[...]
        du = dp * cw[3:4]
        gw_ref[3:4, :] += jnp.sum(dp * u_ref[...], 0, keepdims=True)
        for k in (2, 1, 0):
            j = 3 - k
            du = du + pltpu.roll(ext_d, n_ext - j, 0)[:ts] * cw[k:k + 1]
            gw_ref[k:k + 1, :] += jnp.sum(dp * pltpu.roll(ext_u, j, 0)[CONV_HALO:], 0, keepdims=True)
        gb_ref[...] += jnp.sum(dp, 0, keepdims=True)
        du_ref[...] = du.astype(BF)

    return pl.pallas_call(
        body, name="conv_bwd", grid=(nb, nt),
        in_specs=[pl.BlockSpec((ts, CONV_CH), lambda b, i: (b * nt + i, 0)),
                  pl.BlockSpec((CONV_HALO, CONV_CH), lambda b, i: (jnp.minimum((b * nt + i + 1) * hb, last_block), 0)),
                  pl.BlockSpec((ts, CONV_CH), lambda b, i: (b * nt + i, 1)),
                  pl.BlockSpec((CONV_HALO, CONV_CH), lambda b, i: (jnp.maximum((b * nt + i) * hb - 1, 0), 1)),
                  pl.BlockSpec((4, CONV_CH), lambda b, i: (0, 0))],
        out_specs=[pl.BlockSpec((ts, CONV_CH), lambda b, i: (b * nt + i, 0)),
                   pl.BlockSpec((8, CONV_CH), lambda b, i: (0, 0)), pl.BlockSpec((1, CONV_CH), lambda b, i: (0, 0))],
        out_shape=[jax.ShapeDtypeStruct((nb * seq, CONV_CH), BF), jax.ShapeDtypeStruct((8, CONV_CH), F32),
                   jax.ShapeDtypeStruct((1, CONV_CH), F32)],
        compiler_params=_cp(("arbitrary", "arbitrary"), 48))(d_pre, d_pre, proj, proj, conv_w)


def _in_proj_bwd(d_proj, w_in_p, x, dh1, mod3, g_mix, seq):
    t, d = x.shape
    nb = t // seq
    tm = min(256, seq)
    tps = seq // tm

    def body(dp_ref, w_ref, x_ref, dh1_ref, mod_ref, g_ref, gx_ref, acc_ref, gg_ref):
        i = pl.program_id(0)
        du = _nt(dp_ref[...], w_ref[...])
        xv = x_ref[...]
        r = lax.rsqrt(jnp.mean(xv * xv, -1, keepdims=True) + EPS)
        hh = xv * r
        n1 = hh * g_ref[...]
        dn1 = du * (1.0 + mod_ref[0, 1:2, :])
        dhat = dn1 * g_ref[...]
        gx_ref[...] = dh1_ref[...] + r * (dhat - hh * jnp.mean(dhat * hh, -1, keepdims=True))

        @pl.when(i == 0)
        def _():
            gg_ref[...] = jnp.zeros_like(gg_ref)

        @pl.when(i % tps == 0)
        def _():
            acc_ref[...] = jnp.zeros_like(acc_ref)

        gg_ref[...] += jnp.sum(dn1 * hh, 0, keepdims=True)
        acc_ref[0, 0:1, :] += jnp.sum(du, 0, keepdims=True)
        acc_ref[0, 1:2, :] += jnp.sum(du * n1, 0, keepdims=True)

    row = lambda i: (i, 0)
    vec = pl.BlockSpec((1, d), lambda i: (0, 0))
    return pl.pallas_call(
        body, name="in_proj_bwd", grid=(t // tm,),
        in_specs=[pl.BlockSpec((tm, IN_PAD), row), pl.BlockSpec(w_in_p.shape, lambda i: (0, 0)), pl.BlockSpec((tm, d), row),
                  pl.BlockSpec((tm, d), row), pl.BlockSpec((1, N_MOD, d), lambda i: (i // tps, 0, 0)), vec],
        out_specs=[pl.BlockSpec((tm, d), row), pl.BlockSpec((1, 8, d), lambda i: (i // tps, 0, 0)), vec],
        out_shape=[jax.ShapeDtypeStruct((t, d), F32), jax.ShapeDtypeStruct((nb, 8, d), F32),
                   jax.ShapeDtypeStruct((1, d), F32)],
        compiler_params=_cp(("arbitrary",), 48))(d_proj, w_in_p, x, dh1, mod3, g_mix)


_VEC_LAYOUT = (("g_mix", 1024), ("conv_b", 1536), ("g_ssd", 1024), ("pool_scale", 512), ("g_mlp", 1024),
               ("g_final", 1024), ("dt_bias", 128), ("a_log", 128), ("d_skip_lanes", 1024), ("sq_err", 1024))
_VEC_OFFSET = {}
_off = 0
for _name, _n in _VEC_LAYOUT:
    _VEC_OFFSET[_name] = _off
    _off += _n
_VEC_LANES = _off
_SMALL_PARAMS = ("b_ada", "g_mix", "conv_w", "conv_b", "dt_bias", "a_log", "d_skip", "g_ssd", "w_pool", "pool_scale",
                 "g_mlp", "g_final")


def _pack_vec(parts):
    cols = []
    for name, n in _VEC_LAYOUT:
        v = parts[name]
        if v.shape[1] < n:
            v = jnp.pad(v, ((0, 0), (0, n - v.shape[1])))
        cols.append(v)
    return jnp.concatenate(cols, 1)


def _small_adam(vec_all, wpool_all, convw_all, dmod_all, params):
    names = _SMALL_PARAMS
    nin = 4 + 3 * len(names)

    def body(*refs):
        vec_ref, wp_ref, cw_ref, dm_ref = refs[:4]
        prm = {n: refs[4 + 3 * i:7 + 3 * i] for i, n in enumerate(names)}
        loss_ref = refs[nin]
        outs = {n: refs[nin + 1 + 4 * i:nin + 5 + 4 * i] for i, n in enumerate(names)}
        vsum = vec_ref[0]
        for s in range(1, N_DEV):
            vsum = vsum + vec_ref[s]

        def lanes(name, n):
            off = _VEC_OFFSET[name]
            return vsum[:, off:off + n]

        grads = {n: lanes(n, prm[n][0].shape[1]) for n in ("g_mix", "conv_b", "g_ssd", "pool_scale", "g_mlp", "g_final", "dt_bias")}
        grads["a_log"] = lanes("a_log", SSD_HEADS) * (-jnp.exp(prm["a_log"][0][...]))
        per_lane = jnp.broadcast_to(lanes("d_skip_lanes", SSD_INNER), (8, SSD_INNER))
        grads["d_skip"] = _exact_nn(per_lane, _head_reduce_matrix(SSD_INNER, SSD_HEADS))[0:1]
        gwp = wp_ref[0]
        gcw = cw_ref[0]
        gb = jnp.sum(dm_ref[0], 0, keepdims=True)
        for s in range(1, N_DEV):
            gwp = gwp + wp_ref[s]
            gcw = gcw + cw_ref[s]
            gb = gb + jnp.sum(dm_ref[s], 0, keepdims=True)
        grads["w_pool"] = gwp
        grads["conv_w"] = gcw[0:4]
        grads["b_ada"] = gb
        total = jnp.sum(lanes("sq_err", D_MODEL), 1, keepdims=True) * (0.5 / D_MODEL)
        loss_ref[...] = jnp.broadcast_to(total, loss_ref.shape)
        for n in names:
            w_ref, m_ref, v_ref = prm[n]
            g = grads[n]
            d, m2, v2 = _adam_math(w_ref[...], g, m_ref[...], v_ref[...])
            g_ref, d_ref, m2_ref, v2_ref = outs[n]
            g_ref[...] = g
            d_ref[...] = d
            m2_ref[...] = m2
            v2_ref[...] = v2

    flat = [vec_all, wpool_all, convw_all, dmod_all]
    out_shape = [jax.ShapeDtypeStruct((1, 128), F32)]
    for n in names:
        flat += list(params[n])
        out_shape += [jax.ShapeDtypeStruct(params[n][0].shape, F32)] * 4
    vm = pl.BlockSpec(memory_space=pltpu.VMEM)
    res = pl.pallas_call(body, name="small_adam", out_shape=out_shape, in_specs=[vm] * len(flat),
                         out_specs=[vm] * len(out_shape), compiler_params=_cp(vmem_mb=48))(*flat)
    return res[0], {n: res[1 + 4 * i:5 + 4 * i] for i, n in enumerate(names)}


_WEIGHTS = ("w_ada", "b_ada", "g_mix", "w_in", "conv_w", "conv_b", "dt_bias", "a_log", "d_skip", "g_ssd", "w_pool",
            "pool_scale", "w_out", "g_mlp", "w_up", "w_down", "g_final")


def _local_step(x2, tg2, mod3, seq, w_in_p, w_out_f, w_up4, w_down4, conv_w_full, sp):
    t, d = x2.shape
    nb = t // seq
    w_down_f = w_down4.reshape(D_FF, d)
    dskip_e = jnp.repeat(sp["d_skip"], SSD_HEAD_DIM, axis=1)
    proj, u1 = _in_proj(x2, mod3, sp["g_mix"], w_in_p, seq)
    y_pool, p = _pool_fwd(proj, sp["w_pool"], sp["pool_scale"], nb, seq)
    y_ssd, yssm, h_prev = _ssd_fwd(proj, conv_w_full, sp["conv_b"], sp["dt_bias"], sp["a_log"], dskip_e, sp["g_ssd"], nb, seq)
    h1, o, u2 = _out_proj(y_pool, y_ssd, w_out_f, x2, mod3, sp["g_mlp"], seq)
    a_up = _mlp_up(u2, w_up4)
    d_dn, dh2, sq, gg_final, d_gf = _mlp_down_loss(a_up, w_down_f, h1, mod3, sp["g_final"], tg2, seq)

    gw_down = _tn_matmul(a_up, d_dn, 512, d, "grad_w_down", square_relu=True)
    d_a = _mlp_down_bwd(d_dn, w_down4, a_up)
    gw_up4 = _tn_matmul(u2, d_a, 512, d, "grad_w_up", out3=True)
    dh1, d_o, accf, gg_mlp = _mlp_up_bwd(d_a, w_up4, h1, dh2, o, mod3, sp["g_mlp"], seq)
    gw_out_pool = _tn_matmul(y_pool, d_o, 512, d, "grad_w_out_pool")
    gw_out_ssd = _tn_matmul(y_ssd, d_o, 512, d, "grad_w_out_ssd")
    d_ypool, d_yssd = _out_proj_bwd(d_o, w_out_f)
    d_upool, gw_pool, g_ps = _pool_bwd(d_ypool, p, sp["w_pool"], sp["pool_scale"], nb, seq)
    d_z, d_pre, d_udt, gg_ssd, gdsk, ga, gdtb = _ssd_bwd(proj, d_yssd, yssm, h_prev, conv_w_full, sp["conv_b"],
                                                        sp["dt_bias"], sp["a_log"], dskip_e, sp["g_ssd"], nb, seq)
    d_uxbc, gconvw, gconvb = _conv_bwd(d_pre, proj, conv_w_full, nb, seq)
    d_proj = jnp.concatenate([d_upool, d_z, d_uxbc, d_udt], 1)
    gw_in_p = _tn_matmul(u1, d_proj, 512, 640, "grad_w_in")
    gx, accm, gg_mix = _in_proj_bwd(d_proj, w_in_p, x2, dh1, mod3, sp["g_mix"], seq)

    d_mod = jnp.concatenate([accm[:, 0], accm[:, 1], accf[:, 2], accf[:, 0], accf[:, 1], d_gf[:, 0]], 1)
    vec = _pack_vec({"g_mix": gg_mix, "conv_b": gconvb, "g_ssd": gg_ssd, "pool_scale": g_ps, "g_mlp": gg_mlp,
                     "g_final": gg_final, "dt_bias": gdtb, "a_log": ga, "d_skip_lanes": gdsk, "sq_err": sq})
    big = dict(w_in=gw_in_p, w_out=jnp.concatenate([gw_out_pool, gw_out_ssd], 0), w_up=gw_up4, w_down=gw_down)
    return gx, d_mod, vec, gw_pool, gconvw, big


def kernel(x, c, w_ada, b_ada, g_mix, w_in, conv_w, conv_b, dt_bias, a_log, d_skip, g_ssd, w_pool, pool_scale, w_out, g_mlp, w_up, w_down, g_final, loss_target, m_w_ada, m_b_ada, m_g_mix, m_w_in, m_conv_w, m_conv_b, m_dt_bias, m_a_log, m_d_skip, m_g_ssd, m_w_pool, m_pool_scale, m_w_out, m_g_mlp, m_w_up, m_w_down, m_g_final, v_w_ada, v_b_ada, v_g_mix, v_w_in, v_conv_w, v_conv_b, v_dt_bias, v_a_log, v_d_skip, v_g_ssd, v_w_pool, v_pool_scale, v_w_out, v_g_mlp, v_w_up, v_w_down, v_g_final):
    nb, seq, d = x.shape
    t = nb * seq
    xi, yi, ci = _mesh_pos()
    chip = 2 * xi + yi
    me = 4 * xi + 2 * yi + ci
    ada_cols = w_ada.shape[2]
    conv_cols = conv_w.shape[2]
    in_cols = w_in.shape[2]

    c8, convw8 = _all_gather_small([c, conv_w[0]])
    c_all = c8.reshape(N_DEV * nb, d)
    conv_w_full = convw8[0::2].transpose(1, 0, 2).reshape(4, N_CHIPS * conv_cols)
    b_shard = lax.dynamic_slice(b_ada, (0, chip * ada_cols), (1, ada_cols))
    mod_part, c_act = _ada_mod(c_all, w_ada[0], b_shard)
    (mod8,) = _all_gather_small([mod_part])
    mod_all = mod8[0::2].transpose(1, 0, 2).reshape(N_DEV * nb, N_CHIPS * ada_cols)
    mod3 = lax.dynamic_slice(mod_all, (nb * me, 0), (nb, N_CHIPS * ada_cols)).reshape(nb, N_MOD, d)

    w_in4, w_out4, w_up4, w_down4 = _weight_all_gather(
        [w_in[0].astype(BF), w_out[0].astype(BF), w_up[0].astype(BF), w_down[0].astype(BF)])
    w_in_p = jnp.pad(w_in4.transpose(1, 0, 2).reshape(d, N_CHIPS * in_cols), ((0, 0), (0, IN_PAD - N_CHIPS * in_cols)))
    w_out_f = w_out4.reshape(N_CHIPS * w_out.shape[1], d)

    sp = dict(g_mix=g_mix, conv_b=conv_b, dt_bias=dt_bias, a_log=a_log, d_skip=d_skip, g_ssd=g_ssd,
              w_pool=w_pool[0], pool_scale=pool_scale, g_mlp=g_mlp, g_final=g_final.reshape(1, d))
    gx, d_mod, vec, gw_pool, gconvw, big = _local_step(
        x.reshape(t, d), loss_target.reshape(t, d), mod3, seq, w_in_p, w_out_f, w_up4, w_down4, conv_w_full, sp)

    grads4 = [big["w_in"][:, :N_CHIPS * in_cols].reshape(d, N_CHIPS, in_cols).transpose(1, 0, 2),
              big["w_out"].reshape(N_CHIPS, w_out.shape[1], d), big["w_up"], big["w_down"].reshape(N_CHIPS, w_down.shape[1], d)]
    own, got = _grad_to_sibling(grads4)
    parts = [_add_pair(a, b) for a, b in zip(own, got)]
    recv = _grad_to_chips(parts)
    g_in, g_out, g_up, g_down = _halves_exchange([_sum_slots(r) for r in recv])

    vec8, wpool8, convw8g, dmod8 = _all_gather_small([vec, gw_pool.reshape(4 * POOL_GROUP, POOL_GROUP), gconvw, d_mod])
    convw8s = lax.dynamic_slice(convw8g, (0, 0, chip * conv_cols), (N_DEV, 8, conv_cols))
    m_in = dict(b_ada=m_b_ada, g_mix=m_g_mix, conv_w=m_conv_w[0], conv_b=m_conv_b, dt_bias=m_dt_bias, a_log=m_a_log,
                d_skip=m_d_skip, g_ssd=m_g_ssd, w_pool=m_w_pool.reshape(4 * POOL_GROUP, POOL_GROUP), pool_scale=m_pool_scale,
                g_mlp=m_g_mlp, g_final=m_g_final.reshape(1, d))
    v_in = dict(b_ada=v_b_ada, g_mix=v_g_mix, conv_w=v_conv_w[0], conv_b=v_conv_b, dt_bias=v_dt_bias, a_log=v_a_log,
                d_skip=v_d_skip, g_ssd=v_g_ssd, w_pool=v_w_pool.reshape(4 * POOL_GROUP, POOL_GROUP), pool_scale=v_pool_scale,
                g_mlp=v_g_mlp, g_final=v_g_final.reshape(1, d))
    w_small = dict(sp, b_ada=b_ada, conv_w=conv_w[0], w_pool=w_pool.reshape(4 * POOL_GROUP, POOL_GROUP))
    loss_row, small = _small_adam(vec8, wpool8, convw8s, dmod8, {n: (w_small[n], m_in[n], v_in[n]) for n in _SMALL_PARAMS})

    dmod_all = dmod8.reshape(N_DEV * nb, N_CHIPS * ada_cols)
    dmod_cols = lax.dynamic_slice(dmod_all, (0, chip * ada_cols), (N_DEV * nb, ada_cols))
    res = {n: tuple(r.reshape(w.shape) for r in small[n])
           for n, w in (("b_ada", b_ada), ("g_mix", g_mix), ("conv_w", conv_w), ("conv_b", conv_b), ("dt_bias", dt_bias),
                        ("a_log", a_log), ("d_skip", d_skip), ("g_ssd", g_ssd), ("w_pool", w_pool), ("pool_scale", pool_scale),
                        ("g_mlp", g_mlp), ("g_final", g_final))}
    g_ada, d_ada, m_ada, v_ada = _adam_ada(c_act.T.astype(BF), dmod_cols, w_ada[0], m_w_ada[0], v_w_ada[0])
    res["w_ada"] = (g_ada[None], d_ada[None], m_ada[None], v_ada[None])
    for n, g, w, m, v in (("w_in", g_in, w_in, m_w_in, v_w_in), ("w_out", g_out, w_out, m_w_out, v_w_out),
                          ("w_up", g_up, w_up, m_w_up, v_w_up), ("w_down", g_down, w_down, m_w_down, v_w_down)):
        dl, m2, v2 = _adam_big(g, w[0], m[0], v[0])
        res[n] = (g[None], dl[None], m2[None], v2[None])

    loss = loss_row[0, 0]
    return (loss, gx.reshape(nb, seq, d), *[res[n][0] for n in _WEIGHTS], *[res[n][1] for n in _WEIGHTS],
            *[res[n][2] for n in _WEIGHTS], *[res[n][3] for n in _WEIGHTS])
```

```python
import functools

import jax
import jax.numpy as jnp
from jax import lax
from jax.experimental import pallas as pl
from jax.experimental.pallas import tpu as pltpu

F32 = jnp.float32
BF = jnp.bfloat16
MESH = pl.DeviceIdType.MESH

EPS = 1e-5
D_MODEL = 1024
POOL_WIDTH = 512
POOL_WINDOWS = (2, 4, 8, 16)
POOL_GROUP = 128
SSD_INNER = 1024
SSD_HEADS = 16
SSD_HEAD_DIM = 64
SSD_STATE = 128
GROUP_W = 512
CHUNK = 128
CONV_CH = 1536
OFF_Z = 512
OFF_XBC = 1536
OFF_DT = 3072
IN_WIDTH = 3088
IN_PAD = 3200
D_FF = 4096
N_MOD = 6
N_CHIPS = 4
N_DEV = 8
HALO = 16
CONV_HALO = 8

ADAM_LR = 0.001
ADAM_B1 = 0.9
ADAM_B2 = 0.999
ADAM_EPS = 1e-08
ADAM_WD = 0.01
ADAM_STEP = 10

VMEM_BYTES_V7X = 64 * 1024 * 1024


def _cp(semantics=None, vmem_mb=48, **kw):
    args = dict(vmem_limit_bytes=vmem_mb * 1024 * 1024, **kw)
    if semantics is not None:
        args["dimension_semantics"] = semantics
    return pltpu.CompilerParams(**args)


def _nn(a, b):
    return jnp.dot(a, b, preferred_element_type=F32)


def _nt(a, b):
    return lax.dot_general(a, b, (((1,), (1,)), ((), ())), preferred_element_type=F32)


def _tn(a, b):
    return lax.dot_general(a, b, (((0,), (0,)), ((), ())), preferred_element_type=F32)


def _split3(v):
    hi = v.astype(BF)
    r1 = v - hi.astype(F32)
    mid = r1.astype(BF)
    lo = (r1 - mid.astype(F32)).astype(BF)
    return hi, mid, lo


def _exact_nn(v, m01):
    hi, mid, lo = _split3(v)
    return _nn(hi, m01) + _nn(mid, m01) + _nn(lo, m01)


def _exact_nn_left(m01, v):
    hi, mid, lo = _split3(v)
    return _nn(m01, hi) + _nn(m01, mid) + _nn(m01, lo)


def _exact_nt_left(m01, v):
    hi, mid, lo = _split3(v)
    return _nt(m01, hi) + _nt(m01, mid) + _nt(m01, lo)


def _sigmoid(v):
    return 1.0 / (1.0 + jnp.exp(-v))


def _iota(shape, dim):
    return lax.broadcasted_iota(jnp.int32, shape, dim)


def _head_expand_matrix(heads, width):
    return (_iota((heads, width), 1) // SSD_HEAD_DIM == _iota((heads, width), 0)).astype(BF)


def _head_reduce_matrix(width, heads):
    return (_iota((width, heads), 0) // SSD_HEAD_DIM == _iota((width, heads), 1)).astype(BF)


def _mesh_pos():
    return lax.axis_index("x"), lax.axis_index("y"), lax.axis_index("c")


def _flip(v, bit):
    return v + bit - 2 * bit * v


def _all_gather_small(arrays):
    n = len(arrays)

    def body(*refs):
        in_refs, out_refs = refs[:n], refs[n:2 * n]
        send_sems, recv_sems, local_sems = refs[2 * n:]
        x, y, c = _mesh_pos()
        me = 4 * x + 2 * y + c
        local = []
        for a in range(n):
            cp = pltpu.make_async_copy(in_refs[a], out_refs[a].at[me], local_sems.at[a])
            cp.start()
            local.append(cp)
        sends = []
        for k in range(1, N_DEV):
            peer = (_flip(x, (k >> 2) & 1), _flip(y, (k >> 1) & 1), _flip(c, k & 1))
            for a in range(n):
                cp = pltpu.make_async_remote_copy(
                    src_ref=in_refs[a], dst_ref=out_refs[a].at[me],
                    send_sem=send_sems.at[a, k], recv_sem=recv_sems.at[a, k],
                    device_id=peer, device_id_type=MESH)
                cp.start()
                sends.append(cp)
        for k in range(1, N_DEV):
            px, py, pc = _flip(x, (k >> 2) & 1), _flip(y, (k >> 1) & 1), _flip(c, k & 1)
            src = 4 * px + 2 * py + pc
            for a in range(n):
                pltpu.make_async_remote_copy(
                    src_ref=in_refs[a], dst_ref=out_refs[a].at[src],
                    send_sem=send_sems.at[a, k], recv_sem=recv_sems.at[a, k],
                    device_id=(px, py, pc), device_id_type=MESH).wait_recv()
        for cp in sends:
            cp.wait_send()
        for cp in local:
            cp.wait()

    vm = pl.BlockSpec(memory_space=pltpu.VMEM)
    return pl.pallas_call(
        body, name="all_gather_small",
        out_shape=[jax.ShapeDtypeStruct((N_DEV,) + a.shape, a.dtype) for a in arrays],
        in_specs=[vm] * n, out_specs=[vm] * n,
        scratch_shapes=[pltpu.SemaphoreType.DMA((n, N_DEV)), pltpu.SemaphoreType.DMA((n, N_DEV)),
                        pltpu.SemaphoreType.DMA((n,))],
        compiler_params=_cp(vmem_mb=32),
    )(*arrays)


def _weight_all_gather(shards):
    n = len(shards)
    any_spec = pl.BlockSpec(memory_space=pl.ANY)

    def body(*refs):
        in_refs, out_refs = refs[:n], refs[n:2 * n]
        send_sems, recv_sems, fsend_sems, frecv_sems, local_sems = refs[2 * n:]
        x, y, c = _mesh_pos()
        chip = 2 * x + y
        local, sends = [], []
        for a in range(n):
            cp = pltpu.make_async_copy(in_refs[a], out_refs[a].at[chip], local_sems.at[a])
            cp.start()
            local.append(cp)

        def half(a, which):
            hc = shards[a].shape[1] // 2
            return pl.ds(pl.multiple_of(which * hc, 128), hc)

        for j in range(1, N_CHIPS):
            px, py = _flip(x, (j >> 1) & 1), _flip(y, j & 1)
            for a in range(n):
                cp = pltpu.make_async_remote_copy(
                    src_ref=in_refs[a].at[:, half(a, c)], dst_ref=out_refs[a].at[chip, :, half(a, c)],
                    send_sem=send_sems.at[a, j], recv_sem=recv_sems.at[a, j],
                    device_id=(px, py, c), device_id_type=MESH)
                cp.start()
                sends.append(cp)
        for j in range(1, N_CHIPS):
            px, py = _flip(x, (j >> 1) & 1), _flip(y, j & 1)
            src = 2 * px + py
            for a in range(n):
                landed = out_refs[a].at[src, :, half(a, c)]
                pltpu.make_async_remote_copy(
                    src_ref=landed, dst_ref=landed, send_sem=send_sems.at[a, j], recv_sem=recv_sems.at[a, j],
                    device_id=(px, py, c), device_id_type=MESH).wait_recv()
                cp = pltpu.make_async_remote_copy(
                    src_ref=landed, dst_ref=landed, send_sem=fsend_sems.at[a, j], recv_sem=frecv_sems.at[a, j],
                    device_id=(x, y, 1 - c), device_id_type=MESH)
                cp.start()
                sends.append(cp)
        for j in range(1, N_CHIPS):
            px, py = _flip(x, (j >> 1) & 1), _flip(y, j & 1)
            src = 2 * px + py
            for a in range(n):
                other = out_refs[a].at[src, :, half(a, 1 - c)]
                pltpu.make_async_remote_copy(
                    src_ref=other, dst_ref=other, send_sem=fsend_sems.at[a, j], recv_sem=frecv_sems.at[a, j],
                    device_id=(x, y, 1 - c), device_id_type=MESH).wait_recv()
        for cp in sends:
            cp.wait_send()
        for cp in local:
            cp.wait()

    return pl.pallas_call(
        body, name="weight_all_gather",
        out_shape=[jax.ShapeDtypeStruct((N_CHIPS,) + s.shape, s.dtype) for s in shards],
        in_specs=[any_spec] * n, out_specs=[any_spec] * n,
        scratch_shapes=[pltpu.SemaphoreType.DMA((n, N_CHIPS))] * 4 + [pltpu.SemaphoreType.DMA((n,))],
        compiler_params=_cp(vmem_mb=16),
    )(*shards)


def _grad_to_sibling(grads):
    n = len(grads)
    any_spec = pl.BlockSpec(memory_space=pl.ANY)

    def body(*refs):
        in_refs, own_refs, got_refs = refs[:n], refs[n:2 * n], refs[2 * n:3 * n]
        send_sems, recv_sems, local_sems = refs[3 * n:]
        x, y, c = _mesh_pos()
        copies, local = [], []
        for a in range(n):
            hc = grads[a].shape[2] // 2
            mine = pl.ds(pl.multiple_of(c * hc, 128), hc)
            theirs = pl.ds(pl.multiple_of((1 - c) * hc, 128), hc)
            lc = pltpu.make_async_copy(in_refs[a].at[:, :, mine], own_refs[a], local_sems.at[a])
            lc.start()
            local.append(lc)
            cp = pltpu.make_async_remote_copy(
                src_ref=in_refs[a].at[:, :, theirs], dst_ref=got_refs[a],
                send_sem=send_sems.at[a], recv_sem=recv_sems.at[a],
                device_id=(x, y, 1 - c), device_id_type=MESH)
            cp.start()
            copies.append(cp)
        for cp in copies:
            cp.wait_recv()
        for cp in copies:
            cp.wait_send()
        for lc in local:
            lc.wait()

    half = [jax.ShapeDtypeStruct((g.shape[0], g.shape[1], g.shape[2] // 2), g.dtype) for g in grads]
    outs = pl.pallas_call(
        body, name="grad_to_sibling", out_shape=half + half,
        in_specs=[any_spec] * n, out_specs=[any_spec] * (2 * n),
        scratch_shapes=[pltpu.SemaphoreType.DMA((n,))] * 3,
        compiler_params=_cp(vmem_mb=16),
    )(*grads)
    return outs[:n], outs[n:]


def _grad_to_chips(parts):
    n = len(parts)
    any_spec = pl.BlockSpec(memory_space=pl.ANY)

    def body(*refs):
        in_refs, out_refs = refs[:n], refs[n:2 * n]
        send_sems, recv_sems, local_sems = refs[2 * n:]
        x, y, c = _mesh_pos()
        chip = 2 * x + y
        sends, local = [], []
        for a in range(n):
            lc = pltpu.make_async_copy(in_refs[a].at[chip], out_refs[a].at[chip], local_sems.at[a])
            lc.start()
            local.append(lc)
        for j in range(1, N_CHIPS):
            px, py = _flip(x, (j >> 1) & 1), _flip(y, j & 1)
            dst_chip = 2 * px + py
            for a in range(n):
                cp = pltpu.make_async_remote_copy(
                    src_ref=in_refs[a].at[dst_chip], dst_ref=out_refs[a].at[chip],
                    send_sem=send_sems.at[a, j], recv_sem=recv_sems.at[a, j],
                    device_id=(px, py, c), device_id_type=MESH)
                cp.start()
                sends.append(cp)
        for j in range(1, N_CHIPS):
            px, py = _flip(x, (j >> 1) & 1), _flip(y, j & 1)
            src = 2 * px + py
            for a in range(n):
                pltpu.make_async_remote_copy(
                    src_ref=in_refs[a].at[src], dst_ref=out_refs[a].at[src],
                    send_sem=send_sems.at[a, j], recv_sem=recv_sems.at[a, j],
                    device_id=(px, py, c), device_id_type=MESH).wait_recv()
        for cp in sends:
            cp.wait_send()
        for lc in local:
            lc.wait()

    return pl.pallas_call(
        body, name="grad_to_chips",
        out_shape=[jax.ShapeDtypeStruct(p.shape, p.dtype) for p in parts],
        in_specs=[any_spec] * n, out_specs=[any_spec] * n,
        scratch_shapes=[pltpu.SemaphoreType.DMA((n, N_CHIPS))] * 2 + [pltpu.SemaphoreType.DMA((n,))],
        compiler_params=_cp(vmem_mb=16),
    )(*parts)


def _halves_exchange(halves):
    n = len(halves)
    any_spec = pl.BlockSpec(memory_space=pl.ANY)

    def body(*refs):
        in_refs, out_refs = refs[:n], refs[n:2 * n]
        send_sems, recv_sems, local_sems = refs[2 * n:]
        x, y, c = _mesh_pos()
        copies, local = [], []
        for a in range(n):
            hc = halves[a].shape[1]
            mine = pl.ds(pl.multiple_of(c * hc, 128), hc)
            lc = pltpu.make_async_copy(in_refs[a], out_refs[a].at[:, mine], local_sems.at[a])
            lc.start()
            local.append(lc)
            cp = pltpu.make_async_remote_copy(
                src_ref=in_refs[a], dst_ref=out_refs[a].at[:, mine],
                send_sem=send_sems.at[a], recv_sem=recv_sems.at[a],
                device_id=(x, y, 1 - c), device_id_type=MESH)
            cp.start()
            copies.append(cp)
        for a in range(n):
            hc = halves[a].shape[1]
            theirs = pl.ds(pl.multiple_of((1 - c) * hc, 128), hc)
            pltpu.make_async_remote_copy(
                src_ref=in_refs[a], dst_ref=out_refs[a].at[:, theirs],
                send_sem=send_sems.at[a], recv_sem=recv_sems.at[a],
                device_id=(x, y, 1 - c), device_id_type=MESH).wait_recv()
        for cp in copies:
            cp.wait_send()
        for lc in local:
            lc.wait()

    return pl.pallas_call(
        body, name="halves_exchange",
        out_shape=[jax.ShapeDtypeStruct((h.shape[0], 2 * h.shape[1]), h.dtype) for h in halves],
        in_specs=[any_spec] * n, out_specs=[any_spec] * n,
        scratch_shapes=[pltpu.SemaphoreType.DMA((n,))] * 3,
        compiler_params=_cp(vmem_mb=16),
    )(*halves)


def _add_pair(a, b):
    k, h, c = a.shape

    def body(a_ref, b_ref, o_ref):
        o_ref[...] = (a_ref[...].astype(F32) + b_ref[...].astype(F32)).astype(o_ref.dtype)

    spec = pl.BlockSpec((1, h, c), lambda i: (i, 0, 0))
    return pl.pallas_call(body, name="add_pair", grid=(k,), in_specs=[spec, spec], out_specs=spec,
                          out_shape=jax.ShapeDtypeStruct(a.shape, BF), compiler_params=_cp(("parallel",), 32))(a, b)


def _sum_slots(p):
    k, h, c = p.shape
    tc = c // 2

    def body(p_ref, o_ref):
        acc = p_ref[0].astype(F32)
        for s in range(1, k):
            acc = acc + p_ref[s].astype(F32)
        o_ref[...] = acc

    return pl.pallas_call(body, name="sum_slots", grid=(2,),
                          in_specs=[pl.BlockSpec((k, h, tc), lambda i: (0, 0, i))],
                          out_specs=pl.BlockSpec((h, tc), lambda i: (0, i)),
                          out_shape=jax.ShapeDtypeStruct((h, c), F32), compiler_params=_cp(("parallel",), 32))(p)


def _adam_math(w, g, m, v):
    m2 = ADAM_B1 * m + (1.0 - ADAM_B1) * g
    v2 = ADAM_B2 * v + (1.0 - ADAM_B2) * (g * g)
    m_hat = m2 / (1.0 - ADAM_B1 ** ADAM_STEP)
    v_hat = v2 / (1.0 - ADAM_B2 ** ADAM_STEP)
    delta = -ADAM_LR * (m_hat / (jnp.sqrt(v_hat) + ADAM_EPS) + ADAM_WD * w)
    return delta, m2, v2


def _adam_big(g, w, m, v):
    r, c = w.shape
    tc = c // 4

    def body(g_ref, w_ref, m_ref, v_ref, d_ref, m2_ref, v2_ref):
        d, m2, v2 = _adam_math(w_ref[...], g_ref[...], m_ref[...], v_ref[...])
        d_ref[...] = d
        m2_ref[...] = m2
        v2_ref[...] = v2

    spec = pl.BlockSpec((r, tc), lambda i: (0, i))
    sh = jax.ShapeDtypeStruct((r, c), F32)
    return pl.pallas_call(body, name="adam_big", grid=(4,), in_specs=[spec] * 4, out_specs=[spec] * 3,
                          out_shape=[sh, sh, sh], compiler_params=_cp(("parallel",), 32))(g, w, m, v)


def _adam_ada(c_act_t, dmod_cols, w, m, v):
    r, c = w.shape
    tc = 512

    def body(ct_ref, dm_ref, w_ref, m_ref, v_ref, g_ref, d_ref, m2_ref, v2_ref):
        g = _nn(ct_ref[...], dm_ref[...].astype(BF))
        d, m2, v2 = _adam_math(w_ref[...], g, m_ref[...], v_ref[...])
        g_ref[...] = g
        d_ref[...] = d
        m2_ref[...] = m2
        v2_ref[...] = v2

    spec = pl.BlockSpec((r, tc), lambda i: (0, i))
    sh = jax.ShapeDtypeStruct((r, c), F32)
    return pl.pallas_call(
        body, name="adam_ada", grid=(c // tc,),
        in_specs=[pl.BlockSpec(c_act_t.shape, lambda i: (0, 0)), pl.BlockSpec((dmod_cols.shape[0], tc), lambda i: (0, i)),
                  spec, spec, spec],
        out_specs=[spec] * 4, out_shape=[sh] * 4, compiler_params=_cp(("parallel",), 48))(c_act_t, dmod_cols, w, m, v)


def _ada_mod(c_all, w_shard, b_shard):
    nb, d = c_all.shape
    cols = w_shard.shape[1]
    tc = 512

    def body(c_ref, w_ref, b_ref, mod_ref, act_ref):
        cv = c_ref[...]
        act = cv * _sigmoid(cv)
        act_ref[...] = act
        mod_ref[...] = _nn(act.astype(BF), w_ref[...].astype(BF)) + b_ref[...]

    return pl.pallas_call(
        body, name="ada_mod", grid=(cols // tc,),
        in_specs=[pl.BlockSpec((nb, d), lambda i: (0, 0)), pl.BlockSpec((d, tc), lambda i: (0, i)),
                  pl.BlockSpec((1, tc), lambda i: (0, i))],
        out_specs=[pl.BlockSpec((nb, tc), lambda i: (0, i)), pl.BlockSpec((nb, d), lambda i: (0, 0))],
        out_shape=[jax.ShapeDtypeStruct((nb, cols), F32), jax.ShapeDtypeStruct((nb, d), F32)],
        compiler_params=_cp(("arbitrary",), 32))(c_all, w_shard, b_shard)


def _in_proj(x, mod3, g_mix, w_in_t, seq):
    t, d = x.shape
    tm = 256
    tps = seq // tm

    def body(x_ref, mod_ref, g_ref, w_ref, proj_ref, u1_ref):
        xv = x_ref[...]
        r = lax.rsqrt(jnp.mean(xv * xv, -1, keepdims=True) + EPS)
        u = (xv * r * g_ref[...]) * (1.0 + mod_ref[0, 1:2, :]) + mod_ref[0, 0:1, :]
        ub = u.astype(BF)
        u1_ref[...] = ub
        proj_ref[...] = _nt(ub, w_ref[...])

    return pl.pallas_call(
        body, name="in_proj", grid=(t // tm,),
        in_specs=[pl.BlockSpec((tm, d), lambda i: (i, 0)), pl.BlockSpec((1, N_MOD, d), lambda i: (i // tps, 0, 0)),
                  pl.BlockSpec((1, d), lambda i: (0, 0)), pl.BlockSpec((IN_PAD, d), lambda i: (0, 0))],
        out_specs=[pl.BlockSpec((tm, IN_PAD), lambda i: (i, 0)), pl.BlockSpec((tm, d), lambda i: (i, 0))],
        out_shape=[jax.ShapeDtypeStruct((t, IN_PAD), F32), jax.ShapeDtypeStruct((t, d), BF)],
        compiler_params=_cp(("parallel",), 48))(x, mod3, g_mix, w_in_t)


def _pool_tile(seq):
    return min(512, seq)


def _pool_fwd(proj, w_pool, pool_scale, nb, seq):
    ts = _pool_tile(seq)
    nt = seq // ts

    def body(u_ref, halo_ref, wp_ref, ps_ref, yp_ref, p_ref):
        i = pl.program_id(1)
        halo = jnp.where(i == 0, 0.0, halo_ref[...])
        u = u_ref[...]
        ext = jnp.concatenate([halo, u], 0)
        tpos = i * ts + _iota((ts, 1), 0)
        for g, w in enumerate(POOL_WINDOWS):
            gs = slice(g * POOL_GROUP, (g + 1) * POOL_GROUP)
            s = ext[:, gs]
            sh = 1
            while sh < w:
                s = s + pltpu.roll(s, sh, 0)
                sh *= 2
            cnt = jnp.minimum(tpos + 1, w).astype(F32)
            pb = (s[HALO:] / cnt - u[:, gs]).astype(BF)
            p_ref[:, gs] = pb
            yp_ref[:, gs] = (_nn(pb, wp_ref[g].astype(BF)) * ps_ref[:, gs]).astype(BF)

    hb = ts // HALO
    return pl.pallas_call(
        body, name="pool_fwd", grid=(nb, nt),
        in_specs=[pl.BlockSpec((ts, POOL_WIDTH), lambda b, i: (b * nt + i, 0)),
                  pl.BlockSpec((HALO, POOL_WIDTH), lambda b, i: (jnp.maximum((b * nt + i) * hb - 1, 0), 0)),
                  pl.BlockSpec((4, POOL_GROUP, POOL_GROUP), lambda b, i: (0, 0, 0)),
                  pl.BlockSpec((1, POOL_WIDTH), lambda b, i: (0, 0))],
        out_specs=[pl.BlockSpec((ts, POOL_WIDTH), lambda b, i: (b * nt + i, 0))] * 2,
        out_shape=[jax.ShapeDtypeStruct((nb * seq, POOL_WIDTH), BF)] * 2,
        compiler_params=_cp(("parallel", "parallel"), 32))(proj, proj, w_pool, pool_scale)


def _chunk_recompute(uxbc, halo, udt, cw, cb, dtb, alog, first):
    halo = jnp.where(first, 0.0, halo)
    ext = jnp.concatenate([halo, uxbc], 0)
    pre = cb + uxbc * cw[3:4]
    for k in (2, 1, 0):
        pre = pre + pltpu.roll(ext, 3 - k, 0)[CONV_HALO:] * cw[k:k + 1]
    sg = _sigmoid(pre)
    xbc = pre * sg
    dtp = udt[:, :SSD_HEADS] + dtb
    dt = jnp.maximum(dtp, 0.0) + jnp.log(1.0 + jnp.exp(-jnp.abs(dtp)))
    a = -jnp.exp(alog)
    da = dt * a
    tril = (_iota((CHUNK, CHUNK), 0) >= _iota((CHUNK, CHUNK), 1))
    acum = _exact_nn_left(tril.astype(BF), da)
    eye = (_iota((SSD_HEADS, SSD_HEADS), 0) == _iota((SSD_HEADS, SSD_HEADS), 1)).astype(BF)
    acum_t = _exact_nt_left(eye, acum)
    expand = _head_expand_matrix(SSD_HEADS, SSD_INNER)
    acum_e = _exact_nn(acum, expand)
    dt_e = _exact_nn(dt, expand)
    last_e = acum_e[CHUNK - 1:CHUNK]
    return dict(pre=pre, sg=sg, xbc=xbc, dtp=dtp, dt=dt, a=a, acum=acum, acum_t=acum_t, tril=tril,
                dt_e=dt_e, e_a=jnp.exp(acum_e), d_out=jnp.exp(last_e - acum_e), c_dec=jnp.exp(last_e))


def _head_decay(r, h):
    seg = r["acum"][:, h:h + 1] - r["acum_t"][h:h + 1, :]
    return jnp.where(r["tril"], jnp.exp(jnp.minimum(seg, 0.0)), 0.0)


def _ssd_specs(nb, seq, reverse):
    nc = seq // CHUNK
    per = seq // CONV_HALO

    def cidx(c):
        return (nc - 1 - c) if reverse else c

    def row(b, c):
        return b * nc + cidx(c)

    specs = [
        pl.BlockSpec((CHUNK, CONV_CH), lambda b, c: (row(b, c), 1)),
        pl.BlockSpec((CONV_HALO, CONV_CH),
                     lambda b, c: (jnp.maximum(b * per + cidx(c) * (CHUNK // CONV_HALO) - 1, 0), 1)),
        pl.BlockSpec((CHUNK, GROUP_W), lambda b, c: (row(b, c), 1)),
        pl.BlockSpec((CHUNK, GROUP_W), lambda b, c: (row(b, c), 2)),
        pl.BlockSpec((CHUNK, 128), lambda b, c: (row(b, c), OFF_DT // 128)),
    ]
    return specs, row, cidx, nc


def _const_spec(shape):
    return pl.BlockSpec(shape, lambda b, c: (0,) * len(shape))


def _ssd_fwd(proj, conv_w, conv_b, dt_bias, a_log, dskip_e, g_ssd, nb, seq):
    specs, row, cidx, nc = _ssd_specs(nb, seq, reverse=False)

    def body(uxbc_ref, halo_ref, z0_ref, z1_ref, udt_ref, cw_ref, cb_ref, dtb_ref, alog_ref, dsk_ref, gs_ref,
             yssd_ref, yssm_ref, hprev_ref, h_ref, yd_ref):
        c = pl.program_id(1)

        @pl.when(c == 0)
        def _():
            h_ref[...] = jnp.zeros_like(h_ref)

        r = _chunk_recompute(uxbc_ref[...], halo_ref[...], udt_ref[...], cw_ref[...], cb_ref[...],
                             dtb_ref[...], alog_ref[...], c == 0)
        xbc = r["xbc"]
        xs = xbc[:, :SSD_INNER]
        xdt = xs * r["dt_e"]
        xdt_b = xdt.astype(BF)
        xdo_b = (xdt * r["d_out"]).astype(BF)
        hprev_ref[0, 0] = h_ref[...]
        for g in range(2):
            gs = slice(g * GROUP_W, (g + 1) * GROUP_W)
            bg = xbc[:, SSD_INNER + g * SSD_STATE:SSD_INNER + (g + 1) * SSD_STATE].astype(BF)
            cg = xbc[:, SSD_INNER + (2 + g) * SSD_STATE:SSD_INNER + (3 + g) * SSD_STATE].astype(BF)
            scores = _nt(cg, bg)
            hg = h_ref[g]
            y_off = _nn(cg, hg.astype(BF)) * r["e_a"][:, gs]
            for hh in range(8):
                h = g * 8 + hh
                m = (scores * _head_decay(r, h)).astype(BF)
                yd_ref[:, h * SSD_HEAD_DIM:(h + 1) * SSD_HEAD_DIM] = _nn(m, xdt_b[:, h * SSD_HEAD_DIM:(h + 1) * SSD_HEAD_DIM])
            h_ref[g] = hg * r["c_dec"][:, gs] + _tn(bg, xdo_b[:, gs])
            y = yd_ref[:, gs] + y_off + dsk_ref[:, gs] * xs[:, gs]
            yssm_ref[:, gs] = y
            zg = (z0_ref if g == 0 else z1_ref)[...]
            yg = y * (zg * _sigmoid(zg))
            rg = lax.rsqrt(jnp.mean(yg * yg, -1, keepdims=True) + EPS)
            yssd_ref[:, gs] = (yg * rg * gs_ref[:, gs]).astype(BF)

    t = nb * seq
    return pl.pallas_call(
        body, name="ssd_fwd", grid=(nb, nc),
        in_specs=specs + [_const_spec((4, CONV_CH)), _const_spec((1, CONV_CH)), _const_spec((1, SSD_HEADS)),
                          _const_spec((1, SSD_HEADS)), _const_spec((1, SSD_INNER)), _const_spec((1, SSD_INNER))],
        out_specs=[pl.BlockSpec((CHUNK, SSD_INNER), lambda b, c: (row(b, c), 0)),
                   pl.BlockSpec((CHUNK, SSD_INNER), lambda b, c: (row(b, c), 0)),
                   pl.BlockSpec((1, 1, 2, SSD_STATE, GROUP_W), lambda b, c: (b, c, 0, 0, 0))],
        out_shape=[jax.ShapeDtypeStruct((t, SSD_INNER), BF), jax.ShapeDtypeStruct((t, SSD_INNER), F32),
                   jax.ShapeDtypeStruct((nb, nc, 2, SSD_STATE, GROUP_W), F32)],
        scratch_shapes=[pltpu.VMEM((2, SSD_STATE, GROUP_W), F32), pltpu.VMEM((CHUNK, SSD_INNER), F32)],
        compiler_params=_cp(("arbitrary", "arbitrary"), 48),
    )(proj, proj, proj, proj, proj, conv_w, conv_b, dt_bias, a_log, dskip_e, g_ssd)


def _out_proj(y_pool, y_ssd, w_out, x, mod3, g_mlp, seq):
    t, d = x.shape
    tm = 512
    tps = seq // tm if seq >= tm else 1
    tm = min(tm, seq)

    def body(yp_ref, ys_ref, w_ref, x_ref, mod_ref, g_ref, h1_ref, o_ref, u2_ref):
        o = _nn(yp_ref[...], w_ref[0:POOL_WIDTH, :]) + _nn(ys_ref[...], w_ref[POOL_WIDTH:, :])
        o_ref[...] = o.astype(BF)
        h1 = x_ref[...] + mod_ref[0, 2:3, :] * o
        h1_ref[...] = h1
        r = lax.rsqrt(jnp.mean(h1 * h1, -1, keepdims=True) + EPS)
        u2_ref[...] = ((h1 * r * g_ref[...]) * (1.0 + mod_ref[0, 4:5, :]) + mod_ref[0, 3:4, :]).astype(BF)

    row = lambda i: (i, 0)
    return pl.pallas_call(
        body, name="out_proj", grid=(t // tm,),
        in_specs=[pl.BlockSpec((tm, POOL_WIDTH), row), pl.BlockSpec((tm, SSD_INNER), row),
                  pl.BlockSpec(w_out.shape, lambda i: (0, 0)), pl.BlockSpec((tm, d), row),
                  pl.BlockSpec((1, N_MOD, d), lambda i: (i // tps, 0, 0)), pl.BlockSpec((1, d), lambda i: (0, 0))],
        out_specs=[pl.BlockSpec((tm, d), row)] * 3,
        out_shape=[jax.ShapeDtypeStruct((t, d), F32), jax.ShapeDtypeStruct((t, d), BF), jax.ShapeDtypeStruct((t, d), BF)],
        compiler_params=_cp(("parallel",), 48))(y_pool, y_ssd, w_out, x, mod3, g_mlp)


def _mlp_up(u2, w_up4):
    t, d = u2.shape
    tm = min(512, t)
    nk, _, cols = w_up4.shape

    def body(u_ref, w_ref, a_ref):
        a_ref[...] = _nn(u_ref[...], w_ref[0]).astype(BF)

    return pl.pallas_call(
        body, name="mlp_up", grid=(nk, t // tm),
        in_specs=[pl.BlockSpec((tm, d), lambda k, i: (i, 0)), pl.BlockSpec((1, d, cols), lambda k, i: (k, 0, 0))],
        out_specs=pl.BlockSpec((tm, cols), lambda k, i: (i, k)),
        out_shape=jax.ShapeDtypeStruct((t, nk * cols), BF),
        compiler_params=_cp(("parallel", "parallel"), 32))(u2, w_up4)


def _mlp_down_loss(a_up, w_down, h1, mod3, g_final, target, seq):
    t, d = h1.shape
    nb = t // seq
    tm = min(256, seq)
    tps = seq // tm

    def body(a_ref, w_ref, h1_ref, mod_ref, g_ref, tg_ref, ddn_ref, dh2_ref, sq_ref, gg_ref, dgf_ref):
        i = pl.program_id(0)
        f = jnp.square(jnp.maximum(a_ref[...], 0))
        dn = _nn(f, w_ref[...])
        gate = mod_ref[0, 5:6, :]
        h2 = h1_ref[...] + gate * dn
        r = lax.rsqrt(jnp.mean(h2 * h2, -1, keepdims=True) + EPS)
        hh = h2 * r
        err = hh * g_ref[...] - tg_ref[...]
        dy = err * (1.0 / d)
        dhat = dy * g_ref[...]
        dh2 = r * (dhat - hh * jnp.mean(dhat * hh, -1, keepdims=True))
        dh2_ref[...] = dh2
        ddn_ref[...] = (dh2 * gate).astype(BF)

        @pl.when(i == 0)
        def _():
            sq_ref[...] = jnp.zeros_like(sq_ref)
            gg_ref[...] = jnp.zeros_like(gg_ref)

        @pl.when(i % tps == 0)
        def _():
            dgf_ref[...] = jnp.zeros_like(dgf_ref)

        sq_ref[...] += jnp.sum(err * err, 0, keepdims=True)
        gg_ref[...] += jnp.sum(dy * hh, 0, keepdims=True)
        dgf_ref[0] += jnp.sum(dh2 * dn, 0, keepdims=True)

    row = lambda i: (i, 0)
    vec = pl.BlockSpec((1, d), lambda i: (0, 0))
    return pl.pallas_call(
        body, name="mlp_down_loss", grid=(t // tm,),
        in_specs=[pl.BlockSpec((tm, D_FF), row), pl.BlockSpec(w_down.shape, lambda i: (0, 0)), pl.BlockSpec((tm, d), row),
                  pl.BlockSpec((1, N_MOD, d), lambda i: (i // tps, 0, 0)), vec, pl.BlockSpec((tm, d), row)],
        out_specs=[pl.BlockSpec((tm, d), row), pl.BlockSpec((tm, d), row), vec, vec,
                   pl.BlockSpec((1, 1, d), lambda i: (i // tps, 0, 0))],
        out_shape=[jax.ShapeDtypeStruct((t, d), BF), jax.ShapeDtypeStruct((t, d), F32), jax.ShapeDtypeStruct((1, d), F32),
                   jax.ShapeDtypeStruct((1, d), F32), jax.ShapeDtypeStruct((nb, 1, d), F32)],
        compiler_params=_cp(("arbitrary",), 56))(a_up, w_down, h1, mod3, g_final, target)


def _tn_matmul(a, b, tk, tn, name, square_relu=False, out3=False):
    t, kdim = a.shape
    ndim = b.shape[1]

    def body(a_ref, b_ref, o_ref):
        av = a_ref[...]
        if square_relu:
            av = jnp.square(jnp.maximum(av, 0))
        res = _tn(av, b_ref[...]).astype(BF)
        if out3:
            o_ref[0] = res
        else:
            o_ref[...] = res

    if out3:
        out_spec = pl.BlockSpec((1, tk, tn), lambda j, i: (j, i, 0))
        out_shape = jax.ShapeDtypeStruct((ndim // tn, kdim, tn), BF)
    else:
        out_spec = pl.BlockSpec((tk, tn), lambda j, i: (i, j))
        out_shape = jax.ShapeDtypeStruct((kdim, ndim), BF)
    return pl.pallas_call(
        body, name=name, grid=(ndim // tn, kdim // tk),
        in_specs=[pl.BlockSpec((t, tk), lambda j, i: (0, i)), pl.BlockSpec((t, tn), lambda j, i: (0, j))],
        out_specs=out_spec, out_shape=out_shape,
        compiler_params=_cp(("parallel", "parallel"), 56))(a, b)


def _mlp_down_bwd(d_dn, w_down4, a_up):
    t, d = d_dn.shape
    tm = min(512, t)
    nk, rows, _ = w_down4.shape

    def body(g_ref, w_ref, a_ref, o_ref):
        df = _nt(g_ref[...], w_ref[0])
        o_ref[...] = (df * (2.0 * jnp.maximum(a_ref[...], 0).astype(F32))).astype(BF)

    return pl.pallas_call(
        body, name="mlp_down_bwd", grid=(nk, t // tm),
        in_specs=[pl.BlockSpec((tm, d), lambda k, i: (i, 0)), pl.BlockSpec((1, rows, d), lambda k, i: (k, 0, 0)),
                  pl.BlockSpec((tm, rows), lambda k, i: (i, k))],
        out_specs=pl.BlockSpec((tm, rows), lambda k, i: (i, k)),
        out_shape=jax.ShapeDtypeStruct((t, nk * rows), BF),
        compiler_params=_cp(("parallel", "parallel"), 32))(d_dn, w_down4, a_up)


def _mlp_up_bwd(d_a, w_up4, h1, dh2, o, mod3, g_mlp, seq):
    t, d = h1.shape
    nb = t // seq
    tm = min(256, seq)
    tps = seq // tm
    nk = w_up4.shape[0]
    cols = w_up4.shape[2]

    def body(da_ref, w_ref, h1_ref, dh2_ref, o_ref, mod_ref, g_ref, dh1_ref, do_ref, acc_ref, gg_ref):
        i = pl.program_id(0)
        du = _nt(da_ref[:, 0:cols], w_ref[0])
        for k in range(1, nk):
            du = du + _nt(da_ref[:, k * cols:(k + 1) * cols], w_ref[k])
        h1 = h1_ref[...]
        r = lax.rsqrt(jnp.mean(h1 * h1, -1, keepdims=True) + EPS)
        hh = h1 * r
        n2 = hh * g_ref[...]
        dn2 = du * (1.0 + mod_ref[0, 4:5, :])
        dhat = dn2 * g_ref[...]
        dh1 = dh2_ref[...] + r * (dhat - hh * jnp.mean(dhat * hh, -1, keepdims=True))
        dh1_ref[...] = dh1
        do_ref[...] = (dh1 * mod_ref[0, 2:3, :]).astype(BF)

        @pl.when(i == 0)
        def _():
            gg_ref[...] = jnp.zeros_like(gg_ref)

        @pl.when(i % tps == 0)
        def _():
            acc_ref[...] = jnp.zeros_like(acc_ref)

        gg_ref[...] += jnp.sum(dn2 * hh, 0, keepdims=True)
        acc_ref[0, 0:1, :] += jnp.sum(du, 0, keepdims=True)
        acc_ref[0, 1:2, :] += jnp.sum(du * n2, 0, keepdims=True)
        acc_ref[0, 2:3, :] += jnp.sum(dh1 * o_ref[...].astype(F32), 0, keepdims=True)

    row = lambda i: (i, 0)
    vec = pl.BlockSpec((1, d), lambda i: (0, 0))
    return pl.pallas_call(
        body, name="mlp_up_bwd", grid=(t // tm,),
        in_specs=[pl.BlockSpec((tm, D_FF), row), pl.BlockSpec(w_up4.shape, lambda i: (0, 0, 0)), pl.BlockSpec((tm, d), row),
                  pl.BlockSpec((tm, d), row), pl.BlockSpec((tm, d), row),
                  pl.BlockSpec((1, N_MOD, d), lambda i: (i // tps, 0, 0)), vec],
        out_specs=[pl.BlockSpec((tm, d), row), pl.BlockSpec((tm, d), row),
                   pl.BlockSpec((1, 8, d), lambda i: (i // tps, 0, 0)), vec],
        out_shape=[jax.ShapeDtypeStruct((t, d), F32), jax.ShapeDtypeStruct((t, d), BF),
                   jax.ShapeDtypeStruct((nb, 8, d), F32), jax.ShapeDtypeStruct((1, d), F32)],
        compiler_params=_cp(("arbitrary",), 56))(d_a, w_up4, h1, dh2, o, mod3, g_mlp)


def _out_proj_bwd(d_o, w_out):
    t, d = d_o.shape
    tm = min(512, t)

    def body(g_ref, w_ref, dp_ref, ds_ref):
        gv = g_ref[...]
        dp_ref[...] = _nt(gv, w_ref[0:POOL_WIDTH, :])
        ds_ref[...] = _nt(gv, w_ref[POOL_WIDTH:, :])

    row = lambda i: (i, 0)
    return pl.pallas_call(
        body, name="out_proj_bwd", grid=(t // tm,),
        in_specs=[pl.BlockSpec((tm, d), row), pl.BlockSpec(w_out.shape, lambda i: (0, 0))],
        out_specs=[pl.BlockSpec((tm, POOL_WIDTH), row), pl.BlockSpec((tm, SSD_INNER), row)],
        out_shape=[jax.ShapeDtypeStruct((t, POOL_WIDTH), F32), jax.ShapeDtypeStruct((t, SSD_INNER), F32)],
        compiler_params=_cp(("parallel",), 32))(d_o, w_out)


def _pool_bwd(d_ypool, p, w_pool, pool_scale, nb, seq):
    ts = _pool_tile(seq)
    nt = seq // ts
    hb = ts // HALO
    last_block = nb * seq // HALO - 1

    def body(dy_ref, halo_ref, p_ref, wp_ref, ps_ref, du_ref, gw_ref, gs_ref):
        b = pl.program_id(0)
        i = pl.program_id(1)

        @pl.when((b == 0) & (i == 0))
        def _():
            gw_ref[...] = jnp.zeros_like(gw_ref)
            gs_ref[...] = jnp.zeros_like(gs_ref)

        halo = jnp.where(i == nt - 1, 0.0, halo_ref[...])
        dy = dy_ref[...]
        ext = jnp.concatenate([dy, halo], 0)
        tpos = i * ts + _iota((ts + HALO, 1), 0)
        n_ext = ts + HALO
        for g, w in enumerate(POOL_WINDOWS):
            gs = slice(g * POOL_GROUP, (g + 1) * POOL_GROUP)
            wg = wp_ref[g].astype(BF)
            pg = p_ref[:, gs]
            pw = _nn(pg, wg)
            gs_ref[:, gs] += jnp.sum(dy[:, gs] * pw, 0, keepdims=True)
            dpw = (ext[:, gs] * ps_ref[:, gs]).astype(BF)
            gw_ref[g] += _tn(pg, dpw[:ts])
            dp = _nt(dpw, wg)
            cnt = jnp.minimum(tpos + 1, w).astype(F32)
            s = dp / cnt
            sh = 1
            while sh < w:
                s = s + pltpu.roll(s, n_ext - sh, 0)
                sh *= 2
            du_ref[:, gs] = (s[:ts] - dp[:ts]).astype(BF)

    return pl.pallas_call(
        body, name="pool_bwd", grid=(nb, nt),
        in_specs=[pl.BlockSpec((ts, POOL_WIDTH), lambda b, i: (b * nt + i, 0)),
                  pl.BlockSpec((HALO, POOL_WIDTH), lambda b, i: (jnp.minimum((b * nt + i + 1) * hb, last_block), 0)),
                  pl.BlockSpec((ts, POOL_WIDTH), lambda b, i: (b * nt + i, 0)),
                  pl.BlockSpec((4, POOL_GROUP, POOL_GROUP), lambda b, i: (0, 0, 0)),
                  pl.BlockSpec((1, POOL_WIDTH), lambda b, i: (0, 0))],
        out_specs=[pl.BlockSpec((ts, POOL_WIDTH), lambda b, i: (b * nt + i, 0)),
                   pl.BlockSpec((4, POOL_GROUP, POOL_GROUP), lambda b, i: (0, 0, 0)),
                   pl.BlockSpec((1, POOL_WIDTH), lambda b, i: (0, 0))],
        out_shape=[jax.ShapeDtypeStruct((nb * seq, POOL_WIDTH), BF), jax.ShapeDtypeStruct((4, POOL_GROUP, POOL_GROUP), F32),
                   jax.ShapeDtypeStruct((1, POOL_WIDTH), F32)],
        compiler_params=_cp(("arbitrary", "arbitrary"), 32))(d_ypool, d_ypool, p, w_pool, pool_scale)


def _ssd_bwd(proj, d_yssd, yssm, h_prev, conv_w, conv_b, dt_bias, a_log, dskip_e, g_ssd, nb, seq):
    specs, row, cidx, nc = _ssd_specs(nb, seq, reverse=True)

    def body(uxbc_ref, halo_ref, z0_ref, z1_ref, udt_ref, dys_ref, yssm_ref, hprev_ref,
             cw_ref, cb_ref, dtb_ref, alog_ref, dsk_ref, gs_ref,
             dz_ref, dpre_ref, dudt_ref, ggs_ref, gdsk_ref, ga_ref, gdtb_ref,
             g_ref, dxdt_ref, dyv_ref):
        b = pl.program_id(0)
        c = pl.program_id(1)

        @pl.when(c == 0)
        def _():
            g_ref[...] = jnp.zeros_like(g_ref)

        @pl.when((b == 0) & (c == 0))
        def _():
            ggs_ref[...] = jnp.zeros_like(ggs_ref)
            gdsk_ref[...] = jnp.zeros_like(gdsk_ref)
            ga_ref[...] = jnp.zeros_like(ga_ref)
            gdtb_ref[...] = jnp.zeros_like(gdtb_ref)

        r = _chunk_recompute(uxbc_ref[...], halo_ref[...], udt_ref[...], cw_ref[...], cb_ref[...],
                             dtb_ref[...], alog_ref[...], c == nc - 1)
        xbc = r["xbc"]
        xs = xbc[:, :SSD_INNER]
        dt_e = r["dt_e"]
        xdt = xs * dt_e
        xdt_b = xdt.astype(BF)
        reduce_m = _head_reduce_matrix(GROUP_W, 8)
        onehot16 = lambda h: (_iota((1, SSD_HEADS), 1) == h).astype(F32)
        onecol16 = lambda h: (_iota((SSD_HEADS, 1), 0) == h).astype(F32)

        d_acum = jnp.zeros((CHUNK, SSD_HEADS), F32)
        d_acum_t = jnp.zeros((SSD_HEADS, CHUNK), F32)
        d_alast = jnp.zeros((1, SSD_HEADS), F32)
        place8 = lambda g: (_iota((8, SSD_HEADS), 1) == _iota((8, SSD_HEADS), 0) + 8 * g).astype(BF)
        d_b, d_c = [], []
        for g in range(2):
            gs = slice(g * GROUP_W, (g + 1) * GROUP_W)
            zg = (z0_ref if g == 0 else z1_ref)[...]
            sz = _sigmoid(zg)
            silu_z = zg * sz
            ys = yssm_ref[:, gs]
            yg = ys * silu_z
            rg = lax.rsqrt(jnp.mean(yg * yg, -1, keepdims=True) + EPS)
            yh = yg * rg
            dys = dys_ref[:, gs]
            ggs_ref[:, gs] += jnp.sum(dys * yh, 0, keepdims=True)
            dyh = dys * gs_ref[:, gs]
            dyg = rg * (dyh - yh * jnp.mean(dyh * yh, -1, keepdims=True))
            dy = dyg * silu_z
            dz_ref[:, gs] = (dyg * ys * (sz * (1.0 + zg * (1.0 - sz)))).astype(BF)
            gdsk_ref[:, gs] += jnp.sum(dy * xs[:, gs], 0, keepdims=True)
            dyv_ref[:, gs] = dy
            dy_b = dy.astype(BF)

            bg = xbc[:, SSD_INNER + g * SSD_STATE:SSD_INNER + (g + 1) * SSD_STATE].astype(BF)
            cg = xbc[:, SSD_INNER + (2 + g) * SSD_STATE:SSD_INNER + (3 + g) * SSD_STATE].astype(BF)
            scores = _nt(cg, bg)
            hg = hprev_ref[0, 0, g]
            hg_b = hg.astype(BF)
            gg = g_ref[g]
            gg_b = gg.astype(BF)
            e_a = r["e_a"][:, gs]
            d_out = r["d_out"][:, gs]
            c_dec = r["c_dec"][:, gs]
            zc = _nn(cg, hg_b)
            wv = e_a * dy
            wv_b = wv.astype(BF)
            da_g = _exact_nn(wv * zc, reduce_m)
            dcg = _nt(wv_b, hg_b)
            d_hprev = _tn(cg, wv_b)
            vg = _nn(bg, gg_b)
            dxdt_g = d_out * vg
            dd_out = _exact_nn(xdt[:, gs] * vg, reduce_m)
            dbg = _nt((xdt[:, gs] * d_out).astype(BF), gg_b)
            dcd = _exact_nn(jnp.sum(gg * hg, 0, keepdims=True), reduce_m)
            d_out8 = jnp.exp(r["acum"][CHUNK - 1:CHUNK, 8 * g:8 * g + 8] - r["acum"][:, 8 * g:8 * g + 8])
            c_dec8 = jnp.exp(r["acum"][CHUNK - 1:CHUNK, 8 * g:8 * g + 8])
            t8 = dd_out * d_out8
            d_alast = d_alast + _exact_nn(jnp.sum(t8, 0, keepdims=True) + dcd * c_dec8, place8(g))
            d_acum = d_acum + _exact_nn(da_g - t8, place8(g))
            dsc = jnp.zeros((CHUNK, CHUNK), F32)
            for hh in range(8):
                h = g * 8 + hh
                hs = slice(h * SSD_HEAD_DIM, (h + 1) * SSD_HEAD_DIM)
                lam = _head_decay(r, h)
                m = scores * lam
                dyh_b = dy_b[:, hh * SSD_HEAD_DIM:(hh + 1) * SSD_HEAD_DIM]
                dm = _nt(dyh_b, xdt_b[:, hs])
                tm_ = dm * m
                d_acum = d_acum + jnp.sum(tm_, 1, keepdims=True) * onehot16(h)
                d_acum_t = d_acum_t + onecol16(h) * jnp.sum(tm_, 0, keepdims=True)
                dsc = dsc + dm * lam
                dxdt_ref[:, hs] = _tn(m.astype(BF), dyh_b) + dxdt_g[:, hh * SSD_HEAD_DIM:(hh + 1) * SSD_HEAD_DIM]
            dsc_b = dsc.astype(BF)
            d_c.append(dcg + _nn(dsc_b, bg))
            d_b.append(dbg + _tn(dsc_b, cg))
            g_ref[g] = d_hprev + c_dec * gg

        eye = (_iota((CHUNK, CHUNK), 0) == _iota((CHUNK, CHUNK), 1)).astype(BF)
        d_acum = d_acum - _exact_nt_left(eye, d_acum_t)
        is_last = (_iota((CHUNK, 1), 0) == CHUNK - 1).astype(F32)
        d_acum = d_acum + is_last * d_alast
        triu = (_iota((CHUNK, CHUNK), 0) <= _iota((CHUNK, CHUNK), 1)).astype(BF)
        d_da = _exact_nn_left(triu, d_acum)
        dt = r["dt"]
        ga_ref[...] += jnp.sum(d_da * dt, 0, keepdims=True)
        dxdt = dxdt_ref[...]
        reduce16 = _head_reduce_matrix(SSD_INNER, SSD_HEADS)
        d_dt = d_da * r["a"] + _exact_nn(dxdt * xs, reduce16)
        d_udt = d_dt * _sigmoid(r["dtp"])
        gdtb_ref[...] += jnp.sum(d_udt, 0, keepdims=True)
        dudt_ref[...] = jnp.zeros_like(dudt_ref)
        dudt_ref[:, 0:SSD_HEADS] = d_udt.astype(BF)
        pre, sg = r["pre"], r["sg"]
        dsilu = sg * (1.0 + pre * (1.0 - sg))
        dpre_ref[:, 0:SSD_INNER] = (dsk_ref[...] * dyv_ref[...] + dxdt * dt_e) * dsilu[:, 0:SSD_INNER]
        for g in range(2):
            bs = slice(SSD_INNER + g * SSD_STATE, SSD_INNER + (g + 1) * SSD_STATE)
            cs = slice(SSD_INNER + (2 + g) * SSD_STATE, SSD_INNER + (3 + g) * SSD_STATE)
            dpre_ref[:, bs] = d_b[g] * dsilu[:, bs]
            dpre_ref[:, cs] = d_c[g] * dsilu[:, cs]

    t = nb * seq
    vec = _const_spec((1, SSD_INNER))
    small = _const_spec((1, SSD_HEADS))
    return pl.pallas_call(
        body, name="ssd_bwd", grid=(nb, nc),
        in_specs=specs + [pl.BlockSpec((CHUNK, SSD_INNER), lambda b, c: (row(b, c), 0)),
                          pl.BlockSpec((CHUNK, SSD_INNER), lambda b, c: (row(b, c), 0)),
                          pl.BlockSpec((1, 1, 2, SSD_STATE, GROUP_W), lambda b, c: (b, cidx(c), 0, 0, 0)),
                          _const_spec((4, CONV_CH)), _const_spec((1, CONV_CH)), small, small, vec, vec],
        out_specs=[pl.BlockSpec((CHUNK, SSD_INNER), lambda b, c: (row(b, c), 0)),
                   pl.BlockSpec((CHUNK, CONV_CH), lambda b, c: (row(b, c), 0)),
                   pl.BlockSpec((CHUNK, 128), lambda b, c: (row(b, c), 0)),
                   vec, vec, small, small],
        out_shape=[jax.ShapeDtypeStruct((t, SSD_INNER), BF), jax.ShapeDtypeStruct((t, CONV_CH), F32),
                   jax.ShapeDtypeStruct((t, 128), BF), jax.ShapeDtypeStruct((1, SSD_INNER), F32),
                   jax.ShapeDtypeStruct((1, SSD_INNER), F32), jax.ShapeDtypeStruct((1, SSD_HEADS), F32),
                   jax.ShapeDtypeStruct((1, SSD_HEADS), F32)],
        scratch_shapes=[pltpu.VMEM((2, SSD_STATE, GROUP_W), F32), pltpu.VMEM((CHUNK, SSD_INNER), F32),
                        pltpu.VMEM((CHUNK, SSD_INNER), F32)],
        compiler_params=_cp(("arbitrary", "arbitrary"), 48),
    )(proj, proj, proj, proj, proj, d_yssd, yssm, h_prev, conv_w, conv_b, dt_bias, a_log, dskip_e, g_ssd)


def _grad_w_in_t(d_upool, d_z, d_uxbc, d_udt, u1):
    t, d = u1.shape
    tk = 512
    n_z, n_x = SSD_INNER // tk, CONV_CH // tk

    def body(p_ref, z_ref, x_ref, dt_ref, u_ref, o_ref):
        i = pl.program_id(0)

        @pl.when(i == 0)
        def _():
            o_ref[...] = _tn(p_ref[...], u_ref[...]).astype(BF)

        @pl.when((i >= 1) & (i < 1 + n_z))
        def _():
            o_ref[...] = _tn(z_ref[...], u_ref[...]).astype(BF)

        @pl.when((i >= 1 + n_z) & (i < 1 + n_z + n_x))
        def _():
            o_ref[...] = _tn(x_ref[...], u_ref[...]).astype(BF)

        @pl.when(i == 1 + n_z + n_x)
        def _():
            o_ref[0:128, :] = _tn(dt_ref[...], u_ref[...]).astype(BF)

    return pl.pallas_call(
        body, name="grad_w_in", grid=(2 + n_z + n_x,),
        in_specs=[pl.BlockSpec((t, tk), lambda i: (0, 0)),
                  pl.BlockSpec((t, tk), lambda i: (0, jnp.clip(i - 1, 0, n_z - 1))),
                  pl.BlockSpec((t, tk), lambda i: (0, jnp.clip(i - 1 - n_z, 0, n_x - 1))),
                  pl.BlockSpec((t, 128), lambda i: (0, 0)), pl.BlockSpec((t, d), lambda i: (0, 0))],
        out_specs=pl.BlockSpec((tk, d), lambda i: (i, 0)),
        out_shape=jax.ShapeDtypeStruct((IN_PAD, d), BF),
        compiler_params=_cp(("parallel",), 56))(d_upool, d_z, d_uxbc, d_udt, u1)


def _conv_bwd(d_pre, proj, conv_w, nb, seq):
    ts = _pool_tile(seq)
    nt = seq // ts
    hb = ts // CONV_HALO
    last_block = nb * seq // CONV_HALO - 1
    n_ext = ts + CONV_HALO

    def body(dp_ref, dnext_ref, u_ref, uprev_ref, cw_ref, du_ref, gw_ref, gb_ref):
        b = pl.program_id(0)
        i = pl.program_id(1)

        @pl.when((b == 0) & (i == 0))
        def _():
            gw_ref[...] = jnp.zeros_like(gw_ref)
            gb_ref[...] = jnp.zeros_like(gb_ref)

        dp = dp_ref[...]
        ext_d = jnp.concatenate([dp, jnp.where(i == nt - 1, 0.0, dnext_ref[...])], 0)
        ext_u = jnp.concatenate([jnp.where(i == 0, 0.0, uprev_ref[...]), u_ref[...]], 0)
        cw = cw_ref[...]
        du = dp * cw[3:4]
        gw_ref[3:4, :] += jnp.sum(dp * u_ref[...], 0, keepdims=True)
        for k in (2, 1, 0):
            j = 3 - k
            du = du + pltpu.roll(ext_d, n_ext - j, 0)[:ts] * cw[k:k + 1]
            gw_ref[k:k + 1, :] += jnp.sum(dp * pltpu.roll(ext_u, j, 0)[CONV_HALO:], 0, keepdims=True)
        gb_ref[...] += jnp.sum(dp, 0, keepdims=True)
        du_ref[...] = du.astype(BF)

    return pl.pallas_call(
        body, name="conv_bwd", grid=(nb, nt),
        in_specs=[pl.BlockSpec((ts, CONV_CH), lambda b, i: (b * nt + i, 0)),
                  pl.BlockSpec((CONV_HALO, CONV_CH), lambda b, i: (jnp.minimum((b * nt + i + 1) * hb, last_block), 0)),
                  pl.BlockSpec((ts, CONV_CH), lambda b, i: (b * nt + i, 1)),
                  pl.BlockSpec((CONV_HALO, CONV_CH), lambda b, i: (jnp.maximum((b * nt + i) * hb - 1, 0), 1)),
                  pl.BlockSpec((4, CONV_CH), lambda b, i: (0, 0))],
        out_specs=[pl.BlockSpec((ts, CONV_CH), lambda b, i: (b * nt + i, 0)),
                   pl.BlockSpec((8, CONV_CH), lambda b, i: (0, 0)), pl.BlockSpec((1, CONV_CH), lambda b, i: (0, 0))],
        out_shape=[jax.ShapeDtypeStruct((nb * seq, CONV_CH), BF), jax.ShapeDtypeStruct((8, CONV_CH), F32),
                   jax.ShapeDtypeStruct((1, CONV_CH), F32)],
        compiler_params=_cp(("arbitrary", "arbitrary"), 48))(d_pre, d_pre, proj, proj, conv_w)


def _in_proj_bwd(d_parts, w_in_t, x, dh1, mod3, g_mix, seq):
    t, d = x.shape
    nb = t // seq
    tm = min(256, seq)
    tps = seq // tm

    widths = [p.shape[1] for p in d_parts]

    def body(d0_ref, d1_ref, d2_ref, d3_ref, w_ref, x_ref, dh1_ref, mod_ref, g_ref, gx_ref, acc_ref, gg_ref):
        i = pl.program_id(0)
        du = None
        off = 0
        for p_ref, wd in zip((d0_ref, d1_ref, d2_ref, d3_ref), widths):
            part = _nn(p_ref[...], w_ref[off:off + wd, :])
            du = part if du is None else du + part
            off += wd
        xv = x_ref[...]
        r = lax.rsqrt(jnp.mean(xv * xv, -1, keepdims=True) + EPS)
        hh = xv * r
        n1 = hh * g_ref[...]
        dn1 = du * (1.0 + mod_ref[0, 1:2, :])
        dhat = dn1 * g_ref[...]
        gx_ref[...] = dh1_ref[...] + r * (dhat - hh * jnp.mean(dhat * hh, -1, keepdims=True))

        @pl.when(i == 0)
        def _():
            gg_ref[...] = jnp.zeros_like(gg_ref)

        @pl.when(i % tps == 0)
        def _():
            acc_ref[...] = jnp.zeros_like(acc_ref)

        gg_ref[...] += jnp.sum(dn1 * hh, 0, keepdims=True)
        acc_ref[0, 0:1, :] += jnp.sum(du, 0, keepdims=True)
        acc_ref[0, 1:2, :] += jnp.sum(du * n1, 0, keepdims=True)

    row = lambda i: (i, 0)
    vec = pl.BlockSpec((1, d), lambda i: (0, 0))
    return pl.pallas_call(
        body, name="in_proj_bwd", grid=(t // tm,),
        in_specs=[pl.BlockSpec((tm, wd), row) for wd in widths] +
                 [pl.BlockSpec(w_in_t.shape, lambda i: (0, 0)), pl.BlockSpec((tm, d), row),
                  pl.BlockSpec((tm, d), row), pl.BlockSpec((1, N_MOD, d), lambda i: (i // tps, 0, 0)), vec],
        out_specs=[pl.BlockSpec((tm, d), row), pl.BlockSpec((1, 8, d), lambda i: (i // tps, 0, 0)), vec],
        out_shape=[jax.ShapeDtypeStruct((t, d), F32), jax.ShapeDtypeStruct((nb, 8, d), F32),
                   jax.ShapeDtypeStruct((1, d), F32)],
        compiler_params=_cp(("arbitrary",), 48))(*d_parts, w_in_t, x, dh1, mod3, g_mix)


_VEC_LAYOUT = (("g_mix", 1024), ("conv_b", 1536), ("g_ssd", 1024), ("pool_scale", 512), ("g_mlp", 1024),
               ("g_final", 1024), ("dt_bias", 128), ("a_log", 128), ("d_skip_lanes", 1024), ("sq_err", 1024))
_VEC_OFFSET = {}
_off = 0
for _name, _n in _VEC_LAYOUT:
    _VEC_OFFSET[_name] = _off
    _off += _n
_VEC_LANES = _off
_SMALL_PARAMS = ("b_ada", "g_mix", "conv_w", "conv_b", "dt_bias", "a_log", "d_skip", "g_ssd", "w_pool", "pool_scale",
                 "g_mlp", "g_final")


def _pack_vec(parts):
    cols = []
    for name, n in _VEC_LAYOUT:
        v = parts[name]
        if v.shape[1] < n:
            v = jnp.pad(v, ((0, 0), (0, n - v.shape[1])))
        cols.append(v)
    return jnp.concatenate(cols, 1)


def _small_adam(vec_all, wpool_all, convw_all, dmod_all, params):
    names = _SMALL_PARAMS
    nin = 4 + 3 * len(names)

    def body(*refs):
        vec_ref, wp_ref, cw_ref, dm_ref = refs[:4]
        prm = {n: refs[4 + 3 * i:7 + 3 * i] for i, n in enumerate(names)}
        loss_ref = refs[nin]
        outs = {n: refs[nin + 1 + 4 * i:nin + 5 + 4 * i] for i, n in enumerate(names)}
        vsum = vec_ref[0]
        for s in range(1, N_DEV):
            vsum = vsum + vec_ref[s]

        def lanes(name, n):
            off = _VEC_OFFSET[name]
            return vsum[:, off:off + n]

        grads = {n: lanes(n, prm[n][0].shape[1]) for n in ("g_mix", "conv_b", "g_ssd", "pool_scale", "g_mlp", "g_final", "dt_bias")}
        grads["a_log"] = lanes("a_log", SSD_HEADS) * (-jnp.exp(prm["a_log"][0][...]))
        per_lane = jnp.broadcast_to(lanes("d_skip_lanes", SSD_INNER), (8, SSD_INNER))
        grads["d_skip"] = _exact_nn(per_lane, _head_reduce_matrix(SSD_INNER, SSD_HEADS))[0:1]
        gwp = wp_ref[0].astype(F32)
        gcw = cw_ref[0]
        gb = jnp.sum(dm_ref[0], 0, keepdims=True)
        for s in range(1, N_DEV):
            gwp = gwp + wp_ref[s].astype(F32)
            gcw = gcw + cw_ref[s]
            gb = gb + jnp.sum(dm_ref[s], 0, keepdims=True)
        grads["w_pool"] = gwp
        grads["conv_w"] = gcw[0:4]
        grads["b_ada"] = gb
        total = jnp.sum(lanes("sq_err", D_MODEL), 1, keepdims=True) * (0.5 / D_MODEL)
        loss_ref[...] = jnp.broadcast_to(total, loss_ref.shape)
        for n in names:
            w_ref, m_ref, v_ref = prm[n]
            g = grads[n]
            d, m2, v2 = _adam_math(w_ref[...], g, m_ref[...], v_ref[...])
            g_ref, d_ref, m2_ref, v2_ref = outs[n]
            g_ref[...] = g
            d_ref[...] = d
            m2_ref[...] = m2
            v2_ref[...] = v2

    flat = [vec_all, wpool_all, convw_all, dmod_all]
    out_shape = [jax.ShapeDtypeStruct((1, 128), F32)]
    for n in names:
        flat += list(params[n])
        out_shape += [jax.ShapeDtypeStruct(params[n][0].shape, F32)] * 4
    vm = pl.BlockSpec(memory_space=pltpu.VMEM)
    res = pl.pallas_call(body, name="small_adam", out_shape=out_shape, in_specs=[vm] * len(flat),
                         out_specs=[vm] * len(out_shape), compiler_params=_cp(vmem_mb=48))(*flat)
    return res[0], {n: res[1 + 4 * i:5 + 4 * i] for i, n in enumerate(names)}


_WEIGHTS = ("w_ada", "b_ada", "g_mix", "w_in", "conv_w", "conv_b", "dt_bias", "a_log", "d_skip", "g_ssd", "w_pool",
            "pool_scale", "w_out", "g_mlp", "w_up", "w_down", "g_final")


def _local_step(x2, tg2, mod3, seq, w_in_t, w_out_f, w_up4, w_down4, conv_w_full, sp):
    t, d = x2.shape
    nb = t // seq
    w_down_f = w_down4.reshape(D_FF, d)
    dskip_e = jnp.repeat(sp["d_skip"], SSD_HEAD_DIM, axis=1)
    proj, u1 = _in_proj(x2, mod3, sp["g_mix"], w_in_t, seq)
    y_pool, p = _pool_fwd(proj, sp["w_pool"], sp["pool_scale"], nb, seq)
    y_ssd, yssm, h_prev = _ssd_fwd(proj, conv_w_full, sp["conv_b"], sp["dt_bias"], sp["a_log"], dskip_e, sp["g_ssd"], nb, seq)
    h1, o, u2 = _out_proj(y_pool, y_ssd, w_out_f, x2, mod3, sp["g_mlp"], seq)
    a_up = _mlp_up(u2, w_up4)
    d_dn, dh2, sq, gg_final, d_gf = _mlp_down_loss(a_up, w_down_f, h1, mod3, sp["g_final"], tg2, seq)

    gw_down = _tn_matmul(a_up, d_dn, 512, d, "grad_w_down", square_relu=True)
    d_a = _mlp_down_bwd(d_dn, w_down4, a_up)
    gw_up4 = _tn_matmul(u2, d_a, 512, d, "grad_w_up", out3=True)
    dh1, d_o, accf, gg_mlp = _mlp_up_bwd(d_a, w_up4, h1, dh2, o, mod3, sp["g_mlp"], seq)
    gw_out_pool = _tn_matmul(y_pool, d_o, 512, d, "grad_w_out_pool")
    gw_out_ssd = _tn_matmul(y_ssd, d_o, 512, d, "grad_w_out_ssd")
    d_ypool, d_yssd = _out_proj_bwd(d_o, w_out_f)
    d_upool, gw_pool, g_ps = _pool_bwd(d_ypool, p, sp["w_pool"], sp["pool_scale"], nb, seq)
    d_z, d_pre, d_udt, gg_ssd, gdsk, ga, gdtb = _ssd_bwd(proj, d_yssd, yssm, h_prev, conv_w_full, sp["conv_b"],
                                                        sp["dt_bias"], sp["a_log"], dskip_e, sp["g_ssd"], nb, seq)
    d_uxbc, gconvw, gconvb = _conv_bwd(d_pre, proj, conv_w_full, nb, seq)
    gw_in_t = _grad_w_in_t(d_upool, d_z, d_uxbc, d_udt, u1)
    gx, accm, gg_mix = _in_proj_bwd([d_upool, d_z, d_uxbc, d_udt], w_in_t, x2, dh1, mod3, sp["g_mix"], seq)

    d_mod = jnp.concatenate([accm[:, 0], accm[:, 1], accf[:, 2], accf[:, 0], accf[:, 1], d_gf[:, 0]], 1)
    vec = _pack_vec({"g_mix": gg_mix, "conv_b": gconvb, "g_ssd": gg_ssd, "pool_scale": g_ps, "g_mlp": gg_mlp,
                     "g_final": gg_final, "dt_bias": gdtb, "a_log": ga, "d_skip_lanes": gdsk, "sq_err": sq})
    big = dict(w_in=gw_in_t, w_out=jnp.concatenate([gw_out_pool, gw_out_ssd], 0), w_up=gw_up4, w_down=gw_down)
    return gx, d_mod, vec, gw_pool, gconvw, big


def kernel(x, c, w_ada, b_ada, g_mix, w_in, conv_w, conv_b, dt_bias, a_log, d_skip, g_ssd, w_pool, pool_scale, w_out, g_mlp, w_up, w_down, g_final, loss_target, m_w_ada, m_b_ada, m_g_mix, m_w_in, m_conv_w, m_conv_b, m_dt_bias, m_a_log, m_d_skip, m_g_ssd, m_w_pool, m_pool_scale, m_w_out, m_g_mlp, m_w_up, m_w_down, m_g_final, v_w_ada, v_b_ada, v_g_mix, v_w_in, v_conv_w, v_conv_b, v_dt_bias, v_a_log, v_d_skip, v_g_ssd, v_w_pool, v_pool_scale, v_w_out, v_g_mlp, v_w_up, v_w_down, v_g_final):
    nb, seq, d = x.shape
    t = nb * seq
    xi, yi, ci = _mesh_pos()
    chip = 2 * xi + yi
    me = 4 * xi + 2 * yi + ci
    ada_cols = w_ada.shape[2]
    conv_cols = conv_w.shape[2]
    in_cols = w_in.shape[2]
    w_in_s, m_w_in_s, v_w_in_s = w_in[0].T, m_w_in[0].T, v_w_in[0].T

    c8, convw8 = _all_gather_small([c, conv_w[0]])
    c_all = c8.reshape(N_DEV * nb, d)
    conv_w_full = convw8[0::2].transpose(1, 0, 2).reshape(4, N_CHIPS * conv_cols)
    b_shard = lax.dynamic_slice(b_ada, (0, chip * ada_cols), (1, ada_cols))
    mod_part, c_act = _ada_mod(c_all, w_ada[0], b_shard)
    (mod8,) = _all_gather_small([mod_part])
    mod_all = mod8[0::2].transpose(1, 0, 2).reshape(N_DEV * nb, N_CHIPS * ada_cols)
    mod3 = lax.dynamic_slice(mod_all, (nb * me, 0), (nb, N_CHIPS * ada_cols)).reshape(nb, N_MOD, d)

    w_in4, w_out4, w_up4, w_down4 = _weight_all_gather(
        [w_in_s.astype(BF), w_out[0].astype(BF), w_up[0].astype(BF), w_down[0].astype(BF)])
    w_in_t = jnp.pad(w_in4.reshape(N_CHIPS * in_cols, d), ((0, IN_PAD - N_CHIPS * in_cols), (0, 0)))
    w_out_f = w_out4.reshape(N_CHIPS * w_out.shape[1], d)

    sp = dict(g_mix=g_mix, conv_b=conv_b, dt_bias=dt_bias, a_log=a_log, d_skip=d_skip, g_ssd=g_ssd,
              w_pool=w_pool[0], pool_scale=pool_scale, g_mlp=g_mlp, g_final=g_final.reshape(1, d))
    gx, d_mod, vec, gw_pool, gconvw, big = _local_step(
        x.reshape(t, d), loss_target.reshape(t, d), mod3, seq, w_in_t, w_out_f, w_up4, w_down4, conv_w_full, sp)

    grads4 = [big["w_in"][:N_CHIPS * in_cols].reshape(N_CHIPS, in_cols, d),
              big["w_out"].reshape(N_CHIPS, w_out.shape[1], d), big["w_up"], big["w_down"].reshape(N_CHIPS, w_down.shape[1], d)]
    own, got = _grad_to_sibling(grads4)
    parts = [_add_pair(a, b) for a, b in zip(own, got)]
    recv = _grad_to_chips(parts)
    g_in, g_out, g_up, g_down = _halves_exchange([_sum_slots(r) for r in recv])

    vec8, wpool8, convw8g, dmod8 = _all_gather_small(
        [vec, gw_pool.reshape(4 * POOL_GROUP, POOL_GROUP).astype(BF), gconvw, d_mod])
    convw8s = lax.dynamic_slice(convw8g, (0, 0, chip * conv_cols), (N_DEV, 8, conv_cols))
    m_in = dict(b_ada=m_b_ada, g_mix=m_g_mix, conv_w=m_conv_w[0], conv_b=m_conv_b, dt_bias=m_dt_bias, a_log=m_a_log,
                d_skip=m_d_skip, g_ssd=m_g_ssd, w_pool=m_w_pool.reshape(4 * POOL_GROUP, POOL_GROUP), pool_scale=m_pool_scale,
                g_mlp=m_g_mlp, g_final=m_g_final.reshape(1, d))
    v_in = dict(b_ada=v_b_ada, g_mix=v_g_mix, conv_w=v_conv_w[0], conv_b=v_conv_b, dt_bias=v_dt_bias, a_log=v_a_log,
                d_skip=v_d_skip, g_ssd=v_g_ssd, w_pool=v_w_pool.reshape(4 * POOL_GROUP, POOL_GROUP), pool_scale=v_pool_scale,
                g_mlp=v_g_mlp, g_final=v_g_final.reshape(1, d))
    w_small = dict(sp, b_ada=b_ada, conv_w=conv_w[0], w_pool=w_pool.reshape(4 * POOL_GROUP, POOL_GROUP))
    loss_row, small = _small_adam(vec8, wpool8, convw8s, dmod8, {n: (w_small[n], m_in[n], v_in[n]) for n in _SMALL_PARAMS})

    dmod_all = dmod8.reshape(N_DEV * nb, N_CHIPS * ada_cols)
    dmod_cols = lax.dynamic_slice(dmod_all, (0, chip * ada_cols), (N_DEV * nb, ada_cols))
    res = {n: tuple(r.reshape(w.shape) for r in small[n])
           for n, w in (("b_ada", b_ada), ("g_mix", g_mix), ("conv_w", conv_w), ("conv_b", conv_b), ("dt_bias", dt_bias),
                        ("a_log", a_log), ("d_skip", d_skip), ("g_ssd", g_ssd), ("w_pool", w_pool), ("pool_scale", pool_scale),
                        ("g_mlp", g_mlp), ("g_final", g_final))}
    g_ada, d_ada, m_ada, v_ada = _adam_ada(c_act.T.astype(BF), dmod_cols, w_ada[0], m_w_ada[0], v_w_ada[0])
    res["w_ada"] = (g_ada[None], d_ada[None], m_ada[None], v_ada[None])
    dl, m2, v2 = _adam_big(g_in, w_in_s, m_w_in_s, v_w_in_s)
    res["w_in"] = (g_in.T[None], dl.T[None], m2.T[None], v2.T[None])
    for n, g, w, m, v in (("w_out", g_out, w_out, m_w_out, v_w_out), ("w_up", g_up, w_up, m_w_up, v_w_up),
                          ("w_down", g_down, w_down, m_w_down, v_w_down)):
        dl, m2, v2 = _adam_big(g, w[0], m[0], v[0])
        res[n] = (g[None], dl[None], m2[None], v2[None])

    loss = loss_row[0, 0]
    return (loss, gx.reshape(nb, seq, d), *[res[n][0] for n in _WEIGHTS], *[res[n][1] for n in _WEIGHTS],
            *[res[n][2] for n in _WEIGHTS], *[res[n][3] for n in _WEIGHTS])
```

```python
import functools

import jax
import jax.numpy as jnp
from jax import lax
from jax.experimental import pallas as pl
from jax.experimental.pallas import tpu as pltpu

F32 = jnp.float32
BF = jnp.bfloat16
MESH = pl.DeviceIdType.MESH

EPS = 1e-5
D_MODEL = 1024
POOL_WIDTH = 512
POOL_WINDOWS = (2, 4, 8, 16)
POOL_GROUP = 128
SSD_INNER = 1024
SSD_HEADS = 16
SSD_HEAD_DIM = 64
SSD_STATE = 128
GROUP_W = 512
CHUNK = 128
CONV_CH = 1536
OFF_Z = 512
OFF_XBC = 1536
OFF_DT = 3072
IN_WIDTH = 3088
IN_PAD = 3200
D_FF = 4096
N_MOD = 6
N_CHIPS = 4
N_DEV = 8
HALO = 16
CONV_HALO = 8

ADAM_LR = 0.001
ADAM_B1 = 0.9
ADAM_B2 = 0.999
ADAM_EPS = 1e-08
ADAM_WD = 0.01
ADAM_STEP = 10

VMEM_BYTES_V7X = 64 * 1024 * 1024


def _cp(semantics=None, vmem_mb=48, **kw):
    args = dict(vmem_limit_bytes=vmem_mb * 1024 * 1024, **kw)
    if semantics is not None:
        args["dimension_semantics"] = semantics
    return pltpu.CompilerParams(**args)


def _nn(a, b):
    return jnp.dot(a, b, preferred_element_type=F32)


def _nt(a, b):
    return lax.dot_general(a, b, (((1,), (1,)), ((), ())), preferred_element_type=F32)


def _tn(a, b):
    return lax.dot_general(a, b, (((0,), (0,)), ((), ())), preferred_element_type=F32)


def _split3(v):
    hi = v.astype(BF)
    r1 = v - hi.astype(F32)
    mid = r1.astype(BF)
    lo = (r1 - mid.astype(F32)).astype(BF)
    return hi, mid, lo


def _exact_nn(v, m01):
    hi, mid, lo = _split3(v)
    return _nn(hi, m01) + _nn(mid, m01) + _nn(lo, m01)


def _exact_nn_left(m01, v):
    hi, mid, lo = _split3(v)
    return _nn(m01, hi) + _nn(m01, mid) + _nn(m01, lo)


def _exact_nt_left(m01, v):
    hi, mid, lo = _split3(v)
    return _nt(m01, hi) + _nt(m01, mid) + _nt(m01, lo)


def _sigmoid(v):
    return 1.0 / (1.0 + jnp.exp(-v))


def _iota(shape, dim):
    return lax.broadcasted_iota(jnp.int32, shape, dim)


def _head_expand_matrix(heads, width):
    return (_iota((heads, width), 1) // SSD_HEAD_DIM == _iota((heads, width), 0)).astype(BF)


def _head_reduce_matrix(width, heads):
    return (_iota((width, heads), 0) // SSD_HEAD_DIM == _iota((width, heads), 1)).astype(BF)


def _mesh_pos():
    return lax.axis_index("x"), lax.axis_index("y"), lax.axis_index("c")


def _flip(v, bit):
    return v + bit - 2 * bit * v


def _all_gather_small(arrays):
    n = len(arrays)

    def body(*refs):
        in_refs, out_refs = refs[:n], refs[n:2 * n]
        send_sems, recv_sems, local_sems = refs[2 * n:]
        x, y, c = _mesh_pos()
        me = 4 * x + 2 * y + c
        local = []
        for a in range(n):
            cp = pltpu.make_async_copy(in_refs[a], out_refs[a].at[me], local_sems.at[a])
            cp.start()
            local.append(cp)
        sends = []
        for k in range(1, N_DEV):
            peer = (_flip(x, (k >> 2) & 1), _flip(y, (k >> 1) & 1), _flip(c, k & 1))
            for a in range(n):
                cp = pltpu.make_async_remote_copy(
                    src_ref=in_refs[a], dst_ref=out_refs[a].at[me],
                    send_sem=send_sems.at[a, k], recv_sem=recv_sems.at[a, k],
                    device_id=peer, device_id_type=MESH)
                cp.start()
                sends.append(cp)
        for k in range(1, N_DEV):
            px, py, pc = _flip(x, (k >> 2) & 1), _flip(y, (k >> 1) & 1), _flip(c, k & 1)
            src = 4 * px + 2 * py + pc
            for a in range(n):
                pltpu.make_async_remote_copy(
                    src_ref=in_refs[a], dst_ref=out_refs[a].at[src],
                    send_sem=send_sems.at[a, k], recv_sem=recv_sems.at[a, k],
                    device_id=(px, py, pc), device_id_type=MESH).wait_recv()
        for cp in sends:
            cp.wait_send()
        for cp in local:
            cp.wait()

    vm = pl.BlockSpec(memory_space=pltpu.VMEM)
    return pl.pallas_call(
        body, name="all_gather_small",
        out_shape=[jax.ShapeDtypeStruct((N_DEV,) + a.shape, a.dtype) for a in arrays],
        in_specs=[vm] * n, out_specs=[vm] * n,
        scratch_shapes=[pltpu.SemaphoreType.DMA((n, N_DEV)), pltpu.SemaphoreType.DMA((n, N_DEV)),
                        pltpu.SemaphoreType.DMA((n,))],
        compiler_params=_cp(vmem_mb=32),
    )(*arrays)


def _weight_all_gather(shards):
    n = len(shards)
    any_spec = pl.BlockSpec(memory_space=pl.ANY)

    def body(*refs):
        in_refs, out_refs = refs[:n], refs[n:2 * n]
        send_sems, recv_sems, fsend_sems, frecv_sems, local_sems = refs[2 * n:]
        x, y, c = _mesh_pos()
        chip = 2 * x + y
        local, sends = [], []
        for a in range(n):
            cp = pltpu.make_async_copy(in_refs[a], out_refs[a].at[chip], local_sems.at[a])
            cp.start()
            local.append(cp)

        def half(a, which):
            hc = shards[a].shape[1] // 2
            return pl.ds(pl.multiple_of(which * hc, 128), hc)

        for j in range(1, N_CHIPS):
            px, py = _flip(x, (j >> 1) & 1), _flip(y, j & 1)
            for a in range(n):
                cp = pltpu.make_async_remote_copy(
                    src_ref=in_refs[a].at[:, half(a, c)], dst_ref=out_refs[a].at[chip, :, half(a, c)],
                    send_sem=send_sems.at[a, j], recv_sem=recv_sems.at[a, j],
                    device_id=(px, py, c), device_id_type=MESH)
                cp.start()
                sends.append(cp)
        for j in range(1, N_CHIPS):
            px, py = _flip(x, (j >> 1) & 1), _flip(y, j & 1)
            src = 2 * px + py
            for a in range(n):
                landed = out_refs[a].at[src, :, half(a, c)]
                pltpu.make_async_remote_copy(
                    src_ref=landed, dst_ref=landed, send_sem=send_sems.at[a, j], recv_sem=recv_sems.at[a, j],
                    device_id=(px, py, c), device_id_type=MESH).wait_recv()
                cp = pltpu.make_async_remote_copy(
                    src_ref=landed, dst_ref=landed, send_sem=fsend_sems.at[a, j], recv_sem=frecv_sems.at[a, j],
                    device_id=(x, y, 1 - c), device_id_type=MESH)
                cp.start()
                sends.append(cp)
        for j in range(1, N_CHIPS):
            px, py = _flip(x, (j >> 1) & 1), _flip(y, j & 1)
            src = 2 * px + py
            for a in range(n):
                other = out_refs[a].at[src, :, half(a, 1 - c)]
                pltpu.make_async_remote_copy(
                    src_ref=other, dst_ref=other, send_sem=fsend_sems.at[a, j], recv_sem=frecv_sems.at[a, j],
                    device_id=(x, y, 1 - c), device_id_type=MESH).wait_recv()
        for cp in sends:
            cp.wait_send()
        for cp in local:
            cp.wait()

    return pl.pallas_call(
        body, name="weight_all_gather",
        out_shape=[jax.ShapeDtypeStruct((N_CHIPS,) + s.shape, s.dtype) for s in shards],
        in_specs=[any_spec] * n, out_specs=[any_spec] * n,
        scratch_shapes=[pltpu.SemaphoreType.DMA((n, N_CHIPS))] * 4 + [pltpu.SemaphoreType.DMA((n,))],
        compiler_params=_cp(vmem_mb=16),
    )(*shards)


_HBM = pl.BlockSpec(memory_space=pltpu.HBM)
_SEM = pl.BlockSpec(memory_space=pltpu.SEMAPHORE)
_DATAFLOW = pltpu.SideEffectType.DATAFLOW_SIDE_EFFECTING


def _peer_chip(x, y, j):
    return _flip(x, (j >> 1) & 1), _flip(y, j & 1)


def _ici_start(srcs, land_shapes, sent, landing, name):
    n = len(srcs)

    def body(*refs):
        src_refs, land_refs = refs[:n], refs[n:2 * n]
        send_sems, recv_sems = refs[2 * n], refs[2 * n + 1]
        token = refs[-1]
        x, y, c = _mesh_pos()
        for j in range(1, N_CHIPS):
            px, py = _peer_chip(x, y, j)
            for a in range(n):
                pltpu.make_async_remote_copy(
                    src_ref=sent(src_refs[a], c, 2 * px + py), dst_ref=landing(land_refs[a], c, 2 * x + y),
                    send_sem=send_sems.at[a * (N_CHIPS - 1) + j - 1], recv_sem=recv_sems.at[a * (N_CHIPS - 1) + j - 1],
                    device_id=(px, py, c), device_id_type=MESH).start()
        token[...] = jnp.zeros_like(token)

    sems = pltpu.SemaphoreType.DMA((n * (N_CHIPS - 1),))
    lands = [pltpu.with_memory_space_constraint(lax.empty(s.shape, s.dtype), pltpu.HBM) for s in land_shapes]
    outs = pl.pallas_call(
        body, name=name,
        out_shape=(sems, sems, *[pltpu.HBM(s.shape, s.dtype) for s in srcs],
                   *[pltpu.HBM(s.shape, s.dtype) for s in land_shapes], jax.ShapeDtypeStruct((8, 128), F32)),
        in_specs=[_HBM] * (2 * n), out_specs=[_SEM, _SEM] + [_HBM] * (2 * n) + [pl.BlockSpec(memory_space=pltpu.VMEM)],
        input_output_aliases={i: 2 + i for i in range(2 * n)},
        compiler_params=pltpu.CompilerParams(has_side_effects=_DATAFLOW),
    )(*[pltpu.with_memory_space_constraint(s, pltpu.HBM) for s in srcs], *lands)
    return outs[0], outs[1], outs[2:2 + n], outs[2 + n:2 + 2 * n], outs[-1]


def _ici_wait(send_sems, recv_sems, src_thru, land_thru, after, sent, landing, name):
    n = len(src_thru)

    def body(*refs):
        src_refs, land_refs = refs[:n], refs[n:2 * n]
        send_sems, recv_sems = refs[2 * n], refs[2 * n + 1]
        x, y, c = _mesh_pos()
        for j in range(1, N_CHIPS):
            px, py = _peer_chip(x, y, j)
            for a in range(n):
                cp = pltpu.make_async_remote_copy(
                    src_ref=sent(src_refs[a], c, 2 * px + py), dst_ref=landing(land_refs[a], c, 2 * px + py),
                    send_sem=send_sems.at[a * (N_CHIPS - 1) + j - 1], recv_sem=recv_sems.at[a * (N_CHIPS - 1) + j - 1],
                    device_id=(px, py, c), device_id_type=MESH)
                cp.wait_send()
                cp.wait_recv()

    outs = pl.pallas_call(
        body, name=name,
        out_shape=tuple(pltpu.HBM(s.shape, s.dtype) for s in (*src_thru, *land_thru)),
        in_specs=[_HBM] * (2 * n) + [_SEM, _SEM, pl.BlockSpec(memory_space=pl.ANY)], out_specs=[_HBM] * (2 * n),
        input_output_aliases={i: i for i in range(2 * n)},
        compiler_params=pltpu.CompilerParams(has_side_effects=_DATAFLOW),
    )(*src_thru, *land_thru, send_sems, recv_sems, after)
    return outs[:n], outs[n:]


def _col_half(ref, which, lead=()):
    hc = ref.shape[-1] // 2
    return ref.at[(*lead, slice(None), pl.ds(pl.multiple_of(which * hc, 128), hc))]


def _gather_sent(ref, c, dst_chip):
    return _col_half(ref, c)


def _gather_landing(ref, c, src_chip):
    return _col_half(ref, c, lead=(src_chip,))


def _reduce_sent(ref, c, dst_chip):
    return ref.at[dst_chip]


def _reduce_landing(ref, c, src_chip):
    return ref.at[src_chip]


def _gather_finish(lands, shards):
    n = len(lands)
    any_spec = pl.BlockSpec(memory_space=pl.ANY)

    def body(*refs):
        shard_refs, out_refs = refs[n:2 * n], refs[2 * n:3 * n]
        send_sems, recv_sems, local_sems = refs[3 * n:]
        x, y, c = _mesh_pos()
        chip = 2 * x + y
        local, sends = [], []
        for a in range(n):
            cp = pltpu.make_async_copy(shard_refs[a], out_refs[a].at[chip], local_sems.at[a])
            cp.start()
            local.append(cp)
        for j in range(1, N_CHIPS):
            px, py = _peer_chip(x, y, j)
            for a in range(n):
                landed = _col_half(out_refs[a], c, lead=(2 * px + py,))
                cp = pltpu.make_async_remote_copy(
                    src_ref=landed, dst_ref=landed, send_sem=send_sems.at[a, j], recv_sem=recv_sems.at[a, j],
                    device_id=(x, y, 1 - c), device_id_type=MESH)
                cp.start()
                sends.append(cp)
        for j in range(1, N_CHIPS):
            px, py = _peer_chip(x, y, j)
            for a in range(n):
                other = _col_half(out_refs[a], 1 - c, lead=(2 * px + py,))
                pltpu.make_async_remote_copy(
                    src_ref=other, dst_ref=other, send_sem=send_sems.at[a, j], recv_sem=recv_sems.at[a, j],
                    device_id=(x, y, 1 - c), device_id_type=MESH).wait_recv()
        for cp in sends:
            cp.wait_send()
        for cp in local:
            cp.wait()

    return pl.pallas_call(
        body, name="gather_finish",
        out_shape=[jax.ShapeDtypeStruct(l.shape, l.dtype) for l in lands],
        in_specs=[any_spec] * (2 * n), out_specs=[any_spec] * n,
        input_output_aliases={i: i for i in range(n)},
        scratch_shapes=[pltpu.SemaphoreType.DMA((n, N_CHIPS))] * 2 + [pltpu.SemaphoreType.DMA((n,))],
        compiler_params=_cp(vmem_mb=16),
    )(*lands, *shards)


def _grad_to_sibling(grads):
    n = len(grads)
    any_spec = pl.BlockSpec(memory_space=pl.ANY)

    def body(*refs):
        in_refs, own_refs, got_refs = refs[:n], refs[n:2 * n], refs[2 * n:3 * n]
        send_sems, recv_sems, local_sems = refs[3 * n:]
        x, y, c = _mesh_pos()
        copies, local = [], []
        for a in range(n):
            hc = grads[a].shape[2] // 2
            mine = pl.ds(pl.multiple_of(c * hc, 128), hc)
            theirs = pl.ds(pl.multiple_of((1 - c) * hc, 128), hc)
            lc = pltpu.make_async_copy(in_refs[a].at[:, :, mine], own_refs[a], local_sems.at[a])
            lc.start()
            local.append(lc)
            cp = pltpu.make_async_remote_copy(
                src_ref=in_refs[a].at[:, :, theirs], dst_ref=got_refs[a],
                send_sem=send_sems.at[a], recv_sem=recv_sems.at[a],
                device_id=(x, y, 1 - c), device_id_type=MESH)
            cp.start()
            copies.append(cp)
        for cp in copies:
            cp.wait_recv()
        for cp in copies:
            cp.wait_send()
        for lc in local:
            lc.wait()

    half = [jax.ShapeDtypeStruct((g.shape[0], g.shape[1], g.shape[2] // 2), g.dtype) for g in grads]
    outs = pl.pallas_call(
        body, name="grad_to_sibling", out_shape=half + half,
        in_specs=[any_spec] * n, out_specs=[any_spec] * (2 * n),
        scratch_shapes=[pltpu.SemaphoreType.DMA((n,))] * 3,
        compiler_params=_cp(vmem_mb=16),
    )(*grads)
    return outs[:n], outs[n:]


def _halves_exchange(halves):
    n = len(halves)
    any_spec = pl.BlockSpec(memory_space=pl.ANY)

    def body(*refs):
        in_refs, out_refs = refs[:n], refs[n:2 * n]
        send_sems, recv_sems, local_sems = refs[2 * n:]
        x, y, c = _mesh_pos()
        copies, local = [], []
        for a in range(n):
            hc = halves[a].shape[1]
            mine = pl.ds(pl.multiple_of(c * hc, 128), hc)
            lc = pltpu.make_async_copy(in_refs[a], out_refs[a].at[:, mine], local_sems.at[a])
            lc.start()
            local.append(lc)
            cp = pltpu.make_async_remote_copy(
                src_ref=in_refs[a], dst_ref=out_refs[a].at[:, mine],
                send_sem=send_sems.at[a], recv_sem=recv_sems.at[a],
                device_id=(x, y, 1 - c), device_id_type=MESH)
            cp.start()
            copies.append(cp)
        for a in range(n):
            hc = halves[a].shape[1]
            theirs = pl.ds(pl.multiple_of((1 - c) * hc, 128), hc)
            pltpu.make_async_remote_copy(
                src_ref=in_refs[a], dst_ref=out_refs[a].at[:, theirs],
                send_sem=send_sems.at[a], recv_sem=recv_sems.at[a],
                device_id=(x, y, 1 - c), device_id_type=MESH).wait_recv()
        for cp in copies:
            cp.wait_send()
        for lc in local:
            lc.wait()

    return pl.pallas_call(
        body, name="halves_exchange",
        out_shape=[jax.ShapeDtypeStruct((h.shape[0], 2 * h.shape[1]), h.dtype) for h in halves],
        in_specs=[any_spec] * n, out_specs=[any_spec] * n,
        scratch_shapes=[pltpu.SemaphoreType.DMA((n,))] * 3,
        compiler_params=_cp(vmem_mb=16),
    )(*halves)


def _add_pair(a, b):
    k, h, c = a.shape

    def body(a_ref, b_ref, o_ref):
        o_ref[...] = (a_ref[...].astype(F32) + b_ref[...].astype(F32)).astype(o_ref.dtype)

    spec = pl.BlockSpec((1, h, c), lambda i: (i, 0, 0))
    return pl.pallas_call(body, name="add_pair", grid=(k,), in_specs=[spec, spec], out_specs=spec,
                          out_shape=jax.ShapeDtypeStruct(a.shape, BF), compiler_params=_cp(("parallel",), 32))(a, b)


def _sum_slots(recv, own):
    k, h, c = recv.shape
    tc = c // 2

    def body(r_ref, p_ref, o_ref):
        chip = 2 * lax.axis_index("x") + lax.axis_index("y")
        o_ref[...] = jnp.zeros_like(o_ref)
        for s in range(k):
            @pl.when(chip == s)
            def _():
                o_ref[...] += p_ref[s].astype(F32)

            @pl.when(chip != s)
            def _():
                o_ref[...] += r_ref[s].astype(F32)

    spec = pl.BlockSpec((k, h, tc), lambda i: (0, 0, i))
    return pl.pallas_call(body, name="sum_slots", grid=(2,), in_specs=[spec, spec],
                          out_specs=pl.BlockSpec((h, tc), lambda i: (0, i)),
                          out_shape=jax.ShapeDtypeStruct((h, c), F32), compiler_params=_cp(("parallel",), 32))(recv, own)


def _adam_math(w, g, m, v):
    m2 = ADAM_B1 * m + (1.0 - ADAM_B1) * g
    v2 = ADAM_B2 * v + (1.0 - ADAM_B2) * (g * g)
    m_hat = m2 / (1.0 - ADAM_B1 ** ADAM_STEP)
    v_hat = v2 / (1.0 - ADAM_B2 ** ADAM_STEP)
    delta = -ADAM_LR * (m_hat / (jnp.sqrt(v_hat) + ADAM_EPS) + ADAM_WD * w)
    return delta, m2, v2


def _adam_big(g, w, m, v):
    r, c = w.shape
    tc = c // 4

    def body(g_ref, w_ref, m_ref, v_ref, d_ref, m2_ref, v2_ref):
        d, m2, v2 = _adam_math(w_ref[...], g_ref[...], m_ref[...], v_ref[...])
        d_ref[...] = d
        m2_ref[...] = m2
        v2_ref[...] = v2

    spec = pl.BlockSpec((r, tc), lambda i: (0, i))
    sh = jax.ShapeDtypeStruct((r, c), F32)
    return pl.pallas_call(body, name="adam_big", grid=(4,), in_specs=[spec] * 4, out_specs=[spec] * 3,
                          out_shape=[sh, sh, sh], compiler_params=_cp(("parallel",), 32))(g, w, m, v)


def _adam_ada(c_act_t, dmod_cols, w, m, v):
    r, c = w.shape
    tc = 512

    def body(ct_ref, dm_ref, w_ref, m_ref, v_ref, g_ref, d_ref, m2_ref, v2_ref):
        g = _nn(ct_ref[...], dm_ref[...].astype(BF))
        d, m2, v2 = _adam_math(w_ref[...], g, m_ref[...], v_ref[...])
        g_ref[...] = g
        d_ref[...] = d
        m2_ref[...] = m2
        v2_ref[...] = v2

    spec = pl.BlockSpec((r, tc), lambda i: (0, i))
    sh = jax.ShapeDtypeStruct((r, c), F32)
    return pl.pallas_call(
        body, name="adam_ada", grid=(c // tc,),
        in_specs=[pl.BlockSpec(c_act_t.shape, lambda i: (0, 0)), pl.BlockSpec((dmod_cols.shape[0], tc), lambda i: (0, i)),
                  spec, spec, spec],
        out_specs=[spec] * 4, out_shape=[sh] * 4, compiler_params=_cp(("parallel",), 48))(c_act_t, dmod_cols, w, m, v)


def _ada_mod(c_all, w_shard, b_shard):
    nb, d = c_all.shape
    cols = w_shard.shape[1]
    tc = 512

    def body(c_ref, w_ref, b_ref, mod_ref, act_ref):
        cv = c_ref[...]
        act = cv * _sigmoid(cv)
        act_ref[...] = act
        mod_ref[...] = _nn(act.astype(BF), w_ref[...].astype(BF)) + b_ref[...]

    return pl.pallas_call(
        body, name="ada_mod", grid=(cols // tc,),
        in_specs=[pl.BlockSpec((nb, d), lambda i: (0, 0)), pl.BlockSpec((d, tc), lambda i: (0, i)),
                  pl.BlockSpec((1, tc), lambda i: (0, i))],
        out_specs=[pl.BlockSpec((nb, tc), lambda i: (0, i)), pl.BlockSpec((nb, d), lambda i: (0, 0))],
        out_shape=[jax.ShapeDtypeStruct((nb, cols), F32), jax.ShapeDtypeStruct((nb, d), F32)],
        compiler_params=_cp(("arbitrary",), 32))(c_all, w_shard, b_shard)


def _token_spec():
    return pl.BlockSpec((8, 128), lambda *_: (0, 0))


def _in_proj(x, mod3, g_mix, w_in_t, seq, token):
    t, d = x.shape
    tm = 256
    tps = seq // tm

    def body(x_ref, mod_ref, g_ref, w_ref, tok_ref, proj_ref, u1_ref):
        xv = x_ref[...]
        r = lax.rsqrt(jnp.mean(xv * xv, -1, keepdims=True) + EPS)
        u = (xv * r * g_ref[...]) * (1.0 + mod_ref[0, 1:2, :]) + mod_ref[0, 0:1, :]
        ub = u.astype(BF)
        u1_ref[...] = ub
        proj_ref[...] = _nt(ub, w_ref[...])

    return pl.pallas_call(
        body, name="in_proj", grid=(t // tm,),
        in_specs=[pl.BlockSpec((tm, d), lambda i: (i, 0)), pl.BlockSpec((1, N_MOD, d), lambda i: (i // tps, 0, 0)),
                  pl.BlockSpec((1, d), lambda i: (0, 0)), pl.BlockSpec((IN_PAD, d), lambda i: (0, 0)), _token_spec()],
        out_specs=[pl.BlockSpec((tm, IN_PAD), lambda i: (i, 0)), pl.BlockSpec((tm, d), lambda i: (i, 0))],
        out_shape=[jax.ShapeDtypeStruct((t, IN_PAD), F32), jax.ShapeDtypeStruct((t, d), BF)],
        compiler_params=_cp(("parallel",), 48))(x, mod3, g_mix, w_in_t, token)


def _pool_tile(seq):
    return min(512, seq)


def _pool_fwd(proj, w_pool, pool_scale, nb, seq):
    ts = _pool_tile(seq)
    nt = seq // ts

    def body(u_ref, halo_ref, wp_ref, ps_ref, yp_ref, p_ref):
        i = pl.program_id(1)
        halo = jnp.where(i == 0, 0.0, halo_ref[...])
        u = u_ref[...]
        ext = jnp.concatenate([halo, u], 0)
        tpos = i * ts + _iota((ts, 1), 0)
        for g, w in enumerate(POOL_WINDOWS):
            gs = slice(g * POOL_GROUP, (g + 1) * POOL_GROUP)
            s = ext[:, gs]
            sh = 1
            while sh < w:
                s = s + pltpu.roll(s, sh, 0)
                sh *= 2
            cnt = jnp.minimum(tpos + 1, w).astype(F32)
            pb = (s[HALO:] / cnt - u[:, gs]).astype(BF)
            p_ref[:, gs] = pb
            yp_ref[:, gs] = (_nn(pb, wp_ref[g].astype(BF)) * ps_ref[:, gs]).astype(BF)

    hb = ts // HALO
    return pl.pallas_call(
        body, name="pool_fwd", grid=(nb, nt),
        in_specs=[pl.BlockSpec((ts, POOL_WIDTH), lambda b, i: (b * nt + i, 0)),
                  pl.BlockSpec((HALO, POOL_WIDTH), lambda b, i: (jnp.maximum((b * nt + i) * hb - 1, 0), 0)),
                  pl.BlockSpec((4, POOL_GROUP, POOL_GROUP), lambda b, i: (0, 0, 0)),
                  pl.BlockSpec((1, POOL_WIDTH), lambda b, i: (0, 0))],
        out_specs=[pl.BlockSpec((ts, POOL_WIDTH), lambda b, i: (b * nt + i, 0))] * 2,
        out_shape=[jax.ShapeDtypeStruct((nb * seq, POOL_WIDTH), BF)] * 2,
        compiler_params=_cp(("parallel", "parallel"), 32))(proj, proj, w_pool, pool_scale)


def _chunk_recompute(uxbc, halo, udt, cw, cb, dtb, alog, first):
    halo = jnp.where(first, 0.0, halo)
    ext = jnp.concatenate([halo, uxbc], 0)
    pre = cb + uxbc * cw[3:4]
    for k in (2, 1, 0):
        pre = pre + pltpu.roll(ext, 3 - k, 0)[CONV_HALO:] * cw[k:k + 1]
    sg = _sigmoid(pre)
    xbc = pre * sg
    dtp = udt[:, :SSD_HEADS] + dtb
    dt = jnp.maximum(dtp, 0.0) + jnp.log(1.0 + jnp.exp(-jnp.abs(dtp)))
    a = -jnp.exp(alog)
    da = dt * a
    tril = (_iota((CHUNK, CHUNK), 0) >= _iota((CHUNK, CHUNK), 1))
    acum = _exact_nn_left(tril.astype(BF), da)
    eye = (_iota((SSD_HEADS, SSD_HEADS), 0) == _iota((SSD_HEADS, SSD_HEADS), 1)).astype(BF)
    acum_t = _exact_nt_left(eye, acum)
    expand = _head_expand_matrix(SSD_HEADS, SSD_INNER)
    acum_e = _exact_nn(acum, expand)
    dt_e = _exact_nn(dt, expand)
    last_e = acum_e[CHUNK - 1:CHUNK]
    return dict(pre=pre, sg=sg, xbc=xbc, dtp=dtp, dt=dt, a=a, acum=acum, acum_t=acum_t, tril=tril,
                dt_e=dt_e, e_a=jnp.exp(acum_e), d_out=jnp.exp(last_e - acum_e), c_dec=jnp.exp(last_e))


def _head_decay(r, h):
    seg = r["acum"][:, h:h + 1] - r["acum_t"][h:h + 1, :]
    return jnp.where(r["tril"], jnp.exp(jnp.minimum(seg, 0.0)), 0.0)


def _ssd_specs(nb, seq, reverse):
    nc = seq // CHUNK
    per = seq // CONV_HALO

    def cidx(c):
        return (nc - 1 - c) if reverse else c

    def row(b, c):
        return b * nc + cidx(c)

    specs = [
        pl.BlockSpec((CHUNK, CONV_CH), lambda b, c: (row(b, c), 1)),
        pl.BlockSpec((CONV_HALO, CONV_CH),
                     lambda b, c: (jnp.maximum(b * per + cidx(c) * (CHUNK // CONV_HALO) - 1, 0), 1)),
        pl.BlockSpec((CHUNK, GROUP_W), lambda b, c: (row(b, c), 1)),
        pl.BlockSpec((CHUNK, GROUP_W), lambda b, c: (row(b, c), 2)),
        pl.BlockSpec((CHUNK, 128), lambda b, c: (row(b, c), OFF_DT // 128)),
    ]
    return specs, row, cidx, nc


def _const_spec(shape):
    return pl.BlockSpec(shape, lambda b, c: (0,) * len(shape))


def _ssd_fwd(proj, conv_w, conv_b, dt_bias, a_log, dskip_e, g_ssd, nb, seq):
    specs, row, cidx, nc = _ssd_specs(nb, seq, reverse=False)

    def body(uxbc_ref, halo_ref, z0_ref, z1_ref, udt_ref, cw_ref, cb_ref, dtb_ref, alog_ref, dsk_ref, gs_ref,
             yssd_ref, yssm_ref, hprev_ref, h_ref, yd_ref):
        c = pl.program_id(1)

        @pl.when(c == 0)
        def _():
            h_ref[...] = jnp.zeros_like(h_ref)

        r = _chunk_recompute(uxbc_ref[...], halo_ref[...], udt_ref[...], cw_ref[...], cb_ref[...],
                             dtb_ref[...], alog_ref[...], c == 0)
        xbc = r["xbc"]
        xs = xbc[:, :SSD_INNER]
        xdt = xs * r["dt_e"]
        xdt_b = xdt.astype(BF)
        xdo_b = (xdt * r["d_out"]).astype(BF)
        hprev_ref[0, 0] = h_ref[...]
        for g in range(2):
            gs = slice(g * GROUP_W, (g + 1) * GROUP_W)
            bg = xbc[:, SSD_INNER + g * SSD_STATE:SSD_INNER + (g + 1) * SSD_STATE].astype(BF)
            cg = xbc[:, SSD_INNER + (2 + g) * SSD_STATE:SSD_INNER + (3 + g) * SSD_STATE].astype(BF)
            scores = _nt(cg, bg)
            hg = h_ref[g]
            y_off = _nn(cg, hg.astype(BF)) * r["e_a"][:, gs]
            for hh in range(8):
                h = g * 8 + hh
                m = (scores * _head_decay(r, h)).astype(BF)
                yd_ref[:, h * SSD_HEAD_DIM:(h + 1) * SSD_HEAD_DIM] = _nn(m, xdt_b[:, h * SSD_HEAD_DIM:(h + 1) * SSD_HEAD_DIM])
            h_ref[g] = hg * r["c_dec"][:, gs] + _tn(bg, xdo_b[:, gs])
            y = yd_ref[:, gs] + y_off + dsk_ref[:, gs] * xs[:, gs]
            yssm_ref[:, gs] = y
            zg = (z0_ref if g == 0 else z1_ref)[...]
            yg = y * (zg * _sigmoid(zg))
            rg = lax.rsqrt(jnp.mean(yg * yg, -1, keepdims=True) + EPS)
            yssd_ref[:, gs] = (yg * rg * gs_ref[:, gs]).astype(BF)

    t = nb * seq
    return pl.pallas_call(
        body, name="ssd_fwd", grid=(nb, nc),
        in_specs=specs + [_const_spec((4, CONV_CH)), _const_spec((1, CONV_CH)), _const_spec((1, SSD_HEADS)),
                          _const_spec((1, SSD_HEADS)), _const_spec((1, SSD_INNER)), _const_spec((1, SSD_INNER))],
        out_specs=[pl.BlockSpec((CHUNK, SSD_INNER), lambda b, c: (row(b, c), 0)),
                   pl.BlockSpec((CHUNK, SSD_INNER), lambda b, c: (row(b, c), 0)),
                   pl.BlockSpec((1, 1, 2, SSD_STATE, GROUP_W), lambda b, c: (b, c, 0, 0, 0))],
        out_shape=[jax.ShapeDtypeStruct((t, SSD_INNER), BF), jax.ShapeDtypeStruct((t, SSD_INNER), F32),
                   jax.ShapeDtypeStruct((nb, nc, 2, SSD_STATE, GROUP_W), F32)],
        scratch_shapes=[pltpu.VMEM((2, SSD_STATE, GROUP_W), F32), pltpu.VMEM((CHUNK, SSD_INNER), F32)],
        compiler_params=_cp(("arbitrary", "arbitrary"), 48),
    )(proj, proj, proj, proj, proj, conv_w, conv_b, dt_bias, a_log, dskip_e, g_ssd)


def _out_proj(y_pool, y_ssd, w_out, x, mod3, g_mlp, seq):
    t, d = x.shape
    tm = 512
    tps = seq // tm if seq >= tm else 1
    tm = min(tm, seq)

    def body(yp_ref, ys_ref, w_ref, x_ref, mod_ref, g_ref, h1_ref, o_ref, u2_ref):
        o = _nn(yp_ref[...], w_ref[0:POOL_WIDTH, :]) + _nn(ys_ref[...], w_ref[POOL_WIDTH:, :])
        o_ref[...] = o.astype(BF)
        h1 = x_ref[...] + mod_ref[0, 2:3, :] * o
        h1_ref[...] = h1
        r = lax.rsqrt(jnp.mean(h1 * h1, -1, keepdims=True) + EPS)
        u2_ref[...] = ((h1 * r * g_ref[...]) * (1.0 + mod_ref[0, 4:5, :]) + mod_ref[0, 3:4, :]).astype(BF)

    row = lambda i: (i, 0)
    return pl.pallas_call(
        body, name="out_proj", grid=(t // tm,),
        in_specs=[pl.BlockSpec((tm, POOL_WIDTH), row), pl.BlockSpec((tm, SSD_INNER), row),
                  pl.BlockSpec(w_out.shape, lambda i: (0, 0)), pl.BlockSpec((tm, d), row),
                  pl.BlockSpec((1, N_MOD, d), lambda i: (i // tps, 0, 0)), pl.BlockSpec((1, d), lambda i: (0, 0))],
        out_specs=[pl.BlockSpec((tm, d), row)] * 3,
        out_shape=[jax.ShapeDtypeStruct((t, d), F32), jax.ShapeDtypeStruct((t, d), BF), jax.ShapeDtypeStruct((t, d), BF)],
        compiler_params=_cp(("parallel",), 48))(y_pool, y_ssd, w_out, x, mod3, g_mlp)


def _mlp_up(u2, w_up4):
    t, d = u2.shape
    tm = min(512, t)
    nk, _, cols = w_up4.shape

    def body(u_ref, w_ref, a_ref):
        a_ref[...] = _nn(u_ref[...], w_ref[0]).astype(BF)

    return pl.pallas_call(
        body, name="mlp_up", grid=(nk, t // tm),
        in_specs=[pl.BlockSpec((tm, d), lambda k, i: (i, 0)), pl.BlockSpec((1, d, cols), lambda k, i: (k, 0, 0))],
        out_specs=pl.BlockSpec((tm, cols), lambda k, i: (i, k)),
        out_shape=jax.ShapeDtypeStruct((t, nk * cols), BF),
        compiler_params=_cp(("parallel", "parallel"), 32))(u2, w_up4)


def _mlp_down_loss(a_up, w_down, h1, mod3, g_final, target, seq):
    t, d = h1.shape
    nb = t // seq
    tm = min(256, seq)
    tps = seq // tm

    def body(a_ref, w_ref, h1_ref, mod_ref, g_ref, tg_ref, ddn_ref, dh2_ref, sq_ref, gg_ref, dgf_ref):
        i = pl.program_id(0)
        f = jnp.square(jnp.maximum(a_ref[...], 0))
        dn = _nn(f, w_ref[...])
        gate = mod_ref[0, 5:6, :]
        h2 = h1_ref[...] + gate * dn
        r = lax.rsqrt(jnp.mean(h2 * h2, -1, keepdims=True) + EPS)
        hh = h2 * r
        err = hh * g_ref[...] - tg_ref[...]
        dy = err * (1.0 / d)
        dhat = dy * g_ref[...]
        dh2 = r * (dhat - hh * jnp.mean(dhat * hh, -1, keepdims=True))
        dh2_ref[...] = dh2
        ddn_ref[...] = (dh2 * gate).astype(BF)

        @pl.when(i == 0)
        def _():
            sq_ref[...] = jnp.zeros_like(sq_ref)
            gg_ref[...] = jnp.zeros_like(gg_ref)

        @pl.when(i % tps == 0)
        def _():
            dgf_ref[...] = jnp.zeros_like(dgf_ref)

        sq_ref[...] += jnp.sum(err * err, 0, keepdims=True)
        gg_ref[...] += jnp.sum(dy * hh, 0, keepdims=True)
        dgf_ref[0] += jnp.sum(dh2 * dn, 0, keepdims=True)

    row = lambda i: (i, 0)
    vec = pl.BlockSpec((1, d), lambda i: (0, 0))
    return pl.pallas_call(
        body, name="mlp_down_loss", grid=(t // tm,),
        in_specs=[pl.BlockSpec((tm, D_FF), row), pl.BlockSpec(w_down.shape, lambda i: (0, 0)), pl.BlockSpec((tm, d), row),
                  pl.BlockSpec((1, N_MOD, d), lambda i: (i // tps, 0, 0)), vec, pl.BlockSpec((tm, d), row)],
        out_specs=[pl.BlockSpec((tm, d), row), pl.BlockSpec((tm, d), row), vec, vec,
                   pl.BlockSpec((1, 1, d), lambda i: (i // tps, 0, 0))],
        out_shape=[jax.ShapeDtypeStruct((t, d), BF), jax.ShapeDtypeStruct((t, d), F32), jax.ShapeDtypeStruct((1, d), F32),
                   jax.ShapeDtypeStruct((1, d), F32), jax.ShapeDtypeStruct((nb, 1, d), F32)],
        compiler_params=_cp(("arbitrary",), 56))(a_up, w_down, h1, mod3, g_final, target)


def _tn_matmul(a, b, tk, tn, name, square_relu=False, out3=False):
    t, kdim = a.shape
    ndim = b.shape[1]

    def body(a_ref, b_ref, o_ref):
        av = a_ref[...]
        if square_relu:
            av = jnp.square(jnp.maximum(av, 0))
        res = _tn(av, b_ref[...]).astype(BF)
        if out3:
            o_ref[0] = res
        else:
            o_ref[...] = res

    if out3:
        out_spec = pl.BlockSpec((1, tk, tn), lambda j, i: (j, i, 0))
        out_shape = jax.ShapeDtypeStruct((ndim // tn, kdim, tn), BF)
    else:
        out_spec = pl.BlockSpec((tk, tn), lambda j, i: (i, j))
        out_shape = jax.ShapeDtypeStruct((kdim, ndim), BF)
    return pl.pallas_call(
        body, name=name, grid=(ndim // tn, kdim // tk),
        in_specs=[pl.BlockSpec((t, tk), lambda j, i: (0, i)), pl.BlockSpec((t, tn), lambda j, i: (0, j))],
        out_specs=out_spec, out_shape=out_shape,
        compiler_params=_cp(("parallel", "parallel"), 56))(a, b)


def _mlp_down_bwd(d_dn, w_down4, a_up, token):
    t, d = d_dn.shape
    tm = min(512, t)
    nk, rows, _ = w_down4.shape

    def body(g_ref, w_ref, a_ref, tok_ref, o_ref):
        df = _nt(g_ref[...], w_ref[0])
        o_ref[...] = (df * (2.0 * jnp.maximum(a_ref[...], 0).astype(F32))).astype(BF)

    return pl.pallas_call(
        body, name="mlp_down_bwd", grid=(nk, t // tm),
        in_specs=[pl.BlockSpec((tm, d), lambda k, i: (i, 0)), pl.BlockSpec((1, rows, d), lambda k, i: (k, 0, 0)),
                  pl.BlockSpec((tm, rows), lambda k, i: (i, k)), _token_spec()],
        out_specs=pl.BlockSpec((tm, rows), lambda k, i: (i, k)),
        out_shape=jax.ShapeDtypeStruct((t, nk * rows), BF),
        compiler_params=_cp(("parallel", "parallel"), 32))(d_dn, w_down4, a_up, token)


def _mlp_up_bwd(d_a, w_up4, h1, dh2, o, mod3, g_mlp, seq, token):
    t, d = h1.shape
    nb = t // seq
    tm = min(256, seq)
    tps = seq // tm
    nk = w_up4.shape[0]
    cols = w_up4.shape[2]

    def body(da_ref, w_ref, h1_ref, dh2_ref, o_ref, mod_ref, g_ref, tok_ref, dh1_ref, do_ref, acc_ref, gg_ref):
        i = pl.program_id(0)
        du = _nt(da_ref[:, 0:cols], w_ref[0])
        for k in range(1, nk):
            du = du + _nt(da_ref[:, k * cols:(k + 1) * cols], w_ref[k])
        h1 = h1_ref[...]
        r = lax.rsqrt(jnp.mean(h1 * h1, -1, keepdims=True) + EPS)
        hh = h1 * r
        n2 = hh * g_ref[...]
        dn2 = du * (1.0 + mod_ref[0, 4:5, :])
        dhat = dn2 * g_ref[...]
        dh1 = dh2_ref[...] + r * (dhat - hh * jnp.mean(dhat * hh, -1, keepdims=True))
        dh1_ref[...] = dh1
        do_ref[...] = (dh1 * mod_ref[0, 2:3, :]).astype(BF)

        @pl.when(i == 0)
        def _():
            gg_ref[...] = jnp.zeros_like(gg_ref)

        @pl.when(i % tps == 0)
        def _():
            acc_ref[...] = jnp.zeros_like(acc_ref)

        gg_ref[...] += jnp.sum(dn2 * hh, 0, keepdims=True)
        acc_ref[0, 0:1, :] += jnp.sum(du, 0, keepdims=True)
        acc_ref[0, 1:2, :] += jnp.sum(du * n2, 0, keepdims=True)
        acc_ref[0, 2:3, :] += jnp.sum(dh1 * o_ref[...].astype(F32), 0, keepdims=True)

    row = lambda i: (i, 0)
    vec = pl.BlockSpec((1, d), lambda i: (0, 0))
    return pl.pallas_call(
        body, name="mlp_up_bwd", grid=(t // tm,),
        in_specs=[pl.BlockSpec((tm, D_FF), row), pl.BlockSpec(w_up4.shape, lambda i: (0, 0, 0)), pl.BlockSpec((tm, d), row),
                  pl.BlockSpec((tm, d), row), pl.BlockSpec((tm, d), row),
                  pl.BlockSpec((1, N_MOD, d), lambda i: (i // tps, 0, 0)), vec, _token_spec()],
        out_specs=[pl.BlockSpec((tm, d), row), pl.BlockSpec((tm, d), row),
                   pl.BlockSpec((1, 8, d), lambda i: (i // tps, 0, 0)), vec],
        out_shape=[jax.ShapeDtypeStruct((t, d), F32), jax.ShapeDtypeStruct((t, d), BF),
                   jax.ShapeDtypeStruct((nb, 8, d), F32), jax.ShapeDtypeStruct((1, d), F32)],
        compiler_params=_cp(("arbitrary",), 56))(d_a, w_up4, h1, dh2, o, mod3, g_mlp, token)


def _out_proj_bwd(d_o, w_out, token):
    t, d = d_o.shape
    tm = min(512, t)

    def body(g_ref, w_ref, tok_ref, dp_ref, ds_ref):
        gv = g_ref[...]
        dp_ref[...] = _nt(gv, w_ref[0:POOL_WIDTH, :])
        ds_ref[...] = _nt(gv, w_ref[POOL_WIDTH:, :])

    row = lambda i: (i, 0)
    return pl.pallas_call(
        body, name="out_proj_bwd", grid=(t // tm,),
        in_specs=[pl.BlockSpec((tm, d), row), pl.BlockSpec(w_out.shape, lambda i: (0, 0)), _token_spec()],
        out_specs=[pl.BlockSpec((tm, POOL_WIDTH), row), pl.BlockSpec((tm, SSD_INNER), row)],
        out_shape=[jax.ShapeDtypeStruct((t, POOL_WIDTH), F32), jax.ShapeDtypeStruct((t, SSD_INNER), F32)],
        compiler_params=_cp(("parallel",), 32))(d_o, w_out, token)


def _pool_bwd(d_ypool, p, w_pool, pool_scale, nb, seq):
    ts = _pool_tile(seq)
    nt = seq // ts
    hb = ts // HALO
    last_block = nb * seq // HALO - 1

    def body(dy_ref, halo_ref, p_ref, wp_ref, ps_ref, du_ref, gw_ref, gs_ref):
        b = pl.program_id(0)
        i = pl.program_id(1)

        @pl.when((b == 0) & (i == 0))
        def _():
            gw_ref[...] = jnp.zeros_like(gw_ref)
            gs_ref[...] = jnp.zeros_like(gs_ref)

        halo = jnp.where(i == nt - 1, 0.0, halo_ref[...])
        dy = dy_ref[...]
        ext = jnp.concatenate([dy, halo], 0)
        tpos = i * ts + _iota((ts + HALO, 1), 0)
        n_ext = ts + HALO
        for g, w in enumerate(POOL_WINDOWS):
            gs = slice(g * POOL_GROUP, (g + 1) * POOL_GROUP)
            wg = wp_ref[g].astype(BF)
            pg = p_ref[:, gs]
            pw = _nn(pg, wg)
            gs_ref[:, gs] += jnp.sum(dy[:, gs] * pw, 0, keepdims=True)
            dpw = (ext[:, gs] * ps_ref[:, gs]).astype(BF)
            gw_ref[g] += _tn(pg, dpw[:ts])
            dp = _nt(dpw, wg)
            cnt = jnp.minimum(tpos + 1, w).astype(F32)
            s = dp / cnt
            sh = 1
            while sh < w:
                s = s + pltpu.roll(s, n_ext - sh, 0)
                sh *= 2
            du_ref[:, gs] = (s[:ts] - dp[:ts]).astype(BF)

    return pl.pallas_call(
        body, name="pool_bwd", grid=(nb, nt),
        in_specs=[pl.BlockSpec((ts, POOL_WIDTH), lambda b, i: (b * nt + i, 0)),
                  pl.BlockSpec((HALO, POOL_WIDTH), lambda b, i: (jnp.minimum((b * nt + i + 1) * hb, last_block), 0)),
                  pl.BlockSpec((ts, POOL_WIDTH), lambda b, i: (b * nt + i, 0)),
                  pl.BlockSpec((4, POOL_GROUP, POOL_GROUP), lambda b, i: (0, 0, 0)),
                  pl.BlockSpec((1, POOL_WIDTH), lambda b, i: (0, 0))],
        out_specs=[pl.BlockSpec((ts, POOL_WIDTH), lambda b, i: (b * nt + i, 0)),
                   pl.BlockSpec((4, POOL_GROUP, POOL_GROUP), lambda b, i: (0, 0, 0)),
                   pl.BlockSpec((1, POOL_WIDTH), lambda b, i: (0, 0))],
        out_shape=[jax.ShapeDtypeStruct((nb * seq, POOL_WIDTH), BF), jax.ShapeDtypeStruct((4, POOL_GROUP, POOL_GROUP), F32),
                   jax.ShapeDtypeStruct((1, POOL_WIDTH), F32)],
        compiler_params=_cp(("arbitrary", "arbitrary"), 32))(d_ypool, d_ypool, p, w_pool, pool_scale)


def _ssd_bwd(proj, d_yssd, yssm, h_prev, conv_w, conv_b, dt_bias, a_log, dskip_e, g_ssd, nb, seq):
    specs, row, cidx, nc = _ssd_specs(nb, seq, reverse=True)

    def body(uxbc_ref, halo_ref, z0_ref, z1_ref, udt_ref, dys_ref, yssm_ref, hprev_ref,
             cw_ref, cb_ref, dtb_ref, alog_ref, dsk_ref, gs_ref,
             dz_ref, dpre_ref, dudt_ref, ggs_ref, gdsk_ref, ga_ref, gdtb_ref,
             g_ref, dxdt_ref, dyv_ref):
        b = pl.program_id(0)
        c = pl.program_id(1)

        @pl.when(c == 0)
        def _():
            g_ref[...] = jnp.zeros_like(g_ref)

        @pl.when((b == 0) & (c == 0))
        def _():
            ggs_ref[...] = jnp.zeros_like(ggs_ref)
            gdsk_ref[...] = jnp.zeros_like(gdsk_ref)
            ga_ref[...] = jnp.zeros_like(ga_ref)
            gdtb_ref[...] = jnp.zeros_like(gdtb_ref)

        r = _chunk_recompute(uxbc_ref[...], halo_ref[...], udt_ref[...], cw_ref[...], cb_ref[...],
                             dtb_ref[...], alog_ref[...], c == nc - 1)
        xbc = r["xbc"]
        xs = xbc[:, :SSD_INNER]
        dt_e = r["dt_e"]
        xdt = xs * dt_e
        xdt_b = xdt.astype(BF)
        reduce_m = _head_reduce_matrix(GROUP_W, 8)
        onehot16 = lambda h: (_iota((1, SSD_HEADS), 1) == h).astype(F32)
        onecol16 = lambda h: (_iota((SSD_HEADS, 1), 0) == h).astype(F32)

        d_acum = jnp.zeros((CHUNK, SSD_HEADS), F32)
        d_acum_t = jnp.zeros((SSD_HEADS, CHUNK), F32)
        d_alast = jnp.zeros((1, SSD_HEADS), F32)
        place8 = lambda g: (_iota((8, SSD_HEADS), 1) == _iota((8, SSD_HEADS), 0) + 8 * g).astype(BF)
        d_b, d_c = [], []
        for g in range(2):
            gs = slice(g * GROUP_W, (g + 1) * GROUP_W)
            zg = (z0_ref if g == 0 else z1_ref)[...]
            sz = _sigmoid(zg)
            silu_z = zg * sz
            ys = yssm_ref[:, gs]
            yg = ys * silu_z
            rg = lax.rsqrt(jnp.mean(yg * yg, -1, keepdims=True) + EPS)
            yh = yg * rg
            dys = dys_ref[:, gs]
            ggs_ref[:, gs] += jnp.sum(dys * yh, 0, keepdims=True)
            dyh = dys * gs_ref[:, gs]
            dyg = rg * (dyh - yh * jnp.mean(dyh * yh, -1, keepdims=True))
            dy = dyg * silu_z
            dz_ref[:, gs] = (dyg * ys * (sz * (1.0 + zg * (1.0 - sz)))).astype(BF)
            gdsk_ref[:, gs] += jnp.sum(dy * xs[:, gs], 0, keepdims=True)
            dyv_ref[:, gs] = dy
            dy_b = dy.astype(BF)

            bg = xbc[:, SSD_INNER + g * SSD_STATE:SSD_INNER + (g + 1) * SSD_STATE].astype(BF)
            cg = xbc[:, SSD_INNER + (2 + g) * SSD_STATE:SSD_INNER + (3 + g) * SSD_STATE].astype(BF)
            scores = _nt(cg, bg)
            hg = hprev_ref[0, 0, g]
            hg_b = hg.astype(BF)
            gg = g_ref[g]
            gg_b = gg.astype(BF)
            e_a = r["e_a"][:, gs]
            d_out = r["d_out"][:, gs]
            c_dec = r["c_dec"][:, gs]
            zc = _nn(cg, hg_b)
            wv = e_a * dy
            wv_b = wv.astype(BF)
            da_g = _exact_nn(wv * zc, reduce_m)
            dcg = _nt(wv_b, hg_b)
            d_hprev = _tn(cg, wv_b)
            vg = _nn(bg, gg_b)
            dxdt_g = d_out * vg
            dd_out = _exact_nn(xdt[:, gs] * vg, reduce_m)
            dbg = _nt((xdt[:, gs] * d_out).astype(BF), gg_b)
            dcd = _exact_nn(jnp.sum(gg * hg, 0, keepdims=True), reduce_m)
            d_out8 = jnp.exp(r["acum"][CHUNK - 1:CHUNK, 8 * g:8 * g + 8] - r["acum"][:, 8 * g:8 * g + 8])
            c_dec8 = jnp.exp(r["acum"][CHUNK - 1:CHUNK, 8 * g:8 * g + 8])
            t8 = dd_out * d_out8
            d_alast = d_alast + _exact_nn(jnp.sum(t8, 0, keepdims=True) + dcd * c_dec8, place8(g))
            d_acum = d_acum + _exact_nn(da_g - t8, place8(g))
            dsc = jnp.zeros((CHUNK, CHUNK), F32)
            for hh in range(8):
                h = g * 8 + hh
                hs = slice(h * SSD_HEAD_DIM, (h + 1) * SSD_HEAD_DIM)
                lam = _head_decay(r, h)
                m = scores * lam
                dyh_b = dy_b[:, hh * SSD_HEAD_DIM:(hh + 1) * SSD_HEAD_DIM]
                dm = _nt(dyh_b, xdt_b[:, hs])
                tm_ = dm * m
                d_acum = d_acum + jnp.sum(tm_, 1, keepdims=True) * onehot16(h)
                d_acum_t = d_acum_t + onecol16(h) * jnp.sum(tm_, 0, keepdims=True)
                dsc = dsc + dm * lam
                dxdt_ref[:, hs] = _tn(m.astype(BF), dyh_b) + dxdt_g[:, hh * SSD_HEAD_DIM:(hh + 1) * SSD_HEAD_DIM]
            dsc_b = dsc.astype(BF)
            d_c.append(dcg + _nn(dsc_b, bg))
            d_b.append(dbg + _tn(dsc_b, cg))
            g_ref[g] = d_hprev + c_dec * gg

        eye = (_iota((CHUNK, CHUNK), 0) == _iota((CHUNK, CHUNK), 1)).astype(BF)
        d_acum = d_acum - _exact_nt_left(eye, d_acum_t)
        is_last = (_iota((CHUNK, 1), 0) == CHUNK - 1).astype(F32)
        d_acum = d_acum + is_last * d_alast
        triu = (_iota((CHUNK, CHUNK), 0) <= _iota((CHUNK, CHUNK), 1)).astype(BF)
        d_da = _exact_nn_left(triu, d_acum)
        dt = r["dt"]
        ga_ref[...] += jnp.sum(d_da * dt, 0, keepdims=True)
        dxdt = dxdt_ref[...]
        reduce16 = _head_reduce_matrix(SSD_INNER, SSD_HEADS)
        d_dt = d_da * r["a"] + _exact_nn(dxdt * xs, reduce16)
        d_udt = d_dt * _sigmoid(r["dtp"])
        gdtb_ref[...] += jnp.sum(d_udt, 0, keepdims=True)
        dudt_ref[...] = jnp.zeros_like(dudt_ref)
        dudt_ref[:, 0:SSD_HEADS] = d_udt.astype(BF)
        pre, sg = r["pre"], r["sg"]
        dsilu = sg * (1.0 + pre * (1.0 - sg))
        dpre_ref[:, 0:SSD_INNER] = (dsk_ref[...] * dyv_ref[...] + dxdt * dt_e) * dsilu[:, 0:SSD_INNER]
        for g in range(2):
            bs = slice(SSD_INNER + g * SSD_STATE, SSD_INNER + (g + 1) * SSD_STATE)
            cs = slice(SSD_INNER + (2 + g) * SSD_STATE, SSD_INNER + (3 + g) * SSD_STATE)
            dpre_ref[:, bs] = d_b[g] * dsilu[:, bs]
            dpre_ref[:, cs] = d_c[g] * dsilu[:, cs]

    t = nb * seq
    vec = _const_spec((1, SSD_INNER))
    small = _const_spec((1, SSD_HEADS))
    return pl.pallas_call(
        body, name="ssd_bwd", grid=(nb, nc),
        in_specs=specs + [pl.BlockSpec((CHUNK, SSD_INNER), lambda b, c: (row(b, c), 0)),
                          pl.BlockSpec((CHUNK, SSD_INNER), lambda b, c: (row(b, c), 0)),
                          pl.BlockSpec((1, 1, 2, SSD_STATE, GROUP_W), lambda b, c: (b, cidx(c), 0, 0, 0)),
                          _const_spec((4, CONV_CH)), _const_spec((1, CONV_CH)), small, small, vec, vec],
        out_specs=[pl.BlockSpec((CHUNK, SSD_INNER), lambda b, c: (row(b, c), 0)),
                   pl.BlockSpec((CHUNK, CONV_CH), lambda b, c: (row(b, c), 0)),
                   pl.BlockSpec((CHUNK, 128), lambda b, c: (row(b, c), 0)),
                   vec, vec, small, small],
        out_shape=[jax.ShapeDtypeStruct((t, SSD_INNER), BF), jax.ShapeDtypeStruct((t, CONV_CH), F32),
                   jax.ShapeDtypeStruct((t, 128), BF), jax.ShapeDtypeStruct((1, SSD_INNER), F32),
                   jax.ShapeDtypeStruct((1, SSD_INNER), F32), jax.ShapeDtypeStruct((1, SSD_HEADS), F32),
                   jax.ShapeDtypeStruct((1, SSD_HEADS), F32)],
        scratch_shapes=[pltpu.VMEM((2, SSD_STATE, GROUP_W), F32), pltpu.VMEM((CHUNK, SSD_INNER), F32),
                        pltpu.VMEM((CHUNK, SSD_INNER), F32)],
        compiler_params=_cp(("arbitrary", "arbitrary"), 48),
    )(proj, proj, proj, proj, proj, d_yssd, yssm, h_prev, conv_w, conv_b, dt_bias, a_log, dskip_e, g_ssd)


def _grad_w_in_t(d_upool, d_z, d_uxbc, d_udt, u1):
    t, d = u1.shape
    tk = 512
    n_z, n_x = SSD_INNER // tk, CONV_CH // tk

    def body(p_ref, z_ref, x_ref, dt_ref, u_ref, o_ref):
        i = pl.program_id(0)

        @pl.when(i == 0)
        def _():
            o_ref[...] = _tn(p_ref[...], u_ref[...]).astype(BF)

        @pl.when((i >= 1) & (i < 1 + n_z))
        def _():
            o_ref[...] = _tn(z_ref[...], u_ref[...]).astype(BF)

        @pl.when((i >= 1 + n_z) & (i < 1 + n_z + n_x))
        def _():
            o_ref[...] = _tn(x_ref[...], u_ref[...]).astype(BF)

        @pl.when(i == 1 + n_z + n_x)
        def _():
            o_ref[0:128, :] = _tn(dt_ref[...], u_ref[...]).astype(BF)

    return pl.pallas_call(
        body, name="grad_w_in", grid=(2 + n_z + n_x,),
        in_specs=[pl.BlockSpec((t, tk), lambda i: (0, 0)),
                  pl.BlockSpec((t, tk), lambda i: (0, jnp.clip(i - 1, 0, n_z - 1))),
                  pl.BlockSpec((t, tk), lambda i: (0, jnp.clip(i - 1 - n_z, 0, n_x - 1))),
                  pl.BlockSpec((t, 128), lambda i: (0, 0)), pl.BlockSpec((t, d), lambda i: (0, 0))],
        out_specs=pl.BlockSpec((tk, d), lambda i: (i, 0)),
        out_shape=jax.ShapeDtypeStruct((IN_PAD, d), BF),
        compiler_params=_cp(("parallel",), 56))(d_upool, d_z, d_uxbc, d_udt, u1)


def _conv_bwd(d_pre, proj, conv_w, nb, seq):
    ts = _pool_tile(seq)
    nt = seq // ts
    hb = ts // CONV_HALO
    last_block = nb * seq // CONV_HALO - 1
    n_ext = ts + CONV_HALO

    def body(dp_ref, dnext_ref, u_ref, uprev_ref, cw_ref, du_ref, gw_ref, gb_ref):
        b = pl.program_id(0)
        i = pl.program_id(1)

        @pl.when((b == 0) & (i == 0))
        def _():
            gw_ref[...] = jnp.zeros_like(gw_ref)
            gb_ref[...] = jnp.zeros_like(gb_ref)

        dp = dp_ref[...]
        ext_d = jnp.concatenate([dp, jnp.where(i == nt - 1, 0.0, dnext_ref[...])], 0)
        ext_u = jnp.concatenate([jnp.where(i == 0, 0.0, uprev_ref[...]), u_ref[...]], 0)
        cw = cw_ref[...]
        du = dp * cw[3:4]
        gw_ref[3:4, :] += jnp.sum(dp * u_ref[...], 0, keepdims=True)
        for k in (2, 1, 0):
            j = 3 - k
            du = du + pltpu.roll(ext_d, n_ext - j, 0)[:ts] * cw[k:k + 1]
            gw_ref[k:k + 1, :] += jnp.sum(dp * pltpu.roll(ext_u, j, 0)[CONV_HALO:], 0, keepdims=True)
        gb_ref[...] += jnp.sum(dp, 0, keepdims=True)
        du_ref[...] = du.astype(BF)

    return pl.pallas_call(
        body, name="conv_bwd", grid=(nb, nt),
        in_specs=[pl.BlockSpec((ts, CONV_CH), lambda b, i: (b * nt + i, 0)),
                  pl.BlockSpec((CONV_HALO, CONV_CH), lambda b, i: (jnp.minimum((b * nt + i + 1) * hb, last_block), 0)),
                  pl.BlockSpec((ts, CONV_CH), lambda b, i: (b * nt + i, 1)),
                  pl.BlockSpec((CONV_HALO, CONV_CH), lambda b, i: (jnp.maximum((b * nt + i) * hb - 1, 0), 1)),
                  pl.BlockSpec((4, CONV_CH), lambda b, i: (0, 0))],
        out_specs=[pl.BlockSpec((ts, CONV_CH), lambda b, i: (b * nt + i, 0)),
                   pl.BlockSpec((8, CONV_CH), lambda b, i: (0, 0)), pl.BlockSpec((1, CONV_CH), lambda b, i: (0, 0))],
        out_shape=[jax.ShapeDtypeStruct((nb * seq, CONV_CH), BF), jax.ShapeDtypeStruct((8, CONV_CH), F32),
                   jax.ShapeDtypeStruct((1, CONV_CH), F32)],
        compiler_params=_cp(("arbitrary", "arbitrary"), 48))(d_pre, d_pre, proj, proj, conv_w)


def _in_proj_bwd(d_parts, w_in_t, x, dh1, mod3, g_mix, seq, token):
    t, d = x.shape
    nb = t // seq
    tm = min(256, seq)
    tps = seq // tm

    widths = [p.shape[1] for p in d_parts]

    def body(d0_ref, d1_ref, d2_ref, d3_ref, w_ref, x_ref, dh1_ref, mod_ref, g_ref, tok_ref, gx_ref, acc_ref, gg_ref):
        i = pl.program_id(0)
        du = None
        off = 0
        for p_ref, wd in zip((d0_ref, d1_ref, d2_ref, d3_ref), widths):
            part = _nn(p_ref[...], w_ref[off:off + wd, :])
            du = part if du is None else du + part
            off += wd
        xv = x_ref[...]
        r = lax.rsqrt(jnp.mean(xv * xv, -1, keepdims=True) + EPS)
        hh = xv * r
        n1 = hh * g_ref[...]
        dn1 = du * (1.0 + mod_ref[0, 1:2, :])
        dhat = dn1 * g_ref[...]
        gx_ref[...] = dh1_ref[...] + r * (dhat - hh * jnp.mean(dhat * hh, -1, keepdims=True))

        @pl.when(i == 0)
        def _():
            gg_ref[...] = jnp.zeros_like(gg_ref)

        @pl.when(i % tps == 0)
        def _():
            acc_ref[...] = jnp.zeros_like(acc_ref)

        gg_ref[...] += jnp.sum(dn1 * hh, 0, keepdims=True)
        acc_ref[0, 0:1, :] += jnp.sum(du, 0, keepdims=True)
        acc_ref[0, 1:2, :] += jnp.sum(du * n1, 0, keepdims=True)

    row = lambda i: (i, 0)
    vec = pl.BlockSpec((1, d), lambda i: (0, 0))
    return pl.pallas_call(
        body, name="in_proj_bwd", grid=(t // tm,),
        in_specs=[pl.BlockSpec((tm, wd), row) for wd in widths] +
                 [pl.BlockSpec(w_in_t.shape, lambda i: (0, 0)), pl.BlockSpec((tm, d), row),
                  pl.BlockSpec((tm, d), row), pl.BlockSpec((1, N_MOD, d), lambda i: (i // tps, 0, 0)), vec, _token_spec()],
        out_specs=[pl.BlockSpec((tm, d), row), pl.BlockSpec((1, 8, d), lambda i: (i // tps, 0, 0)), vec],
        out_shape=[jax.ShapeDtypeStruct((t, d), F32), jax.ShapeDtypeStruct((nb, 8, d), F32),
                   jax.ShapeDtypeStruct((1, d), F32)],
        compiler_params=_cp(("arbitrary",), 48))(*d_parts, w_in_t, x, dh1, mod3, g_mix, token)


_VEC_LAYOUT = (("g_mix", 1024), ("conv_b", 1536), ("g_ssd", 1024), ("pool_scale", 512), ("g_mlp", 1024),
               ("g_final", 1024), ("dt_bias", 128), ("a_log", 128), ("d_skip_lanes", 1024), ("sq_err", 1024))
_VEC_OFFSET = {}
_off = 0
for _name, _n in _VEC_LAYOUT:
    _VEC_OFFSET[_name] = _off
    _off += _n
_VEC_LANES = _off
_SMALL_PARAMS = ("b_ada", "g_mix", "conv_w", "conv_b", "dt_bias", "a_log", "d_skip", "g_ssd", "w_pool", "pool_scale",
                 "g_mlp", "g_final")


def _pack_vec(parts):
    cols = []
    for name, n in _VEC_LAYOUT:
        v = parts[name]
        if v.shape[1] < n:
            v = jnp.pad(v, ((0, 0), (0, n - v.shape[1])))
        cols.append(v)
    return jnp.concatenate(cols, 1)


def _small_adam(vec_all, wpool_all, convw_all, dmod_all, params):
    names = _SMALL_PARAMS
    nin = 4 + 3 * len(names)

    def body(*refs):
        vec_ref, wp_ref, cw_ref, dm_ref = refs[:4]
        prm = {n: refs[4 + 3 * i:7 + 3 * i] for i, n in enumerate(names)}
        loss_ref = refs[nin]
        outs = {n: refs[nin + 1 + 4 * i:nin + 5 + 4 * i] for i, n in enumerate(names)}
        vsum = vec_ref[0]
        for s in range(1, N_DEV):
            vsum = vsum + vec_ref[s]

        def lanes(name, n):
            off = _VEC_OFFSET[name]
            return vsum[:, off:off + n]

        grads = {n: lanes(n, prm[n][0].shape[1]) for n in ("g_mix", "conv_b", "g_ssd", "pool_scale", "g_mlp", "g_final", "dt_bias")}
        grads["a_log"] = lanes("a_log", SSD_HEADS) * (-jnp.exp(prm["a_log"][0][...]))
        per_lane = jnp.broadcast_to(lanes("d_skip_lanes", SSD_INNER), (8, SSD_INNER))
        grads["d_skip"] = _exact_nn(per_lane, _head_reduce_matrix(SSD_INNER, SSD_HEADS))[0:1]
        gwp = wp_ref[0].astype(F32)
        gcw = cw_ref[0]
        gb = jnp.sum(dm_ref[0], 0, keepdims=True)
        for s in range(1, N_DEV):
            gwp = gwp + wp_ref[s].astype(F32)
            gcw = gcw + cw_ref[s]
            gb = gb + jnp.sum(dm_ref[s], 0, keepdims=True)
        grads["w_pool"] = gwp
        grads["conv_w"] = gcw[0:4]
        grads["b_ada"] = gb
        total = jnp.sum(lanes("sq_err", D_MODEL), 1, keepdims=True) * (0.5 / D_MODEL)
        loss_ref[...] = jnp.broadcast_to(total, loss_ref.shape)
        for n in names:
            w_ref, m_ref, v_ref = prm[n]
            g = grads[n]
            d, m2, v2 = _adam_math(w_ref[...], g, m_ref[...], v_ref[...])
            g_ref, d_ref, m2_ref, v2_ref = outs[n]
            g_ref[...] = g
            d_ref[...] = d
            m2_ref[...] = m2
            v2_ref[...] = v2

    flat = [vec_all, wpool_all, convw_all, dmod_all]
    out_shape = [jax.ShapeDtypeStruct((1, 128), F32)]
    for n in names:
        flat += list(params[n])
        out_shape += [jax.ShapeDtypeStruct(params[n][0].shape, F32)] * 4
    vm = pl.BlockSpec(memory_space=pltpu.VMEM)
    res = pl.pallas_call(body, name="small_adam", out_shape=out_shape, in_specs=[vm] * len(flat),
                         out_specs=[vm] * len(out_shape), compiler_params=_cp(vmem_mb=48))(*flat)
    return res[0], {n: res[1 + 4 * i:5 + 4 * i] for i, n in enumerate(names)}


_WEIGHTS = ("w_ada", "b_ada", "g_mix", "w_in", "conv_w", "conv_b", "dt_bias", "a_log", "d_skip", "g_ssd", "w_pool",
            "pool_scale", "w_out", "g_mlp", "w_up", "w_down", "g_final")


def _local_step(x2, tg2, mod3, seq, w_in_t, first_token, weights_later, start_reduce, conv_w_full, sp):
    t, d = x2.shape
    nb = t // seq
    dskip_e = jnp.repeat(sp["d_skip"], SSD_HEAD_DIM, axis=1)
    proj, u1 = _in_proj(x2, mod3, sp["g_mix"], w_in_t, seq, first_token)
    y_pool, p = _pool_fwd(proj, sp["w_pool"], sp["pool_scale"], nb, seq)
    y_ssd, yssm, h_prev = _ssd_fwd(proj, conv_w_full, sp["conv_b"], sp["dt_bias"], sp["a_log"], dskip_e, sp["g_ssd"], nb, seq)
    w_out_f, w_up4, w_down4 = weights_later(y_ssd)
    w_down_f = w_down4.reshape(D_FF, d)
    h1, o, u2 = _out_proj(y_pool, y_ssd, w_out_f, x2, mod3, sp["g_mlp"], seq)
    a_up = _mlp_up(u2, w_up4)
    d_dn, dh2, sq, gg_final, d_gf = _mlp_down_loss(a_up, w_down_f, h1, mod3, sp["g_final"], tg2, seq)

    gw_down = _tn_matmul(a_up, d_dn, 512, d, "grad_w_down", square_relu=True)
    tok = start_reduce("w_down", gw_down.reshape(N_CHIPS, D_FF // N_CHIPS, d))
    d_a = _mlp_down_bwd(d_dn, w_down4, a_up, tok)
    gw_up4 = _tn_matmul(u2, d_a, 512, d, "grad_w_up", out3=True)
    tok = start_reduce("w_up", gw_up4)
    dh1, d_o, accf, gg_mlp = _mlp_up_bwd(d_a, w_up4, h1, dh2, o, mod3, sp["g_mlp"], seq, tok)
    gw_out_pool = _tn_matmul(y_pool, d_o, 512, d, "grad_w_out_pool")
    gw_out_ssd = _tn_matmul(y_ssd, d_o, 512, d, "grad_w_out_ssd")
    gw_out = jnp.concatenate([gw_out_pool, gw_out_ssd], 0)
    tok = start_reduce("w_out", gw_out.reshape(N_CHIPS, gw_out.shape[0] // N_CHIPS, d))
    d_ypool, d_yssd = _out_proj_bwd(d_o, w_out_f, tok)
    d_upool, gw_pool, g_ps = _pool_bwd(d_ypool, p, sp["w_pool"], sp["pool_scale"], nb, seq)
    d_z, d_pre, d_udt, gg_ssd, gdsk, ga, gdtb = _ssd_bwd(proj, d_yssd, yssm, h_prev, conv_w_full, sp["conv_b"],
                                                        sp["dt_bias"], sp["a_log"], dskip_e, sp["g_ssd"], nb, seq)
    d_uxbc, gconvw, gconvb = _conv_bwd(d_pre, proj, conv_w_full, nb, seq)
    gw_in_t = _grad_w_in_t(d_upool, d_z, d_uxbc, d_udt, u1)
    tok = start_reduce("w_in", gw_in_t[:IN_WIDTH].reshape(N_CHIPS, IN_WIDTH // N_CHIPS, d))
    gx, accm, gg_mix = _in_proj_bwd([d_upool, d_z, d_uxbc, d_udt], w_in_t, x2, dh1, mod3, sp["g_mix"], seq, tok)

    d_mod = jnp.concatenate([accm[:, 0], accm[:, 1], accf[:, 2], accf[:, 0], accf[:, 1], d_gf[:, 0]], 1)
    vec = _pack_vec({"g_mix": gg_mix, "conv_b": gconvb, "g_ssd": gg_ssd, "pool_scale": g_ps, "g_mlp": gg_mlp,
                     "g_final": gg_final, "dt_bias": gdtb, "a_log": ga, "d_skip_lanes": gdsk, "sq_err": sq})
    return gx, d_mod, vec, gw_pool, gconvw


def kernel(x, c, w_ada, b_ada, g_mix, w_in, conv_w, conv_b, dt_bias, a_log, d_skip, g_ssd, w_pool, pool_scale, w_out, g_mlp, w_up, w_down, g_final, loss_target, m_w_ada, m_b_ada, m_g_mix, m_w_in, m_conv_w, m_conv_b, m_dt_bias, m_a_log, m_d_skip, m_g_ssd, m_w_pool, m_pool_scale, m_w_out, m_g_mlp, m_w_up, m_w_down, m_g_final, v_w_ada, v_b_ada, v_g_mix, v_w_in, v_conv_w, v_conv_b, v_dt_bias, v_a_log, v_d_skip, v_g_ssd, v_w_pool, v_pool_scale, v_w_out, v_g_mlp, v_w_up, v_w_down, v_g_final):
    nb, seq, d = x.shape
    t = nb * seq
    xi, yi, ci = _mesh_pos()
    chip = 2 * xi + yi
    me = 4 * xi + 2 * yi + ci
    ada_cols = w_ada.shape[2]
    conv_cols = conv_w.shape[2]
    in_cols = w_in.shape[2]
    w_in_s, m_w_in_s, v_w_in_s = w_in[0].T, m_w_in[0].T, v_w_in[0].T

    c8, convw8 = _all_gather_small([c, conv_w[0]])
    c_all = c8.reshape(N_DEV * nb, d)
    conv_w_full = convw8[0::2].transpose(1, 0, 2).reshape(4, N_CHIPS * conv_cols)
    b_shard = lax.dynamic_slice(b_ada, (0, chip * ada_cols), (1, ada_cols))
    mod_part, c_act = _ada_mod(c_all, w_ada[0], b_shard)
    (mod8,) = _all_gather_small([mod_part])
    mod_all = mod8[0::2].transpose(1, 0, 2).reshape(N_DEV * nb, N_CHIPS * ada_cols)
    mod3 = lax.dynamic_slice(mod_all, (nb * me, 0), (nb, N_CHIPS * ada_cols)).reshape(nb, N_MOD, d)

    (w_in4,) = _weight_all_gather([w_in_s.astype(BF)])
    w_in_t = jnp.pad(w_in4.reshape(N_CHIPS * in_cols, d), ((0, IN_PAD - N_CHIPS * in_cols), (0, 0)))
    later = [w_out[0].astype(BF), w_up[0].astype(BF), w_down[0].astype(BF)]
    g_send, g_recv, g_src, g_land, first_token = _ici_start(
        later, [jax.ShapeDtypeStruct((N_CHIPS,) + s.shape, BF) for s in later], _gather_sent, _gather_landing, "gather_start")

    def weights_later(after):
        shards, lands = _ici_wait(g_send, g_recv, g_src, g_land, after, _gather_sent, _gather_landing, "gather_wait")
        w_out4, w_up4, w_down4 = _gather_finish(lands, shards)
        return w_out4.reshape(N_CHIPS * w_out.shape[1], d), w_up4, w_down4

    pending = {}

    def start_reduce(name, grad4):
        own, got = _grad_to_sibling([grad4])
        part = _add_pair(own[0], got[0])
        pending[name] = _ici_start([part], [part], _reduce_sent, _reduce_landing, "reduce_start_" + name)
        return pending[name][4]

    sp = dict(g_mix=g_mix, conv_b=conv_b, dt_bias=dt_bias, a_log=a_log, d_skip=d_skip, g_ssd=g_ssd,
              w_pool=w_pool[0], pool_scale=pool_scale, g_mlp=g_mlp, g_final=g_final.reshape(1, d))
    gx, d_mod, vec, gw_pool, gconvw = _local_step(
        x.reshape(t, d), loss_target.reshape(t, d), mod3, seq, w_in_t, first_token, weights_later, start_reduce, conv_w_full, sp)

    halves = []
    for name in ("w_in", "w_out", "w_up", "w_down"):
        r_send, r_recv, r_src, r_land, _ = pending[name]
        own, recv = _ici_wait(r_send, r_recv, r_src, r_land, gx, _reduce_sent, _reduce_landing, "reduce_wait_" + name)
        halves.append(_sum_slots(recv[0], own[0]))
    g_in, g_out, g_up, g_down = _halves_exchange(halves)

    vec8, wpool8, convw8g, dmod8 = _all_gather_small(
        [vec, gw_pool.reshape(4 * POOL_GROUP, POOL_GROUP).astype(BF), gconvw, d_mod])
    convw8s = lax.dynamic_slice(convw8g, (0, 0, chip * conv_cols), (N_DEV, 8, conv_cols))
    m_in = dict(b_ada=m_b_ada, g_mix=m_g_mix, conv_w=m_conv_w[0], conv_b=m_conv_b, dt_bias=m_dt_bias, a_log=m_a_log,
                d_skip=m_d_skip, g_ssd=m_g_ssd, w_pool=m_w_pool.reshape(4 * POOL_GROUP, POOL_GROUP), pool_scale=m_pool_scale,
                g_mlp=m_g_mlp, g_final=m_g_final.reshape(1, d))
    v_in = dict(b_ada=v_b_ada, g_mix=v_g_mix, conv_w=v_conv_w[0], conv_b=v_conv_b, dt_bias=v_dt_bias, a_log=v_a_log,
                d_skip=v_d_skip, g_ssd=v_g_ssd, w_pool=v_w_pool.reshape(4 * POOL_GROUP, POOL_GROUP), pool_scale=v_pool_scale,
                g_mlp=v_g_mlp, g_final=v_g_final.reshape(1, d))
    w_small = dict(sp, b_ada=b_ada, conv_w=conv_w[0], w_pool=w_pool.reshape(4 * POOL_GROUP, POOL_GROUP))
    loss_row, small = _small_adam(vec8, wpool8, convw8s, dmod8, {n: (w_small[n], m_in[n], v_in[n]) for n in _SMALL_PARAMS})

    dmod_all = dmod8.reshape(N_DEV * nb, N_CHIPS * ada_cols)
    dmod_cols = lax.dynamic_slice(dmod_all, (0, chip * ada_cols), (N_DEV * nb, ada_cols))
    res = {n: tuple(r.reshape(w.shape) for r in small[n])
           for n, w in (("b_ada", b_ada), ("g_mix", g_mix), ("conv_w", conv_w), ("conv_b", conv_b), ("dt_bias", dt_bias),
                        ("a_log", a_log), ("d_skip", d_skip), ("g_ssd", g_ssd), ("w_pool", w_pool), ("pool_scale", pool_scale),
                        ("g_mlp", g_mlp), ("g_final", g_final))}
    g_ada, d_ada, m_ada, v_ada = _adam_ada(c_act.T.astype(BF), dmod_cols, w_ada[0], m_w_ada[0], v_w_ada[0])
    res["w_ada"] = (g_ada[None], d_ada[None], m_ada[None], v_ada[None])
    dl, m2, v2 = _adam_big(g_in, w_in_s, m_w_in_s, v_w_in_s)
    res["w_in"] = (g_in.T[None], dl.T[None], m2.T[None], v2.T[None])
    for n, g, w, m, v in (("w_out", g_out, w_out, m_w_out, v_w_out), ("w_up", g_up, w_up, m_w_up, v_w_up),
                          ("w_down", g_down, w_down, m_w_down, v_w_down)):
        dl, m2, v2 = _adam_big(g, w[0], m[0], v[0])
        res[n] = (g[None], dl[None], m2[None], v2[None])

    loss = loss_row[0, 0]
    return (loss, gx.reshape(nb, seq, d), *[res[n][0] for n in _WEIGHTS], *[res[n][1] for n in _WEIGHTS],
            *[res[n][2] for n in _WEIGHTS], *[res[n][3] for n in _WEIGHTS])
```

```python
import functools

import jax
import jax.numpy as jnp
from jax import lax
from jax.experimental import pallas as pl
from jax.experimental.pallas import tpu as pltpu

F32 = jnp.float32
BF = jnp.bfloat16
MESH = pl.DeviceIdType.MESH

EPS = 1e-5
D_MODEL = 1024
POOL_WIDTH = 512
POOL_WINDOWS = (2, 4, 8, 16)
POOL_GROUP = 128
SSD_INNER = 1024
SSD_HEADS = 16
SSD_HEAD_DIM = 64
SSD_STATE = 128
GROUP_W = 512
CHUNK = 128
CONV_CH = 1536
OFF_Z = 512
OFF_XBC = 1536
OFF_DT = 3072
IN_WIDTH = 3088
IN_PAD = 3200
D_FF = 4096
N_MOD = 6
N_CHIPS = 4
N_DEV = 8
HALO = 16
CONV_HALO = 8

ADAM_LR = 0.001
ADAM_B1 = 0.9
ADAM_B2 = 0.999
ADAM_EPS = 1e-08
ADAM_WD = 0.01
ADAM_STEP = 10

VMEM_BYTES_V7X = 64 * 1024 * 1024


def _cp(semantics=None, vmem_mb=48, **kw):
    args = dict(vmem_limit_bytes=vmem_mb * 1024 * 1024, **kw)
    if semantics is not None:
        args["dimension_semantics"] = semantics
    return pltpu.CompilerParams(**args)


def _pin(*arrays):
    return [pltpu.with_memory_space_constraint(a, pltpu.HBM) for a in arrays]


def _nn(a, b):
    return jnp.dot(a, b, preferred_element_type=F32)


def _nt(a, b):
    return lax.dot_general(a, b, (((1,), (1,)), ((), ())), preferred_element_type=F32)


def _tn(a, b):
    return lax.dot_general(a, b, (((0,), (0,)), ((), ())), preferred_element_type=F32)


def _split3(v):
    hi = v.astype(BF)
    r1 = v - hi.astype(F32)
    mid = r1.astype(BF)
    lo = (r1 - mid.astype(F32)).astype(BF)
    return hi, mid, lo


def _exact_nn(v, m01):
    hi, mid, lo = _split3(v)
    return _nn(hi, m01) + _nn(mid, m01) + _nn(lo, m01)


def _exact_nn_left(m01, v):
    hi, mid, lo = _split3(v)
    return _nn(m01, hi) + _nn(m01, mid) + _nn(m01, lo)


def _exact_nt_left(m01, v):
    hi, mid, lo = _split3(v)
    return _nt(m01, hi) + _nt(m01, mid) + _nt(m01, lo)


def _sigmoid(v):
    return 1.0 / (1.0 + jnp.exp(-v))


def _iota(shape, dim):
    return lax.broadcasted_iota(jnp.int32, shape, dim)


def _head_expand_matrix(heads, width):
    return (_iota((heads, width), 1) // SSD_HEAD_DIM == _iota((heads, width), 0)).astype(BF)


def _head_reduce_matrix(width, heads):
    return (_iota((width, heads), 0) // SSD_HEAD_DIM == _iota((width, heads), 1)).astype(BF)


def _mesh_pos():
    return lax.axis_index("x"), lax.axis_index("y"), lax.axis_index("c")


def _flip(v, bit):
    return v + bit - 2 * bit * v


def _all_gather_small(arrays):
    n = len(arrays)

    def body(*refs):
        in_refs, out_refs = refs[:n], refs[n:2 * n]
        send_sems, recv_sems, local_sems = refs[2 * n:]
        x, y, c = _mesh_pos()
        me = 4 * x + 2 * y + c
        local = []
        for a in range(n):
            cp = pltpu.make_async_copy(in_refs[a], out_refs[a].at[me], local_sems.at[a])
            cp.start()
            local.append(cp)
        sends = []
        for k in range(1, N_DEV):
            peer = (_flip(x, (k >> 2) & 1), _flip(y, (k >> 1) & 1), _flip(c, k & 1))
            for a in range(n):
                cp = pltpu.make_async_remote_copy(
                    src_ref=in_refs[a], dst_ref=out_refs[a].at[me],
                    send_sem=send_sems.at[a, k], recv_sem=recv_sems.at[a, k],
                    device_id=peer, device_id_type=MESH)
                cp.start()
                sends.append(cp)
        for k in range(1, N_DEV):
            px, py, pc = _flip(x, (k >> 2) & 1), _flip(y, (k >> 1) & 1), _flip(c, k & 1)
            src = 4 * px + 2 * py + pc
            for a in range(n):
                pltpu.make_async_remote_copy(
                    src_ref=in_refs[a], dst_ref=out_refs[a].at[src],
                    send_sem=send_sems.at[a, k], recv_sem=recv_sems.at[a, k],
                    device_id=(px, py, pc), device_id_type=MESH).wait_recv()
        for cp in sends:
            cp.wait_send()
        for cp in local:
            cp.wait()

    vm = pl.BlockSpec(memory_space=pltpu.VMEM)
    return pl.pallas_call(
        body, name="all_gather_small",
        out_shape=[jax.ShapeDtypeStruct((N_DEV,) + a.shape, a.dtype) for a in arrays],
        in_specs=[vm] * n, out_specs=[vm] * n,
        scratch_shapes=[pltpu.SemaphoreType.DMA((n, N_DEV)), pltpu.SemaphoreType.DMA((n, N_DEV)),
                        pltpu.SemaphoreType.DMA((n,))],
        compiler_params=_cp(vmem_mb=32),
    )(*arrays)


def _weight_all_gather(shards):
    n = len(shards)
    any_spec = _HBM

    def body(*refs):
        in_refs, out_refs = refs[:n], refs[n:2 * n]
        send_sems, recv_sems, fsend_sems, frecv_sems, local_sems = refs[2 * n:]
        x, y, c = _mesh_pos()
        chip = 2 * x + y
        local, sends = [], []
        for a in range(n):
            cp = pltpu.make_async_copy(in_refs[a], out_refs[a].at[chip], local_sems.at[a])
            cp.start()
            local.append(cp)

        def half(a, which):
            hc = shards[a].shape[1] // 2
            return pl.ds(pl.multiple_of(which * hc, 128), hc)

        for j in range(1, N_CHIPS):
            px, py = _flip(x, (j >> 1) & 1), _flip(y, j & 1)
            for a in range(n):
                cp = pltpu.make_async_remote_copy(
                    src_ref=in_refs[a].at[:, half(a, c)], dst_ref=out_refs[a].at[chip, :, half(a, c)],
                    send_sem=send_sems.at[a, j], recv_sem=recv_sems.at[a, j],
                    device_id=(px, py, c), device_id_type=MESH)
                cp.start()
                sends.append(cp)
        for j in range(1, N_CHIPS):
            px, py = _flip(x, (j >> 1) & 1), _flip(y, j & 1)
            src = 2 * px + py
            for a in range(n):
                landed = out_refs[a].at[src, :, half(a, c)]
                pltpu.make_async_remote_copy(
                    src_ref=landed, dst_ref=landed, send_sem=send_sems.at[a, j], recv_sem=recv_sems.at[a, j],
                    device_id=(px, py, c), device_id_type=MESH).wait_recv()
                cp = pltpu.make_async_remote_copy(
                    src_ref=landed, dst_ref=landed, send_sem=fsend_sems.at[a, j], recv_sem=frecv_sems.at[a, j],
                    device_id=(x, y, 1 - c), device_id_type=MESH)
                cp.start()
                sends.append(cp)
        for j in range(1, N_CHIPS):
            px, py = _flip(x, (j >> 1) & 1), _flip(y, j & 1)
            src = 2 * px + py
            for a in range(n):
                other = out_refs[a].at[src, :, half(a, 1 - c)]
                pltpu.make_async_remote_copy(
                    src_ref=other, dst_ref=other, send_sem=fsend_sems.at[a, j], recv_sem=frecv_sems.at[a, j],
                    device_id=(x, y, 1 - c), device_id_type=MESH).wait_recv()
        for cp in sends:
            cp.wait_send()
        for cp in local:
            cp.wait()

    return pl.pallas_call(
        body, name="weight_all_gather",
        out_shape=[jax.ShapeDtypeStruct((N_CHIPS,) + s.shape, s.dtype) for s in shards],
        in_specs=[any_spec] * n, out_specs=[any_spec] * n,
        scratch_shapes=[pltpu.SemaphoreType.DMA((n, N_CHIPS))] * 4 + [pltpu.SemaphoreType.DMA((n,))],
        compiler_params=_cp(vmem_mb=16),
    )(*shards)


_HBM = pl.BlockSpec(memory_space=pltpu.HBM)
_SEM = pl.BlockSpec(memory_space=pltpu.SEMAPHORE)
_DATAFLOW = pltpu.SideEffectType.DATAFLOW_SIDE_EFFECTING


def _peer_chip(x, y, j):
    return _flip(x, (j >> 1) & 1), _flip(y, j & 1)


def _ici_start(srcs, land_shapes, sent, landing, name, after):
    n = len(srcs)

    def body(*refs):
        src_refs, land_refs = refs[:n], refs[n:2 * n]
        send_sems, recv_sems = refs[2 * n + 1], refs[2 * n + 2]
        token = refs[-1]
        x, y, c = _mesh_pos()
        for j in range(1, N_CHIPS):
            px, py = _peer_chip(x, y, j)
            for a in range(n):
                pltpu.make_async_remote_copy(
                    src_ref=sent(src_refs[a], c, 2 * px + py), dst_ref=landing(land_refs[a], c, 2 * x + y),
                    send_sem=send_sems.at[a * (N_CHIPS - 1) + j - 1], recv_sem=recv_sems.at[a * (N_CHIPS - 1) + j - 1],
                    device_id=(px, py, c), device_id_type=MESH).start()
        token[...] = jnp.zeros_like(token)

    sems = pltpu.SemaphoreType.DMA((n * (N_CHIPS - 1),))
    lands = [pltpu.with_memory_space_constraint(lax.empty(s.shape, s.dtype), pltpu.HBM) for s in land_shapes]
    outs = pl.pallas_call(
        body, name=name,
        out_shape=(sems, sems, *[pltpu.HBM(s.shape, s.dtype) for s in srcs],
                   *[pltpu.HBM(s.shape, s.dtype) for s in land_shapes], jax.ShapeDtypeStruct((8, 128), F32)),
        in_specs=[_HBM] * (2 * n + 1), out_specs=[_SEM, _SEM] + [_HBM] * (2 * n) + [pl.BlockSpec(memory_space=pltpu.VMEM)],
        input_output_aliases={i: 2 + i for i in range(2 * n)},
        compiler_params=pltpu.CompilerParams(has_side_effects=_DATAFLOW),
    )(*_pin(*srcs), *lands, *_pin(after))
    return outs[0], outs[1], outs[2:2 + n], outs[2 + n:2 + 2 * n], outs[-1]


def _ici_wait(send_sems, recv_sems, src_thru, land_thru, after, sent, landing, name):
    n = len(src_thru)

    def body(*refs):
        src_refs, land_refs = refs[:n], refs[n:2 * n]
        send_sems, recv_sems = refs[2 * n], refs[2 * n + 1]
        x, y, c = _mesh_pos()
        for j in range(1, N_CHIPS):
            px, py = _peer_chip(x, y, j)
            for a in range(n):
                cp = pltpu.make_async_remote_copy(
                    src_ref=sent(src_refs[a], c, 2 * px + py), dst_ref=landing(land_refs[a], c, 2 * px + py),
                    send_sem=send_sems.at[a * (N_CHIPS - 1) + j - 1], recv_sem=recv_sems.at[a * (N_CHIPS - 1) + j - 1],
                    device_id=(px, py, c), device_id_type=MESH)
                cp.wait_send()
                cp.wait_recv()

    outs = pl.pallas_call(
        body, name=name,
        out_shape=tuple(pltpu.HBM(s.shape, s.dtype) for s in (*src_thru, *land_thru)),
        in_specs=[_HBM] * (2 * n) + [_SEM, _SEM, _HBM], out_specs=[_HBM] * (2 * n),
        input_output_aliases={i: i for i in range(2 * n)},
        compiler_params=pltpu.CompilerParams(has_side_effects=_DATAFLOW),
    )(*src_thru, *land_thru, send_sems, recv_sems, *_pin(after))
    return outs[:n], outs[n:]


def _col_half(ref, which, lead=()):
    hc = ref.shape[-1] // 2
    return ref.at[(*lead, slice(None), pl.ds(pl.multiple_of(which * hc, 128), hc))]


def _gather_sent(ref, c, dst_chip):
    return _col_half(ref, c)


def _gather_landing(ref, c, src_chip):
    return _col_half(ref, c, lead=(src_chip,))


def _reduce_sent(ref, c, dst_chip):
    return ref.at[dst_chip]


def _reduce_landing(ref, c, src_chip):
    return ref.at[src_chip]


def _gather_finish(lands, shards):
    n = len(lands)
    any_spec = _HBM

    def body(*refs):
        shard_refs, out_refs = refs[n:2 * n], refs[2 * n:3 * n]
        send_sems, recv_sems, local_sems = refs[3 * n:]
        x, y, c = _mesh_pos()
        chip = 2 * x + y
        local, sends = [], []
        for a in range(n):
            cp = pltpu.make_async_copy(shard_refs[a], out_refs[a].at[chip], local_sems.at[a])
            cp.start()
            local.append(cp)
        for j in range(1, N_CHIPS):
            px, py = _peer_chip(x, y, j)
            for a in range(n):
                landed = _col_half(out_refs[a], c, lead=(2 * px + py,))
                cp = pltpu.make_async_remote_copy(
                    src_ref=landed, dst_ref=landed, send_sem=send_sems.at[a, j], recv_sem=recv_sems.at[a, j],
                    device_id=(x, y, 1 - c), device_id_type=MESH)
                cp.start()
                sends.append(cp)
        for j in range(1, N_CHIPS):
            px, py = _peer_chip(x, y, j)
            for a in range(n):
                other = _col_half(out_refs[a], 1 - c, lead=(2 * px + py,))
                pltpu.make_async_remote_copy(
                    src_ref=other, dst_ref=other, send_sem=send_sems.at[a, j], recv_sem=recv_sems.at[a, j],
                    device_id=(x, y, 1 - c), device_id_type=MESH).wait_recv()
        for cp in sends:
            cp.wait_send()
        for cp in local:
            cp.wait()

    return pl.pallas_call(
        body, name="gather_finish",
        out_shape=[jax.ShapeDtypeStruct(l.shape, l.dtype) for l in lands],
        in_specs=[any_spec] * (2 * n), out_specs=[any_spec] * n,
        input_output_aliases={i: i for i in range(n)},
        scratch_shapes=[pltpu.SemaphoreType.DMA((n, N_CHIPS))] * 2 + [pltpu.SemaphoreType.DMA((n,))],
        compiler_params=_cp(vmem_mb=16),
    )(*lands, *shards)


def _grad_to_sibling(grads):
    n = len(grads)
    any_spec = _HBM

    def body(*refs):
        in_refs, own_refs, got_refs = refs[:n], refs[n:2 * n], refs[2 * n:3 * n]
        send_sems, recv_sems, local_sems = refs[3 * n:]
        x, y, c = _mesh_pos()
        copies, local = [], []
        for a in range(n):
            hc = grads[a].shape[2] // 2
            mine = pl.ds(pl.multiple_of(c * hc, 128), hc)
            theirs = pl.ds(pl.multiple_of((1 - c) * hc, 128), hc)
            lc = pltpu.make_async_copy(in_refs[a].at[:, :, mine], own_refs[a], local_sems.at[a])
            lc.start()
            local.append(lc)
            cp = pltpu.make_async_remote_copy(
                src_ref=in_refs[a].at[:, :, theirs], dst_ref=got_refs[a],
                send_sem=send_sems.at[a], recv_sem=recv_sems.at[a],
                device_id=(x, y, 1 - c), device_id_type=MESH)
            cp.start()
            copies.append(cp)
        for cp in copies:
            cp.wait_recv()
        for cp in copies:
            cp.wait_send()
        for lc in local:
            lc.wait()

    half = [jax.ShapeDtypeStruct((g.shape[0], g.shape[1], g.shape[2] // 2), g.dtype) for g in grads]
    outs = pl.pallas_call(
        body, name="grad_to_sibling", out_shape=half + half,
        in_specs=[any_spec] * n, out_specs=[any_spec] * (2 * n),
        scratch_shapes=[pltpu.SemaphoreType.DMA((n,))] * 3,
        compiler_params=_cp(vmem_mb=16),
    )(*grads)
    return outs[:n], outs[n:]


def _halves_exchange(halves):
    n = len(halves)
    any_spec = _HBM

    def body(*refs):
        in_refs, out_refs = refs[:n], refs[n:2 * n]
        send_sems, recv_sems, local_sems = refs[2 * n:]
        x, y, c = _mesh_pos()
        copies, local = [], []
        for a in range(n):
            hc = halves[a].shape[1]
            mine = pl.ds(pl.multiple_of(c * hc, 128), hc)
            lc = pltpu.make_async_copy(in_refs[a], out_refs[a].at[:, mine], local_sems.at[a])
            lc.start()
            local.append(lc)
            cp = pltpu.make_async_remote_copy(
                src_ref=in_refs[a], dst_ref=out_refs[a].at[:, mine],
                send_sem=send_sems.at[a], recv_sem=recv_sems.at[a],
                device_id=(x, y, 1 - c), device_id_type=MESH)
            cp.start()
            copies.append(cp)
        for a in range(n):
            hc = halves[a].shape[1]
            theirs = pl.ds(pl.multiple_of((1 - c) * hc, 128), hc)
            pltpu.make_async_remote_copy(
                src_ref=in_refs[a], dst_ref=out_refs[a].at[:, theirs],
                send_sem=send_sems.at[a], recv_sem=recv_sems.at[a],
                device_id=(x, y, 1 - c), device_id_type=MESH).wait_recv()
        for cp in copies:
            cp.wait_send()
        for lc in local:
            lc.wait()

    return pl.pallas_call(
        body, name="halves_exchange",
        out_shape=[jax.ShapeDtypeStruct((h.shape[0], 2 * h.shape[1]), h.dtype) for h in halves],
        in_specs=[any_spec] * n, out_specs=[any_spec] * n,
        scratch_shapes=[pltpu.SemaphoreType.DMA((n,))] * 3,
        compiler_params=_cp(vmem_mb=16),
    )(*halves)


def _add_pair(a, b):
    k, h, c = a.shape

    def body(a_ref, b_ref, o_ref):
        o_ref[...] = (a_ref[...].astype(F32) + b_ref[...].astype(F32)).astype(o_ref.dtype)

    spec = pl.BlockSpec((1, h, c), lambda i: (i, 0, 0))
    return pl.pallas_call(body, name="add_pair", grid=(k,), in_specs=[spec, spec], out_specs=spec,
                          out_shape=jax.ShapeDtypeStruct(a.shape, BF), compiler_params=_cp(("parallel",), 32))(*_pin(a, b))


def _sum_slots(recv, own):
    k, h, c = recv.shape
    tc = c // 2

    def body(r_ref, p_ref, o_ref):
        chip = 2 * lax.axis_index("x") + lax.axis_index("y")
        o_ref[...] = jnp.zeros_like(o_ref)
        for s in range(k):
            @pl.when(chip == s)
            def _():
                o_ref[...] += p_ref[s].astype(F32)

            @pl.when(chip != s)
            def _():
                o_ref[...] += r_ref[s].astype(F32)

    spec = pl.BlockSpec((k, h, tc), lambda i: (0, 0, i))
    return pl.pallas_call(body, name="sum_slots", grid=(2,), in_specs=[spec, spec],
                          out_specs=pl.BlockSpec((h, tc), lambda i: (0, i)),
                          out_shape=jax.ShapeDtypeStruct((h, c), F32), compiler_params=_cp(("parallel",), 32))(*_pin(recv, own))


def _adam_math(w, g, m, v):
    m2 = ADAM_B1 * m + (1.0 - ADAM_B1) * g
    v2 = ADAM_B2 * v + (1.0 - ADAM_B2) * (g * g)
    m_hat = m2 / (1.0 - ADAM_B1 ** ADAM_STEP)
    v_hat = v2 / (1.0 - ADAM_B2 ** ADAM_STEP)
    delta = -ADAM_LR * (m_hat / (jnp.sqrt(v_hat) + ADAM_EPS) + ADAM_WD * w)
    return delta, m2, v2


def _adam_big(g, w, m, v):
    r, c = w.shape
    tc = c // 4

    def body(g_ref, w_ref, m_ref, v_ref, d_ref, m2_ref, v2_ref):
        d, m2, v2 = _adam_math(w_ref[...], g_ref[...], m_ref[...], v_ref[...])
        d_ref[...] = d
        m2_ref[...] = m2
        v2_ref[...] = v2

    spec = pl.BlockSpec((r, tc), lambda i: (0, i))
    sh = jax.ShapeDtypeStruct((r, c), F32)
    return pl.pallas_call(body, name="adam_big", grid=(4,), in_specs=[spec] * 4, out_specs=[spec] * 3,
                          out_shape=[sh, sh, sh], compiler_params=_cp(("parallel",), 32))(*_pin(g, w, m, v))


def _adam_ada(c_act_t, dmod_cols, w, m, v):
    r, c = w.shape
    tc = 512

    def body(ct_ref, dm_ref, w_ref, m_ref, v_ref, g_ref, d_ref, m2_ref, v2_ref):
        g = _nn(ct_ref[...], dm_ref[...].astype(BF))
        d, m2, v2 = _adam_math(w_ref[...], g, m_ref[...], v_ref[...])
        g_ref[...] = g
        d_ref[...] = d
        m2_ref[...] = m2
        v2_ref[...] = v2

    spec = pl.BlockSpec((r, tc), lambda i: (0, i))
    sh = jax.ShapeDtypeStruct((r, c), F32)
    return pl.pallas_call(
        body, name="adam_ada", grid=(c // tc,),
        in_specs=[pl.BlockSpec(c_act_t.shape, lambda i: (0, 0)), pl.BlockSpec((dmod_cols.shape[0], tc), lambda i: (0, i)),
                  spec, spec, spec],
        out_specs=[spec] * 4, out_shape=[sh] * 4, compiler_params=_cp(("parallel",), 48))(*_pin(c_act_t, dmod_cols, w, m, v))


def _ada_mod(c_all, w_shard, b_shard):
    nb, d = c_all.shape
    cols = w_shard.shape[1]
    tc = 512

    def body(c_ref, w_ref, b_ref, mod_ref, act_ref):
        cv = c_ref[...]
        act = cv * _sigmoid(cv)
        act_ref[...] = act
        mod_ref[...] = _nn(act.astype(BF), w_ref[...].astype(BF)) + b_ref[...]

    return pl.pallas_call(
        body, name="ada_mod", grid=(cols // tc,),
        in_specs=[pl.BlockSpec((nb, d), lambda i: (0, 0)), pl.BlockSpec((d, tc), lambda i: (0, i)),
                  pl.BlockSpec((1, tc), lambda i: (0, i))],
        out_specs=[pl.BlockSpec((nb, tc), lambda i: (0, i)), pl.BlockSpec((nb, d), lambda i: (0, 0))],
        out_shape=[jax.ShapeDtypeStruct((nb, cols), F32), jax.ShapeDtypeStruct((nb, d), F32)],
        compiler_params=_cp(("arbitrary",), 32))(*_pin(c_all, w_shard, b_shard))


def _token_spec():
    return pl.BlockSpec((8, 128), lambda *_: (0, 0))


def _in_proj(x, mod3, g_mix, w_in_t, seq, token):
    t, d = x.shape
    tm = 256
    tps = seq // tm

    def body(x_ref, mod_ref, g_ref, w_ref, tok_ref, proj_ref, u1_ref):
        xv = x_ref[...]
        r = lax.rsqrt(jnp.mean(xv * xv, -1, keepdims=True) + EPS)
        u = (xv * r * g_ref[...]) * (1.0 + mod_ref[0, 1:2, :]) + mod_ref[0, 0:1, :]
        ub = u.astype(BF)
        u1_ref[...] = ub
        proj_ref[...] = _nt(ub, w_ref[...])

    return pl.pallas_call(
        body, name="in_proj", grid=(t // tm,),
        in_specs=[pl.BlockSpec((tm, d), lambda i: (i, 0)), pl.BlockSpec((1, N_MOD, d), lambda i: (i // tps, 0, 0)),
                  pl.BlockSpec((1, d), lambda i: (0, 0)), pl.BlockSpec((IN_PAD, d), lambda i: (0, 0)), _token_spec()],
        out_specs=[pl.BlockSpec((tm, IN_PAD), lambda i: (i, 0)), pl.BlockSpec((tm, d), lambda i: (i, 0))],
        out_shape=[jax.ShapeDtypeStruct((t, IN_PAD), F32), jax.ShapeDtypeStruct((t, d), BF)],
        compiler_params=_cp(("parallel",), 48))(*_pin(x, mod3, g_mix, w_in_t, token))


def _pool_tile(seq):
    return min(512, seq)


def _pool_fwd(proj, w_pool, pool_scale, nb, seq):
    ts = _pool_tile(seq)
    nt = seq // ts

    def body(u_ref, halo_ref, wp_ref, ps_ref, yp_ref, p_ref):
        i = pl.program_id(1)
        halo = jnp.where(i == 0, 0.0, halo_ref[...])
        u = u_ref[...]
        ext = jnp.concatenate([halo, u], 0)
        tpos = i * ts + _iota((ts, 1), 0)
        for g, w in enumerate(POOL_WINDOWS):
            gs = slice(g * POOL_GROUP, (g + 1) * POOL_GROUP)
            s = ext[:, gs]
            sh = 1
            while sh < w:
                s = s + pltpu.roll(s, sh, 0)
                sh *= 2
            cnt = jnp.minimum(tpos + 1, w).astype(F32)
            pb = (s[HALO:] / cnt - u[:, gs]).astype(BF)
            p_ref[:, gs] = pb
            yp_ref[:, gs] = (_nn(pb, wp_ref[g].astype(BF)) * ps_ref[:, gs]).astype(BF)

    hb = ts // HALO
    return pl.pallas_call(
        body, name="pool_fwd", grid=(nb, nt),
        in_specs=[pl.BlockSpec((ts, POOL_WIDTH), lambda b, i: (b * nt + i, 0)),
                  pl.BlockSpec((HALO, POOL_WIDTH), lambda b, i: (jnp.maximum((b * nt + i) * hb - 1, 0), 0)),
                  pl.BlockSpec((4, POOL_GROUP, POOL_GROUP), lambda b, i: (0, 0, 0)),
                  pl.BlockSpec((1, POOL_WIDTH), lambda b, i: (0, 0))],
        out_specs=[pl.BlockSpec((ts, POOL_WIDTH), lambda b, i: (b * nt + i, 0))] * 2,
        out_shape=[jax.ShapeDtypeStruct((nb * seq, POOL_WIDTH), BF)] * 2,
        compiler_params=_cp(("parallel", "parallel"), 32))(*_pin(proj, proj, w_pool, pool_scale))


def _chunk_recompute(uxbc, halo, udt, cw, cb, dtb, alog, first):
    halo = jnp.where(first, 0.0, halo)
    ext = jnp.concatenate([halo, uxbc], 0)
    pre = cb + uxbc * cw[3:4]
    for k in (2, 1, 0):
        pre = pre + pltpu.roll(ext, 3 - k, 0)[CONV_HALO:] * cw[k:k + 1]
    sg = _sigmoid(pre)
    xbc = pre * sg
    dtp = udt[:, :SSD_HEADS] + dtb
    dt = jnp.maximum(dtp, 0.0) + jnp.log(1.0 + jnp.exp(-jnp.abs(dtp)))
    a = -jnp.exp(alog)
    da = dt * a
    tril = (_iota((CHUNK, CHUNK), 0) >= _iota((CHUNK, CHUNK), 1))
    acum = _exact_nn_left(tril.astype(BF), da)
    eye = (_iota((SSD_HEADS, SSD_HEADS), 0) == _iota((SSD_HEADS, SSD_HEADS), 1)).astype(BF)
    acum_t = _exact_nt_left(eye, acum)
    expand = _head_expand_matrix(SSD_HEADS, SSD_INNER)
    acum_e = _exact_nn(acum, expand)
    dt_e = _exact_nn(dt, expand)
    last_e = acum_e[CHUNK - 1:CHUNK]
    return dict(pre=pre, sg=sg, xbc=xbc, dtp=dtp, dt=dt, a=a, acum=acum, acum_t=acum_t, tril=tril,
                dt_e=dt_e, e_a=jnp.exp(acum_e), d_out=jnp.exp(last_e - acum_e), c_dec=jnp.exp(last_e))


def _head_decay(r, h):
    seg = r["acum"][:, h:h + 1] - r["acum_t"][h:h + 1, :]
    return jnp.where(r["tril"], jnp.exp(jnp.minimum(seg, 0.0)), 0.0)


def _ssd_specs(nb, seq, reverse):
    nc = seq // CHUNK
    per = seq // CONV_HALO

    def cidx(c):
        return (nc - 1 - c) if reverse else c

    def row(b, c):
        return b * nc + cidx(c)

    specs = [
        pl.BlockSpec((CHUNK, CONV_CH), lambda b, c: (row(b, c), 1)),
        pl.BlockSpec((CONV_HALO, CONV_CH),
                     lambda b, c: (jnp.maximum(b * per + cidx(c) * (CHUNK // CONV_HALO) - 1, 0), 1)),
        pl.BlockSpec((CHUNK, GROUP_W), lambda b, c: (row(b, c), 1)),
        pl.BlockSpec((CHUNK, GROUP_W), lambda b, c: (row(b, c), 2)),
        pl.BlockSpec((CHUNK, 128), lambda b, c: (row(b, c), OFF_DT // 128)),
    ]
    return specs, row, cidx, nc


def _const_spec(shape):
    return pl.BlockSpec(shape, lambda b, c: (0,) * len(shape))


def _ssd_fwd(proj, conv_w, conv_b, dt_bias, a_log, dskip_e, g_ssd, nb, seq):
    specs, row, cidx, nc = _ssd_specs(nb, seq, reverse=False)

    def body(uxbc_ref, halo_ref, z0_ref, z1_ref, udt_ref, cw_ref, cb_ref, dtb_ref, alog_ref, dsk_ref, gs_ref,
             yssd_ref, yssm_ref, hprev_ref, h_ref, yd_ref):
        c = pl.program_id(1)

        @pl.when(c == 0)
        def _():
            h_ref[...] = jnp.zeros_like(h_ref)

        r = _chunk_recompute(uxbc_ref[...], halo_ref[...], udt_ref[...], cw_ref[...], cb_ref[...],
                             dtb_ref[...], alog_ref[...], c == 0)
        xbc = r["xbc"]
        xs = xbc[:, :SSD_INNER]
        xdt = xs * r["dt_e"]
        xdt_b = xdt.astype(BF)
        xdo_b = (xdt * r["d_out"]).astype(BF)
        hprev_ref[0, 0] = h_ref[...]
        for g in range(2):
            gs = slice(g * GROUP_W, (g + 1) * GROUP_W)
            bg = xbc[:, SSD_INNER + g * SSD_STATE:SSD_INNER + (g + 1) * SSD_STATE].astype(BF)
            cg = xbc[:, SSD_INNER + (2 + g) * SSD_STATE:SSD_INNER + (3 + g) * SSD_STATE].astype(BF)
            scores = _nt(cg, bg)
            hg = h_ref[g]
            y_off = _nn(cg, hg.astype(BF)) * r["e_a"][:, gs]
            for hh in range(8):
                h = g * 8 + hh
                m = (scores * _head_decay(r, h)).astype(BF)
                yd_ref[:, h * SSD_HEAD_DIM:(h + 1) * SSD_HEAD_DIM] = _nn(m, xdt_b[:, h * SSD_HEAD_DIM:(h + 1) * SSD_HEAD_DIM])
            h_ref[g] = hg * r["c_dec"][:, gs] + _tn(bg, xdo_b[:, gs])
            y = yd_ref[:, gs] + y_off + dsk_ref[:, gs] * xs[:, gs]
            yssm_ref[:, gs] = y
            zg = (z0_ref if g == 0 else z1_ref)[...]
            yg = y * (zg * _sigmoid(zg))
            rg = lax.rsqrt(jnp.mean(yg * yg, -1, keepdims=True) + EPS)
            yssd_ref[:, gs] = (yg * rg * gs_ref[:, gs]).astype(BF)

    t = nb * seq
    return pl.pallas_call(
        body, name="ssd_fwd", grid=(nb, nc),
        in_specs=specs + [_const_spec((4, CONV_CH)), _const_spec((1, CONV_CH)), _const_spec((1, SSD_HEADS)),
                          _const_spec((1, SSD_HEADS)), _const_spec((1, SSD_INNER)), _const_spec((1, SSD_INNER))],
        out_specs=[pl.BlockSpec((CHUNK, SSD_INNER), lambda b, c: (row(b, c), 0)),
                   pl.BlockSpec((CHUNK, SSD_INNER), lambda b, c: (row(b, c), 0)),
                   pl.BlockSpec((1, 1, 2, SSD_STATE, GROUP_W), lambda b, c: (b, c, 0, 0, 0))],
        out_shape=[jax.ShapeDtypeStruct((t, SSD_INNER), BF), jax.ShapeDtypeStruct((t, SSD_INNER), F32),
                   jax.ShapeDtypeStruct((nb, nc, 2, SSD_STATE, GROUP_W), F32)],
        scratch_shapes=[pltpu.VMEM((2, SSD_STATE, GROUP_W), F32), pltpu.VMEM((CHUNK, SSD_INNER), F32)],
        compiler_params=_cp(("arbitrary", "arbitrary"), 48),
    )(*_pin(proj, proj, proj, proj, proj, conv_w, conv_b, dt_bias, a_log, dskip_e, g_ssd))


def _out_proj(y_pool, y_ssd, w_out, x, mod3, g_mlp, seq):
    t, d = x.shape
    tm = 512
    tps = seq // tm if seq >= tm else 1
    tm = min(tm, seq)

    def body(yp_ref, ys_ref, w_ref, x_ref, mod_ref, g_ref, h1_ref, o_ref, u2_ref):
        o = _nn(yp_ref[...], w_ref[0:POOL_WIDTH, :]) + _nn(ys_ref[...], w_ref[POOL_WIDTH:, :])
        o_ref[...] = o.astype(BF)
        h1 = x_ref[...] + mod_ref[0, 2:3, :] * o
        h1_ref[...] = h1
        r = lax.rsqrt(jnp.mean(h1 * h1, -1, keepdims=True) + EPS)
        u2_ref[...] = ((h1 * r * g_ref[...]) * (1.0 + mod_ref[0, 4:5, :]) + mod_ref[0, 3:4, :]).astype(BF)

    row = lambda i: (i, 0)
    return pl.pallas_call(
        body, name="out_proj", grid=(t // tm,),
        in_specs=[pl.BlockSpec((tm, POOL_WIDTH), row), pl.BlockSpec((tm, SSD_INNER), row),
                  pl.BlockSpec(w_out.shape, lambda i: (0, 0)), pl.BlockSpec((tm, d), row),
                  pl.BlockSpec((1, N_MOD, d), lambda i: (i // tps, 0, 0)), pl.BlockSpec((1, d), lambda i: (0, 0))],
        out_specs=[pl.BlockSpec((tm, d), row)] * 3,
        out_shape=[jax.ShapeDtypeStruct((t, d), F32), jax.ShapeDtypeStruct((t, d), BF), jax.ShapeDtypeStruct((t, d), BF)],
        compiler_params=_cp(("parallel",), 48))(*_pin(y_pool, y_ssd, w_out, x, mod3, g_mlp))


def _mlp_up(u2, w_up4):
    t, d = u2.shape
    tm = min(512, t)
    nk, _, cols = w_up4.shape

    def body(u_ref, w_ref, a_ref):
        a_ref[...] = _nn(u_ref[...], w_ref[0]).astype(BF)

    return pl.pallas_call(
        body, name="mlp_up", grid=(nk, t // tm),
        in_specs=[pl.BlockSpec((tm, d), lambda k, i: (i, 0)), pl.BlockSpec((1, d, cols), lambda k, i: (k, 0, 0))],
        out_specs=pl.BlockSpec((tm, cols), lambda k, i: (i, k)),
        out_shape=jax.ShapeDtypeStruct((t, nk * cols), BF),
        compiler_params=_cp(("parallel", "parallel"), 32))(*_pin(u2, w_up4))


def _mlp_down_loss(a_up, w_down, h1, mod3, g_final, target, seq):
    t, d = h1.shape
    nb = t // seq
    tm = min(256, seq)
    tps = seq // tm

    def body(a_ref, w_ref, h1_ref, mod_ref, g_ref, tg_ref, ddn_ref, dh2_ref, sq_ref, gg_ref, dgf_ref):
        i = pl.program_id(0)
        f = jnp.square(jnp.maximum(a_ref[...], 0))
        dn = _nn(f, w_ref[...])
        gate = mod_ref[0, 5:6, :]
        h2 = h1_ref[...] + gate * dn
        r = lax.rsqrt(jnp.mean(h2 * h2, -1, keepdims=True) + EPS)
        hh = h2 * r
        err = hh * g_ref[...] - tg_ref[...]
        dy = err * (1.0 / d)
        dhat = dy * g_ref[...]
        dh2 = r * (dhat - hh * jnp.mean(dhat * hh, -1, keepdims=True))
        dh2_ref[...] = dh2
        ddn_ref[...] = (dh2 * gate).astype(BF)

        @pl.when(i == 0)
        def _():
            sq_ref[...] = jnp.zeros_like(sq_ref)
            gg_ref[...] = jnp.zeros_like(gg_ref)

        @pl.when(i % tps == 0)
        def _():
            dgf_ref[...] = jnp.zeros_like(dgf_ref)

        sq_ref[...] += jnp.sum(err * err, 0, keepdims=True)
        gg_ref[...] += jnp.sum(dy * hh, 0, keepdims=True)
        dgf_ref[0] += jnp.sum(dh2 * dn, 0, keepdims=True)

    row = lambda i: (i, 0)
    vec = pl.BlockSpec((1, d), lambda i: (0, 0))
    return pl.pallas_call(
        body, name="mlp_down_loss", grid=(t // tm,),
        in_specs=[pl.BlockSpec((tm, D_FF), row), pl.BlockSpec(w_down.shape, lambda i: (0, 0)), pl.BlockSpec((tm, d), row),
                  pl.BlockSpec((1, N_MOD, d), lambda i: (i // tps, 0, 0)), vec, pl.BlockSpec((tm, d), row)],
        out_specs=[pl.BlockSpec((tm, d), row), pl.BlockSpec((tm, d), row), vec, vec,
                   pl.BlockSpec((1, 1, d), lambda i: (i // tps, 0, 0))],
        out_shape=[jax.ShapeDtypeStruct((t, d), BF), jax.ShapeDtypeStruct((t, d), F32), jax.ShapeDtypeStruct((1, d), F32),
                   jax.ShapeDtypeStruct((1, d), F32), jax.ShapeDtypeStruct((nb, 1, d), F32)],
        compiler_params=_cp(("arbitrary",), 56))(*_pin(a_up, w_down, h1, mod3, g_final, target))


def _tn_matmul(a, b, tk, tn, name, square_relu=False, out3=False):
    t, kdim = a.shape
    ndim = b.shape[1]

    def body(a_ref, b_ref, o_ref):
        av = a_ref[...]
        if square_relu:
            av = jnp.square(jnp.maximum(av, 0))
        res = _tn(av, b_ref[...]).astype(BF)
        if out3:
            o_ref[0] = res
        else:
            o_ref[...] = res

    if out3:
        out_spec = pl.BlockSpec((1, tk, tn), lambda j, i: (j, i, 0))
        out_shape = jax.ShapeDtypeStruct((ndim // tn, kdim, tn), BF)
    else:
        out_spec = pl.BlockSpec((tk, tn), lambda j, i: (i, j))
        out_shape = jax.ShapeDtypeStruct((kdim, ndim), BF)
    return pl.pallas_call(
        body, name=name, grid=(ndim // tn, kdim // tk),
        in_specs=[pl.BlockSpec((t, tk), lambda j, i: (0, i)), pl.BlockSpec((t, tn), lambda j, i: (0, j))],
        out_specs=out_spec, out_shape=out_shape,
        compiler_params=_cp(("parallel", "parallel"), 56))(*_pin(a, b))


def _mlp_down_bwd(d_dn, w_down4, a_up, token):
    t, d = d_dn.shape
    tm = min(512, t)
    nk, rows, _ = w_down4.shape

    def body(g_ref, w_ref, a_ref, tok_ref, o_ref):
        df = _nt(g_ref[...], w_ref[0])
        o_ref[...] = (df * (2.0 * jnp.maximum(a_ref[...], 0).astype(F32))).astype(BF)

    return pl.pallas_call(
        body, name="mlp_down_bwd", grid=(nk, t // tm),
        in_specs=[pl.BlockSpec((tm, d), lambda k, i: (i, 0)), pl.BlockSpec((1, rows, d), lambda k, i: (k, 0, 0)),
                  pl.BlockSpec((tm, rows), lambda k, i: (i, k)), _token_spec()],
        out_specs=pl.BlockSpec((tm, rows), lambda k, i: (i, k)),
        out_shape=jax.ShapeDtypeStruct((t, nk * rows), BF),
        compiler_params=_cp(("parallel", "parallel"), 32))(*_pin(d_dn, w_down4, a_up, token))


def _mlp_up_bwd(d_a, w_up4, h1, dh2, o, mod3, g_mlp, seq, token):
    t, d = h1.shape
    nb = t // seq
    tm = min(256, seq)
    tps = seq // tm
    nk = w_up4.shape[0]
    cols = w_up4.shape[2]

    def body(da_ref, w_ref, h1_ref, dh2_ref, o_ref, mod_ref, g_ref, tok_ref, dh1_ref, do_ref, acc_ref, gg_ref):
        i = pl.program_id(0)
        du = _nt(da_ref[:, 0:cols], w_ref[0])
        for k in range(1, nk):
            du = du + _nt(da_ref[:, k * cols:(k + 1) * cols], w_ref[k])
        h1 = h1_ref[...]
        r = lax.rsqrt(jnp.mean(h1 * h1, -1, keepdims=True) + EPS)
        hh = h1 * r
        n2 = hh * g_ref[...]
        dn2 = du * (1.0 + mod_ref[0, 4:5, :])
        dhat = dn2 * g_ref[...]
        dh1 = dh2_ref[...] + r * (dhat - hh * jnp.mean(dhat * hh, -1, keepdims=True))
        dh1_ref[...] = dh1
        do_ref[...] = (dh1 * mod_ref[0, 2:3, :]).astype(BF)

        @pl.when(i == 0)
        def _():
            gg_ref[...] = jnp.zeros_like(gg_ref)

        @pl.when(i % tps == 0)
        def _():
            acc_ref[...] = jnp.zeros_like(acc_ref)

        gg_ref[...] += jnp.sum(dn2 * hh, 0, keepdims=True)
        acc_ref[0, 0:1, :] += jnp.sum(du, 0, keepdims=True)
        acc_ref[0, 1:2, :] += jnp.sum(du * n2, 0, keepdims=True)
        acc_ref[0, 2:3, :] += jnp.sum(dh1 * o_ref[...].astype(F32), 0, keepdims=True)

    row = lambda i: (i, 0)
    vec = pl.BlockSpec((1, d), lambda i: (0, 0))
    return pl.pallas_call(
        body, name="mlp_up_bwd", grid=(t // tm,),
        in_specs=[pl.BlockSpec((tm, D_FF), row), pl.BlockSpec(w_up4.shape, lambda i: (0, 0, 0)), pl.BlockSpec((tm, d), row),
                  pl.BlockSpec((tm, d), row), pl.BlockSpec((tm, d), row),
                  pl.BlockSpec((1, N_MOD, d), lambda i: (i // tps, 0, 0)), vec, _token_spec()],
        out_specs=[pl.BlockSpec((tm, d), row), pl.BlockSpec((tm, d), row),
                   pl.BlockSpec((1, 8, d), lambda i: (i // tps, 0, 0)), vec],
        out_shape=[jax.ShapeDtypeStruct((t, d), F32), jax.ShapeDtypeStruct((t, d), BF),
                   jax.ShapeDtypeStruct((nb, 8, d), F32), jax.ShapeDtypeStruct((1, d), F32)],
        compiler_params=_cp(("arbitrary",), 56))(*_pin(d_a, w_up4, h1, dh2, o, mod3, g_mlp, token))


def _out_proj_bwd(d_o, w_out, token):
    t, d = d_o.shape
    tm = min(512, t)

    def body(g_ref, w_ref, tok_ref, dp_ref, ds_ref):
        gv = g_ref[...]
        dp_ref[...] = _nt(gv, w_ref[0:POOL_WIDTH, :])
        ds_ref[...] = _nt(gv, w_ref[POOL_WIDTH:, :])

    row = lambda i: (i, 0)
    return pl.pallas_call(
        body, name="out_proj_bwd", grid=(t // tm,),
        in_specs=[pl.BlockSpec((tm, d), row), pl.BlockSpec(w_out.shape, lambda i: (0, 0)), _token_spec()],
        out_specs=[pl.BlockSpec((tm, POOL_WIDTH), row), pl.BlockSpec((tm, SSD_INNER), row)],
        out_shape=[jax.ShapeDtypeStruct((t, POOL_WIDTH), F32), jax.ShapeDtypeStruct((t, SSD_INNER), F32)],
        compiler_params=_cp(("parallel",), 32))(*_pin(d_o, w_out, token))


def _pool_bwd(d_ypool, p, w_pool, pool_scale, nb, seq):
    ts = _pool_tile(seq)
    nt = seq // ts
    hb = ts // HALO
    last_block = nb * seq // HALO - 1

    def body(dy_ref, halo_ref, p_ref, wp_ref, ps_ref, du_ref, gw_ref, gs_ref):
        b = pl.program_id(0)
        i = pl.program_id(1)

        @pl.when((b == 0) & (i == 0))
        def _():
            gw_ref[...] = jnp.zeros_like(gw_ref)
            gs_ref[...] = jnp.zeros_like(gs_ref)

        halo = jnp.where(i == nt - 1, 0.0, halo_ref[...])
        dy = dy_ref[...]
        ext = jnp.concatenate([dy, halo], 0)
        tpos = i * ts + _iota((ts + HALO, 1), 0)
        n_ext = ts + HALO
        for g, w in enumerate(POOL_WINDOWS):
            gs = slice(g * POOL_GROUP, (g + 1) * POOL_GROUP)
            wg = wp_ref[g].astype(BF)
            pg = p_ref[:, gs]
            pw = _nn(pg, wg)
            gs_ref[:, gs] += jnp.sum(dy[:, gs] * pw, 0, keepdims=True)
            dpw = (ext[:, gs] * ps_ref[:, gs]).astype(BF)
            gw_ref[g] += _tn(pg, dpw[:ts])
            dp = _nt(dpw, wg)
            cnt = jnp.minimum(tpos + 1, w).astype(F32)
            s = dp / cnt
            sh = 1
            while sh < w:
                s = s + pltpu.roll(s, n_ext - sh, 0)
                sh *= 2
            du_ref[:, gs] = (s[:ts] - dp[:ts]).astype(BF)

    return pl.pallas_call(
        body, name="pool_bwd", grid=(nb, nt),
        in_specs=[pl.BlockSpec((ts, POOL_WIDTH), lambda b, i: (b * nt + i, 0)),
                  pl.BlockSpec((HALO, POOL_WIDTH), lambda b, i: (jnp.minimum((b * nt + i + 1) * hb, last_block), 0)),
                  pl.BlockSpec((ts, POOL_WIDTH), lambda b, i: (b * nt + i, 0)),
                  pl.BlockSpec((4, POOL_GROUP, POOL_GROUP), lambda b, i: (0, 0, 0)),
                  pl.BlockSpec((1, POOL_WIDTH), lambda b, i: (0, 0))],
        out_specs=[pl.BlockSpec((ts, POOL_WIDTH), lambda b, i: (b * nt + i, 0)),
                   pl.BlockSpec((4, POOL_GROUP, POOL_GROUP), lambda b, i: (0, 0, 0)),
                   pl.BlockSpec((1, POOL_WIDTH), lambda b, i: (0, 0))],
        out_shape=[jax.ShapeDtypeStruct((nb * seq, POOL_WIDTH), BF), jax.ShapeDtypeStruct((4, POOL_GROUP, POOL_GROUP), F32),
                   jax.ShapeDtypeStruct((1, POOL_WIDTH), F32)],
        compiler_params=_cp(("arbitrary", "arbitrary"), 32))(*_pin(d_ypool, d_ypool, p, w_pool, pool_scale))


def _ssd_bwd(proj, d_yssd, yssm, h_prev, conv_w, conv_b, dt_bias, a_log, dskip_e, g_ssd, nb, seq):
    specs, row, cidx, nc = _ssd_specs(nb, seq, reverse=True)

    def body(uxbc_ref, halo_ref, z0_ref, z1_ref, udt_ref, dys_ref, yssm_ref, hprev_ref,
             cw_ref, cb_ref, dtb_ref, alog_ref, dsk_ref, gs_ref,
             dz_ref, dpre_ref, dudt_ref, ggs_ref, gdsk_ref, ga_ref, gdtb_ref,
             g_ref, dxdt_ref, dyv_ref):
        b = pl.program_id(0)
        c = pl.program_id(1)

        @pl.when(c == 0)
        def _():
            g_ref[...] = jnp.zeros_like(g_ref)

        @pl.when((b == 0) & (c == 0))
        def _():
            ggs_ref[...] = jnp.zeros_like(ggs_ref)
            gdsk_ref[...] = jnp.zeros_like(gdsk_ref)
            ga_ref[...] = jnp.zeros_like(ga_ref)
            gdtb_ref[...] = jnp.zeros_like(gdtb_ref)

        r = _chunk_recompute(uxbc_ref[...], halo_ref[...], udt_ref[...], cw_ref[...], cb_ref[...],
                             dtb_ref[...], alog_ref[...], c == nc - 1)
        xbc = r["xbc"]
        xs = xbc[:, :SSD_INNER]
        dt_e = r["dt_e"]
        xdt = xs * dt_e
        xdt_b = xdt.astype(BF)
        reduce_m = _head_reduce_matrix(GROUP_W, 8)
        onehot16 = lambda h: (_iota((1, SSD_HEADS), 1) == h).astype(F32)
        onecol16 = lambda h: (_iota((SSD_HEADS, 1), 0) == h).astype(F32)

        d_acum = jnp.zeros((CHUNK, SSD_HEADS), F32)
        d_acum_t = jnp.zeros((SSD_HEADS, CHUNK), F32)
        d_alast = jnp.zeros((1, SSD_HEADS), F32)
        place8 = lambda g: (_iota((8, SSD_HEADS), 1) == _iota((8, SSD_HEADS), 0) + 8 * g).astype(BF)
        d_b, d_c = [], []
        for g in range(2):
            gs = slice(g * GROUP_W, (g + 1) * GROUP_W)
            zg = (z0_ref if g == 0 else z1_ref)[...]
            sz = _sigmoid(zg)
            silu_z = zg * sz
            ys = yssm_ref[:, gs]
            yg = ys * silu_z
            rg = lax.rsqrt(jnp.mean(yg * yg, -1, keepdims=True) + EPS)
            yh = yg * rg
            dys = dys_ref[:, gs]
            ggs_ref[:, gs] += jnp.sum(dys * yh, 0, keepdims=True)
            dyh = dys * gs_ref[:, gs]
            dyg = rg * (dyh - yh * jnp.mean(dyh * yh, -1, keepdims=True))
            dy = dyg * silu_z
            dz_ref[:, gs] = (dyg * ys * (sz * (1.0 + zg * (1.0 - sz)))).astype(BF)
            gdsk_ref[:, gs] += jnp.sum(dy * xs[:, gs], 0, keepdims=True)
            dyv_ref[:, gs] = dy
            dy_b = dy.astype(BF)

            bg = xbc[:, SSD_INNER + g * SSD_STATE:SSD_INNER + (g + 1) * SSD_STATE].astype(BF)
            cg = xbc[:, SSD_INNER + (2 + g) * SSD_STATE:SSD_INNER + (3 + g) * SSD_STATE].astype(BF)
            scores = _nt(cg, bg)
            hg = hprev_ref[0, 0, g]
            hg_b = hg.astype(BF)
            gg = g_ref[g]
            gg_b = gg.astype(BF)
            e_a = r["e_a"][:, gs]
            d_out = r["d_out"][:, gs]
            c_dec = r["c_dec"][:, gs]
            zc = _nn(cg, hg_b)
            wv = e_a * dy
            wv_b = wv.astype(BF)
            da_g = _exact_nn(wv * zc, reduce_m)
            dcg = _nt(wv_b, hg_b)
            d_hprev = _tn(cg, wv_b)
            vg = _nn(bg, gg_b)
            dxdt_g = d_out * vg
            dd_out = _exact_nn(xdt[:, gs] * vg, reduce_m)
            dbg = _nt((xdt[:, gs] * d_out).astype(BF), gg_b)
            dcd = _exact_nn(jnp.sum(gg * hg, 0, keepdims=True), reduce_m)
            d_out8 = jnp.exp(r["acum"][CHUNK - 1:CHUNK, 8 * g:8 * g + 8] - r["acum"][:, 8 * g:8 * g + 8])
            c_dec8 = jnp.exp(r["acum"][CHUNK - 1:CHUNK, 8 * g:8 * g + 8])
            t8 = dd_out * d_out8
            d_alast = d_alast + _exact_nn(jnp.sum(t8, 0, keepdims=True) + dcd * c_dec8, place8(g))
            d_acum = d_acum + _exact_nn(da_g - t8, place8(g))
            dsc = jnp.zeros((CHUNK, CHUNK), F32)
            for hh in range(8):
                h = g * 8 + hh
                hs = slice(h * SSD_HEAD_DIM, (h + 1) * SSD_HEAD_DIM)
                lam = _head_decay(r, h)
                m = scores * lam
                dyh_b = dy_b[:, hh * SSD_HEAD_DIM:(hh + 1) * SSD_HEAD_DIM]
                dm = _nt(dyh_b, xdt_b[:, hs])
                tm_ = dm * m
                d_acum = d_acum + jnp.sum(tm_, 1, keepdims=True) * onehot16(h)
                d_acum_t = d_acum_t + onecol16(h) * jnp.sum(tm_, 0, keepdims=True)
                dsc = dsc + dm * lam
                dxdt_ref[:, hs] = _tn(m.astype(BF), dyh_b) + dxdt_g[:, hh * SSD_HEAD_DIM:(hh + 1) * SSD_HEAD_DIM]
            dsc_b = dsc.astype(BF)
            d_c.append(dcg + _nn(dsc_b, bg))
            d_b.append(dbg + _tn(dsc_b, cg))
            g_ref[g] = d_hprev + c_dec * gg

        eye = (_iota((CHUNK, CHUNK), 0) == _iota((CHUNK, CHUNK), 1)).astype(BF)
        d_acum = d_acum - _exact_nt_left(eye, d_acum_t)
        is_last = (_iota((CHUNK, 1), 0) == CHUNK - 1).astype(F32)
        d_acum = d_acum + is_last * d_alast
        triu = (_iota((CHUNK, CHUNK), 0) <= _iota((CHUNK, CHUNK), 1)).astype(BF)
        d_da = _exact_nn_left(triu, d_acum)
        dt = r["dt"]
        ga_ref[...] += jnp.sum(d_da * dt, 0, keepdims=True)
        dxdt = dxdt_ref[...]
        reduce16 = _head_reduce_matrix(SSD_INNER, SSD_HEADS)
        d_dt = d_da * r["a"] + _exact_nn(dxdt * xs, reduce16)
        d_udt = d_dt * _sigmoid(r["dtp"])
        gdtb_ref[...] += jnp.sum(d_udt, 0, keepdims=True)
        dudt_ref[...] = jnp.zeros_like(dudt_ref)
        dudt_ref[:, 0:SSD_HEADS] = d_udt.astype(BF)
        pre, sg = r["pre"], r["sg"]
        dsilu = sg * (1.0 + pre * (1.0 - sg))
        dpre_ref[:, 0:SSD_INNER] = (dsk_ref[...] * dyv_ref[...] + dxdt * dt_e) * dsilu[:, 0:SSD_INNER]
        for g in range(2):
            bs = slice(SSD_INNER + g * SSD_STATE, SSD_INNER + (g + 1) * SSD_STATE)
            cs = slice(SSD_INNER + (2 + g) * SSD_STATE, SSD_INNER + (3 + g) * SSD_STATE)
            dpre_ref[:, bs] = d_b[g] * dsilu[:, bs]
            dpre_ref[:, cs] = d_c[g] * dsilu[:, cs]

    t = nb * seq
    vec = _const_spec((1, SSD_INNER))
    small = _const_spec((1, SSD_HEADS))
    return pl.pallas_call(
        body, name="ssd_bwd", grid=(nb, nc),
        in_specs=specs + [pl.BlockSpec((CHUNK, SSD_INNER), lambda b, c: (row(b, c), 0)),
                          pl.BlockSpec((CHUNK, SSD_INNER), lambda b, c: (row(b, c), 0)),
                          pl.BlockSpec((1, 1, 2, SSD_STATE, GROUP_W), lambda b, c: (b, cidx(c), 0, 0, 0)),
                          _const_spec((4, CONV_CH)), _const_spec((1, CONV_CH)), small, small, vec, vec],
        out_specs=[pl.BlockSpec((CHUNK, SSD_INNER), lambda b, c: (row(b, c), 0)),
                   pl.BlockSpec((CHUNK, CONV_CH), lambda b, c: (row(b, c), 0)),
                   pl.BlockSpec((CHUNK, 128), lambda b, c: (row(b, c), 0)),
                   vec, vec, small, small],
        out_shape=[jax.ShapeDtypeStruct((t, SSD_INNER), BF), jax.ShapeDtypeStruct((t, CONV_CH), F32),
                   jax.ShapeDtypeStruct((t, 128), BF), jax.ShapeDtypeStruct((1, SSD_INNER), F32),
                   jax.ShapeDtypeStruct((1, SSD_INNER), F32), jax.ShapeDtypeStruct((1, SSD_HEADS), F32),
                   jax.ShapeDtypeStruct((1, SSD_HEADS), F32)],
        scratch_shapes=[pltpu.VMEM((2, SSD_STATE, GROUP_W), F32), pltpu.VMEM((CHUNK, SSD_INNER), F32),
                        pltpu.VMEM((CHUNK, SSD_INNER), F32)],
        compiler_params=_cp(("arbitrary", "arbitrary"), 48),
    )(*_pin(proj, proj, proj, proj, proj, d_yssd, yssm, h_prev, conv_w, conv_b, dt_bias, a_log, dskip_e, g_ssd))


def _grad_w_in_t(d_upool, d_z, d_uxbc, d_udt, u1):
    t, d = u1.shape
    tk = 512
    n_z, n_x = SSD_INNER // tk, CONV_CH // tk

    def body(p_ref, z_ref, x_ref, dt_ref, u_ref, o_ref):
        i = pl.program_id(0)

        @pl.when(i == 0)
        def _():
            o_ref[...] = _tn(p_ref[...], u_ref[...]).astype(BF)

        @pl.when((i >= 1) & (i < 1 + n_z))
        def _():
            o_ref[...] = _tn(z_ref[...], u_ref[...]).astype(BF)

        @pl.when((i >= 1 + n_z) & (i < 1 + n_z + n_x))
        def _():
            o_ref[...] = _tn(x_ref[...], u_ref[...]).astype(BF)

        @pl.when(i == 1 + n_z + n_x)
        def _():
            o_ref[0:128, :] = _tn(dt_ref[...], u_ref[...]).astype(BF)

    return pl.pallas_call(
        body, name="grad_w_in", grid=(2 + n_z + n_x,),
        in_specs=[pl.BlockSpec((t, tk), lambda i: (0, 0)),
                  pl.BlockSpec((t, tk), lambda i: (0, jnp.clip(i - 1, 0, n_z - 1))),
                  pl.BlockSpec((t, tk), lambda i: (0, jnp.clip(i - 1 - n_z, 0, n_x - 1))),
                  pl.BlockSpec((t, 128), lambda i: (0, 0)), pl.BlockSpec((t, d), lambda i: (0, 0))],
        out_specs=pl.BlockSpec((tk, d), lambda i: (i, 0)),
        out_shape=jax.ShapeDtypeStruct((IN_PAD, d), BF),
        compiler_params=_cp(("parallel",), 56))(*_pin(d_upool, d_z, d_uxbc, d_udt, u1))


def _conv_bwd(d_pre, proj, conv_w, nb, seq):
    ts = _pool_tile(seq)
    nt = seq // ts
    hb = ts // CONV_HALO
    last_block = nb * seq // CONV_HALO - 1
    n_ext = ts + CONV_HALO

    def body(dp_ref, dnext_ref, u_ref, uprev_ref, cw_ref, du_ref, gw_ref, gb_ref):
        b = pl.program_id(0)
        i = pl.program_id(1)

        @pl.when((b == 0) & (i == 0))
        def _():
            gw_ref[...] = jnp.zeros_like(gw_ref)
            gb_ref[...] = jnp.zeros_like(gb_ref)

        dp = dp_ref[...]
        ext_d = jnp.concatenate([dp, jnp.where(i == nt - 1, 0.0, dnext_ref[...])], 0)
        ext_u = jnp.concatenate([jnp.where(i == 0, 0.0, uprev_ref[...]), u_ref[...]], 0)
        cw = cw_ref[...]
        du = dp * cw[3:4]
        gw_ref[3:4, :] += jnp.sum(dp * u_ref[...], 0, keepdims=True)
        for k in (2, 1, 0):
            j = 3 - k
            du = du + pltpu.roll(ext_d, n_ext - j, 0)[:ts] * cw[k:k + 1]
            gw_ref[k:k + 1, :] += jnp.sum(dp * pltpu.roll(ext_u, j, 0)[CONV_HALO:], 0, keepdims=True)
        gb_ref[...] += jnp.sum(dp, 0, keepdims=True)
        du_ref[...] = du.astype(BF)

    return pl.pallas_call(
        body, name="conv_bwd", grid=(nb, nt),
        in_specs=[pl.BlockSpec((ts, CONV_CH), lambda b, i: (b * nt + i, 0)),
                  pl.BlockSpec((CONV_HALO, CONV_CH), lambda b, i: (jnp.minimum((b * nt + i + 1) * hb, last_block), 0)),
                  pl.BlockSpec((ts, CONV_CH), lambda b, i: (b * nt + i, 1)),
                  pl.BlockSpec((CONV_HALO, CONV_CH), lambda b, i: (jnp.maximum((b * nt + i) * hb - 1, 0), 1)),
                  pl.BlockSpec((4, CONV_CH), lambda b, i: (0, 0))],
        out_specs=[pl.BlockSpec((ts, CONV_CH), lambda b, i: (b * nt + i, 0)),
                   pl.BlockSpec((8, CONV_CH), lambda b, i: (0, 0)), pl.BlockSpec((1, CONV_CH), lambda b, i: (0, 0))],
        out_shape=[jax.ShapeDtypeStruct((nb * seq, CONV_CH), BF), jax.ShapeDtypeStruct((8, CONV_CH), F32),
                   jax.ShapeDtypeStruct((1, CONV_CH), F32)],
        compiler_params=_cp(("arbitrary", "arbitrary"), 48))(*_pin(d_pre, d_pre, proj, proj, conv_w))


def _in_proj_bwd(d_parts, w_in_t, x, dh1, mod3, g_mix, seq, token):
    t, d = x.shape
    nb = t // seq
    tm = min(256, seq)
    tps = seq // tm

    widths = [p.shape[1] for p in d_parts]

    def body(d0_ref, d1_ref, d2_ref, d3_ref, w_ref, x_ref, dh1_ref, mod_ref, g_ref, tok_ref, gx_ref, acc_ref, gg_ref):
        i = pl.program_id(0)
        du = None
        off = 0
        for p_ref, wd in zip((d0_ref, d1_ref, d2_ref, d3_ref), widths):
            part = _nn(p_ref[...], w_ref[off:off + wd, :])
            du = part if du is None else du + part
            off += wd
        xv = x_ref[...]
        r = lax.rsqrt(jnp.mean(xv * xv, -1, keepdims=True) + EPS)
        hh = xv * r
        n1 = hh * g_ref[...]
        dn1 = du * (1.0 + mod_ref[0, 1:2, :])
        dhat = dn1 * g_ref[...]
        gx_ref[...] = dh1_ref[...] + r * (dhat - hh * jnp.mean(dhat * hh, -1, keepdims=True))

        @pl.when(i == 0)
        def _():
            gg_ref[...] = jnp.zeros_like(gg_ref)

        @pl.when(i % tps == 0)
        def _():
            acc_ref[...] = jnp.zeros_like(acc_ref)

        gg_ref[...] += jnp.sum(dn1 * hh, 0, keepdims=True)
        acc_ref[0, 0:1, :] += jnp.sum(du, 0, keepdims=True)
        acc_ref[0, 1:2, :] += jnp.sum(du * n1, 0, keepdims=True)

    row = lambda i: (i, 0)
    vec = pl.BlockSpec((1, d), lambda i: (0, 0))
    return pl.pallas_call(
        body, name="in_proj_bwd", grid=(t // tm,),
        in_specs=[pl.BlockSpec((tm, wd), row) for wd in widths] +
                 [pl.BlockSpec(w_in_t.shape, lambda i: (0, 0)), pl.BlockSpec((tm, d), row),
                  pl.BlockSpec((tm, d), row), pl.BlockSpec((1, N_MOD, d), lambda i: (i // tps, 0, 0)), vec, _token_spec()],
        out_specs=[pl.BlockSpec((tm, d), row), pl.BlockSpec((1, 8, d), lambda i: (i // tps, 0, 0)), vec],
        out_shape=[jax.ShapeDtypeStruct((t, d), F32), jax.ShapeDtypeStruct((nb, 8, d), F32),
                   jax.ShapeDtypeStruct((1, d), F32)],
        compiler_params=_cp(("arbitrary",), 48))(*_pin(*d_parts, w_in_t, x, dh1, mod3, g_mix, token))


_VEC_LAYOUT = (("g_mix", 1024), ("conv_b", 1536), ("g_ssd", 1024), ("pool_scale", 512), ("g_mlp", 1024),
               ("g_final", 1024), ("dt_bias", 128), ("a_log", 128), ("d_skip_lanes", 1024), ("sq_err", 1024))
_VEC_OFFSET = {}
_off = 0
for _name, _n in _VEC_LAYOUT:
    _VEC_OFFSET[_name] = _off
    _off += _n
_VEC_LANES = _off
_SMALL_PARAMS = ("b_ada", "g_mix", "conv_w", "conv_b", "dt_bias", "a_log", "d_skip", "g_ssd", "w_pool", "pool_scale",
                 "g_mlp", "g_final")


def _pack_vec(parts):
    cols = []
    for name, n in _VEC_LAYOUT:
        v = parts[name]
        if v.shape[1] < n:
            v = jnp.pad(v, ((0, 0), (0, n - v.shape[1])))
        cols.append(v)
    return jnp.concatenate(cols, 1)


def _small_adam(vec_all, wpool_all, convw_all, dmod_all, params):
    names = _SMALL_PARAMS
    nin = 4 + 3 * len(names)

    def body(*refs):
        vec_ref, wp_ref, cw_ref, dm_ref = refs[:4]
        prm = {n: refs[4 + 3 * i:7 + 3 * i] for i, n in enumerate(names)}
        loss_ref = refs[nin]
        outs = {n: refs[nin + 1 + 4 * i:nin + 5 + 4 * i] for i, n in enumerate(names)}
        vsum = vec_ref[0]
        for s in range(1, N_DEV):
            vsum = vsum + vec_ref[s]

        def lanes(name, n):
            off = _VEC_OFFSET[name]
            return vsum[:, off:off + n]

        grads = {n: lanes(n, prm[n][0].shape[1]) for n in ("g_mix", "conv_b", "g_ssd", "pool_scale", "g_mlp", "g_final", "dt_bias")}
        grads["a_log"] = lanes("a_log", SSD_HEADS) * (-jnp.exp(prm["a_log"][0][...]))
        per_lane = jnp.broadcast_to(lanes("d_skip_lanes", SSD_INNER), (8, SSD_INNER))
        grads["d_skip"] = _exact_nn(per_lane, _head_reduce_matrix(SSD_INNER, SSD_HEADS))[0:1]
        gwp = wp_ref[0].astype(F32)
        gcw = cw_ref[0]
        gb = jnp.sum(dm_ref[0], 0, keepdims=True)
        for s in range(1, N_DEV):
            gwp = gwp + wp_ref[s].astype(F32)
            gcw = gcw + cw_ref[s]
            gb = gb + jnp.sum(dm_ref[s], 0, keepdims=True)
        grads["w_pool"] = gwp
        grads["conv_w"] = gcw[0:4]
        grads["b_ada"] = gb
        total = jnp.sum(lanes("sq_err", D_MODEL), 1, keepdims=True) * (0.5 / D_MODEL)
        loss_ref[...] = jnp.broadcast_to(total, loss_ref.shape)
        for n in names:
            w_ref, m_ref, v_ref = prm[n]
            g = grads[n]
            d, m2, v2 = _adam_math(w_ref[...], g, m_ref[...], v_ref[...])
            g_ref, d_ref, m2_ref, v2_ref = outs[n]
            g_ref[...] = g
            d_ref[...] = d
            m2_ref[...] = m2
            v2_ref[...] = v2

    flat = [vec_all, wpool_all, convw_all, dmod_all]
    out_shape = [jax.ShapeDtypeStruct((1, 128), F32)]
    for n in names:
        flat += list(params[n])
        out_shape += [jax.ShapeDtypeStruct(params[n][0].shape, F32)] * 4
    vm = pl.BlockSpec(memory_space=pltpu.VMEM)
    res = pl.pallas_call(body, name="small_adam", out_shape=out_shape, in_specs=[vm] * len(flat),
                         out_specs=[vm] * len(out_shape), compiler_params=_cp(vmem_mb=48))(*flat)
    return res[0], {n: res[1 + 4 * i:5 + 4 * i] for i, n in enumerate(names)}


_WEIGHTS = ("w_ada", "b_ada", "g_mix", "w_in", "conv_w", "conv_b", "dt_bias", "a_log", "d_skip", "g_ssd", "w_pool",
            "pool_scale", "w_out", "g_mlp", "w_up", "w_down", "g_final")


def _local_step(x2, tg2, mod3, seq, w_in_t, first_token, weights_later, start_reduce, conv_w_full, sp):
    t, d = x2.shape
    nb = t // seq
    dskip_e = jnp.repeat(sp["d_skip"], SSD_HEAD_DIM, axis=1)
    proj, u1 = _in_proj(x2, mod3, sp["g_mix"], w_in_t, seq, first_token)
    y_pool, p = _pool_fwd(proj, sp["w_pool"], sp["pool_scale"], nb, seq)
    y_ssd, yssm, h_prev = _ssd_fwd(proj, conv_w_full, sp["conv_b"], sp["dt_bias"], sp["a_log"], dskip_e, sp["g_ssd"], nb, seq)
    w_out_f, w_up4, w_down4 = weights_later(y_ssd)
    w_down_f = w_down4.reshape(D_FF, d)
    h1, o, u2 = _out_proj(y_pool, y_ssd, w_out_f, x2, mod3, sp["g_mlp"], seq)
    a_up = _mlp_up(u2, w_up4)
    d_dn, dh2, sq, gg_final, d_gf = _mlp_down_loss(a_up, w_down_f, h1, mod3, sp["g_final"], tg2, seq)

    gw_down = _tn_matmul(a_up, d_dn, 512, d, "grad_w_down", square_relu=True)
    tok = start_reduce("w_down", gw_down.reshape(N_CHIPS, D_FF // N_CHIPS, d))
    d_a = _mlp_down_bwd(d_dn, w_down4, a_up, tok)
    gw_up4 = _tn_matmul(u2, d_a, 512, d, "grad_w_up", out3=True)
    tok = start_reduce("w_up", gw_up4)
    dh1, d_o, accf, gg_mlp = _mlp_up_bwd(d_a, w_up4, h1, dh2, o, mod3, sp["g_mlp"], seq, tok)
    gw_out_pool = _tn_matmul(y_pool, d_o, 512, d, "grad_w_out_pool")
    gw_out_ssd = _tn_matmul(y_ssd, d_o, 512, d, "grad_w_out_ssd")
    gw_out = jnp.concatenate([gw_out_pool, gw_out_ssd], 0)
    tok = start_reduce("w_out", gw_out.reshape(N_CHIPS, gw_out.shape[0] // N_CHIPS, d))
    d_ypool, d_yssd = _out_proj_bwd(d_o, w_out_f, tok)
    d_upool, gw_pool, g_ps = _pool_bwd(d_ypool, p, sp["w_pool"], sp["pool_scale"], nb, seq)
    d_z, d_pre, d_udt, gg_ssd, gdsk, ga, gdtb = _ssd_bwd(proj, d_yssd, yssm, h_prev, conv_w_full, sp["conv_b"],
                                                        sp["dt_bias"], sp["a_log"], dskip_e, sp["g_ssd"], nb, seq)
    d_uxbc, gconvw, gconvb = _conv_bwd(d_pre, proj, conv_w_full, nb, seq)
    gw_in_t = _grad_w_in_t(d_upool, d_z, d_uxbc, d_udt, u1)
    tok = start_reduce("w_in", gw_in_t[:IN_WIDTH].reshape(N_CHIPS, IN_WIDTH // N_CHIPS, d))
    gx, accm, gg_mix = _in_proj_bwd([d_upool, d_z, d_uxbc, d_udt], w_in_t, x2, dh1, mod3, sp["g_mix"], seq, tok)

    d_mod = jnp.concatenate([accm[:, 0], accm[:, 1], accf[:, 2], accf[:, 0], accf[:, 1], d_gf[:, 0]], 1)
    vec = _pack_vec({"g_mix": gg_mix, "conv_b": gconvb, "g_ssd": gg_ssd, "pool_scale": g_ps, "g_mlp": gg_mlp,
                     "g_final": gg_final, "dt_bias": gdtb, "a_log": ga, "d_skip_lanes": gdsk, "sq_err": sq})
    return gx, d_mod, vec, gw_pool, gconvw


def kernel(x, c, w_ada, b_ada, g_mix, w_in, conv_w, conv_b, dt_bias, a_log, d_skip, g_ssd, w_pool, pool_scale, w_out, g_mlp, w_up, w_down, g_final, loss_target, m_w_ada, m_b_ada, m_g_mix, m_w_in, m_conv_w, m_conv_b, m_dt_bias, m_a_log, m_d_skip, m_g_ssd, m_w_pool, m_pool_scale, m_w_out, m_g_mlp, m_w_up, m_w_down, m_g_final, v_w_ada, v_b_ada, v_g_mix, v_w_in, v_conv_w, v_conv_b, v_dt_bias, v_a_log, v_d_skip, v_g_ssd, v_w_pool, v_pool_scale, v_w_out, v_g_mlp, v_w_up, v_w_down, v_g_final):
    nb, seq, d = x.shape
    t = nb * seq
    xi, yi, ci = _mesh_pos()
    chip = 2 * xi + yi
    me = 4 * xi + 2 * yi + ci
    ada_cols = w_ada.shape[2]
    conv_cols = conv_w.shape[2]
    in_cols = w_in.shape[2]
    w_in_s, m_w_in_s, v_w_in_s = w_in[0].T, m_w_in[0].T, v_w_in[0].T

    c8, convw8 = _all_gather_small([c, conv_w[0]])
    c_all = c8.reshape(N_DEV * nb, d)
    conv_w_full = convw8[0::2].transpose(1, 0, 2).reshape(4, N_CHIPS * conv_cols)
    b_shard = lax.dynamic_slice(b_ada, (0, chip * ada_cols), (1, ada_cols))
    mod_part, c_act = _ada_mod(c_all, w_ada[0], b_shard)
    (mod8,) = _all_gather_small([mod_part])
    mod_all = mod8[0::2].transpose(1, 0, 2).reshape(N_DEV * nb, N_CHIPS * ada_cols)
    mod3 = lax.dynamic_slice(mod_all, (nb * me, 0), (nb, N_CHIPS * ada_cols)).reshape(nb, N_MOD, d)

    (w_in4,) = _weight_all_gather([w_in_s.astype(BF)])
    w_in_t = jnp.pad(w_in4.reshape(N_CHIPS * in_cols, d), ((0, IN_PAD - N_CHIPS * in_cols), (0, 0)))
    later = [w_out[0].astype(BF), w_up[0].astype(BF), w_down[0].astype(BF)]
    g_send, g_recv, g_src, g_land, first_token = _ici_start(
        later, [jax.ShapeDtypeStruct((N_CHIPS,) + s.shape, BF) for s in later], _gather_sent, _gather_landing, "gather_start",
        after=w_in4)

    def weights_later(after):
        shards, lands = _ici_wait(g_send, g_recv, g_src, g_land, after, _gather_sent, _gather_landing, "gather_wait")
        w_out4, w_up4, w_down4 = _gather_finish(lands, shards)
        return w_out4.reshape(N_CHIPS * w_out.shape[1], d), w_up4, w_down4

    pending = {}

    def start_reduce(name, grad4):
        own, got = _grad_to_sibling([grad4])
        part = _add_pair(own[0], got[0])
        pending[name] = _ici_start([part], [part], _reduce_sent, _reduce_landing, "reduce_start_" + name, after=got[0])
        return pending[name][4]

    sp = dict(g_mix=g_mix, conv_b=conv_b, dt_bias=dt_bias, a_log=a_log, d_skip=d_skip, g_ssd=g_ssd,
              w_pool=w_pool[0], pool_scale=pool_scale, g_mlp=g_mlp, g_final=g_final.reshape(1, d))
    gx, d_mod, vec, gw_pool, gconvw = _local_step(
        x.reshape(t, d), loss_target.reshape(t, d), mod3, seq, w_in_t, first_token, weights_later, start_reduce, conv_w_full, sp)

    halves = []
    for name in ("w_in", "w_out", "w_up", "w_down"):
        r_send, r_recv, r_src, r_land, _ = pending[name]
        own, recv = _ici_wait(r_send, r_recv, r_src, r_land, gx, _reduce_sent, _reduce_landing, "reduce_wait_" + name)
        halves.append(_sum_slots(recv[0], own[0]))
    g_in, g_out, g_up, g_down = _halves_exchange(halves)

    vec8, wpool8, convw8g, dmod8 = _all_gather_small(
        [vec, gw_pool.reshape(4 * POOL_GROUP, POOL_GROUP).astype(BF), gconvw, d_mod])
    convw8s = lax.dynamic_slice(convw8g, (0, 0, chip * conv_cols), (N_DEV, 8, conv_cols))
    m_in = dict(b_ada=m_b_ada, g_mix=m_g_mix, conv_w=m_conv_w[0], conv_b=m_conv_b, dt_bias=m_dt_bias, a_log=m_a_log,
                d_skip=m_d_skip, g_ssd=m_g_ssd, w_pool=m_w_pool.reshape(4 * POOL_GROUP, POOL_GROUP), pool_scale=m_pool_scale,
                g_mlp=m_g_mlp, g_final=m_g_final.reshape(1, d))
    v_in = dict(b_ada=v_b_ada, g_mix=v_g_mix, conv_w=v_conv_w[0], conv_b=v_conv_b, dt_bias=v_dt_bias, a_log=v_a_log,
                d_skip=v_d_skip, g_ssd=v_g_ssd, w_pool=v_w_pool.reshape(4 * POOL_GROUP, POOL_GROUP), pool_scale=v_pool_scale,
                g_mlp=v_g_mlp, g_final=v_g_final.reshape(1, d))
    w_small = dict(sp, b_ada=b_ada, conv_w=conv_w[0], w_pool=w_pool.reshape(4 * POOL_GROUP, POOL_GROUP))
    loss_row, small = _small_adam(vec8, wpool8, convw8s, dmod8, {n: (w_small[n], m_in[n], v_in[n]) for n in _SMALL_PARAMS})

    dmod_all = dmod8.reshape(N_DEV * nb, N_CHIPS * ada_cols)
    dmod_cols = lax.dynamic_slice(dmod_all, (0, chip * ada_cols), (N_DEV * nb, ada_cols))
    res = {n: tuple(r.reshape(w.shape) for r in small[n])
           for n, w in (("b_ada", b_ada), ("g_mix", g_mix), ("conv_w", conv_w), ("conv_b", conv_b), ("dt_bias", dt_bias),
                        ("a_log", a_log), ("d_skip", d_skip), ("g_ssd", g_ssd), ("w_pool", w_pool), ("pool_scale", pool_scale),
                        ("g_mlp", g_mlp), ("g_final", g_final))}
    g_ada, d_ada, m_ada, v_ada = _adam_ada(c_act.T.astype(BF), dmod_cols, w_ada[0], m_w_ada[0], v_w_ada[0])
    res["w_ada"] = (g_ada[None], d_ada[None], m_ada[None], v_ada[None])
    dl, m2, v2 = _adam_big(g_in, w_in_s, m_w_in_s, v_w_in_s)
    res["w_in"] = (g_in.T[None], dl.T[None], m2.T[None], v2.T[None])
    for n, g, w, m, v in (("w_out", g_out, w_out, m_w_out, v_w_out), ("w_up", g_up, w_up, m_w_up, v_w_up),
                          ("w_down", g_down, w_down, m_w_down, v_w_down)):
        dl, m2, v2 = _adam_big(g, w[0], m[0], v[0])
        res[n] = (g[None], dl[None], m2[None], v2[None])

    loss = loss_row[0, 0]
    return (loss, gx.reshape(nb, seq, d), *[res[n][0] for n in _WEIGHTS], *[res[n][1] for n in _WEIGHTS],
            *[res[n][2] for n in _WEIGHTS], *[res[n][3] for n in _WEIGHTS])
```

```python
import functools

import jax
import jax.numpy as jnp
from jax import lax
from jax.experimental import pallas as pl
from jax.experimental.pallas import tpu as pltpu

F32 = jnp.float32
BF = jnp.bfloat16
MESH = pl.DeviceIdType.MESH

EPS = 1e-5
D_MODEL = 1024
POOL_WIDTH = 512
POOL_WINDOWS = (2, 4, 8, 16)
POOL_GROUP = 128
SSD_INNER = 1024
SSD_HEADS = 16
SSD_HEAD_DIM = 64
SSD_STATE = 128
GROUP_W = 512
CHUNK = 128
CONV_CH = 1536
OFF_Z = 512
OFF_XBC = 1536
OFF_DT = 3072
IN_WIDTH = 3088
IN_PAD = 3200
D_FF = 4096
N_MOD = 6
N_CHIPS = 4
N_DEV = 8
HALO = 16
CONV_HALO = 8

ADAM_LR = 0.001
ADAM_B1 = 0.9
ADAM_B2 = 0.999
ADAM_EPS = 1e-08
ADAM_WD = 0.01
ADAM_STEP = 10

VMEM_BYTES_V7X = 64 * 1024 * 1024


def _cp(semantics=None, vmem_mb=48, **kw):
    args = dict(vmem_limit_bytes=vmem_mb * 1024 * 1024, **kw)
    if semantics is not None:
        args["dimension_semantics"] = semantics
    return pltpu.CompilerParams(**args)


def _out(shape, dtype):
    return pltpu.HBM(shape, dtype)


def _pin(*arrays):
    return [pltpu.with_memory_space_constraint(a, pltpu.HBM) for a in arrays]


def _nn(a, b):
    return jnp.dot(a, b, preferred_element_type=F32)


def _nt(a, b):
    return lax.dot_general(a, b, (((1,), (1,)), ((), ())), preferred_element_type=F32)


def _tn(a, b):
    return lax.dot_general(a, b, (((0,), (0,)), ((), ())), preferred_element_type=F32)


def _split3(v):
    hi = v.astype(BF)
    r1 = v - hi.astype(F32)
    mid = r1.astype(BF)
    lo = (r1 - mid.astype(F32)).astype(BF)
    return hi, mid, lo


def _exact_nn(v, m01):
    hi, mid, lo = _split3(v)
    return _nn(hi, m01) + _nn(mid, m01) + _nn(lo, m01)


def _exact_nn_left(m01, v):
    hi, mid, lo = _split3(v)
    return _nn(m01, hi) + _nn(m01, mid) + _nn(m01, lo)


def _exact_nt_left(m01, v):
    hi, mid, lo = _split3(v)
    return _nt(m01, hi) + _nt(m01, mid) + _nt(m01, lo)


def _sigmoid(v):
    return 1.0 / (1.0 + jnp.exp(-v))


def _iota(shape, dim):
    return lax.broadcasted_iota(jnp.int32, shape, dim)


def _head_expand_matrix(heads, width):
    return (_iota((heads, width), 1) // SSD_HEAD_DIM == _iota((heads, width), 0)).astype(BF)


def _head_reduce_matrix(width, heads):
    return (_iota((width, heads), 0) // SSD_HEAD_DIM == _iota((width, heads), 1)).astype(BF)


def _mesh_pos():
    return lax.axis_index("x"), lax.axis_index("y"), lax.axis_index("c")


def _flip(v, bit):
    return v + bit - 2 * bit * v


def _all_gather_small(arrays):
    n = len(arrays)

    def body(*refs):
        in_refs, out_refs = refs[:n], refs[n:2 * n]
        send_sems, recv_sems, local_sems = refs[2 * n:]
        x, y, c = _mesh_pos()
        me = 4 * x + 2 * y + c
        local = []
        for a in range(n):
            cp = pltpu.make_async_copy(in_refs[a], out_refs[a].at[me], local_sems.at[a])
            cp.start()
            local.append(cp)
        sends = []
        for k in range(1, N_DEV):
            peer = (_flip(x, (k >> 2) & 1), _flip(y, (k >> 1) & 1), _flip(c, k & 1))
            for a in range(n):
                cp = pltpu.make_async_remote_copy(
                    src_ref=in_refs[a], dst_ref=out_refs[a].at[me],
                    send_sem=send_sems.at[a, k], recv_sem=recv_sems.at[a, k],
                    device_id=peer, device_id_type=MESH)
                cp.start()
                sends.append(cp)
        for k in range(1, N_DEV):
            px, py, pc = _flip(x, (k >> 2) & 1), _flip(y, (k >> 1) & 1), _flip(c, k & 1)
            src = 4 * px + 2 * py + pc
            for a in range(n):
                pltpu.make_async_remote_copy(
                    src_ref=in_refs[a], dst_ref=out_refs[a].at[src],
                    send_sem=send_sems.at[a, k], recv_sem=recv_sems.at[a, k],
                    device_id=(px, py, pc), device_id_type=MESH).wait_recv()
        for cp in sends:
            cp.wait_send()
        for cp in local:
            cp.wait()

    vm = pl.BlockSpec(memory_space=pltpu.VMEM)
    return pl.pallas_call(
        body, name="all_gather_small",
        out_shape=[jax.ShapeDtypeStruct((N_DEV,) + a.shape, a.dtype) for a in arrays],
        in_specs=[vm] * n, out_specs=[vm] * n,
        scratch_shapes=[pltpu.SemaphoreType.DMA((n, N_DEV)), pltpu.SemaphoreType.DMA((n, N_DEV)),
                        pltpu.SemaphoreType.DMA((n,))],
        compiler_params=_cp(vmem_mb=32),
    )(*arrays)


def _weight_all_gather(shards):
    n = len(shards)
    any_spec = _HBM

    def body(*refs):
        in_refs, out_refs = refs[:n], refs[n:2 * n]
        send_sems, recv_sems, fsend_sems, frecv_sems, local_sems = refs[2 * n:]
        x, y, c = _mesh_pos()
        chip = 2 * x + y
        local, sends = [], []
        for a in range(n):
            cp = pltpu.make_async_copy(in_refs[a], out_refs[a].at[chip], local_sems.at[a])
            cp.start()
            local.append(cp)

        def half(a, which):
            hc = shards[a].shape[1] // 2
            return pl.ds(pl.multiple_of(which * hc, 128), hc)

        for j in range(1, N_CHIPS):
            px, py = _flip(x, (j >> 1) & 1), _flip(y, j & 1)
            for a in range(n):
                cp = pltpu.make_async_remote_copy(
                    src_ref=in_refs[a].at[:, half(a, c)], dst_ref=out_refs[a].at[chip, :, half(a, c)],
                    send_sem=send_sems.at[a, j], recv_sem=recv_sems.at[a, j],
                    device_id=(px, py, c), device_id_type=MESH)
                cp.start()
                sends.append(cp)
        for j in range(1, N_CHIPS):
            px, py = _flip(x, (j >> 1) & 1), _flip(y, j & 1)
            src = 2 * px + py
            for a in range(n):
                landed = out_refs[a].at[src, :, half(a, c)]
                pltpu.make_async_remote_copy(
                    src_ref=landed, dst_ref=landed, send_sem=send_sems.at[a, j], recv_sem=recv_sems.at[a, j],
                    device_id=(px, py, c), device_id_type=MESH).wait_recv()
                cp = pltpu.make_async_remote_copy(
                    src_ref=landed, dst_ref=landed, send_sem=fsend_sems.at[a, j], recv_sem=frecv_sems.at[a, j],
                    device_id=(x, y, 1 - c), device_id_type=MESH)
                cp.start()
                sends.append(cp)
        for j in range(1, N_CHIPS):
            px, py = _flip(x, (j >> 1) & 1), _flip(y, j & 1)
            src = 2 * px + py
            for a in range(n):
                other = out_refs[a].at[src, :, half(a, 1 - c)]
                pltpu.make_async_remote_copy(
                    src_ref=other, dst_ref=other, send_sem=fsend_sems.at[a, j], recv_sem=frecv_sems.at[a, j],
                    device_id=(x, y, 1 - c), device_id_type=MESH).wait_recv()
        for cp in sends:
            cp.wait_send()
        for cp in local:
            cp.wait()

    return pl.pallas_call(
        body, name="weight_all_gather",
        out_shape=[_out((N_CHIPS,) + s.shape, s.dtype) for s in shards],
        in_specs=[any_spec] * n, out_specs=[any_spec] * n,
        scratch_shapes=[pltpu.SemaphoreType.DMA((n, N_CHIPS))] * 4 + [pltpu.SemaphoreType.DMA((n,))],
        compiler_params=_cp(vmem_mb=16),
    )(*shards)


_HBM = pl.BlockSpec(memory_space=pltpu.HBM)
_SEM = pl.BlockSpec(memory_space=pltpu.SEMAPHORE)
_DATAFLOW = pltpu.SideEffectType.DATAFLOW_SIDE_EFFECTING


def _peer_chip(x, y, j):
    return _flip(x, (j >> 1) & 1), _flip(y, j & 1)


def _ici_start(srcs, land_shapes, sent, landing, name, after):
    n = len(srcs)

    def body(*refs):
        src_refs, land_refs = refs[:n], refs[n:2 * n]
        send_sems, recv_sems = refs[2 * n + 1], refs[2 * n + 2]
        token = refs[-1]
        x, y, c = _mesh_pos()
        for j in range(1, N_CHIPS):
            px, py = _peer_chip(x, y, j)
            for a in range(n):
                pltpu.make_async_remote_copy(
                    src_ref=sent(src_refs[a], c, 2 * px + py), dst_ref=landing(land_refs[a], c, 2 * x + y),
                    send_sem=send_sems.at[a * (N_CHIPS - 1) + j - 1], recv_sem=recv_sems.at[a * (N_CHIPS - 1) + j - 1],
                    device_id=(px, py, c), device_id_type=MESH).start()
        token[...] = jnp.zeros_like(token)

    sems = pltpu.SemaphoreType.DMA((n * (N_CHIPS - 1),))
    lands = [pltpu.with_memory_space_constraint(lax.empty(s.shape, s.dtype), pltpu.HBM) for s in land_shapes]
    outs = pl.pallas_call(
        body, name=name,
        out_shape=(sems, sems, *[pltpu.HBM(s.shape, s.dtype) for s in srcs],
                   *[pltpu.HBM(s.shape, s.dtype) for s in land_shapes], jax.ShapeDtypeStruct((8, 128), F32)),
        in_specs=[_HBM] * (2 * n + 1), out_specs=[_SEM, _SEM] + [_HBM] * (2 * n) + [pl.BlockSpec(memory_space=pltpu.VMEM)],
        input_output_aliases={i: 2 + i for i in range(2 * n)},
        compiler_params=pltpu.CompilerParams(has_side_effects=_DATAFLOW),
    )(*_pin(*srcs), *lands, *_pin(after))
    return outs[0], outs[1], outs[2:2 + n], outs[2 + n:2 + 2 * n], outs[-1]


def _ici_wait(send_sems, recv_sems, src_thru, land_thru, after, sent, landing, name):
    n = len(src_thru)

    def body(*refs):
        src_refs, land_refs = refs[:n], refs[n:2 * n]
        send_sems, recv_sems = refs[2 * n], refs[2 * n + 1]
        x, y, c = _mesh_pos()
        for j in range(1, N_CHIPS):
            px, py = _peer_chip(x, y, j)
            for a in range(n):
                cp = pltpu.make_async_remote_copy(
                    src_ref=sent(src_refs[a], c, 2 * px + py), dst_ref=landing(land_refs[a], c, 2 * px + py),
                    send_sem=send_sems.at[a * (N_CHIPS - 1) + j - 1], recv_sem=recv_sems.at[a * (N_CHIPS - 1) + j - 1],
                    device_id=(px, py, c), device_id_type=MESH)
                cp.wait_send()
                cp.wait_recv()

    outs = pl.pallas_call(
        body, name=name,
        out_shape=tuple(pltpu.HBM(s.shape, s.dtype) for s in (*src_thru, *land_thru)),
        in_specs=[_HBM] * (2 * n) + [_SEM, _SEM, _HBM], out_specs=[_HBM] * (2 * n),
        input_output_aliases={i: i for i in range(2 * n)},
        compiler_params=pltpu.CompilerParams(has_side_effects=_DATAFLOW),
    )(*src_thru, *land_thru, send_sems, recv_sems, *_pin(after))
    return outs[:n], outs[n:]


def _col_half(ref, which, lead=()):
    hc = ref.shape[-1] // 2
    return ref.at[(*lead, slice(None), pl.ds(pl.multiple_of(which * hc, 128), hc))]


def _gather_sent(ref, c, dst_chip):
    return _col_half(ref, c)


def _gather_landing(ref, c, src_chip):
    return _col_half(ref, c, lead=(src_chip,))


def _reduce_sent(ref, c, dst_chip):
    return ref.at[dst_chip]


def _reduce_landing(ref, c, src_chip):
    return ref.at[src_chip]


def _gather_finish(lands, shards):
    n = len(lands)
    any_spec = _HBM

    def body(*refs):
        shard_refs, out_refs = refs[n:2 * n], refs[2 * n:3 * n]
        send_sems, recv_sems, local_sems = refs[3 * n:]
        x, y, c = _mesh_pos()
        chip = 2 * x + y
        local, sends = [], []
        for a in range(n):
            cp = pltpu.make_async_copy(shard_refs[a], out_refs[a].at[chip], local_sems.at[a])
            cp.start()
            local.append(cp)
        for j in range(1, N_CHIPS):
            px, py = _peer_chip(x, y, j)
            for a in range(n):
                landed = _col_half(out_refs[a], c, lead=(2 * px + py,))
                cp = pltpu.make_async_remote_copy(
                    src_ref=landed, dst_ref=landed, send_sem=send_sems.at[a, j], recv_sem=recv_sems.at[a, j],
                    device_id=(x, y, 1 - c), device_id_type=MESH)
                cp.start()
                sends.append(cp)
        for j in range(1, N_CHIPS):
            px, py = _peer_chip(x, y, j)
            for a in range(n):
                other = _col_half(out_refs[a], 1 - c, lead=(2 * px + py,))
                pltpu.make_async_remote_copy(
                    src_ref=other, dst_ref=other, send_sem=send_sems.at[a, j], recv_sem=recv_sems.at[a, j],
                    device_id=(x, y, 1 - c), device_id_type=MESH).wait_recv()
        for cp in sends:
            cp.wait_send()
        for cp in local:
            cp.wait()

    return pl.pallas_call(
        body, name="gather_finish",
        out_shape=[_out(l.shape, l.dtype) for l in lands],
        in_specs=[any_spec] * (2 * n), out_specs=[any_spec] * n,
        input_output_aliases={i: i for i in range(n)},
        scratch_shapes=[pltpu.SemaphoreType.DMA((n, N_CHIPS))] * 2 + [pltpu.SemaphoreType.DMA((n,))],
        compiler_params=_cp(vmem_mb=16),
    )(*lands, *shards)


def _grad_to_sibling(grads):
    n = len(grads)
    any_spec = _HBM

    def body(*refs):
        in_refs, own_refs, got_refs = refs[:n], refs[n:2 * n], refs[2 * n:3 * n]
        send_sems, recv_sems, local_sems = refs[3 * n:]
        x, y, c = _mesh_pos()
        copies, local = [], []
        for a in range(n):
            hc = grads[a].shape[2] // 2
            mine = pl.ds(pl.multiple_of(c * hc, 128), hc)
            theirs = pl.ds(pl.multiple_of((1 - c) * hc, 128), hc)
            lc = pltpu.make_async_copy(in_refs[a].at[:, :, mine], own_refs[a], local_sems.at[a])
            lc.start()
            local.append(lc)
            cp = pltpu.make_async_remote_copy(
                src_ref=in_refs[a].at[:, :, theirs], dst_ref=got_refs[a],
                send_sem=send_sems.at[a], recv_sem=recv_sems.at[a],
                device_id=(x, y, 1 - c), device_id_type=MESH)
            cp.start()
            copies.append(cp)
        for cp in copies:
            cp.wait_recv()
        for cp in copies:
            cp.wait_send()
        for lc in local:
            lc.wait()

    half = [_out((g.shape[0], g.shape[1], g.shape[2] // 2), g.dtype) for g in grads]
    outs = pl.pallas_call(
        body, name="grad_to_sibling", out_shape=half + half,
        in_specs=[any_spec] * n, out_specs=[any_spec] * (2 * n),
        scratch_shapes=[pltpu.SemaphoreType.DMA((n,))] * 3,
        compiler_params=_cp(vmem_mb=16),
    )(*grads)
    return outs[:n], outs[n:]


def _halves_exchange(halves):
    n = len(halves)
    any_spec = _HBM

    def body(*refs):
        in_refs, out_refs = refs[:n], refs[n:2 * n]
        send_sems, recv_sems, local_sems = refs[2 * n:]
        x, y, c = _mesh_pos()
        copies, local = [], []
        for a in range(n):
            hc = halves[a].shape[1]
            mine = pl.ds(pl.multiple_of(c * hc, 128), hc)
            lc = pltpu.make_async_copy(in_refs[a], out_refs[a].at[:, mine], local_sems.at[a])
            lc.start()
            local.append(lc)
            cp = pltpu.make_async_remote_copy(
                src_ref=in_refs[a], dst_ref=out_refs[a].at[:, mine],
                send_sem=send_sems.at[a], recv_sem=recv_sems.at[a],
                device_id=(x, y, 1 - c), device_id_type=MESH)
            cp.start()
            copies.append(cp)
        for a in range(n):
            hc = halves[a].shape[1]
            theirs = pl.ds(pl.multiple_of((1 - c) * hc, 128), hc)
            pltpu.make_async_remote_copy(
                src_ref=in_refs[a], dst_ref=out_refs[a].at[:, theirs],
                send_sem=send_sems.at[a], recv_sem=recv_sems.at[a],
                device_id=(x, y, 1 - c), device_id_type=MESH).wait_recv()
        for cp in copies:
            cp.wait_send()
        for lc in local:
            lc.wait()

    return pl.pallas_call(
        body, name="halves_exchange",
        out_shape=[_out((h.shape[0], 2 * h.shape[1]), h.dtype) for h in halves],
        in_specs=[any_spec] * n, out_specs=[any_spec] * n,
        scratch_shapes=[pltpu.SemaphoreType.DMA((n,))] * 3,
        compiler_params=_cp(vmem_mb=16),
    )(*halves)


def _add_pair(a, b):
    k, h, c = a.shape

    def body(a_ref, b_ref, o_ref):
        o_ref[...] = (a_ref[...].astype(F32) + b_ref[...].astype(F32)).astype(o_ref.dtype)

    spec = pl.BlockSpec((1, h, c), lambda i: (i, 0, 0))
    return pl.pallas_call(body, name="add_pair", grid=(k,), in_specs=[spec, spec], out_specs=spec,
                          out_shape=_out(a.shape, BF), compiler_params=_cp(("parallel",), 32))(*_pin(a, b))


def _sum_slots(recv, own):
    k, h, c = recv.shape
    tc = c // 2

    def body(r_ref, p_ref, o_ref):
        chip = 2 * lax.axis_index("x") + lax.axis_index("y")
        o_ref[...] = jnp.zeros_like(o_ref)
        for s in range(k):
            @pl.when(chip == s)
            def _():
                o_ref[...] += p_ref[s].astype(F32)

            @pl.when(chip != s)
            def _():
                o_ref[...] += r_ref[s].astype(F32)

    spec = pl.BlockSpec((k, h, tc), lambda i: (0, 0, i))
    return pl.pallas_call(body, name="sum_slots", grid=(2,), in_specs=[spec, spec],
                          out_specs=pl.BlockSpec((h, tc), lambda i: (0, i)),
                          out_shape=_out((h, c), F32), compiler_params=_cp(("parallel",), 32))(*_pin(recv, own))


def _adam_math(w, g, m, v):
    m2 = ADAM_B1 * m + (1.0 - ADAM_B1) * g
    v2 = ADAM_B2 * v + (1.0 - ADAM_B2) * (g * g)
    m_hat = m2 / (1.0 - ADAM_B1 ** ADAM_STEP)
    v_hat = v2 / (1.0 - ADAM_B2 ** ADAM_STEP)
    delta = -ADAM_LR * (m_hat / (jnp.sqrt(v_hat) + ADAM_EPS) + ADAM_WD * w)
    return delta, m2, v2


def _adam_big(g, w, m, v):
    r, c = w.shape
    tc = c // 4

    def body(g_ref, w_ref, m_ref, v_ref, d_ref, m2_ref, v2_ref):
        d, m2, v2 = _adam_math(w_ref[...], g_ref[...], m_ref[...], v_ref[...])
        d_ref[...] = d
        m2_ref[...] = m2
        v2_ref[...] = v2

    spec = pl.BlockSpec((r, tc), lambda i: (0, i))
    sh = _out((r, c), F32)
    return pl.pallas_call(body, name="adam_big", grid=(4,), in_specs=[spec] * 4, out_specs=[spec] * 3,
                          out_shape=[sh, sh, sh], compiler_params=_cp(("parallel",), 32))(*_pin(g, w, m, v))


def _adam_ada(c_act_t, dmod_cols, w, m, v):
    r, c = w.shape
    tc = 512

    def body(ct_ref, dm_ref, w_ref, m_ref, v_ref, g_ref, d_ref, m2_ref, v2_ref):
        g = _nn(ct_ref[...], dm_ref[...].astype(BF))
        d, m2, v2 = _adam_math(w_ref[...], g, m_ref[...], v_ref[...])
        g_ref[...] = g
        d_ref[...] = d
        m2_ref[...] = m2
        v2_ref[...] = v2

    spec = pl.BlockSpec((r, tc), lambda i: (0, i))
    sh = _out((r, c), F32)
    return pl.pallas_call(
        body, name="adam_ada", grid=(c // tc,),
        in_specs=[pl.BlockSpec(c_act_t.shape, lambda i: (0, 0)), pl.BlockSpec((dmod_cols.shape[0], tc), lambda i: (0, i)),
                  spec, spec, spec],
        out_specs=[spec] * 4, out_shape=[sh] * 4, compiler_params=_cp(("parallel",), 48))(*_pin(c_act_t, dmod_cols, w, m, v))


def _ada_mod(c_all, w_shard, b_shard):
    nb, d = c_all.shape
    cols = w_shard.shape[1]
    tc = 512

    def body(c_ref, w_ref, b_ref, mod_ref, act_ref):
        cv = c_ref[...]
        act = cv * _sigmoid(cv)
        act_ref[...] = act
        mod_ref[...] = _nn(act.astype(BF), w_ref[...].astype(BF)) + b_ref[...]

    return pl.pallas_call(
        body, name="ada_mod", grid=(cols // tc,),
        in_specs=[pl.BlockSpec((nb, d), lambda i: (0, 0)), pl.BlockSpec((d, tc), lambda i: (0, i)),
                  pl.BlockSpec((1, tc), lambda i: (0, i))],
        out_specs=[pl.BlockSpec((nb, tc), lambda i: (0, i)), pl.BlockSpec((nb, d), lambda i: (0, 0))],
        out_shape=[_out((nb, cols), F32), _out((nb, d), F32)],
        compiler_params=_cp(("arbitrary",), 32))(*_pin(c_all, w_shard, b_shard))


def _token_spec():
    return pl.BlockSpec((8, 128), lambda *_: (0, 0))


def _in_proj(x, mod3, g_mix, w_in_t, seq, token):
    t, d = x.shape
    tm = 256
    tps = seq // tm

    def body(x_ref, mod_ref, g_ref, w_ref, tok_ref, proj_ref, u1_ref):
        xv = x_ref[...]
        r = lax.rsqrt(jnp.mean(xv * xv, -1, keepdims=True) + EPS)
        u = (xv * r * g_ref[...]) * (1.0 + mod_ref[0, 1:2, :]) + mod_ref[0, 0:1, :]
        ub = u.astype(BF)
        u1_ref[...] = ub
        proj_ref[...] = _nt(ub, w_ref[...])

    return pl.pallas_call(
        body, name="in_proj", grid=(t // tm,),
        in_specs=[pl.BlockSpec((tm, d), lambda i: (i, 0)), pl.BlockSpec((1, N_MOD, d), lambda i: (i // tps, 0, 0)),
                  pl.BlockSpec((1, d), lambda i: (0, 0)), pl.BlockSpec((IN_PAD, d), lambda i: (0, 0)), _token_spec()],
        out_specs=[pl.BlockSpec((tm, IN_PAD), lambda i: (i, 0)), pl.BlockSpec((tm, d), lambda i: (i, 0))],
        out_shape=[_out((t, IN_PAD), F32), _out((t, d), BF)],
        compiler_params=_cp(("parallel",), 48))(*_pin(x, mod3, g_mix, w_in_t, token))


def _pool_tile(seq):
    return min(512, seq)


def _pool_fwd(proj, w_pool, pool_scale, nb, seq):
    ts = _pool_tile(seq)
    nt = seq // ts

    def body(u_ref, halo_ref, wp_ref, ps_ref, yp_ref, p_ref):
        i = pl.program_id(1)
        halo = jnp.where(i == 0, 0.0, halo_ref[...])
        u = u_ref[...]
        ext = jnp.concatenate([halo, u], 0)
        tpos = i * ts + _iota((ts, 1), 0)
        for g, w in enumerate(POOL_WINDOWS):
            gs = slice(g * POOL_GROUP, (g + 1) * POOL_GROUP)
            s = ext[:, gs]
            sh = 1
            while sh < w:
                s = s + pltpu.roll(s, sh, 0)
                sh *= 2
            cnt = jnp.minimum(tpos + 1, w).astype(F32)
            pb = (s[HALO:] / cnt - u[:, gs]).astype(BF)
            p_ref[:, gs] = pb
            yp_ref[:, gs] = (_nn(pb, wp_ref[g].astype(BF)) * ps_ref[:, gs]).astype(BF)

    hb = ts // HALO
    return pl.pallas_call(
        body, name="pool_fwd", grid=(nb, nt),
        in_specs=[pl.BlockSpec((ts, POOL_WIDTH), lambda b, i: (b * nt + i, 0)),
                  pl.BlockSpec((HALO, POOL_WIDTH), lambda b, i: (jnp.maximum((b * nt + i) * hb - 1, 0), 0)),
                  pl.BlockSpec((4, POOL_GROUP, POOL_GROUP), lambda b, i: (0, 0, 0)),
                  pl.BlockSpec((1, POOL_WIDTH), lambda b, i: (0, 0))],
        out_specs=[pl.BlockSpec((ts, POOL_WIDTH), lambda b, i: (b * nt + i, 0))] * 2,
        out_shape=[_out((nb * seq, POOL_WIDTH), BF)] * 2,
        compiler_params=_cp(("parallel", "parallel"), 32))(*_pin(proj, proj, w_pool, pool_scale))


def _conv_pre(uxbc, halo, cw, cb, first):
    halo = jnp.where(first, 0.0, halo)
    ext = jnp.concatenate([halo, uxbc], 0)
    pre = cb + uxbc * cw[3:4]
    for k in (2, 1, 0):
        pre = pre + pltpu.roll(ext, 3 - k, 0)[CONV_HALO:] * cw[k:k + 1]
    return pre


def _chunk_terms(pre, udt, dtb, alog):
    sg = _sigmoid(pre)
    xbc = pre * sg
    dtp = udt[:, :SSD_HEADS] + dtb
    dt = jnp.maximum(dtp, 0.0) + jnp.log(1.0 + jnp.exp(-jnp.abs(dtp)))
    a = -jnp.exp(alog)
    da = dt * a
    tril = (_iota((CHUNK, CHUNK), 0) >= _iota((CHUNK, CHUNK), 1))
    acum = _exact_nn_left(tril.astype(BF), da)
    eye = (_iota((SSD_HEADS, SSD_HEADS), 0) == _iota((SSD_HEADS, SSD_HEADS), 1)).astype(BF)
    acum_t = _exact_nt_left(eye, acum)
    expand = _head_expand_matrix(SSD_HEADS, SSD_INNER)
    acum_e = _exact_nn(acum, expand)
    dt_e = _exact_nn(dt, expand)
    last_e = acum_e[CHUNK - 1:CHUNK]
    return dict(pre=pre, sg=sg, xbc=xbc, dtp=dtp, dt=dt, a=a, acum=acum, acum_t=acum_t, tril=tril,
                dt_e=dt_e, e_a=jnp.exp(acum_e), d_out=jnp.exp(last_e - acum_e), c_dec=jnp.exp(last_e))


def _head_decay(r, h):
    seg = r["acum"][:, h:h + 1] - r["acum_t"][h:h + 1, :]
    return jnp.where(r["tril"], jnp.exp(jnp.minimum(seg, 0.0)), 0.0)


def _ssd_specs(nb, seq, reverse):
    nc = seq // CHUNK
    per = seq // CONV_HALO

    def cidx(c):
        return (nc - 1 - c) if reverse else c

    def row(b, c):
        return b * nc + cidx(c)

    specs = [
        pl.BlockSpec((CHUNK, CONV_CH), lambda b, c: (row(b, c), 1)),
        pl.BlockSpec((CONV_HALO, CONV_CH),
                     lambda b, c: (jnp.maximum(b * per + cidx(c) * (CHUNK // CONV_HALO) - 1, 0), 1)),
        pl.BlockSpec((CHUNK, GROUP_W), lambda b, c: (row(b, c), 1)),
        pl.BlockSpec((CHUNK, GROUP_W), lambda b, c: (row(b, c), 2)),
        pl.BlockSpec((CHUNK, 128), lambda b, c: (row(b, c), OFF_DT // 128)),
    ]
    return specs, row, cidx, nc


def _const_spec(shape):
    return pl.BlockSpec(shape, lambda b, c: (0,) * len(shape))


def _ssd_fwd(proj, conv_w, conv_b, dt_bias, a_log, dskip_e, g_ssd, nb, seq):
    specs, row, cidx, nc = _ssd_specs(nb, seq, reverse=False)

    def body(uxbc_ref, halo_ref, z0_ref, z1_ref, udt_ref, cw_ref, cb_ref, dtb_ref, alog_ref, dsk_ref, gs_ref,
             yssd_ref, yssm_ref, hprev_ref, pre_ref, h_ref, yd_ref):
        c = pl.program_id(1)

        @pl.when(c == 0)
        def _():
            h_ref[...] = jnp.zeros_like(h_ref)

        pre = _conv_pre(uxbc_ref[...], halo_ref[...], cw_ref[...], cb_ref[...], c == 0)
        pre_ref[...] = pre
        r = _chunk_terms(pre, udt_ref[...], dtb_ref[...], alog_ref[...])
        xbc = r["xbc"]
        xs = xbc[:, :SSD_INNER]
        xdt = xs * r["dt_e"]
        xdt_b = xdt.astype(BF)
        xdo_b = (xdt * r["d_out"]).astype(BF)
        hprev_ref[0, 0] = h_ref[...]
        for g in range(2):
            gs = slice(g * GROUP_W, (g + 1) * GROUP_W)
            bg = xbc[:, SSD_INNER + g * SSD_STATE:SSD_INNER + (g + 1) * SSD_STATE].astype(BF)
            cg = xbc[:, SSD_INNER + (2 + g) * SSD_STATE:SSD_INNER + (3 + g) * SSD_STATE].astype(BF)
            scores = _nt(cg, bg)
            hg = h_ref[g]
            y_off = _nn(cg, hg.astype(BF)) * r["e_a"][:, gs]
            for hh in range(8):
                h = g * 8 + hh
                m = (scores * _head_decay(r, h)).astype(BF)
                yd_ref[:, h * SSD_HEAD_DIM:(h + 1) * SSD_HEAD_DIM] = _nn(m, xdt_b[:, h * SSD_HEAD_DIM:(h + 1) * SSD_HEAD_DIM])
            h_ref[g] = hg * r["c_dec"][:, gs] + _tn(bg, xdo_b[:, gs])
            y = yd_ref[:, gs] + y_off + dsk_ref[:, gs] * xs[:, gs]
            yssm_ref[:, gs] = y
            zg = (z0_ref if g == 0 else z1_ref)[...]
            yg = y * (zg * _sigmoid(zg))
            rg = lax.rsqrt(jnp.mean(yg * yg, -1, keepdims=True) + EPS)
            yssd_ref[:, gs] = (yg * rg * gs_ref[:, gs]).astype(BF)

    t = nb * seq
    return pl.pallas_call(
        body, name="ssd_fwd", grid=(nb, nc),
        in_specs=specs + [_const_spec((4, CONV_CH)), _const_spec((1, CONV_CH)), _const_spec((1, SSD_HEADS)),
                          _const_spec((1, SSD_HEADS)), _const_spec((1, SSD_INNER)), _const_spec((1, SSD_INNER))],
        out_specs=[pl.BlockSpec((CHUNK, SSD_INNER), lambda b, c: (row(b, c), 0)),
                   pl.BlockSpec((CHUNK, SSD_INNER), lambda b, c: (row(b, c), 0)),
                   pl.BlockSpec((1, 1, 2, SSD_STATE, GROUP_W), lambda b, c: (b, c, 0, 0, 0)),
                   pl.BlockSpec((CHUNK, CONV_CH), lambda b, c: (row(b, c), 0))],
        out_shape=[_out((t, SSD_INNER), BF), _out((t, SSD_INNER), F32),
                   _out((nb, nc, 2, SSD_STATE, GROUP_W), F32), _out((t, CONV_CH), F32)],
        scratch_shapes=[pltpu.VMEM((2, SSD_STATE, GROUP_W), F32), pltpu.VMEM((CHUNK, SSD_INNER), F32)],
        compiler_params=_cp(("arbitrary", "arbitrary"), 48),
    )(*_pin(proj, proj, proj, proj, proj, conv_w, conv_b, dt_bias, a_log, dskip_e, g_ssd))


def _out_proj(y_pool, y_ssd, w_out, x, mod3, g_mlp, seq):
    t, d = x.shape
    tm = 512
    tps = seq // tm if seq >= tm else 1
    tm = min(tm, seq)

    def body(yp_ref, ys_ref, w_ref, x_ref, mod_ref, g_ref, h1_ref, o_ref, u2_ref):
        o = _nn(yp_ref[...], w_ref[0:POOL_WIDTH, :]) + _nn(ys_ref[...], w_ref[POOL_WIDTH:, :])
        o_ref[...] = o.astype(BF)
        h1 = x_ref[...] + mod_ref[0, 2:3, :] * o
        h1_ref[...] = h1
        r = lax.rsqrt(jnp.mean(h1 * h1, -1, keepdims=True) + EPS)
        u2_ref[...] = ((h1 * r * g_ref[...]) * (1.0 + mod_ref[0, 4:5, :]) + mod_ref[0, 3:4, :]).astype(BF)

    row = lambda i: (i, 0)
    return pl.pallas_call(
        body, name="out_proj", grid=(t // tm,),
        in_specs=[pl.BlockSpec((tm, POOL_WIDTH), row), pl.BlockSpec((tm, SSD_INNER), row),
                  pl.BlockSpec(w_out.shape, lambda i: (0, 0)), pl.BlockSpec((tm, d), row),
                  pl.BlockSpec((1, N_MOD, d), lambda i: (i // tps, 0, 0)), pl.BlockSpec((1, d), lambda i: (0, 0))],
        out_specs=[pl.BlockSpec((tm, d), row)] * 3,
        out_shape=[_out((t, d), F32), _out((t, d), BF), _out((t, d), BF)],
        compiler_params=_cp(("parallel",), 48))(*_pin(y_pool, y_ssd, w_out, x, mod3, g_mlp))


def _mlp_up(u2, w_up4):
    t, d = u2.shape
    tm = min(1024, t)
    nk, _, cols = w_up4.shape

    def body(u_ref, w_ref, a_ref):
        a_ref[...] = _nn(u_ref[...], w_ref[0]).astype(BF)

    return pl.pallas_call(
        body, name="mlp_up", grid=(nk, t // tm),
        in_specs=[pl.BlockSpec((tm, d), lambda k, i: (i, 0)), pl.BlockSpec((1, d, cols), lambda k, i: (k, 0, 0))],
        out_specs=pl.BlockSpec((tm, cols), lambda k, i: (i, k)),
        out_shape=_out((t, nk * cols), BF),
        compiler_params=_cp(("parallel", "parallel"), 32))(*_pin(u2, w_up4))


def _mlp_down_loss(a_up, w_down, h1, mod3, g_final, target, seq):
    t, d = h1.shape
    nb = t // seq
    tm = min(256, seq)
    tps = seq // tm

    def body(a_ref, w_ref, h1_ref, mod_ref, g_ref, tg_ref, ddn_ref, dh2_ref, sq_ref, gg_ref, dgf_ref):
        i = pl.program_id(0)
        f = jnp.square(jnp.maximum(a_ref[...], 0))
        dn = _nn(f, w_ref[...])
        gate = mod_ref[0, 5:6, :]
        h2 = h1_ref[...] + gate * dn
        r = lax.rsqrt(jnp.mean(h2 * h2, -1, keepdims=True) + EPS)
        hh = h2 * r
        err = hh * g_ref[...] - tg_ref[...]
        dy = err * (1.0 / d)
        dhat = dy * g_ref[...]
        dh2 = r * (dhat - hh * jnp.mean(dhat * hh, -1, keepdims=True))
        dh2_ref[...] = dh2
        ddn_ref[...] = (dh2 * gate).astype(BF)

        @pl.when(i == 0)
        def _():
            sq_ref[...] = jnp.zeros_like(sq_ref)
            gg_ref[...] = jnp.zeros_like(gg_ref)

        @pl.when(i % tps == 0)
        def _():
            dgf_ref[...] = jnp.zeros_like(dgf_ref)

        sq_ref[...] += jnp.sum(err * err, 0, keepdims=True)
        gg_ref[...] += jnp.sum(dy * hh, 0, keepdims=True)
        dgf_ref[0] += jnp.sum(dh2 * dn, 0, keepdims=True)

    row = lambda i: (i, 0)
    vec = pl.BlockSpec((1, d), lambda i: (0, 0))
    return pl.pallas_call(
        body, name="mlp_down_loss", grid=(t // tm,),
        in_specs=[pl.BlockSpec((tm, D_FF), row), pl.BlockSpec(w_down.shape, lambda i: (0, 0)), pl.BlockSpec((tm, d), row),
                  pl.BlockSpec((1, N_MOD, d), lambda i: (i // tps, 0, 0)), vec, pl.BlockSpec((tm, d), row)],
        out_specs=[pl.BlockSpec((tm, d), row), pl.BlockSpec((tm, d), row), vec, vec,
                   pl.BlockSpec((1, 1, d), lambda i: (i // tps, 0, 0))],
        out_shape=[_out((t, d), BF), _out((t, d), F32), _out((1, d), F32),
                   _out((1, d), F32), _out((nb, 1, d), F32)],
        compiler_params=_cp(("arbitrary",), 56))(*_pin(a_up, w_down, h1, mod3, g_final, target))


def _tn_matmul(a, b, tk, tn, name, square_relu=False, out3=False):
    t, kdim = a.shape
    ndim = b.shape[1]

    def body(a_ref, b_ref, o_ref):
        av = a_ref[...]
        if square_relu:
            av = jnp.square(jnp.maximum(av, 0))
        res = _tn(av, b_ref[...]).astype(BF)
        if out3:
            o_ref[0] = res
        else:
            o_ref[...] = res

    if out3:
        out_spec = pl.BlockSpec((1, tk, tn), lambda j, i: (j, i, 0))
        out_shape = _out((ndim // tn, kdim, tn), BF)
    else:
        out_spec = pl.BlockSpec((tk, tn), lambda j, i: (i, j))
        out_shape = _out((kdim, ndim), BF)
    return pl.pallas_call(
        body, name=name, grid=(ndim // tn, kdim // tk),
        in_specs=[pl.BlockSpec((t, tk), lambda j, i: (0, i)), pl.BlockSpec((t, tn), lambda j, i: (0, j))],
        out_specs=out_spec, out_shape=out_shape,
        compiler_params=_cp(("parallel", "parallel"), 56))(*_pin(a, b))


def _mlp_down_bwd(d_dn, w_down4, a_up, token):
    t, d = d_dn.shape
    tm = min(1024, t)
    nk, rows, _ = w_down4.shape

    def body(g_ref, w_ref, a_ref, tok_ref, o_ref):
        df = _nt(g_ref[...], w_ref[0])
        o_ref[...] = (df * (2.0 * jnp.maximum(a_ref[...], 0).astype(F32))).astype(BF)

    return pl.pallas_call(
        body, name="mlp_down_bwd", grid=(nk, t // tm),
        in_specs=[pl.BlockSpec((tm, d), lambda k, i: (i, 0)), pl.BlockSpec((1, rows, d), lambda k, i: (k, 0, 0)),
                  pl.BlockSpec((tm, rows), lambda k, i: (i, k)), _token_spec()],
        out_specs=pl.BlockSpec((tm, rows), lambda k, i: (i, k)),
        out_shape=_out((t, nk * rows), BF),
        compiler_params=_cp(("parallel", "parallel"), 32))(*_pin(d_dn, w_down4, a_up, token))


def _mlp_up_bwd(d_a, w_up4, h1, dh2, o, mod3, g_mlp, seq, token):
    t, d = h1.shape
    nb = t // seq
    tm = min(256, seq)
    tps = seq // tm
    nk = w_up4.shape[0]
    cols = w_up4.shape[2]

    def body(da_ref, w_ref, h1_ref, dh2_ref, o_ref, mod_ref, g_ref, tok_ref, dh1_ref, do_ref, acc_ref, gg_ref):
        i = pl.program_id(0)
        du = _nt(da_ref[:, 0:cols], w_ref[0])
        for k in range(1, nk):
            du = du + _nt(da_ref[:, k * cols:(k + 1) * cols], w_ref[k])
        h1 = h1_ref[...]
        r = lax.rsqrt(jnp.mean(h1 * h1, -1, keepdims=True) + EPS)
        hh = h1 * r
        n2 = hh * g_ref[...]
        dn2 = du * (1.0 + mod_ref[0, 4:5, :])
        dhat = dn2 * g_ref[...]
        dh1 = dh2_ref[...] + r * (dhat - hh * jnp.mean(dhat * hh, -1, keepdims=True))
        dh1_ref[...] = dh1
        do_ref[...] = (dh1 * mod_ref[0, 2:3, :]).astype(BF)

        @pl.when(i == 0)
        def _():
            gg_ref[...] = jnp.zeros_like(gg_ref)

        @pl.when(i % tps == 0)
        def _():
            acc_ref[...] = jnp.zeros_like(acc_ref)

        gg_ref[...] += jnp.sum(dn2 * hh, 0, keepdims=True)
        acc_ref[0, 0:1, :] += jnp.sum(du, 0, keepdims=True)
        acc_ref[0, 1:2, :] += jnp.sum(du * n2, 0, keepdims=True)
        acc_ref[0, 2:3, :] += jnp.sum(dh1 * o_ref[...].astype(F32), 0, keepdims=True)

    row = lambda i: (i, 0)
    vec = pl.BlockSpec((1, d), lambda i: (0, 0))
    return pl.pallas_call(
        body, name="mlp_up_bwd", grid=(t // tm,),
        in_specs=[pl.BlockSpec((tm, D_FF), row), pl.BlockSpec(w_up4.shape, lambda i: (0, 0, 0)), pl.BlockSpec((tm, d), row),
                  pl.BlockSpec((tm, d), row), pl.BlockSpec((tm, d), row),
                  pl.BlockSpec((1, N_MOD, d), lambda i: (i // tps, 0, 0)), vec, _token_spec()],
        out_specs=[pl.BlockSpec((tm, d), row), pl.BlockSpec((tm, d), row),
                   pl.BlockSpec((1, 8, d), lambda i: (i // tps, 0, 0)), vec],
        out_shape=[_out((t, d), F32), _out((t, d), BF),
                   _out((nb, 8, d), F32), _out((1, d), F32)],
        compiler_params=_cp(("arbitrary",), 56))(*_pin(d_a, w_up4, h1, dh2, o, mod3, g_mlp, token))


def _out_proj_bwd(d_o, w_out, token):
    t, d = d_o.shape
    tm = min(512, t)

    def body(g_ref, w_ref, tok_ref, dp_ref, ds_ref):
        gv = g_ref[...]
        dp_ref[...] = _nt(gv, w_ref[0:POOL_WIDTH, :])
        ds_ref[...] = _nt(gv, w_ref[POOL_WIDTH:, :])

    row = lambda i: (i, 0)
    return pl.pallas_call(
        body, name="out_proj_bwd", grid=(t // tm,),
        in_specs=[pl.BlockSpec((tm, d), row), pl.BlockSpec(w_out.shape, lambda i: (0, 0)), _token_spec()],
        out_specs=[pl.BlockSpec((tm, POOL_WIDTH), row), pl.BlockSpec((tm, SSD_INNER), row)],
        out_shape=[_out((t, POOL_WIDTH), F32), _out((t, SSD_INNER), F32)],
        compiler_params=_cp(("parallel",), 32))(*_pin(d_o, w_out, token))


def _pool_bwd(d_ypool, p, w_pool, pool_scale, nb, seq):
    ts = _pool_tile(seq)
    nt = seq // ts
    hb = ts // HALO
    last_block = nb * seq // HALO - 1

    def body(dy_ref, halo_ref, p_ref, wp_ref, ps_ref, du_ref, gw_ref, gs_ref):
        b = pl.program_id(0)
        i = pl.program_id(1)

        @pl.when((b == 0) & (i == 0))
        def _():
            gw_ref[...] = jnp.zeros_like(gw_ref)
            gs_ref[...] = jnp.zeros_like(gs_ref)

        halo = jnp.where(i == nt - 1, 0.0, halo_ref[...])
        dy = dy_ref[...]
        ext = jnp.concatenate([dy, halo], 0)
        tpos = i * ts + _iota((ts + HALO, 1), 0)
        n_ext = ts + HALO
        for g, w in enumerate(POOL_WINDOWS):
            gs = slice(g * POOL_GROUP, (g + 1) * POOL_GROUP)
            wg = wp_ref[g].astype(BF)
            pg = p_ref[:, gs]
            pw = _nn(pg, wg)
            gs_ref[:, gs] += jnp.sum(dy[:, gs] * pw, 0, keepdims=True)
            dpw = (ext[:, gs] * ps_ref[:, gs]).astype(BF)
            gw_ref[g] += _tn(pg, dpw[:ts])
            dp = _nt(dpw, wg)
            cnt = jnp.minimum(tpos + 1, w).astype(F32)
            s = dp / cnt
            sh = 1
            while sh < w:
                s = s + pltpu.roll(s, n_ext - sh, 0)
                sh *= 2
            du_ref[:, gs] = (s[:ts] - dp[:ts]).astype(BF)

    return pl.pallas_call(
        body, name="pool_bwd", grid=(nb, nt),
        in_specs=[pl.BlockSpec((ts, POOL_WIDTH), lambda b, i: (b * nt + i, 0)),
                  pl.BlockSpec((HALO, POOL_WIDTH), lambda b, i: (jnp.minimum((b * nt + i + 1) * hb, last_block), 0)),
                  pl.BlockSpec((ts, POOL_WIDTH), lambda b, i: (b * nt + i, 0)),
                  pl.BlockSpec((4, POOL_GROUP, POOL_GROUP), lambda b, i: (0, 0, 0)),
                  pl.BlockSpec((1, POOL_WIDTH), lambda b, i: (0, 0))],
        out_specs=[pl.BlockSpec((ts, POOL_WIDTH), lambda b, i: (b * nt + i, 0)),
                   pl.BlockSpec((4, POOL_GROUP, POOL_GROUP), lambda b, i: (0, 0, 0)),
                   pl.BlockSpec((1, POOL_WIDTH), lambda b, i: (0, 0))],
        out_shape=[_out((nb * seq, POOL_WIDTH), BF), _out((4, POOL_GROUP, POOL_GROUP), F32),
                   _out((1, POOL_WIDTH), F32)],
        compiler_params=_cp(("arbitrary", "arbitrary"), 32))(*_pin(d_ypool, d_ypool, p, w_pool, pool_scale))


def _ssd_bwd(proj, pre, d_yssd, yssm, h_prev, dt_bias, a_log, dskip_e, g_ssd, nb, seq):
    specs, row, cidx, nc = _ssd_specs(nb, seq, reverse=True)
    specs = specs[2:]

    def body(z0_ref, z1_ref, udt_ref, pre_ref, dys_ref, yssm_ref, hprev_ref,
             dtb_ref, alog_ref, dsk_ref, gs_ref,
             dz_ref, dpre_ref, dudt_ref, ggs_ref, gdsk_ref, ga_ref, gdtb_ref,
             g_ref, dxdt_ref, dyv_ref):
        b = pl.program_id(0)
        c = pl.program_id(1)

        @pl.when(c == 0)
        def _():
            g_ref[...] = jnp.zeros_like(g_ref)

        @pl.when((b == 0) & (c == 0))
        def _():
            ggs_ref[...] = jnp.zeros_like(ggs_ref)
            gdsk_ref[...] = jnp.zeros_like(gdsk_ref)
            ga_ref[...] = jnp.zeros_like(ga_ref)
            gdtb_ref[...] = jnp.zeros_like(gdtb_ref)

        r = _chunk_terms(pre_ref[...], udt_ref[...], dtb_ref[...], alog_ref[...])
        xbc = r["xbc"]
        xs = xbc[:, :SSD_INNER]
        dt_e = r["dt_e"]
        xdt = xs * dt_e
        xdt_b = xdt.astype(BF)
        reduce_m = _head_reduce_matrix(GROUP_W, 8)

        def head_sums(v):
            return _nn(v.astype(BF), reduce_m)

        onehot16 = lambda h: (_iota((1, SSD_HEADS), 1) == h).astype(F32)
        onecol16 = lambda h: (_iota((SSD_HEADS, 1), 0) == h).astype(F32)

        d_acum = jnp.zeros((CHUNK, SSD_HEADS), F32)
        d_acum_t = jnp.zeros((SSD_HEADS, CHUNK), F32)
        d_alast = jnp.zeros((1, SSD_HEADS), F32)
        place8 = lambda g: (_iota((8, SSD_HEADS), 1) == _iota((8, SSD_HEADS), 0) + 8 * g).astype(BF)
        d_b, d_c = [], []
        for g in range(2):
            gs = slice(g * GROUP_W, (g + 1) * GROUP_W)
            zg = (z0_ref if g == 0 else z1_ref)[...]
            sz = _sigmoid(zg)
            silu_z = zg * sz
            ys = yssm_ref[:, gs]
            yg = ys * silu_z
            rg = lax.rsqrt(jnp.mean(yg * yg, -1, keepdims=True) + EPS)
            yh = yg * rg
            dys = dys_ref[:, gs]
            ggs_ref[:, gs] += jnp.sum(dys * yh, 0, keepdims=True)
            dyh = dys * gs_ref[:, gs]
            dyg = rg * (dyh - yh * jnp.mean(dyh * yh, -1, keepdims=True))
            dy = dyg * silu_z
            dz_ref[:, gs] = (dyg * ys * (sz * (1.0 + zg * (1.0 - sz)))).astype(BF)
            gdsk_ref[:, gs] += jnp.sum(dy * xs[:, gs], 0, keepdims=True)
            dyv_ref[:, gs] = dy
            dy_b = dy.astype(BF)

            bg = xbc[:, SSD_INNER + g * SSD_STATE:SSD_INNER + (g + 1) * SSD_STATE].astype(BF)
            cg = xbc[:, SSD_INNER + (2 + g) * SSD_STATE:SSD_INNER + (3 + g) * SSD_STATE].astype(BF)
            scores = _nt(cg, bg)
            hg = hprev_ref[0, 0, g]
            hg_b = hg.astype(BF)
            gg = g_ref[g]
            gg_b = gg.astype(BF)
            e_a = r["e_a"][:, gs]
            d_out = r["d_out"][:, gs]
            c_dec = r["c_dec"][:, gs]
            zc = _nn(cg, hg_b)
            wv = e_a * dy
            wv_b = wv.astype(BF)
            da_g = head_sums(wv * zc)
            dcg = _nt(wv_b, hg_b)
            d_hprev = _tn(cg, wv_b)
            vg = _nn(bg, gg_b)
            dxdt_g = d_out * vg
            dd_out = head_sums(xdt[:, gs] * vg)
            dbg = _nt((xdt[:, gs] * d_out).astype(BF), gg_b)
            dcd = _exact_nn(jnp.sum(gg * hg, 0, keepdims=True), reduce_m)
            d_out8 = jnp.exp(r["acum"][CHUNK - 1:CHUNK, 8 * g:8 * g + 8] - r["acum"][:, 8 * g:8 * g + 8])
            c_dec8 = jnp.exp(r["acum"][CHUNK - 1:CHUNK, 8 * g:8 * g + 8])
            t8 = dd_out * d_out8
            d_alast = d_alast + _exact_nn(jnp.sum(t8, 0, keepdims=True) + dcd * c_dec8, place8(g))
            d_acum = d_acum + _exact_nn(da_g - t8, place8(g))
            dsc = jnp.zeros((CHUNK, CHUNK), F32)
            for hh in range(8):
                h = g * 8 + hh
                hs = slice(h * SSD_HEAD_DIM, (h + 1) * SSD_HEAD_DIM)
                lam = _head_decay(r, h)
                m = scores * lam
                dyh_b = dy_b[:, hh * SSD_HEAD_DIM:(hh + 1) * SSD_HEAD_DIM]
                dm = _nt(dyh_b, xdt_b[:, hs])
                tm_ = dm * m
                d_acum = d_acum + jnp.sum(tm_, 1, keepdims=True) * onehot16(h)
                d_acum_t = d_acum_t + onecol16(h) * jnp.sum(tm_, 0, keepdims=True)
                dsc = dsc + dm * lam
                dxdt_ref[:, hs] = _tn(m.astype(BF), dyh_b) + dxdt_g[:, hh * SSD_HEAD_DIM:(hh + 1) * SSD_HEAD_DIM]
            dsc_b = dsc.astype(BF)
            d_c.append(dcg + _nn(dsc_b, bg))
            d_b.append(dbg + _tn(dsc_b, cg))
            g_ref[g] = d_hprev + c_dec * gg

        eye = (_iota((CHUNK, CHUNK), 0) == _iota((CHUNK, CHUNK), 1)).astype(BF)
        d_acum = d_acum - _exact_nt_left(eye, d_acum_t)
        is_last = (_iota((CHUNK, 1), 0) == CHUNK - 1).astype(F32)
        d_acum = d_acum + is_last * d_alast
        triu = (_iota((CHUNK, CHUNK), 0) <= _iota((CHUNK, CHUNK), 1)).astype(BF)
        d_da = _exact_nn_left(triu, d_acum)
        dt = r["dt"]
        ga_ref[...] += jnp.sum(d_da * dt, 0, keepdims=True)
        dxdt = dxdt_ref[...]
        reduce16 = _head_reduce_matrix(SSD_INNER, SSD_HEADS)
        d_dt = d_da * r["a"] + _nn((dxdt * xs).astype(BF), reduce16)
        d_udt = d_dt * _sigmoid(r["dtp"])
        gdtb_ref[...] += jnp.sum(d_udt, 0, keepdims=True)
        dudt_ref[...] = jnp.zeros_like(dudt_ref)
        dudt_ref[:, 0:SSD_HEADS] = d_udt.astype(BF)
        pre, sg = r["pre"], r["sg"]
        dsilu = sg * (1.0 + pre * (1.0 - sg))
        dpre_ref[:, 0:SSD_INNER] = (dsk_ref[...] * dyv_ref[...] + dxdt * dt_e) * dsilu[:, 0:SSD_INNER]
        for g in range(2):
            bs = slice(SSD_INNER + g * SSD_STATE, SSD_INNER + (g + 1) * SSD_STATE)
            cs = slice(SSD_INNER + (2 + g) * SSD_STATE, SSD_INNER + (3 + g) * SSD_STATE)
            dpre_ref[:, bs] = d_b[g] * dsilu[:, bs]
            dpre_ref[:, cs] = d_c[g] * dsilu[:, cs]

    t = nb * seq
    vec = _const_spec((1, SSD_INNER))
    small = _const_spec((1, SSD_HEADS))
    return pl.pallas_call(
        body, name="ssd_bwd", grid=(nb, nc),
        in_specs=specs + [pl.BlockSpec((CHUNK, CONV_CH), lambda b, c: (row(b, c), 0)),
                          pl.BlockSpec((CHUNK, SSD_INNER), lambda b, c: (row(b, c), 0)),
                          pl.BlockSpec((CHUNK, SSD_INNER), lambda b, c: (row(b, c), 0)),
                          pl.BlockSpec((1, 1, 2, SSD_STATE, GROUP_W), lambda b, c: (b, cidx(c), 0, 0, 0)),
                          small, small, vec, vec],
        out_specs=[pl.BlockSpec((CHUNK, SSD_INNER), lambda b, c: (row(b, c), 0)),
                   pl.BlockSpec((CHUNK, CONV_CH), lambda b, c: (row(b, c), 0)),
                   pl.BlockSpec((CHUNK, 128), lambda b, c: (row(b, c), 0)),
                   vec, vec, small, small],
        out_shape=[_out((t, SSD_INNER), BF), _out((t, CONV_CH), F32),
                   _out((t, 128), BF), _out((1, SSD_INNER), F32),
                   _out((1, SSD_INNER), F32), _out((1, SSD_HEADS), F32),
                   _out((1, SSD_HEADS), F32)],
        scratch_shapes=[pltpu.VMEM((2, SSD_STATE, GROUP_W), F32), pltpu.VMEM((CHUNK, SSD_INNER), F32),
                        pltpu.VMEM((CHUNK, SSD_INNER), F32)],
        compiler_params=_cp(("arbitrary", "arbitrary"), 48),
    )(*_pin(proj, proj, proj, pre, d_yssd, yssm, h_prev, dt_bias, a_log, dskip_e, g_ssd))


def _grad_w_in_t(d_upool, d_z, d_uxbc, d_udt, u1):
    t, d = u1.shape
    tk = 512
    n_z, n_x = SSD_INNER // tk, CONV_CH // tk

    def body(p_ref, z_ref, x_ref, dt_ref, u_ref, o_ref):
        i = pl.program_id(0)

        @pl.when(i == 0)
        def _():
            o_ref[...] = _tn(p_ref[...], u_ref[...]).astype(BF)

        @pl.when((i >= 1) & (i < 1 + n_z))
        def _():
            o_ref[...] = _tn(z_ref[...], u_ref[...]).astype(BF)

        @pl.when((i >= 1 + n_z) & (i < 1 + n_z + n_x))
        def _():
            o_ref[...] = _tn(x_ref[...], u_ref[...]).astype(BF)

        @pl.when(i == 1 + n_z + n_x)
        def _():
            o_ref[0:128, :] = _tn(dt_ref[...], u_ref[...]).astype(BF)

    return pl.pallas_call(
        body, name="grad_w_in", grid=(2 + n_z + n_x,),
        in_specs=[pl.BlockSpec((t, tk), lambda i: (0, 0)),
                  pl.BlockSpec((t, tk), lambda i: (0, jnp.clip(i - 1, 0, n_z - 1))),
                  pl.BlockSpec((t, tk), lambda i: (0, jnp.clip(i - 1 - n_z, 0, n_x - 1))),
                  pl.BlockSpec((t, 128), lambda i: (0, 0)), pl.BlockSpec((t, d), lambda i: (0, 0))],
        out_specs=pl.BlockSpec((tk, d), lambda i: (i, 0)),
        out_shape=_out((IN_PAD, d), BF),
        compiler_params=_cp(("parallel",), 56))(*_pin(d_upool, d_z, d_uxbc, d_udt, u1))


def _conv_bwd(d_pre, proj, conv_w, nb, seq):
    ts = _pool_tile(seq)
    nt = seq // ts
    hb = ts // CONV_HALO
    last_block = nb * seq // CONV_HALO - 1
    n_ext = ts + CONV_HALO

    def body(dp_ref, dnext_ref, u_ref, uprev_ref, cw_ref, du_ref, gw_ref, gb_ref):
        b = pl.program_id(0)
        i = pl.program_id(1)

        @pl.when((b == 0) & (i == 0))
        def _():
            gw_ref[...] = jnp.zeros_like(gw_ref)
            gb_ref[...] = jnp.zeros_like(gb_ref)

        dp = dp_ref[...]
        ext_d = jnp.concatenate([dp, jnp.where(i == nt - 1, 0.0, dnext_ref[...])], 0)
        ext_u = jnp.concatenate([jnp.where(i == 0, 0.0, uprev_ref[...]), u_ref[...]], 0)
        cw = cw_ref[...]
        du = dp * cw[3:4]
        gw_ref[3:4, :] += jnp.sum(dp * u_ref[...], 0, keepdims=True)
        for k in (2, 1, 0):
            j = 3 - k
            du = du + pltpu.roll(ext_d, n_ext - j, 0)[:ts] * cw[k:k + 1]
            gw_ref[k:k + 1, :] += jnp.sum(dp * pltpu.roll(ext_u, j, 0)[CONV_HALO:], 0, keepdims=True)
        gb_ref[...] += jnp.sum(dp, 0, keepdims=True)
        du_ref[...] = du.astype(BF)

    return pl.pallas_call(
        body, name="conv_bwd", grid=(nb, nt),
        in_specs=[pl.BlockSpec((ts, CONV_CH), lambda b, i: (b * nt + i, 0)),
                  pl.BlockSpec((CONV_HALO, CONV_CH), lambda b, i: (jnp.minimum((b * nt + i + 1) * hb, last_block), 0)),
                  pl.BlockSpec((ts, CONV_CH), lambda b, i: (b * nt + i, 1)),
                  pl.BlockSpec((CONV_HALO, CONV_CH), lambda b, i: (jnp.maximum((b * nt + i) * hb - 1, 0), 1)),
                  pl.BlockSpec((4, CONV_CH), lambda b, i: (0, 0))],
        out_specs=[pl.BlockSpec((ts, CONV_CH), lambda b, i: (b * nt + i, 0)),
                   pl.BlockSpec((8, CONV_CH), lambda b, i: (0, 0)), pl.BlockSpec((1, CONV_CH), lambda b, i: (0, 0))],
        out_shape=[_out((nb * seq, CONV_CH), BF), _out((8, CONV_CH), F32),
                   _out((1, CONV_CH), F32)],
        compiler_params=_cp(("arbitrary", "arbitrary"), 48))(*_pin(d_pre, d_pre, proj, proj, conv_w))


def _in_proj_bwd(d_parts, w_in_t, x, dh1, mod3, g_mix, seq, token):
    t, d = x.shape
    nb = t // seq
    tm = min(256, seq)
    tps = seq // tm

    widths = [p.shape[1] for p in d_parts]

    def body(d0_ref, d1_ref, d2_ref, d3_ref, w_ref, x_ref, dh1_ref, mod_ref, g_ref, tok_ref, gx_ref, acc_ref, gg_ref):
        i = pl.program_id(0)
        du = None
        off = 0
        for p_ref, wd in zip((d0_ref, d1_ref, d2_ref, d3_ref), widths):
            part = _nn(p_ref[...], w_ref[off:off + wd, :])
            du = part if du is None else du + part
            off += wd
        xv = x_ref[...]
        r = lax.rsqrt(jnp.mean(xv * xv, -1, keepdims=True) + EPS)
        hh = xv * r
        n1 = hh * g_ref[...]
        dn1 = du * (1.0 + mod_ref[0, 1:2, :])
        dhat = dn1 * g_ref[...]
        gx_ref[...] = dh1_ref[...] + r * (dhat - hh * jnp.mean(dhat * hh, -1, keepdims=True))

        @pl.when(i == 0)
        def _():
            gg_ref[...] = jnp.zeros_like(gg_ref)

        @pl.when(i % tps == 0)
        def _():
            acc_ref[...] = jnp.zeros_like(acc_ref)

        gg_ref[...] += jnp.sum(dn1 * hh, 0, keepdims=True)
        acc_ref[0, 0:1, :] += jnp.sum(du, 0, keepdims=True)
        acc_ref[0, 1:2, :] += jnp.sum(du * n1, 0, keepdims=True)

    row = lambda i: (i, 0)
    vec = pl.BlockSpec((1, d), lambda i: (0, 0))
    return pl.pallas_call(
        body, name="in_proj_bwd", grid=(t // tm,),
        in_specs=[pl.BlockSpec((tm, wd), row) for wd in widths] +
                 [pl.BlockSpec(w_in_t.shape, lambda i: (0, 0)), pl.BlockSpec((tm, d), row),
                  pl.BlockSpec((tm, d), row), pl.BlockSpec((1, N_MOD, d), lambda i: (i // tps, 0, 0)), vec, _token_spec()],
        out_specs=[pl.BlockSpec((tm, d), row), pl.BlockSpec((1, 8, d), lambda i: (i // tps, 0, 0)), vec],
        out_shape=[_out((t, d), F32), _out((nb, 8, d), F32),
                   _out((1, d), F32)],
        compiler_params=_cp(("arbitrary",), 48))(*_pin(*d_parts, w_in_t, x, dh1, mod3, g_mix, token))


_VEC_LAYOUT = (("g_mix", 1024), ("conv_b", 1536), ("g_ssd", 1024), ("pool_scale", 512), ("g_mlp", 1024),
               ("g_final", 1024), ("dt_bias", 128), ("a_log", 128), ("d_skip_lanes", 1024), ("sq_err", 1024))
_VEC_OFFSET = {}
_off = 0
for _name, _n in _VEC_LAYOUT:
    _VEC_OFFSET[_name] = _off
    _off += _n
_VEC_LANES = _off
_SMALL_PARAMS = ("b_ada", "g_mix", "conv_w", "conv_b", "dt_bias", "a_log", "d_skip", "g_ssd", "w_pool", "pool_scale",
                 "g_mlp", "g_final")


def _pack_vec(parts):
    cols = []
    for name, n in _VEC_LAYOUT:
        v = parts[name]
        if v.shape[1] < n:
            v = jnp.pad(v, ((0, 0), (0, n - v.shape[1])))
        cols.append(v)
    return jnp.concatenate(cols, 1)


def _small_adam(vec_all, wpool_all, convw_all, dmod_all, params):
    names = _SMALL_PARAMS
    nin = 4 + 3 * len(names)

    def body(*refs):
        vec_ref, wp_ref, cw_ref, dm_ref = refs[:4]
        prm = {n: refs[4 + 3 * i:7 + 3 * i] for i, n in enumerate(names)}
        loss_ref = refs[nin]
        outs = {n: refs[nin + 1 + 4 * i:nin + 5 + 4 * i] for i, n in enumerate(names)}
        vsum = vec_ref[0]
        for s in range(1, N_DEV):
            vsum = vsum + vec_ref[s]

        def lanes(name, n):
            off = _VEC_OFFSET[name]
            return vsum[:, off:off + n]

        grads = {n: lanes(n, prm[n][0].shape[1]) for n in ("g_mix", "conv_b", "g_ssd", "pool_scale", "g_mlp", "g_final", "dt_bias")}
        grads["a_log"] = lanes("a_log", SSD_HEADS) * (-jnp.exp(prm["a_log"][0][...]))
        per_lane = jnp.broadcast_to(lanes("d_skip_lanes", SSD_INNER), (8, SSD_INNER))
        grads["d_skip"] = _exact_nn(per_lane, _head_reduce_matrix(SSD_INNER, SSD_HEADS))[0:1]
        gwp = wp_ref[0].astype(F32)
        gcw = cw_ref[0]
        gb = jnp.sum(dm_ref[0], 0, keepdims=True)
        for s in range(1, N_DEV):
            gwp = gwp + wp_ref[s].astype(F32)
            gcw = gcw + cw_ref[s]
            gb = gb + jnp.sum(dm_ref[s], 0, keepdims=True)
        grads["w_pool"] = gwp
        grads["conv_w"] = gcw[0:4]
        grads["b_ada"] = gb
        total = jnp.sum(lanes("sq_err", D_MODEL), 1, keepdims=True) * (0.5 / D_MODEL)
        loss_ref[...] = jnp.broadcast_to(total, loss_ref.shape)
        for n in names:
            w_ref, m_ref, v_ref = prm[n]
            g = grads[n]
            d, m2, v2 = _adam_math(w_ref[...], g, m_ref[...], v_ref[...])
            g_ref, d_ref, m2_ref, v2_ref = outs[n]
            g_ref[...] = g
            d_ref[...] = d
            m2_ref[...] = m2
            v2_ref[...] = v2

    flat = [vec_all, wpool_all, convw_all, dmod_all]
    out_shape = [jax.ShapeDtypeStruct((1, 128), F32)]
    for n in names:
        flat += list(params[n])
        out_shape += [jax.ShapeDtypeStruct(params[n][0].shape, F32)] * 4
    vm = pl.BlockSpec(memory_space=pltpu.VMEM)
    res = pl.pallas_call(body, name="small_adam", out_shape=out_shape, in_specs=[vm] * len(flat),
                         out_specs=[vm] * len(out_shape), compiler_params=_cp(vmem_mb=48))(*flat)
    return res[0], {n: res[1 + 4 * i:5 + 4 * i] for i, n in enumerate(names)}


_WEIGHTS = ("w_ada", "b_ada", "g_mix", "w_in", "conv_w", "conv_b", "dt_bias", "a_log", "d_skip", "g_ssd", "w_pool",
            "pool_scale", "w_out", "g_mlp", "w_up", "w_down", "g_final")


def _local_step(x2, tg2, mod3, seq, w_in_t, first_token, weights_later, start_reduce, conv_w_full, sp):
    t, d = x2.shape
    nb = t // seq
    dskip_e = jnp.repeat(sp["d_skip"], SSD_HEAD_DIM, axis=1)
    proj, u1 = _in_proj(x2, mod3, sp["g_mix"], w_in_t, seq, first_token)
    y_pool, p = _pool_fwd(proj, sp["w_pool"], sp["pool_scale"], nb, seq)
    y_ssd, yssm, h_prev, pre = _ssd_fwd(proj, conv_w_full, sp["conv_b"], sp["dt_bias"], sp["a_log"], dskip_e, sp["g_ssd"], nb, seq)
    w_out_f, w_up4, w_down4 = weights_later(y_ssd)
    w_down_f = w_down4.reshape(D_FF, d)
    h1, o, u2 = _out_proj(y_pool, y_ssd, w_out_f, x2, mod3, sp["g_mlp"], seq)
    a_up = _mlp_up(u2, w_up4)
    d_dn, dh2, sq, gg_final, d_gf = _mlp_down_loss(a_up, w_down_f, h1, mod3, sp["g_final"], tg2, seq)

    gw_down = _tn_matmul(a_up, d_dn, 512, d, "grad_w_down", square_relu=True)
    tok = start_reduce("w_down", gw_down.reshape(N_CHIPS, D_FF // N_CHIPS, d))
    d_a = _mlp_down_bwd(d_dn, w_down4, a_up, tok)
    gw_up4 = _tn_matmul(u2, d_a, 512, d, "grad_w_up", out3=True)
    tok = start_reduce("w_up", gw_up4)
    dh1, d_o, accf, gg_mlp = _mlp_up_bwd(d_a, w_up4, h1, dh2, o, mod3, sp["g_mlp"], seq, tok)
    gw_out_pool = _tn_matmul(y_pool, d_o, 512, d, "grad_w_out_pool")
    gw_out_ssd = _tn_matmul(y_ssd, d_o, 512, d, "grad_w_out_ssd")
    gw_out = jnp.concatenate([gw_out_pool, gw_out_ssd], 0)
    tok = start_reduce("w_out", gw_out.reshape(N_CHIPS, gw_out.shape[0] // N_CHIPS, d))
    d_ypool, d_yssd = _out_proj_bwd(d_o, w_out_f, tok)
    d_upool, gw_pool, g_ps = _pool_bwd(d_ypool, p, sp["w_pool"], sp["pool_scale"], nb, seq)
    d_z, d_pre, d_udt, gg_ssd, gdsk, ga, gdtb = _ssd_bwd(proj, pre, d_yssd, yssm, h_prev, sp["dt_bias"], sp["a_log"],
                                                        dskip_e, sp["g_ssd"], nb, seq)
    d_uxbc, gconvw, gconvb = _conv_bwd(d_pre, proj, conv_w_full, nb, seq)
    gw_in_t = _grad_w_in_t(d_upool, d_z, d_uxbc, d_udt, u1)
    tok = start_reduce("w_in", gw_in_t[:IN_WIDTH].reshape(N_CHIPS, IN_WIDTH // N_CHIPS, d))
    gx, accm, gg_mix = _in_proj_bwd([d_upool, d_z, d_uxbc, d_udt], w_in_t, x2, dh1, mod3, sp["g_mix"], seq, tok)

    d_mod = jnp.concatenate([accm[:, 0], accm[:, 1], accf[:, 2], accf[:, 0], accf[:, 1], d_gf[:, 0]], 1)
    vec = _pack_vec({"g_mix": gg_mix, "conv_b": gconvb, "g_ssd": gg_ssd, "pool_scale": g_ps, "g_mlp": gg_mlp,
                     "g_final": gg_final, "dt_bias": gdtb, "a_log": ga, "d_skip_lanes": gdsk, "sq_err": sq})
    return gx, d_mod, vec, gw_pool, gconvw


def kernel(x, c, w_ada, b_ada, g_mix, w_in, conv_w, conv_b, dt_bias, a_log, d_skip, g_ssd, w_pool, pool_scale, w_out, g_mlp, w_up, w_down, g_final, loss_target, m_w_ada, m_b_ada, m_g_mix, m_w_in, m_conv_w, m_conv_b, m_dt_bias, m_a_log, m_d_skip, m_g_ssd, m_w_pool, m_pool_scale, m_w_out, m_g_mlp, m_w_up, m_w_down, m_g_final, v_w_ada, v_b_ada, v_g_mix, v_w_in, v_conv_w, v_conv_b, v_dt_bias, v_a_log, v_d_skip, v_g_ssd, v_w_pool, v_pool_scale, v_w_out, v_g_mlp, v_w_up, v_w_down, v_g_final):
    nb, seq, d = x.shape
    t = nb * seq
    xi, yi, ci = _mesh_pos()
    chip = 2 * xi + yi
    me = 4 * xi + 2 * yi + ci
    ada_cols = w_ada.shape[2]
    conv_cols = conv_w.shape[2]
    in_cols = w_in.shape[2]
    w_in_s, m_w_in_s, v_w_in_s = w_in[0].T, m_w_in[0].T, v_w_in[0].T

    c8, convw8 = _all_gather_small([c, conv_w[0]])
    c_all = c8.reshape(N_DEV * nb, d)
    conv_w_full = convw8[0::2].transpose(1, 0, 2).reshape(4, N_CHIPS * conv_cols)
    b_shard = lax.dynamic_slice(b_ada, (0, chip * ada_cols), (1, ada_cols))
    mod_part, c_act = _ada_mod(c_all, w_ada[0], b_shard)
    (mod8,) = _all_gather_small([mod_part])
    mod_all = mod8[0::2].transpose(1, 0, 2).reshape(N_DEV * nb, N_CHIPS * ada_cols)
    mod3 = lax.dynamic_slice(mod_all, (nb * me, 0), (nb, N_CHIPS * ada_cols)).reshape(nb, N_MOD, d)

    (w_in4,) = _weight_all_gather([w_in_s.astype(BF)])
    w_in_t = jnp.pad(w_in4.reshape(N_CHIPS * in_cols, d), ((0, IN_PAD - N_CHIPS * in_cols), (0, 0)))
    later = [w_out[0].astype(BF), w_up[0].astype(BF), w_down[0].astype(BF)]
    g_send, g_recv, g_src, g_land, first_token = _ici_start(
        later, [jax.ShapeDtypeStruct((N_CHIPS,) + s.shape, BF) for s in later], _gather_sent, _gather_landing, "gather_start",
        after=w_in4)

    def weights_later(after):
        shards, lands = _ici_wait(g_send, g_recv, g_src, g_land, after, _gather_sent, _gather_landing, "gather_wait")
        w_out4, w_up4, w_down4 = _gather_finish(lands, shards)
        return w_out4.reshape(N_CHIPS * w_out.shape[1], d), w_up4, w_down4

    pending = {}

    def start_reduce(name, grad4):
        own, got = _grad_to_sibling([grad4])
        part = _add_pair(own[0], got[0])
        pending[name] = _ici_start([part], [part], _reduce_sent, _reduce_landing, "reduce_start_" + name, after=got[0])
        return pending[name][4]

    sp = dict(g_mix=g_mix, conv_b=conv_b, dt_bias=dt_bias, a_log=a_log, d_skip=d_skip, g_ssd=g_ssd,
              w_pool=w_pool[0], pool_scale=pool_scale, g_mlp=g_mlp, g_final=g_final.reshape(1, d))
    gx, d_mod, vec, gw_pool, gconvw = _local_step(
        x.reshape(t, d), loss_target.reshape(t, d), mod3, seq, w_in_t, first_token, weights_later, start_reduce, conv_w_full, sp)

    halves = []
    for name in ("w_in", "w_out", "w_up", "w_down"):
        r_send, r_recv, r_src, r_land, _ = pending[name]
        own, recv = _ici_wait(r_send, r_recv, r_src, r_land, gx, _reduce_sent, _reduce_landing, "reduce_wait_" + name)
        halves.append(_sum_slots(recv[0], own[0]))
    g_in, g_out, g_up, g_down = _halves_exchange(halves)

    vec8, wpool8, convw8g, dmod8 = _all_gather_small(
        [vec, gw_pool.reshape(4 * POOL_GROUP, POOL_GROUP).astype(BF), gconvw, d_mod])
    convw8s = lax.dynamic_slice(convw8g, (0, 0, chip * conv_cols), (N_DEV, 8, conv_cols))
    m_in = dict(b_ada=m_b_ada, g_mix=m_g_mix, conv_w=m_conv_w[0], conv_b=m_conv_b, dt_bias=m_dt_bias, a_log=m_a_log,
                d_skip=m_d_skip, g_ssd=m_g_ssd, w_pool=m_w_pool.reshape(4 * POOL_GROUP, POOL_GROUP), pool_scale=m_pool_scale,
                g_mlp=m_g_mlp, g_final=m_g_final.reshape(1, d))
    v_in = dict(b_ada=v_b_ada, g_mix=v_g_mix, conv_w=v_conv_w[0], conv_b=v_conv_b, dt_bias=v_dt_bias, a_log=v_a_log,
                d_skip=v_d_skip, g_ssd=v_g_ssd, w_pool=v_w_pool.reshape(4 * POOL_GROUP, POOL_GROUP), pool_scale=v_pool_scale,
                g_mlp=v_g_mlp, g_final=v_g_final.reshape(1, d))
    w_small = dict(sp, b_ada=b_ada, conv_w=conv_w[0], w_pool=w_pool.reshape(4 * POOL_GROUP, POOL_GROUP))
    loss_row, small = _small_adam(vec8, wpool8, convw8s, dmod8, {n: (w_small[n], m_in[n], v_in[n]) for n in _SMALL_PARAMS})

    dmod_all = dmod8.reshape(N_DEV * nb, N_CHIPS * ada_cols)
    dmod_cols = lax.dynamic_slice(dmod_all, (0, chip * ada_cols), (N_DEV * nb, ada_cols))
    res = {n: tuple(r.reshape(w.shape) for r in small[n])
           for n, w in (("b_ada", b_ada), ("g_mix", g_mix), ("conv_w", conv_w), ("conv_b", conv_b), ("dt_bias", dt_bias),
                        ("a_log", a_log), ("d_skip", d_skip), ("g_ssd", g_ssd), ("w_pool", w_pool), ("pool_scale", pool_scale),
                        ("g_mlp", g_mlp), ("g_final", g_final))}
    g_ada, d_ada, m_ada, v_ada = _adam_ada(c_act.T.astype(BF), dmod_cols, w_ada[0], m_w_ada[0], v_w_ada[0])
    res["w_ada"] = (g_ada[None], d_ada[None], m_ada[None], v_ada[None])
    dl, m2, v2 = _adam_big(g_in, w_in_s, m_w_in_s, v_w_in_s)
    res["w_in"] = (g_in.T[None], dl.T[None], m2.T[None], v2.T[None])
    for n, g, w, m, v in (("w_out", g_out, w_out, m_w_out, v_w_out), ("w_up", g_up, w_up, m_w_up, v_w_up),
                          ("w_down", g_down, w_down, m_w_down, v_w_down)):
        dl, m2, v2 = _adam_big(g, w[0], m[0], v[0])
        res[n] = (g[None], dl[None], m2[None], v2[None])

    loss = loss_row[0, 0]
    return (loss, gx.reshape(nb, seq, d), *[res[n][0] for n in _WEIGHTS], *[res[n][1] for n in _WEIGHTS],
            *[res[n][2] for n in _WEIGHTS], *[res[n][3] for n in _WEIGHTS])
```

```python
import functools

import jax
import jax.numpy as jnp
from jax import lax
from jax.experimental import pallas as pl
from jax.experimental.pallas import tpu as pltpu

F32 = jnp.float32
BF = jnp.bfloat16
MESH = pl.DeviceIdType.MESH

EPS = 1e-5
D_MODEL = 1024
POOL_WIDTH = 512
POOL_WINDOWS = (2, 4, 8, 16)
POOL_GROUP = 128
SSD_INNER = 1024
SSD_HEADS = 16
SSD_HEAD_DIM = 64
SSD_STATE = 128
GROUP_W = 512
CHUNK = 128
CONV_CH = 1536
OFF_Z = 512
OFF_XBC = 1536
OFF_DT = 3072
IN_WIDTH = 3088
IN_PAD = 3200
D_FF = 4096
N_MOD = 6
N_CHIPS = 4
N_DEV = 8
HALO = 16
CONV_HALO = 8

ADAM_LR = 0.001
ADAM_B1 = 0.9
ADAM_B2 = 0.999
ADAM_EPS = 1e-08
ADAM_WD = 0.01
ADAM_STEP = 10

VMEM_BYTES_V7X = 64 * 1024 * 1024


def _cp(semantics=None, vmem_mb=48, **kw):
    args = dict(vmem_limit_bytes=vmem_mb * 1024 * 1024, **kw)
    if semantics is not None:
        args["dimension_semantics"] = semantics
    return pltpu.CompilerParams(**args)


def _out(shape, dtype):
    return pltpu.HBM(shape, dtype)


def _pin(*arrays):
    return [pltpu.with_memory_space_constraint(a, pltpu.HBM) for a in arrays]


def _nn(a, b):
    return jnp.dot(a, b, preferred_element_type=F32)


def _nt(a, b):
    return lax.dot_general(a, b, (((1,), (1,)), ((), ())), preferred_element_type=F32)


def _tn(a, b):
    return lax.dot_general(a, b, (((0,), (0,)), ((), ())), preferred_element_type=F32)


def _split3(v):
    hi = v.astype(BF)
    r1 = v - hi.astype(F32)
    mid = r1.astype(BF)
    lo = (r1 - mid.astype(F32)).astype(BF)
    return hi, mid, lo


def _exact_nn(v, m01):
    hi, mid, lo = _split3(v)
    return _nn(hi, m01) + _nn(mid, m01) + _nn(lo, m01)


def _exact_nn_left(m01, v):
    hi, mid, lo = _split3(v)
    return _nn(m01, hi) + _nn(m01, mid) + _nn(m01, lo)


def _exact_nt_left(m01, v):
    hi, mid, lo = _split3(v)
    return _nt(m01, hi) + _nt(m01, mid) + _nt(m01, lo)


def _sigmoid(v):
    return 1.0 / (1.0 + jnp.exp(-v))


def _iota(shape, dim):
    return lax.broadcasted_iota(jnp.int32, shape, dim)


def _head_expand_matrix(heads, width):
    return (_iota((heads, width), 1) // SSD_HEAD_DIM == _iota((heads, width), 0)).astype(BF)


def _head_reduce_matrix(width, heads):
    return (_iota((width, heads), 0) // SSD_HEAD_DIM == _iota((width, heads), 1)).astype(BF)


def _mesh_pos():
    return lax.axis_index("x"), lax.axis_index("y"), lax.axis_index("c")


def _flip(v, bit):
    return v + bit - 2 * bit * v


def _all_gather_small(arrays):
    n = len(arrays)

    def body(*refs):
        in_refs, out_refs = refs[:n], refs[n:2 * n]
        send_sems, recv_sems, local_sems = refs[2 * n:]
        x, y, c = _mesh_pos()
        me = 4 * x + 2 * y + c
        local = []
        for a in range(n):
            cp = pltpu.make_async_copy(in_refs[a], out_refs[a].at[me], local_sems.at[a])
            cp.start()
            local.append(cp)
        sends = []
        for k in range(1, N_DEV):
            peer = (_flip(x, (k >> 2) & 1), _flip(y, (k >> 1) & 1), _flip(c, k & 1))
            for a in range(n):
                cp = pltpu.make_async_remote_copy(
                    src_ref=in_refs[a], dst_ref=out_refs[a].at[me],
                    send_sem=send_sems.at[a, k], recv_sem=recv_sems.at[a, k],
                    device_id=peer, device_id_type=MESH)
                cp.start()
                sends.append(cp)
        for k in range(1, N_DEV):
            px, py, pc = _flip(x, (k >> 2) & 1), _flip(y, (k >> 1) & 1), _flip(c, k & 1)
            src = 4 * px + 2 * py + pc
            for a in range(n):
                pltpu.make_async_remote_copy(
                    src_ref=in_refs[a], dst_ref=out_refs[a].at[src],
                    send_sem=send_sems.at[a, k], recv_sem=recv_sems.at[a, k],
                    device_id=(px, py, pc), device_id_type=MESH).wait_recv()
        for cp in sends:
            cp.wait_send()
        for cp in local:
            cp.wait()

    vm = pl.BlockSpec(memory_space=pltpu.VMEM)
    return pl.pallas_call(
        body, name="all_gather_small",
        out_shape=[jax.ShapeDtypeStruct((N_DEV,) + a.shape, a.dtype) for a in arrays],
        in_specs=[vm] * n, out_specs=[vm] * n,
        scratch_shapes=[pltpu.SemaphoreType.DMA((n, N_DEV)), pltpu.SemaphoreType.DMA((n, N_DEV)),
                        pltpu.SemaphoreType.DMA((n,))],
        compiler_params=_cp(vmem_mb=32),
    )(*arrays)


def _weight_all_gather(shards):
    n = len(shards)
    any_spec = _HBM

    def body(*refs):
        in_refs, out_refs = refs[:n], refs[n:2 * n]
        send_sems, recv_sems, fsend_sems, frecv_sems, local_sems = refs[2 * n:]
        x, y, c = _mesh_pos()
        chip = 2 * x + y
        local, sends = [], []
        for a in range(n):
            cp = pltpu.make_async_copy(in_refs[a], out_refs[a].at[chip], local_sems.at[a])
            cp.start()
            local.append(cp)

        def half(a, which):
            hc = shards[a].shape[1] // 2
            return pl.ds(pl.multiple_of(which * hc, 128), hc)

        for j in range(1, N_CHIPS):
            px, py = _flip(x, (j >> 1) & 1), _flip(y, j & 1)
            for a in range(n):
                cp = pltpu.make_async_remote_copy(
                    src_ref=in_refs[a].at[:, half(a, c)], dst_ref=out_refs[a].at[chip, :, half(a, c)],
                    send_sem=send_sems.at[a, j], recv_sem=recv_sems.at[a, j],
                    device_id=(px, py, c), device_id_type=MESH)
                cp.start()
                sends.append(cp)
        for j in range(1, N_CHIPS):
            px, py = _flip(x, (j >> 1) & 1), _flip(y, j & 1)
            src = 2 * px + py
            for a in range(n):
                landed = out_refs[a].at[src, :, half(a, c)]
                pltpu.make_async_remote_copy(
                    src_ref=landed, dst_ref=landed, send_sem=send_sems.at[a, j], recv_sem=recv_sems.at[a, j],
                    device_id=(px, py, c), device_id_type=MESH).wait_recv()
                cp = pltpu.make_async_remote_copy(
                    src_ref=landed, dst_ref=landed, send_sem=fsend_sems.at[a, j], recv_sem=frecv_sems.at[a, j],
                    device_id=(x, y, 1 - c), device_id_type=MESH)
                cp.start()
                sends.append(cp)
        for j in range(1, N_CHIPS):
            px, py = _flip(x, (j >> 1) & 1), _flip(y, j & 1)
            src = 2 * px + py
            for a in range(n):
                other = out_refs[a].at[src, :, half(a, 1 - c)]
                pltpu.make_async_remote_copy(
                    src_ref=other, dst_ref=other, send_sem=fsend_sems.at[a, j], recv_sem=frecv_sems.at[a, j],
                    device_id=(x, y, 1 - c), device_id_type=MESH).wait_recv()
        for cp in sends:
            cp.wait_send()
        for cp in local:
            cp.wait()

    return pl.pallas_call(
        body, name="weight_all_gather",
        out_shape=[_out((N_CHIPS,) + s.shape, s.dtype) for s in shards],
        in_specs=[any_spec] * n, out_specs=[any_spec] * n,
        scratch_shapes=[pltpu.SemaphoreType.DMA((n, N_CHIPS))] * 4 + [pltpu.SemaphoreType.DMA((n,))],
        compiler_params=_cp(vmem_mb=16),
    )(*shards)


_HBM = pl.BlockSpec(memory_space=pltpu.HBM)
_SEM = pl.BlockSpec(memory_space=pltpu.SEMAPHORE)
_DATAFLOW = pltpu.SideEffectType.DATAFLOW_SIDE_EFFECTING


def _peer_chip(x, y, j):
    return _flip(x, (j >> 1) & 1), _flip(y, j & 1)


def _ici_start(srcs, land_shapes, sent, landing, name, after):
    n = len(srcs)

    def body(*refs):
        src_refs, land_refs = refs[:n], refs[n:2 * n]
        send_sems, recv_sems = refs[2 * n + 1], refs[2 * n + 2]
        token = refs[-1]
        x, y, c = _mesh_pos()
        for j in range(1, N_CHIPS):
            px, py = _peer_chip(x, y, j)
            for a in range(n):
                pltpu.make_async_remote_copy(
                    src_ref=sent(src_refs[a], c, 2 * px + py), dst_ref=landing(land_refs[a], c, 2 * x + y),
                    send_sem=send_sems.at[a * (N_CHIPS - 1) + j - 1], recv_sem=recv_sems.at[a * (N_CHIPS - 1) + j - 1],
                    device_id=(px, py, c), device_id_type=MESH).start()
        token[...] = jnp.zeros_like(token)

    sems = pltpu.SemaphoreType.DMA((n * (N_CHIPS - 1),))
    lands = [pltpu.with_memory_space_constraint(lax.empty(s.shape, s.dtype), pltpu.HBM) for s in land_shapes]
    outs = pl.pallas_call(
        body, name=name,
        out_shape=(sems, sems, *[pltpu.HBM(s.shape, s.dtype) for s in srcs],
                   *[pltpu.HBM(s.shape, s.dtype) for s in land_shapes], jax.ShapeDtypeStruct((8, 128), F32)),
        in_specs=[_HBM] * (2 * n + 1), out_specs=[_SEM, _SEM] + [_HBM] * (2 * n) + [pl.BlockSpec(memory_space=pltpu.VMEM)],
        input_output_aliases={i: 2 + i for i in range(2 * n)},
        compiler_params=pltpu.CompilerParams(has_side_effects=_DATAFLOW),
    )(*_pin(*srcs), *lands, *_pin(after))
    return outs[0], outs[1], outs[2:2 + n], outs[2 + n:2 + 2 * n], outs[-1]


def _ici_wait(send_sems, recv_sems, src_thru, land_thru, after, sent, landing, name):
    n = len(src_thru)

    def body(*refs):
        src_refs, land_refs = refs[:n], refs[n:2 * n]
        send_sems, recv_sems = refs[2 * n], refs[2 * n + 1]
        x, y, c = _mesh_pos()
        for j in range(1, N_CHIPS):
            px, py = _peer_chip(x, y, j)
            for a in range(n):
                cp = pltpu.make_async_remote_copy(
                    src_ref=sent(src_refs[a], c, 2 * px + py), dst_ref=landing(land_refs[a], c, 2 * px + py),
                    send_sem=send_sems.at[a * (N_CHIPS - 1) + j - 1], recv_sem=recv_sems.at[a * (N_CHIPS - 1) + j - 1],
                    device_id=(px, py, c), device_id_type=MESH)
                cp.wait_send()
                cp.wait_recv()

    outs = pl.pallas_call(
        body, name=name,
        out_shape=tuple(pltpu.HBM(s.shape, s.dtype) for s in (*src_thru, *land_thru)),
        in_specs=[_HBM] * (2 * n) + [_SEM, _SEM, _HBM], out_specs=[_HBM] * (2 * n),
        input_output_aliases={i: i for i in range(2 * n)},
        compiler_params=pltpu.CompilerParams(has_side_effects=_DATAFLOW),
    )(*src_thru, *land_thru, send_sems, recv_sems, *_pin(after))
    return outs[:n], outs[n:]


def _col_half(ref, which, lead=()):
    hc = ref.shape[-1] // 2
    return ref.at[(*lead, slice(None), pl.ds(pl.multiple_of(which * hc, 128), hc))]


def _gather_sent(ref, c, dst_chip):
    return _col_half(ref, c)


def _gather_landing(ref, c, src_chip):
    return _col_half(ref, c, lead=(src_chip,))


def _reduce_copy(src_ref, land_ref, send_sems, recv_sems, k, receiving):
    x, y, c = _mesh_pos()
    px, py, pc = _flip(x, (k >> 2) & 1), _flip(y, (k >> 1) & 1), _flip(c, k & 1)
    hc = src_ref.shape[2] // 2
    src = src_ref.at[2 * px + py, :, pl.ds(pl.multiple_of(pc * hc, 128), hc)]
    slot = (4 * px + 2 * py + pc) if receiving else (4 * x + 2 * y + c)
    return pltpu.make_async_remote_copy(
        src_ref=src, dst_ref=land_ref.at[slot], send_sem=send_sems.at[k - 1], recv_sem=recv_sems.at[k - 1],
        device_id=(px, py, pc), device_id_type=MESH)


def _reduce_start(grad4, name):
    k4, r, cols = grad4.shape

    def body(src_ref, land_ref, send_sems, recv_sems, src_thru, land_thru, token):
        for k in range(1, N_DEV):
            _reduce_copy(src_ref, land_ref, send_sems, recv_sems, k, receiving=False).start()
        token[...] = jnp.zeros_like(token)

    sems = pltpu.SemaphoreType.DMA((N_DEV - 1,))
    land = pltpu.with_memory_space_constraint(lax.empty((N_DEV, r, cols // 2), grad4.dtype), pltpu.HBM)
    return pl.pallas_call(
        body, name=name,
        out_shape=(sems, sems, pltpu.HBM(grad4.shape, grad4.dtype), pltpu.HBM(land.shape, land.dtype),
                   jax.ShapeDtypeStruct((8, 128), F32)),
        in_specs=[_HBM, _HBM], out_specs=[_SEM, _SEM, _HBM, _HBM, pl.BlockSpec(memory_space=pltpu.VMEM)],
        input_output_aliases={0: 2, 1: 3},
        compiler_params=pltpu.CompilerParams(has_side_effects=_DATAFLOW),
    )(*_pin(grad4), land)


def _reduce_wait(send_sems, recv_sems, src_thru, land_thru, after, name):
    def body(src_ref, land_ref, send_sems, recv_sems, after_ref, src_out, land_out):
        for k in range(1, N_DEV):
            cp = _reduce_copy(src_ref, land_ref, send_sems, recv_sems, k, receiving=True)
            cp.wait_send()
            cp.wait_recv()

    return pl.pallas_call(
        body, name=name,
        out_shape=(pltpu.HBM(src_thru.shape, src_thru.dtype), pltpu.HBM(land_thru.shape, land_thru.dtype)),
        in_specs=[_HBM, _HBM, _SEM, _SEM, _HBM], out_specs=[_HBM, _HBM],
        input_output_aliases={0: 0, 1: 1},
        compiler_params=pltpu.CompilerParams(has_side_effects=_DATAFLOW),
    )(src_thru, land_thru, send_sems, recv_sems, *_pin(after))


def _sum_eight(recv, grad4, pos):
    n, r, hc = recv.shape
    tc = hc // 2

    def body(pos_ref, r_ref, g_ref, o_ref):
        chip, me = pos_ref[1], pos_ref[2]
        o_ref[...] = jnp.zeros_like(o_ref)
        for s in range(n):
            @pl.when(me == s)
            def _():
                o_ref[...] += g_ref[chip].astype(F32)

            @pl.when(me != s)
            def _():
                o_ref[...] += r_ref[s].astype(F32)

    grid_spec = pltpu.PrefetchScalarGridSpec(
        num_scalar_prefetch=1, grid=(2,),
        in_specs=[pl.BlockSpec((n, r, tc), lambda i, pos: (0, 0, i)),
                  pl.BlockSpec((grad4.shape[0], r, tc), lambda i, pos: (0, 0, pos[0] * 2 + i))],
        out_specs=pl.BlockSpec((r, tc), lambda i, pos: (0, i)))
    return pl.pallas_call(body, name="sum_eight", grid_spec=grid_spec, out_shape=_out((r, hc), F32),
                          compiler_params=_cp(("parallel",), 32))(pos, *_pin(recv, grad4))


def _gather_finish(lands, shards):
    n = len(lands)
    any_spec = _HBM

    def body(*refs):
        shard_refs, out_refs = refs[n:2 * n], refs[2 * n:3 * n]
        send_sems, recv_sems, local_sems = refs[3 * n:]
        x, y, c = _mesh_pos()
        chip = 2 * x + y
        local, sends = [], []
        for a in range(n):
            cp = pltpu.make_async_copy(shard_refs[a], out_refs[a].at[chip], local_sems.at[a])
            cp.start()
            local.append(cp)
        for j in range(1, N_CHIPS):
            px, py = _peer_chip(x, y, j)
            for a in range(n):
                landed = _col_half(out_refs[a], c, lead=(2 * px + py,))
                cp = pltpu.make_async_remote_copy(
                    src_ref=landed, dst_ref=landed, send_sem=send_sems.at[a, j], recv_sem=recv_sems.at[a, j],
                    device_id=(x, y, 1 - c), device_id_type=MESH)
                cp.start()
                sends.append(cp)
        for j in range(1, N_CHIPS):
            px, py = _peer_chip(x, y, j)
            for a in range(n):
                other = _col_half(out_refs[a], 1 - c, lead=(2 * px + py,))
                pltpu.make_async_remote_copy(
                    src_ref=other, dst_ref=other, send_sem=send_sems.at[a, j], recv_sem=recv_sems.at[a, j],
                    device_id=(x, y, 1 - c), device_id_type=MESH).wait_recv()
        for cp in sends:
            cp.wait_send()
        for cp in local:
            cp.wait()

    return pl.pallas_call(
        body, name="gather_finish",
        out_shape=[_out(l.shape, l.dtype) for l in lands],
        in_specs=[any_spec] * (2 * n), out_specs=[any_spec] * n,
        input_output_aliases={i: i for i in range(n)},
        scratch_shapes=[pltpu.SemaphoreType.DMA((n, N_CHIPS))] * 2 + [pltpu.SemaphoreType.DMA((n,))],
        compiler_params=_cp(vmem_mb=16),
    )(*lands, *shards)


def _halves_exchange(halves):
    n = len(halves)
    any_spec = _HBM

    def body(*refs):
        in_refs, out_refs = refs[:n], refs[n:2 * n]
        send_sems, recv_sems, local_sems = refs[2 * n:]
        x, y, c = _mesh_pos()
        copies, local = [], []
        for a in range(n):
            hc = halves[a].shape[1]
            mine = pl.ds(pl.multiple_of(c * hc, 128), hc)
            lc = pltpu.make_async_copy(in_refs[a], out_refs[a].at[:, mine], local_sems.at[a])
            lc.start()
            local.append(lc)
            cp = pltpu.make_async_remote_copy(
                src_ref=in_refs[a], dst_ref=out_refs[a].at[:, mine],
                send_sem=send_sems.at[a], recv_sem=recv_sems.at[a],
                device_id=(x, y, 1 - c), device_id_type=MESH)
            cp.start()
            copies.append(cp)
        for a in range(n):
            hc = halves[a].shape[1]
            theirs = pl.ds(pl.multiple_of((1 - c) * hc, 128), hc)
            pltpu.make_async_remote_copy(
                src_ref=in_refs[a], dst_ref=out_refs[a].at[:, theirs],
                send_sem=send_sems.at[a], recv_sem=recv_sems.at[a],
                device_id=(x, y, 1 - c), device_id_type=MESH).wait_recv()
        for cp in copies:
            cp.wait_send()
        for lc in local:
            lc.wait()

    return pl.pallas_call(
        body, name="halves_exchange",
        out_shape=[_out((h.shape[0], 2 * h.shape[1]), h.dtype) for h in halves],
        in_specs=[any_spec] * n, out_specs=[any_spec] * n,
        scratch_shapes=[pltpu.SemaphoreType.DMA((n,))] * 3,
        compiler_params=_cp(vmem_mb=16),
    )(*halves)


def _adam_math(w, g, m, v):
    m2 = ADAM_B1 * m + (1.0 - ADAM_B1) * g
    v2 = ADAM_B2 * v + (1.0 - ADAM_B2) * (g * g)
    m_hat = m2 / (1.0 - ADAM_B1 ** ADAM_STEP)
    v_hat = v2 / (1.0 - ADAM_B2 ** ADAM_STEP)
    delta = -ADAM_LR * (m_hat / (jnp.sqrt(v_hat) + ADAM_EPS) + ADAM_WD * w)
    return delta, m2, v2


def _adam_big(g, w, m, v):
    r, c = w.shape
    tc = c // 4

    def body(g_ref, w_ref, m_ref, v_ref, d_ref, m2_ref, v2_ref):
        d, m2, v2 = _adam_math(w_ref[...], g_ref[...], m_ref[...], v_ref[...])
        d_ref[...] = d
        m2_ref[...] = m2
        v2_ref[...] = v2

    spec = pl.BlockSpec((r, tc), lambda i: (0, i))
    sh = _out((r, c), F32)
    return pl.pallas_call(body, name="adam_big", grid=(4,), in_specs=[spec] * 4, out_specs=[spec] * 3,
                          out_shape=[sh, sh, sh], compiler_params=_cp(("parallel",), 32))(*_pin(g, w, m, v))


def _adam_ada(c_act_t, dmod_cols, w, m, v):
    r, c = w.shape
    tc = 512

    def body(ct_ref, dm_ref, w_ref, m_ref, v_ref, g_ref, d_ref, m2_ref, v2_ref):
        g = _nn(ct_ref[...], dm_ref[...].astype(BF))
        d, m2, v2 = _adam_math(w_ref[...], g, m_ref[...], v_ref[...])
        g_ref[...] = g
        d_ref[...] = d
        m2_ref[...] = m2
        v2_ref[...] = v2

    spec = pl.BlockSpec((r, tc), lambda i: (0, i))
    sh = _out((r, c), F32)
    return pl.pallas_call(
        body, name="adam_ada", grid=(c // tc,),
        in_specs=[pl.BlockSpec(c_act_t.shape, lambda i: (0, 0)), pl.BlockSpec((dmod_cols.shape[0], tc), lambda i: (0, i)),
                  spec, spec, spec],
        out_specs=[spec] * 4, out_shape=[sh] * 4, compiler_params=_cp(("parallel",), 48))(*_pin(c_act_t, dmod_cols, w, m, v))


def _ada_mod(c_all, w_shard, b_shard):
    nb, d = c_all.shape
    cols = w_shard.shape[1]
    tc = 512

    def body(c_ref, w_ref, b_ref, mod_ref, act_ref):
        cv = c_ref[...]
        act = cv * _sigmoid(cv)
        act_ref[...] = act
        mod_ref[...] = _nn(act.astype(BF), w_ref[...].astype(BF)) + b_ref[...]

    return pl.pallas_call(
        body, name="ada_mod", grid=(cols // tc,),
        in_specs=[pl.BlockSpec((nb, d), lambda i: (0, 0)), pl.BlockSpec((d, tc), lambda i: (0, i)),
                  pl.BlockSpec((1, tc), lambda i: (0, i))],
        out_specs=[pl.BlockSpec((nb, tc), lambda i: (0, i)), pl.BlockSpec((nb, d), lambda i: (0, 0))],
        out_shape=[_out((nb, cols), F32), _out((nb, d), F32)],
        compiler_params=_cp(("arbitrary",), 32))(*_pin(c_all, w_shard, b_shard))


SUB_ROWS = 128


def _sub_rows(tm):
    return [slice(s, s + SUB_ROWS) for s in range(0, tm, SUB_ROWS)] if tm > SUB_ROWS else [slice(0, tm)]


def _token_spec():
    return pl.BlockSpec((8, 128), lambda *_: (0, 0))


def _in_proj(x, mod3, g_mix, w_in_t, seq, token):
    t, d = x.shape
    tm = 256
    tps = seq // tm

    def body(x_ref, mod_ref, g_ref, w_ref, tok_ref, proj_ref, u1_ref):
        xv = x_ref[...]
        r = lax.rsqrt(jnp.mean(xv * xv, -1, keepdims=True) + EPS)
        u = (xv * r * g_ref[...]) * (1.0 + mod_ref[0, 1:2, :]) + mod_ref[0, 0:1, :]
        ub = u.astype(BF)
        u1_ref[...] = ub
        proj_ref[...] = _nt(ub, w_ref[...])

    return pl.pallas_call(
        body, name="in_proj", grid=(t // tm,),
        in_specs=[pl.BlockSpec((tm, d), lambda i: (i, 0)), pl.BlockSpec((1, N_MOD, d), lambda i: (i // tps, 0, 0)),
                  pl.BlockSpec((1, d), lambda i: (0, 0)), pl.BlockSpec((IN_PAD, d), lambda i: (0, 0)), _token_spec()],
        out_specs=[pl.BlockSpec((tm, IN_PAD), lambda i: (i, 0)), pl.BlockSpec((tm, d), lambda i: (i, 0))],
        out_shape=[_out((t, IN_PAD), F32), _out((t, d), BF)],
        compiler_params=_cp(("parallel",), 48))(*_pin(x, mod3, g_mix, w_in_t, token))


def _pool_tile(seq):
    return min(512, seq)


def _pool_fwd(proj, w_pool, pool_scale, nb, seq):
    ts = _pool_tile(seq)
    nt = seq // ts

    def body(u_ref, halo_ref, wp_ref, ps_ref, yp_ref, p_ref):
        i = pl.program_id(1)
        halo = jnp.where(i == 0, 0.0, halo_ref[...])
        u = u_ref[...]
        ext = jnp.concatenate([halo, u], 0)
        tpos = i * ts + _iota((ts, 1), 0)
        for g, w in enumerate(POOL_WINDOWS):
            gs = slice(g * POOL_GROUP, (g + 1) * POOL_GROUP)
            s = ext[:, gs]
            sh = 1
            while sh < w:
                s = s + pltpu.roll(s, sh, 0)
                sh *= 2
            cnt = jnp.minimum(tpos + 1, w).astype(F32)
            pb = (s[HALO:] / cnt - u[:, gs]).astype(BF)
            p_ref[:, gs] = pb
            yp_ref[:, gs] = (_nn(pb, wp_ref[g].astype(BF)) * ps_ref[:, gs]).astype(BF)

    hb = ts // HALO
    return pl.pallas_call(
        body, name="pool_fwd", grid=(nb, nt),
        in_specs=[pl.BlockSpec((ts, POOL_WIDTH), lambda b, i: (b * nt + i, 0)),
                  pl.BlockSpec((HALO, POOL_WIDTH), lambda b, i: (jnp.maximum((b * nt + i) * hb - 1, 0), 0)),
                  pl.BlockSpec((4, POOL_GROUP, POOL_GROUP), lambda b, i: (0, 0, 0)),
                  pl.BlockSpec((1, POOL_WIDTH), lambda b, i: (0, 0))],
        out_specs=[pl.BlockSpec((ts, POOL_WIDTH), lambda b, i: (b * nt + i, 0))] * 2,
        out_shape=[_out((nb * seq, POOL_WIDTH), BF)] * 2,
        compiler_params=_cp(("parallel", "parallel"), 32))(*_pin(proj, proj, w_pool, pool_scale))


def _conv_pre(uxbc, halo, cw, cb, first):
    halo = jnp.where(first, 0.0, halo)
    ext = jnp.concatenate([halo, uxbc], 0)
    pre = cb + uxbc * cw[3:4]
    for k in (2, 1, 0):
        pre = pre + pltpu.roll(ext, 3 - k, 0)[CONV_HALO:] * cw[k:k + 1]
    return pre


def _chunk_terms(pre, udt, dtb, alog):
    sg = _sigmoid(pre)
    xbc = pre * sg
    dtp = udt[:, :SSD_HEADS] + dtb
    dt = jnp.maximum(dtp, 0.0) + jnp.log(1.0 + jnp.exp(-jnp.abs(dtp)))
    a = -jnp.exp(alog)
    da = dt * a
    tril = (_iota((CHUNK, CHUNK), 0) >= _iota((CHUNK, CHUNK), 1))
    acum = _exact_nn_left(tril.astype(BF), da)
    eye = (_iota((SSD_HEADS, SSD_HEADS), 0) == _iota((SSD_HEADS, SSD_HEADS), 1)).astype(BF)
    acum_t = _exact_nt_left(eye, acum)
    expand = _head_expand_matrix(SSD_HEADS, SSD_INNER)
    acum_e = _exact_nn(acum, expand)
    dt_e = _exact_nn(dt, expand)
    last_e = acum_e[CHUNK - 1:CHUNK]
    return dict(pre=pre, sg=sg, xbc=xbc, dtp=dtp, dt=dt, a=a, acum=acum, acum_t=acum_t, tril=tril,
                dt_e=dt_e, e_a=jnp.exp(acum_e), d_out=jnp.exp(last_e - acum_e), c_dec=jnp.exp(last_e))


def _head_decay(r, h):
    seg = r["acum"][:, h:h + 1] - r["acum_t"][h:h + 1, :]
    return jnp.where(r["tril"], jnp.exp(jnp.minimum(seg, 0.0)), 0.0)


def _ssd_specs(nb, seq, reverse):
    nc = seq // CHUNK
    per = seq // CONV_HALO

    def cidx(c):
        return (nc - 1 - c) if reverse else c

    def row(b, c):
        return b * nc + cidx(c)

    specs = [
        pl.BlockSpec((CHUNK, CONV_CH), lambda b, c: (row(b, c), 1)),
        pl.BlockSpec((CONV_HALO, CONV_CH),
                     lambda b, c: (jnp.maximum(b * per + cidx(c) * (CHUNK // CONV_HALO) - 1, 0), 1)),
        pl.BlockSpec((CHUNK, GROUP_W), lambda b, c: (row(b, c), 1)),
        pl.BlockSpec((CHUNK, GROUP_W), lambda b, c: (row(b, c), 2)),
        pl.BlockSpec((CHUNK, 128), lambda b, c: (row(b, c), OFF_DT // 128)),
    ]
    return specs, row, cidx, nc


def _const_spec(shape):
    return pl.BlockSpec(shape, lambda b, c: (0,) * len(shape))


def _ssd_fwd(proj, conv_w, conv_b, dt_bias, a_log, dskip_e, g_ssd, nb, seq):
    specs, row, cidx, nc = _ssd_specs(nb, seq, reverse=False)

    def body(uxbc_ref, halo_ref, z0_ref, z1_ref, udt_ref, cw_ref, cb_ref, dtb_ref, alog_ref, dsk_ref, gs_ref,
             yssd_ref, yssm_ref, hprev_ref, pre_ref, h_ref, yd_ref):
        c = pl.program_id(1)

        @pl.when(c == 0)
        def _():
            h_ref[...] = jnp.zeros_like(h_ref)

        pre = _conv_pre(uxbc_ref[...], halo_ref[...], cw_ref[...], cb_ref[...], c == 0)
        pre_ref[...] = pre
        r = _chunk_terms(pre, udt_ref[...], dtb_ref[...], alog_ref[...])
        xbc = r["xbc"]
        xs = xbc[:, :SSD_INNER]
        xdt = xs * r["dt_e"]
        xdt_b = xdt.astype(BF)
        xdo_b = (xdt * r["d_out"]).astype(BF)
        hprev_ref[0, 0] = h_ref[...]
        for g in range(2):
            gs = slice(g * GROUP_W, (g + 1) * GROUP_W)
            bg = xbc[:, SSD_INNER + g * SSD_STATE:SSD_INNER + (g + 1) * SSD_STATE].astype(BF)
            cg = xbc[:, SSD_INNER + (2 + g) * SSD_STATE:SSD_INNER + (3 + g) * SSD_STATE].astype(BF)
            scores = _nt(cg, bg)
            hg = h_ref[g]
            y_off = _nn(cg, hg.astype(BF)) * r["e_a"][:, gs]
            for hh in range(8):
                h = g * 8 + hh
                m = (scores * _head_decay(r, h)).astype(BF)
                yd_ref[:, h * SSD_HEAD_DIM:(h + 1) * SSD_HEAD_DIM] = _nn(m, xdt_b[:, h * SSD_HEAD_DIM:(h + 1) * SSD_HEAD_DIM])
            h_ref[g] = hg * r["c_dec"][:, gs] + _tn(bg, xdo_b[:, gs])
            y = yd_ref[:, gs] + y_off + dsk_ref[:, gs] * xs[:, gs]
            yssm_ref[:, gs] = y
            zg = (z0_ref if g == 0 else z1_ref)[...]
            yg = y * (zg * _sigmoid(zg))
            rg = lax.rsqrt(jnp.mean(yg * yg, -1, keepdims=True) + EPS)
            yssd_ref[:, gs] = (yg * rg * gs_ref[:, gs]).astype(BF)

    t = nb * seq
    return pl.pallas_call(
        body, name="ssd_fwd", grid=(nb, nc),
        in_specs=specs + [_const_spec((4, CONV_CH)), _const_spec((1, CONV_CH)), _const_spec((1, SSD_HEADS)),
                          _const_spec((1, SSD_HEADS)), _const_spec((1, SSD_INNER)), _const_spec((1, SSD_INNER))],
        out_specs=[pl.BlockSpec((CHUNK, SSD_INNER), lambda b, c: (row(b, c), 0)),
                   pl.BlockSpec((CHUNK, SSD_INNER), lambda b, c: (row(b, c), 0)),
                   pl.BlockSpec((1, 1, 2, SSD_STATE, GROUP_W), lambda b, c: (b, c, 0, 0, 0)),
                   pl.BlockSpec((CHUNK, CONV_CH), lambda b, c: (row(b, c), 0))],
        out_shape=[_out((t, SSD_INNER), BF), _out((t, SSD_INNER), F32),
                   _out((nb, nc, 2, SSD_STATE, GROUP_W), F32), _out((t, CONV_CH), F32)],
        scratch_shapes=[pltpu.VMEM((2, SSD_STATE, GROUP_W), F32), pltpu.VMEM((CHUNK, SSD_INNER), F32)],
        compiler_params=_cp(("arbitrary", "arbitrary"), 48),
    )(*_pin(proj, proj, proj, proj, proj, conv_w, conv_b, dt_bias, a_log, dskip_e, g_ssd))


def _out_proj(y_pool, y_ssd, w_out, x, mod3, g_mlp, seq):
    t, d = x.shape
    tm = 512
    tps = seq // tm if seq >= tm else 1
    tm = min(tm, seq)

    def body(yp_ref, ys_ref, w_ref, x_ref, mod_ref, g_ref, h1_ref, o_ref, u2_ref):
        o = _nn(yp_ref[...], w_ref[0:POOL_WIDTH, :]) + _nn(ys_ref[...], w_ref[POOL_WIDTH:, :])
        o_ref[...] = o.astype(BF)
        h1 = x_ref[...] + mod_ref[0, 2:3, :] * o
        h1_ref[...] = h1
        r = lax.rsqrt(jnp.mean(h1 * h1, -1, keepdims=True) + EPS)
        u2_ref[...] = ((h1 * r * g_ref[...]) * (1.0 + mod_ref[0, 4:5, :]) + mod_ref[0, 3:4, :]).astype(BF)

    row = lambda i: (i, 0)
    return pl.pallas_call(
        body, name="out_proj", grid=(t // tm,),
        in_specs=[pl.BlockSpec((tm, POOL_WIDTH), row), pl.BlockSpec((tm, SSD_INNER), row),
                  pl.BlockSpec(w_out.shape, lambda i: (0, 0)), pl.BlockSpec((tm, d), row),
                  pl.BlockSpec((1, N_MOD, d), lambda i: (i // tps, 0, 0)), pl.BlockSpec((1, d), lambda i: (0, 0))],
        out_specs=[pl.BlockSpec((tm, d), row)] * 3,
        out_shape=[_out((t, d), F32), _out((t, d), BF), _out((t, d), BF)],
        compiler_params=_cp(("parallel",), 48))(*_pin(y_pool, y_ssd, w_out, x, mod3, g_mlp))


def _mlp_up(u2, w_up4):
    t, d = u2.shape
    tm = min(1024, t)
    nk, _, cols = w_up4.shape

    def body(u_ref, w_ref, a_ref):
        a_ref[...] = _nn(u_ref[...], w_ref[0]).astype(BF)

    return pl.pallas_call(
        body, name="mlp_up", grid=(nk, t // tm),
        in_specs=[pl.BlockSpec((tm, d), lambda k, i: (i, 0)), pl.BlockSpec((1, d, cols), lambda k, i: (k, 0, 0))],
        out_specs=pl.BlockSpec((tm, cols), lambda k, i: (i, k)),
        out_shape=_out((t, nk * cols), BF),
        compiler_params=_cp(("parallel", "parallel"), 32))(*_pin(u2, w_up4))


def _mlp_down_loss(a_up, w_down, h1, mod3, g_final, target, seq):
    t, d = h1.shape
    nb = t // seq
    tm = min(256, seq)
    tps = seq // tm

    def body(a_ref, w_ref, h1_ref, mod_ref, g_ref, tg_ref, ddn_ref, dh2_ref, sq_ref, gg_ref, dgf_ref):
        i = pl.program_id(0)

        @pl.when(i == 0)
        def _():
            sq_ref[...] = jnp.zeros_like(sq_ref)
            gg_ref[...] = jnp.zeros_like(gg_ref)

        @pl.when(i % tps == 0)
        def _():
            dgf_ref[...] = jnp.zeros_like(dgf_ref)

        gate = mod_ref[0, 5:6, :]
        sq = gg = dgf = 0.0
        for rows in _sub_rows(tm):
            f = jnp.square(jnp.maximum(a_ref[rows, :], 0))
            dn = _nn(f, w_ref[...])
            h2 = h1_ref[rows, :] + gate * dn
            r = lax.rsqrt(jnp.mean(h2 * h2, -1, keepdims=True) + EPS)
            hh = h2 * r
            err = hh * g_ref[...] - tg_ref[rows, :]
            dy = err * (1.0 / d)
            dhat = dy * g_ref[...]
            dh2 = r * (dhat - hh * jnp.mean(dhat * hh, -1, keepdims=True))
            dh2_ref[rows, :] = dh2
            ddn_ref[rows, :] = (dh2 * gate).astype(BF)
            sq = sq + jnp.sum(err * err, 0, keepdims=True)
            gg = gg + jnp.sum(dy * hh, 0, keepdims=True)
            dgf = dgf + jnp.sum(dh2 * dn, 0, keepdims=True)
        sq_ref[...] += sq
        gg_ref[...] += gg
        dgf_ref[0] += dgf

    row = lambda i: (i, 0)
    vec = pl.BlockSpec((1, d), lambda i: (0, 0))
    return pl.pallas_call(
        body, name="mlp_down_loss", grid=(t // tm,),
        in_specs=[pl.BlockSpec((tm, D_FF), row), pl.BlockSpec(w_down.shape, lambda i: (0, 0)), pl.BlockSpec((tm, d), row),
                  pl.BlockSpec((1, N_MOD, d), lambda i: (i // tps, 0, 0)), vec, pl.BlockSpec((tm, d), row)],
        out_specs=[pl.BlockSpec((tm, d), row), pl.BlockSpec((tm, d), row), vec, vec,
                   pl.BlockSpec((1, 1, d), lambda i: (i // tps, 0, 0))],
        out_shape=[_out((t, d), BF), _out((t, d), F32), _out((1, d), F32),
                   _out((1, d), F32), _out((nb, 1, d), F32)],
        compiler_params=_cp(("arbitrary",), 56))(*_pin(a_up, w_down, h1, mod3, g_final, target))


def _tn_matmul(a, b, tk, tn, name, square_relu=False, out3=False):
    t, kdim = a.shape
    ndim = b.shape[1]

    def body(a_ref, b_ref, o_ref):
        av = a_ref[...]
        if square_relu:
            av = jnp.square(jnp.maximum(av, 0))
        res = _tn(av, b_ref[...]).astype(BF)
        if out3:
            o_ref[0] = res
        else:
            o_ref[...] = res

    if out3:
        out_spec = pl.BlockSpec((1, tk, tn), lambda j, i: (j, i, 0))
        out_shape = _out((ndim // tn, kdim, tn), BF)
    else:
        out_spec = pl.BlockSpec((tk, tn), lambda j, i: (i, j))
        out_shape = _out((kdim, ndim), BF)
    return pl.pallas_call(
        body, name=name, grid=(ndim // tn, kdim // tk),
        in_specs=[pl.BlockSpec((t, tk), lambda j, i: (0, i)), pl.BlockSpec((t, tn), lambda j, i: (0, j))],
        out_specs=out_spec, out_shape=out_shape,
        compiler_params=_cp(("parallel", "parallel"), 56))(*_pin(a, b))


def _mlp_down_bwd(d_dn, w_down4, a_up, token):
    t, d = d_dn.shape
    tm = min(1024, t)
    nk, rows, _ = w_down4.shape

    def body(g_ref, w_ref, a_ref, tok_ref, o_ref):
        df = _nt(g_ref[...], w_ref[0])
        o_ref[...] = (df * (2.0 * jnp.maximum(a_ref[...], 0).astype(F32))).astype(BF)

    return pl.pallas_call(
        body, name="mlp_down_bwd", grid=(nk, t // tm),
        in_specs=[pl.BlockSpec((tm, d), lambda k, i: (i, 0)), pl.BlockSpec((1, rows, d), lambda k, i: (k, 0, 0)),
                  pl.BlockSpec((tm, rows), lambda k, i: (i, k)), _token_spec()],
        out_specs=pl.BlockSpec((tm, rows), lambda k, i: (i, k)),
        out_shape=_out((t, nk * rows), BF),
        compiler_params=_cp(("parallel", "parallel"), 32))(*_pin(d_dn, w_down4, a_up, token))


def _mlp_up_bwd(d_a, w_up4, h1, dh2, o, mod3, g_mlp, seq, token):
    t, d = h1.shape
    nb = t // seq
    tm = min(256, seq)
    tps = seq // tm
    nk = w_up4.shape[0]
    cols = w_up4.shape[2]

    def body(da_ref, w_ref, h1_ref, dh2_ref, o_ref, mod_ref, g_ref, tok_ref, dh1_ref, do_ref, acc_ref, gg_ref):
        i = pl.program_id(0)

        @pl.when(i == 0)
        def _():
            gg_ref[...] = jnp.zeros_like(gg_ref)

        @pl.when(i % tps == 0)
        def _():
            acc_ref[...] = jnp.zeros_like(acc_ref)

        gg = a_shift = a_scale = a_gate = 0.0
        for rows in _sub_rows(tm):
            du = _nt(da_ref[rows, 0:cols], w_ref[0])
            for k in range(1, nk):
                du = du + _nt(da_ref[rows, k * cols:(k + 1) * cols], w_ref[k])
            h1 = h1_ref[rows, :]
            r = lax.rsqrt(jnp.mean(h1 * h1, -1, keepdims=True) + EPS)
            hh = h1 * r
            n2 = hh * g_ref[...]
            dn2 = du * (1.0 + mod_ref[0, 4:5, :])
            dhat = dn2 * g_ref[...]
            dh1 = dh2_ref[rows, :] + r * (dhat - hh * jnp.mean(dhat * hh, -1, keepdims=True))
            dh1_ref[rows, :] = dh1
            do_ref[rows, :] = (dh1 * mod_ref[0, 2:3, :]).astype(BF)
            gg = gg + jnp.sum(dn2 * hh, 0, keepdims=True)
            a_shift = a_shift + jnp.sum(du, 0, keepdims=True)
            a_scale = a_scale + jnp.sum(du * n2, 0, keepdims=True)
            a_gate = a_gate + jnp.sum(dh1 * o_ref[rows, :].astype(F32), 0, keepdims=True)
        gg_ref[...] += gg
        acc_ref[0, 0:1, :] += a_shift
        acc_ref[0, 1:2, :] += a_scale
        acc_ref[0, 2:3, :] += a_gate

    row = lambda i: (i, 0)
    vec = pl.BlockSpec((1, d), lambda i: (0, 0))
    return pl.pallas_call(
        body, name="mlp_up_bwd", grid=(t // tm,),
        in_specs=[pl.BlockSpec((tm, D_FF), row), pl.BlockSpec(w_up4.shape, lambda i: (0, 0, 0)), pl.BlockSpec((tm, d), row),
                  pl.BlockSpec((tm, d), row), pl.BlockSpec((tm, d), row),
                  pl.BlockSpec((1, N_MOD, d), lambda i: (i // tps, 0, 0)), vec, _token_spec()],
        out_specs=[pl.BlockSpec((tm, d), row), pl.BlockSpec((tm, d), row),
                   pl.BlockSpec((1, 8, d), lambda i: (i // tps, 0, 0)), vec],
        out_shape=[_out((t, d), F32), _out((t, d), BF),
                   _out((nb, 8, d), F32), _out((1, d), F32)],
        compiler_params=_cp(("arbitrary",), 56))(*_pin(d_a, w_up4, h1, dh2, o, mod3, g_mlp, token))


def _out_proj_bwd(d_o, w_out, token):
    t, d = d_o.shape
    tm = min(512, t)

    def body(g_ref, w_ref, tok_ref, dp_ref, ds_ref):
        gv = g_ref[...]
        dp_ref[...] = _nt(gv, w_ref[0:POOL_WIDTH, :])
        ds_ref[...] = _nt(gv, w_ref[POOL_WIDTH:, :])

    row = lambda i: (i, 0)
    return pl.pallas_call(
        body, name="out_proj_bwd", grid=(t // tm,),
        in_specs=[pl.BlockSpec((tm, d), row), pl.BlockSpec(w_out.shape, lambda i: (0, 0)), _token_spec()],
        out_specs=[pl.BlockSpec((tm, POOL_WIDTH), row), pl.BlockSpec((tm, SSD_INNER), row)],
        out_shape=[_out((t, POOL_WIDTH), F32), _out((t, SSD_INNER), F32)],
        compiler_params=_cp(("parallel",), 32))(*_pin(d_o, w_out, token))


def _pool_bwd(d_ypool, p, w_pool, pool_scale, nb, seq):
    ts = _pool_tile(seq)
    nt = seq // ts
    hb = ts // HALO
    last_block = nb * seq // HALO - 1

    def body(dy_ref, halo_ref, p_ref, wp_ref, ps_ref, du_ref, gw_ref, gs_ref):
        b = pl.program_id(0)
        i = pl.program_id(1)

        @pl.when((b == 0) & (i == 0))
        def _():
            gw_ref[...] = jnp.zeros_like(gw_ref)
            gs_ref[...] = jnp.zeros_like(gs_ref)

        halo = jnp.where(i == nt - 1, 0.0, halo_ref[...])
        dy = dy_ref[...]
        ext = jnp.concatenate([dy, halo], 0)
        tpos = i * ts + _iota((ts + HALO, 1), 0)
        n_ext = ts + HALO
        for g, w in enumerate(POOL_WINDOWS):
            gs = slice(g * POOL_GROUP, (g + 1) * POOL_GROUP)
            wg = wp_ref[g].astype(BF)
            pg = p_ref[:, gs]
            pw = _nn(pg, wg)
            gs_ref[:, gs] += jnp.sum(dy[:, gs] * pw, 0, keepdims=True)
            dpw = (ext[:, gs] * ps_ref[:, gs]).astype(BF)
            gw_ref[g] += _tn(pg, dpw[:ts])
            dp = _nt(dpw, wg)
            cnt = jnp.minimum(tpos + 1, w).astype(F32)
            s = dp / cnt
            sh = 1
            while sh < w:
                s = s + pltpu.roll(s, n_ext - sh, 0)
                sh *= 2
            du_ref[:, gs] = (s[:ts] - dp[:ts]).astype(BF)

    return pl.pallas_call(
        body, name="pool_bwd", grid=(nb, nt),
        in_specs=[pl.BlockSpec((ts, POOL_WIDTH), lambda b, i: (b * nt + i, 0)),
                  pl.BlockSpec((HALO, POOL_WIDTH), lambda b, i: (jnp.minimum((b * nt + i + 1) * hb, last_block), 0)),
                  pl.BlockSpec((ts, POOL_WIDTH), lambda b, i: (b * nt + i, 0)),
                  pl.BlockSpec((4, POOL_GROUP, POOL_GROUP), lambda b, i: (0, 0, 0)),
                  pl.BlockSpec((1, POOL_WIDTH), lambda b, i: (0, 0))],
        out_specs=[pl.BlockSpec((ts, POOL_WIDTH), lambda b, i: (b * nt + i, 0)),
                   pl.BlockSpec((4, POOL_GROUP, POOL_GROUP), lambda b, i: (0, 0, 0)),
                   pl.BlockSpec((1, POOL_WIDTH), lambda b, i: (0, 0))],
        out_shape=[_out((nb * seq, POOL_WIDTH), BF), _out((4, POOL_GROUP, POOL_GROUP), F32),
                   _out((1, POOL_WIDTH), F32)],
        compiler_params=_cp(("arbitrary", "arbitrary"), 32))(*_pin(d_ypool, d_ypool, p, w_pool, pool_scale))


def _ssd_bwd(proj, pre, d_yssd, yssm, h_prev, dt_bias, a_log, dskip_e, g_ssd, nb, seq):
    specs, row, cidx, nc = _ssd_specs(nb, seq, reverse=True)
    specs = specs[2:]

    def body(z0_ref, z1_ref, udt_ref, pre_ref, dys_ref, yssm_ref, hprev_ref,
             dtb_ref, alog_ref, dsk_ref, gs_ref,
             dz_ref, dpre_ref, dudt_ref, ggs_ref, gdsk_ref, ga_ref, gdtb_ref,
             g_ref, dxdt_ref, dyv_ref):
        b = pl.program_id(0)
        c = pl.program_id(1)

        @pl.when(c == 0)
        def _():
            g_ref[...] = jnp.zeros_like(g_ref)

        @pl.when((b == 0) & (c == 0))
        def _():
            ggs_ref[...] = jnp.zeros_like(ggs_ref)
            gdsk_ref[...] = jnp.zeros_like(gdsk_ref)
            ga_ref[...] = jnp.zeros_like(ga_ref)
            gdtb_ref[...] = jnp.zeros_like(gdtb_ref)

        r = _chunk_terms(pre_ref[...], udt_ref[...], dtb_ref[...], alog_ref[...])
        xbc = r["xbc"]
        xs = xbc[:, :SSD_INNER]
        dt_e = r["dt_e"]
        xdt = xs * dt_e
        xdt_b = xdt.astype(BF)
        reduce_m = _head_reduce_matrix(GROUP_W, 8)

        def head_sums(v):
            return _nn(v.astype(BF), reduce_m)

        onehot16 = lambda h: (_iota((1, SSD_HEADS), 1) == h).astype(F32)
        onecol16 = lambda h: (_iota((SSD_HEADS, 1), 0) == h).astype(F32)

        d_acum = jnp.zeros((CHUNK, SSD_HEADS), F32)
        d_acum_t = jnp.zeros((SSD_HEADS, CHUNK), F32)
        d_alast = jnp.zeros((1, SSD_HEADS), F32)
        place8 = lambda g: (_iota((8, SSD_HEADS), 1) == _iota((8, SSD_HEADS), 0) + 8 * g).astype(BF)
        d_b, d_c = [], []
        for g in range(2):
            gs = slice(g * GROUP_W, (g + 1) * GROUP_W)
            zg = (z0_ref if g == 0 else z1_ref)[...]
            sz = _sigmoid(zg)
            silu_z = zg * sz
            ys = yssm_ref[:, gs]
            yg = ys * silu_z
            rg = lax.rsqrt(jnp.mean(yg * yg, -1, keepdims=True) + EPS)
            yh = yg * rg
            dys = dys_ref[:, gs]
            ggs_ref[:, gs] += jnp.sum(dys * yh, 0, keepdims=True)
            dyh = dys * gs_ref[:, gs]
            dyg = rg * (dyh - yh * jnp.mean(dyh * yh, -1, keepdims=True))
            dy = dyg * silu_z
            dz_ref[:, gs] = (dyg * ys * (sz * (1.0 + zg * (1.0 - sz)))).astype(BF)
            gdsk_ref[:, gs] += jnp.sum(dy * xs[:, gs], 0, keepdims=True)
            dyv_ref[:, gs] = dy
            dy_b = dy.astype(BF)

            bg = xbc[:, SSD_INNER + g * SSD_STATE:SSD_INNER + (g + 1) * SSD_STATE].astype(BF)
            cg = xbc[:, SSD_INNER + (2 + g) * SSD_STATE:SSD_INNER + (3 + g) * SSD_STATE].astype(BF)
            scores = _nt(cg, bg)
            hg = hprev_ref[0, 0, g]
            hg_b = hg.astype(BF)
            gg = g_ref[g]
            gg_b = gg.astype(BF)
            e_a = r["e_a"][:, gs]
            d_out = r["d_out"][:, gs]
            c_dec = r["c_dec"][:, gs]
            zc = _nn(cg, hg_b)
            wv = e_a * dy
            wv_b = wv.astype(BF)
            da_g = head_sums(wv * zc)
            dcg = _nt(wv_b, hg_b)
            d_hprev = _tn(cg, wv_b)
            vg = _nn(bg, gg_b)
            dxdt_g = d_out * vg
            dd_out = head_sums(xdt[:, gs] * vg)
            dbg = _nt((xdt[:, gs] * d_out).astype(BF), gg_b)
            dcd = _exact_nn(jnp.sum(gg * hg, 0, keepdims=True), reduce_m)
            d_out8 = jnp.exp(r["acum"][CHUNK - 1:CHUNK, 8 * g:8 * g + 8] - r["acum"][:, 8 * g:8 * g + 8])
            c_dec8 = jnp.exp(r["acum"][CHUNK - 1:CHUNK, 8 * g:8 * g + 8])
            t8 = dd_out * d_out8
            d_alast = d_alast + _exact_nn(jnp.sum(t8, 0, keepdims=True) + dcd * c_dec8, place8(g))
            d_acum = d_acum + _exact_nn(da_g - t8, place8(g))
            dsc = jnp.zeros((CHUNK, CHUNK), F32)
            for hh in range(8):
                h = g * 8 + hh
                hs = slice(h * SSD_HEAD_DIM, (h + 1) * SSD_HEAD_DIM)
                lam = _head_decay(r, h)
                m = scores * lam
                dyh_b = dy_b[:, hh * SSD_HEAD_DIM:(hh + 1) * SSD_HEAD_DIM]
                dm = _nt(dyh_b, xdt_b[:, hs])
                tm_ = dm * m
                d_acum = d_acum + jnp.sum(tm_, 1, keepdims=True) * onehot16(h)
                d_acum_t = d_acum_t + onecol16(h) * jnp.sum(tm_, 0, keepdims=True)
                dsc = dsc + dm * lam
                dxdt_ref[:, hs] = _tn(m.astype(BF), dyh_b) + dxdt_g[:, hh * SSD_HEAD_DIM:(hh + 1) * SSD_HEAD_DIM]
            dsc_b = dsc.astype(BF)
            d_c.append(dcg + _nn(dsc_b, bg))
            d_b.append(dbg + _tn(dsc_b, cg))
            g_ref[g] = d_hprev + c_dec * gg

        eye = (_iota((CHUNK, CHUNK), 0) == _iota((CHUNK, CHUNK), 1)).astype(BF)
        d_acum = d_acum - _exact_nt_left(eye, d_acum_t)
        is_last = (_iota((CHUNK, 1), 0) == CHUNK - 1).astype(F32)
        d_acum = d_acum + is_last * d_alast
        triu = (_iota((CHUNK, CHUNK), 0) <= _iota((CHUNK, CHUNK), 1)).astype(BF)
        d_da = _exact_nn_left(triu, d_acum)
        dt = r["dt"]
        ga_ref[...] += jnp.sum(d_da * dt, 0, keepdims=True)
        dxdt = dxdt_ref[...]
        reduce16 = _head_reduce_matrix(SSD_INNER, SSD_HEADS)
        d_dt = d_da * r["a"] + _nn((dxdt * xs).astype(BF), reduce16)
        d_udt = d_dt * _sigmoid(r["dtp"])
        gdtb_ref[...] += jnp.sum(d_udt, 0, keepdims=True)
        dudt_ref[...] = jnp.zeros_like(dudt_ref)
        dudt_ref[:, 0:SSD_HEADS] = d_udt.astype(BF)
        pre, sg = r["pre"], r["sg"]
        dsilu = sg * (1.0 + pre * (1.0 - sg))
        dpre_ref[:, 0:SSD_INNER] = (dsk_ref[...] * dyv_ref[...] + dxdt * dt_e) * dsilu[:, 0:SSD_INNER]
        for g in range(2):
            bs = slice(SSD_INNER + g * SSD_STATE, SSD_INNER + (g + 1) * SSD_STATE)
            cs = slice(SSD_INNER + (2 + g) * SSD_STATE, SSD_INNER + (3 + g) * SSD_STATE)
            dpre_ref[:, bs] = d_b[g] * dsilu[:, bs]
            dpre_ref[:, cs] = d_c[g] * dsilu[:, cs]

    t = nb * seq
    vec = _const_spec((1, SSD_INNER))
    small = _const_spec((1, SSD_HEADS))
    return pl.pallas_call(
        body, name="ssd_bwd", grid=(nb, nc),
        in_specs=specs + [pl.BlockSpec((CHUNK, CONV_CH), lambda b, c: (row(b, c), 0)),
                          pl.BlockSpec((CHUNK, SSD_INNER), lambda b, c: (row(b, c), 0)),
                          pl.BlockSpec((CHUNK, SSD_INNER), lambda b, c: (row(b, c), 0)),
                          pl.BlockSpec((1, 1, 2, SSD_STATE, GROUP_W), lambda b, c: (b, cidx(c), 0, 0, 0)),
                          small, small, vec, vec],
        out_specs=[pl.BlockSpec((CHUNK, SSD_INNER), lambda b, c: (row(b, c), 0)),
                   pl.BlockSpec((CHUNK, CONV_CH), lambda b, c: (row(b, c), 0)),
                   pl.BlockSpec((CHUNK, 128), lambda b, c: (row(b, c), 0)),
                   vec, vec, small, small],
        out_shape=[_out((t, SSD_INNER), BF), _out((t, CONV_CH), F32),
                   _out((t, 128), BF), _out((1, SSD_INNER), F32),
                   _out((1, SSD_INNER), F32), _out((1, SSD_HEADS), F32),
                   _out((1, SSD_HEADS), F32)],
        scratch_shapes=[pltpu.VMEM((2, SSD_STATE, GROUP_W), F32), pltpu.VMEM((CHUNK, SSD_INNER), F32),
                        pltpu.VMEM((CHUNK, SSD_INNER), F32)],
        compiler_params=_cp(("arbitrary", "arbitrary"), 48),
    )(*_pin(proj, proj, proj, pre, d_yssd, yssm, h_prev, dt_bias, a_log, dskip_e, g_ssd))


def _grad_w_in_t(d_upool, d_z, d_uxbc, d_udt, u1):
    t, d = u1.shape
    tk = 512
    n_z, n_x = SSD_INNER // tk, CONV_CH // tk

    def body(p_ref, z_ref, x_ref, dt_ref, u_ref, o_ref):
        i = pl.program_id(0)

        @pl.when(i == 0)
        def _():
            o_ref[...] = _tn(p_ref[...], u_ref[...]).astype(BF)

        @pl.when((i >= 1) & (i < 1 + n_z))
        def _():
            o_ref[...] = _tn(z_ref[...], u_ref[...]).astype(BF)

        @pl.when((i >= 1 + n_z) & (i < 1 + n_z + n_x))
        def _():
            o_ref[...] = _tn(x_ref[...], u_ref[...]).astype(BF)

        @pl.when(i == 1 + n_z + n_x)
        def _():
            o_ref[0:128, :] = _tn(dt_ref[...], u_ref[...]).astype(BF)

    return pl.pallas_call(
        body, name="grad_w_in", grid=(2 + n_z + n_x,),
        in_specs=[pl.BlockSpec((t, tk), lambda i: (0, 0)),
                  pl.BlockSpec((t, tk), lambda i: (0, jnp.clip(i - 1, 0, n_z - 1))),
                  pl.BlockSpec((t, tk), lambda i: (0, jnp.clip(i - 1 - n_z, 0, n_x - 1))),
                  pl.BlockSpec((t, 128), lambda i: (0, 0)), pl.BlockSpec((t, d), lambda i: (0, 0))],
        out_specs=pl.BlockSpec((tk, d), lambda i: (i, 0)),
        out_shape=_out((IN_PAD, d), BF),
        compiler_params=_cp(("parallel",), 56))(*_pin(d_upool, d_z, d_uxbc, d_udt, u1))


def _conv_bwd(d_pre, proj, conv_w, nb, seq):
    ts = min(256, seq)
    nt = seq // ts
    hb = ts // CONV_HALO
    last_block = nb * seq // CONV_HALO - 1
    n_ext = ts + CONV_HALO

    def body(dp_ref, dnext_ref, u_ref, cw_ref, du_ref, gw_ref, gb_ref):
        b = pl.program_id(0)
        i = pl.program_id(1)

        @pl.when((b == 0) & (i == 0))
        def _():
            gw_ref[...] = jnp.zeros_like(gw_ref)
            gb_ref[...] = jnp.zeros_like(gb_ref)

        dp = dp_ref[...]
        u = u_ref[...]
        ext_d = jnp.concatenate([dp, jnp.where(i == nt - 1, 0.0, dnext_ref[...])], 0)
        cw = cw_ref[...]
        du = dp * cw[3:4]
        gw_ref[3:4, :] += jnp.sum(dp * u, 0, keepdims=True)
        for k in (2, 1, 0):
            shifted = pltpu.roll(ext_d, n_ext - (3 - k), 0)[:ts]
            du = du + shifted * cw[k:k + 1]
            gw_ref[k:k + 1, :] += jnp.sum(shifted * u, 0, keepdims=True)
        gb_ref[...] += jnp.sum(dp, 0, keepdims=True)
        du_ref[...] = du.astype(BF)

    return pl.pallas_call(
        body, name="conv_bwd", grid=(nb, nt),
        in_specs=[pl.BlockSpec((ts, CONV_CH), lambda b, i: (b * nt + i, 0)),
                  pl.BlockSpec((CONV_HALO, CONV_CH), lambda b, i: (jnp.minimum((b * nt + i + 1) * hb, last_block), 0)),
                  pl.BlockSpec((ts, CONV_CH), lambda b, i: (b * nt + i, 1)),
                  pl.BlockSpec((4, CONV_CH), lambda b, i: (0, 0))],
        out_specs=[pl.BlockSpec((ts, CONV_CH), lambda b, i: (b * nt + i, 0)),
                   pl.BlockSpec((8, CONV_CH), lambda b, i: (0, 0)), pl.BlockSpec((1, CONV_CH), lambda b, i: (0, 0))],
        out_shape=[_out((nb * seq, CONV_CH), BF), _out((8, CONV_CH), F32),
                   _out((1, CONV_CH), F32)],
        compiler_params=_cp(("arbitrary", "arbitrary"), 48))(*_pin(d_pre, d_pre, proj, conv_w))


def _in_proj_bwd(d_parts, w_in_t, x, dh1, mod3, g_mix, seq, token):
    t, d = x.shape
    nb = t // seq
    tm = min(256, seq)
    tps = seq // tm

    widths = [p.shape[1] for p in d_parts]

    def body(d0_ref, d1_ref, d2_ref, d3_ref, w_ref, x_ref, dh1_ref, mod_ref, g_ref, tok_ref, gx_ref, acc_ref, gg_ref):
        i = pl.program_id(0)

        @pl.when(i == 0)
        def _():
            gg_ref[...] = jnp.zeros_like(gg_ref)

        @pl.when(i % tps == 0)
        def _():
            acc_ref[...] = jnp.zeros_like(acc_ref)

        gg = a_shift = a_scale = 0.0
        for rows in _sub_rows(tm):
            du = None
            off = 0
            for p_ref, wd in zip((d0_ref, d1_ref, d2_ref, d3_ref), widths):
                part = _nn(p_ref[rows, :], w_ref[off:off + wd, :])
                du = part if du is None else du + part
                off += wd
            xv = x_ref[rows, :]
            r = lax.rsqrt(jnp.mean(xv * xv, -1, keepdims=True) + EPS)
            hh = xv * r
            n1 = hh * g_ref[...]
            dn1 = du * (1.0 + mod_ref[0, 1:2, :])
            dhat = dn1 * g_ref[...]
            gx_ref[rows, :] = dh1_ref[rows, :] + r * (dhat - hh * jnp.mean(dhat * hh, -1, keepdims=True))
            gg = gg + jnp.sum(dn1 * hh, 0, keepdims=True)
            a_shift = a_shift + jnp.sum(du, 0, keepdims=True)
            a_scale = a_scale + jnp.sum(du * n1, 0, keepdims=True)
        gg_ref[...] += gg
        acc_ref[0, 0:1, :] += a_shift
        acc_ref[0, 1:2, :] += a_scale

    row = lambda i: (i, 0)
    vec = pl.BlockSpec((1, d), lambda i: (0, 0))
    return pl.pallas_call(
        body, name="in_proj_bwd", grid=(t // tm,),
        in_specs=[pl.BlockSpec((tm, wd), row) for wd in widths] +
                 [pl.BlockSpec(w_in_t.shape, lambda i: (0, 0)), pl.BlockSpec((tm, d), row),
                  pl.BlockSpec((tm, d), row), pl.BlockSpec((1, N_MOD, d), lambda i: (i // tps, 0, 0)), vec, _token_spec()],
        out_specs=[pl.BlockSpec((tm, d), row), pl.BlockSpec((1, 8, d), lambda i: (i // tps, 0, 0)), vec],
        out_shape=[_out((t, d), F32), _out((nb, 8, d), F32),
                   _out((1, d), F32)],
        compiler_params=_cp(("arbitrary",), 48))(*_pin(*d_parts, w_in_t, x, dh1, mod3, g_mix, token))


_VEC_LAYOUT = (("g_mix", 1024), ("conv_b", 1536), ("g_ssd", 1024), ("pool_scale", 512), ("g_mlp", 1024),
               ("g_final", 1024), ("dt_bias", 128), ("a_log", 128), ("d_skip_lanes", 1024), ("sq_err", 1024))
_VEC_OFFSET = {}
_off = 0
for _name, _n in _VEC_LAYOUT:
    _VEC_OFFSET[_name] = _off
    _off += _n
_VEC_LANES = _off
_SMALL_PARAMS = ("b_ada", "g_mix", "conv_w", "conv_b", "dt_bias", "a_log", "d_skip", "g_ssd", "w_pool", "pool_scale",
                 "g_mlp", "g_final")


def _pack_vec(parts):
    cols = []
    for name, n in _VEC_LAYOUT:
        v = parts[name]
        if v.shape[1] < n:
            v = jnp.pad(v, ((0, 0), (0, n - v.shape[1])))
        cols.append(v)
    return jnp.concatenate(cols, 1)


def _small_adam(vec_all, wpool_all, convw_all, dmod_all, params):
    names = _SMALL_PARAMS
    nin = 4 + 3 * len(names)

    def body(*refs):
        vec_ref, wp_ref, cw_ref, dm_ref = refs[:4]
        prm = {n: refs[4 + 3 * i:7 + 3 * i] for i, n in enumerate(names)}
        loss_ref = refs[nin]
        outs = {n: refs[nin + 1 + 4 * i:nin + 5 + 4 * i] for i, n in enumerate(names)}
        vsum = vec_ref[0]
        for s in range(1, N_DEV):
            vsum = vsum + vec_ref[s]

        def lanes(name, n):
            off = _VEC_OFFSET[name]
            return vsum[:, off:off + n]

        grads = {n: lanes(n, prm[n][0].shape[1]) for n in ("g_mix", "conv_b", "g_ssd", "pool_scale", "g_mlp", "g_final", "dt_bias")}
        grads["a_log"] = lanes("a_log", SSD_HEADS) * (-jnp.exp(prm["a_log"][0][...]))
        per_lane = jnp.broadcast_to(lanes("d_skip_lanes", SSD_INNER), (8, SSD_INNER))
        grads["d_skip"] = _exact_nn(per_lane, _head_reduce_matrix(SSD_INNER, SSD_HEADS))[0:1]
        gwp = wp_ref[0].astype(F32)
        gcw = cw_ref[0]
        gb = jnp.sum(dm_ref[0], 0, keepdims=True)
        for s in range(1, N_DEV):
            gwp = gwp + wp_ref[s].astype(F32)
            gcw = gcw + cw_ref[s]
            gb = gb + jnp.sum(dm_ref[s], 0, keepdims=True)
        grads["w_pool"] = gwp
        grads["conv_w"] = gcw[0:4]
        grads["b_ada"] = gb
        total = jnp.sum(lanes("sq_err", D_MODEL), 1, keepdims=True) * (0.5 / D_MODEL)
        loss_ref[...] = jnp.broadcast_to(total, loss_ref.shape)
        for n in names:
            w_ref, m_ref, v_ref = prm[n]
            g = grads[n]
            d, m2, v2 = _adam_math(w_ref[...], g, m_ref[...], v_ref[...])
            g_ref, d_ref, m2_ref, v2_ref = outs[n]
            g_ref[...] = g
            d_ref[...] = d
            m2_ref[...] = m2
            v2_ref[...] = v2

    flat = [vec_all, wpool_all, convw_all, dmod_all]
    out_shape = [jax.ShapeDtypeStruct((1, 128), F32)]
    for n in names:
        flat += list(params[n])
        out_shape += [jax.ShapeDtypeStruct(params[n][0].shape, F32)] * 4
    vm = pl.BlockSpec(memory_space=pltpu.VMEM)
    res = pl.pallas_call(body, name="small_adam", out_shape=out_shape, in_specs=[vm] * len(flat),
                         out_specs=[vm] * len(out_shape), compiler_params=_cp(vmem_mb=48))(*flat)
    return res[0], {n: res[1 + 4 * i:5 + 4 * i] for i, n in enumerate(names)}


_WEIGHTS = ("w_ada", "b_ada", "g_mix", "w_in", "conv_w", "conv_b", "dt_bias", "a_log", "d_skip", "g_ssd", "w_pool",
            "pool_scale", "w_out", "g_mlp", "w_up", "w_down", "g_final")


def _local_step(x2, tg2, mod3, seq, w_in_t, first_token, weights_later, start_reduce, conv_w_full, sp):
    t, d = x2.shape
    nb = t // seq
    dskip_e = jnp.repeat(sp["d_skip"], SSD_HEAD_DIM, axis=1)
    proj, u1 = _in_proj(x2, mod3, sp["g_mix"], w_in_t, seq, first_token)
    y_pool, p = _pool_fwd(proj, sp["w_pool"], sp["pool_scale"], nb, seq)
    y_ssd, yssm, h_prev, pre = _ssd_fwd(proj, conv_w_full, sp["conv_b"], sp["dt_bias"], sp["a_log"], dskip_e, sp["g_ssd"], nb, seq)
    w_out_f, w_up4, w_down4 = weights_later(y_ssd)
    w_down_f = w_down4.reshape(D_FF, d)
    h1, o, u2 = _out_proj(y_pool, y_ssd, w_out_f, x2, mod3, sp["g_mlp"], seq)
    a_up = _mlp_up(u2, w_up4)
    d_dn, dh2, sq, gg_final, d_gf = _mlp_down_loss(a_up, w_down_f, h1, mod3, sp["g_final"], tg2, seq)

    gw_down = _tn_matmul(a_up, d_dn, 512, d, "grad_w_down", square_relu=True)
    tok = start_reduce("w_down", gw_down.reshape(N_CHIPS, D_FF // N_CHIPS, d))
    d_a = _mlp_down_bwd(d_dn, w_down4, a_up, tok)
    gw_up4 = _tn_matmul(u2, d_a, 512, d, "grad_w_up", out3=True)
    tok = start_reduce("w_up", gw_up4)
    dh1, d_o, accf, gg_mlp = _mlp_up_bwd(d_a, w_up4, h1, dh2, o, mod3, sp["g_mlp"], seq, tok)
    gw_out_pool = _tn_matmul(y_pool, d_o, 512, d, "grad_w_out_pool")
    gw_out_ssd = _tn_matmul(y_ssd, d_o, 512, d, "grad_w_out_ssd")
    gw_out = jnp.concatenate([gw_out_pool, gw_out_ssd], 0)
    tok = start_reduce("w_out", gw_out.reshape(N_CHIPS, gw_out.shape[0] // N_CHIPS, d))
    d_ypool, d_yssd = _out_proj_bwd(d_o, w_out_f, tok)
    d_upool, gw_pool, g_ps = _pool_bwd(d_ypool, p, sp["w_pool"], sp["pool_scale"], nb, seq)
    d_z, d_pre, d_udt, gg_ssd, gdsk, ga, gdtb = _ssd_bwd(proj, pre, d_yssd, yssm, h_prev, sp["dt_bias"], sp["a_log"],
                                                        dskip_e, sp["g_ssd"], nb, seq)
    d_uxbc, gconvw, gconvb = _conv_bwd(d_pre, proj, conv_w_full, nb, seq)
    gw_in_t = _grad_w_in_t(d_upool, d_z, d_uxbc, d_udt, u1)
    tok = start_reduce("w_in", gw_in_t[:IN_WIDTH].reshape(N_CHIPS, IN_WIDTH // N_CHIPS, d))
    gx, accm, gg_mix = _in_proj_bwd([d_upool, d_z, d_uxbc, d_udt], w_in_t, x2, dh1, mod3, sp["g_mix"], seq, tok)

    d_mod = jnp.concatenate([accm[:, 0], accm[:, 1], accf[:, 2], accf[:, 0], accf[:, 1], d_gf[:, 0]], 1)
    vec = _pack_vec({"g_mix": gg_mix, "conv_b": gconvb, "g_ssd": gg_ssd, "pool_scale": g_ps, "g_mlp": gg_mlp,
                     "g_final": gg_final, "dt_bias": gdtb, "a_log": ga, "d_skip_lanes": gdsk, "sq_err": sq})
    return gx, d_mod, vec, gw_pool, gconvw


def kernel(x, c, w_ada, b_ada, g_mix, w_in, conv_w, conv_b, dt_bias, a_log, d_skip, g_ssd, w_pool, pool_scale, w_out, g_mlp, w_up, w_down, g_final, loss_target, m_w_ada, m_b_ada, m_g_mix, m_w_in, m_conv_w, m_conv_b, m_dt_bias, m_a_log, m_d_skip, m_g_ssd, m_w_pool, m_pool_scale, m_w_out, m_g_mlp, m_w_up, m_w_down, m_g_final, v_w_ada, v_b_ada, v_g_mix, v_w_in, v_conv_w, v_conv_b, v_dt_bias, v_a_log, v_d_skip, v_g_ssd, v_w_pool, v_pool_scale, v_w_out, v_g_mlp, v_w_up, v_w_down, v_g_final):
    nb, seq, d = x.shape
    t = nb * seq
    xi, yi, ci = _mesh_pos()
    chip = 2 * xi + yi
    me = 4 * xi + 2 * yi + ci
    ada_cols = w_ada.shape[2]
    conv_cols = conv_w.shape[2]
    in_cols = w_in.shape[2]
    w_in_s, m_w_in_s, v_w_in_s = w_in[0].T, m_w_in[0].T, v_w_in[0].T

    c8, convw8 = _all_gather_small([c, conv_w[0]])
    c_all = c8.reshape(N_DEV * nb, d)
    conv_w_full = convw8[0::2].transpose(1, 0, 2).reshape(4, N_CHIPS * conv_cols)
    b_shard = lax.dynamic_slice(b_ada, (0, chip * ada_cols), (1, ada_cols))
    mod_part, c_act = _ada_mod(c_all, w_ada[0], b_shard)
    (mod8,) = _all_gather_small([mod_part])
    mod_all = mod8[0::2].transpose(1, 0, 2).reshape(N_DEV * nb, N_CHIPS * ada_cols)
    mod3 = lax.dynamic_slice(mod_all, (nb * me, 0), (nb, N_CHIPS * ada_cols)).reshape(nb, N_MOD, d)

    (w_in4,) = _weight_all_gather([w_in_s.astype(BF)])
    w_in_t = jnp.pad(w_in4.reshape(N_CHIPS * in_cols, d), ((0, IN_PAD - N_CHIPS * in_cols), (0, 0)))
    later = [w_out[0].astype(BF), w_up[0].astype(BF), w_down[0].astype(BF)]
    g_send, g_recv, g_src, g_land, first_token = _ici_start(
        later, [jax.ShapeDtypeStruct((N_CHIPS,) + s.shape, BF) for s in later], _gather_sent, _gather_landing, "gather_start",
        after=w_in4)

    def weights_later(after):
        shards, lands = _ici_wait(g_send, g_recv, g_src, g_land, after, _gather_sent, _gather_landing, "gather_wait")
        w_out4, w_up4, w_down4 = _gather_finish(lands, shards)
        return w_out4.reshape(N_CHIPS * w_out.shape[1], d), w_up4, w_down4

    pending = {}

    def start_reduce(name, grad4):
        pending[name] = _reduce_start(grad4, "reduce_start_" + name)
        return pending[name][4]

    sp = dict(g_mix=g_mix, conv_b=conv_b, dt_bias=dt_bias, a_log=a_log, d_skip=d_skip, g_ssd=g_ssd,
              w_pool=w_pool[0], pool_scale=pool_scale, g_mlp=g_mlp, g_final=g_final.reshape(1, d))
    gx, d_mod, vec, gw_pool, gconvw = _local_step(
        x.reshape(t, d), loss_target.reshape(t, d), mod3, seq, w_in_t, first_token, weights_later, start_reduce, conv_w_full, sp)

    pos = jnp.stack([ci, chip, me]).astype(jnp.int32)
    halves = []
    for name in ("w_in", "w_out", "w_up", "w_down"):
        r_send, r_recv, r_src, r_land, _ = pending[name]
        own, recv = _reduce_wait(r_send, r_recv, r_src, r_land, gx, "reduce_wait_" + name)
        halves.append(_sum_eight(recv, own, pos))
    g_in, g_out, g_up, g_down = _halves_exchange(halves)

    vec8, wpool8, convw8g, dmod8 = _all_gather_small(
        [vec, gw_pool.reshape(4 * POOL_GROUP, POOL_GROUP).astype(BF), gconvw, d_mod])
    convw8s = lax.dynamic_slice(convw8g, (0, 0, chip * conv_cols), (N_DEV, 8, conv_cols))
    m_in = dict(b_ada=m_b_ada, g_mix=m_g_mix, conv_w=m_conv_w[0], conv_b=m_conv_b, dt_bias=m_dt_bias, a_log=m_a_log,
                d_skip=m_d_skip, g_ssd=m_g_ssd, w_pool=m_w_pool.reshape(4 * POOL_GROUP, POOL_GROUP), pool_scale=m_pool_scale,
                g_mlp=m_g_mlp, g_final=m_g_final.reshape(1, d))
    v_in = dict(b_ada=v_b_ada, g_mix=v_g_mix, conv_w=v_conv_w[0], conv_b=v_conv_b, dt_bias=v_dt_bias, a_log=v_a_log,
                d_skip=v_d_skip, g_ssd=v_g_ssd, w_pool=v_w_pool.reshape(4 * POOL_GROUP, POOL_GROUP), pool_scale=v_pool_scale,
                g_mlp=v_g_mlp, g_final=v_g_final.reshape(1, d))
    w_small = dict(sp, b_ada=b_ada, conv_w=conv_w[0], w_pool=w_pool.reshape(4 * POOL_GROUP, POOL_GROUP))
    loss_row, small = _small_adam(vec8, wpool8, convw8s, dmod8, {n: (w_small[n], m_in[n], v_in[n]) for n in _SMALL_PARAMS})

    dmod_all = dmod8.reshape(N_DEV * nb, N_CHIPS * ada_cols)
    dmod_cols = lax.dynamic_slice(dmod_all, (0, chip * ada_cols), (N_DEV * nb, ada_cols))
    res = {n: tuple(r.reshape(w.shape) for r in small[n])
           for n, w in (("b_ada", b_ada), ("g_mix", g_mix), ("conv_w", conv_w), ("conv_b", conv_b), ("dt_bias", dt_bias),
                        ("a_log", a_log), ("d_skip", d_skip), ("g_ssd", g_ssd), ("w_pool", w_pool), ("pool_scale", pool_scale),
                        ("g_mlp", g_mlp), ("g_final", g_final))}
    g_ada, d_ada, m_ada, v_ada = _adam_ada(c_act.T.astype(BF), dmod_cols, w_ada[0], m_w_ada[0], v_w_ada[0])
    res["w_ada"] = (g_ada[None], d_ada[None], m_ada[None], v_ada[None])
    dl, m2, v2 = _adam_big(g_in, w_in_s, m_w_in_s, v_w_in_s)
    res["w_in"] = (g_in.T[None], dl.T[None], m2.T[None], v2.T[None])
    for n, g, w, m, v in (("w_out", g_out, w_out, m_w_out, v_w_out), ("w_up", g_up, w_up, m_w_up, v_w_up),
                          ("w_down", g_down, w_down, m_w_down, v_w_down)):
        dl, m2, v2 = _adam_big(g, w[0], m[0], v[0])
        res[n] = (g[None], dl[None], m2[None], v2[None])

    loss = loss_row[0, 0]
    return (loss, gx.reshape(nb, seq, d), *[res[n][0] for n in _WEIGHTS], *[res[n][1] for n in _WEIGHTS],
            *[res[n][2] for n in _WEIGHTS], *[res[n][3] for n in _WEIGHTS])
```

```python
import functools

import jax
import jax.numpy as jnp
from jax import lax
from jax.experimental import pallas as pl
from jax.experimental.pallas import tpu as pltpu

F32 = jnp.float32
BF = jnp.bfloat16
MESH = pl.DeviceIdType.MESH

EPS = 1e-5
D_MODEL = 1024
POOL_WIDTH = 512
POOL_WINDOWS = (2, 4, 8, 16)
POOL_GROUP = 128
SSD_INNER = 1024
SSD_HEADS = 16
SSD_HEAD_DIM = 64
SSD_STATE = 128
GROUP_W = 512
CHUNK = 128
CONV_CH = 1536
OFF_Z = 512
OFF_XBC = 1536
OFF_DT = 3072
IN_WIDTH = 3088
IN_PAD = 3200
D_FF = 4096
N_MOD = 6
N_CHIPS = 4
N_DEV = 8
HALO = 16
CONV_HALO = 8

ADAM_LR = 0.001
ADAM_B1 = 0.9
ADAM_B2 = 0.999
ADAM_EPS = 1e-08
ADAM_WD = 0.01
ADAM_STEP = 10

VMEM_BYTES_V7X = 64 * 1024 * 1024


def _cp(semantics=None, vmem_mb=48, **kw):
    args = dict(vmem_limit_bytes=vmem_mb * 1024 * 1024, **kw)
    if semantics is not None:
        args["dimension_semantics"] = semantics
    return pltpu.CompilerParams(**args)


def _out(shape, dtype):
    return pltpu.HBM(shape, dtype)


def _pin(*arrays):
    return [pltpu.with_memory_space_constraint(a, pltpu.HBM) for a in arrays]


def _nn(a, b):
    return jnp.dot(a, b, preferred_element_type=F32)


def _nt(a, b):
    return lax.dot_general(a, b, (((1,), (1,)), ((), ())), preferred_element_type=F32)


def _tn(a, b):
    return lax.dot_general(a, b, (((0,), (0,)), ((), ())), preferred_element_type=F32)


def _split3(v):
    hi = v.astype(BF)
    r1 = v - hi.astype(F32)
    mid = r1.astype(BF)
    lo = (r1 - mid.astype(F32)).astype(BF)
    return hi, mid, lo


def _exact_nn(v, m01):
    hi, mid, lo = _split3(v)
    return _nn(hi, m01) + _nn(mid, m01) + _nn(lo, m01)


def _exact_nn_left(m01, v):
    hi, mid, lo = _split3(v)
    return _nn(m01, hi) + _nn(m01, mid) + _nn(m01, lo)


def _exact_nt_left(m01, v):
    hi, mid, lo = _split3(v)
    return _nt(m01, hi) + _nt(m01, mid) + _nt(m01, lo)


def _sigmoid(v):
    return 1.0 / (1.0 + jnp.exp(-v))


def _iota(shape, dim):
    return lax.broadcasted_iota(jnp.int32, shape, dim)


def _head_expand_matrix(heads, width):
    return (_iota((heads, width), 1) // SSD_HEAD_DIM == _iota((heads, width), 0)).astype(BF)


def _head_reduce_matrix(width, heads):
    return (_iota((width, heads), 0) // SSD_HEAD_DIM == _iota((width, heads), 1)).astype(BF)


def _mesh_pos():
    return lax.axis_index("x"), lax.axis_index("y"), lax.axis_index("c")


def _flip(v, bit):
    return v + bit - 2 * bit * v


def _all_gather_small(arrays, token=None):
    n = len(arrays)
    extra = [] if token is None else [token]

    def body(*refs):
        in_refs, out_refs = refs[:n], refs[n + len(extra):2 * n + len(extra)]
        send_sems, recv_sems, local_sems = refs[2 * n + len(extra):]
        x, y, c = _mesh_pos()
        me = 4 * x + 2 * y + c
        local = []
        for a in range(n):
            cp = pltpu.make_async_copy(in_refs[a], out_refs[a].at[me], local_sems.at[a])
            cp.start()
            local.append(cp)
        sends = []
        for k in range(1, N_DEV):
            peer = (_flip(x, (k >> 2) & 1), _flip(y, (k >> 1) & 1), _flip(c, k & 1))
            for a in range(n):
                cp = pltpu.make_async_remote_copy(
                    src_ref=in_refs[a], dst_ref=out_refs[a].at[me],
                    send_sem=send_sems.at[a, k], recv_sem=recv_sems.at[a, k],
                    device_id=peer, device_id_type=MESH)
                cp.start()
                sends.append(cp)
        for k in range(1, N_DEV):
            px, py, pc = _flip(x, (k >> 2) & 1), _flip(y, (k >> 1) & 1), _flip(c, k & 1)
            src = 4 * px + 2 * py + pc
            for a in range(n):
                pltpu.make_async_remote_copy(
                    src_ref=in_refs[a], dst_ref=out_refs[a].at[src],
                    send_sem=send_sems.at[a, k], recv_sem=recv_sems.at[a, k],
                    device_id=(px, py, pc), device_id_type=MESH).wait_recv()
        for cp in sends:
            cp.wait_send()
        for cp in local:
            cp.wait()

    vm = pl.BlockSpec(memory_space=pltpu.VMEM)
    return pl.pallas_call(
        body, name="all_gather_small",
        out_shape=[jax.ShapeDtypeStruct((N_DEV,) + a.shape, a.dtype) for a in arrays],
        in_specs=[vm] * (n + len(extra)), out_specs=[vm] * n,
        scratch_shapes=[pltpu.SemaphoreType.DMA((n, N_DEV)), pltpu.SemaphoreType.DMA((n, N_DEV)),
                        pltpu.SemaphoreType.DMA((n,))],
        compiler_params=_cp(vmem_mb=32),
    )(*arrays, *extra)


def _weight_all_gather(shards):
    n = len(shards)
    any_spec = _HBM

    def body(*refs):
        in_refs, out_refs = refs[:n], refs[n:2 * n]
        send_sems, recv_sems, fsend_sems, frecv_sems, local_sems = refs[2 * n:]
        x, y, c = _mesh_pos()
        chip = 2 * x + y
        local, sends = [], []
        for a in range(n):
            cp = pltpu.make_async_copy(in_refs[a], out_refs[a].at[chip], local_sems.at[a])
            cp.start()
            local.append(cp)

        def half(a, which):
            hc = shards[a].shape[1] // 2
            return pl.ds(pl.multiple_of(which * hc, 128), hc)

        for j in range(1, N_CHIPS):
            px, py = _flip(x, (j >> 1) & 1), _flip(y, j & 1)
            for a in range(n):
                cp = pltpu.make_async_remote_copy(
                    src_ref=in_refs[a].at[:, half(a, c)], dst_ref=out_refs[a].at[chip, :, half(a, c)],
                    send_sem=send_sems.at[a, j], recv_sem=recv_sems.at[a, j],
                    device_id=(px, py, c), device_id_type=MESH)
                cp.start()
                sends.append(cp)
        for j in range(1, N_CHIPS):
            px, py = _flip(x, (j >> 1) & 1), _flip(y, j & 1)
            src = 2 * px + py
            for a in range(n):
                landed = out_refs[a].at[src, :, half(a, c)]
                pltpu.make_async_remote_copy(
                    src_ref=landed, dst_ref=landed, send_sem=send_sems.at[a, j], recv_sem=recv_sems.at[a, j],
                    device_id=(px, py, c), device_id_type=MESH).wait_recv()
                cp = pltpu.make_async_remote_copy(
                    src_ref=landed, dst_ref=landed, send_sem=fsend_sems.at[a, j], recv_sem=frecv_sems.at[a, j],
                    device_id=(x, y, 1 - c), device_id_type=MESH)
                cp.start()
                sends.append(cp)
        for j in range(1, N_CHIPS):
            px, py = _flip(x, (j >> 1) & 1), _flip(y, j & 1)
            src = 2 * px + py
            for a in range(n):
                other = out_refs[a].at[src, :, half(a, 1 - c)]
                pltpu.make_async_remote_copy(
                    src_ref=other, dst_ref=other, send_sem=fsend_sems.at[a, j], recv_sem=frecv_sems.at[a, j],
                    device_id=(x, y, 1 - c), device_id_type=MESH).wait_recv()
        for cp in sends:
            cp.wait_send()
        for cp in local:
            cp.wait()

    return pl.pallas_call(
        body, name="weight_all_gather",
        out_shape=[_out((N_CHIPS,) + s.shape, s.dtype) for s in shards],
        in_specs=[any_spec] * n, out_specs=[any_spec] * n,
        scratch_shapes=[pltpu.SemaphoreType.DMA((n, N_CHIPS))] * 4 + [pltpu.SemaphoreType.DMA((n,))],
        compiler_params=_cp(vmem_mb=16),
    )(*shards)


_HBM = pl.BlockSpec(memory_space=pltpu.HBM)
_SEM = pl.BlockSpec(memory_space=pltpu.SEMAPHORE)
_DATAFLOW = pltpu.SideEffectType.DATAFLOW_SIDE_EFFECTING


def _peer_chip(x, y, j):
    return _flip(x, (j >> 1) & 1), _flip(y, j & 1)


def _ici_start(srcs, land_shapes, sent, landing, name, after):
    n = len(srcs)

    def body(*refs):
        src_refs, land_refs = refs[:n], refs[n:2 * n]
        send_sems, recv_sems = refs[2 * n + 1], refs[2 * n + 2]
        token = refs[-1]
        x, y, c = _mesh_pos()
        for j in range(1, N_CHIPS):
            px, py = _peer_chip(x, y, j)
            for a in range(n):
                pltpu.make_async_remote_copy(
                    src_ref=sent(src_refs[a], c, 2 * px + py), dst_ref=landing(land_refs[a], c, 2 * x + y),
                    send_sem=send_sems.at[a * (N_CHIPS - 1) + j - 1], recv_sem=recv_sems.at[a * (N_CHIPS - 1) + j - 1],
                    device_id=(px, py, c), device_id_type=MESH).start()
        token[...] = jnp.zeros_like(token)

    sems = pltpu.SemaphoreType.DMA((n * (N_CHIPS - 1),))
    lands = [pltpu.with_memory_space_constraint(lax.empty(s.shape, s.dtype), pltpu.HBM) for s in land_shapes]
    outs = pl.pallas_call(
        body, name=name,
        out_shape=(sems, sems, *[pltpu.HBM(s.shape, s.dtype) for s in srcs],
                   *[pltpu.HBM(s.shape, s.dtype) for s in land_shapes], jax.ShapeDtypeStruct((8, 128), F32)),
        in_specs=[_HBM] * (2 * n + 1), out_specs=[_SEM, _SEM] + [_HBM] * (2 * n) + [pl.BlockSpec(memory_space=pltpu.VMEM)],
        input_output_aliases={i: 2 + i for i in range(2 * n)},
        compiler_params=pltpu.CompilerParams(has_side_effects=_DATAFLOW),
    )(*_pin(*srcs), *lands, *_pin(after))
    return outs[0], outs[1], outs[2:2 + n], outs[2 + n:2 + 2 * n], outs[-1]


def _ici_wait(send_sems, recv_sems, src_thru, land_thru, after, sent, landing, name):
    n = len(src_thru)

    def body(*refs):
        src_refs, land_refs = refs[:n], refs[n:2 * n]
        send_sems, recv_sems = refs[2 * n], refs[2 * n + 1]
        x, y, c = _mesh_pos()
        for j in range(1, N_CHIPS):
            px, py = _peer_chip(x, y, j)
            for a in range(n):
                cp = pltpu.make_async_remote_copy(
                    src_ref=sent(src_refs[a], c, 2 * px + py), dst_ref=landing(land_refs[a], c, 2 * px + py),
                    send_sem=send_sems.at[a * (N_CHIPS - 1) + j - 1], recv_sem=recv_sems.at[a * (N_CHIPS - 1) + j - 1],
                    device_id=(px, py, c), device_id_type=MESH)
                cp.wait_send()
                cp.wait_recv()

    outs = pl.pallas_call(
        body, name=name,
        out_shape=tuple(pltpu.HBM(s.shape, s.dtype) for s in (*src_thru, *land_thru)),
        in_specs=[_HBM] * (2 * n) + [_SEM, _SEM, _HBM], out_specs=[_HBM] * (2 * n),
        input_output_aliases={i: i for i in range(2 * n)},
        compiler_params=pltpu.CompilerParams(has_side_effects=_DATAFLOW),
    )(*src_thru, *land_thru, send_sems, recv_sems, *_pin(after))
    return outs[:n], outs[n:]


def _col_half(ref, which, lead=()):
    hc = ref.shape[-1] // 2
    return ref.at[(*lead, slice(None), pl.ds(pl.multiple_of(which * hc, 128), hc))]


def _gather_sent(ref, c, dst_chip):
    return _col_half(ref, c)


def _gather_landing(ref, c, src_chip):
    return _col_half(ref, c, lead=(src_chip,))


def _reduce_copy(src_ref, land_ref, send_sems, recv_sems, k, receiving):
    x, y, c = _mesh_pos()
    px, py, pc = _flip(x, (k >> 2) & 1), _flip(y, (k >> 1) & 1), _flip(c, k & 1)
    hc = src_ref.shape[2] // 2
    src = src_ref.at[2 * px + py, :, pl.ds(pl.multiple_of(pc * hc, 128), hc)]
    slot = (4 * px + 2 * py + pc) if receiving else (4 * x + 2 * y + c)
    return pltpu.make_async_remote_copy(
        src_ref=src, dst_ref=land_ref.at[slot], send_sem=send_sems.at[k - 1], recv_sem=recv_sems.at[k - 1],
        device_id=(px, py, pc), device_id_type=MESH)


def _reduce_start(grad4, name):
    k4, r, cols = grad4.shape

    def body(src_ref, land_ref, send_sems, recv_sems, src_thru, land_thru, token):
        for k in range(1, N_DEV):
            _reduce_copy(src_ref, land_ref, send_sems, recv_sems, k, receiving=False).start()
        token[...] = jnp.zeros_like(token)

    sems = pltpu.SemaphoreType.DMA((N_DEV - 1,))
    land = pltpu.with_memory_space_constraint(lax.empty((N_DEV, r, cols // 2), grad4.dtype), pltpu.HBM)
    return pl.pallas_call(
        body, name=name,
        out_shape=(sems, sems, pltpu.HBM(grad4.shape, grad4.dtype), pltpu.HBM(land.shape, land.dtype),
                   jax.ShapeDtypeStruct((8, 128), F32)),
        in_specs=[_HBM, _HBM], out_specs=[_SEM, _SEM, _HBM, _HBM, pl.BlockSpec(memory_space=pltpu.VMEM)],
        input_output_aliases={0: 2, 1: 3},
        compiler_params=pltpu.CompilerParams(has_side_effects=_DATAFLOW),
    )(*_pin(grad4), land)


def _reduce_wait(send_sems, recv_sems, src_thru, land_thru, after, name):
    def body(src_ref, land_ref, send_sems, recv_sems, after_ref, src_out, land_out):
        for k in range(1, N_DEV):
            cp = _reduce_copy(src_ref, land_ref, send_sems, recv_sems, k, receiving=True)
            cp.wait_send()
            cp.wait_recv()

    return pl.pallas_call(
        body, name=name,
        out_shape=(pltpu.HBM(src_thru.shape, src_thru.dtype), pltpu.HBM(land_thru.shape, land_thru.dtype)),
        in_specs=[_HBM, _HBM, _SEM, _SEM, _HBM], out_specs=[_HBM, _HBM],
        input_output_aliases={0: 0, 1: 1},
        compiler_params=pltpu.CompilerParams(has_side_effects=_DATAFLOW),
    )(src_thru, land_thru, send_sems, recv_sems, *_pin(after))


def _sum_eight(recv, grad4, pos):
    n, r, hc = recv.shape
    tc = hc // 2

    def body(pos_ref, r_ref, g_ref, o_ref):
        me = pos_ref[2]
        o_ref[...] = jnp.zeros_like(o_ref)
        for s in range(n):
            @pl.when(me == s)
            def _():
                o_ref[...] += g_ref[0].astype(F32)

            @pl.when(me != s)
            def _():
                o_ref[...] += r_ref[s].astype(F32)

    grid_spec = pltpu.PrefetchScalarGridSpec(
        num_scalar_prefetch=1, grid=(2,),
        in_specs=[pl.BlockSpec((n, r, tc), lambda i, pos: (0, 0, i)),
                  pl.BlockSpec((1, r, tc), lambda i, pos: (pos[1], 0, pos[0] * 2 + i))],
        out_specs=pl.BlockSpec((r, tc), lambda i, pos: (0, i)))
    return pl.pallas_call(body, name="sum_eight", grid_spec=grid_spec, out_shape=_out((r, hc), F32),
                          compiler_params=_cp(("parallel",), 32))(pos, *_pin(recv, grad4))


def _gather_finish(lands, shards):
    n = len(lands)
    any_spec = _HBM

    def body(*refs):
        shard_refs, out_refs = refs[n:2 * n], refs[2 * n:3 * n]
        send_sems, recv_sems, local_sems = refs[3 * n:]
        x, y, c = _mesh_pos()
        chip = 2 * x + y
        local, sends = [], []
        for a in range(n):
            cp = pltpu.make_async_copy(shard_refs[a], out_refs[a].at[chip], local_sems.at[a])
            cp.start()
            local.append(cp)
        for j in range(1, N_CHIPS):
            px, py = _peer_chip(x, y, j)
            for a in range(n):
                landed = _col_half(out_refs[a], c, lead=(2 * px + py,))
                cp = pltpu.make_async_remote_copy(
                    src_ref=landed, dst_ref=landed, send_sem=send_sems.at[a, j], recv_sem=recv_sems.at[a, j],
                    device_id=(x, y, 1 - c), device_id_type=MESH)
                cp.start()
                sends.append(cp)
        for j in range(1, N_CHIPS):
            px, py = _peer_chip(x, y, j)
            for a in range(n):
                other = _col_half(out_refs[a], 1 - c, lead=(2 * px + py,))
                pltpu.make_async_remote_copy(
                    src_ref=other, dst_ref=other, send_sem=send_sems.at[a, j], recv_sem=recv_sems.at[a, j],
                    device_id=(x, y, 1 - c), device_id_type=MESH).wait_recv()
        for cp in sends:
            cp.wait_send()
        for cp in local:
            cp.wait()

    return pl.pallas_call(
        body, name="gather_finish",
        out_shape=[_out(l.shape, l.dtype) for l in lands],
        in_specs=[any_spec] * (2 * n), out_specs=[any_spec] * n,
        input_output_aliases={i: i for i in range(n)},
        scratch_shapes=[pltpu.SemaphoreType.DMA((n, N_CHIPS))] * 2 + [pltpu.SemaphoreType.DMA((n,))],
        compiler_params=_cp(vmem_mb=16),
    )(*lands, *shards)


def _halves_exchange(halves):
    n = len(halves)
    any_spec = _HBM

    def body(*refs):
        in_refs, out_refs = refs[:n], refs[n:2 * n]
        send_sems, recv_sems, local_sems = refs[2 * n:]
        x, y, c = _mesh_pos()
        copies, local = [], []
        for a in range(n):
            hc = halves[a].shape[1]
            mine = pl.ds(pl.multiple_of(c * hc, 128), hc)
            lc = pltpu.make_async_copy(in_refs[a], out_refs[a].at[:, mine], local_sems.at[a])
            lc.start()
            local.append(lc)
            cp = pltpu.make_async_remote_copy(
                src_ref=in_refs[a], dst_ref=out_refs[a].at[:, mine],
                send_sem=send_sems.at[a], recv_sem=recv_sems.at[a],
                device_id=(x, y, 1 - c), device_id_type=MESH)
            cp.start()
            copies.append(cp)
        for a in range(n):
            hc = halves[a].shape[1]
            theirs = pl.ds(pl.multiple_of((1 - c) * hc, 128), hc)
            pltpu.make_async_remote_copy(
                src_ref=in_refs[a], dst_ref=out_refs[a].at[:, theirs],
                send_sem=send_sems.at[a], recv_sem=recv_sems.at[a],
                device_id=(x, y, 1 - c), device_id_type=MESH).wait_recv()
        for cp in copies:
            cp.wait_send()
        for lc in local:
            lc.wait()

    return pl.pallas_call(
        body, name="halves_exchange",
        out_shape=[_out((h.shape[0], 2 * h.shape[1]), h.dtype) for h in halves],
        in_specs=[any_spec] * n, out_specs=[any_spec] * n,
        scratch_shapes=[pltpu.SemaphoreType.DMA((n,))] * 3,
        compiler_params=_cp(vmem_mb=16),
    )(*halves)


def _adam_math(w, g, m, v):
    m2 = ADAM_B1 * m + (1.0 - ADAM_B1) * g
    v2 = ADAM_B2 * v + (1.0 - ADAM_B2) * (g * g)
    m_hat = m2 / (1.0 - ADAM_B1 ** ADAM_STEP)
    v_hat = v2 / (1.0 - ADAM_B2 ** ADAM_STEP)
    delta = -ADAM_LR * (m_hat / (jnp.sqrt(v_hat) + ADAM_EPS) + ADAM_WD * w)
    return delta, m2, v2


def _adam_big(g, w, m, v):
    r, c = w.shape
    tc = c // 4

    def body(g_ref, w_ref, m_ref, v_ref, d_ref, m2_ref, v2_ref):
        d, m2, v2 = _adam_math(w_ref[...], g_ref[...], m_ref[...], v_ref[...])
        d_ref[...] = d
        m2_ref[...] = m2
        v2_ref[...] = v2

    spec = pl.BlockSpec((r, tc), lambda i: (0, i))
    sh = _out((r, c), F32)
    return pl.pallas_call(body, name="adam_big", grid=(4,), in_specs=[spec] * 4, out_specs=[spec] * 3,
                          out_shape=[sh, sh, sh], compiler_params=_cp(("parallel",), 32))(*_pin(g, w, m, v))


def _adam_ada(c_act_t, dmod_cols, w, m, v):
    r, c = w.shape
    tc = 512

    def body(ct_ref, dm_ref, w_ref, m_ref, v_ref, g_ref, d_ref, m2_ref, v2_ref):
        g = _nn(ct_ref[...], dm_ref[...].astype(BF))
        d, m2, v2 = _adam_math(w_ref[...], g, m_ref[...], v_ref[...])
        g_ref[...] = g
        d_ref[...] = d
        m2_ref[...] = m2
        v2_ref[...] = v2

    spec = pl.BlockSpec((r, tc), lambda i: (0, i))
    sh = _out((r, c), F32)
    return pl.pallas_call(
        body, name="adam_ada", grid=(c // tc,),
        in_specs=[pl.BlockSpec(c_act_t.shape, lambda i: (0, 0)), pl.BlockSpec((dmod_cols.shape[0], tc), lambda i: (0, i)),
                  spec, spec, spec],
        out_specs=[spec] * 4, out_shape=[sh] * 4, compiler_params=_cp(("parallel",), 48))(*_pin(c_act_t, dmod_cols, w, m, v))


def _ada_mod(c_all, w_shard, b_shard):
    nb, d = c_all.shape
    cols = w_shard.shape[1]
    tc = 512

    def body(c_ref, w_ref, b_ref, mod_ref, act_ref):
        cv = c_ref[...]
        act = cv * _sigmoid(cv)
        act_ref[...] = act
        mod_ref[...] = _nn(act.astype(BF), w_ref[...].astype(BF)) + b_ref[...]

    return pl.pallas_call(
        body, name="ada_mod", grid=(cols // tc,),
        in_specs=[pl.BlockSpec((nb, d), lambda i: (0, 0)), pl.BlockSpec((d, tc), lambda i: (0, i)),
                  pl.BlockSpec((1, tc), lambda i: (0, i))],
        out_specs=[pl.BlockSpec((nb, tc), lambda i: (0, i)), pl.BlockSpec((nb, d), lambda i: (0, 0))],
        out_shape=[_out((nb, cols), F32), _out((nb, d), F32)],
        compiler_params=_cp(("arbitrary",), 32))(*_pin(c_all, w_shard, b_shard))


SUB_ROWS = 256
ROW_TILE = 512


def _sub_rows(tm):
    return [slice(s, s + SUB_ROWS) for s in range(0, tm, SUB_ROWS)] if tm > SUB_ROWS else [slice(0, tm)]


def _token_spec():
    return pl.BlockSpec((8, 128), lambda *_: (0, 0))


def _in_proj(x, mod3, g_mix, w_in_t, seq, token):
    t, d = x.shape
    tm = min(ROW_TILE, seq)
    tps = seq // tm

    def body(x_ref, mod_ref, g_ref, w_ref, tok_ref, proj_ref, u1_ref):
        for rows in _sub_rows(tm):
            xv = x_ref[rows, :]
            r = lax.rsqrt(jnp.mean(xv * xv, -1, keepdims=True) + EPS)
            u = (xv * r * g_ref[...]) * (1.0 + mod_ref[0, 1:2, :]) + mod_ref[0, 0:1, :]
            ub = u.astype(BF)
            u1_ref[rows, :] = ub
            proj_ref[rows, :] = _nt(ub, w_ref[...])

    return pl.pallas_call(
        body, name="in_proj", grid=(t // tm,),
        in_specs=[pl.BlockSpec((tm, d), lambda i: (i, 0)), pl.BlockSpec((1, N_MOD, d), lambda i: (i // tps, 0, 0)),
                  pl.BlockSpec((1, d), lambda i: (0, 0)), pl.BlockSpec((IN_PAD, d), lambda i: (0, 0)), _token_spec()],
        out_specs=[pl.BlockSpec((tm, IN_PAD), lambda i: (i, 0)), pl.BlockSpec((tm, d), lambda i: (i, 0))],
        out_shape=[_out((t, IN_PAD), F32), _out((t, d), BF)],
        compiler_params=_cp(("parallel",), 48))(*_pin(x, mod3, g_mix, w_in_t, token))


def _pool_tile(seq):
    return min(512, seq)


def _pool_fwd(proj, w_pool, pool_scale, nb, seq):
    ts = _pool_tile(seq)
    nt = seq // ts

    def body(u_ref, halo_ref, wp_ref, ps_ref, yp_ref, p_ref):
        i = pl.program_id(1)
        halo = jnp.where(i == 0, 0.0, halo_ref[...])
        u = u_ref[...]
        ext = jnp.concatenate([halo, u], 0)
        tpos = i * ts + _iota((ts, 1), 0)
        for g, w in enumerate(POOL_WINDOWS):
            gs = slice(g * POOL_GROUP, (g + 1) * POOL_GROUP)
            s = ext[:, gs]
            sh = 1
            while sh < w:
                s = s + pltpu.roll(s, sh, 0)
                sh *= 2
            cnt = jnp.minimum(tpos + 1, w).astype(F32)
            pb = (s[HALO:] / cnt - u[:, gs]).astype(BF)
            p_ref[:, gs] = pb
            yp_ref[:, gs] = (_nn(pb, wp_ref[g].astype(BF)) * ps_ref[:, gs]).astype(BF)

    hb = ts // HALO
    return pl.pallas_call(
        body, name="pool_fwd", grid=(nb, nt),
        in_specs=[pl.BlockSpec((ts, POOL_WIDTH), lambda b, i: (b * nt + i, 0)),
                  pl.BlockSpec((HALO, POOL_WIDTH), lambda b, i: (jnp.maximum((b * nt + i) * hb - 1, 0), 0)),
                  pl.BlockSpec((4, POOL_GROUP, POOL_GROUP), lambda b, i: (0, 0, 0)),
                  pl.BlockSpec((1, POOL_WIDTH), lambda b, i: (0, 0))],
        out_specs=[pl.BlockSpec((ts, POOL_WIDTH), lambda b, i: (b * nt + i, 0))] * 2,
        out_shape=[_out((nb * seq, POOL_WIDTH), BF)] * 2,
        compiler_params=_cp(("parallel", "parallel"), 32))(*_pin(proj, proj, w_pool, pool_scale))


def _conv_pre(uxbc, halo, cw, cb, first):
    halo = jnp.where(first, 0.0, halo)
    ext = jnp.concatenate([halo, uxbc], 0)
    pre = cb + uxbc * cw[3:4]
    for k in (2, 1, 0):
        pre = pre + pltpu.roll(ext, 3 - k, 0)[CONV_HALO:] * cw[k:k + 1]
    return pre


def _chunk_terms(pre, udt, dtb, alog):
    sg = _sigmoid(pre)
    xbc = pre * sg
    dtp = udt[:, :SSD_HEADS] + dtb
    dt = jnp.maximum(dtp, 0.0) + jnp.log(1.0 + jnp.exp(-jnp.abs(dtp)))
    a = -jnp.exp(alog)
    da = dt * a
    tril = (_iota((CHUNK, CHUNK), 0) >= _iota((CHUNK, CHUNK), 1))
    acum = _exact_nn_left(tril.astype(BF), da)
    eye = (_iota((SSD_HEADS, SSD_HEADS), 0) == _iota((SSD_HEADS, SSD_HEADS), 1)).astype(BF)
    acum_t = _exact_nt_left(eye, acum)
    expand = _head_expand_matrix(SSD_HEADS, SSD_INNER)
    acum_e = _exact_nn(acum, expand)
    dt_e = _exact_nn(dt, expand)
    last_e = acum_e[CHUNK - 1:CHUNK]
    return dict(pre=pre, sg=sg, xbc=xbc, dtp=dtp, dt=dt, a=a, acum=acum, acum_t=acum_t, tril=tril,
                dt_e=dt_e, e_a=jnp.exp(acum_e), d_out=jnp.exp(last_e - acum_e), c_dec=jnp.exp(last_e))


def _head_decay(r, h):
    seg = r["acum"][:, h:h + 1] - r["acum_t"][h:h + 1, :]
    return jnp.where(r["tril"], jnp.exp(jnp.minimum(seg, 0.0)), 0.0)


def _ssd_specs(nb, seq, reverse):
    nc = seq // CHUNK
    per = seq // CONV_HALO

    def cidx(c):
        return (nc - 1 - c) if reverse else c

    def row(b, c):
        return b * nc + cidx(c)

    specs = [
        pl.BlockSpec((CHUNK, CONV_CH), lambda b, c: (row(b, c), 1)),
        pl.BlockSpec((CONV_HALO, CONV_CH),
                     lambda b, c: (jnp.maximum(b * per + cidx(c) * (CHUNK // CONV_HALO) - 1, 0), 1)),
        pl.BlockSpec((CHUNK, GROUP_W), lambda b, c: (row(b, c), 1)),
        pl.BlockSpec((CHUNK, GROUP_W), lambda b, c: (row(b, c), 2)),
        pl.BlockSpec((CHUNK, 128), lambda b, c: (row(b, c), OFF_DT // 128)),
    ]
    return specs, row, cidx, nc


def _const_spec(shape):
    return pl.BlockSpec(shape, lambda b, c: (0,) * len(shape))


def _ssd_fwd(proj, conv_w, conv_b, dt_bias, a_log, dskip_e, g_ssd, nb, seq):
    specs, row, cidx, nc = _ssd_specs(nb, seq, reverse=False)

    def body(uxbc_ref, halo_ref, z0_ref, z1_ref, udt_ref, cw_ref, cb_ref, dtb_ref, alog_ref, dsk_ref, gs_ref,
             yssd_ref, yssm_ref, hprev_ref, pre_ref, h_ref, yd_ref):
        c = pl.program_id(1)

        @pl.when(c == 0)
        def _():
            h_ref[...] = jnp.zeros_like(h_ref)

        pre = _conv_pre(uxbc_ref[...], halo_ref[...], cw_ref[...], cb_ref[...], c == 0)
        pre_ref[...] = pre
        r = _chunk_terms(pre, udt_ref[...], dtb_ref[...], alog_ref[...])
        xbc = r["xbc"]
        xs = xbc[:, :SSD_INNER]
        xdt = xs * r["dt_e"]
        xdt_b = xdt.astype(BF)
        xdo_b = (xdt * r["d_out"]).astype(BF)
        hprev_ref[0, 0] = h_ref[...]
        for g in range(2):
            gs = slice(g * GROUP_W, (g + 1) * GROUP_W)
            bg = xbc[:, SSD_INNER + g * SSD_STATE:SSD_INNER + (g + 1) * SSD_STATE].astype(BF)
            cg = xbc[:, SSD_INNER + (2 + g) * SSD_STATE:SSD_INNER + (3 + g) * SSD_STATE].astype(BF)
            scores = _nt(cg, bg)
            hg = h_ref[g]
            y_off = _nn(cg, hg.astype(BF)) * r["e_a"][:, gs]
            for hh in range(8):
                h = g * 8 + hh
                m = (scores * _head_decay(r, h)).astype(BF)
                yd_ref[:, h * SSD_HEAD_DIM:(h + 1) * SSD_HEAD_DIM] = _nn(m, xdt_b[:, h * SSD_HEAD_DIM:(h + 1) * SSD_HEAD_DIM])
            h_ref[g] = hg * r["c_dec"][:, gs] + _tn(bg, xdo_b[:, gs])
            y = yd_ref[:, gs] + y_off + dsk_ref[:, gs] * xs[:, gs]
            yssm_ref[:, gs] = y
            zg = (z0_ref if g == 0 else z1_ref)[...]
            yg = y * (zg * _sigmoid(zg))
            rg = lax.rsqrt(jnp.mean(yg * yg, -1, keepdims=True) + EPS)
            yssd_ref[:, gs] = (yg * rg * gs_ref[:, gs]).astype(BF)

    t = nb * seq
    return pl.pallas_call(
        body, name="ssd_fwd", grid=(nb, nc),
        in_specs=specs + [_const_spec((4, CONV_CH)), _const_spec((1, CONV_CH)), _const_spec((1, SSD_HEADS)),
                          _const_spec((1, SSD_HEADS)), _const_spec((1, SSD_INNER)), _const_spec((1, SSD_INNER))],
        out_specs=[pl.BlockSpec((CHUNK, SSD_INNER), lambda b, c: (row(b, c), 0)),
                   pl.BlockSpec((CHUNK, SSD_INNER), lambda b, c: (row(b, c), 0)),
                   pl.BlockSpec((1, 1, 2, SSD_STATE, GROUP_W), lambda b, c: (b, c, 0, 0, 0)),
                   pl.BlockSpec((CHUNK, CONV_CH), lambda b, c: (row(b, c), 0))],
        out_shape=[_out((t, SSD_INNER), BF), _out((t, SSD_INNER), F32),
                   _out((nb, nc, 2, SSD_STATE, GROUP_W), F32), _out((t, CONV_CH), F32)],
        scratch_shapes=[pltpu.VMEM((2, SSD_STATE, GROUP_W), F32), pltpu.VMEM((CHUNK, SSD_INNER), F32)],
        compiler_params=_cp(("arbitrary", "arbitrary"), 48),
    )(*_pin(proj, proj, proj, proj, proj, conv_w, conv_b, dt_bias, a_log, dskip_e, g_ssd))


def _out_proj(y_pool, y_ssd, w_out, x, mod3, g_mlp, seq):
    t, d = x.shape
    tm = 512
    tps = seq // tm if seq >= tm else 1
    tm = min(tm, seq)

    def body(yp_ref, ys_ref, w_ref, x_ref, mod_ref, g_ref, h1_ref, o_ref, u2_ref):
        o = _nn(yp_ref[...], w_ref[0:POOL_WIDTH, :]) + _nn(ys_ref[...], w_ref[POOL_WIDTH:, :])
        o_ref[...] = o.astype(BF)
        h1 = x_ref[...] + mod_ref[0, 2:3, :] * o
        h1_ref[...] = h1
        r = lax.rsqrt(jnp.mean(h1 * h1, -1, keepdims=True) + EPS)
        u2_ref[...] = ((h1 * r * g_ref[...]) * (1.0 + mod_ref[0, 4:5, :]) + mod_ref[0, 3:4, :]).astype(BF)

    row = lambda i: (i, 0)
    return pl.pallas_call(
        body, name="out_proj", grid=(t // tm,),
        in_specs=[pl.BlockSpec((tm, POOL_WIDTH), row), pl.BlockSpec((tm, SSD_INNER), row),
                  pl.BlockSpec(w_out.shape, lambda i: (0, 0)), pl.BlockSpec((tm, d), row),
                  pl.BlockSpec((1, N_MOD, d), lambda i: (i // tps, 0, 0)), pl.BlockSpec((1, d), lambda i: (0, 0))],
        out_specs=[pl.BlockSpec((tm, d), row)] * 3,
        out_shape=[_out((t, d), F32), _out((t, d), BF), _out((t, d), BF)],
        compiler_params=_cp(("parallel",), 48))(*_pin(y_pool, y_ssd, w_out, x, mod3, g_mlp))


def _mlp_up(u2, w_up4):
    t, d = u2.shape
    tm = min(1024, t)
    nk, _, cols = w_up4.shape

    def body(u_ref, w_ref, a_ref):
        a_ref[...] = _nn(u_ref[...], w_ref[0]).astype(BF)

    return pl.pallas_call(
        body, name="mlp_up", grid=(nk, t // tm),
        in_specs=[pl.BlockSpec((tm, d), lambda k, i: (i, 0)), pl.BlockSpec((1, d, cols), lambda k, i: (k, 0, 0))],
        out_specs=pl.BlockSpec((tm, cols), lambda k, i: (i, k)),
        out_shape=_out((t, nk * cols), BF),
        compiler_params=_cp(("parallel", "parallel"), 32))(*_pin(u2, w_up4))


def _mlp_down_loss(a_up, w_down, h1, mod3, g_final, target, seq):
    t, d = h1.shape
    nb = t // seq
    tm = min(ROW_TILE, seq)
    tps = seq // tm

    def body(a_ref, w_ref, h1_ref, mod_ref, g_ref, tg_ref, ddn_ref, dh2_ref, sq_ref, gg_ref, dgf_ref):
        i = pl.program_id(0)

        @pl.when(i == 0)
        def _():
            sq_ref[...] = jnp.zeros_like(sq_ref)
            gg_ref[...] = jnp.zeros_like(gg_ref)

        @pl.when(i % tps == 0)
        def _():
            dgf_ref[...] = jnp.zeros_like(dgf_ref)

        gate = mod_ref[0, 5:6, :]
        sq = gg = dgf = 0.0
        for rows in _sub_rows(tm):
            f = jnp.square(jnp.maximum(a_ref[rows, :], 0))
            dn = _nn(f, w_ref[...])
            h2 = h1_ref[rows, :] + gate * dn
            r = lax.rsqrt(jnp.mean(h2 * h2, -1, keepdims=True) + EPS)
            hh = h2 * r
            err = hh * g_ref[...] - tg_ref[rows, :]
            dy = err * (1.0 / d)
            dhat = dy * g_ref[...]
            dh2 = r * (dhat - hh * jnp.mean(dhat * hh, -1, keepdims=True))
            dh2_ref[rows, :] = dh2
            ddn_ref[rows, :] = (dh2 * gate).astype(BF)
            sq = sq + jnp.sum(err * err, 0, keepdims=True)
            gg = gg + jnp.sum(dy * hh, 0, keepdims=True)
            dgf = dgf + jnp.sum(dh2 * dn, 0, keepdims=True)
        sq_ref[...] += sq
        gg_ref[...] += gg
        dgf_ref[0] += dgf

    row = lambda i: (i, 0)
    vec = pl.BlockSpec((1, d), lambda i: (0, 0))
    return pl.pallas_call(
        body, name="mlp_down_loss", grid=(t // tm,),
        in_specs=[pl.BlockSpec((tm, D_FF), row), pl.BlockSpec(w_down.shape, lambda i: (0, 0)), pl.BlockSpec((tm, d), row),
                  pl.BlockSpec((1, N_MOD, d), lambda i: (i // tps, 0, 0)), vec, pl.BlockSpec((tm, d), row)],
        out_specs=[pl.BlockSpec((tm, d), row), pl.BlockSpec((tm, d), row), vec, vec,
                   pl.BlockSpec((1, 1, d), lambda i: (i // tps, 0, 0))],
        out_shape=[_out((t, d), BF), _out((t, d), F32), _out((1, d), F32),
                   _out((1, d), F32), _out((nb, 1, d), F32)],
        compiler_params=_cp(("arbitrary",), 56))(*_pin(a_up, w_down, h1, mod3, g_final, target))


def _tn_matmul(a, b, tk, tn, name, square_relu=False, out3=False):
    t, kdim = a.shape
    ndim = b.shape[1]

    def body(a_ref, b_ref, o_ref):
        av = a_ref[...]
        if square_relu:
            av = jnp.square(jnp.maximum(av, 0))
        res = _tn(av, b_ref[...]).astype(BF)
        if out3:
            o_ref[0] = res
        else:
            o_ref[...] = res

    if out3:
        out_spec = pl.BlockSpec((1, tk, tn), lambda j, i: (j, i, 0))
        out_shape = _out((ndim // tn, kdim, tn), BF)
    else:
        out_spec = pl.BlockSpec((tk, tn), lambda j, i: (i, j))
        out_shape = _out((kdim, ndim), BF)
    return pl.pallas_call(
        body, name=name, grid=(ndim // tn, kdim // tk),
        in_specs=[pl.BlockSpec((t, tk), lambda j, i: (0, i)), pl.BlockSpec((t, tn), lambda j, i: (0, j))],
        out_specs=out_spec, out_shape=out_shape,
        compiler_params=_cp(("parallel", "parallel"), 56))(*_pin(a, b))


def _mlp_down_bwd(d_dn, w_down4, a_up, token):
    t, d = d_dn.shape
    tm = min(1024, t)
    nk, rows, _ = w_down4.shape

    def body(g_ref, w_ref, a_ref, tok_ref, o_ref):
        df = _nt(g_ref[...], w_ref[0])
        o_ref[...] = (df * (2.0 * jnp.maximum(a_ref[...], 0).astype(F32))).astype(BF)

    return pl.pallas_call(
        body, name="mlp_down_bwd", grid=(nk, t // tm),
        in_specs=[pl.BlockSpec((tm, d), lambda k, i: (i, 0)), pl.BlockSpec((1, rows, d), lambda k, i: (k, 0, 0)),
                  pl.BlockSpec((tm, rows), lambda k, i: (i, k)), _token_spec()],
        out_specs=pl.BlockSpec((tm, rows), lambda k, i: (i, k)),
        out_shape=_out((t, nk * rows), BF),
        compiler_params=_cp(("parallel", "parallel"), 32))(*_pin(d_dn, w_down4, a_up, token))


def _mlp_up_bwd(d_a, w_up4, h1, dh2, o, mod3, g_mlp, seq, token):
    t, d = h1.shape
    nb = t // seq
    tm = min(ROW_TILE, seq)
    tps = seq // tm
    nk = w_up4.shape[0]
    cols = w_up4.shape[2]

    def body(da_ref, w_ref, h1_ref, dh2_ref, o_ref, mod_ref, g_ref, tok_ref, dh1_ref, do_ref, acc_ref, gg_ref):
        i = pl.program_id(0)

        @pl.when(i == 0)
        def _():
            gg_ref[...] = jnp.zeros_like(gg_ref)

        @pl.when(i % tps == 0)
        def _():
            acc_ref[...] = jnp.zeros_like(acc_ref)

        gg = a_shift = a_scale = a_gate = 0.0
        for rows in _sub_rows(tm):
            du = _nt(da_ref[rows, 0:cols], w_ref[0])
            for k in range(1, nk):
                du = du + _nt(da_ref[rows, k * cols:(k + 1) * cols], w_ref[k])
            h1 = h1_ref[rows, :]
            r = lax.rsqrt(jnp.mean(h1 * h1, -1, keepdims=True) + EPS)
            hh = h1 * r
            n2 = hh * g_ref[...]
            dn2 = du * (1.0 + mod_ref[0, 4:5, :])
            dhat = dn2 * g_ref[...]
            dh1 = dh2_ref[rows, :] + r * (dhat - hh * jnp.mean(dhat * hh, -1, keepdims=True))
            dh1_ref[rows, :] = dh1
            do_ref[rows, :] = (dh1 * mod_ref[0, 2:3, :]).astype(BF)
            gg = gg + jnp.sum(dn2 * hh, 0, keepdims=True)
            a_shift = a_shift + jnp.sum(du, 0, keepdims=True)
            a_scale = a_scale + jnp.sum(du * n2, 0, keepdims=True)
            a_gate = a_gate + jnp.sum(dh1 * o_ref[rows, :].astype(F32), 0, keepdims=True)
        gg_ref[...] += gg
        acc_ref[0, 0:1, :] += a_shift
        acc_ref[0, 1:2, :] += a_scale
        acc_ref[0, 2:3, :] += a_gate

    row = lambda i: (i, 0)
    vec = pl.BlockSpec((1, d), lambda i: (0, 0))
    return pl.pallas_call(
        body, name="mlp_up_bwd", grid=(t // tm,),
        in_specs=[pl.BlockSpec((tm, D_FF), row), pl.BlockSpec(w_up4.shape, lambda i: (0, 0, 0)), pl.BlockSpec((tm, d), row),
                  pl.BlockSpec((tm, d), row), pl.BlockSpec((tm, d), row),
                  pl.BlockSpec((1, N_MOD, d), lambda i: (i // tps, 0, 0)), vec, _token_spec()],
        out_specs=[pl.BlockSpec((tm, d), row), pl.BlockSpec((tm, d), row),
                   pl.BlockSpec((1, 8, d), lambda i: (i // tps, 0, 0)), vec],
        out_shape=[_out((t, d), F32), _out((t, d), BF),
                   _out((nb, 8, d), F32), _out((1, d), F32)],
        compiler_params=_cp(("arbitrary",), 56))(*_pin(d_a, w_up4, h1, dh2, o, mod3, g_mlp, token))


def _out_proj_bwd(d_o, w_out, token):
    t, d = d_o.shape
    tm = min(512, t)

    def body(g_ref, w_ref, tok_ref, dp_ref, ds_ref):
        gv = g_ref[...]
        dp_ref[...] = _nt(gv, w_ref[0:POOL_WIDTH, :])
        ds_ref[...] = _nt(gv, w_ref[POOL_WIDTH:, :])

    row = lambda i: (i, 0)
    return pl.pallas_call(
        body, name="out_proj_bwd", grid=(t // tm,),
        in_specs=[pl.BlockSpec((tm, d), row), pl.BlockSpec(w_out.shape, lambda i: (0, 0)), _token_spec()],
        out_specs=[pl.BlockSpec((tm, POOL_WIDTH), row), pl.BlockSpec((tm, SSD_INNER), row)],
        out_shape=[_out((t, POOL_WIDTH), F32), _out((t, SSD_INNER), F32)],
        compiler_params=_cp(("parallel",), 32))(*_pin(d_o, w_out, token))


def _pool_bwd(d_ypool, p, w_pool, pool_scale, nb, seq):
    ts = _pool_tile(seq)
    nt = seq // ts
    hb = ts // HALO
    last_block = nb * seq // HALO - 1

    def body(dy_ref, halo_ref, p_ref, wp_ref, ps_ref, du_ref, gw_ref, gs_ref):
        b = pl.program_id(0)
        i = pl.program_id(1)

        @pl.when((b == 0) & (i == 0))
        def _():
            gw_ref[...] = jnp.zeros_like(gw_ref)
            gs_ref[...] = jnp.zeros_like(gs_ref)

        halo = jnp.where(i == nt - 1, 0.0, halo_ref[...])
        dy = dy_ref[...]
        ext = jnp.concatenate([dy, halo], 0)
        tpos = i * ts + _iota((ts + HALO, 1), 0)
        n_ext = ts + HALO
        for g, w in enumerate(POOL_WINDOWS):
            gs = slice(g * POOL_GROUP, (g + 1) * POOL_GROUP)
            wg = wp_ref[g].astype(BF)
            pg = p_ref[:, gs]
            pw = _nn(pg, wg)
            gs_ref[:, gs] += jnp.sum(dy[:, gs] * pw, 0, keepdims=True)
            dpw = (ext[:, gs] * ps_ref[:, gs]).astype(BF)
            gw_ref[g] += _tn(pg, dpw[:ts])
            dp = _nt(dpw, wg)
            cnt = jnp.minimum(tpos + 1, w).astype(F32)
            s = dp / cnt
            sh = 1
            while sh < w:
                s = s + pltpu.roll(s, n_ext - sh, 0)
                sh *= 2
            du_ref[:, gs] = (s[:ts] - dp[:ts]).astype(BF)

    return pl.pallas_call(
        body, name="pool_bwd", grid=(nb, nt),
        in_specs=[pl.BlockSpec((ts, POOL_WIDTH), lambda b, i: (b * nt + i, 0)),
                  pl.BlockSpec((HALO, POOL_WIDTH), lambda b, i: (jnp.minimum((b * nt + i + 1) * hb, last_block), 0)),
                  pl.BlockSpec((ts, POOL_WIDTH), lambda b, i: (b * nt + i, 0)),
                  pl.BlockSpec((4, POOL_GROUP, POOL_GROUP), lambda b, i: (0, 0, 0)),
                  pl.BlockSpec((1, POOL_WIDTH), lambda b, i: (0, 0))],
        out_specs=[pl.BlockSpec((ts, POOL_WIDTH), lambda b, i: (b * nt + i, 0)),
                   pl.BlockSpec((4, POOL_GROUP, POOL_GROUP), lambda b, i: (0, 0, 0)),
                   pl.BlockSpec((1, POOL_WIDTH), lambda b, i: (0, 0))],
        out_shape=[_out((nb * seq, POOL_WIDTH), BF), _out((4, POOL_GROUP, POOL_GROUP), F32),
                   _out((1, POOL_WIDTH), F32)],
        compiler_params=_cp(("arbitrary", "arbitrary"), 32))(*_pin(d_ypool, d_ypool, p, w_pool, pool_scale))


def _ssd_bwd(proj, pre, d_yssd, yssm, h_prev, dt_bias, a_log, dskip_e, g_ssd, nb, seq):
    specs, row, cidx, nc = _ssd_specs(nb, seq, reverse=True)
    specs = specs[2:]

    def body(z0_ref, z1_ref, udt_ref, pre_ref, dys_ref, yssm_ref, hprev_ref,
             dtb_ref, alog_ref, dsk_ref, gs_ref,
             dz_ref, dpre_ref, dudt_ref, ggs_ref, gdsk_ref, ga_ref, gdtb_ref,
             g_ref, dxdt_ref, dyv_ref):
        b = pl.program_id(0)
        c = pl.program_id(1)

        @pl.when(c == 0)
        def _():
            g_ref[...] = jnp.zeros_like(g_ref)

        @pl.when((b == 0) & (c == 0))
        def _():
            ggs_ref[...] = jnp.zeros_like(ggs_ref)
            gdsk_ref[...] = jnp.zeros_like(gdsk_ref)
            ga_ref[...] = jnp.zeros_like(ga_ref)
            gdtb_ref[...] = jnp.zeros_like(gdtb_ref)

        r = _chunk_terms(pre_ref[...], udt_ref[...], dtb_ref[...], alog_ref[...])
        xbc = r["xbc"]
        xs = xbc[:, :SSD_INNER]
        dt_e = r["dt_e"]
        xdt = xs * dt_e
        xdt_b = xdt.astype(BF)
        reduce_m = _head_reduce_matrix(GROUP_W, 8)

        def head_sums(v):
            return _nn(v.astype(BF), reduce_m)

        onehot16 = lambda h: (_iota((1, SSD_HEADS), 1) == h).astype(F32)
        onecol16 = lambda h: (_iota((SSD_HEADS, 1), 0) == h).astype(F32)

        d_acum = jnp.zeros((CHUNK, SSD_HEADS), F32)
        d_acum_t = jnp.zeros((SSD_HEADS, CHUNK), F32)
        d_alast = jnp.zeros((1, SSD_HEADS), F32)
        place8 = lambda g: (_iota((8, SSD_HEADS), 1) == _iota((8, SSD_HEADS), 0) + 8 * g).astype(BF)
        d_b, d_c = [], []
        for g in range(2):
            gs = slice(g * GROUP_W, (g + 1) * GROUP_W)
            zg = (z0_ref if g == 0 else z1_ref)[...]
            sz = _sigmoid(zg)
            silu_z = zg * sz
            ys = yssm_ref[:, gs]
            yg = ys * silu_z
            rg = lax.rsqrt(jnp.mean(yg * yg, -1, keepdims=True) + EPS)
            yh = yg * rg
            dys = dys_ref[:, gs]
            ggs_ref[:, gs] += jnp.sum(dys * yh, 0, keepdims=True)
            dyh = dys * gs_ref[:, gs]
            dyg = rg * (dyh - yh * jnp.mean(dyh * yh, -1, keepdims=True))
            dy = dyg * silu_z
            dz_ref[:, gs] = (dyg * ys * (sz * (1.0 + zg * (1.0 - sz)))).astype(BF)
            gdsk_ref[:, gs] += jnp.sum(dy * xs[:, gs], 0, keepdims=True)
            dyv_ref[:, gs] = dy
            dy_b = dy.astype(BF)

            bg = xbc[:, SSD_INNER + g * SSD_STATE:SSD_INNER + (g + 1) * SSD_STATE].astype(BF)
            cg = xbc[:, SSD_INNER + (2 + g) * SSD_STATE:SSD_INNER + (3 + g) * SSD_STATE].astype(BF)
            scores = _nt(cg, bg)
            hg = hprev_ref[0, 0, g]
            hg_b = hg.astype(BF)
            gg = g_ref[g]
            gg_b = gg.astype(BF)
            e_a = r["e_a"][:, gs]
            d_out = r["d_out"][:, gs]
            c_dec = r["c_dec"][:, gs]
            zc = _nn(cg, hg_b)
            wv = e_a * dy
            wv_b = wv.astype(BF)
            da_g = head_sums(wv * zc)
            dcg = _nt(wv_b, hg_b)
            d_hprev = _tn(cg, wv_b)
            vg = _nn(bg, gg_b)
            dxdt_g = d_out * vg
            dd_out = head_sums(xdt[:, gs] * vg)
            dbg = _nt((xdt[:, gs] * d_out).astype(BF), gg_b)
            dcd = _exact_nn(jnp.sum(gg * hg, 0, keepdims=True), reduce_m)
            d_out8 = jnp.exp(r["acum"][CHUNK - 1:CHUNK, 8 * g:8 * g + 8] - r["acum"][:, 8 * g:8 * g + 8])
            c_dec8 = jnp.exp(r["acum"][CHUNK - 1:CHUNK, 8 * g:8 * g + 8])
            t8 = dd_out * d_out8
            d_alast = d_alast + _exact_nn(jnp.sum(t8, 0, keepdims=True) + dcd * c_dec8, place8(g))
            d_acum = d_acum + _exact_nn(da_g - t8, place8(g))
            dsc = jnp.zeros((CHUNK, CHUNK), F32)
            for hh in range(8):
                h = g * 8 + hh
                hs = slice(h * SSD_HEAD_DIM, (h + 1) * SSD_HEAD_DIM)
                lam = _head_decay(r, h)
                m = scores * lam
                dyh_b = dy_b[:, hh * SSD_HEAD_DIM:(hh + 1) * SSD_HEAD_DIM]
                dm = _nt(dyh_b, xdt_b[:, hs])
                tm_ = dm * m
                d_acum = d_acum + jnp.sum(tm_, 1, keepdims=True) * onehot16(h)
                d_acum_t = d_acum_t + onecol16(h) * jnp.sum(tm_, 0, keepdims=True)
                dsc = dsc + dm * lam
                dxdt_ref[:, hs] = _tn(m.astype(BF), dyh_b) + dxdt_g[:, hh * SSD_HEAD_DIM:(hh + 1) * SSD_HEAD_DIM]
            dsc_b = dsc.astype(BF)
            d_c.append(dcg + _nn(dsc_b, bg))
            d_b.append(dbg + _tn(dsc_b, cg))
            g_ref[g] = d_hprev + c_dec * gg

        eye = (_iota((CHUNK, CHUNK), 0) == _iota((CHUNK, CHUNK), 1)).astype(BF)
        d_acum = d_acum - _exact_nt_left(eye, d_acum_t)
        is_last = (_iota((CHUNK, 1), 0) == CHUNK - 1).astype(F32)
        d_acum = d_acum + is_last * d_alast
        triu = (_iota((CHUNK, CHUNK), 0) <= _iota((CHUNK, CHUNK), 1)).astype(BF)
        d_da = _exact_nn_left(triu, d_acum)
        dt = r["dt"]
        ga_ref[...] += jnp.sum(d_da * dt, 0, keepdims=True)
        dxdt = dxdt_ref[...]
        reduce16 = _head_reduce_matrix(SSD_INNER, SSD_HEADS)
        d_dt = d_da * r["a"] + _nn((dxdt * xs).astype(BF), reduce16)
        d_udt = d_dt * _sigmoid(r["dtp"])
        gdtb_ref[...] += jnp.sum(d_udt, 0, keepdims=True)
        dudt_ref[...] = jnp.zeros_like(dudt_ref)
        dudt_ref[:, 0:SSD_HEADS] = d_udt.astype(BF)
        pre, sg = r["pre"], r["sg"]
        dsilu = sg * (1.0 + pre * (1.0 - sg))
        dpre_ref[:, 0:SSD_INNER] = (dsk_ref[...] * dyv_ref[...] + dxdt * dt_e) * dsilu[:, 0:SSD_INNER]
        for g in range(2):
            bs = slice(SSD_INNER + g * SSD_STATE, SSD_INNER + (g + 1) * SSD_STATE)
            cs = slice(SSD_INNER + (2 + g) * SSD_STATE, SSD_INNER + (3 + g) * SSD_STATE)
            dpre_ref[:, bs] = d_b[g] * dsilu[:, bs]
            dpre_ref[:, cs] = d_c[g] * dsilu[:, cs]

    t = nb * seq
    vec = _const_spec((1, SSD_INNER))
    small = _const_spec((1, SSD_HEADS))
    return pl.pallas_call(
        body, name="ssd_bwd", grid=(nb, nc),
        in_specs=specs + [pl.BlockSpec((CHUNK, CONV_CH), lambda b, c: (row(b, c), 0)),
                          pl.BlockSpec((CHUNK, SSD_INNER), lambda b, c: (row(b, c), 0)),
                          pl.BlockSpec((CHUNK, SSD_INNER), lambda b, c: (row(b, c), 0)),
                          pl.BlockSpec((1, 1, 2, SSD_STATE, GROUP_W), lambda b, c: (b, cidx(c), 0, 0, 0)),
                          small, small, vec, vec],
        out_specs=[pl.BlockSpec((CHUNK, SSD_INNER), lambda b, c: (row(b, c), 0)),
                   pl.BlockSpec((CHUNK, CONV_CH), lambda b, c: (row(b, c), 0)),
                   pl.BlockSpec((CHUNK, 128), lambda b, c: (row(b, c), 0)),
                   vec, vec, small, small],
        out_shape=[_out((t, SSD_INNER), BF), _out((t, CONV_CH), F32),
                   _out((t, 128), BF), _out((1, SSD_INNER), F32),
                   _out((1, SSD_INNER), F32), _out((1, SSD_HEADS), F32),
                   _out((1, SSD_HEADS), F32)],
        scratch_shapes=[pltpu.VMEM((2, SSD_STATE, GROUP_W), F32), pltpu.VMEM((CHUNK, SSD_INNER), F32),
                        pltpu.VMEM((CHUNK, SSD_INNER), F32)],
        compiler_params=_cp(("arbitrary", "arbitrary"), 48),
    )(*_pin(proj, proj, proj, pre, d_yssd, yssm, h_prev, dt_bias, a_log, dskip_e, g_ssd))


def _grad_w_in_t(d_upool, d_z, d_uxbc, d_udt, u1):
    t, d = u1.shape
    tk = 512
    n_z, n_x = SSD_INNER // tk, CONV_CH // tk

    def body(p_ref, z_ref, x_ref, dt_ref, u_ref, o_ref):
        i = pl.program_id(0)

        @pl.when(i == 0)
        def _():
            o_ref[...] = _tn(p_ref[...], u_ref[...]).astype(BF)

        @pl.when((i >= 1) & (i < 1 + n_z))
        def _():
            o_ref[...] = _tn(z_ref[...], u_ref[...]).astype(BF)

        @pl.when((i >= 1 + n_z) & (i < 1 + n_z + n_x))
        def _():
            o_ref[...] = _tn(x_ref[...], u_ref[...]).astype(BF)

        @pl.when(i == 1 + n_z + n_x)
        def _():
            o_ref[0:128, :] = _tn(dt_ref[...], u_ref[...]).astype(BF)

    return pl.pallas_call(
        body, name="grad_w_in", grid=(2 + n_z + n_x,),
        in_specs=[pl.BlockSpec((t, tk), lambda i: (0, 0)),
                  pl.BlockSpec((t, tk), lambda i: (0, jnp.clip(i - 1, 0, n_z - 1))),
                  pl.BlockSpec((t, tk), lambda i: (0, jnp.clip(i - 1 - n_z, 0, n_x - 1))),
                  pl.BlockSpec((t, 128), lambda i: (0, 0)), pl.BlockSpec((t, d), lambda i: (0, 0))],
        out_specs=pl.BlockSpec((tk, d), lambda i: (i, 0)),
        out_shape=_out((IN_PAD, d), BF),
        compiler_params=_cp(("parallel",), 56))(*_pin(d_upool, d_z, d_uxbc, d_udt, u1))


def _conv_bwd(d_pre, proj, conv_w, nb, seq):
    ts = min(256, seq)
    nt = seq // ts
    hb = ts // CONV_HALO
    last_block = nb * seq // CONV_HALO - 1
    n_ext = CHUNK + CONV_HALO

    def body(dp_ref, dnext_ref, u_ref, cw_ref, du_ref, gw_ref, gb_ref):
        b = pl.program_id(0)
        i = pl.program_id(1)

        @pl.when((b == 0) & (i == 0))
        def _():
            gw_ref[...] = jnp.zeros_like(gw_ref)
            gb_ref[...] = jnp.zeros_like(gb_ref)

        for c0 in range(0, CONV_CH, 128):
            cs = slice(c0, c0 + 128)
            cw = cw_ref[:, cs]
            gw = [0.0] * 4
            gb = 0.0
            for r0 in range(0, ts, CHUNK):
                dp = dp_ref[r0:r0 + CHUNK, cs]
                u = u_ref[r0:r0 + CHUNK, cs]
                if r0 + CHUNK < ts:
                    below = dp_ref[r0 + CHUNK:r0 + CHUNK + CONV_HALO, cs]
                else:
                    below = jnp.where(i == nt - 1, 0.0, dnext_ref[:, cs])
                ext_d = jnp.concatenate([dp, below], 0)
                du = dp * cw[3:4]
                gw[3] = gw[3] + jnp.sum(dp * u, 0, keepdims=True)
                for k in (2, 1, 0):
                    shifted = pltpu.roll(ext_d, n_ext - (3 - k), 0)[:CHUNK]
                    du = du + shifted * cw[k:k + 1]
                    gw[k] = gw[k] + jnp.sum(shifted * u, 0, keepdims=True)
                gb = gb + jnp.sum(dp, 0, keepdims=True)
                du_ref[r0:r0 + CHUNK, cs] = du.astype(BF)
            for k in range(4):
                gw_ref[k:k + 1, cs] += gw[k]
            gb_ref[:, cs] += gb

    return pl.pallas_call(
        body, name="conv_bwd", grid=(nb, nt),
        in_specs=[pl.BlockSpec((ts, CONV_CH), lambda b, i: (b * nt + i, 0)),
                  pl.BlockSpec((CONV_HALO, CONV_CH), lambda b, i: (jnp.minimum((b * nt + i + 1) * hb, last_block), 0)),
                  pl.BlockSpec((ts, CONV_CH), lambda b, i: (b * nt + i, 1)),
                  pl.BlockSpec((4, CONV_CH), lambda b, i: (0, 0))],
        out_specs=[pl.BlockSpec((ts, CONV_CH), lambda b, i: (b * nt + i, 0)),
                   pl.BlockSpec((8, CONV_CH), lambda b, i: (0, 0)), pl.BlockSpec((1, CONV_CH), lambda b, i: (0, 0))],
        out_shape=[_out((nb * seq, CONV_CH), BF), _out((8, CONV_CH), F32),
                   _out((1, CONV_CH), F32)],
        compiler_params=_cp(("arbitrary", "arbitrary"), 48))(*_pin(d_pre, d_pre, proj, conv_w))


def _in_proj_bwd(d_parts, w_in_t, x, dh1, mod3, g_mix, seq, token):
    t, d = x.shape
    nb = t // seq
    tm = min(ROW_TILE, seq)
    tps = seq // tm

    widths = [p.shape[1] for p in d_parts]

    def body(d0_ref, d1_ref, d2_ref, d3_ref, w_ref, x_ref, dh1_ref, mod_ref, g_ref, tok_ref, gx_ref, acc_ref, gg_ref):
        i = pl.program_id(0)

        @pl.when(i == 0)
        def _():
            gg_ref[...] = jnp.zeros_like(gg_ref)

        @pl.when(i % tps == 0)
        def _():
            acc_ref[...] = jnp.zeros_like(acc_ref)

        gg = a_shift = a_scale = 0.0
        for rows in _sub_rows(tm):
            du = None
            off = 0
            for p_ref, wd in zip((d0_ref, d1_ref, d2_ref, d3_ref), widths):
                part = _nn(p_ref[rows, :], w_ref[off:off + wd, :])
                du = part if du is None else du + part
                off += wd
            xv = x_ref[rows, :]
            r = lax.rsqrt(jnp.mean(xv * xv, -1, keepdims=True) + EPS)
            hh = xv * r
            n1 = hh * g_ref[...]
            dn1 = du * (1.0 + mod_ref[0, 1:2, :])
            dhat = dn1 * g_ref[...]
            gx_ref[rows, :] = dh1_ref[rows, :] + r * (dhat - hh * jnp.mean(dhat * hh, -1, keepdims=True))
            gg = gg + jnp.sum(dn1 * hh, 0, keepdims=True)
            a_shift = a_shift + jnp.sum(du, 0, keepdims=True)
            a_scale = a_scale + jnp.sum(du * n1, 0, keepdims=True)
        gg_ref[...] += gg
        acc_ref[0, 0:1, :] += a_shift
        acc_ref[0, 1:2, :] += a_scale

    row = lambda i: (i, 0)
    vec = pl.BlockSpec((1, d), lambda i: (0, 0))
    return pl.pallas_call(
        body, name="in_proj_bwd", grid=(t // tm,),
        in_specs=[pl.BlockSpec((tm, wd), row) for wd in widths] +
                 [pl.BlockSpec(w_in_t.shape, lambda i: (0, 0)), pl.BlockSpec((tm, d), row),
                  pl.BlockSpec((tm, d), row), pl.BlockSpec((1, N_MOD, d), lambda i: (i // tps, 0, 0)), vec, _token_spec()],
        out_specs=[pl.BlockSpec((tm, d), row), pl.BlockSpec((1, 8, d), lambda i: (i // tps, 0, 0)), vec],
        out_shape=[_out((t, d), F32), _out((nb, 8, d), F32),
                   _out((1, d), F32)],
        compiler_params=_cp(("arbitrary",), 48))(*_pin(*d_parts, w_in_t, x, dh1, mod3, g_mix, token))


_VEC_LAYOUT = (("g_mix", 1024), ("conv_b", 1536), ("g_ssd", 1024), ("pool_scale", 512), ("g_mlp", 1024),
               ("g_final", 1024), ("dt_bias", 128), ("a_log", 128), ("d_skip_lanes", 1024), ("sq_err", 1024))
_VEC_OFFSET = {}
_off = 0
for _name, _n in _VEC_LAYOUT:
    _VEC_OFFSET[_name] = _off
    _off += _n
_VEC_LANES = _off
_SMALL_PARAMS = ("b_ada", "g_mix", "conv_w", "conv_b", "dt_bias", "a_log", "d_skip", "g_ssd", "w_pool", "pool_scale",
                 "g_mlp", "g_final")


def _pack_vec(parts):
    cols = []
    for name, n in _VEC_LAYOUT:
        v = parts[name]
        if v.shape[1] < n:
            v = jnp.pad(v, ((0, 0), (0, n - v.shape[1])))
        cols.append(v)
    return jnp.concatenate(cols, 1)


def _small_adam(vec_all, wpool_all, convw_all, dmod_all, params):
    names = _SMALL_PARAMS
    nin = 4 + 3 * len(names)

    def body(*refs):
        vec_ref, wp_ref, cw_ref, dm_ref = refs[:4]
        prm = {n: refs[4 + 3 * i:7 + 3 * i] for i, n in enumerate(names)}
        loss_ref = refs[nin]
        outs = {n: refs[nin + 1 + 4 * i:nin + 5 + 4 * i] for i, n in enumerate(names)}
        vsum = vec_ref[0]
        for s in range(1, N_DEV):
            vsum = vsum + vec_ref[s]

        def lanes(name, n):
            off = _VEC_OFFSET[name]
            return vsum[:, off:off + n]

        grads = {n: lanes(n, prm[n][0].shape[1]) for n in ("g_mix", "conv_b", "g_ssd", "pool_scale", "g_mlp", "g_final", "dt_bias")}
        grads["a_log"] = lanes("a_log", SSD_HEADS) * (-jnp.exp(prm["a_log"][0][...]))
        per_lane = jnp.broadcast_to(lanes("d_skip_lanes", SSD_INNER), (8, SSD_INNER))
        grads["d_skip"] = _exact_nn(per_lane, _head_reduce_matrix(SSD_INNER, SSD_HEADS))[0:1]
        gwp = wp_ref[0].astype(F32)
        gcw = cw_ref[0]
        gb = jnp.sum(dm_ref[0], 0, keepdims=True)
        for s in range(1, N_DEV):
            gwp = gwp + wp_ref[s].astype(F32)
            gcw = gcw + cw_ref[s]
            gb = gb + jnp.sum(dm_ref[s], 0, keepdims=True)
        grads["w_pool"] = gwp
        grads["conv_w"] = gcw[0:4]
        grads["b_ada"] = gb
        total = jnp.sum(lanes("sq_err", D_MODEL), 1, keepdims=True) * (0.5 / D_MODEL)
        loss_ref[...] = jnp.broadcast_to(total, loss_ref.shape)
        for n in names:
            w_ref, m_ref, v_ref = prm[n]
            g = grads[n]
            d, m2, v2 = _adam_math(w_ref[...], g, m_ref[...], v_ref[...])
            g_ref, d_ref, m2_ref, v2_ref = outs[n]
            g_ref[...] = g
            d_ref[...] = d
            m2_ref[...] = m2
            v2_ref[...] = v2

    flat = [vec_all, wpool_all, convw_all, dmod_all]
    out_shape = [jax.ShapeDtypeStruct((1, 128), F32)]
    for n in names:
        flat += list(params[n])
        out_shape += [jax.ShapeDtypeStruct(params[n][0].shape, F32)] * 4
    vm = pl.BlockSpec(memory_space=pltpu.VMEM)
    res = pl.pallas_call(body, name="small_adam", out_shape=out_shape, in_specs=[vm] * len(flat),
                         out_specs=[vm] * len(out_shape), compiler_params=_cp(vmem_mb=48))(*flat)
    return res[0], {n: res[1 + 4 * i:5 + 4 * i] for i, n in enumerate(names)}


_WEIGHTS = ("w_ada", "b_ada", "g_mix", "w_in", "conv_w", "conv_b", "dt_bias", "a_log", "d_skip", "g_ssd", "w_pool",
            "pool_scale", "w_out", "g_mlp", "w_up", "w_down", "g_final")


def _local_step(x2, tg2, mod3, seq, w_in_t, first_token, weights_later, start_reduce, conv_w_full, sp):
    t, d = x2.shape
    nb = t // seq
    dskip_e = jnp.repeat(sp["d_skip"], SSD_HEAD_DIM, axis=1)
    proj, u1 = _in_proj(x2, mod3, sp["g_mix"], w_in_t, seq, first_token)
    y_pool, p = _pool_fwd(proj, sp["w_pool"], sp["pool_scale"], nb, seq)
    y_ssd, yssm, h_prev, pre = _ssd_fwd(proj, conv_w_full, sp["conv_b"], sp["dt_bias"], sp["a_log"], dskip_e, sp["g_ssd"], nb, seq)
    w_out_f, w_up4, w_down4 = weights_later(y_ssd)
    w_down_f = w_down4.reshape(D_FF, d)
    h1, o, u2 = _out_proj(y_pool, y_ssd, w_out_f, x2, mod3, sp["g_mlp"], seq)
    a_up = _mlp_up(u2, w_up4)
    d_dn, dh2, sq, gg_final, d_gf = _mlp_down_loss(a_up, w_down_f, h1, mod3, sp["g_final"], tg2, seq)

    gw_down = _tn_matmul(a_up, d_dn, 512, d, "grad_w_down", square_relu=True)
    tok = start_reduce("w_down", gw_down.reshape(N_CHIPS, D_FF // N_CHIPS, d))
    d_a = _mlp_down_bwd(d_dn, w_down4, a_up, tok)
    gw_up4 = _tn_matmul(u2, d_a, 512, d, "grad_w_up", out3=True)
    tok = start_reduce("w_up", gw_up4)
    dh1, d_o, accf, gg_mlp = _mlp_up_bwd(d_a, w_up4, h1, dh2, o, mod3, sp["g_mlp"], seq, tok)
    gw_out_pool = _tn_matmul(y_pool, d_o, 512, d, "grad_w_out_pool")
    gw_out_ssd = _tn_matmul(y_ssd, d_o, 512, d, "grad_w_out_ssd")
    gw_out = jnp.concatenate([gw_out_pool, gw_out_ssd], 0)
    tok = start_reduce("w_out", gw_out.reshape(N_CHIPS, gw_out.shape[0] // N_CHIPS, d))
    d_ypool, d_yssd = _out_proj_bwd(d_o, w_out_f, tok)
    d_upool, gw_pool, g_ps = _pool_bwd(d_ypool, p, sp["w_pool"], sp["pool_scale"], nb, seq)
    d_z, d_pre, d_udt, gg_ssd, gdsk, ga, gdtb = _ssd_bwd(proj, pre, d_yssd, yssm, h_prev, sp["dt_bias"], sp["a_log"],
                                                        dskip_e, sp["g_ssd"], nb, seq)
    d_uxbc, gconvw, gconvb = _conv_bwd(d_pre, proj, conv_w_full, nb, seq)
    gw_in_t = _grad_w_in_t(d_upool, d_z, d_uxbc, d_udt, u1)
    tok = start_reduce("w_in", gw_in_t[:IN_WIDTH].reshape(N_CHIPS, IN_WIDTH // N_CHIPS, d))
    gx, accm, gg_mix = _in_proj_bwd([d_upool, d_z, d_uxbc, d_udt], w_in_t, x2, dh1, mod3, sp["g_mix"], seq, tok)

    d_mod = jnp.concatenate([accm[:, 0], accm[:, 1], accf[:, 2], accf[:, 0], accf[:, 1], d_gf[:, 0]], 1)
    vec = _pack_vec({"g_mix": gg_mix, "conv_b": gconvb, "g_ssd": gg_ssd, "pool_scale": g_ps, "g_mlp": gg_mlp,
                     "g_final": gg_final, "dt_bias": gdtb, "a_log": ga, "d_skip_lanes": gdsk, "sq_err": sq})
    return gx, d_mod, vec, gw_pool, gconvw


def kernel(x, c, w_ada, b_ada, g_mix, w_in, conv_w, conv_b, dt_bias, a_log, d_skip, g_ssd, w_pool, pool_scale, w_out, g_mlp, w_up, w_down, g_final, loss_target, m_w_ada, m_b_ada, m_g_mix, m_w_in, m_conv_w, m_conv_b, m_dt_bias, m_a_log, m_d_skip, m_g_ssd, m_w_pool, m_pool_scale, m_w_out, m_g_mlp, m_w_up, m_w_down, m_g_final, v_w_ada, v_b_ada, v_g_mix, v_w_in, v_conv_w, v_conv_b, v_dt_bias, v_a_log, v_d_skip, v_g_ssd, v_w_pool, v_pool_scale, v_w_out, v_g_mlp, v_w_up, v_w_down, v_g_final):
    nb, seq, d = x.shape
    t = nb * seq
    xi, yi, ci = _mesh_pos()
    chip = 2 * xi + yi
    me = 4 * xi + 2 * yi + ci
    ada_cols = w_ada.shape[2]
    conv_cols = conv_w.shape[2]
    in_cols = w_in.shape[2]
    w_in_s, m_w_in_s, v_w_in_s = w_in[0].T, m_w_in[0].T, v_w_in[0].T

    w_in_b = w_in_s.astype(BF)
    i_send, i_recv, i_src, i_land, in_token = _ici_start(
        [w_in_b], [jax.ShapeDtypeStruct((N_CHIPS,) + w_in_b.shape, BF)], _gather_sent, _gather_landing, "gather_start_w_in",
        after=w_in_b)

    c8, convw8 = _all_gather_small([c, conv_w[0]], token=in_token)
    c_all = c8.reshape(N_DEV * nb, d)
    conv_w_full = convw8[0::2].transpose(1, 0, 2).reshape(4, N_CHIPS * conv_cols)
    b_shard = lax.dynamic_slice(b_ada, (0, chip * ada_cols), (1, ada_cols))
    mod_part, c_act = _ada_mod(c_all, w_ada[0], b_shard)
    (mod8,) = _all_gather_small([mod_part])
    mod_all = mod8[0::2].transpose(1, 0, 2).reshape(N_DEV * nb, N_CHIPS * ada_cols)
    mod3 = lax.dynamic_slice(mod_all, (nb * me, 0), (nb, N_CHIPS * ada_cols)).reshape(nb, N_MOD, d)

    in_shard, in_land = _ici_wait(i_send, i_recv, i_src, i_land, mod8, _gather_sent, _gather_landing, "gather_wait_w_in")
    (w_in4,) = _gather_finish(in_land, in_shard)
    w_in_t = jnp.pad(w_in4.reshape(N_CHIPS * in_cols, d), ((0, IN_PAD - N_CHIPS * in_cols), (0, 0)))
    later = [w_out[0].astype(BF), w_up[0].astype(BF), w_down[0].astype(BF)]
    g_send, g_recv, g_src, g_land, first_token = _ici_start(
        later, [jax.ShapeDtypeStruct((N_CHIPS,) + s.shape, BF) for s in later], _gather_sent, _gather_landing, "gather_start",
        after=w_in4)

    def weights_later(after):
        shards, lands = _ici_wait(g_send, g_recv, g_src, g_land, after, _gather_sent, _gather_landing, "gather_wait")
        w_out4, w_up4, w_down4 = _gather_finish(lands, shards)
        return w_out4.reshape(N_CHIPS * w_out.shape[1], d), w_up4, w_down4

    pending = {}

    def start_reduce(name, grad4):
        pending[name] = _reduce_start(grad4, "reduce_start_" + name)
        return pending[name][4]

    sp = dict(g_mix=g_mix, conv_b=conv_b, dt_bias=dt_bias, a_log=a_log, d_skip=d_skip, g_ssd=g_ssd,
              w_pool=w_pool[0], pool_scale=pool_scale, g_mlp=g_mlp, g_final=g_final.reshape(1, d))
    gx, d_mod, vec, gw_pool, gconvw = _local_step(
        x.reshape(t, d), loss_target.reshape(t, d), mod3, seq, w_in_t, first_token, weights_later, start_reduce, conv_w_full, sp)

    pos = jnp.stack([ci, chip, me]).astype(jnp.int32)
    halves = []
    for name in ("w_in", "w_out", "w_up", "w_down"):
        r_send, r_recv, r_src, r_land, _ = pending[name]
        own, recv = _reduce_wait(r_send, r_recv, r_src, r_land, gx, "reduce_wait_" + name)
        halves.append(_sum_eight(recv, own, pos))
    g_in, g_out, g_up, g_down = _halves_exchange(halves)

    vec8, wpool8, convw8g, dmod8 = _all_gather_small(
        [vec, gw_pool.reshape(4 * POOL_GROUP, POOL_GROUP).astype(BF), gconvw, d_mod])
    convw8s = lax.dynamic_slice(convw8g, (0, 0, chip * conv_cols), (N_DEV, 8, conv_cols))
    m_in = dict(b_ada=m_b_ada, g_mix=m_g_mix, conv_w=m_conv_w[0], conv_b=m_conv_b, dt_bias=m_dt_bias, a_log=m_a_log,
                d_skip=m_d_skip, g_ssd=m_g_ssd, w_pool=m_w_pool.reshape(4 * POOL_GROUP, POOL_GROUP), pool_scale=m_pool_scale,
                g_mlp=m_g_mlp, g_final=m_g_final.reshape(1, d))
    v_in = dict(b_ada=v_b_ada, g_mix=v_g_mix, conv_w=v_conv_w[0], conv_b=v_conv_b, dt_bias=v_dt_bias, a_log=v_a_log,
                d_skip=v_d_skip, g_ssd=v_g_ssd, w_pool=v_w_pool.reshape(4 * POOL_GROUP, POOL_GROUP), pool_scale=v_pool_scale,
                g_mlp=v_g_mlp, g_final=v_g_final.reshape(1, d))
    w_small = dict(sp, b_ada=b_ada, conv_w=conv_w[0], w_pool=w_pool.reshape(4 * POOL_GROUP, POOL_GROUP))
    loss_row, small = _small_adam(vec8, wpool8, convw8s, dmod8, {n: (w_small[n], m_in[n], v_in[n]) for n in _SMALL_PARAMS})

    dmod_all = dmod8.reshape(N_DEV * nb, N_CHIPS * ada_cols)
    dmod_cols = lax.dynamic_slice(dmod_all, (0, chip * ada_cols), (N_DEV * nb, ada_cols))
    res = {n: tuple(r.reshape(w.shape) for r in small[n])
           for n, w in (("b_ada", b_ada), ("g_mix", g_mix), ("conv_w", conv_w), ("conv_b", conv_b), ("dt_bias", dt_bias),
                        ("a_log", a_log), ("d_skip", d_skip), ("g_ssd", g_ssd), ("w_pool", w_pool), ("pool_scale", pool_scale),
                        ("g_mlp", g_mlp), ("g_final", g_final))}
    g_ada, d_ada, m_ada, v_ada = _adam_ada(c_act.T.astype(BF), dmod_cols, w_ada[0], m_w_ada[0], v_w_ada[0])
    res["w_ada"] = (g_ada[None], d_ada[None], m_ada[None], v_ada[None])
    dl, m2, v2 = _adam_big(g_in, w_in_s, m_w_in_s, v_w_in_s)
    res["w_in"] = (g_in.T[None], dl.T[None], m2.T[None], v2.T[None])
    for n, g, w, m, v in (("w_out", g_out, w_out, m_w_out, v_w_out), ("w_up", g_up, w_up, m_w_up, v_w_up),
                          ("w_down", g_down, w_down, m_w_down, v_w_down)):
        dl, m2, v2 = _adam_big(g, w[0], m[0], v[0])
        res[n] = (g[None], dl[None], m2[None], v2[None])

    loss = loss_row[0, 0]
    return (loss, gx.reshape(nb, seq, d), *[res[n][0] for n in _WEIGHTS], *[res[n][1] for n in _WEIGHTS],
            *[res[n][2] for n in _WEIGHTS], *[res[n][3] for n in _WEIGHTS])
```

```python
import functools

import jax
import jax.numpy as jnp
from jax import lax
from jax.experimental import pallas as pl
from jax.experimental.pallas import tpu as pltpu

F32 = jnp.float32
BF = jnp.bfloat16
MESH = pl.DeviceIdType.MESH

EPS = 1e-5
D_MODEL = 1024
POOL_WIDTH = 512
POOL_WINDOWS = (2, 4, 8, 16)
POOL_GROUP = 128
SSD_INNER = 1024
SSD_HEADS = 16
SSD_HEAD_DIM = 64
SSD_STATE = 128
GROUP_W = 512
CHUNK = 128
CONV_CH = 1536
OFF_Z = 512
OFF_XBC = 1536
OFF_DT = 3072
IN_WIDTH = 3088
IN_PAD = 3200
D_FF = 4096
N_MOD = 6
N_CHIPS = 4
N_DEV = 8
HALO = 16
CONV_HALO = 8

ADAM_LR = 0.001
ADAM_B1 = 0.9
ADAM_B2 = 0.999
ADAM_EPS = 1e-08
ADAM_WD = 0.01
ADAM_STEP = 10

VMEM_BYTES_V7X = 64 * 1024 * 1024


def _cp(semantics=None, vmem_mb=48, **kw):
    args = dict(vmem_limit_bytes=vmem_mb * 1024 * 1024, **kw)
    if semantics is not None:
        args["dimension_semantics"] = semantics
    return pltpu.CompilerParams(**args)


def _out(shape, dtype):
    return pltpu.HBM(shape, dtype)


def _pin(*arrays):
    return [pltpu.with_memory_space_constraint(a, pltpu.HBM) for a in arrays]


def _nn(a, b):
    return jnp.dot(a, b, preferred_element_type=F32)


def _nt(a, b):
    return lax.dot_general(a, b, (((1,), (1,)), ((), ())), preferred_element_type=F32)


def _tn(a, b):
    return lax.dot_general(a, b, (((0,), (0,)), ((), ())), preferred_element_type=F32)


def _split3(v):
    hi = v.astype(BF)
    r1 = v - hi.astype(F32)
    mid = r1.astype(BF)
    lo = (r1 - mid.astype(F32)).astype(BF)
    return hi, mid, lo


def _exact_nn(v, m01):
    hi, mid, lo = _split3(v)
    return _nn(hi, m01) + _nn(mid, m01) + _nn(lo, m01)


def _exact_nn_left(m01, v):
    hi, mid, lo = _split3(v)
    return _nn(m01, hi) + _nn(m01, mid) + _nn(m01, lo)


def _exact_nt_left(m01, v):
    hi, mid, lo = _split3(v)
    return _nt(m01, hi) + _nt(m01, mid) + _nt(m01, lo)


def _sigmoid(v):
    return 1.0 / (1.0 + jnp.exp(-v))


def _iota(shape, dim):
    return lax.broadcasted_iota(jnp.int32, shape, dim)


def _head_expand_matrix(heads, width):
    return (_iota((heads, width), 1) // SSD_HEAD_DIM == _iota((heads, width), 0)).astype(BF)


def _head_reduce_matrix(width, heads):
    return (_iota((width, heads), 0) // SSD_HEAD_DIM == _iota((width, heads), 1)).astype(BF)


def _mesh_pos():
    return lax.axis_index("x"), lax.axis_index("y"), lax.axis_index("c")


def _flip(v, bit):
    return v + bit - 2 * bit * v


def _all_gather_small(arrays, token=None):
    n = len(arrays)
    extra = [] if token is None else [token]

    def body(*refs):
        in_refs, out_refs = refs[:n], refs[n + len(extra):2 * n + len(extra)]
        send_sems, recv_sems, local_sems = refs[2 * n + len(extra):]
        x, y, c = _mesh_pos()
        me = 4 * x + 2 * y + c
        local = []
        for a in range(n):
            cp = pltpu.make_async_copy(in_refs[a], out_refs[a].at[me], local_sems.at[a])
            cp.start()
            local.append(cp)
        sends = []
        for k in range(1, N_DEV):
            peer = (_flip(x, (k >> 2) & 1), _flip(y, (k >> 1) & 1), _flip(c, k & 1))
            for a in range(n):
                cp = pltpu.make_async_remote_copy(
                    src_ref=in_refs[a], dst_ref=out_refs[a].at[me],
                    send_sem=send_sems.at[a, k], recv_sem=recv_sems.at[a, k],
                    device_id=peer, device_id_type=MESH)
                cp.start()
                sends.append(cp)
        for k in range(1, N_DEV):
            px, py, pc = _flip(x, (k >> 2) & 1), _flip(y, (k >> 1) & 1), _flip(c, k & 1)
            src = 4 * px + 2 * py + pc
            for a in range(n):
                pltpu.make_async_remote_copy(
                    src_ref=in_refs[a], dst_ref=out_refs[a].at[src],
                    send_sem=send_sems.at[a, k], recv_sem=recv_sems.at[a, k],
                    device_id=(px, py, pc), device_id_type=MESH).wait_recv()
        for cp in sends:
            cp.wait_send()
        for cp in local:
            cp.wait()

    vm = pl.BlockSpec(memory_space=pltpu.VMEM)
    return pl.pallas_call(
        body, name="all_gather_small",
        out_shape=[jax.ShapeDtypeStruct((N_DEV,) + a.shape, a.dtype) for a in arrays],
        in_specs=[vm] * (n + len(extra)), out_specs=[vm] * n,
        scratch_shapes=[pltpu.SemaphoreType.DMA((n, N_DEV)), pltpu.SemaphoreType.DMA((n, N_DEV)),
                        pltpu.SemaphoreType.DMA((n,))],
        compiler_params=_cp(vmem_mb=32),
    )(*arrays, *extra)


def _weight_all_gather(shards):
    n = len(shards)
    any_spec = _HBM

    def body(*refs):
        in_refs, out_refs = refs[:n], refs[n:2 * n]
        send_sems, recv_sems, fsend_sems, frecv_sems, local_sems = refs[2 * n:]
        x, y, c = _mesh_pos()
        chip = 2 * x + y
        local, sends = [], []
        for a in range(n):
            cp = pltpu.make_async_copy(in_refs[a], out_refs[a].at[chip], local_sems.at[a])
            cp.start()
            local.append(cp)

        def half(a, which):
            hc = shards[a].shape[1] // 2
            return pl.ds(pl.multiple_of(which * hc, 128), hc)

        for j in range(1, N_CHIPS):
            px, py = _flip(x, (j >> 1) & 1), _flip(y, j & 1)
            for a in range(n):
                cp = pltpu.make_async_remote_copy(
                    src_ref=in_refs[a].at[:, half(a, c)], dst_ref=out_refs[a].at[chip, :, half(a, c)],
                    send_sem=send_sems.at[a, j], recv_sem=recv_sems.at[a, j],
                    device_id=(px, py, c), device_id_type=MESH)
                cp.start()
                sends.append(cp)
        for j in range(1, N_CHIPS):
            px, py = _flip(x, (j >> 1) & 1), _flip(y, j & 1)
            src = 2 * px + py
            for a in range(n):
                landed = out_refs[a].at[src, :, half(a, c)]
                pltpu.make_async_remote_copy(
                    src_ref=landed, dst_ref=landed, send_sem=send_sems.at[a, j], recv_sem=recv_sems.at[a, j],
                    device_id=(px, py, c), device_id_type=MESH).wait_recv()
                cp = pltpu.make_async_remote_copy(
                    src_ref=landed, dst_ref=landed, send_sem=fsend_sems.at[a, j], recv_sem=frecv_sems.at[a, j],
                    device_id=(x, y, 1 - c), device_id_type=MESH)
                cp.start()
                sends.append(cp)
        for j in range(1, N_CHIPS):
            px, py = _flip(x, (j >> 1) & 1), _flip(y, j & 1)
            src = 2 * px + py
            for a in range(n):
                other = out_refs[a].at[src, :, half(a, 1 - c)]
                pltpu.make_async_remote_copy(
                    src_ref=other, dst_ref=other, send_sem=fsend_sems.at[a, j], recv_sem=frecv_sems.at[a, j],
                    device_id=(x, y, 1 - c), device_id_type=MESH).wait_recv()
        for cp in sends:
            cp.wait_send()
        for cp in local:
            cp.wait()

    return pl.pallas_call(
        body, name="weight_all_gather",
        out_shape=[_out((N_CHIPS,) + s.shape, s.dtype) for s in shards],
        in_specs=[any_spec] * n, out_specs=[any_spec] * n,
        scratch_shapes=[pltpu.SemaphoreType.DMA((n, N_CHIPS))] * 4 + [pltpu.SemaphoreType.DMA((n,))],
        compiler_params=_cp(vmem_mb=16),
    )(*shards)


_HBM = pl.BlockSpec(memory_space=pltpu.HBM)
_SEM = pl.BlockSpec(memory_space=pltpu.SEMAPHORE)
_DATAFLOW = pltpu.SideEffectType.DATAFLOW_SIDE_EFFECTING


def _peer_chip(x, y, j):
    return _flip(x, (j >> 1) & 1), _flip(y, j & 1)


def _ici_start(srcs, land_shapes, sent, landing, name, after):
    n = len(srcs)

    def body(*refs):
        src_refs, land_refs = refs[:n], refs[n:2 * n]
        send_sems, recv_sems = refs[2 * n + 1], refs[2 * n + 2]
        token = refs[-1]
        x, y, c = _mesh_pos()
        for j in range(1, N_CHIPS):
            px, py = _peer_chip(x, y, j)
            for a in range(n):
                pltpu.make_async_remote_copy(
                    src_ref=sent(src_refs[a], c, 2 * px + py), dst_ref=landing(land_refs[a], c, 2 * x + y),
                    send_sem=send_sems.at[a * (N_CHIPS - 1) + j - 1], recv_sem=recv_sems.at[a * (N_CHIPS - 1) + j - 1],
                    device_id=(px, py, c), device_id_type=MESH).start()
        token[...] = jnp.zeros_like(token)

    sems = pltpu.SemaphoreType.DMA((n * (N_CHIPS - 1),))
    lands = [pltpu.with_memory_space_constraint(lax.empty(s.shape, s.dtype), pltpu.HBM) for s in land_shapes]
    outs = pl.pallas_call(
        body, name=name,
        out_shape=(sems, sems, *[pltpu.HBM(s.shape, s.dtype) for s in srcs],
                   *[pltpu.HBM(s.shape, s.dtype) for s in land_shapes], jax.ShapeDtypeStruct((8, 128), F32)),
        in_specs=[_HBM] * (2 * n + 1), out_specs=[_SEM, _SEM] + [_HBM] * (2 * n) + [pl.BlockSpec(memory_space=pltpu.VMEM)],
        input_output_aliases={i: 2 + i for i in range(2 * n)},
        compiler_params=pltpu.CompilerParams(has_side_effects=_DATAFLOW),
    )(*_pin(*srcs), *lands, *_pin(after))
    return outs[0], outs[1], outs[2:2 + n], outs[2 + n:2 + 2 * n], outs[-1]


def _ici_wait(send_sems, recv_sems, src_thru, land_thru, after, sent, landing, name):
    n = len(src_thru)

    def body(*refs):
        src_refs, land_refs = refs[:n], refs[n:2 * n]
        send_sems, recv_sems = refs[2 * n], refs[2 * n + 1]
        x, y, c = _mesh_pos()
        for j in range(1, N_CHIPS):
            px, py = _peer_chip(x, y, j)
            for a in range(n):
                cp = pltpu.make_async_remote_copy(
                    src_ref=sent(src_refs[a], c, 2 * px + py), dst_ref=landing(land_refs[a], c, 2 * px + py),
                    send_sem=send_sems.at[a * (N_CHIPS - 1) + j - 1], recv_sem=recv_sems.at[a * (N_CHIPS - 1) + j - 1],
                    device_id=(px, py, c), device_id_type=MESH)
                cp.wait_send()
                cp.wait_recv()

    outs = pl.pallas_call(
        body, name=name,
        out_shape=tuple(pltpu.HBM(s.shape, s.dtype) for s in (*src_thru, *land_thru)),
        in_specs=[_HBM] * (2 * n) + [_SEM, _SEM, _HBM], out_specs=[_HBM] * (2 * n),
        input_output_aliases={i: i for i in range(2 * n)},
        compiler_params=pltpu.CompilerParams(has_side_effects=_DATAFLOW),
    )(*src_thru, *land_thru, send_sems, recv_sems, *_pin(after))
    return outs[:n], outs[n:]


def _col_half(ref, which, lead=()):
    hc = ref.shape[-1] // 2
    return ref.at[(*lead, slice(None), pl.ds(pl.multiple_of(which * hc, 128), hc))]


def _gather_sent(ref, c, dst_chip):
    return _col_half(ref, c)


def _gather_landing(ref, c, src_chip):
    return _col_half(ref, c, lead=(src_chip,))


def _reduce_copy(src_ref, land_ref, send_sems, recv_sems, k, receiving):
    x, y, c = _mesh_pos()
    px, py, pc = _flip(x, (k >> 2) & 1), _flip(y, (k >> 1) & 1), _flip(c, k & 1)
    hc = src_ref.shape[2] // 2
    src = src_ref.at[2 * px + py, :, pl.ds(pl.multiple_of(pc * hc, 128), hc)]
    slot = (4 * px + 2 * py + pc) if receiving else (4 * x + 2 * y + c)
    return pltpu.make_async_remote_copy(
        src_ref=src, dst_ref=land_ref.at[slot], send_sem=send_sems.at[k - 1], recv_sem=recv_sems.at[k - 1],
        device_id=(px, py, pc), device_id_type=MESH)


def _reduce_start(grad4, name):
    k4, r, cols = grad4.shape

    def body(src_ref, land_ref, send_sems, recv_sems, src_thru, land_thru, token):
        for k in range(1, N_DEV):
            _reduce_copy(src_ref, land_ref, send_sems, recv_sems, k, receiving=False).start()
        token[...] = jnp.zeros_like(token)

    sems = pltpu.SemaphoreType.DMA((N_DEV - 1,))
    land = pltpu.with_memory_space_constraint(lax.empty((N_DEV, r, cols // 2), grad4.dtype), pltpu.HBM)
    return pl.pallas_call(
        body, name=name,
        out_shape=(sems, sems, pltpu.HBM(grad4.shape, grad4.dtype), pltpu.HBM(land.shape, land.dtype),
                   jax.ShapeDtypeStruct((8, 128), F32)),
        in_specs=[_HBM, _HBM], out_specs=[_SEM, _SEM, _HBM, _HBM, pl.BlockSpec(memory_space=pltpu.VMEM)],
        input_output_aliases={0: 2, 1: 3},
        compiler_params=pltpu.CompilerParams(has_side_effects=_DATAFLOW),
    )(*_pin(grad4), land)


def _reduce_wait(send_sems, recv_sems, src_thru, land_thru, after, name):
    def body(src_ref, land_ref, send_sems, recv_sems, after_ref, src_out, land_out):
        for k in range(1, N_DEV):
            cp = _reduce_copy(src_ref, land_ref, send_sems, recv_sems, k, receiving=True)
            cp.wait_send()
            cp.wait_recv()

    return pl.pallas_call(
        body, name=name,
        out_shape=(pltpu.HBM(src_thru.shape, src_thru.dtype), pltpu.HBM(land_thru.shape, land_thru.dtype)),
        in_specs=[_HBM, _HBM, _SEM, _SEM, _HBM], out_specs=[_HBM, _HBM],
        input_output_aliases={0: 0, 1: 1},
        compiler_params=pltpu.CompilerParams(has_side_effects=_DATAFLOW),
    )(src_thru, land_thru, send_sems, recv_sems, *_pin(after))


def _peer_copy(src_ref, land_ref, send_sems, recv_sems, idx, k, receiving):
    x, y, c = _mesh_pos()
    px, py, pc = _flip(x, (k >> 2) & 1), _flip(y, (k >> 1) & 1), _flip(c, k & 1)
    if land_ref.shape[0] == N_DEV:
        slot = (4 * px + 2 * py + pc) if receiving else (4 * x + 2 * y + c)
    else:
        slot = pc if receiving else c
    return pltpu.make_async_remote_copy(
        src_ref=src_ref, dst_ref=land_ref.at[slot], send_sem=send_sems.at[idx], recv_sem=recv_sems.at[idx],
        device_id=(px, py, pc), device_id_type=MESH)


def _exchange_start(arrays, peers, name):
    n = len(arrays)

    def body(*refs):
        src_refs, land_refs = refs[:n], refs[n:2 * n]
        send_sems, recv_sems = refs[2 * n], refs[2 * n + 1]
        token = refs[-1]
        for j, k in enumerate(peers):
            for a in range(n):
                _peer_copy(src_refs[a], land_refs[a], send_sems, recv_sems, a * len(peers) + j, k, receiving=False).start()
        token[...] = jnp.zeros_like(token)

    sems = pltpu.SemaphoreType.DMA((n * len(peers),))
    n_slots = N_DEV if len(peers) > 1 else 2
    lands = [pltpu.with_memory_space_constraint(lax.empty((n_slots,) + a.shape, a.dtype), pltpu.HBM) for a in arrays]
    outs = pl.pallas_call(
        body, name=name,
        out_shape=(sems, sems, *[pltpu.HBM(a.shape, a.dtype) for a in arrays], *[pltpu.HBM(l.shape, l.dtype) for l in lands],
                   jax.ShapeDtypeStruct((8, 128), F32)),
        in_specs=[_HBM] * (2 * n), out_specs=[_SEM, _SEM] + [_HBM] * (2 * n) + [pl.BlockSpec(memory_space=pltpu.VMEM)],
        input_output_aliases={i: 2 + i for i in range(2 * n)},
        compiler_params=pltpu.CompilerParams(has_side_effects=_DATAFLOW),
    )(*_pin(*arrays), *lands)
    return outs[0], outs[1], outs[2:2 + n], outs[2 + n:2 + 2 * n], outs[-1]


def _exchange_wait(send_sems, recv_sems, src_thru, land_thru, peers, after, name):
    n = len(src_thru)

    def body(*refs):
        src_refs, land_refs = refs[:n], refs[n:2 * n]
        send_sems, recv_sems = refs[2 * n], refs[2 * n + 1]
        for j, k in enumerate(peers):
            for a in range(n):
                cp = _peer_copy(src_refs[a], land_refs[a], send_sems, recv_sems, a * len(peers) + j, k, receiving=True)
                cp.wait_send()
                cp.wait_recv()

    outs = pl.pallas_call(
        body, name=name,
        out_shape=tuple(pltpu.HBM(s.shape, s.dtype) for s in (*src_thru, *land_thru)),
        in_specs=[_HBM] * (2 * n) + [_SEM, _SEM, _HBM], out_specs=[_HBM] * (2 * n),
        input_output_aliases={i: i for i in range(2 * n)},
        compiler_params=pltpu.CompilerParams(has_side_effects=_DATAFLOW),
    )(*src_thru, *land_thru, send_sems, recv_sems, *_pin(after))
    return outs[:n], outs[n:]


ALL_PEERS = tuple(range(1, N_DEV))
SIBLING = (1,)


def _sum_eight(recv, grad4, pos):
    n, r, hc = recv.shape
    tc = hc // 2

    def body(pos_ref, r_ref, g_ref, o_ref):
        me = pos_ref[2]
        o_ref[...] = jnp.zeros_like(o_ref)
        for s in range(n):
            @pl.when(me == s)
            def _():
                o_ref[...] += g_ref[0].astype(F32)

            @pl.when(me != s)
            def _():
                o_ref[...] += r_ref[s].astype(F32)

    grid_spec = pltpu.PrefetchScalarGridSpec(
        num_scalar_prefetch=1, grid=(2,),
        in_specs=[pl.BlockSpec((n, r, tc), lambda i, pos: (0, 0, i)),
                  pl.BlockSpec((1, r, tc), lambda i, pos: (pos[1], 0, pos[0] * 2 + i))],
        out_specs=pl.BlockSpec((r, tc), lambda i, pos: (0, i)))
    return pl.pallas_call(body, name="sum_eight", grid_spec=grid_spec, out_shape=_out((r, hc), F32),
                          compiler_params=_cp(("parallel",), 32))(pos, *_pin(recv, grad4))


def _gather_finish(lands, shards):
    n = len(lands)
    any_spec = _HBM

    def body(*refs):
        shard_refs, out_refs = refs[n:2 * n], refs[2 * n:3 * n]
        send_sems, recv_sems, local_sems = refs[3 * n:]
        x, y, c = _mesh_pos()
        chip = 2 * x + y
        local, sends = [], []
        for a in range(n):
            cp = pltpu.make_async_copy(shard_refs[a], out_refs[a].at[chip], local_sems.at[a])
            cp.start()
            local.append(cp)
        for j in range(1, N_CHIPS):
            px, py = _peer_chip(x, y, j)
            for a in range(n):
                landed = _col_half(out_refs[a], c, lead=(2 * px + py,))
                cp = pltpu.make_async_remote_copy(
                    src_ref=landed, dst_ref=landed, send_sem=send_sems.at[a, j], recv_sem=recv_sems.at[a, j],
                    device_id=(x, y, 1 - c), device_id_type=MESH)
                cp.start()
                sends.append(cp)
        for j in range(1, N_CHIPS):
            px, py = _peer_chip(x, y, j)
            for a in range(n):
                other = _col_half(out_refs[a], 1 - c, lead=(2 * px + py,))
                pltpu.make_async_remote_copy(
                    src_ref=other, dst_ref=other, send_sem=send_sems.at[a, j], recv_sem=recv_sems.at[a, j],
                    device_id=(x, y, 1 - c), device_id_type=MESH).wait_recv()
        for cp in sends:
            cp.wait_send()
        for cp in local:
            cp.wait()

    return pl.pallas_call(
        body, name="gather_finish",
        out_shape=[_out(l.shape, l.dtype) for l in lands],
        in_specs=[any_spec] * (2 * n), out_specs=[any_spec] * n,
        input_output_aliases={i: i for i in range(n)},
        scratch_shapes=[pltpu.SemaphoreType.DMA((n, N_CHIPS))] * 2 + [pltpu.SemaphoreType.DMA((n,))],
        compiler_params=_cp(vmem_mb=16),
    )(*lands, *shards)


def _halves_exchange(halves):
    n = len(halves)
    any_spec = _HBM

    def body(*refs):
        in_refs, out_refs = refs[:n], refs[n:2 * n]
        send_sems, recv_sems, local_sems = refs[2 * n:]
        x, y, c = _mesh_pos()
        copies, local = [], []
        for a in range(n):
            hc = halves[a].shape[1]
            mine = pl.ds(pl.multiple_of(c * hc, 128), hc)
            lc = pltpu.make_async_copy(in_refs[a], out_refs[a].at[:, mine], local_sems.at[a])
            lc.start()
            local.append(lc)
            cp = pltpu.make_async_remote_copy(
                src_ref=in_refs[a], dst_ref=out_refs[a].at[:, mine],
                send_sem=send_sems.at[a], recv_sem=recv_sems.at[a],
                device_id=(x, y, 1 - c), device_id_type=MESH)
            cp.start()
            copies.append(cp)
        for a in range(n):
            hc = halves[a].shape[1]
            theirs = pl.ds(pl.multiple_of((1 - c) * hc, 128), hc)
            pltpu.make_async_remote_copy(
                src_ref=in_refs[a], dst_ref=out_refs[a].at[:, theirs],
                send_sem=send_sems.at[a], recv_sem=recv_sems.at[a],
                device_id=(x, y, 1 - c), device_id_type=MESH).wait_recv()
        for cp in copies:
            cp.wait_send()
        for lc in local:
            lc.wait()

    return pl.pallas_call(
        body, name="halves_exchange",
        out_shape=[_out((h.shape[0], 2 * h.shape[1]), h.dtype) for h in halves],
        in_specs=[any_spec] * n, out_specs=[any_spec] * n,
        scratch_shapes=[pltpu.SemaphoreType.DMA((n,))] * 3,
        compiler_params=_cp(vmem_mb=16),
    )(*halves)


def _adam_math(w, g, m, v):
    m2 = ADAM_B1 * m + (1.0 - ADAM_B1) * g
    v2 = ADAM_B2 * v + (1.0 - ADAM_B2) * (g * g)
    m_hat = m2 / (1.0 - ADAM_B1 ** ADAM_STEP)
    v_hat = v2 / (1.0 - ADAM_B2 ** ADAM_STEP)
    delta = -ADAM_LR * (m_hat / (jnp.sqrt(v_hat) + ADAM_EPS) + ADAM_WD * w)
    return delta, m2, v2


def _adam_big(g_own, g_pair, w, m, v, pos):
    r, c = w.shape
    tc = c // 4

    def body(pos_ref, go_ref, gp_ref, w_ref, m_ref, v_ref, g_ref, d_ref, m2_ref, v2_ref):
        half = pl.program_id(0) // 2
        g = jnp.where(half == pos_ref[0], go_ref[...], gp_ref[0])
        d, m2, v2 = _adam_math(w_ref[...], g, m_ref[...], v_ref[...])
        g_ref[...] = g
        d_ref[...] = d
        m2_ref[...] = m2
        v2_ref[...] = v2

    spec = pl.BlockSpec((r, tc), lambda i, pos: (0, i))
    grid_spec = pltpu.PrefetchScalarGridSpec(
        num_scalar_prefetch=1, grid=(4,),
        in_specs=[pl.BlockSpec((r, tc), lambda i, pos: (0, i % 2)),
                  pl.BlockSpec((1, r, tc), lambda i, pos: (1 - pos[0], 0, i % 2)), spec, spec, spec],
        out_specs=[spec] * 4)
    sh = _out((r, c), F32)
    return pl.pallas_call(body, name="adam_big", grid_spec=grid_spec, out_shape=[sh] * 4,
                          compiler_params=_cp(("parallel",), 32))(pos, *_pin(g_own, g_pair, w, m, v))


def _adam_ada(c_act_t, dmod_cols, w, m, v):
    r, c = w.shape
    tc = 512

    def body(ct_ref, dm_ref, w_ref, m_ref, v_ref, g_ref, d_ref, m2_ref, v2_ref):
        g = _nn(ct_ref[...], dm_ref[...].astype(BF))
        d, m2, v2 = _adam_math(w_ref[...], g, m_ref[...], v_ref[...])
        g_ref[...] = g
        d_ref[...] = d
        m2_ref[...] = m2
        v2_ref[...] = v2

    spec = pl.BlockSpec((r, tc), lambda i: (0, i))
    sh = _out((r, c), F32)
    return pl.pallas_call(
        body, name="adam_ada", grid=(c // tc,),
        in_specs=[pl.BlockSpec(c_act_t.shape, lambda i: (0, 0)), pl.BlockSpec((dmod_cols.shape[0], tc), lambda i: (0, i)),
                  spec, spec, spec],
        out_specs=[spec] * 4, out_shape=[sh] * 4, compiler_params=_cp(("parallel",), 48))(*_pin(c_act_t, dmod_cols, w, m, v))


def _ada_mod(c_all, w_shard, b_shard):
    nb, d = c_all.shape
    cols = w_shard.shape[1]
    tc = 512

    def body(c_ref, w_ref, b_ref, mod_ref, act_ref):
        cv = c_ref[...]
        act = cv * _sigmoid(cv)
        act_ref[...] = act
        mod_ref[...] = _nn(act.astype(BF), w_ref[...].astype(BF)) + b_ref[...]

    return pl.pallas_call(
        body, name="ada_mod", grid=(cols // tc,),
        in_specs=[pl.BlockSpec((nb, d), lambda i: (0, 0)), pl.BlockSpec((d, tc), lambda i: (0, i)),
                  pl.BlockSpec((1, tc), lambda i: (0, i))],
        out_specs=[pl.BlockSpec((nb, tc), lambda i: (0, i)), pl.BlockSpec((nb, d), lambda i: (0, 0))],
        out_shape=[_out((nb, cols), F32), _out((nb, d), F32)],
        compiler_params=_cp(("arbitrary",), 32))(*_pin(c_all, w_shard, b_shard))


SUB_ROWS = 256
ROW_TILE = 512


def _sub_rows(tm):
    return [slice(s, s + SUB_ROWS) for s in range(0, tm, SUB_ROWS)] if tm > SUB_ROWS else [slice(0, tm)]


def _token_spec():
    return pl.BlockSpec((8, 128), lambda *_: (0, 0))


def _in_proj(x, mod3, g_mix, w_in_t, seq, token):
    t, d = x.shape
    tm = min(ROW_TILE, seq)
    tps = seq // tm

    def body(x_ref, mod_ref, g_ref, w_ref, tok_ref, proj_ref, u1_ref):
        for rows in _sub_rows(tm):
            xv = x_ref[rows, :]
            r = lax.rsqrt(jnp.mean(xv * xv, -1, keepdims=True) + EPS)
            u = (xv * r * g_ref[...]) * (1.0 + mod_ref[0, 1:2, :]) + mod_ref[0, 0:1, :]
            ub = u.astype(BF)
            u1_ref[rows, :] = ub
            proj_ref[rows, :] = _nt(ub, w_ref[...])

    return pl.pallas_call(
        body, name="in_proj", grid=(t // tm,),
        in_specs=[pl.BlockSpec((tm, d), lambda i: (i, 0)), pl.BlockSpec((1, N_MOD, d), lambda i: (i // tps, 0, 0)),
                  pl.BlockSpec((1, d), lambda i: (0, 0)), pl.BlockSpec((IN_PAD, d), lambda i: (0, 0)), _token_spec()],
        out_specs=[pl.BlockSpec((tm, IN_PAD), lambda i: (i, 0)), pl.BlockSpec((tm, d), lambda i: (i, 0))],
        out_shape=[_out((t, IN_PAD), F32), _out((t, d), BF)],
        compiler_params=_cp(("parallel",), 48))(*_pin(x, mod3, g_mix, w_in_t, token))


def _pool_tile(seq):
    return min(512, seq)


def _pool_fwd(proj, w_pool, pool_scale, nb, seq):
    ts = _pool_tile(seq)
    nt = seq // ts

    def body(u_ref, halo_ref, wp_ref, ps_ref, yp_ref, p_ref):
        i = pl.program_id(1)
        halo = jnp.where(i == 0, 0.0, halo_ref[...])
        u = u_ref[...]
        ext = jnp.concatenate([halo, u], 0)
        tpos = i * ts + _iota((ts, 1), 0)
        for g, w in enumerate(POOL_WINDOWS):
            gs = slice(g * POOL_GROUP, (g + 1) * POOL_GROUP)
            s = ext[:, gs]
            sh = 1
            while sh < w:
                s = s + pltpu.roll(s, sh, 0)
                sh *= 2
            cnt = jnp.minimum(tpos + 1, w).astype(F32)
            pb = (s[HALO:] / cnt - u[:, gs]).astype(BF)
            p_ref[:, gs] = pb
            yp_ref[:, gs] = (_nn(pb, wp_ref[g].astype(BF)) * ps_ref[:, gs]).astype(BF)

    hb = ts // HALO
    return pl.pallas_call(
        body, name="pool_fwd", grid=(nb, nt),
        in_specs=[pl.BlockSpec((ts, POOL_WIDTH), lambda b, i: (b * nt + i, 0)),
                  pl.BlockSpec((HALO, POOL_WIDTH), lambda b, i: (jnp.maximum((b * nt + i) * hb - 1, 0), 0)),
                  pl.BlockSpec((4, POOL_GROUP, POOL_GROUP), lambda b, i: (0, 0, 0)),
                  pl.BlockSpec((1, POOL_WIDTH), lambda b, i: (0, 0))],
        out_specs=[pl.BlockSpec((ts, POOL_WIDTH), lambda b, i: (b * nt + i, 0))] * 2,
        out_shape=[_out((nb * seq, POOL_WIDTH), BF)] * 2,
        compiler_params=_cp(("parallel", "parallel"), 32))(*_pin(proj, proj, w_pool, pool_scale))


def _conv_pre(uxbc, halo, cw, cb, first):
    halo = jnp.where(first, 0.0, halo)
    ext = jnp.concatenate([halo, uxbc], 0)
    pre = cb + uxbc * cw[3:4]
    for k in (2, 1, 0):
        pre = pre + pltpu.roll(ext, 3 - k, 0)[CONV_HALO:] * cw[k:k + 1]
    return pre


def _chunk_terms(pre, udt, dtb, alog):
    sg = _sigmoid(pre)
    xbc = pre * sg
    dtp = udt[:, :SSD_HEADS] + dtb
    dt = jnp.maximum(dtp, 0.0) + jnp.log(1.0 + jnp.exp(-jnp.abs(dtp)))
    a = -jnp.exp(alog)
    da = dt * a
    tril = (_iota((CHUNK, CHUNK), 0) >= _iota((CHUNK, CHUNK), 1))
    acum = _exact_nn_left(tril.astype(BF), da)
    eye = (_iota((SSD_HEADS, SSD_HEADS), 0) == _iota((SSD_HEADS, SSD_HEADS), 1)).astype(BF)
    acum_t = _exact_nt_left(eye, acum)
    expand = _head_expand_matrix(SSD_HEADS, SSD_INNER)
    acum_e = _exact_nn(acum, expand)
    dt_e = _exact_nn(dt, expand)
    last_e = acum_e[CHUNK - 1:CHUNK]
    return dict(pre=pre, sg=sg, xbc=xbc, dtp=dtp, dt=dt, a=a, acum=acum, acum_t=acum_t, tril=tril,
                dt_e=dt_e, e_a=jnp.exp(acum_e), d_out=jnp.exp(last_e - acum_e), c_dec=jnp.exp(last_e))


def _head_decay(r, h):
    seg = r["acum"][:, h:h + 1] - r["acum_t"][h:h + 1, :]
    return jnp.where(r["tril"], jnp.exp(jnp.minimum(seg, 0.0)), 0.0)


def _ssd_specs(nb, seq, reverse):
    nc = seq // CHUNK
    per = seq // CONV_HALO

    def cidx(c):
        return (nc - 1 - c) if reverse else c

    def row(b, c):
        return b * nc + cidx(c)

    specs = [
        pl.BlockSpec((CHUNK, CONV_CH), lambda b, c: (row(b, c), 1)),
        pl.BlockSpec((CONV_HALO, CONV_CH),
                     lambda b, c: (jnp.maximum(b * per + cidx(c) * (CHUNK // CONV_HALO) - 1, 0), 1)),
        pl.BlockSpec((CHUNK, GROUP_W), lambda b, c: (row(b, c), 1)),
        pl.BlockSpec((CHUNK, GROUP_W), lambda b, c: (row(b, c), 2)),
        pl.BlockSpec((CHUNK, 128), lambda b, c: (row(b, c), OFF_DT // 128)),
    ]
    return specs, row, cidx, nc


def _const_spec(shape):
    return pl.BlockSpec(shape, lambda b, c: (0,) * len(shape))


def _ssd_fwd(proj, conv_w, conv_b, dt_bias, a_log, dskip_e, g_ssd, nb, seq):
    specs, row, cidx, nc = _ssd_specs(nb, seq, reverse=False)

    def body(uxbc_ref, halo_ref, z0_ref, z1_ref, udt_ref, cw_ref, cb_ref, dtb_ref, alog_ref, dsk_ref, gs_ref,
             yssd_ref, yssm_ref, hprev_ref, pre_ref, h_ref, yd_ref):
        c = pl.program_id(1)

        @pl.when(c == 0)
        def _():
            h_ref[...] = jnp.zeros_like(h_ref)

        pre = _conv_pre(uxbc_ref[...], halo_ref[...], cw_ref[...], cb_ref[...], c == 0)
        pre_ref[...] = pre
        r = _chunk_terms(pre, udt_ref[...], dtb_ref[...], alog_ref[...])
        xbc = r["xbc"]
        xs = xbc[:, :SSD_INNER]
        xdt = xs * r["dt_e"]
        xdt_b = xdt.astype(BF)
        xdo_b = (xdt * r["d_out"]).astype(BF)
        hprev_ref[0, 0] = h_ref[...]
        for g in range(2):
            gs = slice(g * GROUP_W, (g + 1) * GROUP_W)
            bg = xbc[:, SSD_INNER + g * SSD_STATE:SSD_INNER + (g + 1) * SSD_STATE].astype(BF)
            cg = xbc[:, SSD_INNER + (2 + g) * SSD_STATE:SSD_INNER + (3 + g) * SSD_STATE].astype(BF)
            scores = _nt(cg, bg)
            hg = h_ref[g]
            y_off = _nn(cg, hg.astype(BF)) * r["e_a"][:, gs]
            for hh in range(8):
                h = g * 8 + hh
                m = (scores * _head_decay(r, h)).astype(BF)
                yd_ref[:, h * SSD_HEAD_DIM:(h + 1) * SSD_HEAD_DIM] = _nn(m, xdt_b[:, h * SSD_HEAD_DIM:(h + 1) * SSD_HEAD_DIM])
            h_ref[g] = hg * r["c_dec"][:, gs] + _tn(bg, xdo_b[:, gs])
            y = yd_ref[:, gs] + y_off + dsk_ref[:, gs] * xs[:, gs]
            yssm_ref[:, gs] = y
            zg = (z0_ref if g == 0 else z1_ref)[...]
            yg = y * (zg * _sigmoid(zg))
            rg = lax.rsqrt(jnp.mean(yg * yg, -1, keepdims=True) + EPS)
            yssd_ref[:, gs] = (yg * rg * gs_ref[:, gs]).astype(BF)

    t = nb * seq
    return pl.pallas_call(
        body, name="ssd_fwd", grid=(nb, nc),
        in_specs=specs + [_const_spec((4, CONV_CH)), _const_spec((1, CONV_CH)), _const_spec((1, SSD_HEADS)),
                          _const_spec((1, SSD_HEADS)), _const_spec((1, SSD_INNER)), _const_spec((1, SSD_INNER))],
        out_specs=[pl.BlockSpec((CHUNK, SSD_INNER), lambda b, c: (row(b, c), 0)),
                   pl.BlockSpec((CHUNK, SSD_INNER), lambda b, c: (row(b, c), 0)),
                   pl.BlockSpec((1, 1, 2, SSD_STATE, GROUP_W), lambda b, c: (b, c, 0, 0, 0)),
                   pl.BlockSpec((CHUNK, CONV_CH), lambda b, c: (row(b, c), 0))],
        out_shape=[_out((t, SSD_INNER), BF), _out((t, SSD_INNER), F32),
                   _out((nb, nc, 2, SSD_STATE, GROUP_W), F32), _out((t, CONV_CH), F32)],
        scratch_shapes=[pltpu.VMEM((2, SSD_STATE, GROUP_W), F32), pltpu.VMEM((CHUNK, SSD_INNER), F32)],
        compiler_params=_cp(("arbitrary", "arbitrary"), 48),
    )(*_pin(proj, proj, proj, proj, proj, conv_w, conv_b, dt_bias, a_log, dskip_e, g_ssd))


def _out_proj(y_pool, y_ssd, w_out, x, mod3, g_mlp, seq):
    t, d = x.shape
    tm = 512
    tps = seq // tm if seq >= tm else 1
    tm = min(tm, seq)

    def body(yp_ref, ys_ref, w_ref, x_ref, mod_ref, g_ref, h1_ref, o_ref, u2_ref):
        o = _nn(yp_ref[...], w_ref[0:POOL_WIDTH, :]) + _nn(ys_ref[...], w_ref[POOL_WIDTH:, :])
        o_ref[...] = o.astype(BF)
        h1 = x_ref[...] + mod_ref[0, 2:3, :] * o
        h1_ref[...] = h1
        r = lax.rsqrt(jnp.mean(h1 * h1, -1, keepdims=True) + EPS)
        u2_ref[...] = ((h1 * r * g_ref[...]) * (1.0 + mod_ref[0, 4:5, :]) + mod_ref[0, 3:4, :]).astype(BF)

    row = lambda i: (i, 0)
    return pl.pallas_call(
        body, name="out_proj", grid=(t // tm,),
        in_specs=[pl.BlockSpec((tm, POOL_WIDTH), row), pl.BlockSpec((tm, SSD_INNER), row),
                  pl.BlockSpec(w_out.shape, lambda i: (0, 0)), pl.BlockSpec((tm, d), row),
                  pl.BlockSpec((1, N_MOD, d), lambda i: (i // tps, 0, 0)), pl.BlockSpec((1, d), lambda i: (0, 0))],
        out_specs=[pl.BlockSpec((tm, d), row)] * 3,
        out_shape=[_out((t, d), F32), _out((t, d), BF), _out((t, d), BF)],
        compiler_params=_cp(("parallel",), 48))(*_pin(y_pool, y_ssd, w_out, x, mod3, g_mlp))


def _mlp_up(u2, w_up4):
    t, d = u2.shape
    tm = min(1024, t)
    nk, _, cols = w_up4.shape

    def body(u_ref, w_ref, a_ref):
        a_ref[...] = _nn(u_ref[...], w_ref[0]).astype(BF)

    return pl.pallas_call(
        body, name="mlp_up", grid=(nk, t // tm),
        in_specs=[pl.BlockSpec((tm, d), lambda k, i: (i, 0)), pl.BlockSpec((1, d, cols), lambda k, i: (k, 0, 0))],
        out_specs=pl.BlockSpec((tm, cols), lambda k, i: (i, k)),
        out_shape=_out((t, nk * cols), BF),
        compiler_params=_cp(("parallel", "parallel"), 32))(*_pin(u2, w_up4))


def _mlp_down_loss(a_up, w_down, h1, mod3, g_final, target, seq):
    t, d = h1.shape
    nb = t // seq
    tm = min(ROW_TILE, seq)
    tps = seq // tm

    def body(a_ref, w_ref, h1_ref, mod_ref, g_ref, tg_ref, ddn_ref, dh2_ref, sq_ref, gg_ref, dgf_ref):
        i = pl.program_id(0)

        @pl.when(i == 0)
        def _():
            sq_ref[...] = jnp.zeros_like(sq_ref)
            gg_ref[...] = jnp.zeros_like(gg_ref)

        @pl.when(i % tps == 0)
        def _():
            dgf_ref[...] = jnp.zeros_like(dgf_ref)

        gate = mod_ref[0, 5:6, :]
        sq = gg = dgf = 0.0
        for rows in _sub_rows(tm):
            f = jnp.square(jnp.maximum(a_ref[rows, :], 0))
            dn = _nn(f, w_ref[...])
            h2 = h1_ref[rows, :] + gate * dn
            r = lax.rsqrt(jnp.mean(h2 * h2, -1, keepdims=True) + EPS)
            hh = h2 * r
            err = hh * g_ref[...] - tg_ref[rows, :]
            dy = err * (1.0 / d)
            dhat = dy * g_ref[...]
            dh2 = r * (dhat - hh * jnp.mean(dhat * hh, -1, keepdims=True))
            dh2_ref[rows, :] = dh2
            ddn_ref[rows, :] = (dh2 * gate).astype(BF)
            sq = sq + jnp.sum(err * err, 0, keepdims=True)
            gg = gg + jnp.sum(dy * hh, 0, keepdims=True)
            dgf = dgf + jnp.sum(dh2 * dn, 0, keepdims=True)
        sq_ref[...] += sq
        gg_ref[...] += gg
        dgf_ref[0] += dgf

    row = lambda i: (i, 0)
    vec = pl.BlockSpec((1, d), lambda i: (0, 0))
    return pl.pallas_call(
        body, name="mlp_down_loss", grid=(t // tm,),
        in_specs=[pl.BlockSpec((tm, D_FF), row), pl.BlockSpec(w_down.shape, lambda i: (0, 0)), pl.BlockSpec((tm, d), row),
                  pl.BlockSpec((1, N_MOD, d), lambda i: (i // tps, 0, 0)), vec, pl.BlockSpec((tm, d), row)],
        out_specs=[pl.BlockSpec((tm, d), row), pl.BlockSpec((tm, d), row), vec, vec,
                   pl.BlockSpec((1, 1, d), lambda i: (i // tps, 0, 0))],
        out_shape=[_out((t, d), BF), _out((t, d), F32), _out((1, d), F32),
                   _out((1, d), F32), _out((nb, 1, d), F32)],
        compiler_params=_cp(("arbitrary",), 56))(*_pin(a_up, w_down, h1, mod3, g_final, target))


def _tn_matmul(a, b, tk, tn, name, square_relu=False, out3=False):
    t, kdim = a.shape
    ndim = b.shape[1]

    def body(a_ref, b_ref, o_ref):
        av = a_ref[...]
        if square_relu:
            av = jnp.square(jnp.maximum(av, 0))
        res = _tn(av, b_ref[...]).astype(BF)
        if out3:
            o_ref[0] = res
        else:
            o_ref[...] = res

    if out3:
        out_spec = pl.BlockSpec((1, tk, tn), lambda j, i: (j, i, 0))
        out_shape = _out((ndim // tn, kdim, tn), BF)
    else:
        out_spec = pl.BlockSpec((tk, tn), lambda j, i: (i, j))
        out_shape = _out((kdim, ndim), BF)
    return pl.pallas_call(
        body, name=name, grid=(ndim // tn, kdim // tk),
        in_specs=[pl.BlockSpec((t, tk), lambda j, i: (0, i)), pl.BlockSpec((t, tn), lambda j, i: (0, j))],
        out_specs=out_spec, out_shape=out_shape,
        compiler_params=_cp(("parallel", "parallel"), 56))(*_pin(a, b))


def _mlp_down_bwd(d_dn, w_down4, a_up, token):
    t, d = d_dn.shape
    tm = min(1024, t)
    nk, rows, _ = w_down4.shape

    def body(g_ref, w_ref, a_ref, tok_ref, o_ref):
        df = _nt(g_ref[...], w_ref[0])
        o_ref[...] = (df * (2.0 * jnp.maximum(a_ref[...], 0).astype(F32))).astype(BF)

    return pl.pallas_call(
        body, name="mlp_down_bwd", grid=(nk, t // tm),
        in_specs=[pl.BlockSpec((tm, d), lambda k, i: (i, 0)), pl.BlockSpec((1, rows, d), lambda k, i: (k, 0, 0)),
                  pl.BlockSpec((tm, rows), lambda k, i: (i, k)), _token_spec()],
        out_specs=pl.BlockSpec((tm, rows), lambda k, i: (i, k)),
        out_shape=_out((t, nk * rows), BF),
        compiler_params=_cp(("parallel", "parallel"), 32))(*_pin(d_dn, w_down4, a_up, token))


def _mlp_up_bwd(d_a, w_up4, h1, dh2, o, mod3, g_mlp, seq, token):
    t, d = h1.shape
    nb = t // seq
    tm = min(ROW_TILE, seq)
    tps = seq // tm
    nk = w_up4.shape[0]
    cols = w_up4.shape[2]

    def body(da_ref, w_ref, h1_ref, dh2_ref, o_ref, mod_ref, g_ref, tok_ref, dh1_ref, do_ref, acc_ref, gg_ref):
        i = pl.program_id(0)

        @pl.when(i == 0)
        def _():
            gg_ref[...] = jnp.zeros_like(gg_ref)

        @pl.when(i % tps == 0)
        def _():
            acc_ref[...] = jnp.zeros_like(acc_ref)

        gg = a_shift = a_scale = a_gate = 0.0
        for rows in _sub_rows(tm):
            du = _nt(da_ref[rows, 0:cols], w_ref[0])
            for k in range(1, nk):
                du = du + _nt(da_ref[rows, k * cols:(k + 1) * cols], w_ref[k])
            h1 = h1_ref[rows, :]
            r = lax.rsqrt(jnp.mean(h1 * h1, -1, keepdims=True) + EPS)
            hh = h1 * r
            n2 = hh * g_ref[...]
            dn2 = du * (1.0 + mod_ref[0, 4:5, :])
            dhat = dn2 * g_ref[...]
            dh1 = dh2_ref[rows, :] + r * (dhat - hh * jnp.mean(dhat * hh, -1, keepdims=True))
            dh1_ref[rows, :] = dh1
            do_ref[rows, :] = (dh1 * mod_ref[0, 2:3, :]).astype(BF)
            gg = gg + jnp.sum(dn2 * hh, 0, keepdims=True)
            a_shift = a_shift + jnp.sum(du, 0, keepdims=True)
            a_scale = a_scale + jnp.sum(du * n2, 0, keepdims=True)
            a_gate = a_gate + jnp.sum(dh1 * o_ref[rows, :].astype(F32), 0, keepdims=True)
        gg_ref[...] += gg
        acc_ref[0, 0:1, :] += a_shift
        acc_ref[0, 1:2, :] += a_scale
        acc_ref[0, 2:3, :] += a_gate

    row = lambda i: (i, 0)
    vec = pl.BlockSpec((1, d), lambda i: (0, 0))
    return pl.pallas_call(
        body, name="mlp_up_bwd", grid=(t // tm,),
        in_specs=[pl.BlockSpec((tm, D_FF), row), pl.BlockSpec(w_up4.shape, lambda i: (0, 0, 0)), pl.BlockSpec((tm, d), row),
                  pl.BlockSpec((tm, d), row), pl.BlockSpec((tm, d), row),
                  pl.BlockSpec((1, N_MOD, d), lambda i: (i // tps, 0, 0)), vec, _token_spec()],
        out_specs=[pl.BlockSpec((tm, d), row), pl.BlockSpec((tm, d), row),
                   pl.BlockSpec((1, 8, d), lambda i: (i // tps, 0, 0)), vec],
        out_shape=[_out((t, d), F32), _out((t, d), BF),
                   _out((nb, 8, d), F32), _out((1, d), F32)],
        compiler_params=_cp(("arbitrary",), 56))(*_pin(d_a, w_up4, h1, dh2, o, mod3, g_mlp, token))


def _out_proj_bwd(d_o, w_out, token):
    t, d = d_o.shape
    tm = min(512, t)

    def body(g_ref, w_ref, tok_ref, dp_ref, ds_ref):
        gv = g_ref[...]
        dp_ref[...] = _nt(gv, w_ref[0:POOL_WIDTH, :])
        ds_ref[...] = _nt(gv, w_ref[POOL_WIDTH:, :])

    row = lambda i: (i, 0)
    return pl.pallas_call(
        body, name="out_proj_bwd", grid=(t // tm,),
        in_specs=[pl.BlockSpec((tm, d), row), pl.BlockSpec(w_out.shape, lambda i: (0, 0)), _token_spec()],
        out_specs=[pl.BlockSpec((tm, POOL_WIDTH), row), pl.BlockSpec((tm, SSD_INNER), row)],
        out_shape=[_out((t, POOL_WIDTH), F32), _out((t, SSD_INNER), F32)],
        compiler_params=_cp(("parallel",), 32))(*_pin(d_o, w_out, token))


def _pool_bwd(d_ypool, p, w_pool, pool_scale, nb, seq):
    ts = _pool_tile(seq)
    nt = seq // ts
    hb = ts // HALO
    last_block = nb * seq // HALO - 1

    def body(dy_ref, halo_ref, p_ref, wp_ref, ps_ref, du_ref, gw_ref, gs_ref):
        b = pl.program_id(0)
        i = pl.program_id(1)

        @pl.when((b == 0) & (i == 0))
        def _():
            gw_ref[...] = jnp.zeros_like(gw_ref)
            gs_ref[...] = jnp.zeros_like(gs_ref)

        halo = jnp.where(i == nt - 1, 0.0, halo_ref[...])
        dy = dy_ref[...]
        ext = jnp.concatenate([dy, halo], 0)
        tpos = i * ts + _iota((ts + HALO, 1), 0)
        n_ext = ts + HALO
        for g, w in enumerate(POOL_WINDOWS):
            gs = slice(g * POOL_GROUP, (g + 1) * POOL_GROUP)
            wg = wp_ref[g].astype(BF)
            pg = p_ref[:, gs]
            pw = _nn(pg, wg)
            gs_ref[:, gs] += jnp.sum(dy[:, gs] * pw, 0, keepdims=True)
            dpw = (ext[:, gs] * ps_ref[:, gs]).astype(BF)
            gw_ref[g] += _tn(pg, dpw[:ts])
            dp = _nt(dpw, wg)
            cnt = jnp.minimum(tpos + 1, w).astype(F32)
            s = dp / cnt
            sh = 1
            while sh < w:
                s = s + pltpu.roll(s, n_ext - sh, 0)
                sh *= 2
            du_ref[:, gs] = (s[:ts] - dp[:ts]).astype(BF)

    return pl.pallas_call(
        body, name="pool_bwd", grid=(nb, nt),
        in_specs=[pl.BlockSpec((ts, POOL_WIDTH), lambda b, i: (b * nt + i, 0)),
                  pl.BlockSpec((HALO, POOL_WIDTH), lambda b, i: (jnp.minimum((b * nt + i + 1) * hb, last_block), 0)),
                  pl.BlockSpec((ts, POOL_WIDTH), lambda b, i: (b * nt + i, 0)),
                  pl.BlockSpec((4, POOL_GROUP, POOL_GROUP), lambda b, i: (0, 0, 0)),
                  pl.BlockSpec((1, POOL_WIDTH), lambda b, i: (0, 0))],
        out_specs=[pl.BlockSpec((ts, POOL_WIDTH), lambda b, i: (b * nt + i, 0)),
                   pl.BlockSpec((4, POOL_GROUP, POOL_GROUP), lambda b, i: (0, 0, 0)),
                   pl.BlockSpec((1, POOL_WIDTH), lambda b, i: (0, 0))],
        out_shape=[_out((nb * seq, POOL_WIDTH), BF), _out((4, POOL_GROUP, POOL_GROUP), F32),
                   _out((1, POOL_WIDTH), F32)],
        compiler_params=_cp(("arbitrary", "arbitrary"), 32))(*_pin(d_ypool, d_ypool, p, w_pool, pool_scale))


def _ssd_bwd(proj, pre, d_yssd, yssm, h_prev, dt_bias, a_log, dskip_e, g_ssd, nb, seq):
    specs, row, cidx, nc = _ssd_specs(nb, seq, reverse=True)
    specs = specs[2:]

    def body(z0_ref, z1_ref, udt_ref, pre_ref, dys_ref, yssm_ref, hprev_ref,
             dtb_ref, alog_ref, dsk_ref, gs_ref,
             dz_ref, dpre_ref, dudt_ref, ggs_ref, gdsk_ref, ga_ref, gdtb_ref,
             g_ref, dxdt_ref, dyv_ref):
        b = pl.program_id(0)
        c = pl.program_id(1)

        @pl.when(c == 0)
        def _():
            g_ref[...] = jnp.zeros_like(g_ref)

        @pl.when((b == 0) & (c == 0))
        def _():
            ggs_ref[...] = jnp.zeros_like(ggs_ref)
            gdsk_ref[...] = jnp.zeros_like(gdsk_ref)
            ga_ref[...] = jnp.zeros_like(ga_ref)
            gdtb_ref[...] = jnp.zeros_like(gdtb_ref)

        r = _chunk_terms(pre_ref[...], udt_ref[...], dtb_ref[...], alog_ref[...])
        xbc = r["xbc"]
        xs = xbc[:, :SSD_INNER]
        dt_e = r["dt_e"]
        xdt = xs * dt_e
        xdt_b = xdt.astype(BF)
        reduce_m = _head_reduce_matrix(GROUP_W, 8)

        def head_sums(v):
            return _nn(v.astype(BF), reduce_m)

        onehot16 = lambda h: (_iota((1, SSD_HEADS), 1) == h).astype(F32)
        onecol16 = lambda h: (_iota((SSD_HEADS, 1), 0) == h).astype(F32)

        d_acum = jnp.zeros((CHUNK, SSD_HEADS), F32)
        d_acum_t = jnp.zeros((SSD_HEADS, CHUNK), F32)
        d_alast = jnp.zeros((1, SSD_HEADS), F32)
        place8 = lambda g: (_iota((8, SSD_HEADS), 1) == _iota((8, SSD_HEADS), 0) + 8 * g).astype(BF)
        d_b, d_c = [], []
        for g in range(2):
            gs = slice(g * GROUP_W, (g + 1) * GROUP_W)
            zg = (z0_ref if g == 0 else z1_ref)[...]
            sz = _sigmoid(zg)
            silu_z = zg * sz
            ys = yssm_ref[:, gs]
            yg = ys * silu_z
            rg = lax.rsqrt(jnp.mean(yg * yg, -1, keepdims=True) + EPS)
            yh = yg * rg
            dys = dys_ref[:, gs]
            ggs_ref[:, gs] += jnp.sum(dys * yh, 0, keepdims=True)
            dyh = dys * gs_ref[:, gs]
            dyg = rg * (dyh - yh * jnp.mean(dyh * yh, -1, keepdims=True))
            dy = dyg * silu_z
            dz_ref[:, gs] = (dyg * ys * (sz * (1.0 + zg * (1.0 - sz)))).astype(BF)
            gdsk_ref[:, gs] += jnp.sum(dy * xs[:, gs], 0, keepdims=True)
            dyv_ref[:, gs] = dy
            dy_b = dy.astype(BF)

            bg = xbc[:, SSD_INNER + g * SSD_STATE:SSD_INNER + (g + 1) * SSD_STATE].astype(BF)
            cg = xbc[:, SSD_INNER + (2 + g) * SSD_STATE:SSD_INNER + (3 + g) * SSD_STATE].astype(BF)
            scores = _nt(cg, bg)
            hg = hprev_ref[0, 0, g]
            hg_b = hg.astype(BF)
            gg = g_ref[g]
            gg_b = gg.astype(BF)
            e_a = r["e_a"][:, gs]
            d_out = r["d_out"][:, gs]
            c_dec = r["c_dec"][:, gs]
            zc = _nn(cg, hg_b)
            wv = e_a * dy
            wv_b = wv.astype(BF)
            da_g = head_sums(wv * zc)
            dcg = _nt(wv_b, hg_b)
            d_hprev = _tn(cg, wv_b)
            vg = _nn(bg, gg_b)
            dxdt_g = d_out * vg
            dd_out = head_sums(xdt[:, gs] * vg)
            dbg = _nt((xdt[:, gs] * d_out).astype(BF), gg_b)
            dcd = _exact_nn(jnp.sum(gg * hg, 0, keepdims=True), reduce_m)
            d_out8 = jnp.exp(r["acum"][CHUNK - 1:CHUNK, 8 * g:8 * g + 8] - r["acum"][:, 8 * g:8 * g + 8])
            c_dec8 = jnp.exp(r["acum"][CHUNK - 1:CHUNK, 8 * g:8 * g + 8])
            t8 = dd_out * d_out8
            d_alast = d_alast + _exact_nn(jnp.sum(t8, 0, keepdims=True) + dcd * c_dec8, place8(g))
            d_acum = d_acum + _exact_nn(da_g - t8, place8(g))
            dsc = jnp.zeros((CHUNK, CHUNK), F32)
            for hh in range(8):
                h = g * 8 + hh
                hs = slice(h * SSD_HEAD_DIM, (h + 1) * SSD_HEAD_DIM)
                lam = _head_decay(r, h)
                m = scores * lam
                dyh_b = dy_b[:, hh * SSD_HEAD_DIM:(hh + 1) * SSD_HEAD_DIM]
                dm = _nt(dyh_b, xdt_b[:, hs])
                tm_ = dm * m
                d_acum = d_acum + jnp.sum(tm_, 1, keepdims=True) * onehot16(h)
                d_acum_t = d_acum_t + onecol16(h) * jnp.sum(tm_, 0, keepdims=True)
                dsc = dsc + dm * lam
                dxdt_ref[:, hs] = _tn(m.astype(BF), dyh_b) + dxdt_g[:, hh * SSD_HEAD_DIM:(hh + 1) * SSD_HEAD_DIM]
            dsc_b = dsc.astype(BF)
            d_c.append(dcg + _nn(dsc_b, bg))
            d_b.append(dbg + _tn(dsc_b, cg))
            g_ref[g] = d_hprev + c_dec * gg

        eye = (_iota((CHUNK, CHUNK), 0) == _iota((CHUNK, CHUNK), 1)).astype(BF)
        d_acum = d_acum - _exact_nt_left(eye, d_acum_t)
        is_last = (_iota((CHUNK, 1), 0) == CHUNK - 1).astype(F32)
        d_acum = d_acum + is_last * d_alast
        triu = (_iota((CHUNK, CHUNK), 0) <= _iota((CHUNK, CHUNK), 1)).astype(BF)
        d_da = _exact_nn_left(triu, d_acum)
        dt = r["dt"]
        ga_ref[...] += jnp.sum(d_da * dt, 0, keepdims=True)
        dxdt = dxdt_ref[...]
        reduce16 = _head_reduce_matrix(SSD_INNER, SSD_HEADS)
        d_dt = d_da * r["a"] + _nn((dxdt * xs).astype(BF), reduce16)
        d_udt = d_dt * _sigmoid(r["dtp"])
        gdtb_ref[...] += jnp.sum(d_udt, 0, keepdims=True)
        dudt_ref[...] = jnp.zeros_like(dudt_ref)
        dudt_ref[:, 0:SSD_HEADS] = d_udt.astype(BF)
        pre, sg = r["pre"], r["sg"]
        dsilu = sg * (1.0 + pre * (1.0 - sg))
        dpre_ref[:, 0:SSD_INNER] = (dsk_ref[...] * dyv_ref[...] + dxdt * dt_e) * dsilu[:, 0:SSD_INNER]
        for g in range(2):
            bs = slice(SSD_INNER + g * SSD_STATE, SSD_INNER + (g + 1) * SSD_STATE)
            cs = slice(SSD_INNER + (2 + g) * SSD_STATE, SSD_INNER + (3 + g) * SSD_STATE)
            dpre_ref[:, bs] = d_b[g] * dsilu[:, bs]
            dpre_ref[:, cs] = d_c[g] * dsilu[:, cs]

    t = nb * seq
    vec = _const_spec((1, SSD_INNER))
    small = _const_spec((1, SSD_HEADS))
    return pl.pallas_call(
        body, name="ssd_bwd", grid=(nb, nc),
        in_specs=specs + [pl.BlockSpec((CHUNK, CONV_CH), lambda b, c: (row(b, c), 0)),
                          pl.BlockSpec((CHUNK, SSD_INNER), lambda b, c: (row(b, c), 0)),
                          pl.BlockSpec((CHUNK, SSD_INNER), lambda b, c: (row(b, c), 0)),
                          pl.BlockSpec((1, 1, 2, SSD_STATE, GROUP_W), lambda b, c: (b, cidx(c), 0, 0, 0)),
                          small, small, vec, vec],
        out_specs=[pl.BlockSpec((CHUNK, SSD_INNER), lambda b, c: (row(b, c), 0)),
                   pl.BlockSpec((CHUNK, CONV_CH), lambda b, c: (row(b, c), 0)),
                   pl.BlockSpec((CHUNK, 128), lambda b, c: (row(b, c), 0)),
                   vec, vec, small, small],
        out_shape=[_out((t, SSD_INNER), BF), _out((t, CONV_CH), F32),
                   _out((t, 128), BF), _out((1, SSD_INNER), F32),
                   _out((1, SSD_INNER), F32), _out((1, SSD_HEADS), F32),
                   _out((1, SSD_HEADS), F32)],
        scratch_shapes=[pltpu.VMEM((2, SSD_STATE, GROUP_W), F32), pltpu.VMEM((CHUNK, SSD_INNER), F32),
                        pltpu.VMEM((CHUNK, SSD_INNER), F32)],
        compiler_params=_cp(("arbitrary", "arbitrary"), 48),
    )(*_pin(proj, proj, proj, pre, d_yssd, yssm, h_prev, dt_bias, a_log, dskip_e, g_ssd))


def _grad_w_in_t(d_upool, d_z, d_uxbc, d_udt, u1):
    t, d = u1.shape
    tk = 512
    n_z, n_x = SSD_INNER // tk, CONV_CH // tk

    def body(p_ref, z_ref, x_ref, dt_ref, u_ref, o_ref):
        i = pl.program_id(0)

        @pl.when(i == 0)
        def _():
            o_ref[...] = _tn(p_ref[...], u_ref[...]).astype(BF)

        @pl.when((i >= 1) & (i < 1 + n_z))
        def _():
            o_ref[...] = _tn(z_ref[...], u_ref[...]).astype(BF)

        @pl.when((i >= 1 + n_z) & (i < 1 + n_z + n_x))
        def _():
            o_ref[...] = _tn(x_ref[...], u_ref[...]).astype(BF)

        @pl.when(i == 1 + n_z + n_x)
        def _():
            o_ref[0:128, :] = _tn(dt_ref[...], u_ref[...]).astype(BF)

    return pl.pallas_call(
        body, name="grad_w_in", grid=(2 + n_z + n_x,),
        in_specs=[pl.BlockSpec((t, tk), lambda i: (0, 0)),
                  pl.BlockSpec((t, tk), lambda i: (0, jnp.clip(i - 1, 0, n_z - 1))),
                  pl.BlockSpec((t, tk), lambda i: (0, jnp.clip(i - 1 - n_z, 0, n_x - 1))),
                  pl.BlockSpec((t, 128), lambda i: (0, 0)), pl.BlockSpec((t, d), lambda i: (0, 0))],
        out_specs=pl.BlockSpec((tk, d), lambda i: (i, 0)),
        out_shape=_out((IN_PAD, d), BF),
        compiler_params=_cp(("parallel",), 56))(*_pin(d_upool, d_z, d_uxbc, d_udt, u1))


def _conv_bwd(d_pre, proj, conv_w, nb, seq):
    ts = min(256, seq)
    nt = seq // ts
    hb = ts // CONV_HALO
    last_block = nb * seq // CONV_HALO - 1
    n_ext = CHUNK + CONV_HALO

    def body(dp_ref, dnext_ref, u_ref, cw_ref, du_ref, gw_ref, gb_ref):
        b = pl.program_id(0)
        i = pl.program_id(1)

        @pl.when((b == 0) & (i == 0))
        def _():
            gw_ref[...] = jnp.zeros_like(gw_ref)
            gb_ref[...] = jnp.zeros_like(gb_ref)

        for c0 in range(0, CONV_CH, 128):
            cs = slice(c0, c0 + 128)
            cw = cw_ref[:, cs]
            gw = [0.0] * 4
            gb = 0.0
            for r0 in range(0, ts, CHUNK):
                dp = dp_ref[r0:r0 + CHUNK, cs]
                u = u_ref[r0:r0 + CHUNK, cs]
                if r0 + CHUNK < ts:
                    below = dp_ref[r0 + CHUNK:r0 + CHUNK + CONV_HALO, cs]
                else:
                    below = jnp.where(i == nt - 1, 0.0, dnext_ref[:, cs])
                ext_d = jnp.concatenate([dp, below], 0)
                du = dp * cw[3:4]
                gw[3] = gw[3] + jnp.sum(dp * u, 0, keepdims=True)
                for k in (2, 1, 0):
                    shifted = pltpu.roll(ext_d, n_ext - (3 - k), 0)[:CHUNK]
                    du = du + shifted * cw[k:k + 1]
                    gw[k] = gw[k] + jnp.sum(shifted * u, 0, keepdims=True)
                gb = gb + jnp.sum(dp, 0, keepdims=True)
                du_ref[r0:r0 + CHUNK, cs] = du.astype(BF)
            for k in range(4):
                gw_ref[k:k + 1, cs] += gw[k]
            gb_ref[:, cs] += gb

    return pl.pallas_call(
        body, name="conv_bwd", grid=(nb, nt),
        in_specs=[pl.BlockSpec((ts, CONV_CH), lambda b, i: (b * nt + i, 0)),
                  pl.BlockSpec((CONV_HALO, CONV_CH), lambda b, i: (jnp.minimum((b * nt + i + 1) * hb, last_block), 0)),
                  pl.BlockSpec((ts, CONV_CH), lambda b, i: (b * nt + i, 1)),
                  pl.BlockSpec((4, CONV_CH), lambda b, i: (0, 0))],
        out_specs=[pl.BlockSpec((ts, CONV_CH), lambda b, i: (b * nt + i, 0)),
                   pl.BlockSpec((8, CONV_CH), lambda b, i: (0, 0)), pl.BlockSpec((1, CONV_CH), lambda b, i: (0, 0))],
        out_shape=[_out((nb * seq, CONV_CH), BF), _out((8, CONV_CH), F32),
                   _out((1, CONV_CH), F32)],
        compiler_params=_cp(("arbitrary", "arbitrary"), 48))(*_pin(d_pre, d_pre, proj, conv_w))


def _in_proj_bwd(d_parts, w_in_t, x, dh1, mod3, g_mix, seq, token):
    t, d = x.shape
    nb = t // seq
    tm = min(ROW_TILE, seq)
    tps = seq // tm

    widths = [p.shape[1] for p in d_parts]

    def body(d0_ref, d1_ref, d2_ref, d3_ref, w_ref, x_ref, dh1_ref, mod_ref, g_ref, tok_ref, gx_ref, acc_ref, gg_ref):
        i = pl.program_id(0)

        @pl.when(i == 0)
        def _():
            gg_ref[...] = jnp.zeros_like(gg_ref)

        @pl.when(i % tps == 0)
        def _():
            acc_ref[...] = jnp.zeros_like(acc_ref)

        gg = a_shift = a_scale = 0.0
        for rows in _sub_rows(tm):
            du = None
            off = 0
            for p_ref, wd in zip((d0_ref, d1_ref, d2_ref, d3_ref), widths):
                part = _nn(p_ref[rows, :], w_ref[off:off + wd, :])
                du = part if du is None else du + part
                off += wd
            xv = x_ref[rows, :]
            r = lax.rsqrt(jnp.mean(xv * xv, -1, keepdims=True) + EPS)
            hh = xv * r
            n1 = hh * g_ref[...]
            dn1 = du * (1.0 + mod_ref[0, 1:2, :])
            dhat = dn1 * g_ref[...]
            gx_ref[rows, :] = dh1_ref[rows, :] + r * (dhat - hh * jnp.mean(dhat * hh, -1, keepdims=True))
            gg = gg + jnp.sum(dn1 * hh, 0, keepdims=True)
            a_shift = a_shift + jnp.sum(du, 0, keepdims=True)
            a_scale = a_scale + jnp.sum(du * n1, 0, keepdims=True)
        gg_ref[...] += gg
        acc_ref[0, 0:1, :] += a_shift
        acc_ref[0, 1:2, :] += a_scale

    row = lambda i: (i, 0)
    vec = pl.BlockSpec((1, d), lambda i: (0, 0))
    return pl.pallas_call(
        body, name="in_proj_bwd", grid=(t // tm,),
        in_specs=[pl.BlockSpec((tm, wd), row) for wd in widths] +
                 [pl.BlockSpec(w_in_t.shape, lambda i: (0, 0)), pl.BlockSpec((tm, d), row),
                  pl.BlockSpec((tm, d), row), pl.BlockSpec((1, N_MOD, d), lambda i: (i // tps, 0, 0)), vec, _token_spec()],
        out_specs=[pl.BlockSpec((tm, d), row), pl.BlockSpec((1, 8, d), lambda i: (i // tps, 0, 0)), vec],
        out_shape=[_out((t, d), F32), _out((nb, 8, d), F32),
                   _out((1, d), F32)],
        compiler_params=_cp(("arbitrary",), 48))(*_pin(*d_parts, w_in_t, x, dh1, mod3, g_mix, token))


_VEC_LAYOUT = (("g_mix", 1024), ("conv_b", 1536), ("g_ssd", 1024), ("pool_scale", 512), ("g_mlp", 1024),
               ("g_final", 1024), ("dt_bias", 128), ("a_log", 128), ("d_skip_lanes", 1024), ("sq_err", 1024))
_VEC_OFFSET = {}
_off = 0
for _name, _n in _VEC_LAYOUT:
    _VEC_OFFSET[_name] = _off
    _off += _n
_VEC_LANES = _off
_SMALL_PARAMS = ("b_ada", "g_mix", "conv_w", "conv_b", "dt_bias", "a_log", "d_skip", "g_ssd", "w_pool", "pool_scale",
                 "g_mlp", "g_final")


def _pack_vec(parts):
    cols = []
    for name, n in _VEC_LAYOUT:
        v = parts[name]
        if v.shape[1] < n:
            v = jnp.pad(v, ((0, 0), (0, n - v.shape[1])))
        cols.append(v)
    return jnp.concatenate(cols, 1)


def _small_adam(vec_all, wpool_all, convw_all, dmod_all, params):
    names = _SMALL_PARAMS
    nin = 4 + 3 * len(names)

    def body(*refs):
        vec_ref, wp_ref, cw_ref, dm_ref = refs[:4]
        prm = {n: refs[4 + 3 * i:7 + 3 * i] for i, n in enumerate(names)}
        loss_ref = refs[nin]
        outs = {n: refs[nin + 1 + 4 * i:nin + 5 + 4 * i] for i, n in enumerate(names)}
        vsum = vec_ref[0]
        for s in range(1, N_DEV):
            vsum = vsum + vec_ref[s]

        def lanes(name, n):
            off = _VEC_OFFSET[name]
            return vsum[:, off:off + n]

        grads = {n: lanes(n, prm[n][0].shape[1]) for n in ("g_mix", "conv_b", "g_ssd", "pool_scale", "g_mlp", "g_final", "dt_bias")}
        grads["a_log"] = lanes("a_log", SSD_HEADS) * (-jnp.exp(prm["a_log"][0][...]))
        per_lane = jnp.broadcast_to(lanes("d_skip_lanes", SSD_INNER), (8, SSD_INNER))
        grads["d_skip"] = _exact_nn(per_lane, _head_reduce_matrix(SSD_INNER, SSD_HEADS))[0:1]
        gwp = wp_ref[0].astype(F32)
        gcw = cw_ref[0]
        gb = jnp.sum(dm_ref[0], 0, keepdims=True)
        for s in range(1, N_DEV):
            gwp = gwp + wp_ref[s].astype(F32)
            gcw = gcw + cw_ref[s]
            gb = gb + jnp.sum(dm_ref[s], 0, keepdims=True)
        grads["w_pool"] = gwp
        grads["conv_w"] = gcw[0:4]
        grads["b_ada"] = gb
        total = jnp.sum(lanes("sq_err", D_MODEL), 1, keepdims=True) * (0.5 / D_MODEL)
        loss_ref[...] = jnp.broadcast_to(total, loss_ref.shape)
        for n in names:
            w_ref, m_ref, v_ref = prm[n]
            g = grads[n]
            d, m2, v2 = _adam_math(w_ref[...], g, m_ref[...], v_ref[...])
            g_ref, d_ref, m2_ref, v2_ref = outs[n]
            g_ref[...] = g
            d_ref[...] = d
            m2_ref[...] = m2
            v2_ref[...] = v2

    flat = [vec_all, wpool_all, convw_all, dmod_all]
    out_shape = [jax.ShapeDtypeStruct((1, 128), F32)]
    for n in names:
        flat += list(params[n])
        out_shape += [jax.ShapeDtypeStruct(params[n][0].shape, F32)] * 4
    vm = pl.BlockSpec(memory_space=pltpu.VMEM)
    res = pl.pallas_call(body, name="small_adam", out_shape=out_shape, in_specs=[vm] * len(flat),
                         out_specs=[vm] * len(out_shape), compiler_params=_cp(vmem_mb=48))(*flat)
    return res[0], {n: res[1 + 4 * i:5 + 4 * i] for i, n in enumerate(names)}


_WEIGHTS = ("w_ada", "b_ada", "g_mix", "w_in", "conv_w", "conv_b", "dt_bias", "a_log", "d_skip", "g_ssd", "w_pool",
            "pool_scale", "w_out", "g_mlp", "w_up", "w_down", "g_final")


def _local_step(x2, tg2, mod3, seq, w_in_t, first_token, weights_later, start_reduce, conv_w_full, sp):
    t, d = x2.shape
    nb = t // seq
    dskip_e = jnp.repeat(sp["d_skip"], SSD_HEAD_DIM, axis=1)
    proj, u1 = _in_proj(x2, mod3, sp["g_mix"], w_in_t, seq, first_token)
    y_pool, p = _pool_fwd(proj, sp["w_pool"], sp["pool_scale"], nb, seq)
    y_ssd, yssm, h_prev, pre = _ssd_fwd(proj, conv_w_full, sp["conv_b"], sp["dt_bias"], sp["a_log"], dskip_e, sp["g_ssd"], nb, seq)
    w_out_f, w_up4, w_down4 = weights_later(y_ssd)
    w_down_f = w_down4.reshape(D_FF, d)
    h1, o, u2 = _out_proj(y_pool, y_ssd, w_out_f, x2, mod3, sp["g_mlp"], seq)
    a_up = _mlp_up(u2, w_up4)
    d_dn, dh2, sq, gg_final, d_gf = _mlp_down_loss(a_up, w_down_f, h1, mod3, sp["g_final"], tg2, seq)

    gw_down = _tn_matmul(a_up, d_dn, 512, d, "grad_w_down", square_relu=True)
    tok = start_reduce("w_down", gw_down.reshape(N_CHIPS, D_FF // N_CHIPS, d))
    d_a = _mlp_down_bwd(d_dn, w_down4, a_up, tok)
    gw_up4 = _tn_matmul(u2, d_a, 512, d, "grad_w_up", out3=True)
    tok = start_reduce("w_up", gw_up4)
    dh1, d_o, accf, gg_mlp = _mlp_up_bwd(d_a, w_up4, h1, dh2, o, mod3, sp["g_mlp"], seq, tok)
    gw_out_pool = _tn_matmul(y_pool, d_o, 512, d, "grad_w_out_pool")
    gw_out_ssd = _tn_matmul(y_ssd, d_o, 512, d, "grad_w_out_ssd")
    gw_out = jnp.concatenate([gw_out_pool, gw_out_ssd], 0)
    tok = start_reduce("w_out", gw_out.reshape(N_CHIPS, gw_out.shape[0] // N_CHIPS, d))
    d_ypool, d_yssd = _out_proj_bwd(d_o, w_out_f, tok)
    d_upool, gw_pool, g_ps = _pool_bwd(d_ypool, p, sp["w_pool"], sp["pool_scale"], nb, seq)
    d_z, d_pre, d_udt, gg_ssd, gdsk, ga, gdtb = _ssd_bwd(proj, pre, d_yssd, yssm, h_prev, sp["dt_bias"], sp["a_log"],
                                                        dskip_e, sp["g_ssd"], nb, seq)
    d_uxbc, gconvw, gconvb = _conv_bwd(d_pre, proj, conv_w_full, nb, seq)
    gw_in_t = _grad_w_in_t(d_upool, d_z, d_uxbc, d_udt, u1)
    tok = start_reduce("w_in", gw_in_t[:IN_WIDTH].reshape(N_CHIPS, IN_WIDTH // N_CHIPS, d))
    gx, accm, gg_mix = _in_proj_bwd([d_upool, d_z, d_uxbc, d_udt], w_in_t, x2, dh1, mod3, sp["g_mix"], seq, tok)

    d_mod = jnp.concatenate([accm[:, 0], accm[:, 1], accf[:, 2], accf[:, 0], accf[:, 1], d_gf[:, 0]], 1)
    vec = _pack_vec({"g_mix": gg_mix, "conv_b": gconvb, "g_ssd": gg_ssd, "pool_scale": g_ps, "g_mlp": gg_mlp,
                     "g_final": gg_final, "dt_bias": gdtb, "a_log": ga, "d_skip_lanes": gdsk, "sq_err": sq})
    return gx, d_mod, vec, gw_pool, gconvw


def kernel(x, c, w_ada, b_ada, g_mix, w_in, conv_w, conv_b, dt_bias, a_log, d_skip, g_ssd, w_pool, pool_scale, w_out, g_mlp, w_up, w_down, g_final, loss_target, m_w_ada, m_b_ada, m_g_mix, m_w_in, m_conv_w, m_conv_b, m_dt_bias, m_a_log, m_d_skip, m_g_ssd, m_w_pool, m_pool_scale, m_w_out, m_g_mlp, m_w_up, m_w_down, m_g_final, v_w_ada, v_b_ada, v_g_mix, v_w_in, v_conv_w, v_conv_b, v_dt_bias, v_a_log, v_d_skip, v_g_ssd, v_w_pool, v_pool_scale, v_w_out, v_g_mlp, v_w_up, v_w_down, v_g_final):
    nb, seq, d = x.shape
    t = nb * seq
    xi, yi, ci = _mesh_pos()
    chip = 2 * xi + yi
    me = 4 * xi + 2 * yi + ci
    ada_cols = w_ada.shape[2]
    conv_cols = conv_w.shape[2]
    in_cols = w_in.shape[2]
    w_in_s, m_w_in_s, v_w_in_s = w_in[0].T, m_w_in[0].T, v_w_in[0].T

    w_in_b = w_in_s.astype(BF)
    i_send, i_recv, i_src, i_land, in_token = _ici_start(
        [w_in_b], [jax.ShapeDtypeStruct((N_CHIPS,) + w_in_b.shape, BF)], _gather_sent, _gather_landing, "gather_start_w_in",
        after=w_in_b)

    c8, convw8 = _all_gather_small([c, conv_w[0]], token=in_token)
    c_all = c8.reshape(N_DEV * nb, d)
    conv_w_full = convw8[0::2].transpose(1, 0, 2).reshape(4, N_CHIPS * conv_cols)
    b_shard = lax.dynamic_slice(b_ada, (0, chip * ada_cols), (1, ada_cols))
    mod_part, c_act = _ada_mod(c_all, w_ada[0], b_shard)
    (mod8,) = _all_gather_small([mod_part])
    mod_all = mod8[0::2].transpose(1, 0, 2).reshape(N_DEV * nb, N_CHIPS * ada_cols)
    mod3 = lax.dynamic_slice(mod_all, (nb * me, 0), (nb, N_CHIPS * ada_cols)).reshape(nb, N_MOD, d)

    in_shard, in_land = _ici_wait(i_send, i_recv, i_src, i_land, mod8, _gather_sent, _gather_landing, "gather_wait_w_in")
    (w_in4,) = _gather_finish(in_land, in_shard)
    w_in_t = jnp.pad(w_in4.reshape(N_CHIPS * in_cols, d), ((0, IN_PAD - N_CHIPS * in_cols), (0, 0)))
    later = [w_out[0].astype(BF), w_up[0].astype(BF), w_down[0].astype(BF)]
    g_send, g_recv, g_src, g_land, first_token = _ici_start(
        later, [jax.ShapeDtypeStruct((N_CHIPS,) + s.shape, BF) for s in later], _gather_sent, _gather_landing, "gather_start",
        after=w_in4)

    def weights_later(after):
        shards, lands = _ici_wait(g_send, g_recv, g_src, g_land, after, _gather_sent, _gather_landing, "gather_wait")
        w_out4, w_up4, w_down4 = _gather_finish(lands, shards)
        return w_out4.reshape(N_CHIPS * w_out.shape[1], d), w_up4, w_down4

    pending = {}

    def start_reduce(name, grad4):
        pending[name] = _reduce_start(grad4, "reduce_start_" + name)
        return pending[name][4]

    sp = dict(g_mix=g_mix, conv_b=conv_b, dt_bias=dt_bias, a_log=a_log, d_skip=d_skip, g_ssd=g_ssd,
              w_pool=w_pool[0], pool_scale=pool_scale, g_mlp=g_mlp, g_final=g_final.reshape(1, d))
    gx, d_mod, vec, gw_pool, gconvw = _local_step(
        x.reshape(t, d), loss_target.reshape(t, d), mod3, seq, w_in_t, first_token, weights_later, start_reduce, conv_w_full, sp)

    pos = jnp.stack([ci, chip, me]).astype(jnp.int32)
    small_parts = [vec, gw_pool.reshape(4 * POOL_GROUP, POOL_GROUP).astype(BF), gconvw, d_mod]
    s_send, s_recv, s_src, s_land, s_token = _exchange_start(small_parts, ALL_PEERS, "small_start")
    halves = []
    for name in ("w_in", "w_out", "w_up", "w_down"):
        r_send, r_recv, r_src, r_land, _ = pending[name]
        own, recv = _reduce_wait(r_send, r_recv, r_src, r_land, s_token, "reduce_wait_" + name)
        halves.append(_sum_eight(recv, own, pos))
    h_send, h_recv, h_src, h_land, h_token = _exchange_start(halves, SIBLING, "halves_start")

    s_own, s_got = _exchange_wait(s_send, s_recv, s_src, s_land, ALL_PEERS, h_token, "small_wait")
    vec8, wpool8, convw8g, dmod8 = [lax.dynamic_update_slice(got, mine[None], (me,) + (0,) * mine.ndim)
                                    for got, mine in zip(s_got, s_own)]
    convw8s = lax.dynamic_slice(convw8g, (0, 0, chip * conv_cols), (N_DEV, 8, conv_cols))
    m_in = dict(b_ada=m_b_ada, g_mix=m_g_mix, conv_w=m_conv_w[0], conv_b=m_conv_b, dt_bias=m_dt_bias, a_log=m_a_log,
                d_skip=m_d_skip, g_ssd=m_g_ssd, w_pool=m_w_pool.reshape(4 * POOL_GROUP, POOL_GROUP), pool_scale=m_pool_scale,
                g_mlp=m_g_mlp, g_final=m_g_final.reshape(1, d))
    v_in = dict(b_ada=v_b_ada, g_mix=v_g_mix, conv_w=v_conv_w[0], conv_b=v_conv_b, dt_bias=v_dt_bias, a_log=v_a_log,
                d_skip=v_d_skip, g_ssd=v_g_ssd, w_pool=v_w_pool.reshape(4 * POOL_GROUP, POOL_GROUP), pool_scale=v_pool_scale,
                g_mlp=v_g_mlp, g_final=v_g_final.reshape(1, d))
    w_small = dict(sp, b_ada=b_ada, conv_w=conv_w[0], w_pool=w_pool.reshape(4 * POOL_GROUP, POOL_GROUP))
    loss_row, small = _small_adam(vec8, wpool8, convw8s, dmod8, {n: (w_small[n], m_in[n], v_in[n]) for n in _SMALL_PARAMS})

    dmod_all = dmod8.reshape(N_DEV * nb, N_CHIPS * ada_cols)
    dmod_cols = lax.dynamic_slice(dmod_all, (0, chip * ada_cols), (N_DEV * nb, ada_cols))
    res = {n: tuple(r.reshape(w.shape) for r in small[n])
           for n, w in (("b_ada", b_ada), ("g_mix", g_mix), ("conv_w", conv_w), ("conv_b", conv_b), ("dt_bias", dt_bias),
                        ("a_log", a_log), ("d_skip", d_skip), ("g_ssd", g_ssd), ("w_pool", w_pool), ("pool_scale", pool_scale),
                        ("g_mlp", g_mlp), ("g_final", g_final))}
    g_ada, d_ada, m_ada, v_ada = _adam_ada(c_act.T.astype(BF), dmod_cols, w_ada[0], m_w_ada[0], v_w_ada[0])
    res["w_ada"] = (g_ada[None], d_ada[None], m_ada[None], v_ada[None])
    h_own, h_got = _exchange_wait(h_send, h_recv, h_src, h_land, SIBLING, g_ada, "halves_wait")
    g_in, dl, m2, v2 = _adam_big(h_own[0], h_got[0], w_in_s, m_w_in_s, v_w_in_s, pos)
    res["w_in"] = (g_in.T[None], dl.T[None], m2.T[None], v2.T[None])
    for i, (n, w, m, v) in enumerate((("w_out", w_out, m_w_out, v_w_out), ("w_up", w_up, m_w_up, v_w_up),
                                      ("w_down", w_down, m_w_down, v_w_down))):
        g, dl, m2, v2 = _adam_big(h_own[i + 1], h_got[i + 1], w[0], m[0], v[0], pos)
        res[n] = (g[None], dl[None], m2[None], v2[None])

    loss = loss_row[0, 0]
    return (loss, gx.reshape(nb, seq, d), *[res[n][0] for n in _WEIGHTS], *[res[n][1] for n in _WEIGHTS],
            *[res[n][2] for n in _WEIGHTS], *[res[n][3] for n in _WEIGHTS])
```

```python
import functools

import jax
import jax.numpy as jnp
from jax import lax
from jax.experimental import pallas as pl
from jax.experimental.pallas import tpu as pltpu

F32 = jnp.float32
BF = jnp.bfloat16
MESH = pl.DeviceIdType.MESH

EPS = 1e-5
D_MODEL = 1024
POOL_WIDTH = 512
POOL_WINDOWS = (2, 4, 8, 16)
POOL_GROUP = 128
SSD_INNER = 1024
SSD_HEADS = 16
SSD_HEAD_DIM = 64
SSD_STATE = 128
GROUP_W = 512
CHUNK = 128
CONV_CH = 1536
OFF_Z = 512
OFF_XBC = 1536
OFF_DT = 3072
IN_WIDTH = 3088
IN_PAD = 3200
D_FF = 4096
N_MOD = 6
N_CHIPS = 4
N_DEV = 8
HALO = 16
CONV_HALO = 8

ADAM_LR = 0.001
ADAM_B1 = 0.9
ADAM_B2 = 0.999
ADAM_EPS = 1e-08
ADAM_WD = 0.01
ADAM_STEP = 10

VMEM_BYTES_V7X = 64 * 1024 * 1024


def _cp(semantics=None, vmem_mb=48, **kw):
    args = dict(vmem_limit_bytes=vmem_mb * 1024 * 1024, **kw)
    if semantics is not None:
        args["dimension_semantics"] = semantics
    return pltpu.CompilerParams(**args)


def _out(shape, dtype):
    return pltpu.HBM(shape, dtype)


def _pin(*arrays):
    return [pltpu.with_memory_space_constraint(a, pltpu.HBM) for a in arrays]


def _nn(a, b):
    return jnp.dot(a, b, preferred_element_type=F32)


def _nt(a, b):
    return lax.dot_general(a, b, (((1,), (1,)), ((), ())), preferred_element_type=F32)


def _tn(a, b):
    return lax.dot_general(a, b, (((0,), (0,)), ((), ())), preferred_element_type=F32)


def _split3(v):
    hi = v.astype(BF)
    r1 = v - hi.astype(F32)
    mid = r1.astype(BF)
    lo = (r1 - mid.astype(F32)).astype(BF)
    return hi, mid, lo


def _exact_nn(v, m01):
    hi, mid, lo = _split3(v)
    return _nn(hi, m01) + _nn(mid, m01) + _nn(lo, m01)


def _exact_nn_left(m01, v):
    hi, mid, lo = _split3(v)
    return _nn(m01, hi) + _nn(m01, mid) + _nn(m01, lo)


def _exact_nt_left(m01, v):
    hi, mid, lo = _split3(v)
    return _nt(m01, hi) + _nt(m01, mid) + _nt(m01, lo)


def _sigmoid(v):
    return 1.0 / (1.0 + jnp.exp(-v))


def _iota(shape, dim):
    return lax.broadcasted_iota(jnp.int32, shape, dim)


def _head_expand_matrix(heads, width):
    return (_iota((heads, width), 1) // SSD_HEAD_DIM == _iota((heads, width), 0)).astype(BF)


def _head_reduce_matrix(width, heads):
    return (_iota((width, heads), 0) // SSD_HEAD_DIM == _iota((width, heads), 1)).astype(BF)


def _mesh_pos():
    return lax.axis_index("x"), lax.axis_index("y"), lax.axis_index("c")


def _flip(v, bit):
    return v + bit - 2 * bit * v


def _all_gather_small(arrays, token=None):
    n = len(arrays)
    extra = [] if token is None else [token]

    def body(*refs):
        in_refs, out_refs = refs[:n], refs[n + len(extra):2 * n + len(extra)]
        send_sems, recv_sems, local_sems = refs[2 * n + len(extra):]
        x, y, c = _mesh_pos()
        me = 4 * x + 2 * y + c
        local = []
        for a in range(n):
            cp = pltpu.make_async_copy(in_refs[a], out_refs[a].at[me], local_sems.at[a])
            cp.start()
            local.append(cp)
        sends = []
        for k in range(1, N_DEV):
            peer = (_flip(x, (k >> 2) & 1), _flip(y, (k >> 1) & 1), _flip(c, k & 1))
            for a in range(n):
                cp = pltpu.make_async_remote_copy(
                    src_ref=in_refs[a], dst_ref=out_refs[a].at[me],
                    send_sem=send_sems.at[a, k], recv_sem=recv_sems.at[a, k],
                    device_id=peer, device_id_type=MESH)
                cp.start()
                sends.append(cp)
        for k in range(1, N_DEV):
            px, py, pc = _flip(x, (k >> 2) & 1), _flip(y, (k >> 1) & 1), _flip(c, k & 1)
            src = 4 * px + 2 * py + pc
            for a in range(n):
                pltpu.make_async_remote_copy(
                    src_ref=in_refs[a], dst_ref=out_refs[a].at[src],
                    send_sem=send_sems.at[a, k], recv_sem=recv_sems.at[a, k],
                    device_id=(px, py, pc), device_id_type=MESH).wait_recv()
        for cp in sends:
            cp.wait_send()
        for cp in local:
            cp.wait()

    vm = pl.BlockSpec(memory_space=pltpu.VMEM)
    return pl.pallas_call(
        body, name="all_gather_small",
        out_shape=[jax.ShapeDtypeStruct((N_DEV,) + a.shape, a.dtype) for a in arrays],
        in_specs=[vm] * (n + len(extra)), out_specs=[vm] * n,
        scratch_shapes=[pltpu.SemaphoreType.DMA((n, N_DEV)), pltpu.SemaphoreType.DMA((n, N_DEV)),
                        pltpu.SemaphoreType.DMA((n,))],
        compiler_params=_cp(vmem_mb=32),
    )(*arrays, *extra)


def _weight_all_gather(shards):
    n = len(shards)
    any_spec = _HBM

    def body(*refs):
        in_refs, out_refs = refs[:n], refs[n:2 * n]
        send_sems, recv_sems, fsend_sems, frecv_sems, local_sems = refs[2 * n:]
        x, y, c = _mesh_pos()
        chip = 2 * x + y
        local, sends = [], []
        for a in range(n):
            cp = pltpu.make_async_copy(in_refs[a], out_refs[a].at[chip], local_sems.at[a])
            cp.start()
            local.append(cp)

        def half(a, which):
            hc = shards[a].shape[1] // 2
            return pl.ds(pl.multiple_of(which * hc, 128), hc)

        for j in range(1, N_CHIPS):
            px, py = _flip(x, (j >> 1) & 1), _flip(y, j & 1)
            for a in range(n):
                cp = pltpu.make_async_remote_copy(
                    src_ref=in_refs[a].at[:, half(a, c)], dst_ref=out_refs[a].at[chip, :, half(a, c)],
                    send_sem=send_sems.at[a, j], recv_sem=recv_sems.at[a, j],
                    device_id=(px, py, c), device_id_type=MESH)
                cp.start()
                sends.append(cp)
        for j in range(1, N_CHIPS):
            px, py = _flip(x, (j >> 1) & 1), _flip(y, j & 1)
            src = 2 * px + py
            for a in range(n):
                landed = out_refs[a].at[src, :, half(a, c)]
                pltpu.make_async_remote_copy(
                    src_ref=landed, dst_ref=landed, send_sem=send_sems.at[a, j], recv_sem=recv_sems.at[a, j],
                    device_id=(px, py, c), device_id_type=MESH).wait_recv()
                cp = pltpu.make_async_remote_copy(
                    src_ref=landed, dst_ref=landed, send_sem=fsend_sems.at[a, j], recv_sem=frecv_sems.at[a, j],
                    device_id=(x, y, 1 - c), device_id_type=MESH)
                cp.start()
                sends.append(cp)
        for j in range(1, N_CHIPS):
            px, py = _flip(x, (j >> 1) & 1), _flip(y, j & 1)
            src = 2 * px + py
            for a in range(n):
                other = out_refs[a].at[src, :, half(a, 1 - c)]
                pltpu.make_async_remote_copy(
                    src_ref=other, dst_ref=other, send_sem=fsend_sems.at[a, j], recv_sem=frecv_sems.at[a, j],
                    device_id=(x, y, 1 - c), device_id_type=MESH).wait_recv()
        for cp in sends:
            cp.wait_send()
        for cp in local:
            cp.wait()

    return pl.pallas_call(
        body, name="weight_all_gather",
        out_shape=[_out((N_CHIPS,) + s.shape, s.dtype) for s in shards],
        in_specs=[any_spec] * n, out_specs=[any_spec] * n,
        scratch_shapes=[pltpu.SemaphoreType.DMA((n, N_CHIPS))] * 4 + [pltpu.SemaphoreType.DMA((n,))],
        compiler_params=_cp(vmem_mb=16),
    )(*shards)


_HBM = pl.BlockSpec(memory_space=pltpu.HBM)
_SEM = pl.BlockSpec(memory_space=pltpu.SEMAPHORE)
_DATAFLOW = pltpu.SideEffectType.DATAFLOW_SIDE_EFFECTING


def _peer_chip(x, y, j):
    return _flip(x, (j >> 1) & 1), _flip(y, j & 1)


def _ici_start(srcs, land_shapes, sent, landing, name, after):
    n = len(srcs)

    def body(*refs):
        src_refs, land_refs = refs[:n], refs[n:2 * n]
        send_sems, recv_sems = refs[2 * n + 1], refs[2 * n + 2]
        token = refs[-1]
        x, y, c = _mesh_pos()
        for j in range(1, N_CHIPS):
            px, py = _peer_chip(x, y, j)
            for a in range(n):
                pltpu.make_async_remote_copy(
                    src_ref=sent(src_refs[a], c, 2 * px + py), dst_ref=landing(land_refs[a], c, 2 * x + y),
                    send_sem=send_sems.at[a * (N_CHIPS - 1) + j - 1], recv_sem=recv_sems.at[a * (N_CHIPS - 1) + j - 1],
                    device_id=(px, py, c), device_id_type=MESH).start()
        token[...] = jnp.zeros_like(token)

    sems = pltpu.SemaphoreType.DMA((n * (N_CHIPS - 1),))
    lands = [pltpu.with_memory_space_constraint(lax.empty(s.shape, s.dtype), pltpu.HBM) for s in land_shapes]
    outs = pl.pallas_call(
        body, name=name,
        out_shape=(sems, sems, *[pltpu.HBM(s.shape, s.dtype) for s in srcs],
                   *[pltpu.HBM(s.shape, s.dtype) for s in land_shapes], jax.ShapeDtypeStruct((8, 128), F32)),
        in_specs=[_HBM] * (2 * n + 1), out_specs=[_SEM, _SEM] + [_HBM] * (2 * n) + [pl.BlockSpec(memory_space=pltpu.VMEM)],
        input_output_aliases={i: 2 + i for i in range(2 * n)},
        compiler_params=pltpu.CompilerParams(has_side_effects=_DATAFLOW),
    )(*_pin(*srcs), *lands, *_pin(after))
    return outs[0], outs[1], outs[2:2 + n], outs[2 + n:2 + 2 * n], outs[-1]


def _ici_wait(send_sems, recv_sems, src_thru, land_thru, after, sent, landing, name):
    n = len(src_thru)

    def body(*refs):
        src_refs, land_refs = refs[:n], refs[n:2 * n]
        send_sems, recv_sems = refs[2 * n], refs[2 * n + 1]
        x, y, c = _mesh_pos()
        for j in range(1, N_CHIPS):
            px, py = _peer_chip(x, y, j)
            for a in range(n):
                cp = pltpu.make_async_remote_copy(
                    src_ref=sent(src_refs[a], c, 2 * px + py), dst_ref=landing(land_refs[a], c, 2 * px + py),
                    send_sem=send_sems.at[a * (N_CHIPS - 1) + j - 1], recv_sem=recv_sems.at[a * (N_CHIPS - 1) + j - 1],
                    device_id=(px, py, c), device_id_type=MESH)
                cp.wait_send()
                cp.wait_recv()

    outs = pl.pallas_call(
        body, name=name,
        out_shape=tuple(pltpu.HBM(s.shape, s.dtype) for s in (*src_thru, *land_thru)),
        in_specs=[_HBM] * (2 * n) + [_SEM, _SEM, _HBM], out_specs=[_HBM] * (2 * n),
        input_output_aliases={i: i for i in range(2 * n)},
        compiler_params=pltpu.CompilerParams(has_side_effects=_DATAFLOW),
    )(*src_thru, *land_thru, send_sems, recv_sems, *_pin(after))
    return outs[:n], outs[n:]


def _col_half(ref, which, lead=()):
    hc = ref.shape[-1] // 2
    return ref.at[(*lead, slice(None), pl.ds(pl.multiple_of(which * hc, 128), hc))]


def _gather_sent(ref, c, dst_chip):
    return _col_half(ref, c)


def _gather_landing(ref, c, src_chip):
    return _col_half(ref, c, lead=(src_chip,))


def _reduce_copy(src_ref, land_ref, send_sems, recv_sems, k, receiving):
    x, y, c = _mesh_pos()
    px, py, pc = _flip(x, (k >> 2) & 1), _flip(y, (k >> 1) & 1), _flip(c, k & 1)
    hc = src_ref.shape[2] // 2
    src = src_ref.at[2 * px + py, :, pl.ds(pl.multiple_of(pc * hc, 128), hc)]
    slot = (4 * px + 2 * py + pc) if receiving else (4 * x + 2 * y + c)
    return pltpu.make_async_remote_copy(
        src_ref=src, dst_ref=land_ref.at[slot], send_sem=send_sems.at[k - 1], recv_sem=recv_sems.at[k - 1],
        device_id=(px, py, pc), device_id_type=MESH)


def _reduce_start(grad4, name):
    k4, r, cols = grad4.shape

    def body(src_ref, land_ref, send_sems, recv_sems, src_thru, land_thru, token):
        for k in range(1, N_DEV):
            _reduce_copy(src_ref, land_ref, send_sems, recv_sems, k, receiving=False).start()
        token[...] = jnp.zeros_like(token)

    sems = pltpu.SemaphoreType.DMA((N_DEV - 1,))
    land = pltpu.with_memory_space_constraint(lax.empty((N_DEV, r, cols // 2), grad4.dtype), pltpu.HBM)
    return pl.pallas_call(
        body, name=name,
        out_shape=(sems, sems, pltpu.HBM(grad4.shape, grad4.dtype), pltpu.HBM(land.shape, land.dtype),
                   jax.ShapeDtypeStruct((8, 128), F32)),
        in_specs=[_HBM, _HBM], out_specs=[_SEM, _SEM, _HBM, _HBM, pl.BlockSpec(memory_space=pltpu.VMEM)],
        input_output_aliases={0: 2, 1: 3},
        compiler_params=pltpu.CompilerParams(has_side_effects=_DATAFLOW),
    )(*_pin(grad4), land)


def _reduce_wait(send_sems, recv_sems, src_thru, land_thru, after, name):
    def body(src_ref, land_ref, send_sems, recv_sems, after_ref, src_out, land_out):
        for k in range(1, N_DEV):
            cp = _reduce_copy(src_ref, land_ref, send_sems, recv_sems, k, receiving=True)
            cp.wait_send()
            cp.wait_recv()

    return pl.pallas_call(
        body, name=name,
        out_shape=(pltpu.HBM(src_thru.shape, src_thru.dtype), pltpu.HBM(land_thru.shape, land_thru.dtype)),
        in_specs=[_HBM, _HBM, _SEM, _SEM, _HBM], out_specs=[_HBM, _HBM],
        input_output_aliases={0: 0, 1: 1},
        compiler_params=pltpu.CompilerParams(has_side_effects=_DATAFLOW),
    )(src_thru, land_thru, send_sems, recv_sems, *_pin(after))


def _peer_copy(src_ref, land_ref, send_sems, recv_sems, idx, k, receiving):
    x, y, c = _mesh_pos()
    px, py, pc = _flip(x, (k >> 2) & 1), _flip(y, (k >> 1) & 1), _flip(c, k & 1)
    if land_ref.shape[0] == N_DEV:
        slot = (4 * px + 2 * py + pc) if receiving else (4 * x + 2 * y + c)
    else:
        slot = pc if receiving else c
    return pltpu.make_async_remote_copy(
        src_ref=src_ref, dst_ref=land_ref.at[slot], send_sem=send_sems.at[idx], recv_sem=recv_sems.at[idx],
        device_id=(px, py, pc), device_id_type=MESH)


def _exchange_start(arrays, peers, name):
    n = len(arrays)

    def body(*refs):
        src_refs, land_refs = refs[:n], refs[n:2 * n]
        send_sems, recv_sems = refs[2 * n], refs[2 * n + 1]
        token = refs[-1]
        for j, k in enumerate(peers):
            for a in range(n):
                _peer_copy(src_refs[a], land_refs[a], send_sems, recv_sems, a * len(peers) + j, k, receiving=False).start()
        token[...] = jnp.zeros_like(token)

    sems = pltpu.SemaphoreType.DMA((n * len(peers),))
    n_slots = N_DEV if len(peers) > 1 else 2
    lands = [pltpu.with_memory_space_constraint(lax.empty((n_slots,) + a.shape, a.dtype), pltpu.HBM) for a in arrays]
    outs = pl.pallas_call(
        body, name=name,
        out_shape=(sems, sems, *[pltpu.HBM(a.shape, a.dtype) for a in arrays], *[pltpu.HBM(l.shape, l.dtype) for l in lands],
                   jax.ShapeDtypeStruct((8, 128), F32)),
        in_specs=[_HBM] * (2 * n), out_specs=[_SEM, _SEM] + [_HBM] * (2 * n) + [pl.BlockSpec(memory_space=pltpu.VMEM)],
        input_output_aliases={i: 2 + i for i in range(2 * n)},
        compiler_params=pltpu.CompilerParams(has_side_effects=_DATAFLOW),
    )(*_pin(*arrays), *lands)
    return outs[0], outs[1], outs[2:2 + n], outs[2 + n:2 + 2 * n], outs[-1]


def _exchange_wait(send_sems, recv_sems, src_thru, land_thru, peers, after, name):
    n = len(src_thru)

    def body(*refs):
        src_refs, land_refs = refs[:n], refs[n:2 * n]
        send_sems, recv_sems = refs[2 * n], refs[2 * n + 1]
        for j, k in enumerate(peers):
            for a in range(n):
                cp = _peer_copy(src_refs[a], land_refs[a], send_sems, recv_sems, a * len(peers) + j, k, receiving=True)
                cp.wait_send()
                cp.wait_recv()

    outs = pl.pallas_call(
        body, name=name,
        out_shape=tuple(pltpu.HBM(s.shape, s.dtype) for s in (*src_thru, *land_thru)),
        in_specs=[_HBM] * (2 * n) + [_SEM, _SEM, _HBM], out_specs=[_HBM] * (2 * n),
        input_output_aliases={i: i for i in range(2 * n)},
        compiler_params=pltpu.CompilerParams(has_side_effects=_DATAFLOW),
    )(*src_thru, *land_thru, send_sems, recv_sems, *_pin(after))
    return outs[:n], outs[n:]


ALL_PEERS = tuple(range(1, N_DEV))
SIBLING = (1,)


def _sum_eight(recv, grad4, pos):
    n, r, hc = recv.shape
    steps = 4
    tc = hc // steps

    def body(pos_ref, r_ref, g_ref, o_ref):
        me = pos_ref[2]
        o_ref[...] = jnp.zeros_like(o_ref)
        for s in range(n):
            @pl.when(me == s)
            def _():
                o_ref[...] += g_ref[0].astype(F32)

            @pl.when(me != s)
            def _():
                o_ref[...] += r_ref[s].astype(F32)

    grid_spec = pltpu.PrefetchScalarGridSpec(
        num_scalar_prefetch=1, grid=(steps,),
        in_specs=[pl.BlockSpec((n, r, tc), lambda i, pos: (0, 0, i)),
                  pl.BlockSpec((1, r, tc), lambda i, pos: (pos[1], 0, pos[0] * steps + i))],
        out_specs=pl.BlockSpec((r, tc), lambda i, pos: (0, i)))
    return pl.pallas_call(body, name="sum_eight", grid_spec=grid_spec, out_shape=_out((r, hc), F32),
                          compiler_params=_cp(("parallel",), 32))(pos, *_pin(recv, grad4))


def _gather_finish(lands, shards):
    n = len(lands)
    any_spec = _HBM

    def body(*refs):
        shard_refs, out_refs = refs[n:2 * n], refs[2 * n:3 * n]
        send_sems, recv_sems, local_sems = refs[3 * n:]
        x, y, c = _mesh_pos()
        chip = 2 * x + y
        local, sends = [], []
        for a in range(n):
            cp = pltpu.make_async_copy(shard_refs[a], out_refs[a].at[chip], local_sems.at[a])
            cp.start()
            local.append(cp)
        for j in range(1, N_CHIPS):
            px, py = _peer_chip(x, y, j)
            for a in range(n):
                landed = _col_half(out_refs[a], c, lead=(2 * px + py,))
                cp = pltpu.make_async_remote_copy(
                    src_ref=landed, dst_ref=landed, send_sem=send_sems.at[a, j], recv_sem=recv_sems.at[a, j],
                    device_id=(x, y, 1 - c), device_id_type=MESH)
                cp.start()
                sends.append(cp)
        for j in range(1, N_CHIPS):
            px, py = _peer_chip(x, y, j)
            for a in range(n):
                other = _col_half(out_refs[a], 1 - c, lead=(2 * px + py,))
                pltpu.make_async_remote_copy(
                    src_ref=other, dst_ref=other, send_sem=send_sems.at[a, j], recv_sem=recv_sems.at[a, j],
                    device_id=(x, y, 1 - c), device_id_type=MESH).wait_recv()
        for cp in sends:
            cp.wait_send()
        for cp in local:
            cp.wait()

    return pl.pallas_call(
        body, name="gather_finish",
        out_shape=[_out(l.shape, l.dtype) for l in lands],
        in_specs=[any_spec] * (2 * n), out_specs=[any_spec] * n,
        input_output_aliases={i: i for i in range(n)},
        scratch_shapes=[pltpu.SemaphoreType.DMA((n, N_CHIPS))] * 2 + [pltpu.SemaphoreType.DMA((n,))],
        compiler_params=_cp(vmem_mb=16),
    )(*lands, *shards)


def _halves_exchange(halves):
    n = len(halves)
    any_spec = _HBM

    def body(*refs):
        in_refs, out_refs = refs[:n], refs[n:2 * n]
        send_sems, recv_sems, local_sems = refs[2 * n:]
        x, y, c = _mesh_pos()
        copies, local = [], []
        for a in range(n):
            hc = halves[a].shape[1]
            mine = pl.ds(pl.multiple_of(c * hc, 128), hc)
            lc = pltpu.make_async_copy(in_refs[a], out_refs[a].at[:, mine], local_sems.at[a])
            lc.start()
            local.append(lc)
            cp = pltpu.make_async_remote_copy(
                src_ref=in_refs[a], dst_ref=out_refs[a].at[:, mine],
                send_sem=send_sems.at[a], recv_sem=recv_sems.at[a],
                device_id=(x, y, 1 - c), device_id_type=MESH)
            cp.start()
            copies.append(cp)
        for a in range(n):
            hc = halves[a].shape[1]
            theirs = pl.ds(pl.multiple_of((1 - c) * hc, 128), hc)
            pltpu.make_async_remote_copy(
                src_ref=in_refs[a], dst_ref=out_refs[a].at[:, theirs],
                send_sem=send_sems.at[a], recv_sem=recv_sems.at[a],
                device_id=(x, y, 1 - c), device_id_type=MESH).wait_recv()
        for cp in copies:
            cp.wait_send()
        for lc in local:
            lc.wait()

    return pl.pallas_call(
        body, name="halves_exchange",
        out_shape=[_out((h.shape[0], 2 * h.shape[1]), h.dtype) for h in halves],
        in_specs=[any_spec] * n, out_specs=[any_spec] * n,
        scratch_shapes=[pltpu.SemaphoreType.DMA((n,))] * 3,
        compiler_params=_cp(vmem_mb=16),
    )(*halves)


def _adam_math(w, g, m, v):
    m2 = ADAM_B1 * m + (1.0 - ADAM_B1) * g
    v2 = ADAM_B2 * v + (1.0 - ADAM_B2) * (g * g)
    m_hat = m2 / (1.0 - ADAM_B1 ** ADAM_STEP)
    v_hat = v2 / (1.0 - ADAM_B2 ** ADAM_STEP)
    delta = -ADAM_LR * (m_hat / (jnp.sqrt(v_hat) + ADAM_EPS) + ADAM_WD * w)
    return delta, m2, v2


def _adam_big(g_own, g_pair, w, m, v, pos):
    r, c = w.shape
    tc = c // 4

    def body(pos_ref, go_ref, gp_ref, w_ref, m_ref, v_ref, g_ref, d_ref, m2_ref, v2_ref):
        half = pl.program_id(0) // 2
        g = jnp.where(half == pos_ref[0], go_ref[...], gp_ref[0])
        d, m2, v2 = _adam_math(w_ref[...], g, m_ref[...], v_ref[...])
        g_ref[...] = g
        d_ref[...] = d
        m2_ref[...] = m2
        v2_ref[...] = v2

    spec = pl.BlockSpec((r, tc), lambda i, pos: (0, i))
    grid_spec = pltpu.PrefetchScalarGridSpec(
        num_scalar_prefetch=1, grid=(4,),
        in_specs=[pl.BlockSpec((r, tc), lambda i, pos: (0, i % 2)),
                  pl.BlockSpec((1, r, tc), lambda i, pos: (1 - pos[0], 0, i % 2)), spec, spec, spec],
        out_specs=[spec] * 4)
    sh = _out((r, c), F32)
    return pl.pallas_call(body, name="adam_big", grid_spec=grid_spec, out_shape=[sh] * 4,
                          compiler_params=_cp(("parallel",), 32))(pos, *_pin(g_own, g_pair, w, m, v))


def _adam_ada(c_act_t, dmod_cols, w, m, v):
    r, c = w.shape
    tc = 512

    def body(ct_ref, dm_ref, w_ref, m_ref, v_ref, g_ref, d_ref, m2_ref, v2_ref):
        g = _nn(ct_ref[...], dm_ref[...].astype(BF))
        d, m2, v2 = _adam_math(w_ref[...], g, m_ref[...], v_ref[...])
        g_ref[...] = g
        d_ref[...] = d
        m2_ref[...] = m2
        v2_ref[...] = v2

    spec = pl.BlockSpec((r, tc), lambda i: (0, i))
    sh = _out((r, c), F32)
    return pl.pallas_call(
        body, name="adam_ada", grid=(c // tc,),
        in_specs=[pl.BlockSpec(c_act_t.shape, lambda i: (0, 0)), pl.BlockSpec((dmod_cols.shape[0], tc), lambda i: (0, i)),
                  spec, spec, spec],
        out_specs=[spec] * 4, out_shape=[sh] * 4, compiler_params=_cp(("parallel",), 48))(*_pin(c_act_t, dmod_cols, w, m, v))


def _ada_mod(c_all, w_shard, b_shard):
    nb, d = c_all.shape
    cols = w_shard.shape[1]
    tc = 512

    def body(c_ref, w_ref, b_ref, mod_ref, act_ref):
        cv = c_ref[...]
        act = cv * _sigmoid(cv)
        act_ref[...] = act
        mod_ref[...] = _nn(act.astype(BF), w_ref[...].astype(BF)) + b_ref[...]

    return pl.pallas_call(
        body, name="ada_mod", grid=(cols // tc,),
        in_specs=[pl.BlockSpec((nb, d), lambda i: (0, 0)), pl.BlockSpec((d, tc), lambda i: (0, i)),
                  pl.BlockSpec((1, tc), lambda i: (0, i))],
        out_specs=[pl.BlockSpec((nb, tc), lambda i: (0, i)), pl.BlockSpec((nb, d), lambda i: (0, 0))],
        out_shape=[_out((nb, cols), F32), _out((nb, d), F32)],
        compiler_params=_cp(("arbitrary",), 32))(*_pin(c_all, w_shard, b_shard))


SUB_ROWS = 256
ROW_TILE = 512


def _sub_rows(tm):
    return [slice(s, s + SUB_ROWS) for s in range(0, tm, SUB_ROWS)] if tm > SUB_ROWS else [slice(0, tm)]


_RESIDENT = pl.BlockSpec(memory_space=pltpu.VMEM)


def _token_spec():
    return pl.BlockSpec((8, 128), lambda *_: (0, 0))


def _in_proj(x, mod3, g_mix, w_in_t, seq, token):
    t, d = x.shape
    tm = min(ROW_TILE, seq)
    tps = seq // tm

    def body(x_ref, mod_ref, g_ref, w_ref, tok_ref, proj_ref, u1_ref):
        for rows in _sub_rows(tm):
            xv = x_ref[rows, :]
            r = lax.rsqrt(jnp.mean(xv * xv, -1, keepdims=True) + EPS)
            u = (xv * r * g_ref[...]) * (1.0 + mod_ref[0, 1:2, :]) + mod_ref[0, 0:1, :]
            ub = u.astype(BF)
            u1_ref[rows, :] = ub
            proj_ref[rows, :] = _nt(ub, w_ref[...])

    return pl.pallas_call(
        body, name="in_proj", grid=(t // tm,),
        in_specs=[pl.BlockSpec((tm, d), lambda i: (i, 0)), pl.BlockSpec((1, N_MOD, d), lambda i: (i // tps, 0, 0)),
                  pl.BlockSpec((1, d), lambda i: (0, 0)), _RESIDENT, _token_spec()],
        out_specs=[pl.BlockSpec((tm, IN_PAD), lambda i: (i, 0)), pl.BlockSpec((tm, d), lambda i: (i, 0))],
        out_shape=[_out((t, IN_PAD), F32), _out((t, d), BF)],
        compiler_params=_cp(("parallel",), 40))(*_pin(x, mod3, g_mix), w_in_t, *_pin(token))


def _pool_tile(seq):
    return min(512, seq)


def _pool_fwd(proj, w_pool, pool_scale, nb, seq):
    ts = _pool_tile(seq)
    nt = seq // ts

    def body(u_ref, halo_ref, wp_ref, ps_ref, yp_ref, p_ref):
        i = pl.program_id(1)
        halo = jnp.where(i == 0, 0.0, halo_ref[...])
        u = u_ref[...]
        ext = jnp.concatenate([halo, u], 0)
        tpos = i * ts + _iota((ts, 1), 0)
        for g, w in enumerate(POOL_WINDOWS):
            gs = slice(g * POOL_GROUP, (g + 1) * POOL_GROUP)
            s = ext[:, gs]
            sh = 1
            while sh < w:
                s = s + pltpu.roll(s, sh, 0)
                sh *= 2
            cnt = jnp.minimum(tpos + 1, w).astype(F32)
            pb = (s[HALO:] / cnt - u[:, gs]).astype(BF)
            p_ref[:, gs] = pb
            yp_ref[:, gs] = (_nn(pb, wp_ref[g].astype(BF)) * ps_ref[:, gs]).astype(BF)

    hb = ts // HALO
    return pl.pallas_call(
        body, name="pool_fwd", grid=(nb, nt),
        in_specs=[pl.BlockSpec((ts, POOL_WIDTH), lambda b, i: (b * nt + i, 0)),
                  pl.BlockSpec((HALO, POOL_WIDTH), lambda b, i: (jnp.maximum((b * nt + i) * hb - 1, 0), 0)),
                  pl.BlockSpec((4, POOL_GROUP, POOL_GROUP), lambda b, i: (0, 0, 0)),
                  pl.BlockSpec((1, POOL_WIDTH), lambda b, i: (0, 0))],
        out_specs=[pl.BlockSpec((ts, POOL_WIDTH), lambda b, i: (b * nt + i, 0))] * 2,
        out_shape=[_out((nb * seq, POOL_WIDTH), BF)] * 2,
        compiler_params=_cp(("parallel", "parallel"), 32))(*_pin(proj, proj, w_pool, pool_scale))


def _conv_pre(uxbc, halo, cw, cb, first):
    halo = jnp.where(first, 0.0, halo)
    ext = jnp.concatenate([halo, uxbc], 0)
    pre = cb + uxbc * cw[3:4]
    for k in (2, 1, 0):
        pre = pre + pltpu.roll(ext, 3 - k, 0)[CONV_HALO:] * cw[k:k + 1]
    return pre


def _chunk_terms(pre, udt, dtb, alog):
    sg = _sigmoid(pre)
    xbc = pre * sg
    dtp = udt[:, :SSD_HEADS] + dtb
    dt = jnp.maximum(dtp, 0.0) + jnp.log(1.0 + jnp.exp(-jnp.abs(dtp)))
    a = -jnp.exp(alog)
    da = dt * a
    tril = (_iota((CHUNK, CHUNK), 0) >= _iota((CHUNK, CHUNK), 1))
    acum = _exact_nn_left(tril.astype(BF), da)
    eye = (_iota((SSD_HEADS, SSD_HEADS), 0) == _iota((SSD_HEADS, SSD_HEADS), 1)).astype(BF)
    acum_t = _exact_nt_left(eye, acum)
    expand = _head_expand_matrix(SSD_HEADS, SSD_INNER)
    acum_e = _exact_nn(acum, expand)
    dt_e = _exact_nn(dt, expand)
    last_e = acum_e[CHUNK - 1:CHUNK]
    return dict(pre=pre, sg=sg, xbc=xbc, dtp=dtp, dt=dt, a=a, acum=acum, acum_t=acum_t, tril=tril,
                dt_e=dt_e, e_a=jnp.exp(acum_e), d_out=jnp.exp(last_e - acum_e), c_dec=jnp.exp(last_e))


def _head_decay(r, h):
    seg = r["acum"][:, h:h + 1] - r["acum_t"][h:h + 1, :]
    return jnp.where(r["tril"], jnp.exp(jnp.minimum(seg, 0.0)), 0.0)


def _ssd_specs(nb, seq, reverse):
    nc = seq // CHUNK
    per = seq // CONV_HALO

    def cidx(c):
        return (nc - 1 - c) if reverse else c

    def row(b, c):
        return b * nc + cidx(c)

    specs = [
        pl.BlockSpec((CHUNK, CONV_CH), lambda b, c: (row(b, c), 1)),
        pl.BlockSpec((CONV_HALO, CONV_CH),
                     lambda b, c: (jnp.maximum(b * per + cidx(c) * (CHUNK // CONV_HALO) - 1, 0), 1)),
        pl.BlockSpec((CHUNK, GROUP_W), lambda b, c: (row(b, c), 1)),
        pl.BlockSpec((CHUNK, GROUP_W), lambda b, c: (row(b, c), 2)),
        pl.BlockSpec((CHUNK, 128), lambda b, c: (row(b, c), OFF_DT // 128)),
    ]
    return specs, row, cidx, nc


def _const_spec(shape):
    return pl.BlockSpec(shape, lambda b, c: (0,) * len(shape))


def _ssd_fwd(proj, conv_w, conv_b, dt_bias, a_log, dskip_e, g_ssd, nb, seq):
    specs, row, cidx, nc = _ssd_specs(nb, seq, reverse=False)

    def body(uxbc_ref, halo_ref, z0_ref, z1_ref, udt_ref, cw_ref, cb_ref, dtb_ref, alog_ref, dsk_ref, gs_ref,
             yssd_ref, yssm_ref, hprev_ref, pre_ref, h_ref, yd_ref):
        c = pl.program_id(1)

        @pl.when(c == 0)
        def _():
            h_ref[...] = jnp.zeros_like(h_ref)

        pre = _conv_pre(uxbc_ref[...], halo_ref[...], cw_ref[...], cb_ref[...], c == 0)
        pre_ref[...] = pre
        r = _chunk_terms(pre, udt_ref[...], dtb_ref[...], alog_ref[...])
        xbc = r["xbc"]
        xs = xbc[:, :SSD_INNER]
        xdt = xs * r["dt_e"]
        xdt_b = xdt.astype(BF)
        xdo_b = (xdt * r["d_out"]).astype(BF)
        hprev_ref[0, 0] = h_ref[...]
        for g in range(2):
            gs = slice(g * GROUP_W, (g + 1) * GROUP_W)
            bg = xbc[:, SSD_INNER + g * SSD_STATE:SSD_INNER + (g + 1) * SSD_STATE].astype(BF)
            cg = xbc[:, SSD_INNER + (2 + g) * SSD_STATE:SSD_INNER + (3 + g) * SSD_STATE].astype(BF)
            scores = _nt(cg, bg)
            hg = h_ref[g]
            y_off = _nn(cg, hg.astype(BF)) * r["e_a"][:, gs]
            for hh in range(8):
                h = g * 8 + hh
                m = (scores * _head_decay(r, h)).astype(BF)
                yd_ref[:, h * SSD_HEAD_DIM:(h + 1) * SSD_HEAD_DIM] = _nn(m, xdt_b[:, h * SSD_HEAD_DIM:(h + 1) * SSD_HEAD_DIM])
            h_ref[g] = hg * r["c_dec"][:, gs] + _tn(bg, xdo_b[:, gs])
            y = yd_ref[:, gs] + y_off + dsk_ref[:, gs] * xs[:, gs]
            yssm_ref[:, gs] = y
            zg = (z0_ref if g == 0 else z1_ref)[...]
            yg = y * (zg * _sigmoid(zg))
            rg = lax.rsqrt(jnp.mean(yg * yg, -1, keepdims=True) + EPS)
            yssd_ref[:, gs] = (yg * rg * gs_ref[:, gs]).astype(BF)

    t = nb * seq
    return pl.pallas_call(
        body, name="ssd_fwd", grid=(nb, nc),
        in_specs=specs + [_const_spec((4, CONV_CH)), _const_spec((1, CONV_CH)), _const_spec((1, SSD_HEADS)),
                          _const_spec((1, SSD_HEADS)), _const_spec((1, SSD_INNER)), _const_spec((1, SSD_INNER))],
        out_specs=[pl.BlockSpec((CHUNK, SSD_INNER), lambda b, c: (row(b, c), 0)),
                   pl.BlockSpec((CHUNK, SSD_INNER), lambda b, c: (row(b, c), 0)),
                   pl.BlockSpec((1, 1, 2, SSD_STATE, GROUP_W), lambda b, c: (b, c, 0, 0, 0)),
                   pl.BlockSpec((CHUNK, CONV_CH), lambda b, c: (row(b, c), 0))],
        out_shape=[_out((t, SSD_INNER), BF), _out((t, SSD_INNER), F32),
                   _out((nb, nc, 2, SSD_STATE, GROUP_W), F32), _out((t, CONV_CH), F32)],
        scratch_shapes=[pltpu.VMEM((2, SSD_STATE, GROUP_W), F32), pltpu.VMEM((CHUNK, SSD_INNER), F32)],
        compiler_params=_cp(("arbitrary", "arbitrary"), 48),
    )(*_pin(proj, proj, proj, proj, proj, conv_w, conv_b, dt_bias, a_log, dskip_e, g_ssd))


def _out_proj(y_pool, y_ssd, w_out, x, mod3, g_mlp, seq):
    t, d = x.shape
    tm = 512
    tps = seq // tm if seq >= tm else 1
    tm = min(tm, seq)

    def body(yp_ref, ys_ref, w_ref, x_ref, mod_ref, g_ref, h1_ref, o_ref, u2_ref):
        o = _nn(yp_ref[...], w_ref[0:POOL_WIDTH, :]) + _nn(ys_ref[...], w_ref[POOL_WIDTH:, :])
        o_ref[...] = o.astype(BF)
        h1 = x_ref[...] + mod_ref[0, 2:3, :] * o
        h1_ref[...] = h1
        r = lax.rsqrt(jnp.mean(h1 * h1, -1, keepdims=True) + EPS)
        u2_ref[...] = ((h1 * r * g_ref[...]) * (1.0 + mod_ref[0, 4:5, :]) + mod_ref[0, 3:4, :]).astype(BF)

    row = lambda i: (i, 0)
    return pl.pallas_call(
        body, name="out_proj", grid=(t // tm,),
        in_specs=[pl.BlockSpec((tm, POOL_WIDTH), row), pl.BlockSpec((tm, SSD_INNER), row),
                  _RESIDENT, pl.BlockSpec((tm, d), row),
                  pl.BlockSpec((1, N_MOD, d), lambda i: (i // tps, 0, 0)), pl.BlockSpec((1, d), lambda i: (0, 0))],
        out_specs=[pl.BlockSpec((tm, d), row)] * 3,
        out_shape=[_out((t, d), F32), _out((t, d), BF), _out((t, d), BF)],
        compiler_params=_cp(("parallel",), 48))(*_pin(y_pool, y_ssd), w_out, *_pin(x, mod3, g_mlp))


def _mlp_up(u2, w_up4):
    t, d = u2.shape
    tm = min(1024, t)
    nk, _, cols = w_up4.shape

    def body(u_ref, w_ref, a_ref):
        a_ref[...] = _nn(u_ref[...], w_ref[pl.program_id(1)]).astype(BF)

    return pl.pallas_call(
        body, name="mlp_up", grid=(t // tm, nk),
        in_specs=[pl.BlockSpec((tm, d), lambda i, k: (i, 0)), _RESIDENT],
        out_specs=pl.BlockSpec((tm, cols), lambda i, k: (i, k)),
        out_shape=_out((t, nk * cols), BF),
        compiler_params=_cp(("parallel", "parallel"), 32))(*_pin(u2), w_up4)


def _mlp_down_loss(a_up, w_down, h1, mod3, g_final, target, seq):
    t, d = h1.shape
    nb = t // seq
    tm = min(ROW_TILE, seq)
    tps = seq // tm

    def body(a_ref, w_ref, h1_ref, mod_ref, g_ref, tg_ref, ddn_ref, dh2_ref, sq_ref, gg_ref, dgf_ref):
        i = pl.program_id(0)

        @pl.when(i == 0)
        def _():
            sq_ref[...] = jnp.zeros_like(sq_ref)
            gg_ref[...] = jnp.zeros_like(gg_ref)

        @pl.when(i % tps == 0)
        def _():
            dgf_ref[...] = jnp.zeros_like(dgf_ref)

        gate = mod_ref[0, 5:6, :]
        sq = gg = dgf = 0.0
        for rows in _sub_rows(tm):
            f = jnp.square(jnp.maximum(a_ref[rows, :], 0))
            dn = _nn(f, w_ref[...])
            h2 = h1_ref[rows, :] + gate * dn
            r = lax.rsqrt(jnp.mean(h2 * h2, -1, keepdims=True) + EPS)
            hh = h2 * r
            err = hh * g_ref[...] - tg_ref[rows, :]
            dy = err * (1.0 / d)
            dhat = dy * g_ref[...]
            dh2 = r * (dhat - hh * jnp.mean(dhat * hh, -1, keepdims=True))
            dh2_ref[rows, :] = dh2
            ddn_ref[rows, :] = (dh2 * gate).astype(BF)
            sq = sq + jnp.sum(err * err, 0, keepdims=True)
            gg = gg + jnp.sum(dy * hh, 0, keepdims=True)
            dgf = dgf + jnp.sum(dh2 * dn, 0, keepdims=True)
        sq_ref[...] += sq
        gg_ref[...] += gg
        dgf_ref[0] += dgf

    row = lambda i: (i, 0)
    vec = pl.BlockSpec((1, d), lambda i: (0, 0))
    return pl.pallas_call(
        body, name="mlp_down_loss", grid=(t // tm,),
        in_specs=[pl.BlockSpec((tm, D_FF), row), _RESIDENT, pl.BlockSpec((tm, d), row),
                  pl.BlockSpec((1, N_MOD, d), lambda i: (i // tps, 0, 0)), vec, pl.BlockSpec((tm, d), row)],
        out_specs=[pl.BlockSpec((tm, d), row), pl.BlockSpec((tm, d), row), vec, vec,
                   pl.BlockSpec((1, 1, d), lambda i: (i // tps, 0, 0))],
        out_shape=[_out((t, d), BF), _out((t, d), F32), _out((1, d), F32),
                   _out((1, d), F32), _out((nb, 1, d), F32)],
        compiler_params=_cp(("arbitrary",), 44))(*_pin(a_up), w_down, *_pin(h1, mod3, g_final, target))


def _tn_matmul(a, b, tk, tn, name, square_relu=False, out3=False):
    t, kdim = a.shape
    ndim = b.shape[1]

    def body(a_ref, b_ref, o_ref):
        av = a_ref[...]
        if square_relu:
            av = jnp.square(jnp.maximum(av, 0))
        res = _tn(av, b_ref[...]).astype(BF)
        if out3:
            o_ref[0] = res
        else:
            o_ref[...] = res

    if out3:
        out_spec = pl.BlockSpec((1, tk, tn), lambda j, i: (j, i, 0))
        out_shape = _out((ndim // tn, kdim, tn), BF)
    else:
        out_spec = pl.BlockSpec((tk, tn), lambda j, i: (i, j))
        out_shape = _out((kdim, ndim), BF)
    return pl.pallas_call(
        body, name=name, grid=(ndim // tn, kdim // tk),
        in_specs=[pl.BlockSpec((t, tk), lambda j, i: (0, i)), pl.BlockSpec((t, tn), lambda j, i: (0, j))],
        out_specs=out_spec, out_shape=out_shape,
        compiler_params=_cp(("parallel", "parallel"), 56))(*_pin(a, b))


def _mlp_down_bwd(d_dn, w_down4, a_up, token):
    t, d = d_dn.shape
    tm = min(1024, t)
    nk, rows, _ = w_down4.shape

    def body(g_ref, w_ref, a_ref, tok_ref, o_ref):
        df = _nt(g_ref[...], w_ref[pl.program_id(1)])
        o_ref[...] = (df * (2.0 * jnp.maximum(a_ref[...], 0).astype(F32))).astype(BF)

    return pl.pallas_call(
        body, name="mlp_down_bwd", grid=(t // tm, nk),
        in_specs=[pl.BlockSpec((tm, d), lambda i, k: (i, 0)), _RESIDENT,
                  pl.BlockSpec((tm, rows), lambda i, k: (i, k)), _token_spec()],
        out_specs=pl.BlockSpec((tm, rows), lambda i, k: (i, k)),
        out_shape=_out((t, nk * rows), BF),
        compiler_params=_cp(("parallel", "parallel"), 32))(*_pin(d_dn), w_down4, *_pin(a_up, token))


def _mlp_up_bwd(d_a, w_up4, h1, dh2, o, mod3, g_mlp, seq, token):
    t, d = h1.shape
    nb = t // seq
    tm = min(ROW_TILE, seq)
    tps = seq // tm
    nk = w_up4.shape[0]
    cols = w_up4.shape[2]

    def body(da_ref, w_ref, h1_ref, dh2_ref, o_ref, mod_ref, g_ref, tok_ref, dh1_ref, do_ref, acc_ref, gg_ref):
        i = pl.program_id(0)

        @pl.when(i == 0)
        def _():
            gg_ref[...] = jnp.zeros_like(gg_ref)

        @pl.when(i % tps == 0)
        def _():
            acc_ref[...] = jnp.zeros_like(acc_ref)

        gg = a_shift = a_scale = a_gate = 0.0
        for rows in _sub_rows(tm):
            du = _nt(da_ref[rows, 0:cols], w_ref[0])
            for k in range(1, nk):
                du = du + _nt(da_ref[rows, k * cols:(k + 1) * cols], w_ref[k])
            h1 = h1_ref[rows, :]
            r = lax.rsqrt(jnp.mean(h1 * h1, -1, keepdims=True) + EPS)
            hh = h1 * r
            n2 = hh * g_ref[...]
            dn2 = du * (1.0 + mod_ref[0, 4:5, :])
            dhat = dn2 * g_ref[...]
            dh1 = dh2_ref[rows, :] + r * (dhat - hh * jnp.mean(dhat * hh, -1, keepdims=True))
            dh1_ref[rows, :] = dh1
            do_ref[rows, :] = (dh1 * mod_ref[0, 2:3, :]).astype(BF)
            gg = gg + jnp.sum(dn2 * hh, 0, keepdims=True)
            a_shift = a_shift + jnp.sum(du, 0, keepdims=True)
            a_scale = a_scale + jnp.sum(du * n2, 0, keepdims=True)
            a_gate = a_gate + jnp.sum(dh1 * o_ref[rows, :].astype(F32), 0, keepdims=True)
        gg_ref[...] += gg
        acc_ref[0, 0:1, :] += a_shift
        acc_ref[0, 1:2, :] += a_scale
        acc_ref[0, 2:3, :] += a_gate

    row = lambda i: (i, 0)
    vec = pl.BlockSpec((1, d), lambda i: (0, 0))
    return pl.pallas_call(
        body, name="mlp_up_bwd", grid=(t // tm,),
        in_specs=[pl.BlockSpec((tm, D_FF), row), _RESIDENT, pl.BlockSpec((tm, d), row),
                  pl.BlockSpec((tm, d), row), pl.BlockSpec((tm, d), row),
                  pl.BlockSpec((1, N_MOD, d), lambda i: (i // tps, 0, 0)), vec, _token_spec()],
        out_specs=[pl.BlockSpec((tm, d), row), pl.BlockSpec((tm, d), row),
                   pl.BlockSpec((1, 8, d), lambda i: (i // tps, 0, 0)), vec],
        out_shape=[_out((t, d), F32), _out((t, d), BF),
                   _out((nb, 8, d), F32), _out((1, d), F32)],
        compiler_params=_cp(("arbitrary",), 44))(*_pin(d_a), w_up4, *_pin(h1, dh2, o, mod3, g_mlp, token))


def _out_proj_bwd(d_o, w_out, token):
    t, d = d_o.shape
    tm = min(512, t)

    def body(g_ref, w_ref, tok_ref, dp_ref, ds_ref):
        gv = g_ref[...]
        dp_ref[...] = _nt(gv, w_ref[0:POOL_WIDTH, :])
        ds_ref[...] = _nt(gv, w_ref[POOL_WIDTH:, :])

    row = lambda i: (i, 0)
    return pl.pallas_call(
        body, name="out_proj_bwd", grid=(t // tm,),
        in_specs=[pl.BlockSpec((tm, d), row), _RESIDENT, _token_spec()],
        out_specs=[pl.BlockSpec((tm, POOL_WIDTH), row), pl.BlockSpec((tm, SSD_INNER), row)],
        out_shape=[_out((t, POOL_WIDTH), F32), _out((t, SSD_INNER), F32)],
        compiler_params=_cp(("parallel",), 32))(*_pin(d_o), w_out, *_pin(token))


def _pool_bwd(d_ypool, p, w_pool, pool_scale, nb, seq):
    ts = _pool_tile(seq)
    nt = seq // ts
    hb = ts // HALO
    last_block = nb * seq // HALO - 1

    def body(dy_ref, halo_ref, p_ref, wp_ref, ps_ref, du_ref, gw_ref, gs_ref):
        b = pl.program_id(0)
        i = pl.program_id(1)

        @pl.when((b == 0) & (i == 0))
        def _():
            gw_ref[...] = jnp.zeros_like(gw_ref)
            gs_ref[...] = jnp.zeros_like(gs_ref)

        halo = jnp.where(i == nt - 1, 0.0, halo_ref[...])
        dy = dy_ref[...]
        ext = jnp.concatenate([dy, halo], 0)
        tpos = i * ts + _iota((ts + HALO, 1), 0)
        n_ext = ts + HALO
        for g, w in enumerate(POOL_WINDOWS):
            gs = slice(g * POOL_GROUP, (g + 1) * POOL_GROUP)
            wg = wp_ref[g].astype(BF)
            pg = p_ref[:, gs]
            pw = _nn(pg, wg)
            gs_ref[:, gs] += jnp.sum(dy[:, gs] * pw, 0, keepdims=True)
            dpw = (ext[:, gs] * ps_ref[:, gs]).astype(BF)
            gw_ref[g] += _tn(pg, dpw[:ts])
            dp = _nt(dpw, wg)
            cnt = jnp.minimum(tpos + 1, w).astype(F32)
            s = dp / cnt
            sh = 1
            while sh < w:
                s = s + pltpu.roll(s, n_ext - sh, 0)
                sh *= 2
            du_ref[:, gs] = (s[:ts] - dp[:ts]).astype(BF)

    return pl.pallas_call(
        body, name="pool_bwd", grid=(nb, nt),
        in_specs=[pl.BlockSpec((ts, POOL_WIDTH), lambda b, i: (b * nt + i, 0)),
                  pl.BlockSpec((HALO, POOL_WIDTH), lambda b, i: (jnp.minimum((b * nt + i + 1) * hb, last_block), 0)),
                  pl.BlockSpec((ts, POOL_WIDTH), lambda b, i: (b * nt + i, 0)),
                  pl.BlockSpec((4, POOL_GROUP, POOL_GROUP), lambda b, i: (0, 0, 0)),
                  pl.BlockSpec((1, POOL_WIDTH), lambda b, i: (0, 0))],
        out_specs=[pl.BlockSpec((ts, POOL_WIDTH), lambda b, i: (b * nt + i, 0)),
                   pl.BlockSpec((4, POOL_GROUP, POOL_GROUP), lambda b, i: (0, 0, 0)),
                   pl.BlockSpec((1, POOL_WIDTH), lambda b, i: (0, 0))],
        out_shape=[_out((nb * seq, POOL_WIDTH), BF), _out((4, POOL_GROUP, POOL_GROUP), F32),
                   _out((1, POOL_WIDTH), F32)],
        compiler_params=_cp(("arbitrary", "arbitrary"), 32))(*_pin(d_ypool, d_ypool, p, w_pool, pool_scale))


def _ssd_bwd(proj, pre, d_yssd, yssm, h_prev, dt_bias, a_log, dskip_e, g_ssd, nb, seq):
    specs, row, cidx, nc = _ssd_specs(nb, seq, reverse=True)
    specs = specs[2:]

    def body(z0_ref, z1_ref, udt_ref, pre_ref, dys_ref, yssm_ref, hprev_ref,
             dtb_ref, alog_ref, dsk_ref, gs_ref,
             dz_ref, dpre_ref, dudt_ref, ggs_ref, gdsk_ref, ga_ref, gdtb_ref,
             g_ref, dxdt_ref, dyv_ref):
        b = pl.program_id(0)
        c = pl.program_id(1)

        @pl.when(c == 0)
        def _():
            g_ref[...] = jnp.zeros_like(g_ref)

        @pl.when((b == 0) & (c == 0))
        def _():
            ggs_ref[...] = jnp.zeros_like(ggs_ref)
            gdsk_ref[...] = jnp.zeros_like(gdsk_ref)
            ga_ref[...] = jnp.zeros_like(ga_ref)
            gdtb_ref[...] = jnp.zeros_like(gdtb_ref)

        r = _chunk_terms(pre_ref[...], udt_ref[...], dtb_ref[...], alog_ref[...])
        xbc = r["xbc"]
        xs = xbc[:, :SSD_INNER]
        dt_e = r["dt_e"]
        xdt = xs * dt_e
        xdt_b = xdt.astype(BF)
        reduce_m = _head_reduce_matrix(GROUP_W, 8)

        def head_sums(v):
            return _nn(v.astype(BF), reduce_m)

        onehot16 = lambda h: (_iota((1, SSD_HEADS), 1) == h).astype(F32)
        onecol16 = lambda h: (_iota((SSD_HEADS, 1), 0) == h).astype(F32)

        d_acum = jnp.zeros((CHUNK, SSD_HEADS), F32)
        d_acum_t = jnp.zeros((SSD_HEADS, CHUNK), F32)
        d_alast = jnp.zeros((1, SSD_HEADS), F32)
        place8 = lambda g: (_iota((8, SSD_HEADS), 1) == _iota((8, SSD_HEADS), 0) + 8 * g).astype(BF)
        d_b, d_c = [], []
        for g in range(2):
            gs = slice(g * GROUP_W, (g + 1) * GROUP_W)
            zg = (z0_ref if g == 0 else z1_ref)[...]
            sz = _sigmoid(zg)
            silu_z = zg * sz
            ys = yssm_ref[:, gs]
            yg = ys * silu_z
            rg = lax.rsqrt(jnp.mean(yg * yg, -1, keepdims=True) + EPS)
            yh = yg * rg
            dys = dys_ref[:, gs]
            ggs_ref[:, gs] += jnp.sum(dys * yh, 0, keepdims=True)
            dyh = dys * gs_ref[:, gs]
            dyg = rg * (dyh - yh * jnp.mean(dyh * yh, -1, keepdims=True))
            dy = dyg * silu_z
            dz_ref[:, gs] = (dyg * ys * (sz * (1.0 + zg * (1.0 - sz)))).astype(BF)
            gdsk_ref[:, gs] += jnp.sum(dy * xs[:, gs], 0, keepdims=True)
            dyv_ref[:, gs] = dy
            dy_b = dy.astype(BF)

            bg = xbc[:, SSD_INNER + g * SSD_STATE:SSD_INNER + (g + 1) * SSD_STATE].astype(BF)
            cg = xbc[:, SSD_INNER + (2 + g) * SSD_STATE:SSD_INNER + (3 + g) * SSD_STATE].astype(BF)
            scores = _nt(cg, bg)
            hg = hprev_ref[0, 0, g]
            hg_b = hg.astype(BF)
            gg = g_ref[g]
            gg_b = gg.astype(BF)
            e_a = r["e_a"][:, gs]
            d_out = r["d_out"][:, gs]
            c_dec = r["c_dec"][:, gs]
            zc = _nn(cg, hg_b)
            wv = e_a * dy
            wv_b = wv.astype(BF)
            da_g = head_sums(wv * zc)
            dcg = _nt(wv_b, hg_b)
            d_hprev = _tn(cg, wv_b)
            vg = _nn(bg, gg_b)
            dxdt_g = d_out * vg
            dd_out = head_sums(xdt[:, gs] * vg)
            dbg = _nt((xdt[:, gs] * d_out).astype(BF), gg_b)
            dcd = _exact_nn(jnp.sum(gg * hg, 0, keepdims=True), reduce_m)
            d_out8 = jnp.exp(r["acum"][CHUNK - 1:CHUNK, 8 * g:8 * g + 8] - r["acum"][:, 8 * g:8 * g + 8])
            c_dec8 = jnp.exp(r["acum"][CHUNK - 1:CHUNK, 8 * g:8 * g + 8])
            t8 = dd_out * d_out8
            d_alast = d_alast + _exact_nn(jnp.sum(t8, 0, keepdims=True) + dcd * c_dec8, place8(g))
            d_acum = d_acum + _exact_nn(da_g - t8, place8(g))
            dsc = jnp.zeros((CHUNK, CHUNK), F32)
            for hh in range(8):
                h = g * 8 + hh
                hs = slice(h * SSD_HEAD_DIM, (h + 1) * SSD_HEAD_DIM)
                lam = _head_decay(r, h)
                m = scores * lam
                dyh_b = dy_b[:, hh * SSD_HEAD_DIM:(hh + 1) * SSD_HEAD_DIM]
                dm = _nt(dyh_b, xdt_b[:, hs])
                tm_ = dm * m
                d_acum = d_acum + jnp.sum(tm_, 1, keepdims=True) * onehot16(h)
                d_acum_t = d_acum_t + onecol16(h) * jnp.sum(tm_, 0, keepdims=True)
                dsc = dsc + dm * lam
                dxdt_ref[:, hs] = _tn(m.astype(BF), dyh_b) + dxdt_g[:, hh * SSD_HEAD_DIM:(hh + 1) * SSD_HEAD_DIM]
            dsc_b = dsc.astype(BF)
            d_c.append(dcg + _nn(dsc_b, bg))
            d_b.append(dbg + _tn(dsc_b, cg))
            g_ref[g] = d_hprev + c_dec * gg

        eye = (_iota((CHUNK, CHUNK), 0) == _iota((CHUNK, CHUNK), 1)).astype(BF)
        d_acum = d_acum - _exact_nt_left(eye, d_acum_t)
        is_last = (_iota((CHUNK, 1), 0) == CHUNK - 1).astype(F32)
        d_acum = d_acum + is_last * d_alast
        triu = (_iota((CHUNK, CHUNK), 0) <= _iota((CHUNK, CHUNK), 1)).astype(BF)
        d_da = _exact_nn_left(triu, d_acum)
        dt = r["dt"]
        ga_ref[...] += jnp.sum(d_da * dt, 0, keepdims=True)
        dxdt = dxdt_ref[...]
        reduce16 = _head_reduce_matrix(SSD_INNER, SSD_HEADS)
        d_dt = d_da * r["a"] + _nn((dxdt * xs).astype(BF), reduce16)
        d_udt = d_dt * _sigmoid(r["dtp"])
        gdtb_ref[...] += jnp.sum(d_udt, 0, keepdims=True)
        dudt_ref[...] = jnp.zeros_like(dudt_ref)
        dudt_ref[:, 0:SSD_HEADS] = d_udt.astype(BF)
        pre, sg = r["pre"], r["sg"]
        dsilu = sg * (1.0 + pre * (1.0 - sg))
        dpre_ref[:, 0:SSD_INNER] = (dsk_ref[...] * dyv_ref[...] + dxdt * dt_e) * dsilu[:, 0:SSD_INNER]
        for g in range(2):
            bs = slice(SSD_INNER + g * SSD_STATE, SSD_INNER + (g + 1) * SSD_STATE)
            cs = slice(SSD_INNER + (2 + g) * SSD_STATE, SSD_INNER + (3 + g) * SSD_STATE)
            dpre_ref[:, bs] = d_b[g] * dsilu[:, bs]
            dpre_ref[:, cs] = d_c[g] * dsilu[:, cs]

    t = nb * seq
    vec = _const_spec((1, SSD_INNER))
    small = _const_spec((1, SSD_HEADS))
    return pl.pallas_call(
        body, name="ssd_bwd", grid=(nb, nc),
        in_specs=specs + [pl.BlockSpec((CHUNK, CONV_CH), lambda b, c: (row(b, c), 0)),
                          pl.BlockSpec((CHUNK, SSD_INNER), lambda b, c: (row(b, c), 0)),
                          pl.BlockSpec((CHUNK, SSD_INNER), lambda b, c: (row(b, c), 0)),
                          pl.BlockSpec((1, 1, 2, SSD_STATE, GROUP_W), lambda b, c: (b, cidx(c), 0, 0, 0)),
                          small, small, vec, vec],
        out_specs=[pl.BlockSpec((CHUNK, SSD_INNER), lambda b, c: (row(b, c), 0)),
                   pl.BlockSpec((CHUNK, CONV_CH), lambda b, c: (row(b, c), 0)),
                   pl.BlockSpec((CHUNK, 128), lambda b, c: (row(b, c), 0)),
                   vec, vec, small, small],
        out_shape=[_out((t, SSD_INNER), BF), _out((t, CONV_CH), F32),
                   _out((t, 128), BF), _out((1, SSD_INNER), F32),
                   _out((1, SSD_INNER), F32), _out((1, SSD_HEADS), F32),
                   _out((1, SSD_HEADS), F32)],
        scratch_shapes=[pltpu.VMEM((2, SSD_STATE, GROUP_W), F32), pltpu.VMEM((CHUNK, SSD_INNER), F32),
                        pltpu.VMEM((CHUNK, SSD_INNER), F32)],
        compiler_params=_cp(("arbitrary", "arbitrary"), 48),
    )(*_pin(proj, proj, proj, pre, d_yssd, yssm, h_prev, dt_bias, a_log, dskip_e, g_ssd))


def _grad_w_in_t(d_upool, d_z, d_uxbc, d_udt, u1):
    t, d = u1.shape
    tk = 512
    n_z, n_x = SSD_INNER // tk, CONV_CH // tk

    def body(p_ref, z_ref, x_ref, dt_ref, u_ref, o_ref):
        i = pl.program_id(0)

        @pl.when(i == 0)
        def _():
            o_ref[...] = _tn(p_ref[...], u_ref[...]).astype(BF)

        @pl.when((i >= 1) & (i < 1 + n_z))
        def _():
            o_ref[...] = _tn(z_ref[...], u_ref[...]).astype(BF)

        @pl.when((i >= 1 + n_z) & (i < 1 + n_z + n_x))
        def _():
            o_ref[...] = _tn(x_ref[...], u_ref[...]).astype(BF)

        @pl.when(i == 1 + n_z + n_x)
        def _():
            o_ref[0:128, :] = _tn(dt_ref[...], u_ref[...]).astype(BF)

    return pl.pallas_call(
        body, name="grad_w_in", grid=(2 + n_z + n_x,),
        in_specs=[pl.BlockSpec((t, tk), lambda i: (0, 0)),
                  pl.BlockSpec((t, tk), lambda i: (0, jnp.clip(i - 1, 0, n_z - 1))),
                  pl.BlockSpec((t, tk), lambda i: (0, jnp.clip(i - 1 - n_z, 0, n_x - 1))),
                  pl.BlockSpec((t, 128), lambda i: (0, 0)), pl.BlockSpec((t, d), lambda i: (0, 0))],
        out_specs=pl.BlockSpec((tk, d), lambda i: (i, 0)),
        out_shape=_out((IN_PAD, d), BF),
        compiler_params=_cp(("parallel",), 56))(*_pin(d_upool, d_z, d_uxbc, d_udt, u1))


def _conv_bwd(d_pre, proj, conv_w, nb, seq):
    ts = min(256, seq)
    nt = seq // ts
    hb = ts // CONV_HALO
    last_block = nb * seq // CONV_HALO - 1
    n_ext = CHUNK + CONV_HALO

    def body(dp_ref, dnext_ref, u_ref, cw_ref, du_ref, gw_ref, gb_ref):
        b = pl.program_id(0)
        i = pl.program_id(1)

        @pl.when((b == 0) & (i == 0))
        def _():
            gw_ref[...] = jnp.zeros_like(gw_ref)
            gb_ref[...] = jnp.zeros_like(gb_ref)

        for c0 in range(0, CONV_CH, 128):
            cs = slice(c0, c0 + 128)
            cw = cw_ref[:, cs]
            gw = [0.0] * 4
            gb = 0.0
            for r0 in range(0, ts, CHUNK):
                dp = dp_ref[r0:r0 + CHUNK, cs]
                u = u_ref[r0:r0 + CHUNK, cs]
                if r0 + CHUNK < ts:
                    below = dp_ref[r0 + CHUNK:r0 + CHUNK + CONV_HALO, cs]
                else:
                    below = jnp.where(i == nt - 1, 0.0, dnext_ref[:, cs])
                ext_d = jnp.concatenate([dp, below], 0)
                du = dp * cw[3:4]
                gw[3] = gw[3] + jnp.sum(dp * u, 0, keepdims=True)
                for k in (2, 1, 0):
                    shifted = pltpu.roll(ext_d, n_ext - (3 - k), 0)[:CHUNK]
                    du = du + shifted * cw[k:k + 1]
                    gw[k] = gw[k] + jnp.sum(shifted * u, 0, keepdims=True)
                gb = gb + jnp.sum(dp, 0, keepdims=True)
                du_ref[r0:r0 + CHUNK, cs] = du.astype(BF)
            for k in range(4):
                gw_ref[k:k + 1, cs] += gw[k]
            gb_ref[:, cs] += gb

    return pl.pallas_call(
        body, name="conv_bwd", grid=(nb, nt),
        in_specs=[pl.BlockSpec((ts, CONV_CH), lambda b, i: (b * nt + i, 0)),
                  pl.BlockSpec((CONV_HALO, CONV_CH), lambda b, i: (jnp.minimum((b * nt + i + 1) * hb, last_block), 0)),
                  pl.BlockSpec((ts, CONV_CH), lambda b, i: (b * nt + i, 1)),
                  pl.BlockSpec((4, CONV_CH), lambda b, i: (0, 0))],
        out_specs=[pl.BlockSpec((ts, CONV_CH), lambda b, i: (b * nt + i, 0)),
                   pl.BlockSpec((8, CONV_CH), lambda b, i: (0, 0)), pl.BlockSpec((1, CONV_CH), lambda b, i: (0, 0))],
        out_shape=[_out((nb * seq, CONV_CH), BF), _out((8, CONV_CH), F32),
                   _out((1, CONV_CH), F32)],
        compiler_params=_cp(("arbitrary", "arbitrary"), 48))(*_pin(d_pre, d_pre, proj, conv_w))


def _in_proj_bwd(d_parts, w_in_t, x, dh1, mod3, g_mix, seq, token):
    t, d = x.shape
    nb = t // seq
    tm = min(ROW_TILE, seq)
    tps = seq // tm

    widths = [p.shape[1] for p in d_parts]

    def body(d0_ref, d1_ref, d2_ref, d3_ref, w_ref, x_ref, dh1_ref, mod_ref, g_ref, tok_ref, gx_ref, acc_ref, gg_ref):
        i = pl.program_id(0)

        @pl.when(i == 0)
        def _():
            gg_ref[...] = jnp.zeros_like(gg_ref)

        @pl.when(i % tps == 0)
        def _():
            acc_ref[...] = jnp.zeros_like(acc_ref)

        gg = a_shift = a_scale = 0.0
        for rows in _sub_rows(tm):
            du = None
            off = 0
            for p_ref, wd in zip((d0_ref, d1_ref, d2_ref, d3_ref), widths):
                part = _nn(p_ref[rows, :], w_ref[off:off + wd, :])
                du = part if du is None else du + part
                off += wd
            xv = x_ref[rows, :]
            r = lax.rsqrt(jnp.mean(xv * xv, -1, keepdims=True) + EPS)
            hh = xv * r
            n1 = hh * g_ref[...]
            dn1 = du * (1.0 + mod_ref[0, 1:2, :])
            dhat = dn1 * g_ref[...]
            gx_ref[rows, :] = dh1_ref[rows, :] + r * (dhat - hh * jnp.mean(dhat * hh, -1, keepdims=True))
            gg = gg + jnp.sum(dn1 * hh, 0, keepdims=True)
            a_shift = a_shift + jnp.sum(du, 0, keepdims=True)
            a_scale = a_scale + jnp.sum(du * n1, 0, keepdims=True)
        gg_ref[...] += gg
        acc_ref[0, 0:1, :] += a_shift
        acc_ref[0, 1:2, :] += a_scale

    row = lambda i: (i, 0)
    vec = pl.BlockSpec((1, d), lambda i: (0, 0))
    return pl.pallas_call(
        body, name="in_proj_bwd", grid=(t // tm,),
        in_specs=[pl.BlockSpec((tm, wd), row) for wd in widths] +
                 [_RESIDENT, pl.BlockSpec((tm, d), row),
                  pl.BlockSpec((tm, d), row), pl.BlockSpec((1, N_MOD, d), lambda i: (i // tps, 0, 0)), vec, _token_spec()],
        out_specs=[pl.BlockSpec((tm, d), row), pl.BlockSpec((1, 8, d), lambda i: (i // tps, 0, 0)), vec],
        out_shape=[_out((t, d), F32), _out((nb, 8, d), F32),
                   _out((1, d), F32)],
        compiler_params=_cp(("arbitrary",), 40))(*_pin(*d_parts), w_in_t, *_pin(x, dh1, mod3, g_mix, token))


_VEC_LAYOUT = (("g_mix", 1024), ("conv_b", 1536), ("g_ssd", 1024), ("pool_scale", 512), ("g_mlp", 1024),
               ("g_final", 1024), ("dt_bias", 128), ("a_log", 128), ("d_skip_lanes", 1024), ("sq_err", 1024))
_VEC_OFFSET = {}
_off = 0
for _name, _n in _VEC_LAYOUT:
    _VEC_OFFSET[_name] = _off
    _off += _n
_VEC_LANES = _off
_SMALL_PARAMS = ("b_ada", "g_mix", "conv_w", "conv_b", "dt_bias", "a_log", "d_skip", "g_ssd", "w_pool", "pool_scale",
                 "g_mlp", "g_final")


def _pack_vec(parts):
    cols = []
    for name, n in _VEC_LAYOUT:
        v = parts[name]
        if v.shape[1] < n:
            v = jnp.pad(v, ((0, 0), (0, n - v.shape[1])))
        cols.append(v)
    return jnp.concatenate(cols, 1)


def _small_adam(vec_all, wpool_all, convw_all, dmod_all, params):
    names = _SMALL_PARAMS
    nin = 4 + 3 * len(names)

    def body(*refs):
        vec_ref, wp_ref, cw_ref, dm_ref = refs[:4]
        prm = {n: refs[4 + 3 * i:7 + 3 * i] for i, n in enumerate(names)}
        loss_ref = refs[nin]
        outs = {n: refs[nin + 1 + 4 * i:nin + 5 + 4 * i] for i, n in enumerate(names)}
        vsum = vec_ref[0]
        for s in range(1, N_DEV):
            vsum = vsum + vec_ref[s]

        def lanes(name, n):
            off = _VEC_OFFSET[name]
            return vsum[:, off:off + n]

        grads = {n: lanes(n, prm[n][0].shape[1]) for n in ("g_mix", "conv_b", "g_ssd", "pool_scale", "g_mlp", "g_final", "dt_bias")}
        grads["a_log"] = lanes("a_log", SSD_HEADS) * (-jnp.exp(prm["a_log"][0][...]))
        per_lane = jnp.broadcast_to(lanes("d_skip_lanes", SSD_INNER), (8, SSD_INNER))
        grads["d_skip"] = _exact_nn(per_lane, _head_reduce_matrix(SSD_INNER, SSD_HEADS))[0:1]
        gwp = wp_ref[0].astype(F32)
        gcw = cw_ref[0]
        gb = jnp.sum(dm_ref[0], 0, keepdims=True)
        for s in range(1, N_DEV):
            gwp = gwp + wp_ref[s].astype(F32)
            gcw = gcw + cw_ref[s]
            gb = gb + jnp.sum(dm_ref[s], 0, keepdims=True)
        grads["w_pool"] = gwp
        grads["conv_w"] = gcw[0:4]
        grads["b_ada"] = gb
        total = jnp.sum(lanes("sq_err", D_MODEL), 1, keepdims=True) * (0.5 / D_MODEL)
        loss_ref[...] = jnp.broadcast_to(total, loss_ref.shape)
        for n in names:
            w_ref, m_ref, v_ref = prm[n]
            g = grads[n]
            d, m2, v2 = _adam_math(w_ref[...], g, m_ref[...], v_ref[...])
            g_ref, d_ref, m2_ref, v2_ref = outs[n]
            g_ref[...] = g
            d_ref[...] = d
            m2_ref[...] = m2
            v2_ref[...] = v2

    flat = [vec_all, wpool_all, convw_all, dmod_all]
    out_shape = [jax.ShapeDtypeStruct((1, 128), F32)]
    for n in names:
        flat += list(params[n])
        out_shape += [jax.ShapeDtypeStruct(params[n][0].shape, F32)] * 4
    vm = pl.BlockSpec(memory_space=pltpu.VMEM)
    res = pl.pallas_call(body, name="small_adam", out_shape=out_shape, in_specs=[vm] * len(flat),
                         out_specs=[vm] * len(out_shape), compiler_params=_cp(vmem_mb=48))(*flat)
    return res[0], {n: res[1 + 4 * i:5 + 4 * i] for i, n in enumerate(names)}


_WEIGHTS = ("w_ada", "b_ada", "g_mix", "w_in", "conv_w", "conv_b", "dt_bias", "a_log", "d_skip", "g_ssd", "w_pool",
            "pool_scale", "w_out", "g_mlp", "w_up", "w_down", "g_final")


def _local_step(x2, tg2, mod3, seq, w_in_t, first_token, weights_later, start_reduce, conv_w_full, sp):
    t, d = x2.shape
    nb = t // seq
    dskip_e = jnp.repeat(sp["d_skip"], SSD_HEAD_DIM, axis=1)
    proj, u1 = _in_proj(x2, mod3, sp["g_mix"], w_in_t, seq, first_token)
    y_pool, p = _pool_fwd(proj, sp["w_pool"], sp["pool_scale"], nb, seq)
    y_ssd, yssm, h_prev, pre = _ssd_fwd(proj, conv_w_full, sp["conv_b"], sp["dt_bias"], sp["a_log"], dskip_e, sp["g_ssd"], nb, seq)
    w_out_f, w_up4, w_down4 = weights_later(y_ssd)
    w_down_f = w_down4.reshape(D_FF, d)
    h1, o, u2 = _out_proj(y_pool, y_ssd, w_out_f, x2, mod3, sp["g_mlp"], seq)
    a_up = _mlp_up(u2, w_up4)
    d_dn, dh2, sq, gg_final, d_gf = _mlp_down_loss(a_up, w_down_f, h1, mod3, sp["g_final"], tg2, seq)

    gw_down = _tn_matmul(a_up, d_dn, 512, d, "grad_w_down", square_relu=True)
    tok = start_reduce("w_down", gw_down.reshape(N_CHIPS, D_FF // N_CHIPS, d))
    d_a = _mlp_down_bwd(d_dn, w_down4, a_up, tok)
    gw_up4 = _tn_matmul(u2, d_a, 512, d, "grad_w_up", out3=True)
    tok = start_reduce("w_up", gw_up4)
    dh1, d_o, accf, gg_mlp = _mlp_up_bwd(d_a, w_up4, h1, dh2, o, mod3, sp["g_mlp"], seq, tok)
    gw_out_pool = _tn_matmul(y_pool, d_o, 512, d, "grad_w_out_pool")
    gw_out_ssd = _tn_matmul(y_ssd, d_o, 512, d, "grad_w_out_ssd")
    gw_out = jnp.concatenate([gw_out_pool, gw_out_ssd], 0)
    tok = start_reduce("w_out", gw_out.reshape(N_CHIPS, gw_out.shape[0] // N_CHIPS, d))
    d_ypool, d_yssd = _out_proj_bwd(d_o, w_out_f, tok)
    d_upool, gw_pool, g_ps = _pool_bwd(d_ypool, p, sp["w_pool"], sp["pool_scale"], nb, seq)
    d_z, d_pre, d_udt, gg_ssd, gdsk, ga, gdtb = _ssd_bwd(proj, pre, d_yssd, yssm, h_prev, sp["dt_bias"], sp["a_log"],
                                                        dskip_e, sp["g_ssd"], nb, seq)
    d_uxbc, gconvw, gconvb = _conv_bwd(d_pre, proj, conv_w_full, nb, seq)
    gw_in_t = _grad_w_in_t(d_upool, d_z, d_uxbc, d_udt, u1)
    tok = start_reduce("w_in", gw_in_t[:IN_WIDTH].reshape(N_CHIPS, IN_WIDTH // N_CHIPS, d))
    gx, accm, gg_mix = _in_proj_bwd([d_upool, d_z, d_uxbc, d_udt], w_in_t, x2, dh1, mod3, sp["g_mix"], seq, tok)

    d_mod = jnp.concatenate([accm[:, 0], accm[:, 1], accf[:, 2], accf[:, 0], accf[:, 1], d_gf[:, 0]], 1)
    vec = _pack_vec({"g_mix": gg_mix, "conv_b": gconvb, "g_ssd": gg_ssd, "pool_scale": g_ps, "g_mlp": gg_mlp,
                     "g_final": gg_final, "dt_bias": gdtb, "a_log": ga, "d_skip_lanes": gdsk, "sq_err": sq})
    return gx, d_mod, vec, gw_pool, gconvw


def kernel(x, c, w_ada, b_ada, g_mix, w_in, conv_w, conv_b, dt_bias, a_log, d_skip, g_ssd, w_pool, pool_scale, w_out, g_mlp, w_up, w_down, g_final, loss_target, m_w_ada, m_b_ada, m_g_mix, m_w_in, m_conv_w, m_conv_b, m_dt_bias, m_a_log, m_d_skip, m_g_ssd, m_w_pool, m_pool_scale, m_w_out, m_g_mlp, m_w_up, m_w_down, m_g_final, v_w_ada, v_b_ada, v_g_mix, v_w_in, v_conv_w, v_conv_b, v_dt_bias, v_a_log, v_d_skip, v_g_ssd, v_w_pool, v_pool_scale, v_w_out, v_g_mlp, v_w_up, v_w_down, v_g_final):
    nb, seq, d = x.shape
    t = nb * seq
    xi, yi, ci = _mesh_pos()
    chip = 2 * xi + yi
    me = 4 * xi + 2 * yi + ci
    ada_cols = w_ada.shape[2]
    conv_cols = conv_w.shape[2]
    in_cols = w_in.shape[2]
    w_in_s, m_w_in_s, v_w_in_s = w_in[0].T, m_w_in[0].T, v_w_in[0].T

    w_in_b = w_in_s.astype(BF)
    i_send, i_recv, i_src, i_land, in_token = _ici_start(
        [w_in_b], [jax.ShapeDtypeStruct((N_CHIPS,) + w_in_b.shape, BF)], _gather_sent, _gather_landing, "gather_start_w_in",
        after=w_in_b)

    c8, convw8 = _all_gather_small([c, conv_w[0]], token=in_token)
    c_all = c8.reshape(N_DEV * nb, d)
    conv_w_full = convw8[0::2].transpose(1, 0, 2).reshape(4, N_CHIPS * conv_cols)
    b_shard = lax.dynamic_slice(b_ada, (0, chip * ada_cols), (1, ada_cols))
    mod_part, c_act = _ada_mod(c_all, w_ada[0], b_shard)
    (mod8,) = _all_gather_small([mod_part])
    mod_all = mod8[0::2].transpose(1, 0, 2).reshape(N_DEV * nb, N_CHIPS * ada_cols)
    mod3 = lax.dynamic_slice(mod_all, (nb * me, 0), (nb, N_CHIPS * ada_cols)).reshape(nb, N_MOD, d)

    in_shard, in_land = _ici_wait(i_send, i_recv, i_src, i_land, mod8, _gather_sent, _gather_landing, "gather_wait_w_in")
    (w_in4,) = _gather_finish(in_land, in_shard)
    w_in_t = jnp.pad(w_in4.reshape(N_CHIPS * in_cols, d), ((0, IN_PAD - N_CHIPS * in_cols), (0, 0)))
    later = [w_out[0].astype(BF), w_up[0].astype(BF), w_down[0].astype(BF)]
    g_send, g_recv, g_src, g_land, first_token = _ici_start(
        later, [jax.ShapeDtypeStruct((N_CHIPS,) + s.shape, BF) for s in later], _gather_sent, _gather_landing, "gather_start",
        after=w_in4)

    def weights_later(after):
        shards, lands = _ici_wait(g_send, g_recv, g_src, g_land, after, _gather_sent, _gather_landing, "gather_wait")
        w_out4, w_up4, w_down4 = _gather_finish(lands, shards)
        return w_out4.reshape(N_CHIPS * w_out.shape[1], d), w_up4, w_down4

    pending = {}

    def start_reduce(name, grad4):
        pending[name] = _reduce_start(grad4, "reduce_start_" + name)
        return pending[name][4]

    sp = dict(g_mix=g_mix, conv_b=conv_b, dt_bias=dt_bias, a_log=a_log, d_skip=d_skip, g_ssd=g_ssd,
              w_pool=w_pool[0], pool_scale=pool_scale, g_mlp=g_mlp, g_final=g_final.reshape(1, d))
    gx, d_mod, vec, gw_pool, gconvw = _local_step(
        x.reshape(t, d), loss_target.reshape(t, d), mod3, seq, w_in_t, first_token, weights_later, start_reduce, conv_w_full, sp)

    pos = jnp.stack([ci, chip, me]).astype(jnp.int32)
    small_parts = [vec, gw_pool.reshape(4 * POOL_GROUP, POOL_GROUP).astype(BF), gconvw, d_mod]
    s_send, s_recv, s_src, s_land, s_token = _exchange_start(small_parts, ALL_PEERS, "small_start")
    halves = []
    for name in ("w_in", "w_out", "w_up", "w_down"):
        r_send, r_recv, r_src, r_land, _ = pending[name]
        own, recv = _reduce_wait(r_send, r_recv, r_src, r_land, s_token, "reduce_wait_" + name)
        halves.append(_sum_eight(recv, own, pos))
    h_send, h_recv, h_src, h_land, h_token = _exchange_start(halves, SIBLING, "halves_start")

    s_own, s_got = _exchange_wait(s_send, s_recv, s_src, s_land, ALL_PEERS, h_token, "small_wait")
    vec8, wpool8, convw8g, dmod8 = [lax.dynamic_update_slice(got, mine[None], (me,) + (0,) * mine.ndim)
                                    for got, mine in zip(s_got, s_own)]
    convw8s = lax.dynamic_slice(convw8g, (0, 0, chip * conv_cols), (N_DEV, 8, conv_cols))
    m_in = dict(b_ada=m_b_ada, g_mix=m_g_mix, conv_w=m_conv_w[0], conv_b=m_conv_b, dt_bias=m_dt_bias, a_log=m_a_log,
                d_skip=m_d_skip, g_ssd=m_g_ssd, w_pool=m_w_pool.reshape(4 * POOL_GROUP, POOL_GROUP), pool_scale=m_pool_scale,
                g_mlp=m_g_mlp, g_final=m_g_final.reshape(1, d))
    v_in = dict(b_ada=v_b_ada, g_mix=v_g_mix, conv_w=v_conv_w[0], conv_b=v_conv_b, dt_bias=v_dt_bias, a_log=v_a_log,
                d_skip=v_d_skip, g_ssd=v_g_ssd, w_pool=v_w_pool.reshape(4 * POOL_GROUP, POOL_GROUP), pool_scale=v_pool_scale,
                g_mlp=v_g_mlp, g_final=v_g_final.reshape(1, d))
    w_small = dict(sp, b_ada=b_ada, conv_w=conv_w[0], w_pool=w_pool.reshape(4 * POOL_GROUP, POOL_GROUP))
    loss_row, small = _small_adam(vec8, wpool8, convw8s, dmod8, {n: (w_small[n], m_in[n], v_in[n]) for n in _SMALL_PARAMS})

    dmod_all = dmod8.reshape(N_DEV * nb, N_CHIPS * ada_cols)
    dmod_cols = lax.dynamic_slice(dmod_all, (0, chip * ada_cols), (N_DEV * nb, ada_cols))
    res = {n: tuple(r.reshape(w.shape) for r in small[n])
           for n, w in (("b_ada", b_ada), ("g_mix", g_mix), ("conv_w", conv_w), ("conv_b", conv_b), ("dt_bias", dt_bias),
                        ("a_log", a_log), ("d_skip", d_skip), ("g_ssd", g_ssd), ("w_pool", w_pool), ("pool_scale", pool_scale),
                        ("g_mlp", g_mlp), ("g_final", g_final))}
    g_ada, d_ada, m_ada, v_ada = _adam_ada(c_act.T.astype(BF), dmod_cols, w_ada[0], m_w_ada[0], v_w_ada[0])
    res["w_ada"] = (g_ada[None], d_ada[None], m_ada[None], v_ada[None])
    h_own, h_got = _exchange_wait(h_send, h_recv, h_src, h_land, SIBLING, g_ada, "halves_wait")
    g_in, dl, m2, v2 = _adam_big(h_own[0], h_got[0], w_in_s, m_w_in_s, v_w_in_s, pos)
    res["w_in"] = (g_in.T[None], dl.T[None], m2.T[None], v2.T[None])
    for i, (n, w, m, v) in enumerate((("w_out", w_out, m_w_out, v_w_out), ("w_up", w_up, m_w_up, v_w_up),
                                      ("w_down", w_down, m_w_down, v_w_down))):
        g, dl, m2, v2 = _adam_big(h_own[i + 1], h_got[i + 1], w[0], m[0], v[0], pos)
        res[n] = (g[None], dl[None], m2[None], v2[None])

    loss = loss_row[0, 0]
    return (loss, gx.reshape(nb, seq, d), *[res[n][0] for n in _WEIGHTS], *[res[n][1] for n in _WEIGHTS],
            *[res[n][2] for n in _WEIGHTS], *[res[n][3] for n in _WEIGHTS])
```

```python
import functools

import jax
import jax.numpy as jnp
from jax import lax
from jax.experimental import pallas as pl
from jax.experimental.pallas import tpu as pltpu

F32 = jnp.float32
BF = jnp.bfloat16
MESH = pl.DeviceIdType.MESH

EPS = 1e-5
D_MODEL = 1024
POOL_WIDTH = 512
POOL_WINDOWS = (2, 4, 8, 16)
POOL_GROUP = 128
SSD_INNER = 1024
SSD_HEADS = 16
SSD_HEAD_DIM = 64
SSD_STATE = 128
GROUP_W = 512
CHUNK = 128
CONV_CH = 1536
OFF_Z = 512
OFF_XBC = 1536
OFF_DT = 3072
IN_WIDTH = 3088
IN_PAD = 3200
D_FF = 4096
N_MOD = 6
N_CHIPS = 4
N_DEV = 8
HALO = 16
CONV_HALO = 8

ADAM_LR = 0.001
ADAM_B1 = 0.9
ADAM_B2 = 0.999
ADAM_EPS = 1e-08
ADAM_WD = 0.01
ADAM_STEP = 10

VMEM_BYTES_V7X = 64 * 1024 * 1024


def _cp(semantics=None, vmem_mb=48, **kw):
    args = dict(vmem_limit_bytes=vmem_mb * 1024 * 1024, **kw)
    if semantics is not None:
        args["dimension_semantics"] = semantics
    return pltpu.CompilerParams(**args)


def _out(shape, dtype):
    return pltpu.HBM(shape, dtype)


def _pin(*arrays):
    return [pltpu.with_memory_space_constraint(a, pltpu.HBM) for a in arrays]


def _nn(a, b):
    return jnp.dot(a, b, preferred_element_type=F32)


def _nt(a, b):
    return lax.dot_general(a, b, (((1,), (1,)), ((), ())), preferred_element_type=F32)


def _tn(a, b):
    return lax.dot_general(a, b, (((0,), (0,)), ((), ())), preferred_element_type=F32)


def _split3(v):
    hi = v.astype(BF)
    r1 = v - hi.astype(F32)
    mid = r1.astype(BF)
    lo = (r1 - mid.astype(F32)).astype(BF)
    return hi, mid, lo


def _exact_nn(v, m01):
    hi, mid, lo = _split3(v)
    return _nn(hi, m01) + _nn(mid, m01) + _nn(lo, m01)


def _exact_nn_left(m01, v):
    hi, mid, lo = _split3(v)
    return _nn(m01, hi) + _nn(m01, mid) + _nn(m01, lo)


def _exact_nt_left(m01, v):
    hi, mid, lo = _split3(v)
    return _nt(m01, hi) + _nt(m01, mid) + _nt(m01, lo)


def _sigmoid(v):
    return 1.0 / (1.0 + jnp.exp(-v))


def _iota(shape, dim):
    return lax.broadcasted_iota(jnp.int32, shape, dim)


def _head_expand_matrix(heads, width):
    return (_iota((heads, width), 1) // SSD_HEAD_DIM == _iota((heads, width), 0)).astype(BF)


def _head_reduce_matrix(width, heads):
    return (_iota((width, heads), 0) // SSD_HEAD_DIM == _iota((width, heads), 1)).astype(BF)


def _mesh_pos():
    return lax.axis_index("x"), lax.axis_index("y"), lax.axis_index("c")


def _flip(v, bit):
    return v + bit - 2 * bit * v


def _all_gather_small(arrays, token=None):
    n = len(arrays)
    extra = [] if token is None else [token]

    def body(*refs):
        in_refs, out_refs = refs[:n], refs[n + len(extra):2 * n + len(extra)]
        send_sems, recv_sems, local_sems = refs[2 * n + len(extra):]
        x, y, c = _mesh_pos()
        me = 4 * x + 2 * y + c
        local = []
        for a in range(n):
            cp = pltpu.make_async_copy(in_refs[a], out_refs[a].at[me], local_sems.at[a])
            cp.start()
            local.append(cp)
        sends = []
        for k in range(1, N_DEV):
            peer = (_flip(x, (k >> 2) & 1), _flip(y, (k >> 1) & 1), _flip(c, k & 1))
            for a in range(n):
                cp = pltpu.make_async_remote_copy(
                    src_ref=in_refs[a], dst_ref=out_refs[a].at[me],
                    send_sem=send_sems.at[a, k], recv_sem=recv_sems.at[a, k],
                    device_id=peer, device_id_type=MESH)
                cp.start()
                sends.append(cp)
        for k in range(1, N_DEV):
            px, py, pc = _flip(x, (k >> 2) & 1), _flip(y, (k >> 1) & 1), _flip(c, k & 1)
            src = 4 * px + 2 * py + pc
            for a in range(n):
                pltpu.make_async_remote_copy(
                    src_ref=in_refs[a], dst_ref=out_refs[a].at[src],
                    send_sem=send_sems.at[a, k], recv_sem=recv_sems.at[a, k],
                    device_id=(px, py, pc), device_id_type=MESH).wait_recv()
        for cp in sends:
            cp.wait_send()
        for cp in local:
            cp.wait()

    vm = pl.BlockSpec(memory_space=pltpu.VMEM)
    return pl.pallas_call(
        body, name="all_gather_small",
        out_shape=[jax.ShapeDtypeStruct((N_DEV,) + a.shape, a.dtype) for a in arrays],
        in_specs=[vm] * (n + len(extra)), out_specs=[vm] * n,
        scratch_shapes=[pltpu.SemaphoreType.DMA((n, N_DEV)), pltpu.SemaphoreType.DMA((n, N_DEV)),
                        pltpu.SemaphoreType.DMA((n,))],
        compiler_params=_cp(vmem_mb=32),
    )(*arrays, *extra)


def _weight_all_gather(shards):
    n = len(shards)
    any_spec = _HBM

    def body(*refs):
        in_refs, out_refs = refs[:n], refs[n:2 * n]
        send_sems, recv_sems, fsend_sems, frecv_sems, local_sems = refs[2 * n:]
        x, y, c = _mesh_pos()
        chip = 2 * x + y
        local, sends = [], []
        for a in range(n):
            cp = pltpu.make_async_copy(in_refs[a], out_refs[a].at[chip], local_sems.at[a])
            cp.start()
            local.append(cp)

        def half(a, which):
            hc = shards[a].shape[1] // 2
            return pl.ds(pl.multiple_of(which * hc, 128), hc)

        for j in range(1, N_CHIPS):
            px, py = _flip(x, (j >> 1) & 1), _flip(y, j & 1)
            for a in range(n):
                cp = pltpu.make_async_remote_copy(
                    src_ref=in_refs[a].at[:, half(a, c)], dst_ref=out_refs[a].at[chip, :, half(a, c)],
                    send_sem=send_sems.at[a, j], recv_sem=recv_sems.at[a, j],
                    device_id=(px, py, c), device_id_type=MESH)
                cp.start()
                sends.append(cp)
        for j in range(1, N_CHIPS):
            px, py = _flip(x, (j >> 1) & 1), _flip(y, j & 1)
            src = 2 * px + py
            for a in range(n):
                landed = out_refs[a].at[src, :, half(a, c)]
                pltpu.make_async_remote_copy(
                    src_ref=landed, dst_ref=landed, send_sem=send_sems.at[a, j], recv_sem=recv_sems.at[a, j],
                    device_id=(px, py, c), device_id_type=MESH).wait_recv()
                cp = pltpu.make_async_remote_copy(
                    src_ref=landed, dst_ref=landed, send_sem=fsend_sems.at[a, j], recv_sem=frecv_sems.at[a, j],
                    device_id=(x, y, 1 - c), device_id_type=MESH)
                cp.start()
                sends.append(cp)
        for j in range(1, N_CHIPS):
            px, py = _flip(x, (j >> 1) & 1), _flip(y, j & 1)
            src = 2 * px + py
            for a in range(n):
                other = out_refs[a].at[src, :, half(a, 1 - c)]
                pltpu.make_async_remote_copy(
                    src_ref=other, dst_ref=other, send_sem=fsend_sems.at[a, j], recv_sem=frecv_sems.at[a, j],
                    device_id=(x, y, 1 - c), device_id_type=MESH).wait_recv()
        for cp in sends:
            cp.wait_send()
        for cp in local:
            cp.wait()

    return pl.pallas_call(
        body, name="weight_all_gather",
        out_shape=[_out((N_CHIPS,) + s.shape, s.dtype) for s in shards],
        in_specs=[any_spec] * n, out_specs=[any_spec] * n,
        scratch_shapes=[pltpu.SemaphoreType.DMA((n, N_CHIPS))] * 4 + [pltpu.SemaphoreType.DMA((n,))],
        compiler_params=_cp(vmem_mb=16),
    )(*shards)


_HBM = pl.BlockSpec(memory_space=pltpu.HBM)
_SEM = pl.BlockSpec(memory_space=pltpu.SEMAPHORE)
_DATAFLOW = pltpu.SideEffectType.DATAFLOW_SIDE_EFFECTING


def _peer_chip(x, y, j):
    return _flip(x, (j >> 1) & 1), _flip(y, j & 1)


def _ici_start(srcs, land_shapes, sent, landing, name, after):
    n = len(srcs)

    def body(*refs):
        src_refs, land_refs = refs[:n], refs[n:2 * n]
        send_sems, recv_sems = refs[2 * n + 1], refs[2 * n + 2]
        token = refs[-1]
        x, y, c = _mesh_pos()
        for j in range(1, N_CHIPS):
            px, py = _peer_chip(x, y, j)
            for a in range(n):
                pltpu.make_async_remote_copy(
                    src_ref=sent(src_refs[a], c, 2 * px + py), dst_ref=landing(land_refs[a], c, 2 * x + y),
                    send_sem=send_sems.at[a * (N_CHIPS - 1) + j - 1], recv_sem=recv_sems.at[a * (N_CHIPS - 1) + j - 1],
                    device_id=(px, py, c), device_id_type=MESH).start()
        token[...] = jnp.zeros_like(token)

    sems = pltpu.SemaphoreType.DMA((n * (N_CHIPS - 1),))
    lands = [pltpu.with_memory_space_constraint(lax.empty(s.shape, s.dtype), pltpu.HBM) for s in land_shapes]
    outs = pl.pallas_call(
        body, name=name,
        out_shape=(sems, sems, *[pltpu.HBM(s.shape, s.dtype) for s in srcs],
                   *[pltpu.HBM(s.shape, s.dtype) for s in land_shapes], jax.ShapeDtypeStruct((8, 128), F32)),
        in_specs=[_HBM] * (2 * n + 1), out_specs=[_SEM, _SEM] + [_HBM] * (2 * n) + [pl.BlockSpec(memory_space=pltpu.VMEM)],
        input_output_aliases={i: 2 + i for i in range(2 * n)},
        compiler_params=pltpu.CompilerParams(has_side_effects=_DATAFLOW),
    )(*_pin(*srcs), *lands, *_pin(after))
    return outs[0], outs[1], outs[2:2 + n], outs[2 + n:2 + 2 * n], outs[-1]


def _ici_wait(send_sems, recv_sems, src_thru, land_thru, after, sent, landing, name):
    n = len(src_thru)

    def body(*refs):
        src_refs, land_refs = refs[:n], refs[n:2 * n]
        send_sems, recv_sems = refs[2 * n], refs[2 * n + 1]
        x, y, c = _mesh_pos()
        for j in range(1, N_CHIPS):
            px, py = _peer_chip(x, y, j)
            for a in range(n):
                cp = pltpu.make_async_remote_copy(
                    src_ref=sent(src_refs[a], c, 2 * px + py), dst_ref=landing(land_refs[a], c, 2 * px + py),
                    send_sem=send_sems.at[a * (N_CHIPS - 1) + j - 1], recv_sem=recv_sems.at[a * (N_CHIPS - 1) + j - 1],
                    device_id=(px, py, c), device_id_type=MESH)
                cp.wait_send()
                cp.wait_recv()

    outs = pl.pallas_call(
        body, name=name,
        out_shape=tuple(pltpu.HBM(s.shape, s.dtype) for s in (*src_thru, *land_thru)),
        in_specs=[_HBM] * (2 * n) + [_SEM, _SEM, _HBM], out_specs=[_HBM] * (2 * n),
        input_output_aliases={i: i for i in range(2 * n)},
        compiler_params=pltpu.CompilerParams(has_side_effects=_DATAFLOW),
    )(*src_thru, *land_thru, send_sems, recv_sems, *_pin(after))
    return outs[:n], outs[n:]


def _col_half(ref, which, lead=()):
    hc = ref.shape[-1] // 2
    return ref.at[(*lead, slice(None), pl.ds(pl.multiple_of(which * hc, 128), hc))]


def _gather_sent(ref, c, dst_chip):
    return _col_half(ref, c)


def _gather_landing(ref, c, src_chip):
    return _col_half(ref, c, lead=(src_chip,))


def _mod_sent(ref, c, dst_chip):
    return ref.at[2 * dst_chip + c]


def _mod_landing(ref, c, src_chip):
    return ref.at[src_chip]


def _reduce_copy(src_ref, land_ref, send_sems, recv_sems, k, receiving):
    x, y, c = _mesh_pos()
    px, py, pc = _flip(x, (k >> 2) & 1), _flip(y, (k >> 1) & 1), _flip(c, k & 1)
    hc = src_ref.shape[2] // 2
    src = src_ref.at[2 * px + py, :, pl.ds(pl.multiple_of(pc * hc, 128), hc)]
    slot = (4 * px + 2 * py + pc) if receiving else (4 * x + 2 * y + c)
    return pltpu.make_async_remote_copy(
        src_ref=src, dst_ref=land_ref.at[slot], send_sem=send_sems.at[k - 1], recv_sem=recv_sems.at[k - 1],
        device_id=(px, py, pc), device_id_type=MESH)


def _reduce_start(grad4, name):
    k4, r, cols = grad4.shape

    def body(src_ref, land_ref, send_sems, recv_sems, src_thru, land_thru, token):
        for k in range(1, N_DEV):
            _reduce_copy(src_ref, land_ref, send_sems, recv_sems, k, receiving=False).start()
        token[...] = jnp.zeros_like(token)

    sems = pltpu.SemaphoreType.DMA((N_DEV - 1,))
    land = pltpu.with_memory_space_constraint(lax.empty((N_DEV, r, cols // 2), grad4.dtype), pltpu.HBM)
    return pl.pallas_call(
        body, name=name,
        out_shape=(sems, sems, pltpu.HBM(grad4.shape, grad4.dtype), pltpu.HBM(land.shape, land.dtype),
                   jax.ShapeDtypeStruct((8, 128), F32)),
        in_specs=[_HBM, _HBM], out_specs=[_SEM, _SEM, _HBM, _HBM, pl.BlockSpec(memory_space=pltpu.VMEM)],
        input_output_aliases={0: 2, 1: 3},
        compiler_params=pltpu.CompilerParams(has_side_effects=_DATAFLOW),
    )(*_pin(grad4), land)


def _reduce_wait(send_sems, recv_sems, src_thru, land_thru, after, name):
    def body(src_ref, land_ref, send_sems, recv_sems, after_ref, src_out, land_out):
        for k in range(1, N_DEV):
            cp = _reduce_copy(src_ref, land_ref, send_sems, recv_sems, k, receiving=True)
            cp.wait_send()
            cp.wait_recv()

    return pl.pallas_call(
        body, name=name,
        out_shape=(pltpu.HBM(src_thru.shape, src_thru.dtype), pltpu.HBM(land_thru.shape, land_thru.dtype)),
        in_specs=[_HBM, _HBM, _SEM, _SEM, _HBM], out_specs=[_HBM, _HBM],
        input_output_aliases={0: 0, 1: 1},
        compiler_params=pltpu.CompilerParams(has_side_effects=_DATAFLOW),
    )(src_thru, land_thru, send_sems, recv_sems, *_pin(after))


def _peer_copy(src_ref, land_ref, send_sems, recv_sems, idx, k, receiving):
    x, y, c = _mesh_pos()
    px, py, pc = _flip(x, (k >> 2) & 1), _flip(y, (k >> 1) & 1), _flip(c, k & 1)
    if land_ref.shape[0] == N_DEV:
        slot = (4 * px + 2 * py + pc) if receiving else (4 * x + 2 * y + c)
    else:
        slot = pc if receiving else c
    return pltpu.make_async_remote_copy(
        src_ref=src_ref, dst_ref=land_ref.at[slot], send_sem=send_sems.at[idx], recv_sem=recv_sems.at[idx],
        device_id=(px, py, pc), device_id_type=MESH)


def _exchange_start(arrays, peers, name):
    n = len(arrays)

    def body(*refs):
        src_refs, land_refs = refs[:n], refs[n:2 * n]
        send_sems, recv_sems = refs[2 * n], refs[2 * n + 1]
        token = refs[-1]
        for j, k in enumerate(peers):
            for a in range(n):
                _peer_copy(src_refs[a], land_refs[a], send_sems, recv_sems, a * len(peers) + j, k, receiving=False).start()
        token[...] = jnp.zeros_like(token)

    sems = pltpu.SemaphoreType.DMA((n * len(peers),))
    n_slots = N_DEV if len(peers) > 1 else 2
    lands = [pltpu.with_memory_space_constraint(lax.empty((n_slots,) + a.shape, a.dtype), pltpu.HBM) for a in arrays]
    outs = pl.pallas_call(
        body, name=name,
        out_shape=(sems, sems, *[pltpu.HBM(a.shape, a.dtype) for a in arrays], *[pltpu.HBM(l.shape, l.dtype) for l in lands],
                   jax.ShapeDtypeStruct((8, 128), F32)),
        in_specs=[_HBM] * (2 * n), out_specs=[_SEM, _SEM] + [_HBM] * (2 * n) + [pl.BlockSpec(memory_space=pltpu.VMEM)],
        input_output_aliases={i: 2 + i for i in range(2 * n)},
        compiler_params=pltpu.CompilerParams(has_side_effects=_DATAFLOW),
    )(*_pin(*arrays), *lands)
    return outs[0], outs[1], outs[2:2 + n], outs[2 + n:2 + 2 * n], outs[-1]


def _exchange_wait(send_sems, recv_sems, src_thru, land_thru, peers, after, name):
    n = len(src_thru)

    def body(*refs):
        src_refs, land_refs = refs[:n], refs[n:2 * n]
        send_sems, recv_sems = refs[2 * n], refs[2 * n + 1]
        for j, k in enumerate(peers):
            for a in range(n):
                cp = _peer_copy(src_refs[a], land_refs[a], send_sems, recv_sems, a * len(peers) + j, k, receiving=True)
                cp.wait_send()
                cp.wait_recv()

    outs = pl.pallas_call(
        body, name=name,
        out_shape=tuple(pltpu.HBM(s.shape, s.dtype) for s in (*src_thru, *land_thru)),
        in_specs=[_HBM] * (2 * n) + [_SEM, _SEM, _HBM], out_specs=[_HBM] * (2 * n),
        input_output_aliases={i: i for i in range(2 * n)},
        compiler_params=pltpu.CompilerParams(has_side_effects=_DATAFLOW),
    )(*src_thru, *land_thru, send_sems, recv_sems, *_pin(after))
    return outs[:n], outs[n:]


ALL_PEERS = tuple(range(1, N_DEV))
SIBLING = (1,)


def _sum_eight(recv, grad4, pos):
    n, r, hc = recv.shape
    steps = 4
    tc = hc // steps

    def body(pos_ref, r_ref, g_ref, o_ref):
        me = pos_ref[2]
        o_ref[...] = jnp.zeros_like(o_ref)
        for s in range(n):
            @pl.when(me == s)
            def _():
                o_ref[...] += g_ref[0].astype(F32)

            @pl.when(me != s)
            def _():
                o_ref[...] += r_ref[s].astype(F32)

    grid_spec = pltpu.PrefetchScalarGridSpec(
        num_scalar_prefetch=1, grid=(steps,),
        in_specs=[pl.BlockSpec((n, r, tc), lambda i, pos: (0, 0, i)),
                  pl.BlockSpec((1, r, tc), lambda i, pos: (pos[1], 0, pos[0] * steps + i))],
        out_specs=pl.BlockSpec((r, tc), lambda i, pos: (0, i)))
    return pl.pallas_call(body, name="sum_eight", grid_spec=grid_spec, out_shape=_out((r, hc), F32),
                          compiler_params=_cp(("parallel",), 32))(pos, *_pin(recv, grad4))


def _gather_finish(lands, shards):
    n = len(lands)
    any_spec = _HBM

    def body(*refs):
        shard_refs, out_refs = refs[n:2 * n], refs[2 * n:3 * n]
        send_sems, recv_sems, local_sems = refs[3 * n:]
        x, y, c = _mesh_pos()
        chip = 2 * x + y
        local, sends = [], []
        for a in range(n):
            cp = pltpu.make_async_copy(shard_refs[a], out_refs[a].at[chip], local_sems.at[a])
            cp.start()
            local.append(cp)
        for j in range(1, N_CHIPS):
            px, py = _peer_chip(x, y, j)
            for a in range(n):
                landed = _col_half(out_refs[a], c, lead=(2 * px + py,))
                cp = pltpu.make_async_remote_copy(
                    src_ref=landed, dst_ref=landed, send_sem=send_sems.at[a, j], recv_sem=recv_sems.at[a, j],
                    device_id=(x, y, 1 - c), device_id_type=MESH)
                cp.start()
                sends.append(cp)
        for j in range(1, N_CHIPS):
            px, py = _peer_chip(x, y, j)
            for a in range(n):
                other = _col_half(out_refs[a], 1 - c, lead=(2 * px + py,))
                pltpu.make_async_remote_copy(
                    src_ref=other, dst_ref=other, send_sem=send_sems.at[a, j], recv_sem=recv_sems.at[a, j],
                    device_id=(x, y, 1 - c), device_id_type=MESH).wait_recv()
        for cp in sends:
            cp.wait_send()
        for cp in local:
            cp.wait()

    return pl.pallas_call(
        body, name="gather_finish",
        out_shape=[_out(l.shape, l.dtype) for l in lands],
        in_specs=[any_spec] * (2 * n), out_specs=[any_spec] * n,
        input_output_aliases={i: i for i in range(n)},
        scratch_shapes=[pltpu.SemaphoreType.DMA((n, N_CHIPS))] * 2 + [pltpu.SemaphoreType.DMA((n,))],
        compiler_params=_cp(vmem_mb=16),
    )(*lands, *shards)


def _halves_exchange(halves):
    n = len(halves)
    any_spec = _HBM

    def body(*refs):
        in_refs, out_refs = refs[:n], refs[n:2 * n]
        send_sems, recv_sems, local_sems = refs[2 * n:]
        x, y, c = _mesh_pos()
        copies, local = [], []
        for a in range(n):
            hc = halves[a].shape[1]
            mine = pl.ds(pl.multiple_of(c * hc, 128), hc)
            lc = pltpu.make_async_copy(in_refs[a], out_refs[a].at[:, mine], local_sems.at[a])
            lc.start()
            local.append(lc)
            cp = pltpu.make_async_remote_copy(
                src_ref=in_refs[a], dst_ref=out_refs[a].at[:, mine],
                send_sem=send_sems.at[a], recv_sem=recv_sems.at[a],
                device_id=(x, y, 1 - c), device_id_type=MESH)
            cp.start()
            copies.append(cp)
        for a in range(n):
            hc = halves[a].shape[1]
            theirs = pl.ds(pl.multiple_of((1 - c) * hc, 128), hc)
            pltpu.make_async_remote_copy(
                src_ref=in_refs[a], dst_ref=out_refs[a].at[:, theirs],
                send_sem=send_sems.at[a], recv_sem=recv_sems.at[a],
                device_id=(x, y, 1 - c), device_id_type=MESH).wait_recv()
        for cp in copies:
            cp.wait_send()
        for lc in local:
            lc.wait()

    return pl.pallas_call(
        body, name="halves_exchange",
        out_shape=[_out((h.shape[0], 2 * h.shape[1]), h.dtype) for h in halves],
        in_specs=[any_spec] * n, out_specs=[any_spec] * n,
        scratch_shapes=[pltpu.SemaphoreType.DMA((n,))] * 3,
        compiler_params=_cp(vmem_mb=16),
    )(*halves)


def _adam_math(w, g, m, v):
    m2 = ADAM_B1 * m + (1.0 - ADAM_B1) * g
    v2 = ADAM_B2 * v + (1.0 - ADAM_B2) * (g * g)
    m_hat = m2 / (1.0 - ADAM_B1 ** ADAM_STEP)
    v_hat = v2 / (1.0 - ADAM_B2 ** ADAM_STEP)
    delta = -ADAM_LR * (m_hat / (jnp.sqrt(v_hat) + ADAM_EPS) + ADAM_WD * w)
    return delta, m2, v2


def _adam_big(g_own, g_pair, w, m, v, pos):
    r, c = w.shape
    tc = c // 4

    def body(pos_ref, go_ref, gp_ref, w_ref, m_ref, v_ref, g_ref, d_ref, m2_ref, v2_ref):
        half = pl.program_id(0) // 2
        g = jnp.where(half == pos_ref[0], go_ref[...], gp_ref[0])
        d, m2, v2 = _adam_math(w_ref[...], g, m_ref[...], v_ref[...])
        g_ref[...] = g
        d_ref[...] = d
        m2_ref[...] = m2
        v2_ref[...] = v2

    spec = pl.BlockSpec((r, tc), lambda i, pos: (0, i))
    grid_spec = pltpu.PrefetchScalarGridSpec(
        num_scalar_prefetch=1, grid=(4,),
        in_specs=[pl.BlockSpec((r, tc), lambda i, pos: (0, i % 2)),
                  pl.BlockSpec((1, r, tc), lambda i, pos: (1 - pos[0], 0, i % 2)), spec, spec, spec],
        out_specs=[spec] * 4)
    sh = _out((r, c), F32)
    return pl.pallas_call(body, name="adam_big", grid_spec=grid_spec, out_shape=[sh] * 4,
                          compiler_params=_cp(("parallel",), 32))(pos, *_pin(g_own, g_pair, w, m, v))


def _adam_ada(c_act_t, dmod_cols, w, m, v):
    r, c = w.shape
    tc = 512

    def body(ct_ref, dm_ref, w_ref, m_ref, v_ref, g_ref, d_ref, m2_ref, v2_ref):
        g = _nn(ct_ref[...], dm_ref[...].astype(BF))
        d, m2, v2 = _adam_math(w_ref[...], g, m_ref[...], v_ref[...])
        g_ref[...] = g
        d_ref[...] = d
        m2_ref[...] = m2
        v2_ref[...] = v2

    spec = pl.BlockSpec((r, tc), lambda i: (0, i))
    sh = _out((r, c), F32)
    return pl.pallas_call(
        body, name="adam_ada", grid=(c // tc,),
        in_specs=[pl.BlockSpec(c_act_t.shape, lambda i: (0, 0)), pl.BlockSpec((dmod_cols.shape[0], tc), lambda i: (0, i)),
                  spec, spec, spec],
        out_specs=[spec] * 4, out_shape=[sh] * 4, compiler_params=_cp(("parallel",), 48))(*_pin(c_act_t, dmod_cols, w, m, v))


def _ada_mod(c_all, w_shard, b_shard):
    nb, d = c_all.shape
    cols = w_shard.shape[1]
    tc = 512

    def body(c_ref, w_ref, b_ref, mod_ref, act_ref):
        cv = c_ref[...]
        act = cv * _sigmoid(cv)
        act_ref[...] = act
        mod_ref[...] = _nn(act.astype(BF), w_ref[...].astype(BF)) + b_ref[...]

    return pl.pallas_call(
        body, name="ada_mod", grid=(cols // tc,),
        in_specs=[pl.BlockSpec((nb, d), lambda i: (0, 0)), pl.BlockSpec((d, tc), lambda i: (0, i)),
                  pl.BlockSpec((1, tc), lambda i: (0, i))],
        out_specs=[pl.BlockSpec((nb, tc), lambda i: (0, i)), pl.BlockSpec((nb, d), lambda i: (0, 0))],
        out_shape=[_out((nb, cols), F32), _out((nb, d), F32)],
        compiler_params=_cp(("arbitrary",), 32))(*_pin(c_all, w_shard, b_shard))


SUB_ROWS = 256
ROW_TILE = 512


def _sub_rows(tm):
    return [slice(s, s + SUB_ROWS) for s in range(0, tm, SUB_ROWS)] if tm > SUB_ROWS else [slice(0, tm)]


_RESIDENT = pl.BlockSpec(memory_space=pltpu.VMEM)


def _token_spec():
    return pl.BlockSpec((8, 128), lambda *_: (0, 0))


def _in_proj(x, mod3, g_mix, w_in_t, seq, token):
    t, d = x.shape
    tm = min(ROW_TILE, seq)
    tps = seq // tm

    def body(x_ref, mod_ref, g_ref, w_ref, tok_ref, proj_ref, u1_ref):
        for rows in _sub_rows(tm):
            xv = x_ref[rows, :]
            r = lax.rsqrt(jnp.mean(xv * xv, -1, keepdims=True) + EPS)
            u = (xv * r * g_ref[...]) * (1.0 + mod_ref[0, 1:2, :]) + mod_ref[0, 0:1, :]
            ub = u.astype(BF)
            u1_ref[rows, :] = ub
            proj_ref[rows, :] = _nt(ub, w_ref[...])

    return pl.pallas_call(
        body, name="in_proj", grid=(t // tm,),
        in_specs=[pl.BlockSpec((tm, d), lambda i: (i, 0)), pl.BlockSpec((1, N_MOD, d), lambda i: (i // tps, 0, 0)),
                  pl.BlockSpec((1, d), lambda i: (0, 0)), _RESIDENT, _token_spec()],
        out_specs=[pl.BlockSpec((tm, IN_PAD), lambda i: (i, 0)), pl.BlockSpec((tm, d), lambda i: (i, 0))],
        out_shape=[_out((t, IN_PAD), F32), _out((t, d), BF)],
        compiler_params=_cp(("parallel",), 40))(*_pin(x, mod3, g_mix), w_in_t, *_pin(token))


def _pool_tile(seq):
    return min(512, seq)


def _pool_fwd(proj, w_pool, pool_scale, nb, seq):
    ts = _pool_tile(seq)
    nt = seq // ts

    def body(u_ref, halo_ref, wp_ref, ps_ref, yp_ref, p_ref):
        i = pl.program_id(1)
        halo = jnp.where(i == 0, 0.0, halo_ref[...])
        u = u_ref[...]
        ext = jnp.concatenate([halo, u], 0)
        tpos = i * ts + _iota((ts, 1), 0)
        for g, w in enumerate(POOL_WINDOWS):
            gs = slice(g * POOL_GROUP, (g + 1) * POOL_GROUP)
            s = ext[:, gs]
            sh = 1
            while sh < w:
                s = s + pltpu.roll(s, sh, 0)
                sh *= 2
            cnt = jnp.minimum(tpos + 1, w).astype(F32)
            pb = (s[HALO:] / cnt - u[:, gs]).astype(BF)
            p_ref[:, gs] = pb
            yp_ref[:, gs] = (_nn(pb, wp_ref[g].astype(BF)) * ps_ref[:, gs]).astype(BF)

    hb = ts // HALO
    return pl.pallas_call(
        body, name="pool_fwd", grid=(nb, nt),
        in_specs=[pl.BlockSpec((ts, POOL_WIDTH), lambda b, i: (b * nt + i, 0)),
                  pl.BlockSpec((HALO, POOL_WIDTH), lambda b, i: (jnp.maximum((b * nt + i) * hb - 1, 0), 0)),
                  pl.BlockSpec((4, POOL_GROUP, POOL_GROUP), lambda b, i: (0, 0, 0)),
                  pl.BlockSpec((1, POOL_WIDTH), lambda b, i: (0, 0))],
        out_specs=[pl.BlockSpec((ts, POOL_WIDTH), lambda b, i: (b * nt + i, 0))] * 2,
        out_shape=[_out((nb * seq, POOL_WIDTH), BF)] * 2,
        compiler_params=_cp(("parallel", "parallel"), 32))(*_pin(proj, proj, w_pool, pool_scale))


def _conv_pre(uxbc, halo, cw, cb, first):
    halo = jnp.where(first, 0.0, halo)
    ext = jnp.concatenate([halo, uxbc], 0)
    pre = cb + uxbc * cw[3:4]
    for k in (2, 1, 0):
        pre = pre + pltpu.roll(ext, 3 - k, 0)[CONV_HALO:] * cw[k:k + 1]
    return pre


def _chunk_terms(pre, udt, dtb, alog):
    sg = _sigmoid(pre)
    xbc = pre * sg
    dtp = udt[:, :SSD_HEADS] + dtb
    dt = jnp.maximum(dtp, 0.0) + jnp.log(1.0 + jnp.exp(-jnp.abs(dtp)))
    a = -jnp.exp(alog)
    da = dt * a
    tril = (_iota((CHUNK, CHUNK), 0) >= _iota((CHUNK, CHUNK), 1))
    acum = _exact_nn_left(tril.astype(BF), da)
    eye = (_iota((SSD_HEADS, SSD_HEADS), 0) == _iota((SSD_HEADS, SSD_HEADS), 1)).astype(BF)
    acum_t = _exact_nt_left(eye, acum)
    expand = _head_expand_matrix(SSD_HEADS, SSD_INNER)
    acum_e = _exact_nn(acum, expand)
    dt_e = _exact_nn(dt, expand)
    last_e = acum_e[CHUNK - 1:CHUNK]
    return dict(pre=pre, sg=sg, xbc=xbc, dtp=dtp, dt=dt, a=a, acum=acum, acum_t=acum_t, tril=tril,
                dt_e=dt_e, e_a=jnp.exp(acum_e), d_out=jnp.exp(last_e - acum_e), c_dec=jnp.exp(last_e))


def _head_decay(r, h):
    seg = r["acum"][:, h:h + 1] - r["acum_t"][h:h + 1, :]
    return jnp.where(r["tril"], jnp.exp(jnp.minimum(seg, 0.0)), 0.0)


def _ssd_specs(nb, seq, reverse):
    nc = seq // CHUNK
    per = seq // CONV_HALO

    def cidx(c):
        return (nc - 1 - c) if reverse else c

    def row(b, c):
        return b * nc + cidx(c)

    specs = [
        pl.BlockSpec((CHUNK, CONV_CH), lambda b, c: (row(b, c), 1)),
        pl.BlockSpec((CONV_HALO, CONV_CH),
                     lambda b, c: (jnp.maximum(b * per + cidx(c) * (CHUNK // CONV_HALO) - 1, 0), 1)),
        pl.BlockSpec((CHUNK, GROUP_W), lambda b, c: (row(b, c), 1)),
        pl.BlockSpec((CHUNK, GROUP_W), lambda b, c: (row(b, c), 2)),
        pl.BlockSpec((CHUNK, 128), lambda b, c: (row(b, c), OFF_DT // 128)),
    ]
    return specs, row, cidx, nc


def _const_spec(shape):
    return pl.BlockSpec(shape, lambda b, c: (0,) * len(shape))


def _ssd_fwd(proj, conv_w, conv_b, dt_bias, a_log, dskip_e, g_ssd, nb, seq):
    specs, row, cidx, nc = _ssd_specs(nb, seq, reverse=False)

    def body(uxbc_ref, halo_ref, z0_ref, z1_ref, udt_ref, cw_ref, cb_ref, dtb_ref, alog_ref, dsk_ref, gs_ref,
             yssd_ref, yssm_ref, hprev_ref, pre_ref, h_ref, yd_ref):
        c = pl.program_id(1)

        @pl.when(c == 0)
        def _():
            h_ref[...] = jnp.zeros_like(h_ref)

        pre = _conv_pre(uxbc_ref[...], halo_ref[...], cw_ref[...], cb_ref[...], c == 0)
        pre_ref[...] = pre
        r = _chunk_terms(pre, udt_ref[...], dtb_ref[...], alog_ref[...])
        xbc = r["xbc"]
        xs = xbc[:, :SSD_INNER]
        xdt = xs * r["dt_e"]
        xdt_b = xdt.astype(BF)
        xdo_b = (xdt * r["d_out"]).astype(BF)
        hprev_ref[0, 0] = h_ref[...]
        for g in range(2):
            gs = slice(g * GROUP_W, (g + 1) * GROUP_W)
            bg = xbc[:, SSD_INNER + g * SSD_STATE:SSD_INNER + (g + 1) * SSD_STATE].astype(BF)
            cg = xbc[:, SSD_INNER + (2 + g) * SSD_STATE:SSD_INNER + (3 + g) * SSD_STATE].astype(BF)
            scores = _nt(cg, bg)
            hg = h_ref[g]
            y_off = _nn(cg, hg.astype(BF)) * r["e_a"][:, gs]
            for hh in range(8):
                h = g * 8 + hh
                m = (scores * _head_decay(r, h)).astype(BF)
                yd_ref[:, h * SSD_HEAD_DIM:(h + 1) * SSD_HEAD_DIM] = _nn(m, xdt_b[:, h * SSD_HEAD_DIM:(h + 1) * SSD_HEAD_DIM])
            h_ref[g] = hg * r["c_dec"][:, gs] + _tn(bg, xdo_b[:, gs])
            y = yd_ref[:, gs] + y_off + dsk_ref[:, gs] * xs[:, gs]
            yssm_ref[:, gs] = y
            zg = (z0_ref if g == 0 else z1_ref)[...]
            yg = y * (zg * _sigmoid(zg))
            rg = lax.rsqrt(jnp.mean(yg * yg, -1, keepdims=True) + EPS)
            yssd_ref[:, gs] = (yg * rg * gs_ref[:, gs]).astype(BF)

    t = nb * seq
    return pl.pallas_call(
        body, name="ssd_fwd", grid=(nb, nc),
        in_specs=specs + [_const_spec((4, CONV_CH)), _const_spec((1, CONV_CH)), _const_spec((1, SSD_HEADS)),
                          _const_spec((1, SSD_HEADS)), _const_spec((1, SSD_INNER)), _const_spec((1, SSD_INNER))],
        out_specs=[pl.BlockSpec((CHUNK, SSD_INNER), lambda b, c: (row(b, c), 0)),
                   pl.BlockSpec((CHUNK, SSD_INNER), lambda b, c: (row(b, c), 0)),
                   pl.BlockSpec((1, 1, 2, SSD_STATE, GROUP_W), lambda b, c: (b, c, 0, 0, 0)),
                   pl.BlockSpec((CHUNK, CONV_CH), lambda b, c: (row(b, c), 0))],
        out_shape=[_out((t, SSD_INNER), BF), _out((t, SSD_INNER), F32),
                   _out((nb, nc, 2, SSD_STATE, GROUP_W), F32), _out((t, CONV_CH), F32)],
        scratch_shapes=[pltpu.VMEM((2, SSD_STATE, GROUP_W), F32), pltpu.VMEM((CHUNK, SSD_INNER), F32)],
        compiler_params=_cp(("arbitrary", "arbitrary"), 48),
    )(*_pin(proj, proj, proj, proj, proj, conv_w, conv_b, dt_bias, a_log, dskip_e, g_ssd))


def _out_proj(y_pool, y_ssd, w_out, x, mod3, g_mlp, seq):
    t, d = x.shape
    tm = 512
    tps = seq // tm if seq >= tm else 1
    tm = min(tm, seq)

    def body(yp_ref, ys_ref, w_ref, x_ref, mod_ref, g_ref, h1_ref, o_ref, u2_ref):
        o = _nn(yp_ref[...], w_ref[0:POOL_WIDTH, :]) + _nn(ys_ref[...], w_ref[POOL_WIDTH:, :])
        o_ref[...] = o.astype(BF)
        h1 = x_ref[...] + mod_ref[0, 2:3, :] * o
        h1_ref[...] = h1
        r = lax.rsqrt(jnp.mean(h1 * h1, -1, keepdims=True) + EPS)
        u2_ref[...] = ((h1 * r * g_ref[...]) * (1.0 + mod_ref[0, 4:5, :]) + mod_ref[0, 3:4, :]).astype(BF)

    row = lambda i: (i, 0)
    return pl.pallas_call(
        body, name="out_proj", grid=(t // tm,),
        in_specs=[pl.BlockSpec((tm, POOL_WIDTH), row), pl.BlockSpec((tm, SSD_INNER), row),
                  _RESIDENT, pl.BlockSpec((tm, d), row),
                  pl.BlockSpec((1, N_MOD, d), lambda i: (i // tps, 0, 0)), pl.BlockSpec((1, d), lambda i: (0, 0))],
        out_specs=[pl.BlockSpec((tm, d), row)] * 3,
        out_shape=[_out((t, d), F32), _out((t, d), BF), _out((t, d), BF)],
        compiler_params=_cp(("parallel",), 48))(*_pin(y_pool, y_ssd), w_out, *_pin(x, mod3, g_mlp))


def _mlp_up(u2, w_up4):
    t, d = u2.shape
    tm = min(1024, t)
    nk, _, cols = w_up4.shape

    def body(u_ref, w_ref, a_ref):
        a_ref[...] = _nn(u_ref[...], w_ref[pl.program_id(1)]).astype(BF)

    return pl.pallas_call(
        body, name="mlp_up", grid=(t // tm, nk),
        in_specs=[pl.BlockSpec((tm, d), lambda i, k: (i, 0)), _RESIDENT],
        out_specs=pl.BlockSpec((tm, cols), lambda i, k: (i, k)),
        out_shape=_out((t, nk * cols), BF),
        compiler_params=_cp(("parallel", "parallel"), 32))(*_pin(u2), w_up4)


def _mlp_down_loss(a_up, w_down, h1, mod3, g_final, target, seq):
    t, d = h1.shape
    nb = t // seq
    tm = min(ROW_TILE, seq)
    tps = seq // tm

    def body(a_ref, w_ref, h1_ref, mod_ref, g_ref, tg_ref, ddn_ref, dh2_ref, sq_ref, gg_ref, dgf_ref):
        i = pl.program_id(0)

        @pl.when(i == 0)
        def _():
            sq_ref[...] = jnp.zeros_like(sq_ref)
            gg_ref[...] = jnp.zeros_like(gg_ref)

        @pl.when(i % tps == 0)
        def _():
            dgf_ref[...] = jnp.zeros_like(dgf_ref)

        gate = mod_ref[0, 5:6, :]
        sq = gg = dgf = 0.0
        for rows in _sub_rows(tm):
            f = jnp.square(jnp.maximum(a_ref[rows, :], 0))
            dn = _nn(f, w_ref[...])
            h2 = h1_ref[rows, :] + gate * dn
            r = lax.rsqrt(jnp.mean(h2 * h2, -1, keepdims=True) + EPS)
            hh = h2 * r
            err = hh * g_ref[...] - tg_ref[rows, :]
            dy = err * (1.0 / d)
            dhat = dy * g_ref[...]
            dh2 = r * (dhat - hh * jnp.mean(dhat * hh, -1, keepdims=True))
            dh2_ref[rows, :] = dh2
            ddn_ref[rows, :] = (dh2 * gate).astype(BF)
            sq = sq + jnp.sum(err * err, 0, keepdims=True)
            gg = gg + jnp.sum(dy * hh, 0, keepdims=True)
            dgf = dgf + jnp.sum(dh2 * dn, 0, keepdims=True)
        sq_ref[...] += sq
        gg_ref[...] += gg
        dgf_ref[0] += dgf

    row = lambda i: (i, 0)
    vec = pl.BlockSpec((1, d), lambda i: (0, 0))
    return pl.pallas_call(
        body, name="mlp_down_loss", grid=(t // tm,),
        in_specs=[pl.BlockSpec((tm, D_FF), row), _RESIDENT, pl.BlockSpec((tm, d), row),
                  pl.BlockSpec((1, N_MOD, d), lambda i: (i // tps, 0, 0)), vec, pl.BlockSpec((tm, d), row)],
        out_specs=[pl.BlockSpec((tm, d), row), pl.BlockSpec((tm, d), row), vec, vec,
                   pl.BlockSpec((1, 1, d), lambda i: (i // tps, 0, 0))],
        out_shape=[_out((t, d), BF), _out((t, d), F32), _out((1, d), F32),
                   _out((1, d), F32), _out((nb, 1, d), F32)],
        compiler_params=_cp(("arbitrary",), 44))(*_pin(a_up), w_down, *_pin(h1, mod3, g_final, target))


def _tn_matmul(a, b, tk, tn, name, square_relu=False, out3=False):
    t, kdim = a.shape
    ndim = b.shape[1]

    def body(a_ref, b_ref, o_ref):
        av = a_ref[...]
        if square_relu:
            av = jnp.square(jnp.maximum(av, 0))
        res = _tn(av, b_ref[...]).astype(BF)
        if out3:
            o_ref[0] = res
        else:
            o_ref[...] = res

    if out3:
        out_spec = pl.BlockSpec((1, tk, tn), lambda j, i: (j, i, 0))
        out_shape = _out((ndim // tn, kdim, tn), BF)
    else:
        out_spec = pl.BlockSpec((tk, tn), lambda j, i: (i, j))
        out_shape = _out((kdim, ndim), BF)
    return pl.pallas_call(
        body, name=name, grid=(ndim // tn, kdim // tk),
        in_specs=[pl.BlockSpec((t, tk), lambda j, i: (0, i)), pl.BlockSpec((t, tn), lambda j, i: (0, j))],
        out_specs=out_spec, out_shape=out_shape,
        compiler_params=_cp(("parallel", "parallel"), 56))(*_pin(a, b))


def _mlp_down_bwd(d_dn, w_down4, a_up, token):
    t, d = d_dn.shape
    tm = min(1024, t)
    nk, rows, _ = w_down4.shape

    def body(g_ref, w_ref, a_ref, tok_ref, o_ref):
        df = _nt(g_ref[...], w_ref[pl.program_id(1)])
        o_ref[...] = (df * (2.0 * jnp.maximum(a_ref[...], 0).astype(F32))).astype(BF)

    return pl.pallas_call(
        body, name="mlp_down_bwd", grid=(t // tm, nk),
        in_specs=[pl.BlockSpec((tm, d), lambda i, k: (i, 0)), _RESIDENT,
                  pl.BlockSpec((tm, rows), lambda i, k: (i, k)), _token_spec()],
        out_specs=pl.BlockSpec((tm, rows), lambda i, k: (i, k)),
        out_shape=_out((t, nk * rows), BF),
        compiler_params=_cp(("parallel", "parallel"), 32))(*_pin(d_dn), w_down4, *_pin(a_up, token))


def _mlp_up_bwd(d_a, w_up4, h1, dh2, o, mod3, g_mlp, seq, token):
    t, d = h1.shape
    nb = t // seq
    tm = min(ROW_TILE, seq)
    tps = seq // tm
    nk = w_up4.shape[0]
    cols = w_up4.shape[2]

    def body(da_ref, w_ref, h1_ref, dh2_ref, o_ref, mod_ref, g_ref, tok_ref, dh1_ref, do_ref, acc_ref, gg_ref):
        i = pl.program_id(0)

        @pl.when(i == 0)
        def _():
            gg_ref[...] = jnp.zeros_like(gg_ref)

        @pl.when(i % tps == 0)
        def _():
            acc_ref[...] = jnp.zeros_like(acc_ref)

        gg = a_shift = a_scale = a_gate = 0.0
        for rows in _sub_rows(tm):
            du = _nt(da_ref[rows, 0:cols], w_ref[0])
            for k in range(1, nk):
                du = du + _nt(da_ref[rows, k * cols:(k + 1) * cols], w_ref[k])
            h1 = h1_ref[rows, :]
            r = lax.rsqrt(jnp.mean(h1 * h1, -1, keepdims=True) + EPS)
            hh = h1 * r
            n2 = hh * g_ref[...]
            dn2 = du * (1.0 + mod_ref[0, 4:5, :])
            dhat = dn2 * g_ref[...]
            dh1 = dh2_ref[rows, :] + r * (dhat - hh * jnp.mean(dhat * hh, -1, keepdims=True))
            dh1_ref[rows, :] = dh1
            do_ref[rows, :] = (dh1 * mod_ref[0, 2:3, :]).astype(BF)
            gg = gg + jnp.sum(dn2 * hh, 0, keepdims=True)
            a_shift = a_shift + jnp.sum(du, 0, keepdims=True)
            a_scale = a_scale + jnp.sum(du * n2, 0, keepdims=True)
            a_gate = a_gate + jnp.sum(dh1 * o_ref[rows, :].astype(F32), 0, keepdims=True)
        gg_ref[...] += gg
        acc_ref[0, 0:1, :] += a_shift
        acc_ref[0, 1:2, :] += a_scale
        acc_ref[0, 2:3, :] += a_gate

    row = lambda i: (i, 0)
    vec = pl.BlockSpec((1, d), lambda i: (0, 0))
    return pl.pallas_call(
        body, name="mlp_up_bwd", grid=(t // tm,),
        in_specs=[pl.BlockSpec((tm, D_FF), row), _RESIDENT, pl.BlockSpec((tm, d), row),
                  pl.BlockSpec((tm, d), row), pl.BlockSpec((tm, d), row),
                  pl.BlockSpec((1, N_MOD, d), lambda i: (i // tps, 0, 0)), vec, _token_spec()],
        out_specs=[pl.BlockSpec((tm, d), row), pl.BlockSpec((tm, d), row),
                   pl.BlockSpec((1, 8, d), lambda i: (i // tps, 0, 0)), vec],
        out_shape=[_out((t, d), F32), _out((t, d), BF),
                   _out((nb, 8, d), F32), _out((1, d), F32)],
        compiler_params=_cp(("arbitrary",), 44))(*_pin(d_a), w_up4, *_pin(h1, dh2, o, mod3, g_mlp, token))


def _out_proj_bwd(d_o, w_out, token):
    t, d = d_o.shape
    tm = min(512, t)

    def body(g_ref, w_ref, tok_ref, dp_ref, ds_ref):
        gv = g_ref[...]
        dp_ref[...] = _nt(gv, w_ref[0:POOL_WIDTH, :])
        ds_ref[...] = _nt(gv, w_ref[POOL_WIDTH:, :])

    row = lambda i: (i, 0)
    return pl.pallas_call(
        body, name="out_proj_bwd", grid=(t // tm,),
        in_specs=[pl.BlockSpec((tm, d), row), _RESIDENT, _token_spec()],
        out_specs=[pl.BlockSpec((tm, POOL_WIDTH), row), pl.BlockSpec((tm, SSD_INNER), row)],
        out_shape=[_out((t, POOL_WIDTH), F32), _out((t, SSD_INNER), F32)],
        compiler_params=_cp(("parallel",), 32))(*_pin(d_o), w_out, *_pin(token))


def _pool_bwd(d_ypool, p, w_pool, pool_scale, nb, seq):
    ts = _pool_tile(seq)
    nt = seq // ts
    hb = ts // HALO
    last_block = nb * seq // HALO - 1

    def body(dy_ref, halo_ref, p_ref, wp_ref, ps_ref, du_ref, gw_ref, gs_ref):
        b = pl.program_id(0)
        i = pl.program_id(1)

        @pl.when((b == 0) & (i == 0))
        def _():
            gw_ref[...] = jnp.zeros_like(gw_ref)
            gs_ref[...] = jnp.zeros_like(gs_ref)

        halo = jnp.where(i == nt - 1, 0.0, halo_ref[...])
        dy = dy_ref[...]
        ext = jnp.concatenate([dy, halo], 0)
        tpos = i * ts + _iota((ts + HALO, 1), 0)
        n_ext = ts + HALO
        for g, w in enumerate(POOL_WINDOWS):
            gs = slice(g * POOL_GROUP, (g + 1) * POOL_GROUP)
            wg = wp_ref[g].astype(BF)
            pg = p_ref[:, gs]
            pw = _nn(pg, wg)
            gs_ref[:, gs] += jnp.sum(dy[:, gs] * pw, 0, keepdims=True)
            dpw = (ext[:, gs] * ps_ref[:, gs]).astype(BF)
            gw_ref[g] += _tn(pg, dpw[:ts])
            dp = _nt(dpw, wg)
            cnt = jnp.minimum(tpos + 1, w).astype(F32)
            s = dp / cnt
            sh = 1
            while sh < w:
                s = s + pltpu.roll(s, n_ext - sh, 0)
                sh *= 2
            du_ref[:, gs] = (s[:ts] - dp[:ts]).astype(BF)

    return pl.pallas_call(
        body, name="pool_bwd", grid=(nb, nt),
        in_specs=[pl.BlockSpec((ts, POOL_WIDTH), lambda b, i: (b * nt + i, 0)),
                  pl.BlockSpec((HALO, POOL_WIDTH), lambda b, i: (jnp.minimum((b * nt + i + 1) * hb, last_block), 0)),
                  pl.BlockSpec((ts, POOL_WIDTH), lambda b, i: (b * nt + i, 0)),
                  pl.BlockSpec((4, POOL_GROUP, POOL_GROUP), lambda b, i: (0, 0, 0)),
                  pl.BlockSpec((1, POOL_WIDTH), lambda b, i: (0, 0))],
        out_specs=[pl.BlockSpec((ts, POOL_WIDTH), lambda b, i: (b * nt + i, 0)),
                   pl.BlockSpec((4, POOL_GROUP, POOL_GROUP), lambda b, i: (0, 0, 0)),
                   pl.BlockSpec((1, POOL_WIDTH), lambda b, i: (0, 0))],
        out_shape=[_out((nb * seq, POOL_WIDTH), BF), _out((4, POOL_GROUP, POOL_GROUP), F32),
                   _out((1, POOL_WIDTH), F32)],
        compiler_params=_cp(("arbitrary", "arbitrary"), 32))(*_pin(d_ypool, d_ypool, p, w_pool, pool_scale))


def _ssd_bwd(proj, pre, d_yssd, yssm, h_prev, dt_bias, a_log, dskip_e, g_ssd, nb, seq):
    specs, row, cidx, nc = _ssd_specs(nb, seq, reverse=True)
    specs = specs[2:]

    def body(z0_ref, z1_ref, udt_ref, pre_ref, dys_ref, yssm_ref, hprev_ref,
             dtb_ref, alog_ref, dsk_ref, gs_ref,
             dz_ref, dpre_ref, dudt_ref, ggs_ref, gdsk_ref, ga_ref, gdtb_ref,
             g_ref, dxdt_ref, dyv_ref):
        b = pl.program_id(0)
        c = pl.program_id(1)

        @pl.when(c == 0)
        def _():
            g_ref[...] = jnp.zeros_like(g_ref)

        @pl.when((b == 0) & (c == 0))
        def _():
            ggs_ref[...] = jnp.zeros_like(ggs_ref)
            gdsk_ref[...] = jnp.zeros_like(gdsk_ref)
            ga_ref[...] = jnp.zeros_like(ga_ref)
            gdtb_ref[...] = jnp.zeros_like(gdtb_ref)

        r = _chunk_terms(pre_ref[...], udt_ref[...], dtb_ref[...], alog_ref[...])
        xbc = r["xbc"]
        xs = xbc[:, :SSD_INNER]
        dt_e = r["dt_e"]
        xdt = xs * dt_e
        xdt_b = xdt.astype(BF)
        reduce_m = _head_reduce_matrix(GROUP_W, 8)

        def head_sums(v):
            return _nn(v.astype(BF), reduce_m)

        onehot16 = lambda h: (_iota((1, SSD_HEADS), 1) == h).astype(F32)
        onecol16 = lambda h: (_iota((SSD_HEADS, 1), 0) == h).astype(F32)

        d_acum = jnp.zeros((CHUNK, SSD_HEADS), F32)
        d_acum_t = jnp.zeros((SSD_HEADS, CHUNK), F32)
        d_alast = jnp.zeros((1, SSD_HEADS), F32)
        place8 = lambda g: (_iota((8, SSD_HEADS), 1) == _iota((8, SSD_HEADS), 0) + 8 * g).astype(BF)
        d_b, d_c = [], []
        for g in range(2):
            gs = slice(g * GROUP_W, (g + 1) * GROUP_W)
            zg = (z0_ref if g == 0 else z1_ref)[...]
            sz = _sigmoid(zg)
            silu_z = zg * sz
            ys = yssm_ref[:, gs]
            yg = ys * silu_z
            rg = lax.rsqrt(jnp.mean(yg * yg, -1, keepdims=True) + EPS)
            yh = yg * rg
            dys = dys_ref[:, gs]
            ggs_ref[:, gs] += jnp.sum(dys * yh, 0, keepdims=True)
            dyh = dys * gs_ref[:, gs]
            dyg = rg * (dyh - yh * jnp.mean(dyh * yh, -1, keepdims=True))
            dy = dyg * silu_z
            dz_ref[:, gs] = (dyg * ys * (sz * (1.0 + zg * (1.0 - sz)))).astype(BF)
            gdsk_ref[:, gs] += jnp.sum(dy * xs[:, gs], 0, keepdims=True)
            dyv_ref[:, gs] = dy
            dy_b = dy.astype(BF)

            bg = xbc[:, SSD_INNER + g * SSD_STATE:SSD_INNER + (g + 1) * SSD_STATE].astype(BF)
            cg = xbc[:, SSD_INNER + (2 + g) * SSD_STATE:SSD_INNER + (3 + g) * SSD_STATE].astype(BF)
            scores = _nt(cg, bg)
            hg = hprev_ref[0, 0, g]
            hg_b = hg.astype(BF)
            gg = g_ref[g]
            gg_b = gg.astype(BF)
            e_a = r["e_a"][:, gs]
            d_out = r["d_out"][:, gs]
            c_dec = r["c_dec"][:, gs]
            zc = _nn(cg, hg_b)
            wv = e_a * dy
            wv_b = wv.astype(BF)
            da_g = head_sums(wv * zc)
            dcg = _nt(wv_b, hg_b)
            d_hprev = _tn(cg, wv_b)
            vg = _nn(bg, gg_b)
            dxdt_g = d_out * vg
            dd_out = head_sums(xdt[:, gs] * vg)
            dbg = _nt((xdt[:, gs] * d_out).astype(BF), gg_b)
            dcd = _exact_nn(jnp.sum(gg * hg, 0, keepdims=True), reduce_m)
            d_out8 = jnp.exp(r["acum"][CHUNK - 1:CHUNK, 8 * g:8 * g + 8] - r["acum"][:, 8 * g:8 * g + 8])
            c_dec8 = jnp.exp(r["acum"][CHUNK - 1:CHUNK, 8 * g:8 * g + 8])
            t8 = dd_out * d_out8
            d_alast = d_alast + _exact_nn(jnp.sum(t8, 0, keepdims=True) + dcd * c_dec8, place8(g))
            d_acum = d_acum + _exact_nn(da_g - t8, place8(g))
            dsc = jnp.zeros((CHUNK, CHUNK), F32)
            for hh in range(8):
                h = g * 8 + hh
                hs = slice(h * SSD_HEAD_DIM, (h + 1) * SSD_HEAD_DIM)
                lam = _head_decay(r, h)
                m = scores * lam
                dyh_b = dy_b[:, hh * SSD_HEAD_DIM:(hh + 1) * SSD_HEAD_DIM]
                dm = _nt(dyh_b, xdt_b[:, hs])
                tm_ = dm * m
                d_acum = d_acum + jnp.sum(tm_, 1, keepdims=True) * onehot16(h)
                d_acum_t = d_acum_t + onecol16(h) * jnp.sum(tm_, 0, keepdims=True)
                dsc = dsc + dm * lam
                dxdt_ref[:, hs] = _tn(m.astype(BF), dyh_b) + dxdt_g[:, hh * SSD_HEAD_DIM:(hh + 1) * SSD_HEAD_DIM]
            dsc_b = dsc.astype(BF)
            d_c.append(dcg + _nn(dsc_b, bg))
            d_b.append(dbg + _tn(dsc_b, cg))
            g_ref[g] = d_hprev + c_dec * gg

        eye = (_iota((CHUNK, CHUNK), 0) == _iota((CHUNK, CHUNK), 1)).astype(BF)
        d_acum = d_acum - _exact_nt_left(eye, d_acum_t)
        is_last = (_iota((CHUNK, 1), 0) == CHUNK - 1).astype(F32)
        d_acum = d_acum + is_last * d_alast
        triu = (_iota((CHUNK, CHUNK), 0) <= _iota((CHUNK, CHUNK), 1)).astype(BF)
        d_da = _exact_nn_left(triu, d_acum)
        dt = r["dt"]
        ga_ref[...] += jnp.sum(d_da * dt, 0, keepdims=True)
        dxdt = dxdt_ref[...]
        reduce16 = _head_reduce_matrix(SSD_INNER, SSD_HEADS)
        d_dt = d_da * r["a"] + _nn((dxdt * xs).astype(BF), reduce16)
        d_udt = d_dt * _sigmoid(r["dtp"])
        gdtb_ref[...] += jnp.sum(d_udt, 0, keepdims=True)
        dudt_ref[...] = jnp.zeros_like(dudt_ref)
        dudt_ref[:, 0:SSD_HEADS] = d_udt.astype(BF)
        pre, sg = r["pre"], r["sg"]
        dsilu = sg * (1.0 + pre * (1.0 - sg))
        dpre_ref[:, 0:SSD_INNER] = (dsk_ref[...] * dyv_ref[...] + dxdt * dt_e) * dsilu[:, 0:SSD_INNER]
        for g in range(2):
            bs = slice(SSD_INNER + g * SSD_STATE, SSD_INNER + (g + 1) * SSD_STATE)
            cs = slice(SSD_INNER + (2 + g) * SSD_STATE, SSD_INNER + (3 + g) * SSD_STATE)
            dpre_ref[:, bs] = d_b[g] * dsilu[:, bs]
            dpre_ref[:, cs] = d_c[g] * dsilu[:, cs]

    t = nb * seq
    vec = _const_spec((1, SSD_INNER))
    small = _const_spec((1, SSD_HEADS))
    return pl.pallas_call(
        body, name="ssd_bwd", grid=(nb, nc),
        in_specs=specs + [pl.BlockSpec((CHUNK, CONV_CH), lambda b, c: (row(b, c), 0)),
                          pl.BlockSpec((CHUNK, SSD_INNER), lambda b, c: (row(b, c), 0)),
                          pl.BlockSpec((CHUNK, SSD_INNER), lambda b, c: (row(b, c), 0)),
                          pl.BlockSpec((1, 1, 2, SSD_STATE, GROUP_W), lambda b, c: (b, cidx(c), 0, 0, 0)),
                          small, small, vec, vec],
        out_specs=[pl.BlockSpec((CHUNK, SSD_INNER), lambda b, c: (row(b, c), 0)),
                   pl.BlockSpec((CHUNK, CONV_CH), lambda b, c: (row(b, c), 0)),
                   pl.BlockSpec((CHUNK, 128), lambda b, c: (row(b, c), 0)),
                   vec, vec, small, small],
        out_shape=[_out((t, SSD_INNER), BF), _out((t, CONV_CH), F32),
                   _out((t, 128), BF), _out((1, SSD_INNER), F32),
                   _out((1, SSD_INNER), F32), _out((1, SSD_HEADS), F32),
                   _out((1, SSD_HEADS), F32)],
        scratch_shapes=[pltpu.VMEM((2, SSD_STATE, GROUP_W), F32), pltpu.VMEM((CHUNK, SSD_INNER), F32),
                        pltpu.VMEM((CHUNK, SSD_INNER), F32)],
        compiler_params=_cp(("arbitrary", "arbitrary"), 48),
    )(*_pin(proj, proj, proj, pre, d_yssd, yssm, h_prev, dt_bias, a_log, dskip_e, g_ssd))


def _grad_w_in_t(d_upool, d_z, d_uxbc, d_udt, u1):
    t, d = u1.shape
    tk = 512
    n_z, n_x = SSD_INNER // tk, CONV_CH // tk

    def body(p_ref, z_ref, x_ref, dt_ref, u_ref, o_ref):
        i = pl.program_id(0)

        @pl.when(i == 0)
        def _():
            o_ref[...] = _tn(p_ref[...], u_ref[...]).astype(BF)

        @pl.when((i >= 1) & (i < 1 + n_z))
        def _():
            o_ref[...] = _tn(z_ref[...], u_ref[...]).astype(BF)

        @pl.when((i >= 1 + n_z) & (i < 1 + n_z + n_x))
        def _():
            o_ref[...] = _tn(x_ref[...], u_ref[...]).astype(BF)

        @pl.when(i == 1 + n_z + n_x)
        def _():
            o_ref[0:128, :] = _tn(dt_ref[...], u_ref[...]).astype(BF)

    return pl.pallas_call(
        body, name="grad_w_in", grid=(2 + n_z + n_x,),
        in_specs=[pl.BlockSpec((t, tk), lambda i: (0, 0)),
                  pl.BlockSpec((t, tk), lambda i: (0, jnp.clip(i - 1, 0, n_z - 1))),
                  pl.BlockSpec((t, tk), lambda i: (0, jnp.clip(i - 1 - n_z, 0, n_x - 1))),
                  pl.BlockSpec((t, 128), lambda i: (0, 0)), pl.BlockSpec((t, d), lambda i: (0, 0))],
        out_specs=pl.BlockSpec((tk, d), lambda i: (i, 0)),
        out_shape=_out((IN_PAD, d), BF),
        compiler_params=_cp(("parallel",), 56))(*_pin(d_upool, d_z, d_uxbc, d_udt, u1))


def _conv_bwd(d_pre, proj, conv_w, nb, seq):
    ts = min(256, seq)
    nt = seq // ts
    hb = ts // CONV_HALO
    last_block = nb * seq // CONV_HALO - 1
    n_ext = CHUNK + CONV_HALO

    def body(dp_ref, dnext_ref, u_ref, cw_ref, du_ref, gw_ref, gb_ref):
        b = pl.program_id(0)
        i = pl.program_id(1)

        @pl.when((b == 0) & (i == 0))
        def _():
            gw_ref[...] = jnp.zeros_like(gw_ref)
            gb_ref[...] = jnp.zeros_like(gb_ref)

        for c0 in range(0, CONV_CH, 128):
            cs = slice(c0, c0 + 128)
            cw = cw_ref[:, cs]
            gw = [0.0] * 4
            gb = 0.0
            for r0 in range(0, ts, CHUNK):
                dp = dp_ref[r0:r0 + CHUNK, cs]
                u = u_ref[r0:r0 + CHUNK, cs]
                if r0 + CHUNK < ts:
                    below = dp_ref[r0 + CHUNK:r0 + CHUNK + CONV_HALO, cs]
                else:
                    below = jnp.where(i == nt - 1, 0.0, dnext_ref[:, cs])
                ext_d = jnp.concatenate([dp, below], 0)
                du = dp * cw[3:4]
                gw[3] = gw[3] + jnp.sum(dp * u, 0, keepdims=True)
                for k in (2, 1, 0):
                    shifted = pltpu.roll(ext_d, n_ext - (3 - k), 0)[:CHUNK]
                    du = du + shifted * cw[k:k + 1]
                    gw[k] = gw[k] + jnp.sum(shifted * u, 0, keepdims=True)
                gb = gb + jnp.sum(dp, 0, keepdims=True)
                du_ref[r0:r0 + CHUNK, cs] = du.astype(BF)
            for k in range(4):
                gw_ref[k:k + 1, cs] += gw[k]
            gb_ref[:, cs] += gb

    return pl.pallas_call(
        body, name="conv_bwd", grid=(nb, nt),
        in_specs=[pl.BlockSpec((ts, CONV_CH), lambda b, i: (b * nt + i, 0)),
                  pl.BlockSpec((CONV_HALO, CONV_CH), lambda b, i: (jnp.minimum((b * nt + i + 1) * hb, last_block), 0)),
                  pl.BlockSpec((ts, CONV_CH), lambda b, i: (b * nt + i, 1)),
                  pl.BlockSpec((4, CONV_CH), lambda b, i: (0, 0))],
        out_specs=[pl.BlockSpec((ts, CONV_CH), lambda b, i: (b * nt + i, 0)),
                   pl.BlockSpec((8, CONV_CH), lambda b, i: (0, 0)), pl.BlockSpec((1, CONV_CH), lambda b, i: (0, 0))],
        out_shape=[_out((nb * seq, CONV_CH), BF), _out((8, CONV_CH), F32),
                   _out((1, CONV_CH), F32)],
        compiler_params=_cp(("arbitrary", "arbitrary"), 48))(*_pin(d_pre, d_pre, proj, conv_w))


def _in_proj_bwd(d_parts, w_in_t, x, dh1, mod3, g_mix, seq, token):
    t, d = x.shape
    nb = t // seq
    tm = min(ROW_TILE, seq)
    tps = seq // tm

    widths = [p.shape[1] for p in d_parts]

    def body(d0_ref, d1_ref, d2_ref, d3_ref, w_ref, x_ref, dh1_ref, mod_ref, g_ref, tok_ref, gx_ref, acc_ref, gg_ref):
        i = pl.program_id(0)

        @pl.when(i == 0)
        def _():
            gg_ref[...] = jnp.zeros_like(gg_ref)

        @pl.when(i % tps == 0)
        def _():
            acc_ref[...] = jnp.zeros_like(acc_ref)

        gg = a_shift = a_scale = 0.0
        for rows in _sub_rows(tm):
            du = None
            off = 0
            for p_ref, wd in zip((d0_ref, d1_ref, d2_ref, d3_ref), widths):
                part = _nn(p_ref[rows, :], w_ref[off:off + wd, :])
                du = part if du is None else du + part
                off += wd
            xv = x_ref[rows, :]
            r = lax.rsqrt(jnp.mean(xv * xv, -1, keepdims=True) + EPS)
            hh = xv * r
            n1 = hh * g_ref[...]
            dn1 = du * (1.0 + mod_ref[0, 1:2, :])
            dhat = dn1 * g_ref[...]
            gx_ref[rows, :] = dh1_ref[rows, :] + r * (dhat - hh * jnp.mean(dhat * hh, -1, keepdims=True))
            gg = gg + jnp.sum(dn1 * hh, 0, keepdims=True)
            a_shift = a_shift + jnp.sum(du, 0, keepdims=True)
            a_scale = a_scale + jnp.sum(du * n1, 0, keepdims=True)
        gg_ref[...] += gg
        acc_ref[0, 0:1, :] += a_shift
        acc_ref[0, 1:2, :] += a_scale

    row = lambda i: (i, 0)
    vec = pl.BlockSpec((1, d), lambda i: (0, 0))
    return pl.pallas_call(
        body, name="in_proj_bwd", grid=(t // tm,),
        in_specs=[pl.BlockSpec((tm, wd), row) for wd in widths] +
                 [_RESIDENT, pl.BlockSpec((tm, d), row),
                  pl.BlockSpec((tm, d), row), pl.BlockSpec((1, N_MOD, d), lambda i: (i // tps, 0, 0)), vec, _token_spec()],
        out_specs=[pl.BlockSpec((tm, d), row), pl.BlockSpec((1, 8, d), lambda i: (i // tps, 0, 0)), vec],
        out_shape=[_out((t, d), F32), _out((nb, 8, d), F32),
                   _out((1, d), F32)],
        compiler_params=_cp(("arbitrary",), 40))(*_pin(*d_parts), w_in_t, *_pin(x, dh1, mod3, g_mix, token))


_VEC_LAYOUT = (("g_mix", 1024), ("conv_b", 1536), ("g_ssd", 1024), ("pool_scale", 512), ("g_mlp", 1024),
               ("g_final", 1024), ("dt_bias", 128), ("a_log", 128), ("d_skip_lanes", 1024), ("sq_err", 1024))
_VEC_OFFSET = {}
_off = 0
for _name, _n in _VEC_LAYOUT:
    _VEC_OFFSET[_name] = _off
    _off += _n
_VEC_LANES = _off
_SMALL_PARAMS = ("b_ada", "g_mix", "conv_w", "conv_b", "dt_bias", "a_log", "d_skip", "g_ssd", "w_pool", "pool_scale",
                 "g_mlp", "g_final")


def _pack_vec(parts):
    cols = []
    for name, n in _VEC_LAYOUT:
        v = parts[name]
        if v.shape[1] < n:
            v = jnp.pad(v, ((0, 0), (0, n - v.shape[1])))
        cols.append(v)
    return jnp.concatenate(cols, 1)


def _small_adam(vec_all, wpool_all, convw_all, dmod_all, params):
    names = _SMALL_PARAMS
    nin = 4 + 3 * len(names)

    def body(*refs):
        vec_ref, wp_ref, cw_ref, dm_ref = refs[:4]
        prm = {n: refs[4 + 3 * i:7 + 3 * i] for i, n in enumerate(names)}
        loss_ref = refs[nin]
        outs = {n: refs[nin + 1 + 4 * i:nin + 5 + 4 * i] for i, n in enumerate(names)}
        vsum = vec_ref[0]
        for s in range(1, N_DEV):
            vsum = vsum + vec_ref[s]

        def lanes(name, n):
            off = _VEC_OFFSET[name]
            return vsum[:, off:off + n]

        grads = {n: lanes(n, prm[n][0].shape[1]) for n in ("g_mix", "conv_b", "g_ssd", "pool_scale", "g_mlp", "g_final", "dt_bias")}
        grads["a_log"] = lanes("a_log", SSD_HEADS) * (-jnp.exp(prm["a_log"][0][...]))
        per_lane = jnp.broadcast_to(lanes("d_skip_lanes", SSD_INNER), (8, SSD_INNER))
        grads["d_skip"] = _exact_nn(per_lane, _head_reduce_matrix(SSD_INNER, SSD_HEADS))[0:1]
        gwp = wp_ref[0].astype(F32)
        gcw = cw_ref[0]
        gb = jnp.sum(dm_ref[0], 0, keepdims=True)
        for s in range(1, N_DEV):
            gwp = gwp + wp_ref[s].astype(F32)
            gcw = gcw + cw_ref[s]
            gb = gb + jnp.sum(dm_ref[s], 0, keepdims=True)
        grads["w_pool"] = gwp
        grads["conv_w"] = gcw[0:4]
        grads["b_ada"] = gb
        total = jnp.sum(lanes("sq_err", D_MODEL), 1, keepdims=True) * (0.5 / D_MODEL)
        loss_ref[...] = jnp.broadcast_to(total, loss_ref.shape)
        for n in names:
            w_ref, m_ref, v_ref = prm[n]
            g = grads[n]
            d, m2, v2 = _adam_math(w_ref[...], g, m_ref[...], v_ref[...])
            g_ref, d_ref, m2_ref, v2_ref = outs[n]
            g_ref[...] = g
            d_ref[...] = d
            m2_ref[...] = m2
            v2_ref[...] = v2

    flat = [vec_all, wpool_all, convw_all, dmod_all]
    out_shape = [jax.ShapeDtypeStruct((1, 128), F32)]
    for n in names:
        flat += list(params[n])
        out_shape += [jax.ShapeDtypeStruct(params[n][0].shape, F32)] * 4
    vm = pl.BlockSpec(memory_space=pltpu.VMEM)
    res = pl.pallas_call(body, name="small_adam", out_shape=out_shape, in_specs=[vm] * len(flat),
                         out_specs=[vm] * len(out_shape), compiler_params=_cp(vmem_mb=48))(*flat)
    return res[0], {n: res[1 + 4 * i:5 + 4 * i] for i, n in enumerate(names)}


_WEIGHTS = ("w_ada", "b_ada", "g_mix", "w_in", "conv_w", "conv_b", "dt_bias", "a_log", "d_skip", "g_ssd", "w_pool",
            "pool_scale", "w_out", "g_mlp", "w_up", "w_down", "g_final")


def _local_step(x2, tg2, mod3, seq, w_in_t, first_token, weights_later, start_reduce, conv_w_full, sp):
    t, d = x2.shape
    nb = t // seq
    dskip_e = jnp.repeat(sp["d_skip"], SSD_HEAD_DIM, axis=1)
    proj, u1 = _in_proj(x2, mod3, sp["g_mix"], w_in_t, seq, first_token)
    y_pool, p = _pool_fwd(proj, sp["w_pool"], sp["pool_scale"], nb, seq)
    y_ssd, yssm, h_prev, pre = _ssd_fwd(proj, conv_w_full, sp["conv_b"], sp["dt_bias"], sp["a_log"], dskip_e, sp["g_ssd"], nb, seq)
    w_out_f, w_up4, w_down4 = weights_later(y_ssd)
    w_down_f = w_down4.reshape(D_FF, d)
    h1, o, u2 = _out_proj(y_pool, y_ssd, w_out_f, x2, mod3, sp["g_mlp"], seq)
    a_up = _mlp_up(u2, w_up4)
    d_dn, dh2, sq, gg_final, d_gf = _mlp_down_loss(a_up, w_down_f, h1, mod3, sp["g_final"], tg2, seq)

    gw_down = _tn_matmul(a_up, d_dn, 512, d, "grad_w_down", square_relu=True)
    tok = start_reduce("w_down", gw_down.reshape(N_CHIPS, D_FF // N_CHIPS, d))
    d_a = _mlp_down_bwd(d_dn, w_down4, a_up, tok)
    gw_up4 = _tn_matmul(u2, d_a, 512, d, "grad_w_up", out3=True)
    tok = start_reduce("w_up", gw_up4)
    dh1, d_o, accf, gg_mlp = _mlp_up_bwd(d_a, w_up4, h1, dh2, o, mod3, sp["g_mlp"], seq, tok)
    gw_out_pool = _tn_matmul(y_pool, d_o, 512, d, "grad_w_out_pool")
    gw_out_ssd = _tn_matmul(y_ssd, d_o, 512, d, "grad_w_out_ssd")
    gw_out = jnp.concatenate([gw_out_pool, gw_out_ssd], 0)
    tok = start_reduce("w_out", gw_out.reshape(N_CHIPS, gw_out.shape[0] // N_CHIPS, d))
    d_ypool, d_yssd = _out_proj_bwd(d_o, w_out_f, tok)
    d_upool, gw_pool, g_ps = _pool_bwd(d_ypool, p, sp["w_pool"], sp["pool_scale"], nb, seq)
    d_z, d_pre, d_udt, gg_ssd, gdsk, ga, gdtb = _ssd_bwd(proj, pre, d_yssd, yssm, h_prev, sp["dt_bias"], sp["a_log"],
                                                        dskip_e, sp["g_ssd"], nb, seq)
    d_uxbc, gconvw, gconvb = _conv_bwd(d_pre, proj, conv_w_full, nb, seq)
    gw_in_t = _grad_w_in_t(d_upool, d_z, d_uxbc, d_udt, u1)
    tok = start_reduce("w_in", gw_in_t[:IN_WIDTH].reshape(N_CHIPS, IN_WIDTH // N_CHIPS, d))
    gx, accm, gg_mix = _in_proj_bwd([d_upool, d_z, d_uxbc, d_udt], w_in_t, x2, dh1, mod3, sp["g_mix"], seq, tok)

    d_mod = jnp.concatenate([accm[:, 0], accm[:, 1], accf[:, 2], accf[:, 0], accf[:, 1], d_gf[:, 0]], 1)
    vec = _pack_vec({"g_mix": gg_mix, "conv_b": gconvb, "g_ssd": gg_ssd, "pool_scale": g_ps, "g_mlp": gg_mlp,
                     "g_final": gg_final, "dt_bias": gdtb, "a_log": ga, "d_skip_lanes": gdsk, "sq_err": sq})
    return gx, d_mod, vec, gw_pool, gconvw


def kernel(x, c, w_ada, b_ada, g_mix, w_in, conv_w, conv_b, dt_bias, a_log, d_skip, g_ssd, w_pool, pool_scale, w_out, g_mlp, w_up, w_down, g_final, loss_target, m_w_ada, m_b_ada, m_g_mix, m_w_in, m_conv_w, m_conv_b, m_dt_bias, m_a_log, m_d_skip, m_g_ssd, m_w_pool, m_pool_scale, m_w_out, m_g_mlp, m_w_up, m_w_down, m_g_final, v_w_ada, v_b_ada, v_g_mix, v_w_in, v_conv_w, v_conv_b, v_dt_bias, v_a_log, v_d_skip, v_g_ssd, v_w_pool, v_pool_scale, v_w_out, v_g_mlp, v_w_up, v_w_down, v_g_final):
    nb, seq, d = x.shape
    t = nb * seq
    xi, yi, ci = _mesh_pos()
    chip = 2 * xi + yi
    me = 4 * xi + 2 * yi + ci
    ada_cols = w_ada.shape[2]
    conv_cols = conv_w.shape[2]
    in_cols = w_in.shape[2]
    w_in_s, m_w_in_s, v_w_in_s = w_in[0].T, m_w_in[0].T, v_w_in[0].T

    c8, convw8 = _all_gather_small([c, conv_w[0]])
    w_in_b = w_in_s.astype(BF)
    i_send, i_recv, i_src, i_land, _ = _ici_start(
        [w_in_b], [jax.ShapeDtypeStruct((N_CHIPS,) + w_in_b.shape, BF)], _gather_sent, _gather_landing, "gather_start_w_in",
        after=c8)
    c_all = c8.reshape(N_DEV * nb, d)
    conv_w_full = convw8[0::2].transpose(1, 0, 2).reshape(4, N_CHIPS * conv_cols)
    b_shard = lax.dynamic_slice(b_ada, (0, chip * ada_cols), (1, ada_cols))
    mod_part, c_act = _ada_mod(c_all, w_ada[0], b_shard)
    mod_rows = mod_part.reshape(N_DEV, nb, ada_cols)
    m_send, m_recv, m_src, m_land, _ = _ici_start(
        [mod_rows], [jax.ShapeDtypeStruct((N_CHIPS, nb, ada_cols), F32)], _mod_sent, _mod_landing, "mod_start", after=mod_part)

    in_shard, in_land = _ici_wait(i_send, i_recv, i_src, i_land, m_src[0], _gather_sent, _gather_landing, "gather_wait_w_in")
    (w_in4,) = _gather_finish(in_land, in_shard)
    w_in_t = jnp.pad(w_in4.reshape(N_CHIPS * in_cols, d), ((0, IN_PAD - N_CHIPS * in_cols), (0, 0)))
    mod_mine, mod_land = _ici_wait(m_send, m_recv, m_src, m_land, w_in_t, _mod_sent, _mod_landing, "mod_wait")
    mod_own = lax.dynamic_slice(mod_mine[0], (me, 0, 0), (1, nb, ada_cols))
    mod4 = lax.dynamic_update_slice(mod_land[0], mod_own, (chip, 0, 0))
    mod3 = mod4.transpose(1, 0, 2).reshape(nb, N_MOD, d)
    later = [w_out[0].astype(BF), w_up[0].astype(BF), w_down[0].astype(BF)]
    g_send, g_recv, g_src, g_land, first_token = _ici_start(
        later, [jax.ShapeDtypeStruct((N_CHIPS,) + s.shape, BF) for s in later], _gather_sent, _gather_landing, "gather_start",
        after=w_in4)

    def weights_later(after):
        shards, lands = _ici_wait(g_send, g_recv, g_src, g_land, after, _gather_sent, _gather_landing, "gather_wait")
        w_out4, w_up4, w_down4 = _gather_finish(lands, shards)
        return w_out4.reshape(N_CHIPS * w_out.shape[1], d), w_up4, w_down4

    pending = {}

    def start_reduce(name, grad4):
        pending[name] = _reduce_start(grad4, "reduce_start_" + name)
        return pending[name][4]

    sp = dict(g_mix=g_mix, conv_b=conv_b, dt_bias=dt_bias, a_log=a_log, d_skip=d_skip, g_ssd=g_ssd,
              w_pool=w_pool[0], pool_scale=pool_scale, g_mlp=g_mlp, g_final=g_final.reshape(1, d))
    gx, d_mod, vec, gw_pool, gconvw = _local_step(
        x.reshape(t, d), loss_target.reshape(t, d), mod3, seq, w_in_t, first_token, weights_later, start_reduce, conv_w_full, sp)

    pos = jnp.stack([ci, chip, me]).astype(jnp.int32)
    small_parts = [vec, gw_pool.reshape(4 * POOL_GROUP, POOL_GROUP).astype(BF), gconvw, d_mod]
    s_send, s_recv, s_src, s_land, s_token = _exchange_start(small_parts, ALL_PEERS, "small_start")
    halves = []
    for name in ("w_in", "w_out", "w_up", "w_down"):
        r_send, r_recv, r_src, r_land, _ = pending[name]
        own, recv = _reduce_wait(r_send, r_recv, r_src, r_land, s_token, "reduce_wait_" + name)
        halves.append(_sum_eight(recv, own, pos))
    h_send, h_recv, h_src, h_land, h_token = _exchange_start(halves, SIBLING, "halves_start")

    s_own, s_got = _exchange_wait(s_send, s_recv, s_src, s_land, ALL_PEERS, h_token, "small_wait")
    vec8, wpool8, convw8g, dmod8 = [lax.dynamic_update_slice(got, mine[None], (me,) + (0,) * mine.ndim)
                                    for got, mine in zip(s_got, s_own)]
    convw8s = lax.dynamic_slice(convw8g, (0, 0, chip * conv_cols), (N_DEV, 8, conv_cols))
    m_in = dict(b_ada=m_b_ada, g_mix=m_g_mix, conv_w=m_conv_w[0], conv_b=m_conv_b, dt_bias=m_dt_bias, a_log=m_a_log,
                d_skip=m_d_skip, g_ssd=m_g_ssd, w_pool=m_w_pool.reshape(4 * POOL_GROUP, POOL_GROUP), pool_scale=m_pool_scale,
                g_mlp=m_g_mlp, g_final=m_g_final.reshape(1, d))
    v_in = dict(b_ada=v_b_ada, g_mix=v_g_mix, conv_w=v_conv_w[0], conv_b=v_conv_b, dt_bias=v_dt_bias, a_log=v_a_log,
                d_skip=v_d_skip, g_ssd=v_g_ssd, w_pool=v_w_pool.reshape(4 * POOL_GROUP, POOL_GROUP), pool_scale=v_pool_scale,
                g_mlp=v_g_mlp, g_final=v_g_final.reshape(1, d))
    w_small = dict(sp, b_ada=b_ada, conv_w=conv_w[0], w_pool=w_pool.reshape(4 * POOL_GROUP, POOL_GROUP))
    loss_row, small = _small_adam(vec8, wpool8, convw8s, dmod8, {n: (w_small[n], m_in[n], v_in[n]) for n in _SMALL_PARAMS})

    dmod_all = dmod8.reshape(N_DEV * nb, N_CHIPS * ada_cols)
    dmod_cols = lax.dynamic_slice(dmod_all, (0, chip * ada_cols), (N_DEV * nb, ada_cols))
    res = {n: tuple(r.reshape(w.shape) for r in small[n])
           for n, w in (("b_ada", b_ada), ("g_mix", g_mix), ("conv_w", conv_w), ("conv_b", conv_b), ("dt_bias", dt_bias),
                        ("a_log", a_log), ("d_skip", d_skip), ("g_ssd", g_ssd), ("w_pool", w_pool), ("pool_scale", pool_scale),
                        ("g_mlp", g_mlp), ("g_final", g_final))}
    g_ada, d_ada, m_ada, v_ada = _adam_ada(c_act.T.astype(BF), dmod_cols, w_ada[0], m_w_ada[0], v_w_ada[0])
    res["w_ada"] = (g_ada[None], d_ada[None], m_ada[None], v_ada[None])
    h_own, h_got = _exchange_wait(h_send, h_recv, h_src, h_land, SIBLING, g_ada, "halves_wait")
    g_in, dl, m2, v2 = _adam_big(h_own[0], h_got[0], w_in_s, m_w_in_s, v_w_in_s, pos)
    res["w_in"] = (g_in.T[None], dl.T[None], m2.T[None], v2.T[None])
    for i, (n, w, m, v) in enumerate((("w_out", w_out, m_w_out, v_w_out), ("w_up", w_up, m_w_up, v_w_up),
                                      ("w_down", w_down, m_w_down, v_w_down))):
        g, dl, m2, v2 = _adam_big(h_own[i + 1], h_got[i + 1], w[0], m[0], v[0], pos)
        res[n] = (g[None], dl[None], m2[None], v2[None])

    loss = loss_row[0, 0]
    return (loss, gx.reshape(nb, seq, d), *[res[n][0] for n in _WEIGHTS], *[res[n][1] for n in _WEIGHTS],
            *[res[n][2] for n in _WEIGHTS], *[res[n][3] for n in _WEIGHTS])
```

```python
import functools

import jax
import jax.numpy as jnp
from jax import lax
from jax.experimental import pallas as pl
from jax.experimental.pallas import tpu as pltpu

F32 = jnp.float32
BF = jnp.bfloat16
MESH = pl.DeviceIdType.MESH

EPS = 1e-5
D_MODEL = 1024
POOL_WIDTH = 512
POOL_WINDOWS = (2, 4, 8, 16)
POOL_GROUP = 128
SSD_INNER = 1024
SSD_HEADS = 16
SSD_HEAD_DIM = 64
SSD_STATE = 128
GROUP_W = 512
CHUNK = 128
CONV_CH = 1536
OFF_Z = 512
OFF_XBC = 1536
OFF_DT = 3072
IN_WIDTH = 3088
IN_PAD = 3200
D_FF = 4096
N_MOD = 6
N_CHIPS = 4
N_DEV = 8
HALO = 16
CONV_HALO = 8

ADAM_LR = 0.001
ADAM_B1 = 0.9
ADAM_B2 = 0.999
ADAM_EPS = 1e-08
ADAM_WD = 0.01
ADAM_STEP = 10

VMEM_BYTES_V7X = 64 * 1024 * 1024


def _cp(semantics=None, vmem_mb=48, **kw):
    args = dict(vmem_limit_bytes=vmem_mb * 1024 * 1024, **kw)
    if semantics is not None:
        args["dimension_semantics"] = semantics
    return pltpu.CompilerParams(**args)


def _out(shape, dtype):
    return pltpu.HBM(shape, dtype)


def _pin(*arrays):
    return [pltpu.with_memory_space_constraint(a, pltpu.HBM) for a in arrays]


def _nn(a, b):
    return jnp.dot(a, b, preferred_element_type=F32)


def _nt(a, b):
    return lax.dot_general(a, b, (((1,), (1,)), ((), ())), preferred_element_type=F32)


def _tn(a, b):
    return lax.dot_general(a, b, (((0,), (0,)), ((), ())), preferred_element_type=F32)


def _split3(v):
    hi = v.astype(BF)
    r1 = v - hi.astype(F32)
    mid = r1.astype(BF)
    lo = (r1 - mid.astype(F32)).astype(BF)
    return hi, mid, lo


def _exact_nn(v, m01):
    hi, mid, lo = _split3(v)
    return _nn(hi, m01) + _nn(mid, m01) + _nn(lo, m01)


def _exact_nn_left(m01, v):
    hi, mid, lo = _split3(v)
    return _nn(m01, hi) + _nn(m01, mid) + _nn(m01, lo)


def _exact_nt_left(m01, v):
    hi, mid, lo = _split3(v)
    return _nt(m01, hi) + _nt(m01, mid) + _nt(m01, lo)


def _sigmoid(v):
    return 1.0 / (1.0 + jnp.exp(-v))


def _iota(shape, dim):
    return lax.broadcasted_iota(jnp.int32, shape, dim)


def _head_expand_matrix(heads, width):
    return (_iota((heads, width), 1) // SSD_HEAD_DIM == _iota((heads, width), 0)).astype(BF)


def _head_reduce_matrix(width, heads):
    return (_iota((width, heads), 0) // SSD_HEAD_DIM == _iota((width, heads), 1)).astype(BF)


def _mesh_pos():
    return lax.axis_index("x"), lax.axis_index("y"), lax.axis_index("c")


def _flip(v, bit):
    return v + bit - 2 * bit * v


def _all_gather_small(arrays, token=None):
    n = len(arrays)
    extra = [] if token is None else [token]

    def body(*refs):
        in_refs, out_refs = refs[:n], refs[n + len(extra):2 * n + len(extra)]
        send_sems, recv_sems, local_sems = refs[2 * n + len(extra):]
        x, y, c = _mesh_pos()
        me = 4 * x + 2 * y + c
        local = []
        for a in range(n):
            cp = pltpu.make_async_copy(in_refs[a], out_refs[a].at[me], local_sems.at[a])
            cp.start()
            local.append(cp)
        sends = []
        for k in range(1, N_DEV):
            peer = (_flip(x, (k >> 2) & 1), _flip(y, (k >> 1) & 1), _flip(c, k & 1))
            for a in range(n):
                cp = pltpu.make_async_remote_copy(
                    src_ref=in_refs[a], dst_ref=out_refs[a].at[me],
                    send_sem=send_sems.at[a, k], recv_sem=recv_sems.at[a, k],
                    device_id=peer, device_id_type=MESH)
                cp.start()
                sends.append(cp)
        for k in range(1, N_DEV):
            px, py, pc = _flip(x, (k >> 2) & 1), _flip(y, (k >> 1) & 1), _flip(c, k & 1)
            src = 4 * px + 2 * py + pc
            for a in range(n):
                pltpu.make_async_remote_copy(
                    src_ref=in_refs[a], dst_ref=out_refs[a].at[src],
                    send_sem=send_sems.at[a, k], recv_sem=recv_sems.at[a, k],
                    device_id=(px, py, pc), device_id_type=MESH).wait_recv()
        for cp in sends:
            cp.wait_send()
        for cp in local:
            cp.wait()

    vm = pl.BlockSpec(memory_space=pltpu.VMEM)
    return pl.pallas_call(
        body, name="all_gather_small",
        out_shape=[jax.ShapeDtypeStruct((N_DEV,) + a.shape, a.dtype) for a in arrays],
        in_specs=[vm] * (n + len(extra)), out_specs=[vm] * n,
        scratch_shapes=[pltpu.SemaphoreType.DMA((n, N_DEV)), pltpu.SemaphoreType.DMA((n, N_DEV)),
                        pltpu.SemaphoreType.DMA((n,))],
        compiler_params=_cp(vmem_mb=32),
    )(*arrays, *extra)


def _weight_all_gather(shards):
    n = len(shards)
    any_spec = _HBM

    def body(*refs):
        in_refs, out_refs = refs[:n], refs[n:2 * n]
        send_sems, recv_sems, fsend_sems, frecv_sems, local_sems = refs[2 * n:]
        x, y, c = _mesh_pos()
        chip = 2 * x + y
        local, sends = [], []
        for a in range(n):
            cp = pltpu.make_async_copy(in_refs[a], out_refs[a].at[chip], local_sems.at[a])
            cp.start()
            local.append(cp)

        def half(a, which):
            hc = shards[a].shape[1] // 2
            return pl.ds(pl.multiple_of(which * hc, 128), hc)

        for j in range(1, N_CHIPS):
            px, py = _flip(x, (j >> 1) & 1), _flip(y, j & 1)
            for a in range(n):
                cp = pltpu.make_async_remote_copy(
                    src_ref=in_refs[a].at[:, half(a, c)], dst_ref=out_refs[a].at[chip, :, half(a, c)],
                    send_sem=send_sems.at[a, j], recv_sem=recv_sems.at[a, j],
                    device_id=(px, py, c), device_id_type=MESH)
                cp.start()
                sends.append(cp)
        for j in range(1, N_CHIPS):
            px, py = _flip(x, (j >> 1) & 1), _flip(y, j & 1)
            src = 2 * px + py
            for a in range(n):
                landed = out_refs[a].at[src, :, half(a, c)]
                pltpu.make_async_remote_copy(
                    src_ref=landed, dst_ref=landed, send_sem=send_sems.at[a, j], recv_sem=recv_sems.at[a, j],
                    device_id=(px, py, c), device_id_type=MESH).wait_recv()
                cp = pltpu.make_async_remote_copy(
                    src_ref=landed, dst_ref=landed, send_sem=fsend_sems.at[a, j], recv_sem=frecv_sems.at[a, j],
                    device_id=(x, y, 1 - c), device_id_type=MESH)
                cp.start()
                sends.append(cp)
        for j in range(1, N_CHIPS):
            px, py = _flip(x, (j >> 1) & 1), _flip(y, j & 1)
            src = 2 * px + py
            for a in range(n):
                other = out_refs[a].at[src, :, half(a, 1 - c)]
                pltpu.make_async_remote_copy(
                    src_ref=other, dst_ref=other, send_sem=fsend_sems.at[a, j], recv_sem=frecv_sems.at[a, j],
                    device_id=(x, y, 1 - c), device_id_type=MESH).wait_recv()
        for cp in sends:
            cp.wait_send()
        for cp in local:
            cp.wait()

    return pl.pallas_call(
        body, name="weight_all_gather",
        out_shape=[_out((N_CHIPS,) + s.shape, s.dtype) for s in shards],
        in_specs=[any_spec] * n, out_specs=[any_spec] * n,
        scratch_shapes=[pltpu.SemaphoreType.DMA((n, N_CHIPS))] * 4 + [pltpu.SemaphoreType.DMA((n,))],
        compiler_params=_cp(vmem_mb=16),
    )(*shards)


_HBM = pl.BlockSpec(memory_space=pltpu.HBM)
_SEM = pl.BlockSpec(memory_space=pltpu.SEMAPHORE)
_DATAFLOW = pltpu.SideEffectType.DATAFLOW_SIDE_EFFECTING


def _peer_chip(x, y, j):
    return _flip(x, (j >> 1) & 1), _flip(y, j & 1)


def _ici_start(srcs, land_shapes, sent, landing, name, after):
    n = len(srcs)

    def body(*refs):
        src_refs, land_refs = refs[:n], refs[n:2 * n]
        send_sems, recv_sems = refs[2 * n + 1], refs[2 * n + 2]
        token = refs[-1]
        x, y, c = _mesh_pos()
        for j in range(1, N_CHIPS):
            px, py = _peer_chip(x, y, j)
            for a in range(n):
                pltpu.make_async_remote_copy(
                    src_ref=sent(src_refs[a], c, 2 * px + py), dst_ref=landing(land_refs[a], c, 2 * x + y),
                    send_sem=send_sems.at[a * (N_CHIPS - 1) + j - 1], recv_sem=recv_sems.at[a * (N_CHIPS - 1) + j - 1],
                    device_id=(px, py, c), device_id_type=MESH).start()
        token[...] = jnp.zeros_like(token)

    sems = pltpu.SemaphoreType.DMA((n * (N_CHIPS - 1),))
    lands = [pltpu.with_memory_space_constraint(lax.empty(s.shape, s.dtype), pltpu.HBM) for s in land_shapes]
    outs = pl.pallas_call(
        body, name=name,
        out_shape=(sems, sems, *[pltpu.HBM(s.shape, s.dtype) for s in srcs],
                   *[pltpu.HBM(s.shape, s.dtype) for s in land_shapes], jax.ShapeDtypeStruct((8, 128), F32)),
        in_specs=[_HBM] * (2 * n + 1), out_specs=[_SEM, _SEM] + [_HBM] * (2 * n) + [pl.BlockSpec(memory_space=pltpu.VMEM)],
        input_output_aliases={i: 2 + i for i in range(2 * n)},
        compiler_params=pltpu.CompilerParams(has_side_effects=_DATAFLOW),
    )(*_pin(*srcs), *lands, *_pin(after))
    return outs[0], outs[1], outs[2:2 + n], outs[2 + n:2 + 2 * n], outs[-1]


def _ici_wait(send_sems, recv_sems, src_thru, land_thru, after, sent, landing, name):
    n = len(src_thru)

    def body(*refs):
        src_refs, land_refs = refs[:n], refs[n:2 * n]
        send_sems, recv_sems = refs[2 * n], refs[2 * n + 1]
        x, y, c = _mesh_pos()
        for j in range(1, N_CHIPS):
            px, py = _peer_chip(x, y, j)
            for a in range(n):
                cp = pltpu.make_async_remote_copy(
                    src_ref=sent(src_refs[a], c, 2 * px + py), dst_ref=landing(land_refs[a], c, 2 * px + py),
                    send_sem=send_sems.at[a * (N_CHIPS - 1) + j - 1], recv_sem=recv_sems.at[a * (N_CHIPS - 1) + j - 1],
                    device_id=(px, py, c), device_id_type=MESH)
                cp.wait_send()
                cp.wait_recv()

    outs = pl.pallas_call(
        body, name=name,
        out_shape=tuple(pltpu.HBM(s.shape, s.dtype) for s in (*src_thru, *land_thru)),
        in_specs=[_HBM] * (2 * n) + [_SEM, _SEM, _HBM], out_specs=[_HBM] * (2 * n),
        input_output_aliases={i: i for i in range(2 * n)},
        compiler_params=pltpu.CompilerParams(has_side_effects=_DATAFLOW),
    )(*src_thru, *land_thru, send_sems, recv_sems, *_pin(after))
    return outs[:n], outs[n:]


def _col_half(ref, which, lead=()):
    hc = ref.shape[-1] // 2
    return ref.at[(*lead, slice(None), pl.ds(pl.multiple_of(which * hc, 128), hc))]


def _gather_sent(ref, c, dst_chip):
    return _col_half(ref, c)


def _gather_landing(ref, c, src_chip):
    return _col_half(ref, c, lead=(src_chip,))


def _mod_sent(ref, c, dst_chip):
    return ref.at[2 * dst_chip + c]


def _mod_landing(ref, c, src_chip):
    return ref.at[src_chip]


def _reduce_copy(src_ref, land_ref, send_sems, recv_sems, k, receiving):
    x, y, c = _mesh_pos()
    px, py, pc = _flip(x, (k >> 2) & 1), _flip(y, (k >> 1) & 1), _flip(c, k & 1)
    hc = src_ref.shape[2] // 2
    src = src_ref.at[2 * px + py, :, pl.ds(pl.multiple_of(pc * hc, 128), hc)]
    slot = (4 * px + 2 * py + pc) if receiving else (4 * x + 2 * y + c)
    return pltpu.make_async_remote_copy(
        src_ref=src, dst_ref=land_ref.at[slot], send_sem=send_sems.at[k - 1], recv_sem=recv_sems.at[k - 1],
        device_id=(px, py, pc), device_id_type=MESH)


def _reduce_start(grad4, name):
    k4, r, cols = grad4.shape

    def body(src_ref, land_ref, send_sems, recv_sems, src_thru, land_thru, token):
        for k in range(1, N_DEV):
            _reduce_copy(src_ref, land_ref, send_sems, recv_sems, k, receiving=False).start()
        token[...] = jnp.zeros_like(token)

    sems = pltpu.SemaphoreType.DMA((N_DEV - 1,))
    land = pltpu.with_memory_space_constraint(lax.empty((N_DEV, r, cols // 2), grad4.dtype), pltpu.HBM)
    return pl.pallas_call(
        body, name=name,
        out_shape=(sems, sems, pltpu.HBM(grad4.shape, grad4.dtype), pltpu.HBM(land.shape, land.dtype),
                   jax.ShapeDtypeStruct((8, 128), F32)),
        in_specs=[_HBM, _HBM], out_specs=[_SEM, _SEM, _HBM, _HBM, pl.BlockSpec(memory_space=pltpu.VMEM)],
        input_output_aliases={0: 2, 1: 3},
        compiler_params=pltpu.CompilerParams(has_side_effects=_DATAFLOW),
    )(*_pin(grad4), land)


def _reduce_wait(send_sems, recv_sems, src_thru, land_thru, after, name):
    def body(src_ref, land_ref, send_sems, recv_sems, after_ref, src_out, land_out):
        for k in range(1, N_DEV):
            cp = _reduce_copy(src_ref, land_ref, send_sems, recv_sems, k, receiving=True)
            cp.wait_send()
            cp.wait_recv()

    return pl.pallas_call(
        body, name=name,
        out_shape=(pltpu.HBM(src_thru.shape, src_thru.dtype), pltpu.HBM(land_thru.shape, land_thru.dtype)),
        in_specs=[_HBM, _HBM, _SEM, _SEM, _HBM], out_specs=[_HBM, _HBM],
        input_output_aliases={0: 0, 1: 1},
        compiler_params=pltpu.CompilerParams(has_side_effects=_DATAFLOW),
    )(src_thru, land_thru, send_sems, recv_sems, *_pin(after))


def _peer_copy(src_ref, land_ref, send_sems, recv_sems, idx, k, receiving):
    x, y, c = _mesh_pos()
    px, py, pc = _flip(x, (k >> 2) & 1), _flip(y, (k >> 1) & 1), _flip(c, k & 1)
    if land_ref.shape[0] == N_DEV:
        slot = (4 * px + 2 * py + pc) if receiving else (4 * x + 2 * y + c)
    else:
        slot = pc if receiving else c
    return pltpu.make_async_remote_copy(
        src_ref=src_ref, dst_ref=land_ref.at[slot], send_sem=send_sems.at[idx], recv_sem=recv_sems.at[idx],
        device_id=(px, py, pc), device_id_type=MESH)


def _exchange_start(arrays, peers, name):
    n = len(arrays)

    def body(*refs):
        src_refs, land_refs = refs[:n], refs[n:2 * n]
        send_sems, recv_sems = refs[2 * n], refs[2 * n + 1]
        token = refs[-1]
        for j, k in enumerate(peers):
            for a in range(n):
                _peer_copy(src_refs[a], land_refs[a], send_sems, recv_sems, a * len(peers) + j, k, receiving=False).start()
        token[...] = jnp.zeros_like(token)

    sems = pltpu.SemaphoreType.DMA((n * len(peers),))
    n_slots = N_DEV if len(peers) > 1 else 2
    lands = [pltpu.with_memory_space_constraint(lax.empty((n_slots,) + a.shape, a.dtype), pltpu.HBM) for a in arrays]
    outs = pl.pallas_call(
        body, name=name,
        out_shape=(sems, sems, *[pltpu.HBM(a.shape, a.dtype) for a in arrays], *[pltpu.HBM(l.shape, l.dtype) for l in lands],
                   jax.ShapeDtypeStruct((8, 128), F32)),
        in_specs=[_HBM] * (2 * n), out_specs=[_SEM, _SEM] + [_HBM] * (2 * n) + [pl.BlockSpec(memory_space=pltpu.VMEM)],
        input_output_aliases={i: 2 + i for i in range(2 * n)},
        compiler_params=pltpu.CompilerParams(has_side_effects=_DATAFLOW),
    )(*_pin(*arrays), *lands)
    return outs[0], outs[1], outs[2:2 + n], outs[2 + n:2 + 2 * n], outs[-1]


def _exchange_wait(send_sems, recv_sems, src_thru, land_thru, peers, after, name):
    n = len(src_thru)

    def body(*refs):
        src_refs, land_refs = refs[:n], refs[n:2 * n]
        send_sems, recv_sems = refs[2 * n], refs[2 * n + 1]
        for j, k in enumerate(peers):
            for a in range(n):
                cp = _peer_copy(src_refs[a], land_refs[a], send_sems, recv_sems, a * len(peers) + j, k, receiving=True)
                cp.wait_send()
                cp.wait_recv()

    outs = pl.pallas_call(
        body, name=name,
        out_shape=tuple(pltpu.HBM(s.shape, s.dtype) for s in (*src_thru, *land_thru)),
        in_specs=[_HBM] * (2 * n) + [_SEM, _SEM, _HBM], out_specs=[_HBM] * (2 * n),
        input_output_aliases={i: i for i in range(2 * n)},
        compiler_params=pltpu.CompilerParams(has_side_effects=_DATAFLOW),
    )(*src_thru, *land_thru, send_sems, recv_sems, *_pin(after))
    return outs[:n], outs[n:]


ALL_PEERS = tuple(range(1, N_DEV))
SIBLING = (1,)


def _sum_eight(recv, grad4, pos):
    n, r, hc = recv.shape
    steps = 4
    tc = hc // steps

    def body(pos_ref, r_ref, g_ref, o_ref):
        me = pos_ref[2]
        o_ref[...] = jnp.zeros_like(o_ref)
        for s in range(n):
            @pl.when(me == s)
            def _():
                o_ref[...] += g_ref[0].astype(F32)

            @pl.when(me != s)
            def _():
                o_ref[...] += r_ref[s].astype(F32)

    grid_spec = pltpu.PrefetchScalarGridSpec(
        num_scalar_prefetch=1, grid=(steps,),
        in_specs=[pl.BlockSpec((n, r, tc), lambda i, pos: (0, 0, i)),
                  pl.BlockSpec((1, r, tc), lambda i, pos: (pos[1], 0, pos[0] * steps + i))],
        out_specs=pl.BlockSpec((r, tc), lambda i, pos: (0, i)))
    return pl.pallas_call(body, name="sum_eight", grid_spec=grid_spec, out_shape=_out((r, hc), F32),
                          compiler_params=_cp(("parallel",), 32))(pos, *_pin(recv, grad4))


def _gather_finish(lands, shards):
    n = len(lands)
    any_spec = _HBM

    def body(*refs):
        shard_refs, out_refs = refs[n:2 * n], refs[2 * n:3 * n]
        send_sems, recv_sems, local_sems = refs[3 * n:]
        x, y, c = _mesh_pos()
        chip = 2 * x + y
        local, sends = [], []
        for a in range(n):
            cp = pltpu.make_async_copy(shard_refs[a], out_refs[a].at[chip], local_sems.at[a])
            cp.start()
            local.append(cp)
        for j in range(1, N_CHIPS):
            px, py = _peer_chip(x, y, j)
            for a in range(n):
                landed = _col_half(out_refs[a], c, lead=(2 * px + py,))
                cp = pltpu.make_async_remote_copy(
                    src_ref=landed, dst_ref=landed, send_sem=send_sems.at[a, j], recv_sem=recv_sems.at[a, j],
                    device_id=(x, y, 1 - c), device_id_type=MESH)
                cp.start()
                sends.append(cp)
        for j in range(1, N_CHIPS):
            px, py = _peer_chip(x, y, j)
            for a in range(n):
                other = _col_half(out_refs[a], 1 - c, lead=(2 * px + py,))
                pltpu.make_async_remote_copy(
                    src_ref=other, dst_ref=other, send_sem=send_sems.at[a, j], recv_sem=recv_sems.at[a, j],
                    device_id=(x, y, 1 - c), device_id_type=MESH).wait_recv()
        for cp in sends:
            cp.wait_send()
        for cp in local:
            cp.wait()

    return pl.pallas_call(
        body, name="gather_finish",
        out_shape=[_out(l.shape, l.dtype) for l in lands],
        in_specs=[any_spec] * (2 * n), out_specs=[any_spec] * n,
        input_output_aliases={i: i for i in range(n)},
        scratch_shapes=[pltpu.SemaphoreType.DMA((n, N_CHIPS))] * 2 + [pltpu.SemaphoreType.DMA((n,))],
        compiler_params=_cp(vmem_mb=16),
    )(*lands, *shards)


def _halves_exchange(halves):
    n = len(halves)
    any_spec = _HBM

    def body(*refs):
        in_refs, out_refs = refs[:n], refs[n:2 * n]
        send_sems, recv_sems, local_sems = refs[2 * n:]
        x, y, c = _mesh_pos()
        copies, local = [], []
        for a in range(n):
            hc = halves[a].shape[1]
            mine = pl.ds(pl.multiple_of(c * hc, 128), hc)
            lc = pltpu.make_async_copy(in_refs[a], out_refs[a].at[:, mine], local_sems.at[a])
            lc.start()
            local.append(lc)
            cp = pltpu.make_async_remote_copy(
                src_ref=in_refs[a], dst_ref=out_refs[a].at[:, mine],
                send_sem=send_sems.at[a], recv_sem=recv_sems.at[a],
                device_id=(x, y, 1 - c), device_id_type=MESH)
            cp.start()
            copies.append(cp)
        for a in range(n):
            hc = halves[a].shape[1]
            theirs = pl.ds(pl.multiple_of((1 - c) * hc, 128), hc)
            pltpu.make_async_remote_copy(
                src_ref=in_refs[a], dst_ref=out_refs[a].at[:, theirs],
                send_sem=send_sems.at[a], recv_sem=recv_sems.at[a],
                device_id=(x, y, 1 - c), device_id_type=MESH).wait_recv()
        for cp in copies:
            cp.wait_send()
        for lc in local:
            lc.wait()

    return pl.pallas_call(
        body, name="halves_exchange",
        out_shape=[_out((h.shape[0], 2 * h.shape[1]), h.dtype) for h in halves],
        in_specs=[any_spec] * n, out_specs=[any_spec] * n,
        scratch_shapes=[pltpu.SemaphoreType.DMA((n,))] * 3,
        compiler_params=_cp(vmem_mb=16),
    )(*halves)


def _adam_math(w, g, m, v):
    m2 = ADAM_B1 * m + (1.0 - ADAM_B1) * g
    v2 = ADAM_B2 * v + (1.0 - ADAM_B2) * (g * g)
    m_hat = m2 / (1.0 - ADAM_B1 ** ADAM_STEP)
    v_hat = v2 / (1.0 - ADAM_B2 ** ADAM_STEP)
    delta = -ADAM_LR * (m_hat / (jnp.sqrt(v_hat) + ADAM_EPS) + ADAM_WD * w)
    return delta, m2, v2


def _adam_big(g_own, g_pair, w, m, v, pos):
    r, c = w.shape
    tc = c // 4

    def body(pos_ref, go_ref, gp_ref, w_ref, m_ref, v_ref, g_ref, d_ref, m2_ref, v2_ref):
        half = pl.program_id(0) // 2
        g = jnp.where(half == pos_ref[0], go_ref[...], gp_ref[0])
        d, m2, v2 = _adam_math(w_ref[...], g, m_ref[...], v_ref[...])
        g_ref[...] = g
        d_ref[...] = d
        m2_ref[...] = m2
        v2_ref[...] = v2

    spec = pl.BlockSpec((r, tc), lambda i, pos: (0, i))
    grid_spec = pltpu.PrefetchScalarGridSpec(
        num_scalar_prefetch=1, grid=(4,),
        in_specs=[pl.BlockSpec((r, tc), lambda i, pos: (0, i % 2)),
                  pl.BlockSpec((1, r, tc), lambda i, pos: (1 - pos[0], 0, i % 2)), spec, spec, spec],
        out_specs=[spec] * 4)
    sh = _out((r, c), F32)
    return pl.pallas_call(body, name="adam_big", grid_spec=grid_spec, out_shape=[sh] * 4,
                          compiler_params=_cp(("parallel",), 32))(pos, *_pin(g_own, g_pair, w, m, v))


def _adam_ada(c_act_t, dmod_cols, w, m, v):
    r, c = w.shape
    tc = 512

    def body(ct_ref, dm_ref, w_ref, m_ref, v_ref, g_ref, d_ref, m2_ref, v2_ref):
        g = _nn(ct_ref[...], dm_ref[...].astype(BF))
        d, m2, v2 = _adam_math(w_ref[...], g, m_ref[...], v_ref[...])
        g_ref[...] = g
        d_ref[...] = d
        m2_ref[...] = m2
        v2_ref[...] = v2

    spec = pl.BlockSpec((r, tc), lambda i: (0, i))
    sh = _out((r, c), F32)
    return pl.pallas_call(
        body, name="adam_ada", grid=(c // tc,),
        in_specs=[pl.BlockSpec(c_act_t.shape, lambda i: (0, 0)), pl.BlockSpec((dmod_cols.shape[0], tc), lambda i: (0, i)),
                  spec, spec, spec],
        out_specs=[spec] * 4, out_shape=[sh] * 4, compiler_params=_cp(("parallel",), 48))(*_pin(c_act_t, dmod_cols, w, m, v))


def _ada_mod(c_all, w_shard, b_shard, token):
    nb, d = c_all.shape
    cols = w_shard.shape[1]
    tc = 512

    def body(c_ref, w_ref, b_ref, tok_ref, mod_ref, act_ref):
        cv = c_ref[...]
        act = cv * _sigmoid(cv)
        act_ref[...] = act
        mod_ref[...] = _nn(act.astype(BF), w_ref[...].astype(BF)) + b_ref[...]

    return pl.pallas_call(
        body, name="ada_mod", grid=(cols // tc,),
        in_specs=[pl.BlockSpec((nb, d), lambda i: (0, 0)), pl.BlockSpec((d, tc), lambda i: (0, i)),
                  pl.BlockSpec((1, tc), lambda i: (0, i)), _token_spec()],
        out_specs=[pl.BlockSpec((nb, tc), lambda i: (0, i)), pl.BlockSpec((nb, d), lambda i: (0, 0))],
        out_shape=[_out((nb, cols), F32), _out((nb, d), F32)],
        compiler_params=_cp(("arbitrary",), 32))(*_pin(c_all, w_shard, b_shard, token))


SUB_ROWS = 256
ROW_TILE = 512


def _sub_rows(tm):
    return [slice(s, s + SUB_ROWS) for s in range(0, tm, SUB_ROWS)] if tm > SUB_ROWS else [slice(0, tm)]


_RESIDENT = pl.BlockSpec(memory_space=pltpu.VMEM)


def _token_spec():
    return pl.BlockSpec((8, 128), lambda *_: (0, 0))


def _in_proj(x, mod3, g_mix, w_in_t, seq, token):
    t, d = x.shape
    tm = min(ROW_TILE, seq)
    tps = seq // tm

    def body(x_ref, mod_ref, g_ref, w_ref, tok_ref, proj_ref, u1_ref):
        for rows in _sub_rows(tm):
            xv = x_ref[rows, :]
            r = lax.rsqrt(jnp.mean(xv * xv, -1, keepdims=True) + EPS)
            u = (xv * r * g_ref[...]) * (1.0 + mod_ref[0, 1:2, :]) + mod_ref[0, 0:1, :]
            ub = u.astype(BF)
            u1_ref[rows, :] = ub
            proj_ref[rows, :] = _nt(ub, w_ref[...])

    return pl.pallas_call(
        body, name="in_proj", grid=(t // tm,),
        in_specs=[pl.BlockSpec((tm, d), lambda i: (i, 0)), pl.BlockSpec((1, N_MOD, d), lambda i: (i // tps, 0, 0)),
                  pl.BlockSpec((1, d), lambda i: (0, 0)), _RESIDENT, _token_spec()],
        out_specs=[pl.BlockSpec((tm, IN_PAD), lambda i: (i, 0)), pl.BlockSpec((tm, d), lambda i: (i, 0))],
        out_shape=[_out((t, IN_PAD), F32), _out((t, d), BF)],
        compiler_params=_cp(("parallel",), 40))(*_pin(x, mod3, g_mix), w_in_t, *_pin(token))


def _pool_tile(seq):
    return min(512, seq)


def _pool_fwd(proj, w_pool, pool_scale, nb, seq):
    ts = _pool_tile(seq)
    nt = seq // ts

    def body(u_ref, halo_ref, wp_ref, ps_ref, yp_ref, p_ref):
        i = pl.program_id(1)
        halo = jnp.where(i == 0, 0.0, halo_ref[...])
        u = u_ref[...]
        ext = jnp.concatenate([halo, u], 0)
        tpos = i * ts + _iota((ts, 1), 0)
        for g, w in enumerate(POOL_WINDOWS):
            gs = slice(g * POOL_GROUP, (g + 1) * POOL_GROUP)
            s = ext[:, gs]
            sh = 1
            while sh < w:
                s = s + pltpu.roll(s, sh, 0)
                sh *= 2
            cnt = jnp.minimum(tpos + 1, w).astype(F32)
            pb = (s[HALO:] / cnt - u[:, gs]).astype(BF)
            p_ref[:, gs] = pb
            yp_ref[:, gs] = (_nn(pb, wp_ref[g].astype(BF)) * ps_ref[:, gs]).astype(BF)

    hb = ts // HALO
    return pl.pallas_call(
        body, name="pool_fwd", grid=(nb, nt),
        in_specs=[pl.BlockSpec((ts, POOL_WIDTH), lambda b, i: (b * nt + i, 0)),
                  pl.BlockSpec((HALO, POOL_WIDTH), lambda b, i: (jnp.maximum((b * nt + i) * hb - 1, 0), 0)),
                  pl.BlockSpec((4, POOL_GROUP, POOL_GROUP), lambda b, i: (0, 0, 0)),
                  pl.BlockSpec((1, POOL_WIDTH), lambda b, i: (0, 0))],
        out_specs=[pl.BlockSpec((ts, POOL_WIDTH), lambda b, i: (b * nt + i, 0))] * 2,
        out_shape=[_out((nb * seq, POOL_WIDTH), BF)] * 2,
        compiler_params=_cp(("parallel", "parallel"), 32))(*_pin(proj, proj, w_pool, pool_scale))


def _conv_pre(uxbc, halo, cw, cb, first):
    halo = jnp.where(first, 0.0, halo)
    ext = jnp.concatenate([halo, uxbc], 0)
    pre = cb + uxbc * cw[3:4]
    for k in (2, 1, 0):
        pre = pre + pltpu.roll(ext, 3 - k, 0)[CONV_HALO:] * cw[k:k + 1]
    return pre


def _chunk_terms(pre, udt, dtb, alog):
    sg = _sigmoid(pre)
    xbc = pre * sg
    dtp = udt[:, :SSD_HEADS] + dtb
    dt = jnp.maximum(dtp, 0.0) + jnp.log(1.0 + jnp.exp(-jnp.abs(dtp)))
    a = -jnp.exp(alog)
    da = dt * a
    tril = (_iota((CHUNK, CHUNK), 0) >= _iota((CHUNK, CHUNK), 1))
    acum = _exact_nn_left(tril.astype(BF), da)
    eye = (_iota((SSD_HEADS, SSD_HEADS), 0) == _iota((SSD_HEADS, SSD_HEADS), 1)).astype(BF)
    acum_t = _exact_nt_left(eye, acum)
    expand = _head_expand_matrix(SSD_HEADS, SSD_INNER)
    acum_e = _exact_nn(acum, expand)
    dt_e = _exact_nn(dt, expand)
    last_e = acum_e[CHUNK - 1:CHUNK]
    return dict(pre=pre, sg=sg, xbc=xbc, dtp=dtp, dt=dt, a=a, acum=acum, acum_t=acum_t, tril=tril,
                dt_e=dt_e, e_a=jnp.exp(acum_e), d_out=jnp.exp(last_e - acum_e), c_dec=jnp.exp(last_e))


def _head_decay(r, h):
    seg = r["acum"][:, h:h + 1] - r["acum_t"][h:h + 1, :]
    return jnp.where(r["tril"], jnp.exp(jnp.minimum(seg, 0.0)), 0.0)


def _ssd_specs(nb, seq, reverse):
    nc = seq // CHUNK
    per = seq // CONV_HALO

    def cidx(c):
        return (nc - 1 - c) if reverse else c

    def row(b, c):
        return b * nc + cidx(c)

    specs = [
        pl.BlockSpec((CHUNK, CONV_CH), lambda b, c: (row(b, c), 1)),
        pl.BlockSpec((CONV_HALO, CONV_CH),
                     lambda b, c: (jnp.maximum(b * per + cidx(c) * (CHUNK // CONV_HALO) - 1, 0), 1)),
        pl.BlockSpec((CHUNK, GROUP_W), lambda b, c: (row(b, c), 1)),
        pl.BlockSpec((CHUNK, GROUP_W), lambda b, c: (row(b, c), 2)),
        pl.BlockSpec((CHUNK, 128), lambda b, c: (row(b, c), OFF_DT // 128)),
    ]
    return specs, row, cidx, nc


def _const_spec(shape):
    return pl.BlockSpec(shape, lambda b, c: (0,) * len(shape))


def _ssd_fwd(proj, conv_w, conv_b, dt_bias, a_log, dskip_e, g_ssd, nb, seq):
    specs, row, cidx, nc = _ssd_specs(nb, seq, reverse=False)

    def body(uxbc_ref, halo_ref, z0_ref, z1_ref, udt_ref, cw_ref, cb_ref, dtb_ref, alog_ref, dsk_ref, gs_ref,
             yssd_ref, yssm_ref, hprev_ref, pre_ref, h_ref, yd_ref):
        c = pl.program_id(1)

        @pl.when(c == 0)
        def _():
            h_ref[...] = jnp.zeros_like(h_ref)

        pre = _conv_pre(uxbc_ref[...], halo_ref[...], cw_ref[...], cb_ref[...], c == 0)
        pre_ref[...] = pre
        r = _chunk_terms(pre, udt_ref[...], dtb_ref[...], alog_ref[...])
        xbc = r["xbc"]
        xs = xbc[:, :SSD_INNER]
        xdt = xs * r["dt_e"]
        xdt_b = xdt.astype(BF)
        xdo_b = (xdt * r["d_out"]).astype(BF)
        hprev_ref[0, 0] = h_ref[...]
        for g in range(2):
            gs = slice(g * GROUP_W, (g + 1) * GROUP_W)
            bg = xbc[:, SSD_INNER + g * SSD_STATE:SSD_INNER + (g + 1) * SSD_STATE].astype(BF)
            cg = xbc[:, SSD_INNER + (2 + g) * SSD_STATE:SSD_INNER + (3 + g) * SSD_STATE].astype(BF)
            scores = _nt(cg, bg)
            hg = h_ref[g]
            y_off = _nn(cg, hg.astype(BF)) * r["e_a"][:, gs]
            for hh in range(8):
                h = g * 8 + hh
                m = (scores * _head_decay(r, h)).astype(BF)
                yd_ref[:, h * SSD_HEAD_DIM:(h + 1) * SSD_HEAD_DIM] = _nn(m, xdt_b[:, h * SSD_HEAD_DIM:(h + 1) * SSD_HEAD_DIM])
            h_ref[g] = hg * r["c_dec"][:, gs] + _tn(bg, xdo_b[:, gs])
            y = yd_ref[:, gs] + y_off + dsk_ref[:, gs] * xs[:, gs]
            yssm_ref[:, gs] = y
            zg = (z0_ref if g == 0 else z1_ref)[...]
            yg = y * (zg * _sigmoid(zg))
            rg = lax.rsqrt(jnp.mean(yg * yg, -1, keepdims=True) + EPS)
            yssd_ref[:, gs] = (yg * rg * gs_ref[:, gs]).astype(BF)

    t = nb * seq
    return pl.pallas_call(
        body, name="ssd_fwd", grid=(nb, nc),
        in_specs=specs + [_const_spec((4, CONV_CH)), _const_spec((1, CONV_CH)), _const_spec((1, SSD_HEADS)),
                          _const_spec((1, SSD_HEADS)), _const_spec((1, SSD_INNER)), _const_spec((1, SSD_INNER))],
        out_specs=[pl.BlockSpec((CHUNK, SSD_INNER), lambda b, c: (row(b, c), 0)),
                   pl.BlockSpec((CHUNK, SSD_INNER), lambda b, c: (row(b, c), 0)),
                   pl.BlockSpec((1, 1, 2, SSD_STATE, GROUP_W), lambda b, c: (b, c, 0, 0, 0)),
                   pl.BlockSpec((CHUNK, CONV_CH), lambda b, c: (row(b, c), 0))],
        out_shape=[_out((t, SSD_INNER), BF), _out((t, SSD_INNER), F32),
                   _out((nb, nc, 2, SSD_STATE, GROUP_W), F32), _out((t, CONV_CH), F32)],
        scratch_shapes=[pltpu.VMEM((2, SSD_STATE, GROUP_W), F32), pltpu.VMEM((CHUNK, SSD_INNER), F32)],
        compiler_params=_cp(("arbitrary", "arbitrary"), 48),
    )(*_pin(proj, proj, proj, proj, proj, conv_w, conv_b, dt_bias, a_log, dskip_e, g_ssd))


def _out_proj(y_pool, y_ssd, w_out, x, mod3, g_mlp, seq):
    t, d = x.shape
    tm = 512
    tps = seq // tm if seq >= tm else 1
    tm = min(tm, seq)

    def body(yp_ref, ys_ref, w_ref, x_ref, mod_ref, g_ref, h1_ref, o_ref, u2_ref):
        o = _nn(yp_ref[...], w_ref[0:POOL_WIDTH, :]) + _nn(ys_ref[...], w_ref[POOL_WIDTH:, :])
        o_ref[...] = o.astype(BF)
        h1 = x_ref[...] + mod_ref[0, 2:3, :] * o
        h1_ref[...] = h1
        r = lax.rsqrt(jnp.mean(h1 * h1, -1, keepdims=True) + EPS)
        u2_ref[...] = ((h1 * r * g_ref[...]) * (1.0 + mod_ref[0, 4:5, :]) + mod_ref[0, 3:4, :]).astype(BF)

    row = lambda i: (i, 0)
    return pl.pallas_call(
        body, name="out_proj", grid=(t // tm,),
        in_specs=[pl.BlockSpec((tm, POOL_WIDTH), row), pl.BlockSpec((tm, SSD_INNER), row),
                  _RESIDENT, pl.BlockSpec((tm, d), row),
                  pl.BlockSpec((1, N_MOD, d), lambda i: (i // tps, 0, 0)), pl.BlockSpec((1, d), lambda i: (0, 0))],
        out_specs=[pl.BlockSpec((tm, d), row)] * 3,
        out_shape=[_out((t, d), F32), _out((t, d), BF), _out((t, d), BF)],
        compiler_params=_cp(("parallel",), 48))(*_pin(y_pool, y_ssd), w_out, *_pin(x, mod3, g_mlp))


def _mlp_up(u2, w_up4):
    t, d = u2.shape
    tm = min(1024, t)
    nk, _, cols = w_up4.shape

    def body(u_ref, w_ref, a_ref):
        a_ref[...] = _nn(u_ref[...], w_ref[pl.program_id(1)]).astype(BF)

    return pl.pallas_call(
        body, name="mlp_up", grid=(t // tm, nk),
        in_specs=[pl.BlockSpec((tm, d), lambda i, k: (i, 0)), _RESIDENT],
        out_specs=pl.BlockSpec((tm, cols), lambda i, k: (i, k)),
        out_shape=_out((t, nk * cols), BF),
        compiler_params=_cp(("parallel", "parallel"), 32))(*_pin(u2), w_up4)


def _mlp_down_loss(a_up, w_down, h1, mod3, g_final, target, seq):
    t, d = h1.shape
    nb = t // seq
    tm = min(ROW_TILE, seq)
    tps = seq // tm

    def body(a_ref, w_ref, h1_ref, mod_ref, g_ref, tg_ref, ddn_ref, dh2_ref, sq_ref, gg_ref, dgf_ref):
        i = pl.program_id(0)

        @pl.when(i == 0)
        def _():
            sq_ref[...] = jnp.zeros_like(sq_ref)
            gg_ref[...] = jnp.zeros_like(gg_ref)

        @pl.when(i % tps == 0)
        def _():
            dgf_ref[...] = jnp.zeros_like(dgf_ref)

        gate = mod_ref[0, 5:6, :]
        sq = gg = dgf = 0.0
        for rows in _sub_rows(tm):
            f = jnp.square(jnp.maximum(a_ref[rows, :], 0))
            dn = _nn(f, w_ref[...])
            h2 = h1_ref[rows, :] + gate * dn
            r = lax.rsqrt(jnp.mean(h2 * h2, -1, keepdims=True) + EPS)
            hh = h2 * r
            err = hh * g_ref[...] - tg_ref[rows, :]
            dy = err * (1.0 / d)
            dhat = dy * g_ref[...]
            dh2 = r * (dhat - hh * jnp.mean(dhat * hh, -1, keepdims=True))
            dh2_ref[rows, :] = dh2
            ddn_ref[rows, :] = (dh2 * gate).astype(BF)
            sq = sq + jnp.sum(err * err, 0, keepdims=True)
            gg = gg + jnp.sum(dy * hh, 0, keepdims=True)
            dgf = dgf + jnp.sum(dh2 * dn, 0, keepdims=True)
        sq_ref[...] += sq
        gg_ref[...] += gg
        dgf_ref[0] += dgf

    row = lambda i: (i, 0)
    vec = pl.BlockSpec((1, d), lambda i: (0, 0))
    return pl.pallas_call(
        body, name="mlp_down_loss", grid=(t // tm,),
        in_specs=[pl.BlockSpec((tm, D_FF), row), _RESIDENT, pl.BlockSpec((tm, d), row),
                  pl.BlockSpec((1, N_MOD, d), lambda i: (i // tps, 0, 0)), vec, pl.BlockSpec((tm, d), row)],
        out_specs=[pl.BlockSpec((tm, d), row), pl.BlockSpec((tm, d), row), vec, vec,
                   pl.BlockSpec((1, 1, d), lambda i: (i // tps, 0, 0))],
        out_shape=[_out((t, d), BF), _out((t, d), F32), _out((1, d), F32),
                   _out((1, d), F32), _out((nb, 1, d), F32)],
        compiler_params=_cp(("arbitrary",), 44))(*_pin(a_up), w_down, *_pin(h1, mod3, g_final, target))


def _tn_matmul(a, b, tk, tn, name, square_relu=False, out3=False):
    t, kdim = a.shape
    ndim = b.shape[1]

    def body(a_ref, b_ref, o_ref):
        av = a_ref[...]
        if square_relu:
            av = jnp.square(jnp.maximum(av, 0))
        res = _tn(av, b_ref[...]).astype(BF)
        if out3:
            o_ref[0] = res
        else:
            o_ref[...] = res

    if out3:
        out_spec = pl.BlockSpec((1, tk, tn), lambda j, i: (j, i, 0))
        out_shape = _out((ndim // tn, kdim, tn), BF)
    else:
        out_spec = pl.BlockSpec((tk, tn), lambda j, i: (i, j))
        out_shape = _out((kdim, ndim), BF)
    return pl.pallas_call(
        body, name=name, grid=(ndim // tn, kdim // tk),
        in_specs=[pl.BlockSpec((t, tk), lambda j, i: (0, i)), pl.BlockSpec((t, tn), lambda j, i: (0, j))],
        out_specs=out_spec, out_shape=out_shape,
        compiler_params=_cp(("parallel", "parallel"), 56))(*_pin(a, b))


def _mlp_down_bwd(d_dn, w_down4, a_up, token):
    t, d = d_dn.shape
    tm = min(1024, t)
    nk, rows, _ = w_down4.shape

    def body(g_ref, w_ref, a_ref, tok_ref, o_ref):
        df = _nt(g_ref[...], w_ref[pl.program_id(1)])
        o_ref[...] = (df * (2.0 * jnp.maximum(a_ref[...], 0).astype(F32))).astype(BF)

    return pl.pallas_call(
        body, name="mlp_down_bwd", grid=(t // tm, nk),
        in_specs=[pl.BlockSpec((tm, d), lambda i, k: (i, 0)), _RESIDENT,
                  pl.BlockSpec((tm, rows), lambda i, k: (i, k)), _token_spec()],
        out_specs=pl.BlockSpec((tm, rows), lambda i, k: (i, k)),
        out_shape=_out((t, nk * rows), BF),
        compiler_params=_cp(("parallel", "parallel"), 32))(*_pin(d_dn), w_down4, *_pin(a_up, token))


def _mlp_up_bwd(d_a, w_up4, h1, dh2, o, mod3, g_mlp, seq, token):
    t, d = h1.shape
    nb = t // seq
    tm = min(ROW_TILE, seq)
    tps = seq // tm
    nk = w_up4.shape[0]
    cols = w_up4.shape[2]

    def body(da_ref, w_ref, h1_ref, dh2_ref, o_ref, mod_ref, g_ref, tok_ref, dh1_ref, do_ref, acc_ref, gg_ref):
        i = pl.program_id(0)

        @pl.when(i == 0)
        def _():
            gg_ref[...] = jnp.zeros_like(gg_ref)

        @pl.when(i % tps == 0)
        def _():
            acc_ref[...] = jnp.zeros_like(acc_ref)

        gg = a_shift = a_scale = a_gate = 0.0
        for rows in _sub_rows(tm):
            du = _nt(da_ref[rows, 0:cols], w_ref[0])
            for k in range(1, nk):
                du = du + _nt(da_ref[rows, k * cols:(k + 1) * cols], w_ref[k])
            h1 = h1_ref[rows, :]
            r = lax.rsqrt(jnp.mean(h1 * h1, -1, keepdims=True) + EPS)
            hh = h1 * r
            n2 = hh * g_ref[...]
            dn2 = du * (1.0 + mod_ref[0, 4:5, :])
            dhat = dn2 * g_ref[...]
            dh1 = dh2_ref[rows, :] + r * (dhat - hh * jnp.mean(dhat * hh, -1, keepdims=True))
            dh1_ref[rows, :] = dh1
            do_ref[rows, :] = (dh1 * mod_ref[0, 2:3, :]).astype(BF)
            gg = gg + jnp.sum(dn2 * hh, 0, keepdims=True)
            a_shift = a_shift + jnp.sum(du, 0, keepdims=True)
            a_scale = a_scale + jnp.sum(du * n2, 0, keepdims=True)
            a_gate = a_gate + jnp.sum(dh1 * o_ref[rows, :].astype(F32), 0, keepdims=True)
        gg_ref[...] += gg
        acc_ref[0, 0:1, :] += a_shift
        acc_ref[0, 1:2, :] += a_scale
        acc_ref[0, 2:3, :] += a_gate

    row = lambda i: (i, 0)
    vec = pl.BlockSpec((1, d), lambda i: (0, 0))
    return pl.pallas_call(
        body, name="mlp_up_bwd", grid=(t // tm,),
        in_specs=[pl.BlockSpec((tm, D_FF), row), _RESIDENT, pl.BlockSpec((tm, d), row),
                  pl.BlockSpec((tm, d), row), pl.BlockSpec((tm, d), row),
                  pl.BlockSpec((1, N_MOD, d), lambda i: (i // tps, 0, 0)), vec, _token_spec()],
        out_specs=[pl.BlockSpec((tm, d), row), pl.BlockSpec((tm, d), row),
                   pl.BlockSpec((1, 8, d), lambda i: (i // tps, 0, 0)), vec],
        out_shape=[_out((t, d), F32), _out((t, d), BF),
                   _out((nb, 8, d), F32), _out((1, d), F32)],
        compiler_params=_cp(("arbitrary",), 44))(*_pin(d_a), w_up4, *_pin(h1, dh2, o, mod3, g_mlp, token))


def _out_proj_bwd(d_o, w_out, token):
    t, d = d_o.shape
    tm = min(512, t)

    def body(g_ref, w_ref, tok_ref, dp_ref, ds_ref):
        gv = g_ref[...]
        dp_ref[...] = _nt(gv, w_ref[0:POOL_WIDTH, :])
        ds_ref[...] = _nt(gv, w_ref[POOL_WIDTH:, :])

    row = lambda i: (i, 0)
    return pl.pallas_call(
        body, name="out_proj_bwd", grid=(t // tm,),
        in_specs=[pl.BlockSpec((tm, d), row), _RESIDENT, _token_spec()],
        out_specs=[pl.BlockSpec((tm, POOL_WIDTH), row), pl.BlockSpec((tm, SSD_INNER), row)],
        out_shape=[_out((t, POOL_WIDTH), F32), _out((t, SSD_INNER), F32)],
        compiler_params=_cp(("parallel",), 32))(*_pin(d_o), w_out, *_pin(token))


def _pool_bwd(d_ypool, p, w_pool, pool_scale, nb, seq):
    ts = _pool_tile(seq)
    nt = seq // ts
    hb = ts // HALO
    last_block = nb * seq // HALO - 1

    def body(dy_ref, halo_ref, p_ref, wp_ref, ps_ref, du_ref, gw_ref, gs_ref):
        b = pl.program_id(0)
        i = pl.program_id(1)

        @pl.when((b == 0) & (i == 0))
        def _():
            gw_ref[...] = jnp.zeros_like(gw_ref)
            gs_ref[...] = jnp.zeros_like(gs_ref)

        halo = jnp.where(i == nt - 1, 0.0, halo_ref[...])
        dy = dy_ref[...]
        ext = jnp.concatenate([dy, halo], 0)
        tpos = i * ts + _iota((ts + HALO, 1), 0)
        n_ext = ts + HALO
        for g, w in enumerate(POOL_WINDOWS):
            gs = slice(g * POOL_GROUP, (g + 1) * POOL_GROUP)
            wg = wp_ref[g].astype(BF)
            pg = p_ref[:, gs]
            pw = _nn(pg, wg)
            gs_ref[:, gs] += jnp.sum(dy[:, gs] * pw, 0, keepdims=True)
            dpw = (ext[:, gs] * ps_ref[:, gs]).astype(BF)
            gw_ref[g] += _tn(pg, dpw[:ts])
            dp = _nt(dpw, wg)
            cnt = jnp.minimum(tpos + 1, w).astype(F32)
            s = dp / cnt
            sh = 1
            while sh < w:
                s = s + pltpu.roll(s, n_ext - sh, 0)
                sh *= 2
            du_ref[:, gs] = (s[:ts] - dp[:ts]).astype(BF)

    return pl.pallas_call(
        body, name="pool_bwd", grid=(nb, nt),
        in_specs=[pl.BlockSpec((ts, POOL_WIDTH), lambda b, i: (b * nt + i, 0)),
                  pl.BlockSpec((HALO, POOL_WIDTH), lambda b, i: (jnp.minimum((b * nt + i + 1) * hb, last_block), 0)),
                  pl.BlockSpec((ts, POOL_WIDTH), lambda b, i: (b * nt + i, 0)),
                  pl.BlockSpec((4, POOL_GROUP, POOL_GROUP), lambda b, i: (0, 0, 0)),
                  pl.BlockSpec((1, POOL_WIDTH), lambda b, i: (0, 0))],
        out_specs=[pl.BlockSpec((ts, POOL_WIDTH), lambda b, i: (b * nt + i, 0)),
                   pl.BlockSpec((4, POOL_GROUP, POOL_GROUP), lambda b, i: (0, 0, 0)),
                   pl.BlockSpec((1, POOL_WIDTH), lambda b, i: (0, 0))],
        out_shape=[_out((nb * seq, POOL_WIDTH), BF), _out((4, POOL_GROUP, POOL_GROUP), F32),
                   _out((1, POOL_WIDTH), F32)],
        compiler_params=_cp(("arbitrary", "arbitrary"), 32))(*_pin(d_ypool, d_ypool, p, w_pool, pool_scale))


def _ssd_bwd(proj, pre, d_yssd, yssm, h_prev, dt_bias, a_log, dskip_e, g_ssd, nb, seq):
    specs, row, cidx, nc = _ssd_specs(nb, seq, reverse=True)
    specs = specs[2:]

    def body(z0_ref, z1_ref, udt_ref, pre_ref, dys_ref, yssm_ref, hprev_ref,
             dtb_ref, alog_ref, dsk_ref, gs_ref,
             dz_ref, dpre_ref, dudt_ref, ggs_ref, gdsk_ref, ga_ref, gdtb_ref,
             g_ref, dxdt_ref, dyv_ref):
        b = pl.program_id(0)
        c = pl.program_id(1)

        @pl.when(c == 0)
        def _():
            g_ref[...] = jnp.zeros_like(g_ref)

        @pl.when((b == 0) & (c == 0))
        def _():
            ggs_ref[...] = jnp.zeros_like(ggs_ref)
            gdsk_ref[...] = jnp.zeros_like(gdsk_ref)
            ga_ref[...] = jnp.zeros_like(ga_ref)
            gdtb_ref[...] = jnp.zeros_like(gdtb_ref)

        r = _chunk_terms(pre_ref[...], udt_ref[...], dtb_ref[...], alog_ref[...])
        xbc = r["xbc"]
        xs = xbc[:, :SSD_INNER]
        dt_e = r["dt_e"]
        xdt = xs * dt_e
        xdt_b = xdt.astype(BF)
        reduce_m = _head_reduce_matrix(GROUP_W, 8)

        def head_sums(v):
            return _nn(v.astype(BF), reduce_m)

        onehot16 = lambda h: (_iota((1, SSD_HEADS), 1) == h).astype(F32)
        onecol16 = lambda h: (_iota((SSD_HEADS, 1), 0) == h).astype(F32)

        d_acum = jnp.zeros((CHUNK, SSD_HEADS), F32)
        d_acum_t = jnp.zeros((SSD_HEADS, CHUNK), F32)
        d_alast = jnp.zeros((1, SSD_HEADS), F32)
        place8 = lambda g: (_iota((8, SSD_HEADS), 1) == _iota((8, SSD_HEADS), 0) + 8 * g).astype(BF)
        d_b, d_c = [], []
        for g in range(2):
            gs = slice(g * GROUP_W, (g + 1) * GROUP_W)
            zg = (z0_ref if g == 0 else z1_ref)[...]
            sz = _sigmoid(zg)
            silu_z = zg * sz
            ys = yssm_ref[:, gs]
            yg = ys * silu_z
            rg = lax.rsqrt(jnp.mean(yg * yg, -1, keepdims=True) + EPS)
            yh = yg * rg
            dys = dys_ref[:, gs]
            ggs_ref[:, gs] += jnp.sum(dys * yh, 0, keepdims=True)
            dyh = dys * gs_ref[:, gs]
            dyg = rg * (dyh - yh * jnp.mean(dyh * yh, -1, keepdims=True))
            dy = dyg * silu_z
            dz_ref[:, gs] = (dyg * ys * (sz * (1.0 + zg * (1.0 - sz)))).astype(BF)
            gdsk_ref[:, gs] += jnp.sum(dy * xs[:, gs], 0, keepdims=True)
            dyv_ref[:, gs] = dy
            dy_b = dy.astype(BF)

            bg = xbc[:, SSD_INNER + g * SSD_STATE:SSD_INNER + (g + 1) * SSD_STATE].astype(BF)
            cg = xbc[:, SSD_INNER + (2 + g) * SSD_STATE:SSD_INNER + (3 + g) * SSD_STATE].astype(BF)
            scores = _nt(cg, bg)
            hg = hprev_ref[0, 0, g]
            hg_b = hg.astype(BF)
            gg = g_ref[g]
            gg_b = gg.astype(BF)
            e_a = r["e_a"][:, gs]
            d_out = r["d_out"][:, gs]
            c_dec = r["c_dec"][:, gs]
            zc = _nn(cg, hg_b)
            wv = e_a * dy
            wv_b = wv.astype(BF)
            da_g = head_sums(wv * zc)
            dcg = _nt(wv_b, hg_b)
            d_hprev = _tn(cg, wv_b)
            vg = _nn(bg, gg_b)
            dxdt_g = d_out * vg
            dd_out = head_sums(xdt[:, gs] * vg)
            dbg = _nt((xdt[:, gs] * d_out).astype(BF), gg_b)
            dcd = _exact_nn(jnp.sum(gg * hg, 0, keepdims=True), reduce_m)
            d_out8 = jnp.exp(r["acum"][CHUNK - 1:CHUNK, 8 * g:8 * g + 8] - r["acum"][:, 8 * g:8 * g + 8])
            c_dec8 = jnp.exp(r["acum"][CHUNK - 1:CHUNK, 8 * g:8 * g + 8])
            t8 = dd_out * d_out8
            d_alast = d_alast + _exact_nn(jnp.sum(t8, 0, keepdims=True) + dcd * c_dec8, place8(g))
            d_acum = d_acum + _exact_nn(da_g - t8, place8(g))
            dsc = jnp.zeros((CHUNK, CHUNK), F32)
            for hh in range(8):
                h = g * 8 + hh
                hs = slice(h * SSD_HEAD_DIM, (h + 1) * SSD_HEAD_DIM)
                lam = _head_decay(r, h)
                m = scores * lam
                dyh_b = dy_b[:, hh * SSD_HEAD_DIM:(hh + 1) * SSD_HEAD_DIM]
                dm = _nt(dyh_b, xdt_b[:, hs])
                tm_ = dm * m
                d_acum = d_acum + jnp.sum(tm_, 1, keepdims=True) * onehot16(h)
                d_acum_t = d_acum_t + onecol16(h) * jnp.sum(tm_, 0, keepdims=True)
                dsc = dsc + dm * lam
                dxdt_ref[:, hs] = _tn(m.astype(BF), dyh_b) + dxdt_g[:, hh * SSD_HEAD_DIM:(hh + 1) * SSD_HEAD_DIM]
            dsc_b = dsc.astype(BF)
            d_c.append(dcg + _nn(dsc_b, bg))
            d_b.append(dbg + _tn(dsc_b, cg))
            g_ref[g] = d_hprev + c_dec * gg

        eye = (_iota((CHUNK, CHUNK), 0) == _iota((CHUNK, CHUNK), 1)).astype(BF)
        d_acum = d_acum - _exact_nt_left(eye, d_acum_t)
        is_last = (_iota((CHUNK, 1), 0) == CHUNK - 1).astype(F32)
        d_acum = d_acum + is_last * d_alast
        triu = (_iota((CHUNK, CHUNK), 0) <= _iota((CHUNK, CHUNK), 1)).astype(BF)
        d_da = _exact_nn_left(triu, d_acum)
        dt = r["dt"]
        ga_ref[...] += jnp.sum(d_da * dt, 0, keepdims=True)
        dxdt = dxdt_ref[...]
        reduce16 = _head_reduce_matrix(SSD_INNER, SSD_HEADS)
        d_dt = d_da * r["a"] + _nn((dxdt * xs).astype(BF), reduce16)
        d_udt = d_dt * _sigmoid(r["dtp"])
        gdtb_ref[...] += jnp.sum(d_udt, 0, keepdims=True)
        dudt_ref[...] = jnp.zeros_like(dudt_ref)
        dudt_ref[:, 0:SSD_HEADS] = d_udt.astype(BF)
        pre, sg = r["pre"], r["sg"]
        dsilu = sg * (1.0 + pre * (1.0 - sg))
        dpre_ref[:, 0:SSD_INNER] = (dsk_ref[...] * dyv_ref[...] + dxdt * dt_e) * dsilu[:, 0:SSD_INNER]
        for g in range(2):
            bs = slice(SSD_INNER + g * SSD_STATE, SSD_INNER + (g + 1) * SSD_STATE)
            cs = slice(SSD_INNER + (2 + g) * SSD_STATE, SSD_INNER + (3 + g) * SSD_STATE)
            dpre_ref[:, bs] = d_b[g] * dsilu[:, bs]
            dpre_ref[:, cs] = d_c[g] * dsilu[:, cs]

    t = nb * seq
    vec = _const_spec((1, SSD_INNER))
    small = _const_spec((1, SSD_HEADS))
    return pl.pallas_call(
        body, name="ssd_bwd", grid=(nb, nc),
        in_specs=specs + [pl.BlockSpec((CHUNK, CONV_CH), lambda b, c: (row(b, c), 0)),
                          pl.BlockSpec((CHUNK, SSD_INNER), lambda b, c: (row(b, c), 0)),
                          pl.BlockSpec((CHUNK, SSD_INNER), lambda b, c: (row(b, c), 0)),
                          pl.BlockSpec((1, 1, 2, SSD_STATE, GROUP_W), lambda b, c: (b, cidx(c), 0, 0, 0)),
                          small, small, vec, vec],
        out_specs=[pl.BlockSpec((CHUNK, SSD_INNER), lambda b, c: (row(b, c), 0)),
                   pl.BlockSpec((CHUNK, CONV_CH), lambda b, c: (row(b, c), 0)),
                   pl.BlockSpec((CHUNK, 128), lambda b, c: (row(b, c), 0)),
                   vec, vec, small, small],
        out_shape=[_out((t, SSD_INNER), BF), _out((t, CONV_CH), F32),
                   _out((t, 128), BF), _out((1, SSD_INNER), F32),
                   _out((1, SSD_INNER), F32), _out((1, SSD_HEADS), F32),
                   _out((1, SSD_HEADS), F32)],
        scratch_shapes=[pltpu.VMEM((2, SSD_STATE, GROUP_W), F32), pltpu.VMEM((CHUNK, SSD_INNER), F32),
                        pltpu.VMEM((CHUNK, SSD_INNER), F32)],
        compiler_params=_cp(("arbitrary", "arbitrary"), 48),
    )(*_pin(proj, proj, proj, pre, d_yssd, yssm, h_prev, dt_bias, a_log, dskip_e, g_ssd))


def _grad_w_in_t(d_upool, d_z, d_uxbc, d_udt, u1):
    t, d = u1.shape
    tk = 512
    n_z, n_x = SSD_INNER // tk, CONV_CH // tk

    def body(p_ref, z_ref, x_ref, dt_ref, u_ref, o_ref):
        i = pl.program_id(0)

        @pl.when(i == 0)
        def _():
            o_ref[...] = _tn(p_ref[...], u_ref[...]).astype(BF)

        @pl.when((i >= 1) & (i < 1 + n_z))
        def _():
            o_ref[...] = _tn(z_ref[...], u_ref[...]).astype(BF)

        @pl.when((i >= 1 + n_z) & (i < 1 + n_z + n_x))
        def _():
            o_ref[...] = _tn(x_ref[...], u_ref[...]).astype(BF)

        @pl.when(i == 1 + n_z + n_x)
        def _():
            o_ref[0:128, :] = _tn(dt_ref[...], u_ref[...]).astype(BF)

    return pl.pallas_call(
        body, name="grad_w_in", grid=(2 + n_z + n_x,),
        in_specs=[pl.BlockSpec((t, tk), lambda i: (0, 0)),
                  pl.BlockSpec((t, tk), lambda i: (0, jnp.clip(i - 1, 0, n_z - 1))),
                  pl.BlockSpec((t, tk), lambda i: (0, jnp.clip(i - 1 - n_z, 0, n_x - 1))),
                  pl.BlockSpec((t, 128), lambda i: (0, 0)), pl.BlockSpec((t, d), lambda i: (0, 0))],
        out_specs=pl.BlockSpec((tk, d), lambda i: (i, 0)),
        out_shape=_out((IN_PAD, d), BF),
        compiler_params=_cp(("parallel",), 56))(*_pin(d_upool, d_z, d_uxbc, d_udt, u1))


def _conv_bwd(d_pre, proj, conv_w, nb, seq):
    ts = min(256, seq)
    nt = seq // ts
    hb = ts // CONV_HALO
    last_block = nb * seq // CONV_HALO - 1
    n_ext = CHUNK + CONV_HALO

    def body(dp_ref, dnext_ref, u_ref, cw_ref, du_ref, gw_ref, gb_ref):
        b = pl.program_id(0)
        i = pl.program_id(1)

        @pl.when((b == 0) & (i == 0))
        def _():
            gw_ref[...] = jnp.zeros_like(gw_ref)
            gb_ref[...] = jnp.zeros_like(gb_ref)

        for c0 in range(0, CONV_CH, 128):
            cs = slice(c0, c0 + 128)
            cw = cw_ref[:, cs]
            gw = [0.0] * 4
            gb = 0.0
            for r0 in range(0, ts, CHUNK):
                dp = dp_ref[r0:r0 + CHUNK, cs]
                u = u_ref[r0:r0 + CHUNK, cs]
                if r0 + CHUNK < ts:
                    below = dp_ref[r0 + CHUNK:r0 + CHUNK + CONV_HALO, cs]
                else:
                    below = jnp.where(i == nt - 1, 0.0, dnext_ref[:, cs])
                ext_d = jnp.concatenate([dp, below], 0)
                du = dp * cw[3:4]
                gw[3] = gw[3] + jnp.sum(dp * u, 0, keepdims=True)
                for k in (2, 1, 0):
                    shifted = pltpu.roll(ext_d, n_ext - (3 - k), 0)[:CHUNK]
                    du = du + shifted * cw[k:k + 1]
                    gw[k] = gw[k] + jnp.sum(shifted * u, 0, keepdims=True)
                gb = gb + jnp.sum(dp, 0, keepdims=True)
                du_ref[r0:r0 + CHUNK, cs] = du.astype(BF)
            for k in range(4):
                gw_ref[k:k + 1, cs] += gw[k]
            gb_ref[:, cs] += gb

    return pl.pallas_call(
        body, name="conv_bwd", grid=(nb, nt),
        in_specs=[pl.BlockSpec((ts, CONV_CH), lambda b, i: (b * nt + i, 0)),
                  pl.BlockSpec((CONV_HALO, CONV_CH), lambda b, i: (jnp.minimum((b * nt + i + 1) * hb, last_block), 0)),
                  pl.BlockSpec((ts, CONV_CH), lambda b, i: (b * nt + i, 1)),
                  pl.BlockSpec((4, CONV_CH), lambda b, i: (0, 0))],
        out_specs=[pl.BlockSpec((ts, CONV_CH), lambda b, i: (b * nt + i, 0)),
                   pl.BlockSpec((8, CONV_CH), lambda b, i: (0, 0)), pl.BlockSpec((1, CONV_CH), lambda b, i: (0, 0))],
        out_shape=[_out((nb * seq, CONV_CH), BF), _out((8, CONV_CH), F32),
                   _out((1, CONV_CH), F32)],
        compiler_params=_cp(("arbitrary", "arbitrary"), 48))(*_pin(d_pre, d_pre, proj, conv_w))


def _in_proj_bwd(d_parts, w_in_t, x, dh1, mod3, g_mix, seq, token):
    t, d = x.shape
    nb = t // seq
    tm = min(ROW_TILE, seq)
    tps = seq // tm

    widths = [p.shape[1] for p in d_parts]

    def body(d0_ref, d1_ref, d2_ref, d3_ref, w_ref, x_ref, dh1_ref, mod_ref, g_ref, tok_ref, gx_ref, acc_ref, gg_ref):
        i = pl.program_id(0)

        @pl.when(i == 0)
        def _():
            gg_ref[...] = jnp.zeros_like(gg_ref)

        @pl.when(i % tps == 0)
        def _():
            acc_ref[...] = jnp.zeros_like(acc_ref)

        gg = a_shift = a_scale = 0.0
        for rows in _sub_rows(tm):
            du = None
            off = 0
            for p_ref, wd in zip((d0_ref, d1_ref, d2_ref, d3_ref), widths):
                part = _nn(p_ref[rows, :], w_ref[off:off + wd, :])
                du = part if du is None else du + part
                off += wd
            xv = x_ref[rows, :]
            r = lax.rsqrt(jnp.mean(xv * xv, -1, keepdims=True) + EPS)
            hh = xv * r
            n1 = hh * g_ref[...]
            dn1 = du * (1.0 + mod_ref[0, 1:2, :])
            dhat = dn1 * g_ref[...]
            gx_ref[rows, :] = dh1_ref[rows, :] + r * (dhat - hh * jnp.mean(dhat * hh, -1, keepdims=True))
            gg = gg + jnp.sum(dn1 * hh, 0, keepdims=True)
            a_shift = a_shift + jnp.sum(du, 0, keepdims=True)
            a_scale = a_scale + jnp.sum(du * n1, 0, keepdims=True)
        gg_ref[...] += gg
        acc_ref[0, 0:1, :] += a_shift
        acc_ref[0, 1:2, :] += a_scale

    row = lambda i: (i, 0)
    vec = pl.BlockSpec((1, d), lambda i: (0, 0))
    return pl.pallas_call(
        body, name="in_proj_bwd", grid=(t // tm,),
        in_specs=[pl.BlockSpec((tm, wd), row) for wd in widths] +
                 [_RESIDENT, pl.BlockSpec((tm, d), row),
                  pl.BlockSpec((tm, d), row), pl.BlockSpec((1, N_MOD, d), lambda i: (i // tps, 0, 0)), vec, _token_spec()],
        out_specs=[pl.BlockSpec((tm, d), row), pl.BlockSpec((1, 8, d), lambda i: (i // tps, 0, 0)), vec],
        out_shape=[_out((t, d), F32), _out((nb, 8, d), F32),
                   _out((1, d), F32)],
        compiler_params=_cp(("arbitrary",), 40))(*_pin(*d_parts), w_in_t, *_pin(x, dh1, mod3, g_mix, token))


_VEC_LAYOUT = (("g_mix", 1024), ("conv_b", 1536), ("g_ssd", 1024), ("pool_scale", 512), ("g_mlp", 1024),
               ("g_final", 1024), ("dt_bias", 128), ("a_log", 128), ("d_skip_lanes", 1024), ("sq_err", 1024))
_VEC_OFFSET = {}
_off = 0
for _name, _n in _VEC_LAYOUT:
    _VEC_OFFSET[_name] = _off
    _off += _n
_VEC_LANES = _off
_SMALL_PARAMS = ("b_ada", "g_mix", "conv_w", "conv_b", "dt_bias", "a_log", "d_skip", "g_ssd", "w_pool", "pool_scale",
                 "g_mlp", "g_final")


def _pack_vec(parts):
    cols = []
    for name, n in _VEC_LAYOUT:
        v = parts[name]
        if v.shape[1] < n:
            v = jnp.pad(v, ((0, 0), (0, n - v.shape[1])))
        cols.append(v)
    return jnp.concatenate(cols, 1)


def _small_adam(vec_all, wpool_all, convw_all, dmod_all, params):
    names = _SMALL_PARAMS
    nin = 4 + 3 * len(names)

    def body(*refs):
        vec_ref, wp_ref, cw_ref, dm_ref = refs[:4]
        prm = {n: refs[4 + 3 * i:7 + 3 * i] for i, n in enumerate(names)}
        loss_ref = refs[nin]
        outs = {n: refs[nin + 1 + 4 * i:nin + 5 + 4 * i] for i, n in enumerate(names)}
        vsum = vec_ref[0]
        for s in range(1, N_DEV):
            vsum = vsum + vec_ref[s]

        def lanes(name, n):
            off = _VEC_OFFSET[name]
            return vsum[:, off:off + n]

        grads = {n: lanes(n, prm[n][0].shape[1]) for n in ("g_mix", "conv_b", "g_ssd", "pool_scale", "g_mlp", "g_final", "dt_bias")}
        grads["a_log"] = lanes("a_log", SSD_HEADS) * (-jnp.exp(prm["a_log"][0][...]))
        per_lane = jnp.broadcast_to(lanes("d_skip_lanes", SSD_INNER), (8, SSD_INNER))
        grads["d_skip"] = _exact_nn(per_lane, _head_reduce_matrix(SSD_INNER, SSD_HEADS))[0:1]
        gwp = wp_ref[0].astype(F32)
        gcw = cw_ref[0]
        gb = jnp.sum(dm_ref[0], 0, keepdims=True)
        for s in range(1, N_DEV):
            gwp = gwp + wp_ref[s].astype(F32)
            gcw = gcw + cw_ref[s]
            gb = gb + jnp.sum(dm_ref[s], 0, keepdims=True)
        grads["w_pool"] = gwp
        grads["conv_w"] = gcw[0:4]
        grads["b_ada"] = gb
        total = jnp.sum(lanes("sq_err", D_MODEL), 1, keepdims=True) * (0.5 / D_MODEL)
        loss_ref[...] = jnp.broadcast_to(total, loss_ref.shape)
        for n in names:
            w_ref, m_ref, v_ref = prm[n]
            g = grads[n]
            d, m2, v2 = _adam_math(w_ref[...], g, m_ref[...], v_ref[...])
            g_ref, d_ref, m2_ref, v2_ref = outs[n]
            g_ref[...] = g
            d_ref[...] = d
            m2_ref[...] = m2
            v2_ref[...] = v2

    flat = [vec_all, wpool_all, convw_all, dmod_all]
    out_shape = [jax.ShapeDtypeStruct((1, 128), F32)]
    for n in names:
        flat += list(params[n])
        out_shape += [jax.ShapeDtypeStruct(params[n][0].shape, F32)] * 4
    vm = pl.BlockSpec(memory_space=pltpu.VMEM)
    res = pl.pallas_call(body, name="small_adam", out_shape=out_shape, in_specs=[vm] * len(flat),
                         out_specs=[vm] * len(out_shape), compiler_params=_cp(vmem_mb=48))(*flat)
    return res[0], {n: res[1 + 4 * i:5 + 4 * i] for i, n in enumerate(names)}


_WEIGHTS = ("w_ada", "b_ada", "g_mix", "w_in", "conv_w", "conv_b", "dt_bias", "a_log", "d_skip", "g_ssd", "w_pool",
            "pool_scale", "w_out", "g_mlp", "w_up", "w_down", "g_final")


def _local_step(x2, tg2, mod3, seq, w_in_t, first_token, weights_later, start_reduce, conv_w_full, sp):
    t, d = x2.shape
    nb = t // seq
    dskip_e = jnp.repeat(sp["d_skip"], SSD_HEAD_DIM, axis=1)
    proj, u1 = _in_proj(x2, mod3, sp["g_mix"], w_in_t, seq, first_token)
    y_pool, p = _pool_fwd(proj, sp["w_pool"], sp["pool_scale"], nb, seq)
    y_ssd, yssm, h_prev, pre = _ssd_fwd(proj, conv_w_full, sp["conv_b"], sp["dt_bias"], sp["a_log"], dskip_e, sp["g_ssd"], nb, seq)
    w_out_f, w_up4, w_down4 = weights_later(y_ssd)
    w_down_f = w_down4.reshape(D_FF, d)
    h1, o, u2 = _out_proj(y_pool, y_ssd, w_out_f, x2, mod3, sp["g_mlp"], seq)
    a_up = _mlp_up(u2, w_up4)
    d_dn, dh2, sq, gg_final, d_gf = _mlp_down_loss(a_up, w_down_f, h1, mod3, sp["g_final"], tg2, seq)

    gw_down = _tn_matmul(a_up, d_dn, 512, d, "grad_w_down", square_relu=True)
    tok = start_reduce("w_down", gw_down.reshape(N_CHIPS, D_FF // N_CHIPS, d))
    d_a = _mlp_down_bwd(d_dn, w_down4, a_up, tok)
    gw_up4 = _tn_matmul(u2, d_a, 512, d, "grad_w_up", out3=True)
    tok = start_reduce("w_up", gw_up4)
    dh1, d_o, accf, gg_mlp = _mlp_up_bwd(d_a, w_up4, h1, dh2, o, mod3, sp["g_mlp"], seq, tok)
    gw_out_pool = _tn_matmul(y_pool, d_o, 512, d, "grad_w_out_pool")
    gw_out_ssd = _tn_matmul(y_ssd, d_o, 512, d, "grad_w_out_ssd")
    gw_out = jnp.concatenate([gw_out_pool, gw_out_ssd], 0)
    tok = start_reduce("w_out", gw_out.reshape(N_CHIPS, gw_out.shape[0] // N_CHIPS, d))
    d_ypool, d_yssd = _out_proj_bwd(d_o, w_out_f, tok)
    d_upool, gw_pool, g_ps = _pool_bwd(d_ypool, p, sp["w_pool"], sp["pool_scale"], nb, seq)
    d_z, d_pre, d_udt, gg_ssd, gdsk, ga, gdtb = _ssd_bwd(proj, pre, d_yssd, yssm, h_prev, sp["dt_bias"], sp["a_log"],
                                                        dskip_e, sp["g_ssd"], nb, seq)
    d_uxbc, gconvw, gconvb = _conv_bwd(d_pre, proj, conv_w_full, nb, seq)
    gw_in_t = _grad_w_in_t(d_upool, d_z, d_uxbc, d_udt, u1)
    tok = start_reduce("w_in", gw_in_t[:IN_WIDTH].reshape(N_CHIPS, IN_WIDTH // N_CHIPS, d))
    gx, accm, gg_mix = _in_proj_bwd([d_upool, d_z, d_uxbc, d_udt], w_in_t, x2, dh1, mod3, sp["g_mix"], seq, tok)

    d_mod = jnp.concatenate([accm[:, 0], accm[:, 1], accf[:, 2], accf[:, 0], accf[:, 1], d_gf[:, 0]], 1)
    vec = _pack_vec({"g_mix": gg_mix, "conv_b": gconvb, "g_ssd": gg_ssd, "pool_scale": g_ps, "g_mlp": gg_mlp,
                     "g_final": gg_final, "dt_bias": gdtb, "a_log": ga, "d_skip_lanes": gdsk, "sq_err": sq})
    return gx, d_mod, vec, gw_pool, gconvw


def kernel(x, c, w_ada, b_ada, g_mix, w_in, conv_w, conv_b, dt_bias, a_log, d_skip, g_ssd, w_pool, pool_scale, w_out, g_mlp, w_up, w_down, g_final, loss_target, m_w_ada, m_b_ada, m_g_mix, m_w_in, m_conv_w, m_conv_b, m_dt_bias, m_a_log, m_d_skip, m_g_ssd, m_w_pool, m_pool_scale, m_w_out, m_g_mlp, m_w_up, m_w_down, m_g_final, v_w_ada, v_b_ada, v_g_mix, v_w_in, v_conv_w, v_conv_b, v_dt_bias, v_a_log, v_d_skip, v_g_ssd, v_w_pool, v_pool_scale, v_w_out, v_g_mlp, v_w_up, v_w_down, v_g_final):
    nb, seq, d = x.shape
    t = nb * seq
    xi, yi, ci = _mesh_pos()
    chip = 2 * xi + yi
    me = 4 * xi + 2 * yi + ci
    ada_cols = w_ada.shape[2]
    conv_cols = conv_w.shape[2]
    in_cols = w_in.shape[2]
    w_in_s, m_w_in_s, v_w_in_s = w_in[0].T, m_w_in[0].T, v_w_in[0].T

    c8, convw8 = _all_gather_small([c, conv_w[0]])
    w_in_b = w_in_s.astype(BF)
    i_send, i_recv, i_src, i_land, in_token = _ici_start(
        [w_in_b], [jax.ShapeDtypeStruct((N_CHIPS,) + w_in_b.shape, BF)], _gather_sent, _gather_landing, "gather_start_w_in",
        after=c8)
    c_all = c8.reshape(N_DEV * nb, d)
    conv_w_full = convw8[0::2].transpose(1, 0, 2).reshape(4, N_CHIPS * conv_cols)
    b_shard = lax.dynamic_slice(b_ada, (0, chip * ada_cols), (1, ada_cols))
    mod_part, c_act = _ada_mod(c_all, w_ada[0], b_shard, in_token)
    mod_rows = mod_part.reshape(N_DEV, nb, ada_cols)
    m_send, m_recv, m_src, m_land, _ = _ici_start(
        [mod_rows], [jax.ShapeDtypeStruct((N_CHIPS, nb, ada_cols), F32)], _mod_sent, _mod_landing, "mod_start", after=mod_part)

    in_shard, in_land = _ici_wait(i_send, i_recv, i_src, i_land, m_src[0], _gather_sent, _gather_landing, "gather_wait_w_in")
    (w_in4,) = _gather_finish(in_land, in_shard)
    w_in_t = jnp.pad(w_in4.reshape(N_CHIPS * in_cols, d), ((0, IN_PAD - N_CHIPS * in_cols), (0, 0)))
    mod_mine, mod_land = _ici_wait(m_send, m_recv, m_src, m_land, w_in_t, _mod_sent, _mod_landing, "mod_wait")
    mod_own = lax.dynamic_slice(mod_mine[0], (me, 0, 0), (1, nb, ada_cols))
    mod4 = lax.dynamic_update_slice(mod_land[0], mod_own, (chip, 0, 0))
    mod3 = mod4.transpose(1, 0, 2).reshape(nb, N_MOD, d)
    later = [w_out[0].astype(BF), w_up[0].astype(BF), w_down[0].astype(BF)]
    g_send, g_recv, g_src, g_land, first_token = _ici_start(
        later, [jax.ShapeDtypeStruct((N_CHIPS,) + s.shape, BF) for s in later], _gather_sent, _gather_landing, "gather_start",
        after=w_in4)

    def weights_later(after):
        shards, lands = _ici_wait(g_send, g_recv, g_src, g_land, after, _gather_sent, _gather_landing, "gather_wait")
        w_out4, w_up4, w_down4 = _gather_finish(lands, shards)
        return w_out4.reshape(N_CHIPS * w_out.shape[1], d), w_up4, w_down4

    pending = {}

    def start_reduce(name, grad4):
        pending[name] = _reduce_start(grad4, "reduce_start_" + name)
        return pending[name][4]

    sp = dict(g_mix=g_mix, conv_b=conv_b, dt_bias=dt_bias, a_log=a_log, d_skip=d_skip, g_ssd=g_ssd,
              w_pool=w_pool[0], pool_scale=pool_scale, g_mlp=g_mlp, g_final=g_final.reshape(1, d))
    gx, d_mod, vec, gw_pool, gconvw = _local_step(
        x.reshape(t, d), loss_target.reshape(t, d), mod3, seq, w_in_t, first_token, weights_later, start_reduce, conv_w_full, sp)

    pos = jnp.stack([ci, chip, me]).astype(jnp.int32)
    small_parts = [vec, gw_pool.reshape(4 * POOL_GROUP, POOL_GROUP).astype(BF), gconvw, d_mod]
    s_send, s_recv, s_src, s_land, s_token = _exchange_start(small_parts, ALL_PEERS, "small_start")
    halves = []
    for name in ("w_in", "w_out", "w_up", "w_down"):
        r_send, r_recv, r_src, r_land, _ = pending[name]
        own, recv = _reduce_wait(r_send, r_recv, r_src, r_land, s_token, "reduce_wait_" + name)
        halves.append(_sum_eight(recv, own, pos))
    h_send, h_recv, h_src, h_land, h_token = _exchange_start(halves, SIBLING, "halves_start")

    s_own, s_got = _exchange_wait(s_send, s_recv, s_src, s_land, ALL_PEERS, h_token, "small_wait")
    vec8, wpool8, convw8g, dmod8 = [lax.dynamic_update_slice(got, mine[None], (me,) + (0,) * mine.ndim)
                                    for got, mine in zip(s_got, s_own)]
    convw8s = lax.dynamic_slice(convw8g, (0, 0, chip * conv_cols), (N_DEV, 8, conv_cols))
    m_in = dict(b_ada=m_b_ada, g_mix=m_g_mix, conv_w=m_conv_w[0], conv_b=m_conv_b, dt_bias=m_dt_bias, a_log=m_a_log,
                d_skip=m_d_skip, g_ssd=m_g_ssd, w_pool=m_w_pool.reshape(4 * POOL_GROUP, POOL_GROUP), pool_scale=m_pool_scale,
                g_mlp=m_g_mlp, g_final=m_g_final.reshape(1, d))
    v_in = dict(b_ada=v_b_ada, g_mix=v_g_mix, conv_w=v_conv_w[0], conv_b=v_conv_b, dt_bias=v_dt_bias, a_log=v_a_log,
                d_skip=v_d_skip, g_ssd=v_g_ssd, w_pool=v_w_pool.reshape(4 * POOL_GROUP, POOL_GROUP), pool_scale=v_pool_scale,
                g_mlp=v_g_mlp, g_final=v_g_final.reshape(1, d))
    w_small = dict(sp, b_ada=b_ada, conv_w=conv_w[0], w_pool=w_pool.reshape(4 * POOL_GROUP, POOL_GROUP))
    loss_row, small = _small_adam(vec8, wpool8, convw8s, dmod8, {n: (w_small[n], m_in[n], v_in[n]) for n in _SMALL_PARAMS})

    dmod_all = dmod8.reshape(N_DEV * nb, N_CHIPS * ada_cols)
    dmod_cols = lax.dynamic_slice(dmod_all, (0, chip * ada_cols), (N_DEV * nb, ada_cols))
    res = {n: tuple(r.reshape(w.shape) for r in small[n])
           for n, w in (("b_ada", b_ada), ("g_mix", g_mix), ("conv_w", conv_w), ("conv_b", conv_b), ("dt_bias", dt_bias),
                        ("a_log", a_log), ("d_skip", d_skip), ("g_ssd", g_ssd), ("w_pool", w_pool), ("pool_scale", pool_scale),
                        ("g_mlp", g_mlp), ("g_final", g_final))}
    g_ada, d_ada, m_ada, v_ada = _adam_ada(c_act.T.astype(BF), dmod_cols, w_ada[0], m_w_ada[0], v_w_ada[0])
    res["w_ada"] = (g_ada[None], d_ada[None], m_ada[None], v_ada[None])
    h_own, h_got = _exchange_wait(h_send, h_recv, h_src, h_land, SIBLING, g_ada, "halves_wait")
    g_in, dl, m2, v2 = _adam_big(h_own[0], h_got[0], w_in_s, m_w_in_s, v_w_in_s, pos)
    res["w_in"] = (g_in.T[None], dl.T[None], m2.T[None], v2.T[None])
    for i, (n, w, m, v) in enumerate((("w_out", w_out, m_w_out, v_w_out), ("w_up", w_up, m_w_up, v_w_up),
                                      ("w_down", w_down, m_w_down, v_w_down))):
        g, dl, m2, v2 = _adam_big(h_own[i + 1], h_got[i + 1], w[0], m[0], v[0], pos)
        res[n] = (g[None], dl[None], m2[None], v2[None])

    loss = loss_row[0, 0]
    return (loss, gx.reshape(nb, seq, d), *[res[n][0] for n in _WEIGHTS], *[res[n][1] for n in _WEIGHTS],
            *[res[n][2] for n in _WEIGHTS], *[res[n][3] for n in _WEIGHTS])
```

```python
import functools

import jax
import jax.numpy as jnp
from jax import lax
from jax.experimental import pallas as pl
from jax.experimental.pallas import tpu as pltpu

F32 = jnp.float32
BF = jnp.bfloat16
MESH = pl.DeviceIdType.MESH

EPS = 1e-5
D_MODEL = 1024
POOL_WIDTH = 512
POOL_WINDOWS = (2, 4, 8, 16)
POOL_GROUP = 128
SSD_INNER = 1024
SSD_HEADS = 16
SSD_HEAD_DIM = 64
SSD_STATE = 128
GROUP_W = 512
CHUNK = 128
CONV_CH = 1536
OFF_Z = 512
OFF_XBC = 1536
OFF_DT = 3072
IN_WIDTH = 3088
IN_PAD = 3200
D_FF = 4096
N_MOD = 6
N_CHIPS = 4
N_DEV = 8
HALO = 16
CONV_HALO = 8

ADAM_LR = 0.001
ADAM_B1 = 0.9
ADAM_B2 = 0.999
ADAM_EPS = 1e-08
ADAM_WD = 0.01
ADAM_STEP = 10

VMEM_BYTES_V7X = 64 * 1024 * 1024


def _cp(semantics=None, vmem_mb=48, **kw):
    args = dict(vmem_limit_bytes=vmem_mb * 1024 * 1024, **kw)
    if semantics is not None:
        args["dimension_semantics"] = semantics
    return pltpu.CompilerParams(**args)


def _out(shape, dtype):
    return pltpu.HBM(shape, dtype)


def _pin(*arrays):
    return [pltpu.with_memory_space_constraint(a, pltpu.HBM) for a in arrays]


def _nn(a, b):
    return jnp.dot(a, b, preferred_element_type=F32)


def _nt(a, b):
    return lax.dot_general(a, b, (((1,), (1,)), ((), ())), preferred_element_type=F32)


def _tn(a, b):
    return lax.dot_general(a, b, (((0,), (0,)), ((), ())), preferred_element_type=F32)


def _split3(v):
    hi = v.astype(BF)
    r1 = v - hi.astype(F32)
    mid = r1.astype(BF)
    lo = (r1 - mid.astype(F32)).astype(BF)
    return hi, mid, lo


def _exact_nn(v, m01):
    hi, mid, lo = _split3(v)
    return _nn(hi, m01) + _nn(mid, m01) + _nn(lo, m01)


def _exact_nn_left(m01, v):
    hi, mid, lo = _split3(v)
    return _nn(m01, hi) + _nn(m01, mid) + _nn(m01, lo)


def _exact_nt_left(m01, v):
    hi, mid, lo = _split3(v)
    return _nt(m01, hi) + _nt(m01, mid) + _nt(m01, lo)


def _sigmoid(v):
    return 1.0 / (1.0 + jnp.exp(-v))


def _iota(shape, dim):
    return lax.broadcasted_iota(jnp.int32, shape, dim)


def _head_expand_matrix(heads, width):
    return (_iota((heads, width), 1) // SSD_HEAD_DIM == _iota((heads, width), 0)).astype(BF)


def _head_reduce_matrix(width, heads):
    return (_iota((width, heads), 0) // SSD_HEAD_DIM == _iota((width, heads), 1)).astype(BF)


def _mesh_pos():
    return lax.axis_index("x"), lax.axis_index("y"), lax.axis_index("c")


def _flip(v, bit):
    return v + bit - 2 * bit * v


def _all_gather_small(arrays, token=None):
    n = len(arrays)
    extra = [] if token is None else [token]

    def body(*refs):
        in_refs, out_refs = refs[:n], refs[n + len(extra):2 * n + len(extra)]
        send_sems, recv_sems, local_sems = refs[2 * n + len(extra):]
        x, y, c = _mesh_pos()
        me = 4 * x + 2 * y + c
        local = []
        for a in range(n):
            cp = pltpu.make_async_copy(in_refs[a], out_refs[a].at[me], local_sems.at[a])
            cp.start()
            local.append(cp)
        sends = []
        for k in range(1, N_DEV):
            peer = (_flip(x, (k >> 2) & 1), _flip(y, (k >> 1) & 1), _flip(c, k & 1))
            for a in range(n):
                cp = pltpu.make_async_remote_copy(
                    src_ref=in_refs[a], dst_ref=out_refs[a].at[me],
                    send_sem=send_sems.at[a, k], recv_sem=recv_sems.at[a, k],
                    device_id=peer, device_id_type=MESH)
                cp.start()
                sends.append(cp)
        for k in range(1, N_DEV):
            px, py, pc = _flip(x, (k >> 2) & 1), _flip(y, (k >> 1) & 1), _flip(c, k & 1)
            src = 4 * px + 2 * py + pc
            for a in range(n):
                pltpu.make_async_remote_copy(
                    src_ref=in_refs[a], dst_ref=out_refs[a].at[src],
                    send_sem=send_sems.at[a, k], recv_sem=recv_sems.at[a, k],
                    device_id=(px, py, pc), device_id_type=MESH).wait_recv()
        for cp in sends:
            cp.wait_send()
        for cp in local:
            cp.wait()

    vm = pl.BlockSpec(memory_space=pltpu.VMEM)
    return pl.pallas_call(
        body, name="all_gather_small",
        out_shape=[jax.ShapeDtypeStruct((N_DEV,) + a.shape, a.dtype) for a in arrays],
        in_specs=[vm] * (n + len(extra)), out_specs=[vm] * n,
        scratch_shapes=[pltpu.SemaphoreType.DMA((n, N_DEV)), pltpu.SemaphoreType.DMA((n, N_DEV)),
                        pltpu.SemaphoreType.DMA((n,))],
        compiler_params=_cp(vmem_mb=32),
    )(*arrays, *extra)


def _weight_all_gather(shards):
    n = len(shards)
    any_spec = _HBM

    def body(*refs):
        in_refs, out_refs = refs[:n], refs[n:2 * n]
        send_sems, recv_sems, fsend_sems, frecv_sems, local_sems = refs[2 * n:]
        x, y, c = _mesh_pos()
        chip = 2 * x + y
        local, sends = [], []
        for a in range(n):
            cp = pltpu.make_async_copy(in_refs[a], out_refs[a].at[chip], local_sems.at[a])
            cp.start()
            local.append(cp)

        def half(a, which):
            hc = shards[a].shape[1] // 2
            return pl.ds(pl.multiple_of(which * hc, 128), hc)

        for j in range(1, N_CHIPS):
            px, py = _flip(x, (j >> 1) & 1), _flip(y, j & 1)
            for a in range(n):
                cp = pltpu.make_async_remote_copy(
                    src_ref=in_refs[a].at[:, half(a, c)], dst_ref=out_refs[a].at[chip, :, half(a, c)],
                    send_sem=send_sems.at[a, j], recv_sem=recv_sems.at[a, j],
                    device_id=(px, py, c), device_id_type=MESH)
                cp.start()
                sends.append(cp)
        for j in range(1, N_CHIPS):
            px, py = _flip(x, (j >> 1) & 1), _flip(y, j & 1)
            src = 2 * px + py
            for a in range(n):
                landed = out_refs[a].at[src, :, half(a, c)]
                pltpu.make_async_remote_copy(
                    src_ref=landed, dst_ref=landed, send_sem=send_sems.at[a, j], recv_sem=recv_sems.at[a, j],
                    device_id=(px, py, c), device_id_type=MESH).wait_recv()
                cp = pltpu.make_async_remote_copy(
                    src_ref=landed, dst_ref=landed, send_sem=fsend_sems.at[a, j], recv_sem=frecv_sems.at[a, j],
                    device_id=(x, y, 1 - c), device_id_type=MESH)
                cp.start()
                sends.append(cp)
        for j in range(1, N_CHIPS):
            px, py = _flip(x, (j >> 1) & 1), _flip(y, j & 1)
            src = 2 * px + py
            for a in range(n):
                other = out_refs[a].at[src, :, half(a, 1 - c)]
                pltpu.make_async_remote_copy(
                    src_ref=other, dst_ref=other, send_sem=fsend_sems.at[a, j], recv_sem=frecv_sems.at[a, j],
                    device_id=(x, y, 1 - c), device_id_type=MESH).wait_recv()
        for cp in sends:
            cp.wait_send()
        for cp in local:
            cp.wait()

    return pl.pallas_call(
        body, name="weight_all_gather",
        out_shape=[_out((N_CHIPS,) + s.shape, s.dtype) for s in shards],
        in_specs=[any_spec] * n, out_specs=[any_spec] * n,
        scratch_shapes=[pltpu.SemaphoreType.DMA((n, N_CHIPS))] * 4 + [pltpu.SemaphoreType.DMA((n,))],
        compiler_params=_cp(vmem_mb=16),
    )(*shards)


_HBM = pl.BlockSpec(memory_space=pltpu.HBM)
_SEM = pl.BlockSpec(memory_space=pltpu.SEMAPHORE)
_DATAFLOW = pltpu.SideEffectType.DATAFLOW_SIDE_EFFECTING


def _peer_chip(x, y, j):
    return _flip(x, (j >> 1) & 1), _flip(y, j & 1)


def _ici_start(srcs, land_shapes, sent, landing, name, after):
    n = len(srcs)

    def body(*refs):
        src_refs, land_refs = refs[:n], refs[n:2 * n]
        send_sems, recv_sems = refs[2 * n + 1], refs[2 * n + 2]
        token = refs[-1]
        x, y, c = _mesh_pos()
        for j in range(1, N_CHIPS):
            px, py = _peer_chip(x, y, j)
            for a in range(n):
                pltpu.make_async_remote_copy(
                    src_ref=sent(src_refs[a], c, 2 * px + py), dst_ref=landing(land_refs[a], c, 2 * x + y),
                    send_sem=send_sems.at[a * (N_CHIPS - 1) + j - 1], recv_sem=recv_sems.at[a * (N_CHIPS - 1) + j - 1],
                    device_id=(px, py, c), device_id_type=MESH).start()
        token[...] = jnp.zeros_like(token)

    sems = pltpu.SemaphoreType.DMA((n * (N_CHIPS - 1),))
    lands = [pltpu.with_memory_space_constraint(lax.empty(s.shape, s.dtype), pltpu.HBM) for s in land_shapes]
    outs = pl.pallas_call(
        body, name=name,
        out_shape=(sems, sems, *[pltpu.HBM(s.shape, s.dtype) for s in srcs],
                   *[pltpu.HBM(s.shape, s.dtype) for s in land_shapes], jax.ShapeDtypeStruct((8, 128), F32)),
        in_specs=[_HBM] * (2 * n + 1), out_specs=[_SEM, _SEM] + [_HBM] * (2 * n) + [pl.BlockSpec(memory_space=pltpu.VMEM)],
        input_output_aliases={i: 2 + i for i in range(2 * n)},
        compiler_params=pltpu.CompilerParams(has_side_effects=_DATAFLOW),
    )(*_pin(*srcs), *lands, *_pin(after))
    return outs[0], outs[1], outs[2:2 + n], outs[2 + n:2 + 2 * n], outs[-1]


def _ici_wait(send_sems, recv_sems, src_thru, land_thru, after, sent, landing, name):
    n = len(src_thru)
    afters = list(after) if isinstance(after, (list, tuple)) else [after]

    def body(*refs):
        src_refs, land_refs = refs[:n], refs[n:2 * n]
        send_sems, recv_sems = refs[2 * n], refs[2 * n + 1]
        x, y, c = _mesh_pos()
        for j in range(1, N_CHIPS):
            px, py = _peer_chip(x, y, j)
            for a in range(n):
                cp = pltpu.make_async_remote_copy(
                    src_ref=sent(src_refs[a], c, 2 * px + py), dst_ref=landing(land_refs[a], c, 2 * px + py),
                    send_sem=send_sems.at[a * (N_CHIPS - 1) + j - 1], recv_sem=recv_sems.at[a * (N_CHIPS - 1) + j - 1],
                    device_id=(px, py, c), device_id_type=MESH)
                cp.wait_send()
                cp.wait_recv()

    outs = pl.pallas_call(
        body, name=name,
        out_shape=tuple(pltpu.HBM(s.shape, s.dtype) for s in (*src_thru, *land_thru)),
        in_specs=[_HBM] * (2 * n) + [_SEM, _SEM] + [_HBM] * len(afters), out_specs=[_HBM] * (2 * n),
        input_output_aliases={i: i for i in range(2 * n)},
        compiler_params=pltpu.CompilerParams(has_side_effects=_DATAFLOW),
    )(*src_thru, *land_thru, send_sems, recv_sems, *_pin(*afters))
    return outs[:n], outs[n:]


def _col_half(ref, which, lead=()):
    hc = ref.shape[-1] // 2
    return ref.at[(*lead, slice(None), pl.ds(pl.multiple_of(which * hc, 128), hc))]


def _gather_sent(ref, c, dst_chip):
    return _col_half(ref, c)


def _gather_landing(ref, c, src_chip):
    return _col_half(ref, c, lead=(src_chip,))


def _mod_sent(ref, c, dst_chip):
    return ref.at[2 * dst_chip + c]


def _mod_landing(ref, c, src_chip):
    return ref.at[src_chip]


def _reduce_copy(src_ref, land_ref, send_sems, recv_sems, k, receiving):
    x, y, c = _mesh_pos()
    px, py, pc = _flip(x, (k >> 2) & 1), _flip(y, (k >> 1) & 1), _flip(c, k & 1)
    hc = src_ref.shape[2] // 2
    src = src_ref.at[2 * px + py, :, pl.ds(pl.multiple_of(pc * hc, 128), hc)]
    slot = (4 * px + 2 * py + pc) if receiving else (4 * x + 2 * y + c)
    return pltpu.make_async_remote_copy(
        src_ref=src, dst_ref=land_ref.at[slot], send_sem=send_sems.at[k - 1], recv_sem=recv_sems.at[k - 1],
        device_id=(px, py, pc), device_id_type=MESH)


def _reduce_start(grad4, name):
    k4, r, cols = grad4.shape

    def body(src_ref, land_ref, send_sems, recv_sems, src_thru, land_thru, token):
        for k in range(1, N_DEV):
            _reduce_copy(src_ref, land_ref, send_sems, recv_sems, k, receiving=False).start()
        token[...] = jnp.zeros_like(token)

    sems = pltpu.SemaphoreType.DMA((N_DEV - 1,))
    land = pltpu.with_memory_space_constraint(lax.empty((N_DEV, r, cols // 2), grad4.dtype), pltpu.HBM)
    return pl.pallas_call(
        body, name=name,
        out_shape=(sems, sems, pltpu.HBM(grad4.shape, grad4.dtype), pltpu.HBM(land.shape, land.dtype),
                   jax.ShapeDtypeStruct((8, 128), F32)),
        in_specs=[_HBM, _HBM], out_specs=[_SEM, _SEM, _HBM, _HBM, pl.BlockSpec(memory_space=pltpu.VMEM)],
        input_output_aliases={0: 2, 1: 3},
        compiler_params=pltpu.CompilerParams(has_side_effects=_DATAFLOW),
    )(*_pin(grad4), land)


def _reduce_wait(send_sems, recv_sems, src_thru, land_thru, after, name):
    def body(src_ref, land_ref, send_sems, recv_sems, after_ref, src_out, land_out):
        for k in range(1, N_DEV):
            cp = _reduce_copy(src_ref, land_ref, send_sems, recv_sems, k, receiving=True)
            cp.wait_send()
            cp.wait_recv()

    return pl.pallas_call(
        body, name=name,
        out_shape=(pltpu.HBM(src_thru.shape, src_thru.dtype), pltpu.HBM(land_thru.shape, land_thru.dtype)),
        in_specs=[_HBM, _HBM, _SEM, _SEM, _HBM], out_specs=[_HBM, _HBM],
        input_output_aliases={0: 0, 1: 1},
        compiler_params=pltpu.CompilerParams(has_side_effects=_DATAFLOW),
    )(src_thru, land_thru, send_sems, recv_sems, *_pin(after))


def _peer_copy(src_ref, land_ref, send_sems, recv_sems, idx, k, receiving):
    x, y, c = _mesh_pos()
    px, py, pc = _flip(x, (k >> 2) & 1), _flip(y, (k >> 1) & 1), _flip(c, k & 1)
    if land_ref.shape[0] == N_DEV:
        slot = (4 * px + 2 * py + pc) if receiving else (4 * x + 2 * y + c)
    else:
        slot = pc if receiving else c
    return pltpu.make_async_remote_copy(
        src_ref=src_ref, dst_ref=land_ref.at[slot], send_sem=send_sems.at[idx], recv_sem=recv_sems.at[idx],
        device_id=(px, py, pc), device_id_type=MESH)


def _exchange_start(arrays, peers, name):
    n = len(arrays)

    def body(*refs):
        src_refs, land_refs = refs[:n], refs[n:2 * n]
        send_sems, recv_sems = refs[2 * n], refs[2 * n + 1]
        token = refs[-1]
        for j, k in enumerate(peers):
            for a in range(n):
                _peer_copy(src_refs[a], land_refs[a], send_sems, recv_sems, a * len(peers) + j, k, receiving=False).start()
        token[...] = jnp.zeros_like(token)

    sems = pltpu.SemaphoreType.DMA((n * len(peers),))
    n_slots = N_DEV if len(peers) > 1 else 2
    lands = [pltpu.with_memory_space_constraint(lax.empty((n_slots,) + a.shape, a.dtype), pltpu.HBM) for a in arrays]
    outs = pl.pallas_call(
        body, name=name,
        out_shape=(sems, sems, *[pltpu.HBM(a.shape, a.dtype) for a in arrays], *[pltpu.HBM(l.shape, l.dtype) for l in lands],
                   jax.ShapeDtypeStruct((8, 128), F32)),
        in_specs=[_HBM] * (2 * n), out_specs=[_SEM, _SEM] + [_HBM] * (2 * n) + [pl.BlockSpec(memory_space=pltpu.VMEM)],
        input_output_aliases={i: 2 + i for i in range(2 * n)},
        compiler_params=pltpu.CompilerParams(has_side_effects=_DATAFLOW),
    )(*_pin(*arrays), *lands)
    return outs[0], outs[1], outs[2:2 + n], outs[2 + n:2 + 2 * n], outs[-1]


def _exchange_wait(send_sems, recv_sems, src_thru, land_thru, peers, after, name):
    n = len(src_thru)

    def body(*refs):
        src_refs, land_refs = refs[:n], refs[n:2 * n]
        send_sems, recv_sems = refs[2 * n], refs[2 * n + 1]
        for j, k in enumerate(peers):
            for a in range(n):
                cp = _peer_copy(src_refs[a], land_refs[a], send_sems, recv_sems, a * len(peers) + j, k, receiving=True)
                cp.wait_send()
                cp.wait_recv()

    outs = pl.pallas_call(
        body, name=name,
        out_shape=tuple(pltpu.HBM(s.shape, s.dtype) for s in (*src_thru, *land_thru)),
        in_specs=[_HBM] * (2 * n) + [_SEM, _SEM, _HBM], out_specs=[_HBM] * (2 * n),
        input_output_aliases={i: i for i in range(2 * n)},
        compiler_params=pltpu.CompilerParams(has_side_effects=_DATAFLOW),
    )(*src_thru, *land_thru, send_sems, recv_sems, *_pin(after))
    return outs[:n], outs[n:]


ALL_PEERS = tuple(range(1, N_DEV))
SIBLING = (1,)


def _sum_eight(recv, grad4, pos):
    n, r, hc = recv.shape
    steps = 4
    tc = hc // steps

    def body(pos_ref, r_ref, g_ref, o_ref):
        me = pos_ref[2]
        o_ref[...] = jnp.zeros_like(o_ref)
        for s in range(n):
            @pl.when(me == s)
            def _():
                o_ref[...] += g_ref[0].astype(F32)

            @pl.when(me != s)
            def _():
                o_ref[...] += r_ref[s].astype(F32)

    grid_spec = pltpu.PrefetchScalarGridSpec(
        num_scalar_prefetch=1, grid=(steps,),
        in_specs=[pl.BlockSpec((n, r, tc), lambda i, pos: (0, 0, i)),
                  pl.BlockSpec((1, r, tc), lambda i, pos: (pos[1], 0, pos[0] * steps + i))],
        out_specs=pl.BlockSpec((r, tc), lambda i, pos: (0, i)))
    return pl.pallas_call(body, name="sum_eight", grid_spec=grid_spec, out_shape=_out((r, hc), F32),
                          compiler_params=_cp(("parallel",), 32))(pos, *_pin(recv, grad4))


def _gather_finish(lands, shards):
    n = len(lands)
    any_spec = _HBM

    def body(*refs):
        shard_refs, out_refs = refs[n:2 * n], refs[2 * n:3 * n]
        send_sems, recv_sems, local_sems = refs[3 * n:]
        x, y, c = _mesh_pos()
        chip = 2 * x + y
        local, sends = [], []
        for a in range(n):
            cp = pltpu.make_async_copy(shard_refs[a], out_refs[a].at[chip], local_sems.at[a])
            cp.start()
            local.append(cp)
        for j in range(1, N_CHIPS):
            px, py = _peer_chip(x, y, j)
            for a in range(n):
                landed = _col_half(out_refs[a], c, lead=(2 * px + py,))
                cp = pltpu.make_async_remote_copy(
                    src_ref=landed, dst_ref=landed, send_sem=send_sems.at[a, j], recv_sem=recv_sems.at[a, j],
                    device_id=(x, y, 1 - c), device_id_type=MESH)
                cp.start()
                sends.append(cp)
        for j in range(1, N_CHIPS):
            px, py = _peer_chip(x, y, j)
            for a in range(n):
                other = _col_half(out_refs[a], 1 - c, lead=(2 * px + py,))
                pltpu.make_async_remote_copy(
                    src_ref=other, dst_ref=other, send_sem=send_sems.at[a, j], recv_sem=recv_sems.at[a, j],
                    device_id=(x, y, 1 - c), device_id_type=MESH).wait_recv()
        for cp in sends:
            cp.wait_send()
        for cp in local:
            cp.wait()

    return pl.pallas_call(
        body, name="gather_finish",
        out_shape=[_out(l.shape, l.dtype) for l in lands],
        in_specs=[any_spec] * (2 * n), out_specs=[any_spec] * n,
        input_output_aliases={i: i for i in range(n)},
        scratch_shapes=[pltpu.SemaphoreType.DMA((n, N_CHIPS))] * 2 + [pltpu.SemaphoreType.DMA((n,))],
        compiler_params=_cp(vmem_mb=16),
    )(*lands, *shards)


def _halves_exchange(halves):
    n = len(halves)
    any_spec = _HBM

    def body(*refs):
        in_refs, out_refs = refs[:n], refs[n:2 * n]
        send_sems, recv_sems, local_sems = refs[2 * n:]
        x, y, c = _mesh_pos()
        copies, local = [], []
        for a in range(n):
            hc = halves[a].shape[1]
            mine = pl.ds(pl.multiple_of(c * hc, 128), hc)
            lc = pltpu.make_async_copy(in_refs[a], out_refs[a].at[:, mine], local_sems.at[a])
            lc.start()
            local.append(lc)
            cp = pltpu.make_async_remote_copy(
                src_ref=in_refs[a], dst_ref=out_refs[a].at[:, mine],
                send_sem=send_sems.at[a], recv_sem=recv_sems.at[a],
                device_id=(x, y, 1 - c), device_id_type=MESH)
            cp.start()
            copies.append(cp)
        for a in range(n):
            hc = halves[a].shape[1]
            theirs = pl.ds(pl.multiple_of((1 - c) * hc, 128), hc)
            pltpu.make_async_remote_copy(
                src_ref=in_refs[a], dst_ref=out_refs[a].at[:, theirs],
                send_sem=send_sems.at[a], recv_sem=recv_sems.at[a],
                device_id=(x, y, 1 - c), device_id_type=MESH).wait_recv()
        for cp in copies:
            cp.wait_send()
        for lc in local:
            lc.wait()

    return pl.pallas_call(
        body, name="halves_exchange",
        out_shape=[_out((h.shape[0], 2 * h.shape[1]), h.dtype) for h in halves],
        in_specs=[any_spec] * n, out_specs=[any_spec] * n,
        scratch_shapes=[pltpu.SemaphoreType.DMA((n,))] * 3,
        compiler_params=_cp(vmem_mb=16),
    )(*halves)


def _adam_math(w, g, m, v):
    m2 = ADAM_B1 * m + (1.0 - ADAM_B1) * g
    v2 = ADAM_B2 * v + (1.0 - ADAM_B2) * (g * g)
    m_hat = m2 / (1.0 - ADAM_B1 ** ADAM_STEP)
    v_hat = v2 / (1.0 - ADAM_B2 ** ADAM_STEP)
    delta = -ADAM_LR * (m_hat / (jnp.sqrt(v_hat) + ADAM_EPS) + ADAM_WD * w)
    return delta, m2, v2


def _adam_big(g_own, g_pair, w, m, v, pos):
    r, c = w.shape
    tc = c // 4

    def body(pos_ref, go_ref, gp_ref, w_ref, m_ref, v_ref, g_ref, d_ref, m2_ref, v2_ref):
        half = pl.program_id(0) // 2
        g = jnp.where(half == pos_ref[0], go_ref[...], gp_ref[0])
        d, m2, v2 = _adam_math(w_ref[...], g, m_ref[...], v_ref[...])
        g_ref[...] = g
        d_ref[...] = d
        m2_ref[...] = m2
        v2_ref[...] = v2

    spec = pl.BlockSpec((r, tc), lambda i, pos: (0, i))
    grid_spec = pltpu.PrefetchScalarGridSpec(
        num_scalar_prefetch=1, grid=(4,),
        in_specs=[pl.BlockSpec((r, tc), lambda i, pos: (0, i % 2)),
                  pl.BlockSpec((1, r, tc), lambda i, pos: (1 - pos[0], 0, i % 2)), spec, spec, spec],
        out_specs=[spec] * 4)
    sh = _out((r, c), F32)
    return pl.pallas_call(body, name="adam_big", grid_spec=grid_spec, out_shape=[sh] * 4,
                          compiler_params=_cp(("parallel",), 32))(pos, *_pin(g_own, g_pair, w, m, v))


def _adam_ada(c_act_t, dmod_cols, w, m, v):
    r, c = w.shape
    tc = 512

    def body(ct_ref, dm_ref, w_ref, m_ref, v_ref, g_ref, d_ref, m2_ref, v2_ref):
        g = _nn(ct_ref[...], dm_ref[...].astype(BF))
        d, m2, v2 = _adam_math(w_ref[...], g, m_ref[...], v_ref[...])
        g_ref[...] = g
        d_ref[...] = d
        m2_ref[...] = m2
        v2_ref[...] = v2

    spec = pl.BlockSpec((r, tc), lambda i: (0, i))
    sh = _out((r, c), F32)
    return pl.pallas_call(
        body, name="adam_ada", grid=(c // tc,),
        in_specs=[pl.BlockSpec(c_act_t.shape, lambda i: (0, 0)), pl.BlockSpec((dmod_cols.shape[0], tc), lambda i: (0, i)),
                  spec, spec, spec],
        out_specs=[spec] * 4, out_shape=[sh] * 4, compiler_params=_cp(("parallel",), 48))(*_pin(c_act_t, dmod_cols, w, m, v))


def _ada_mod(c_all, w_shard, b_shard, token):
    nb, d = c_all.shape
    cols = w_shard.shape[1]
    tc = 512

    def body(c_ref, w_ref, b_ref, tok_ref, mod_ref, act_ref):
        cv = c_ref[...]
        act = cv * _sigmoid(cv)
        act_ref[...] = act
        mod_ref[...] = _nn(act.astype(BF), w_ref[...].astype(BF)) + b_ref[...]

    return pl.pallas_call(
        body, name="ada_mod", grid=(cols // tc,),
        in_specs=[pl.BlockSpec((nb, d), lambda i: (0, 0)), pl.BlockSpec((d, tc), lambda i: (0, i)),
                  pl.BlockSpec((1, tc), lambda i: (0, i)), _token_spec()],
        out_specs=[pl.BlockSpec((nb, tc), lambda i: (0, i)), pl.BlockSpec((nb, d), lambda i: (0, 0))],
        out_shape=[_out((nb, cols), F32), _out((nb, d), F32)],
        compiler_params=_cp(("arbitrary",), 32))(*_pin(c_all, w_shard, b_shard, token))


SUB_ROWS = 256
ROW_TILE = 512


def _sub_rows(tm):
    return [slice(s, s + SUB_ROWS) for s in range(0, tm, SUB_ROWS)] if tm > SUB_ROWS else [slice(0, tm)]


_RESIDENT = pl.BlockSpec(memory_space=pltpu.VMEM)


def _token_spec():
    return pl.BlockSpec((8, 128), lambda *_: (0, 0))


def _in_proj(x, mod3, g_mix, w_in_t, seq, token):
    t, d = x.shape
    tm = min(ROW_TILE, seq)
    tps = seq // tm

    def body(x_ref, mod_ref, g_ref, w_ref, tok_ref, proj_ref, u1_ref):
        for rows in _sub_rows(tm):
            xv = x_ref[rows, :]
            r = lax.rsqrt(jnp.mean(xv * xv, -1, keepdims=True) + EPS)
            u = (xv * r * g_ref[...]) * (1.0 + mod_ref[0, 1:2, :]) + mod_ref[0, 0:1, :]
            ub = u.astype(BF)
            u1_ref[rows, :] = ub
            proj_ref[rows, :] = _nt(ub, w_ref[...])

    return pl.pallas_call(
        body, name="in_proj", grid=(t // tm,),
        in_specs=[pl.BlockSpec((tm, d), lambda i: (i, 0)), pl.BlockSpec((1, N_MOD, d), lambda i: (i // tps, 0, 0)),
                  pl.BlockSpec((1, d), lambda i: (0, 0)), _RESIDENT, _token_spec()],
        out_specs=[pl.BlockSpec((tm, IN_PAD), lambda i: (i, 0)), pl.BlockSpec((tm, d), lambda i: (i, 0))],
        out_shape=[_out((t, IN_PAD), F32), _out((t, d), BF)],
        compiler_params=_cp(("parallel",), 40))(*_pin(x, mod3, g_mix), w_in_t, *_pin(token))


def _pool_tile(seq):
    return min(512, seq)


def _pool_fwd(proj, w_pool, pool_scale, nb, seq):
    ts = _pool_tile(seq)
    nt = seq // ts

    def body(u_ref, halo_ref, wp_ref, ps_ref, yp_ref, p_ref):
        i = pl.program_id(1)
        halo = jnp.where(i == 0, 0.0, halo_ref[...])
        u = u_ref[...]
        ext = jnp.concatenate([halo, u], 0)
        tpos = i * ts + _iota((ts, 1), 0)
        for g, w in enumerate(POOL_WINDOWS):
            gs = slice(g * POOL_GROUP, (g + 1) * POOL_GROUP)
            s = ext[:, gs]
            sh = 1
            while sh < w:
                s = s + pltpu.roll(s, sh, 0)
                sh *= 2
            cnt = jnp.minimum(tpos + 1, w).astype(F32)
            pb = (s[HALO:] / cnt - u[:, gs]).astype(BF)
            p_ref[:, gs] = pb
            yp_ref[:, gs] = (_nn(pb, wp_ref[g].astype(BF)) * ps_ref[:, gs]).astype(BF)

    hb = ts // HALO
    return pl.pallas_call(
        body, name="pool_fwd", grid=(nb, nt),
        in_specs=[pl.BlockSpec((ts, POOL_WIDTH), lambda b, i: (b * nt + i, 0)),
                  pl.BlockSpec((HALO, POOL_WIDTH), lambda b, i: (jnp.maximum((b * nt + i) * hb - 1, 0), 0)),
                  pl.BlockSpec((4, POOL_GROUP, POOL_GROUP), lambda b, i: (0, 0, 0)),
                  pl.BlockSpec((1, POOL_WIDTH), lambda b, i: (0, 0))],
        out_specs=[pl.BlockSpec((ts, POOL_WIDTH), lambda b, i: (b * nt + i, 0))] * 2,
        out_shape=[_out((nb * seq, POOL_WIDTH), BF)] * 2,
        compiler_params=_cp(("parallel", "parallel"), 32))(*_pin(proj, proj, w_pool, pool_scale))


def _conv_pre(uxbc, halo, cw, cb, first):
    halo = jnp.where(first, 0.0, halo)
    ext = jnp.concatenate([halo, uxbc], 0)
    pre = cb + uxbc * cw[3:4]
    for k in (2, 1, 0):
        pre = pre + pltpu.roll(ext, 3 - k, 0)[CONV_HALO:] * cw[k:k + 1]
    return pre


def _chunk_terms(pre, udt, dtb, alog):
    sg = _sigmoid(pre)
    xbc = pre * sg
    dtp = udt[:, :SSD_HEADS] + dtb
    dt = jnp.maximum(dtp, 0.0) + jnp.log(1.0 + jnp.exp(-jnp.abs(dtp)))
    a = -jnp.exp(alog)
    da = dt * a
    tril = (_iota((CHUNK, CHUNK), 0) >= _iota((CHUNK, CHUNK), 1))
    acum = _exact_nn_left(tril.astype(BF), da)
    eye = (_iota((SSD_HEADS, SSD_HEADS), 0) == _iota((SSD_HEADS, SSD_HEADS), 1)).astype(BF)
    acum_t = _exact_nt_left(eye, acum)
    expand = _head_expand_matrix(SSD_HEADS, SSD_INNER)
    acum_e = _exact_nn(acum, expand)
    dt_e = _exact_nn(dt, expand)
    last_e = acum_e[CHUNK - 1:CHUNK]
    return dict(pre=pre, sg=sg, xbc=xbc, dtp=dtp, dt=dt, a=a, acum=acum, acum_t=acum_t, tril=tril,
                dt_e=dt_e, e_a=jnp.exp(acum_e), d_out=jnp.exp(last_e - acum_e), c_dec=jnp.exp(last_e))


def _head_decay(r, h):
    seg = r["acum"][:, h:h + 1] - r["acum_t"][h:h + 1, :]
    return jnp.where(r["tril"], jnp.exp(jnp.minimum(seg, 0.0)), 0.0)


SSD_SUB = 2
SSD_ROWS = SSD_SUB * CHUNK


def _ssd_specs(nb, seq, reverse):
    ns = seq // SSD_ROWS
    per = seq // CONV_HALO

    def cidx(c):
        return (ns - 1 - c) if reverse else c

    def row(b, c):
        return b * ns + cidx(c)

    specs = [
        pl.BlockSpec((SSD_ROWS, CONV_CH), lambda b, c: (row(b, c), 1)),
        pl.BlockSpec((CONV_HALO, CONV_CH),
                     lambda b, c: (jnp.maximum(b * per + cidx(c) * (SSD_ROWS // CONV_HALO) - 1, 0), 1)),
        pl.BlockSpec((SSD_ROWS, GROUP_W), lambda b, c: (row(b, c), 1)),
        pl.BlockSpec((SSD_ROWS, GROUP_W), lambda b, c: (row(b, c), 2)),
        pl.BlockSpec((SSD_ROWS, 128), lambda b, c: (row(b, c), OFF_DT // 128)),
    ]
    return specs, row, cidx, ns


def _const_spec(shape):
    return pl.BlockSpec(shape, lambda b, c: (0,) * len(shape))


def _ssd_fwd(proj, conv_w, conv_b, dt_bias, a_log, dskip_e, g_ssd, nb, seq):
    specs, row, cidx, ns = _ssd_specs(nb, seq, reverse=False)

    def body(uxbc_ref, halo_ref, z0_ref, z1_ref, udt_ref, cw_ref, cb_ref, dtb_ref, alog_ref, dsk_ref, gs_ref,
             yssd_ref, yssm_ref, hprev_ref, pre_ref, h_ref, yd_ref):
        c = pl.program_id(1)

        @pl.when(c == 0)
        def _():
            h_ref[...] = jnp.zeros_like(h_ref)

        for sub in range(SSD_SUB):
            rows = slice(sub * CHUNK, (sub + 1) * CHUNK)
            if sub == 0:
                halo, first = halo_ref[...], c == 0
            else:
                halo, first = uxbc_ref[sub * CHUNK - CONV_HALO:sub * CHUNK, :], False
            pre = _conv_pre(uxbc_ref[rows, :], halo, cw_ref[...], cb_ref[...], first)
            pre_ref[rows, :] = pre
            r = _chunk_terms(pre, udt_ref[rows, :], dtb_ref[...], alog_ref[...])
            xbc = r["xbc"]
            xs = xbc[:, :SSD_INNER]
            xdt = xs * r["dt_e"]
            xdt_b = xdt.astype(BF)
            xdo_b = (xdt * r["d_out"]).astype(BF)
            hprev_ref[0, sub] = h_ref[...]
            for g in range(2):
                gs = slice(g * GROUP_W, (g + 1) * GROUP_W)
                bg = xbc[:, SSD_INNER + g * SSD_STATE:SSD_INNER + (g + 1) * SSD_STATE].astype(BF)
                cg = xbc[:, SSD_INNER + (2 + g) * SSD_STATE:SSD_INNER + (3 + g) * SSD_STATE].astype(BF)
                scores = _nt(cg, bg)
                hg = h_ref[g]
                y_off = _nn(cg, hg.astype(BF)) * r["e_a"][:, gs]
                for hh in range(8):
                    h = g * 8 + hh
                    hs = slice(h * SSD_HEAD_DIM, (h + 1) * SSD_HEAD_DIM)
                    m = (scores * _head_decay(r, h)).astype(BF)
                    yd_ref[sub, :, hs] = _nn(m, xdt_b[:, hs])
                h_ref[g] = hg * r["c_dec"][:, gs] + _tn(bg, xdo_b[:, gs])
                y = yd_ref[sub, :, gs] + y_off + dsk_ref[:, gs] * xs[:, gs]
                yssm_ref[rows, gs] = y
                zg = (z0_ref if g == 0 else z1_ref)[rows, :]
                yg = y * (zg * _sigmoid(zg))
                rg = lax.rsqrt(jnp.mean(yg * yg, -1, keepdims=True) + EPS)
                yssd_ref[rows, gs] = (yg * rg * gs_ref[:, gs]).astype(BF)

    t = nb * seq
    return pl.pallas_call(
        body, name="ssd_fwd", grid=(nb, ns),
        in_specs=specs + [_const_spec((4, CONV_CH)), _const_spec((1, CONV_CH)), _const_spec((1, SSD_HEADS)),
                          _const_spec((1, SSD_HEADS)), _const_spec((1, SSD_INNER)), _const_spec((1, SSD_INNER))],
        out_specs=[pl.BlockSpec((SSD_ROWS, SSD_INNER), lambda b, c: (row(b, c), 0)),
                   pl.BlockSpec((SSD_ROWS, SSD_INNER), lambda b, c: (row(b, c), 0)),
                   pl.BlockSpec((1, SSD_SUB, 2, SSD_STATE, GROUP_W), lambda b, c: (b, c, 0, 0, 0)),
                   pl.BlockSpec((SSD_ROWS, CONV_CH), lambda b, c: (row(b, c), 0))],
        out_shape=[_out((t, SSD_INNER), BF), _out((t, SSD_INNER), F32),
                   _out((nb, seq // CHUNK, 2, SSD_STATE, GROUP_W), F32), _out((t, CONV_CH), F32)],
        scratch_shapes=[pltpu.VMEM((2, SSD_STATE, GROUP_W), F32), pltpu.VMEM((SSD_SUB, CHUNK, SSD_INNER), F32)],
        compiler_params=_cp(("arbitrary", "arbitrary"), 48),
    )(*_pin(proj, proj, proj, proj, proj, conv_w, conv_b, dt_bias, a_log, dskip_e, g_ssd))


def _out_proj(y_pool, y_ssd, w_out, x, mod3, g_mlp, seq):
    t, d = x.shape
    tm = 512
    tps = seq // tm if seq >= tm else 1
    tm = min(tm, seq)

    def body(yp_ref, ys_ref, w_ref, x_ref, mod_ref, g_ref, h1_ref, o_ref, u2_ref):
        o = _nn(yp_ref[...], w_ref[0:POOL_WIDTH, :]) + _nn(ys_ref[...], w_ref[POOL_WIDTH:, :])
        o_ref[...] = o.astype(BF)
        h1 = x_ref[...] + mod_ref[0, 2:3, :] * o
        h1_ref[...] = h1
        r = lax.rsqrt(jnp.mean(h1 * h1, -1, keepdims=True) + EPS)
        u2_ref[...] = ((h1 * r * g_ref[...]) * (1.0 + mod_ref[0, 4:5, :]) + mod_ref[0, 3:4, :]).astype(BF)

    row = lambda i: (i, 0)
    return pl.pallas_call(
        body, name="out_proj", grid=(t // tm,),
        in_specs=[pl.BlockSpec((tm, POOL_WIDTH), row), pl.BlockSpec((tm, SSD_INNER), row),
                  _RESIDENT, pl.BlockSpec((tm, d), row),
                  pl.BlockSpec((1, N_MOD, d), lambda i: (i // tps, 0, 0)), pl.BlockSpec((1, d), lambda i: (0, 0))],
        out_specs=[pl.BlockSpec((tm, d), row)] * 3,
        out_shape=[_out((t, d), F32), _out((t, d), BF), _out((t, d), BF)],
        compiler_params=_cp(("parallel",), 48))(*_pin(y_pool, y_ssd), w_out, *_pin(x, mod3, g_mlp))


def _mlp_up(u2, w_up4):
    t, d = u2.shape
    tm = min(1024, t)
    nk, _, cols = w_up4.shape

    def body(u_ref, w_ref, a_ref):
        a_ref[...] = _nn(u_ref[...], w_ref[pl.program_id(1)]).astype(BF)

    return pl.pallas_call(
        body, name="mlp_up", grid=(t // tm, nk),
        in_specs=[pl.BlockSpec((tm, d), lambda i, k: (i, 0)), _RESIDENT],
        out_specs=pl.BlockSpec((tm, cols), lambda i, k: (i, k)),
        out_shape=_out((t, nk * cols), BF),
        compiler_params=_cp(("parallel", "parallel"), 32))(*_pin(u2), w_up4)


def _mlp_down_loss(a_up, w_down, h1, mod3, g_final, target, seq):
    t, d = h1.shape
    nb = t // seq
    tm = min(ROW_TILE, seq)
    tps = seq // tm

    def body(a_ref, w_ref, h1_ref, mod_ref, g_ref, tg_ref, ddn_ref, dh2_ref, sq_ref, gg_ref, dgf_ref):
        i = pl.program_id(0)

        @pl.when(i == 0)
        def _():
            sq_ref[...] = jnp.zeros_like(sq_ref)
            gg_ref[...] = jnp.zeros_like(gg_ref)

        @pl.when(i % tps == 0)
        def _():
            dgf_ref[...] = jnp.zeros_like(dgf_ref)

        gate = mod_ref[0, 5:6, :]
        sq = gg = dgf = 0.0
        for rows in _sub_rows(tm):
            f = jnp.square(jnp.maximum(a_ref[rows, :], 0))
            dn = _nn(f, w_ref[...])
            h2 = h1_ref[rows, :] + gate * dn
            r = lax.rsqrt(jnp.mean(h2 * h2, -1, keepdims=True) + EPS)
            hh = h2 * r
            err = hh * g_ref[...] - tg_ref[rows, :]
            dy = err * (1.0 / d)
            dhat = dy * g_ref[...]
            dh2 = r * (dhat - hh * jnp.mean(dhat * hh, -1, keepdims=True))
            dh2_ref[rows, :] = dh2
            ddn_ref[rows, :] = (dh2 * gate).astype(BF)
            sq = sq + jnp.sum(err * err, 0, keepdims=True)
            gg = gg + jnp.sum(dy * hh, 0, keepdims=True)
            dgf = dgf + jnp.sum(dh2 * dn, 0, keepdims=True)
        sq_ref[...] += sq
        gg_ref[...] += gg
        dgf_ref[0] += dgf

    row = lambda i: (i, 0)
    vec = pl.BlockSpec((1, d), lambda i: (0, 0))
    return pl.pallas_call(
        body, name="mlp_down_loss", grid=(t // tm,),
        in_specs=[pl.BlockSpec((tm, D_FF), row), _RESIDENT, pl.BlockSpec((tm, d), row),
                  pl.BlockSpec((1, N_MOD, d), lambda i: (i // tps, 0, 0)), vec, pl.BlockSpec((tm, d), row)],
        out_specs=[pl.BlockSpec((tm, d), row), pl.BlockSpec((tm, d), row), vec, vec,
                   pl.BlockSpec((1, 1, d), lambda i: (i // tps, 0, 0))],
        out_shape=[_out((t, d), BF), _out((t, d), F32), _out((1, d), F32),
                   _out((1, d), F32), _out((nb, 1, d), F32)],
        compiler_params=_cp(("arbitrary",), 44))(*_pin(a_up), w_down, *_pin(h1, mod3, g_final, target))


def _tn_matmul(a, b, tk, tn, name, square_relu=False, out3=False):
    t, kdim = a.shape
    ndim = b.shape[1]

    def body(a_ref, b_ref, o_ref):
        av = a_ref[...]
        if square_relu:
            av = jnp.square(jnp.maximum(av, 0))
        res = _tn(av, b_ref[...]).astype(BF)
        if out3:
            o_ref[0] = res
        else:
            o_ref[...] = res

    if out3:
        out_spec = pl.BlockSpec((1, tk, tn), lambda j, i: (j, i, 0))
        out_shape = _out((ndim // tn, kdim, tn), BF)
    else:
        out_spec = pl.BlockSpec((tk, tn), lambda j, i: (i, j))
        out_shape = _out((kdim, ndim), BF)
    return pl.pallas_call(
        body, name=name, grid=(ndim // tn, kdim // tk),
        in_specs=[pl.BlockSpec((t, tk), lambda j, i: (0, i)), pl.BlockSpec((t, tn), lambda j, i: (0, j))],
        out_specs=out_spec, out_shape=out_shape,
        compiler_params=_cp(("parallel", "parallel"), 56))(*_pin(a, b))


def _mlp_down_bwd(d_dn, w_down4, a_up, token):
    t, d = d_dn.shape
    tm = min(1024, t)
    nk, rows, _ = w_down4.shape

    def body(g_ref, w_ref, a_ref, tok_ref, o_ref):
        df = _nt(g_ref[...], w_ref[pl.program_id(1)])
        o_ref[...] = (df * (2.0 * jnp.maximum(a_ref[...], 0).astype(F32))).astype(BF)

    return pl.pallas_call(
        body, name="mlp_down_bwd", grid=(t // tm, nk),
        in_specs=[pl.BlockSpec((tm, d), lambda i, k: (i, 0)), _RESIDENT,
                  pl.BlockSpec((tm, rows), lambda i, k: (i, k)), _token_spec()],
        out_specs=pl.BlockSpec((tm, rows), lambda i, k: (i, k)),
        out_shape=_out((t, nk * rows), BF),
        compiler_params=_cp(("parallel", "parallel"), 32))(*_pin(d_dn), w_down4, *_pin(a_up, token))


def _mlp_up_bwd(d_a, w_up4, h1, dh2, o, mod3, g_mlp, seq, token):
    t, d = h1.shape
    nb = t // seq
    tm = min(ROW_TILE, seq)
    tps = seq // tm
    nk = w_up4.shape[0]
    cols = w_up4.shape[2]

    def body(da_ref, w_ref, h1_ref, dh2_ref, o_ref, mod_ref, g_ref, tok_ref, dh1_ref, do_ref, acc_ref, gg_ref):
        i = pl.program_id(0)

        @pl.when(i == 0)
        def _():
            gg_ref[...] = jnp.zeros_like(gg_ref)

        @pl.when(i % tps == 0)
        def _():
            acc_ref[...] = jnp.zeros_like(acc_ref)

        gg = a_shift = a_scale = a_gate = 0.0
        for rows in _sub_rows(tm):
            du = _nt(da_ref[rows, 0:cols], w_ref[0])
            for k in range(1, nk):
                du = du + _nt(da_ref[rows, k * cols:(k + 1) * cols], w_ref[k])
            h1 = h1_ref[rows, :]
            r = lax.rsqrt(jnp.mean(h1 * h1, -1, keepdims=True) + EPS)
            hh = h1 * r
            n2 = hh * g_ref[...]
            dn2 = du * (1.0 + mod_ref[0, 4:5, :])
            dhat = dn2 * g_ref[...]
            dh1 = dh2_ref[rows, :] + r * (dhat - hh * jnp.mean(dhat * hh, -1, keepdims=True))
            dh1_ref[rows, :] = dh1
            do_ref[rows, :] = (dh1 * mod_ref[0, 2:3, :]).astype(BF)
            gg = gg + jnp.sum(dn2 * hh, 0, keepdims=True)
            a_shift = a_shift + jnp.sum(du, 0, keepdims=True)
            a_scale = a_scale + jnp.sum(du * n2, 0, keepdims=True)
            a_gate = a_gate + jnp.sum(dh1 * o_ref[rows, :].astype(F32), 0, keepdims=True)
        gg_ref[...] += gg
        acc_ref[0, 0:1, :] += a_shift
        acc_ref[0, 1:2, :] += a_scale
        acc_ref[0, 2:3, :] += a_gate

    row = lambda i: (i, 0)
    vec = pl.BlockSpec((1, d), lambda i: (0, 0))
    return pl.pallas_call(
        body, name="mlp_up_bwd", grid=(t // tm,),
        in_specs=[pl.BlockSpec((tm, D_FF), row), _RESIDENT, pl.BlockSpec((tm, d), row),
                  pl.BlockSpec((tm, d), row), pl.BlockSpec((tm, d), row),
                  pl.BlockSpec((1, N_MOD, d), lambda i: (i // tps, 0, 0)), vec, _token_spec()],
        out_specs=[pl.BlockSpec((tm, d), row), pl.BlockSpec((tm, d), row),
                   pl.BlockSpec((1, 8, d), lambda i: (i // tps, 0, 0)), vec],
        out_shape=[_out((t, d), F32), _out((t, d), BF),
                   _out((nb, 8, d), F32), _out((1, d), F32)],
        compiler_params=_cp(("arbitrary",), 44))(*_pin(d_a), w_up4, *_pin(h1, dh2, o, mod3, g_mlp, token))


def _out_proj_bwd(d_o, w_out, token):
    t, d = d_o.shape
    tm = min(512, t)

    def body(g_ref, w_ref, tok_ref, dp_ref, ds_ref):
        gv = g_ref[...]
        dp_ref[...] = _nt(gv, w_ref[0:POOL_WIDTH, :])
        ds_ref[...] = _nt(gv, w_ref[POOL_WIDTH:, :])

    row = lambda i: (i, 0)
    return pl.pallas_call(
        body, name="out_proj_bwd", grid=(t // tm,),
        in_specs=[pl.BlockSpec((tm, d), row), _RESIDENT, _token_spec()],
        out_specs=[pl.BlockSpec((tm, POOL_WIDTH), row), pl.BlockSpec((tm, SSD_INNER), row)],
        out_shape=[_out((t, POOL_WIDTH), F32), _out((t, SSD_INNER), F32)],
        compiler_params=_cp(("parallel",), 32))(*_pin(d_o), w_out, *_pin(token))


def _pool_bwd(d_ypool, p, w_pool, pool_scale, nb, seq):
    ts = _pool_tile(seq)
    nt = seq // ts
    hb = ts // HALO
    last_block = nb * seq // HALO - 1

    def body(dy_ref, halo_ref, p_ref, wp_ref, ps_ref, du_ref, gw_ref, gs_ref):
        b = pl.program_id(0)
        i = pl.program_id(1)

        @pl.when((b == 0) & (i == 0))
        def _():
            gw_ref[...] = jnp.zeros_like(gw_ref)
            gs_ref[...] = jnp.zeros_like(gs_ref)

        halo = jnp.where(i == nt - 1, 0.0, halo_ref[...])
        dy = dy_ref[...]
        ext = jnp.concatenate([dy, halo], 0)
        tpos = i * ts + _iota((ts + HALO, 1), 0)
        n_ext = ts + HALO
        for g, w in enumerate(POOL_WINDOWS):
            gs = slice(g * POOL_GROUP, (g + 1) * POOL_GROUP)
            wg = wp_ref[g].astype(BF)
            pg = p_ref[:, gs]
            pw = _nn(pg, wg)
            gs_ref[:, gs] += jnp.sum(dy[:, gs] * pw, 0, keepdims=True)
            dpw = (ext[:, gs] * ps_ref[:, gs]).astype(BF)
            gw_ref[g] += _tn(pg, dpw[:ts])
            dp = _nt(dpw, wg)
            cnt = jnp.minimum(tpos + 1, w).astype(F32)
            s = dp / cnt
            sh = 1
            while sh < w:
                s = s + pltpu.roll(s, n_ext - sh, 0)
                sh *= 2
            du_ref[:, gs] = (s[:ts] - dp[:ts]).astype(BF)

    return pl.pallas_call(
        body, name="pool_bwd", grid=(nb, nt),
        in_specs=[pl.BlockSpec((ts, POOL_WIDTH), lambda b, i: (b * nt + i, 0)),
                  pl.BlockSpec((HALO, POOL_WIDTH), lambda b, i: (jnp.minimum((b * nt + i + 1) * hb, last_block), 0)),
                  pl.BlockSpec((ts, POOL_WIDTH), lambda b, i: (b * nt + i, 0)),
                  pl.BlockSpec((4, POOL_GROUP, POOL_GROUP), lambda b, i: (0, 0, 0)),
                  pl.BlockSpec((1, POOL_WIDTH), lambda b, i: (0, 0))],
        out_specs=[pl.BlockSpec((ts, POOL_WIDTH), lambda b, i: (b * nt + i, 0)),
                   pl.BlockSpec((4, POOL_GROUP, POOL_GROUP), lambda b, i: (0, 0, 0)),
                   pl.BlockSpec((1, POOL_WIDTH), lambda b, i: (0, 0))],
        out_shape=[_out((nb * seq, POOL_WIDTH), BF), _out((4, POOL_GROUP, POOL_GROUP), F32),
                   _out((1, POOL_WIDTH), F32)],
        compiler_params=_cp(("arbitrary", "arbitrary"), 32))(*_pin(d_ypool, d_ypool, p, w_pool, pool_scale))


def _ssd_bwd(proj, pre, d_yssd, yssm, h_prev, dt_bias, a_log, dskip_e, g_ssd, nb, seq):
    specs, row, cidx, ns = _ssd_specs(nb, seq, reverse=True)
    specs = specs[2:]

    def body(z0_ref, z1_ref, udt_ref, pre_ref, dys_ref, yssm_ref, hprev_ref,
             dtb_ref, alog_ref, dsk_ref, gs_ref,
             dz_ref, dpre_ref, dudt_ref, ggs_ref, gdsk_ref, ga_ref, gdtb_ref,
             g_ref, dxdt_ref, dyv_ref):
        b = pl.program_id(0)
        c = pl.program_id(1)

        @pl.when(c == 0)
        def _():
            g_ref[...] = jnp.zeros_like(g_ref)

        @pl.when((b == 0) & (c == 0))
        def _():
            ggs_ref[...] = jnp.zeros_like(ggs_ref)
            gdsk_ref[...] = jnp.zeros_like(gdsk_ref)
            ga_ref[...] = jnp.zeros_like(ga_ref)
            gdtb_ref[...] = jnp.zeros_like(gdtb_ref)

        for sub in reversed(range(SSD_SUB)):
            chunk(sub, z0_ref, z1_ref, udt_ref, pre_ref, dys_ref, yssm_ref, hprev_ref, dtb_ref, alog_ref, dsk_ref, gs_ref,
                  dz_ref, dpre_ref, dudt_ref, ggs_ref, gdsk_ref, ga_ref, gdtb_ref, g_ref, dxdt_ref.at[sub], dyv_ref.at[sub])

    def chunk(sub, z0_ref, z1_ref, udt_ref, pre_ref, dys_ref, yssm_ref, hprev_ref,
              dtb_ref, alog_ref, dsk_ref, gs_ref,
              dz_ref, dpre_ref, dudt_ref, ggs_ref, gdsk_ref, ga_ref, gdtb_ref,
              g_ref, dxdt_ref, dyv_ref):
        rows = slice(sub * CHUNK, (sub + 1) * CHUNK)
        r = _chunk_terms(pre_ref[rows, :], udt_ref[rows, :], dtb_ref[...], alog_ref[...])
        xbc = r["xbc"]
        xs = xbc[:, :SSD_INNER]
        dt_e = r["dt_e"]
        xdt = xs * dt_e
        xdt_b = xdt.astype(BF)
        reduce_m = _head_reduce_matrix(GROUP_W, 8)

        def head_sums(v):
            return _nn(v.astype(BF), reduce_m)

        onehot16 = lambda h: (_iota((1, SSD_HEADS), 1) == h).astype(F32)
        onecol16 = lambda h: (_iota((SSD_HEADS, 1), 0) == h).astype(F32)

        d_acum = jnp.zeros((CHUNK, SSD_HEADS), F32)
        d_acum_t = jnp.zeros((SSD_HEADS, CHUNK), F32)
        d_alast = jnp.zeros((1, SSD_HEADS), F32)
        place8 = lambda g: (_iota((8, SSD_HEADS), 1) == _iota((8, SSD_HEADS), 0) + 8 * g).astype(BF)
        d_b, d_c = [], []
        for g in range(2):
            gs = slice(g * GROUP_W, (g + 1) * GROUP_W)
            zg = (z0_ref if g == 0 else z1_ref)[rows, :]
            sz = _sigmoid(zg)
            silu_z = zg * sz
            ys = yssm_ref[rows, gs]
            yg = ys * silu_z
            rg = lax.rsqrt(jnp.mean(yg * yg, -1, keepdims=True) + EPS)
            yh = yg * rg
            dys = dys_ref[rows, gs]
            ggs_ref[:, gs] += jnp.sum(dys * yh, 0, keepdims=True)
            dyh = dys * gs_ref[:, gs]
            dyg = rg * (dyh - yh * jnp.mean(dyh * yh, -1, keepdims=True))
            dy = dyg * silu_z
            dz_ref[rows, gs] = (dyg * ys * (sz * (1.0 + zg * (1.0 - sz)))).astype(BF)
            gdsk_ref[:, gs] += jnp.sum(dy * xs[:, gs], 0, keepdims=True)
            dyv_ref[:, gs] = dy
            dy_b = dy.astype(BF)

            bg = xbc[:, SSD_INNER + g * SSD_STATE:SSD_INNER + (g + 1) * SSD_STATE].astype(BF)
            cg = xbc[:, SSD_INNER + (2 + g) * SSD_STATE:SSD_INNER + (3 + g) * SSD_STATE].astype(BF)
            scores = _nt(cg, bg)
            hg = hprev_ref[0, sub, g]
            hg_b = hg.astype(BF)
            gg = g_ref[g]
            gg_b = gg.astype(BF)
            e_a = r["e_a"][:, gs]
            d_out = r["d_out"][:, gs]
            c_dec = r["c_dec"][:, gs]
            zc = _nn(cg, hg_b)
            wv = e_a * dy
            wv_b = wv.astype(BF)
            da_g = head_sums(wv * zc)
            dcg = _nt(wv_b, hg_b)
            d_hprev = _tn(cg, wv_b)
            vg = _nn(bg, gg_b)
            dxdt_g = d_out * vg
            dd_out = head_sums(xdt[:, gs] * vg)
            dbg = _nt((xdt[:, gs] * d_out).astype(BF), gg_b)
            dcd = _exact_nn(jnp.sum(gg * hg, 0, keepdims=True), reduce_m)
            d_out8 = jnp.exp(r["acum"][CHUNK - 1:CHUNK, 8 * g:8 * g + 8] - r["acum"][:, 8 * g:8 * g + 8])
            c_dec8 = jnp.exp(r["acum"][CHUNK - 1:CHUNK, 8 * g:8 * g + 8])
            t8 = dd_out * d_out8
            d_alast = d_alast + _exact_nn(jnp.sum(t8, 0, keepdims=True) + dcd * c_dec8, place8(g))
            d_acum = d_acum + _exact_nn(da_g - t8, place8(g))
            dsc = jnp.zeros((CHUNK, CHUNK), F32)
            for hh in range(8):
                h = g * 8 + hh
                hs = slice(h * SSD_HEAD_DIM, (h + 1) * SSD_HEAD_DIM)
                lam = _head_decay(r, h)
                m = scores * lam
                dyh_b = dy_b[:, hh * SSD_HEAD_DIM:(hh + 1) * SSD_HEAD_DIM]
                dm = _nt(dyh_b, xdt_b[:, hs])
                tm_ = dm * m
                d_acum = d_acum + jnp.sum(tm_, 1, keepdims=True) * onehot16(h)
                d_acum_t = d_acum_t + onecol16(h) * jnp.sum(tm_, 0, keepdims=True)
                dsc = dsc + dm * lam
                dxdt_ref[:, hs] = _tn(m.astype(BF), dyh_b) + dxdt_g[:, hh * SSD_HEAD_DIM:(hh + 1) * SSD_HEAD_DIM]
            dsc_b = dsc.astype(BF)
            d_c.append(dcg + _nn(dsc_b, bg))
            d_b.append(dbg + _tn(dsc_b, cg))
            g_ref[g] = d_hprev + c_dec * gg

        eye = (_iota((CHUNK, CHUNK), 0) == _iota((CHUNK, CHUNK), 1)).astype(BF)
        d_acum = d_acum - _exact_nt_left(eye, d_acum_t)
        is_last = (_iota((CHUNK, 1), 0) == CHUNK - 1).astype(F32)
        d_acum = d_acum + is_last * d_alast
        triu = (_iota((CHUNK, CHUNK), 0) <= _iota((CHUNK, CHUNK), 1)).astype(BF)
        d_da = _exact_nn_left(triu, d_acum)
        dt = r["dt"]
        ga_ref[...] += jnp.sum(d_da * dt, 0, keepdims=True)
        dxdt = dxdt_ref[...]
        reduce16 = _head_reduce_matrix(SSD_INNER, SSD_HEADS)
        d_dt = d_da * r["a"] + _nn((dxdt * xs).astype(BF), reduce16)
        d_udt = d_dt * _sigmoid(r["dtp"])
        gdtb_ref[...] += jnp.sum(d_udt, 0, keepdims=True)
        dudt_ref[rows, :] = jnp.zeros((CHUNK, dudt_ref.shape[1]), BF)
        dudt_ref[rows, 0:SSD_HEADS] = d_udt.astype(BF)
        pre, sg = r["pre"], r["sg"]
        dsilu = sg * (1.0 + pre * (1.0 - sg))
        dpre_ref[rows, 0:SSD_INNER] = (dsk_ref[...] * dyv_ref[...] + dxdt * dt_e) * dsilu[:, 0:SSD_INNER]
        for g in range(2):
            bs = slice(SSD_INNER + g * SSD_STATE, SSD_INNER + (g + 1) * SSD_STATE)
            cs = slice(SSD_INNER + (2 + g) * SSD_STATE, SSD_INNER + (3 + g) * SSD_STATE)
            dpre_ref[rows, bs] = d_b[g] * dsilu[:, bs]
            dpre_ref[rows, cs] = d_c[g] * dsilu[:, cs]

    t = nb * seq
    vec = _const_spec((1, SSD_INNER))
    small = _const_spec((1, SSD_HEADS))
    return pl.pallas_call(
        body, name="ssd_bwd", grid=(nb, ns),
        in_specs=specs + [pl.BlockSpec((SSD_ROWS, CONV_CH), lambda b, c: (row(b, c), 0)),
                          pl.BlockSpec((SSD_ROWS, SSD_INNER), lambda b, c: (row(b, c), 0)),
                          pl.BlockSpec((SSD_ROWS, SSD_INNER), lambda b, c: (row(b, c), 0)),
                          pl.BlockSpec((1, SSD_SUB, 2, SSD_STATE, GROUP_W), lambda b, c: (b, cidx(c), 0, 0, 0)),
                          small, small, vec, vec],
        out_specs=[pl.BlockSpec((SSD_ROWS, SSD_INNER), lambda b, c: (row(b, c), 0)),
                   pl.BlockSpec((SSD_ROWS, CONV_CH), lambda b, c: (row(b, c), 0)),
                   pl.BlockSpec((SSD_ROWS, 128), lambda b, c: (row(b, c), 0)),
                   vec, vec, small, small],
        out_shape=[_out((t, SSD_INNER), BF), _out((t, CONV_CH), F32),
                   _out((t, 128), BF), _out((1, SSD_INNER), F32),
                   _out((1, SSD_INNER), F32), _out((1, SSD_HEADS), F32),
                   _out((1, SSD_HEADS), F32)],
        scratch_shapes=[pltpu.VMEM((2, SSD_STATE, GROUP_W), F32), pltpu.VMEM((SSD_SUB, CHUNK, SSD_INNER), F32),
                        pltpu.VMEM((SSD_SUB, CHUNK, SSD_INNER), F32)],
        compiler_params=_cp(("arbitrary", "arbitrary"), 48),
    )(*_pin(proj, proj, proj, pre, d_yssd, yssm, h_prev, dt_bias, a_log, dskip_e, g_ssd))


def _grad_w_in_t(d_upool, d_z, d_uxbc, d_udt, u1):
    t, d = u1.shape
    tk = 512
    n_z, n_x = SSD_INNER // tk, CONV_CH // tk

    def body(p_ref, z_ref, x_ref, dt_ref, u_ref, o_ref):
        i = pl.program_id(0)

        @pl.when(i == 0)
        def _():
            o_ref[...] = _tn(p_ref[...], u_ref[...]).astype(BF)

        @pl.when((i >= 1) & (i < 1 + n_z))
        def _():
            o_ref[...] = _tn(z_ref[...], u_ref[...]).astype(BF)

        @pl.when((i >= 1 + n_z) & (i < 1 + n_z + n_x))
        def _():
            o_ref[...] = _tn(x_ref[...], u_ref[...]).astype(BF)

        @pl.when(i == 1 + n_z + n_x)
        def _():
            o_ref[0:128, :] = _tn(dt_ref[...], u_ref[...]).astype(BF)

    return pl.pallas_call(
        body, name="grad_w_in", grid=(2 + n_z + n_x,),
        in_specs=[pl.BlockSpec((t, tk), lambda i: (0, 0)),
                  pl.BlockSpec((t, tk), lambda i: (0, jnp.clip(i - 1, 0, n_z - 1))),
                  pl.BlockSpec((t, tk), lambda i: (0, jnp.clip(i - 1 - n_z, 0, n_x - 1))),
                  pl.BlockSpec((t, 128), lambda i: (0, 0)), pl.BlockSpec((t, d), lambda i: (0, 0))],
        out_specs=pl.BlockSpec((tk, d), lambda i: (i, 0)),
        out_shape=_out((IN_PAD, d), BF),
        compiler_params=_cp(("parallel",), 56))(*_pin(d_upool, d_z, d_uxbc, d_udt, u1))


def _conv_bwd(d_pre, proj, conv_w, nb, seq):
    ts = min(256, seq)
    nt = seq // ts
    hb = ts // CONV_HALO
    last_block = nb * seq // CONV_HALO - 1
    n_ext = CHUNK + CONV_HALO

    def body(dp_ref, dnext_ref, u_ref, cw_ref, du_ref, gw_ref, gb_ref):
        b = pl.program_id(0)
        i = pl.program_id(1)

        @pl.when((b == 0) & (i == 0))
        def _():
            gw_ref[...] = jnp.zeros_like(gw_ref)
            gb_ref[...] = jnp.zeros_like(gb_ref)

        for c0 in range(0, CONV_CH, 128):
            cs = slice(c0, c0 + 128)
            cw = cw_ref[:, cs]
            gw = [0.0] * 4
            gb = 0.0
            for r0 in range(0, ts, CHUNK):
                dp = dp_ref[r0:r0 + CHUNK, cs]
                u = u_ref[r0:r0 + CHUNK, cs]
                if r0 + CHUNK < ts:
                    below = dp_ref[r0 + CHUNK:r0 + CHUNK + CONV_HALO, cs]
                else:
                    below = jnp.where(i == nt - 1, 0.0, dnext_ref[:, cs])
                ext_d = jnp.concatenate([dp, below], 0)
                du = dp * cw[3:4]
                gw[3] = gw[3] + jnp.sum(dp * u, 0, keepdims=True)
                for k in (2, 1, 0):
                    shifted = pltpu.roll(ext_d, n_ext - (3 - k), 0)[:CHUNK]
                    du = du + shifted * cw[k:k + 1]
                    gw[k] = gw[k] + jnp.sum(shifted * u, 0, keepdims=True)
                gb = gb + jnp.sum(dp, 0, keepdims=True)
                du_ref[r0:r0 + CHUNK, cs] = du.astype(BF)
            for k in range(4):
                gw_ref[k:k + 1, cs] += gw[k]
            gb_ref[:, cs] += gb

    return pl.pallas_call(
        body, name="conv_bwd", grid=(nb, nt),
        in_specs=[pl.BlockSpec((ts, CONV_CH), lambda b, i: (b * nt + i, 0)),
                  pl.BlockSpec((CONV_HALO, CONV_CH), lambda b, i: (jnp.minimum((b * nt + i + 1) * hb, last_block), 0)),
                  pl.BlockSpec((ts, CONV_CH), lambda b, i: (b * nt + i, 1)),
                  pl.BlockSpec((4, CONV_CH), lambda b, i: (0, 0))],
        out_specs=[pl.BlockSpec((ts, CONV_CH), lambda b, i: (b * nt + i, 0)),
                   pl.BlockSpec((8, CONV_CH), lambda b, i: (0, 0)), pl.BlockSpec((1, CONV_CH), lambda b, i: (0, 0))],
        out_shape=[_out((nb * seq, CONV_CH), BF), _out((8, CONV_CH), F32),
                   _out((1, CONV_CH), F32)],
        compiler_params=_cp(("arbitrary", "arbitrary"), 48))(*_pin(d_pre, d_pre, proj, conv_w))


def _in_proj_bwd(d_parts, w_in_t, x, dh1, mod3, g_mix, seq, token):
    t, d = x.shape
    nb = t // seq
    tm = min(ROW_TILE, seq)
    tps = seq // tm

    widths = [p.shape[1] for p in d_parts]

    def body(d0_ref, d1_ref, d2_ref, d3_ref, w_ref, x_ref, dh1_ref, mod_ref, g_ref, tok_ref, gx_ref, acc_ref, gg_ref):
        i = pl.program_id(0)

        @pl.when(i == 0)
        def _():
            gg_ref[...] = jnp.zeros_like(gg_ref)

        @pl.when(i % tps == 0)
        def _():
            acc_ref[...] = jnp.zeros_like(acc_ref)

        gg = a_shift = a_scale = 0.0
        for rows in _sub_rows(tm):
            du = None
            off = 0
            for p_ref, wd in zip((d0_ref, d1_ref, d2_ref, d3_ref), widths):
                part = _nn(p_ref[rows, :], w_ref[off:off + wd, :])
                du = part if du is None else du + part
                off += wd
            xv = x_ref[rows, :]
            r = lax.rsqrt(jnp.mean(xv * xv, -1, keepdims=True) + EPS)
            hh = xv * r
            n1 = hh * g_ref[...]
            dn1 = du * (1.0 + mod_ref[0, 1:2, :])
            dhat = dn1 * g_ref[...]
            gx_ref[rows, :] = dh1_ref[rows, :] + r * (dhat - hh * jnp.mean(dhat * hh, -1, keepdims=True))
            gg = gg + jnp.sum(dn1 * hh, 0, keepdims=True)
            a_shift = a_shift + jnp.sum(du, 0, keepdims=True)
            a_scale = a_scale + jnp.sum(du * n1, 0, keepdims=True)
        gg_ref[...] += gg
        acc_ref[0, 0:1, :] += a_shift
        acc_ref[0, 1:2, :] += a_scale

    row = lambda i: (i, 0)
    vec = pl.BlockSpec((1, d), lambda i: (0, 0))
    return pl.pallas_call(
        body, name="in_proj_bwd", grid=(t // tm,),
        in_specs=[pl.BlockSpec((tm, wd), row) for wd in widths] +
                 [_RESIDENT, pl.BlockSpec((tm, d), row),
                  pl.BlockSpec((tm, d), row), pl.BlockSpec((1, N_MOD, d), lambda i: (i // tps, 0, 0)), vec, _token_spec()],
        out_specs=[pl.BlockSpec((tm, d), row), pl.BlockSpec((1, 8, d), lambda i: (i // tps, 0, 0)), vec],
        out_shape=[_out((t, d), F32), _out((nb, 8, d), F32),
                   _out((1, d), F32)],
        compiler_params=_cp(("arbitrary",), 40))(*_pin(*d_parts), w_in_t, *_pin(x, dh1, mod3, g_mix, token))


_VEC_LAYOUT = (("g_mix", 1024), ("conv_b", 1536), ("g_ssd", 1024), ("pool_scale", 512), ("g_mlp", 1024),
               ("g_final", 1024), ("dt_bias", 128), ("a_log", 128), ("d_skip_lanes", 1024), ("sq_err", 1024))
_VEC_OFFSET = {}
_off = 0
for _name, _n in _VEC_LAYOUT:
    _VEC_OFFSET[_name] = _off
    _off += _n
_VEC_LANES = _off
_SMALL_PARAMS = ("b_ada", "g_mix", "conv_w", "conv_b", "dt_bias", "a_log", "d_skip", "g_ssd", "w_pool", "pool_scale",
                 "g_mlp", "g_final")


def _pack_vec(parts):
    cols = []
    for name, n in _VEC_LAYOUT:
        v = parts[name]
        if v.shape[1] < n:
            v = jnp.pad(v, ((0, 0), (0, n - v.shape[1])))
        cols.append(v)
    return jnp.concatenate(cols, 1)


def _small_adam(vec_all, wpool_all, convw_all, dmod_all, params):
    names = _SMALL_PARAMS
    nin = 4 + 3 * len(names)

    def body(*refs):
        vec_ref, wp_ref, cw_ref, dm_ref = refs[:4]
        prm = {n: refs[4 + 3 * i:7 + 3 * i] for i, n in enumerate(names)}
        loss_ref = refs[nin]
        outs = {n: refs[nin + 1 + 4 * i:nin + 5 + 4 * i] for i, n in enumerate(names)}
        vsum = vec_ref[0]
        for s in range(1, N_DEV):
            vsum = vsum + vec_ref[s]

        def lanes(name, n):
            off = _VEC_OFFSET[name]
            return vsum[:, off:off + n]

        grads = {n: lanes(n, prm[n][0].shape[1]) for n in ("g_mix", "conv_b", "g_ssd", "pool_scale", "g_mlp", "g_final", "dt_bias")}
        grads["a_log"] = lanes("a_log", SSD_HEADS) * (-jnp.exp(prm["a_log"][0][...]))
        per_lane = jnp.broadcast_to(lanes("d_skip_lanes", SSD_INNER), (8, SSD_INNER))
        grads["d_skip"] = _exact_nn(per_lane, _head_reduce_matrix(SSD_INNER, SSD_HEADS))[0:1]
        gwp = wp_ref[0].astype(F32)
        gcw = cw_ref[0]
        gb = jnp.sum(dm_ref[0], 0, keepdims=True)
        for s in range(1, N_DEV):
            gwp = gwp + wp_ref[s].astype(F32)
            gcw = gcw + cw_ref[s]
            gb = gb + jnp.sum(dm_ref[s], 0, keepdims=True)
        grads["w_pool"] = gwp
        grads["conv_w"] = gcw[0:4]
        grads["b_ada"] = gb
        total = jnp.sum(lanes("sq_err", D_MODEL), 1, keepdims=True) * (0.5 / D_MODEL)
        loss_ref[...] = jnp.broadcast_to(total, loss_ref.shape)
        for n in names:
            w_ref, m_ref, v_ref = prm[n]
            g = grads[n]
            d, m2, v2 = _adam_math(w_ref[...], g, m_ref[...], v_ref[...])
            g_ref, d_ref, m2_ref, v2_ref = outs[n]
            g_ref[...] = g
            d_ref[...] = d
            m2_ref[...] = m2
            v2_ref[...] = v2

    flat = [vec_all, wpool_all, convw_all, dmod_all]
    out_shape = [jax.ShapeDtypeStruct((1, 128), F32)]
    for n in names:
        flat += list(params[n])
        out_shape += [jax.ShapeDtypeStruct(params[n][0].shape, F32)] * 4
    vm = pl.BlockSpec(memory_space=pltpu.VMEM)
    res = pl.pallas_call(body, name="small_adam", out_shape=out_shape, in_specs=[vm] * len(flat),
                         out_specs=[vm] * len(out_shape), compiler_params=_cp(vmem_mb=48))(*flat)
    return res[0], {n: res[1 + 4 * i:5 + 4 * i] for i, n in enumerate(names)}


_WEIGHTS = ("w_ada", "b_ada", "g_mix", "w_in", "conv_w", "conv_b", "dt_bias", "a_log", "d_skip", "g_ssd", "w_pool",
            "pool_scale", "w_out", "g_mlp", "w_up", "w_down", "g_final")


def _local_step(x2, tg2, mod3, seq, w_in_t, first_token, weights_later, start_reduce, conv_w_full, sp):
    t, d = x2.shape
    nb = t // seq
    dskip_e = jnp.repeat(sp["d_skip"], SSD_HEAD_DIM, axis=1)
    proj, u1 = _in_proj(x2, mod3, sp["g_mix"], w_in_t, seq, first_token)
    y_pool, p = _pool_fwd(proj, sp["w_pool"], sp["pool_scale"], nb, seq)
    y_ssd, yssm, h_prev, pre = _ssd_fwd(proj, conv_w_full, sp["conv_b"], sp["dt_bias"], sp["a_log"], dskip_e, sp["g_ssd"], nb, seq)
    w_out_f, w_up4, w_down4 = weights_later(y_ssd)
    w_down_f = w_down4.reshape(D_FF, d)
    h1, o, u2 = _out_proj(y_pool, y_ssd, w_out_f, x2, mod3, sp["g_mlp"], seq)
    a_up = _mlp_up(u2, w_up4)
    d_dn, dh2, sq, gg_final, d_gf = _mlp_down_loss(a_up, w_down_f, h1, mod3, sp["g_final"], tg2, seq)

    gw_down = _tn_matmul(a_up, d_dn, 512, d, "grad_w_down", square_relu=True)
    tok = start_reduce("w_down", gw_down.reshape(N_CHIPS, D_FF // N_CHIPS, d))
    d_a = _mlp_down_bwd(d_dn, w_down4, a_up, tok)
    gw_up4 = _tn_matmul(u2, d_a, 512, d, "grad_w_up", out3=True)
    tok = start_reduce("w_up", gw_up4)
    dh1, d_o, accf, gg_mlp = _mlp_up_bwd(d_a, w_up4, h1, dh2, o, mod3, sp["g_mlp"], seq, tok)
    gw_out_pool = _tn_matmul(y_pool, d_o, 512, d, "grad_w_out_pool")
    gw_out_ssd = _tn_matmul(y_ssd, d_o, 512, d, "grad_w_out_ssd")
    gw_out = jnp.concatenate([gw_out_pool, gw_out_ssd], 0)
    tok = start_reduce("w_out", gw_out.reshape(N_CHIPS, gw_out.shape[0] // N_CHIPS, d))
    d_ypool, d_yssd = _out_proj_bwd(d_o, w_out_f, tok)
    d_upool, gw_pool, g_ps = _pool_bwd(d_ypool, p, sp["w_pool"], sp["pool_scale"], nb, seq)
    d_z, d_pre, d_udt, gg_ssd, gdsk, ga, gdtb = _ssd_bwd(proj, pre, d_yssd, yssm, h_prev, sp["dt_bias"], sp["a_log"],
                                                        dskip_e, sp["g_ssd"], nb, seq)
    d_uxbc, gconvw, gconvb = _conv_bwd(d_pre, proj, conv_w_full, nb, seq)
    gw_in_t = _grad_w_in_t(d_upool, d_z, d_uxbc, d_udt, u1)
    tok = start_reduce("w_in", gw_in_t[:IN_WIDTH].reshape(N_CHIPS, IN_WIDTH // N_CHIPS, d))
    gx, accm, gg_mix = _in_proj_bwd([d_upool, d_z, d_uxbc, d_udt], w_in_t, x2, dh1, mod3, sp["g_mix"], seq, tok)

    d_mod = jnp.concatenate([accm[:, 0], accm[:, 1], accf[:, 2], accf[:, 0], accf[:, 1], d_gf[:, 0]], 1)
    vec = _pack_vec({"g_mix": gg_mix, "conv_b": gconvb, "g_ssd": gg_ssd, "pool_scale": g_ps, "g_mlp": gg_mlp,
                     "g_final": gg_final, "dt_bias": gdtb, "a_log": ga, "d_skip_lanes": gdsk, "sq_err": sq})
    return gx, d_mod, vec, gw_pool, gconvw


def kernel(x, c, w_ada, b_ada, g_mix, w_in, conv_w, conv_b, dt_bias, a_log, d_skip, g_ssd, w_pool, pool_scale, w_out, g_mlp, w_up, w_down, g_final, loss_target, m_w_ada, m_b_ada, m_g_mix, m_w_in, m_conv_w, m_conv_b, m_dt_bias, m_a_log, m_d_skip, m_g_ssd, m_w_pool, m_pool_scale, m_w_out, m_g_mlp, m_w_up, m_w_down, m_g_final, v_w_ada, v_b_ada, v_g_mix, v_w_in, v_conv_w, v_conv_b, v_dt_bias, v_a_log, v_d_skip, v_g_ssd, v_w_pool, v_pool_scale, v_w_out, v_g_mlp, v_w_up, v_w_down, v_g_final):
    nb, seq, d = x.shape
    t = nb * seq
    xi, yi, ci = _mesh_pos()
    chip = 2 * xi + yi
    me = 4 * xi + 2 * yi + ci
    ada_cols = w_ada.shape[2]
    conv_cols = conv_w.shape[2]
    in_cols = w_in.shape[2]
    w_in_s, m_w_in_s, v_w_in_s = w_in[0].T, m_w_in[0].T, v_w_in[0].T

    c8, convw8 = _all_gather_small([c, conv_w[0]])
    w_in_b = w_in_s.astype(BF)
    i_send, i_recv, i_src, i_land, in_token = _ici_start(
        [w_in_b], [jax.ShapeDtypeStruct((N_CHIPS,) + w_in_b.shape, BF)], _gather_sent, _gather_landing, "gather_start_w_in",
        after=c8)
    c_all = c8.reshape(N_DEV * nb, d)
    conv_w_full = convw8[0::2].transpose(1, 0, 2).reshape(4, N_CHIPS * conv_cols)
    b_shard = lax.dynamic_slice(b_ada, (0, chip * ada_cols), (1, ada_cols))
    mod_part, c_act = _ada_mod(c_all, w_ada[0], b_shard, in_token)
    mod_rows = mod_part.reshape(N_DEV, nb, ada_cols)
    m_send, m_recv, m_src, m_land, _ = _ici_start(
        [mod_rows], [jax.ShapeDtypeStruct((N_CHIPS, nb, ada_cols), F32)], _mod_sent, _mod_landing, "mod_start", after=mod_part)

    later = [w_out[0].astype(BF), w_up[0].astype(BF), w_down[0].astype(BF)]
    in_shard, in_land = _ici_wait(i_send, i_recv, i_src, i_land, [m_src[0]] + later, _gather_sent, _gather_landing,
                                  "gather_wait_w_in")
    (w_in4,) = _gather_finish(in_land, in_shard)
    w_in_t = jnp.pad(w_in4.reshape(N_CHIPS * in_cols, d), ((0, IN_PAD - N_CHIPS * in_cols), (0, 0)))
    mod_mine, mod_land = _ici_wait(m_send, m_recv, m_src, m_land, w_in_t, _mod_sent, _mod_landing, "mod_wait")
    mod_own = lax.dynamic_slice(mod_mine[0], (me, 0, 0), (1, nb, ada_cols))
    mod4 = lax.dynamic_update_slice(mod_land[0], mod_own, (chip, 0, 0))
    mod3 = mod4.transpose(1, 0, 2).reshape(nb, N_MOD, d)
    g_send, g_recv, g_src, g_land, first_token = _ici_start(
        later, [jax.ShapeDtypeStruct((N_CHIPS,) + s.shape, BF) for s in later], _gather_sent, _gather_landing, "gather_start",
        after=w_in4)

    def weights_later(after):
        shards, lands = _ici_wait(g_send, g_recv, g_src, g_land, after, _gather_sent, _gather_landing, "gather_wait")
        w_out4, w_up4, w_down4 = _gather_finish(lands, shards)
        return w_out4.reshape(N_CHIPS * w_out.shape[1], d), w_up4, w_down4

    pending = {}

    def start_reduce(name, grad4):
        pending[name] = _reduce_start(grad4, "reduce_start_" + name)
        return pending[name][4]

    sp = dict(g_mix=g_mix, conv_b=conv_b, dt_bias=dt_bias, a_log=a_log, d_skip=d_skip, g_ssd=g_ssd,
              w_pool=w_pool[0], pool_scale=pool_scale, g_mlp=g_mlp, g_final=g_final.reshape(1, d))
    gx, d_mod, vec, gw_pool, gconvw = _local_step(
        x.reshape(t, d), loss_target.reshape(t, d), mod3, seq, w_in_t, first_token, weights_later, start_reduce, conv_w_full, sp)

    pos = jnp.stack([ci, chip, me]).astype(jnp.int32)
    small_parts = [vec, gw_pool.reshape(4 * POOL_GROUP, POOL_GROUP).astype(BF), gconvw, d_mod]
    s_send, s_recv, s_src, s_land, s_token = _exchange_start(small_parts, ALL_PEERS, "small_start")
    halves = []
    for name in ("w_in", "w_out", "w_up", "w_down"):
        r_send, r_recv, r_src, r_land, _ = pending[name]
        own, recv = _reduce_wait(r_send, r_recv, r_src, r_land, s_token, "reduce_wait_" + name)
        halves.append(_sum_eight(recv, own, pos))
    h_send, h_recv, h_src, h_land, h_token = _exchange_start(halves, SIBLING, "halves_start")

    s_own, s_got = _exchange_wait(s_send, s_recv, s_src, s_land, ALL_PEERS, h_token, "small_wait")
    vec8, wpool8, convw8g, dmod8 = [lax.dynamic_update_slice(got, mine[None], (me,) + (0,) * mine.ndim)
                                    for got, mine in zip(s_got, s_own)]
    convw8s = lax.dynamic_slice(convw8g, (0, 0, chip * conv_cols), (N_DEV, 8, conv_cols))
    m_in = dict(b_ada=m_b_ada, g_mix=m_g_mix, conv_w=m_conv_w[0], conv_b=m_conv_b, dt_bias=m_dt_bias, a_log=m_a_log,
                d_skip=m_d_skip, g_ssd=m_g_ssd, w_pool=m_w_pool.reshape(4 * POOL_GROUP, POOL_GROUP), pool_scale=m_pool_scale,
                g_mlp=m_g_mlp, g_final=m_g_final.reshape(1, d))
    v_in = dict(b_ada=v_b_ada, g_mix=v_g_mix, conv_w=v_conv_w[0], conv_b=v_conv_b, dt_bias=v_dt_bias, a_log=v_a_log,
                d_skip=v_d_skip, g_ssd=v_g_ssd, w_pool=v_w_pool.reshape(4 * POOL_GROUP, POOL_GROUP), pool_scale=v_pool_scale,
                g_mlp=v_g_mlp, g_final=v_g_final.reshape(1, d))
    w_small = dict(sp, b_ada=b_ada, conv_w=conv_w[0], w_pool=w_pool.reshape(4 * POOL_GROUP, POOL_GROUP))
    loss_row, small = _small_adam(vec8, wpool8, convw8s, dmod8, {n: (w_small[n], m_in[n], v_in[n]) for n in _SMALL_PARAMS})

    dmod_all = dmod8.reshape(N_DEV * nb, N_CHIPS * ada_cols)
    dmod_cols = lax.dynamic_slice(dmod_all, (0, chip * ada_cols), (N_DEV * nb, ada_cols))
    res = {n: tuple(r.reshape(w.shape) for r in small[n])
           for n, w in (("b_ada", b_ada), ("g_mix", g_mix), ("conv_w", conv_w), ("conv_b", conv_b), ("dt_bias", dt_bias),
                        ("a_log", a_log), ("d_skip", d_skip), ("g_ssd", g_ssd), ("w_pool", w_pool), ("pool_scale", pool_scale),
                        ("g_mlp", g_mlp), ("g_final", g_final))}
    g_ada, d_ada, m_ada, v_ada = _adam_ada(c_act.T.astype(BF), dmod_cols, w_ada[0], m_w_ada[0], v_w_ada[0])
    res["w_ada"] = (g_ada[None], d_ada[None], m_ada[None], v_ada[None])
    h_own, h_got = _exchange_wait(h_send, h_recv, h_src, h_land, SIBLING, g_ada, "halves_wait")
    g_in, dl, m2, v2 = _adam_big(h_own[0], h_got[0], w_in_s, m_w_in_s, v_w_in_s, pos)
    res["w_in"] = (g_in.T[None], dl.T[None], m2.T[None], v2.T[None])
    for i, (n, w, m, v) in enumerate((("w_out", w_out, m_w_out, v_w_out), ("w_up", w_up, m_w_up, v_w_up),
                                      ("w_down", w_down, m_w_down, v_w_down))):
        g, dl, m2, v2 = _adam_big(h_own[i + 1], h_got[i + 1], w[0], m[0], v[0], pos)
        res[n] = (g[None], dl[None], m2[None], v2[None])

    loss = loss_row[0, 0]
    return (loss, gx.reshape(nb, seq, d), *[res[n][0] for n in _WEIGHTS], *[res[n][1] for n in _WEIGHTS],
            *[res[n][2] for n in _WEIGHTS], *[res[n][3] for n in _WEIGHTS])
```

```python
import functools

import jax
import jax.numpy as jnp
from jax import lax
from jax.experimental import pallas as pl
from jax.experimental.pallas import tpu as pltpu

F32 = jnp.float32
BF = jnp.bfloat16
MESH = pl.DeviceIdType.MESH

EPS = 1e-5
D_MODEL = 1024
POOL_WIDTH = 512
POOL_WINDOWS = (2, 4, 8, 16)
POOL_GROUP = 128
SSD_INNER = 1024
SSD_HEADS = 16
SSD_HEAD_DIM = 64
SSD_STATE = 128
GROUP_W = 512
CHUNK = 128
CONV_CH = 1536
OFF_Z = 512
OFF_XBC = 1536
OFF_DT = 3072
IN_WIDTH = 3088
IN_PAD = 3200
D_FF = 4096
N_MOD = 6
N_CHIPS = 4
N_DEV = 8
HALO = 16
CONV_HALO = 8

ADAM_LR = 0.001
ADAM_B1 = 0.9
ADAM_B2 = 0.999
ADAM_EPS = 1e-08
ADAM_WD = 0.01
ADAM_STEP = 10

VMEM_BYTES_V7X = 64 * 1024 * 1024


def _cp(semantics=None, vmem_mb=48, **kw):
    args = dict(vmem_limit_bytes=vmem_mb * 1024 * 1024, **kw)
    if semantics is not None:
        args["dimension_semantics"] = semantics
    return pltpu.CompilerParams(**args)


def _out(shape, dtype):
    return pltpu.HBM(shape, dtype)


def _pin(*arrays):
    return [pltpu.with_memory_space_constraint(a, pltpu.HBM) for a in arrays]


def _nn(a, b):
    return jnp.dot(a, b, preferred_element_type=F32)


def _nt(a, b):
    return lax.dot_general(a, b, (((1,), (1,)), ((), ())), preferred_element_type=F32)


def _tn(a, b):
    return lax.dot_general(a, b, (((0,), (0,)), ((), ())), preferred_element_type=F32)


def _split3(v):
    hi = v.astype(BF)
    r1 = v - hi.astype(F32)
    mid = r1.astype(BF)
    lo = (r1 - mid.astype(F32)).astype(BF)
    return hi, mid, lo


def _exact_nn(v, m01):
    hi, mid, lo = _split3(v)
    return _nn(hi, m01) + _nn(mid, m01) + _nn(lo, m01)


def _exact_nn_left(m01, v):
    hi, mid, lo = _split3(v)
    return _nn(m01, hi) + _nn(m01, mid) + _nn(m01, lo)


def _exact_nt_left(m01, v):
    hi, mid, lo = _split3(v)
    return _nt(m01, hi) + _nt(m01, mid) + _nt(m01, lo)


def _sigmoid(v):
    return 1.0 / (1.0 + jnp.exp(-v))


def _iota(shape, dim):
    return lax.broadcasted_iota(jnp.int32, shape, dim)


def _head_expand_matrix(heads, width):
    return (_iota((heads, width), 1) // SSD_HEAD_DIM == _iota((heads, width), 0)).astype(BF)


def _head_reduce_matrix(width, heads):
    return (_iota((width, heads), 0) // SSD_HEAD_DIM == _iota((width, heads), 1)).astype(BF)


def _mesh_pos():
    return lax.axis_index("x"), lax.axis_index("y"), lax.axis_index("c")


def _flip(v, bit):
    return v + bit - 2 * bit * v


def _all_gather_small(arrays, token=None):
    n = len(arrays)
    extra = [] if token is None else [token]

    def body(*refs):
        in_refs, out_refs = refs[:n], refs[n + len(extra):2 * n + len(extra)]
        send_sems, recv_sems, local_sems = refs[2 * n + len(extra):]
        x, y, c = _mesh_pos()
        me = 4 * x + 2 * y + c
        local = []
        for a in range(n):
            cp = pltpu.make_async_copy(in_refs[a], out_refs[a].at[me], local_sems.at[a])
            cp.start()
            local.append(cp)
        sends = []
        for k in range(1, N_DEV):
            peer = (_flip(x, (k >> 2) & 1), _flip(y, (k >> 1) & 1), _flip(c, k & 1))
            for a in range(n):
                cp = pltpu.make_async_remote_copy(
                    src_ref=in_refs[a], dst_ref=out_refs[a].at[me],
                    send_sem=send_sems.at[a, k], recv_sem=recv_sems.at[a, k],
                    device_id=peer, device_id_type=MESH)
                cp.start()
                sends.append(cp)
        for k in range(1, N_DEV):
            px, py, pc = _flip(x, (k >> 2) & 1), _flip(y, (k >> 1) & 1), _flip(c, k & 1)
            src = 4 * px + 2 * py + pc
            for a in range(n):
                pltpu.make_async_remote_copy(
                    src_ref=in_refs[a], dst_ref=out_refs[a].at[src],
                    send_sem=send_sems.at[a, k], recv_sem=recv_sems.at[a, k],
                    device_id=(px, py, pc), device_id_type=MESH).wait_recv()
        for cp in sends:
            cp.wait_send()
        for cp in local:
            cp.wait()

    vm = pl.BlockSpec(memory_space=pltpu.VMEM)
    return pl.pallas_call(
        body, name="all_gather_small",
        out_shape=[jax.ShapeDtypeStruct((N_DEV,) + a.shape, a.dtype) for a in arrays],
        in_specs=[vm] * (n + len(extra)), out_specs=[vm] * n,
        scratch_shapes=[pltpu.SemaphoreType.DMA((n, N_DEV)), pltpu.SemaphoreType.DMA((n, N_DEV)),
                        pltpu.SemaphoreType.DMA((n,))],
        compiler_params=_cp(vmem_mb=32),
    )(*arrays, *extra)


def _weight_all_gather(shards):
    n = len(shards)
    any_spec = _HBM

    def body(*refs):
        in_refs, out_refs = refs[:n], refs[n:2 * n]
        send_sems, recv_sems, fsend_sems, frecv_sems, local_sems = refs[2 * n:]
        x, y, c = _mesh_pos()
        chip = 2 * x + y
        local, sends = [], []
        for a in range(n):
            cp = pltpu.make_async_copy(in_refs[a], out_refs[a].at[chip], local_sems.at[a])
            cp.start()
            local.append(cp)

        def half(a, which):
            hc = shards[a].shape[1] // 2
            return pl.ds(pl.multiple_of(which * hc, 128), hc)

        for j in range(1, N_CHIPS):
            px, py = _flip(x, (j >> 1) & 1), _flip(y, j & 1)
            for a in range(n):
                cp = pltpu.make_async_remote_copy(
                    src_ref=in_refs[a].at[:, half(a, c)], dst_ref=out_refs[a].at[chip, :, half(a, c)],
                    send_sem=send_sems.at[a, j], recv_sem=recv_sems.at[a, j],
                    device_id=(px, py, c), device_id_type=MESH)
                cp.start()
                sends.append(cp)
        for j in range(1, N_CHIPS):
            px, py = _flip(x, (j >> 1) & 1), _flip(y, j & 1)
            src = 2 * px + py
            for a in range(n):
                landed = out_refs[a].at[src, :, half(a, c)]
                pltpu.make_async_remote_copy(
                    src_ref=landed, dst_ref=landed, send_sem=send_sems.at[a, j], recv_sem=recv_sems.at[a, j],
                    device_id=(px, py, c), device_id_type=MESH).wait_recv()
                cp = pltpu.make_async_remote_copy(
                    src_ref=landed, dst_ref=landed, send_sem=fsend_sems.at[a, j], recv_sem=frecv_sems.at[a, j],
                    device_id=(x, y, 1 - c), device_id_type=MESH)
                cp.start()
                sends.append(cp)
        for j in range(1, N_CHIPS):
            px, py = _flip(x, (j >> 1) & 1), _flip(y, j & 1)
            src = 2 * px + py
            for a in range(n):
                other = out_refs[a].at[src, :, half(a, 1 - c)]
                pltpu.make_async_remote_copy(
                    src_ref=other, dst_ref=other, send_sem=fsend_sems.at[a, j], recv_sem=frecv_sems.at[a, j],
                    device_id=(x, y, 1 - c), device_id_type=MESH).wait_recv()
        for cp in sends:
            cp.wait_send()
        for cp in local:
            cp.wait()

    return pl.pallas_call(
        body, name="weight_all_gather",
        out_shape=[_out((N_CHIPS,) + s.shape, s.dtype) for s in shards],
        in_specs=[any_spec] * n, out_specs=[any_spec] * n,
        scratch_shapes=[pltpu.SemaphoreType.DMA((n, N_CHIPS))] * 4 + [pltpu.SemaphoreType.DMA((n,))],
        compiler_params=_cp(vmem_mb=16),
    )(*shards)


_HBM = pl.BlockSpec(memory_space=pltpu.HBM)
_SEM = pl.BlockSpec(memory_space=pltpu.SEMAPHORE)
_DATAFLOW = pltpu.SideEffectType.DATAFLOW_SIDE_EFFECTING


def _peer_chip(x, y, j):
    return _flip(x, (j >> 1) & 1), _flip(y, j & 1)


def _ici_start(srcs, land_shapes, sent, landing, name, after):
    n = len(srcs)

    def body(*refs):
        src_refs, land_refs = refs[:n], refs[n:2 * n]
        send_sems, recv_sems = refs[2 * n + 1], refs[2 * n + 2]
        token = refs[-1]
        x, y, c = _mesh_pos()
        for j in range(1, N_CHIPS):
            px, py = _peer_chip(x, y, j)
            for a in range(n):
                pltpu.make_async_remote_copy(
                    src_ref=sent(src_refs[a], c, 2 * px + py), dst_ref=landing(land_refs[a], c, 2 * x + y),
                    send_sem=send_sems.at[a * (N_CHIPS - 1) + j - 1], recv_sem=recv_sems.at[a * (N_CHIPS - 1) + j - 1],
                    device_id=(px, py, c), device_id_type=MESH).start()
        token[...] = jnp.zeros_like(token)

    sems = pltpu.SemaphoreType.DMA((n * (N_CHIPS - 1),))
    lands = [pltpu.with_memory_space_constraint(lax.empty(s.shape, s.dtype), pltpu.HBM) for s in land_shapes]
    outs = pl.pallas_call(
        body, name=name,
        out_shape=(sems, sems, *[pltpu.HBM(s.shape, s.dtype) for s in srcs],
                   *[pltpu.HBM(s.shape, s.dtype) for s in land_shapes], jax.ShapeDtypeStruct((8, 128), F32)),
        in_specs=[_HBM] * (2 * n + 1), out_specs=[_SEM, _SEM] + [_HBM] * (2 * n) + [pl.BlockSpec(memory_space=pltpu.VMEM)],
        input_output_aliases={i: 2 + i for i in range(2 * n)},
        compiler_params=pltpu.CompilerParams(has_side_effects=_DATAFLOW),
    )(*_pin(*srcs), *lands, *_pin(after))
    return outs[0], outs[1], outs[2:2 + n], outs[2 + n:2 + 2 * n], outs[-1]


def _ici_wait(send_sems, recv_sems, src_thru, land_thru, after, sent, landing, name):
    n = len(src_thru)
    afters = list(after) if isinstance(after, (list, tuple)) else [after]

    def body(*refs):
        src_refs, land_refs = refs[:n], refs[n:2 * n]
        send_sems, recv_sems = refs[2 * n], refs[2 * n + 1]
        x, y, c = _mesh_pos()
        for j in range(1, N_CHIPS):
            px, py = _peer_chip(x, y, j)
            for a in range(n):
                cp = pltpu.make_async_remote_copy(
                    src_ref=sent(src_refs[a], c, 2 * px + py), dst_ref=landing(land_refs[a], c, 2 * px + py),
                    send_sem=send_sems.at[a * (N_CHIPS - 1) + j - 1], recv_sem=recv_sems.at[a * (N_CHIPS - 1) + j - 1],
                    device_id=(px, py, c), device_id_type=MESH)
                cp.wait_send()
                cp.wait_recv()

    outs = pl.pallas_call(
        body, name=name,
        out_shape=tuple(pltpu.HBM(s.shape, s.dtype) for s in (*src_thru, *land_thru)),
        in_specs=[_HBM] * (2 * n) + [_SEM, _SEM] + [_HBM] * len(afters), out_specs=[_HBM] * (2 * n),
        input_output_aliases={i: i for i in range(2 * n)},
        compiler_params=pltpu.CompilerParams(has_side_effects=_DATAFLOW),
    )(*src_thru, *land_thru, send_sems, recv_sems, *_pin(*afters))
    return outs[:n], outs[n:]


def _col_half(ref, which, lead=()):
    hc = ref.shape[-1] // 2
    return ref.at[(*lead, slice(None), pl.ds(pl.multiple_of(which * hc, 128), hc))]


def _gather_sent(ref, c, dst_chip):
    return _col_half(ref, c)


def _gather_landing(ref, c, src_chip):
    return _col_half(ref, c, lead=(src_chip,))


def _mod_sent(ref, c, dst_chip):
    return ref.at[2 * dst_chip + c]


def _mod_landing(ref, c, src_chip):
    return ref.at[src_chip]


def _reduce_copy(src_ref, land_ref, send_sems, recv_sems, k, receiving):
    x, y, c = _mesh_pos()
    px, py, pc = _flip(x, (k >> 2) & 1), _flip(y, (k >> 1) & 1), _flip(c, k & 1)
    hc = src_ref.shape[2] // 2
    src = src_ref.at[2 * px + py, :, pl.ds(pl.multiple_of(pc * hc, 128), hc)]
    slot = (4 * px + 2 * py + pc) if receiving else (4 * x + 2 * y + c)
    return pltpu.make_async_remote_copy(
        src_ref=src, dst_ref=land_ref.at[slot], send_sem=send_sems.at[k - 1], recv_sem=recv_sems.at[k - 1],
        device_id=(px, py, pc), device_id_type=MESH)


def _reduce_start(grad4, name):
    k4, r, cols = grad4.shape

    def body(src_ref, land_ref, send_sems, recv_sems, src_thru, land_thru, token):
        for k in range(1, N_DEV):
            _reduce_copy(src_ref, land_ref, send_sems, recv_sems, k, receiving=False).start()
        token[...] = jnp.zeros_like(token)

    sems = pltpu.SemaphoreType.DMA((N_DEV - 1,))
    land = pltpu.with_memory_space_constraint(lax.empty((N_DEV, r, cols // 2), grad4.dtype), pltpu.HBM)
    return pl.pallas_call(
        body, name=name,
        out_shape=(sems, sems, pltpu.HBM(grad4.shape, grad4.dtype), pltpu.HBM(land.shape, land.dtype),
                   jax.ShapeDtypeStruct((8, 128), F32)),
        in_specs=[_HBM, _HBM], out_specs=[_SEM, _SEM, _HBM, _HBM, pl.BlockSpec(memory_space=pltpu.VMEM)],
        input_output_aliases={0: 2, 1: 3},
        compiler_params=pltpu.CompilerParams(has_side_effects=_DATAFLOW),
    )(*_pin(grad4), land)


def _reduce_wait(send_sems, recv_sems, src_thru, land_thru, after, name):
    def body(src_ref, land_ref, send_sems, recv_sems, after_ref, src_out, land_out):
        for k in range(1, N_DEV):
            cp = _reduce_copy(src_ref, land_ref, send_sems, recv_sems, k, receiving=True)
            cp.wait_send()
            cp.wait_recv()

    return pl.pallas_call(
        body, name=name,
        out_shape=(pltpu.HBM(src_thru.shape, src_thru.dtype), pltpu.HBM(land_thru.shape, land_thru.dtype)),
        in_specs=[_HBM, _HBM, _SEM, _SEM, _HBM], out_specs=[_HBM, _HBM],
        input_output_aliases={0: 0, 1: 1},
        compiler_params=pltpu.CompilerParams(has_side_effects=_DATAFLOW),
    )(src_thru, land_thru, send_sems, recv_sems, *_pin(after))


def _peer_copy(src_ref, land_ref, send_sems, recv_sems, idx, k, receiving):
    x, y, c = _mesh_pos()
    px, py, pc = _flip(x, (k >> 2) & 1), _flip(y, (k >> 1) & 1), _flip(c, k & 1)
    if land_ref.shape[0] == N_DEV:
        slot = (4 * px + 2 * py + pc) if receiving else (4 * x + 2 * y + c)
    else:
        slot = pc if receiving else c
    return pltpu.make_async_remote_copy(
        src_ref=src_ref, dst_ref=land_ref.at[slot], send_sem=send_sems.at[idx], recv_sem=recv_sems.at[idx],
        device_id=(px, py, pc), device_id_type=MESH)


def _exchange_start(arrays, peers, name):
    n = len(arrays)

    def body(*refs):
        src_refs, land_refs = refs[:n], refs[n:2 * n]
        send_sems, recv_sems = refs[2 * n], refs[2 * n + 1]
        token = refs[-1]
        for j, k in enumerate(peers):
            for a in range(n):
                _peer_copy(src_refs[a], land_refs[a], send_sems, recv_sems, a * len(peers) + j, k, receiving=False).start()
        token[...] = jnp.zeros_like(token)

    sems = pltpu.SemaphoreType.DMA((n * len(peers),))
    n_slots = N_DEV if len(peers) > 1 else 2
    lands = [pltpu.with_memory_space_constraint(lax.empty((n_slots,) + a.shape, a.dtype), pltpu.HBM) for a in arrays]
    outs = pl.pallas_call(
        body, name=name,
        out_shape=(sems, sems, *[pltpu.HBM(a.shape, a.dtype) for a in arrays], *[pltpu.HBM(l.shape, l.dtype) for l in lands],
                   jax.ShapeDtypeStruct((8, 128), F32)),
        in_specs=[_HBM] * (2 * n), out_specs=[_SEM, _SEM] + [_HBM] * (2 * n) + [pl.BlockSpec(memory_space=pltpu.VMEM)],
        input_output_aliases={i: 2 + i for i in range(2 * n)},
        compiler_params=pltpu.CompilerParams(has_side_effects=_DATAFLOW),
    )(*_pin(*arrays), *lands)
    return outs[0], outs[1], outs[2:2 + n], outs[2 + n:2 + 2 * n], outs[-1]


def _exchange_wait(send_sems, recv_sems, src_thru, land_thru, peers, after, name):
    n = len(src_thru)

    def body(*refs):
        src_refs, land_refs = refs[:n], refs[n:2 * n]
        send_sems, recv_sems = refs[2 * n], refs[2 * n + 1]
        for j, k in enumerate(peers):
            for a in range(n):
                cp = _peer_copy(src_refs[a], land_refs[a], send_sems, recv_sems, a * len(peers) + j, k, receiving=True)
                cp.wait_send()
                cp.wait_recv()

    outs = pl.pallas_call(
        body, name=name,
        out_shape=tuple(pltpu.HBM(s.shape, s.dtype) for s in (*src_thru, *land_thru)),
        in_specs=[_HBM] * (2 * n) + [_SEM, _SEM, _HBM], out_specs=[_HBM] * (2 * n),
        input_output_aliases={i: i for i in range(2 * n)},
        compiler_params=pltpu.CompilerParams(has_side_effects=_DATAFLOW),
    )(*src_thru, *land_thru, send_sems, recv_sems, *_pin(after))
    return outs[:n], outs[n:]


ALL_PEERS = tuple(range(1, N_DEV))
SIBLING = (1,)


def _sum_eight(recv, grad4, pos):
    n, r, hc = recv.shape
    steps = 4
    tc = hc // steps

    def body(pos_ref, r_ref, g_ref, o_ref):
        me = pos_ref[2]
        o_ref[...] = jnp.zeros_like(o_ref)
        for s in range(n):
            @pl.when(me == s)
            def _():
                o_ref[...] += g_ref[0].astype(F32)

            @pl.when(me != s)
            def _():
                o_ref[...] += r_ref[s].astype(F32)

    grid_spec = pltpu.PrefetchScalarGridSpec(
        num_scalar_prefetch=1, grid=(steps,),
        in_specs=[pl.BlockSpec((n, r, tc), lambda i, pos: (0, 0, i)),
                  pl.BlockSpec((1, r, tc), lambda i, pos: (pos[1], 0, pos[0] * steps + i))],
        out_specs=pl.BlockSpec((r, tc), lambda i, pos: (0, i)))
    return pl.pallas_call(body, name="sum_eight", grid_spec=grid_spec, out_shape=_out((r, hc), F32),
                          compiler_params=_cp(("parallel",), 32))(pos, *_pin(recv, grad4))


def _gather_finish(lands, shards):
    n = len(lands)
    any_spec = _HBM

    def body(*refs):
        shard_refs, out_refs = refs[n:2 * n], refs[2 * n:3 * n]
        send_sems, recv_sems, local_sems = refs[3 * n:]
        x, y, c = _mesh_pos()
        chip = 2 * x + y
        local, sends = [], []
        for a in range(n):
            cp = pltpu.make_async_copy(shard_refs[a], out_refs[a].at[chip], local_sems.at[a])
            cp.start()
            local.append(cp)
        for j in range(1, N_CHIPS):
            px, py = _peer_chip(x, y, j)
            for a in range(n):
                landed = _col_half(out_refs[a], c, lead=(2 * px + py,))
                cp = pltpu.make_async_remote_copy(
                    src_ref=landed, dst_ref=landed, send_sem=send_sems.at[a, j], recv_sem=recv_sems.at[a, j],
                    device_id=(x, y, 1 - c), device_id_type=MESH)
                cp.start()
                sends.append(cp)
        for j in range(1, N_CHIPS):
            px, py = _peer_chip(x, y, j)
            for a in range(n):
                other = _col_half(out_refs[a], 1 - c, lead=(2 * px + py,))
                pltpu.make_async_remote_copy(
                    src_ref=other, dst_ref=other, send_sem=send_sems.at[a, j], recv_sem=recv_sems.at[a, j],
                    device_id=(x, y, 1 - c), device_id_type=MESH).wait_recv()
        for cp in sends:
            cp.wait_send()
        for cp in local:
            cp.wait()

    return pl.pallas_call(
        body, name="gather_finish",
        out_shape=[_out(l.shape, l.dtype) for l in lands],
        in_specs=[any_spec] * (2 * n), out_specs=[any_spec] * n,
        input_output_aliases={i: i for i in range(n)},
        scratch_shapes=[pltpu.SemaphoreType.DMA((n, N_CHIPS))] * 2 + [pltpu.SemaphoreType.DMA((n,))],
        compiler_params=_cp(vmem_mb=16),
    )(*lands, *shards)


def _halves_exchange(halves):
    n = len(halves)
    any_spec = _HBM

    def body(*refs):
        in_refs, out_refs = refs[:n], refs[n:2 * n]
        send_sems, recv_sems, local_sems = refs[2 * n:]
        x, y, c = _mesh_pos()
        copies, local = [], []
        for a in range(n):
            hc = halves[a].shape[1]
            mine = pl.ds(pl.multiple_of(c * hc, 128), hc)
            lc = pltpu.make_async_copy(in_refs[a], out_refs[a].at[:, mine], local_sems.at[a])
            lc.start()
            local.append(lc)
            cp = pltpu.make_async_remote_copy(
                src_ref=in_refs[a], dst_ref=out_refs[a].at[:, mine],
                send_sem=send_sems.at[a], recv_sem=recv_sems.at[a],
                device_id=(x, y, 1 - c), device_id_type=MESH)
            cp.start()
            copies.append(cp)
        for a in range(n):
            hc = halves[a].shape[1]
            theirs = pl.ds(pl.multiple_of((1 - c) * hc, 128), hc)
            pltpu.make_async_remote_copy(
                src_ref=in_refs[a], dst_ref=out_refs[a].at[:, theirs],
                send_sem=send_sems.at[a], recv_sem=recv_sems.at[a],
                device_id=(x, y, 1 - c), device_id_type=MESH).wait_recv()
        for cp in copies:
            cp.wait_send()
        for lc in local:
            lc.wait()

    return pl.pallas_call(
        body, name="halves_exchange",
        out_shape=[_out((h.shape[0], 2 * h.shape[1]), h.dtype) for h in halves],
        in_specs=[any_spec] * n, out_specs=[any_spec] * n,
        scratch_shapes=[pltpu.SemaphoreType.DMA((n,))] * 3,
        compiler_params=_cp(vmem_mb=16),
    )(*halves)


def _adam_math(w, g, m, v):
    m2 = ADAM_B1 * m + (1.0 - ADAM_B1) * g
    v2 = ADAM_B2 * v + (1.0 - ADAM_B2) * (g * g)
    m_hat = m2 / (1.0 - ADAM_B1 ** ADAM_STEP)
    v_hat = v2 / (1.0 - ADAM_B2 ** ADAM_STEP)
    delta = -ADAM_LR * (m_hat / (jnp.sqrt(v_hat) + ADAM_EPS) + ADAM_WD * w)
    return delta, m2, v2


def _adam_big(g_own, g_pair, w, m, v, pos):
    r, c = w.shape
    tc = c // 4

    def body(pos_ref, go_ref, gp_ref, w_ref, m_ref, v_ref, g_ref, d_ref, m2_ref, v2_ref):
        half = pl.program_id(0) // 2
        g = jnp.where(half == pos_ref[0], go_ref[...], gp_ref[0])
        d, m2, v2 = _adam_math(w_ref[...], g, m_ref[...], v_ref[...])
        g_ref[...] = g
        d_ref[...] = d
        m2_ref[...] = m2
        v2_ref[...] = v2

    spec = pl.BlockSpec((r, tc), lambda i, pos: (0, i))
    grid_spec = pltpu.PrefetchScalarGridSpec(
        num_scalar_prefetch=1, grid=(4,),
        in_specs=[pl.BlockSpec((r, tc), lambda i, pos: (0, i % 2)),
                  pl.BlockSpec((1, r, tc), lambda i, pos: (1 - pos[0], 0, i % 2)), spec, spec, spec],
        out_specs=[spec] * 4)
    sh = _out((r, c), F32)
    return pl.pallas_call(body, name="adam_big", grid_spec=grid_spec, out_shape=[sh] * 4,
                          compiler_params=_cp(("parallel",), 32))(pos, *_pin(g_own, g_pair, w, m, v))


def _adam_ada(c_act_t, dmod_cols, w, m, v):
    r, c = w.shape
    tc = 512

    def body(ct_ref, dm_ref, w_ref, m_ref, v_ref, g_ref, d_ref, m2_ref, v2_ref):
        g = _nn(ct_ref[...], dm_ref[...].astype(BF))
        d, m2, v2 = _adam_math(w_ref[...], g, m_ref[...], v_ref[...])
        g_ref[...] = g
        d_ref[...] = d
        m2_ref[...] = m2
        v2_ref[...] = v2

    spec = pl.BlockSpec((r, tc), lambda i: (0, i))
    sh = _out((r, c), F32)
    return pl.pallas_call(
        body, name="adam_ada", grid=(c // tc,),
        in_specs=[pl.BlockSpec(c_act_t.shape, lambda i: (0, 0)), pl.BlockSpec((dmod_cols.shape[0], tc), lambda i: (0, i)),
                  spec, spec, spec],
        out_specs=[spec] * 4, out_shape=[sh] * 4, compiler_params=_cp(("parallel",), 48))(*_pin(c_act_t, dmod_cols, w, m, v))


def _ada_mod(c_all, w_shard, b_shard, token):
    nb, d = c_all.shape
    cols = w_shard.shape[1]
    tc = 512

    def body(c_ref, w_ref, b_ref, tok_ref, mod_ref, act_ref):
        cv = c_ref[...]
        act = cv * _sigmoid(cv)
        act_ref[...] = act
        mod_ref[...] = _nn(act.astype(BF), w_ref[...].astype(BF)) + b_ref[...]

    return pl.pallas_call(
        body, name="ada_mod", grid=(cols // tc,),
        in_specs=[pl.BlockSpec((nb, d), lambda i: (0, 0)), pl.BlockSpec((d, tc), lambda i: (0, i)),
                  pl.BlockSpec((1, tc), lambda i: (0, i)), _token_spec()],
        out_specs=[pl.BlockSpec((nb, tc), lambda i: (0, i)), pl.BlockSpec((nb, d), lambda i: (0, 0))],
        out_shape=[_out((nb, cols), F32), _out((nb, d), F32)],
        compiler_params=_cp(("arbitrary",), 32))(*_pin(c_all, w_shard, b_shard, token))


SUB_ROWS = 256
ROW_TILE = 512


def _sub_rows(tm):
    return [slice(s, s + SUB_ROWS) for s in range(0, tm, SUB_ROWS)] if tm > SUB_ROWS else [slice(0, tm)]


_RESIDENT = pl.BlockSpec(memory_space=pltpu.VMEM)


def _token_spec():
    return pl.BlockSpec((8, 128), lambda *_: (0, 0))


def _in_proj(x, mod3, g_mix, w_in_t, seq, token):
    t, d = x.shape
    tm = min(ROW_TILE, seq)
    tps = seq // tm

    def body(x_ref, mod_ref, g_ref, w_ref, tok_ref, proj_ref, u1_ref):
        for rows in _sub_rows(tm):
            xv = x_ref[rows, :]
            r = lax.rsqrt(jnp.mean(xv * xv, -1, keepdims=True) + EPS)
            u = (xv * r * g_ref[...]) * (1.0 + mod_ref[0, 1:2, :]) + mod_ref[0, 0:1, :]
            ub = u.astype(BF)
            u1_ref[rows, :] = ub
            proj_ref[rows, :] = _nt(ub, w_ref[...])

    return pl.pallas_call(
        body, name="in_proj", grid=(t // tm,),
        in_specs=[pl.BlockSpec((tm, d), lambda i: (i, 0)), pl.BlockSpec((1, N_MOD, d), lambda i: (i // tps, 0, 0)),
                  pl.BlockSpec((1, d), lambda i: (0, 0)), _RESIDENT, _token_spec()],
        out_specs=[pl.BlockSpec((tm, IN_PAD), lambda i: (i, 0)), pl.BlockSpec((tm, d), lambda i: (i, 0))],
        out_shape=[_out((t, IN_PAD), F32), _out((t, d), BF)],
        compiler_params=_cp(("parallel",), 40))(*_pin(x, mod3, g_mix), w_in_t, *_pin(token))


def _pool_tile(seq):
    return min(512, seq)


def _pool_fwd(proj, w_pool, pool_scale, nb, seq):
    ts = _pool_tile(seq)
    nt = seq // ts

    def body(u_ref, halo_ref, wp_ref, ps_ref, yp_ref, p_ref):
        i = pl.program_id(1)
        halo = jnp.where(i == 0, 0.0, halo_ref[...])
        u = u_ref[...]
        ext = jnp.concatenate([halo, u], 0)
        tpos = i * ts + _iota((ts, 1), 0)
        for g, w in enumerate(POOL_WINDOWS):
            gs = slice(g * POOL_GROUP, (g + 1) * POOL_GROUP)
            s = ext[:, gs]
            sh = 1
            while sh < w:
                s = s + pltpu.roll(s, sh, 0)
                sh *= 2
            cnt = jnp.minimum(tpos + 1, w).astype(F32)
            pb = (s[HALO:] / cnt - u[:, gs]).astype(BF)
            p_ref[:, gs] = pb
            yp_ref[:, gs] = (_nn(pb, wp_ref[g].astype(BF)) * ps_ref[:, gs]).astype(BF)

    hb = ts // HALO
    return pl.pallas_call(
        body, name="pool_fwd", grid=(nb, nt),
        in_specs=[pl.BlockSpec((ts, POOL_WIDTH), lambda b, i: (b * nt + i, 0)),
                  pl.BlockSpec((HALO, POOL_WIDTH), lambda b, i: (jnp.maximum((b * nt + i) * hb - 1, 0), 0)),
                  pl.BlockSpec((4, POOL_GROUP, POOL_GROUP), lambda b, i: (0, 0, 0)),
                  pl.BlockSpec((1, POOL_WIDTH), lambda b, i: (0, 0))],
        out_specs=[pl.BlockSpec((ts, POOL_WIDTH), lambda b, i: (b * nt + i, 0))] * 2,
        out_shape=[_out((nb * seq, POOL_WIDTH), BF)] * 2,
        compiler_params=_cp(("parallel", "parallel"), 32))(*_pin(proj, proj, w_pool, pool_scale))


def _conv_pre(uxbc, halo, cw, cb, first):
    halo = jnp.where(first, 0.0, halo)
    ext = jnp.concatenate([halo, uxbc], 0)
    pre = cb + uxbc * cw[3:4]
    for k in (2, 1, 0):
        pre = pre + pltpu.roll(ext, 3 - k, 0)[CONV_HALO:] * cw[k:k + 1]
    return pre


def _chunk_terms(pre, udt, dtb, alog):
    sg = _sigmoid(pre)
    xbc = pre * sg
    dtp = udt[:, :SSD_HEADS] + dtb
    dt = jnp.maximum(dtp, 0.0) + jnp.log(1.0 + jnp.exp(-jnp.abs(dtp)))
    a = -jnp.exp(alog)
    da = dt * a
    tril = (_iota((CHUNK, CHUNK), 0) >= _iota((CHUNK, CHUNK), 1))
    acum = _exact_nn_left(tril.astype(BF), da)
    eye = (_iota((SSD_HEADS, SSD_HEADS), 0) == _iota((SSD_HEADS, SSD_HEADS), 1)).astype(BF)
    acum_t = _exact_nt_left(eye, acum)
    expand = _head_expand_matrix(SSD_HEADS, SSD_INNER)
    acum_e = _exact_nn(acum, expand)
    dt_e = _exact_nn(dt, expand)
    last_e = acum_e[CHUNK - 1:CHUNK]
    return dict(pre=pre, sg=sg, xbc=xbc, dtp=dtp, dt=dt, a=a, acum=acum, acum_t=acum_t, tril=tril,
                dt_e=dt_e, e_a=jnp.exp(acum_e), d_out=jnp.exp(last_e - acum_e), c_dec=jnp.exp(last_e))


def _head_decay(r, h):
    seg = r["acum"][:, h:h + 1] - r["acum_t"][h:h + 1, :]
    return jnp.where(r["tril"], jnp.exp(jnp.minimum(seg, 0.0)), 0.0)


SSD_SUB = 4
SSD_ROWS = SSD_SUB * CHUNK


def _ssd_specs(nb, seq, reverse):
    ns = seq // SSD_ROWS
    per = seq // CONV_HALO

    def cidx(c):
        return (ns - 1 - c) if reverse else c

    def row(b, c):
        return b * ns + cidx(c)

    specs = [
        pl.BlockSpec((SSD_ROWS, CONV_CH), lambda b, c: (row(b, c), 1)),
        pl.BlockSpec((CONV_HALO, CONV_CH),
                     lambda b, c: (jnp.maximum(b * per + cidx(c) * (SSD_ROWS // CONV_HALO) - 1, 0), 1)),
        pl.BlockSpec((SSD_ROWS, GROUP_W), lambda b, c: (row(b, c), 1)),
        pl.BlockSpec((SSD_ROWS, GROUP_W), lambda b, c: (row(b, c), 2)),
        pl.BlockSpec((SSD_ROWS, 128), lambda b, c: (row(b, c), OFF_DT // 128)),
    ]
    return specs, row, cidx, ns


def _const_spec(shape):
    return pl.BlockSpec(shape, lambda b, c: (0,) * len(shape))


def _ssd_fwd(proj, conv_w, conv_b, dt_bias, a_log, dskip_e, g_ssd, nb, seq):
    specs, row, cidx, ns = _ssd_specs(nb, seq, reverse=False)

    def body(uxbc_ref, halo_ref, z0_ref, z1_ref, udt_ref, cw_ref, cb_ref, dtb_ref, alog_ref, dsk_ref, gs_ref,
             yssd_ref, yssm_ref, hprev_ref, pre_ref, h_ref, yd_ref):
        c = pl.program_id(1)

        @pl.when(c == 0)
        def _():
            h_ref[...] = jnp.zeros_like(h_ref)

        for sub in range(SSD_SUB):
            rows = slice(sub * CHUNK, (sub + 1) * CHUNK)
            if sub == 0:
                halo, first = halo_ref[...], c == 0
            else:
                halo, first = uxbc_ref[sub * CHUNK - CONV_HALO:sub * CHUNK, :], False
            pre = _conv_pre(uxbc_ref[rows, :], halo, cw_ref[...], cb_ref[...], first)
            pre_ref[rows, :] = pre
            r = _chunk_terms(pre, udt_ref[rows, :], dtb_ref[...], alog_ref[...])
            xbc = r["xbc"]
            xs = xbc[:, :SSD_INNER]
            xdt = xs * r["dt_e"]
            xdt_b = xdt.astype(BF)
            xdo_b = (xdt * r["d_out"]).astype(BF)
            hprev_ref[0, sub] = h_ref[...]
            for g in range(2):
                gs = slice(g * GROUP_W, (g + 1) * GROUP_W)
                bg = xbc[:, SSD_INNER + g * SSD_STATE:SSD_INNER + (g + 1) * SSD_STATE].astype(BF)
                cg = xbc[:, SSD_INNER + (2 + g) * SSD_STATE:SSD_INNER + (3 + g) * SSD_STATE].astype(BF)
                scores = _nt(cg, bg)
                hg = h_ref[g]
                y_off = _nn(cg, hg.astype(BF)) * r["e_a"][:, gs]
                for hh in range(8):
                    h = g * 8 + hh
                    hs = slice(h * SSD_HEAD_DIM, (h + 1) * SSD_HEAD_DIM)
                    m = (scores * _head_decay(r, h)).astype(BF)
                    yd_ref[sub, :, hs] = _nn(m, xdt_b[:, hs])
                h_ref[g] = hg * r["c_dec"][:, gs] + _tn(bg, xdo_b[:, gs])
                y = yd_ref[sub, :, gs] + y_off + dsk_ref[:, gs] * xs[:, gs]
                yssm_ref[rows, gs] = y
                zg = (z0_ref if g == 0 else z1_ref)[rows, :]
                yg = y * (zg * _sigmoid(zg))
                rg = lax.rsqrt(jnp.mean(yg * yg, -1, keepdims=True) + EPS)
                yssd_ref[rows, gs] = (yg * rg * gs_ref[:, gs]).astype(BF)

    t = nb * seq
    return pl.pallas_call(
        body, name="ssd_fwd", grid=(nb, ns),
        in_specs=specs + [_const_spec((4, CONV_CH)), _const_spec((1, CONV_CH)), _const_spec((1, SSD_HEADS)),
                          _const_spec((1, SSD_HEADS)), _const_spec((1, SSD_INNER)), _const_spec((1, SSD_INNER))],
        out_specs=[pl.BlockSpec((SSD_ROWS, SSD_INNER), lambda b, c: (row(b, c), 0)),
                   pl.BlockSpec((SSD_ROWS, SSD_INNER), lambda b, c: (row(b, c), 0)),
                   pl.BlockSpec((1, SSD_SUB, 2, SSD_STATE, GROUP_W), lambda b, c: (b, c, 0, 0, 0)),
                   pl.BlockSpec((SSD_ROWS, CONV_CH), lambda b, c: (row(b, c), 0))],
        out_shape=[_out((t, SSD_INNER), BF), _out((t, SSD_INNER), F32),
                   _out((nb, seq // CHUNK, 2, SSD_STATE, GROUP_W), F32), _out((t, CONV_CH), F32)],
        scratch_shapes=[pltpu.VMEM((2, SSD_STATE, GROUP_W), F32), pltpu.VMEM((SSD_SUB, CHUNK, SSD_INNER), F32)],
        compiler_params=_cp(("arbitrary", "arbitrary"), 56),
    )(*_pin(proj, proj, proj, proj, proj, conv_w, conv_b, dt_bias, a_log, dskip_e, g_ssd))


def _out_proj(y_pool, y_ssd, w_out, x, mod3, g_mlp, seq):
    t, d = x.shape
    tm = 512
    tps = seq // tm if seq >= tm else 1
    tm = min(tm, seq)

    def body(yp_ref, ys_ref, w_ref, x_ref, mod_ref, g_ref, h1_ref, o_ref, u2_ref):
        for rows in _sub_rows(tm):
            o = _nn(yp_ref[rows, :], w_ref[0:POOL_WIDTH, :]) + _nn(ys_ref[rows, :], w_ref[POOL_WIDTH:, :])
            o_ref[rows, :] = o.astype(BF)
            h1 = x_ref[rows, :] + mod_ref[0, 2:3, :] * o
            h1_ref[rows, :] = h1
            r = lax.rsqrt(jnp.mean(h1 * h1, -1, keepdims=True) + EPS)
            u2_ref[rows, :] = ((h1 * r * g_ref[...]) * (1.0 + mod_ref[0, 4:5, :]) + mod_ref[0, 3:4, :]).astype(BF)

    row = lambda i: (i, 0)
    return pl.pallas_call(
        body, name="out_proj", grid=(t // tm,),
        in_specs=[pl.BlockSpec((tm, POOL_WIDTH), row), pl.BlockSpec((tm, SSD_INNER), row),
                  _RESIDENT, pl.BlockSpec((tm, d), row),
                  pl.BlockSpec((1, N_MOD, d), lambda i: (i // tps, 0, 0)), pl.BlockSpec((1, d), lambda i: (0, 0))],
        out_specs=[pl.BlockSpec((tm, d), row)] * 3,
        out_shape=[_out((t, d), F32), _out((t, d), BF), _out((t, d), BF)],
        compiler_params=_cp(("parallel",), 48))(*_pin(y_pool, y_ssd), w_out, *_pin(x, mod3, g_mlp))


def _mlp_up(u2, w_up4):
    t, d = u2.shape
    tm = min(1024, t)
    nk, _, cols = w_up4.shape

    def body(u_ref, w_ref, a_ref):
        a_ref[...] = _nn(u_ref[...], w_ref[pl.program_id(1)]).astype(BF)

    return pl.pallas_call(
        body, name="mlp_up", grid=(t // tm, nk),
        in_specs=[pl.BlockSpec((tm, d), lambda i, k: (i, 0)), _RESIDENT],
        out_specs=pl.BlockSpec((tm, cols), lambda i, k: (i, k)),
        out_shape=_out((t, nk * cols), BF),
        compiler_params=_cp(("parallel", "parallel"), 32))(*_pin(u2), w_up4)


def _mlp_down_loss(a_up, w_down, h1, mod3, g_final, target, seq):
    t, d = h1.shape
    nb = t // seq
    tm = min(ROW_TILE, seq)
    tps = seq // tm

    def body(a_ref, w_ref, h1_ref, mod_ref, g_ref, tg_ref, ddn_ref, dh2_ref, sq_ref, gg_ref, dgf_ref):
        i = pl.program_id(0)

        @pl.when(i == 0)
        def _():
            sq_ref[...] = jnp.zeros_like(sq_ref)
            gg_ref[...] = jnp.zeros_like(gg_ref)

        @pl.when(i % tps == 0)
        def _():
            dgf_ref[...] = jnp.zeros_like(dgf_ref)

        gate = mod_ref[0, 5:6, :]
        sq = gg = dgf = 0.0
        for rows in _sub_rows(tm):
            f = jnp.square(jnp.maximum(a_ref[rows, :], 0))
            dn = _nn(f, w_ref[...])
            h2 = h1_ref[rows, :] + gate * dn
            r = lax.rsqrt(jnp.mean(h2 * h2, -1, keepdims=True) + EPS)
            hh = h2 * r
            err = hh * g_ref[...] - tg_ref[rows, :]
            dy = err * (1.0 / d)
            dhat = dy * g_ref[...]
            dh2 = r * (dhat - hh * jnp.mean(dhat * hh, -1, keepdims=True))
            dh2_ref[rows, :] = dh2
            ddn_ref[rows, :] = (dh2 * gate).astype(BF)
            sq = sq + jnp.sum(err * err, 0, keepdims=True)
            gg = gg + jnp.sum(dy * hh, 0, keepdims=True)
            dgf = dgf + jnp.sum(dh2 * dn, 0, keepdims=True)
        sq_ref[...] += sq
        gg_ref[...] += gg
        dgf_ref[0] += dgf

    row = lambda i: (i, 0)
    vec = pl.BlockSpec((1, d), lambda i: (0, 0))
    return pl.pallas_call(
        body, name="mlp_down_loss", grid=(t // tm,),
        in_specs=[pl.BlockSpec((tm, D_FF), row), _RESIDENT, pl.BlockSpec((tm, d), row),
                  pl.BlockSpec((1, N_MOD, d), lambda i: (i // tps, 0, 0)), vec, pl.BlockSpec((tm, d), row)],
        out_specs=[pl.BlockSpec((tm, d), row), pl.BlockSpec((tm, d), row), vec, vec,
                   pl.BlockSpec((1, 1, d), lambda i: (i // tps, 0, 0))],
        out_shape=[_out((t, d), BF), _out((t, d), F32), _out((1, d), F32),
                   _out((1, d), F32), _out((nb, 1, d), F32)],
        compiler_params=_cp(("arbitrary",), 44))(*_pin(a_up), w_down, *_pin(h1, mod3, g_final, target))


def _tn_matmul(a, b, tk, tn, name, square_relu=False, out3=False):
    t, kdim = a.shape
    ndim = b.shape[1]

    def body(a_ref, b_ref, o_ref):
        av = a_ref[...]
        if square_relu:
            av = jnp.square(jnp.maximum(av, 0))
        res = _tn(av, b_ref[...]).astype(BF)
        if out3:
            o_ref[0] = res
        else:
            o_ref[...] = res

    if out3:
        out_spec = pl.BlockSpec((1, tk, tn), lambda j, i: (j, i, 0))
        out_shape = _out((ndim // tn, kdim, tn), BF)
    else:
        out_spec = pl.BlockSpec((tk, tn), lambda j, i: (i, j))
        out_shape = _out((kdim, ndim), BF)
    return pl.pallas_call(
        body, name=name, grid=(ndim // tn, kdim // tk),
        in_specs=[pl.BlockSpec((t, tk), lambda j, i: (0, i)), pl.BlockSpec((t, tn), lambda j, i: (0, j))],
        out_specs=out_spec, out_shape=out_shape,
        compiler_params=_cp(("parallel", "parallel"), 56))(*_pin(a, b))


def _mlp_down_bwd(d_dn, w_down4, a_up, token):
    t, d = d_dn.shape
    tm = min(1024, t)
    nk, rows, _ = w_down4.shape

    def body(g_ref, w_ref, a_ref, tok_ref, o_ref):
        df = _nt(g_ref[...], w_ref[pl.program_id(1)])
        o_ref[...] = (df * (2.0 * jnp.maximum(a_ref[...], 0).astype(F32))).astype(BF)

    return pl.pallas_call(
        body, name="mlp_down_bwd", grid=(t // tm, nk),
        in_specs=[pl.BlockSpec((tm, d), lambda i, k: (i, 0)), _RESIDENT,
                  pl.BlockSpec((tm, rows), lambda i, k: (i, k)), _token_spec()],
        out_specs=pl.BlockSpec((tm, rows), lambda i, k: (i, k)),
        out_shape=_out((t, nk * rows), BF),
        compiler_params=_cp(("parallel", "parallel"), 32))(*_pin(d_dn), w_down4, *_pin(a_up, token))


def _mlp_up_bwd(d_a, w_up4, h1, dh2, o, mod3, g_mlp, seq, token):
    t, d = h1.shape
    nb = t // seq
    tm = min(ROW_TILE, seq)
    tps = seq // tm
    nk = w_up4.shape[0]
    cols = w_up4.shape[2]

    def body(da_ref, w_ref, h1_ref, dh2_ref, o_ref, mod_ref, g_ref, tok_ref, dh1_ref, do_ref, acc_ref, gg_ref):
        i = pl.program_id(0)

        @pl.when(i == 0)
        def _():
            gg_ref[...] = jnp.zeros_like(gg_ref)

        @pl.when(i % tps == 0)
        def _():
            acc_ref[...] = jnp.zeros_like(acc_ref)

        gg = a_shift = a_scale = a_gate = 0.0
        for rows in _sub_rows(tm):
            du = _nt(da_ref[rows, 0:cols], w_ref[0])
            for k in range(1, nk):
                du = du + _nt(da_ref[rows, k * cols:(k + 1) * cols], w_ref[k])
            h1 = h1_ref[rows, :]
            r = lax.rsqrt(jnp.mean(h1 * h1, -1, keepdims=True) + EPS)
            hh = h1 * r
            n2 = hh * g_ref[...]
            dn2 = du * (1.0 + mod_ref[0, 4:5, :])
            dhat = dn2 * g_ref[...]
            dh1 = dh2_ref[rows, :] + r * (dhat - hh * jnp.mean(dhat * hh, -1, keepdims=True))
            dh1_ref[rows, :] = dh1
            do_ref[rows, :] = (dh1 * mod_ref[0, 2:3, :]).astype(BF)
            gg = gg + jnp.sum(dn2 * hh, 0, keepdims=True)
            a_shift = a_shift + jnp.sum(du, 0, keepdims=True)
            a_scale = a_scale + jnp.sum(du * n2, 0, keepdims=True)
            a_gate = a_gate + jnp.sum(dh1 * o_ref[rows, :].astype(F32), 0, keepdims=True)
        gg_ref[...] += gg
        acc_ref[0, 0:1, :] += a_shift
        acc_ref[0, 1:2, :] += a_scale
        acc_ref[0, 2:3, :] += a_gate

    row = lambda i: (i, 0)
    vec = pl.BlockSpec((1, d), lambda i: (0, 0))
    return pl.pallas_call(
        body, name="mlp_up_bwd", grid=(t // tm,),
        in_specs=[pl.BlockSpec((tm, D_FF), row), _RESIDENT, pl.BlockSpec((tm, d), row),
                  pl.BlockSpec((tm, d), row), pl.BlockSpec((tm, d), row),
                  pl.BlockSpec((1, N_MOD, d), lambda i: (i // tps, 0, 0)), vec, _token_spec()],
        out_specs=[pl.BlockSpec((tm, d), row), pl.BlockSpec((tm, d), row),
                   pl.BlockSpec((1, 8, d), lambda i: (i // tps, 0, 0)), vec],
        out_shape=[_out((t, d), F32), _out((t, d), BF),
                   _out((nb, 8, d), F32), _out((1, d), F32)],
        compiler_params=_cp(("arbitrary",), 44))(*_pin(d_a), w_up4, *_pin(h1, dh2, o, mod3, g_mlp, token))


def _out_proj_bwd(d_o, w_out, token):
    t, d = d_o.shape
    tm = min(512, t)

    def body(g_ref, w_ref, tok_ref, dp_ref, ds_ref):
        gv = g_ref[...]
        dp_ref[...] = _nt(gv, w_ref[0:POOL_WIDTH, :])
        ds_ref[...] = _nt(gv, w_ref[POOL_WIDTH:, :])

    row = lambda i: (i, 0)
    return pl.pallas_call(
        body, name="out_proj_bwd", grid=(t // tm,),
        in_specs=[pl.BlockSpec((tm, d), row), _RESIDENT, _token_spec()],
        out_specs=[pl.BlockSpec((tm, POOL_WIDTH), row), pl.BlockSpec((tm, SSD_INNER), row)],
        out_shape=[_out((t, POOL_WIDTH), F32), _out((t, SSD_INNER), F32)],
        compiler_params=_cp(("parallel",), 32))(*_pin(d_o), w_out, *_pin(token))


def _pool_bwd(d_ypool, p, w_pool, pool_scale, nb, seq):
    ts = _pool_tile(seq)
    nt = seq // ts
    hb = ts // HALO
    last_block = nb * seq // HALO - 1

    def body(dy_ref, halo_ref, p_ref, wp_ref, ps_ref, du_ref, gw_ref, gs_ref):
        b = pl.program_id(0)
        i = pl.program_id(1)

        @pl.when((b == 0) & (i == 0))
        def _():
            gw_ref[...] = jnp.zeros_like(gw_ref)
            gs_ref[...] = jnp.zeros_like(gs_ref)

        halo = jnp.where(i == nt - 1, 0.0, halo_ref[...])
        dy = dy_ref[...]
        ext = jnp.concatenate([dy, halo], 0)
        tpos = i * ts + _iota((ts + HALO, 1), 0)
        n_ext = ts + HALO
        for g, w in enumerate(POOL_WINDOWS):
            gs = slice(g * POOL_GROUP, (g + 1) * POOL_GROUP)
            wg = wp_ref[g].astype(BF)
            pg = p_ref[:, gs]
            pw = _nn(pg, wg)
            gs_ref[:, gs] += jnp.sum(dy[:, gs] * pw, 0, keepdims=True)
            dpw = (ext[:, gs] * ps_ref[:, gs]).astype(BF)
            gw_ref[g] += _tn(pg, dpw[:ts])
            dp = _nt(dpw, wg)
            cnt = jnp.minimum(tpos + 1, w).astype(F32)
            s = dp / cnt
            sh = 1
            while sh < w:
                s = s + pltpu.roll(s, n_ext - sh, 0)
                sh *= 2
            du_ref[:, gs] = (s[:ts] - dp[:ts]).astype(BF)

    return pl.pallas_call(
        body, name="pool_bwd", grid=(nb, nt),
        in_specs=[pl.BlockSpec((ts, POOL_WIDTH), lambda b, i: (b * nt + i, 0)),
                  pl.BlockSpec((HALO, POOL_WIDTH), lambda b, i: (jnp.minimum((b * nt + i + 1) * hb, last_block), 0)),
                  pl.BlockSpec((ts, POOL_WIDTH), lambda b, i: (b * nt + i, 0)),
                  pl.BlockSpec((4, POOL_GROUP, POOL_GROUP), lambda b, i: (0, 0, 0)),
                  pl.BlockSpec((1, POOL_WIDTH), lambda b, i: (0, 0))],
        out_specs=[pl.BlockSpec((ts, POOL_WIDTH), lambda b, i: (b * nt + i, 0)),
                   pl.BlockSpec((4, POOL_GROUP, POOL_GROUP), lambda b, i: (0, 0, 0)),
                   pl.BlockSpec((1, POOL_WIDTH), lambda b, i: (0, 0))],
        out_shape=[_out((nb * seq, POOL_WIDTH), BF), _out((4, POOL_GROUP, POOL_GROUP), F32),
                   _out((1, POOL_WIDTH), F32)],
        compiler_params=_cp(("arbitrary", "arbitrary"), 32))(*_pin(d_ypool, d_ypool, p, w_pool, pool_scale))


def _ssd_bwd(proj, pre, d_yssd, yssm, h_prev, dt_bias, a_log, dskip_e, g_ssd, nb, seq):
    specs, row, cidx, ns = _ssd_specs(nb, seq, reverse=True)
    specs = specs[2:]

    def body(z0_ref, z1_ref, udt_ref, pre_ref, dys_ref, yssm_ref, hprev_ref,
             dtb_ref, alog_ref, dsk_ref, gs_ref,
             dz_ref, dpre_ref, dudt_ref, ggs_ref, gdsk_ref, ga_ref, gdtb_ref,
             g_ref, dxdt_ref, dyv_ref):
        b = pl.program_id(0)
        c = pl.program_id(1)

        @pl.when(c == 0)
        def _():
            g_ref[...] = jnp.zeros_like(g_ref)

        @pl.when((b == 0) & (c == 0))
        def _():
            ggs_ref[...] = jnp.zeros_like(ggs_ref)
            gdsk_ref[...] = jnp.zeros_like(gdsk_ref)
            ga_ref[...] = jnp.zeros_like(ga_ref)
            gdtb_ref[...] = jnp.zeros_like(gdtb_ref)

        for sub in reversed(range(SSD_SUB)):
            chunk(sub, z0_ref, z1_ref, udt_ref, pre_ref, dys_ref, yssm_ref, hprev_ref, dtb_ref, alog_ref, dsk_ref, gs_ref,
                  dz_ref, dpre_ref, dudt_ref, ggs_ref, gdsk_ref, ga_ref, gdtb_ref, g_ref, dxdt_ref.at[sub], dyv_ref.at[sub])

    def chunk(sub, z0_ref, z1_ref, udt_ref, pre_ref, dys_ref, yssm_ref, hprev_ref,
              dtb_ref, alog_ref, dsk_ref, gs_ref,
              dz_ref, dpre_ref, dudt_ref, ggs_ref, gdsk_ref, ga_ref, gdtb_ref,
              g_ref, dxdt_ref, dyv_ref):
        rows = slice(sub * CHUNK, (sub + 1) * CHUNK)
        r = _chunk_terms(pre_ref[rows, :], udt_ref[rows, :], dtb_ref[...], alog_ref[...])
        xbc = r["xbc"]
        xs = xbc[:, :SSD_INNER]
        dt_e = r["dt_e"]
        xdt = xs * dt_e
        xdt_b = xdt.astype(BF)
        reduce_m = _head_reduce_matrix(GROUP_W, 8)

        def head_sums(v):
            return _nn(v.astype(BF), reduce_m)

        onehot16 = lambda h: (_iota((1, SSD_HEADS), 1) == h).astype(F32)
        onecol16 = lambda h: (_iota((SSD_HEADS, 1), 0) == h).astype(F32)

        d_acum = jnp.zeros((CHUNK, SSD_HEADS), F32)
        d_acum_t = jnp.zeros((SSD_HEADS, CHUNK), F32)
        d_alast = jnp.zeros((1, SSD_HEADS), F32)
        place8 = lambda g: (_iota((8, SSD_HEADS), 1) == _iota((8, SSD_HEADS), 0) + 8 * g).astype(BF)
        d_b, d_c = [], []
        for g in range(2):
            gs = slice(g * GROUP_W, (g + 1) * GROUP_W)
            zg = (z0_ref if g == 0 else z1_ref)[rows, :]
            sz = _sigmoid(zg)
            silu_z = zg * sz
            ys = yssm_ref[rows, gs]
            yg = ys * silu_z
            rg = lax.rsqrt(jnp.mean(yg * yg, -1, keepdims=True) + EPS)
            yh = yg * rg
            dys = dys_ref[rows, gs]
            ggs_ref[:, gs] += jnp.sum(dys * yh, 0, keepdims=True)
            dyh = dys * gs_ref[:, gs]
            dyg = rg * (dyh - yh * jnp.mean(dyh * yh, -1, keepdims=True))
            dy = dyg * silu_z
            dz_ref[rows, gs] = (dyg * ys * (sz * (1.0 + zg * (1.0 - sz)))).astype(BF)
            gdsk_ref[:, gs] += jnp.sum(dy * xs[:, gs], 0, keepdims=True)
            dyv_ref[:, gs] = dy
            dy_b = dy.astype(BF)

            bg = xbc[:, SSD_INNER + g * SSD_STATE:SSD_INNER + (g + 1) * SSD_STATE].astype(BF)
            cg = xbc[:, SSD_INNER + (2 + g) * SSD_STATE:SSD_INNER + (3 + g) * SSD_STATE].astype(BF)
            scores = _nt(cg, bg)
            hg = hprev_ref[0, sub, g]
            hg_b = hg.astype(BF)
            gg = g_ref[g]
            gg_b = gg.astype(BF)
            e_a = r["e_a"][:, gs]
            d_out = r["d_out"][:, gs]
            c_dec = r["c_dec"][:, gs]
            zc = _nn(cg, hg_b)
            wv = e_a * dy
            wv_b = wv.astype(BF)
            da_g = head_sums(wv * zc)
            dcg = _nt(wv_b, hg_b)
            d_hprev = _tn(cg, wv_b)
            vg = _nn(bg, gg_b)
            dxdt_g = d_out * vg
            dd_out = head_sums(xdt[:, gs] * vg)
            dbg = _nt((xdt[:, gs] * d_out).astype(BF), gg_b)
            dcd = _exact_nn(jnp.sum(gg * hg, 0, keepdims=True), reduce_m)
            d_out8 = jnp.exp(r["acum"][CHUNK - 1:CHUNK, 8 * g:8 * g + 8] - r["acum"][:, 8 * g:8 * g + 8])
            c_dec8 = jnp.exp(r["acum"][CHUNK - 1:CHUNK, 8 * g:8 * g + 8])
            t8 = dd_out * d_out8
            d_alast = d_alast + _exact_nn(jnp.sum(t8, 0, keepdims=True) + dcd * c_dec8, place8(g))
            d_acum = d_acum + _exact_nn(da_g - t8, place8(g))
            dsc = jnp.zeros((CHUNK, CHUNK), F32)
            for hh in range(8):
                h = g * 8 + hh
                hs = slice(h * SSD_HEAD_DIM, (h + 1) * SSD_HEAD_DIM)
                lam = _head_decay(r, h)
                m = scores * lam
                dyh_b = dy_b[:, hh * SSD_HEAD_DIM:(hh + 1) * SSD_HEAD_DIM]
                dm = _nt(dyh_b, xdt_b[:, hs])
                tm_ = dm * m
                d_acum = d_acum + jnp.sum(tm_, 1, keepdims=True) * onehot16(h)
                d_acum_t = d_acum_t + onecol16(h) * jnp.sum(tm_, 0, keepdims=True)
                dsc = dsc + dm * lam
                dxdt_ref[:, hs] = _tn(m.astype(BF), dyh_b) + dxdt_g[:, hh * SSD_HEAD_DIM:(hh + 1) * SSD_HEAD_DIM]
            dsc_b = dsc.astype(BF)
            d_c.append(dcg + _nn(dsc_b, bg))
            d_b.append(dbg + _tn(dsc_b, cg))
            g_ref[g] = d_hprev + c_dec * gg

        eye = (_iota((CHUNK, CHUNK), 0) == _iota((CHUNK, CHUNK), 1)).astype(BF)
        d_acum = d_acum - _exact_nt_left(eye, d_acum_t)
        is_last = (_iota((CHUNK, 1), 0) == CHUNK - 1).astype(F32)
        d_acum = d_acum + is_last * d_alast
        triu = (_iota((CHUNK, CHUNK), 0) <= _iota((CHUNK, CHUNK), 1)).astype(BF)
        d_da = _exact_nn_left(triu, d_acum)
        dt = r["dt"]
        ga_ref[...] += jnp.sum(d_da * dt, 0, keepdims=True)
        dxdt = dxdt_ref[...]
        reduce16 = _head_reduce_matrix(SSD_INNER, SSD_HEADS)
        d_dt = d_da * r["a"] + _nn((dxdt * xs).astype(BF), reduce16)
        d_udt = d_dt * _sigmoid(r["dtp"])
        gdtb_ref[...] += jnp.sum(d_udt, 0, keepdims=True)
        dudt_ref[rows, :] = jnp.zeros((CHUNK, dudt_ref.shape[1]), BF)
        dudt_ref[rows, 0:SSD_HEADS] = d_udt.astype(BF)
        pre, sg = r["pre"], r["sg"]
        dsilu = sg * (1.0 + pre * (1.0 - sg))
        dpre_ref[rows, 0:SSD_INNER] = (dsk_ref[...] * dyv_ref[...] + dxdt * dt_e) * dsilu[:, 0:SSD_INNER]
        for g in range(2):
            bs = slice(SSD_INNER + g * SSD_STATE, SSD_INNER + (g + 1) * SSD_STATE)
            cs = slice(SSD_INNER + (2 + g) * SSD_STATE, SSD_INNER + (3 + g) * SSD_STATE)
            dpre_ref[rows, bs] = d_b[g] * dsilu[:, bs]
            dpre_ref[rows, cs] = d_c[g] * dsilu[:, cs]

    t = nb * seq
    vec = _const_spec((1, SSD_INNER))
    small = _const_spec((1, SSD_HEADS))
    return pl.pallas_call(
        body, name="ssd_bwd", grid=(nb, ns),
        in_specs=specs + [pl.BlockSpec((SSD_ROWS, CONV_CH), lambda b, c: (row(b, c), 0)),
                          pl.BlockSpec((SSD_ROWS, SSD_INNER), lambda b, c: (row(b, c), 0)),
                          pl.BlockSpec((SSD_ROWS, SSD_INNER), lambda b, c: (row(b, c), 0)),
                          pl.BlockSpec((1, SSD_SUB, 2, SSD_STATE, GROUP_W), lambda b, c: (b, cidx(c), 0, 0, 0)),
                          small, small, vec, vec],
        out_specs=[pl.BlockSpec((SSD_ROWS, SSD_INNER), lambda b, c: (row(b, c), 0)),
                   pl.BlockSpec((SSD_ROWS, CONV_CH), lambda b, c: (row(b, c), 0)),
                   pl.BlockSpec((SSD_ROWS, 128), lambda b, c: (row(b, c), 0)),
                   vec, vec, small, small],
        out_shape=[_out((t, SSD_INNER), BF), _out((t, CONV_CH), F32),
                   _out((t, 128), BF), _out((1, SSD_INNER), F32),
                   _out((1, SSD_INNER), F32), _out((1, SSD_HEADS), F32),
                   _out((1, SSD_HEADS), F32)],
        scratch_shapes=[pltpu.VMEM((2, SSD_STATE, GROUP_W), F32), pltpu.VMEM((SSD_SUB, CHUNK, SSD_INNER), F32),
                        pltpu.VMEM((SSD_SUB, CHUNK, SSD_INNER), F32)],
        compiler_params=_cp(("arbitrary", "arbitrary"), 56),
    )(*_pin(proj, proj, proj, pre, d_yssd, yssm, h_prev, dt_bias, a_log, dskip_e, g_ssd))


def _grad_w_out(y_pool, y_ssd, d_o):
    t, d = d_o.shape
    tk = POOL_WIDTH
    n_s = SSD_INNER // tk

    def body(p_ref, s_ref, g_ref, o_ref):
        i = pl.program_id(0)

        @pl.when(i == 0)
        def _():
            o_ref[...] = _tn(p_ref[...], g_ref[...]).astype(BF)

        @pl.when(i > 0)
        def _():
            o_ref[...] = _tn(s_ref[...], g_ref[...]).astype(BF)

    return pl.pallas_call(
        body, name="grad_w_out", grid=(1 + n_s,),
        in_specs=[pl.BlockSpec((t, tk), lambda i: (0, 0)), pl.BlockSpec((t, tk), lambda i: (0, jnp.maximum(i - 1, 0))),
                  pl.BlockSpec((t, d), lambda i: (0, 0))],
        out_specs=pl.BlockSpec((tk, d), lambda i: (i, 0)),
        out_shape=_out((POOL_WIDTH + SSD_INNER, d), BF),
        compiler_params=_cp(("parallel",), 56))(*_pin(y_pool, y_ssd, d_o))


def _grad_w_in_t(d_upool, d_z, d_uxbc, d_udt, u1):
    t, d = u1.shape
    tk = 512
    n_z, n_x = SSD_INNER // tk, CONV_CH // tk

    def body(p_ref, z_ref, x_ref, dt_ref, u_ref, o_ref):
        i = pl.program_id(0)

        @pl.when(i == 0)
        def _():
            o_ref[...] = _tn(p_ref[...], u_ref[...]).astype(BF)

        @pl.when((i >= 1) & (i < 1 + n_z))
        def _():
            o_ref[...] = _tn(z_ref[...], u_ref[...]).astype(BF)

        @pl.when((i >= 1 + n_z) & (i < 1 + n_z + n_x))
        def _():
            o_ref[...] = _tn(x_ref[...], u_ref[...]).astype(BF)

        @pl.when(i == 1 + n_z + n_x)
        def _():
            o_ref[0:128, :] = _tn(dt_ref[...], u_ref[...]).astype(BF)

    return pl.pallas_call(
        body, name="grad_w_in", grid=(2 + n_z + n_x,),
        in_specs=[pl.BlockSpec((t, tk), lambda i: (0, 0)),
                  pl.BlockSpec((t, tk), lambda i: (0, jnp.clip(i - 1, 0, n_z - 1))),
                  pl.BlockSpec((t, tk), lambda i: (0, jnp.clip(i - 1 - n_z, 0, n_x - 1))),
                  pl.BlockSpec((t, 128), lambda i: (0, 0)), pl.BlockSpec((t, d), lambda i: (0, 0))],
        out_specs=pl.BlockSpec((tk, d), lambda i: (i, 0)),
        out_shape=_out((IN_PAD, d), BF),
        compiler_params=_cp(("parallel",), 56))(*_pin(d_upool, d_z, d_uxbc, d_udt, u1))


def _conv_bwd(d_pre, proj, conv_w, nb, seq):
    ts = min(256, seq)
    nt = seq // ts
    hb = ts // CONV_HALO
    last_block = nb * seq // CONV_HALO - 1
    n_ext = CHUNK + CONV_HALO

    def body(dp_ref, dnext_ref, u_ref, cw_ref, du_ref, gw_ref, gb_ref):
        b = pl.program_id(0)
        i = pl.program_id(1)

        @pl.when((b == 0) & (i == 0))
        def _():
            gw_ref[...] = jnp.zeros_like(gw_ref)
            gb_ref[...] = jnp.zeros_like(gb_ref)

        for c0 in range(0, CONV_CH, 128):
            cs = slice(c0, c0 + 128)
            cw = cw_ref[:, cs]
            gw = [0.0] * 4
            gb = 0.0
            for r0 in range(0, ts, CHUNK):
                dp = dp_ref[r0:r0 + CHUNK, cs]
                u = u_ref[r0:r0 + CHUNK, cs]
                if r0 + CHUNK < ts:
                    below = dp_ref[r0 + CHUNK:r0 + CHUNK + CONV_HALO, cs]
                else:
                    below = jnp.where(i == nt - 1, 0.0, dnext_ref[:, cs])
                ext_d = jnp.concatenate([dp, below], 0)
                du = dp * cw[3:4]
                gw[3] = gw[3] + jnp.sum(dp * u, 0, keepdims=True)
                for k in (2, 1, 0):
                    shifted = pltpu.roll(ext_d, n_ext - (3 - k), 0)[:CHUNK]
                    du = du + shifted * cw[k:k + 1]
                    gw[k] = gw[k] + jnp.sum(shifted * u, 0, keepdims=True)
                gb = gb + jnp.sum(dp, 0, keepdims=True)
                du_ref[r0:r0 + CHUNK, cs] = du.astype(BF)
            for k in range(4):
                gw_ref[k:k + 1, cs] += gw[k]
            gb_ref[:, cs] += gb

    return pl.pallas_call(
        body, name="conv_bwd", grid=(nb, nt),
        in_specs=[pl.BlockSpec((ts, CONV_CH), lambda b, i: (b * nt + i, 0)),
                  pl.BlockSpec((CONV_HALO, CONV_CH), lambda b, i: (jnp.minimum((b * nt + i + 1) * hb, last_block), 0)),
                  pl.BlockSpec((ts, CONV_CH), lambda b, i: (b * nt + i, 1)),
                  pl.BlockSpec((4, CONV_CH), lambda b, i: (0, 0))],
        out_specs=[pl.BlockSpec((ts, CONV_CH), lambda b, i: (b * nt + i, 0)),
                   pl.BlockSpec((8, CONV_CH), lambda b, i: (0, 0)), pl.BlockSpec((1, CONV_CH), lambda b, i: (0, 0))],
        out_shape=[_out((nb * seq, CONV_CH), BF), _out((8, CONV_CH), F32),
                   _out((1, CONV_CH), F32)],
        compiler_params=_cp(("arbitrary", "arbitrary"), 48))(*_pin(d_pre, d_pre, proj, conv_w))


def _in_proj_bwd(d_parts, w_in_t, x, dh1, mod3, g_mix, seq, token):
    t, d = x.shape
    nb = t // seq
    tm = min(ROW_TILE, seq)
    tps = seq // tm

    widths = [p.shape[1] for p in d_parts]

    def body(d0_ref, d1_ref, d2_ref, d3_ref, w_ref, x_ref, dh1_ref, mod_ref, g_ref, tok_ref, gx_ref, acc_ref, gg_ref):
        i = pl.program_id(0)

        @pl.when(i == 0)
        def _():
            gg_ref[...] = jnp.zeros_like(gg_ref)

        @pl.when(i % tps == 0)
        def _():
            acc_ref[...] = jnp.zeros_like(acc_ref)

        gg = a_shift = a_scale = 0.0
        for rows in _sub_rows(tm):
            du = None
            off = 0
            for p_ref, wd in zip((d0_ref, d1_ref, d2_ref, d3_ref), widths):
                part = _nn(p_ref[rows, :], w_ref[off:off + wd, :])
                du = part if du is None else du + part
                off += wd
            xv = x_ref[rows, :]
            r = lax.rsqrt(jnp.mean(xv * xv, -1, keepdims=True) + EPS)
            hh = xv * r
            n1 = hh * g_ref[...]
            dn1 = du * (1.0 + mod_ref[0, 1:2, :])
            dhat = dn1 * g_ref[...]
            gx_ref[rows, :] = dh1_ref[rows, :] + r * (dhat - hh * jnp.mean(dhat * hh, -1, keepdims=True))
            gg = gg + jnp.sum(dn1 * hh, 0, keepdims=True)
            a_shift = a_shift + jnp.sum(du, 0, keepdims=True)
            a_scale = a_scale + jnp.sum(du * n1, 0, keepdims=True)
        gg_ref[...] += gg
        acc_ref[0, 0:1, :] += a_shift
        acc_ref[0, 1:2, :] += a_scale

    row = lambda i: (i, 0)
    vec = pl.BlockSpec((1, d), lambda i: (0, 0))
    return pl.pallas_call(
        body, name="in_proj_bwd", grid=(t // tm,),
        in_specs=[pl.BlockSpec((tm, wd), row) for wd in widths] +
                 [_RESIDENT, pl.BlockSpec((tm, d), row),
                  pl.BlockSpec((tm, d), row), pl.BlockSpec((1, N_MOD, d), lambda i: (i // tps, 0, 0)), vec, _token_spec()],
        out_specs=[pl.BlockSpec((tm, d), row), pl.BlockSpec((1, 8, d), lambda i: (i // tps, 0, 0)), vec],
        out_shape=[_out((t, d), F32), _out((nb, 8, d), F32),
                   _out((1, d), F32)],
        compiler_params=_cp(("arbitrary",), 40))(*_pin(*d_parts), w_in_t, *_pin(x, dh1, mod3, g_mix, token))


_VEC_LAYOUT = (("g_mix", 1024), ("conv_b", 1536), ("g_ssd", 1024), ("pool_scale", 512), ("g_mlp", 1024),
               ("g_final", 1024), ("dt_bias", 128), ("a_log", 128), ("d_skip_lanes", 1024), ("sq_err", 1024))
_VEC_OFFSET = {}
_off = 0
for _name, _n in _VEC_LAYOUT:
    _VEC_OFFSET[_name] = _off
    _off += _n
_VEC_LANES = _off
_SMALL_PARAMS = ("b_ada", "g_mix", "conv_w", "conv_b", "dt_bias", "a_log", "d_skip", "g_ssd", "w_pool", "pool_scale",
                 "g_mlp", "g_final")


def _pack_vec(parts):
    cols = []
    for name, n in _VEC_LAYOUT:
        v = parts[name]
        if v.shape[1] < n:
            v = jnp.pad(v, ((0, 0), (0, n - v.shape[1])))
        cols.append(v)
    return jnp.concatenate(cols, 1)


def _small_adam(vec_all, wpool_all, convw_all, dmod_all, params):
    names = _SMALL_PARAMS
    nin = 4 + 3 * len(names)

    def body(*refs):
        vec_ref, wp_ref, cw_ref, dm_ref = refs[:4]
        prm = {n: refs[4 + 3 * i:7 + 3 * i] for i, n in enumerate(names)}
        loss_ref = refs[nin]
        outs = {n: refs[nin + 1 + 4 * i:nin + 5 + 4 * i] for i, n in enumerate(names)}
        vsum = vec_ref[0]
        for s in range(1, N_DEV):
            vsum = vsum + vec_ref[s]

        def lanes(name, n):
            off = _VEC_OFFSET[name]
            return vsum[:, off:off + n]

        grads = {n: lanes(n, prm[n][0].shape[1]) for n in ("g_mix", "conv_b", "g_ssd", "pool_scale", "g_mlp", "g_final", "dt_bias")}
        grads["a_log"] = lanes("a_log", SSD_HEADS) * (-jnp.exp(prm["a_log"][0][...]))
        per_lane = jnp.broadcast_to(lanes("d_skip_lanes", SSD_INNER), (8, SSD_INNER))
        grads["d_skip"] = _exact_nn(per_lane, _head_reduce_matrix(SSD_INNER, SSD_HEADS))[0:1]
        gwp = wp_ref[0].astype(F32)
        gcw = cw_ref[0]
        gb = jnp.sum(dm_ref[0], 0, keepdims=True)
        for s in range(1, N_DEV):
            gwp = gwp + wp_ref[s].astype(F32)
            gcw = gcw + cw_ref[s]
            gb = gb + jnp.sum(dm_ref[s], 0, keepdims=True)
        grads["w_pool"] = gwp
        grads["conv_w"] = gcw[0:4]
        grads["b_ada"] = gb
        total = jnp.sum(lanes("sq_err", D_MODEL), 1, keepdims=True) * (0.5 / D_MODEL)
        loss_ref[...] = jnp.broadcast_to(total, loss_ref.shape)
        for n in names:
            w_ref, m_ref, v_ref = prm[n]
            g = grads[n]
            d, m2, v2 = _adam_math(w_ref[...], g, m_ref[...], v_ref[...])
            g_ref, d_ref, m2_ref, v2_ref = outs[n]
            g_ref[...] = g
            d_ref[...] = d
            m2_ref[...] = m2
            v2_ref[...] = v2

    flat = [vec_all, wpool_all, convw_all, dmod_all]
    out_shape = [jax.ShapeDtypeStruct((1, 128), F32)]
    for n in names:
        flat += list(params[n])
        out_shape += [jax.ShapeDtypeStruct(params[n][0].shape, F32)] * 4
    vm = pl.BlockSpec(memory_space=pltpu.VMEM)
    res = pl.pallas_call(body, name="small_adam", out_shape=out_shape, in_specs=[vm] * len(flat),
                         out_specs=[vm] * len(out_shape), compiler_params=_cp(vmem_mb=48))(*flat)
    return res[0], {n: res[1 + 4 * i:5 + 4 * i] for i, n in enumerate(names)}


_WEIGHTS = ("w_ada", "b_ada", "g_mix", "w_in", "conv_w", "conv_b", "dt_bias", "a_log", "d_skip", "g_ssd", "w_pool",
            "pool_scale", "w_out", "g_mlp", "w_up", "w_down", "g_final")


def _local_step(x2, tg2, mod3, seq, w_in_t, first_token, weights_later, start_reduce, conv_w_full, sp):
    t, d = x2.shape
    nb = t // seq
    dskip_e = jnp.repeat(sp["d_skip"], SSD_HEAD_DIM, axis=1)
    proj, u1 = _in_proj(x2, mod3, sp["g_mix"], w_in_t, seq, first_token)
    y_pool, p = _pool_fwd(proj, sp["w_pool"], sp["pool_scale"], nb, seq)
    y_ssd, yssm, h_prev, pre = _ssd_fwd(proj, conv_w_full, sp["conv_b"], sp["dt_bias"], sp["a_log"], dskip_e, sp["g_ssd"], nb, seq)
    w_out_f, w_up4, w_down4 = weights_later(y_ssd)
    w_down_f = w_down4.reshape(D_FF, d)
    h1, o, u2 = _out_proj(y_pool, y_ssd, w_out_f, x2, mod3, sp["g_mlp"], seq)
    a_up = _mlp_up(u2, w_up4)
    d_dn, dh2, sq, gg_final, d_gf = _mlp_down_loss(a_up, w_down_f, h1, mod3, sp["g_final"], tg2, seq)

    gw_down = _tn_matmul(a_up, d_dn, 512, d, "grad_w_down", square_relu=True)
    tok = start_reduce("w_down", gw_down.reshape(N_CHIPS, D_FF // N_CHIPS, d))
    d_a = _mlp_down_bwd(d_dn, w_down4, a_up, tok)
    gw_up4 = _tn_matmul(u2, d_a, 512, d, "grad_w_up", out3=True)
    tok = start_reduce("w_up", gw_up4)
    dh1, d_o, accf, gg_mlp = _mlp_up_bwd(d_a, w_up4, h1, dh2, o, mod3, sp["g_mlp"], seq, tok)
    gw_out = _grad_w_out(y_pool, y_ssd, d_o)
    tok = start_reduce("w_out", gw_out.reshape(N_CHIPS, gw_out.shape[0] // N_CHIPS, d))
    d_ypool, d_yssd = _out_proj_bwd(d_o, w_out_f, tok)
    d_upool, gw_pool, g_ps = _pool_bwd(d_ypool, p, sp["w_pool"], sp["pool_scale"], nb, seq)
    d_z, d_pre, d_udt, gg_ssd, gdsk, ga, gdtb = _ssd_bwd(proj, pre, d_yssd, yssm, h_prev, sp["dt_bias"], sp["a_log"],
                                                        dskip_e, sp["g_ssd"], nb, seq)
    d_uxbc, gconvw, gconvb = _conv_bwd(d_pre, proj, conv_w_full, nb, seq)
    gw_in_t = _grad_w_in_t(d_upool, d_z, d_uxbc, d_udt, u1)
    tok = start_reduce("w_in", gw_in_t[:IN_WIDTH].reshape(N_CHIPS, IN_WIDTH // N_CHIPS, d))
    gx, accm, gg_mix = _in_proj_bwd([d_upool, d_z, d_uxbc, d_udt], w_in_t, x2, dh1, mod3, sp["g_mix"], seq, tok)

    d_mod = jnp.concatenate([accm[:, 0], accm[:, 1], accf[:, 2], accf[:, 0], accf[:, 1], d_gf[:, 0]], 1)
    vec = _pack_vec({"g_mix": gg_mix, "conv_b": gconvb, "g_ssd": gg_ssd, "pool_scale": g_ps, "g_mlp": gg_mlp,
                     "g_final": gg_final, "dt_bias": gdtb, "a_log": ga, "d_skip_lanes": gdsk, "sq_err": sq})
    return gx, d_mod, vec, gw_pool, gconvw


def kernel(x, c, w_ada, b_ada, g_mix, w_in, conv_w, conv_b, dt_bias, a_log, d_skip, g_ssd, w_pool, pool_scale, w_out, g_mlp, w_up, w_down, g_final, loss_target, m_w_ada, m_b_ada, m_g_mix, m_w_in, m_conv_w, m_conv_b, m_dt_bias, m_a_log, m_d_skip, m_g_ssd, m_w_pool, m_pool_scale, m_w_out, m_g_mlp, m_w_up, m_w_down, m_g_final, v_w_ada, v_b_ada, v_g_mix, v_w_in, v_conv_w, v_conv_b, v_dt_bias, v_a_log, v_d_skip, v_g_ssd, v_w_pool, v_pool_scale, v_w_out, v_g_mlp, v_w_up, v_w_down, v_g_final):
    nb, seq, d = x.shape
    t = nb * seq
    xi, yi, ci = _mesh_pos()
    chip = 2 * xi + yi
    me = 4 * xi + 2 * yi + ci
    ada_cols = w_ada.shape[2]
    conv_cols = conv_w.shape[2]
    in_cols = w_in.shape[2]
    w_in_s, m_w_in_s, v_w_in_s = w_in[0].T, m_w_in[0].T, v_w_in[0].T

    c8, convw8 = _all_gather_small([c, conv_w[0]])
    w_in_b = w_in_s.astype(BF)
    i_send, i_recv, i_src, i_land, in_token = _ici_start(
        [w_in_b], [jax.ShapeDtypeStruct((N_CHIPS,) + w_in_b.shape, BF)], _gather_sent, _gather_landing, "gather_start_w_in",
        after=c8)
    c_all = c8.reshape(N_DEV * nb, d)
    conv_w_full = convw8[0::2].transpose(1, 0, 2).reshape(4, N_CHIPS * conv_cols)
    b_shard = lax.dynamic_slice(b_ada, (0, chip * ada_cols), (1, ada_cols))
    mod_part, c_act = _ada_mod(c_all, w_ada[0], b_shard, in_token)
    mod_rows = mod_part.reshape(N_DEV, nb, ada_cols)
    m_send, m_recv, m_src, m_land, _ = _ici_start(
        [mod_rows], [jax.ShapeDtypeStruct((N_CHIPS, nb, ada_cols), F32)], _mod_sent, _mod_landing, "mod_start", after=mod_part)

    later = [w_out[0].astype(BF), w_up[0].astype(BF), w_down[0].astype(BF)]
    in_shard, in_land = _ici_wait(i_send, i_recv, i_src, i_land, [m_src[0]] + later, _gather_sent, _gather_landing,
                                  "gather_wait_w_in")
    (w_in4,) = _gather_finish(in_land, in_shard)
    w_in_t = jnp.pad(w_in4.reshape(N_CHIPS * in_cols, d), ((0, IN_PAD - N_CHIPS * in_cols), (0, 0)))
    mod_mine, mod_land = _ici_wait(m_send, m_recv, m_src, m_land, w_in_t, _mod_sent, _mod_landing, "mod_wait")
    mod_own = lax.dynamic_slice(mod_mine[0], (me, 0, 0), (1, nb, ada_cols))
    mod4 = lax.dynamic_update_slice(mod_land[0], mod_own, (chip, 0, 0))
    mod3 = mod4.transpose(1, 0, 2).reshape(nb, N_MOD, d)
    g_send, g_recv, g_src, g_land, first_token = _ici_start(
        later, [jax.ShapeDtypeStruct((N_CHIPS,) + s.shape, BF) for s in later], _gather_sent, _gather_landing, "gather_start",
        after=w_in4)

    def weights_later(after):
        shards, lands = _ici_wait(g_send, g_recv, g_src, g_land, after, _gather_sent, _gather_landing, "gather_wait")
        w_out4, w_up4, w_down4 = _gather_finish(lands, shards)
        return w_out4.reshape(N_CHIPS * w_out.shape[1], d), w_up4, w_down4

    pending = {}

    def start_reduce(name, grad4):
        pending[name] = _reduce_start(grad4, "reduce_start_" + name)
        return pending[name][4]

    sp = dict(g_mix=g_mix, conv_b=conv_b, dt_bias=dt_bias, a_log=a_log, d_skip=d_skip, g_ssd=g_ssd,
              w_pool=w_pool[0], pool_scale=pool_scale, g_mlp=g_mlp, g_final=g_final.reshape(1, d))
    gx, d_mod, vec, gw_pool, gconvw = _local_step(
        x.reshape(t, d), loss_target.reshape(t, d), mod3, seq, w_in_t, first_token, weights_later, start_reduce, conv_w_full, sp)

    pos = jnp.stack([ci, chip, me]).astype(jnp.int32)
    small_parts = [vec, gw_pool.reshape(4 * POOL_GROUP, POOL_GROUP).astype(BF), gconvw, d_mod]
    s_send, s_recv, s_src, s_land, s_token = _exchange_start(small_parts, ALL_PEERS, "small_start")
    halves = []
    for name in ("w_in", "w_out", "w_up", "w_down"):
        r_send, r_recv, r_src, r_land, _ = pending[name]
        own, recv = _reduce_wait(r_send, r_recv, r_src, r_land, s_token, "reduce_wait_" + name)
        halves.append(_sum_eight(recv, own, pos))
    h_send, h_recv, h_src, h_land, h_token = _exchange_start(halves, SIBLING, "halves_start")

    s_own, s_got = _exchange_wait(s_send, s_recv, s_src, s_land, ALL_PEERS, h_token, "small_wait")
    vec8, wpool8, convw8g, dmod8 = [lax.dynamic_update_slice(got, mine[None], (me,) + (0,) * mine.ndim)
                                    for got, mine in zip(s_got, s_own)]
    convw8s = lax.dynamic_slice(convw8g, (0, 0, chip * conv_cols), (N_DEV, 8, conv_cols))
    m_in = dict(b_ada=m_b_ada, g_mix=m_g_mix, conv_w=m_conv_w[0], conv_b=m_conv_b, dt_bias=m_dt_bias, a_log=m_a_log,
                d_skip=m_d_skip, g_ssd=m_g_ssd, w_pool=m_w_pool.reshape(4 * POOL_GROUP, POOL_GROUP), pool_scale=m_pool_scale,
                g_mlp=m_g_mlp, g_final=m_g_final.reshape(1, d))
    v_in = dict(b_ada=v_b_ada, g_mix=v_g_mix, conv_w=v_conv_w[0], conv_b=v_conv_b, dt_bias=v_dt_bias, a_log=v_a_log,
                d_skip=v_d_skip, g_ssd=v_g_ssd, w_pool=v_w_pool.reshape(4 * POOL_GROUP, POOL_GROUP), pool_scale=v_pool_scale,
                g_mlp=v_g_mlp, g_final=v_g_final.reshape(1, d))
    w_small = dict(sp, b_ada=b_ada, conv_w=conv_w[0], w_pool=w_pool.reshape(4 * POOL_GROUP, POOL_GROUP))
    loss_row, small = _small_adam(vec8, wpool8, convw8s, dmod8, {n: (w_small[n], m_in[n], v_in[n]) for n in _SMALL_PARAMS})

    dmod_all = dmod8.reshape(N_DEV * nb, N_CHIPS * ada_cols)
    dmod_cols = lax.dynamic_slice(dmod_all, (0, chip * ada_cols), (N_DEV * nb, ada_cols))
    res = {n: tuple(r.reshape(w.shape) for r in small[n])
           for n, w in (("b_ada", b_ada), ("g_mix", g_mix), ("conv_w", conv_w), ("conv_b", conv_b), ("dt_bias", dt_bias),
                        ("a_log", a_log), ("d_skip", d_skip), ("g_ssd", g_ssd), ("w_pool", w_pool), ("pool_scale", pool_scale),
                        ("g_mlp", g_mlp), ("g_final", g_final))}
    g_ada, d_ada, m_ada, v_ada = _adam_ada(c_act.T.astype(BF), dmod_cols, w_ada[0], m_w_ada[0], v_w_ada[0])
    res["w_ada"] = (g_ada[None], d_ada[None], m_ada[None], v_ada[None])
    h_own, h_got = _exchange_wait(h_send, h_recv, h_src, h_land, SIBLING, g_ada, "halves_wait")
    g_in, dl, m2, v2 = _adam_big(h_own[0], h_got[0], w_in_s, m_w_in_s, v_w_in_s, pos)
    res["w_in"] = (g_in.T[None], dl.T[None], m2.T[None], v2.T[None])
    for i, (n, w, m, v) in enumerate((("w_out", w_out, m_w_out, v_w_out), ("w_up", w_up, m_w_up, v_w_up),
                                      ("w_down", w_down, m_w_down, v_w_down))):
        g, dl, m2, v2 = _adam_big(h_own[i + 1], h_got[i + 1], w[0], m[0], v[0], pos)
        res[n] = (g[None], dl[None], m2[None], v2[None])

    loss = loss_row[0, 0]
    return (loss, gx.reshape(nb, seq, d), *[res[n][0] for n in _WEIGHTS], *[res[n][1] for n in _WEIGHTS],
            *[res[n][2] for n in _WEIGHTS], *[res[n][3] for n in _WEIGHTS])
```

```python
import jax
import jax.numpy as jnp
from jax import lax
from jax.experimental import pallas as pl
from jax.experimental.pallas import tpu as pltpu

F32 = jnp.float32
BF = jnp.bfloat16
MESH = pl.DeviceIdType.MESH

EPS = 1e-5
D_MODEL = 1024
POOL_WIDTH = 512
POOL_WINDOWS = (2, 4, 8, 16)
POOL_GROUP = 128
SSD_INNER = 1024
SSD_HEADS = 16
SSD_HEAD_DIM = 64
SSD_STATE = 128
GROUP_W = 512
CHUNK = 128
CONV_CH = 1536
OFF_DT = 3072
IN_WIDTH = 3088
IN_PAD = 3200
D_FF = 4096
N_MOD = 6
N_CHIPS = 4
N_DEV = 8
HALO = 16
CONV_HALO = 8

ADAM_LR = 0.001
ADAM_B1 = 0.9
ADAM_B2 = 0.999
ADAM_EPS = 1e-08
ADAM_WD = 0.01
ADAM_STEP = 10

VMEM_BYTES_V7X = 64 * 1024 * 1024


def _cp(semantics=None, vmem_mb=48, **kw):
    assert vmem_mb * 1024 * 1024 < VMEM_BYTES_V7X
    args = dict(vmem_limit_bytes=vmem_mb * 1024 * 1024, **kw)
    if semantics is not None:
        args["dimension_semantics"] = semantics
    return pltpu.CompilerParams(**args)


def _out(shape, dtype):
    return pltpu.HBM(shape, dtype)


def _pin(*arrays):
    return [pltpu.with_memory_space_constraint(a, pltpu.HBM) for a in arrays]


def _nn(a, b):
    return jnp.dot(a, b, preferred_element_type=F32)


def _nt(a, b):
    return lax.dot_general(a, b, (((1,), (1,)), ((), ())), preferred_element_type=F32)


def _tn(a, b):
    return lax.dot_general(a, b, (((0,), (0,)), ((), ())), preferred_element_type=F32)


def _split3(v):
    hi = v.astype(BF)
    r1 = v - hi.astype(F32)
    mid = r1.astype(BF)
    lo = (r1 - mid.astype(F32)).astype(BF)
    return hi, mid, lo


def _exact_nn(v, m01):
    hi, mid, lo = _split3(v)
    return _nn(hi, m01) + _nn(mid, m01) + _nn(lo, m01)


def _exact_nn_left(m01, v):
    hi, mid, lo = _split3(v)
    return _nn(m01, hi) + _nn(m01, mid) + _nn(m01, lo)


def _exact_nt_left(m01, v):
    hi, mid, lo = _split3(v)
    return _nt(m01, hi) + _nt(m01, mid) + _nt(m01, lo)


def _sigmoid(v):
    return 1.0 / (1.0 + jnp.exp(-v))


def _iota(shape, dim):
    return lax.broadcasted_iota(jnp.int32, shape, dim)


def _head_expand_matrix(heads, width):
    return (_iota((heads, width), 1) // SSD_HEAD_DIM == _iota((heads, width), 0)).astype(BF)


def _head_reduce_matrix(width, heads):
    return (_iota((width, heads), 0) // SSD_HEAD_DIM == _iota((width, heads), 1)).astype(BF)


def _mesh_pos():
    return lax.axis_index("x"), lax.axis_index("y"), lax.axis_index("c")


def _flip(v, bit):
    return v + bit - 2 * bit * v


def _all_gather_small(arrays):
    n = len(arrays)

    def body(*refs):
        in_refs, out_refs = refs[:n], refs[n:2 * n]
        send_sems, recv_sems, local_sems = refs[2 * n:]
        x, y, c = _mesh_pos()
        me = 4 * x + 2 * y + c
        local = []
        for a in range(n):
            cp = pltpu.make_async_copy(in_refs[a], out_refs[a].at[me], local_sems.at[a])
            cp.start()
            local.append(cp)
        sends = []
        for k in range(1, N_DEV):
            peer = (_flip(x, (k >> 2) & 1), _flip(y, (k >> 1) & 1), _flip(c, k & 1))
            for a in range(n):
                cp = pltpu.make_async_remote_copy(
                    src_ref=in_refs[a], dst_ref=out_refs[a].at[me],
                    send_sem=send_sems.at[a, k], recv_sem=recv_sems.at[a, k],
                    device_id=peer, device_id_type=MESH)
                cp.start()
                sends.append(cp)
        for k in range(1, N_DEV):
            px, py, pc = _flip(x, (k >> 2) & 1), _flip(y, (k >> 1) & 1), _flip(c, k & 1)
            src = 4 * px + 2 * py + pc
            for a in range(n):
                pltpu.make_async_remote_copy(
                    src_ref=in_refs[a], dst_ref=out_refs[a].at[src],
                    send_sem=send_sems.at[a, k], recv_sem=recv_sems.at[a, k],
                    device_id=(px, py, pc), device_id_type=MESH).wait_recv()
        for cp in sends:
            cp.wait_send()
        for cp in local:
            cp.wait()

    vm = pl.BlockSpec(memory_space=pltpu.VMEM)
    return pl.pallas_call(
        body, name="all_gather_small",
        out_shape=[jax.ShapeDtypeStruct((N_DEV,) + a.shape, a.dtype) for a in arrays],
        in_specs=[vm] * n, out_specs=[vm] * n,
        scratch_shapes=[pltpu.SemaphoreType.DMA((n, N_DEV)), pltpu.SemaphoreType.DMA((n, N_DEV)),
                        pltpu.SemaphoreType.DMA((n,))],
        compiler_params=_cp(vmem_mb=32),
    )(*arrays)


_HBM = pl.BlockSpec(memory_space=pltpu.HBM)
_SEM = pl.BlockSpec(memory_space=pltpu.SEMAPHORE)
_DATAFLOW = pltpu.SideEffectType.DATAFLOW_SIDE_EFFECTING


def _peer_chip(x, y, j):
    return _flip(x, (j >> 1) & 1), _flip(y, j & 1)


def _ici_start(srcs, land_shapes, sent, landing, name, after):
    n = len(srcs)

    def body(*refs):
        src_refs, land_refs = refs[:n], refs[n:2 * n]
        send_sems, recv_sems = refs[2 * n + 1], refs[2 * n + 2]
        token = refs[-1]
        x, y, c = _mesh_pos()
        for j in range(1, N_CHIPS):
            px, py = _peer_chip(x, y, j)
            for a in range(n):
                pltpu.make_async_remote_copy(
                    src_ref=sent(src_refs[a], c, 2 * px + py), dst_ref=landing(land_refs[a], c, 2 * x + y),
                    send_sem=send_sems.at[a * (N_CHIPS - 1) + j - 1], recv_sem=recv_sems.at[a * (N_CHIPS - 1) + j - 1],
                    device_id=(px, py, c), device_id_type=MESH).start()
        token[...] = jnp.zeros_like(token)

    sems = pltpu.SemaphoreType.DMA((n * (N_CHIPS - 1),))
    lands = [pltpu.with_memory_space_constraint(lax.empty(s.shape, s.dtype), pltpu.HBM) for s in land_shapes]
    outs = pl.pallas_call(
        body, name=name,
        out_shape=(sems, sems, *[pltpu.HBM(s.shape, s.dtype) for s in srcs],
                   *[pltpu.HBM(s.shape, s.dtype) for s in land_shapes], jax.ShapeDtypeStruct((8, 128), F32)),
        in_specs=[_HBM] * (2 * n + 1), out_specs=[_SEM, _SEM] + [_HBM] * (2 * n) + [pl.BlockSpec(memory_space=pltpu.VMEM)],
        input_output_aliases={i: 2 + i for i in range(2 * n)},
        compiler_params=pltpu.CompilerParams(has_side_effects=_DATAFLOW),
    )(*_pin(*srcs), *lands, *_pin(after))
    return outs[0], outs[1], outs[2:2 + n], outs[2 + n:2 + 2 * n], outs[-1]


def _ici_wait(send_sems, recv_sems, src_thru, land_thru, after, sent, landing, name):
    n = len(src_thru)
    afters = list(after) if isinstance(after, (list, tuple)) else [after]

    def body(*refs):
        src_refs, land_refs = refs[:n], refs[n:2 * n]
        send_sems, recv_sems = refs[2 * n], refs[2 * n + 1]
        x, y, c = _mesh_pos()
        for j in range(1, N_CHIPS):
            px, py = _peer_chip(x, y, j)
            for a in range(n):
                cp = pltpu.make_async_remote_copy(
                    src_ref=sent(src_refs[a], c, 2 * px + py), dst_ref=landing(land_refs[a], c, 2 * px + py),
                    send_sem=send_sems.at[a * (N_CHIPS - 1) + j - 1], recv_sem=recv_sems.at[a * (N_CHIPS - 1) + j - 1],
                    device_id=(px, py, c), device_id_type=MESH)
                cp.wait_send()
                cp.wait_recv()

    outs = pl.pallas_call(
        body, name=name,
        out_shape=tuple(pltpu.HBM(s.shape, s.dtype) for s in (*src_thru, *land_thru)),
        in_specs=[_HBM] * (2 * n) + [_SEM, _SEM] + [_HBM] * len(afters), out_specs=[_HBM] * (2 * n),
        input_output_aliases={i: i for i in range(2 * n)},
        compiler_params=pltpu.CompilerParams(has_side_effects=_DATAFLOW),
    )(*src_thru, *land_thru, send_sems, recv_sems, *_pin(*afters))
    return outs[:n], outs[n:]


def _col_half(ref, which, lead=()):
    hc = ref.shape[-1] // 2
    return ref.at[(*lead, slice(None), pl.ds(pl.multiple_of(which * hc, 128), hc))]


def _gather_sent(ref, c, dst_chip):
    return _col_half(ref, c)


def _gather_landing(ref, c, src_chip):
    return _col_half(ref, c, lead=(src_chip,))


def _mod_sent(ref, c, dst_chip):
    return ref.at[2 * dst_chip + c]


def _mod_landing(ref, c, src_chip):
    return ref.at[src_chip]


def _reduce_copy(src_ref, land_ref, send_sems, recv_sems, k, receiving):
    x, y, c = _mesh_pos()
    px, py, pc = _flip(x, (k >> 2) & 1), _flip(y, (k >> 1) & 1), _flip(c, k & 1)
    hc = src_ref.shape[2] // 2
    src = src_ref.at[2 * px + py, :, pl.ds(pl.multiple_of(pc * hc, 128), hc)]
    slot = (4 * px + 2 * py + pc) if receiving else (4 * x + 2 * y + c)
    return pltpu.make_async_remote_copy(
        src_ref=src, dst_ref=land_ref.at[slot], send_sem=send_sems.at[k - 1], recv_sem=recv_sems.at[k - 1],
        device_id=(px, py, pc), device_id_type=MESH)


def _reduce_start(grad4, name):
    k4, r, cols = grad4.shape

    def body(src_ref, land_ref, send_sems, recv_sems, src_thru, land_thru, token):
        for k in range(1, N_DEV):
            _reduce_copy(src_ref, land_ref, send_sems, recv_sems, k, receiving=False).start()
        token[...] = jnp.zeros_like(token)

    sems = pltpu.SemaphoreType.DMA((N_DEV - 1,))
    land = pltpu.with_memory_space_constraint(lax.empty((N_DEV, r, cols // 2), grad4.dtype), pltpu.HBM)
    return pl.pallas_call(
        body, name=name,
        out_shape=(sems, sems, pltpu.HBM(grad4.shape, grad4.dtype), pltpu.HBM(land.shape, land.dtype),
                   jax.ShapeDtypeStruct((8, 128), F32)),
        in_specs=[_HBM, _HBM], out_specs=[_SEM, _SEM, _HBM, _HBM, pl.BlockSpec(memory_space=pltpu.VMEM)],
        input_output_aliases={0: 2, 1: 3},
        compiler_params=pltpu.CompilerParams(has_side_effects=_DATAFLOW),
    )(*_pin(grad4), land)


def _reduce_wait(send_sems, recv_sems, src_thru, land_thru, after, name):
    def body(src_ref, land_ref, send_sems, recv_sems, after_ref, src_out, land_out):
        for k in range(1, N_DEV):
            cp = _reduce_copy(src_ref, land_ref, send_sems, recv_sems, k, receiving=True)
            cp.wait_send()
            cp.wait_recv()

    return pl.pallas_call(
        body, name=name,
        out_shape=(pltpu.HBM(src_thru.shape, src_thru.dtype), pltpu.HBM(land_thru.shape, land_thru.dtype)),
        in_specs=[_HBM, _HBM, _SEM, _SEM, _HBM], out_specs=[_HBM, _HBM],
        input_output_aliases={0: 0, 1: 1},
        compiler_params=pltpu.CompilerParams(has_side_effects=_DATAFLOW),
    )(src_thru, land_thru, send_sems, recv_sems, *_pin(after))


def _peer_copy(src_ref, land_ref, send_sems, recv_sems, idx, k, receiving):
    x, y, c = _mesh_pos()
    px, py, pc = _flip(x, (k >> 2) & 1), _flip(y, (k >> 1) & 1), _flip(c, k & 1)
    if land_ref.shape[0] == N_DEV:
        slot = (4 * px + 2 * py + pc) if receiving else (4 * x + 2 * y + c)
    else:
        slot = pc if receiving else c
    return pltpu.make_async_remote_copy(
        src_ref=src_ref, dst_ref=land_ref.at[slot], send_sem=send_sems.at[idx], recv_sem=recv_sems.at[idx],
        device_id=(px, py, pc), device_id_type=MESH)


def _exchange_start(arrays, peers, name):
    n = len(arrays)

    def body(*refs):
        src_refs, land_refs = refs[:n], refs[n:2 * n]
        send_sems, recv_sems = refs[2 * n], refs[2 * n + 1]
        token = refs[-1]
        for j, k in enumerate(peers):
            for a in range(n):
                _peer_copy(src_refs[a], land_refs[a], send_sems, recv_sems, a * len(peers) + j, k, receiving=False).start()
        token[...] = jnp.zeros_like(token)

    sems = pltpu.SemaphoreType.DMA((n * len(peers),))
    n_slots = N_DEV if len(peers) > 1 else 2
    lands = [pltpu.with_memory_space_constraint(lax.empty((n_slots,) + a.shape, a.dtype), pltpu.HBM) for a in arrays]
    outs = pl.pallas_call(
        body, name=name,
        out_shape=(sems, sems, *[pltpu.HBM(a.shape, a.dtype) for a in arrays], *[pltpu.HBM(l.shape, l.dtype) for l in lands],
                   jax.ShapeDtypeStruct((8, 128), F32)),
        in_specs=[_HBM] * (2 * n), out_specs=[_SEM, _SEM] + [_HBM] * (2 * n) + [pl.BlockSpec(memory_space=pltpu.VMEM)],
        input_output_aliases={i: 2 + i for i in range(2 * n)},
        compiler_params=pltpu.CompilerParams(has_side_effects=_DATAFLOW),
    )(*_pin(*arrays), *lands)
    return outs[0], outs[1], outs[2:2 + n], outs[2 + n:2 + 2 * n], outs[-1]


def _exchange_wait(send_sems, recv_sems, src_thru, land_thru, peers, after, name):
    n = len(src_thru)

    def body(*refs):
        src_refs, land_refs = refs[:n], refs[n:2 * n]
        send_sems, recv_sems = refs[2 * n], refs[2 * n + 1]
        for j, k in enumerate(peers):
            for a in range(n):
                cp = _peer_copy(src_refs[a], land_refs[a], send_sems, recv_sems, a * len(peers) + j, k, receiving=True)
                cp.wait_send()
                cp.wait_recv()

    outs = pl.pallas_call(
        body, name=name,
        out_shape=tuple(pltpu.HBM(s.shape, s.dtype) for s in (*src_thru, *land_thru)),
        in_specs=[_HBM] * (2 * n) + [_SEM, _SEM, _HBM], out_specs=[_HBM] * (2 * n),
        input_output_aliases={i: i for i in range(2 * n)},
        compiler_params=pltpu.CompilerParams(has_side_effects=_DATAFLOW),
    )(*src_thru, *land_thru, send_sems, recv_sems, *_pin(after))
    return outs[:n], outs[n:]


ALL_PEERS = tuple(range(1, N_DEV))
SIBLING = (1,)


def _sum_eight(recv, grad4, pos):
    n, r, hc = recv.shape
    steps = 4
    tc = hc // steps

    def body(pos_ref, r_ref, g_ref, o_ref):
        me = pos_ref[2]
        o_ref[...] = jnp.zeros_like(o_ref)
        for s in range(n):
            @pl.when(me == s)
            def _():
                o_ref[...] += g_ref[0].astype(F32)

            @pl.when(me != s)
            def _():
                o_ref[...] += r_ref[s].astype(F32)

    grid_spec = pltpu.PrefetchScalarGridSpec(
        num_scalar_prefetch=1, grid=(steps,),
        in_specs=[pl.BlockSpec((n, r, tc), lambda i, pos: (0, 0, i)),
                  pl.BlockSpec((1, r, tc), lambda i, pos: (pos[1], 0, pos[0] * steps + i))],
        out_specs=pl.BlockSpec((r, tc), lambda i, pos: (0, i)))
    return pl.pallas_call(body, name="sum_eight", grid_spec=grid_spec, out_shape=_out((r, hc), F32),
                          compiler_params=_cp(("parallel",), 32))(pos, *_pin(recv, grad4))


def _gather_finish(lands, shards):
    n = len(lands)
    any_spec = _HBM

    def body(*refs):
        shard_refs, out_refs = refs[n:2 * n], refs[2 * n:3 * n]
        send_sems, recv_sems, local_sems = refs[3 * n:]
        x, y, c = _mesh_pos()
        chip = 2 * x + y
        local, sends = [], []
        for a in range(n):
            cp = pltpu.make_async_copy(shard_refs[a], out_refs[a].at[chip], local_sems.at[a])
            cp.start()
            local.append(cp)
        for j in range(1, N_CHIPS):
            px, py = _peer_chip(x, y, j)
            for a in range(n):
                landed = _col_half(out_refs[a], c, lead=(2 * px + py,))
                cp = pltpu.make_async_remote_copy(
                    src_ref=landed, dst_ref=landed, send_sem=send_sems.at[a, j], recv_sem=recv_sems.at[a, j],
                    device_id=(x, y, 1 - c), device_id_type=MESH)
                cp.start()
                sends.append(cp)
        for j in range(1, N_CHIPS):
            px, py = _peer_chip(x, y, j)
            for a in range(n):
                other = _col_half(out_refs[a], 1 - c, lead=(2 * px + py,))
                pltpu.make_async_remote_copy(
                    src_ref=other, dst_ref=other, send_sem=send_sems.at[a, j], recv_sem=recv_sems.at[a, j],
                    device_id=(x, y, 1 - c), device_id_type=MESH).wait_recv()
        for cp in sends:
            cp.wait_send()
        for cp in local:
            cp.wait()

    return pl.pallas_call(
        body, name="gather_finish",
        out_shape=[_out(l.shape, l.dtype) for l in lands],
        in_specs=[any_spec] * (2 * n), out_specs=[any_spec] * n,
        input_output_aliases={i: i for i in range(n)},
        scratch_shapes=[pltpu.SemaphoreType.DMA((n, N_CHIPS))] * 2 + [pltpu.SemaphoreType.DMA((n,))],
        compiler_params=_cp(vmem_mb=16),
    )(*lands, *shards)


def _adam_math(w, g, m, v):
    m2 = ADAM_B1 * m + (1.0 - ADAM_B1) * g
    v2 = ADAM_B2 * v + (1.0 - ADAM_B2) * (g * g)
    m_hat = m2 / (1.0 - ADAM_B1 ** ADAM_STEP)
    v_hat = v2 / (1.0 - ADAM_B2 ** ADAM_STEP)
    delta = -ADAM_LR * (m_hat / (jnp.sqrt(v_hat) + ADAM_EPS) + ADAM_WD * w)
    return delta, m2, v2


def _adam_big(g_own, g_pair, w, m, v, pos):
    r, c = w.shape
    per_half = 4
    tc = c // (2 * per_half)

    def body(pos_ref, go_ref, gp_ref, w_ref, m_ref, v_ref, g_ref, d_ref, m2_ref, v2_ref):
        half = pl.program_id(0) // per_half
        g = jnp.where(half == pos_ref[0], go_ref[...], gp_ref[0])
        d, m2, v2 = _adam_math(w_ref[...], g, m_ref[...], v_ref[...])
        g_ref[...] = g
        d_ref[...] = d
        m2_ref[...] = m2
        v2_ref[...] = v2

    spec = pl.BlockSpec((r, tc), lambda i, pos: (0, i))
    grid_spec = pltpu.PrefetchScalarGridSpec(
        num_scalar_prefetch=1, grid=(2 * per_half,),
        in_specs=[pl.BlockSpec((r, tc), lambda i, pos: (0, i % per_half)),
                  pl.BlockSpec((1, r, tc), lambda i, pos: (1 - pos[0], 0, i % per_half)), spec, spec, spec],
        out_specs=[spec] * 4)
    sh = _out((r, c), F32)
    return pl.pallas_call(body, name="adam_big", grid_spec=grid_spec, out_shape=[sh] * 4,
                          compiler_params=_cp(("parallel",), 32))(pos, *_pin(g_own, g_pair, w, m, v))


def _adam_ada(c_act_t, dmod_cols, w, m, v):
    r, c = w.shape
    tc = 256

    def body(ct_ref, dm_ref, w_ref, m_ref, v_ref, g_ref, d_ref, m2_ref, v2_ref):
        g = _nn(ct_ref[...], dm_ref[...].astype(BF))
        d, m2, v2 = _adam_math(w_ref[...], g, m_ref[...], v_ref[...])
        g_ref[...] = g
        d_ref[...] = d
        m2_ref[...] = m2
        v2_ref[...] = v2

    spec = pl.BlockSpec((r, tc), lambda i: (0, i))
    sh = _out((r, c), F32)
    return pl.pallas_call(
        body, name="adam_ada", grid=(c // tc,),
        in_specs=[pl.BlockSpec(c_act_t.shape, lambda i: (0, 0)), pl.BlockSpec((dmod_cols.shape[0], tc), lambda i: (0, i)),
                  spec, spec, spec],
        out_specs=[spec] * 4, out_shape=[sh] * 4, compiler_params=_cp(("parallel",), 48))(*_pin(c_act_t, dmod_cols, w, m, v))


def _ada_mod(c_all, w_shard, b_shard, token):
    nb, d = c_all.shape
    cols = w_shard.shape[1]
    tc = 512

    def body(c_ref, w_ref, b_ref, tok_ref, mod_ref, act_ref):
        cv = c_ref[...]
        act = cv * _sigmoid(cv)
        act_ref[...] = act
        mod_ref[...] = _nn(act.astype(BF), w_ref[...].astype(BF)) + b_ref[...]

    return pl.pallas_call(
        body, name="ada_mod", grid=(cols // tc,),
        in_specs=[pl.BlockSpec((nb, d), lambda i: (0, 0)), pl.BlockSpec((d, tc), lambda i: (0, i)),
                  pl.BlockSpec((1, tc), lambda i: (0, i)), _token_spec()],
        out_specs=[pl.BlockSpec((nb, tc), lambda i: (0, i)), pl.BlockSpec((nb, d), lambda i: (0, 0))],
        out_shape=[_out((nb, cols), F32), _out((nb, d), F32)],
        compiler_params=_cp(("arbitrary",), 32))(*_pin(c_all, w_shard, b_shard, token))


SUB_ROWS = 256
ROW_TILE = 512


def _sub_rows(tm):
    return [slice(s, s + SUB_ROWS) for s in range(0, tm, SUB_ROWS)] if tm > SUB_ROWS else [slice(0, tm)]


_RESIDENT = pl.BlockSpec(memory_space=pltpu.VMEM)


def _token_spec():
    return pl.BlockSpec((8, 128), lambda *_: (0, 0))


def _in_proj(x, mod3, g_mix, w_in_t, seq, token):
    t, d = x.shape
    tm = min(ROW_TILE, seq)
    tps = seq // tm

    def body(x_ref, mod_ref, g_ref, w_ref, tok_ref, proj_ref, u1_ref):
        for rows in _sub_rows(tm):
            xv = x_ref[rows, :]
            r = lax.rsqrt(jnp.mean(xv * xv, -1, keepdims=True) + EPS)
            u = (xv * r * g_ref[...]) * (1.0 + mod_ref[0, 1:2, :]) + mod_ref[0, 0:1, :]
            ub = u.astype(BF)
            u1_ref[rows, :] = ub
            proj_ref[rows, :] = _nt(ub, w_ref[...])

    return pl.pallas_call(
        body, name="in_proj", grid=(t // tm,),
        in_specs=[pl.BlockSpec((tm, d), lambda i: (i, 0)), pl.BlockSpec((1, N_MOD, d), lambda i: (i // tps, 0, 0)),
                  pl.BlockSpec((1, d), lambda i: (0, 0)), _RESIDENT, _token_spec()],
        out_specs=[pl.BlockSpec((tm, IN_PAD), lambda i: (i, 0)), pl.BlockSpec((tm, d), lambda i: (i, 0))],
        out_shape=[_out((t, IN_PAD), F32), _out((t, d), BF)],
        compiler_params=_cp(("parallel",), 40))(*_pin(x, mod3, g_mix), w_in_t, *_pin(token))


def _pool_tile(seq):
    return min(512, seq)


def _pool_fwd(proj, w_pool, pool_scale, nb, seq):
    ts = _pool_tile(seq)
    nt = seq // ts

    def body(u_ref, halo_ref, wp_ref, ps_ref, yp_ref, p_ref):
        i = pl.program_id(1)
        halo = jnp.where(i == 0, 0.0, halo_ref[...])
        u = u_ref[...]
        ext = jnp.concatenate([halo, u], 0)
        tpos = i * ts + _iota((ts, 1), 0)
        for g, w in enumerate(POOL_WINDOWS):
            gs = slice(g * POOL_GROUP, (g + 1) * POOL_GROUP)
            s = ext[:, gs]
            sh = 1
            while sh < w:
                s = s + pltpu.roll(s, sh, 0)
                sh *= 2
            cnt = jnp.minimum(tpos + 1, w).astype(F32)
            pb = (s[HALO:] / cnt - u[:, gs]).astype(BF)
            p_ref[:, gs] = pb
            yp_ref[:, gs] = (_nn(pb, wp_ref[g].astype(BF)) * ps_ref[:, gs]).astype(BF)

    hb = ts // HALO
    return pl.pallas_call(
        body, name="pool_fwd", grid=(nb, nt),
        in_specs=[pl.BlockSpec((ts, POOL_WIDTH), lambda b, i: (b * nt + i, 0)),
                  pl.BlockSpec((HALO, POOL_WIDTH), lambda b, i: (jnp.maximum((b * nt + i) * hb - 1, 0), 0)),
                  pl.BlockSpec((4, POOL_GROUP, POOL_GROUP), lambda b, i: (0, 0, 0)),
                  pl.BlockSpec((1, POOL_WIDTH), lambda b, i: (0, 0))],
        out_specs=[pl.BlockSpec((ts, POOL_WIDTH), lambda b, i: (b * nt + i, 0))] * 2,
        out_shape=[_out((nb * seq, POOL_WIDTH), BF)] * 2,
        compiler_params=_cp(("parallel", "parallel"), 32))(*_pin(proj, proj, w_pool, pool_scale))


def _conv_pre(uxbc, halo, cw, cb, first):
    halo = jnp.where(first, 0.0, halo)
    ext = jnp.concatenate([halo, uxbc], 0)
    pre = cb + uxbc * cw[3:4]
    for k in (2, 1, 0):
        pre = pre + pltpu.roll(ext, 3 - k, 0)[CONV_HALO:] * cw[k:k + 1]
    return pre


def _chunk_terms(pre, udt, dtb, alog):
    sg = _sigmoid(pre)
    xbc = pre * sg
    dtp = udt[:, :SSD_HEADS] + dtb
    dt = jnp.maximum(dtp, 0.0) + jnp.log(1.0 + jnp.exp(-jnp.abs(dtp)))
    a = -jnp.exp(alog)
    da = dt * a
    tril = (_iota((CHUNK, CHUNK), 0) >= _iota((CHUNK, CHUNK), 1))
    acum = _exact_nn_left(tril.astype(BF), da)
    eye = (_iota((SSD_HEADS, SSD_HEADS), 0) == _iota((SSD_HEADS, SSD_HEADS), 1)).astype(BF)
    acum_t = _exact_nt_left(eye, acum)
    expand = _head_expand_matrix(SSD_HEADS, SSD_INNER)
    acum_e = _exact_nn(acum, expand)
    dt_e = _exact_nn(dt, expand)
    last_e = acum_e[CHUNK - 1:CHUNK]
    return dict(pre=pre, sg=sg, xbc=xbc, dtp=dtp, dt=dt, a=a, acum=acum, acum_t=acum_t, tril=tril,
                dt_e=dt_e, e_a=jnp.exp(acum_e), d_out=jnp.exp(last_e - acum_e), c_dec=jnp.exp(last_e))


def _head_decay(r, h):
    seg = r["acum"][:, h:h + 1] - r["acum_t"][h:h + 1, :]
    return jnp.where(r["tril"], jnp.exp(jnp.minimum(seg, 0.0)), 0.0)


SSD_SUB = 4
SSD_ROWS = SSD_SUB * CHUNK


def _ssd_specs(nb, seq, reverse):
    ns = seq // SSD_ROWS
    per = seq // CONV_HALO

    def cidx(c):
        return (ns - 1 - c) if reverse else c

    def row(b, c):
        return b * ns + cidx(c)

    specs = [
        pl.BlockSpec((SSD_ROWS, CONV_CH), lambda b, c: (row(b, c), 1)),
        pl.BlockSpec((CONV_HALO, CONV_CH),
                     lambda b, c: (jnp.maximum(b * per + cidx(c) * (SSD_ROWS // CONV_HALO) - 1, 0), 1)),
        pl.BlockSpec((SSD_ROWS, GROUP_W), lambda b, c: (row(b, c), 1)),
        pl.BlockSpec((SSD_ROWS, GROUP_W), lambda b, c: (row(b, c), 2)),
        pl.BlockSpec((SSD_ROWS, 128), lambda b, c: (row(b, c), OFF_DT // 128)),
    ]
    return specs, row, cidx, ns


def _const_spec(shape):
    return pl.BlockSpec(shape, lambda b, c: (0,) * len(shape))


def _ssd_fwd(proj, conv_w, conv_b, dt_bias, a_log, dskip_e, g_ssd, nb, seq):
    specs, row, cidx, ns = _ssd_specs(nb, seq, reverse=False)

    def body(uxbc_ref, halo_ref, z0_ref, z1_ref, udt_ref, cw_ref, cb_ref, dtb_ref, alog_ref, dsk_ref, gs_ref,
             yssd_ref, yssm_ref, hprev_ref, pre_ref, h_ref, yd_ref):
        c = pl.program_id(1)

        @pl.when(c == 0)
        def _():
            h_ref[...] = jnp.zeros_like(h_ref)

        for sub in range(SSD_SUB):
            rows = slice(sub * CHUNK, (sub + 1) * CHUNK)
            if sub == 0:
                halo, first = halo_ref[...], c == 0
            else:
                halo, first = uxbc_ref[sub * CHUNK - CONV_HALO:sub * CHUNK, :], False
            pre = _conv_pre(uxbc_ref[rows, :], halo, cw_ref[...], cb_ref[...], first)
            pre_ref[rows, :] = pre
            r = _chunk_terms(pre, udt_ref[rows, :], dtb_ref[...], alog_ref[...])
            xbc = r["xbc"]
            xs = xbc[:, :SSD_INNER]
            xdt = xs * r["dt_e"]
            xdt_b = xdt.astype(BF)
            xdo_b = (xdt * r["d_out"]).astype(BF)
            hprev_ref[0, sub] = h_ref[...]
            for g in range(2):
                gs = slice(g * GROUP_W, (g + 1) * GROUP_W)
                bg = xbc[:, SSD_INNER + g * SSD_STATE:SSD_INNER + (g + 1) * SSD_STATE].astype(BF)
                cg = xbc[:, SSD_INNER + (2 + g) * SSD_STATE:SSD_INNER + (3 + g) * SSD_STATE].astype(BF)
                scores = _nt(cg, bg)
                hg = h_ref[g]
                y_off = _nn(cg, hg.astype(BF)) * r["e_a"][:, gs]
                for hh in range(8):
                    h = g * 8 + hh
                    hs = slice(h * SSD_HEAD_DIM, (h + 1) * SSD_HEAD_DIM)
                    m = (scores * _head_decay(r, h)).astype(BF)
                    yd_ref[sub, :, hs] = _nn(m, xdt_b[:, hs])
                h_ref[g] = hg * r["c_dec"][:, gs] + _tn(bg, xdo_b[:, gs])
                y = yd_ref[sub, :, gs] + y_off + dsk_ref[:, gs] * xs[:, gs]
                yssm_ref[rows, gs] = y
                zg = (z0_ref if g == 0 else z1_ref)[rows, :]
                yg = y * (zg * _sigmoid(zg))
                rg = lax.rsqrt(jnp.mean(yg * yg, -1, keepdims=True) + EPS)
                yssd_ref[rows, gs] = (yg * rg * gs_ref[:, gs]).astype(BF)

    t = nb * seq
    return pl.pallas_call(
        body, name="ssd_fwd", grid=(nb, ns),
        in_specs=specs + [_const_spec((4, CONV_CH)), _const_spec((1, CONV_CH)), _const_spec((1, SSD_HEADS)),
                          _const_spec((1, SSD_HEADS)), _const_spec((1, SSD_INNER)), _const_spec((1, SSD_INNER))],
        out_specs=[pl.BlockSpec((SSD_ROWS, SSD_INNER), lambda b, c: (row(b, c), 0)),
                   pl.BlockSpec((SSD_ROWS, SSD_INNER), lambda b, c: (row(b, c), 0)),
                   pl.BlockSpec((1, SSD_SUB, 2, SSD_STATE, GROUP_W), lambda b, c: (b, c, 0, 0, 0)),
                   pl.BlockSpec((SSD_ROWS, CONV_CH), lambda b, c: (row(b, c), 0))],
        out_shape=[_out((t, SSD_INNER), BF), _out((t, SSD_INNER), F32),
                   _out((nb, seq // CHUNK, 2, SSD_STATE, GROUP_W), F32), _out((t, CONV_CH), F32)],
        scratch_shapes=[pltpu.VMEM((2, SSD_STATE, GROUP_W), F32), pltpu.VMEM((SSD_SUB, CHUNK, SSD_INNER), F32)],
        compiler_params=_cp(("arbitrary", "arbitrary"), 56),
    )(*_pin(proj, proj, proj, proj, proj, conv_w, conv_b, dt_bias, a_log, dskip_e, g_ssd))


def _out_proj(y_pool, y_ssd, w_out, x, mod3, g_mlp, seq):
    t, d = x.shape
    tm = 512
    tps = seq // tm if seq >= tm else 1
    tm = min(tm, seq)

    def body(yp_ref, ys_ref, w_ref, x_ref, mod_ref, g_ref, h1_ref, o_ref, u2_ref):
        for rows in _sub_rows(tm):
            o = _nn(jnp.concatenate([yp_ref[rows, :], ys_ref[rows, :]], 1), w_ref[...])
            o_ref[rows, :] = o.astype(BF)
            h1 = x_ref[rows, :] + mod_ref[0, 2:3, :] * o
            h1_ref[rows, :] = h1
            r = lax.rsqrt(jnp.mean(h1 * h1, -1, keepdims=True) + EPS)
            u2_ref[rows, :] = ((h1 * r * g_ref[...]) * (1.0 + mod_ref[0, 4:5, :]) + mod_ref[0, 3:4, :]).astype(BF)

    row = lambda i: (i, 0)
    return pl.pallas_call(
        body, name="out_proj", grid=(t // tm,),
        in_specs=[pl.BlockSpec((tm, POOL_WIDTH), row), pl.BlockSpec((tm, SSD_INNER), row),
                  _RESIDENT, pl.BlockSpec((tm, d), row),
                  pl.BlockSpec((1, N_MOD, d), lambda i: (i // tps, 0, 0)), pl.BlockSpec((1, d), lambda i: (0, 0))],
        out_specs=[pl.BlockSpec((tm, d), row)] * 3,
        out_shape=[_out((t, d), F32), _out((t, d), BF), _out((t, d), BF)],
        compiler_params=_cp(("parallel",), 48))(*_pin(y_pool, y_ssd), w_out, *_pin(x, mod3, g_mlp))


def _mlp_up(u2, w_up4):
    t, d = u2.shape
    tm = min(1024, t)
    nk, _, cols = w_up4.shape

    def body(u_ref, w_ref, a_ref):
        a_ref[...] = _nn(u_ref[...], w_ref[pl.program_id(1)]).astype(BF)

    return pl.pallas_call(
        body, name="mlp_up", grid=(t // tm, nk),
        in_specs=[pl.BlockSpec((tm, d), lambda i, k: (i, 0)), _RESIDENT],
        out_specs=pl.BlockSpec((tm, cols), lambda i, k: (i, k)),
        out_shape=_out((t, nk * cols), BF),
        compiler_params=_cp(("parallel", "parallel"), 32))(*_pin(u2), w_up4)


def _mlp_down_loss(a_up, w_down, h1, mod3, g_final, target, seq):
    t, d = h1.shape
    nb = t // seq
    tm = min(ROW_TILE, seq)
    tps = seq // tm

    def body(a_ref, w_ref, h1_ref, mod_ref, g_ref, tg_ref, ddn_ref, dh2_ref, sq_ref, gg_ref, dgf_ref):
        i = pl.program_id(0)

        @pl.when(i == 0)
        def _():
            sq_ref[...] = jnp.zeros_like(sq_ref)
            gg_ref[...] = jnp.zeros_like(gg_ref)

        @pl.when(i % tps == 0)
        def _():
            dgf_ref[...] = jnp.zeros_like(dgf_ref)

        gate = mod_ref[0, 5:6, :]
        sq = gg = dgf = 0.0
        for rows in _sub_rows(tm):
            f = jnp.square(jnp.maximum(a_ref[rows, :], 0))
            dn = _nn(f, w_ref[...])
            h2 = h1_ref[rows, :] + gate * dn
            r = lax.rsqrt(jnp.mean(h2 * h2, -1, keepdims=True) + EPS)
            hh = h2 * r
            err = hh * g_ref[...] - tg_ref[rows, :]
            dy = err * (1.0 / d)
            dhat = dy * g_ref[...]
            dh2 = r * (dhat - hh * jnp.mean(dhat * hh, -1, keepdims=True))
            dh2_ref[rows, :] = dh2
            ddn_ref[rows, :] = (dh2 * gate).astype(BF)
            sq = sq + jnp.sum(err * err, 0, keepdims=True)
            gg = gg + jnp.sum(dy * hh, 0, keepdims=True)
            dgf = dgf + jnp.sum(dh2 * dn, 0, keepdims=True)
        sq_ref[...] += sq
        gg_ref[...] += gg
        dgf_ref[0] += dgf

    row = lambda i: (i, 0)
    vec = pl.BlockSpec((1, d), lambda i: (0, 0))
    return pl.pallas_call(
        body, name="mlp_down_loss", grid=(t // tm,),
        in_specs=[pl.BlockSpec((tm, D_FF), row), _RESIDENT, pl.BlockSpec((tm, d), row),
                  pl.BlockSpec((1, N_MOD, d), lambda i: (i // tps, 0, 0)), vec, pl.BlockSpec((tm, d), row)],
        out_specs=[pl.BlockSpec((tm, d), row), pl.BlockSpec((tm, d), row), vec, vec,
                   pl.BlockSpec((1, 1, d), lambda i: (i // tps, 0, 0))],
        out_shape=[_out((t, d), BF), _out((t, d), F32), _out((1, d), F32),
                   _out((1, d), F32), _out((nb, 1, d), F32)],
        compiler_params=_cp(("arbitrary",), 44))(*_pin(a_up), w_down, *_pin(h1, mod3, g_final, target))


def _tn_matmul(a, b, tk, tn, name, square_relu=False, out3=False):
    t, kdim = a.shape
    ndim = b.shape[1]

    def body(a_ref, b_ref, o_ref):
        av = a_ref[...]
        if square_relu:
            av = jnp.square(jnp.maximum(av, 0))
        res = _tn(av, b_ref[...]).astype(BF)
        if out3:
            o_ref[0] = res
        else:
            o_ref[...] = res

    if out3:
        out_spec = pl.BlockSpec((1, tk, tn), lambda j, i: (j, i, 0))
        out_shape = _out((ndim // tn, kdim, tn), BF)
    else:
        out_spec = pl.BlockSpec((tk, tn), lambda j, i: (i, j))
        out_shape = _out((kdim, ndim), BF)
    return pl.pallas_call(
        body, name=name, grid=(ndim // tn, kdim // tk),
        in_specs=[pl.BlockSpec((t, tk), lambda j, i: (0, i)), pl.BlockSpec((t, tn), lambda j, i: (0, j))],
        out_specs=out_spec, out_shape=out_shape,
        compiler_params=_cp(("parallel", "parallel"), 56))(*_pin(a, b))


def _mlp_down_bwd(d_dn, w_down4, a_up, token):
    t, d = d_dn.shape
    tm = min(1024, t)
    nk, rows, _ = w_down4.shape

    def body(g_ref, w_ref, a_ref, tok_ref, o_ref):
        df = _nt(g_ref[...], w_ref[pl.program_id(1)])
        o_ref[...] = (df * (2.0 * jnp.maximum(a_ref[...], 0).astype(F32))).astype(BF)

    return pl.pallas_call(
        body, name="mlp_down_bwd", grid=(t // tm, nk),
        in_specs=[pl.BlockSpec((tm, d), lambda i, k: (i, 0)), _RESIDENT,
                  pl.BlockSpec((tm, rows), lambda i, k: (i, k)), _token_spec()],
        out_specs=pl.BlockSpec((tm, rows), lambda i, k: (i, k)),
        out_shape=_out((t, nk * rows), BF),
        compiler_params=_cp(("parallel", "parallel"), 32))(*_pin(d_dn), w_down4, *_pin(a_up, token))


def _mlp_up_bwd(d_a, w_up4, h1, dh2, o, mod3, g_mlp, seq, token):
    t, d = h1.shape
    nb = t // seq
    tm = min(ROW_TILE, seq)
    tps = seq // tm
    nk = w_up4.shape[0]
    cols = w_up4.shape[2]

    def body(da_ref, w_ref, h1_ref, dh2_ref, o_ref, mod_ref, g_ref, tok_ref, dh1_ref, do_ref, acc_ref, gg_ref):
        i = pl.program_id(0)

        @pl.when(i == 0)
        def _():
            gg_ref[...] = jnp.zeros_like(gg_ref)

        @pl.when(i % tps == 0)
        def _():
            acc_ref[...] = jnp.zeros_like(acc_ref)

        gg = a_shift = a_scale = a_gate = 0.0
        for rows in _sub_rows(tm):
            du = _nt(da_ref[rows, 0:cols], w_ref[0])
            for k in range(1, nk):
                du = du + _nt(da_ref[rows, k * cols:(k + 1) * cols], w_ref[k])
            h1 = h1_ref[rows, :]
            r = lax.rsqrt(jnp.mean(h1 * h1, -1, keepdims=True) + EPS)
            hh = h1 * r
            n2 = hh * g_ref[...]
            dn2 = du * (1.0 + mod_ref[0, 4:5, :])
            dhat = dn2 * g_ref[...]
            dh1 = dh2_ref[rows, :] + r * (dhat - hh * jnp.mean(dhat * hh, -1, keepdims=True))
            dh1_ref[rows, :] = dh1
            do_ref[rows, :] = (dh1 * mod_ref[0, 2:3, :]).astype(BF)
            gg = gg + jnp.sum(dn2 * hh, 0, keepdims=True)
            a_shift = a_shift + jnp.sum(du, 0, keepdims=True)
            a_scale = a_scale + jnp.sum(du * n2, 0, keepdims=True)
            a_gate = a_gate + jnp.sum(dh1 * o_ref[rows, :].astype(F32), 0, keepdims=True)
        gg_ref[...] += gg
        acc_ref[0, 0:1, :] += a_shift
        acc_ref[0, 1:2, :] += a_scale
        acc_ref[0, 2:3, :] += a_gate

    row = lambda i: (i, 0)
    vec = pl.BlockSpec((1, d), lambda i: (0, 0))
    return pl.pallas_call(
        body, name="mlp_up_bwd", grid=(t // tm,),
        in_specs=[pl.BlockSpec((tm, D_FF), row), _RESIDENT, pl.BlockSpec((tm, d), row),
                  pl.BlockSpec((tm, d), row), pl.BlockSpec((tm, d), row),
                  pl.BlockSpec((1, N_MOD, d), lambda i: (i // tps, 0, 0)), vec, _token_spec()],
        out_specs=[pl.BlockSpec((tm, d), row), pl.BlockSpec((tm, d), row),
                   pl.BlockSpec((1, 8, d), lambda i: (i // tps, 0, 0)), vec],
        out_shape=[_out((t, d), F32), _out((t, d), BF),
                   _out((nb, 8, d), F32), _out((1, d), F32)],
        compiler_params=_cp(("arbitrary",), 44))(*_pin(d_a), w_up4, *_pin(h1, dh2, o, mod3, g_mlp, token))


def _out_proj_bwd(d_o, w_out, token):
    t, d = d_o.shape
    tm = min(512, t)

    def body(g_ref, w_ref, tok_ref, dp_ref, ds_ref):
        gv = g_ref[...]
        dp_ref[...] = _nt(gv, w_ref[0:POOL_WIDTH, :])
        ds_ref[...] = _nt(gv, w_ref[POOL_WIDTH:, :])

    row = lambda i: (i, 0)
    return pl.pallas_call(
        body, name="out_proj_bwd", grid=(t // tm,),
        in_specs=[pl.BlockSpec((tm, d), row), _RESIDENT, _token_spec()],
        out_specs=[pl.BlockSpec((tm, POOL_WIDTH), row), pl.BlockSpec((tm, SSD_INNER), row)],
        out_shape=[_out((t, POOL_WIDTH), F32), _out((t, SSD_INNER), F32)],
        compiler_params=_cp(("parallel",), 32))(*_pin(d_o), w_out, *_pin(token))


def _pool_bwd(d_ypool, p, w_pool, pool_scale, nb, seq):
    ts = _pool_tile(seq)
    nt = seq // ts
    hb = ts // HALO
    last_block = nb * seq // HALO - 1

    def body(dy_ref, halo_ref, p_ref, wp_ref, ps_ref, du_ref, gw_ref, gs_ref):
        b = pl.program_id(0)
        i = pl.program_id(1)

        @pl.when((b == 0) & (i == 0))
        def _():
            gw_ref[...] = jnp.zeros_like(gw_ref)
            gs_ref[...] = jnp.zeros_like(gs_ref)

        halo = jnp.where(i == nt - 1, 0.0, halo_ref[...])
        dy = dy_ref[...]
        ext = jnp.concatenate([dy, halo], 0)
        tpos = i * ts + _iota((ts + HALO, 1), 0)
        n_ext = ts + HALO
        for g, w in enumerate(POOL_WINDOWS):
            gs = slice(g * POOL_GROUP, (g + 1) * POOL_GROUP)
            wg = wp_ref[g].astype(BF)
            pg = p_ref[:, gs]
            pw = _nn(pg, wg)
            gs_ref[:, gs] += jnp.sum(dy[:, gs] * pw, 0, keepdims=True)
            dpw = (ext[:, gs] * ps_ref[:, gs]).astype(BF)
            gw_ref[g] += _tn(pg, dpw[:ts])
            dp = _nt(dpw, wg)
            cnt = jnp.minimum(tpos + 1, w).astype(F32)
            s = dp / cnt
            sh = 1
            while sh < w:
                s = s + pltpu.roll(s, n_ext - sh, 0)
                sh *= 2
            du_ref[:, gs] = (s[:ts] - dp[:ts]).astype(BF)

    return pl.pallas_call(
        body, name="pool_bwd", grid=(nb, nt),
        in_specs=[pl.BlockSpec((ts, POOL_WIDTH), lambda b, i: (b * nt + i, 0)),
                  pl.BlockSpec((HALO, POOL_WIDTH), lambda b, i: (jnp.minimum((b * nt + i + 1) * hb, last_block), 0)),
                  pl.BlockSpec((ts, POOL_WIDTH), lambda b, i: (b * nt + i, 0)),
                  pl.BlockSpec((4, POOL_GROUP, POOL_GROUP), lambda b, i: (0, 0, 0)),
                  pl.BlockSpec((1, POOL_WIDTH), lambda b, i: (0, 0))],
        out_specs=[pl.BlockSpec((ts, POOL_WIDTH), lambda b, i: (b * nt + i, 0)),
                   pl.BlockSpec((4, POOL_GROUP, POOL_GROUP), lambda b, i: (0, 0, 0)),
                   pl.BlockSpec((1, POOL_WIDTH), lambda b, i: (0, 0))],
        out_shape=[_out((nb * seq, POOL_WIDTH), BF), _out((4, POOL_GROUP, POOL_GROUP), F32),
                   _out((1, POOL_WIDTH), F32)],
        compiler_params=_cp(("arbitrary", "arbitrary"), 32))(*_pin(d_ypool, d_ypool, p, w_pool, pool_scale))


def _ssd_bwd(proj, pre, d_yssd, yssm, h_prev, dt_bias, a_log, dskip_e, g_ssd, nb, seq):
    specs, row, cidx, ns = _ssd_specs(nb, seq, reverse=True)
    specs = specs[2:]

    def body(z0_ref, z1_ref, udt_ref, pre_ref, dys_ref, yssm_ref, hprev_ref,
             dtb_ref, alog_ref, dsk_ref, gs_ref,
             dz_ref, dpre_ref, dudt_ref, ggs_ref, gdsk_ref, ga_ref, gdtb_ref,
             g_ref, dxdt_ref, dyv_ref):
        b = pl.program_id(0)
        c = pl.program_id(1)

        @pl.when(c == 0)
        def _():
            g_ref[...] = jnp.zeros_like(g_ref)

        @pl.when((b == 0) & (c == 0))
        def _():
            ggs_ref[...] = jnp.zeros_like(ggs_ref)
            gdsk_ref[...] = jnp.zeros_like(gdsk_ref)
            ga_ref[...] = jnp.zeros_like(ga_ref)
            gdtb_ref[...] = jnp.zeros_like(gdtb_ref)

        for sub in reversed(range(SSD_SUB)):
            chunk(sub, z0_ref, z1_ref, udt_ref, pre_ref, dys_ref, yssm_ref, hprev_ref, dtb_ref, alog_ref, dsk_ref, gs_ref,
                  dz_ref, dpre_ref, dudt_ref, ggs_ref, gdsk_ref, ga_ref, gdtb_ref, g_ref, dxdt_ref.at[sub], dyv_ref.at[sub])

    def chunk(sub, z0_ref, z1_ref, udt_ref, pre_ref, dys_ref, yssm_ref, hprev_ref,
              dtb_ref, alog_ref, dsk_ref, gs_ref,
              dz_ref, dpre_ref, dudt_ref, ggs_ref, gdsk_ref, ga_ref, gdtb_ref,
              g_ref, dxdt_ref, dyv_ref):
        rows = slice(sub * CHUNK, (sub + 1) * CHUNK)
        r = _chunk_terms(pre_ref[rows, :], udt_ref[rows, :], dtb_ref[...], alog_ref[...])
        xbc = r["xbc"]
        xs = xbc[:, :SSD_INNER]
        dt_e = r["dt_e"]
        xdt = xs * dt_e
        xdt_b = xdt.astype(BF)
        reduce_m = _head_reduce_matrix(GROUP_W, 8)

        def head_sums(v):
            return _nn(v.astype(BF), reduce_m)

        onehot16 = lambda h: (_iota((1, SSD_HEADS), 1) == h).astype(F32)
        onecol16 = lambda h: (_iota((SSD_HEADS, 1), 0) == h).astype(F32)

        d_acum = jnp.zeros((CHUNK, SSD_HEADS), F32)
        d_acum_t = jnp.zeros((SSD_HEADS, CHUNK), F32)
        d_alast = jnp.zeros((1, SSD_HEADS), F32)
        place8 = lambda g: (_iota((8, SSD_HEADS), 1) == _iota((8, SSD_HEADS), 0) + 8 * g).astype(BF)
        d_b, d_c = [], []
        for g in range(2):
            gs = slice(g * GROUP_W, (g + 1) * GROUP_W)
            zg = (z0_ref if g == 0 else z1_ref)[rows, :]
            sz = _sigmoid(zg)
            silu_z = zg * sz
            ys = yssm_ref[rows, gs]
            yg = ys * silu_z
            rg = lax.rsqrt(jnp.mean(yg * yg, -1, keepdims=True) + EPS)
            yh = yg * rg
            dys = dys_ref[rows, gs]
            ggs_ref[:, gs] += jnp.sum(dys * yh, 0, keepdims=True)
            dyh = dys * gs_ref[:, gs]
            dyg = rg * (dyh - yh * jnp.mean(dyh * yh, -1, keepdims=True))
            dy = dyg * silu_z
            dz_ref[rows, gs] = (dyg * ys * (sz * (1.0 + zg * (1.0 - sz)))).astype(BF)
            gdsk_ref[:, gs] += jnp.sum(dy * xs[:, gs], 0, keepdims=True)
            dyv_ref[:, gs] = dy
            dy_b = dy.astype(BF)

            bg = xbc[:, SSD_INNER + g * SSD_STATE:SSD_INNER + (g + 1) * SSD_STATE].astype(BF)
            cg = xbc[:, SSD_INNER + (2 + g) * SSD_STATE:SSD_INNER + (3 + g) * SSD_STATE].astype(BF)
            scores = _nt(cg, bg)
            hg = hprev_ref[0, sub, g]
            hg_b = hg.astype(BF)
            gg = g_ref[g]
            gg_b = gg.astype(BF)
            e_a = r["e_a"][:, gs]
            d_out = r["d_out"][:, gs]
            c_dec = r["c_dec"][:, gs]
            zc = _nn(cg, hg_b)
            wv = e_a * dy
            wv_b = wv.astype(BF)
            da_g = head_sums(wv * zc)
            dcg = _nt(wv_b, hg_b)
            d_hprev = _tn(cg, wv_b)
            vg = _nn(bg, gg_b)
            dxdt_g = d_out * vg
            dd_out = head_sums(xdt[:, gs] * vg)
            dbg = _nt((xdt[:, gs] * d_out).astype(BF), gg_b)
            dcd = _exact_nn(jnp.sum(gg * hg, 0, keepdims=True), reduce_m)
            d_out8 = jnp.exp(r["acum"][CHUNK - 1:CHUNK, 8 * g:8 * g + 8] - r["acum"][:, 8 * g:8 * g + 8])
            c_dec8 = jnp.exp(r["acum"][CHUNK - 1:CHUNK, 8 * g:8 * g + 8])
            t8 = dd_out * d_out8
            d_alast = d_alast + _exact_nn(jnp.sum(t8, 0, keepdims=True) + dcd * c_dec8, place8(g))
            d_acum = d_acum + _exact_nn(da_g - t8, place8(g))
            dsc = jnp.zeros((CHUNK, CHUNK), F32)
            for hh in range(8):
                h = g * 8 + hh
                hs = slice(h * SSD_HEAD_DIM, (h + 1) * SSD_HEAD_DIM)
                lam = _head_decay(r, h)
                m = scores * lam
                dyh_b = dy_b[:, hh * SSD_HEAD_DIM:(hh + 1) * SSD_HEAD_DIM]
                dm = _nt(dyh_b, xdt_b[:, hs])
                tm_ = dm * m
                d_acum = d_acum + jnp.sum(tm_, 1, keepdims=True) * onehot16(h)
                d_acum_t = d_acum_t + onecol16(h) * jnp.sum(tm_, 0, keepdims=True)
                dsc = dsc + dm * lam
                dxdt_ref[:, hs] = _tn(m.astype(BF), dyh_b) + dxdt_g[:, hh * SSD_HEAD_DIM:(hh + 1) * SSD_HEAD_DIM]
            dsc_b = dsc.astype(BF)
            d_c.append(dcg + _nn(dsc_b, bg))
            d_b.append(dbg + _tn(dsc_b, cg))
            g_ref[g] = d_hprev + c_dec * gg

        eye = (_iota((CHUNK, CHUNK), 0) == _iota((CHUNK, CHUNK), 1)).astype(BF)
        d_acum = d_acum - _exact_nt_left(eye, d_acum_t)
        is_last = (_iota((CHUNK, 1), 0) == CHUNK - 1).astype(F32)
        d_acum = d_acum + is_last * d_alast
        triu = (_iota((CHUNK, CHUNK), 0) <= _iota((CHUNK, CHUNK), 1)).astype(BF)
        d_da = _exact_nn_left(triu, d_acum)
        dt = r["dt"]
        ga_ref[...] += jnp.sum(d_da * dt, 0, keepdims=True)
        dxdt = dxdt_ref[...]
        reduce16 = _head_reduce_matrix(SSD_INNER, SSD_HEADS)
        d_dt = d_da * r["a"] + _nn((dxdt * xs).astype(BF), reduce16)
        d_udt = d_dt * _sigmoid(r["dtp"])
        gdtb_ref[...] += jnp.sum(d_udt, 0, keepdims=True)
        dudt_ref[rows, :] = jnp.zeros((CHUNK, dudt_ref.shape[1]), BF)
        dudt_ref[rows, 0:SSD_HEADS] = d_udt.astype(BF)
        pre, sg = r["pre"], r["sg"]
        dsilu = sg * (1.0 + pre * (1.0 - sg))
        dpre_ref[rows, 0:SSD_INNER] = (dsk_ref[...] * dyv_ref[...] + dxdt * dt_e) * dsilu[:, 0:SSD_INNER]
        for g in range(2):
            bs = slice(SSD_INNER + g * SSD_STATE, SSD_INNER + (g + 1) * SSD_STATE)
            cs = slice(SSD_INNER + (2 + g) * SSD_STATE, SSD_INNER + (3 + g) * SSD_STATE)
            dpre_ref[rows, bs] = d_b[g] * dsilu[:, bs]
            dpre_ref[rows, cs] = d_c[g] * dsilu[:, cs]

    t = nb * seq
    vec = _const_spec((1, SSD_INNER))
    small = _const_spec((1, SSD_HEADS))
    return pl.pallas_call(
        body, name="ssd_bwd", grid=(nb, ns),
        in_specs=specs + [pl.BlockSpec((SSD_ROWS, CONV_CH), lambda b, c: (row(b, c), 0)),
                          pl.BlockSpec((SSD_ROWS, SSD_INNER), lambda b, c: (row(b, c), 0)),
                          pl.BlockSpec((SSD_ROWS, SSD_INNER), lambda b, c: (row(b, c), 0)),
                          pl.BlockSpec((1, SSD_SUB, 2, SSD_STATE, GROUP_W), lambda b, c: (b, cidx(c), 0, 0, 0)),
                          small, small, vec, vec],
        out_specs=[pl.BlockSpec((SSD_ROWS, SSD_INNER), lambda b, c: (row(b, c), 0)),
                   pl.BlockSpec((SSD_ROWS, CONV_CH), lambda b, c: (row(b, c), 0)),
                   pl.BlockSpec((SSD_ROWS, 128), lambda b, c: (row(b, c), 0)),
                   vec, vec, small, small],
        out_shape=[_out((t, SSD_INNER), BF), _out((t, CONV_CH), F32),
                   _out((t, 128), BF), _out((1, SSD_INNER), F32),
                   _out((1, SSD_INNER), F32), _out((1, SSD_HEADS), F32),
                   _out((1, SSD_HEADS), F32)],
        scratch_shapes=[pltpu.VMEM((2, SSD_STATE, GROUP_W), F32), pltpu.VMEM((SSD_SUB, CHUNK, SSD_INNER), F32),
                        pltpu.VMEM((SSD_SUB, CHUNK, SSD_INNER), F32)],
        compiler_params=_cp(("arbitrary", "arbitrary"), 56),
    )(*_pin(proj, proj, proj, pre, d_yssd, yssm, h_prev, dt_bias, a_log, dskip_e, g_ssd))


def _grad_w_out(y_pool, y_ssd, d_o):
    t, d = d_o.shape
    tk = POOL_WIDTH
    n_s = SSD_INNER // tk

    def body(p_ref, s_ref, g_ref, o_ref):
        i = pl.program_id(0)

        @pl.when(i == 0)
        def _():
            o_ref[...] = _tn(p_ref[...], g_ref[...]).astype(BF)

        @pl.when(i > 0)
        def _():
            o_ref[...] = _tn(s_ref[...], g_ref[...]).astype(BF)

    return pl.pallas_call(
        body, name="grad_w_out", grid=(1 + n_s,),
        in_specs=[pl.BlockSpec((t, tk), lambda i: (0, 0)), pl.BlockSpec((t, tk), lambda i: (0, jnp.maximum(i - 1, 0))),
                  pl.BlockSpec((t, d), lambda i: (0, 0))],
        out_specs=pl.BlockSpec((tk, d), lambda i: (i, 0)),
        out_shape=_out((POOL_WIDTH + SSD_INNER, d), BF),
        compiler_params=_cp(("parallel",), 56))(*_pin(y_pool, y_ssd, d_o))


def _grad_w_in_t(d_upool, d_z, d_uxbc, d_udt, u1):
    t, d = u1.shape
    tk = 512
    n_z, n_x = SSD_INNER // tk, CONV_CH // tk

    def body(p_ref, z_ref, x_ref, dt_ref, u_ref, o_ref):
        i = pl.program_id(0)

        @pl.when(i == 0)
        def _():
            o_ref[...] = _tn(p_ref[...], u_ref[...]).astype(BF)

        @pl.when((i >= 1) & (i < 1 + n_z))
        def _():
            o_ref[...] = _tn(z_ref[...], u_ref[...]).astype(BF)

        @pl.when((i >= 1 + n_z) & (i < 1 + n_z + n_x))
        def _():
            o_ref[...] = _tn(x_ref[...], u_ref[...]).astype(BF)

        @pl.when(i == 1 + n_z + n_x)
        def _():
            o_ref[0:128, :] = _tn(dt_ref[...], u_ref[...]).astype(BF)

    return pl.pallas_call(
        body, name="grad_w_in", grid=(2 + n_z + n_x,),
        in_specs=[pl.BlockSpec((t, tk), lambda i: (0, 0)),
                  pl.BlockSpec((t, tk), lambda i: (0, jnp.clip(i - 1, 0, n_z - 1))),
                  pl.BlockSpec((t, tk), lambda i: (0, jnp.clip(i - 1 - n_z, 0, n_x - 1))),
                  pl.BlockSpec((t, 128), lambda i: (0, 0)), pl.BlockSpec((t, d), lambda i: (0, 0))],
        out_specs=pl.BlockSpec((tk, d), lambda i: (i, 0)),
        out_shape=_out((IN_PAD, d), BF),
        compiler_params=_cp(("parallel",), 56))(*_pin(d_upool, d_z, d_uxbc, d_udt, u1))


def _conv_bwd(d_pre, proj, conv_w, nb, seq):
    ts = min(256, seq)
    nt = seq // ts
    hb = ts // CONV_HALO
    last_block = nb * seq // CONV_HALO - 1
    n_ext = CHUNK + CONV_HALO

    def body(dp_ref, dnext_ref, u_ref, cw_ref, du_ref, gw_ref, gb_ref):
        b = pl.program_id(0)
        i = pl.program_id(1)

        @pl.when((b == 0) & (i == 0))
        def _():
            gw_ref[...] = jnp.zeros_like(gw_ref)
            gb_ref[...] = jnp.zeros_like(gb_ref)

        for c0 in range(0, CONV_CH, 128):
            cs = slice(c0, c0 + 128)
            cw = cw_ref[:, cs]
            gw = [0.0] * 4
            gb = 0.0
            for r0 in range(0, ts, CHUNK):
                dp = dp_ref[r0:r0 + CHUNK, cs]
                u = u_ref[r0:r0 + CHUNK, cs]
                if r0 + CHUNK < ts:
                    below = dp_ref[r0 + CHUNK:r0 + CHUNK + CONV_HALO, cs]
                else:
                    below = jnp.where(i == nt - 1, 0.0, dnext_ref[:, cs])
                ext_d = jnp.concatenate([dp, below], 0)
                du = dp * cw[3:4]
                gw[3] = gw[3] + jnp.sum(dp * u, 0, keepdims=True)
                for k in (2, 1, 0):
                    shifted = pltpu.roll(ext_d, n_ext - (3 - k), 0)[:CHUNK]
                    du = du + shifted * cw[k:k + 1]
                    gw[k] = gw[k] + jnp.sum(shifted * u, 0, keepdims=True)
                gb = gb + jnp.sum(dp, 0, keepdims=True)
                du_ref[r0:r0 + CHUNK, cs] = du.astype(BF)
            for k in range(4):
                gw_ref[k:k + 1, cs] += gw[k]
            gb_ref[:, cs] += gb

    return pl.pallas_call(
        body, name="conv_bwd", grid=(nb, nt),
        in_specs=[pl.BlockSpec((ts, CONV_CH), lambda b, i: (b * nt + i, 0)),
                  pl.BlockSpec((CONV_HALO, CONV_CH), lambda b, i: (jnp.minimum((b * nt + i + 1) * hb, last_block), 0)),
                  pl.BlockSpec((ts, CONV_CH), lambda b, i: (b * nt + i, 1)),
                  pl.BlockSpec((4, CONV_CH), lambda b, i: (0, 0))],
        out_specs=[pl.BlockSpec((ts, CONV_CH), lambda b, i: (b * nt + i, 0)),
                   pl.BlockSpec((8, CONV_CH), lambda b, i: (0, 0)), pl.BlockSpec((1, CONV_CH), lambda b, i: (0, 0))],
        out_shape=[_out((nb * seq, CONV_CH), BF), _out((8, CONV_CH), F32),
                   _out((1, CONV_CH), F32)],
        compiler_params=_cp(("arbitrary", "arbitrary"), 48))(*_pin(d_pre, d_pre, proj, conv_w))


def _in_proj_bwd(d_parts, w_in_t, x, dh1, mod3, g_mix, seq, token):
    t, d = x.shape
    nb = t // seq
    tm = min(ROW_TILE, seq)
    tps = seq // tm

    widths = [p.shape[1] for p in d_parts]

    def body(d0_ref, d1_ref, d2_ref, d3_ref, w_ref, x_ref, dh1_ref, mod_ref, g_ref, tok_ref, gx_ref, acc_ref, gg_ref):
        i = pl.program_id(0)

        @pl.when(i == 0)
        def _():
            gg_ref[...] = jnp.zeros_like(gg_ref)

        @pl.when(i % tps == 0)
        def _():
            acc_ref[...] = jnp.zeros_like(acc_ref)

        gg = a_shift = a_scale = 0.0
        for rows in _sub_rows(tm):
            d_cat = jnp.concatenate([p_ref[rows, :] for p_ref in (d0_ref, d1_ref, d2_ref, d3_ref)], 1)
            du = _nn(d_cat, w_ref[...])
            xv = x_ref[rows, :]
            r = lax.rsqrt(jnp.mean(xv * xv, -1, keepdims=True) + EPS)
            hh = xv * r
            n1 = hh * g_ref[...]
            dn1 = du * (1.0 + mod_ref[0, 1:2, :])
            dhat = dn1 * g_ref[...]
            gx_ref[rows, :] = dh1_ref[rows, :] + r * (dhat - hh * jnp.mean(dhat * hh, -1, keepdims=True))
            gg = gg + jnp.sum(dn1 * hh, 0, keepdims=True)
            a_shift = a_shift + jnp.sum(du, 0, keepdims=True)
            a_scale = a_scale + jnp.sum(du * n1, 0, keepdims=True)
        gg_ref[...] += gg
        acc_ref[0, 0:1, :] += a_shift
        acc_ref[0, 1:2, :] += a_scale

    row = lambda i: (i, 0)
    vec = pl.BlockSpec((1, d), lambda i: (0, 0))
    return pl.pallas_call(
        body, name="in_proj_bwd", grid=(t // tm,),
        in_specs=[pl.BlockSpec((tm, wd), row) for wd in widths] +
                 [_RESIDENT, pl.BlockSpec((tm, d), row),
                  pl.BlockSpec((tm, d), row), pl.BlockSpec((1, N_MOD, d), lambda i: (i // tps, 0, 0)), vec, _token_spec()],
        out_specs=[pl.BlockSpec((tm, d), row), pl.BlockSpec((1, 8, d), lambda i: (i // tps, 0, 0)), vec],
        out_shape=[_out((t, d), F32), _out((nb, 8, d), F32),
                   _out((1, d), F32)],
        compiler_params=_cp(("arbitrary",), 40))(*_pin(*d_parts), w_in_t, *_pin(x, dh1, mod3, g_mix, token))


_VEC_LAYOUT = (("g_mix", 1024), ("conv_b", 1536), ("g_ssd", 1024), ("pool_scale", 512), ("g_mlp", 1024),
               ("g_final", 1024), ("dt_bias", 128), ("a_log", 128), ("d_skip_lanes", 1024), ("sq_err", 1024))
_VEC_OFFSET = {}
_off = 0
for _name, _n in _VEC_LAYOUT:
    _VEC_OFFSET[_name] = _off
    _off += _n
_VEC_LANES = _off
_SMALL_PARAMS = ("b_ada", "g_mix", "conv_w", "conv_b", "dt_bias", "a_log", "d_skip", "g_ssd", "w_pool", "pool_scale",
                 "g_mlp", "g_final")


def _pack_vec(parts):
    cols = []
    for name, n in _VEC_LAYOUT:
        v = parts[name]
        if v.shape[1] < n:
            v = jnp.pad(v, ((0, 0), (0, n - v.shape[1])))
        cols.append(v)
    return jnp.concatenate(cols, 1)


def _small_adam(vec_all, wpool_all, convw_all, dmod_all, params):
    names = _SMALL_PARAMS
    nin = 4 + 3 * len(names)

    def body(*refs):
        vec_ref, wp_ref, cw_ref, dm_ref = refs[:4]
        prm = {n: refs[4 + 3 * i:7 + 3 * i] for i, n in enumerate(names)}
        loss_ref = refs[nin]
        outs = {n: refs[nin + 1 + 4 * i:nin + 5 + 4 * i] for i, n in enumerate(names)}
        vsum = vec_ref[0]
        for s in range(1, N_DEV):
            vsum = vsum + vec_ref[s]

        def lanes(name, n):
            off = _VEC_OFFSET[name]
            return vsum[:, off:off + n]

        grads = {n: lanes(n, prm[n][0].shape[1]) for n in ("g_mix", "conv_b", "g_ssd", "pool_scale", "g_mlp", "g_final", "dt_bias")}
        grads["a_log"] = lanes("a_log", SSD_HEADS) * (-jnp.exp(prm["a_log"][0][...]))
        per_lane = jnp.broadcast_to(lanes("d_skip_lanes", SSD_INNER), (8, SSD_INNER))
        grads["d_skip"] = _exact_nn(per_lane, _head_reduce_matrix(SSD_INNER, SSD_HEADS))[0:1]
        gwp = wp_ref[0].astype(F32)
        gcw = cw_ref[0]
        gb = jnp.sum(dm_ref[0], 0, keepdims=True)
        for s in range(1, N_DEV):
            gwp = gwp + wp_ref[s].astype(F32)
            gcw = gcw + cw_ref[s]
            gb = gb + jnp.sum(dm_ref[s], 0, keepdims=True)
        grads["w_pool"] = gwp
        grads["conv_w"] = gcw[0:4]
        grads["b_ada"] = gb
        total = jnp.sum(lanes("sq_err", D_MODEL), 1, keepdims=True) * (0.5 / D_MODEL)
        loss_ref[...] = jnp.broadcast_to(total, loss_ref.shape)
        for n in names:
            w_ref, m_ref, v_ref = prm[n]
            g = grads[n]
            d, m2, v2 = _adam_math(w_ref[...], g, m_ref[...], v_ref[...])
            g_ref, d_ref, m2_ref, v2_ref = outs[n]
            g_ref[...] = g
            d_ref[...] = d
            m2_ref[...] = m2
            v2_ref[...] = v2

    flat = [vec_all, wpool_all, convw_all, dmod_all]
    out_shape = [jax.ShapeDtypeStruct((1, 128), F32)]
    for n in names:
        flat += list(params[n])
        out_shape += [jax.ShapeDtypeStruct(params[n][0].shape, F32)] * 4
    vm = pl.BlockSpec(memory_space=pltpu.VMEM)
    res = pl.pallas_call(body, name="small_adam", out_shape=out_shape, in_specs=[vm] * len(flat),
                         out_specs=[vm] * len(out_shape), compiler_params=_cp(vmem_mb=48))(*flat)
    return res[0], {n: res[1 + 4 * i:5 + 4 * i] for i, n in enumerate(names)}


_WEIGHTS = ("w_ada", "b_ada", "g_mix", "w_in", "conv_w", "conv_b", "dt_bias", "a_log", "d_skip", "g_ssd", "w_pool",
            "pool_scale", "w_out", "g_mlp", "w_up", "w_down", "g_final")


def _local_step(x2, tg2, mod3, seq, w_in_t, first_token, weights_later, start_reduce, conv_w_full, sp):
    t, d = x2.shape
    nb = t // seq
    dskip_e = jnp.repeat(sp["d_skip"], SSD_HEAD_DIM, axis=1)
    proj, u1 = _in_proj(x2, mod3, sp["g_mix"], w_in_t, seq, first_token)
    y_pool, p = _pool_fwd(proj, sp["w_pool"], sp["pool_scale"], nb, seq)
    y_ssd, yssm, h_prev, pre = _ssd_fwd(proj, conv_w_full, sp["conv_b"], sp["dt_bias"], sp["a_log"], dskip_e, sp["g_ssd"], nb, seq)
    w_out_f, w_up4, w_down4 = weights_later(y_ssd)
    w_down_f = w_down4.reshape(D_FF, d)
    h1, o, u2 = _out_proj(y_pool, y_ssd, w_out_f, x2, mod3, sp["g_mlp"], seq)
    a_up = _mlp_up(u2, w_up4)
    d_dn, dh2, sq, gg_final, d_gf = _mlp_down_loss(a_up, w_down_f, h1, mod3, sp["g_final"], tg2, seq)

    gw_down = _tn_matmul(a_up, d_dn, 512, d, "grad_w_down", square_relu=True)
    tok = start_reduce("w_down", gw_down.reshape(N_CHIPS, D_FF // N_CHIPS, d))
    d_a = _mlp_down_bwd(d_dn, w_down4, a_up, tok)
    gw_up4 = _tn_matmul(u2, d_a, 512, d, "grad_w_up", out3=True)
    tok = start_reduce("w_up", gw_up4)
    dh1, d_o, accf, gg_mlp = _mlp_up_bwd(d_a, w_up4, h1, dh2, o, mod3, sp["g_mlp"], seq, tok)
    gw_out = _grad_w_out(y_pool, y_ssd, d_o)
    tok = start_reduce("w_out", gw_out.reshape(N_CHIPS, gw_out.shape[0] // N_CHIPS, d))
    d_ypool, d_yssd = _out_proj_bwd(d_o, w_out_f, tok)
    d_upool, gw_pool, g_ps = _pool_bwd(d_ypool, p, sp["w_pool"], sp["pool_scale"], nb, seq)
    d_z, d_pre, d_udt, gg_ssd, gdsk, ga, gdtb = _ssd_bwd(proj, pre, d_yssd, yssm, h_prev, sp["dt_bias"], sp["a_log"],
                                                        dskip_e, sp["g_ssd"], nb, seq)
    d_uxbc, gconvw, gconvb = _conv_bwd(d_pre, proj, conv_w_full, nb, seq)
    gw_in_t = _grad_w_in_t(d_upool, d_z, d_uxbc, d_udt, u1)
    tok = start_reduce("w_in", gw_in_t[:IN_WIDTH].reshape(N_CHIPS, IN_WIDTH // N_CHIPS, d))
    gx, accm, gg_mix = _in_proj_bwd([d_upool, d_z, d_uxbc, d_udt], w_in_t, x2, dh1, mod3, sp["g_mix"], seq, tok)

    d_mod = jnp.concatenate([accm[:, 0], accm[:, 1], accf[:, 2], accf[:, 0], accf[:, 1], d_gf[:, 0]], 1)
    vec = _pack_vec({"g_mix": gg_mix, "conv_b": gconvb, "g_ssd": gg_ssd, "pool_scale": g_ps, "g_mlp": gg_mlp,
                     "g_final": gg_final, "dt_bias": gdtb, "a_log": ga, "d_skip_lanes": gdsk, "sq_err": sq})
    return gx, d_mod, vec, gw_pool, gconvw


def kernel(x, c, w_ada, b_ada, g_mix, w_in, conv_w, conv_b, dt_bias, a_log, d_skip, g_ssd, w_pool, pool_scale, w_out, g_mlp, w_up, w_down, g_final, loss_target, m_w_ada, m_b_ada, m_g_mix, m_w_in, m_conv_w, m_conv_b, m_dt_bias, m_a_log, m_d_skip, m_g_ssd, m_w_pool, m_pool_scale, m_w_out, m_g_mlp, m_w_up, m_w_down, m_g_final, v_w_ada, v_b_ada, v_g_mix, v_w_in, v_conv_w, v_conv_b, v_dt_bias, v_a_log, v_d_skip, v_g_ssd, v_w_pool, v_pool_scale, v_w_out, v_g_mlp, v_w_up, v_w_down, v_g_final):
    nb, seq, d = x.shape
    t = nb * seq
    xi, yi, ci = _mesh_pos()
    chip = 2 * xi + yi
    me = 4 * xi + 2 * yi + ci
    ada_cols = w_ada.shape[2]
    conv_cols = conv_w.shape[2]
    in_cols = w_in.shape[2]
    w_in_s, m_w_in_s, v_w_in_s = w_in[0].T, m_w_in[0].T, v_w_in[0].T

    c8, convw8 = _all_gather_small([c, conv_w[0]])
    w_in_b = w_in_s.astype(BF)
    i_send, i_recv, i_src, i_land, in_token = _ici_start(
        [w_in_b], [jax.ShapeDtypeStruct((N_CHIPS,) + w_in_b.shape, BF)], _gather_sent, _gather_landing, "gather_start_w_in",
        after=c8)
    c_all = c8.reshape(N_DEV * nb, d)
    conv_w_full = convw8[0::2].transpose(1, 0, 2).reshape(4, N_CHIPS * conv_cols)
    b_shard = lax.dynamic_slice(b_ada, (0, chip * ada_cols), (1, ada_cols))
    mod_part, c_act = _ada_mod(c_all, w_ada[0], b_shard, in_token)
    mod_rows = mod_part.reshape(N_DEV, nb, ada_cols)
    m_send, m_recv, m_src, m_land, _ = _ici_start(
        [mod_rows], [jax.ShapeDtypeStruct((N_CHIPS, nb, ada_cols), F32)], _mod_sent, _mod_landing, "mod_start", after=mod_part)

    later = [w_out[0].astype(BF), w_up[0].astype(BF), w_down[0].astype(BF)]
    in_shard, in_land = _ici_wait(i_send, i_recv, i_src, i_land, [m_src[0]] + later, _gather_sent, _gather_landing,
                                  "gather_wait_w_in")
    (w_in4,) = _gather_finish(in_land, in_shard)
    w_in_t = jnp.pad(w_in4.reshape(N_CHIPS * in_cols, d), ((0, IN_PAD - N_CHIPS * in_cols), (0, 0)))
    mod_mine, mod_land = _ici_wait(m_send, m_recv, m_src, m_land, w_in_t, _mod_sent, _mod_landing, "mod_wait")
    mod_own = lax.dynamic_slice(mod_mine[0], (me, 0, 0), (1, nb, ada_cols))
    mod4 = lax.dynamic_update_slice(mod_land[0], mod_own, (chip, 0, 0))
    mod3 = mod4.transpose(1, 0, 2).reshape(nb, N_MOD, d)
    g_send, g_recv, g_src, g_land, first_token = _ici_start(
        later, [jax.ShapeDtypeStruct((N_CHIPS,) + s.shape, BF) for s in later], _gather_sent, _gather_landing, "gather_start",
        after=w_in4)

    def weights_later(after):
        shards, lands = _ici_wait(g_send, g_recv, g_src, g_land, after, _gather_sent, _gather_landing, "gather_wait")
        w_out4, w_up4, w_down4 = _gather_finish(lands, shards)
        return w_out4.reshape(N_CHIPS * w_out.shape[1], d), w_up4, w_down4

    pending = {}

    def start_reduce(name, grad4):
        pending[name] = _reduce_start(grad4, "reduce_start_" + name)
        return pending[name][4]

    sp = dict(g_mix=g_mix, conv_b=conv_b, dt_bias=dt_bias, a_log=a_log, d_skip=d_skip, g_ssd=g_ssd,
              w_pool=w_pool[0], pool_scale=pool_scale, g_mlp=g_mlp, g_final=g_final.reshape(1, d))
    gx, d_mod, vec, gw_pool, gconvw = _local_step(
        x.reshape(t, d), loss_target.reshape(t, d), mod3, seq, w_in_t, first_token, weights_later, start_reduce, conv_w_full, sp)

    pos = jnp.stack([ci, chip, me]).astype(jnp.int32)
    small_parts = [vec, gw_pool.reshape(4 * POOL_GROUP, POOL_GROUP).astype(BF), gconvw, d_mod]
    s_send, s_recv, s_src, s_land, s_token = _exchange_start(small_parts, ALL_PEERS, "small_start")
    halves = []
    for name in ("w_in", "w_out", "w_up", "w_down"):
        r_send, r_recv, r_src, r_land, _ = pending[name]
        own, recv = _reduce_wait(r_send, r_recv, r_src, r_land, s_token, "reduce_wait_" + name)
        halves.append(_sum_eight(recv, own, pos))
    h_send, h_recv, h_src, h_land, h_token = _exchange_start(halves, SIBLING, "halves_start")

    s_own, s_got = _exchange_wait(s_send, s_recv, s_src, s_land, ALL_PEERS, h_token, "small_wait")
    vec8, wpool8, convw8g, dmod8 = [lax.dynamic_update_slice(got, mine[None], (me,) + (0,) * mine.ndim)
                                    for got, mine in zip(s_got, s_own)]
    convw8s = lax.dynamic_slice(convw8g, (0, 0, chip * conv_cols), (N_DEV, 8, conv_cols))
    m_in = dict(b_ada=m_b_ada, g_mix=m_g_mix, conv_w=m_conv_w[0], conv_b=m_conv_b, dt_bias=m_dt_bias, a_log=m_a_log,
                d_skip=m_d_skip, g_ssd=m_g_ssd, w_pool=m_w_pool.reshape(4 * POOL_GROUP, POOL_GROUP), pool_scale=m_pool_scale,
                g_mlp=m_g_mlp, g_final=m_g_final.reshape(1, d))
    v_in = dict(b_ada=v_b_ada, g_mix=v_g_mix, conv_w=v_conv_w[0], conv_b=v_conv_b, dt_bias=v_dt_bias, a_log=v_a_log,
                d_skip=v_d_skip, g_ssd=v_g_ssd, w_pool=v_w_pool.reshape(4 * POOL_GROUP, POOL_GROUP), pool_scale=v_pool_scale,
                g_mlp=v_g_mlp, g_final=v_g_final.reshape(1, d))
    w_small = dict(sp, b_ada=b_ada, conv_w=conv_w[0], w_pool=w_pool.reshape(4 * POOL_GROUP, POOL_GROUP))
    loss_row, small = _small_adam(vec8, wpool8, convw8s, dmod8, {n: (w_small[n], m_in[n], v_in[n]) for n in _SMALL_PARAMS})

    dmod_all = dmod8.reshape(N_DEV * nb, N_CHIPS * ada_cols)
    dmod_cols = lax.dynamic_slice(dmod_all, (0, chip * ada_cols), (N_DEV * nb, ada_cols))
    res = {n: tuple(r.reshape(w.shape) for r in small[n])
           for n, w in (("b_ada", b_ada), ("g_mix", g_mix), ("conv_w", conv_w), ("conv_b", conv_b), ("dt_bias", dt_bias),
                        ("a_log", a_log), ("d_skip", d_skip), ("g_ssd", g_ssd), ("w_pool", w_pool), ("pool_scale", pool_scale),
                        ("g_mlp", g_mlp), ("g_final", g_final))}
    g_ada, d_ada, m_ada, v_ada = _adam_ada(c_act.T.astype(BF), dmod_cols, w_ada[0], m_w_ada[0], v_w_ada[0])
    res["w_ada"] = (g_ada[None], d_ada[None], m_ada[None], v_ada[None])
    h_own, h_got = _exchange_wait(h_send, h_recv, h_src, h_land, SIBLING, g_ada, "halves_wait")
    g_in, dl, m2, v2 = _adam_big(h_own[0], h_got[0], w_in_s, m_w_in_s, v_w_in_s, pos)
    res["w_in"] = (g_in.T[None], dl.T[None], m2.T[None], v2.T[None])
    for i, (n, w, m, v) in enumerate((("w_out", w_out, m_w_out, v_w_out), ("w_up", w_up, m_w_up, v_w_up),
                                      ("w_down", w_down, m_w_down, v_w_down))):
        g, dl, m2, v2 = _adam_big(h_own[i + 1], h_got[i + 1], w[0], m[0], v[0], pos)
        res[n] = (g[None], dl[None], m2[None], v2[None])

    loss = loss_row[0, 0]
    return (loss, gx.reshape(nb, seq, d), *[res[n][0] for n in _WEIGHTS], *[res[n][1] for n in _WEIGHTS],
            *[res[n][2] for n in _WEIGHTS], *[res[n][3] for n in _WEIGHTS])
```

```python
import jax
import jax.numpy as jnp
from jax import lax
from jax.experimental import pallas as pl
from jax.experimental.pallas import tpu as pltpu

F32 = jnp.float32
BF = jnp.bfloat16
MESH = pl.DeviceIdType.MESH

EPS = 1e-5
D_MODEL = 1024
POOL_WIDTH = 512
POOL_WINDOWS = (2, 4, 8, 16)
POOL_GROUP = 128
SSD_INNER = 1024
SSD_HEADS = 16
SSD_HEAD_DIM = 64
SSD_STATE = 128
GROUP_W = 512
CHUNK = 128
CONV_CH = 1536
OFF_DT = 3072
IN_WIDTH = 3088
IN_PAD = 3200
D_FF = 4096
N_MOD = 6
N_CHIPS = 4
N_DEV = 8
HALO = 16
CONV_HALO = 8

ADAM_LR = 0.001
ADAM_B1 = 0.9
ADAM_B2 = 0.999
ADAM_EPS = 1e-08
ADAM_WD = 0.01
ADAM_STEP = 10

VMEM_BYTES_V7X = 64 * 1024 * 1024


def _cp(semantics=None, vmem_mb=48, **kw):
    assert vmem_mb * 1024 * 1024 < VMEM_BYTES_V7X
    args = dict(vmem_limit_bytes=vmem_mb * 1024 * 1024, **kw)
    if semantics is not None:
        args["dimension_semantics"] = semantics
    return pltpu.CompilerParams(**args)


def _out(shape, dtype):
    return pltpu.HBM(shape, dtype)


def _pin(*arrays):
    return [pltpu.with_memory_space_constraint(a, pltpu.HBM) for a in arrays]


def _nn(a, b):
    return jnp.dot(a, b, preferred_element_type=F32)


def _nt(a, b):
    return lax.dot_general(a, b, (((1,), (1,)), ((), ())), preferred_element_type=F32)


def _tn(a, b):
    return lax.dot_general(a, b, (((0,), (0,)), ((), ())), preferred_element_type=F32)


def _split3(v):
    hi = v.astype(BF)
    r1 = v - hi.astype(F32)
    mid = r1.astype(BF)
    lo = (r1 - mid.astype(F32)).astype(BF)
    return hi, mid, lo


def _exact_nn(v, m01):
    hi, mid, lo = _split3(v)
    return _nn(hi, m01) + _nn(mid, m01) + _nn(lo, m01)


def _exact_nn_left(m01, v):
    hi, mid, lo = _split3(v)
    return _nn(m01, hi) + _nn(m01, mid) + _nn(m01, lo)


def _exact_nt_left(m01, v):
    hi, mid, lo = _split3(v)
    return _nt(m01, hi) + _nt(m01, mid) + _nt(m01, lo)


def _sigmoid(v):
    return 1.0 / (1.0 + jnp.exp(-v))


def _iota(shape, dim):
    return lax.broadcasted_iota(jnp.int32, shape, dim)


def _head_expand_matrix(heads, width):
    return (_iota((heads, width), 1) // SSD_HEAD_DIM == _iota((heads, width), 0)).astype(BF)


def _head_reduce_matrix(width, heads):
    return (_iota((width, heads), 0) // SSD_HEAD_DIM == _iota((width, heads), 1)).astype(BF)


def _mesh_pos():
    return lax.axis_index("x"), lax.axis_index("y"), lax.axis_index("c")


def _flip(v, bit):
    return v + bit - 2 * bit * v


def _all_gather_small(arrays):
    n = len(arrays)

    def body(*refs):
        in_refs, out_refs = refs[:n], refs[n:2 * n]
        send_sems, recv_sems, local_sems = refs[2 * n:]
        x, y, c = _mesh_pos()
        me = 4 * x + 2 * y + c
        local = []
        for a in range(n):
            cp = pltpu.make_async_copy(in_refs[a], out_refs[a].at[me], local_sems.at[a])
            cp.start()
            local.append(cp)
        sends = []
        for k in range(1, N_DEV):
            peer = (_flip(x, (k >> 2) & 1), _flip(y, (k >> 1) & 1), _flip(c, k & 1))
            for a in range(n):
                cp = pltpu.make_async_remote_copy(
                    src_ref=in_refs[a], dst_ref=out_refs[a].at[me],
                    send_sem=send_sems.at[a, k], recv_sem=recv_sems.at[a, k],
                    device_id=peer, device_id_type=MESH)
                cp.start()
                sends.append(cp)
        for k in range(1, N_DEV):
            px, py, pc = _flip(x, (k >> 2) & 1), _flip(y, (k >> 1) & 1), _flip(c, k & 1)
            src = 4 * px + 2 * py + pc
            for a in range(n):
                pltpu.make_async_remote_copy(
                    src_ref=in_refs[a], dst_ref=out_refs[a].at[src],
                    send_sem=send_sems.at[a, k], recv_sem=recv_sems.at[a, k],
                    device_id=(px, py, pc), device_id_type=MESH).wait_recv()
        for cp in sends:
            cp.wait_send()
        for cp in local:
            cp.wait()

    vm = pl.BlockSpec(memory_space=pltpu.VMEM)
    return pl.pallas_call(
        body, name="all_gather_small",
        out_shape=[jax.ShapeDtypeStruct((N_DEV,) + a.shape, a.dtype) for a in arrays],
        in_specs=[vm] * n, out_specs=[vm] * n,
        scratch_shapes=[pltpu.SemaphoreType.DMA((n, N_DEV)), pltpu.SemaphoreType.DMA((n, N_DEV)),
                        pltpu.SemaphoreType.DMA((n,))],
        compiler_params=_cp(vmem_mb=32),
    )(*arrays)


_HBM = pl.BlockSpec(memory_space=pltpu.HBM)
_SEM = pl.BlockSpec(memory_space=pltpu.SEMAPHORE)
_DATAFLOW = pltpu.SideEffectType.DATAFLOW_SIDE_EFFECTING


def _peer_chip(x, y, j):
    return _flip(x, (j >> 1) & 1), _flip(y, j & 1)


def _ici_start(srcs, land_shapes, sent, landing, name, after):
    n = len(srcs)

    def body(*refs):
        src_refs, land_refs = refs[:n], refs[n:2 * n]
        send_sems, recv_sems = refs[2 * n + 1], refs[2 * n + 2]
        token = refs[-1]
        x, y, c = _mesh_pos()
        for j in range(1, N_CHIPS):
            px, py = _peer_chip(x, y, j)
            for a in range(n):
                pltpu.make_async_remote_copy(
                    src_ref=sent(src_refs[a], c, 2 * px + py), dst_ref=landing(land_refs[a], c, 2 * x + y),
                    send_sem=send_sems.at[a * (N_CHIPS - 1) + j - 1], recv_sem=recv_sems.at[a * (N_CHIPS - 1) + j - 1],
                    device_id=(px, py, c), device_id_type=MESH).start()
        token[...] = jnp.zeros_like(token)

    sems = pltpu.SemaphoreType.DMA((n * (N_CHIPS - 1),))
    lands = [pltpu.with_memory_space_constraint(lax.empty(s.shape, s.dtype), pltpu.HBM) for s in land_shapes]
    outs = pl.pallas_call(
        body, name=name,
        out_shape=(sems, sems, *[pltpu.HBM(s.shape, s.dtype) for s in srcs],
                   *[pltpu.HBM(s.shape, s.dtype) for s in land_shapes], jax.ShapeDtypeStruct((8, 128), F32)),
        in_specs=[_HBM] * (2 * n + 1), out_specs=[_SEM, _SEM] + [_HBM] * (2 * n) + [pl.BlockSpec(memory_space=pltpu.VMEM)],
        input_output_aliases={i: 2 + i for i in range(2 * n)},
        compiler_params=pltpu.CompilerParams(has_side_effects=_DATAFLOW),
    )(*_pin(*srcs), *lands, *_pin(after))
    return outs[0], outs[1], outs[2:2 + n], outs[2 + n:2 + 2 * n], outs[-1]


def _ici_wait(send_sems, recv_sems, src_thru, land_thru, after, sent, landing, name):
    n = len(src_thru)
    afters = list(after) if isinstance(after, (list, tuple)) else [after]

    def body(*refs):
        src_refs, land_refs = refs[:n], refs[n:2 * n]
        send_sems, recv_sems = refs[2 * n], refs[2 * n + 1]
        x, y, c = _mesh_pos()
        for j in range(1, N_CHIPS):
            px, py = _peer_chip(x, y, j)
            for a in range(n):
                cp = pltpu.make_async_remote_copy(
                    src_ref=sent(src_refs[a], c, 2 * px + py), dst_ref=landing(land_refs[a], c, 2 * px + py),
                    send_sem=send_sems.at[a * (N_CHIPS - 1) + j - 1], recv_sem=recv_sems.at[a * (N_CHIPS - 1) + j - 1],
                    device_id=(px, py, c), device_id_type=MESH)
                cp.wait_send()
                cp.wait_recv()

    outs = pl.pallas_call(
        body, name=name,
        out_shape=tuple(pltpu.HBM(s.shape, s.dtype) for s in (*src_thru, *land_thru)),
        in_specs=[_HBM] * (2 * n) + [_SEM, _SEM] + [_HBM] * len(afters), out_specs=[_HBM] * (2 * n),
        input_output_aliases={i: i for i in range(2 * n)},
        compiler_params=pltpu.CompilerParams(has_side_effects=_DATAFLOW),
    )(*src_thru, *land_thru, send_sems, recv_sems, *_pin(*afters))
    return outs[:n], outs[n:]


def _col_half(ref, which, lead=()):
    hc = ref.shape[-1] // 2
    return ref.at[(*lead, slice(None), pl.ds(pl.multiple_of(which * hc, 128), hc))]


def _gather_sent(ref, c, dst_chip):
    return _col_half(ref, c)


def _gather_landing(ref, c, src_chip):
    return _col_half(ref, c, lead=(src_chip,))


def _mod_sent(ref, c, dst_chip):
    return ref.at[2 * dst_chip + c]


def _mod_landing(ref, c, src_chip):
    return ref.at[src_chip]


def _reduce_copy(src_ref, land_ref, send_sems, recv_sems, k, receiving):
    x, y, c = _mesh_pos()
    px, py, pc = _flip(x, (k >> 2) & 1), _flip(y, (k >> 1) & 1), _flip(c, k & 1)
    hc = src_ref.shape[2] // 2
    src = src_ref.at[2 * px + py, :, pl.ds(pl.multiple_of(pc * hc, 128), hc)]
    slot = (4 * px + 2 * py + pc) if receiving else (4 * x + 2 * y + c)
    return pltpu.make_async_remote_copy(
        src_ref=src, dst_ref=land_ref.at[slot], send_sem=send_sems.at[k - 1], recv_sem=recv_sems.at[k - 1],
        device_id=(px, py, pc), device_id_type=MESH)


def _reduce_start(grad4, name):
    k4, r, cols = grad4.shape

    def body(src_ref, land_ref, send_sems, recv_sems, src_thru, land_thru, token):
        for k in range(1, N_DEV):
            _reduce_copy(src_ref, land_ref, send_sems, recv_sems, k, receiving=False).start()
        token[...] = jnp.zeros_like(token)

    sems = pltpu.SemaphoreType.DMA((N_DEV - 1,))
    land = pltpu.with_memory_space_constraint(lax.empty((N_DEV, r, cols // 2), grad4.dtype), pltpu.HBM)
    return pl.pallas_call(
        body, name=name,
        out_shape=(sems, sems, pltpu.HBM(grad4.shape, grad4.dtype), pltpu.HBM(land.shape, land.dtype),
                   jax.ShapeDtypeStruct((8, 128), F32)),
        in_specs=[_HBM, _HBM], out_specs=[_SEM, _SEM, _HBM, _HBM, pl.BlockSpec(memory_space=pltpu.VMEM)],
        input_output_aliases={0: 2, 1: 3},
        compiler_params=pltpu.CompilerParams(has_side_effects=_DATAFLOW),
    )(*_pin(grad4), land)


def _reduce_wait(send_sems, recv_sems, src_thru, land_thru, after, name):
    def body(src_ref, land_ref, send_sems, recv_sems, after_ref, src_out, land_out):
        for k in range(1, N_DEV):
            cp = _reduce_copy(src_ref, land_ref, send_sems, recv_sems, k, receiving=True)
            cp.wait_send()
            cp.wait_recv()

    return pl.pallas_call(
        body, name=name,
        out_shape=(pltpu.HBM(src_thru.shape, src_thru.dtype), pltpu.HBM(land_thru.shape, land_thru.dtype)),
        in_specs=[_HBM, _HBM, _SEM, _SEM, _HBM], out_specs=[_HBM, _HBM],
        input_output_aliases={0: 0, 1: 1},
        compiler_params=pltpu.CompilerParams(has_side_effects=_DATAFLOW),
    )(src_thru, land_thru, send_sems, recv_sems, *_pin(after))


def _peer_copy(src_ref, land_ref, send_sems, recv_sems, idx, k, receiving):
    x, y, c = _mesh_pos()
    px, py, pc = _flip(x, (k >> 2) & 1), _flip(y, (k >> 1) & 1), _flip(c, k & 1)
    if land_ref.shape[0] == N_DEV:
        slot = (4 * px + 2 * py + pc) if receiving else (4 * x + 2 * y + c)
    else:
        slot = pc if receiving else c
    return pltpu.make_async_remote_copy(
        src_ref=src_ref, dst_ref=land_ref.at[slot], send_sem=send_sems.at[idx], recv_sem=recv_sems.at[idx],
        device_id=(px, py, pc), device_id_type=MESH)


def _exchange_start(arrays, peers, name):
    n = len(arrays)

    def body(*refs):
        src_refs, land_refs = refs[:n], refs[n:2 * n]
        send_sems, recv_sems = refs[2 * n], refs[2 * n + 1]
        token = refs[-1]
        for j, k in enumerate(peers):
            for a in range(n):
                _peer_copy(src_refs[a], land_refs[a], send_sems, recv_sems, a * len(peers) + j, k, receiving=False).start()
        token[...] = jnp.zeros_like(token)

    sems = pltpu.SemaphoreType.DMA((n * len(peers),))
    n_slots = N_DEV if len(peers) > 1 else 2
    lands = [pltpu.with_memory_space_constraint(lax.empty((n_slots,) + a.shape, a.dtype), pltpu.HBM) for a in arrays]
    outs = pl.pallas_call(
        body, name=name,
        out_shape=(sems, sems, *[pltpu.HBM(a.shape, a.dtype) for a in arrays], *[pltpu.HBM(l.shape, l.dtype) for l in lands],
                   jax.ShapeDtypeStruct((8, 128), F32)),
        in_specs=[_HBM] * (2 * n), out_specs=[_SEM, _SEM] + [_HBM] * (2 * n) + [pl.BlockSpec(memory_space=pltpu.VMEM)],
        input_output_aliases={i: 2 + i for i in range(2 * n)},
        compiler_params=pltpu.CompilerParams(has_side_effects=_DATAFLOW),
    )(*_pin(*arrays), *lands)
    return outs[0], outs[1], outs[2:2 + n], outs[2 + n:2 + 2 * n], outs[-1]


def _exchange_wait(send_sems, recv_sems, src_thru, land_thru, peers, after, name):
    n = len(src_thru)

    def body(*refs):
        src_refs, land_refs = refs[:n], refs[n:2 * n]
        send_sems, recv_sems = refs[2 * n], refs[2 * n + 1]
        for j, k in enumerate(peers):
            for a in range(n):
                cp = _peer_copy(src_refs[a], land_refs[a], send_sems, recv_sems, a * len(peers) + j, k, receiving=True)
                cp.wait_send()
                cp.wait_recv()

    outs = pl.pallas_call(
        body, name=name,
        out_shape=tuple(pltpu.HBM(s.shape, s.dtype) for s in (*src_thru, *land_thru)),
        in_specs=[_HBM] * (2 * n) + [_SEM, _SEM, _HBM], out_specs=[_HBM] * (2 * n),
        input_output_aliases={i: i for i in range(2 * n)},
        compiler_params=pltpu.CompilerParams(has_side_effects=_DATAFLOW),
    )(*src_thru, *land_thru, send_sems, recv_sems, *_pin(after))
    return outs[:n], outs[n:]


ALL_PEERS = tuple(range(1, N_DEV))
SIBLING = (1,)


def _sum_eight(recv, grad4, pos):
    n, r, hc = recv.shape
    steps = 2
    tc = hc // steps

    def body(pos_ref, r_ref, g_ref, o_ref):
        me = pos_ref[2]
        o_ref[...] = jnp.zeros_like(o_ref)
        for s in range(n):
            @pl.when(me == s)
            def _():
                o_ref[...] += g_ref[0].astype(F32)

            @pl.when(me != s)
            def _():
                o_ref[...] += r_ref[s].astype(F32)

    grid_spec = pltpu.PrefetchScalarGridSpec(
        num_scalar_prefetch=1, grid=(steps,),
        in_specs=[pl.BlockSpec((n, r, tc), lambda i, pos: (0, 0, i)),
                  pl.BlockSpec((1, r, tc), lambda i, pos: (pos[1], 0, pos[0] * steps + i))],
        out_specs=pl.BlockSpec((r, tc), lambda i, pos: (0, i)))
    return pl.pallas_call(body, name="sum_eight", grid_spec=grid_spec, out_shape=_out((r, hc), F32),
                          compiler_params=_cp(("parallel",), 32))(pos, *_pin(recv, grad4))


def _gather_finish(lands, shards):
    n = len(lands)
    any_spec = _HBM

    def body(*refs):
        shard_refs, out_refs = refs[n:2 * n], refs[2 * n:3 * n]
        send_sems, recv_sems, local_sems = refs[3 * n:]
        x, y, c = _mesh_pos()
        chip = 2 * x + y
        local, sends = [], []
        for a in range(n):
            cp = pltpu.make_async_copy(shard_refs[a], out_refs[a].at[chip], local_sems.at[a])
            cp.start()
            local.append(cp)
        for j in range(1, N_CHIPS):
            px, py = _peer_chip(x, y, j)
            for a in range(n):
                landed = _col_half(out_refs[a], c, lead=(2 * px + py,))
                cp = pltpu.make_async_remote_copy(
                    src_ref=landed, dst_ref=landed, send_sem=send_sems.at[a, j], recv_sem=recv_sems.at[a, j],
                    device_id=(x, y, 1 - c), device_id_type=MESH)
                cp.start()
                sends.append(cp)
        for j in range(1, N_CHIPS):
            px, py = _peer_chip(x, y, j)
            for a in range(n):
                other = _col_half(out_refs[a], 1 - c, lead=(2 * px + py,))
                pltpu.make_async_remote_copy(
                    src_ref=other, dst_ref=other, send_sem=send_sems.at[a, j], recv_sem=recv_sems.at[a, j],
                    device_id=(x, y, 1 - c), device_id_type=MESH).wait_recv()
        for cp in sends:
            cp.wait_send()
        for cp in local:
            cp.wait()

    return pl.pallas_call(
        body, name="gather_finish",
        out_shape=[_out(l.shape, l.dtype) for l in lands],
        in_specs=[any_spec] * (2 * n), out_specs=[any_spec] * n,
        input_output_aliases={i: i for i in range(n)},
        scratch_shapes=[pltpu.SemaphoreType.DMA((n, N_CHIPS))] * 2 + [pltpu.SemaphoreType.DMA((n,))],
        compiler_params=_cp(vmem_mb=16),
    )(*lands, *shards)


def _adam_math(w, g, m, v):
    m2 = ADAM_B1 * m + (1.0 - ADAM_B1) * g
    v2 = ADAM_B2 * v + (1.0 - ADAM_B2) * (g * g)
    m_hat = m2 / (1.0 - ADAM_B1 ** ADAM_STEP)
    v_hat = v2 / (1.0 - ADAM_B2 ** ADAM_STEP)
    delta = -ADAM_LR * (m_hat / (jnp.sqrt(v_hat) + ADAM_EPS) + ADAM_WD * w)
    return delta, m2, v2


def _adam_big(g_own, g_pair, w, m, v, pos):
    r, c = w.shape
    per_half = 2
    tc = c // (2 * per_half)

    def body(pos_ref, go_ref, gp_ref, w_ref, m_ref, v_ref, g_ref, d_ref, m2_ref, v2_ref):
        half = pl.program_id(0) // per_half
        g = jnp.where(half == pos_ref[0], go_ref[...], gp_ref[0])
        d, m2, v2 = _adam_math(w_ref[...], g, m_ref[...], v_ref[...])
        g_ref[...] = g
        d_ref[...] = d
        m2_ref[...] = m2
        v2_ref[...] = v2

    spec = pl.BlockSpec((r, tc), lambda i, pos: (0, i))
    grid_spec = pltpu.PrefetchScalarGridSpec(
        num_scalar_prefetch=1, grid=(2 * per_half,),
        in_specs=[pl.BlockSpec((r, tc), lambda i, pos: (0, i % per_half)),
                  pl.BlockSpec((1, r, tc), lambda i, pos: (1 - pos[0], 0, i % per_half)), spec, spec, spec],
        out_specs=[spec] * 4)
    sh = _out((r, c), F32)
    return pl.pallas_call(body, name="adam_big", grid_spec=grid_spec, out_shape=[sh] * 4,
                          compiler_params=_cp(("parallel",), 32))(pos, *_pin(g_own, g_pair, w, m, v))


def _adam_ada(c_act_t, dmod_cols, w, m, v):
    r, c = w.shape
    tc = 512

    def body(ct_ref, dm_ref, w_ref, m_ref, v_ref, g_ref, d_ref, m2_ref, v2_ref):
        g = _nn(ct_ref[...], dm_ref[...].astype(BF))
        d, m2, v2 = _adam_math(w_ref[...], g, m_ref[...], v_ref[...])
        g_ref[...] = g
        d_ref[...] = d
        m2_ref[...] = m2
        v2_ref[...] = v2

    spec = pl.BlockSpec((r, tc), lambda i: (0, i))
    sh = _out((r, c), F32)
    return pl.pallas_call(
        body, name="adam_ada", grid=(c // tc,),
        in_specs=[pl.BlockSpec(c_act_t.shape, lambda i: (0, 0)), pl.BlockSpec((dmod_cols.shape[0], tc), lambda i: (0, i)),
                  spec, spec, spec],
        out_specs=[spec] * 4, out_shape=[sh] * 4, compiler_params=_cp(("parallel",), 48))(*_pin(c_act_t, dmod_cols, w, m, v))


def _ada_mod(c_all, w_shard, b_shard, token):
    nb, d = c_all.shape
    cols = w_shard.shape[1]
    tc = 512

    def body(c_ref, w_ref, b_ref, tok_ref, mod_ref, act_ref):
        cv = c_ref[...]
        act = cv * _sigmoid(cv)
        act_ref[...] = act
        mod_ref[...] = _nn(act.astype(BF), w_ref[...].astype(BF)) + b_ref[...]

    return pl.pallas_call(
        body, name="ada_mod", grid=(cols // tc,),
        in_specs=[pl.BlockSpec((nb, d), lambda i: (0, 0)), pl.BlockSpec((d, tc), lambda i: (0, i)),
                  pl.BlockSpec((1, tc), lambda i: (0, i)), _token_spec()],
        out_specs=[pl.BlockSpec((nb, tc), lambda i: (0, i)), pl.BlockSpec((nb, d), lambda i: (0, 0))],
        out_shape=[_out((nb, cols), F32), _out((nb, d), F32)],
        compiler_params=_cp(("arbitrary",), 32))(*_pin(c_all, w_shard, b_shard, token))


SUB_ROWS = 256
ROW_TILE = 512


def _sub_rows(tm):
    return [slice(s, s + SUB_ROWS) for s in range(0, tm, SUB_ROWS)] if tm > SUB_ROWS else [slice(0, tm)]


_RESIDENT = pl.BlockSpec(memory_space=pltpu.VMEM)


def _token_spec():
    return pl.BlockSpec((8, 128), lambda *_: (0, 0))


def _in_proj(x, mod3, g_mix, w_in_t, seq, token):
    t, d = x.shape
    tm = min(ROW_TILE, seq)
    tps = seq // tm

    def body(x_ref, mod_ref, g_ref, w_ref, tok_ref, proj_ref, u1_ref):
        for rows in _sub_rows(tm):
            xv = x_ref[rows, :]
            r = lax.rsqrt(jnp.mean(xv * xv, -1, keepdims=True) + EPS)
            u = (xv * r * g_ref[...]) * (1.0 + mod_ref[0, 1:2, :]) + mod_ref[0, 0:1, :]
            ub = u.astype(BF)
            u1_ref[rows, :] = ub
            proj_ref[rows, :] = _nt(ub, w_ref[...])

    return pl.pallas_call(
        body, name="in_proj", grid=(t // tm,),
        in_specs=[pl.BlockSpec((tm, d), lambda i: (i, 0)), pl.BlockSpec((1, N_MOD, d), lambda i: (i // tps, 0, 0)),
                  pl.BlockSpec((1, d), lambda i: (0, 0)), _RESIDENT, _token_spec()],
        out_specs=[pl.BlockSpec((tm, IN_PAD), lambda i: (i, 0)), pl.BlockSpec((tm, d), lambda i: (i, 0))],
        out_shape=[_out((t, IN_PAD), F32), _out((t, d), BF)],
        compiler_params=_cp(("parallel",), 40))(*_pin(x, mod3, g_mix), w_in_t, *_pin(token))


def _pool_tile(seq):
    return min(512, seq)


def _pool_fwd(proj, w_pool, pool_scale, nb, seq):
    ts = _pool_tile(seq)
    nt = seq // ts

    def body(u_ref, halo_ref, wp_ref, ps_ref, yp_ref, p_ref):
        i = pl.program_id(1)
        halo = jnp.where(i == 0, 0.0, halo_ref[...])
        u = u_ref[...]
        ext = jnp.concatenate([halo, u], 0)
        tpos = i * ts + _iota((ts, 1), 0)
        for g, w in enumerate(POOL_WINDOWS):
            gs = slice(g * POOL_GROUP, (g + 1) * POOL_GROUP)
            s = ext[:, gs]
            sh = 1
            while sh < w:
                s = s + pltpu.roll(s, sh, 0)
                sh *= 2
            cnt = jnp.minimum(tpos + 1, w).astype(F32)
            pb = (s[HALO:] / cnt - u[:, gs]).astype(BF)
            p_ref[:, gs] = pb
            yp_ref[:, gs] = (_nn(pb, wp_ref[g].astype(BF)) * ps_ref[:, gs]).astype(BF)

    hb = ts // HALO
    return pl.pallas_call(
        body, name="pool_fwd", grid=(nb, nt),
        in_specs=[pl.BlockSpec((ts, POOL_WIDTH), lambda b, i: (b * nt + i, 0)),
                  pl.BlockSpec((HALO, POOL_WIDTH), lambda b, i: (jnp.maximum((b * nt + i) * hb - 1, 0), 0)),
                  pl.BlockSpec((4, POOL_GROUP, POOL_GROUP), lambda b, i: (0, 0, 0)),
                  pl.BlockSpec((1, POOL_WIDTH), lambda b, i: (0, 0))],
        out_specs=[pl.BlockSpec((ts, POOL_WIDTH), lambda b, i: (b * nt + i, 0))] * 2,
        out_shape=[_out((nb * seq, POOL_WIDTH), BF)] * 2,
        compiler_params=_cp(("parallel", "parallel"), 32))(*_pin(proj, proj, w_pool, pool_scale))


def _conv_pre(uxbc, halo, cw, cb, first):
    halo = jnp.where(first, 0.0, halo)
    ext = jnp.concatenate([halo, uxbc], 0)
    pre = cb + uxbc * cw[3:4]
    for k in (2, 1, 0):
        pre = pre + pltpu.roll(ext, 3 - k, 0)[CONV_HALO:] * cw[k:k + 1]
    return pre


def _chunk_terms(pre, udt, dtb, alog):
    sg = _sigmoid(pre)
    xbc = pre * sg
    dtp = udt[:, :SSD_HEADS] + dtb
    dt = jnp.maximum(dtp, 0.0) + jnp.log(1.0 + jnp.exp(-jnp.abs(dtp)))
    a = -jnp.exp(alog)
    da = dt * a
    tril = (_iota((CHUNK, CHUNK), 0) >= _iota((CHUNK, CHUNK), 1))
    acum = _exact_nn_left(tril.astype(BF), da)
    eye = (_iota((SSD_HEADS, SSD_HEADS), 0) == _iota((SSD_HEADS, SSD_HEADS), 1)).astype(BF)
    acum_t = _exact_nt_left(eye, acum)
    expand = _head_expand_matrix(SSD_HEADS, SSD_INNER)
    acum_e = _exact_nn(acum, expand)
    dt_e = _exact_nn(dt, expand)
    last_e = acum_e[CHUNK - 1:CHUNK]
    return dict(pre=pre, sg=sg, xbc=xbc, dtp=dtp, dt=dt, a=a, acum=acum, acum_t=acum_t, tril=tril,
                dt_e=dt_e, e_a=jnp.exp(acum_e), d_out=jnp.exp(last_e - acum_e), c_dec=jnp.exp(last_e))


def _head_decay(r, h):
    seg = r["acum"][:, h:h + 1] - r["acum_t"][h:h + 1, :]
    return jnp.where(r["tril"], jnp.exp(jnp.minimum(seg, 0.0)), 0.0)


SSD_SUB = 4
SSD_ROWS = SSD_SUB * CHUNK


def _ssd_specs(nb, seq, reverse):
    ns = seq // SSD_ROWS
    per = seq // CONV_HALO

    def cidx(c):
        return (ns - 1 - c) if reverse else c

    def row(b, c):
        return b * ns + cidx(c)

    specs = [
        pl.BlockSpec((SSD_ROWS, CONV_CH), lambda b, c: (row(b, c), 1)),
        pl.BlockSpec((CONV_HALO, CONV_CH),
                     lambda b, c: (jnp.maximum(b * per + cidx(c) * (SSD_ROWS // CONV_HALO) - 1, 0), 1)),
        pl.BlockSpec((SSD_ROWS, GROUP_W), lambda b, c: (row(b, c), 1)),
        pl.BlockSpec((SSD_ROWS, GROUP_W), lambda b, c: (row(b, c), 2)),
        pl.BlockSpec((SSD_ROWS, 128), lambda b, c: (row(b, c), OFF_DT // 128)),
    ]
    return specs, row, cidx, ns


def _const_spec(shape):
    return pl.BlockSpec(shape, lambda b, c: (0,) * len(shape))


def _ssd_fwd(proj, conv_w, conv_b, dt_bias, a_log, dskip_e, g_ssd, nb, seq):
    specs, row, cidx, ns = _ssd_specs(nb, seq, reverse=False)

    def body(uxbc_ref, halo_ref, z0_ref, z1_ref, udt_ref, cw_ref, cb_ref, dtb_ref, alog_ref, dsk_ref, gs_ref,
             yssd_ref, yssm_ref, hprev_ref, pre_ref, h_ref, yd_ref):
        c = pl.program_id(1)

        @pl.when(c == 0)
        def _():
            h_ref[...] = jnp.zeros_like(h_ref)

        for sub in range(SSD_SUB):
            rows = slice(sub * CHUNK, (sub + 1) * CHUNK)
            if sub == 0:
                halo, first = halo_ref[...], c == 0
            else:
                halo, first = uxbc_ref[sub * CHUNK - CONV_HALO:sub * CHUNK, :], False
            pre = _conv_pre(uxbc_ref[rows, :], halo, cw_ref[...], cb_ref[...], first)
            pre_ref[rows, :] = pre
            r = _chunk_terms(pre, udt_ref[rows, :], dtb_ref[...], alog_ref[...])
            xbc = r["xbc"]
            xs = xbc[:, :SSD_INNER]
            xdt = xs * r["dt_e"]
            xdt_b = xdt.astype(BF)
            xdo_b = (xdt * r["d_out"]).astype(BF)
            hprev_ref[0, sub] = h_ref[...]
            for g in range(2):
                gs = slice(g * GROUP_W, (g + 1) * GROUP_W)
                bg = xbc[:, SSD_INNER + g * SSD_STATE:SSD_INNER + (g + 1) * SSD_STATE].astype(BF)
                cg = xbc[:, SSD_INNER + (2 + g) * SSD_STATE:SSD_INNER + (3 + g) * SSD_STATE].astype(BF)
                scores = _nt(cg, bg)
                hg = h_ref[g]
                y_off = _nn(cg, hg.astype(BF)) * r["e_a"][:, gs]
                for hh in range(8):
                    h = g * 8 + hh
                    hs = slice(h * SSD_HEAD_DIM, (h + 1) * SSD_HEAD_DIM)
                    m = (scores * _head_decay(r, h)).astype(BF)
                    yd_ref[sub, :, hs] = _nn(m, xdt_b[:, hs])
                h_ref[g] = hg * r["c_dec"][:, gs] + _tn(bg, xdo_b[:, gs])
                y = yd_ref[sub, :, gs] + y_off + dsk_ref[:, gs] * xs[:, gs]
                yssm_ref[rows, gs] = y
                zg = (z0_ref if g == 0 else z1_ref)[rows, :]
                yg = y * (zg * _sigmoid(zg))
                rg = lax.rsqrt(jnp.mean(yg * yg, -1, keepdims=True) + EPS)
                yssd_ref[rows, gs] = (yg * rg * gs_ref[:, gs]).astype(BF)

    t = nb * seq
    return pl.pallas_call(
        body, name="ssd_fwd", grid=(nb, ns),
        in_specs=specs + [_const_spec((4, CONV_CH)), _const_spec((1, CONV_CH)), _const_spec((1, SSD_HEADS)),
                          _const_spec((1, SSD_HEADS)), _const_spec((1, SSD_INNER)), _const_spec((1, SSD_INNER))],
        out_specs=[pl.BlockSpec((SSD_ROWS, SSD_INNER), lambda b, c: (row(b, c), 0)),
                   pl.BlockSpec((SSD_ROWS, SSD_INNER), lambda b, c: (row(b, c), 0)),
                   pl.BlockSpec((1, SSD_SUB, 2, SSD_STATE, GROUP_W), lambda b, c: (b, c, 0, 0, 0)),
                   pl.BlockSpec((SSD_ROWS, CONV_CH), lambda b, c: (row(b, c), 0))],
        out_shape=[_out((t, SSD_INNER), BF), _out((t, SSD_INNER), F32),
                   _out((nb, seq // CHUNK, 2, SSD_STATE, GROUP_W), F32), _out((t, CONV_CH), F32)],
        scratch_shapes=[pltpu.VMEM((2, SSD_STATE, GROUP_W), F32), pltpu.VMEM((SSD_SUB, CHUNK, SSD_INNER), F32)],
        compiler_params=_cp(("arbitrary", "arbitrary"), 56),
    )(*_pin(proj, proj, proj, proj, proj, conv_w, conv_b, dt_bias, a_log, dskip_e, g_ssd))


def _out_proj(y_pool, y_ssd, w_out, x, mod3, g_mlp, seq):
    t, d = x.shape
    tm = 512
    tps = seq // tm if seq >= tm else 1
    tm = min(tm, seq)

    def body(yp_ref, ys_ref, w_ref, x_ref, mod_ref, g_ref, h1_ref, o_ref, u2_ref):
        for rows in _sub_rows(tm):
            o = _nn(jnp.concatenate([yp_ref[rows, :], ys_ref[rows, :]], 1), w_ref[...])
            o_ref[rows, :] = o.astype(BF)
            h1 = x_ref[rows, :] + mod_ref[0, 2:3, :] * o
            h1_ref[rows, :] = h1
            r = lax.rsqrt(jnp.mean(h1 * h1, -1, keepdims=True) + EPS)
            u2_ref[rows, :] = ((h1 * r * g_ref[...]) * (1.0 + mod_ref[0, 4:5, :]) + mod_ref[0, 3:4, :]).astype(BF)

    row = lambda i: (i, 0)
    return pl.pallas_call(
        body, name="out_proj", grid=(t // tm,),
        in_specs=[pl.BlockSpec((tm, POOL_WIDTH), row), pl.BlockSpec((tm, SSD_INNER), row),
                  _RESIDENT, pl.BlockSpec((tm, d), row),
                  pl.BlockSpec((1, N_MOD, d), lambda i: (i // tps, 0, 0)), pl.BlockSpec((1, d), lambda i: (0, 0))],
        out_specs=[pl.BlockSpec((tm, d), row)] * 3,
        out_shape=[_out((t, d), F32), _out((t, d), BF), _out((t, d), BF)],
        compiler_params=_cp(("parallel",), 48))(*_pin(y_pool, y_ssd), w_out, *_pin(x, mod3, g_mlp))


def _mlp_up(u2, w_up4):
    t, d = u2.shape
    tm = min(1024, t)
    nk, _, cols = w_up4.shape

    def body(u_ref, w_ref, a_ref):
        a_ref[...] = _nn(u_ref[...], w_ref[pl.program_id(1)]).astype(BF)

    return pl.pallas_call(
        body, name="mlp_up", grid=(t // tm, nk),
        in_specs=[pl.BlockSpec((tm, d), lambda i, k: (i, 0)), _RESIDENT],
        out_specs=pl.BlockSpec((tm, cols), lambda i, k: (i, k)),
        out_shape=_out((t, nk * cols), BF),
        compiler_params=_cp(("parallel", "parallel"), 32))(*_pin(u2), w_up4)


def _mlp_down_loss(a_up, w_down, h1, mod3, g_final, target, seq):
    t, d = h1.shape
    nb = t // seq
    tm = min(ROW_TILE, seq)
    tps = seq // tm

    def body(a_ref, w_ref, h1_ref, mod_ref, g_ref, tg_ref, ddn_ref, dh2_ref, sq_ref, gg_ref, dgf_ref):
        i = pl.program_id(0)

        @pl.when(i == 0)
        def _():
            sq_ref[...] = jnp.zeros_like(sq_ref)
            gg_ref[...] = jnp.zeros_like(gg_ref)

        @pl.when(i % tps == 0)
        def _():
            dgf_ref[...] = jnp.zeros_like(dgf_ref)

        gate = mod_ref[0, 5:6, :]
        sq = gg = dgf = 0.0
        for rows in _sub_rows(tm):
            f = jnp.square(jnp.maximum(a_ref[rows, :], 0))
            dn = _nn(f, w_ref[...])
            h2 = h1_ref[rows, :] + gate * dn
            r = lax.rsqrt(jnp.mean(h2 * h2, -1, keepdims=True) + EPS)
            hh = h2 * r
            err = hh * g_ref[...] - tg_ref[rows, :]
            dy = err * (1.0 / d)
            dhat = dy * g_ref[...]
            dh2 = r * (dhat - hh * jnp.mean(dhat * hh, -1, keepdims=True))
            dh2_ref[rows, :] = dh2
            ddn_ref[rows, :] = (dh2 * gate).astype(BF)
            sq = sq + jnp.sum(err * err, 0, keepdims=True)
            gg = gg + jnp.sum(dy * hh, 0, keepdims=True)
            dgf = dgf + jnp.sum(dh2 * dn, 0, keepdims=True)
        sq_ref[...] += sq
        gg_ref[...] += gg
        dgf_ref[0] += dgf

    row = lambda i: (i, 0)
    vec = pl.BlockSpec((1, d), lambda i: (0, 0))
    return pl.pallas_call(
        body, name="mlp_down_loss", grid=(t // tm,),
        in_specs=[pl.BlockSpec((tm, D_FF), row), _RESIDENT, pl.BlockSpec((tm, d), row),
                  pl.BlockSpec((1, N_MOD, d), lambda i: (i // tps, 0, 0)), vec, pl.BlockSpec((tm, d), row)],
        out_specs=[pl.BlockSpec((tm, d), row), pl.BlockSpec((tm, d), row), vec, vec,
                   pl.BlockSpec((1, 1, d), lambda i: (i // tps, 0, 0))],
        out_shape=[_out((t, d), BF), _out((t, d), F32), _out((1, d), F32),
                   _out((1, d), F32), _out((nb, 1, d), F32)],
        compiler_params=_cp(("arbitrary",), 44))(*_pin(a_up), w_down, *_pin(h1, mod3, g_final, target))


def _tn_matmul(a, b, tk, tn, name, square_relu=False, out3=False):
    t, kdim = a.shape
    ndim = b.shape[1]

    def body(a_ref, b_ref, o_ref):
        av = a_ref[...]
        if square_relu:
            av = jnp.square(jnp.maximum(av, 0))
        res = _tn(av, b_ref[...]).astype(BF)
        if out3:
            o_ref[0] = res
        else:
            o_ref[...] = res

    if out3:
        out_spec = pl.BlockSpec((1, tk, tn), lambda j, i: (j, i, 0))
        out_shape = _out((ndim // tn, kdim, tn), BF)
    else:
        out_spec = pl.BlockSpec((tk, tn), lambda j, i: (i, j))
        out_shape = _out((kdim, ndim), BF)
    return pl.pallas_call(
        body, name=name, grid=(ndim // tn, kdim // tk),
        in_specs=[pl.BlockSpec((t, tk), lambda j, i: (0, i)), pl.BlockSpec((t, tn), lambda j, i: (0, j))],
        out_specs=out_spec, out_shape=out_shape,
        compiler_params=_cp(("parallel", "parallel"), 56))(*_pin(a, b))


def _mlp_down_bwd(d_dn, w_down4, a_up, token):
    t, d = d_dn.shape
    tm = min(1024, t)
    nk, rows, _ = w_down4.shape

    def body(g_ref, w_ref, a_ref, tok_ref, o_ref):
        df = _nt(g_ref[...], w_ref[pl.program_id(1)])
        o_ref[...] = (df * (2.0 * jnp.maximum(a_ref[...], 0).astype(F32))).astype(BF)

    return pl.pallas_call(
        body, name="mlp_down_bwd", grid=(t // tm, nk),
        in_specs=[pl.BlockSpec((tm, d), lambda i, k: (i, 0)), _RESIDENT,
                  pl.BlockSpec((tm, rows), lambda i, k: (i, k)), _token_spec()],
        out_specs=pl.BlockSpec((tm, rows), lambda i, k: (i, k)),
        out_shape=_out((t, nk * rows), BF),
        compiler_params=_cp(("parallel", "parallel"), 32))(*_pin(d_dn), w_down4, *_pin(a_up, token))


def _mlp_up_bwd(d_a, w_up4, h1, dh2, o, mod3, g_mlp, seq, token):
    t, d = h1.shape
    nb = t // seq
    tm = min(ROW_TILE, seq)
    tps = seq // tm
    nk = w_up4.shape[0]
    cols = w_up4.shape[2]

    def body(da_ref, w_ref, h1_ref, dh2_ref, o_ref, mod_ref, g_ref, tok_ref, dh1_ref, do_ref, acc_ref, gg_ref):
        i = pl.program_id(0)

        @pl.when(i == 0)
        def _():
            gg_ref[...] = jnp.zeros_like(gg_ref)

        @pl.when(i % tps == 0)
        def _():
            acc_ref[...] = jnp.zeros_like(acc_ref)

        gg = a_shift = a_scale = a_gate = 0.0
        for rows in _sub_rows(tm):
            du = _nt(da_ref[rows, 0:cols], w_ref[0])
            for k in range(1, nk):
                du = du + _nt(da_ref[rows, k * cols:(k + 1) * cols], w_ref[k])
            h1 = h1_ref[rows, :]
            r = lax.rsqrt(jnp.mean(h1 * h1, -1, keepdims=True) + EPS)
            hh = h1 * r
            n2 = hh * g_ref[...]
            dn2 = du * (1.0 + mod_ref[0, 4:5, :])
            dhat = dn2 * g_ref[...]
            dh1 = dh2_ref[rows, :] + r * (dhat - hh * jnp.mean(dhat * hh, -1, keepdims=True))
            dh1_ref[rows, :] = dh1
            do_ref[rows, :] = (dh1 * mod_ref[0, 2:3, :]).astype(BF)
            gg = gg + jnp.sum(dn2 * hh, 0, keepdims=True)
            a_shift = a_shift + jnp.sum(du, 0, keepdims=True)
            a_scale = a_scale + jnp.sum(du * n2, 0, keepdims=True)
            a_gate = a_gate + jnp.sum(dh1 * o_ref[rows, :].astype(F32), 0, keepdims=True)
        gg_ref[...] += gg
        acc_ref[0, 0:1, :] += a_shift
        acc_ref[0, 1:2, :] += a_scale
        acc_ref[0, 2:3, :] += a_gate

    row = lambda i: (i, 0)
    vec = pl.BlockSpec((1, d), lambda i: (0, 0))
    return pl.pallas_call(
        body, name="mlp_up_bwd", grid=(t // tm,),
        in_specs=[pl.BlockSpec((tm, D_FF), row), _RESIDENT, pl.BlockSpec((tm, d), row),
                  pl.BlockSpec((tm, d), row), pl.BlockSpec((tm, d), row),
                  pl.BlockSpec((1, N_MOD, d), lambda i: (i // tps, 0, 0)), vec, _token_spec()],
        out_specs=[pl.BlockSpec((tm, d), row), pl.BlockSpec((tm, d), row),
                   pl.BlockSpec((1, 8, d), lambda i: (i // tps, 0, 0)), vec],
        out_shape=[_out((t, d), F32), _out((t, d), BF),
                   _out((nb, 8, d), F32), _out((1, d), F32)],
        compiler_params=_cp(("arbitrary",), 44))(*_pin(d_a), w_up4, *_pin(h1, dh2, o, mod3, g_mlp, token))


def _out_proj_bwd(d_o, w_out, token):
    t, d = d_o.shape
    tm = min(512, t)

    def body(g_ref, w_ref, tok_ref, dp_ref, ds_ref):
        gv = g_ref[...]
        dp_ref[...] = _nt(gv, w_ref[0:POOL_WIDTH, :])
        ds_ref[...] = _nt(gv, w_ref[POOL_WIDTH:, :])

    row = lambda i: (i, 0)
    return pl.pallas_call(
        body, name="out_proj_bwd", grid=(t // tm,),
        in_specs=[pl.BlockSpec((tm, d), row), _RESIDENT, _token_spec()],
        out_specs=[pl.BlockSpec((tm, POOL_WIDTH), row), pl.BlockSpec((tm, SSD_INNER), row)],
        out_shape=[_out((t, POOL_WIDTH), F32), _out((t, SSD_INNER), F32)],
        compiler_params=_cp(("parallel",), 32))(*_pin(d_o), w_out, *_pin(token))


def _pool_bwd(d_ypool, p, w_pool, pool_scale, nb, seq):
    ts = _pool_tile(seq)
    nt = seq // ts
    hb = ts // HALO
    last_block = nb * seq // HALO - 1

    def body(dy_ref, halo_ref, p_ref, wp_ref, ps_ref, du_ref, gw_ref, gs_ref):
        b = pl.program_id(0)
        i = pl.program_id(1)

        @pl.when((b == 0) & (i == 0))
        def _():
            gw_ref[...] = jnp.zeros_like(gw_ref)
            gs_ref[...] = jnp.zeros_like(gs_ref)

        halo = jnp.where(i == nt - 1, 0.0, halo_ref[...])
        dy = dy_ref[...]
        ext = jnp.concatenate([dy, halo], 0)
        tpos = i * ts + _iota((ts + HALO, 1), 0)
        n_ext = ts + HALO
        for g, w in enumerate(POOL_WINDOWS):
            gs = slice(g * POOL_GROUP, (g + 1) * POOL_GROUP)
            wg = wp_ref[g].astype(BF)
            pg = p_ref[:, gs]
            pw = _nn(pg, wg)
            gs_ref[:, gs] += jnp.sum(dy[:, gs] * pw, 0, keepdims=True)
            dpw = (ext[:, gs] * ps_ref[:, gs]).astype(BF)
            gw_ref[g] += _tn(pg, dpw[:ts])
            dp = _nt(dpw, wg)
            cnt = jnp.minimum(tpos + 1, w).astype(F32)
            s = dp / cnt
            sh = 1
            while sh < w:
                s = s + pltpu.roll(s, n_ext - sh, 0)
                sh *= 2
            du_ref[:, gs] = (s[:ts] - dp[:ts]).astype(BF)

    return pl.pallas_call(
        body, name="pool_bwd", grid=(nb, nt),
        in_specs=[pl.BlockSpec((ts, POOL_WIDTH), lambda b, i: (b * nt + i, 0)),
                  pl.BlockSpec((HALO, POOL_WIDTH), lambda b, i: (jnp.minimum((b * nt + i + 1) * hb, last_block), 0)),
                  pl.BlockSpec((ts, POOL_WIDTH), lambda b, i: (b * nt + i, 0)),
                  pl.BlockSpec((4, POOL_GROUP, POOL_GROUP), lambda b, i: (0, 0, 0)),
                  pl.BlockSpec((1, POOL_WIDTH), lambda b, i: (0, 0))],
        out_specs=[pl.BlockSpec((ts, POOL_WIDTH), lambda b, i: (b * nt + i, 0)),
                   pl.BlockSpec((4, POOL_GROUP, POOL_GROUP), lambda b, i: (0, 0, 0)),
                   pl.BlockSpec((1, POOL_WIDTH), lambda b, i: (0, 0))],
        out_shape=[_out((nb * seq, POOL_WIDTH), BF), _out((4, POOL_GROUP, POOL_GROUP), F32),
                   _out((1, POOL_WIDTH), F32)],
        compiler_params=_cp(("arbitrary", "arbitrary"), 32))(*_pin(d_ypool, d_ypool, p, w_pool, pool_scale))


def _ssd_bwd(proj, pre, d_yssd, yssm, h_prev, dt_bias, a_log, dskip_e, g_ssd, nb, seq):
    specs, row, cidx, ns = _ssd_specs(nb, seq, reverse=True)
    specs = specs[2:]

    def body(z0_ref, z1_ref, udt_ref, pre_ref, dys_ref, yssm_ref, hprev_ref,
             dtb_ref, alog_ref, dsk_ref, gs_ref,
             dz_ref, dpre_ref, dudt_ref, ggs_ref, gdsk_ref, ga_ref, gdtb_ref,
             g_ref, dxdt_ref, dyv_ref):
        b = pl.program_id(0)
        c = pl.program_id(1)

        @pl.when(c == 0)
        def _():
            g_ref[...] = jnp.zeros_like(g_ref)

        @pl.when((b == 0) & (c == 0))
        def _():
            ggs_ref[...] = jnp.zeros_like(ggs_ref)
            gdsk_ref[...] = jnp.zeros_like(gdsk_ref)
            ga_ref[...] = jnp.zeros_like(ga_ref)
            gdtb_ref[...] = jnp.zeros_like(gdtb_ref)

        for sub in reversed(range(SSD_SUB)):
            chunk(sub, z0_ref, z1_ref, udt_ref, pre_ref, dys_ref, yssm_ref, hprev_ref, dtb_ref, alog_ref, dsk_ref, gs_ref,
                  dz_ref, dpre_ref, dudt_ref, ggs_ref, gdsk_ref, ga_ref, gdtb_ref, g_ref, dxdt_ref.at[sub], dyv_ref.at[sub])

    def chunk(sub, z0_ref, z1_ref, udt_ref, pre_ref, dys_ref, yssm_ref, hprev_ref,
              dtb_ref, alog_ref, dsk_ref, gs_ref,
              dz_ref, dpre_ref, dudt_ref, ggs_ref, gdsk_ref, ga_ref, gdtb_ref,
              g_ref, dxdt_ref, dyv_ref):
        rows = slice(sub * CHUNK, (sub + 1) * CHUNK)
        r = _chunk_terms(pre_ref[rows, :], udt_ref[rows, :], dtb_ref[...], alog_ref[...])
        xbc = r["xbc"]
        xs = xbc[:, :SSD_INNER]
        dt_e = r["dt_e"]
        xdt = xs * dt_e
        xdt_b = xdt.astype(BF)
        reduce_m = _head_reduce_matrix(GROUP_W, 8)

        def head_sums(v):
            return _nn(v.astype(BF), reduce_m)

        onehot16 = lambda h: (_iota((1, SSD_HEADS), 1) == h).astype(F32)
        onecol16 = lambda h: (_iota((SSD_HEADS, 1), 0) == h).astype(F32)

        d_acum = jnp.zeros((CHUNK, SSD_HEADS), F32)
        d_acum_t = jnp.zeros((SSD_HEADS, CHUNK), F32)
        d_alast = jnp.zeros((1, SSD_HEADS), F32)
        place8 = lambda g: (_iota((8, SSD_HEADS), 1) == _iota((8, SSD_HEADS), 0) + 8 * g).astype(BF)
        d_b, d_c = [], []
        for g in range(2):
            gs = slice(g * GROUP_W, (g + 1) * GROUP_W)
            zg = (z0_ref if g == 0 else z1_ref)[rows, :]
            sz = _sigmoid(zg)
            silu_z = zg * sz
            ys = yssm_ref[rows, gs]
            yg = ys * silu_z
            rg = lax.rsqrt(jnp.mean(yg * yg, -1, keepdims=True) + EPS)
            yh = yg * rg
            dys = dys_ref[rows, gs]
            ggs_ref[:, gs] += jnp.sum(dys * yh, 0, keepdims=True)
            dyh = dys * gs_ref[:, gs]
            dyg = rg * (dyh - yh * jnp.mean(dyh * yh, -1, keepdims=True))
            dy = dyg * silu_z
            dz_ref[rows, gs] = (dyg * ys * (sz * (1.0 + zg * (1.0 - sz)))).astype(BF)
            gdsk_ref[:, gs] += jnp.sum(dy * xs[:, gs], 0, keepdims=True)
            dyv_ref[:, gs] = dy
            dy_b = dy.astype(BF)

            bg = xbc[:, SSD_INNER + g * SSD_STATE:SSD_INNER + (g + 1) * SSD_STATE].astype(BF)
            cg = xbc[:, SSD_INNER + (2 + g) * SSD_STATE:SSD_INNER + (3 + g) * SSD_STATE].astype(BF)
            scores = _nt(cg, bg)
            hg = hprev_ref[0, sub, g]
            hg_b = hg.astype(BF)
            gg = g_ref[g]
            gg_b = gg.astype(BF)
            e_a = r["e_a"][:, gs]
            d_out = r["d_out"][:, gs]
            c_dec = r["c_dec"][:, gs]
            zc = _nn(cg, hg_b)
            wv = e_a * dy
            wv_b = wv.astype(BF)
            da_g = head_sums(wv * zc)
            dcg = _nt(wv_b, hg_b)
            d_hprev = _tn(cg, wv_b)
            vg = _nn(bg, gg_b)
            dxdt_g = d_out * vg
            dd_out = head_sums(xdt[:, gs] * vg)
            dbg = _nt((xdt[:, gs] * d_out).astype(BF), gg_b)
            dcd = _exact_nn(jnp.sum(gg * hg, 0, keepdims=True), reduce_m)
            d_out8 = jnp.exp(r["acum"][CHUNK - 1:CHUNK, 8 * g:8 * g + 8] - r["acum"][:, 8 * g:8 * g + 8])
            c_dec8 = jnp.exp(r["acum"][CHUNK - 1:CHUNK, 8 * g:8 * g + 8])
            t8 = dd_out * d_out8
            d_alast = d_alast + _exact_nn(jnp.sum(t8, 0, keepdims=True) + dcd * c_dec8, place8(g))
            d_acum = d_acum + _exact_nn(da_g - t8, place8(g))
            dsc = jnp.zeros((CHUNK, CHUNK), F32)
            for hh in range(8):
                h = g * 8 + hh
                hs = slice(h * SSD_HEAD_DIM, (h + 1) * SSD_HEAD_DIM)
                lam = _head_decay(r, h)
                m = scores * lam
                dyh_b = dy_b[:, hh * SSD_HEAD_DIM:(hh + 1) * SSD_HEAD_DIM]
                dm = _nt(dyh_b, xdt_b[:, hs])
                tm_ = dm * m
                d_acum = d_acum + jnp.sum(tm_, 1, keepdims=True) * onehot16(h)
                d_acum_t = d_acum_t + onecol16(h) * jnp.sum(tm_, 0, keepdims=True)
                dsc = dsc + dm * lam
                dxdt_ref[:, hs] = _tn(m.astype(BF), dyh_b) + dxdt_g[:, hh * SSD_HEAD_DIM:(hh + 1) * SSD_HEAD_DIM]
            dsc_b = dsc.astype(BF)
            d_c.append(dcg + _nn(dsc_b, bg))
            d_b.append(dbg + _tn(dsc_b, cg))
            g_ref[g] = d_hprev + c_dec * gg

        eye = (_iota((CHUNK, CHUNK), 0) == _iota((CHUNK, CHUNK), 1)).astype(BF)
        d_acum = d_acum - _exact_nt_left(eye, d_acum_t)
        is_last = (_iota((CHUNK, 1), 0) == CHUNK - 1).astype(F32)
        d_acum = d_acum + is_last * d_alast
        triu = (_iota((CHUNK, CHUNK), 0) <= _iota((CHUNK, CHUNK), 1)).astype(BF)
        d_da = _exact_nn_left(triu, d_acum)
        dt = r["dt"]
        ga_ref[...] += jnp.sum(d_da * dt, 0, keepdims=True)
        dxdt = dxdt_ref[...]
        reduce16 = _head_reduce_matrix(SSD_INNER, SSD_HEADS)
        d_dt = d_da * r["a"] + _nn((dxdt * xs).astype(BF), reduce16)
        d_udt = d_dt * _sigmoid(r["dtp"])
        gdtb_ref[...] += jnp.sum(d_udt, 0, keepdims=True)
        dudt_ref[rows, :] = jnp.zeros((CHUNK, dudt_ref.shape[1]), BF)
        dudt_ref[rows, 0:SSD_HEADS] = d_udt.astype(BF)
        pre, sg = r["pre"], r["sg"]
        dsilu = sg * (1.0 + pre * (1.0 - sg))
        dpre_ref[rows, 0:SSD_INNER] = (dsk_ref[...] * dyv_ref[...] + dxdt * dt_e) * dsilu[:, 0:SSD_INNER]
        for g in range(2):
            bs = slice(SSD_INNER + g * SSD_STATE, SSD_INNER + (g + 1) * SSD_STATE)
            cs = slice(SSD_INNER + (2 + g) * SSD_STATE, SSD_INNER + (3 + g) * SSD_STATE)
            dpre_ref[rows, bs] = d_b[g] * dsilu[:, bs]
            dpre_ref[rows, cs] = d_c[g] * dsilu[:, cs]

    t = nb * seq
    vec = _const_spec((1, SSD_INNER))
    small = _const_spec((1, SSD_HEADS))
    return pl.pallas_call(
        body, name="ssd_bwd", grid=(nb, ns),
        in_specs=specs + [pl.BlockSpec((SSD_ROWS, CONV_CH), lambda b, c: (row(b, c), 0)),
                          pl.BlockSpec((SSD_ROWS, SSD_INNER), lambda b, c: (row(b, c), 0)),
                          pl.BlockSpec((SSD_ROWS, SSD_INNER), lambda b, c: (row(b, c), 0)),
                          pl.BlockSpec((1, SSD_SUB, 2, SSD_STATE, GROUP_W), lambda b, c: (b, cidx(c), 0, 0, 0)),
                          small, small, vec, vec],
        out_specs=[pl.BlockSpec((SSD_ROWS, SSD_INNER), lambda b, c: (row(b, c), 0)),
                   pl.BlockSpec((SSD_ROWS, CONV_CH), lambda b, c: (row(b, c), 0)),
                   pl.BlockSpec((SSD_ROWS, 128), lambda b, c: (row(b, c), 0)),
                   vec, vec, small, small],
        out_shape=[_out((t, SSD_INNER), BF), _out((t, CONV_CH), F32),
                   _out((t, 128), BF), _out((1, SSD_INNER), F32),
                   _out((1, SSD_INNER), F32), _out((1, SSD_HEADS), F32),
                   _out((1, SSD_HEADS), F32)],
        scratch_shapes=[pltpu.VMEM((2, SSD_STATE, GROUP_W), F32), pltpu.VMEM((SSD_SUB, CHUNK, SSD_INNER), F32),
                        pltpu.VMEM((SSD_SUB, CHUNK, SSD_INNER), F32)],
        compiler_params=_cp(("arbitrary", "arbitrary"), 56),
    )(*_pin(proj, proj, proj, pre, d_yssd, yssm, h_prev, dt_bias, a_log, dskip_e, g_ssd))


def _grad_w_out(y_pool, y_ssd, d_o):
    t, d = d_o.shape
    tk = POOL_WIDTH
    n_s = SSD_INNER // tk

    def body(p_ref, s_ref, g_ref, o_ref):
        i = pl.program_id(0)

        @pl.when(i == 0)
        def _():
            o_ref[...] = _tn(p_ref[...], g_ref[...]).astype(BF)

        @pl.when(i > 0)
        def _():
            o_ref[...] = _tn(s_ref[...], g_ref[...]).astype(BF)

    return pl.pallas_call(
        body, name="grad_w_out", grid=(1 + n_s,),
        in_specs=[pl.BlockSpec((t, tk), lambda i: (0, 0)), pl.BlockSpec((t, tk), lambda i: (0, jnp.maximum(i - 1, 0))),
                  pl.BlockSpec((t, d), lambda i: (0, 0))],
        out_specs=pl.BlockSpec((tk, d), lambda i: (i, 0)),
        out_shape=_out((POOL_WIDTH + SSD_INNER, d), BF),
        compiler_params=_cp(("parallel",), 56))(*_pin(y_pool, y_ssd, d_o))


def _grad_w_in_t(d_upool, d_z, d_uxbc, d_udt, u1):
    t, d = u1.shape
    tk = 512
    n_z, n_x = SSD_INNER // tk, CONV_CH // tk

    def body(p_ref, z_ref, x_ref, dt_ref, u_ref, o_ref):
        i = pl.program_id(0)

        @pl.when(i == 0)
        def _():
            o_ref[...] = _tn(p_ref[...], u_ref[...]).astype(BF)

        @pl.when((i >= 1) & (i < 1 + n_z))
        def _():
            o_ref[...] = _tn(z_ref[...], u_ref[...]).astype(BF)

        @pl.when((i >= 1 + n_z) & (i < 1 + n_z + n_x))
        def _():
            o_ref[...] = _tn(x_ref[...], u_ref[...]).astype(BF)

        @pl.when(i == 1 + n_z + n_x)
        def _():
            o_ref[0:128, :] = _tn(dt_ref[...], u_ref[...]).astype(BF)

    return pl.pallas_call(
        body, name="grad_w_in", grid=(2 + n_z + n_x,),
        in_specs=[pl.BlockSpec((t, tk), lambda i: (0, 0)),
                  pl.BlockSpec((t, tk), lambda i: (0, jnp.clip(i - 1, 0, n_z - 1))),
                  pl.BlockSpec((t, tk), lambda i: (0, jnp.clip(i - 1 - n_z, 0, n_x - 1))),
                  pl.BlockSpec((t, 128), lambda i: (0, 0)), pl.BlockSpec((t, d), lambda i: (0, 0))],
        out_specs=pl.BlockSpec((tk, d), lambda i: (i, 0)),
        out_shape=_out((IN_PAD, d), BF),
        compiler_params=_cp(("parallel",), 56))(*_pin(d_upool, d_z, d_uxbc, d_udt, u1))


def _conv_bwd(d_pre, proj, conv_w, nb, seq):
    ts = min(256, seq)
    nt = seq // ts
    hb = ts // CONV_HALO
    last_block = nb * seq // CONV_HALO - 1
    n_ext = CHUNK + CONV_HALO

    def body(dp_ref, dnext_ref, u_ref, cw_ref, du_ref, gw_ref, gb_ref):
        b = pl.program_id(0)
        i = pl.program_id(1)

        @pl.when((b == 0) & (i == 0))
        def _():
            gw_ref[...] = jnp.zeros_like(gw_ref)
            gb_ref[...] = jnp.zeros_like(gb_ref)

        for c0 in range(0, CONV_CH, 128):
            cs = slice(c0, c0 + 128)
            cw = cw_ref[:, cs]
            gw = [0.0] * 4
            gb = 0.0
            for r0 in range(0, ts, CHUNK):
                dp = dp_ref[r0:r0 + CHUNK, cs]
                u = u_ref[r0:r0 + CHUNK, cs]
                if r0 + CHUNK < ts:
                    below = dp_ref[r0 + CHUNK:r0 + CHUNK + CONV_HALO, cs]
                else:
                    below = jnp.where(i == nt - 1, 0.0, dnext_ref[:, cs])
                ext_d = jnp.concatenate([dp, below], 0)
                du = dp * cw[3:4]
                gw[3] = gw[3] + jnp.sum(dp * u, 0, keepdims=True)
                for k in (2, 1, 0):
                    shifted = pltpu.roll(ext_d, n_ext - (3 - k), 0)[:CHUNK]
                    du = du + shifted * cw[k:k + 1]
                    gw[k] = gw[k] + jnp.sum(shifted * u, 0, keepdims=True)
                gb = gb + jnp.sum(dp, 0, keepdims=True)
                du_ref[r0:r0 + CHUNK, cs] = du.astype(BF)
            for k in range(4):
                gw_ref[k:k + 1, cs] += gw[k]
            gb_ref[:, cs] += gb

    return pl.pallas_call(
        body, name="conv_bwd", grid=(nb, nt),
        in_specs=[pl.BlockSpec((ts, CONV_CH), lambda b, i: (b * nt + i, 0)),
                  pl.BlockSpec((CONV_HALO, CONV_CH), lambda b, i: (jnp.minimum((b * nt + i + 1) * hb, last_block), 0)),
                  pl.BlockSpec((ts, CONV_CH), lambda b, i: (b * nt + i, 1)),
                  pl.BlockSpec((4, CONV_CH), lambda b, i: (0, 0))],
        out_specs=[pl.BlockSpec((ts, CONV_CH), lambda b, i: (b * nt + i, 0)),
                   pl.BlockSpec((8, CONV_CH), lambda b, i: (0, 0)), pl.BlockSpec((1, CONV_CH), lambda b, i: (0, 0))],
        out_shape=[_out((nb * seq, CONV_CH), BF), _out((8, CONV_CH), F32),
                   _out((1, CONV_CH), F32)],
        compiler_params=_cp(("arbitrary", "arbitrary"), 48))(*_pin(d_pre, d_pre, proj, conv_w))


def _in_proj_bwd(d_parts, w_in_t, x, dh1, mod3, g_mix, seq, token):
    t, d = x.shape
    nb = t // seq
    tm = min(ROW_TILE, seq)
    tps = seq // tm

    widths = [p.shape[1] for p in d_parts]

    def body(d0_ref, d1_ref, d2_ref, d3_ref, w_ref, x_ref, dh1_ref, mod_ref, g_ref, tok_ref, gx_ref, acc_ref, gg_ref):
        i = pl.program_id(0)

        @pl.when(i == 0)
        def _():
            gg_ref[...] = jnp.zeros_like(gg_ref)

        @pl.when(i % tps == 0)
        def _():
            acc_ref[...] = jnp.zeros_like(acc_ref)

        gg = a_shift = a_scale = 0.0
        for rows in _sub_rows(tm):
            d_cat = jnp.concatenate([p_ref[rows, :] for p_ref in (d0_ref, d1_ref, d2_ref, d3_ref)], 1)
            du = _nn(d_cat, w_ref[...])
            xv = x_ref[rows, :]
            r = lax.rsqrt(jnp.mean(xv * xv, -1, keepdims=True) + EPS)
            hh = xv * r
            n1 = hh * g_ref[...]
            dn1 = du * (1.0 + mod_ref[0, 1:2, :])
            dhat = dn1 * g_ref[...]
            gx_ref[rows, :] = dh1_ref[rows, :] + r * (dhat - hh * jnp.mean(dhat * hh, -1, keepdims=True))
            gg = gg + jnp.sum(dn1 * hh, 0, keepdims=True)
            a_shift = a_shift + jnp.sum(du, 0, keepdims=True)
            a_scale = a_scale + jnp.sum(du * n1, 0, keepdims=True)
        gg_ref[...] += gg
        acc_ref[0, 0:1, :] += a_shift
        acc_ref[0, 1:2, :] += a_scale

    row = lambda i: (i, 0)
    vec = pl.BlockSpec((1, d), lambda i: (0, 0))
    return pl.pallas_call(
        body, name="in_proj_bwd", grid=(t // tm,),
        in_specs=[pl.BlockSpec((tm, wd), row) for wd in widths] +
                 [_RESIDENT, pl.BlockSpec((tm, d), row),
                  pl.BlockSpec((tm, d), row), pl.BlockSpec((1, N_MOD, d), lambda i: (i // tps, 0, 0)), vec, _token_spec()],
        out_specs=[pl.BlockSpec((tm, d), row), pl.BlockSpec((1, 8, d), lambda i: (i // tps, 0, 0)), vec],
        out_shape=[_out((t, d), F32), _out((nb, 8, d), F32),
                   _out((1, d), F32)],
        compiler_params=_cp(("arbitrary",), 40))(*_pin(*d_parts), w_in_t, *_pin(x, dh1, mod3, g_mix, token))


_VEC_LAYOUT = (("g_mix", 1024), ("conv_b", 1536), ("g_ssd", 1024), ("pool_scale", 512), ("g_mlp", 1024),
               ("g_final", 1024), ("dt_bias", 128), ("a_log", 128), ("d_skip_lanes", 1024), ("sq_err", 1024))
_VEC_OFFSET = {}
_off = 0
for _name, _n in _VEC_LAYOUT:
    _VEC_OFFSET[_name] = _off
    _off += _n
_VEC_LANES = _off
_SMALL_PARAMS = ("b_ada", "g_mix", "conv_w", "conv_b", "dt_bias", "a_log", "d_skip", "g_ssd", "w_pool", "pool_scale",
                 "g_mlp", "g_final")


def _pack_vec(parts):
    cols = []
    for name, n in _VEC_LAYOUT:
        v = parts[name]
        if v.shape[1] < n:
            v = jnp.pad(v, ((0, 0), (0, n - v.shape[1])))
        cols.append(v)
    return jnp.concatenate(cols, 1)


def _small_adam(vec_all, wpool_all, convw_all, dmod_all, params):
    names = _SMALL_PARAMS
    nin = 4 + 3 * len(names)

    def body(*refs):
        vec_ref, wp_ref, cw_ref, dm_ref = refs[:4]
        prm = {n: refs[4 + 3 * i:7 + 3 * i] for i, n in enumerate(names)}
        loss_ref = refs[nin]
        outs = {n: refs[nin + 1 + 4 * i:nin + 5 + 4 * i] for i, n in enumerate(names)}
        vsum = vec_ref[0]
        for s in range(1, N_DEV):
            vsum = vsum + vec_ref[s]

        def lanes(name, n):
            off = _VEC_OFFSET[name]
            return vsum[:, off:off + n]

        grads = {n: lanes(n, prm[n][0].shape[1]) for n in ("g_mix", "conv_b", "g_ssd", "pool_scale", "g_mlp", "g_final", "dt_bias")}
        grads["a_log"] = lanes("a_log", SSD_HEADS) * (-jnp.exp(prm["a_log"][0][...]))
        per_lane = jnp.broadcast_to(lanes("d_skip_lanes", SSD_INNER), (8, SSD_INNER))
        grads["d_skip"] = _exact_nn(per_lane, _head_reduce_matrix(SSD_INNER, SSD_HEADS))[0:1]
        gwp = wp_ref[0].astype(F32)
        gcw = cw_ref[0]
        gb = jnp.sum(dm_ref[0], 0, keepdims=True)
        for s in range(1, N_DEV):
            gwp = gwp + wp_ref[s].astype(F32)
            gcw = gcw + cw_ref[s]
            gb = gb + jnp.sum(dm_ref[s], 0, keepdims=True)
        grads["w_pool"] = gwp
        grads["conv_w"] = gcw[0:4]
        grads["b_ada"] = gb
        total = jnp.sum(lanes("sq_err", D_MODEL), 1, keepdims=True) * (0.5 / D_MODEL)
        loss_ref[...] = jnp.broadcast_to(total, loss_ref.shape)
        for n in names:
            w_ref, m_ref, v_ref = prm[n]
            g = grads[n]
            d, m2, v2 = _adam_math(w_ref[...], g, m_ref[...], v_ref[...])
            g_ref, d_ref, m2_ref, v2_ref = outs[n]
            g_ref[...] = g
            d_ref[...] = d
            m2_ref[...] = m2
            v2_ref[...] = v2

    flat = [vec_all, wpool_all, convw_all, dmod_all]
    out_shape = [jax.ShapeDtypeStruct((1, 128), F32)]
    for n in names:
        flat += list(params[n])
        out_shape += [jax.ShapeDtypeStruct(params[n][0].shape, F32)] * 4
    vm = pl.BlockSpec(memory_space=pltpu.VMEM)
    res = pl.pallas_call(body, name="small_adam", out_shape=out_shape, in_specs=[vm] * len(flat),
                         out_specs=[vm] * len(out_shape), compiler_params=_cp(vmem_mb=48))(*flat)
    return res[0], {n: res[1 + 4 * i:5 + 4 * i] for i, n in enumerate(names)}


_WEIGHTS = ("w_ada", "b_ada", "g_mix", "w_in", "conv_w", "conv_b", "dt_bias", "a_log", "d_skip", "g_ssd", "w_pool",
            "pool_scale", "w_out", "g_mlp", "w_up", "w_down", "g_final")


def _local_step(x2, tg2, mod3, seq, w_in_t, first_token, weights_later, start_reduce, before_last, conv_w_full, sp):
    t, d = x2.shape
    nb = t // seq
    dskip_e = jnp.repeat(sp["d_skip"], SSD_HEAD_DIM, axis=1)
    proj, u1 = _in_proj(x2, mod3, sp["g_mix"], w_in_t, seq, first_token)
    y_pool, p = _pool_fwd(proj, sp["w_pool"], sp["pool_scale"], nb, seq)
    y_ssd, yssm, h_prev, pre = _ssd_fwd(proj, conv_w_full, sp["conv_b"], sp["dt_bias"], sp["a_log"], dskip_e, sp["g_ssd"], nb, seq)
    w_out_f, w_up4, w_down4 = weights_later(y_ssd)
    w_down_f = w_down4.reshape(D_FF, d)
    h1, o, u2 = _out_proj(y_pool, y_ssd, w_out_f, x2, mod3, sp["g_mlp"], seq)
    a_up = _mlp_up(u2, w_up4)
    d_dn, dh2, sq, gg_final, d_gf = _mlp_down_loss(a_up, w_down_f, h1, mod3, sp["g_final"], tg2, seq)

    gw_down = _tn_matmul(a_up, d_dn, 512, d, "grad_w_down", square_relu=True)
    tok = start_reduce("w_down", gw_down.reshape(N_CHIPS, D_FF // N_CHIPS, d))
    d_a = _mlp_down_bwd(d_dn, w_down4, a_up, tok)
    gw_up4 = _tn_matmul(u2, d_a, 512, d, "grad_w_up", out3=True)
    tok = start_reduce("w_up", gw_up4)
    dh1, d_o, accf, gg_mlp = _mlp_up_bwd(d_a, w_up4, h1, dh2, o, mod3, sp["g_mlp"], seq, tok)
    gw_out = _grad_w_out(y_pool, y_ssd, d_o)
    tok = start_reduce("w_out", gw_out.reshape(N_CHIPS, gw_out.shape[0] // N_CHIPS, d))
    d_ypool, d_yssd = _out_proj_bwd(d_o, w_out_f, tok)
    d_upool, gw_pool, g_ps = _pool_bwd(d_ypool, p, sp["w_pool"], sp["pool_scale"], nb, seq)
    d_z, d_pre, d_udt, gg_ssd, gdsk, ga, gdtb = _ssd_bwd(proj, pre, d_yssd, yssm, h_prev, sp["dt_bias"], sp["a_log"],
                                                        dskip_e, sp["g_ssd"], nb, seq)
    d_uxbc, gconvw, gconvb = _conv_bwd(d_pre, proj, conv_w_full, nb, seq)
    gw_in_t = _grad_w_in_t(d_upool, d_z, d_uxbc, d_udt, u1)
    tok = start_reduce("w_in", gw_in_t[:IN_WIDTH].reshape(N_CHIPS, IN_WIDTH // N_CHIPS, d))
    gx, accm, gg_mix = _in_proj_bwd([d_upool, d_z, d_uxbc, d_udt], w_in_t, x2, dh1, mod3, sp["g_mix"], seq, before_last(tok))

    d_mod = jnp.concatenate([accm[:, 0], accm[:, 1], accf[:, 2], accf[:, 0], accf[:, 1], d_gf[:, 0]], 1)
    vec = _pack_vec({"g_mix": gg_mix, "conv_b": gconvb, "g_ssd": gg_ssd, "pool_scale": g_ps, "g_mlp": gg_mlp,
                     "g_final": gg_final, "dt_bias": gdtb, "a_log": ga, "d_skip_lanes": gdsk, "sq_err": sq})
    return gx, d_mod, vec, gw_pool, gconvw


def kernel(x, c, w_ada, b_ada, g_mix, w_in, conv_w, conv_b, dt_bias, a_log, d_skip, g_ssd, w_pool, pool_scale, w_out, g_mlp, w_up, w_down, g_final, loss_target, m_w_ada, m_b_ada, m_g_mix, m_w_in, m_conv_w, m_conv_b, m_dt_bias, m_a_log, m_d_skip, m_g_ssd, m_w_pool, m_pool_scale, m_w_out, m_g_mlp, m_w_up, m_w_down, m_g_final, v_w_ada, v_b_ada, v_g_mix, v_w_in, v_conv_w, v_conv_b, v_dt_bias, v_a_log, v_d_skip, v_g_ssd, v_w_pool, v_pool_scale, v_w_out, v_g_mlp, v_w_up, v_w_down, v_g_final):
    nb, seq, d = x.shape
    t = nb * seq
    xi, yi, ci = _mesh_pos()
    chip = 2 * xi + yi
    me = 4 * xi + 2 * yi + ci
    ada_cols = w_ada.shape[2]
    conv_cols = conv_w.shape[2]
    in_cols = w_in.shape[2]
    w_in_s, m_w_in_s, v_w_in_s = w_in[0].T, m_w_in[0].T, v_w_in[0].T

    c8, convw8 = _all_gather_small([c, conv_w[0]])
    w_in_b = w_in_s.astype(BF)
    i_send, i_recv, i_src, i_land, in_token = _ici_start(
        [w_in_b], [jax.ShapeDtypeStruct((N_CHIPS,) + w_in_b.shape, BF)], _gather_sent, _gather_landing, "gather_start_w_in",
        after=c8)
    c_all = c8.reshape(N_DEV * nb, d)
    conv_w_full = convw8[0::2].transpose(1, 0, 2).reshape(4, N_CHIPS * conv_cols)
    b_shard = lax.dynamic_slice(b_ada, (0, chip * ada_cols), (1, ada_cols))
    mod_part, c_act = _ada_mod(c_all, w_ada[0], b_shard, in_token)
    mod_rows = mod_part.reshape(N_DEV, nb, ada_cols)
    m_send, m_recv, m_src, m_land, _ = _ici_start(
        [mod_rows], [jax.ShapeDtypeStruct((N_CHIPS, nb, ada_cols), F32)], _mod_sent, _mod_landing, "mod_start", after=mod_part)

    later = [w_out[0].astype(BF), w_up[0].astype(BF), w_down[0].astype(BF)]
    in_shard, in_land = _ici_wait(i_send, i_recv, i_src, i_land, [m_src[0]] + later, _gather_sent, _gather_landing,
                                  "gather_wait_w_in")
    (w_in4,) = _gather_finish(in_land, in_shard)
    w_in_t = jnp.pad(w_in4.reshape(N_CHIPS * in_cols, d), ((0, IN_PAD - N_CHIPS * in_cols), (0, 0)))
    mod_mine, mod_land = _ici_wait(m_send, m_recv, m_src, m_land, w_in_t, _mod_sent, _mod_landing, "mod_wait")
    mod_own = lax.dynamic_slice(mod_mine[0], (me, 0, 0), (1, nb, ada_cols))
    mod4 = lax.dynamic_update_slice(mod_land[0], mod_own, (chip, 0, 0))
    mod3 = mod4.transpose(1, 0, 2).reshape(nb, N_MOD, d)
    g_send, g_recv, g_src, g_land, first_token = _ici_start(
        later, [jax.ShapeDtypeStruct((N_CHIPS,) + s.shape, BF) for s in later], _gather_sent, _gather_landing, "gather_start",
        after=w_in4)

    def weights_later(after):
        shards, lands = _ici_wait(g_send, g_recv, g_src, g_land, after, _gather_sent, _gather_landing, "gather_wait")
        w_out4, w_up4, w_down4 = _gather_finish(lands, shards)
        return w_out4.reshape(N_CHIPS * w_out.shape[1], d), w_up4, w_down4

    pending = {}

    def start_reduce(name, grad4):
        pending[name] = _reduce_start(grad4, "reduce_start_" + name)
        return pending[name][4]

    pos = jnp.stack([ci, chip, me]).astype(jnp.int32)
    early = ("w_out", "w_up", "w_down")

    def summed_half(name, after):
        r_send, r_recv, r_src, r_land, _ = pending[name]
        own, recv = _reduce_wait(r_send, r_recv, r_src, r_land, after, "reduce_wait_" + name)
        return _sum_eight(recv, own, pos)

    def before_last(token):
        pending["early_halves"] = _exchange_start([summed_half(n, token) for n in early], SIBLING, "halves_start_early")
        return pending["early_halves"][4]

    sp = dict(g_mix=g_mix, conv_b=conv_b, dt_bias=dt_bias, a_log=a_log, d_skip=d_skip, g_ssd=g_ssd,
              w_pool=w_pool[0], pool_scale=pool_scale, g_mlp=g_mlp, g_final=g_final.reshape(1, d))
    gx, d_mod, vec, gw_pool, gconvw = _local_step(
        x.reshape(t, d), loss_target.reshape(t, d), mod3, seq, w_in_t, first_token, weights_later, start_reduce, before_last,
        conv_w_full, sp)

    small_parts = [vec, gw_pool.reshape(4 * POOL_GROUP, POOL_GROUP).astype(BF), gconvw, d_mod]
    s_send, s_recv, s_src, s_land, s_token = _exchange_start(small_parts, ALL_PEERS, "small_start")
    h_send, h_recv, h_src, h_land, h_token = _exchange_start([summed_half("w_in", s_token)], SIBLING, "halves_start")
    e_send, e_recv, e_src, e_land, _ = pending["early_halves"]
    e_own, e_got = _exchange_wait(e_send, e_recv, e_src, e_land, SIBLING, h_token, "halves_wait_early")
    res = {}
    for i, (n, w, m, v) in enumerate((("w_out", w_out, m_w_out, v_w_out), ("w_up", w_up, m_w_up, v_w_up),
                                      ("w_down", w_down, m_w_down, v_w_down))):
        g, dl, m2, v2 = _adam_big(e_own[i], e_got[i], w[0], m[0], v[0], pos)
        res[n] = (g[None], dl[None], m2[None], v2[None])

    s_own, s_got = _exchange_wait(s_send, s_recv, s_src, s_land, ALL_PEERS, res["w_down"][1], "small_wait")
    vec8, wpool8, convw8g, dmod8 = [lax.dynamic_update_slice(got, mine[None], (me,) + (0,) * mine.ndim)
                                    for got, mine in zip(s_got, s_own)]
    convw8s = lax.dynamic_slice(convw8g, (0, 0, chip * conv_cols), (N_DEV, 8, conv_cols))
    m_in = dict(b_ada=m_b_ada, g_mix=m_g_mix, conv_w=m_conv_w[0], conv_b=m_conv_b, dt_bias=m_dt_bias, a_log=m_a_log,
                d_skip=m_d_skip, g_ssd=m_g_ssd, w_pool=m_w_pool.reshape(4 * POOL_GROUP, POOL_GROUP), pool_scale=m_pool_scale,
                g_mlp=m_g_mlp, g_final=m_g_final.reshape(1, d))
    v_in = dict(b_ada=v_b_ada, g_mix=v_g_mix, conv_w=v_conv_w[0], conv_b=v_conv_b, dt_bias=v_dt_bias, a_log=v_a_log,
                d_skip=v_d_skip, g_ssd=v_g_ssd, w_pool=v_w_pool.reshape(4 * POOL_GROUP, POOL_GROUP), pool_scale=v_pool_scale,
                g_mlp=v_g_mlp, g_final=v_g_final.reshape(1, d))
    w_small = dict(sp, b_ada=b_ada, conv_w=conv_w[0], w_pool=w_pool.reshape(4 * POOL_GROUP, POOL_GROUP))
    loss_row, small = _small_adam(vec8, wpool8, convw8s, dmod8, {n: (w_small[n], m_in[n], v_in[n]) for n in _SMALL_PARAMS})

    dmod_all = dmod8.reshape(N_DEV * nb, N_CHIPS * ada_cols)
    dmod_cols = lax.dynamic_slice(dmod_all, (0, chip * ada_cols), (N_DEV * nb, ada_cols))
    res.update({n: tuple(r.reshape(w.shape) for r in small[n])
                for n, w in (("b_ada", b_ada), ("g_mix", g_mix), ("conv_w", conv_w), ("conv_b", conv_b), ("dt_bias", dt_bias),
                             ("a_log", a_log), ("d_skip", d_skip), ("g_ssd", g_ssd), ("w_pool", w_pool),
                             ("pool_scale", pool_scale), ("g_mlp", g_mlp), ("g_final", g_final))})
    g_ada, d_ada, m_ada, v_ada = _adam_ada(c_act.T.astype(BF), dmod_cols, w_ada[0], m_w_ada[0], v_w_ada[0])
    res["w_ada"] = (g_ada[None], d_ada[None], m_ada[None], v_ada[None])
    h_own, h_got = _exchange_wait(h_send, h_recv, h_src, h_land, SIBLING, g_ada, "halves_wait")
    g_in, dl, m2, v2 = _adam_big(h_own[0], h_got[0], w_in_s, m_w_in_s, v_w_in_s, pos)
    res["w_in"] = (g_in.T[None], dl.T[None], m2.T[None], v2.T[None])

    loss = loss_row[0, 0]
    return (loss, gx.reshape(nb, seq, d), *[res[n][0] for n in _WEIGHTS], *[res[n][1] for n in _WEIGHTS],
            *[res[n][2] for n in _WEIGHTS], *[res[n][3] for n in _WEIGHTS])
```

```python
import jax
import jax.numpy as jnp
from jax import lax
from jax.experimental import pallas as pl
from jax.experimental.pallas import tpu as pltpu

F32 = jnp.float32
BF = jnp.bfloat16
MESH = pl.DeviceIdType.MESH

EPS = 1e-5
D_MODEL = 1024
POOL_WIDTH = 512
POOL_WINDOWS = (2, 4, 8, 16)
POOL_GROUP = 128
SSD_INNER = 1024
SSD_HEADS = 16
SSD_HEAD_DIM = 64
SSD_STATE = 128
GROUP_W = 512
CHUNK = 128
CONV_CH = 1536
OFF_DT = 3072
IN_WIDTH = 3088
IN_PAD = 3200
D_FF = 4096
N_MOD = 6
N_CHIPS = 4
N_DEV = 8
HALO = 16
CONV_HALO = 8

ADAM_LR = 0.001
ADAM_B1 = 0.9
ADAM_B2 = 0.999
ADAM_EPS = 1e-08
ADAM_WD = 0.01
ADAM_STEP = 10

VMEM_BYTES_V7X = 64 * 1024 * 1024


def _cp(semantics=None, vmem_mb=48, **kw):
    assert vmem_mb * 1024 * 1024 < VMEM_BYTES_V7X
    args = dict(vmem_limit_bytes=vmem_mb * 1024 * 1024, **kw)
    if semantics is not None:
        args["dimension_semantics"] = semantics
    return pltpu.CompilerParams(**args)


def _out(shape, dtype):
    return pltpu.HBM(shape, dtype)


def _pin(*arrays):
    return [pltpu.with_memory_space_constraint(a, pltpu.HBM) for a in arrays]


def _nn(a, b):
    return jnp.dot(a, b, preferred_element_type=F32)


def _nt(a, b):
    return lax.dot_general(a, b, (((1,), (1,)), ((), ())), preferred_element_type=F32)


def _tn(a, b):
    return lax.dot_general(a, b, (((0,), (0,)), ((), ())), preferred_element_type=F32)


def _split3(v):
    hi = v.astype(BF)
    r1 = v - hi.astype(F32)
    mid = r1.astype(BF)
    lo = (r1 - mid.astype(F32)).astype(BF)
    return hi, mid, lo


def _exact_nn(v, m01):
    hi, mid, lo = _split3(v)
    return _nn(hi, m01) + _nn(mid, m01) + _nn(lo, m01)


def _exact_nn_left(m01, v):
    hi, mid, lo = _split3(v)
    return _nn(m01, hi) + _nn(m01, mid) + _nn(m01, lo)


def _exact_nt_left(m01, v):
    hi, mid, lo = _split3(v)
    return _nt(m01, hi) + _nt(m01, mid) + _nt(m01, lo)


def _sigmoid(v):
    return 1.0 / (1.0 + jnp.exp(-v))


def _iota(shape, dim):
    return lax.broadcasted_iota(jnp.int32, shape, dim)


def _head_expand_matrix(heads, width):
    return (_iota((heads, width), 1) // SSD_HEAD_DIM == _iota((heads, width), 0)).astype(BF)


def _head_reduce_matrix(width, heads):
    return (_iota((width, heads), 0) // SSD_HEAD_DIM == _iota((width, heads), 1)).astype(BF)


def _mesh_pos():
    return lax.axis_index("x"), lax.axis_index("y"), lax.axis_index("c")


def _flip(v, bit):
    return v + bit - 2 * bit * v


def _all_gather_small(arrays):
    n = len(arrays)

    def body(*refs):
        in_refs, out_refs = refs[:n], refs[n:2 * n]
        send_sems, recv_sems, local_sems = refs[2 * n:]
        x, y, c = _mesh_pos()
        me = 4 * x + 2 * y + c
        local = []
        for a in range(n):
            cp = pltpu.make_async_copy(in_refs[a], out_refs[a].at[me], local_sems.at[a])
            cp.start()
            local.append(cp)
        sends = []
        for k in range(1, N_DEV):
            peer = (_flip(x, (k >> 2) & 1), _flip(y, (k >> 1) & 1), _flip(c, k & 1))
            for a in range(n):
                cp = pltpu.make_async_remote_copy(
                    src_ref=in_refs[a], dst_ref=out_refs[a].at[me],
                    send_sem=send_sems.at[a, k], recv_sem=recv_sems.at[a, k],
                    device_id=peer, device_id_type=MESH)
                cp.start()
                sends.append(cp)
        for k in range(1, N_DEV):
            px, py, pc = _flip(x, (k >> 2) & 1), _flip(y, (k >> 1) & 1), _flip(c, k & 1)
            src = 4 * px + 2 * py + pc
            for a in range(n):
                pltpu.make_async_remote_copy(
                    src_ref=in_refs[a], dst_ref=out_refs[a].at[src],
                    send_sem=send_sems.at[a, k], recv_sem=recv_sems.at[a, k],
                    device_id=(px, py, pc), device_id_type=MESH).wait_recv()
        for cp in sends:
            cp.wait_send()
        for cp in local:
            cp.wait()

    vm = pl.BlockSpec(memory_space=pltpu.VMEM)
    return pl.pallas_call(
        body, name="all_gather_small",
        out_shape=[jax.ShapeDtypeStruct((N_DEV,) + a.shape, a.dtype) for a in arrays],
        in_specs=[vm] * n, out_specs=[vm] * n,
        scratch_shapes=[pltpu.SemaphoreType.DMA((n, N_DEV)), pltpu.SemaphoreType.DMA((n, N_DEV)),
                        pltpu.SemaphoreType.DMA((n,))],
        compiler_params=_cp(vmem_mb=32),
    )(*arrays)


_HBM = pl.BlockSpec(memory_space=pltpu.HBM)
_SEM = pl.BlockSpec(memory_space=pltpu.SEMAPHORE)
_DATAFLOW = pltpu.SideEffectType.DATAFLOW_SIDE_EFFECTING


def _peer_chip(x, y, j):
    return _flip(x, (j >> 1) & 1), _flip(y, j & 1)


def _ici_start(srcs, land_shapes, sent, landing, name, after):
    n = len(srcs)

    def body(*refs):
        src_refs, land_refs = refs[:n], refs[n:2 * n]
        send_sems, recv_sems = refs[2 * n + 1], refs[2 * n + 2]
        token = refs[-1]
        x, y, c = _mesh_pos()
        for j in range(1, N_CHIPS):
            px, py = _peer_chip(x, y, j)
            for a in range(n):
                pltpu.make_async_remote_copy(
                    src_ref=sent(src_refs[a], c, 2 * px + py), dst_ref=landing(land_refs[a], c, 2 * x + y),
                    send_sem=send_sems.at[a * (N_CHIPS - 1) + j - 1], recv_sem=recv_sems.at[a * (N_CHIPS - 1) + j - 1],
                    device_id=(px, py, c), device_id_type=MESH).start()
        token[...] = jnp.zeros_like(token)

    sems = pltpu.SemaphoreType.DMA((n * (N_CHIPS - 1),))
    lands = [pltpu.with_memory_space_constraint(lax.empty(s.shape, s.dtype), pltpu.HBM) for s in land_shapes]
    outs = pl.pallas_call(
        body, name=name,
        out_shape=(sems, sems, *[pltpu.HBM(s.shape, s.dtype) for s in srcs],
                   *[pltpu.HBM(s.shape, s.dtype) for s in land_shapes], jax.ShapeDtypeStruct((8, 128), F32)),
        in_specs=[_HBM] * (2 * n + 1), out_specs=[_SEM, _SEM] + [_HBM] * (2 * n) + [pl.BlockSpec(memory_space=pltpu.VMEM)],
        input_output_aliases={i: 2 + i for i in range(2 * n)},
        compiler_params=pltpu.CompilerParams(has_side_effects=_DATAFLOW),
    )(*_pin(*srcs), *lands, *_pin(after))
    return outs[0], outs[1], outs[2:2 + n], outs[2 + n:2 + 2 * n], outs[-1]


def _ici_wait(send_sems, recv_sems, src_thru, land_thru, after, sent, landing, name):
    n = len(src_thru)
    afters = list(after) if isinstance(after, (list, tuple)) else [after]

    def body(*refs):
        src_refs, land_refs = refs[:n], refs[n:2 * n]
        send_sems, recv_sems = refs[2 * n], refs[2 * n + 1]
        x, y, c = _mesh_pos()
        for j in range(1, N_CHIPS):
            px, py = _peer_chip(x, y, j)
            for a in range(n):
                cp = pltpu.make_async_remote_copy(
                    src_ref=sent(src_refs[a], c, 2 * px + py), dst_ref=landing(land_refs[a], c, 2 * px + py),
                    send_sem=send_sems.at[a * (N_CHIPS - 1) + j - 1], recv_sem=recv_sems.at[a * (N_CHIPS - 1) + j - 1],
                    device_id=(px, py, c), device_id_type=MESH)
                cp.wait_send()
                cp.wait_recv()

    outs = pl.pallas_call(
        body, name=name,
        out_shape=tuple(pltpu.HBM(s.shape, s.dtype) for s in (*src_thru, *land_thru)),
        in_specs=[_HBM] * (2 * n) + [_SEM, _SEM] + [_HBM] * len(afters), out_specs=[_HBM] * (2 * n),
        input_output_aliases={i: i for i in range(2 * n)},
        compiler_params=pltpu.CompilerParams(has_side_effects=_DATAFLOW),
    )(*src_thru, *land_thru, send_sems, recv_sems, *_pin(*afters))
    return outs[:n], outs[n:]


def _col_half(ref, which, lead=()):
    hc = ref.shape[-1] // 2
    return ref.at[(*lead, slice(None), pl.ds(pl.multiple_of(which * hc, 128), hc))]


def _gather_sent(ref, c, dst_chip):
    return _col_half(ref, c)


def _gather_landing(ref, c, src_chip):
    return _col_half(ref, c, lead=(src_chip,))


def _mod_sent(ref, c, dst_chip):
    return ref.at[2 * dst_chip + c]


def _mod_landing(ref, c, src_chip):
    return ref.at[src_chip]


def _reduce_copy(src_ref, land_ref, send_sems, recv_sems, k, receiving):
    x, y, c = _mesh_pos()
    px, py, pc = _flip(x, (k >> 2) & 1), _flip(y, (k >> 1) & 1), _flip(c, k & 1)
    hc = src_ref.shape[2] // 2
    src = src_ref.at[2 * px + py, :, pl.ds(pl.multiple_of(pc * hc, 128), hc)]
    slot = (4 * px + 2 * py + pc) if receiving else (4 * x + 2 * y + c)
    return pltpu.make_async_remote_copy(
        src_ref=src, dst_ref=land_ref.at[slot], send_sem=send_sems.at[k - 1], recv_sem=recv_sems.at[k - 1],
        device_id=(px, py, pc), device_id_type=MESH)


def _reduce_start(grad4, name):
    k4, r, cols = grad4.shape

    def body(src_ref, land_ref, send_sems, recv_sems, src_thru, land_thru, token):
        for k in range(1, N_DEV):
            _reduce_copy(src_ref, land_ref, send_sems, recv_sems, k, receiving=False).start()
        token[...] = jnp.zeros_like(token)

    sems = pltpu.SemaphoreType.DMA((N_DEV - 1,))
    land = pltpu.with_memory_space_constraint(lax.empty((N_DEV, r, cols // 2), grad4.dtype), pltpu.HBM)
    return pl.pallas_call(
        body, name=name,
        out_shape=(sems, sems, pltpu.HBM(grad4.shape, grad4.dtype), pltpu.HBM(land.shape, land.dtype),
                   jax.ShapeDtypeStruct((8, 128), F32)),
        in_specs=[_HBM, _HBM], out_specs=[_SEM, _SEM, _HBM, _HBM, pl.BlockSpec(memory_space=pltpu.VMEM)],
        input_output_aliases={0: 2, 1: 3},
        compiler_params=pltpu.CompilerParams(has_side_effects=_DATAFLOW),
    )(*_pin(grad4), land)


def _reduce_wait(send_sems, recv_sems, src_thru, land_thru, after, name):
    def body(src_ref, land_ref, send_sems, recv_sems, after_ref, src_out, land_out):
        for k in range(1, N_DEV):
            cp = _reduce_copy(src_ref, land_ref, send_sems, recv_sems, k, receiving=True)
            cp.wait_send()
            cp.wait_recv()

    return pl.pallas_call(
        body, name=name,
        out_shape=(pltpu.HBM(src_thru.shape, src_thru.dtype), pltpu.HBM(land_thru.shape, land_thru.dtype)),
        in_specs=[_HBM, _HBM, _SEM, _SEM, _HBM], out_specs=[_HBM, _HBM],
        input_output_aliases={0: 0, 1: 1},
        compiler_params=pltpu.CompilerParams(has_side_effects=_DATAFLOW),
    )(src_thru, land_thru, send_sems, recv_sems, *_pin(after))


def _peer_copy(src_ref, land_ref, send_sems, recv_sems, idx, k, receiving):
    x, y, c = _mesh_pos()
    px, py, pc = _flip(x, (k >> 2) & 1), _flip(y, (k >> 1) & 1), _flip(c, k & 1)
    if land_ref.shape[0] == N_DEV:
        slot = (4 * px + 2 * py + pc) if receiving else (4 * x + 2 * y + c)
    else:
        slot = pc if receiving else c
    return pltpu.make_async_remote_copy(
        src_ref=src_ref, dst_ref=land_ref.at[slot], send_sem=send_sems.at[idx], recv_sem=recv_sems.at[idx],
        device_id=(px, py, pc), device_id_type=MESH)


def _exchange_start(arrays, peers, name):
    n = len(arrays)

    def body(*refs):
        src_refs, land_refs = refs[:n], refs[n:2 * n]
        send_sems, recv_sems = refs[2 * n], refs[2 * n + 1]
        token = refs[-1]
        for j, k in enumerate(peers):
            for a in range(n):
                _peer_copy(src_refs[a], land_refs[a], send_sems, recv_sems, a * len(peers) + j, k, receiving=False).start()
        token[...] = jnp.zeros_like(token)

    sems = pltpu.SemaphoreType.DMA((n * len(peers),))
    n_slots = N_DEV if len(peers) > 1 else 2
    lands = [pltpu.with_memory_space_constraint(lax.empty((n_slots,) + a.shape, a.dtype), pltpu.HBM) for a in arrays]
    outs = pl.pallas_call(
        body, name=name,
        out_shape=(sems, sems, *[pltpu.HBM(a.shape, a.dtype) for a in arrays], *[pltpu.HBM(l.shape, l.dtype) for l in lands],
                   jax.ShapeDtypeStruct((8, 128), F32)),
        in_specs=[_HBM] * (2 * n), out_specs=[_SEM, _SEM] + [_HBM] * (2 * n) + [pl.BlockSpec(memory_space=pltpu.VMEM)],
        input_output_aliases={i: 2 + i for i in range(2 * n)},
        compiler_params=pltpu.CompilerParams(has_side_effects=_DATAFLOW),
    )(*_pin(*arrays), *lands)
    return outs[0], outs[1], outs[2:2 + n], outs[2 + n:2 + 2 * n], outs[-1]


def _exchange_wait(send_sems, recv_sems, src_thru, land_thru, peers, after, name):
    n = len(src_thru)

    def body(*refs):
        src_refs, land_refs = refs[:n], refs[n:2 * n]
        send_sems, recv_sems = refs[2 * n], refs[2 * n + 1]
        for j, k in enumerate(peers):
            for a in range(n):
                cp = _peer_copy(src_refs[a], land_refs[a], send_sems, recv_sems, a * len(peers) + j, k, receiving=True)
                cp.wait_send()
                cp.wait_recv()

    outs = pl.pallas_call(
        body, name=name,
        out_shape=tuple(pltpu.HBM(s.shape, s.dtype) for s in (*src_thru, *land_thru)),
        in_specs=[_HBM] * (2 * n) + [_SEM, _SEM, _HBM], out_specs=[_HBM] * (2 * n),
        input_output_aliases={i: i for i in range(2 * n)},
        compiler_params=pltpu.CompilerParams(has_side_effects=_DATAFLOW),
    )(*src_thru, *land_thru, send_sems, recv_sems, *_pin(after))
    return outs[:n], outs[n:]


ALL_PEERS = tuple(range(1, N_DEV))
SIBLING = (1,)


def _sum_eight(recv, grad4, pos):
    n, r, hc = recv.shape
    steps = 2
    tc = hc // steps

    def body(pos_ref, r_ref, g_ref, o_ref):
        me = pos_ref[2]
        o_ref[...] = jnp.zeros_like(o_ref)
        for s in range(n):
            @pl.when(me == s)
            def _():
                o_ref[...] += g_ref[0].astype(F32)

            @pl.when(me != s)
            def _():
                o_ref[...] += r_ref[s].astype(F32)

    grid_spec = pltpu.PrefetchScalarGridSpec(
        num_scalar_prefetch=1, grid=(steps,),
        in_specs=[pl.BlockSpec((n, r, tc), lambda i, pos: (0, 0, i)),
                  pl.BlockSpec((1, r, tc), lambda i, pos: (pos[1], 0, pos[0] * steps + i))],
        out_specs=pl.BlockSpec((r, tc), lambda i, pos: (0, i)))
    return pl.pallas_call(body, name="sum_eight", grid_spec=grid_spec, out_shape=_out((r, hc), F32),
                          compiler_params=_cp(("parallel",), 32))(pos, *_pin(recv, grad4))


def _forward_copy(land_ref, send_sems, recv_sems, idx, j, receiving):
    x, y, c = _mesh_pos()
    px, py = _peer_chip(x, y, j)
    mine = _col_half(land_ref, c, lead=(2 * px + py,))
    theirs = _col_half(land_ref, 1 - c, lead=(2 * px + py,))
    return pltpu.make_async_remote_copy(
        src_ref=mine, dst_ref=theirs if receiving else mine, send_sem=send_sems.at[idx], recv_sem=recv_sems.at[idx],
        device_id=(x, y, 1 - c), device_id_type=MESH)


def _forward_start(lands, name):
    n = len(lands)

    def body(*refs):
        land_refs, send_sems, recv_sems, token = refs[:n], refs[n], refs[n + 1], refs[-1]
        for j in range(1, N_CHIPS):
            for a in range(n):
                _forward_copy(land_refs[a], send_sems, recv_sems, a * (N_CHIPS - 1) + j - 1, j, receiving=False).start()
        token[...] = jnp.zeros_like(token)

    sems = pltpu.SemaphoreType.DMA((n * (N_CHIPS - 1),))
    outs = pl.pallas_call(
        body, name=name,
        out_shape=(sems, sems, *[pltpu.HBM(l.shape, l.dtype) for l in lands], jax.ShapeDtypeStruct((8, 128), F32)),
        in_specs=[_HBM] * n, out_specs=[_SEM, _SEM] + [_HBM] * n + [pl.BlockSpec(memory_space=pltpu.VMEM)],
        input_output_aliases={i: 2 + i for i in range(n)},
        compiler_params=pltpu.CompilerParams(has_side_effects=_DATAFLOW),
    )(*lands)
    return outs[0], outs[1], outs[2:2 + n], outs[-1]


def _forward_wait(send_sems, recv_sems, lands_thru, after, name):
    n = len(lands_thru)

    def body(*refs):
        land_refs, send_sems, recv_sems = refs[:n], refs[n], refs[n + 1]
        for j in range(1, N_CHIPS):
            for a in range(n):
                cp = _forward_copy(land_refs[a], send_sems, recv_sems, a * (N_CHIPS - 1) + j - 1, j, receiving=True)
                cp.wait_send()
                cp.wait_recv()

    return pl.pallas_call(
        body, name=name,
        out_shape=tuple(pltpu.HBM(l.shape, l.dtype) for l in lands_thru),
        in_specs=[_HBM] * n + [_SEM, _SEM, _HBM], out_specs=[_HBM] * n,
        input_output_aliases={i: i for i in range(n)},
        compiler_params=pltpu.CompilerParams(has_side_effects=_DATAFLOW),
    )(*lands_thru, send_sems, recv_sems, *_pin(after))


def _gather_finish(lands, shards):
    n = len(lands)
    any_spec = _HBM

    def body(*refs):
        shard_refs, out_refs = refs[n:2 * n], refs[2 * n:3 * n]
        send_sems, recv_sems, local_sems = refs[3 * n:]
        x, y, c = _mesh_pos()
        chip = 2 * x + y
        local, sends = [], []
        for a in range(n):
            cp = pltpu.make_async_copy(shard_refs[a], out_refs[a].at[chip], local_sems.at[a])
            cp.start()
            local.append(cp)
        for j in range(1, N_CHIPS):
            px, py = _peer_chip(x, y, j)
            for a in range(n):
                landed = _col_half(out_refs[a], c, lead=(2 * px + py,))
                cp = pltpu.make_async_remote_copy(
                    src_ref=landed, dst_ref=landed, send_sem=send_sems.at[a, j], recv_sem=recv_sems.at[a, j],
                    device_id=(x, y, 1 - c), device_id_type=MESH)
                cp.start()
                sends.append(cp)
        for j in range(1, N_CHIPS):
            px, py = _peer_chip(x, y, j)
            for a in range(n):
                other = _col_half(out_refs[a], 1 - c, lead=(2 * px + py,))
                pltpu.make_async_remote_copy(
                    src_ref=other, dst_ref=other, send_sem=send_sems.at[a, j], recv_sem=recv_sems.at[a, j],
                    device_id=(x, y, 1 - c), device_id_type=MESH).wait_recv()
        for cp in sends:
            cp.wait_send()
        for cp in local:
            cp.wait()

    return pl.pallas_call(
        body, name="gather_finish",
        out_shape=[_out(l.shape, l.dtype) for l in lands],
        in_specs=[any_spec] * (2 * n), out_specs=[any_spec] * n,
        input_output_aliases={i: i for i in range(n)},
        scratch_shapes=[pltpu.SemaphoreType.DMA((n, N_CHIPS))] * 2 + [pltpu.SemaphoreType.DMA((n,))],
        compiler_params=_cp(vmem_mb=16),
    )(*lands, *shards)


def _adam_math(w, g, m, v):
    m2 = ADAM_B1 * m + (1.0 - ADAM_B1) * g
    v2 = ADAM_B2 * v + (1.0 - ADAM_B2) * (g * g)
    m_hat = m2 / (1.0 - ADAM_B1 ** ADAM_STEP)
    v_hat = v2 / (1.0 - ADAM_B2 ** ADAM_STEP)
    delta = -ADAM_LR * (m_hat / (jnp.sqrt(v_hat) + ADAM_EPS) + ADAM_WD * w)
    return delta, m2, v2


def _adam_big(g_own, g_pair, w, m, v, pos):
    r, c = w.shape
    per_half = 2
    tc = c // (2 * per_half)

    def body(pos_ref, go_ref, gp_ref, w_ref, m_ref, v_ref, g_ref, d_ref, m2_ref, v2_ref):
        half = pl.program_id(0) // per_half
        g = jnp.where(half == pos_ref[0], go_ref[...], gp_ref[0])
        d, m2, v2 = _adam_math(w_ref[...], g, m_ref[...], v_ref[...])
        g_ref[...] = g
        d_ref[...] = d
        m2_ref[...] = m2
        v2_ref[...] = v2

    spec = pl.BlockSpec((r, tc), lambda i, pos: (0, i))
    grid_spec = pltpu.PrefetchScalarGridSpec(
        num_scalar_prefetch=1, grid=(2 * per_half,),
        in_specs=[pl.BlockSpec((r, tc), lambda i, pos: (0, i % per_half)),
                  pl.BlockSpec((1, r, tc), lambda i, pos: (1 - pos[0], 0, i % per_half)), spec, spec, spec],
        out_specs=[spec] * 4)
    sh = _out((r, c), F32)
    return pl.pallas_call(body, name="adam_big", grid_spec=grid_spec, out_shape=[sh] * 4,
                          compiler_params=_cp(("parallel",), 32))(pos, *_pin(g_own, g_pair, w, m, v))


def _adam_ada(c_act_t, dmod_cols, w, m, v):
    r, c = w.shape
    tc = 512

    def body(ct_ref, dm_ref, w_ref, m_ref, v_ref, g_ref, d_ref, m2_ref, v2_ref):
        g = _nn(ct_ref[...], dm_ref[...].astype(BF))
        d, m2, v2 = _adam_math(w_ref[...], g, m_ref[...], v_ref[...])
        g_ref[...] = g
        d_ref[...] = d
        m2_ref[...] = m2
        v2_ref[...] = v2

    spec = pl.BlockSpec((r, tc), lambda i: (0, i))
    sh = _out((r, c), F32)
    return pl.pallas_call(
        body, name="adam_ada", grid=(c // tc,),
        in_specs=[pl.BlockSpec(c_act_t.shape, lambda i: (0, 0)), pl.BlockSpec((dmod_cols.shape[0], tc), lambda i: (0, i)),
                  spec, spec, spec],
        out_specs=[spec] * 4, out_shape=[sh] * 4, compiler_params=_cp(("parallel",), 48))(*_pin(c_act_t, dmod_cols, w, m, v))


def _ada_mod(c_all, w_shard, b_shard, token):
    nb, d = c_all.shape
    cols = w_shard.shape[1]
    tc = 512

    def body(c_ref, w_ref, b_ref, tok_ref, mod_ref, act_ref):
        cv = c_ref[...]
        act = cv * _sigmoid(cv)
        act_ref[...] = act
        mod_ref[...] = _nn(act.astype(BF), w_ref[...].astype(BF)) + b_ref[...]

    return pl.pallas_call(
        body, name="ada_mod", grid=(cols // tc,),
        in_specs=[pl.BlockSpec((nb, d), lambda i: (0, 0)), pl.BlockSpec((d, tc), lambda i: (0, i)),
                  pl.BlockSpec((1, tc), lambda i: (0, i)), _token_spec()],
        out_specs=[pl.BlockSpec((nb, tc), lambda i: (0, i)), pl.BlockSpec((nb, d), lambda i: (0, 0))],
        out_shape=[_out((nb, cols), F32), _out((nb, d), F32)],
        compiler_params=_cp(("arbitrary",), 32))(*_pin(c_all, w_shard, b_shard, token))


SUB_ROWS = 256
ROW_TILE = 512


def _sub_rows(tm):
    return [slice(s, s + SUB_ROWS) for s in range(0, tm, SUB_ROWS)] if tm > SUB_ROWS else [slice(0, tm)]


_RESIDENT = pl.BlockSpec(memory_space=pltpu.VMEM)


def _token_spec():
    return pl.BlockSpec((8, 128), lambda *_: (0, 0))


def _in_proj(x, mod3, g_mix, w_in_t, seq, token):
    t, d = x.shape
    tm = min(ROW_TILE, seq)
    tps = seq // tm

    def body(x_ref, mod_ref, g_ref, w_ref, tok_ref, proj_ref, u1_ref):
        for rows in _sub_rows(tm):
            xv = x_ref[rows, :]
            r = lax.rsqrt(jnp.mean(xv * xv, -1, keepdims=True) + EPS)
            u = (xv * r * g_ref[...]) * (1.0 + mod_ref[0, 1:2, :]) + mod_ref[0, 0:1, :]
            ub = u.astype(BF)
            u1_ref[rows, :] = ub
            proj_ref[rows, :] = _nt(ub, w_ref[...])

    return pl.pallas_call(
        body, name="in_proj", grid=(t // tm,),
        in_specs=[pl.BlockSpec((tm, d), lambda i: (i, 0)), pl.BlockSpec((1, N_MOD, d), lambda i: (i // tps, 0, 0)),
                  pl.BlockSpec((1, d), lambda i: (0, 0)), _RESIDENT, _token_spec()],
        out_specs=[pl.BlockSpec((tm, IN_PAD), lambda i: (i, 0)), pl.BlockSpec((tm, d), lambda i: (i, 0))],
        out_shape=[_out((t, IN_PAD), F32), _out((t, d), BF)],
        compiler_params=_cp(("parallel",), 40))(*_pin(x, mod3, g_mix), w_in_t, *_pin(token))


def _pool_tile(seq):
    return min(512, seq)


def _pool_fwd(proj, w_pool, pool_scale, nb, seq, token):
    ts = _pool_tile(seq)
    nt = seq // ts

    def body(u_ref, halo_ref, wp_ref, ps_ref, tok_ref, yp_ref, p_ref):
        i = pl.program_id(1)
        halo = jnp.where(i == 0, 0.0, halo_ref[...])
        u = u_ref[...]
        ext = jnp.concatenate([halo, u], 0)
        tpos = i * ts + _iota((ts, 1), 0)
        for g, w in enumerate(POOL_WINDOWS):
            gs = slice(g * POOL_GROUP, (g + 1) * POOL_GROUP)
            s = ext[:, gs]
            sh = 1
            while sh < w:
                s = s + pltpu.roll(s, sh, 0)
                sh *= 2
            cnt = jnp.minimum(tpos + 1, w).astype(F32)
            pb = (s[HALO:] / cnt - u[:, gs]).astype(BF)
            p_ref[:, gs] = pb
            yp_ref[:, gs] = (_nn(pb, wp_ref[g].astype(BF)) * ps_ref[:, gs]).astype(BF)

    hb = ts // HALO
    return pl.pallas_call(
        body, name="pool_fwd", grid=(nb, nt),
        in_specs=[pl.BlockSpec((ts, POOL_WIDTH), lambda b, i: (b * nt + i, 0)),
                  pl.BlockSpec((HALO, POOL_WIDTH), lambda b, i: (jnp.maximum((b * nt + i) * hb - 1, 0), 0)),
                  pl.BlockSpec((4, POOL_GROUP, POOL_GROUP), lambda b, i: (0, 0, 0)),
                  pl.BlockSpec((1, POOL_WIDTH), lambda b, i: (0, 0)), _token_spec()],
        out_specs=[pl.BlockSpec((ts, POOL_WIDTH), lambda b, i: (b * nt + i, 0))] * 2,
        out_shape=[_out((nb * seq, POOL_WIDTH), BF)] * 2,
        compiler_params=_cp(("parallel", "parallel"), 32))(*_pin(proj, proj, w_pool, pool_scale, token))


def _conv_pre(uxbc, halo, cw, cb, first):
    halo = jnp.where(first, 0.0, halo)
    ext = jnp.concatenate([halo, uxbc], 0)
    pre = cb + uxbc * cw[3:4]
    for k in (2, 1, 0):
        pre = pre + pltpu.roll(ext, 3 - k, 0)[CONV_HALO:] * cw[k:k + 1]
    return pre


def _chunk_terms(pre, udt, dtb, alog):
    sg = _sigmoid(pre)
    xbc = pre * sg
    dtp = udt[:, :SSD_HEADS] + dtb
    dt = jnp.maximum(dtp, 0.0) + jnp.log(1.0 + jnp.exp(-jnp.abs(dtp)))
    a = -jnp.exp(alog)
    da = dt * a
    tril = (_iota((CHUNK, CHUNK), 0) >= _iota((CHUNK, CHUNK), 1))
    acum = _exact_nn_left(tril.astype(BF), da)
    eye = (_iota((SSD_HEADS, SSD_HEADS), 0) == _iota((SSD_HEADS, SSD_HEADS), 1)).astype(BF)
    acum_t = _exact_nt_left(eye, acum)
    expand = _head_expand_matrix(SSD_HEADS, SSD_INNER)
    acum_e = _exact_nn(acum, expand)
    dt_e = _exact_nn(dt, expand)
    last_e = acum_e[CHUNK - 1:CHUNK]
    return dict(pre=pre, sg=sg, xbc=xbc, dtp=dtp, dt=dt, a=a, acum=acum, acum_t=acum_t, tril=tril,
                dt_e=dt_e, e_a=jnp.exp(acum_e), d_out=jnp.exp(last_e - acum_e), c_dec=jnp.exp(last_e))


def _head_decay(r, h):
    seg = r["acum"][:, h:h + 1] - r["acum_t"][h:h + 1, :]
    return jnp.where(r["tril"], jnp.exp(jnp.minimum(seg, 0.0)), 0.0)


SSD_SUB = 4
SSD_ROWS = SSD_SUB * CHUNK


def _ssd_specs(nb, seq, reverse):
    ns = seq // SSD_ROWS
    per = seq // CONV_HALO

    def cidx(c):
        return (ns - 1 - c) if reverse else c

    def row(b, c):
        return b * ns + cidx(c)

    specs = [
        pl.BlockSpec((SSD_ROWS, CONV_CH), lambda b, c: (row(b, c), 1)),
        pl.BlockSpec((CONV_HALO, CONV_CH),
                     lambda b, c: (jnp.maximum(b * per + cidx(c) * (SSD_ROWS // CONV_HALO) - 1, 0), 1)),
        pl.BlockSpec((SSD_ROWS, GROUP_W), lambda b, c: (row(b, c), 1)),
        pl.BlockSpec((SSD_ROWS, GROUP_W), lambda b, c: (row(b, c), 2)),
        pl.BlockSpec((SSD_ROWS, 128), lambda b, c: (row(b, c), OFF_DT // 128)),
    ]
    return specs, row, cidx, ns


def _const_spec(shape):
    return pl.BlockSpec(shape, lambda b, c: (0,) * len(shape))


def _ssd_fwd(proj, conv_w, conv_b, dt_bias, a_log, dskip_e, g_ssd, nb, seq):
    specs, row, cidx, ns = _ssd_specs(nb, seq, reverse=False)

    def body(uxbc_ref, halo_ref, z0_ref, z1_ref, udt_ref, cw_ref, cb_ref, dtb_ref, alog_ref, dsk_ref, gs_ref,
             yssd_ref, yssm_ref, hprev_ref, pre_ref, h_ref, yd_ref):
        c = pl.program_id(1)

        @pl.when(c == 0)
        def _():
            h_ref[...] = jnp.zeros_like(h_ref)

        for sub in range(SSD_SUB):
            rows = slice(sub * CHUNK, (sub + 1) * CHUNK)
            if sub == 0:
                halo, first = halo_ref[...], c == 0
            else:
                halo, first = uxbc_ref[sub * CHUNK - CONV_HALO:sub * CHUNK, :], False
            pre = _conv_pre(uxbc_ref[rows, :], halo, cw_ref[...], cb_ref[...], first)
            pre_ref[rows, :] = pre
            r = _chunk_terms(pre, udt_ref[rows, :], dtb_ref[...], alog_ref[...])
            xbc = r["xbc"]
            xs = xbc[:, :SSD_INNER]
            xdt = xs * r["dt_e"]
            xdt_b = xdt.astype(BF)
            xdo_b = (xdt * r["d_out"]).astype(BF)
            hprev_ref[0, sub] = h_ref[...]
            for g in range(2):
                gs = slice(g * GROUP_W, (g + 1) * GROUP_W)
                bg = xbc[:, SSD_INNER + g * SSD_STATE:SSD_INNER + (g + 1) * SSD_STATE].astype(BF)
                cg = xbc[:, SSD_INNER + (2 + g) * SSD_STATE:SSD_INNER + (3 + g) * SSD_STATE].astype(BF)
                scores = _nt(cg, bg)
                hg = h_ref[g]
                y_off = _nn(cg, hg.astype(BF)) * r["e_a"][:, gs]
                for hh in range(8):
                    h = g * 8 + hh
                    hs = slice(h * SSD_HEAD_DIM, (h + 1) * SSD_HEAD_DIM)
                    m = (scores * _head_decay(r, h)).astype(BF)
                    yd_ref[sub, :, hs] = _nn(m, xdt_b[:, hs])
                h_ref[g] = hg * r["c_dec"][:, gs] + _tn(bg, xdo_b[:, gs])
                y = yd_ref[sub, :, gs] + y_off + dsk_ref[:, gs] * xs[:, gs]
                yssm_ref[rows, gs] = y
                zg = (z0_ref if g == 0 else z1_ref)[rows, :]
                yg = y * (zg * _sigmoid(zg))
                rg = lax.rsqrt(jnp.mean(yg * yg, -1, keepdims=True) + EPS)
                yssd_ref[rows, gs] = (yg * rg * gs_ref[:, gs]).astype(BF)

    t = nb * seq
    return pl.pallas_call(
        body, name="ssd_fwd", grid=(nb, ns),
        in_specs=specs + [_const_spec((4, CONV_CH)), _const_spec((1, CONV_CH)), _const_spec((1, SSD_HEADS)),
                          _const_spec((1, SSD_HEADS)), _const_spec((1, SSD_INNER)), _const_spec((1, SSD_INNER))],
        out_specs=[pl.BlockSpec((SSD_ROWS, SSD_INNER), lambda b, c: (row(b, c), 0)),
                   pl.BlockSpec((SSD_ROWS, SSD_INNER), lambda b, c: (row(b, c), 0)),
                   pl.BlockSpec((1, SSD_SUB, 2, SSD_STATE, GROUP_W), lambda b, c: (b, c, 0, 0, 0)),
                   pl.BlockSpec((SSD_ROWS, CONV_CH), lambda b, c: (row(b, c), 0))],
        out_shape=[_out((t, SSD_INNER), BF), _out((t, SSD_INNER), F32),
                   _out((nb, seq // CHUNK, 2, SSD_STATE, GROUP_W), F32), _out((t, CONV_CH), F32)],
        scratch_shapes=[pltpu.VMEM((2, SSD_STATE, GROUP_W), F32), pltpu.VMEM((SSD_SUB, CHUNK, SSD_INNER), F32)],
        compiler_params=_cp(("arbitrary", "arbitrary"), 56),
    )(*_pin(proj, proj, proj, proj, proj, conv_w, conv_b, dt_bias, a_log, dskip_e, g_ssd))


def _out_proj(y_pool, y_ssd, w_out, x, mod3, g_mlp, seq):
    t, d = x.shape
    tm = 512
    tps = seq // tm if seq >= tm else 1
    tm = min(tm, seq)

    def body(yp_ref, ys_ref, w_ref, x_ref, mod_ref, g_ref, h1_ref, o_ref, u2_ref):
        for rows in _sub_rows(tm):
            o = _nn(jnp.concatenate([yp_ref[rows, :], ys_ref[rows, :]], 1), w_ref[...])
            o_ref[rows, :] = o.astype(BF)
            h1 = x_ref[rows, :] + mod_ref[0, 2:3, :] * o
            h1_ref[rows, :] = h1
            r = lax.rsqrt(jnp.mean(h1 * h1, -1, keepdims=True) + EPS)
            u2_ref[rows, :] = ((h1 * r * g_ref[...]) * (1.0 + mod_ref[0, 4:5, :]) + mod_ref[0, 3:4, :]).astype(BF)

    row = lambda i: (i, 0)
    return pl.pallas_call(
        body, name="out_proj", grid=(t // tm,),
        in_specs=[pl.BlockSpec((tm, POOL_WIDTH), row), pl.BlockSpec((tm, SSD_INNER), row),
                  _RESIDENT, pl.BlockSpec((tm, d), row),
                  pl.BlockSpec((1, N_MOD, d), lambda i: (i // tps, 0, 0)), pl.BlockSpec((1, d), lambda i: (0, 0))],
        out_specs=[pl.BlockSpec((tm, d), row)] * 3,
        out_shape=[_out((t, d), F32), _out((t, d), BF), _out((t, d), BF)],
        compiler_params=_cp(("parallel",), 48))(*_pin(y_pool, y_ssd), w_out, *_pin(x, mod3, g_mlp))


def _mlp_up(u2, w_up4):
    t, d = u2.shape
    tm = min(1024, t)
    nk, _, cols = w_up4.shape

    def body(u_ref, w_ref, a_ref):
        a_ref[...] = _nn(u_ref[...], w_ref[pl.program_id(1)]).astype(BF)

    return pl.pallas_call(
        body, name="mlp_up", grid=(t // tm, nk),
        in_specs=[pl.BlockSpec((tm, d), lambda i, k: (i, 0)), _RESIDENT],
        out_specs=pl.BlockSpec((tm, cols), lambda i, k: (i, k)),
        out_shape=_out((t, nk * cols), BF),
        compiler_params=_cp(("parallel", "parallel"), 32))(*_pin(u2), w_up4)


def _mlp_down_loss(a_up, w_down, h1, mod3, g_final, target, seq):
    t, d = h1.shape
    nb = t // seq
    tm = min(ROW_TILE, seq)
    tps = seq // tm

    def body(a_ref, w_ref, h1_ref, mod_ref, g_ref, tg_ref, ddn_ref, dh2_ref, sq_ref, gg_ref, dgf_ref):
        i = pl.program_id(0)

        @pl.when(i == 0)
        def _():
            sq_ref[...] = jnp.zeros_like(sq_ref)
            gg_ref[...] = jnp.zeros_like(gg_ref)

        @pl.when(i % tps == 0)
        def _():
            dgf_ref[...] = jnp.zeros_like(dgf_ref)

        gate = mod_ref[0, 5:6, :]
        sq = gg = dgf = 0.0
        for rows in _sub_rows(tm):
            f = jnp.square(jnp.maximum(a_ref[rows, :], 0))
            dn = _nn(f, w_ref[...])
            h2 = h1_ref[rows, :] + gate * dn
            r = lax.rsqrt(jnp.mean(h2 * h2, -1, keepdims=True) + EPS)
            hh = h2 * r
            err = hh * g_ref[...] - tg_ref[rows, :]
            dy = err * (1.0 / d)
            dhat = dy * g_ref[...]
            dh2 = r * (dhat - hh * jnp.mean(dhat * hh, -1, keepdims=True))
            dh2_ref[rows, :] = dh2
            ddn_ref[rows, :] = (dh2 * gate).astype(BF)
            sq = sq + jnp.sum(err * err, 0, keepdims=True)
            gg = gg + jnp.sum(dy * hh, 0, keepdims=True)
            dgf = dgf + jnp.sum(dh2 * dn, 0, keepdims=True)
        sq_ref[...] += sq
        gg_ref[...] += gg
        dgf_ref[0] += dgf

    row = lambda i: (i, 0)
    vec = pl.BlockSpec((1, d), lambda i: (0, 0))
    return pl.pallas_call(
        body, name="mlp_down_loss", grid=(t // tm,),
        in_specs=[pl.BlockSpec((tm, D_FF), row), _RESIDENT, pl.BlockSpec((tm, d), row),
                  pl.BlockSpec((1, N_MOD, d), lambda i: (i // tps, 0, 0)), vec, pl.BlockSpec((tm, d), row)],
        out_specs=[pl.BlockSpec((tm, d), row), pl.BlockSpec((tm, d), row), vec, vec,
                   pl.BlockSpec((1, 1, d), lambda i: (i // tps, 0, 0))],
        out_shape=[_out((t, d), BF), _out((t, d), F32), _out((1, d), F32),
                   _out((1, d), F32), _out((nb, 1, d), F32)],
        compiler_params=_cp(("arbitrary",), 44))(*_pin(a_up), w_down, *_pin(h1, mod3, g_final, target))


def _tn_matmul(a, b, tk, tn, name, square_relu=False, out3=False):
    t, kdim = a.shape
    ndim = b.shape[1]

    def body(a_ref, b_ref, o_ref):
        av = a_ref[...]
        if square_relu:
            av = jnp.square(jnp.maximum(av, 0))
        res = _tn(av, b_ref[...]).astype(BF)
        if out3:
            o_ref[0] = res
        else:
            o_ref[...] = res

    if out3:
        out_spec = pl.BlockSpec((1, tk, tn), lambda j, i: (j, i, 0))
        out_shape = _out((ndim // tn, kdim, tn), BF)
    else:
        out_spec = pl.BlockSpec((tk, tn), lambda j, i: (i, j))
        out_shape = _out((kdim, ndim), BF)
    return pl.pallas_call(
        body, name=name, grid=(ndim // tn, kdim // tk),
        in_specs=[pl.BlockSpec((t, tk), lambda j, i: (0, i)), pl.BlockSpec((t, tn), lambda j, i: (0, j))],
        out_specs=out_spec, out_shape=out_shape,
        compiler_params=_cp(("parallel", "parallel"), 56))(*_pin(a, b))


def _mlp_down_bwd(d_dn, w_down4, a_up, token):
    t, d = d_dn.shape
    tm = min(1024, t)
    nk, rows, _ = w_down4.shape

    def body(g_ref, w_ref, a_ref, tok_ref, o_ref):
        df = _nt(g_ref[...], w_ref[pl.program_id(1)])
        o_ref[...] = (df * (2.0 * jnp.maximum(a_ref[...], 0).astype(F32))).astype(BF)

    return pl.pallas_call(
        body, name="mlp_down_bwd", grid=(t // tm, nk),
        in_specs=[pl.BlockSpec((tm, d), lambda i, k: (i, 0)), _RESIDENT,
                  pl.BlockSpec((tm, rows), lambda i, k: (i, k)), _token_spec()],
        out_specs=pl.BlockSpec((tm, rows), lambda i, k: (i, k)),
        out_shape=_out((t, nk * rows), BF),
        compiler_params=_cp(("parallel", "parallel"), 32))(*_pin(d_dn), w_down4, *_pin(a_up, token))


def _mlp_up_bwd(d_a, w_up4, h1, dh2, o, mod3, g_mlp, seq, token):
    t, d = h1.shape
    nb = t // seq
    tm = min(ROW_TILE, seq)
    tps = seq // tm
    nk = w_up4.shape[0]
    cols = w_up4.shape[2]

    def body(da_ref, w_ref, h1_ref, dh2_ref, o_ref, mod_ref, g_ref, tok_ref, dh1_ref, do_ref, acc_ref, gg_ref):
        i = pl.program_id(0)

        @pl.when(i == 0)
        def _():
            gg_ref[...] = jnp.zeros_like(gg_ref)

        @pl.when(i % tps == 0)
        def _():
            acc_ref[...] = jnp.zeros_like(acc_ref)

        gg = a_shift = a_scale = a_gate = 0.0
        for rows in _sub_rows(tm):
            du = _nt(da_ref[rows, 0:cols], w_ref[0])
            for k in range(1, nk):
                du = du + _nt(da_ref[rows, k * cols:(k + 1) * cols], w_ref[k])
            h1 = h1_ref[rows, :]
            r = lax.rsqrt(jnp.mean(h1 * h1, -1, keepdims=True) + EPS)
            hh = h1 * r
            n2 = hh * g_ref[...]
            dn2 = du * (1.0 + mod_ref[0, 4:5, :])
            dhat = dn2 * g_ref[...]
            dh1 = dh2_ref[rows, :] + r * (dhat - hh * jnp.mean(dhat * hh, -1, keepdims=True))
            dh1_ref[rows, :] = dh1
            do_ref[rows, :] = (dh1 * mod_ref[0, 2:3, :]).astype(BF)
            gg = gg + jnp.sum(dn2 * hh, 0, keepdims=True)
            a_shift = a_shift + jnp.sum(du, 0, keepdims=True)
            a_scale = a_scale + jnp.sum(du * n2, 0, keepdims=True)
            a_gate = a_gate + jnp.sum(dh1 * o_ref[rows, :].astype(F32), 0, keepdims=True)
        gg_ref[...] += gg
        acc_ref[0, 0:1, :] += a_shift
        acc_ref[0, 1:2, :] += a_scale
        acc_ref[0, 2:3, :] += a_gate

    row = lambda i: (i, 0)
    vec = pl.BlockSpec((1, d), lambda i: (0, 0))
    return pl.pallas_call(
        body, name="mlp_up_bwd", grid=(t // tm,),
        in_specs=[pl.BlockSpec((tm, D_FF), row), _RESIDENT, pl.BlockSpec((tm, d), row),
                  pl.BlockSpec((tm, d), row), pl.BlockSpec((tm, d), row),
                  pl.BlockSpec((1, N_MOD, d), lambda i: (i // tps, 0, 0)), vec, _token_spec()],
        out_specs=[pl.BlockSpec((tm, d), row), pl.BlockSpec((tm, d), row),
                   pl.BlockSpec((1, 8, d), lambda i: (i // tps, 0, 0)), vec],
        out_shape=[_out((t, d), F32), _out((t, d), BF),
                   _out((nb, 8, d), F32), _out((1, d), F32)],
        compiler_params=_cp(("arbitrary",), 44))(*_pin(d_a), w_up4, *_pin(h1, dh2, o, mod3, g_mlp, token))


def _out_proj_bwd(d_o, w_out, token):
    t, d = d_o.shape
    tm = min(512, t)

    def body(g_ref, w_ref, tok_ref, dp_ref, ds_ref):
        gv = g_ref[...]
        dp_ref[...] = _nt(gv, w_ref[0:POOL_WIDTH, :])
        ds_ref[...] = _nt(gv, w_ref[POOL_WIDTH:, :])

    row = lambda i: (i, 0)
    return pl.pallas_call(
        body, name="out_proj_bwd", grid=(t // tm,),
        in_specs=[pl.BlockSpec((tm, d), row), _RESIDENT, _token_spec()],
        out_specs=[pl.BlockSpec((tm, POOL_WIDTH), row), pl.BlockSpec((tm, SSD_INNER), row)],
        out_shape=[_out((t, POOL_WIDTH), F32), _out((t, SSD_INNER), F32)],
        compiler_params=_cp(("parallel",), 32))(*_pin(d_o), w_out, *_pin(token))


def _pool_bwd(d_ypool, p, w_pool, pool_scale, nb, seq):
    ts = _pool_tile(seq)
    nt = seq // ts
    hb = ts // HALO
    last_block = nb * seq // HALO - 1

    def body(dy_ref, halo_ref, p_ref, wp_ref, ps_ref, du_ref, gw_ref, gs_ref):
        b = pl.program_id(0)
        i = pl.program_id(1)

        @pl.when((b == 0) & (i == 0))
        def _():
            gw_ref[...] = jnp.zeros_like(gw_ref)
            gs_ref[...] = jnp.zeros_like(gs_ref)

        halo = jnp.where(i == nt - 1, 0.0, halo_ref[...])
        dy = dy_ref[...]
        ext = jnp.concatenate([dy, halo], 0)
        tpos = i * ts + _iota((ts + HALO, 1), 0)
        n_ext = ts + HALO
        for g, w in enumerate(POOL_WINDOWS):
            gs = slice(g * POOL_GROUP, (g + 1) * POOL_GROUP)
            wg = wp_ref[g].astype(BF)
            pg = p_ref[:, gs]
            pw = _nn(pg, wg)
            gs_ref[:, gs] += jnp.sum(dy[:, gs] * pw, 0, keepdims=True)
            dpw = (ext[:, gs] * ps_ref[:, gs]).astype(BF)
            gw_ref[g] += _tn(pg, dpw[:ts])
            dp = _nt(dpw, wg)
            cnt = jnp.minimum(tpos + 1, w).astype(F32)
            s = dp / cnt
            sh = 1
            while sh < w:
                s = s + pltpu.roll(s, n_ext - sh, 0)
                sh *= 2
            du_ref[:, gs] = (s[:ts] - dp[:ts]).astype(BF)

    return pl.pallas_call(
        body, name="pool_bwd", grid=(nb, nt),
        in_specs=[pl.BlockSpec((ts, POOL_WIDTH), lambda b, i: (b * nt + i, 0)),
                  pl.BlockSpec((HALO, POOL_WIDTH), lambda b, i: (jnp.minimum((b * nt + i + 1) * hb, last_block), 0)),
                  pl.BlockSpec((ts, POOL_WIDTH), lambda b, i: (b * nt + i, 0)),
                  pl.BlockSpec((4, POOL_GROUP, POOL_GROUP), lambda b, i: (0, 0, 0)),
                  pl.BlockSpec((1, POOL_WIDTH), lambda b, i: (0, 0))],
        out_specs=[pl.BlockSpec((ts, POOL_WIDTH), lambda b, i: (b * nt + i, 0)),
                   pl.BlockSpec((4, POOL_GROUP, POOL_GROUP), lambda b, i: (0, 0, 0)),
                   pl.BlockSpec((1, POOL_WIDTH), lambda b, i: (0, 0))],
        out_shape=[_out((nb * seq, POOL_WIDTH), BF), _out((4, POOL_GROUP, POOL_GROUP), F32),
                   _out((1, POOL_WIDTH), F32)],
        compiler_params=_cp(("arbitrary", "arbitrary"), 32))(*_pin(d_ypool, d_ypool, p, w_pool, pool_scale))


def _ssd_bwd(proj, pre, d_yssd, yssm, h_prev, dt_bias, a_log, dskip_e, g_ssd, nb, seq):
    specs, row, cidx, ns = _ssd_specs(nb, seq, reverse=True)
    specs = specs[2:]

    def body(z0_ref, z1_ref, udt_ref, pre_ref, dys_ref, yssm_ref, hprev_ref,
             dtb_ref, alog_ref, dsk_ref, gs_ref,
             dz_ref, dpre_ref, dudt_ref, ggs_ref, gdsk_ref, ga_ref, gdtb_ref,
             g_ref, dxdt_ref, dyv_ref):
        b = pl.program_id(0)
        c = pl.program_id(1)

        @pl.when(c == 0)
        def _():
            g_ref[...] = jnp.zeros_like(g_ref)

        @pl.when((b == 0) & (c == 0))
        def _():
            ggs_ref[...] = jnp.zeros_like(ggs_ref)
            gdsk_ref[...] = jnp.zeros_like(gdsk_ref)
            ga_ref[...] = jnp.zeros_like(ga_ref)
            gdtb_ref[...] = jnp.zeros_like(gdtb_ref)

        for sub in reversed(range(SSD_SUB)):
            chunk(sub, z0_ref, z1_ref, udt_ref, pre_ref, dys_ref, yssm_ref, hprev_ref, dtb_ref, alog_ref, dsk_ref, gs_ref,
                  dz_ref, dpre_ref, dudt_ref, ggs_ref, gdsk_ref, ga_ref, gdtb_ref, g_ref, dxdt_ref.at[sub], dyv_ref.at[sub])

    def chunk(sub, z0_ref, z1_ref, udt_ref, pre_ref, dys_ref, yssm_ref, hprev_ref,
              dtb_ref, alog_ref, dsk_ref, gs_ref,
              dz_ref, dpre_ref, dudt_ref, ggs_ref, gdsk_ref, ga_ref, gdtb_ref,
              g_ref, dxdt_ref, dyv_ref):
        rows = slice(sub * CHUNK, (sub + 1) * CHUNK)
        r = _chunk_terms(pre_ref[rows, :], udt_ref[rows, :], dtb_ref[...], alog_ref[...])
        xbc = r["xbc"]
        xs = xbc[:, :SSD_INNER]
        dt_e = r["dt_e"]
        xdt = xs * dt_e
        xdt_b = xdt.astype(BF)
        reduce_m = _head_reduce_matrix(GROUP_W, 8)

        def head_sums(v):
            return _nn(v.astype(BF), reduce_m)

        onehot16 = lambda h: (_iota((1, SSD_HEADS), 1) == h).astype(F32)
        onecol16 = lambda h: (_iota((SSD_HEADS, 1), 0) == h).astype(F32)

        d_acum = jnp.zeros((CHUNK, SSD_HEADS), F32)
        d_acum_t = jnp.zeros((SSD_HEADS, CHUNK), F32)
        d_alast = jnp.zeros((1, SSD_HEADS), F32)
        place8 = lambda g: (_iota((8, SSD_HEADS), 1) == _iota((8, SSD_HEADS), 0) + 8 * g).astype(BF)
        d_b, d_c = [], []
        for g in range(2):
            gs = slice(g * GROUP_W, (g + 1) * GROUP_W)
            zg = (z0_ref if g == 0 else z1_ref)[rows, :]
            sz = _sigmoid(zg)
            silu_z = zg * sz
            ys = yssm_ref[rows, gs]
            yg = ys * silu_z
            rg = lax.rsqrt(jnp.mean(yg * yg, -1, keepdims=True) + EPS)
            yh = yg * rg
            dys = dys_ref[rows, gs]
            ggs_ref[:, gs] += jnp.sum(dys * yh, 0, keepdims=True)
            dyh = dys * gs_ref[:, gs]
            dyg = rg * (dyh - yh * jnp.mean(dyh * yh, -1, keepdims=True))
            dy = dyg * silu_z
            dz_ref[rows, gs] = (dyg * ys * (sz * (1.0 + zg * (1.0 - sz)))).astype(BF)
            gdsk_ref[:, gs] += jnp.sum(dy * xs[:, gs], 0, keepdims=True)
            dyv_ref[:, gs] = dy
            dy_b = dy.astype(BF)

            bg = xbc[:, SSD_INNER + g * SSD_STATE:SSD_INNER + (g + 1) * SSD_STATE].astype(BF)
            cg = xbc[:, SSD_INNER + (2 + g) * SSD_STATE:SSD_INNER + (3 + g) * SSD_STATE].astype(BF)
            scores = _nt(cg, bg)
            hg = hprev_ref[0, sub, g]
            hg_b = hg.astype(BF)
            gg = g_ref[g]
            gg_b = gg.astype(BF)
            e_a = r["e_a"][:, gs]
            d_out = r["d_out"][:, gs]
            c_dec = r["c_dec"][:, gs]
            zc = _nn(cg, hg_b)
            wv = e_a * dy
            wv_b = wv.astype(BF)
            da_g = head_sums(wv * zc)
            dcg = _nt(wv_b, hg_b)
            d_hprev = _tn(cg, wv_b)
            vg = _nn(bg, gg_b)
            dxdt_g = d_out * vg
            dd_out = head_sums(xdt[:, gs] * vg)
            dbg = _nt((xdt[:, gs] * d_out).astype(BF), gg_b)
            dcd = _exact_nn(jnp.sum(gg * hg, 0, keepdims=True), reduce_m)
            d_out8 = jnp.exp(r["acum"][CHUNK - 1:CHUNK, 8 * g:8 * g + 8] - r["acum"][:, 8 * g:8 * g + 8])
            c_dec8 = jnp.exp(r["acum"][CHUNK - 1:CHUNK, 8 * g:8 * g + 8])
            t8 = dd_out * d_out8
            d_alast = d_alast + _exact_nn(jnp.sum(t8, 0, keepdims=True) + dcd * c_dec8, place8(g))
            d_acum = d_acum + _exact_nn(da_g - t8, place8(g))
            dsc = jnp.zeros((CHUNK, CHUNK), F32)
            for hh in range(8):
                h = g * 8 + hh
                hs = slice(h * SSD_HEAD_DIM, (h + 1) * SSD_HEAD_DIM)
                lam = _head_decay(r, h)
                m = scores * lam
                dyh_b = dy_b[:, hh * SSD_HEAD_DIM:(hh + 1) * SSD_HEAD_DIM]
                dm = _nt(dyh_b, xdt_b[:, hs])
                tm_ = dm * m
                d_acum = d_acum + jnp.sum(tm_, 1, keepdims=True) * onehot16(h)
                d_acum_t = d_acum_t + onecol16(h) * jnp.sum(tm_, 0, keepdims=True)
                dsc = dsc + dm * lam
                dxdt_ref[:, hs] = _tn(m.astype(BF), dyh_b) + dxdt_g[:, hh * SSD_HEAD_DIM:(hh + 1) * SSD_HEAD_DIM]
            dsc_b = dsc.astype(BF)
            d_c.append(dcg + _nn(dsc_b, bg))
            d_b.append(dbg + _tn(dsc_b, cg))
            g_ref[g] = d_hprev + c_dec * gg

        eye = (_iota((CHUNK, CHUNK), 0) == _iota((CHUNK, CHUNK), 1)).astype(BF)
        d_acum = d_acum - _exact_nt_left(eye, d_acum_t)
        is_last = (_iota((CHUNK, 1), 0) == CHUNK - 1).astype(F32)
        d_acum = d_acum + is_last * d_alast
        triu = (_iota((CHUNK, CHUNK), 0) <= _iota((CHUNK, CHUNK), 1)).astype(BF)
        d_da = _exact_nn_left(triu, d_acum)
        dt = r["dt"]
        ga_ref[...] += jnp.sum(d_da * dt, 0, keepdims=True)
        dxdt = dxdt_ref[...]
        reduce16 = _head_reduce_matrix(SSD_INNER, SSD_HEADS)
        d_dt = d_da * r["a"] + _nn((dxdt * xs).astype(BF), reduce16)
        d_udt = d_dt * _sigmoid(r["dtp"])
        gdtb_ref[...] += jnp.sum(d_udt, 0, keepdims=True)
        dudt_ref[rows, :] = jnp.zeros((CHUNK, dudt_ref.shape[1]), BF)
        dudt_ref[rows, 0:SSD_HEADS] = d_udt.astype(BF)
        pre, sg = r["pre"], r["sg"]
        dsilu = sg * (1.0 + pre * (1.0 - sg))
        dpre_ref[rows, 0:SSD_INNER] = (dsk_ref[...] * dyv_ref[...] + dxdt * dt_e) * dsilu[:, 0:SSD_INNER]
        for g in range(2):
            bs = slice(SSD_INNER + g * SSD_STATE, SSD_INNER + (g + 1) * SSD_STATE)
            cs = slice(SSD_INNER + (2 + g) * SSD_STATE, SSD_INNER + (3 + g) * SSD_STATE)
            dpre_ref[rows, bs] = d_b[g] * dsilu[:, bs]
            dpre_ref[rows, cs] = d_c[g] * dsilu[:, cs]

    t = nb * seq
    vec = _const_spec((1, SSD_INNER))
    small = _const_spec((1, SSD_HEADS))
    return pl.pallas_call(
        body, name="ssd_bwd", grid=(nb, ns),
        in_specs=specs + [pl.BlockSpec((SSD_ROWS, CONV_CH), lambda b, c: (row(b, c), 0)),
                          pl.BlockSpec((SSD_ROWS, SSD_INNER), lambda b, c: (row(b, c), 0)),
                          pl.BlockSpec((SSD_ROWS, SSD_INNER), lambda b, c: (row(b, c), 0)),
                          pl.BlockSpec((1, SSD_SUB, 2, SSD_STATE, GROUP_W), lambda b, c: (b, cidx(c), 0, 0, 0)),
                          small, small, vec, vec],
        out_specs=[pl.BlockSpec((SSD_ROWS, SSD_INNER), lambda b, c: (row(b, c), 0)),
                   pl.BlockSpec((SSD_ROWS, CONV_CH), lambda b, c: (row(b, c), 0)),
                   pl.BlockSpec((SSD_ROWS, 128), lambda b, c: (row(b, c), 0)),
                   vec, vec, small, small],
        out_shape=[_out((t, SSD_INNER), BF), _out((t, CONV_CH), F32),
                   _out((t, 128), BF), _out((1, SSD_INNER), F32),
                   _out((1, SSD_INNER), F32), _out((1, SSD_HEADS), F32),
                   _out((1, SSD_HEADS), F32)],
        scratch_shapes=[pltpu.VMEM((2, SSD_STATE, GROUP_W), F32), pltpu.VMEM((SSD_SUB, CHUNK, SSD_INNER), F32),
                        pltpu.VMEM((SSD_SUB, CHUNK, SSD_INNER), F32)],
        compiler_params=_cp(("arbitrary", "arbitrary"), 56),
    )(*_pin(proj, proj, proj, pre, d_yssd, yssm, h_prev, dt_bias, a_log, dskip_e, g_ssd))


def _grad_w_out(y_pool, y_ssd, d_o):
    t, d = d_o.shape
    tk = POOL_WIDTH
    n_s = SSD_INNER // tk

    def body(p_ref, s_ref, g_ref, o_ref):
        i = pl.program_id(0)

        @pl.when(i == 0)
        def _():
            o_ref[...] = _tn(p_ref[...], g_ref[...]).astype(BF)

        @pl.when(i > 0)
        def _():
            o_ref[...] = _tn(s_ref[...], g_ref[...]).astype(BF)

    return pl.pallas_call(
        body, name="grad_w_out", grid=(1 + n_s,),
        in_specs=[pl.BlockSpec((t, tk), lambda i: (0, 0)), pl.BlockSpec((t, tk), lambda i: (0, jnp.maximum(i - 1, 0))),
                  pl.BlockSpec((t, d), lambda i: (0, 0))],
        out_specs=pl.BlockSpec((tk, d), lambda i: (i, 0)),
        out_shape=_out((POOL_WIDTH + SSD_INNER, d), BF),
        compiler_params=_cp(("parallel",), 56))(*_pin(y_pool, y_ssd, d_o))


def _grad_w_in_t(d_upool, d_z, d_uxbc, d_udt, u1):
    t, d = u1.shape
    tk = 512
    n_z, n_x = SSD_INNER // tk, CONV_CH // tk

    def body(p_ref, z_ref, x_ref, dt_ref, u_ref, o_ref):
        i = pl.program_id(0)

        @pl.when(i == 0)
        def _():
            o_ref[...] = _tn(p_ref[...], u_ref[...]).astype(BF)

        @pl.when((i >= 1) & (i < 1 + n_z))
        def _():
            o_ref[...] = _tn(z_ref[...], u_ref[...]).astype(BF)

        @pl.when((i >= 1 + n_z) & (i < 1 + n_z + n_x))
        def _():
            o_ref[...] = _tn(x_ref[...], u_ref[...]).astype(BF)

        @pl.when(i == 1 + n_z + n_x)
        def _():
            o_ref[0:128, :] = _tn(dt_ref[...], u_ref[...]).astype(BF)

    return pl.pallas_call(
        body, name="grad_w_in", grid=(2 + n_z + n_x,),
        in_specs=[pl.BlockSpec((t, tk), lambda i: (0, 0)),
                  pl.BlockSpec((t, tk), lambda i: (0, jnp.clip(i - 1, 0, n_z - 1))),
                  pl.BlockSpec((t, tk), lambda i: (0, jnp.clip(i - 1 - n_z, 0, n_x - 1))),
                  pl.BlockSpec((t, 128), lambda i: (0, 0)), pl.BlockSpec((t, d), lambda i: (0, 0))],
        out_specs=pl.BlockSpec((tk, d), lambda i: (i, 0)),
        out_shape=_out((IN_PAD, d), BF),
        compiler_params=_cp(("parallel",), 56))(*_pin(d_upool, d_z, d_uxbc, d_udt, u1))


def _conv_bwd(d_pre, proj, conv_w, nb, seq):
    ts = min(256, seq)
    nt = seq // ts
    hb = ts // CONV_HALO
    last_block = nb * seq // CONV_HALO - 1
    n_ext = CHUNK + CONV_HALO

    def body(dp_ref, dnext_ref, u_ref, cw_ref, du_ref, gw_ref, gb_ref):
        b = pl.program_id(0)
        i = pl.program_id(1)

        @pl.when((b == 0) & (i == 0))
        def _():
            gw_ref[...] = jnp.zeros_like(gw_ref)
            gb_ref[...] = jnp.zeros_like(gb_ref)

        for c0 in range(0, CONV_CH, 128):
            cs = slice(c0, c0 + 128)
            cw = cw_ref[:, cs]
            gw = [0.0] * 4
            gb = 0.0
            for r0 in range(0, ts, CHUNK):
                dp = dp_ref[r0:r0 + CHUNK, cs]
                u = u_ref[r0:r0 + CHUNK, cs]
                if r0 + CHUNK < ts:
                    below = dp_ref[r0 + CHUNK:r0 + CHUNK + CONV_HALO, cs]
                else:
                    below = jnp.where(i == nt - 1, 0.0, dnext_ref[:, cs])
                ext_d = jnp.concatenate([dp, below], 0)
                du = dp * cw[3:4]
                gw[3] = gw[3] + jnp.sum(dp * u, 0, keepdims=True)
                for k in (2, 1, 0):
                    shifted = pltpu.roll(ext_d, n_ext - (3 - k), 0)[:CHUNK]
                    du = du + shifted * cw[k:k + 1]
                    gw[k] = gw[k] + jnp.sum(shifted * u, 0, keepdims=True)
                gb = gb + jnp.sum(dp, 0, keepdims=True)
                du_ref[r0:r0 + CHUNK, cs] = du.astype(BF)
            for k in range(4):
                gw_ref[k:k + 1, cs] += gw[k]
            gb_ref[:, cs] += gb

    return pl.pallas_call(
        body, name="conv_bwd", grid=(nb, nt),
        in_specs=[pl.BlockSpec((ts, CONV_CH), lambda b, i: (b * nt + i, 0)),
                  pl.BlockSpec((CONV_HALO, CONV_CH), lambda b, i: (jnp.minimum((b * nt + i + 1) * hb, last_block), 0)),
                  pl.BlockSpec((ts, CONV_CH), lambda b, i: (b * nt + i, 1)),
                  pl.BlockSpec((4, CONV_CH), lambda b, i: (0, 0))],
        out_specs=[pl.BlockSpec((ts, CONV_CH), lambda b, i: (b * nt + i, 0)),
                   pl.BlockSpec((8, CONV_CH), lambda b, i: (0, 0)), pl.BlockSpec((1, CONV_CH), lambda b, i: (0, 0))],
        out_shape=[_out((nb * seq, CONV_CH), BF), _out((8, CONV_CH), F32),
                   _out((1, CONV_CH), F32)],
        compiler_params=_cp(("arbitrary", "arbitrary"), 48))(*_pin(d_pre, d_pre, proj, conv_w))


def _in_proj_bwd(d_parts, w_in_t, x, dh1, mod3, g_mix, seq, token):
    t, d = x.shape
    nb = t // seq
    tm = min(ROW_TILE, seq)
    tps = seq // tm

    widths = [p.shape[1] for p in d_parts]

    def body(d0_ref, d1_ref, d2_ref, d3_ref, w_ref, x_ref, dh1_ref, mod_ref, g_ref, tok_ref, gx_ref, acc_ref, gg_ref):
        i = pl.program_id(0)

        @pl.when(i == 0)
        def _():
            gg_ref[...] = jnp.zeros_like(gg_ref)

        @pl.when(i % tps == 0)
        def _():
            acc_ref[...] = jnp.zeros_like(acc_ref)

        gg = a_shift = a_scale = 0.0
        for rows in _sub_rows(tm):
            d_cat = jnp.concatenate([p_ref[rows, :] for p_ref in (d0_ref, d1_ref, d2_ref, d3_ref)], 1)
            du = _nn(d_cat, w_ref[...])
            xv = x_ref[rows, :]
            r = lax.rsqrt(jnp.mean(xv * xv, -1, keepdims=True) + EPS)
            hh = xv * r
            n1 = hh * g_ref[...]
            dn1 = du * (1.0 + mod_ref[0, 1:2, :])
            dhat = dn1 * g_ref[...]
            gx_ref[rows, :] = dh1_ref[rows, :] + r * (dhat - hh * jnp.mean(dhat * hh, -1, keepdims=True))
            gg = gg + jnp.sum(dn1 * hh, 0, keepdims=True)
            a_shift = a_shift + jnp.sum(du, 0, keepdims=True)
            a_scale = a_scale + jnp.sum(du * n1, 0, keepdims=True)
        gg_ref[...] += gg
        acc_ref[0, 0:1, :] += a_shift
        acc_ref[0, 1:2, :] += a_scale

    row = lambda i: (i, 0)
    vec = pl.BlockSpec((1, d), lambda i: (0, 0))
    return pl.pallas_call(
        body, name="in_proj_bwd", grid=(t // tm,),
        in_specs=[pl.BlockSpec((tm, wd), row) for wd in widths] +
                 [_RESIDENT, pl.BlockSpec((tm, d), row),
                  pl.BlockSpec((tm, d), row), pl.BlockSpec((1, N_MOD, d), lambda i: (i // tps, 0, 0)), vec, _token_spec()],
        out_specs=[pl.BlockSpec((tm, d), row), pl.BlockSpec((1, 8, d), lambda i: (i // tps, 0, 0)), vec],
        out_shape=[_out((t, d), F32), _out((nb, 8, d), F32),
                   _out((1, d), F32)],
        compiler_params=_cp(("arbitrary",), 40))(*_pin(*d_parts), w_in_t, *_pin(x, dh1, mod3, g_mix, token))


_VEC_LAYOUT = (("g_mix", 1024), ("conv_b", 1536), ("g_ssd", 1024), ("pool_scale", 512), ("g_mlp", 1024),
               ("g_final", 1024), ("dt_bias", 128), ("a_log", 128), ("d_skip_lanes", 1024), ("sq_err", 1024))
_VEC_OFFSET = {}
_off = 0
for _name, _n in _VEC_LAYOUT:
    _VEC_OFFSET[_name] = _off
    _off += _n
_VEC_LANES = _off
_SMALL_PARAMS = ("b_ada", "g_mix", "conv_w", "conv_b", "dt_bias", "a_log", "d_skip", "g_ssd", "w_pool", "pool_scale",
                 "g_mlp", "g_final")


def _pack_vec(parts):
    cols = []
    for name, n in _VEC_LAYOUT:
        v = parts[name]
        if v.shape[1] < n:
            v = jnp.pad(v, ((0, 0), (0, n - v.shape[1])))
        cols.append(v)
    return jnp.concatenate(cols, 1)


def _small_adam(vec_all, wpool_all, convw_all, dmod_all, params):
    names = _SMALL_PARAMS
    nin = 4 + 3 * len(names)

    def body(*refs):
        vec_ref, wp_ref, cw_ref, dm_ref = refs[:4]
        prm = {n: refs[4 + 3 * i:7 + 3 * i] for i, n in enumerate(names)}
        loss_ref = refs[nin]
        outs = {n: refs[nin + 1 + 4 * i:nin + 5 + 4 * i] for i, n in enumerate(names)}
        vsum = vec_ref[0]
        for s in range(1, N_DEV):
            vsum = vsum + vec_ref[s]

        def lanes(name, n):
            off = _VEC_OFFSET[name]
            return vsum[:, off:off + n]

        grads = {n: lanes(n, prm[n][0].shape[1]) for n in ("g_mix", "conv_b", "g_ssd", "pool_scale", "g_mlp", "g_final", "dt_bias")}
        grads["a_log"] = lanes("a_log", SSD_HEADS) * (-jnp.exp(prm["a_log"][0][...]))
        per_lane = jnp.broadcast_to(lanes("d_skip_lanes", SSD_INNER), (8, SSD_INNER))
        grads["d_skip"] = _exact_nn(per_lane, _head_reduce_matrix(SSD_INNER, SSD_HEADS))[0:1]
        gwp = wp_ref[0].astype(F32)
        gcw = cw_ref[0]
        gb = jnp.sum(dm_ref[0], 0, keepdims=True)
        for s in range(1, N_DEV):
            gwp = gwp + wp_ref[s].astype(F32)
            gcw = gcw + cw_ref[s]
            gb = gb + jnp.sum(dm_ref[s], 0, keepdims=True)
        grads["w_pool"] = gwp
        grads["conv_w"] = gcw[0:4]
        grads["b_ada"] = gb
        total = jnp.sum(lanes("sq_err", D_MODEL), 1, keepdims=True) * (0.5 / D_MODEL)
        loss_ref[...] = jnp.broadcast_to(total, loss_ref.shape)
        for n in names:
            w_ref, m_ref, v_ref = prm[n]
            g = grads[n]
            d, m2, v2 = _adam_math(w_ref[...], g, m_ref[...], v_ref[...])
            g_ref, d_ref, m2_ref, v2_ref = outs[n]
            g_ref[...] = g
            d_ref[...] = d
            m2_ref[...] = m2
            v2_ref[...] = v2

    flat = [vec_all, wpool_all, convw_all, dmod_all]
    out_shape = [jax.ShapeDtypeStruct((1, 128), F32)]
    for n in names:
        flat += list(params[n])
        out_shape += [jax.ShapeDtypeStruct(params[n][0].shape, F32)] * 4
    vm = pl.BlockSpec(memory_space=pltpu.VMEM)
    res = pl.pallas_call(body, name="small_adam", out_shape=out_shape, in_specs=[vm] * len(flat),
                         out_specs=[vm] * len(out_shape), compiler_params=_cp(vmem_mb=48))(*flat)
    return res[0], {n: res[1 + 4 * i:5 + 4 * i] for i, n in enumerate(names)}


_WEIGHTS = ("w_ada", "b_ada", "g_mix", "w_in", "conv_w", "conv_b", "dt_bias", "a_log", "d_skip", "g_ssd", "w_pool",
            "pool_scale", "w_out", "g_mlp", "w_up", "w_down", "g_final")


def _local_step(x2, tg2, mod3, seq, w_in_t, first_token, weights_arrived, weights_later, start_reduce, before_last,
                conv_w_full, sp):
    t, d = x2.shape
    nb = t // seq
    dskip_e = jnp.repeat(sp["d_skip"], SSD_HEAD_DIM, axis=1)
    proj, u1 = _in_proj(x2, mod3, sp["g_mix"], w_in_t, seq, first_token)
    y_ssd, yssm, h_prev, pre = _ssd_fwd(proj, conv_w_full, sp["conv_b"], sp["dt_bias"], sp["a_log"], dskip_e, sp["g_ssd"], nb, seq)
    y_pool, p = _pool_fwd(proj, sp["w_pool"], sp["pool_scale"], nb, seq, weights_arrived(y_ssd))
    w_out_f, w_up4, w_down4 = weights_later(y_pool)
    w_down_f = w_down4.reshape(D_FF, d)
    h1, o, u2 = _out_proj(y_pool, y_ssd, w_out_f, x2, mod3, sp["g_mlp"], seq)
    a_up = _mlp_up(u2, w_up4)
    d_dn, dh2, sq, gg_final, d_gf = _mlp_down_loss(a_up, w_down_f, h1, mod3, sp["g_final"], tg2, seq)

    gw_down = _tn_matmul(a_up, d_dn, 512, d, "grad_w_down", square_relu=True)
    tok = start_reduce("w_down", gw_down.reshape(N_CHIPS, D_FF // N_CHIPS, d))
    d_a = _mlp_down_bwd(d_dn, w_down4, a_up, tok)
    gw_up4 = _tn_matmul(u2, d_a, 512, d, "grad_w_up", out3=True)
    tok = start_reduce("w_up", gw_up4)
    dh1, d_o, accf, gg_mlp = _mlp_up_bwd(d_a, w_up4, h1, dh2, o, mod3, sp["g_mlp"], seq, tok)
    gw_out = _grad_w_out(y_pool, y_ssd, d_o)
    tok = start_reduce("w_out", gw_out.reshape(N_CHIPS, gw_out.shape[0] // N_CHIPS, d))
    d_ypool, d_yssd = _out_proj_bwd(d_o, w_out_f, tok)
    d_upool, gw_pool, g_ps = _pool_bwd(d_ypool, p, sp["w_pool"], sp["pool_scale"], nb, seq)
    d_z, d_pre, d_udt, gg_ssd, gdsk, ga, gdtb = _ssd_bwd(proj, pre, d_yssd, yssm, h_prev, sp["dt_bias"], sp["a_log"],
                                                        dskip_e, sp["g_ssd"], nb, seq)
    d_uxbc, gconvw, gconvb = _conv_bwd(d_pre, proj, conv_w_full, nb, seq)
    gw_in_t = _grad_w_in_t(d_upool, d_z, d_uxbc, d_udt, u1)
    shard_rows = IN_WIDTH // N_CHIPS
    tok = start_reduce("w_in", jnp.stack([gw_in_t[k * shard_rows:(k + 1) * shard_rows] for k in range(N_CHIPS)]))
    gx, accm, gg_mix = _in_proj_bwd([d_upool, d_z, d_uxbc, d_udt], w_in_t, x2, dh1, mod3, sp["g_mix"], seq, before_last(tok))

    d_mod = jnp.concatenate([accm[:, 0], accm[:, 1], accf[:, 2], accf[:, 0], accf[:, 1], d_gf[:, 0]], 1)
    vec = _pack_vec({"g_mix": gg_mix, "conv_b": gconvb, "g_ssd": gg_ssd, "pool_scale": g_ps, "g_mlp": gg_mlp,
                     "g_final": gg_final, "dt_bias": gdtb, "a_log": ga, "d_skip_lanes": gdsk, "sq_err": sq})
    return gx, d_mod, vec, gw_pool, gconvw


def kernel(x, c, w_ada, b_ada, g_mix, w_in, conv_w, conv_b, dt_bias, a_log, d_skip, g_ssd, w_pool, pool_scale, w_out, g_mlp, w_up, w_down, g_final, loss_target, m_w_ada, m_b_ada, m_g_mix, m_w_in, m_conv_w, m_conv_b, m_dt_bias, m_a_log, m_d_skip, m_g_ssd, m_w_pool, m_pool_scale, m_w_out, m_g_mlp, m_w_up, m_w_down, m_g_final, v_w_ada, v_b_ada, v_g_mix, v_w_in, v_conv_w, v_conv_b, v_dt_bias, v_a_log, v_d_skip, v_g_ssd, v_w_pool, v_pool_scale, v_w_out, v_g_mlp, v_w_up, v_w_down, v_g_final):
    nb, seq, d = x.shape
    t = nb * seq
    xi, yi, ci = _mesh_pos()
    chip = 2 * xi + yi
    me = 4 * xi + 2 * yi + ci
    ada_cols = w_ada.shape[2]
    conv_cols = conv_w.shape[2]
    in_cols = w_in.shape[2]
    w_in_s, m_w_in_s, v_w_in_s = w_in[0].T, m_w_in[0].T, v_w_in[0].T

    c8, convw8 = _all_gather_small([c, conv_w[0]])
    w_in_b = w_in_s.astype(BF)
    i_send, i_recv, i_src, i_land, in_token = _ici_start(
        [w_in_b], [jax.ShapeDtypeStruct((N_CHIPS,) + w_in_b.shape, BF)], _gather_sent, _gather_landing, "gather_start_w_in",
        after=c8)
    c_all = c8.reshape(N_DEV * nb, d)
    conv_w_full = convw8[0::2].transpose(1, 0, 2).reshape(4, N_CHIPS * conv_cols)
    b_shard = lax.dynamic_slice(b_ada, (0, chip * ada_cols), (1, ada_cols))
    mod_part, c_act = _ada_mod(c_all, w_ada[0], b_shard, in_token)
    mod_rows = mod_part.reshape(N_DEV, nb, ada_cols)
    m_send, m_recv, m_src, m_land, _ = _ici_start(
        [mod_rows], [jax.ShapeDtypeStruct((N_CHIPS, nb, ada_cols), F32)], _mod_sent, _mod_landing, "mod_start", after=mod_part)

    later = [w_out[0].astype(BF), w_up[0].astype(BF), w_down[0].astype(BF)]
    in_shard, in_land = _ici_wait(i_send, i_recv, i_src, i_land, [m_src[0]] + later, _gather_sent, _gather_landing,
                                  "gather_wait_w_in")
    (w_in4,) = _gather_finish(in_land, in_shard)
    w_in_t = jnp.pad(w_in4.reshape(N_CHIPS * in_cols, d), ((0, IN_PAD - N_CHIPS * in_cols), (0, 0)))
    mod_mine, mod_land = _ici_wait(m_send, m_recv, m_src, m_land, w_in_t, _mod_sent, _mod_landing, "mod_wait")
    mod_own = lax.dynamic_slice(mod_mine[0], (me, 0, 0), (1, nb, ada_cols))
    mod4 = lax.dynamic_update_slice(mod_land[0], mod_own, (chip, 0, 0))
    mod3 = mod4.transpose(1, 0, 2).reshape(nb, N_MOD, d)
    g_send, g_recv, g_src, g_land, first_token = _ici_start(
        later, [jax.ShapeDtypeStruct((N_CHIPS,) + s.shape, BF) for s in later], _gather_sent, _gather_landing, "gather_start",
        after=w_in4)

    def weights_arrived(after):
        shards, lands = _ici_wait(g_send, g_recv, g_src, g_land, after, _gather_sent, _gather_landing, "gather_wait")
        pending["forward"] = _forward_start(lands, "forward_start") + (shards,)
        return pending["forward"][3]

    def weights_later(after):
        f_send, f_recv, f_land, _, shards = pending["forward"]
        lands = _forward_wait(f_send, f_recv, f_land, after, "forward_wait")
        w_out4, w_up4, w_down4 = [lax.dynamic_update_slice(land, shard[None], (chip, 0, 0)) for land, shard in zip(lands, shards)]
        return w_out4.reshape(N_CHIPS * w_out.shape[1], d), w_up4, w_down4

    pending = {}

    def start_reduce(name, grad4):
        pending[name] = _reduce_start(grad4, "reduce_start_" + name)
        return pending[name][4]

    pos = jnp.stack([ci, chip, me]).astype(jnp.int32)
    early = ("w_out", "w_up", "w_down")

    def summed_half(name, after):
        r_send, r_recv, r_src, r_land, _ = pending[name]
        own, recv = _reduce_wait(r_send, r_recv, r_src, r_land, after, "reduce_wait_" + name)
        return _sum_eight(recv, own, pos)

    def before_last(token):
        pending["early_halves"] = _exchange_start([summed_half(n, token) for n in early], SIBLING, "halves_start_early")
        return pending["early_halves"][4]

    sp = dict(g_mix=g_mix, conv_b=conv_b, dt_bias=dt_bias, a_log=a_log, d_skip=d_skip, g_ssd=g_ssd,
              w_pool=w_pool[0], pool_scale=pool_scale, g_mlp=g_mlp, g_final=g_final.reshape(1, d))
    gx, d_mod, vec, gw_pool, gconvw = _local_step(
        x.reshape(t, d), loss_target.reshape(t, d), mod3, seq, w_in_t, first_token, weights_arrived, weights_later, start_reduce,
        before_last, conv_w_full, sp)

    small_parts = [vec, gw_pool.reshape(4 * POOL_GROUP, POOL_GROUP).astype(BF), gconvw, d_mod]
    s_send, s_recv, s_src, s_land, s_token = _exchange_start(small_parts, ALL_PEERS, "small_start")
    h_send, h_recv, h_src, h_land, h_token = _exchange_start([summed_half("w_in", s_token)], SIBLING, "halves_start")
    e_send, e_recv, e_src, e_land, _ = pending["early_halves"]
    e_own, e_got = _exchange_wait(e_send, e_recv, e_src, e_land, SIBLING, h_token, "halves_wait_early")
    res = {}
    for i, (n, w, m, v) in enumerate((("w_out", w_out, m_w_out, v_w_out), ("w_up", w_up, m_w_up, v_w_up),
                                      ("w_down", w_down, m_w_down, v_w_down))):
        g, dl, m2, v2 = _adam_big(e_own[i], e_got[i], w[0], m[0], v[0], pos)
        res[n] = (g[None], dl[None], m2[None], v2[None])

    s_own, s_got = _exchange_wait(s_send, s_recv, s_src, s_land, ALL_PEERS, res["w_down"][1], "small_wait")
    vec8, wpool8, convw8g, dmod8 = [lax.dynamic_update_slice(got, mine[None], (me,) + (0,) * mine.ndim)
                                    for got, mine in zip(s_got, s_own)]
    convw8s = lax.dynamic_slice(convw8g, (0, 0, chip * conv_cols), (N_DEV, 8, conv_cols))
    m_in = dict(b_ada=m_b_ada, g_mix=m_g_mix, conv_w=m_conv_w[0], conv_b=m_conv_b, dt_bias=m_dt_bias, a_log=m_a_log,
                d_skip=m_d_skip, g_ssd=m_g_ssd, w_pool=m_w_pool.reshape(4 * POOL_GROUP, POOL_GROUP), pool_scale=m_pool_scale,
                g_mlp=m_g_mlp, g_final=m_g_final.reshape(1, d))
    v_in = dict(b_ada=v_b_ada, g_mix=v_g_mix, conv_w=v_conv_w[0], conv_b=v_conv_b, dt_bias=v_dt_bias, a_log=v_a_log,
                d_skip=v_d_skip, g_ssd=v_g_ssd, w_pool=v_w_pool.reshape(4 * POOL_GROUP, POOL_GROUP), pool_scale=v_pool_scale,
                g_mlp=v_g_mlp, g_final=v_g_final.reshape(1, d))
    w_small = dict(sp, b_ada=b_ada, conv_w=conv_w[0], w_pool=w_pool.reshape(4 * POOL_GROUP, POOL_GROUP))
    loss_row, small = _small_adam(vec8, wpool8, convw8s, dmod8, {n: (w_small[n], m_in[n], v_in[n]) for n in _SMALL_PARAMS})

    dmod_all = dmod8.reshape(N_DEV * nb, N_CHIPS * ada_cols)
    dmod_cols = lax.dynamic_slice(dmod_all, (0, chip * ada_cols), (N_DEV * nb, ada_cols))
    res.update({n: tuple(r.reshape(w.shape) for r in small[n])
                for n, w in (("b_ada", b_ada), ("g_mix", g_mix), ("conv_w", conv_w), ("conv_b", conv_b), ("dt_bias", dt_bias),
                             ("a_log", a_log), ("d_skip", d_skip), ("g_ssd", g_ssd), ("w_pool", w_pool),
                             ("pool_scale", pool_scale), ("g_mlp", g_mlp), ("g_final", g_final))})
    g_ada, d_ada, m_ada, v_ada = _adam_ada(c_act.T.astype(BF), dmod_cols, w_ada[0], m_w_ada[0], v_w_ada[0])
    res["w_ada"] = (g_ada[None], d_ada[None], m_ada[None], v_ada[None])
    h_own, h_got = _exchange_wait(h_send, h_recv, h_src, h_land, SIBLING, g_ada, "halves_wait")
    g_in, dl, m2, v2 = _adam_big(h_own[0], h_got[0], w_in_s, m_w_in_s, v_w_in_s, pos)
    res["w_in"] = (g_in.T[None], dl.T[None], m2.T[None], v2.T[None])

    loss = loss_row[0, 0]
    return (loss, gx.reshape(nb, seq, d), *[res[n][0] for n in _WEIGHTS], *[res[n][1] for n in _WEIGHTS],
            *[res[n][2] for n in _WEIGHTS], *[res[n][3] for n in _WEIGHTS])
```

```python
import jax
import jax.numpy as jnp
from jax import lax
from jax.experimental import pallas as pl
from jax.experimental.pallas import tpu as pltpu

F32 = jnp.float32
BF = jnp.bfloat16
MESH = pl.DeviceIdType.MESH

EPS = 1e-5
D_MODEL = 1024
POOL_WIDTH = 512
POOL_WINDOWS = (2, 4, 8, 16)
POOL_GROUP = 128
SSD_INNER = 1024
SSD_HEADS = 16
SSD_HEAD_DIM = 64
SSD_STATE = 128
GROUP_W = 512
CHUNK = 128
CONV_CH = 1536
OFF_DT = 3072
IN_WIDTH = 3088
IN_PAD = 3200
D_FF = 4096
N_MOD = 6
N_CHIPS = 4
N_DEV = 8
HALO = 16
CONV_HALO = 8

ADAM_LR = 0.001
ADAM_B1 = 0.9
ADAM_B2 = 0.999
ADAM_EPS = 1e-08
ADAM_WD = 0.01
ADAM_STEP = 10

VMEM_BYTES_V7X = 64 * 1024 * 1024


def _cp(semantics=None, vmem_mb=48, **kw):
    assert vmem_mb * 1024 * 1024 < VMEM_BYTES_V7X
    args = dict(vmem_limit_bytes=vmem_mb * 1024 * 1024, **kw)
    if semantics is not None:
        args["dimension_semantics"] = semantics
    return pltpu.CompilerParams(**args)


def _out(shape, dtype):
    return pltpu.HBM(shape, dtype)


def _pin(*arrays):
    return [pltpu.with_memory_space_constraint(a, pltpu.HBM) for a in arrays]


def _nn(a, b):
    return jnp.dot(a, b, preferred_element_type=F32)


def _nt(a, b):
    return lax.dot_general(a, b, (((1,), (1,)), ((), ())), preferred_element_type=F32)


def _tn(a, b):
    return lax.dot_general(a, b, (((0,), (0,)), ((), ())), preferred_element_type=F32)


def _split3(v):
    hi = v.astype(BF)
    r1 = v - hi.astype(F32)
    mid = r1.astype(BF)
    lo = (r1 - mid.astype(F32)).astype(BF)
    return hi, mid, lo


def _exact_nn(v, m01):
    hi, mid, lo = _split3(v)
    return _nn(hi, m01) + _nn(mid, m01) + _nn(lo, m01)


def _exact_nn_left(m01, v):
    hi, mid, lo = _split3(v)
    return _nn(m01, hi) + _nn(m01, mid) + _nn(m01, lo)


def _exact_nt_left(m01, v):
    hi, mid, lo = _split3(v)
    return _nt(m01, hi) + _nt(m01, mid) + _nt(m01, lo)


def _sigmoid(v):
    return 1.0 / (1.0 + jnp.exp(-v))


def _iota(shape, dim):
    return lax.broadcasted_iota(jnp.int32, shape, dim)


def _head_expand_matrix(heads, width):
    return (_iota((heads, width), 1) // SSD_HEAD_DIM == _iota((heads, width), 0)).astype(BF)


def _head_reduce_matrix(width, heads):
    return (_iota((width, heads), 0) // SSD_HEAD_DIM == _iota((width, heads), 1)).astype(BF)


def _mesh_pos():
    return lax.axis_index("x"), lax.axis_index("y"), lax.axis_index("c")


def _flip(v, bit):
    return v + bit - 2 * bit * v


def _all_gather_small(arrays):
    n = len(arrays)

    def body(*refs):
        in_refs, out_refs = refs[:n], refs[n:2 * n]
        send_sems, recv_sems, local_sems = refs[2 * n:]
        x, y, c = _mesh_pos()
        me = 4 * x + 2 * y + c
        local = []
        for a in range(n):
            cp = pltpu.make_async_copy(in_refs[a], out_refs[a].at[me], local_sems.at[a])
            cp.start()
            local.append(cp)
        sends = []
        for k in range(1, N_DEV):
            peer = (_flip(x, (k >> 2) & 1), _flip(y, (k >> 1) & 1), _flip(c, k & 1))
            for a in range(n):
                cp = pltpu.make_async_remote_copy(
                    src_ref=in_refs[a], dst_ref=out_refs[a].at[me],
                    send_sem=send_sems.at[a, k], recv_sem=recv_sems.at[a, k],
                    device_id=peer, device_id_type=MESH)
                cp.start()
                sends.append(cp)
        for k in range(1, N_DEV):
            px, py, pc = _flip(x, (k >> 2) & 1), _flip(y, (k >> 1) & 1), _flip(c, k & 1)
            src = 4 * px + 2 * py + pc
            for a in range(n):
                pltpu.make_async_remote_copy(
                    src_ref=in_refs[a], dst_ref=out_refs[a].at[src],
                    send_sem=send_sems.at[a, k], recv_sem=recv_sems.at[a, k],
                    device_id=(px, py, pc), device_id_type=MESH).wait_recv()
        for cp in sends:
            cp.wait_send()
        for cp in local:
            cp.wait()

    vm = pl.BlockSpec(memory_space=pltpu.VMEM)
    return pl.pallas_call(
        body, name="all_gather_small",
        out_shape=[jax.ShapeDtypeStruct((N_DEV,) + a.shape, a.dtype) for a in arrays],
        in_specs=[vm] * n, out_specs=[vm] * n,
        scratch_shapes=[pltpu.SemaphoreType.DMA((n, N_DEV)), pltpu.SemaphoreType.DMA((n, N_DEV)),
                        pltpu.SemaphoreType.DMA((n,))],
        compiler_params=_cp(vmem_mb=32),
    )(*arrays)


_HBM = pl.BlockSpec(memory_space=pltpu.HBM)
_SEM = pl.BlockSpec(memory_space=pltpu.SEMAPHORE)
_DATAFLOW = pltpu.SideEffectType.DATAFLOW_SIDE_EFFECTING


def _peer_chip(x, y, j):
    return _flip(x, (j >> 1) & 1), _flip(y, j & 1)


def _ici_start(srcs, land_shapes, sent, landing, name, after):
    n = len(srcs)

    def body(*refs):
        src_refs, land_refs = refs[:n], refs[n:2 * n]
        send_sems, recv_sems = refs[2 * n + 1], refs[2 * n + 2]
        token = refs[-1]
        x, y, c = _mesh_pos()
        for j in range(1, N_CHIPS):
            px, py = _peer_chip(x, y, j)
            for a in range(n):
                pltpu.make_async_remote_copy(
                    src_ref=sent(src_refs[a], c, 2 * px + py), dst_ref=landing(land_refs[a], c, 2 * x + y),
                    send_sem=send_sems.at[a * (N_CHIPS - 1) + j - 1], recv_sem=recv_sems.at[a * (N_CHIPS - 1) + j - 1],
                    device_id=(px, py, c), device_id_type=MESH).start()
        token[...] = jnp.zeros_like(token)

    sems = pltpu.SemaphoreType.DMA((n * (N_CHIPS - 1),))
    lands = [pltpu.with_memory_space_constraint(lax.empty(s.shape, s.dtype), pltpu.HBM) for s in land_shapes]
    outs = pl.pallas_call(
        body, name=name,
        out_shape=(sems, sems, *[pltpu.HBM(s.shape, s.dtype) for s in srcs],
                   *[pltpu.HBM(s.shape, s.dtype) for s in land_shapes], jax.ShapeDtypeStruct((8, 128), F32)),
        in_specs=[_HBM] * (2 * n + 1), out_specs=[_SEM, _SEM] + [_HBM] * (2 * n) + [pl.BlockSpec(memory_space=pltpu.VMEM)],
        input_output_aliases={i: 2 + i for i in range(2 * n)},
        compiler_params=pltpu.CompilerParams(has_side_effects=_DATAFLOW),
    )(*_pin(*srcs), *lands, *_pin(after))
    return outs[0], outs[1], outs[2:2 + n], outs[2 + n:2 + 2 * n], outs[-1]


def _ici_wait(send_sems, recv_sems, src_thru, land_thru, after, sent, landing, name):
    n = len(src_thru)
    afters = list(after) if isinstance(after, (list, tuple)) else [after]

    def body(*refs):
        src_refs, land_refs = refs[:n], refs[n:2 * n]
        send_sems, recv_sems = refs[2 * n], refs[2 * n + 1]
        x, y, c = _mesh_pos()
        for j in range(1, N_CHIPS):
            px, py = _peer_chip(x, y, j)
            for a in range(n):
                cp = pltpu.make_async_remote_copy(
                    src_ref=sent(src_refs[a], c, 2 * px + py), dst_ref=landing(land_refs[a], c, 2 * px + py),
                    send_sem=send_sems.at[a * (N_CHIPS - 1) + j - 1], recv_sem=recv_sems.at[a * (N_CHIPS - 1) + j - 1],
                    device_id=(px, py, c), device_id_type=MESH)
                cp.wait_send()
                cp.wait_recv()

    outs = pl.pallas_call(
        body, name=name,
        out_shape=tuple(pltpu.HBM(s.shape, s.dtype) for s in (*src_thru, *land_thru)),
        in_specs=[_HBM] * (2 * n) + [_SEM, _SEM] + [_HBM] * len(afters), out_specs=[_HBM] * (2 * n),
        input_output_aliases={i: i for i in range(2 * n)},
        compiler_params=pltpu.CompilerParams(has_side_effects=_DATAFLOW),
    )(*src_thru, *land_thru, send_sems, recv_sems, *_pin(*afters))
    return outs[:n], outs[n:]


def _col_half(ref, which, lead=()):
    hc = ref.shape[-1] // 2
    return ref.at[(*lead, slice(None), pl.ds(pl.multiple_of(which * hc, 128), hc))]


def _gather_sent(ref, c, dst_chip):
    return _col_half(ref, c)


def _gather_landing(ref, c, src_chip):
    return _col_half(ref, c, lead=(src_chip,))


def _mod_sent(ref, c, dst_chip):
    return ref.at[2 * dst_chip + c]


def _mod_landing(ref, c, src_chip):
    return ref.at[src_chip]


def _reduce_copy(src_ref, land_ref, send_sems, recv_sems, k, receiving):
    x, y, c = _mesh_pos()
    px, py, pc = _flip(x, (k >> 2) & 1), _flip(y, (k >> 1) & 1), _flip(c, k & 1)
    hc = src_ref.shape[2] // 2
    src = src_ref.at[2 * px + py, :, pl.ds(pl.multiple_of(pc * hc, 128), hc)]
    slot = (4 * px + 2 * py + pc) if receiving else (4 * x + 2 * y + c)
    return pltpu.make_async_remote_copy(
        src_ref=src, dst_ref=land_ref.at[slot], send_sem=send_sems.at[k - 1], recv_sem=recv_sems.at[k - 1],
        device_id=(px, py, pc), device_id_type=MESH)


def _reduce_start(grad4, name):
    k4, r, cols = grad4.shape

    def body(src_ref, land_ref, send_sems, recv_sems, src_thru, land_thru, token):
        for k in range(1, N_DEV):
            _reduce_copy(src_ref, land_ref, send_sems, recv_sems, k, receiving=False).start()
        token[...] = jnp.zeros_like(token)

    sems = pltpu.SemaphoreType.DMA((N_DEV - 1,))
    land = pltpu.with_memory_space_constraint(lax.empty((N_DEV, r, cols // 2), grad4.dtype), pltpu.HBM)
    return pl.pallas_call(
        body, name=name,
        out_shape=(sems, sems, pltpu.HBM(grad4.shape, grad4.dtype), pltpu.HBM(land.shape, land.dtype),
                   jax.ShapeDtypeStruct((8, 128), F32)),
        in_specs=[_HBM, _HBM], out_specs=[_SEM, _SEM, _HBM, _HBM, pl.BlockSpec(memory_space=pltpu.VMEM)],
        input_output_aliases={0: 2, 1: 3},
        compiler_params=pltpu.CompilerParams(has_side_effects=_DATAFLOW),
    )(*_pin(grad4), land)


def _reduce_wait(send_sems, recv_sems, src_thru, land_thru, after, name):
    def body(src_ref, land_ref, send_sems, recv_sems, after_ref, src_out, land_out):
        for k in range(1, N_DEV):
            cp = _reduce_copy(src_ref, land_ref, send_sems, recv_sems, k, receiving=True)
            cp.wait_send()
            cp.wait_recv()

    return pl.pallas_call(
        body, name=name,
        out_shape=(pltpu.HBM(src_thru.shape, src_thru.dtype), pltpu.HBM(land_thru.shape, land_thru.dtype)),
        in_specs=[_HBM, _HBM, _SEM, _SEM, _HBM], out_specs=[_HBM, _HBM],
        input_output_aliases={0: 0, 1: 1},
        compiler_params=pltpu.CompilerParams(has_side_effects=_DATAFLOW),
    )(src_thru, land_thru, send_sems, recv_sems, *_pin(after))


def _peer_copy(src_ref, land_ref, send_sems, recv_sems, idx, k, receiving):
    x, y, c = _mesh_pos()
    px, py, pc = _flip(x, (k >> 2) & 1), _flip(y, (k >> 1) & 1), _flip(c, k & 1)
    if land_ref.shape[0] == N_DEV:
        slot = (4 * px + 2 * py + pc) if receiving else (4 * x + 2 * y + c)
    else:
        slot = pc if receiving else c
    return pltpu.make_async_remote_copy(
        src_ref=src_ref, dst_ref=land_ref.at[slot], send_sem=send_sems.at[idx], recv_sem=recv_sems.at[idx],
        device_id=(px, py, pc), device_id_type=MESH)


def _exchange_start(arrays, peers, name):
    n = len(arrays)

    def body(*refs):
        src_refs, land_refs = refs[:n], refs[n:2 * n]
        send_sems, recv_sems = refs[2 * n], refs[2 * n + 1]
        token = refs[-1]
        for j, k in enumerate(peers):
            for a in range(n):
                _peer_copy(src_refs[a], land_refs[a], send_sems, recv_sems, a * len(peers) + j, k, receiving=False).start()
        token[...] = jnp.zeros_like(token)

    sems = pltpu.SemaphoreType.DMA((n * len(peers),))
    n_slots = N_DEV if len(peers) > 1 else 2
    lands = [pltpu.with_memory_space_constraint(lax.empty((n_slots,) + a.shape, a.dtype), pltpu.HBM) for a in arrays]
    outs = pl.pallas_call(
        body, name=name,
        out_shape=(sems, sems, *[pltpu.HBM(a.shape, a.dtype) for a in arrays], *[pltpu.HBM(l.shape, l.dtype) for l in lands],
                   jax.ShapeDtypeStruct((8, 128), F32)),
        in_specs=[_HBM] * (2 * n), out_specs=[_SEM, _SEM] + [_HBM] * (2 * n) + [pl.BlockSpec(memory_space=pltpu.VMEM)],
        input_output_aliases={i: 2 + i for i in range(2 * n)},
        compiler_params=pltpu.CompilerParams(has_side_effects=_DATAFLOW),
    )(*_pin(*arrays), *lands)
    return outs[0], outs[1], outs[2:2 + n], outs[2 + n:2 + 2 * n], outs[-1]


def _exchange_wait(send_sems, recv_sems, src_thru, land_thru, peers, after, name):
    n = len(src_thru)

    def body(*refs):
        src_refs, land_refs = refs[:n], refs[n:2 * n]
        send_sems, recv_sems = refs[2 * n], refs[2 * n + 1]
        for j, k in enumerate(peers):
            for a in range(n):
                cp = _peer_copy(src_refs[a], land_refs[a], send_sems, recv_sems, a * len(peers) + j, k, receiving=True)
                cp.wait_send()
                cp.wait_recv()

    outs = pl.pallas_call(
        body, name=name,
        out_shape=tuple(pltpu.HBM(s.shape, s.dtype) for s in (*src_thru, *land_thru)),
        in_specs=[_HBM] * (2 * n) + [_SEM, _SEM, _HBM], out_specs=[_HBM] * (2 * n),
        input_output_aliases={i: i for i in range(2 * n)},
        compiler_params=pltpu.CompilerParams(has_side_effects=_DATAFLOW),
    )(*src_thru, *land_thru, send_sems, recv_sems, *_pin(after))
    return outs[:n], outs[n:]


ALL_PEERS = tuple(range(1, N_DEV))
SIBLING = (1,)


def _sum_eight(recv, grad4, pos):
    n, r, hc = recv.shape
    steps = 2
    tc = hc // steps

    def body(pos_ref, r_ref, g_ref, o_ref):
        me = pos_ref[2]
        o_ref[...] = jnp.zeros_like(o_ref)
        for s in range(n):
            @pl.when(me == s)
            def _():
                o_ref[...] += g_ref[0].astype(F32)

            @pl.when(me != s)
            def _():
                o_ref[...] += r_ref[s].astype(F32)

    grid_spec = pltpu.PrefetchScalarGridSpec(
        num_scalar_prefetch=1, grid=(steps,),
        in_specs=[pl.BlockSpec((n, r, tc), lambda i, pos: (0, 0, i)),
                  pl.BlockSpec((1, r, tc), lambda i, pos: (pos[1], 0, pos[0] * steps + i))],
        out_specs=pl.BlockSpec((r, tc), lambda i, pos: (0, i)))
    return pl.pallas_call(body, name="sum_eight", grid_spec=grid_spec, out_shape=_out((r, hc), F32),
                          compiler_params=_cp(("parallel",), 32))(pos, *_pin(recv, grad4))


def _forward_copy(land_ref, send_sems, recv_sems, idx, j, receiving):
    x, y, c = _mesh_pos()
    px, py = _peer_chip(x, y, j)
    mine = _col_half(land_ref, c, lead=(2 * px + py,))
    theirs = _col_half(land_ref, 1 - c, lead=(2 * px + py,))
    return pltpu.make_async_remote_copy(
        src_ref=mine, dst_ref=theirs if receiving else mine, send_sem=send_sems.at[idx], recv_sem=recv_sems.at[idx],
        device_id=(x, y, 1 - c), device_id_type=MESH)


def _forward_start(lands, name):
    n = len(lands)

    def body(*refs):
        land_refs, send_sems, recv_sems, token = refs[:n], refs[n], refs[n + 1], refs[-1]
        for j in range(1, N_CHIPS):
            for a in range(n):
                _forward_copy(land_refs[a], send_sems, recv_sems, a * (N_CHIPS - 1) + j - 1, j, receiving=False).start()
        token[...] = jnp.zeros_like(token)

    sems = pltpu.SemaphoreType.DMA((n * (N_CHIPS - 1),))
    outs = pl.pallas_call(
        body, name=name,
        out_shape=(sems, sems, *[pltpu.HBM(l.shape, l.dtype) for l in lands], jax.ShapeDtypeStruct((8, 128), F32)),
        in_specs=[_HBM] * n, out_specs=[_SEM, _SEM] + [_HBM] * n + [pl.BlockSpec(memory_space=pltpu.VMEM)],
        input_output_aliases={i: 2 + i for i in range(n)},
        compiler_params=pltpu.CompilerParams(has_side_effects=_DATAFLOW),
    )(*lands)
    return outs[0], outs[1], outs[2:2 + n], outs[-1]


def _forward_wait(send_sems, recv_sems, lands_thru, after, name):
    n = len(lands_thru)

    def body(*refs):
        land_refs, send_sems, recv_sems = refs[:n], refs[n], refs[n + 1]
        for j in range(1, N_CHIPS):
            for a in range(n):
                cp = _forward_copy(land_refs[a], send_sems, recv_sems, a * (N_CHIPS - 1) + j - 1, j, receiving=True)
                cp.wait_send()
                cp.wait_recv()

    return pl.pallas_call(
        body, name=name,
        out_shape=tuple(pltpu.HBM(l.shape, l.dtype) for l in lands_thru),
        in_specs=[_HBM] * n + [_SEM, _SEM, _HBM], out_specs=[_HBM] * n,
        input_output_aliases={i: i for i in range(n)},
        compiler_params=pltpu.CompilerParams(has_side_effects=_DATAFLOW),
    )(*lands_thru, send_sems, recv_sems, *_pin(after))


def _gather_finish(lands, shards):
    n = len(lands)
    any_spec = _HBM

    def body(*refs):
        shard_refs, out_refs = refs[n:2 * n], refs[2 * n:3 * n]
        send_sems, recv_sems, local_sems = refs[3 * n:]
        x, y, c = _mesh_pos()
        chip = 2 * x + y
        local, sends = [], []
        for a in range(n):
            cp = pltpu.make_async_copy(shard_refs[a], out_refs[a].at[chip], local_sems.at[a])
            cp.start()
            local.append(cp)
        for j in range(1, N_CHIPS):
            px, py = _peer_chip(x, y, j)
            for a in range(n):
                landed = _col_half(out_refs[a], c, lead=(2 * px + py,))
                cp = pltpu.make_async_remote_copy(
                    src_ref=landed, dst_ref=landed, send_sem=send_sems.at[a, j], recv_sem=recv_sems.at[a, j],
                    device_id=(x, y, 1 - c), device_id_type=MESH)
                cp.start()
                sends.append(cp)
        for j in range(1, N_CHIPS):
            px, py = _peer_chip(x, y, j)
            for a in range(n):
                other = _col_half(out_refs[a], 1 - c, lead=(2 * px + py,))
                pltpu.make_async_remote_copy(
                    src_ref=other, dst_ref=other, send_sem=send_sems.at[a, j], recv_sem=recv_sems.at[a, j],
                    device_id=(x, y, 1 - c), device_id_type=MESH).wait_recv()
        for cp in sends:
            cp.wait_send()
        for cp in local:
            cp.wait()

    return pl.pallas_call(
        body, name="gather_finish",
        out_shape=[_out(l.shape, l.dtype) for l in lands],
        in_specs=[any_spec] * (2 * n), out_specs=[any_spec] * n,
        input_output_aliases={i: i for i in range(n)},
        scratch_shapes=[pltpu.SemaphoreType.DMA((n, N_CHIPS))] * 2 + [pltpu.SemaphoreType.DMA((n,))],
        compiler_params=_cp(vmem_mb=16),
    )(*lands, *shards)


def _adam_math(w, g, m, v):
    m2 = ADAM_B1 * m + (1.0 - ADAM_B1) * g
    v2 = ADAM_B2 * v + (1.0 - ADAM_B2) * (g * g)
    m_hat = m2 / (1.0 - ADAM_B1 ** ADAM_STEP)
    v_hat = v2 / (1.0 - ADAM_B2 ** ADAM_STEP)
    delta = -ADAM_LR * (m_hat / (jnp.sqrt(v_hat) + ADAM_EPS) + ADAM_WD * w)
    return delta, m2, v2


def _adam_big(g_own, g_pair, w, m, v, pos):
    r, c = w.shape
    per_half = 2
    tc = c // (2 * per_half)

    def body(pos_ref, go_ref, gp_ref, w_ref, m_ref, v_ref, g_ref, d_ref, m2_ref, v2_ref):
        half = pl.program_id(0) // per_half
        g = jnp.where(half == pos_ref[0], go_ref[...], gp_ref[0])
        d, m2, v2 = _adam_math(w_ref[...], g, m_ref[...], v_ref[...])
        g_ref[...] = g
        d_ref[...] = d
        m2_ref[...] = m2
        v2_ref[...] = v2

    spec = pl.BlockSpec((r, tc), lambda i, pos: (0, i))
    grid_spec = pltpu.PrefetchScalarGridSpec(
        num_scalar_prefetch=1, grid=(2 * per_half,),
        in_specs=[pl.BlockSpec((r, tc), lambda i, pos: (0, i % per_half)),
                  pl.BlockSpec((1, r, tc), lambda i, pos: (1 - pos[0], 0, i % per_half)), spec, spec, spec],
        out_specs=[spec] * 4)
    sh = _out((r, c), F32)
    return pl.pallas_call(body, name="adam_big", grid_spec=grid_spec, out_shape=[sh] * 4,
                          compiler_params=_cp(("parallel",), 32))(pos, *_pin(g_own, g_pair, w, m, v))


def _adam_rows(g_own, g_pair, w3, m3, v3, pos):
    r, _, c = w3.shape
    tr = 128

    def body(pos_ref, go_ref, gp_ref, w_ref, m_ref, v_ref, g_ref, d_ref, m2_ref, v2_ref):
        core = pos_ref[0]
        g = jnp.concatenate([jnp.where(core == 0, go_ref[...], gp_ref[0]), jnp.where(core == 1, go_ref[...], gp_ref[0])], 1)
        d, m2, v2 = _adam_math(w_ref[:, 0, :], g, m_ref[:, 0, :], v_ref[:, 0, :])
        g_ref[:, 0, :] = g
        d_ref[:, 0, :] = d
        m2_ref[:, 0, :] = m2
        v2_ref[:, 0, :] = v2

    spec = pl.BlockSpec((tr, 1, c), lambda i, pos: (i, 0, 0))
    grid_spec = pltpu.PrefetchScalarGridSpec(
        num_scalar_prefetch=1, grid=(pl.cdiv(r, tr),),
        in_specs=[pl.BlockSpec((tr, c // 2), lambda i, pos: (i, 0)),
                  pl.BlockSpec((1, tr, c // 2), lambda i, pos: (1 - pos[0], i, 0)), spec, spec, spec],
        out_specs=[spec] * 4)
    sh = _out(w3.shape, F32)
    return pl.pallas_call(body, name="adam_rows", grid_spec=grid_spec, out_shape=[sh] * 4,
                          compiler_params=_cp(("parallel",), 32))(pos, *_pin(g_own, g_pair, w3, m3, v3))


def _adam_ada(c_act_t, dmod_cols, w, m, v):
    r, c = w.shape
    tc = 512

    def body(ct_ref, dm_ref, w_ref, m_ref, v_ref, g_ref, d_ref, m2_ref, v2_ref):
        g = _nn(ct_ref[...], dm_ref[...].astype(BF))
        d, m2, v2 = _adam_math(w_ref[...], g, m_ref[...], v_ref[...])
        g_ref[...] = g
        d_ref[...] = d
        m2_ref[...] = m2
        v2_ref[...] = v2

    spec = pl.BlockSpec((r, tc), lambda i: (0, i))
    sh = _out((r, c), F32)
    return pl.pallas_call(
        body, name="adam_ada", grid=(c // tc,),
        in_specs=[pl.BlockSpec(c_act_t.shape, lambda i: (0, 0)), pl.BlockSpec((dmod_cols.shape[0], tc), lambda i: (0, i)),
                  spec, spec, spec],
        out_specs=[spec] * 4, out_shape=[sh] * 4, compiler_params=_cp(("parallel",), 48))(*_pin(c_act_t, dmod_cols, w, m, v))


def _ada_mod(c_all, w_shard, b_shard, token):
    nb, d = c_all.shape
    cols = w_shard.shape[1]
    tc = 512

    def body(c_ref, w_ref, b_ref, tok_ref, mod_ref, act_ref):
        cv = c_ref[...]
        act = cv * _sigmoid(cv)
        act_ref[...] = act
        mod_ref[...] = _nn(act.astype(BF), w_ref[...].astype(BF)) + b_ref[...]

    return pl.pallas_call(
        body, name="ada_mod", grid=(cols // tc,),
        in_specs=[pl.BlockSpec((nb, d), lambda i: (0, 0)), pl.BlockSpec((d, tc), lambda i: (0, i)),
                  pl.BlockSpec((1, tc), lambda i: (0, i)), _token_spec()],
        out_specs=[pl.BlockSpec((nb, tc), lambda i: (0, i)), pl.BlockSpec((nb, d), lambda i: (0, 0))],
        out_shape=[_out((nb, cols), F32), _out((nb, d), F32)],
        compiler_params=_cp(("arbitrary",), 32))(*_pin(c_all, w_shard, b_shard, token))


SUB_ROWS = 256
ROW_TILE = 512


def _sub_rows(tm):
    return [slice(s, s + SUB_ROWS) for s in range(0, tm, SUB_ROWS)] if tm > SUB_ROWS else [slice(0, tm)]


_RESIDENT = pl.BlockSpec(memory_space=pltpu.VMEM)


def _token_spec():
    return pl.BlockSpec((8, 128), lambda *_: (0, 0))


def _in_proj(x, mod3, g_mix, w_in_t, seq, token):
    t, d = x.shape
    tm = min(ROW_TILE, seq)
    tps = seq // tm

    def body(x_ref, mod_ref, g_ref, w_ref, tok_ref, proj_ref, u1_ref):
        for rows in _sub_rows(tm):
            xv = x_ref[rows, :]
            r = lax.rsqrt(jnp.mean(xv * xv, -1, keepdims=True) + EPS)
            u = (xv * r * g_ref[...]) * (1.0 + mod_ref[0, 1:2, :]) + mod_ref[0, 0:1, :]
            ub = u.astype(BF)
            u1_ref[rows, :] = ub
            proj_ref[rows, :] = _nt(ub, w_ref[...])

    return pl.pallas_call(
        body, name="in_proj", grid=(t // tm,),
        in_specs=[pl.BlockSpec((tm, d), lambda i: (i, 0)), pl.BlockSpec((1, N_MOD, d), lambda i: (i // tps, 0, 0)),
                  pl.BlockSpec((1, d), lambda i: (0, 0)), _RESIDENT, _token_spec()],
        out_specs=[pl.BlockSpec((tm, IN_PAD), lambda i: (i, 0)), pl.BlockSpec((tm, d), lambda i: (i, 0))],
        out_shape=[_out((t, IN_PAD), F32), _out((t, d), BF)],
        compiler_params=_cp(("parallel",), 40))(*_pin(x, mod3, g_mix), w_in_t, *_pin(token))


def _pool_tile(seq):
    return min(512, seq)


def _pool_fwd(proj, w_pool, pool_scale, nb, seq, token):
    ts = _pool_tile(seq)
    nt = seq // ts

    def body(u_ref, halo_ref, wp_ref, ps_ref, tok_ref, yp_ref, p_ref):
        i = pl.program_id(1)
        halo = jnp.where(i == 0, 0.0, halo_ref[...])
        u = u_ref[...]
        ext = jnp.concatenate([halo, u], 0)
        tpos = i * ts + _iota((ts, 1), 0)
        for g, w in enumerate(POOL_WINDOWS):
            gs = slice(g * POOL_GROUP, (g + 1) * POOL_GROUP)
            s = ext[:, gs]
            sh = 1
            while sh < w:
                s = s + pltpu.roll(s, sh, 0)
                sh *= 2
            cnt = jnp.minimum(tpos + 1, w).astype(F32)
            pb = (s[HALO:] / cnt - u[:, gs]).astype(BF)
            p_ref[:, gs] = pb
            yp_ref[:, gs] = (_nn(pb, wp_ref[g].astype(BF)) * ps_ref[:, gs]).astype(BF)

    hb = ts // HALO
    return pl.pallas_call(
        body, name="pool_fwd", grid=(nb, nt),
        in_specs=[pl.BlockSpec((ts, POOL_WIDTH), lambda b, i: (b * nt + i, 0)),
                  pl.BlockSpec((HALO, POOL_WIDTH), lambda b, i: (jnp.maximum((b * nt + i) * hb - 1, 0), 0)),
                  pl.BlockSpec((4, POOL_GROUP, POOL_GROUP), lambda b, i: (0, 0, 0)),
                  pl.BlockSpec((1, POOL_WIDTH), lambda b, i: (0, 0)), _token_spec()],
        out_specs=[pl.BlockSpec((ts, POOL_WIDTH), lambda b, i: (b * nt + i, 0))] * 2,
        out_shape=[_out((nb * seq, POOL_WIDTH), BF)] * 2,
        compiler_params=_cp(("parallel", "parallel"), 32))(*_pin(proj, proj, w_pool, pool_scale, token))


def _conv_pre(uxbc, halo, cw, cb, first):
    halo = jnp.where(first, 0.0, halo)
    ext = jnp.concatenate([halo, uxbc], 0)
    pre = cb + uxbc * cw[3:4]
    for k in (2, 1, 0):
        pre = pre + pltpu.roll(ext, 3 - k, 0)[CONV_HALO:] * cw[k:k + 1]
    return pre


def _chunk_terms(pre, udt, dtb, alog):
    sg = _sigmoid(pre)
    xbc = pre * sg
    dtp = udt[:, :SSD_HEADS] + dtb
    dt = jnp.maximum(dtp, 0.0) + jnp.log(1.0 + jnp.exp(-jnp.abs(dtp)))
    a = -jnp.exp(alog)
    da = dt * a
    tril = (_iota((CHUNK, CHUNK), 0) >= _iota((CHUNK, CHUNK), 1))
    acum = _exact_nn_left(tril.astype(BF), da)
    eye = (_iota((SSD_HEADS, SSD_HEADS), 0) == _iota((SSD_HEADS, SSD_HEADS), 1)).astype(BF)
    acum_t = _exact_nt_left(eye, acum)
    expand = _head_expand_matrix(SSD_HEADS, SSD_INNER)
    acum_e = _exact_nn(acum, expand)
    dt_e = _exact_nn(dt, expand)
    last_e = acum_e[CHUNK - 1:CHUNK]
    return dict(pre=pre, sg=sg, xbc=xbc, dtp=dtp, dt=dt, a=a, acum=acum, acum_t=acum_t, tril=tril,
                dt_e=dt_e, e_a=jnp.exp(acum_e), d_out=jnp.exp(last_e - acum_e), c_dec=jnp.exp(last_e))


def _head_decay(r, h):
    seg = r["acum"][:, h:h + 1] - r["acum_t"][h:h + 1, :]
    return jnp.where(r["tril"], jnp.exp(jnp.minimum(seg, 0.0)), 0.0)


SSD_SUB = 4
SSD_ROWS = SSD_SUB * CHUNK


def _ssd_specs(nb, seq, reverse):
    ns = seq // SSD_ROWS
    per = seq // CONV_HALO

    def cidx(c):
        return (ns - 1 - c) if reverse else c

    def row(b, c):
        return b * ns + cidx(c)

    specs = [
        pl.BlockSpec((SSD_ROWS, CONV_CH), lambda b, c: (row(b, c), 1)),
        pl.BlockSpec((CONV_HALO, CONV_CH),
                     lambda b, c: (jnp.maximum(b * per + cidx(c) * (SSD_ROWS // CONV_HALO) - 1, 0), 1)),
        pl.BlockSpec((SSD_ROWS, GROUP_W), lambda b, c: (row(b, c), 1)),
        pl.BlockSpec((SSD_ROWS, GROUP_W), lambda b, c: (row(b, c), 2)),
        pl.BlockSpec((SSD_ROWS, 128), lambda b, c: (row(b, c), OFF_DT // 128)),
    ]
    return specs, row, cidx, ns


def _const_spec(shape):
    return pl.BlockSpec(shape, lambda b, c: (0,) * len(shape))


def _ssd_fwd(proj, conv_w, conv_b, dt_bias, a_log, dskip_e, g_ssd, nb, seq):
    specs, row, cidx, ns = _ssd_specs(nb, seq, reverse=False)

    def body(uxbc_ref, halo_ref, z0_ref, z1_ref, udt_ref, cw_ref, cb_ref, dtb_ref, alog_ref, dsk_ref, gs_ref,
             yssd_ref, yssm_ref, hprev_ref, pre_ref, h_ref, yd_ref):
        c = pl.program_id(1)

        @pl.when(c == 0)
        def _():
            h_ref[...] = jnp.zeros_like(h_ref)

        for sub in range(SSD_SUB):
            rows = slice(sub * CHUNK, (sub + 1) * CHUNK)
            if sub == 0:
                halo, first = halo_ref[...], c == 0
            else:
                halo, first = uxbc_ref[sub * CHUNK - CONV_HALO:sub * CHUNK, :], False
            pre = _conv_pre(uxbc_ref[rows, :], halo, cw_ref[...], cb_ref[...], first)
            pre_ref[rows, :] = pre
            r = _chunk_terms(pre, udt_ref[rows, :], dtb_ref[...], alog_ref[...])
            xbc = r["xbc"]
            xs = xbc[:, :SSD_INNER]
            xdt = xs * r["dt_e"]
            xdt_b = xdt.astype(BF)
            xdo_b = (xdt * r["d_out"]).astype(BF)
            hprev_ref[0, sub] = h_ref[...]
            for g in range(2):
                gs = slice(g * GROUP_W, (g + 1) * GROUP_W)
                bg = xbc[:, SSD_INNER + g * SSD_STATE:SSD_INNER + (g + 1) * SSD_STATE].astype(BF)
                cg = xbc[:, SSD_INNER + (2 + g) * SSD_STATE:SSD_INNER + (3 + g) * SSD_STATE].astype(BF)
                scores = _nt(cg, bg)
                hg = h_ref[g]
                y_off = _nn(cg, hg.astype(BF)) * r["e_a"][:, gs]
                for hh in range(8):
                    h = g * 8 + hh
                    hs = slice(h * SSD_HEAD_DIM, (h + 1) * SSD_HEAD_DIM)
                    m = (scores * _head_decay(r, h)).astype(BF)
                    yd_ref[sub, :, hs] = _nn(m, xdt_b[:, hs])
                h_ref[g] = hg * r["c_dec"][:, gs] + _tn(bg, xdo_b[:, gs])
                y = yd_ref[sub, :, gs] + y_off + dsk_ref[:, gs] * xs[:, gs]
                yssm_ref[rows, gs] = y
                zg = (z0_ref if g == 0 else z1_ref)[rows, :]
                yg = y * (zg * _sigmoid(zg))
                rg = lax.rsqrt(jnp.mean(yg * yg, -1, keepdims=True) + EPS)
                yssd_ref[rows, gs] = (yg * rg * gs_ref[:, gs]).astype(BF)

    t = nb * seq
    return pl.pallas_call(
        body, name="ssd_fwd", grid=(nb, ns),
        in_specs=specs + [_const_spec((4, CONV_CH)), _const_spec((1, CONV_CH)), _const_spec((1, SSD_HEADS)),
                          _const_spec((1, SSD_HEADS)), _const_spec((1, SSD_INNER)), _const_spec((1, SSD_INNER))],
        out_specs=[pl.BlockSpec((SSD_ROWS, SSD_INNER), lambda b, c: (row(b, c), 0)),
                   pl.BlockSpec((SSD_ROWS, SSD_INNER), lambda b, c: (row(b, c), 0)),
                   pl.BlockSpec((1, SSD_SUB, 2, SSD_STATE, GROUP_W), lambda b, c: (b, c, 0, 0, 0)),
                   pl.BlockSpec((SSD_ROWS, CONV_CH), lambda b, c: (row(b, c), 0))],
        out_shape=[_out((t, SSD_INNER), BF), _out((t, SSD_INNER), F32),
                   _out((nb, seq // CHUNK, 2, SSD_STATE, GROUP_W), F32), _out((t, CONV_CH), F32)],
        scratch_shapes=[pltpu.VMEM((2, SSD_STATE, GROUP_W), F32), pltpu.VMEM((SSD_SUB, CHUNK, SSD_INNER), F32)],
        compiler_params=_cp(("arbitrary", "arbitrary"), 56),
    )(*_pin(proj, proj, proj, proj, proj, conv_w, conv_b, dt_bias, a_log, dskip_e, g_ssd))


def _out_proj(y_pool, y_ssd, w_out, x, mod3, g_mlp, seq):
    t, d = x.shape
    tm = 512
    tps = seq // tm if seq >= tm else 1
    tm = min(tm, seq)

    def body(yp_ref, ys_ref, w_ref, x_ref, mod_ref, g_ref, h1_ref, o_ref, u2_ref):
        for rows in _sub_rows(tm):
            o = _nn(jnp.concatenate([yp_ref[rows, :], ys_ref[rows, :]], 1), w_ref[...])
            o_ref[rows, :] = o.astype(BF)
            h1 = x_ref[rows, :] + mod_ref[0, 2:3, :] * o
            h1_ref[rows, :] = h1
            r = lax.rsqrt(jnp.mean(h1 * h1, -1, keepdims=True) + EPS)
            u2_ref[rows, :] = ((h1 * r * g_ref[...]) * (1.0 + mod_ref[0, 4:5, :]) + mod_ref[0, 3:4, :]).astype(BF)

    row = lambda i: (i, 0)
    return pl.pallas_call(
        body, name="out_proj", grid=(t // tm,),
        in_specs=[pl.BlockSpec((tm, POOL_WIDTH), row), pl.BlockSpec((tm, SSD_INNER), row),
                  _RESIDENT, pl.BlockSpec((tm, d), row),
                  pl.BlockSpec((1, N_MOD, d), lambda i: (i // tps, 0, 0)), pl.BlockSpec((1, d), lambda i: (0, 0))],
        out_specs=[pl.BlockSpec((tm, d), row)] * 3,
        out_shape=[_out((t, d), F32), _out((t, d), BF), _out((t, d), BF)],
        compiler_params=_cp(("parallel",), 48))(*_pin(y_pool, y_ssd), w_out, *_pin(x, mod3, g_mlp))


def _mlp_up(u2, w_up4):
    t, d = u2.shape
    tm = min(1024, t)
    nk, _, cols = w_up4.shape

    def body(u_ref, w_ref, a_ref):
        a_ref[...] = _nn(u_ref[...], w_ref[pl.program_id(1)]).astype(BF)

    return pl.pallas_call(
        body, name="mlp_up", grid=(t // tm, nk),
        in_specs=[pl.BlockSpec((tm, d), lambda i, k: (i, 0)), _RESIDENT],
        out_specs=pl.BlockSpec((tm, cols), lambda i, k: (i, k)),
        out_shape=_out((t, nk * cols), BF),
        compiler_params=_cp(("parallel", "parallel"), 32))(*_pin(u2), w_up4)


def _mlp_down_loss(a_up, w_down, h1, mod3, g_final, target, seq):
    t, d = h1.shape
    nb = t // seq
    tm = min(ROW_TILE, seq)
    tps = seq // tm

    def body(a_ref, w_ref, h1_ref, mod_ref, g_ref, tg_ref, ddn_ref, dh2_ref, sq_ref, gg_ref, dgf_ref):
        i = pl.program_id(0)

        @pl.when(i == 0)
        def _():
            sq_ref[...] = jnp.zeros_like(sq_ref)
            gg_ref[...] = jnp.zeros_like(gg_ref)

        @pl.when(i % tps == 0)
        def _():
            dgf_ref[...] = jnp.zeros_like(dgf_ref)

        gate = mod_ref[0, 5:6, :]
        sq = gg = dgf = 0.0
        for rows in _sub_rows(tm):
            f = jnp.square(jnp.maximum(a_ref[rows, :], 0))
            dn = _nn(f, w_ref[...])
            h2 = h1_ref[rows, :] + gate * dn
            r = lax.rsqrt(jnp.mean(h2 * h2, -1, keepdims=True) + EPS)
            hh = h2 * r
            err = hh * g_ref[...] - tg_ref[rows, :]
            dy = err * (1.0 / d)
            dhat = dy * g_ref[...]
            dh2 = r * (dhat - hh * jnp.mean(dhat * hh, -1, keepdims=True))
            dh2_ref[rows, :] = dh2
            ddn_ref[rows, :] = (dh2 * gate).astype(BF)
            sq = sq + jnp.sum(err * err, 0, keepdims=True)
            gg = gg + jnp.sum(dy * hh, 0, keepdims=True)
            dgf = dgf + jnp.sum(dh2 * dn, 0, keepdims=True)
        sq_ref[...] += sq
        gg_ref[...] += gg
        dgf_ref[0] += dgf

    row = lambda i: (i, 0)
    vec = pl.BlockSpec((1, d), lambda i: (0, 0))
    return pl.pallas_call(
        body, name="mlp_down_loss", grid=(t // tm,),
        in_specs=[pl.BlockSpec((tm, D_FF), row), _RESIDENT, pl.BlockSpec((tm, d), row),
                  pl.BlockSpec((1, N_MOD, d), lambda i: (i // tps, 0, 0)), vec, pl.BlockSpec((tm, d), row)],
        out_specs=[pl.BlockSpec((tm, d), row), pl.BlockSpec((tm, d), row), vec, vec,
                   pl.BlockSpec((1, 1, d), lambda i: (i // tps, 0, 0))],
        out_shape=[_out((t, d), BF), _out((t, d), F32), _out((1, d), F32),
                   _out((1, d), F32), _out((nb, 1, d), F32)],
        compiler_params=_cp(("arbitrary",), 44))(*_pin(a_up), w_down, *_pin(h1, mod3, g_final, target))


def _tn_matmul(a, b, tk, tn, name, square_relu=False, out3=False):
    t, kdim = a.shape
    ndim = b.shape[1]

    def body(a_ref, b_ref, o_ref):
        av = a_ref[...]
        if square_relu:
            av = jnp.square(jnp.maximum(av, 0))
        res = _tn(av, b_ref[...]).astype(BF)
        if out3:
            o_ref[0] = res
        else:
            o_ref[...] = res

    if out3:
        out_spec = pl.BlockSpec((1, tk, tn), lambda j, i: (j, i, 0))
        out_shape = _out((ndim // tn, kdim, tn), BF)
    else:
        out_spec = pl.BlockSpec((tk, tn), lambda j, i: (i, j))
        out_shape = _out((kdim, ndim), BF)
    return pl.pallas_call(
        body, name=name, grid=(ndim // tn, kdim // tk),
        in_specs=[pl.BlockSpec((t, tk), lambda j, i: (0, i)), pl.BlockSpec((t, tn), lambda j, i: (0, j))],
        out_specs=out_spec, out_shape=out_shape,
        compiler_params=_cp(("parallel", "parallel"), 56))(*_pin(a, b))


def _mlp_down_bwd(d_dn, w_down4, a_up, token):
    t, d = d_dn.shape
    tm = min(1024, t)
    nk, rows, _ = w_down4.shape

    def body(g_ref, w_ref, a_ref, tok_ref, o_ref):
        df = _nt(g_ref[...], w_ref[pl.program_id(1)])
        o_ref[...] = (df * (2.0 * jnp.maximum(a_ref[...], 0).astype(F32))).astype(BF)

    return pl.pallas_call(
        body, name="mlp_down_bwd", grid=(t // tm, nk),
        in_specs=[pl.BlockSpec((tm, d), lambda i, k: (i, 0)), _RESIDENT,
                  pl.BlockSpec((tm, rows), lambda i, k: (i, k)), _token_spec()],
        out_specs=pl.BlockSpec((tm, rows), lambda i, k: (i, k)),
        out_shape=_out((t, nk * rows), BF),
        compiler_params=_cp(("parallel", "parallel"), 32))(*_pin(d_dn), w_down4, *_pin(a_up, token))


def _mlp_up_bwd(d_a, w_up4, h1, dh2, o, mod3, g_mlp, seq, token):
    t, d = h1.shape
    nb = t // seq
    tm = min(ROW_TILE, seq)
    tps = seq // tm
    nk = w_up4.shape[0]
    cols = w_up4.shape[2]

    def body(da_ref, w_ref, h1_ref, dh2_ref, o_ref, mod_ref, g_ref, tok_ref, dh1_ref, do_ref, acc_ref, gg_ref):
        i = pl.program_id(0)

        @pl.when(i == 0)
        def _():
            gg_ref[...] = jnp.zeros_like(gg_ref)

        @pl.when(i % tps == 0)
        def _():
            acc_ref[...] = jnp.zeros_like(acc_ref)

        gg = a_shift = a_scale = a_gate = 0.0
        for rows in _sub_rows(tm):
            du = _nt(da_ref[rows, 0:cols], w_ref[0])
            for k in range(1, nk):
                du = du + _nt(da_ref[rows, k * cols:(k + 1) * cols], w_ref[k])
            h1 = h1_ref[rows, :]
            r = lax.rsqrt(jnp.mean(h1 * h1, -1, keepdims=True) + EPS)
            hh = h1 * r
            n2 = hh * g_ref[...]
            dn2 = du * (1.0 + mod_ref[0, 4:5, :])
            dhat = dn2 * g_ref[...]
            dh1 = dh2_ref[rows, :] + r * (dhat - hh * jnp.mean(dhat * hh, -1, keepdims=True))
            dh1_ref[rows, :] = dh1
            do_ref[rows, :] = (dh1 * mod_ref[0, 2:3, :]).astype(BF)
            gg = gg + jnp.sum(dn2 * hh, 0, keepdims=True)
            a_shift = a_shift + jnp.sum(du, 0, keepdims=True)
            a_scale = a_scale + jnp.sum(du * n2, 0, keepdims=True)
            a_gate = a_gate + jnp.sum(dh1 * o_ref[rows, :].astype(F32), 0, keepdims=True)
        gg_ref[...] += gg
        acc_ref[0, 0:1, :] += a_shift
        acc_ref[0, 1:2, :] += a_scale
        acc_ref[0, 2:3, :] += a_gate

    row = lambda i: (i, 0)
    vec = pl.BlockSpec((1, d), lambda i: (0, 0))
    return pl.pallas_call(
        body, name="mlp_up_bwd", grid=(t // tm,),
        in_specs=[pl.BlockSpec((tm, D_FF), row), _RESIDENT, pl.BlockSpec((tm, d), row),
                  pl.BlockSpec((tm, d), row), pl.BlockSpec((tm, d), row),
                  pl.BlockSpec((1, N_MOD, d), lambda i: (i // tps, 0, 0)), vec, _token_spec()],
        out_specs=[pl.BlockSpec((tm, d), row), pl.BlockSpec((tm, d), row),
                   pl.BlockSpec((1, 8, d), lambda i: (i // tps, 0, 0)), vec],
        out_shape=[_out((t, d), F32), _out((t, d), BF),
                   _out((nb, 8, d), F32), _out((1, d), F32)],
        compiler_params=_cp(("arbitrary",), 44))(*_pin(d_a), w_up4, *_pin(h1, dh2, o, mod3, g_mlp, token))


def _out_proj_bwd(d_o, w_out, token):
    t, d = d_o.shape
    tm = min(512, t)

    def body(g_ref, w_ref, tok_ref, dp_ref, ds_ref):
        gv = g_ref[...]
        dp_ref[...] = _nt(gv, w_ref[0:POOL_WIDTH, :])
        ds_ref[...] = _nt(gv, w_ref[POOL_WIDTH:, :])

    row = lambda i: (i, 0)
    return pl.pallas_call(
        body, name="out_proj_bwd", grid=(t // tm,),
        in_specs=[pl.BlockSpec((tm, d), row), _RESIDENT, _token_spec()],
        out_specs=[pl.BlockSpec((tm, POOL_WIDTH), row), pl.BlockSpec((tm, SSD_INNER), row)],
        out_shape=[_out((t, POOL_WIDTH), F32), _out((t, SSD_INNER), F32)],
        compiler_params=_cp(("parallel",), 32))(*_pin(d_o), w_out, *_pin(token))


def _pool_bwd(d_ypool, p, w_pool, pool_scale, nb, seq):
    ts = _pool_tile(seq)
    nt = seq // ts
    hb = ts // HALO
    last_block = nb * seq // HALO - 1

    def body(dy_ref, halo_ref, p_ref, wp_ref, ps_ref, du_ref, gw_ref, gs_ref):
        b = pl.program_id(0)
        i = pl.program_id(1)

        @pl.when((b == 0) & (i == 0))
        def _():
            gw_ref[...] = jnp.zeros_like(gw_ref)
            gs_ref[...] = jnp.zeros_like(gs_ref)

        halo = jnp.where(i == nt - 1, 0.0, halo_ref[...])
        dy = dy_ref[...]
        ext = jnp.concatenate([dy, halo], 0)
        tpos = i * ts + _iota((ts + HALO, 1), 0)
        n_ext = ts + HALO
        for g, w in enumerate(POOL_WINDOWS):
            gs = slice(g * POOL_GROUP, (g + 1) * POOL_GROUP)
            wg = wp_ref[g].astype(BF)
            pg = p_ref[:, gs]
            pw = _nn(pg, wg)
            gs_ref[:, gs] += jnp.sum(dy[:, gs] * pw, 0, keepdims=True)
            dpw = (ext[:, gs] * ps_ref[:, gs]).astype(BF)
            gw_ref[g] += _tn(pg, dpw[:ts])
            dp = _nt(dpw, wg)
            cnt = jnp.minimum(tpos + 1, w).astype(F32)
            s = dp / cnt
            sh = 1
            while sh < w:
                s = s + pltpu.roll(s, n_ext - sh, 0)
                sh *= 2
            du_ref[:, gs] = (s[:ts] - dp[:ts]).astype(BF)

    return pl.pallas_call(
        body, name="pool_bwd", grid=(nb, nt),
        in_specs=[pl.BlockSpec((ts, POOL_WIDTH), lambda b, i: (b * nt + i, 0)),
                  pl.BlockSpec((HALO, POOL_WIDTH), lambda b, i: (jnp.minimum((b * nt + i + 1) * hb, last_block), 0)),
                  pl.BlockSpec((ts, POOL_WIDTH), lambda b, i: (b * nt + i, 0)),
                  pl.BlockSpec((4, POOL_GROUP, POOL_GROUP), lambda b, i: (0, 0, 0)),
                  pl.BlockSpec((1, POOL_WIDTH), lambda b, i: (0, 0))],
        out_specs=[pl.BlockSpec((ts, POOL_WIDTH), lambda b, i: (b * nt + i, 0)),
                   pl.BlockSpec((4, POOL_GROUP, POOL_GROUP), lambda b, i: (0, 0, 0)),
                   pl.BlockSpec((1, POOL_WIDTH), lambda b, i: (0, 0))],
        out_shape=[_out((nb * seq, POOL_WIDTH), BF), _out((4, POOL_GROUP, POOL_GROUP), F32),
                   _out((1, POOL_WIDTH), F32)],
        compiler_params=_cp(("arbitrary", "arbitrary"), 32))(*_pin(d_ypool, d_ypool, p, w_pool, pool_scale))


def _ssd_bwd(proj, pre, d_yssd, yssm, h_prev, dt_bias, a_log, dskip_e, g_ssd, nb, seq):
    specs, row, cidx, ns = _ssd_specs(nb, seq, reverse=True)
    specs = specs[2:]

    def body(z0_ref, z1_ref, udt_ref, pre_ref, dys_ref, yssm_ref, hprev_ref,
             dtb_ref, alog_ref, dsk_ref, gs_ref,
             dz_ref, dpre_ref, dudt_ref, ggs_ref, gdsk_ref, ga_ref, gdtb_ref,
             g_ref, dxdt_ref, dyv_ref):
        b = pl.program_id(0)
        c = pl.program_id(1)

        @pl.when(c == 0)
        def _():
            g_ref[...] = jnp.zeros_like(g_ref)

        @pl.when((b == 0) & (c == 0))
        def _():
            ggs_ref[...] = jnp.zeros_like(ggs_ref)
            gdsk_ref[...] = jnp.zeros_like(gdsk_ref)
            ga_ref[...] = jnp.zeros_like(ga_ref)
            gdtb_ref[...] = jnp.zeros_like(gdtb_ref)

        for sub in reversed(range(SSD_SUB)):
            chunk(sub, z0_ref, z1_ref, udt_ref, pre_ref, dys_ref, yssm_ref, hprev_ref, dtb_ref, alog_ref, dsk_ref, gs_ref,
                  dz_ref, dpre_ref, dudt_ref, ggs_ref, gdsk_ref, ga_ref, gdtb_ref, g_ref, dxdt_ref.at[sub], dyv_ref.at[sub])

    def chunk(sub, z0_ref, z1_ref, udt_ref, pre_ref, dys_ref, yssm_ref, hprev_ref,
              dtb_ref, alog_ref, dsk_ref, gs_ref,
              dz_ref, dpre_ref, dudt_ref, ggs_ref, gdsk_ref, ga_ref, gdtb_ref,
              g_ref, dxdt_ref, dyv_ref):
        rows = slice(sub * CHUNK, (sub + 1) * CHUNK)
        r = _chunk_terms(pre_ref[rows, :], udt_ref[rows, :], dtb_ref[...], alog_ref[...])
        xbc = r["xbc"]
        xs = xbc[:, :SSD_INNER]
        dt_e = r["dt_e"]
        xdt = xs * dt_e
        xdt_b = xdt.astype(BF)
        reduce_m = _head_reduce_matrix(GROUP_W, 8)

        def head_sums(v):
            return _nn(v.astype(BF), reduce_m)

        onehot16 = lambda h: (_iota((1, SSD_HEADS), 1) == h).astype(F32)
        onecol16 = lambda h: (_iota((SSD_HEADS, 1), 0) == h).astype(F32)

        d_acum = jnp.zeros((CHUNK, SSD_HEADS), F32)
        d_acum_t = jnp.zeros((SSD_HEADS, CHUNK), F32)
        d_alast = jnp.zeros((1, SSD_HEADS), F32)
        place8 = lambda g: (_iota((8, SSD_HEADS), 1) == _iota((8, SSD_HEADS), 0) + 8 * g).astype(BF)
        d_b, d_c = [], []
        for g in range(2):
            gs = slice(g * GROUP_W, (g + 1) * GROUP_W)
            zg = (z0_ref if g == 0 else z1_ref)[rows, :]
            sz = _sigmoid(zg)
            silu_z = zg * sz
            ys = yssm_ref[rows, gs]
            yg = ys * silu_z
            rg = lax.rsqrt(jnp.mean(yg * yg, -1, keepdims=True) + EPS)
            yh = yg * rg
            dys = dys_ref[rows, gs]
            ggs_ref[:, gs] += jnp.sum(dys * yh, 0, keepdims=True)
            dyh = dys * gs_ref[:, gs]
            dyg = rg * (dyh - yh * jnp.mean(dyh * yh, -1, keepdims=True))
            dy = dyg * silu_z
            dz_ref[rows, gs] = (dyg * ys * (sz * (1.0 + zg * (1.0 - sz)))).astype(BF)
            gdsk_ref[:, gs] += jnp.sum(dy * xs[:, gs], 0, keepdims=True)
            dyv_ref[:, gs] = dy
            dy_b = dy.astype(BF)

            bg = xbc[:, SSD_INNER + g * SSD_STATE:SSD_INNER + (g + 1) * SSD_STATE].astype(BF)
            cg = xbc[:, SSD_INNER + (2 + g) * SSD_STATE:SSD_INNER + (3 + g) * SSD_STATE].astype(BF)
            scores = _nt(cg, bg)
            hg = hprev_ref[0, sub, g]
            hg_b = hg.astype(BF)
            gg = g_ref[g]
            gg_b = gg.astype(BF)
            e_a = r["e_a"][:, gs]
            d_out = r["d_out"][:, gs]
            c_dec = r["c_dec"][:, gs]
            zc = _nn(cg, hg_b)
            wv = e_a * dy
            wv_b = wv.astype(BF)
            da_g = head_sums(wv * zc)
            dcg = _nt(wv_b, hg_b)
            d_hprev = _tn(cg, wv_b)
            vg = _nn(bg, gg_b)
            dxdt_g = d_out * vg
            dd_out = head_sums(xdt[:, gs] * vg)
            dbg = _nt((xdt[:, gs] * d_out).astype(BF), gg_b)
            dcd = _exact_nn(jnp.sum(gg * hg, 0, keepdims=True), reduce_m)
            d_out8 = jnp.exp(r["acum"][CHUNK - 1:CHUNK, 8 * g:8 * g + 8] - r["acum"][:, 8 * g:8 * g + 8])
            c_dec8 = jnp.exp(r["acum"][CHUNK - 1:CHUNK, 8 * g:8 * g + 8])
            t8 = dd_out * d_out8
            d_alast = d_alast + _exact_nn(jnp.sum(t8, 0, keepdims=True) + dcd * c_dec8, place8(g))
            d_acum = d_acum + _exact_nn(da_g - t8, place8(g))
            dsc = jnp.zeros((CHUNK, CHUNK), F32)
            for hh in range(8):
                h = g * 8 + hh
                hs = slice(h * SSD_HEAD_DIM, (h + 1) * SSD_HEAD_DIM)
                lam = _head_decay(r, h)
                m = scores * lam
                dyh_b = dy_b[:, hh * SSD_HEAD_DIM:(hh + 1) * SSD_HEAD_DIM]
                dm = _nt(dyh_b, xdt_b[:, hs])
                tm_ = dm * m
                d_acum = d_acum + jnp.sum(tm_, 1, keepdims=True) * onehot16(h)
                d_acum_t = d_acum_t + onecol16(h) * jnp.sum(tm_, 0, keepdims=True)
                dsc = dsc + dm * lam
                dxdt_ref[:, hs] = _tn(m.astype(BF), dyh_b) + dxdt_g[:, hh * SSD_HEAD_DIM:(hh + 1) * SSD_HEAD_DIM]
            dsc_b = dsc.astype(BF)
            d_c.append(dcg + _nn(dsc_b, bg))
            d_b.append(dbg + _tn(dsc_b, cg))
            g_ref[g] = d_hprev + c_dec * gg

        eye = (_iota((CHUNK, CHUNK), 0) == _iota((CHUNK, CHUNK), 1)).astype(BF)
        d_acum = d_acum - _exact_nt_left(eye, d_acum_t)
        is_last = (_iota((CHUNK, 1), 0) == CHUNK - 1).astype(F32)
        d_acum = d_acum + is_last * d_alast
        triu = (_iota((CHUNK, CHUNK), 0) <= _iota((CHUNK, CHUNK), 1)).astype(BF)
        d_da = _exact_nn_left(triu, d_acum)
        dt = r["dt"]
        ga_ref[...] += jnp.sum(d_da * dt, 0, keepdims=True)
        dxdt = dxdt_ref[...]
        reduce16 = _head_reduce_matrix(SSD_INNER, SSD_HEADS)
        d_dt = d_da * r["a"] + _nn((dxdt * xs).astype(BF), reduce16)
        d_udt = d_dt * _sigmoid(r["dtp"])
        gdtb_ref[...] += jnp.sum(d_udt, 0, keepdims=True)
        dudt_ref[rows, :] = jnp.zeros((CHUNK, dudt_ref.shape[1]), BF)
        dudt_ref[rows, 0:SSD_HEADS] = d_udt.astype(BF)
        pre, sg = r["pre"], r["sg"]
        dsilu = sg * (1.0 + pre * (1.0 - sg))
        dpre_ref[rows, 0:SSD_INNER] = (dsk_ref[...] * dyv_ref[...] + dxdt * dt_e) * dsilu[:, 0:SSD_INNER]
        for g in range(2):
            bs = slice(SSD_INNER + g * SSD_STATE, SSD_INNER + (g + 1) * SSD_STATE)
            cs = slice(SSD_INNER + (2 + g) * SSD_STATE, SSD_INNER + (3 + g) * SSD_STATE)
            dpre_ref[rows, bs] = d_b[g] * dsilu[:, bs]
            dpre_ref[rows, cs] = d_c[g] * dsilu[:, cs]

    t = nb * seq
    vec = _const_spec((1, SSD_INNER))
    small = _const_spec((1, SSD_HEADS))
    return pl.pallas_call(
        body, name="ssd_bwd", grid=(nb, ns),
        in_specs=specs + [pl.BlockSpec((SSD_ROWS, CONV_CH), lambda b, c: (row(b, c), 0)),
                          pl.BlockSpec((SSD_ROWS, SSD_INNER), lambda b, c: (row(b, c), 0)),
                          pl.BlockSpec((SSD_ROWS, SSD_INNER), lambda b, c: (row(b, c), 0)),
                          pl.BlockSpec((1, SSD_SUB, 2, SSD_STATE, GROUP_W), lambda b, c: (b, cidx(c), 0, 0, 0)),
                          small, small, vec, vec],
        out_specs=[pl.BlockSpec((SSD_ROWS, SSD_INNER), lambda b, c: (row(b, c), 0)),
                   pl.BlockSpec((SSD_ROWS, CONV_CH), lambda b, c: (row(b, c), 0)),
                   pl.BlockSpec((SSD_ROWS, 128), lambda b, c: (row(b, c), 0)),
                   vec, vec, small, small],
        out_shape=[_out((t, SSD_INNER), BF), _out((t, CONV_CH), F32),
                   _out((t, 128), BF), _out((1, SSD_INNER), F32),
                   _out((1, SSD_INNER), F32), _out((1, SSD_HEADS), F32),
                   _out((1, SSD_HEADS), F32)],
        scratch_shapes=[pltpu.VMEM((2, SSD_STATE, GROUP_W), F32), pltpu.VMEM((SSD_SUB, CHUNK, SSD_INNER), F32),
                        pltpu.VMEM((SSD_SUB, CHUNK, SSD_INNER), F32)],
        compiler_params=_cp(("arbitrary", "arbitrary"), 56),
    )(*_pin(proj, proj, proj, pre, d_yssd, yssm, h_prev, dt_bias, a_log, dskip_e, g_ssd))


def _grad_w_out(y_pool, y_ssd, d_o):
    t, d = d_o.shape
    tk = POOL_WIDTH
    n_s = SSD_INNER // tk

    def body(p_ref, s_ref, g_ref, o_ref):
        i = pl.program_id(0)

        @pl.when(i == 0)
        def _():
            o_ref[...] = _tn(p_ref[...], g_ref[...]).astype(BF)

        @pl.when(i > 0)
        def _():
            o_ref[...] = _tn(s_ref[...], g_ref[...]).astype(BF)

    return pl.pallas_call(
        body, name="grad_w_out", grid=(1 + n_s,),
        in_specs=[pl.BlockSpec((t, tk), lambda i: (0, 0)), pl.BlockSpec((t, tk), lambda i: (0, jnp.maximum(i - 1, 0))),
                  pl.BlockSpec((t, d), lambda i: (0, 0))],
        out_specs=pl.BlockSpec((tk, d), lambda i: (i, 0)),
        out_shape=_out((POOL_WIDTH + SSD_INNER, d), BF),
        compiler_params=_cp(("parallel",), 56))(*_pin(y_pool, y_ssd, d_o))


def _grad_w_in_t(d_upool, d_z, d_uxbc, d_udt, u1):
    t, d = u1.shape
    tk = 512
    n_z, n_x = SSD_INNER // tk, CONV_CH // tk

    def body(p_ref, z_ref, x_ref, dt_ref, u_ref, o_ref):
        i = pl.program_id(0)

        @pl.when(i == 0)
        def _():
            o_ref[...] = _tn(p_ref[...], u_ref[...]).astype(BF)

        @pl.when((i >= 1) & (i < 1 + n_z))
        def _():
            o_ref[...] = _tn(z_ref[...], u_ref[...]).astype(BF)

        @pl.when((i >= 1 + n_z) & (i < 1 + n_z + n_x))
        def _():
            o_ref[...] = _tn(x_ref[...], u_ref[...]).astype(BF)

        @pl.when(i == 1 + n_z + n_x)
        def _():
            o_ref[0:128, :] = _tn(dt_ref[...], u_ref[...]).astype(BF)

    return pl.pallas_call(
        body, name="grad_w_in", grid=(2 + n_z + n_x,),
        in_specs=[pl.BlockSpec((t, tk), lambda i: (0, 0)),
                  pl.BlockSpec((t, tk), lambda i: (0, jnp.clip(i - 1, 0, n_z - 1))),
                  pl.BlockSpec((t, tk), lambda i: (0, jnp.clip(i - 1 - n_z, 0, n_x - 1))),
                  pl.BlockSpec((t, 128), lambda i: (0, 0)), pl.BlockSpec((t, d), lambda i: (0, 0))],
        out_specs=pl.BlockSpec((tk, d), lambda i: (i, 0)),
        out_shape=_out((IN_PAD, d), BF),
        compiler_params=_cp(("parallel",), 56))(*_pin(d_upool, d_z, d_uxbc, d_udt, u1))


def _conv_bwd(d_pre, proj, conv_w, nb, seq):
    ts = min(256, seq)
    nt = seq // ts
    hb = ts // CONV_HALO
    last_block = nb * seq // CONV_HALO - 1
    n_ext = CHUNK + CONV_HALO

    def body(dp_ref, dnext_ref, u_ref, cw_ref, du_ref, gw_ref, gb_ref):
        b = pl.program_id(0)
        i = pl.program_id(1)

        @pl.when((b == 0) & (i == 0))
        def _():
            gw_ref[...] = jnp.zeros_like(gw_ref)
            gb_ref[...] = jnp.zeros_like(gb_ref)

        for c0 in range(0, CONV_CH, 128):
            cs = slice(c0, c0 + 128)
            cw = cw_ref[:, cs]
            gw = [0.0] * 4
            gb = 0.0
            for r0 in range(0, ts, CHUNK):
                dp = dp_ref[r0:r0 + CHUNK, cs]
                u = u_ref[r0:r0 + CHUNK, cs]
                if r0 + CHUNK < ts:
                    below = dp_ref[r0 + CHUNK:r0 + CHUNK + CONV_HALO, cs]
                else:
                    below = jnp.where(i == nt - 1, 0.0, dnext_ref[:, cs])
                ext_d = jnp.concatenate([dp, below], 0)
                du = dp * cw[3:4]
                gw[3] = gw[3] + jnp.sum(dp * u, 0, keepdims=True)
                for k in (2, 1, 0):
                    shifted = pltpu.roll(ext_d, n_ext - (3 - k), 0)[:CHUNK]
                    du = du + shifted * cw[k:k + 1]
                    gw[k] = gw[k] + jnp.sum(shifted * u, 0, keepdims=True)
                gb = gb + jnp.sum(dp, 0, keepdims=True)
                du_ref[r0:r0 + CHUNK, cs] = du.astype(BF)
            for k in range(4):
                gw_ref[k:k + 1, cs] += gw[k]
            gb_ref[:, cs] += gb

    return pl.pallas_call(
        body, name="conv_bwd", grid=(nb, nt),
        in_specs=[pl.BlockSpec((ts, CONV_CH), lambda b, i: (b * nt + i, 0)),
                  pl.BlockSpec((CONV_HALO, CONV_CH), lambda b, i: (jnp.minimum((b * nt + i + 1) * hb, last_block), 0)),
                  pl.BlockSpec((ts, CONV_CH), lambda b, i: (b * nt + i, 1)),
                  pl.BlockSpec((4, CONV_CH), lambda b, i: (0, 0))],
        out_specs=[pl.BlockSpec((ts, CONV_CH), lambda b, i: (b * nt + i, 0)),
                   pl.BlockSpec((8, CONV_CH), lambda b, i: (0, 0)), pl.BlockSpec((1, CONV_CH), lambda b, i: (0, 0))],
        out_shape=[_out((nb * seq, CONV_CH), BF), _out((8, CONV_CH), F32),
                   _out((1, CONV_CH), F32)],
        compiler_params=_cp(("arbitrary", "arbitrary"), 48))(*_pin(d_pre, d_pre, proj, conv_w))


def _in_proj_bwd(d_parts, w_in_t, x, dh1, mod3, g_mix, seq, token):
    t, d = x.shape
    nb = t // seq
    tm = min(ROW_TILE, seq)
    tps = seq // tm

    widths = [p.shape[1] for p in d_parts]

    def body(d0_ref, d1_ref, d2_ref, d3_ref, w_ref, x_ref, dh1_ref, mod_ref, g_ref, tok_ref, gx_ref, acc_ref, gg_ref):
        i = pl.program_id(0)

        @pl.when(i == 0)
        def _():
            gg_ref[...] = jnp.zeros_like(gg_ref)

        @pl.when(i % tps == 0)
        def _():
            acc_ref[...] = jnp.zeros_like(acc_ref)

        gg = a_shift = a_scale = 0.0
        for rows in _sub_rows(tm):
            d_cat = jnp.concatenate([p_ref[rows, :] for p_ref in (d0_ref, d1_ref, d2_ref, d3_ref)], 1)
            du = _nn(d_cat, w_ref[...])
            xv = x_ref[rows, :]
            r = lax.rsqrt(jnp.mean(xv * xv, -1, keepdims=True) + EPS)
            hh = xv * r
            n1 = hh * g_ref[...]
            dn1 = du * (1.0 + mod_ref[0, 1:2, :])
            dhat = dn1 * g_ref[...]
            gx_ref[rows, :] = dh1_ref[rows, :] + r * (dhat - hh * jnp.mean(dhat * hh, -1, keepdims=True))
            gg = gg + jnp.sum(dn1 * hh, 0, keepdims=True)
            a_shift = a_shift + jnp.sum(du, 0, keepdims=True)
            a_scale = a_scale + jnp.sum(du * n1, 0, keepdims=True)
        gg_ref[...] += gg
        acc_ref[0, 0:1, :] += a_shift
        acc_ref[0, 1:2, :] += a_scale

    row = lambda i: (i, 0)
    vec = pl.BlockSpec((1, d), lambda i: (0, 0))
    return pl.pallas_call(
        body, name="in_proj_bwd", grid=(t // tm,),
        in_specs=[pl.BlockSpec((tm, wd), row) for wd in widths] +
                 [_RESIDENT, pl.BlockSpec((tm, d), row),
                  pl.BlockSpec((tm, d), row), pl.BlockSpec((1, N_MOD, d), lambda i: (i // tps, 0, 0)), vec, _token_spec()],
        out_specs=[pl.BlockSpec((tm, d), row), pl.BlockSpec((1, 8, d), lambda i: (i // tps, 0, 0)), vec],
        out_shape=[_out((t, d), F32), _out((nb, 8, d), F32),
                   _out((1, d), F32)],
        compiler_params=_cp(("arbitrary",), 40))(*_pin(*d_parts), w_in_t, *_pin(x, dh1, mod3, g_mix, token))


_VEC_LAYOUT = (("g_mix", 1024), ("conv_b", 1536), ("g_ssd", 1024), ("pool_scale", 512), ("g_mlp", 1024),
               ("g_final", 1024), ("dt_bias", 128), ("a_log", 128), ("d_skip_lanes", 1024), ("sq_err", 1024))
_VEC_OFFSET = {}
_off = 0
for _name, _n in _VEC_LAYOUT:
    _VEC_OFFSET[_name] = _off
    _off += _n
_VEC_LANES = _off
_SMALL_PARAMS = ("b_ada", "g_mix", "conv_w", "conv_b", "dt_bias", "a_log", "d_skip", "g_ssd", "w_pool", "pool_scale",
                 "g_mlp", "g_final")


def _pack_vec(parts):
    cols = []
    for name, n in _VEC_LAYOUT:
        v = parts[name]
        if v.shape[1] < n:
            v = jnp.pad(v, ((0, 0), (0, n - v.shape[1])))
        cols.append(v)
    return jnp.concatenate(cols, 1)


def _small_adam(vec_all, wpool_all, convw_all, dmod_all, params):
    names = _SMALL_PARAMS
    nin = 4 + 3 * len(names)

    def body(*refs):
        vec_ref, wp_ref, cw_ref, dm_ref = refs[:4]
        prm = {n: refs[4 + 3 * i:7 + 3 * i] for i, n in enumerate(names)}
        loss_ref = refs[nin]
        outs = {n: refs[nin + 1 + 4 * i:nin + 5 + 4 * i] for i, n in enumerate(names)}
        vsum = vec_ref[0]
        for s in range(1, N_DEV):
            vsum = vsum + vec_ref[s]

        def lanes(name, n):
            off = _VEC_OFFSET[name]
            return vsum[:, off:off + n]

        grads = {n: lanes(n, prm[n][0].shape[1]) for n in ("g_mix", "conv_b", "g_ssd", "pool_scale", "g_mlp", "g_final", "dt_bias")}
        grads["a_log"] = lanes("a_log", SSD_HEADS) * (-jnp.exp(prm["a_log"][0][...]))
        per_lane = jnp.broadcast_to(lanes("d_skip_lanes", SSD_INNER), (8, SSD_INNER))
        grads["d_skip"] = _exact_nn(per_lane, _head_reduce_matrix(SSD_INNER, SSD_HEADS))[0:1]
        gwp = wp_ref[0].astype(F32)
        gcw = cw_ref[0]
        gb = jnp.sum(dm_ref[0], 0, keepdims=True)
        for s in range(1, N_DEV):
            gwp = gwp + wp_ref[s].astype(F32)
            gcw = gcw + cw_ref[s]
            gb = gb + jnp.sum(dm_ref[s], 0, keepdims=True)
        grads["w_pool"] = gwp
        grads["conv_w"] = gcw[0:4]
        grads["b_ada"] = gb
        total = jnp.sum(lanes("sq_err", D_MODEL), 1, keepdims=True) * (0.5 / D_MODEL)
        loss_ref[...] = jnp.broadcast_to(total, loss_ref.shape)
        for n in names:
            w_ref, m_ref, v_ref = prm[n]
            g = grads[n]
            d, m2, v2 = _adam_math(w_ref[...], g, m_ref[...], v_ref[...])
            g_ref, d_ref, m2_ref, v2_ref = outs[n]
            g_ref[...] = g
            d_ref[...] = d
            m2_ref[...] = m2
            v2_ref[...] = v2

    flat = [vec_all, wpool_all, convw_all, dmod_all]
    out_shape = [jax.ShapeDtypeStruct((1, 128), F32)]
    for n in names:
        flat += list(params[n])
        out_shape += [jax.ShapeDtypeStruct(params[n][0].shape, F32)] * 4
    vm = pl.BlockSpec(memory_space=pltpu.VMEM)
    res = pl.pallas_call(body, name="small_adam", out_shape=out_shape, in_specs=[vm] * len(flat),
                         out_specs=[vm] * len(out_shape), compiler_params=_cp(vmem_mb=48))(*flat)
    return res[0], {n: res[1 + 4 * i:5 + 4 * i] for i, n in enumerate(names)}


_WEIGHTS = ("w_ada", "b_ada", "g_mix", "w_in", "conv_w", "conv_b", "dt_bias", "a_log", "d_skip", "g_ssd", "w_pool",
            "pool_scale", "w_out", "g_mlp", "w_up", "w_down", "g_final")


def _local_step(x2, tg2, mod3, seq, w_in_t, first_token, weights_arrived, weights_later, start_reduce, before_last,
                conv_w_full, sp):
    t, d = x2.shape
    nb = t // seq
    dskip_e = jnp.repeat(sp["d_skip"], SSD_HEAD_DIM, axis=1)
    proj, u1 = _in_proj(x2, mod3, sp["g_mix"], w_in_t, seq, first_token)
    y_ssd, yssm, h_prev, pre = _ssd_fwd(proj, conv_w_full, sp["conv_b"], sp["dt_bias"], sp["a_log"], dskip_e, sp["g_ssd"], nb, seq)
    y_pool, p = _pool_fwd(proj, sp["w_pool"], sp["pool_scale"], nb, seq, weights_arrived(y_ssd))
    w_out_f, w_up4, w_down4 = weights_later(y_pool)
    w_down_f = w_down4.reshape(D_FF, d)
    h1, o, u2 = _out_proj(y_pool, y_ssd, w_out_f, x2, mod3, sp["g_mlp"], seq)
    a_up = _mlp_up(u2, w_up4)
    d_dn, dh2, sq, gg_final, d_gf = _mlp_down_loss(a_up, w_down_f, h1, mod3, sp["g_final"], tg2, seq)

    gw_down = _tn_matmul(a_up, d_dn, 512, d, "grad_w_down", square_relu=True)
    tok = start_reduce("w_down", gw_down.reshape(N_CHIPS, D_FF // N_CHIPS, d))
    d_a = _mlp_down_bwd(d_dn, w_down4, a_up, tok)
    gw_up4 = _tn_matmul(u2, d_a, 512, d, "grad_w_up", out3=True)
    tok = start_reduce("w_up", gw_up4)
    dh1, d_o, accf, gg_mlp = _mlp_up_bwd(d_a, w_up4, h1, dh2, o, mod3, sp["g_mlp"], seq, tok)
    gw_out = _grad_w_out(y_pool, y_ssd, d_o)
    tok = start_reduce("w_out", gw_out.reshape(N_CHIPS, gw_out.shape[0] // N_CHIPS, d))
    d_ypool, d_yssd = _out_proj_bwd(d_o, w_out_f, tok)
    d_upool, gw_pool, g_ps = _pool_bwd(d_ypool, p, sp["w_pool"], sp["pool_scale"], nb, seq)
    d_z, d_pre, d_udt, gg_ssd, gdsk, ga, gdtb = _ssd_bwd(proj, pre, d_yssd, yssm, h_prev, sp["dt_bias"], sp["a_log"],
                                                        dskip_e, sp["g_ssd"], nb, seq)
    d_uxbc, gconvw, gconvb = _conv_bwd(d_pre, proj, conv_w_full, nb, seq)
    gw_in_t = _grad_w_in_t(d_upool, d_z, d_uxbc, d_udt, u1)
    shard_rows = IN_WIDTH // N_CHIPS
    tok = start_reduce("w_in", jnp.stack([gw_in_t[k * shard_rows:(k + 1) * shard_rows] for k in range(N_CHIPS)]))
    gx, accm, gg_mix = _in_proj_bwd([d_upool, d_z, d_uxbc, d_udt], w_in_t, x2, dh1, mod3, sp["g_mix"], seq, before_last(tok))

    d_mod = jnp.concatenate([accm[:, 0], accm[:, 1], accf[:, 2], accf[:, 0], accf[:, 1], d_gf[:, 0]], 1)
    vec = _pack_vec({"g_mix": gg_mix, "conv_b": gconvb, "g_ssd": gg_ssd, "pool_scale": g_ps, "g_mlp": gg_mlp,
                     "g_final": gg_final, "dt_bias": gdtb, "a_log": ga, "d_skip_lanes": gdsk, "sq_err": sq})
    return gx, d_mod, vec, gw_pool, gconvw


def kernel(x, c, w_ada, b_ada, g_mix, w_in, conv_w, conv_b, dt_bias, a_log, d_skip, g_ssd, w_pool, pool_scale, w_out, g_mlp, w_up, w_down, g_final, loss_target, m_w_ada, m_b_ada, m_g_mix, m_w_in, m_conv_w, m_conv_b, m_dt_bias, m_a_log, m_d_skip, m_g_ssd, m_w_pool, m_pool_scale, m_w_out, m_g_mlp, m_w_up, m_w_down, m_g_final, v_w_ada, v_b_ada, v_g_mix, v_w_in, v_conv_w, v_conv_b, v_dt_bias, v_a_log, v_d_skip, v_g_ssd, v_w_pool, v_pool_scale, v_w_out, v_g_mlp, v_w_up, v_w_down, v_g_final):
    nb, seq, d = x.shape
    t = nb * seq
    xi, yi, ci = _mesh_pos()
    chip = 2 * xi + yi
    me = 4 * xi + 2 * yi + ci
    ada_cols = w_ada.shape[2]
    conv_cols = conv_w.shape[2]
    in_cols = w_in.shape[2]
    w_in_s, m_w_in_s, v_w_in_s = w_in[0].T, m_w_in[0].T, v_w_in[0].T

    c8, convw8 = _all_gather_small([c, conv_w[0]])
    w_in_b = w_in_s.astype(BF)
    i_send, i_recv, i_src, i_land, in_token = _ici_start(
        [w_in_b], [jax.ShapeDtypeStruct((N_CHIPS,) + w_in_b.shape, BF)], _gather_sent, _gather_landing, "gather_start_w_in",
        after=c8)
    c_all = c8.reshape(N_DEV * nb, d)
    conv_w_full = convw8[0::2].transpose(1, 0, 2).reshape(4, N_CHIPS * conv_cols)
    b_shard = lax.dynamic_slice(b_ada, (0, chip * ada_cols), (1, ada_cols))
    mod_part, c_act = _ada_mod(c_all, w_ada[0], b_shard, in_token)
    mod_rows = mod_part.reshape(N_DEV, nb, ada_cols)
    m_send, m_recv, m_src, m_land, _ = _ici_start(
        [mod_rows], [jax.ShapeDtypeStruct((N_CHIPS, nb, ada_cols), F32)], _mod_sent, _mod_landing, "mod_start", after=mod_part)

    later = [w_out[0].astype(BF), w_up[0].astype(BF), w_down[0].astype(BF)]
    in_shard, in_land = _ici_wait(i_send, i_recv, i_src, i_land, [m_src[0]] + later, _gather_sent, _gather_landing,
                                  "gather_wait_w_in")
    (w_in4,) = _gather_finish(in_land, in_shard)
    w_in_t = jnp.pad(w_in4.reshape(N_CHIPS * in_cols, d), ((0, IN_PAD - N_CHIPS * in_cols), (0, 0)))
    mod_mine, mod_land = _ici_wait(m_send, m_recv, m_src, m_land, w_in_t, _mod_sent, _mod_landing, "mod_wait")
    mod_own = lax.dynamic_slice(mod_mine[0], (me, 0, 0), (1, nb, ada_cols))
    mod4 = lax.dynamic_update_slice(mod_land[0], mod_own, (chip, 0, 0))
    mod3 = mod4.transpose(1, 0, 2).reshape(nb, N_MOD, d)
    g_send, g_recv, g_src, g_land, first_token = _ici_start(
        later, [jax.ShapeDtypeStruct((N_CHIPS,) + s.shape, BF) for s in later], _gather_sent, _gather_landing, "gather_start",
        after=w_in4)

    def weights_arrived(after):
        shards, lands = _ici_wait(g_send, g_recv, g_src, g_land, after, _gather_sent, _gather_landing, "gather_wait")
        pending["forward"] = _forward_start(lands, "forward_start") + (shards,)
        return pending["forward"][3]

    def weights_later(after):
        f_send, f_recv, f_land, _, shards = pending["forward"]
        lands = _forward_wait(f_send, f_recv, f_land, after, "forward_wait")
        w_out4, w_up4, w_down4 = [lax.dynamic_update_slice(land, shard[None], (chip, 0, 0)) for land, shard in zip(lands, shards)]
        return w_out4.reshape(N_CHIPS * w_out.shape[1], d), w_up4, w_down4

    pending = {}

    def start_reduce(name, grad4):
        pending[name] = _reduce_start(grad4, "reduce_start_" + name)
        return pending[name][4]

    pos = jnp.stack([ci, chip, me]).astype(jnp.int32)
    early = ("w_out", "w_up", "w_down")

    def summed_half(name, after):
        r_send, r_recv, r_src, r_land, _ = pending[name]
        own, recv = _reduce_wait(r_send, r_recv, r_src, r_land, after, "reduce_wait_" + name)
        return _sum_eight(recv, own, pos)

    def before_last(token):
        pending["early_halves"] = _exchange_start([summed_half(n, token) for n in early], SIBLING, "halves_start_early")
        return pending["early_halves"][4]

    sp = dict(g_mix=g_mix, conv_b=conv_b, dt_bias=dt_bias, a_log=a_log, d_skip=d_skip, g_ssd=g_ssd,
              w_pool=w_pool[0], pool_scale=pool_scale, g_mlp=g_mlp, g_final=g_final.reshape(1, d))
    gx, d_mod, vec, gw_pool, gconvw = _local_step(
        x.reshape(t, d), loss_target.reshape(t, d), mod3, seq, w_in_t, first_token, weights_arrived, weights_later, start_reduce,
        before_last, conv_w_full, sp)

    small_parts = [vec, gw_pool.reshape(4 * POOL_GROUP, POOL_GROUP).astype(BF), gconvw, d_mod]
    s_send, s_recv, s_src, s_land, s_token = _exchange_start(small_parts, ALL_PEERS, "small_start")
    h_send, h_recv, h_src, h_land, h_token = _exchange_start([summed_half("w_in", s_token)], SIBLING, "halves_start")
    e_send, e_recv, e_src, e_land, _ = pending["early_halves"]
    e_own, e_got = _exchange_wait(e_send, e_recv, e_src, e_land, SIBLING, h_token, "halves_wait_early")
    res = {}
    for i, (n, w, m, v) in enumerate((("w_out", w_out, m_w_out, v_w_out), ("w_up", w_up, m_w_up, v_w_up),
                                      ("w_down", w_down, m_w_down, v_w_down))):
        g, dl, m2, v2 = _adam_big(e_own[i], e_got[i], w[0], m[0], v[0], pos)
        res[n] = (g[None], dl[None], m2[None], v2[None])

    s_own, s_got = _exchange_wait(s_send, s_recv, s_src, s_land, ALL_PEERS, res["w_down"][1], "small_wait")
    vec8, wpool8, convw8g, dmod8 = [lax.dynamic_update_slice(got, mine[None], (me,) + (0,) * mine.ndim)
                                    for got, mine in zip(s_got, s_own)]
    convw8s = lax.dynamic_slice(convw8g, (0, 0, chip * conv_cols), (N_DEV, 8, conv_cols))
    m_in = dict(b_ada=m_b_ada, g_mix=m_g_mix, conv_w=m_conv_w[0], conv_b=m_conv_b, dt_bias=m_dt_bias, a_log=m_a_log,
                d_skip=m_d_skip, g_ssd=m_g_ssd, w_pool=m_w_pool.reshape(4 * POOL_GROUP, POOL_GROUP), pool_scale=m_pool_scale,
                g_mlp=m_g_mlp, g_final=m_g_final.reshape(1, d))
    v_in = dict(b_ada=v_b_ada, g_mix=v_g_mix, conv_w=v_conv_w[0], conv_b=v_conv_b, dt_bias=v_dt_bias, a_log=v_a_log,
                d_skip=v_d_skip, g_ssd=v_g_ssd, w_pool=v_w_pool.reshape(4 * POOL_GROUP, POOL_GROUP), pool_scale=v_pool_scale,
                g_mlp=v_g_mlp, g_final=v_g_final.reshape(1, d))
    w_small = dict(sp, b_ada=b_ada, conv_w=conv_w[0], w_pool=w_pool.reshape(4 * POOL_GROUP, POOL_GROUP))
    loss_row, small = _small_adam(vec8, wpool8, convw8s, dmod8, {n: (w_small[n], m_in[n], v_in[n]) for n in _SMALL_PARAMS})

    dmod_all = dmod8.reshape(N_DEV * nb, N_CHIPS * ada_cols)
    dmod_cols = lax.dynamic_slice(dmod_all, (0, chip * ada_cols), (N_DEV * nb, ada_cols))
    res.update({n: tuple(r.reshape(w.shape) for r in small[n])
                for n, w in (("b_ada", b_ada), ("g_mix", g_mix), ("conv_w", conv_w), ("conv_b", conv_b), ("dt_bias", dt_bias),
                             ("a_log", a_log), ("d_skip", d_skip), ("g_ssd", g_ssd), ("w_pool", w_pool),
                             ("pool_scale", pool_scale), ("g_mlp", g_mlp), ("g_final", g_final))})
    g_ada, d_ada, m_ada, v_ada = _adam_ada(c_act.T.astype(BF), dmod_cols, w_ada[0], m_w_ada[0], v_w_ada[0])
    res["w_ada"] = (g_ada[None], d_ada[None], m_ada[None], v_ada[None])
    h_own, h_got = _exchange_wait(h_send, h_recv, h_src, h_land, SIBLING, g_ada, "halves_wait")
    rows3 = lambda a: jnp.transpose(a, (2, 0, 1))
    res["w_in"] = tuple(jnp.transpose(r, (1, 2, 0))
                        for r in _adam_rows(h_own[0], h_got[0], rows3(w_in), rows3(m_w_in), rows3(v_w_in), pos))

    loss = loss_row[0, 0]
    return (loss, gx.reshape(nb, seq, d), *[res[n][0] for n in _WEIGHTS], *[res[n][1] for n in _WEIGHTS],
            *[res[n][2] for n in _WEIGHTS], *[res[n][3] for n in _WEIGHTS])
```

```python
import jax
import jax.numpy as jnp
from jax import lax
from jax.experimental import pallas as pl
from jax.experimental.pallas import tpu as pltpu

F32 = jnp.float32
BF = jnp.bfloat16
MESH = pl.DeviceIdType.MESH

EPS = 1e-5
D_MODEL = 1024
POOL_WIDTH = 512
POOL_WINDOWS = (2, 4, 8, 16)
POOL_GROUP = 128
SSD_INNER = 1024
SSD_HEADS = 16
SSD_HEAD_DIM = 64
SSD_STATE = 128
GROUP_W = 512
CHUNK = 128
CONV_CH = 1536
OFF_DT = 3072
IN_WIDTH = 3088
IN_PAD = 3200
D_FF = 4096
N_MOD = 6
N_CHIPS = 4
N_DEV = 8
HALO = 16
CONV_HALO = 8

ADAM_LR = 0.001
ADAM_B1 = 0.9
ADAM_B2 = 0.999
ADAM_EPS = 1e-08
ADAM_WD = 0.01
ADAM_STEP = 10

VMEM_BYTES_V7X = 64 * 1024 * 1024


def _cp(semantics=None, vmem_mb=48, **kw):
    assert vmem_mb * 1024 * 1024 < VMEM_BYTES_V7X
    args = dict(vmem_limit_bytes=vmem_mb * 1024 * 1024, **kw)
    if semantics is not None:
        args["dimension_semantics"] = semantics
    return pltpu.CompilerParams(**args)


def _out(shape, dtype):
    return pltpu.HBM(shape, dtype)


def _pin(*arrays):
    return [pltpu.with_memory_space_constraint(a, pltpu.HBM) for a in arrays]


def _nn(a, b):
    return jnp.dot(a, b, preferred_element_type=F32)


def _nt(a, b):
    return lax.dot_general(a, b, (((1,), (1,)), ((), ())), preferred_element_type=F32)


def _tn(a, b):
    return lax.dot_general(a, b, (((0,), (0,)), ((), ())), preferred_element_type=F32)


def _split3(v):
    hi = v.astype(BF)
    r1 = v - hi.astype(F32)
    mid = r1.astype(BF)
    lo = (r1 - mid.astype(F32)).astype(BF)
    return hi, mid, lo


def _exact_nn(v, m01):
    hi, mid, lo = _split3(v)
    return _nn(hi, m01) + _nn(mid, m01) + _nn(lo, m01)


def _exact_nn_left(m01, v):
    hi, mid, lo = _split3(v)
    return _nn(m01, hi) + _nn(m01, mid) + _nn(m01, lo)


def _exact_nt_left(m01, v):
    hi, mid, lo = _split3(v)
    return _nt(m01, hi) + _nt(m01, mid) + _nt(m01, lo)


def _sigmoid(v):
    return 1.0 / (1.0 + jnp.exp(-v))


def _iota(shape, dim):
    return lax.broadcasted_iota(jnp.int32, shape, dim)


def _head_expand_matrix(heads, width):
    return (_iota((heads, width), 1) // SSD_HEAD_DIM == _iota((heads, width), 0)).astype(BF)


def _head_reduce_matrix(width, heads):
    return (_iota((width, heads), 0) // SSD_HEAD_DIM == _iota((width, heads), 1)).astype(BF)


def _mesh_pos():
    return lax.axis_index("x"), lax.axis_index("y"), lax.axis_index("c")


def _flip(v, bit):
    return v + bit - 2 * bit * v


_HBM = pl.BlockSpec(memory_space=pltpu.HBM)
_SEM = pl.BlockSpec(memory_space=pltpu.SEMAPHORE)
_DATAFLOW = pltpu.SideEffectType.DATAFLOW_SIDE_EFFECTING


def _peer_chip(x, y, j):
    return _flip(x, (j >> 1) & 1), _flip(y, j & 1)


def _ici_start(srcs, land_shapes, sent, landing, name, after):
    n = len(srcs)

    def body(*refs):
        src_refs, land_refs = refs[:n], refs[n:2 * n]
        send_sems, recv_sems = refs[2 * n + 1], refs[2 * n + 2]
        token = refs[-1]
        x, y, c = _mesh_pos()
        for j in range(1, N_CHIPS):
            px, py = _peer_chip(x, y, j)
            for a in range(n):
                pltpu.make_async_remote_copy(
                    src_ref=sent(src_refs[a], c, 2 * px + py), dst_ref=landing(land_refs[a], c, 2 * x + y),
                    send_sem=send_sems.at[a * (N_CHIPS - 1) + j - 1], recv_sem=recv_sems.at[a * (N_CHIPS - 1) + j - 1],
                    device_id=(px, py, c), device_id_type=MESH).start()
        token[...] = jnp.zeros_like(token)

    sems = pltpu.SemaphoreType.DMA((n * (N_CHIPS - 1),))
    lands = [pltpu.with_memory_space_constraint(lax.empty(s.shape, s.dtype), pltpu.HBM) for s in land_shapes]
    outs = pl.pallas_call(
        body, name=name,
        out_shape=(sems, sems, *[pltpu.HBM(s.shape, s.dtype) for s in srcs],
                   *[pltpu.HBM(s.shape, s.dtype) for s in land_shapes], jax.ShapeDtypeStruct((8, 128), F32)),
        in_specs=[_HBM] * (2 * n + 1), out_specs=[_SEM, _SEM] + [_HBM] * (2 * n) + [pl.BlockSpec(memory_space=pltpu.VMEM)],
        input_output_aliases={i: 2 + i for i in range(2 * n)},
        compiler_params=pltpu.CompilerParams(has_side_effects=_DATAFLOW),
    )(*_pin(*srcs), *lands, *_pin(after))
    return outs[0], outs[1], outs[2:2 + n], outs[2 + n:2 + 2 * n], outs[-1]


def _ici_wait(send_sems, recv_sems, src_thru, land_thru, after, sent, landing, name):
    n = len(src_thru)
    afters = list(after) if isinstance(after, (list, tuple)) else [after]

    def body(*refs):
        src_refs, land_refs = refs[:n], refs[n:2 * n]
        send_sems, recv_sems = refs[2 * n], refs[2 * n + 1]
        x, y, c = _mesh_pos()
        for j in range(1, N_CHIPS):
            px, py = _peer_chip(x, y, j)
            for a in range(n):
                cp = pltpu.make_async_remote_copy(
                    src_ref=sent(src_refs[a], c, 2 * px + py), dst_ref=landing(land_refs[a], c, 2 * px + py),
                    send_sem=send_sems.at[a * (N_CHIPS - 1) + j - 1], recv_sem=recv_sems.at[a * (N_CHIPS - 1) + j - 1],
                    device_id=(px, py, c), device_id_type=MESH)
                cp.wait_send()
                cp.wait_recv()

    outs = pl.pallas_call(
        body, name=name,
        out_shape=tuple(pltpu.HBM(s.shape, s.dtype) for s in (*src_thru, *land_thru)),
        in_specs=[_HBM] * (2 * n) + [_SEM, _SEM] + [_HBM] * len(afters), out_specs=[_HBM] * (2 * n),
        input_output_aliases={i: i for i in range(2 * n)},
        compiler_params=pltpu.CompilerParams(has_side_effects=_DATAFLOW),
    )(*src_thru, *land_thru, send_sems, recv_sems, *_pin(*afters))
    return outs[:n], outs[n:]


def _col_half(ref, which, lead=()):
    hc = ref.shape[-1] // 2
    return ref.at[(*lead, slice(None), pl.ds(pl.multiple_of(which * hc, 128), hc))]


def _gather_sent(ref, c, dst_chip):
    return _col_half(ref, c)


def _gather_landing(ref, c, src_chip):
    return _col_half(ref, c, lead=(src_chip,))


def _mod_sent(ref, c, dst_chip):
    return ref.at[2 * dst_chip + c]


def _mod_landing(ref, c, src_chip):
    return ref.at[src_chip]


def _reduce_copy(src_ref, land_ref, send_sems, recv_sems, k, receiving):
    x, y, c = _mesh_pos()
    px, py, pc = _flip(x, (k >> 2) & 1), _flip(y, (k >> 1) & 1), _flip(c, k & 1)
    hc = src_ref.shape[2] // 2
    src = src_ref.at[2 * px + py, :, pl.ds(pl.multiple_of(pc * hc, 128), hc)]
    slot = (4 * px + 2 * py + pc) if receiving else (4 * x + 2 * y + c)
    return pltpu.make_async_remote_copy(
        src_ref=src, dst_ref=land_ref.at[slot], send_sem=send_sems.at[k - 1], recv_sem=recv_sems.at[k - 1],
        device_id=(px, py, pc), device_id_type=MESH)


def _reduce_start(grad4, name):
    k4, r, cols = grad4.shape

    def body(src_ref, land_ref, send_sems, recv_sems, src_thru, land_thru, token):
        for k in range(1, N_DEV):
            _reduce_copy(src_ref, land_ref, send_sems, recv_sems, k, receiving=False).start()
        token[...] = jnp.zeros_like(token)

    sems = pltpu.SemaphoreType.DMA((N_DEV - 1,))
    land = pltpu.with_memory_space_constraint(lax.empty((N_DEV, r, cols // 2), grad4.dtype), pltpu.HBM)
    return pl.pallas_call(
        body, name=name,
        out_shape=(sems, sems, pltpu.HBM(grad4.shape, grad4.dtype), pltpu.HBM(land.shape, land.dtype),
                   jax.ShapeDtypeStruct((8, 128), F32)),
        in_specs=[_HBM, _HBM], out_specs=[_SEM, _SEM, _HBM, _HBM, pl.BlockSpec(memory_space=pltpu.VMEM)],
        input_output_aliases={0: 2, 1: 3},
        compiler_params=pltpu.CompilerParams(has_side_effects=_DATAFLOW),
    )(*_pin(grad4), land)


def _reduce_wait(send_sems, recv_sems, src_thru, land_thru, after, name):
    def body(src_ref, land_ref, send_sems, recv_sems, after_ref, src_out, land_out):
        for k in range(1, N_DEV):
            cp = _reduce_copy(src_ref, land_ref, send_sems, recv_sems, k, receiving=True)
            cp.wait_send()
            cp.wait_recv()

    return pl.pallas_call(
        body, name=name,
        out_shape=(pltpu.HBM(src_thru.shape, src_thru.dtype), pltpu.HBM(land_thru.shape, land_thru.dtype)),
        in_specs=[_HBM, _HBM, _SEM, _SEM, _HBM], out_specs=[_HBM, _HBM],
        input_output_aliases={0: 0, 1: 1},
        compiler_params=pltpu.CompilerParams(has_side_effects=_DATAFLOW),
    )(src_thru, land_thru, send_sems, recv_sems, *_pin(after))


def _peer_copy(src_ref, land_ref, send_sems, recv_sems, idx, k, receiving):
    x, y, c = _mesh_pos()
    px, py, pc = _flip(x, (k >> 2) & 1), _flip(y, (k >> 1) & 1), _flip(c, k & 1)
    if land_ref.shape[0] == N_DEV:
        slot = (4 * px + 2 * py + pc) if receiving else (4 * x + 2 * y + c)
    else:
        slot = pc if receiving else c
    return pltpu.make_async_remote_copy(
        src_ref=src_ref, dst_ref=land_ref.at[slot], send_sem=send_sems.at[idx], recv_sem=recv_sems.at[idx],
        device_id=(px, py, pc), device_id_type=MESH)


def _exchange_start(arrays, peers, name):
    n = len(arrays)

    def body(*refs):
        src_refs, land_refs = refs[:n], refs[n:2 * n]
        send_sems, recv_sems = refs[2 * n], refs[2 * n + 1]
        token = refs[-1]
        for j, k in enumerate(peers):
            for a in range(n):
                _peer_copy(src_refs[a], land_refs[a], send_sems, recv_sems, a * len(peers) + j, k, receiving=False).start()
        token[...] = jnp.zeros_like(token)

    sems = pltpu.SemaphoreType.DMA((n * len(peers),))
    n_slots = N_DEV if len(peers) > 1 else 2
    lands = [pltpu.with_memory_space_constraint(lax.empty((n_slots,) + a.shape, a.dtype), pltpu.HBM) for a in arrays]
    outs = pl.pallas_call(
        body, name=name,
        out_shape=(sems, sems, *[pltpu.HBM(a.shape, a.dtype) for a in arrays], *[pltpu.HBM(l.shape, l.dtype) for l in lands],
                   jax.ShapeDtypeStruct((8, 128), F32)),
        in_specs=[_HBM] * (2 * n), out_specs=[_SEM, _SEM] + [_HBM] * (2 * n) + [pl.BlockSpec(memory_space=pltpu.VMEM)],
        input_output_aliases={i: 2 + i for i in range(2 * n)},
        compiler_params=pltpu.CompilerParams(has_side_effects=_DATAFLOW),
    )(*_pin(*arrays), *lands)
    return outs[0], outs[1], outs[2:2 + n], outs[2 + n:2 + 2 * n], outs[-1]


def _exchange_wait(send_sems, recv_sems, src_thru, land_thru, peers, after, name):
    n = len(src_thru)

    def body(*refs):
        src_refs, land_refs = refs[:n], refs[n:2 * n]
        send_sems, recv_sems = refs[2 * n], refs[2 * n + 1]
        for j, k in enumerate(peers):
            for a in range(n):
                cp = _peer_copy(src_refs[a], land_refs[a], send_sems, recv_sems, a * len(peers) + j, k, receiving=True)
                cp.wait_send()
                cp.wait_recv()

    outs = pl.pallas_call(
        body, name=name,
        out_shape=tuple(pltpu.HBM(s.shape, s.dtype) for s in (*src_thru, *land_thru)),
        in_specs=[_HBM] * (2 * n) + [_SEM, _SEM, _HBM], out_specs=[_HBM] * (2 * n),
        input_output_aliases={i: i for i in range(2 * n)},
        compiler_params=pltpu.CompilerParams(has_side_effects=_DATAFLOW),
    )(*src_thru, *land_thru, send_sems, recv_sems, *_pin(after))
    return outs[:n], outs[n:]


ALL_PEERS = tuple(range(1, N_DEV))
SIBLING = (1,)


def _sum_eight(recv, grad4, pos):
    n, r, hc = recv.shape
    steps = 2
    tc = hc // steps

    def body(pos_ref, r_ref, g_ref, o_ref):
        me = pos_ref[2]
        o_ref[...] = jnp.zeros_like(o_ref)
        for s in range(n):
            @pl.when(me == s)
            def _():
                o_ref[...] += g_ref[0].astype(F32)

            @pl.when(me != s)
            def _():
                o_ref[...] += r_ref[s].astype(F32)

    grid_spec = pltpu.PrefetchScalarGridSpec(
        num_scalar_prefetch=1, grid=(steps,),
        in_specs=[pl.BlockSpec((n, r, tc), lambda i, pos: (0, 0, i)),
                  pl.BlockSpec((1, r, tc), lambda i, pos: (pos[1], 0, pos[0] * steps + i))],
        out_specs=pl.BlockSpec((r, tc), lambda i, pos: (0, i)))
    return pl.pallas_call(body, name="sum_eight", grid_spec=grid_spec, out_shape=_out((r, hc), F32),
                          compiler_params=_cp(("parallel",), 32))(pos, *_pin(recv, grad4))


def _forward_copy(land_ref, send_sems, recv_sems, idx, j, receiving):
    x, y, c = _mesh_pos()
    px, py = _peer_chip(x, y, j)
    mine = _col_half(land_ref, c, lead=(2 * px + py,))
    theirs = _col_half(land_ref, 1 - c, lead=(2 * px + py,))
    return pltpu.make_async_remote_copy(
        src_ref=mine, dst_ref=theirs if receiving else mine, send_sem=send_sems.at[idx], recv_sem=recv_sems.at[idx],
        device_id=(x, y, 1 - c), device_id_type=MESH)


def _forward_start(lands, name):
    n = len(lands)

    def body(*refs):
        land_refs, send_sems, recv_sems, token = refs[:n], refs[n], refs[n + 1], refs[-1]
        for j in range(1, N_CHIPS):
            for a in range(n):
                _forward_copy(land_refs[a], send_sems, recv_sems, a * (N_CHIPS - 1) + j - 1, j, receiving=False).start()
        token[...] = jnp.zeros_like(token)

    sems = pltpu.SemaphoreType.DMA((n * (N_CHIPS - 1),))
    outs = pl.pallas_call(
        body, name=name,
        out_shape=(sems, sems, *[pltpu.HBM(l.shape, l.dtype) for l in lands], jax.ShapeDtypeStruct((8, 128), F32)),
        in_specs=[_HBM] * n, out_specs=[_SEM, _SEM] + [_HBM] * n + [pl.BlockSpec(memory_space=pltpu.VMEM)],
        input_output_aliases={i: 2 + i for i in range(n)},
        compiler_params=pltpu.CompilerParams(has_side_effects=_DATAFLOW),
    )(*lands)
    return outs[0], outs[1], outs[2:2 + n], outs[-1]


def _forward_wait(send_sems, recv_sems, lands_thru, after, name):
    n = len(lands_thru)

    def body(*refs):
        land_refs, send_sems, recv_sems = refs[:n], refs[n], refs[n + 1]
        for j in range(1, N_CHIPS):
            for a in range(n):
                cp = _forward_copy(land_refs[a], send_sems, recv_sems, a * (N_CHIPS - 1) + j - 1, j, receiving=True)
                cp.wait_send()
                cp.wait_recv()

    return pl.pallas_call(
        body, name=name,
        out_shape=tuple(pltpu.HBM(l.shape, l.dtype) for l in lands_thru),
        in_specs=[_HBM] * n + [_SEM, _SEM, _HBM], out_specs=[_HBM] * n,
        input_output_aliases={i: i for i in range(n)},
        compiler_params=pltpu.CompilerParams(has_side_effects=_DATAFLOW),
    )(*lands_thru, send_sems, recv_sems, *_pin(after))


def _gather_finish(lands, shards):
    n = len(lands)
    any_spec = _HBM

    def body(*refs):
        shard_refs, out_refs = refs[n:2 * n], refs[2 * n:3 * n]
        send_sems, recv_sems, local_sems = refs[3 * n:]
        x, y, c = _mesh_pos()
        chip = 2 * x + y
        local, sends = [], []
        for a in range(n):
            cp = pltpu.make_async_copy(shard_refs[a], out_refs[a].at[chip], local_sems.at[a])
            cp.start()
            local.append(cp)
        for j in range(1, N_CHIPS):
            px, py = _peer_chip(x, y, j)
            for a in range(n):
                landed = _col_half(out_refs[a], c, lead=(2 * px + py,))
                cp = pltpu.make_async_remote_copy(
                    src_ref=landed, dst_ref=landed, send_sem=send_sems.at[a, j], recv_sem=recv_sems.at[a, j],
                    device_id=(x, y, 1 - c), device_id_type=MESH)
                cp.start()
                sends.append(cp)
        for j in range(1, N_CHIPS):
            px, py = _peer_chip(x, y, j)
            for a in range(n):
                other = _col_half(out_refs[a], 1 - c, lead=(2 * px + py,))
                pltpu.make_async_remote_copy(
                    src_ref=other, dst_ref=other, send_sem=send_sems.at[a, j], recv_sem=recv_sems.at[a, j],
                    device_id=(x, y, 1 - c), device_id_type=MESH).wait_recv()
        for cp in sends:
            cp.wait_send()
        for cp in local:
            cp.wait()

    return pl.pallas_call(
        body, name="gather_finish",
        out_shape=[_out(l.shape, l.dtype) for l in lands],
        in_specs=[any_spec] * (2 * n), out_specs=[any_spec] * n,
        input_output_aliases={i: i for i in range(n)},
        scratch_shapes=[pltpu.SemaphoreType.DMA((n, N_CHIPS))] * 2 + [pltpu.SemaphoreType.DMA((n,))],
        compiler_params=_cp(vmem_mb=16),
    )(*lands, *shards)


def _adam_math(w, g, m, v):
    m2 = ADAM_B1 * m + (1.0 - ADAM_B1) * g
    v2 = ADAM_B2 * v + (1.0 - ADAM_B2) * (g * g)
    m_hat = m2 / (1.0 - ADAM_B1 ** ADAM_STEP)
    v_hat = v2 / (1.0 - ADAM_B2 ** ADAM_STEP)
    delta = -ADAM_LR * (m_hat / (jnp.sqrt(v_hat) + ADAM_EPS) + ADAM_WD * w)
    return delta, m2, v2


def _adam_big(g_own, g_pair, w, m, v, pos):
    r, c = w.shape
    per_half = 2
    tc = c // (2 * per_half)

    def body(pos_ref, go_ref, gp_ref, w_ref, m_ref, v_ref, g_ref, d_ref, m2_ref, v2_ref):
        half = pl.program_id(0) // per_half
        g = jnp.where(half == pos_ref[0], go_ref[...], gp_ref[0])
        d, m2, v2 = _adam_math(w_ref[...], g, m_ref[...], v_ref[...])
        g_ref[...] = g
        d_ref[...] = d
        m2_ref[...] = m2
        v2_ref[...] = v2

    spec = pl.BlockSpec((r, tc), lambda i, pos: (0, i))
    grid_spec = pltpu.PrefetchScalarGridSpec(
        num_scalar_prefetch=1, grid=(2 * per_half,),
        in_specs=[pl.BlockSpec((r, tc), lambda i, pos: (0, i % per_half)),
                  pl.BlockSpec((1, r, tc), lambda i, pos: (1 - pos[0], 0, i % per_half)), spec, spec, spec],
        out_specs=[spec] * 4)
    sh = _out((r, c), F32)
    return pl.pallas_call(body, name="adam_big", grid_spec=grid_spec, out_shape=[sh] * 4,
                          compiler_params=_cp(("parallel",), 32))(pos, *_pin(g_own, g_pair, w, m, v))


def _adam_rows(g_own, g_pair, w3, m3, v3, pos):
    r, _, c = w3.shape
    tr = 128

    def body(pos_ref, go_ref, gp_ref, w_ref, m_ref, v_ref, g_ref, d_ref, m2_ref, v2_ref):
        core = pos_ref[0]
        g = jnp.concatenate([jnp.where(core == 0, go_ref[...], gp_ref[0]), jnp.where(core == 1, go_ref[...], gp_ref[0])], 1)
        d, m2, v2 = _adam_math(w_ref[:, 0, :], g, m_ref[:, 0, :], v_ref[:, 0, :])
        g_ref[:, 0, :] = g
        d_ref[:, 0, :] = d
        m2_ref[:, 0, :] = m2
        v2_ref[:, 0, :] = v2

    spec = pl.BlockSpec((tr, 1, c), lambda i, pos: (i, 0, 0))
    grid_spec = pltpu.PrefetchScalarGridSpec(
        num_scalar_prefetch=1, grid=(pl.cdiv(r, tr),),
        in_specs=[pl.BlockSpec((tr, c // 2), lambda i, pos: (i, 0)),
                  pl.BlockSpec((1, tr, c // 2), lambda i, pos: (1 - pos[0], i, 0)), spec, spec, spec],
        out_specs=[spec] * 4)
    sh = _out(w3.shape, F32)
    return pl.pallas_call(body, name="adam_rows", grid_spec=grid_spec, out_shape=[sh] * 4,
                          compiler_params=_cp(("parallel",), 32))(pos, *_pin(g_own, g_pair, w3, m3, v3))


def _adam_ada(c_act_t, dmod_cols, w, m, v):
    r, c = w.shape
    tc = 512

    def body(ct_ref, dm_ref, w_ref, m_ref, v_ref, g_ref, d_ref, m2_ref, v2_ref):
        g = _nn(ct_ref[...], dm_ref[...].astype(BF))
        d, m2, v2 = _adam_math(w_ref[...], g, m_ref[...], v_ref[...])
        g_ref[...] = g
        d_ref[...] = d
        m2_ref[...] = m2
        v2_ref[...] = v2

    spec = pl.BlockSpec((r, tc), lambda i: (0, i))
    sh = _out((r, c), F32)
    return pl.pallas_call(
        body, name="adam_ada", grid=(c // tc,),
        in_specs=[pl.BlockSpec(c_act_t.shape, lambda i: (0, 0)), pl.BlockSpec((dmod_cols.shape[0], tc), lambda i: (0, i)),
                  spec, spec, spec],
        out_specs=[spec] * 4, out_shape=[sh] * 4, compiler_params=_cp(("parallel",), 48))(*_pin(c_act_t, dmod_cols, w, m, v))


def _ada_mod(c_all, w_shard, b_shard, token):
    nb, d = c_all.shape
    cols = w_shard.shape[1]
    tc = 512

    def body(c_ref, w_ref, b_ref, tok_ref, mod_ref, act_ref):
        cv = c_ref[...]
        act = cv * _sigmoid(cv)
        act_ref[...] = act
        mod_ref[...] = _nn(act.astype(BF), w_ref[...].astype(BF)) + b_ref[...]

    return pl.pallas_call(
        body, name="ada_mod", grid=(cols // tc,),
        in_specs=[pl.BlockSpec((nb, d), lambda i: (0, 0)), pl.BlockSpec((d, tc), lambda i: (0, i)),
                  pl.BlockSpec((1, tc), lambda i: (0, i)), _token_spec()],
        out_specs=[pl.BlockSpec((nb, tc), lambda i: (0, i)), pl.BlockSpec((nb, d), lambda i: (0, 0))],
        out_shape=[_out((nb, cols), F32), _out((nb, d), F32)],
        compiler_params=_cp(("arbitrary",), 32))(*_pin(c_all, w_shard, b_shard, token))


SUB_ROWS = 256
ROW_TILE = 512


def _sub_rows(tm):
    return [slice(s, s + SUB_ROWS) for s in range(0, tm, SUB_ROWS)] if tm > SUB_ROWS else [slice(0, tm)]


_RESIDENT = pl.BlockSpec(memory_space=pltpu.VMEM)


def _token_spec():
    return pl.BlockSpec((8, 128), lambda *_: (0, 0))


def _in_proj(x, mod3, g_mix, w_in_t, seq, token):
    t, d = x.shape
    tm = min(ROW_TILE, seq)
    tps = seq // tm

    def body(x_ref, mod_ref, g_ref, w_ref, tok_ref, proj_ref, u1_ref):
        for rows in _sub_rows(tm):
            xv = x_ref[rows, :]
            r = lax.rsqrt(jnp.mean(xv * xv, -1, keepdims=True) + EPS)
            u = (xv * r * g_ref[...]) * (1.0 + mod_ref[0, 1:2, :]) + mod_ref[0, 0:1, :]
            ub = u.astype(BF)
            u1_ref[rows, :] = ub
            proj_ref[rows, :] = _nt(ub, w_ref[...])

    return pl.pallas_call(
        body, name="in_proj", grid=(t // tm,),
        in_specs=[pl.BlockSpec((tm, d), lambda i: (i, 0)), pl.BlockSpec((1, N_MOD, d), lambda i: (i // tps, 0, 0)),
                  pl.BlockSpec((1, d), lambda i: (0, 0)), _RESIDENT, _token_spec()],
        out_specs=[pl.BlockSpec((tm, IN_PAD), lambda i: (i, 0)), pl.BlockSpec((tm, d), lambda i: (i, 0))],
        out_shape=[_out((t, IN_PAD), F32), _out((t, d), BF)],
        compiler_params=_cp(("parallel",), 40))(*_pin(x, mod3, g_mix), w_in_t, *_pin(token))


def _pool_tile(seq):
    return min(512, seq)


def _pool_fwd(proj, w_pool, pool_scale, nb, seq, token):
    ts = _pool_tile(seq)
    nt = seq // ts

    def body(u_ref, halo_ref, wp_ref, ps_ref, tok_ref, yp_ref, p_ref):
        i = pl.program_id(1)
        halo = jnp.where(i == 0, 0.0, halo_ref[...])
        u = u_ref[...]
        ext = jnp.concatenate([halo, u], 0)
        tpos = i * ts + _iota((ts, 1), 0)
        for g, w in enumerate(POOL_WINDOWS):
            gs = slice(g * POOL_GROUP, (g + 1) * POOL_GROUP)
            s = ext[:, gs]
            sh = 1
            while sh < w:
                s = s + pltpu.roll(s, sh, 0)
                sh *= 2
            cnt = jnp.minimum(tpos + 1, w).astype(F32)
            pb = (s[HALO:] / cnt - u[:, gs]).astype(BF)
            p_ref[:, gs] = pb
            yp_ref[:, gs] = (_nn(pb, wp_ref[g].astype(BF)) * ps_ref[:, gs]).astype(BF)

    hb = ts // HALO
    return pl.pallas_call(
        body, name="pool_fwd", grid=(nb, nt),
        in_specs=[pl.BlockSpec((ts, POOL_WIDTH), lambda b, i: (b * nt + i, 0)),
                  pl.BlockSpec((HALO, POOL_WIDTH), lambda b, i: (jnp.maximum((b * nt + i) * hb - 1, 0), 0)),
                  pl.BlockSpec((4, POOL_GROUP, POOL_GROUP), lambda b, i: (0, 0, 0)),
                  pl.BlockSpec((1, POOL_WIDTH), lambda b, i: (0, 0)), _token_spec()],
        out_specs=[pl.BlockSpec((ts, POOL_WIDTH), lambda b, i: (b * nt + i, 0))] * 2,
        out_shape=[_out((nb * seq, POOL_WIDTH), BF)] * 2,
        compiler_params=_cp(("parallel", "parallel"), 32))(*_pin(proj, proj, w_pool, pool_scale, token))


def _conv_pre(uxbc, halo, cw, cb, first):
    halo = jnp.where(first, 0.0, halo)
    ext = jnp.concatenate([halo, uxbc], 0)
    pre = cb + uxbc * cw[3:4]
    for k in (2, 1, 0):
        pre = pre + pltpu.roll(ext, 3 - k, 0)[CONV_HALO:] * cw[k:k + 1]
    return pre


def _chunk_terms(pre, udt, dtb, alog):
    sg = _sigmoid(pre)
    xbc = pre * sg
    dtp = udt[:, :SSD_HEADS] + dtb
    dt = jnp.maximum(dtp, 0.0) + jnp.log(1.0 + jnp.exp(-jnp.abs(dtp)))
    a = -jnp.exp(alog)
    da = dt * a
    tril = (_iota((CHUNK, CHUNK), 0) >= _iota((CHUNK, CHUNK), 1))
    acum = _exact_nn_left(tril.astype(BF), da)
    eye = (_iota((SSD_HEADS, SSD_HEADS), 0) == _iota((SSD_HEADS, SSD_HEADS), 1)).astype(BF)
    acum_t = _exact_nt_left(eye, acum)
    expand = _head_expand_matrix(SSD_HEADS, SSD_INNER)
    acum_e = _exact_nn(acum, expand)
    dt_e = _exact_nn(dt, expand)
    last_e = acum_e[CHUNK - 1:CHUNK]
    return dict(pre=pre, sg=sg, xbc=xbc, dtp=dtp, dt=dt, a=a, acum=acum, acum_t=acum_t, tril=tril,
                dt_e=dt_e, e_a=jnp.exp(acum_e), d_out=jnp.exp(last_e - acum_e), c_dec=jnp.exp(last_e))


def _head_decay(r, h):
    seg = r["acum"][:, h:h + 1] - r["acum_t"][h:h + 1, :]
    return jnp.where(r["tril"], jnp.exp(jnp.minimum(seg, 0.0)), 0.0)


SSD_SUB = 4
SSD_ROWS = SSD_SUB * CHUNK


def _ssd_specs(nb, seq, reverse):
    ns = seq // SSD_ROWS
    per = seq // CONV_HALO

    def cidx(c):
        return (ns - 1 - c) if reverse else c

    def row(b, c):
        return b * ns + cidx(c)

    specs = [
        pl.BlockSpec((SSD_ROWS, CONV_CH), lambda b, c: (row(b, c), 1)),
        pl.BlockSpec((CONV_HALO, CONV_CH),
                     lambda b, c: (jnp.maximum(b * per + cidx(c) * (SSD_ROWS // CONV_HALO) - 1, 0), 1)),
        pl.BlockSpec((SSD_ROWS, GROUP_W), lambda b, c: (row(b, c), 1)),
        pl.BlockSpec((SSD_ROWS, GROUP_W), lambda b, c: (row(b, c), 2)),
        pl.BlockSpec((SSD_ROWS, 128), lambda b, c: (row(b, c), OFF_DT // 128)),
    ]
    return specs, row, cidx, ns


def _const_spec(shape):
    return pl.BlockSpec(shape, lambda b, c: (0,) * len(shape))


def _ssd_fwd(proj, conv_w, conv_b, dt_bias, a_log, dskip_e, g_ssd, nb, seq):
    specs, row, cidx, ns = _ssd_specs(nb, seq, reverse=False)

    def body(uxbc_ref, halo_ref, z0_ref, z1_ref, udt_ref, cw_ref, cb_ref, dtb_ref, alog_ref, dsk_ref, gs_ref,
             yssd_ref, yssm_ref, hprev_ref, pre_ref, h_ref, yd_ref):
        c = pl.program_id(1)

        @pl.when(c == 0)
        def _():
            h_ref[...] = jnp.zeros_like(h_ref)

        for sub in range(SSD_SUB):
            rows = slice(sub * CHUNK, (sub + 1) * CHUNK)
            if sub == 0:
                halo, first = halo_ref[...], c == 0
            else:
                halo, first = uxbc_ref[sub * CHUNK - CONV_HALO:sub * CHUNK, :], False
            pre = _conv_pre(uxbc_ref[rows, :], halo, cw_ref[...], cb_ref[...], first)
            pre_ref[rows, :] = pre
            r = _chunk_terms(pre, udt_ref[rows, :], dtb_ref[...], alog_ref[...])
            xbc = r["xbc"]
            xs = xbc[:, :SSD_INNER]
            xdt = xs * r["dt_e"]
            xdt_b = xdt.astype(BF)
            xdo_b = (xdt * r["d_out"]).astype(BF)
            hprev_ref[0, sub] = h_ref[...]
            for g in range(2):
                gs = slice(g * GROUP_W, (g + 1) * GROUP_W)
                bg = xbc[:, SSD_INNER + g * SSD_STATE:SSD_INNER + (g + 1) * SSD_STATE].astype(BF)
                cg = xbc[:, SSD_INNER + (2 + g) * SSD_STATE:SSD_INNER + (3 + g) * SSD_STATE].astype(BF)
                scores = _nt(cg, bg)
                hg = h_ref[g]
                y_off = _nn(cg, hg.astype(BF)) * r["e_a"][:, gs]
                for hh in range(8):
                    h = g * 8 + hh
                    hs = slice(h * SSD_HEAD_DIM, (h + 1) * SSD_HEAD_DIM)
                    m = (scores * _head_decay(r, h)).astype(BF)
                    yd_ref[sub, :, hs] = _nn(m, xdt_b[:, hs])
                h_ref[g] = hg * r["c_dec"][:, gs] + _tn(bg, xdo_b[:, gs])
                y = yd_ref[sub, :, gs] + y_off + dsk_ref[:, gs] * xs[:, gs]
                yssm_ref[rows, gs] = y
                zg = (z0_ref if g == 0 else z1_ref)[rows, :]
                yg = y * (zg * _sigmoid(zg))
                rg = lax.rsqrt(jnp.mean(yg * yg, -1, keepdims=True) + EPS)
                yssd_ref[rows, gs] = (yg * rg * gs_ref[:, gs]).astype(BF)

    t = nb * seq
    return pl.pallas_call(
        body, name="ssd_fwd", grid=(nb, ns),
        in_specs=specs + [_const_spec((4, CONV_CH)), _const_spec((1, CONV_CH)), _const_spec((1, SSD_HEADS)),
                          _const_spec((1, SSD_HEADS)), _const_spec((1, SSD_INNER)), _const_spec((1, SSD_INNER))],
        out_specs=[pl.BlockSpec((SSD_ROWS, SSD_INNER), lambda b, c: (row(b, c), 0)),
                   pl.BlockSpec((SSD_ROWS, SSD_INNER), lambda b, c: (row(b, c), 0)),
                   pl.BlockSpec((1, SSD_SUB, 2, SSD_STATE, GROUP_W), lambda b, c: (b, c, 0, 0, 0)),
                   pl.BlockSpec((SSD_ROWS, CONV_CH), lambda b, c: (row(b, c), 0))],
        out_shape=[_out((t, SSD_INNER), BF), _out((t, SSD_INNER), F32),
                   _out((nb, seq // CHUNK, 2, SSD_STATE, GROUP_W), F32), _out((t, CONV_CH), F32)],
        scratch_shapes=[pltpu.VMEM((2, SSD_STATE, GROUP_W), F32), pltpu.VMEM((SSD_SUB, CHUNK, SSD_INNER), F32)],
        compiler_params=_cp(("arbitrary", "arbitrary"), 56),
    )(*_pin(proj, proj, proj, proj, proj, conv_w, conv_b, dt_bias, a_log, dskip_e, g_ssd))


def _out_proj(y_pool, y_ssd, w_out, x, mod3, g_mlp, seq):
    t, d = x.shape
    tm = 512
    tps = seq // tm if seq >= tm else 1
    tm = min(tm, seq)

    def body(yp_ref, ys_ref, w_ref, x_ref, mod_ref, g_ref, h1_ref, o_ref, u2_ref):
        for rows in _sub_rows(tm):
            o = _nn(jnp.concatenate([yp_ref[rows, :], ys_ref[rows, :]], 1), w_ref[...])
            o_ref[rows, :] = o.astype(BF)
            h1 = x_ref[rows, :] + mod_ref[0, 2:3, :] * o
            h1_ref[rows, :] = h1
            r = lax.rsqrt(jnp.mean(h1 * h1, -1, keepdims=True) + EPS)
            u2_ref[rows, :] = ((h1 * r * g_ref[...]) * (1.0 + mod_ref[0, 4:5, :]) + mod_ref[0, 3:4, :]).astype(BF)

    row = lambda i: (i, 0)
    return pl.pallas_call(
        body, name="out_proj", grid=(t // tm,),
        in_specs=[pl.BlockSpec((tm, POOL_WIDTH), row), pl.BlockSpec((tm, SSD_INNER), row),
                  _RESIDENT, pl.BlockSpec((tm, d), row),
                  pl.BlockSpec((1, N_MOD, d), lambda i: (i // tps, 0, 0)), pl.BlockSpec((1, d), lambda i: (0, 0))],
        out_specs=[pl.BlockSpec((tm, d), row)] * 3,
        out_shape=[_out((t, d), F32), _out((t, d), BF), _out((t, d), BF)],
        compiler_params=_cp(("parallel",), 48))(*_pin(y_pool, y_ssd), w_out, *_pin(x, mod3, g_mlp))


def _mlp_up(u2, w_up4):
    t, d = u2.shape
    tm = min(1024, t)
    nk, _, cols = w_up4.shape

    def body(u_ref, w_ref, a_ref):
        a_ref[...] = _nn(u_ref[...], w_ref[pl.program_id(1)]).astype(BF)

    return pl.pallas_call(
        body, name="mlp_up", grid=(t // tm, nk),
        in_specs=[pl.BlockSpec((tm, d), lambda i, k: (i, 0)), _RESIDENT],
        out_specs=pl.BlockSpec((tm, cols), lambda i, k: (i, k)),
        out_shape=_out((t, nk * cols), BF),
        compiler_params=_cp(("parallel", "parallel"), 32))(*_pin(u2), w_up4)


def _mlp_down_loss(a_up, w_down, h1, mod3, g_final, target, seq):
    t, d = h1.shape
    nb = t // seq
    tm = min(ROW_TILE, seq)
    tps = seq // tm

    def body(a_ref, w_ref, h1_ref, mod_ref, g_ref, tg_ref, ddn_ref, dh2_ref, sq_ref, gg_ref, dgf_ref):
        i = pl.program_id(0)

        @pl.when(i == 0)
        def _():
            sq_ref[...] = jnp.zeros_like(sq_ref)
            gg_ref[...] = jnp.zeros_like(gg_ref)

        @pl.when(i % tps == 0)
        def _():
            dgf_ref[...] = jnp.zeros_like(dgf_ref)

        gate = mod_ref[0, 5:6, :]
        sq = gg = dgf = 0.0
        for rows in _sub_rows(tm):
            f = jnp.square(jnp.maximum(a_ref[rows, :], 0))
            dn = _nn(f, w_ref[...])
            h2 = h1_ref[rows, :] + gate * dn
            r = lax.rsqrt(jnp.mean(h2 * h2, -1, keepdims=True) + EPS)
            hh = h2 * r
            err = hh * g_ref[...] - tg_ref[rows, :]
            dy = err * (1.0 / d)
            dhat = dy * g_ref[...]
            dh2 = r * (dhat - hh * jnp.mean(dhat * hh, -1, keepdims=True))
            dh2_ref[rows, :] = dh2
            ddn_ref[rows, :] = (dh2 * gate).astype(BF)
            sq = sq + jnp.sum(err * err, 0, keepdims=True)
            gg = gg + jnp.sum(dy * hh, 0, keepdims=True)
            dgf = dgf + jnp.sum(dh2 * dn, 0, keepdims=True)
        sq_ref[...] += sq
        gg_ref[...] += gg
        dgf_ref[0] += dgf

    row = lambda i: (i, 0)
    vec = pl.BlockSpec((1, d), lambda i: (0, 0))
    return pl.pallas_call(
        body, name="mlp_down_loss", grid=(t // tm,),
        in_specs=[pl.BlockSpec((tm, D_FF), row), _RESIDENT, pl.BlockSpec((tm, d), row),
                  pl.BlockSpec((1, N_MOD, d), lambda i: (i // tps, 0, 0)), vec, pl.BlockSpec((tm, d), row)],
        out_specs=[pl.BlockSpec((tm, d), row), pl.BlockSpec((tm, d), row), vec, vec,
                   pl.BlockSpec((1, 1, d), lambda i: (i // tps, 0, 0))],
        out_shape=[_out((t, d), BF), _out((t, d), F32), _out((1, d), F32),
                   _out((1, d), F32), _out((nb, 1, d), F32)],
        compiler_params=_cp(("arbitrary",), 44))(*_pin(a_up), w_down, *_pin(h1, mod3, g_final, target))


def _tn_matmul(a, b, tk, tn, name, square_relu=False, out3=False):
    t, kdim = a.shape
    ndim = b.shape[1]

    def body(a_ref, b_ref, o_ref):
        av = a_ref[...]
        if square_relu:
            av = jnp.square(jnp.maximum(av, 0))
        res = _tn(av, b_ref[...]).astype(BF)
        if out3:
            o_ref[0] = res
        else:
            o_ref[...] = res

    if out3:
        out_spec = pl.BlockSpec((1, tk, tn), lambda j, i: (j, i, 0))
        out_shape = _out((ndim // tn, kdim, tn), BF)
    else:
        out_spec = pl.BlockSpec((tk, tn), lambda j, i: (i, j))
        out_shape = _out((kdim, ndim), BF)
    return pl.pallas_call(
        body, name=name, grid=(ndim // tn, kdim // tk),
        in_specs=[pl.BlockSpec((t, tk), lambda j, i: (0, i)), pl.BlockSpec((t, tn), lambda j, i: (0, j))],
        out_specs=out_spec, out_shape=out_shape,
        compiler_params=_cp(("parallel", "parallel"), 56))(*_pin(a, b))


def _mlp_down_bwd(d_dn, w_down4, a_up, token):
    t, d = d_dn.shape
    tm = min(1024, t)
    nk, rows, _ = w_down4.shape

    def body(g_ref, w_ref, a_ref, tok_ref, o_ref):
        df = _nt(g_ref[...], w_ref[pl.program_id(1)])
        o_ref[...] = (df * (2.0 * jnp.maximum(a_ref[...], 0).astype(F32))).astype(BF)

    return pl.pallas_call(
        body, name="mlp_down_bwd", grid=(t // tm, nk),
        in_specs=[pl.BlockSpec((tm, d), lambda i, k: (i, 0)), _RESIDENT,
                  pl.BlockSpec((tm, rows), lambda i, k: (i, k)), _token_spec()],
        out_specs=pl.BlockSpec((tm, rows), lambda i, k: (i, k)),
        out_shape=_out((t, nk * rows), BF),
        compiler_params=_cp(("parallel", "parallel"), 32))(*_pin(d_dn), w_down4, *_pin(a_up, token))


def _mlp_up_bwd(d_a, w_up4, h1, dh2, o, mod3, g_mlp, seq, token):
    t, d = h1.shape
    nb = t // seq
    tm = min(ROW_TILE, seq)
    tps = seq // tm
    nk = w_up4.shape[0]
    cols = w_up4.shape[2]

    def body(da_ref, w_ref, h1_ref, dh2_ref, o_ref, mod_ref, g_ref, tok_ref, dh1_ref, do_ref, acc_ref, gg_ref):
        i = pl.program_id(0)

        @pl.when(i == 0)
        def _():
            gg_ref[...] = jnp.zeros_like(gg_ref)

        @pl.when(i % tps == 0)
        def _():
            acc_ref[...] = jnp.zeros_like(acc_ref)

        gg = a_shift = a_scale = a_gate = 0.0
        for rows in _sub_rows(tm):
            du = _nt(da_ref[rows, 0:cols], w_ref[0])
            for k in range(1, nk):
                du = du + _nt(da_ref[rows, k * cols:(k + 1) * cols], w_ref[k])
            h1 = h1_ref[rows, :]
            r = lax.rsqrt(jnp.mean(h1 * h1, -1, keepdims=True) + EPS)
            hh = h1 * r
            n2 = hh * g_ref[...]
            dn2 = du * (1.0 + mod_ref[0, 4:5, :])
            dhat = dn2 * g_ref[...]
            dh1 = dh2_ref[rows, :] + r * (dhat - hh * jnp.mean(dhat * hh, -1, keepdims=True))
            dh1_ref[rows, :] = dh1
            do_ref[rows, :] = (dh1 * mod_ref[0, 2:3, :]).astype(BF)
            gg = gg + jnp.sum(dn2 * hh, 0, keepdims=True)
            a_shift = a_shift + jnp.sum(du, 0, keepdims=True)
            a_scale = a_scale + jnp.sum(du * n2, 0, keepdims=True)
            a_gate = a_gate + jnp.sum(dh1 * o_ref[rows, :].astype(F32), 0, keepdims=True)
        gg_ref[...] += gg
        acc_ref[0, 0:1, :] += a_shift
        acc_ref[0, 1:2, :] += a_scale
        acc_ref[0, 2:3, :] += a_gate

    row = lambda i: (i, 0)
    vec = pl.BlockSpec((1, d), lambda i: (0, 0))
    return pl.pallas_call(
        body, name="mlp_up_bwd", grid=(t // tm,),
        in_specs=[pl.BlockSpec((tm, D_FF), row), _RESIDENT, pl.BlockSpec((tm, d), row),
                  pl.BlockSpec((tm, d), row), pl.BlockSpec((tm, d), row),
                  pl.BlockSpec((1, N_MOD, d), lambda i: (i // tps, 0, 0)), vec, _token_spec()],
        out_specs=[pl.BlockSpec((tm, d), row), pl.BlockSpec((tm, d), row),
                   pl.BlockSpec((1, 8, d), lambda i: (i // tps, 0, 0)), vec],
        out_shape=[_out((t, d), F32), _out((t, d), BF),
                   _out((nb, 8, d), F32), _out((1, d), F32)],
        compiler_params=_cp(("arbitrary",), 44))(*_pin(d_a), w_up4, *_pin(h1, dh2, o, mod3, g_mlp, token))


def _out_proj_bwd(d_o, w_out, token):
    t, d = d_o.shape
    tm = min(512, t)

    def body(g_ref, w_ref, tok_ref, dp_ref, ds_ref):
        gv = g_ref[...]
        dp_ref[...] = _nt(gv, w_ref[0:POOL_WIDTH, :])
        ds_ref[...] = _nt(gv, w_ref[POOL_WIDTH:, :])

    row = lambda i: (i, 0)
    return pl.pallas_call(
        body, name="out_proj_bwd", grid=(t // tm,),
        in_specs=[pl.BlockSpec((tm, d), row), _RESIDENT, _token_spec()],
        out_specs=[pl.BlockSpec((tm, POOL_WIDTH), row), pl.BlockSpec((tm, SSD_INNER), row)],
        out_shape=[_out((t, POOL_WIDTH), F32), _out((t, SSD_INNER), F32)],
        compiler_params=_cp(("parallel",), 32))(*_pin(d_o), w_out, *_pin(token))


def _pool_bwd(d_ypool, p, w_pool, pool_scale, nb, seq):
    ts = _pool_tile(seq)
    nt = seq // ts
    hb = ts // HALO
    last_block = nb * seq // HALO - 1

    def body(dy_ref, halo_ref, p_ref, wp_ref, ps_ref, du_ref, gw_ref, gs_ref):
        b = pl.program_id(0)
        i = pl.program_id(1)

        @pl.when((b == 0) & (i == 0))
        def _():
            gw_ref[...] = jnp.zeros_like(gw_ref)
            gs_ref[...] = jnp.zeros_like(gs_ref)

        halo = jnp.where(i == nt - 1, 0.0, halo_ref[...])
        dy = dy_ref[...]
        ext = jnp.concatenate([dy, halo], 0)
        tpos = i * ts + _iota((ts + HALO, 1), 0)
        n_ext = ts + HALO
        for g, w in enumerate(POOL_WINDOWS):
            gs = slice(g * POOL_GROUP, (g + 1) * POOL_GROUP)
            wg = wp_ref[g].astype(BF)
            pg = p_ref[:, gs]
            pw = _nn(pg, wg)
            gs_ref[:, gs] += jnp.sum(dy[:, gs] * pw, 0, keepdims=True)
            dpw = (ext[:, gs] * ps_ref[:, gs]).astype(BF)
            gw_ref[g] += _tn(pg, dpw[:ts])
            dp = _nt(dpw, wg)
            cnt = jnp.minimum(tpos + 1, w).astype(F32)
            s = dp / cnt
            sh = 1
            while sh < w:
                s = s + pltpu.roll(s, n_ext - sh, 0)
                sh *= 2
            du_ref[:, gs] = (s[:ts] - dp[:ts]).astype(BF)

    return pl.pallas_call(
        body, name="pool_bwd", grid=(nb, nt),
        in_specs=[pl.BlockSpec((ts, POOL_WIDTH), lambda b, i: (b * nt + i, 0)),
                  pl.BlockSpec((HALO, POOL_WIDTH), lambda b, i: (jnp.minimum((b * nt + i + 1) * hb, last_block), 0)),
                  pl.BlockSpec((ts, POOL_WIDTH), lambda b, i: (b * nt + i, 0)),
                  pl.BlockSpec((4, POOL_GROUP, POOL_GROUP), lambda b, i: (0, 0, 0)),
                  pl.BlockSpec((1, POOL_WIDTH), lambda b, i: (0, 0))],
        out_specs=[pl.BlockSpec((ts, POOL_WIDTH), lambda b, i: (b * nt + i, 0)),
                   pl.BlockSpec((4, POOL_GROUP, POOL_GROUP), lambda b, i: (0, 0, 0)),
                   pl.BlockSpec((1, POOL_WIDTH), lambda b, i: (0, 0))],
        out_shape=[_out((nb * seq, POOL_WIDTH), BF), _out((4, POOL_GROUP, POOL_GROUP), F32),
                   _out((1, POOL_WIDTH), F32)],
        compiler_params=_cp(("arbitrary", "arbitrary"), 32))(*_pin(d_ypool, d_ypool, p, w_pool, pool_scale))


def _ssd_bwd(proj, pre, d_yssd, yssm, h_prev, dt_bias, a_log, dskip_e, g_ssd, nb, seq):
    specs, row, cidx, ns = _ssd_specs(nb, seq, reverse=True)
    specs = specs[2:]

    def body(z0_ref, z1_ref, udt_ref, pre_ref, dys_ref, yssm_ref, hprev_ref,
             dtb_ref, alog_ref, dsk_ref, gs_ref,
             dz_ref, dpre_ref, dudt_ref, ggs_ref, gdsk_ref, ga_ref, gdtb_ref,
             g_ref, dxdt_ref, dyv_ref):
        b = pl.program_id(0)
        c = pl.program_id(1)

        @pl.when(c == 0)
        def _():
            g_ref[...] = jnp.zeros_like(g_ref)

        @pl.when((b == 0) & (c == 0))
        def _():
            ggs_ref[...] = jnp.zeros_like(ggs_ref)
            gdsk_ref[...] = jnp.zeros_like(gdsk_ref)
            ga_ref[...] = jnp.zeros_like(ga_ref)
            gdtb_ref[...] = jnp.zeros_like(gdtb_ref)

        for sub in reversed(range(SSD_SUB)):
            chunk(sub, z0_ref, z1_ref, udt_ref, pre_ref, dys_ref, yssm_ref, hprev_ref, dtb_ref, alog_ref, dsk_ref, gs_ref,
                  dz_ref, dpre_ref, dudt_ref, ggs_ref, gdsk_ref, ga_ref, gdtb_ref, g_ref, dxdt_ref.at[sub], dyv_ref.at[sub])

    def chunk(sub, z0_ref, z1_ref, udt_ref, pre_ref, dys_ref, yssm_ref, hprev_ref,
              dtb_ref, alog_ref, dsk_ref, gs_ref,
              dz_ref, dpre_ref, dudt_ref, ggs_ref, gdsk_ref, ga_ref, gdtb_ref,
              g_ref, dxdt_ref, dyv_ref):
        rows = slice(sub * CHUNK, (sub + 1) * CHUNK)
        r = _chunk_terms(pre_ref[rows, :], udt_ref[rows, :], dtb_ref[...], alog_ref[...])
        xbc = r["xbc"]
        xs = xbc[:, :SSD_INNER]
        dt_e = r["dt_e"]
        xdt = xs * dt_e
        xdt_b = xdt.astype(BF)
        reduce_m = _head_reduce_matrix(GROUP_W, 8)

        def head_sums(v):
            return _nn(v.astype(BF), reduce_m)

        onehot16 = lambda h: (_iota((1, SSD_HEADS), 1) == h).astype(F32)
        onecol16 = lambda h: (_iota((SSD_HEADS, 1), 0) == h).astype(F32)

        d_acum = jnp.zeros((CHUNK, SSD_HEADS), F32)
        d_acum_t = jnp.zeros((SSD_HEADS, CHUNK), F32)
        d_alast = jnp.zeros((1, SSD_HEADS), F32)
        place8 = lambda g: (_iota((8, SSD_HEADS), 1) == _iota((8, SSD_HEADS), 0) + 8 * g).astype(BF)
        d_b, d_c = [], []
        for g in range(2):
            gs = slice(g * GROUP_W, (g + 1) * GROUP_W)
            zg = (z0_ref if g == 0 else z1_ref)[rows, :]
            sz = _sigmoid(zg)
            silu_z = zg * sz
            ys = yssm_ref[rows, gs]
            yg = ys * silu_z
            rg = lax.rsqrt(jnp.mean(yg * yg, -1, keepdims=True) + EPS)
            yh = yg * rg
            dys = dys_ref[rows, gs]
            ggs_ref[:, gs] += jnp.sum(dys * yh, 0, keepdims=True)
            dyh = dys * gs_ref[:, gs]
            dyg = rg * (dyh - yh * jnp.mean(dyh * yh, -1, keepdims=True))
            dy = dyg * silu_z
            dz_ref[rows, gs] = (dyg * ys * (sz * (1.0 + zg * (1.0 - sz)))).astype(BF)
            gdsk_ref[:, gs] += jnp.sum(dy * xs[:, gs], 0, keepdims=True)
            dyv_ref[:, gs] = dy
            dy_b = dy.astype(BF)

            bg = xbc[:, SSD_INNER + g * SSD_STATE:SSD_INNER + (g + 1) * SSD_STATE].astype(BF)
            cg = xbc[:, SSD_INNER + (2 + g) * SSD_STATE:SSD_INNER + (3 + g) * SSD_STATE].astype(BF)
            scores = _nt(cg, bg)
            hg = hprev_ref[0, sub, g]
            hg_b = hg.astype(BF)
            gg = g_ref[g]
            gg_b = gg.astype(BF)
            e_a = r["e_a"][:, gs]
            d_out = r["d_out"][:, gs]
            c_dec = r["c_dec"][:, gs]
            zc = _nn(cg, hg_b)
            wv = e_a * dy
            wv_b = wv.astype(BF)
            da_g = head_sums(wv * zc)
            dcg = _nt(wv_b, hg_b)
            d_hprev = _tn(cg, wv_b)
            vg = _nn(bg, gg_b)
            dxdt_g = d_out * vg
            dd_out = head_sums(xdt[:, gs] * vg)
            dbg = _nt((xdt[:, gs] * d_out).astype(BF), gg_b)
            dcd = _exact_nn(jnp.sum(gg * hg, 0, keepdims=True), reduce_m)
            d_out8 = jnp.exp(r["acum"][CHUNK - 1:CHUNK, 8 * g:8 * g + 8] - r["acum"][:, 8 * g:8 * g + 8])
            c_dec8 = jnp.exp(r["acum"][CHUNK - 1:CHUNK, 8 * g:8 * g + 8])
            t8 = dd_out * d_out8
            d_alast = d_alast + _exact_nn(jnp.sum(t8, 0, keepdims=True) + dcd * c_dec8, place8(g))
            d_acum = d_acum + _exact_nn(da_g - t8, place8(g))
            dsc = jnp.zeros((CHUNK, CHUNK), F32)
            for hh in range(8):
                h = g * 8 + hh
                hs = slice(h * SSD_HEAD_DIM, (h + 1) * SSD_HEAD_DIM)
                lam = _head_decay(r, h)
                m = scores * lam
                dyh_b = dy_b[:, hh * SSD_HEAD_DIM:(hh + 1) * SSD_HEAD_DIM]
                dm = _nt(dyh_b, xdt_b[:, hs])
                tm_ = dm * m
                d_acum = d_acum + jnp.sum(tm_, 1, keepdims=True) * onehot16(h)
                d_acum_t = d_acum_t + onecol16(h) * jnp.sum(tm_, 0, keepdims=True)
                dsc = dsc + dm * lam
                dxdt_ref[:, hs] = _tn(m.astype(BF), dyh_b) + dxdt_g[:, hh * SSD_HEAD_DIM:(hh + 1) * SSD_HEAD_DIM]
            dsc_b = dsc.astype(BF)
            d_c.append(dcg + _nn(dsc_b, bg))
            d_b.append(dbg + _tn(dsc_b, cg))
            g_ref[g] = d_hprev + c_dec * gg

        eye = (_iota((CHUNK, CHUNK), 0) == _iota((CHUNK, CHUNK), 1)).astype(BF)
        d_acum = d_acum - _exact_nt_left(eye, d_acum_t)
        is_last = (_iota((CHUNK, 1), 0) == CHUNK - 1).astype(F32)
        d_acum = d_acum + is_last * d_alast
        triu = (_iota((CHUNK, CHUNK), 0) <= _iota((CHUNK, CHUNK), 1)).astype(BF)
        d_da = _exact_nn_left(triu, d_acum)
        dt = r["dt"]
        ga_ref[...] += jnp.sum(d_da * dt, 0, keepdims=True)
        dxdt = dxdt_ref[...]
        reduce16 = _head_reduce_matrix(SSD_INNER, SSD_HEADS)
        d_dt = d_da * r["a"] + _nn((dxdt * xs).astype(BF), reduce16)
        d_udt = d_dt * _sigmoid(r["dtp"])
        gdtb_ref[...] += jnp.sum(d_udt, 0, keepdims=True)
        dudt_ref[rows, :] = jnp.zeros((CHUNK, dudt_ref.shape[1]), BF)
        dudt_ref[rows, 0:SSD_HEADS] = d_udt.astype(BF)
        pre, sg = r["pre"], r["sg"]
        dsilu = sg * (1.0 + pre * (1.0 - sg))
        dpre_ref[rows, 0:SSD_INNER] = (dsk_ref[...] * dyv_ref[...] + dxdt * dt_e) * dsilu[:, 0:SSD_INNER]
        for g in range(2):
            bs = slice(SSD_INNER + g * SSD_STATE, SSD_INNER + (g + 1) * SSD_STATE)
            cs = slice(SSD_INNER + (2 + g) * SSD_STATE, SSD_INNER + (3 + g) * SSD_STATE)
            dpre_ref[rows, bs] = d_b[g] * dsilu[:, bs]
            dpre_ref[rows, cs] = d_c[g] * dsilu[:, cs]

    t = nb * seq
    vec = _const_spec((1, SSD_INNER))
    small = _const_spec((1, SSD_HEADS))
    return pl.pallas_call(
        body, name="ssd_bwd", grid=(nb, ns),
        in_specs=specs + [pl.BlockSpec((SSD_ROWS, CONV_CH), lambda b, c: (row(b, c), 0)),
                          pl.BlockSpec((SSD_ROWS, SSD_INNER), lambda b, c: (row(b, c), 0)),
                          pl.BlockSpec((SSD_ROWS, SSD_INNER), lambda b, c: (row(b, c), 0)),
                          pl.BlockSpec((1, SSD_SUB, 2, SSD_STATE, GROUP_W), lambda b, c: (b, cidx(c), 0, 0, 0)),
                          small, small, vec, vec],
        out_specs=[pl.BlockSpec((SSD_ROWS, SSD_INNER), lambda b, c: (row(b, c), 0)),
                   pl.BlockSpec((SSD_ROWS, CONV_CH), lambda b, c: (row(b, c), 0)),
                   pl.BlockSpec((SSD_ROWS, 128), lambda b, c: (row(b, c), 0)),
                   vec, vec, small, small],
        out_shape=[_out((t, SSD_INNER), BF), _out((t, CONV_CH), F32),
                   _out((t, 128), BF), _out((1, SSD_INNER), F32),
                   _out((1, SSD_INNER), F32), _out((1, SSD_HEADS), F32),
                   _out((1, SSD_HEADS), F32)],
        scratch_shapes=[pltpu.VMEM((2, SSD_STATE, GROUP_W), F32), pltpu.VMEM((SSD_SUB, CHUNK, SSD_INNER), F32),
                        pltpu.VMEM((SSD_SUB, CHUNK, SSD_INNER), F32)],
        compiler_params=_cp(("arbitrary", "arbitrary"), 56),
    )(*_pin(proj, proj, proj, pre, d_yssd, yssm, h_prev, dt_bias, a_log, dskip_e, g_ssd))


def _grad_w_out(y_pool, y_ssd, d_o):
    t, d = d_o.shape
    tk = POOL_WIDTH
    n_s = SSD_INNER // tk

    def body(p_ref, s_ref, g_ref, o_ref):
        i = pl.program_id(0)

        @pl.when(i == 0)
        def _():
            o_ref[...] = _tn(p_ref[...], g_ref[...]).astype(BF)

        @pl.when(i > 0)
        def _():
            o_ref[...] = _tn(s_ref[...], g_ref[...]).astype(BF)

    return pl.pallas_call(
        body, name="grad_w_out", grid=(1 + n_s,),
        in_specs=[pl.BlockSpec((t, tk), lambda i: (0, 0)), pl.BlockSpec((t, tk), lambda i: (0, jnp.maximum(i - 1, 0))),
                  pl.BlockSpec((t, d), lambda i: (0, 0))],
        out_specs=pl.BlockSpec((tk, d), lambda i: (i, 0)),
        out_shape=_out((POOL_WIDTH + SSD_INNER, d), BF),
        compiler_params=_cp(("parallel",), 56))(*_pin(y_pool, y_ssd, d_o))


def _grad_w_in_t(d_upool, d_z, d_uxbc, d_udt, u1):
    t, d = u1.shape
    tk = 512
    n_z, n_x = SSD_INNER // tk, CONV_CH // tk

    def body(p_ref, z_ref, x_ref, dt_ref, u_ref, o_ref):
        i = pl.program_id(0)

        @pl.when(i == 0)
        def _():
            o_ref[...] = _tn(p_ref[...], u_ref[...]).astype(BF)

        @pl.when((i >= 1) & (i < 1 + n_z))
        def _():
            o_ref[...] = _tn(z_ref[...], u_ref[...]).astype(BF)

        @pl.when((i >= 1 + n_z) & (i < 1 + n_z + n_x))
        def _():
            o_ref[...] = _tn(x_ref[...], u_ref[...]).astype(BF)

        @pl.when(i == 1 + n_z + n_x)
        def _():
            o_ref[0:128, :] = _tn(dt_ref[...], u_ref[...]).astype(BF)

    return pl.pallas_call(
        body, name="grad_w_in", grid=(2 + n_z + n_x,),
        in_specs=[pl.BlockSpec((t, tk), lambda i: (0, 0)),
                  pl.BlockSpec((t, tk), lambda i: (0, jnp.clip(i - 1, 0, n_z - 1))),
                  pl.BlockSpec((t, tk), lambda i: (0, jnp.clip(i - 1 - n_z, 0, n_x - 1))),
                  pl.BlockSpec((t, 128), lambda i: (0, 0)), pl.BlockSpec((t, d), lambda i: (0, 0))],
        out_specs=pl.BlockSpec((tk, d), lambda i: (i, 0)),
        out_shape=_out((IN_PAD, d), BF),
        compiler_params=_cp(("parallel",), 56))(*_pin(d_upool, d_z, d_uxbc, d_udt, u1))


def _conv_bwd(d_pre, proj, conv_w, nb, seq):
    ts = min(256, seq)
    nt = seq // ts
    hb = ts // CONV_HALO
    last_block = nb * seq // CONV_HALO - 1
    n_ext = CHUNK + CONV_HALO

    def body(dp_ref, dnext_ref, u_ref, cw_ref, du_ref, gw_ref, gb_ref):
        b = pl.program_id(0)
        i = pl.program_id(1)

        @pl.when((b == 0) & (i == 0))
        def _():
            gw_ref[...] = jnp.zeros_like(gw_ref)
            gb_ref[...] = jnp.zeros_like(gb_ref)

        for c0 in range(0, CONV_CH, 128):
            cs = slice(c0, c0 + 128)
            cw = cw_ref[:, cs]
            gw = [0.0] * 4
            gb = 0.0
            for r0 in range(0, ts, CHUNK):
                dp = dp_ref[r0:r0 + CHUNK, cs]
                u = u_ref[r0:r0 + CHUNK, cs]
                if r0 + CHUNK < ts:
                    below = dp_ref[r0 + CHUNK:r0 + CHUNK + CONV_HALO, cs]
                else:
                    below = jnp.where(i == nt - 1, 0.0, dnext_ref[:, cs])
                ext_d = jnp.concatenate([dp, below], 0)
                du = dp * cw[3:4]
                gw[3] = gw[3] + jnp.sum(dp * u, 0, keepdims=True)
                for k in (2, 1, 0):
                    shifted = pltpu.roll(ext_d, n_ext - (3 - k), 0)[:CHUNK]
                    du = du + shifted * cw[k:k + 1]
                    gw[k] = gw[k] + jnp.sum(shifted * u, 0, keepdims=True)
                gb = gb + jnp.sum(dp, 0, keepdims=True)
                du_ref[r0:r0 + CHUNK, cs] = du.astype(BF)
            for k in range(4):
                gw_ref[k:k + 1, cs] += gw[k]
            gb_ref[:, cs] += gb

    return pl.pallas_call(
        body, name="conv_bwd", grid=(nb, nt),
        in_specs=[pl.BlockSpec((ts, CONV_CH), lambda b, i: (b * nt + i, 0)),
                  pl.BlockSpec((CONV_HALO, CONV_CH), lambda b, i: (jnp.minimum((b * nt + i + 1) * hb, last_block), 0)),
                  pl.BlockSpec((ts, CONV_CH), lambda b, i: (b * nt + i, 1)),
                  pl.BlockSpec((4, CONV_CH), lambda b, i: (0, 0))],
        out_specs=[pl.BlockSpec((ts, CONV_CH), lambda b, i: (b * nt + i, 0)),
                   pl.BlockSpec((8, CONV_CH), lambda b, i: (0, 0)), pl.BlockSpec((1, CONV_CH), lambda b, i: (0, 0))],
        out_shape=[_out((nb * seq, CONV_CH), BF), _out((8, CONV_CH), F32),
                   _out((1, CONV_CH), F32)],
        compiler_params=_cp(("arbitrary", "arbitrary"), 48))(*_pin(d_pre, d_pre, proj, conv_w))


def _in_proj_bwd(d_parts, w_in_t, x, dh1, mod3, g_mix, seq, token):
    t, d = x.shape
    nb = t // seq
    tm = min(ROW_TILE, seq)
    tps = seq // tm

    widths = [p.shape[1] for p in d_parts]

    def body(d0_ref, d1_ref, d2_ref, d3_ref, w_ref, x_ref, dh1_ref, mod_ref, g_ref, tok_ref, gx_ref, acc_ref, gg_ref):
        i = pl.program_id(0)

        @pl.when(i == 0)
        def _():
            gg_ref[...] = jnp.zeros_like(gg_ref)

        @pl.when(i % tps == 0)
        def _():
            acc_ref[...] = jnp.zeros_like(acc_ref)

        gg = a_shift = a_scale = 0.0
        for rows in _sub_rows(tm):
            d_cat = jnp.concatenate([p_ref[rows, :] for p_ref in (d0_ref, d1_ref, d2_ref, d3_ref)], 1)
            du = _nn(d_cat, w_ref[...])
            xv = x_ref[rows, :]
            r = lax.rsqrt(jnp.mean(xv * xv, -1, keepdims=True) + EPS)
            hh = xv * r
            n1 = hh * g_ref[...]
            dn1 = du * (1.0 + mod_ref[0, 1:2, :])
            dhat = dn1 * g_ref[...]
            gx_ref[rows, :] = dh1_ref[rows, :] + r * (dhat - hh * jnp.mean(dhat * hh, -1, keepdims=True))
            gg = gg + jnp.sum(dn1 * hh, 0, keepdims=True)
            a_shift = a_shift + jnp.sum(du, 0, keepdims=True)
            a_scale = a_scale + jnp.sum(du * n1, 0, keepdims=True)
        gg_ref[...] += gg
        acc_ref[0, 0:1, :] += a_shift
        acc_ref[0, 1:2, :] += a_scale

    row = lambda i: (i, 0)
    vec = pl.BlockSpec((1, d), lambda i: (0, 0))
    return pl.pallas_call(
        body, name="in_proj_bwd", grid=(t // tm,),
        in_specs=[pl.BlockSpec((tm, wd), row) for wd in widths] +
                 [_RESIDENT, pl.BlockSpec((tm, d), row),
                  pl.BlockSpec((tm, d), row), pl.BlockSpec((1, N_MOD, d), lambda i: (i // tps, 0, 0)), vec, _token_spec()],
        out_specs=[pl.BlockSpec((tm, d), row), pl.BlockSpec((1, 8, d), lambda i: (i // tps, 0, 0)), vec],
        out_shape=[_out((t, d), F32), _out((nb, 8, d), F32),
                   _out((1, d), F32)],
        compiler_params=_cp(("arbitrary",), 40))(*_pin(*d_parts), w_in_t, *_pin(x, dh1, mod3, g_mix, token))


_VEC_LAYOUT = (("g_mix", 1024), ("conv_b", 1536), ("g_ssd", 1024), ("pool_scale", 512), ("g_mlp", 1024),
               ("g_final", 1024), ("dt_bias", 128), ("a_log", 128), ("d_skip_lanes", 1024), ("sq_err", 1024))
_VEC_OFFSET = {}
_off = 0
for _name, _n in _VEC_LAYOUT:
    _VEC_OFFSET[_name] = _off
    _off += _n
_VEC_LANES = _off
_SMALL_PARAMS = ("b_ada", "g_mix", "conv_w", "conv_b", "dt_bias", "a_log", "d_skip", "g_ssd", "w_pool", "pool_scale",
                 "g_mlp", "g_final")


def _pack_vec(parts):
    cols = []
    for name, n in _VEC_LAYOUT:
        v = parts[name]
        if v.shape[1] < n:
            v = jnp.pad(v, ((0, 0), (0, n - v.shape[1])))
        cols.append(v)
    return jnp.concatenate(cols, 1)


def _small_adam(vec_all, wpool_all, convw_all, dmod_all, params):
    names = _SMALL_PARAMS
    nin = 4 + 3 * len(names)

    def body(*refs):
        vec_ref, wp_ref, cw_ref, dm_ref = refs[:4]
        prm = {n: refs[4 + 3 * i:7 + 3 * i] for i, n in enumerate(names)}
        loss_ref = refs[nin]
        outs = {n: refs[nin + 1 + 4 * i:nin + 5 + 4 * i] for i, n in enumerate(names)}
        vsum = vec_ref[0]
        for s in range(1, N_DEV):
            vsum = vsum + vec_ref[s]

        def lanes(name, n):
            off = _VEC_OFFSET[name]
            return vsum[:, off:off + n]

        grads = {n: lanes(n, prm[n][0].shape[1]) for n in ("g_mix", "conv_b", "g_ssd", "pool_scale", "g_mlp", "g_final", "dt_bias")}
        grads["a_log"] = lanes("a_log", SSD_HEADS) * (-jnp.exp(prm["a_log"][0][...]))
        per_lane = jnp.broadcast_to(lanes("d_skip_lanes", SSD_INNER), (8, SSD_INNER))
        grads["d_skip"] = _exact_nn(per_lane, _head_reduce_matrix(SSD_INNER, SSD_HEADS))[0:1]
        gwp = wp_ref[0].astype(F32)
        gcw = cw_ref[0]
        gb = jnp.sum(dm_ref[0], 0, keepdims=True)
        for s in range(1, N_DEV):
            gwp = gwp + wp_ref[s].astype(F32)
            gcw = gcw + cw_ref[s]
            gb = gb + jnp.sum(dm_ref[s], 0, keepdims=True)
        grads["w_pool"] = gwp
        grads["conv_w"] = gcw[0:4]
        grads["b_ada"] = gb
        total = jnp.sum(lanes("sq_err", D_MODEL), 1, keepdims=True) * (0.5 / D_MODEL)
        loss_ref[...] = jnp.broadcast_to(total, loss_ref.shape)
        for n in names:
            w_ref, m_ref, v_ref = prm[n]
            g = grads[n]
            d, m2, v2 = _adam_math(w_ref[...], g, m_ref[...], v_ref[...])
            g_ref, d_ref, m2_ref, v2_ref = outs[n]
            g_ref[...] = g
            d_ref[...] = d
            m2_ref[...] = m2
            v2_ref[...] = v2

    flat = [vec_all, wpool_all, convw_all, dmod_all]
    out_shape = [jax.ShapeDtypeStruct((1, 128), F32)]
    for n in names:
        flat += list(params[n])
        out_shape += [jax.ShapeDtypeStruct(params[n][0].shape, F32)] * 4
    vm = pl.BlockSpec(memory_space=pltpu.VMEM)
    res = pl.pallas_call(body, name="small_adam", out_shape=out_shape, in_specs=[vm] * len(flat),
                         out_specs=[vm] * len(out_shape), compiler_params=_cp(vmem_mb=48))(*flat)
    return res[0], {n: res[1 + 4 * i:5 + 4 * i] for i, n in enumerate(names)}


_WEIGHTS = ("w_ada", "b_ada", "g_mix", "w_in", "conv_w", "conv_b", "dt_bias", "a_log", "d_skip", "g_ssd", "w_pool",
            "pool_scale", "w_out", "g_mlp", "w_up", "w_down", "g_final")


def _local_step(x2, tg2, mod3, seq, w_in_t, first_token, weights_arrived, weights_later, start_reduce, before_last,
                conv_w_full, sp):
    t, d = x2.shape
    nb = t // seq
    dskip_e = jnp.repeat(sp["d_skip"], SSD_HEAD_DIM, axis=1)
    proj, u1 = _in_proj(x2, mod3, sp["g_mix"], w_in_t, seq, first_token)
    y_ssd, yssm, h_prev, pre = _ssd_fwd(proj, conv_w_full, sp["conv_b"], sp["dt_bias"], sp["a_log"], dskip_e, sp["g_ssd"], nb, seq)
    y_pool, p = _pool_fwd(proj, sp["w_pool"], sp["pool_scale"], nb, seq, weights_arrived(y_ssd))
    w_out_f, w_up4, w_down4 = weights_later(y_pool)
    w_down_f = w_down4.reshape(D_FF, d)
    h1, o, u2 = _out_proj(y_pool, y_ssd, w_out_f, x2, mod3, sp["g_mlp"], seq)
    a_up = _mlp_up(u2, w_up4)
    d_dn, dh2, sq, gg_final, d_gf = _mlp_down_loss(a_up, w_down_f, h1, mod3, sp["g_final"], tg2, seq)

    gw_down = _tn_matmul(a_up, d_dn, 512, d, "grad_w_down", square_relu=True)
    tok = start_reduce("w_down", gw_down.reshape(N_CHIPS, D_FF // N_CHIPS, d))
    d_a = _mlp_down_bwd(d_dn, w_down4, a_up, tok)
    gw_up4 = _tn_matmul(u2, d_a, 512, d, "grad_w_up", out3=True)
    tok = start_reduce("w_up", gw_up4)
    dh1, d_o, accf, gg_mlp = _mlp_up_bwd(d_a, w_up4, h1, dh2, o, mod3, sp["g_mlp"], seq, tok)
    gw_out = _grad_w_out(y_pool, y_ssd, d_o)
    tok = start_reduce("w_out", gw_out.reshape(N_CHIPS, gw_out.shape[0] // N_CHIPS, d))
    d_ypool, d_yssd = _out_proj_bwd(d_o, w_out_f, tok)
    d_upool, gw_pool, g_ps = _pool_bwd(d_ypool, p, sp["w_pool"], sp["pool_scale"], nb, seq)
    d_z, d_pre, d_udt, gg_ssd, gdsk, ga, gdtb = _ssd_bwd(proj, pre, d_yssd, yssm, h_prev, sp["dt_bias"], sp["a_log"],
                                                        dskip_e, sp["g_ssd"], nb, seq)
    d_uxbc, gconvw, gconvb = _conv_bwd(d_pre, proj, conv_w_full, nb, seq)
    gw_in_t = _grad_w_in_t(d_upool, d_z, d_uxbc, d_udt, u1)
    shard_rows = IN_WIDTH // N_CHIPS
    tok = start_reduce("w_in", jnp.stack([gw_in_t[k * shard_rows:(k + 1) * shard_rows] for k in range(N_CHIPS)]))
    gx, accm, gg_mix = _in_proj_bwd([d_upool, d_z, d_uxbc, d_udt], w_in_t, x2, dh1, mod3, sp["g_mix"], seq, before_last(tok))

    d_mod = jnp.concatenate([accm[:, 0], accm[:, 1], accf[:, 2], accf[:, 0], accf[:, 1], d_gf[:, 0]], 1)
    vec = _pack_vec({"g_mix": gg_mix, "conv_b": gconvb, "g_ssd": gg_ssd, "pool_scale": g_ps, "g_mlp": gg_mlp,
                     "g_final": gg_final, "dt_bias": gdtb, "a_log": ga, "d_skip_lanes": gdsk, "sq_err": sq})
    return gx, d_mod, vec, gw_pool, gconvw


def kernel(x, c, w_ada, b_ada, g_mix, w_in, conv_w, conv_b, dt_bias, a_log, d_skip, g_ssd, w_pool, pool_scale, w_out, g_mlp, w_up, w_down, g_final, loss_target, m_w_ada, m_b_ada, m_g_mix, m_w_in, m_conv_w, m_conv_b, m_dt_bias, m_a_log, m_d_skip, m_g_ssd, m_w_pool, m_pool_scale, m_w_out, m_g_mlp, m_w_up, m_w_down, m_g_final, v_w_ada, v_b_ada, v_g_mix, v_w_in, v_conv_w, v_conv_b, v_dt_bias, v_a_log, v_d_skip, v_g_ssd, v_w_pool, v_pool_scale, v_w_out, v_g_mlp, v_w_up, v_w_down, v_g_final):
    nb, seq, d = x.shape
    t = nb * seq
    xi, yi, ci = _mesh_pos()
    chip = 2 * xi + yi
    me = 4 * xi + 2 * yi + ci
    ada_cols = w_ada.shape[2]
    conv_cols = conv_w.shape[2]
    in_cols = w_in.shape[2]
    w_in_s, m_w_in_s, v_w_in_s = w_in[0].T, m_w_in[0].T, v_w_in[0].T

    c_send, c_recv, c_src, c_land, c_token = _exchange_start([c, conv_w[0]], ALL_PEERS, "cond_start")
    w_in_b = w_in_s.astype(BF)
    i_send, i_recv, i_src, i_land, in_token = _ici_start(
        [w_in_b], [jax.ShapeDtypeStruct((N_CHIPS,) + w_in_b.shape, BF)], _gather_sent, _gather_landing, "gather_start_w_in",
        after=c_token)
    c_own, c_got = _exchange_wait(c_send, c_recv, c_src, c_land, ALL_PEERS, in_token, "cond_wait")
    c8, convw8 = [lax.dynamic_update_slice(got, mine[None], (me,) + (0,) * mine.ndim) for got, mine in zip(c_got, c_own)]
    c_all = c8.reshape(N_DEV * nb, d)
    conv_w_full = convw8[0::2].transpose(1, 0, 2).reshape(4, N_CHIPS * conv_cols)
    b_shard = lax.dynamic_slice(b_ada, (0, chip * ada_cols), (1, ada_cols))
    mod_part, c_act = _ada_mod(c_all, w_ada[0], b_shard, in_token)
    mod_rows = mod_part.reshape(N_DEV, nb, ada_cols)
    m_send, m_recv, m_src, m_land, _ = _ici_start(
        [mod_rows], [jax.ShapeDtypeStruct((N_CHIPS, nb, ada_cols), F32)], _mod_sent, _mod_landing, "mod_start", after=mod_part)

    later = [w_out[0].astype(BF), w_up[0].astype(BF), w_down[0].astype(BF)]
    in_shard, in_land = _ici_wait(i_send, i_recv, i_src, i_land, [m_src[0]] + later, _gather_sent, _gather_landing,
                                  "gather_wait_w_in")
    (w_in4,) = _gather_finish(in_land, in_shard)
    w_in_t = jnp.pad(w_in4.reshape(N_CHIPS * in_cols, d), ((0, IN_PAD - N_CHIPS * in_cols), (0, 0)))
    mod_mine, mod_land = _ici_wait(m_send, m_recv, m_src, m_land, w_in_t, _mod_sent, _mod_landing, "mod_wait")
    mod_own = lax.dynamic_slice(mod_mine[0], (me, 0, 0), (1, nb, ada_cols))
    mod4 = lax.dynamic_update_slice(mod_land[0], mod_own, (chip, 0, 0))
    mod3 = mod4.transpose(1, 0, 2).reshape(nb, N_MOD, d)
    g_send, g_recv, g_src, g_land, first_token = _ici_start(
        later, [jax.ShapeDtypeStruct((N_CHIPS,) + s.shape, BF) for s in later], _gather_sent, _gather_landing, "gather_start",
        after=w_in4)

    def weights_arrived(after):
        shards, lands = _ici_wait(g_send, g_recv, g_src, g_land, after, _gather_sent, _gather_landing, "gather_wait")
        pending["forward"] = _forward_start(lands, "forward_start") + (shards,)
        return pending["forward"][3]

    def weights_later(after):
        f_send, f_recv, f_land, _, shards = pending["forward"]
        lands = _forward_wait(f_send, f_recv, f_land, after, "forward_wait")
        w_out4, w_up4, w_down4 = [lax.dynamic_update_slice(land, shard[None], (chip, 0, 0)) for land, shard in zip(lands, shards)]
        return w_out4.reshape(N_CHIPS * w_out.shape[1], d), w_up4, w_down4

    pending = {}

    def start_reduce(name, grad4):
        pending[name] = _reduce_start(grad4, "reduce_start_" + name)
        return pending[name][4]

    pos = jnp.stack([ci, chip, me]).astype(jnp.int32)
    early = ("w_out", "w_up", "w_down")

    def summed_half(name, after):
        r_send, r_recv, r_src, r_land, _ = pending[name]
        own, recv = _reduce_wait(r_send, r_recv, r_src, r_land, after, "reduce_wait_" + name)
        return _sum_eight(recv, own, pos)

    def before_last(token):
        pending["early_halves"] = _exchange_start([summed_half(n, token) for n in early], SIBLING, "halves_start_early")
        return pending["early_halves"][4]

    sp = dict(g_mix=g_mix, conv_b=conv_b, dt_bias=dt_bias, a_log=a_log, d_skip=d_skip, g_ssd=g_ssd,
              w_pool=w_pool[0], pool_scale=pool_scale, g_mlp=g_mlp, g_final=g_final.reshape(1, d))
    gx, d_mod, vec, gw_pool, gconvw = _local_step(
        x.reshape(t, d), loss_target.reshape(t, d), mod3, seq, w_in_t, first_token, weights_arrived, weights_later, start_reduce,
        before_last, conv_w_full, sp)

    small_parts = [vec, gw_pool.reshape(4 * POOL_GROUP, POOL_GROUP).astype(BF), gconvw, d_mod]
    s_send, s_recv, s_src, s_land, s_token = _exchange_start(small_parts, ALL_PEERS, "small_start")
    h_send, h_recv, h_src, h_land, h_token = _exchange_start([summed_half("w_in", s_token)], SIBLING, "halves_start")
    e_send, e_recv, e_src, e_land, _ = pending["early_halves"]
    e_own, e_got = _exchange_wait(e_send, e_recv, e_src, e_land, SIBLING, h_token, "halves_wait_early")
    res = {}
    for i, (n, w, m, v) in enumerate((("w_out", w_out, m_w_out, v_w_out), ("w_up", w_up, m_w_up, v_w_up),
                                      ("w_down", w_down, m_w_down, v_w_down))):
        g, dl, m2, v2 = _adam_big(e_own[i], e_got[i], w[0], m[0], v[0], pos)
        res[n] = (g[None], dl[None], m2[None], v2[None])

    s_own, s_got = _exchange_wait(s_send, s_recv, s_src, s_land, ALL_PEERS, res["w_down"][1], "small_wait")
    vec8, wpool8, convw8g, dmod8 = [lax.dynamic_update_slice(got, mine[None], (me,) + (0,) * mine.ndim)
                                    for got, mine in zip(s_got, s_own)]
    convw8s = lax.dynamic_slice(convw8g, (0, 0, chip * conv_cols), (N_DEV, 8, conv_cols))
    m_in = dict(b_ada=m_b_ada, g_mix=m_g_mix, conv_w=m_conv_w[0], conv_b=m_conv_b, dt_bias=m_dt_bias, a_log=m_a_log,
                d_skip=m_d_skip, g_ssd=m_g_ssd, w_pool=m_w_pool.reshape(4 * POOL_GROUP, POOL_GROUP), pool_scale=m_pool_scale,
                g_mlp=m_g_mlp, g_final=m_g_final.reshape(1, d))
    v_in = dict(b_ada=v_b_ada, g_mix=v_g_mix, conv_w=v_conv_w[0], conv_b=v_conv_b, dt_bias=v_dt_bias, a_log=v_a_log,
                d_skip=v_d_skip, g_ssd=v_g_ssd, w_pool=v_w_pool.reshape(4 * POOL_GROUP, POOL_GROUP), pool_scale=v_pool_scale,
                g_mlp=v_g_mlp, g_final=v_g_final.reshape(1, d))
    w_small = dict(sp, b_ada=b_ada, conv_w=conv_w[0], w_pool=w_pool.reshape(4 * POOL_GROUP, POOL_GROUP))
    loss_row, small = _small_adam(vec8, wpool8, convw8s, dmod8, {n: (w_small[n], m_in[n], v_in[n]) for n in _SMALL_PARAMS})

    dmod_all = dmod8.reshape(N_DEV * nb, N_CHIPS * ada_cols)
    dmod_cols = lax.dynamic_slice(dmod_all, (0, chip * ada_cols), (N_DEV * nb, ada_cols))
    res.update({n: tuple(r.reshape(w.shape) for r in small[n])
                for n, w in (("b_ada", b_ada), ("g_mix", g_mix), ("conv_w", conv_w), ("conv_b", conv_b), ("dt_bias", dt_bias),
                             ("a_log", a_log), ("d_skip", d_skip), ("g_ssd", g_ssd), ("w_pool", w_pool),
                             ("pool_scale", pool_scale), ("g_mlp", g_mlp), ("g_final", g_final))})
    g_ada, d_ada, m_ada, v_ada = _adam_ada(c_act.T.astype(BF), dmod_cols, w_ada[0], m_w_ada[0], v_w_ada[0])
    res["w_ada"] = (g_ada[None], d_ada[None], m_ada[None], v_ada[None])
    h_own, h_got = _exchange_wait(h_send, h_recv, h_src, h_land, SIBLING, g_ada, "halves_wait")
    rows3 = lambda a: jnp.transpose(a, (2, 0, 1))
    res["w_in"] = tuple(jnp.transpose(r, (1, 2, 0))
                        for r in _adam_rows(h_own[0], h_got[0], rows3(w_in), rows3(m_w_in), rows3(v_w_in), pos))

    loss = loss_row[0, 0]
    return (loss, gx.reshape(nb, seq, d), *[res[n][0] for n in _WEIGHTS], *[res[n][1] for n in _WEIGHTS],
            *[res[n][2] for n in _WEIGHTS], *[res[n][3] for n in _WEIGHTS])
```

```python
import jax
import jax.numpy as jnp
from jax import lax
from jax.experimental import pallas as pl
from jax.experimental.pallas import tpu as pltpu

F32 = jnp.float32
BF = jnp.bfloat16
MESH = pl.DeviceIdType.MESH

EPS = 1e-5
D_MODEL = 1024
POOL_WIDTH = 512
POOL_WINDOWS = (2, 4, 8, 16)
POOL_GROUP = 128
SSD_INNER = 1024
SSD_HEADS = 16
SSD_HEAD_DIM = 64
SSD_STATE = 128
GROUP_W = 512
CHUNK = 128
CONV_CH = 1536
OFF_DT = 3072
IN_WIDTH = 3088
IN_PAD = 3200
D_FF = 4096
N_MOD = 6
N_CHIPS = 4
N_DEV = 8
HALO = 16
CONV_HALO = 8

ADAM_LR = 0.001
ADAM_B1 = 0.9
ADAM_B2 = 0.999
ADAM_EPS = 1e-08
ADAM_WD = 0.01
ADAM_STEP = 10

VMEM_BYTES_V7X = 64 * 1024 * 1024


def _cp(semantics=None, vmem_mb=48, **kw):
    assert vmem_mb * 1024 * 1024 < VMEM_BYTES_V7X
    args = dict(vmem_limit_bytes=vmem_mb * 1024 * 1024, **kw)
    if semantics is not None:
        args["dimension_semantics"] = semantics
    return pltpu.CompilerParams(**args)


def _out(shape, dtype):
    return pltpu.HBM(shape, dtype)


def _pin(*arrays):
    return [pltpu.with_memory_space_constraint(a, pltpu.HBM) for a in arrays]


def _nn(a, b):
    return jnp.dot(a, b, preferred_element_type=F32)


def _nt(a, b):
    return lax.dot_general(a, b, (((1,), (1,)), ((), ())), preferred_element_type=F32)


def _tn(a, b):
    return lax.dot_general(a, b, (((0,), (0,)), ((), ())), preferred_element_type=F32)


def _split3(v):
    hi = v.astype(BF)
    r1 = v - hi.astype(F32)
    mid = r1.astype(BF)
    lo = (r1 - mid.astype(F32)).astype(BF)
    return hi, mid, lo


def _exact_nn(v, m01):
    hi, mid, lo = _split3(v)
    return _nn(hi, m01) + _nn(mid, m01) + _nn(lo, m01)


def _exact_nn_left(m01, v):
    hi, mid, lo = _split3(v)
    return _nn(m01, hi) + _nn(m01, mid) + _nn(m01, lo)


def _exact_nt_left(m01, v):
    hi, mid, lo = _split3(v)
    return _nt(m01, hi) + _nt(m01, mid) + _nt(m01, lo)


def _sigmoid(v):
    return 1.0 / (1.0 + jnp.exp(-v))


def _iota(shape, dim):
    return lax.broadcasted_iota(jnp.int32, shape, dim)


def _head_expand_matrix(heads, width):
    return (_iota((heads, width), 1) // SSD_HEAD_DIM == _iota((heads, width), 0)).astype(BF)


def _head_reduce_matrix(width, heads):
    return (_iota((width, heads), 0) // SSD_HEAD_DIM == _iota((width, heads), 1)).astype(BF)


def _mesh_pos():
    return lax.axis_index("x"), lax.axis_index("y"), lax.axis_index("c")


def _flip(v, bit):
    return v + bit - 2 * bit * v


_HBM = pl.BlockSpec(memory_space=pltpu.HBM)
_SEM = pl.BlockSpec(memory_space=pltpu.SEMAPHORE)
_DATAFLOW = pltpu.SideEffectType.DATAFLOW_SIDE_EFFECTING


def _peer_chip(x, y, j):
    return _flip(x, (j >> 1) & 1), _flip(y, j & 1)


def _ici_start(srcs, land_shapes, sent, landing, name, after):
    n = len(srcs)

    def body(*refs):
        src_refs, land_refs = refs[:n], refs[n:2 * n]
        send_sems, recv_sems = refs[2 * n + 1], refs[2 * n + 2]
        token = refs[-1]
        x, y, c = _mesh_pos()
        for j in range(1, N_CHIPS):
            px, py = _peer_chip(x, y, j)
            for a in range(n):
                pltpu.make_async_remote_copy(
                    src_ref=sent(src_refs[a], c, 2 * px + py), dst_ref=landing(land_refs[a], c, 2 * x + y),
                    send_sem=send_sems.at[a * (N_CHIPS - 1) + j - 1], recv_sem=recv_sems.at[a * (N_CHIPS - 1) + j - 1],
                    device_id=(px, py, c), device_id_type=MESH).start()
        token[...] = jnp.zeros_like(token)

    sems = pltpu.SemaphoreType.DMA((n * (N_CHIPS - 1),))
    lands = [pltpu.with_memory_space_constraint(lax.empty(s.shape, s.dtype), pltpu.HBM) for s in land_shapes]
    outs = pl.pallas_call(
        body, name=name,
        out_shape=(sems, sems, *[pltpu.HBM(s.shape, s.dtype) for s in srcs],
                   *[pltpu.HBM(s.shape, s.dtype) for s in land_shapes], jax.ShapeDtypeStruct((8, 128), F32)),
        in_specs=[_HBM] * (2 * n + 1), out_specs=[_SEM, _SEM] + [_HBM] * (2 * n) + [pl.BlockSpec(memory_space=pltpu.VMEM)],
        input_output_aliases={i: 2 + i for i in range(2 * n)},
        compiler_params=pltpu.CompilerParams(has_side_effects=_DATAFLOW),
    )(*_pin(*srcs), *lands, *_pin(after))
    return outs[0], outs[1], outs[2:2 + n], outs[2 + n:2 + 2 * n], outs[-1]


def _ici_wait(send_sems, recv_sems, src_thru, land_thru, after, sent, landing, name):
    n = len(src_thru)
    afters = list(after) if isinstance(after, (list, tuple)) else [after]

    def body(*refs):
        src_refs, land_refs = refs[:n], refs[n:2 * n]
        send_sems, recv_sems = refs[2 * n], refs[2 * n + 1]
        x, y, c = _mesh_pos()
        for j in range(1, N_CHIPS):
            px, py = _peer_chip(x, y, j)
            for a in range(n):
                cp = pltpu.make_async_remote_copy(
                    src_ref=sent(src_refs[a], c, 2 * px + py), dst_ref=landing(land_refs[a], c, 2 * px + py),
                    send_sem=send_sems.at[a * (N_CHIPS - 1) + j - 1], recv_sem=recv_sems.at[a * (N_CHIPS - 1) + j - 1],
                    device_id=(px, py, c), device_id_type=MESH)
                cp.wait_send()
                cp.wait_recv()

    outs = pl.pallas_call(
        body, name=name,
        out_shape=tuple(pltpu.HBM(s.shape, s.dtype) for s in (*src_thru, *land_thru)),
        in_specs=[_HBM] * (2 * n) + [_SEM, _SEM] + [_HBM] * len(afters), out_specs=[_HBM] * (2 * n),
        input_output_aliases={i: i for i in range(2 * n)},
        compiler_params=pltpu.CompilerParams(has_side_effects=_DATAFLOW),
    )(*src_thru, *land_thru, send_sems, recv_sems, *_pin(*afters))
    return outs[:n], outs[n:]


def _col_half(ref, which, lead=()):
    hc = ref.shape[-1] // 2
    return ref.at[(*lead, slice(None), pl.ds(pl.multiple_of(which * hc, 128), hc))]


def _gather_sent(ref, c, dst_chip):
    return _col_half(ref, c)


def _gather_landing(ref, c, src_chip):
    return _col_half(ref, c, lead=(src_chip,))


def _mod_sent(ref, c, dst_chip):
    return ref.at[2 * dst_chip + c]


def _mod_landing(ref, c, src_chip):
    return ref.at[src_chip]


def _reduce_copy(src_ref, land_ref, send_sems, recv_sems, k, receiving):
    x, y, c = _mesh_pos()
    px, py, pc = _flip(x, (k >> 2) & 1), _flip(y, (k >> 1) & 1), _flip(c, k & 1)
    hc = src_ref.shape[2] // 2
    src = src_ref.at[2 * px + py, :, pl.ds(pl.multiple_of(pc * hc, 128), hc)]
    slot = (4 * px + 2 * py + pc) if receiving else (4 * x + 2 * y + c)
    return pltpu.make_async_remote_copy(
        src_ref=src, dst_ref=land_ref.at[slot], send_sem=send_sems.at[k - 1], recv_sem=recv_sems.at[k - 1],
        device_id=(px, py, pc), device_id_type=MESH)


def _reduce_start(grad4, name):
    k4, r, cols = grad4.shape

    def body(src_ref, land_ref, send_sems, recv_sems, src_thru, land_thru, token):
        for k in range(1, N_DEV):
            _reduce_copy(src_ref, land_ref, send_sems, recv_sems, k, receiving=False).start()
        token[...] = jnp.zeros_like(token)

    sems = pltpu.SemaphoreType.DMA((N_DEV - 1,))
    land = pltpu.with_memory_space_constraint(lax.empty((N_DEV, r, cols // 2), grad4.dtype), pltpu.HBM)
    return pl.pallas_call(
        body, name=name,
        out_shape=(sems, sems, pltpu.HBM(grad4.shape, grad4.dtype), pltpu.HBM(land.shape, land.dtype),
                   jax.ShapeDtypeStruct((8, 128), F32)),
        in_specs=[_HBM, _HBM], out_specs=[_SEM, _SEM, _HBM, _HBM, pl.BlockSpec(memory_space=pltpu.VMEM)],
        input_output_aliases={0: 2, 1: 3},
        compiler_params=pltpu.CompilerParams(has_side_effects=_DATAFLOW),
    )(*_pin(grad4), land)


def _reduce_wait(send_sems, recv_sems, src_thru, land_thru, after, name):
    def body(src_ref, land_ref, send_sems, recv_sems, after_ref, src_out, land_out):
        for k in range(1, N_DEV):
            cp = _reduce_copy(src_ref, land_ref, send_sems, recv_sems, k, receiving=True)
            cp.wait_send()
            cp.wait_recv()

    return pl.pallas_call(
        body, name=name,
        out_shape=(pltpu.HBM(src_thru.shape, src_thru.dtype), pltpu.HBM(land_thru.shape, land_thru.dtype)),
        in_specs=[_HBM, _HBM, _SEM, _SEM, _HBM], out_specs=[_HBM, _HBM],
        input_output_aliases={0: 0, 1: 1},
        compiler_params=pltpu.CompilerParams(has_side_effects=_DATAFLOW),
    )(src_thru, land_thru, send_sems, recv_sems, *_pin(after))


def _peer_copy(src_ref, land_ref, send_sems, recv_sems, idx, k, receiving):
    x, y, c = _mesh_pos()
    px, py, pc = _flip(x, (k >> 2) & 1), _flip(y, (k >> 1) & 1), _flip(c, k & 1)
    if land_ref.shape[0] == N_DEV:
        slot = (4 * px + 2 * py + pc) if receiving else (4 * x + 2 * y + c)
    else:
        slot = pc if receiving else c
    return pltpu.make_async_remote_copy(
        src_ref=src_ref, dst_ref=land_ref.at[slot], send_sem=send_sems.at[idx], recv_sem=recv_sems.at[idx],
        device_id=(px, py, pc), device_id_type=MESH)


def _exchange_start(arrays, peers, name):
    n = len(arrays)

    def body(*refs):
        src_refs, land_refs = refs[:n], refs[n:2 * n]
        send_sems, recv_sems = refs[2 * n], refs[2 * n + 1]
        token = refs[-1]
        for j, k in enumerate(peers):
            for a in range(n):
                _peer_copy(src_refs[a], land_refs[a], send_sems, recv_sems, a * len(peers) + j, k, receiving=False).start()
        token[...] = jnp.zeros_like(token)

    sems = pltpu.SemaphoreType.DMA((n * len(peers),))
    n_slots = N_DEV if len(peers) > 1 else 2
    lands = [pltpu.with_memory_space_constraint(lax.empty((n_slots,) + a.shape, a.dtype), pltpu.HBM) for a in arrays]
    outs = pl.pallas_call(
        body, name=name,
        out_shape=(sems, sems, *[pltpu.HBM(a.shape, a.dtype) for a in arrays], *[pltpu.HBM(l.shape, l.dtype) for l in lands],
                   jax.ShapeDtypeStruct((8, 128), F32)),
        in_specs=[_HBM] * (2 * n), out_specs=[_SEM, _SEM] + [_HBM] * (2 * n) + [pl.BlockSpec(memory_space=pltpu.VMEM)],
        input_output_aliases={i: 2 + i for i in range(2 * n)},
        compiler_params=pltpu.CompilerParams(has_side_effects=_DATAFLOW),
    )(*_pin(*arrays), *lands)
    return outs[0], outs[1], outs[2:2 + n], outs[2 + n:2 + 2 * n], outs[-1]


def _exchange_wait(send_sems, recv_sems, src_thru, land_thru, peers, after, name):
    n = len(src_thru)

    def body(*refs):
        src_refs, land_refs = refs[:n], refs[n:2 * n]
        send_sems, recv_sems = refs[2 * n], refs[2 * n + 1]
        for j, k in enumerate(peers):
            for a in range(n):
                cp = _peer_copy(src_refs[a], land_refs[a], send_sems, recv_sems, a * len(peers) + j, k, receiving=True)
                cp.wait_send()
                cp.wait_recv()

    outs = pl.pallas_call(
        body, name=name,
        out_shape=tuple(pltpu.HBM(s.shape, s.dtype) for s in (*src_thru, *land_thru)),
        in_specs=[_HBM] * (2 * n) + [_SEM, _SEM, _HBM], out_specs=[_HBM] * (2 * n),
        input_output_aliases={i: i for i in range(2 * n)},
        compiler_params=pltpu.CompilerParams(has_side_effects=_DATAFLOW),
    )(*src_thru, *land_thru, send_sems, recv_sems, *_pin(after))
    return outs[:n], outs[n:]


ALL_PEERS = tuple(range(1, N_DEV))
SIBLING = (1,)


def _sum_eight(recv, grad4, pos):
    n, r, hc = recv.shape
    steps = 2
    tc = hc // steps

    def body(pos_ref, r_ref, g_ref, o_ref):
        me = pos_ref[2]
        o_ref[...] = jnp.zeros_like(o_ref)
        for s in range(n):
            @pl.when(me == s)
            def _():
                o_ref[...] += g_ref[0].astype(F32)

            @pl.when(me != s)
            def _():
                o_ref[...] += r_ref[s].astype(F32)

    grid_spec = pltpu.PrefetchScalarGridSpec(
        num_scalar_prefetch=1, grid=(steps,),
        in_specs=[pl.BlockSpec((n, r, tc), lambda i, pos: (0, 0, i)),
                  pl.BlockSpec((1, r, tc), lambda i, pos: (pos[1], 0, pos[0] * steps + i))],
        out_specs=pl.BlockSpec((r, tc), lambda i, pos: (0, i)))
    return pl.pallas_call(body, name="sum_eight", grid_spec=grid_spec, out_shape=_out((r, hc), F32),
                          compiler_params=_cp(("parallel",), 32))(pos, *_pin(recv, grad4))


def _forward_copy(land_ref, send_sems, recv_sems, idx, j, receiving):
    x, y, c = _mesh_pos()
    px, py = _peer_chip(x, y, j)
    mine = _col_half(land_ref, c, lead=(2 * px + py,))
    theirs = _col_half(land_ref, 1 - c, lead=(2 * px + py,))
    return pltpu.make_async_remote_copy(
        src_ref=mine, dst_ref=theirs if receiving else mine, send_sem=send_sems.at[idx], recv_sem=recv_sems.at[idx],
        device_id=(x, y, 1 - c), device_id_type=MESH)


def _forward_start(lands, name):
    n = len(lands)

    def body(*refs):
        land_refs, send_sems, recv_sems, token = refs[:n], refs[n], refs[n + 1], refs[-1]
        for j in range(1, N_CHIPS):
            for a in range(n):
                _forward_copy(land_refs[a], send_sems, recv_sems, a * (N_CHIPS - 1) + j - 1, j, receiving=False).start()
        token[...] = jnp.zeros_like(token)

    sems = pltpu.SemaphoreType.DMA((n * (N_CHIPS - 1),))
    outs = pl.pallas_call(
        body, name=name,
        out_shape=(sems, sems, *[pltpu.HBM(l.shape, l.dtype) for l in lands], jax.ShapeDtypeStruct((8, 128), F32)),
        in_specs=[_HBM] * n, out_specs=[_SEM, _SEM] + [_HBM] * n + [pl.BlockSpec(memory_space=pltpu.VMEM)],
        input_output_aliases={i: 2 + i for i in range(n)},
        compiler_params=pltpu.CompilerParams(has_side_effects=_DATAFLOW),
    )(*lands)
    return outs[0], outs[1], outs[2:2 + n], outs[-1]


def _forward_wait(send_sems, recv_sems, lands_thru, after, name):
    n = len(lands_thru)

    def body(*refs):
        land_refs, send_sems, recv_sems = refs[:n], refs[n], refs[n + 1]
        for j in range(1, N_CHIPS):
            for a in range(n):
                cp = _forward_copy(land_refs[a], send_sems, recv_sems, a * (N_CHIPS - 1) + j - 1, j, receiving=True)
                cp.wait_send()
                cp.wait_recv()

    return pl.pallas_call(
        body, name=name,
        out_shape=tuple(pltpu.HBM(l.shape, l.dtype) for l in lands_thru),
        in_specs=[_HBM] * n + [_SEM, _SEM, _HBM], out_specs=[_HBM] * n,
        input_output_aliases={i: i for i in range(n)},
        compiler_params=pltpu.CompilerParams(has_side_effects=_DATAFLOW),
    )(*lands_thru, send_sems, recv_sems, *_pin(after))


def _gather_finish(lands, shards):
    n = len(lands)
    any_spec = _HBM

    def body(*refs):
        shard_refs, out_refs = refs[n:2 * n], refs[2 * n:3 * n]
        send_sems, recv_sems, local_sems = refs[3 * n:]
        x, y, c = _mesh_pos()
        chip = 2 * x + y
        local, sends = [], []
        for a in range(n):
            cp = pltpu.make_async_copy(shard_refs[a], out_refs[a].at[chip], local_sems.at[a])
            cp.start()
            local.append(cp)
        for j in range(1, N_CHIPS):
            px, py = _peer_chip(x, y, j)
            for a in range(n):
                landed = _col_half(out_refs[a], c, lead=(2 * px + py,))
                cp = pltpu.make_async_remote_copy(
                    src_ref=landed, dst_ref=landed, send_sem=send_sems.at[a, j], recv_sem=recv_sems.at[a, j],
                    device_id=(x, y, 1 - c), device_id_type=MESH)
                cp.start()
                sends.append(cp)
        for j in range(1, N_CHIPS):
            px, py = _peer_chip(x, y, j)
            for a in range(n):
                other = _col_half(out_refs[a], 1 - c, lead=(2 * px + py,))
                pltpu.make_async_remote_copy(
                    src_ref=other, dst_ref=other, send_sem=send_sems.at[a, j], recv_sem=recv_sems.at[a, j],
                    device_id=(x, y, 1 - c), device_id_type=MESH).wait_recv()
        for cp in sends:
            cp.wait_send()
        for cp in local:
            cp.wait()

    return pl.pallas_call(
        body, name="gather_finish",
        out_shape=[_out(l.shape, l.dtype) for l in lands],
        in_specs=[any_spec] * (2 * n), out_specs=[any_spec] * n,
        input_output_aliases={i: i for i in range(n)},
        scratch_shapes=[pltpu.SemaphoreType.DMA((n, N_CHIPS))] * 2 + [pltpu.SemaphoreType.DMA((n,))],
        compiler_params=_cp(vmem_mb=16),
    )(*lands, *shards)


def _adam_math(w, g, m, v):
    m2 = ADAM_B1 * m + (1.0 - ADAM_B1) * g
    v2 = ADAM_B2 * v + (1.0 - ADAM_B2) * (g * g)
    m_hat = m2 / (1.0 - ADAM_B1 ** ADAM_STEP)
    v_hat = v2 / (1.0 - ADAM_B2 ** ADAM_STEP)
    delta = -ADAM_LR * (m_hat / (jnp.sqrt(v_hat) + ADAM_EPS) + ADAM_WD * w)
    return delta, m2, v2


def _adam_big(g_own, g_pair, w, m, v, pos):
    r, c = w.shape
    per_half = 2
    tc = c // (2 * per_half)

    def body(pos_ref, go_ref, gp_ref, w_ref, m_ref, v_ref, g_ref, d_ref, m2_ref, v2_ref):
        half = pl.program_id(0) // per_half
        g = jnp.where(half == pos_ref[0], go_ref[...], gp_ref[0])
        d, m2, v2 = _adam_math(w_ref[...], g, m_ref[...], v_ref[...])
        g_ref[...] = g
        d_ref[...] = d
        m2_ref[...] = m2
        v2_ref[...] = v2

    spec = pl.BlockSpec((r, tc), lambda i, pos: (0, i))
    grid_spec = pltpu.PrefetchScalarGridSpec(
        num_scalar_prefetch=1, grid=(2 * per_half,),
        in_specs=[pl.BlockSpec((r, tc), lambda i, pos: (0, i % per_half)),
                  pl.BlockSpec((1, r, tc), lambda i, pos: (1 - pos[0], 0, i % per_half)), spec, spec, spec],
        out_specs=[spec] * 4)
    sh = _out((r, c), F32)
    return pl.pallas_call(body, name="adam_big", grid_spec=grid_spec, out_shape=[sh] * 4,
                          compiler_params=_cp(("parallel",), 32))(pos, *_pin(g_own, g_pair, w, m, v))


def _adam_rows(g_own, g_pair, w3, m3, v3, pos):
    r, _, c = w3.shape
    tr = 128

    def body(pos_ref, go_ref, gp_ref, w_ref, m_ref, v_ref, g_ref, d_ref, m2_ref, v2_ref):
        core = pos_ref[0]
        g = jnp.concatenate([jnp.where(core == 0, go_ref[...], gp_ref[0]), jnp.where(core == 1, go_ref[...], gp_ref[0])], 1)
        d, m2, v2 = _adam_math(w_ref[:, 0, :], g, m_ref[:, 0, :], v_ref[:, 0, :])
        g_ref[:, 0, :] = g
        d_ref[:, 0, :] = d
        m2_ref[:, 0, :] = m2
        v2_ref[:, 0, :] = v2

    spec = pl.BlockSpec((tr, 1, c), lambda i, pos: (i, 0, 0))
    grid_spec = pltpu.PrefetchScalarGridSpec(
        num_scalar_prefetch=1, grid=(pl.cdiv(r, tr),),
        in_specs=[pl.BlockSpec((tr, c // 2), lambda i, pos: (i, 0)),
                  pl.BlockSpec((1, tr, c // 2), lambda i, pos: (1 - pos[0], i, 0)), spec, spec, spec],
        out_specs=[spec] * 4)
    sh = _out(w3.shape, F32)
    return pl.pallas_call(body, name="adam_rows", grid_spec=grid_spec, out_shape=[sh] * 4,
                          compiler_params=_cp(("parallel",), 32))(pos, *_pin(g_own, g_pair, w3, m3, v3))


def _adam_ada(c_act_t, dmod_cols, w, m, v):
    r, c = w.shape
    tc = 512

    def body(ct_ref, dm_ref, w_ref, m_ref, v_ref, g_ref, d_ref, m2_ref, v2_ref):
        g = _nn(ct_ref[...], dm_ref[...].astype(BF))
        d, m2, v2 = _adam_math(w_ref[...], g, m_ref[...], v_ref[...])
        g_ref[...] = g
        d_ref[...] = d
        m2_ref[...] = m2
        v2_ref[...] = v2

    spec = pl.BlockSpec((r, tc), lambda i: (0, i))
    sh = _out((r, c), F32)
    return pl.pallas_call(
        body, name="adam_ada", grid=(c // tc,),
        in_specs=[pl.BlockSpec(c_act_t.shape, lambda i: (0, 0)), pl.BlockSpec((dmod_cols.shape[0], tc), lambda i: (0, i)),
                  spec, spec, spec],
        out_specs=[spec] * 4, out_shape=[sh] * 4, compiler_params=_cp(("parallel",), 48))(*_pin(c_act_t, dmod_cols, w, m, v))


def _ada_mod(c_all, w_shard, b_shard, token):
    nb, d = c_all.shape
    cols = w_shard.shape[1]
    tc = 512

    def body(c_ref, w_ref, b_ref, tok_ref, mod_ref, act_ref):
        cv = c_ref[...]
        act = cv * _sigmoid(cv)
        act_ref[...] = act
        mod_ref[...] = _nn(act.astype(BF), w_ref[...].astype(BF)) + b_ref[...]

    return pl.pallas_call(
        body, name="ada_mod", grid=(cols // tc,),
        in_specs=[pl.BlockSpec((nb, d), lambda i: (0, 0)), pl.BlockSpec((d, tc), lambda i: (0, i)),
                  pl.BlockSpec((1, tc), lambda i: (0, i)), _token_spec()],
        out_specs=[pl.BlockSpec((nb, tc), lambda i: (0, i)), pl.BlockSpec((nb, d), lambda i: (0, 0))],
        out_shape=[_out((nb, cols), F32), _out((nb, d), F32)],
        compiler_params=_cp(("arbitrary",), 32))(*_pin(c_all, w_shard, b_shard, token))


SUB_ROWS = 256
ROW_TILE = 512


def _sub_rows(tm):
    return [slice(s, s + SUB_ROWS) for s in range(0, tm, SUB_ROWS)] if tm > SUB_ROWS else [slice(0, tm)]


_RESIDENT = pl.BlockSpec(memory_space=pltpu.VMEM)


def _token_spec():
    return pl.BlockSpec((8, 128), lambda *_: (0, 0))


def _in_proj(x, mod3, g_mix, w_in_t, seq, token):
    t, d = x.shape
    tm = min(ROW_TILE, seq)
    tps = seq // tm

    def body(x_ref, mod_ref, g_ref, w_ref, tok_ref, proj_ref, u1_ref):
        for rows in _sub_rows(tm):
            xv = x_ref[rows, :]
            r = lax.rsqrt(jnp.mean(xv * xv, -1, keepdims=True) + EPS)
            u = (xv * r * g_ref[...]) * (1.0 + mod_ref[0, 1:2, :]) + mod_ref[0, 0:1, :]
            ub = u.astype(BF)
            u1_ref[rows, :] = ub
            proj_ref[rows, 0:OFF_DT] = _nt(ub, w_ref[0:OFF_DT, :])
            proj_ref[rows, OFF_DT:IN_PAD] = jnp.zeros((rows.stop - rows.start, IN_PAD - OFF_DT), F32)
            proj_ref[rows, OFF_DT:IN_WIDTH] = _nt(ub, w_ref[OFF_DT:IN_WIDTH, :])

    return pl.pallas_call(
        body, name="in_proj", grid=(t // tm,),
        in_specs=[pl.BlockSpec((tm, d), lambda i: (i, 0)), pl.BlockSpec((1, N_MOD, d), lambda i: (i // tps, 0, 0)),
                  pl.BlockSpec((1, d), lambda i: (0, 0)), _RESIDENT, _token_spec()],
        out_specs=[pl.BlockSpec((tm, IN_PAD), lambda i: (i, 0)), pl.BlockSpec((tm, d), lambda i: (i, 0))],
        out_shape=[_out((t, IN_PAD), F32), _out((t, d), BF)],
        compiler_params=_cp(("parallel",), 40))(*_pin(x, mod3, g_mix), w_in_t, *_pin(token))


def _pool_tile(seq):
    return min(512, seq)


def _pool_fwd(proj, w_pool, pool_scale, nb, seq, token):
    ts = _pool_tile(seq)
    nt = seq // ts

    def body(u_ref, halo_ref, wp_ref, ps_ref, tok_ref, yp_ref, p_ref):
        i = pl.program_id(1)
        halo = jnp.where(i == 0, 0.0, halo_ref[...])
        u = u_ref[...]
        ext = jnp.concatenate([halo, u], 0)
        tpos = i * ts + _iota((ts, 1), 0)
        for g, w in enumerate(POOL_WINDOWS):
            gs = slice(g * POOL_GROUP, (g + 1) * POOL_GROUP)
            s = ext[:, gs]
            sh = 1
            while sh < w:
                s = s + pltpu.roll(s, sh, 0)
                sh *= 2
            cnt = jnp.minimum(tpos + 1, w).astype(F32)
            pb = (s[HALO:] / cnt - u[:, gs]).astype(BF)
            p_ref[:, gs] = pb
            yp_ref[:, gs] = (_nn(pb, wp_ref[g].astype(BF)) * ps_ref[:, gs]).astype(BF)

    hb = ts // HALO
    return pl.pallas_call(
        body, name="pool_fwd", grid=(nb, nt),
        in_specs=[pl.BlockSpec((ts, POOL_WIDTH), lambda b, i: (b * nt + i, 0)),
                  pl.BlockSpec((HALO, POOL_WIDTH), lambda b, i: (jnp.maximum((b * nt + i) * hb - 1, 0), 0)),
                  pl.BlockSpec((4, POOL_GROUP, POOL_GROUP), lambda b, i: (0, 0, 0)),
                  pl.BlockSpec((1, POOL_WIDTH), lambda b, i: (0, 0)), _token_spec()],
        out_specs=[pl.BlockSpec((ts, POOL_WIDTH), lambda b, i: (b * nt + i, 0))] * 2,
        out_shape=[_out((nb * seq, POOL_WIDTH), BF)] * 2,
        compiler_params=_cp(("parallel", "parallel"), 32))(*_pin(proj, proj, w_pool, pool_scale, token))


def _conv_pre(uxbc, halo, cw, cb, first):
    halo = jnp.where(first, 0.0, halo)
    ext = jnp.concatenate([halo, uxbc], 0)
    pre = cb + uxbc * cw[3:4]
    for k in (2, 1, 0):
        pre = pre + pltpu.roll(ext, 3 - k, 0)[CONV_HALO:] * cw[k:k + 1]
    return pre


def _chunk_terms(pre, udt, dtb, alog):
    sg = _sigmoid(pre)
    xbc = pre * sg
    dtp = udt[:, :SSD_HEADS] + dtb
    dt = jnp.maximum(dtp, 0.0) + jnp.log(1.0 + jnp.exp(-jnp.abs(dtp)))
    a = -jnp.exp(alog)
    da = dt * a
    tril = (_iota((CHUNK, CHUNK), 0) >= _iota((CHUNK, CHUNK), 1))
    acum = _exact_nn_left(tril.astype(BF), da)
    eye = (_iota((SSD_HEADS, SSD_HEADS), 0) == _iota((SSD_HEADS, SSD_HEADS), 1)).astype(BF)
    acum_t = _exact_nt_left(eye, acum)
    expand = _head_expand_matrix(SSD_HEADS, SSD_INNER)
    acum_e = _exact_nn(acum, expand)
    dt_e = _exact_nn(dt, expand)
    last_e = acum_e[CHUNK - 1:CHUNK]
    return dict(pre=pre, sg=sg, xbc=xbc, dtp=dtp, dt=dt, a=a, acum=acum, acum_t=acum_t, tril=tril,
                dt_e=dt_e, e_a=jnp.exp(acum_e), d_out=jnp.exp(last_e - acum_e), c_dec=jnp.exp(last_e))


def _head_decay(r, h):
    seg = r["acum"][:, h:h + 1] - r["acum_t"][h:h + 1, :]
    return jnp.where(r["tril"], jnp.exp(jnp.minimum(seg, 0.0)), 0.0)


SSD_SUB = 4
SSD_ROWS = SSD_SUB * CHUNK


def _ssd_specs(nb, seq, reverse):
    ns = seq // SSD_ROWS
    per = seq // CONV_HALO

    def cidx(c):
        return (ns - 1 - c) if reverse else c

    def row(b, c):
        return b * ns + cidx(c)

    specs = [
        pl.BlockSpec((SSD_ROWS, CONV_CH), lambda b, c: (row(b, c), 1)),
        pl.BlockSpec((CONV_HALO, CONV_CH),
                     lambda b, c: (jnp.maximum(b * per + cidx(c) * (SSD_ROWS // CONV_HALO) - 1, 0), 1)),
        pl.BlockSpec((SSD_ROWS, GROUP_W), lambda b, c: (row(b, c), 1)),
        pl.BlockSpec((SSD_ROWS, GROUP_W), lambda b, c: (row(b, c), 2)),
        pl.BlockSpec((SSD_ROWS, 128), lambda b, c: (row(b, c), OFF_DT // 128)),
    ]
    return specs, row, cidx, ns


def _const_spec(shape):
    return pl.BlockSpec(shape, lambda b, c: (0,) * len(shape))


def _ssd_fwd(proj, conv_w, conv_b, dt_bias, a_log, dskip_e, g_ssd, nb, seq):
    specs, row, cidx, ns = _ssd_specs(nb, seq, reverse=False)

    def body(uxbc_ref, halo_ref, z0_ref, z1_ref, udt_ref, cw_ref, cb_ref, dtb_ref, alog_ref, dsk_ref, gs_ref,
             yssd_ref, yssm_ref, hprev_ref, pre_ref, h_ref, yd_ref):
        c = pl.program_id(1)

        @pl.when(c == 0)
        def _():
            h_ref[...] = jnp.zeros_like(h_ref)

        for sub in range(SSD_SUB):
            rows = slice(sub * CHUNK, (sub + 1) * CHUNK)
            if sub == 0:
                halo, first = halo_ref[...], c == 0
            else:
                halo, first = uxbc_ref[sub * CHUNK - CONV_HALO:sub * CHUNK, :], False
            pre = _conv_pre(uxbc_ref[rows, :], halo, cw_ref[...], cb_ref[...], first)
            pre_ref[rows, :] = pre
            r = _chunk_terms(pre, udt_ref[rows, :], dtb_ref[...], alog_ref[...])
            xbc = r["xbc"]
            xs = xbc[:, :SSD_INNER]
            xdt = xs * r["dt_e"]
            xdt_b = xdt.astype(BF)
            xdo_b = (xdt * r["d_out"]).astype(BF)
            hprev_ref[0, sub] = h_ref[...]
            for g in range(2):
                gs = slice(g * GROUP_W, (g + 1) * GROUP_W)
                bg = xbc[:, SSD_INNER + g * SSD_STATE:SSD_INNER + (g + 1) * SSD_STATE].astype(BF)
                cg = xbc[:, SSD_INNER + (2 + g) * SSD_STATE:SSD_INNER + (3 + g) * SSD_STATE].astype(BF)
                scores = _nt(cg, bg)
                hg = h_ref[g]
                y_off = _nn(cg, hg.astype(BF)) * r["e_a"][:, gs]
                for hh in range(8):
                    h = g * 8 + hh
                    hs = slice(h * SSD_HEAD_DIM, (h + 1) * SSD_HEAD_DIM)
                    m = (scores * _head_decay(r, h)).astype(BF)
                    yd_ref[sub, :, hs] = _nn(m, xdt_b[:, hs])
                h_ref[g] = hg * r["c_dec"][:, gs] + _tn(bg, xdo_b[:, gs])
                y = yd_ref[sub, :, gs] + y_off + dsk_ref[:, gs] * xs[:, gs]
                yssm_ref[rows, gs] = y
                zg = (z0_ref if g == 0 else z1_ref)[rows, :]
                yg = y * (zg * _sigmoid(zg))
                rg = lax.rsqrt(jnp.mean(yg * yg, -1, keepdims=True) + EPS)
                yssd_ref[rows, gs] = (yg * rg * gs_ref[:, gs]).astype(BF)

    t = nb * seq
    return pl.pallas_call(
        body, name="ssd_fwd", grid=(nb, ns),
        in_specs=specs + [_const_spec((4, CONV_CH)), _const_spec((1, CONV_CH)), _const_spec((1, SSD_HEADS)),
                          _const_spec((1, SSD_HEADS)), _const_spec((1, SSD_INNER)), _const_spec((1, SSD_INNER))],
        out_specs=[pl.BlockSpec((SSD_ROWS, SSD_INNER), lambda b, c: (row(b, c), 0)),
                   pl.BlockSpec((SSD_ROWS, SSD_INNER), lambda b, c: (row(b, c), 0)),
                   pl.BlockSpec((1, SSD_SUB, 2, SSD_STATE, GROUP_W), lambda b, c: (b, c, 0, 0, 0)),
                   pl.BlockSpec((SSD_ROWS, CONV_CH), lambda b, c: (row(b, c), 0))],
        out_shape=[_out((t, SSD_INNER), BF), _out((t, SSD_INNER), F32),
                   _out((nb, seq // CHUNK, 2, SSD_STATE, GROUP_W), F32), _out((t, CONV_CH), F32)],
        scratch_shapes=[pltpu.VMEM((2, SSD_STATE, GROUP_W), F32), pltpu.VMEM((SSD_SUB, CHUNK, SSD_INNER), F32)],
        compiler_params=_cp(("arbitrary", "arbitrary"), 56),
    )(*_pin(proj, proj, proj, proj, proj, conv_w, conv_b, dt_bias, a_log, dskip_e, g_ssd))


def _out_proj(y_pool, y_ssd, w_out, x, mod3, g_mlp, seq):
    t, d = x.shape
    tm = 512
    tps = seq // tm if seq >= tm else 1
    tm = min(tm, seq)

    def body(yp_ref, ys_ref, w_ref, x_ref, mod_ref, g_ref, h1_ref, o_ref, u2_ref):
        for rows in _sub_rows(tm):
            o = _nn(jnp.concatenate([yp_ref[rows, :], ys_ref[rows, :]], 1), w_ref[...])
            o_ref[rows, :] = o.astype(BF)
            h1 = x_ref[rows, :] + mod_ref[0, 2:3, :] * o
            h1_ref[rows, :] = h1
            r = lax.rsqrt(jnp.mean(h1 * h1, -1, keepdims=True) + EPS)
            u2_ref[rows, :] = ((h1 * r * g_ref[...]) * (1.0 + mod_ref[0, 4:5, :]) + mod_ref[0, 3:4, :]).astype(BF)

    row = lambda i: (i, 0)
    return pl.pallas_call(
        body, name="out_proj", grid=(t // tm,),
        in_specs=[pl.BlockSpec((tm, POOL_WIDTH), row), pl.BlockSpec((tm, SSD_INNER), row),
                  _RESIDENT, pl.BlockSpec((tm, d), row),
                  pl.BlockSpec((1, N_MOD, d), lambda i: (i // tps, 0, 0)), pl.BlockSpec((1, d), lambda i: (0, 0))],
        out_specs=[pl.BlockSpec((tm, d), row)] * 3,
        out_shape=[_out((t, d), F32), _out((t, d), BF), _out((t, d), BF)],
        compiler_params=_cp(("parallel",), 48))(*_pin(y_pool, y_ssd), w_out, *_pin(x, mod3, g_mlp))


def _mlp_up(u2, w_up4):
    t, d = u2.shape
    tm = min(1024, t)
    nk, _, cols = w_up4.shape

    def body(u_ref, w_ref, a_ref):
        a_ref[...] = _nn(u_ref[...], w_ref[pl.program_id(1)]).astype(BF)

    return pl.pallas_call(
        body, name="mlp_up", grid=(t // tm, nk),
        in_specs=[pl.BlockSpec((tm, d), lambda i, k: (i, 0)), _RESIDENT],
        out_specs=pl.BlockSpec((tm, cols), lambda i, k: (i, k)),
        out_shape=_out((t, nk * cols), BF),
        compiler_params=_cp(("parallel", "parallel"), 32))(*_pin(u2), w_up4)


def _mlp_down_loss(a_up, w_down, h1, mod3, g_final, target, seq):
    t, d = h1.shape
    nb = t // seq
    tm = min(ROW_TILE, seq)
    tps = seq // tm

    def body(a_ref, w_ref, h1_ref, mod_ref, g_ref, tg_ref, ddn_ref, dh2_ref, sq_ref, gg_ref, dgf_ref):
        i = pl.program_id(0)

        @pl.when(i == 0)
        def _():
            sq_ref[...] = jnp.zeros_like(sq_ref)
            gg_ref[...] = jnp.zeros_like(gg_ref)

        @pl.when(i % tps == 0)
        def _():
            dgf_ref[...] = jnp.zeros_like(dgf_ref)

        gate = mod_ref[0, 5:6, :]
        sq = gg = dgf = 0.0
        for rows in _sub_rows(tm):
            f = jnp.square(jnp.maximum(a_ref[rows, :], 0))
            dn = _nn(f, w_ref[...])
            h2 = h1_ref[rows, :] + gate * dn
            r = lax.rsqrt(jnp.mean(h2 * h2, -1, keepdims=True) + EPS)
            hh = h2 * r
            err = hh * g_ref[...] - tg_ref[rows, :]
            dy = err * (1.0 / d)
            dhat = dy * g_ref[...]
            dh2 = r * (dhat - hh * jnp.mean(dhat * hh, -1, keepdims=True))
            dh2_ref[rows, :] = dh2
            ddn_ref[rows, :] = (dh2 * gate).astype(BF)
            sq = sq + jnp.sum(err * err, 0, keepdims=True)
            gg = gg + jnp.sum(dy * hh, 0, keepdims=True)
            dgf = dgf + jnp.sum(dh2 * dn, 0, keepdims=True)
        sq_ref[...] += sq
        gg_ref[...] += gg
        dgf_ref[0] += dgf

    row = lambda i: (i, 0)
    vec = pl.BlockSpec((1, d), lambda i: (0, 0))
    return pl.pallas_call(
        body, name="mlp_down_loss", grid=(t // tm,),
        in_specs=[pl.BlockSpec((tm, D_FF), row), _RESIDENT, pl.BlockSpec((tm, d), row),
                  pl.BlockSpec((1, N_MOD, d), lambda i: (i // tps, 0, 0)), vec, pl.BlockSpec((tm, d), row)],
        out_specs=[pl.BlockSpec((tm, d), row), pl.BlockSpec((tm, d), row), vec, vec,
                   pl.BlockSpec((1, 1, d), lambda i: (i // tps, 0, 0))],
        out_shape=[_out((t, d), BF), _out((t, d), F32), _out((1, d), F32),
                   _out((1, d), F32), _out((nb, 1, d), F32)],
        compiler_params=_cp(("arbitrary",), 44))(*_pin(a_up), w_down, *_pin(h1, mod3, g_final, target))


def _tn_matmul(a, b, tk, tn, name, square_relu=False, out3=False):
    t, kdim = a.shape
    ndim = b.shape[1]

    def body(a_ref, b_ref, o_ref):
        av = a_ref[...]
        if square_relu:
            av = jnp.square(jnp.maximum(av, 0))
        res = _tn(av, b_ref[...]).astype(BF)
        if out3:
            o_ref[0] = res
        else:
            o_ref[...] = res

    if out3:
        out_spec = pl.BlockSpec((1, tk, tn), lambda j, i: (j, i, 0))
        out_shape = _out((ndim // tn, kdim, tn), BF)
    else:
        out_spec = pl.BlockSpec((tk, tn), lambda j, i: (i, j))
        out_shape = _out((kdim, ndim), BF)
    return pl.pallas_call(
        body, name=name, grid=(ndim // tn, kdim // tk),
        in_specs=[pl.BlockSpec((t, tk), lambda j, i: (0, i)), pl.BlockSpec((t, tn), lambda j, i: (0, j))],
        out_specs=out_spec, out_shape=out_shape,
        compiler_params=_cp(("parallel", "parallel"), 56))(*_pin(a, b))


def _mlp_down_bwd(d_dn, w_down4, a_up, token):
    t, d = d_dn.shape
    tm = min(1024, t)
    nk, rows, _ = w_down4.shape

    def body(g_ref, w_ref, a_ref, tok_ref, o_ref):
        df = _nt(g_ref[...], w_ref[pl.program_id(1)])
        o_ref[...] = (df * (2.0 * jnp.maximum(a_ref[...], 0).astype(F32))).astype(BF)

    return pl.pallas_call(
        body, name="mlp_down_bwd", grid=(t // tm, nk),
        in_specs=[pl.BlockSpec((tm, d), lambda i, k: (i, 0)), _RESIDENT,
                  pl.BlockSpec((tm, rows), lambda i, k: (i, k)), _token_spec()],
        out_specs=pl.BlockSpec((tm, rows), lambda i, k: (i, k)),
        out_shape=_out((t, nk * rows), BF),
        compiler_params=_cp(("parallel", "parallel"), 32))(*_pin(d_dn), w_down4, *_pin(a_up, token))


def _mlp_up_bwd(d_a, w_up4, h1, dh2, o, mod3, g_mlp, seq, token):
    t, d = h1.shape
    nb = t // seq
    tm = min(ROW_TILE, seq)
    tps = seq // tm
    nk = w_up4.shape[0]
    cols = w_up4.shape[2]

    def body(da_ref, w_ref, h1_ref, dh2_ref, o_ref, mod_ref, g_ref, tok_ref, dh1_ref, do_ref, acc_ref, gg_ref):
        i = pl.program_id(0)

        @pl.when(i == 0)
        def _():
            gg_ref[...] = jnp.zeros_like(gg_ref)

        @pl.when(i % tps == 0)
        def _():
            acc_ref[...] = jnp.zeros_like(acc_ref)

        gg = a_shift = a_scale = a_gate = 0.0
        for rows in _sub_rows(tm):
            du = _nt(da_ref[rows, 0:cols], w_ref[0])
            for k in range(1, nk):
                du = du + _nt(da_ref[rows, k * cols:(k + 1) * cols], w_ref[k])
            h1 = h1_ref[rows, :]
            r = lax.rsqrt(jnp.mean(h1 * h1, -1, keepdims=True) + EPS)
            hh = h1 * r
            n2 = hh * g_ref[...]
            dn2 = du * (1.0 + mod_ref[0, 4:5, :])
            dhat = dn2 * g_ref[...]
            dh1 = dh2_ref[rows, :] + r * (dhat - hh * jnp.mean(dhat * hh, -1, keepdims=True))
            dh1_ref[rows, :] = dh1
            do_ref[rows, :] = (dh1 * mod_ref[0, 2:3, :]).astype(BF)
            gg = gg + jnp.sum(dn2 * hh, 0, keepdims=True)
            a_shift = a_shift + jnp.sum(du, 0, keepdims=True)
            a_scale = a_scale + jnp.sum(du * n2, 0, keepdims=True)
            a_gate = a_gate + jnp.sum(dh1 * o_ref[rows, :].astype(F32), 0, keepdims=True)
        gg_ref[...] += gg
        acc_ref[0, 0:1, :] += a_shift
        acc_ref[0, 1:2, :] += a_scale
        acc_ref[0, 2:3, :] += a_gate

    row = lambda i: (i, 0)
    vec = pl.BlockSpec((1, d), lambda i: (0, 0))
    return pl.pallas_call(
        body, name="mlp_up_bwd", grid=(t // tm,),
        in_specs=[pl.BlockSpec((tm, D_FF), row), _RESIDENT, pl.BlockSpec((tm, d), row),
                  pl.BlockSpec((tm, d), row), pl.BlockSpec((tm, d), row),
                  pl.BlockSpec((1, N_MOD, d), lambda i: (i // tps, 0, 0)), vec, _token_spec()],
        out_specs=[pl.BlockSpec((tm, d), row), pl.BlockSpec((tm, d), row),
                   pl.BlockSpec((1, 8, d), lambda i: (i // tps, 0, 0)), vec],
        out_shape=[_out((t, d), F32), _out((t, d), BF),
                   _out((nb, 8, d), F32), _out((1, d), F32)],
        compiler_params=_cp(("arbitrary",), 44))(*_pin(d_a), w_up4, *_pin(h1, dh2, o, mod3, g_mlp, token))


def _out_proj_bwd(d_o, w_out, token):
    t, d = d_o.shape
    tm = min(512, t)

    def body(g_ref, w_ref, tok_ref, dp_ref, ds_ref):
        gv = g_ref[...]
        dp_ref[...] = _nt(gv, w_ref[0:POOL_WIDTH, :])
        ds_ref[...] = _nt(gv, w_ref[POOL_WIDTH:, :])

    row = lambda i: (i, 0)
    return pl.pallas_call(
        body, name="out_proj_bwd", grid=(t // tm,),
        in_specs=[pl.BlockSpec((tm, d), row), _RESIDENT, _token_spec()],
        out_specs=[pl.BlockSpec((tm, POOL_WIDTH), row), pl.BlockSpec((tm, SSD_INNER), row)],
        out_shape=[_out((t, POOL_WIDTH), F32), _out((t, SSD_INNER), F32)],
        compiler_params=_cp(("parallel",), 32))(*_pin(d_o), w_out, *_pin(token))


def _pool_bwd(d_ypool, p, w_pool, pool_scale, nb, seq):
    ts = _pool_tile(seq)
    nt = seq // ts
    hb = ts // HALO
    last_block = nb * seq // HALO - 1

    def body(dy_ref, halo_ref, p_ref, wp_ref, ps_ref, du_ref, gw_ref, gs_ref):
        b = pl.program_id(0)
        i = pl.program_id(1)

        @pl.when((b == 0) & (i == 0))
        def _():
            gw_ref[...] = jnp.zeros_like(gw_ref)
            gs_ref[...] = jnp.zeros_like(gs_ref)

        halo = jnp.where(i == nt - 1, 0.0, halo_ref[...])
        dy = dy_ref[...]
        ext = jnp.concatenate([dy, halo], 0)
        tpos = i * ts + _iota((ts + HALO, 1), 0)
        n_ext = ts + HALO
        for g, w in enumerate(POOL_WINDOWS):
            gs = slice(g * POOL_GROUP, (g + 1) * POOL_GROUP)
            wg = wp_ref[g].astype(BF)
            pg = p_ref[:, gs]
            pw = _nn(pg, wg)
            gs_ref[:, gs] += jnp.sum(dy[:, gs] * pw, 0, keepdims=True)
            dpw = (ext[:, gs] * ps_ref[:, gs]).astype(BF)
            gw_ref[g] += _tn(pg, dpw[:ts])
            dp = _nt(dpw, wg)
            cnt = jnp.minimum(tpos + 1, w).astype(F32)
            s = dp / cnt
            sh = 1
            while sh < w:
                s = s + pltpu.roll(s, n_ext - sh, 0)
                sh *= 2
            du_ref[:, gs] = (s[:ts] - dp[:ts]).astype(BF)

    return pl.pallas_call(
        body, name="pool_bwd", grid=(nb, nt),
        in_specs=[pl.BlockSpec((ts, POOL_WIDTH), lambda b, i: (b * nt + i, 0)),
                  pl.BlockSpec((HALO, POOL_WIDTH), lambda b, i: (jnp.minimum((b * nt + i + 1) * hb, last_block), 0)),
                  pl.BlockSpec((ts, POOL_WIDTH), lambda b, i: (b * nt + i, 0)),
                  pl.BlockSpec((4, POOL_GROUP, POOL_GROUP), lambda b, i: (0, 0, 0)),
                  pl.BlockSpec((1, POOL_WIDTH), lambda b, i: (0, 0))],
        out_specs=[pl.BlockSpec((ts, POOL_WIDTH), lambda b, i: (b * nt + i, 0)),
                   pl.BlockSpec((4, POOL_GROUP, POOL_GROUP), lambda b, i: (0, 0, 0)),
                   pl.BlockSpec((1, POOL_WIDTH), lambda b, i: (0, 0))],
        out_shape=[_out((nb * seq, POOL_WIDTH), BF), _out((4, POOL_GROUP, POOL_GROUP), F32),
                   _out((1, POOL_WIDTH), F32)],
        compiler_params=_cp(("arbitrary", "arbitrary"), 32))(*_pin(d_ypool, d_ypool, p, w_pool, pool_scale))


def _ssd_bwd(proj, pre, d_yssd, yssm, h_prev, dt_bias, a_log, dskip_e, g_ssd, nb, seq):
    specs, row, cidx, ns = _ssd_specs(nb, seq, reverse=True)
    specs = specs[2:]

    def body(z0_ref, z1_ref, udt_ref, pre_ref, dys_ref, yssm_ref, hprev_ref,
             dtb_ref, alog_ref, dsk_ref, gs_ref,
             dz_ref, dpre_ref, dudt_ref, ggs_ref, gdsk_ref, ga_ref, gdtb_ref,
             g_ref, dxdt_ref, dyv_ref):
        b = pl.program_id(0)
        c = pl.program_id(1)

        @pl.when(c == 0)
        def _():
            g_ref[...] = jnp.zeros_like(g_ref)

        @pl.when((b == 0) & (c == 0))
        def _():
            ggs_ref[...] = jnp.zeros_like(ggs_ref)
            gdsk_ref[...] = jnp.zeros_like(gdsk_ref)
            ga_ref[...] = jnp.zeros_like(ga_ref)
            gdtb_ref[...] = jnp.zeros_like(gdtb_ref)

        for sub in reversed(range(SSD_SUB)):
            chunk(sub, z0_ref, z1_ref, udt_ref, pre_ref, dys_ref, yssm_ref, hprev_ref, dtb_ref, alog_ref, dsk_ref, gs_ref,
                  dz_ref, dpre_ref, dudt_ref, ggs_ref, gdsk_ref, ga_ref, gdtb_ref, g_ref, dxdt_ref.at[sub], dyv_ref.at[sub])

    def chunk(sub, z0_ref, z1_ref, udt_ref, pre_ref, dys_ref, yssm_ref, hprev_ref,
              dtb_ref, alog_ref, dsk_ref, gs_ref,
              dz_ref, dpre_ref, dudt_ref, ggs_ref, gdsk_ref, ga_ref, gdtb_ref,
              g_ref, dxdt_ref, dyv_ref):
        rows = slice(sub * CHUNK, (sub + 1) * CHUNK)
        r = _chunk_terms(pre_ref[rows, :], udt_ref[rows, :], dtb_ref[...], alog_ref[...])
        xbc = r["xbc"]
        xs = xbc[:, :SSD_INNER]
        dt_e = r["dt_e"]
        xdt = xs * dt_e
        xdt_b = xdt.astype(BF)
        reduce_m = _head_reduce_matrix(GROUP_W, 8)

        def head_sums(v):
            return _nn(v.astype(BF), reduce_m)

        onehot16 = lambda h: (_iota((1, SSD_HEADS), 1) == h).astype(F32)
        onecol16 = lambda h: (_iota((SSD_HEADS, 1), 0) == h).astype(F32)

        d_acum = jnp.zeros((CHUNK, SSD_HEADS), F32)
        d_acum_t = jnp.zeros((SSD_HEADS, CHUNK), F32)
        d_alast = jnp.zeros((1, SSD_HEADS), F32)
        place8 = lambda g: (_iota((8, SSD_HEADS), 1) == _iota((8, SSD_HEADS), 0) + 8 * g).astype(BF)
        d_b, d_c = [], []
        for g in range(2):
            gs = slice(g * GROUP_W, (g + 1) * GROUP_W)
            zg = (z0_ref if g == 0 else z1_ref)[rows, :]
            sz = _sigmoid(zg)
            silu_z = zg * sz
            ys = yssm_ref[rows, gs]
            yg = ys * silu_z
            rg = lax.rsqrt(jnp.mean(yg * yg, -1, keepdims=True) + EPS)
            yh = yg * rg
            dys = dys_ref[rows, gs]
            ggs_ref[:, gs] += jnp.sum(dys * yh, 0, keepdims=True)
            dyh = dys * gs_ref[:, gs]
            dyg = rg * (dyh - yh * jnp.mean(dyh * yh, -1, keepdims=True))
            dy = dyg * silu_z
            dz_ref[rows, gs] = (dyg * ys * (sz * (1.0 + zg * (1.0 - sz)))).astype(BF)
            gdsk_ref[:, gs] += jnp.sum(dy * xs[:, gs], 0, keepdims=True)
            dyv_ref[:, gs] = dy
            dy_b = dy.astype(BF)

            bg = xbc[:, SSD_INNER + g * SSD_STATE:SSD_INNER + (g + 1) * SSD_STATE].astype(BF)
            cg = xbc[:, SSD_INNER + (2 + g) * SSD_STATE:SSD_INNER + (3 + g) * SSD_STATE].astype(BF)
            scores = _nt(cg, bg)
            hg = hprev_ref[0, sub, g]
            hg_b = hg.astype(BF)
            gg = g_ref[g]
            gg_b = gg.astype(BF)
            e_a = r["e_a"][:, gs]
            d_out = r["d_out"][:, gs]
            c_dec = r["c_dec"][:, gs]
            zc = _nn(cg, hg_b)
            wv = e_a * dy
            wv_b = wv.astype(BF)
            da_g = head_sums(wv * zc)
            dcg = _nt(wv_b, hg_b)
            d_hprev = _tn(cg, wv_b)
            vg = _nn(bg, gg_b)
            dxdt_g = d_out * vg
            dd_out = head_sums(xdt[:, gs] * vg)
            dbg = _nt((xdt[:, gs] * d_out).astype(BF), gg_b)
            dcd = _exact_nn(jnp.sum(gg * hg, 0, keepdims=True), reduce_m)
            d_out8 = jnp.exp(r["acum"][CHUNK - 1:CHUNK, 8 * g:8 * g + 8] - r["acum"][:, 8 * g:8 * g + 8])
            c_dec8 = jnp.exp(r["acum"][CHUNK - 1:CHUNK, 8 * g:8 * g + 8])
            t8 = dd_out * d_out8
            d_alast = d_alast + _exact_nn(jnp.sum(t8, 0, keepdims=True) + dcd * c_dec8, place8(g))
            d_acum = d_acum + _exact_nn(da_g - t8, place8(g))
            dsc = jnp.zeros((CHUNK, CHUNK), F32)
            for hh in range(8):
                h = g * 8 + hh
                hs = slice(h * SSD_HEAD_DIM, (h + 1) * SSD_HEAD_DIM)
                lam = _head_decay(r, h)
                m = scores * lam
                dyh_b = dy_b[:, hh * SSD_HEAD_DIM:(hh + 1) * SSD_HEAD_DIM]
                dm = _nt(dyh_b, xdt_b[:, hs])
                tm_ = dm * m
                d_acum = d_acum + jnp.sum(tm_, 1, keepdims=True) * onehot16(h)
                d_acum_t = d_acum_t + onecol16(h) * jnp.sum(tm_, 0, keepdims=True)
                dsc = dsc + dm * lam
                dxdt_ref[:, hs] = _tn(m.astype(BF), dyh_b) + dxdt_g[:, hh * SSD_HEAD_DIM:(hh + 1) * SSD_HEAD_DIM]
            dsc_b = dsc.astype(BF)
            d_c.append(dcg + _nn(dsc_b, bg))
            d_b.append(dbg + _tn(dsc_b, cg))
            g_ref[g] = d_hprev + c_dec * gg

        eye = (_iota((CHUNK, CHUNK), 0) == _iota((CHUNK, CHUNK), 1)).astype(BF)
        d_acum = d_acum - _exact_nt_left(eye, d_acum_t)
        is_last = (_iota((CHUNK, 1), 0) == CHUNK - 1).astype(F32)
        d_acum = d_acum + is_last * d_alast
        triu = (_iota((CHUNK, CHUNK), 0) <= _iota((CHUNK, CHUNK), 1)).astype(BF)
        d_da = _exact_nn_left(triu, d_acum)
        dt = r["dt"]
        ga_ref[...] += jnp.sum(d_da * dt, 0, keepdims=True)
        dxdt = dxdt_ref[...]
        reduce16 = _head_reduce_matrix(SSD_INNER, SSD_HEADS)
        d_dt = d_da * r["a"] + _nn((dxdt * xs).astype(BF), reduce16)
        d_udt = d_dt * _sigmoid(r["dtp"])
        gdtb_ref[...] += jnp.sum(d_udt, 0, keepdims=True)
        dudt_ref[rows, :] = jnp.zeros((CHUNK, dudt_ref.shape[1]), BF)
        dudt_ref[rows, 0:SSD_HEADS] = d_udt.astype(BF)
        pre, sg = r["pre"], r["sg"]
        dsilu = sg * (1.0 + pre * (1.0 - sg))
        dpre_ref[rows, 0:SSD_INNER] = (dsk_ref[...] * dyv_ref[...] + dxdt * dt_e) * dsilu[:, 0:SSD_INNER]
        for g in range(2):
            bs = slice(SSD_INNER + g * SSD_STATE, SSD_INNER + (g + 1) * SSD_STATE)
            cs = slice(SSD_INNER + (2 + g) * SSD_STATE, SSD_INNER + (3 + g) * SSD_STATE)
            dpre_ref[rows, bs] = d_b[g] * dsilu[:, bs]
            dpre_ref[rows, cs] = d_c[g] * dsilu[:, cs]

    t = nb * seq
    vec = _const_spec((1, SSD_INNER))
    small = _const_spec((1, SSD_HEADS))
    return pl.pallas_call(
        body, name="ssd_bwd", grid=(nb, ns),
        in_specs=specs + [pl.BlockSpec((SSD_ROWS, CONV_CH), lambda b, c: (row(b, c), 0)),
                          pl.BlockSpec((SSD_ROWS, SSD_INNER), lambda b, c: (row(b, c), 0)),
                          pl.BlockSpec((SSD_ROWS, SSD_INNER), lambda b, c: (row(b, c), 0)),
                          pl.BlockSpec((1, SSD_SUB, 2, SSD_STATE, GROUP_W), lambda b, c: (b, cidx(c), 0, 0, 0)),
                          small, small, vec, vec],
        out_specs=[pl.BlockSpec((SSD_ROWS, SSD_INNER), lambda b, c: (row(b, c), 0)),
                   pl.BlockSpec((SSD_ROWS, CONV_CH), lambda b, c: (row(b, c), 0)),
                   pl.BlockSpec((SSD_ROWS, 128), lambda b, c: (row(b, c), 0)),
                   vec, vec, small, small],
        out_shape=[_out((t, SSD_INNER), BF), _out((t, CONV_CH), F32),
                   _out((t, 128), BF), _out((1, SSD_INNER), F32),
                   _out((1, SSD_INNER), F32), _out((1, SSD_HEADS), F32),
                   _out((1, SSD_HEADS), F32)],
        scratch_shapes=[pltpu.VMEM((2, SSD_STATE, GROUP_W), F32), pltpu.VMEM((SSD_SUB, CHUNK, SSD_INNER), F32),
                        pltpu.VMEM((SSD_SUB, CHUNK, SSD_INNER), F32)],
        compiler_params=_cp(("arbitrary", "arbitrary"), 56),
    )(*_pin(proj, proj, proj, pre, d_yssd, yssm, h_prev, dt_bias, a_log, dskip_e, g_ssd))


def _grad_w_out(y_pool, y_ssd, d_o):
    t, d = d_o.shape
    tk = POOL_WIDTH
    n_s = SSD_INNER // tk

    def body(p_ref, s_ref, g_ref, o_ref):
        i = pl.program_id(0)

        @pl.when(i == 0)
        def _():
            o_ref[...] = _tn(p_ref[...], g_ref[...]).astype(BF)

        @pl.when(i > 0)
        def _():
            o_ref[...] = _tn(s_ref[...], g_ref[...]).astype(BF)

    return pl.pallas_call(
        body, name="grad_w_out", grid=(1 + n_s,),
        in_specs=[pl.BlockSpec((t, tk), lambda i: (0, 0)), pl.BlockSpec((t, tk), lambda i: (0, jnp.maximum(i - 1, 0))),
                  pl.BlockSpec((t, d), lambda i: (0, 0))],
        out_specs=pl.BlockSpec((tk, d), lambda i: (i, 0)),
        out_shape=_out((POOL_WIDTH + SSD_INNER, d), BF),
        compiler_params=_cp(("parallel",), 56))(*_pin(y_pool, y_ssd, d_o))


def _grad_w_in_t(d_upool, d_z, d_uxbc, d_udt, u1):
    t, d = u1.shape
    tk = 512
    n_z, n_x = SSD_INNER // tk, CONV_CH // tk

    def body(p_ref, z_ref, x_ref, dt_ref, u_ref, o_ref):
        i = pl.program_id(0)

        @pl.when(i == 0)
        def _():
            o_ref[...] = _tn(p_ref[...], u_ref[...]).astype(BF)

        @pl.when((i >= 1) & (i < 1 + n_z))
        def _():
            o_ref[...] = _tn(z_ref[...], u_ref[...]).astype(BF)

        @pl.when((i >= 1 + n_z) & (i < 1 + n_z + n_x))
        def _():
            o_ref[...] = _tn(x_ref[...], u_ref[...]).astype(BF)

        @pl.when(i == 1 + n_z + n_x)
        def _():
            o_ref[0:128, :] = _tn(dt_ref[...], u_ref[...]).astype(BF)

    return pl.pallas_call(
        body, name="grad_w_in", grid=(2 + n_z + n_x,),
        in_specs=[pl.BlockSpec((t, tk), lambda i: (0, 0)),
                  pl.BlockSpec((t, tk), lambda i: (0, jnp.clip(i - 1, 0, n_z - 1))),
                  pl.BlockSpec((t, tk), lambda i: (0, jnp.clip(i - 1 - n_z, 0, n_x - 1))),
                  pl.BlockSpec((t, 128), lambda i: (0, 0)), pl.BlockSpec((t, d), lambda i: (0, 0))],
        out_specs=pl.BlockSpec((tk, d), lambda i: (i, 0)),
        out_shape=_out((IN_PAD, d), BF),
        compiler_params=_cp(("parallel",), 56))(*_pin(d_upool, d_z, d_uxbc, d_udt, u1))


def _conv_bwd(d_pre, proj, conv_w, nb, seq):
    ts = min(256, seq)
    nt = seq // ts
    hb = ts // CONV_HALO
    last_block = nb * seq // CONV_HALO - 1
    n_ext = CHUNK + CONV_HALO

    def body(dp_ref, dnext_ref, u_ref, cw_ref, du_ref, gw_ref, gb_ref):
        b = pl.program_id(0)
        i = pl.program_id(1)

        @pl.when((b == 0) & (i == 0))
        def _():
            gw_ref[...] = jnp.zeros_like(gw_ref)
            gb_ref[...] = jnp.zeros_like(gb_ref)

        for c0 in range(0, CONV_CH, 128):
            cs = slice(c0, c0 + 128)
            cw = cw_ref[:, cs]
            gw = [0.0] * 4
            gb = 0.0
            for r0 in range(0, ts, CHUNK):
                dp = dp_ref[r0:r0 + CHUNK, cs]
                u = u_ref[r0:r0 + CHUNK, cs]
                if r0 + CHUNK < ts:
                    below = dp_ref[r0 + CHUNK:r0 + CHUNK + CONV_HALO, cs]
                else:
                    below = jnp.where(i == nt - 1, 0.0, dnext_ref[:, cs])
                ext_d = jnp.concatenate([dp, below], 0)
                du = dp * cw[3:4]
                gw[3] = gw[3] + jnp.sum(dp * u, 0, keepdims=True)
                for k in (2, 1, 0):
                    shifted = pltpu.roll(ext_d, n_ext - (3 - k), 0)[:CHUNK]
                    du = du + shifted * cw[k:k + 1]
                    gw[k] = gw[k] + jnp.sum(shifted * u, 0, keepdims=True)
                gb = gb + jnp.sum(dp, 0, keepdims=True)
                du_ref[r0:r0 + CHUNK, cs] = du.astype(BF)
            for k in range(4):
                gw_ref[k:k + 1, cs] += gw[k]
            gb_ref[:, cs] += gb

    return pl.pallas_call(
        body, name="conv_bwd", grid=(nb, nt),
        in_specs=[pl.BlockSpec((ts, CONV_CH), lambda b, i: (b * nt + i, 0)),
                  pl.BlockSpec((CONV_HALO, CONV_CH), lambda b, i: (jnp.minimum((b * nt + i + 1) * hb, last_block), 0)),
                  pl.BlockSpec((ts, CONV_CH), lambda b, i: (b * nt + i, 1)),
                  pl.BlockSpec((4, CONV_CH), lambda b, i: (0, 0))],
        out_specs=[pl.BlockSpec((ts, CONV_CH), lambda b, i: (b * nt + i, 0)),
                   pl.BlockSpec((8, CONV_CH), lambda b, i: (0, 0)), pl.BlockSpec((1, CONV_CH), lambda b, i: (0, 0))],
        out_shape=[_out((nb * seq, CONV_CH), BF), _out((8, CONV_CH), F32),
                   _out((1, CONV_CH), F32)],
        compiler_params=_cp(("arbitrary", "arbitrary"), 48))(*_pin(d_pre, d_pre, proj, conv_w))


def _in_proj_bwd(d_parts, w_in_t, x, dh1, mod3, g_mix, seq, token):
    t, d = x.shape
    nb = t // seq
    tm = min(ROW_TILE, seq)
    tps = seq // tm

    widths = [p.shape[1] for p in d_parts]

    def body(d0_ref, d1_ref, d2_ref, d3_ref, w_ref, x_ref, dh1_ref, mod_ref, g_ref, tok_ref, gx_ref, acc_ref, gg_ref):
        i = pl.program_id(0)

        @pl.when(i == 0)
        def _():
            gg_ref[...] = jnp.zeros_like(gg_ref)

        @pl.when(i % tps == 0)
        def _():
            acc_ref[...] = jnp.zeros_like(acc_ref)

        gg = a_shift = a_scale = 0.0
        for rows in _sub_rows(tm):
            d_cat = jnp.concatenate([p_ref[rows, :] for p_ref in (d0_ref, d1_ref, d2_ref)], 1)
            du = _nn(d_cat, w_ref[0:OFF_DT, :]) + _nn(d3_ref[rows, 0:IN_WIDTH - OFF_DT], w_ref[OFF_DT:IN_WIDTH, :])
            xv = x_ref[rows, :]
            r = lax.rsqrt(jnp.mean(xv * xv, -1, keepdims=True) + EPS)
            hh = xv * r
            n1 = hh * g_ref[...]
            dn1 = du * (1.0 + mod_ref[0, 1:2, :])
            dhat = dn1 * g_ref[...]
            gx_ref[rows, :] = dh1_ref[rows, :] + r * (dhat - hh * jnp.mean(dhat * hh, -1, keepdims=True))
            gg = gg + jnp.sum(dn1 * hh, 0, keepdims=True)
            a_shift = a_shift + jnp.sum(du, 0, keepdims=True)
            a_scale = a_scale + jnp.sum(du * n1, 0, keepdims=True)
        gg_ref[...] += gg
        acc_ref[0, 0:1, :] += a_shift
        acc_ref[0, 1:2, :] += a_scale

    row = lambda i: (i, 0)
    vec = pl.BlockSpec((1, d), lambda i: (0, 0))
    return pl.pallas_call(
        body, name="in_proj_bwd", grid=(t // tm,),
        in_specs=[pl.BlockSpec((tm, wd), row) for wd in widths] +
                 [_RESIDENT, pl.BlockSpec((tm, d), row),
                  pl.BlockSpec((tm, d), row), pl.BlockSpec((1, N_MOD, d), lambda i: (i // tps, 0, 0)), vec, _token_spec()],
        out_specs=[pl.BlockSpec((tm, d), row), pl.BlockSpec((1, 8, d), lambda i: (i // tps, 0, 0)), vec],
        out_shape=[_out((t, d), F32), _out((nb, 8, d), F32),
                   _out((1, d), F32)],
        compiler_params=_cp(("arbitrary",), 40))(*_pin(*d_parts), w_in_t, *_pin(x, dh1, mod3, g_mix, token))


_VEC_LAYOUT = (("g_mix", 1024), ("conv_b", 1536), ("g_ssd", 1024), ("pool_scale", 512), ("g_mlp", 1024),
               ("g_final", 1024), ("dt_bias", 128), ("a_log", 128), ("d_skip_lanes", 1024), ("sq_err", 1024))
_VEC_OFFSET = {}
_off = 0
for _name, _n in _VEC_LAYOUT:
    _VEC_OFFSET[_name] = _off
    _off += _n
_VEC_LANES = _off
_SMALL_PARAMS = ("b_ada", "g_mix", "conv_w", "conv_b", "dt_bias", "a_log", "d_skip", "g_ssd", "w_pool", "pool_scale",
                 "g_mlp", "g_final")


def _pack_vec(parts):
    cols = []
    for name, n in _VEC_LAYOUT:
        v = parts[name]
        if v.shape[1] < n:
            v = jnp.pad(v, ((0, 0), (0, n - v.shape[1])))
        cols.append(v)
    return jnp.concatenate(cols, 1)


def _small_adam(vec_all, wpool_all, convw_all, dmod_all, params):
    names = _SMALL_PARAMS
    nin = 4 + 3 * len(names)

    def body(*refs):
        vec_ref, wp_ref, cw_ref, dm_ref = refs[:4]
        prm = {n: refs[4 + 3 * i:7 + 3 * i] for i, n in enumerate(names)}
        loss_ref = refs[nin]
        outs = {n: refs[nin + 1 + 4 * i:nin + 5 + 4 * i] for i, n in enumerate(names)}
        vsum = vec_ref[0]
        for s in range(1, N_DEV):
            vsum = vsum + vec_ref[s]

        def lanes(name, n):
            off = _VEC_OFFSET[name]
            return vsum[:, off:off + n]

        grads = {n: lanes(n, prm[n][0].shape[1]) for n in ("g_mix", "conv_b", "g_ssd", "pool_scale", "g_mlp", "g_final", "dt_bias")}
        grads["a_log"] = lanes("a_log", SSD_HEADS) * (-jnp.exp(prm["a_log"][0][...]))
        per_lane = jnp.broadcast_to(lanes("d_skip_lanes", SSD_INNER), (8, SSD_INNER))
        grads["d_skip"] = _exact_nn(per_lane, _head_reduce_matrix(SSD_INNER, SSD_HEADS))[0:1]
        gwp = wp_ref[0].astype(F32)
        gcw = cw_ref[0]
        gb = jnp.sum(dm_ref[0], 0, keepdims=True)
        for s in range(1, N_DEV):
            gwp = gwp + wp_ref[s].astype(F32)
            gcw = gcw + cw_ref[s]
            gb = gb + jnp.sum(dm_ref[s], 0, keepdims=True)
        grads["w_pool"] = gwp
        grads["conv_w"] = gcw[0:4]
        grads["b_ada"] = gb
        total = jnp.sum(lanes("sq_err", D_MODEL), 1, keepdims=True) * (0.5 / D_MODEL)
        loss_ref[...] = jnp.broadcast_to(total, loss_ref.shape)
        for n in names:
            w_ref, m_ref, v_ref = prm[n]
            g = grads[n]
            d, m2, v2 = _adam_math(w_ref[...], g, m_ref[...], v_ref[...])
            g_ref, d_ref, m2_ref, v2_ref = outs[n]
            g_ref[...] = g
            d_ref[...] = d
            m2_ref[...] = m2
            v2_ref[...] = v2

    flat = [vec_all, wpool_all, convw_all, dmod_all]
    out_shape = [jax.ShapeDtypeStruct((1, 128), F32)]
    for n in names:
        flat += list(params[n])
        out_shape += [jax.ShapeDtypeStruct(params[n][0].shape, F32)] * 4
    vm = pl.BlockSpec(memory_space=pltpu.VMEM)
    res = pl.pallas_call(body, name="small_adam", out_shape=out_shape, in_specs=[vm] * len(flat),
                         out_specs=[vm] * len(out_shape), compiler_params=_cp(vmem_mb=48))(*flat)
    return res[0], {n: res[1 + 4 * i:5 + 4 * i] for i, n in enumerate(names)}


_WEIGHTS = ("w_ada", "b_ada", "g_mix", "w_in", "conv_w", "conv_b", "dt_bias", "a_log", "d_skip", "g_ssd", "w_pool",
            "pool_scale", "w_out", "g_mlp", "w_up", "w_down", "g_final")


def _local_step(x2, tg2, mod3, seq, w_in_t, first_token, weights_arrived, weights_later, start_reduce, before_last,
                conv_w_full, sp):
    t, d = x2.shape
    nb = t // seq
    dskip_e = jnp.repeat(sp["d_skip"], SSD_HEAD_DIM, axis=1)
    proj, u1 = _in_proj(x2, mod3, sp["g_mix"], w_in_t, seq, first_token)
    y_ssd, yssm, h_prev, pre = _ssd_fwd(proj, conv_w_full, sp["conv_b"], sp["dt_bias"], sp["a_log"], dskip_e, sp["g_ssd"], nb, seq)
    y_pool, p = _pool_fwd(proj, sp["w_pool"], sp["pool_scale"], nb, seq, weights_arrived(y_ssd))
    w_out_f, w_up4, w_down4 = weights_later(y_pool)
    w_down_f = w_down4.reshape(D_FF, d)
    h1, o, u2 = _out_proj(y_pool, y_ssd, w_out_f, x2, mod3, sp["g_mlp"], seq)
    a_up = _mlp_up(u2, w_up4)
    d_dn, dh2, sq, gg_final, d_gf = _mlp_down_loss(a_up, w_down_f, h1, mod3, sp["g_final"], tg2, seq)

    gw_down = _tn_matmul(a_up, d_dn, 512, d, "grad_w_down", square_relu=True)
    tok = start_reduce("w_down", gw_down.reshape(N_CHIPS, D_FF // N_CHIPS, d))
    d_a = _mlp_down_bwd(d_dn, w_down4, a_up, tok)
    gw_up4 = _tn_matmul(u2, d_a, 512, d, "grad_w_up", out3=True)
    tok = start_reduce("w_up", gw_up4)
    dh1, d_o, accf, gg_mlp = _mlp_up_bwd(d_a, w_up4, h1, dh2, o, mod3, sp["g_mlp"], seq, tok)
    gw_out = _grad_w_out(y_pool, y_ssd, d_o)
    tok = start_reduce("w_out", gw_out.reshape(N_CHIPS, gw_out.shape[0] // N_CHIPS, d))
    d_ypool, d_yssd = _out_proj_bwd(d_o, w_out_f, tok)
    d_upool, gw_pool, g_ps = _pool_bwd(d_ypool, p, sp["w_pool"], sp["pool_scale"], nb, seq)
    d_z, d_pre, d_udt, gg_ssd, gdsk, ga, gdtb = _ssd_bwd(proj, pre, d_yssd, yssm, h_prev, sp["dt_bias"], sp["a_log"],
                                                        dskip_e, sp["g_ssd"], nb, seq)
    d_uxbc, gconvw, gconvb = _conv_bwd(d_pre, proj, conv_w_full, nb, seq)
    gw_in_t = _grad_w_in_t(d_upool, d_z, d_uxbc, d_udt, u1)
    shard_rows = IN_WIDTH // N_CHIPS
    tok = start_reduce("w_in", jnp.stack([gw_in_t[k * shard_rows:(k + 1) * shard_rows] for k in range(N_CHIPS)]))
    gx, accm, gg_mix = _in_proj_bwd([d_upool, d_z, d_uxbc, d_udt], w_in_t, x2, dh1, mod3, sp["g_mix"], seq, before_last(tok))

    d_mod = jnp.concatenate([accm[:, 0], accm[:, 1], accf[:, 2], accf[:, 0], accf[:, 1], d_gf[:, 0]], 1)
    vec = _pack_vec({"g_mix": gg_mix, "conv_b": gconvb, "g_ssd": gg_ssd, "pool_scale": g_ps, "g_mlp": gg_mlp,
                     "g_final": gg_final, "dt_bias": gdtb, "a_log": ga, "d_skip_lanes": gdsk, "sq_err": sq})
    return gx, d_mod, vec, gw_pool, gconvw


def kernel(x, c, w_ada, b_ada, g_mix, w_in, conv_w, conv_b, dt_bias, a_log, d_skip, g_ssd, w_pool, pool_scale, w_out, g_mlp, w_up, w_down, g_final, loss_target, m_w_ada, m_b_ada, m_g_mix, m_w_in, m_conv_w, m_conv_b, m_dt_bias, m_a_log, m_d_skip, m_g_ssd, m_w_pool, m_pool_scale, m_w_out, m_g_mlp, m_w_up, m_w_down, m_g_final, v_w_ada, v_b_ada, v_g_mix, v_w_in, v_conv_w, v_conv_b, v_dt_bias, v_a_log, v_d_skip, v_g_ssd, v_w_pool, v_pool_scale, v_w_out, v_g_mlp, v_w_up, v_w_down, v_g_final):
    nb, seq, d = x.shape
    t = nb * seq
    xi, yi, ci = _mesh_pos()
    chip = 2 * xi + yi
    me = 4 * xi + 2 * yi + ci
    ada_cols = w_ada.shape[2]
    conv_cols = conv_w.shape[2]
    in_cols = w_in.shape[2]
    w_in_s, m_w_in_s, v_w_in_s = w_in[0].T, m_w_in[0].T, v_w_in[0].T

    w_in_b = w_in_s.astype(BF)
    i_send, i_recv, i_src, i_land, in_token = _ici_start(
        [w_in_b], [jax.ShapeDtypeStruct((N_CHIPS,) + w_in_b.shape, BF)], _gather_sent, _gather_landing, "gather_start_w_in",
        after=w_in_b)
    c_send, c_recv, c_src, c_land, c_token = _exchange_start([c + in_token[0:1, 0:1], conv_w[0]], ALL_PEERS, "cond_start")
    c_own, c_got = _exchange_wait(c_send, c_recv, c_src, c_land, ALL_PEERS, c_token, "cond_wait")
    c8, convw8 = [lax.dynamic_update_slice(got, mine[None], (me,) + (0,) * mine.ndim) for got, mine in zip(c_got, c_own)]
    c_all = c8.reshape(N_DEV * nb, d)
    conv_w_full = convw8[0::2].transpose(1, 0, 2).reshape(4, N_CHIPS * conv_cols)
    b_shard = lax.dynamic_slice(b_ada, (0, chip * ada_cols), (1, ada_cols))
    mod_part, c_act = _ada_mod(c_all, w_ada[0], b_shard, in_token)
    mod_rows = mod_part.reshape(N_DEV, nb, ada_cols)
    m_send, m_recv, m_src, m_land, _ = _ici_start(
        [mod_rows], [jax.ShapeDtypeStruct((N_CHIPS, nb, ada_cols), F32)], _mod_sent, _mod_landing, "mod_start", after=mod_part)

    later = [w_out[0].astype(BF), w_up[0].astype(BF), w_down[0].astype(BF)]
    in_shard, in_land = _ici_wait(i_send, i_recv, i_src, i_land, [m_src[0]] + later, _gather_sent, _gather_landing,
                                  "gather_wait_w_in")
    (w_in4,) = _gather_finish(in_land, in_shard)
    w_in_t = w_in4.reshape(N_CHIPS * in_cols, d)
    mod_mine, mod_land = _ici_wait(m_send, m_recv, m_src, m_land, w_in_t, _mod_sent, _mod_landing, "mod_wait")
    mod_own = lax.dynamic_slice(mod_mine[0], (me, 0, 0), (1, nb, ada_cols))
    mod4 = lax.dynamic_update_slice(mod_land[0], mod_own, (chip, 0, 0))
    mod3 = mod4.transpose(1, 0, 2).reshape(nb, N_MOD, d)
    g_send, g_recv, g_src, g_land, first_token = _ici_start(
        later, [jax.ShapeDtypeStruct((N_CHIPS,) + s.shape, BF) for s in later], _gather_sent, _gather_landing, "gather_start",
        after=w_in4)

    def weights_arrived(after):
        shards, lands = _ici_wait(g_send, g_recv, g_src, g_land, after, _gather_sent, _gather_landing, "gather_wait")
        pending["forward"] = _forward_start(lands, "forward_start") + (shards,)
        return pending["forward"][3]

    def weights_later(after):
        f_send, f_recv, f_land, _, shards = pending["forward"]
        lands = _forward_wait(f_send, f_recv, f_land, after, "forward_wait")
        w_out4, w_up4, w_down4 = [lax.dynamic_update_slice(land, shard[None], (chip, 0, 0)) for land, shard in zip(lands, shards)]
        return w_out4.reshape(N_CHIPS * w_out.shape[1], d), w_up4, w_down4

    pending = {}

    def start_reduce(name, grad4):
        pending[name] = _reduce_start(grad4, "reduce_start_" + name)
        return pending[name][4]

    pos = jnp.stack([ci, chip, me]).astype(jnp.int32)
    early = ("w_out", "w_up", "w_down")

    def summed_half(name, after):
        r_send, r_recv, r_src, r_land, _ = pending[name]
        own, recv = _reduce_wait(r_send, r_recv, r_src, r_land, after, "reduce_wait_" + name)
        return _sum_eight(recv, own, pos)

    def before_last(token):
        pending["early_halves"] = _exchange_start([summed_half(n, token) for n in early], SIBLING, "halves_start_early")
        return pending["early_halves"][4]

    sp = dict(g_mix=g_mix, conv_b=conv_b, dt_bias=dt_bias, a_log=a_log, d_skip=d_skip, g_ssd=g_ssd,
              w_pool=w_pool[0], pool_scale=pool_scale, g_mlp=g_mlp, g_final=g_final.reshape(1, d))
    gx, d_mod, vec, gw_pool, gconvw = _local_step(
        x.reshape(t, d), loss_target.reshape(t, d), mod3, seq, w_in_t, first_token, weights_arrived, weights_later, start_reduce,
        before_last, conv_w_full, sp)

    small_parts = [vec, gw_pool.reshape(4 * POOL_GROUP, POOL_GROUP).astype(BF), gconvw, d_mod]
    s_send, s_recv, s_src, s_land, s_token = _exchange_start(small_parts, ALL_PEERS, "small_start")
    h_send, h_recv, h_src, h_land, h_token = _exchange_start([summed_half("w_in", s_token)], SIBLING, "halves_start")
    e_send, e_recv, e_src, e_land, _ = pending["early_halves"]
    e_own, e_got = _exchange_wait(e_send, e_recv, e_src, e_land, SIBLING, h_token, "halves_wait_early")
    res = {}
    for i, (n, w, m, v) in enumerate((("w_out", w_out, m_w_out, v_w_out), ("w_up", w_up, m_w_up, v_w_up),
                                      ("w_down", w_down, m_w_down, v_w_down))):
        g, dl, m2, v2 = _adam_big(e_own[i], e_got[i], w[0], m[0], v[0], pos)
        res[n] = (g[None], dl[None], m2[None], v2[None])

    s_own, s_got = _exchange_wait(s_send, s_recv, s_src, s_land, ALL_PEERS, res["w_down"][1], "small_wait")
    vec8, wpool8, convw8g, dmod8 = [lax.dynamic_update_slice(got, mine[None], (me,) + (0,) * mine.ndim)
                                    for got, mine in zip(s_got, s_own)]
    convw8s = lax.dynamic_slice(convw8g, (0, 0, chip * conv_cols), (N_DEV, 8, conv_cols))
    m_in = dict(b_ada=m_b_ada, g_mix=m_g_mix, conv_w=m_conv_w[0], conv_b=m_conv_b, dt_bias=m_dt_bias, a_log=m_a_log,
                d_skip=m_d_skip, g_ssd=m_g_ssd, w_pool=m_w_pool.reshape(4 * POOL_GROUP, POOL_GROUP), pool_scale=m_pool_scale,
                g_mlp=m_g_mlp, g_final=m_g_final.reshape(1, d))
    v_in = dict(b_ada=v_b_ada, g_mix=v_g_mix, conv_w=v_conv_w[0], conv_b=v_conv_b, dt_bias=v_dt_bias, a_log=v_a_log,
                d_skip=v_d_skip, g_ssd=v_g_ssd, w_pool=v_w_pool.reshape(4 * POOL_GROUP, POOL_GROUP), pool_scale=v_pool_scale,
                g_mlp=v_g_mlp, g_final=v_g_final.reshape(1, d))
    w_small = dict(sp, b_ada=b_ada, conv_w=conv_w[0], w_pool=w_pool.reshape(4 * POOL_GROUP, POOL_GROUP))
    loss_row, small = _small_adam(vec8, wpool8, convw8s, dmod8, {n: (w_small[n], m_in[n], v_in[n]) for n in _SMALL_PARAMS})

    dmod_all = dmod8.reshape(N_DEV * nb, N_CHIPS * ada_cols)
    dmod_cols = lax.dynamic_slice(dmod_all, (0, chip * ada_cols), (N_DEV * nb, ada_cols))
    res.update({n: tuple(r.reshape(w.shape) for r in small[n])
                for n, w in (("b_ada", b_ada), ("g_mix", g_mix), ("conv_w", conv_w), ("conv_b", conv_b), ("dt_bias", dt_bias),
                             ("a_log", a_log), ("d_skip", d_skip), ("g_ssd", g_ssd), ("w_pool", w_pool),
                             ("pool_scale", pool_scale), ("g_mlp", g_mlp), ("g_final", g_final))})
    g_ada, d_ada, m_ada, v_ada = _adam_ada(c_act.T.astype(BF), dmod_cols, w_ada[0], m_w_ada[0], v_w_ada[0])
    res["w_ada"] = (g_ada[None], d_ada[None], m_ada[None], v_ada[None])
    h_own, h_got = _exchange_wait(h_send, h_recv, h_src, h_land, SIBLING, g_ada, "halves_wait")
    rows3 = lambda a: jnp.transpose(a, (2, 0, 1))
    res["w_in"] = tuple(jnp.transpose(r, (1, 2, 0))
                        for r in _adam_rows(h_own[0], h_got[0], rows3(w_in), rows3(m_w_in), rows3(v_w_in), pos))

    loss = loss_row[0, 0]
    return (loss, gx.reshape(nb, seq, d), *[res[n][0] for n in _WEIGHTS], *[res[n][1] for n in _WEIGHTS],
            *[res[n][2] for n in _WEIGHTS], *[res[n][3] for n in _WEIGHTS])
```

```python
import jax
import jax.numpy as jnp
from jax import lax
from jax.experimental import pallas as pl
from jax.experimental.pallas import tpu as pltpu

F32 = jnp.float32
BF = jnp.bfloat16
MESH = pl.DeviceIdType.MESH

EPS = 1e-5
D_MODEL = 1024
POOL_WIDTH = 512
POOL_WINDOWS = (2, 4, 8, 16)
POOL_GROUP = 128
SSD_INNER = 1024
SSD_HEADS = 16
SSD_HEAD_DIM = 64
SSD_STATE = 128
GROUP_W = 512
CHUNK = 128
CONV_CH = 1536
OFF_DT = 3072
IN_WIDTH = 3088
IN_PAD = 3200
D_FF = 4096
N_MOD = 6
N_CHIPS = 4
N_DEV = 8
HALO = 16
CONV_HALO = 8

ADAM_LR = 0.001
ADAM_B1 = 0.9
ADAM_B2 = 0.999
ADAM_EPS = 1e-08
ADAM_WD = 0.01
ADAM_STEP = 10

VMEM_BYTES_V7X = 64 * 1024 * 1024


def _cp(semantics=None, vmem_mb=48, **kw):
    assert vmem_mb * 1024 * 1024 < VMEM_BYTES_V7X
    args = dict(vmem_limit_bytes=vmem_mb * 1024 * 1024, **kw)
    if semantics is not None:
        args["dimension_semantics"] = semantics
    return pltpu.CompilerParams(**args)


def _out(shape, dtype):
    return pltpu.HBM(shape, dtype)


def _pin(*arrays):
    return [pltpu.with_memory_space_constraint(a, pltpu.HBM) for a in arrays]


TOKEN_SHAPE = (8, 128)


def _order_operand(a):
    if a.shape == TOKEN_SHAPE:
        return pl.BlockSpec(memory_space=pltpu.VMEM), a
    return pl.BlockSpec(memory_space=pltpu.HBM), pltpu.with_memory_space_constraint(a, pltpu.HBM)


def _nn(a, b):
    return jnp.dot(a, b, preferred_element_type=F32)


def _nt(a, b):
    return lax.dot_general(a, b, (((1,), (1,)), ((), ())), preferred_element_type=F32)


def _tn(a, b):
    return lax.dot_general(a, b, (((0,), (0,)), ((), ())), preferred_element_type=F32)


def _split3(v):
    hi = v.astype(BF)
    r1 = v - hi.astype(F32)
    mid = r1.astype(BF)
    lo = (r1 - mid.astype(F32)).astype(BF)
    return hi, mid, lo


def _exact_nn(v, m01):
    hi, mid, lo = _split3(v)
    return _nn(hi, m01) + _nn(mid, m01) + _nn(lo, m01)


def _exact_nn_left(m01, v):
    hi, mid, lo = _split3(v)
    return _nn(m01, hi) + _nn(m01, mid) + _nn(m01, lo)


def _exact_nt_left(m01, v):
    hi, mid, lo = _split3(v)
    return _nt(m01, hi) + _nt(m01, mid) + _nt(m01, lo)


def _sigmoid(v):
    return 1.0 / (1.0 + jnp.exp(-v))


def _iota(shape, dim):
    return lax.broadcasted_iota(jnp.int32, shape, dim)


def _head_expand_matrix(heads, width):
    return (_iota((heads, width), 1) // SSD_HEAD_DIM == _iota((heads, width), 0)).astype(BF)


def _head_reduce_matrix(width, heads):
    return (_iota((width, heads), 0) // SSD_HEAD_DIM == _iota((width, heads), 1)).astype(BF)


def _mesh_pos():
    return lax.axis_index("x"), lax.axis_index("y"), lax.axis_index("c")


def _flip(v, bit):
    return v + bit - 2 * bit * v


_HBM = pl.BlockSpec(memory_space=pltpu.HBM)
_SEM = pl.BlockSpec(memory_space=pltpu.SEMAPHORE)
_DATAFLOW = pltpu.SideEffectType.DATAFLOW_SIDE_EFFECTING


def _peer_chip(x, y, j):
    return _flip(x, (j >> 1) & 1), _flip(y, j & 1)


def _ici_start(srcs, land_shapes, sent, landing, name, after):
    n = len(srcs)

    def body(*refs):
        src_refs, land_refs = refs[:n], refs[n:2 * n]
        send_sems, recv_sems = refs[2 * n + 1], refs[2 * n + 2]
        token = refs[-1]
        x, y, c = _mesh_pos()
        for j in range(1, N_CHIPS):
            px, py = _peer_chip(x, y, j)
            for a in range(n):
                pltpu.make_async_remote_copy(
                    src_ref=sent(src_refs[a], c, 2 * px + py), dst_ref=landing(land_refs[a], c, 2 * x + y),
                    send_sem=send_sems.at[a * (N_CHIPS - 1) + j - 1], recv_sem=recv_sems.at[a * (N_CHIPS - 1) + j - 1],
                    device_id=(px, py, c), device_id_type=MESH).start()
        token[...] = jnp.zeros_like(token)

    sems = pltpu.SemaphoreType.DMA((n * (N_CHIPS - 1),))
    after_spec, after = _order_operand(after)
    lands = [pltpu.with_memory_space_constraint(lax.empty(s.shape, s.dtype), pltpu.HBM) for s in land_shapes]
    outs = pl.pallas_call(
        body, name=name,
        out_shape=(sems, sems, *[pltpu.HBM(s.shape, s.dtype) for s in srcs],
                   *[pltpu.HBM(s.shape, s.dtype) for s in land_shapes], jax.ShapeDtypeStruct((8, 128), F32)),
        in_specs=[_HBM] * (2 * n) + [after_spec],
        out_specs=[_SEM, _SEM] + [_HBM] * (2 * n) + [pl.BlockSpec(memory_space=pltpu.VMEM)],
        input_output_aliases={i: 2 + i for i in range(2 * n)},
        compiler_params=pltpu.CompilerParams(has_side_effects=_DATAFLOW),
    )(*_pin(*srcs), *lands, after)
    return outs[0], outs[1], outs[2:2 + n], outs[2 + n:2 + 2 * n], outs[-1]


def _ici_wait(send_sems, recv_sems, src_thru, land_thru, after, sent, landing, name):
    n = len(src_thru)
    afters = [_order_operand(a) for a in (after if isinstance(after, (list, tuple)) else [after])]

    def body(*refs):
        src_refs, land_refs = refs[:n], refs[n:2 * n]
        send_sems, recv_sems = refs[2 * n], refs[2 * n + 1]
        x, y, c = _mesh_pos()
        for j in range(1, N_CHIPS):
            px, py = _peer_chip(x, y, j)
            for a in range(n):
                cp = pltpu.make_async_remote_copy(
                    src_ref=sent(src_refs[a], c, 2 * px + py), dst_ref=landing(land_refs[a], c, 2 * px + py),
                    send_sem=send_sems.at[a * (N_CHIPS - 1) + j - 1], recv_sem=recv_sems.at[a * (N_CHIPS - 1) + j - 1],
                    device_id=(px, py, c), device_id_type=MESH)
                cp.wait_send()
                cp.wait_recv()

    outs = pl.pallas_call(
        body, name=name,
        out_shape=tuple(pltpu.HBM(s.shape, s.dtype) for s in (*src_thru, *land_thru)),
        in_specs=[_HBM] * (2 * n) + [_SEM, _SEM] + [s for s, _ in afters], out_specs=[_HBM] * (2 * n),
        input_output_aliases={i: i for i in range(2 * n)},
        compiler_params=pltpu.CompilerParams(has_side_effects=_DATAFLOW),
    )(*src_thru, *land_thru, send_sems, recv_sems, *[a for _, a in afters])
    return outs[:n], outs[n:]


def _col_half(ref, which, lead=()):
    hc = ref.shape[-1] // 2
    return ref.at[(*lead, slice(None), pl.ds(pl.multiple_of(which * hc, 128), hc))]


def _gather_sent(ref, c, dst_chip):
    return _col_half(ref, c)


def _gather_landing(ref, c, src_chip):
    return _col_half(ref, c, lead=(src_chip,))


def _mod_sent(ref, c, dst_chip):
    return ref.at[2 * dst_chip + c]


def _mod_landing(ref, c, src_chip):
    return ref.at[src_chip]


def _reduce_copy(src_ref, land_ref, send_sems, recv_sems, k, receiving):
    x, y, c = _mesh_pos()
    px, py, pc = _flip(x, (k >> 2) & 1), _flip(y, (k >> 1) & 1), _flip(c, k & 1)
    hc = src_ref.shape[2] // 2
    src = src_ref.at[2 * px + py, :, pl.ds(pl.multiple_of(pc * hc, 128), hc)]
    slot = (4 * px + 2 * py + pc) if receiving else (4 * x + 2 * y + c)
    return pltpu.make_async_remote_copy(
        src_ref=src, dst_ref=land_ref.at[slot], send_sem=send_sems.at[k - 1], recv_sem=recv_sems.at[k - 1],
        device_id=(px, py, pc), device_id_type=MESH)


def _reduce_start(grad4, name):
    k4, r, cols = grad4.shape

    def body(src_ref, land_ref, send_sems, recv_sems, src_thru, land_thru, token):
        for k in range(1, N_DEV):
            _reduce_copy(src_ref, land_ref, send_sems, recv_sems, k, receiving=False).start()
        token[...] = jnp.zeros_like(token)

    sems = pltpu.SemaphoreType.DMA((N_DEV - 1,))
    land = pltpu.with_memory_space_constraint(lax.empty((N_DEV, r, cols // 2), grad4.dtype), pltpu.HBM)
    return pl.pallas_call(
        body, name=name,
        out_shape=(sems, sems, pltpu.HBM(grad4.shape, grad4.dtype), pltpu.HBM(land.shape, land.dtype),
                   jax.ShapeDtypeStruct((8, 128), F32)),
        in_specs=[_HBM, _HBM], out_specs=[_SEM, _SEM, _HBM, _HBM, pl.BlockSpec(memory_space=pltpu.VMEM)],
        input_output_aliases={0: 2, 1: 3},
        compiler_params=pltpu.CompilerParams(has_side_effects=_DATAFLOW),
    )(*_pin(grad4), land)


def _reduce_wait(send_sems, recv_sems, src_thru, land_thru, after, name):
    def body(src_ref, land_ref, send_sems, recv_sems, after_ref, src_out, land_out):
        for k in range(1, N_DEV):
            cp = _reduce_copy(src_ref, land_ref, send_sems, recv_sems, k, receiving=True)
            cp.wait_send()
            cp.wait_recv()

    return pl.pallas_call(
        body, name=name,
        out_shape=(pltpu.HBM(src_thru.shape, src_thru.dtype), pltpu.HBM(land_thru.shape, land_thru.dtype)),
        in_specs=[_HBM, _HBM, _SEM, _SEM, _order_operand(after)[0]], out_specs=[_HBM, _HBM],
        input_output_aliases={0: 0, 1: 1},
        compiler_params=pltpu.CompilerParams(has_side_effects=_DATAFLOW),
    )(src_thru, land_thru, send_sems, recv_sems, _order_operand(after)[1])


def _peer_copy(src_ref, land_ref, send_sems, recv_sems, idx, k, receiving):
    x, y, c = _mesh_pos()
    px, py, pc = _flip(x, (k >> 2) & 1), _flip(y, (k >> 1) & 1), _flip(c, k & 1)
    if land_ref.shape[0] == N_DEV:
        slot = (4 * px + 2 * py + pc) if receiving else (4 * x + 2 * y + c)
    else:
        slot = pc if receiving else c
    return pltpu.make_async_remote_copy(
        src_ref=src_ref, dst_ref=land_ref.at[slot], send_sem=send_sems.at[idx], recv_sem=recv_sems.at[idx],
        device_id=(px, py, pc), device_id_type=MESH)


def _exchange_start(arrays, peers, name):
    n = len(arrays)

    def body(*refs):
        src_refs, land_refs = refs[:n], refs[n:2 * n]
        send_sems, recv_sems = refs[2 * n], refs[2 * n + 1]
        token = refs[-1]
        for j, k in enumerate(peers):
            for a in range(n):
                _peer_copy(src_refs[a], land_refs[a], send_sems, recv_sems, a * len(peers) + j, k, receiving=False).start()
        token[...] = jnp.zeros_like(token)

    sems = pltpu.SemaphoreType.DMA((n * len(peers),))
    n_slots = N_DEV if len(peers) > 1 else 2
    lands = [pltpu.with_memory_space_constraint(lax.empty((n_slots,) + a.shape, a.dtype), pltpu.HBM) for a in arrays]
    outs = pl.pallas_call(
        body, name=name,
        out_shape=(sems, sems, *[pltpu.HBM(a.shape, a.dtype) for a in arrays], *[pltpu.HBM(l.shape, l.dtype) for l in lands],
                   jax.ShapeDtypeStruct((8, 128), F32)),
        in_specs=[_HBM] * (2 * n), out_specs=[_SEM, _SEM] + [_HBM] * (2 * n) + [pl.BlockSpec(memory_space=pltpu.VMEM)],
        input_output_aliases={i: 2 + i for i in range(2 * n)},
        compiler_params=pltpu.CompilerParams(has_side_effects=_DATAFLOW),
    )(*_pin(*arrays), *lands)
    return outs[0], outs[1], outs[2:2 + n], outs[2 + n:2 + 2 * n], outs[-1]


def _exchange_wait(send_sems, recv_sems, src_thru, land_thru, peers, after, name):
    n = len(src_thru)

    def body(*refs):
        src_refs, land_refs = refs[:n], refs[n:2 * n]
        send_sems, recv_sems = refs[2 * n], refs[2 * n + 1]
        for j, k in enumerate(peers):
            for a in range(n):
                cp = _peer_copy(src_refs[a], land_refs[a], send_sems, recv_sems, a * len(peers) + j, k, receiving=True)
                cp.wait_send()
                cp.wait_recv()

    outs = pl.pallas_call(
        body, name=name,
        out_shape=tuple(pltpu.HBM(s.shape, s.dtype) for s in (*src_thru, *land_thru)),
        in_specs=[_HBM] * (2 * n) + [_SEM, _SEM, _order_operand(after)[0]], out_specs=[_HBM] * (2 * n),
        input_output_aliases={i: i for i in range(2 * n)},
        compiler_params=pltpu.CompilerParams(has_side_effects=_DATAFLOW),
    )(*src_thru, *land_thru, send_sems, recv_sems, _order_operand(after)[1])
    return outs[:n], outs[n:]


ALL_PEERS = tuple(range(1, N_DEV))
SIBLING = (1,)


def _sum_eight(recv, grad4, pos):
    n, r, hc = recv.shape
    steps = 2
    tc = hc // steps

    def body(pos_ref, r_ref, g_ref, o_ref):
        me = pos_ref[2]
        o_ref[...] = jnp.zeros_like(o_ref)
        for s in range(n):
            @pl.when(me == s)
            def _():
                o_ref[...] += g_ref[0].astype(F32)

            @pl.when(me != s)
            def _():
                o_ref[...] += r_ref[s].astype(F32)

    grid_spec = pltpu.PrefetchScalarGridSpec(
        num_scalar_prefetch=1, grid=(steps,),
        in_specs=[pl.BlockSpec((n, r, tc), lambda i, pos: (0, 0, i)),
                  pl.BlockSpec((1, r, tc), lambda i, pos: (pos[1], 0, pos[0] * steps + i))],
        out_specs=pl.BlockSpec((r, tc), lambda i, pos: (0, i)))
    return pl.pallas_call(body, name="sum_eight", grid_spec=grid_spec, out_shape=_out((r, hc), F32),
                          compiler_params=_cp(("parallel",), 32))(pos, *_pin(recv, grad4))


def _forward_copy(land_ref, send_sems, recv_sems, idx, j, receiving):
    x, y, c = _mesh_pos()
    px, py = _peer_chip(x, y, j)
    mine = _col_half(land_ref, c, lead=(2 * px + py,))
    theirs = _col_half(land_ref, 1 - c, lead=(2 * px + py,))
    return pltpu.make_async_remote_copy(
        src_ref=mine, dst_ref=theirs if receiving else mine, send_sem=send_sems.at[idx], recv_sem=recv_sems.at[idx],
        device_id=(x, y, 1 - c), device_id_type=MESH)


def _forward_start(lands, name):
    n = len(lands)

    def body(*refs):
        land_refs, send_sems, recv_sems, token = refs[:n], refs[n], refs[n + 1], refs[-1]
        for j in range(1, N_CHIPS):
            for a in range(n):
                _forward_copy(land_refs[a], send_sems, recv_sems, a * (N_CHIPS - 1) + j - 1, j, receiving=False).start()
        token[...] = jnp.zeros_like(token)

    sems = pltpu.SemaphoreType.DMA((n * (N_CHIPS - 1),))
    outs = pl.pallas_call(
        body, name=name,
        out_shape=(sems, sems, *[pltpu.HBM(l.shape, l.dtype) for l in lands], jax.ShapeDtypeStruct((8, 128), F32)),
        in_specs=[_HBM] * n, out_specs=[_SEM, _SEM] + [_HBM] * n + [pl.BlockSpec(memory_space=pltpu.VMEM)],
        input_output_aliases={i: 2 + i for i in range(n)},
        compiler_params=pltpu.CompilerParams(has_side_effects=_DATAFLOW),
    )(*lands)
    return outs[0], outs[1], outs[2:2 + n], outs[-1]


def _forward_wait(send_sems, recv_sems, lands_thru, after, name):
    n = len(lands_thru)

    def body(*refs):
        land_refs, send_sems, recv_sems = refs[:n], refs[n], refs[n + 1]
        for j in range(1, N_CHIPS):
            for a in range(n):
                cp = _forward_copy(land_refs[a], send_sems, recv_sems, a * (N_CHIPS - 1) + j - 1, j, receiving=True)
                cp.wait_send()
                cp.wait_recv()

    return pl.pallas_call(
        body, name=name,
        out_shape=tuple(pltpu.HBM(l.shape, l.dtype) for l in lands_thru),
        in_specs=[_HBM] * n + [_SEM, _SEM, _order_operand(after)[0]], out_specs=[_HBM] * n,
        input_output_aliases={i: i for i in range(n)},
        compiler_params=pltpu.CompilerParams(has_side_effects=_DATAFLOW),
    )(*lands_thru, send_sems, recv_sems, _order_operand(after)[1])


def _gather_finish(lands, shards):
    n = len(lands)
    any_spec = _HBM

    def body(*refs):
        shard_refs, out_refs = refs[n:2 * n], refs[2 * n:3 * n]
        send_sems, recv_sems, local_sems = refs[3 * n:]
        x, y, c = _mesh_pos()
        chip = 2 * x + y
        local, sends = [], []
        for a in range(n):
            cp = pltpu.make_async_copy(shard_refs[a], out_refs[a].at[chip], local_sems.at[a])
            cp.start()
            local.append(cp)
        for j in range(1, N_CHIPS):
            px, py = _peer_chip(x, y, j)
            for a in range(n):
                landed = _col_half(out_refs[a], c, lead=(2 * px + py,))
                cp = pltpu.make_async_remote_copy(
                    src_ref=landed, dst_ref=landed, send_sem=send_sems.at[a, j], recv_sem=recv_sems.at[a, j],
                    device_id=(x, y, 1 - c), device_id_type=MESH)
                cp.start()
                sends.append(cp)
        for j in range(1, N_CHIPS):
            px, py = _peer_chip(x, y, j)
            for a in range(n):
                other = _col_half(out_refs[a], 1 - c, lead=(2 * px + py,))
                pltpu.make_async_remote_copy(
                    src_ref=other, dst_ref=other, send_sem=send_sems.at[a, j], recv_sem=recv_sems.at[a, j],
                    device_id=(x, y, 1 - c), device_id_type=MESH).wait_recv()
        for cp in sends:
            cp.wait_send()
        for cp in local:
            cp.wait()

    return pl.pallas_call(
        body, name="gather_finish",
        out_shape=[_out(l.shape, l.dtype) for l in lands],
        in_specs=[any_spec] * (2 * n), out_specs=[any_spec] * n,
        input_output_aliases={i: i for i in range(n)},
        scratch_shapes=[pltpu.SemaphoreType.DMA((n, N_CHIPS))] * 2 + [pltpu.SemaphoreType.DMA((n,))],
        compiler_params=_cp(vmem_mb=16),
    )(*lands, *shards)


def _adam_math(w, g, m, v):
    m2 = ADAM_B1 * m + (1.0 - ADAM_B1) * g
    v2 = ADAM_B2 * v + (1.0 - ADAM_B2) * (g * g)
    m_hat = m2 / (1.0 - ADAM_B1 ** ADAM_STEP)
    v_hat = v2 / (1.0 - ADAM_B2 ** ADAM_STEP)
    delta = -ADAM_LR * (m_hat / (jnp.sqrt(v_hat) + ADAM_EPS) + ADAM_WD * w)
    return delta, m2, v2


def _adam_big(g_own, g_pair, w, m, v, pos):
    r, c = w.shape
    per_half = 2
    tc = c // (2 * per_half)

    def body(pos_ref, go_ref, gp_ref, w_ref, m_ref, v_ref, g_ref, d_ref, m2_ref, v2_ref):
        half = pl.program_id(0) // per_half
        g = jnp.where(half == pos_ref[0], go_ref[...], gp_ref[0])
        d, m2, v2 = _adam_math(w_ref[...], g, m_ref[...], v_ref[...])
        g_ref[...] = g
        d_ref[...] = d
        m2_ref[...] = m2
        v2_ref[...] = v2

    spec = pl.BlockSpec((r, tc), lambda i, pos: (0, i))
    grid_spec = pltpu.PrefetchScalarGridSpec(
        num_scalar_prefetch=1, grid=(2 * per_half,),
        in_specs=[pl.BlockSpec((r, tc), lambda i, pos: (0, i % per_half)),
                  pl.BlockSpec((1, r, tc), lambda i, pos: (1 - pos[0], 0, i % per_half)), spec, spec, spec],
        out_specs=[spec] * 4)
    sh = _out((r, c), F32)
    return pl.pallas_call(body, name="adam_big", grid_spec=grid_spec, out_shape=[sh] * 4,
                          compiler_params=_cp(("parallel",), 32))(pos, *_pin(g_own, g_pair, w, m, v))


def _adam_rows(g_own, g_pair, w3, m3, v3, pos):
    r, _, c = w3.shape
    tr = 128

    def body(pos_ref, go_ref, gp_ref, w_ref, m_ref, v_ref, g_ref, d_ref, m2_ref, v2_ref):
        core = pos_ref[0]
        g = jnp.concatenate([jnp.where(core == 0, go_ref[...], gp_ref[0]), jnp.where(core == 1, go_ref[...], gp_ref[0])], 1)
        d, m2, v2 = _adam_math(w_ref[:, 0, :], g, m_ref[:, 0, :], v_ref[:, 0, :])
        g_ref[:, 0, :] = g
        d_ref[:, 0, :] = d
        m2_ref[:, 0, :] = m2
        v2_ref[:, 0, :] = v2

    spec = pl.BlockSpec((tr, 1, c), lambda i, pos: (i, 0, 0))
    grid_spec = pltpu.PrefetchScalarGridSpec(
        num_scalar_prefetch=1, grid=(pl.cdiv(r, tr),),
        in_specs=[pl.BlockSpec((tr, c // 2), lambda i, pos: (i, 0)),
                  pl.BlockSpec((1, tr, c // 2), lambda i, pos: (1 - pos[0], i, 0)), spec, spec, spec],
        out_specs=[spec] * 4)
    sh = _out(w3.shape, F32)
    return pl.pallas_call(body, name="adam_rows", grid_spec=grid_spec, out_shape=[sh] * 4,
                          compiler_params=_cp(("parallel",), 32))(pos, *_pin(g_own, g_pair, w3, m3, v3))


def _adam_ada(c_act_t, dmod_cols, w, m, v):
    r, c = w.shape
    tc = 512

    def body(ct_ref, dm_ref, w_ref, m_ref, v_ref, g_ref, d_ref, m2_ref, v2_ref):
        g = _nn(ct_ref[...], dm_ref[...].astype(BF))
        d, m2, v2 = _adam_math(w_ref[...], g, m_ref[...], v_ref[...])
        g_ref[...] = g
        d_ref[...] = d
        m2_ref[...] = m2
        v2_ref[...] = v2

    spec = pl.BlockSpec((r, tc), lambda i: (0, i))
    sh = _out((r, c), F32)
    return pl.pallas_call(
        body, name="adam_ada", grid=(c // tc,),
        in_specs=[pl.BlockSpec(c_act_t.shape, lambda i: (0, 0)), pl.BlockSpec((dmod_cols.shape[0], tc), lambda i: (0, i)),
                  spec, spec, spec],
        out_specs=[spec] * 4, out_shape=[sh] * 4, compiler_params=_cp(("parallel",), 48))(*_pin(c_act_t, dmod_cols, w, m, v))


def _ada_mod(c_all, w_shard, b_shard, token):
    nb, d = c_all.shape
    cols = w_shard.shape[1]
    tc = 512

    def body(c_ref, w_ref, b_ref, tok_ref, mod_ref, act_ref):
        cv = c_ref[...]
        act = cv * _sigmoid(cv)
        act_ref[...] = act
        mod_ref[...] = _nn(act.astype(BF), w_ref[...].astype(BF)) + b_ref[...]

    return pl.pallas_call(
        body, name="ada_mod", grid=(cols // tc,),
        in_specs=[pl.BlockSpec((nb, d), lambda i: (0, 0)), pl.BlockSpec((d, tc), lambda i: (0, i)),
                  pl.BlockSpec((1, tc), lambda i: (0, i)), _token_spec()],
        out_specs=[pl.BlockSpec((nb, tc), lambda i: (0, i)), pl.BlockSpec((nb, d), lambda i: (0, 0))],
        out_shape=[_out((nb, cols), F32), _out((nb, d), F32)],
        compiler_params=_cp(("arbitrary",), 32))(*_pin(c_all, w_shard, b_shard, token))


SUB_ROWS = 256
ROW_TILE = 512


def _sub_rows(tm):
    return [slice(s, s + SUB_ROWS) for s in range(0, tm, SUB_ROWS)] if tm > SUB_ROWS else [slice(0, tm)]


_RESIDENT = pl.BlockSpec(memory_space=pltpu.VMEM)


def _token_spec():
    return pl.BlockSpec((8, 128), lambda *_: (0, 0))


def _in_proj(x, mod3, g_mix, w_in_t, seq, token):
    t, d = x.shape
    tm = min(ROW_TILE, seq)
    tps = seq // tm

    def body(x_ref, mod_ref, g_ref, w_ref, tok_ref, proj_ref, u1_ref):
        for rows in _sub_rows(tm):
            xv = x_ref[rows, :]
            r = lax.rsqrt(jnp.mean(xv * xv, -1, keepdims=True) + EPS)
            u = (xv * r * g_ref[...]) * (1.0 + mod_ref[0, 1:2, :]) + mod_ref[0, 0:1, :]
            ub = u.astype(BF)
            u1_ref[rows, :] = ub
            proj_ref[rows, 0:OFF_DT] = _nt(ub, w_ref[0:OFF_DT, :])
            proj_ref[rows, OFF_DT:IN_PAD] = jnp.zeros((rows.stop - rows.start, IN_PAD - OFF_DT), F32)
            proj_ref[rows, OFF_DT:IN_WIDTH] = _nt(ub, w_ref[OFF_DT:IN_WIDTH, :])

    return pl.pallas_call(
        body, name="in_proj", grid=(t // tm,),
        in_specs=[pl.BlockSpec((tm, d), lambda i: (i, 0)), pl.BlockSpec((1, N_MOD, d), lambda i: (i // tps, 0, 0)),
                  pl.BlockSpec((1, d), lambda i: (0, 0)), _RESIDENT, _token_spec()],
        out_specs=[pl.BlockSpec((tm, IN_PAD), lambda i: (i, 0)), pl.BlockSpec((tm, d), lambda i: (i, 0))],
        out_shape=[_out((t, IN_PAD), F32), _out((t, d), BF)],
        compiler_params=_cp(("parallel",), 40))(*_pin(x, mod3, g_mix), w_in_t, *_pin(token))


def _pool_tile(seq):
    return min(512, seq)


def _pool_fwd(proj, w_pool, pool_scale, nb, seq, token):
    ts = _pool_tile(seq)
    nt = seq // ts

    def body(u_ref, halo_ref, wp_ref, ps_ref, tok_ref, yp_ref, p_ref):
        i = pl.program_id(1)
        halo = jnp.where(i == 0, 0.0, halo_ref[...])
        u = u_ref[...]
        ext = jnp.concatenate([halo, u], 0)
        tpos = i * ts + _iota((ts, 1), 0)
        for g, w in enumerate(POOL_WINDOWS):
            gs = slice(g * POOL_GROUP, (g + 1) * POOL_GROUP)
            s = ext[:, gs]
            sh = 1
            while sh < w:
                s = s + pltpu.roll(s, sh, 0)
                sh *= 2
            cnt = jnp.minimum(tpos + 1, w).astype(F32)
            pb = (s[HALO:] / cnt - u[:, gs]).astype(BF)
            p_ref[:, gs] = pb
            yp_ref[:, gs] = (_nn(pb, wp_ref[g].astype(BF)) * ps_ref[:, gs]).astype(BF)

    hb = ts // HALO
    return pl.pallas_call(
        body, name="pool_fwd", grid=(nb, nt),
        in_specs=[pl.BlockSpec((ts, POOL_WIDTH), lambda b, i: (b * nt + i, 0)),
                  pl.BlockSpec((HALO, POOL_WIDTH), lambda b, i: (jnp.maximum((b * nt + i) * hb - 1, 0), 0)),
                  pl.BlockSpec((4, POOL_GROUP, POOL_GROUP), lambda b, i: (0, 0, 0)),
                  pl.BlockSpec((1, POOL_WIDTH), lambda b, i: (0, 0)), _token_spec()],
        out_specs=[pl.BlockSpec((ts, POOL_WIDTH), lambda b, i: (b * nt + i, 0))] * 2,
        out_shape=[_out((nb * seq, POOL_WIDTH), BF)] * 2,
        compiler_params=_cp(("parallel", "parallel"), 32))(*_pin(proj, proj, w_pool, pool_scale, token))


def _conv_pre(uxbc, halo, cw, cb, first):
    halo = jnp.where(first, 0.0, halo)
    ext = jnp.concatenate([halo, uxbc], 0)
    pre = cb + uxbc * cw[3:4]
    for k in (2, 1, 0):
        pre = pre + pltpu.roll(ext, 3 - k, 0)[CONV_HALO:] * cw[k:k + 1]
    return pre


def _chunk_terms(pre, udt, dtb, alog):
    sg = _sigmoid(pre)
    xbc = pre * sg
    dtp = udt[:, :SSD_HEADS] + dtb
    dt = jnp.maximum(dtp, 0.0) + jnp.log(1.0 + jnp.exp(-jnp.abs(dtp)))
    a = -jnp.exp(alog)
    da = dt * a
    tril = (_iota((CHUNK, CHUNK), 0) >= _iota((CHUNK, CHUNK), 1))
    acum = _exact_nn_left(tril.astype(BF), da)
    eye = (_iota((SSD_HEADS, SSD_HEADS), 0) == _iota((SSD_HEADS, SSD_HEADS), 1)).astype(BF)
    acum_t = _exact_nt_left(eye, acum)
    expand = _head_expand_matrix(SSD_HEADS, SSD_INNER)
    acum_e = _exact_nn(acum, expand)
    dt_e = _exact_nn(dt, expand)
    last_e = acum_e[CHUNK - 1:CHUNK]
    return dict(pre=pre, sg=sg, xbc=xbc, dtp=dtp, dt=dt, a=a, acum=acum, acum_t=acum_t, tril=tril,
                dt_e=dt_e, e_a=jnp.exp(acum_e), d_out=jnp.exp(last_e - acum_e), c_dec=jnp.exp(last_e))


def _head_decay(r, h):
    seg = r["acum"][:, h:h + 1] - r["acum_t"][h:h + 1, :]
    return jnp.where(r["tril"], jnp.exp(jnp.minimum(seg, 0.0)), 0.0)


SSD_SUB = 4
SSD_ROWS = SSD_SUB * CHUNK


def _ssd_specs(nb, seq, reverse):
    ns = seq // SSD_ROWS
    per = seq // CONV_HALO

    def cidx(c):
        return (ns - 1 - c) if reverse else c

    def row(b, c):
        return b * ns + cidx(c)

    specs = [
        pl.BlockSpec((SSD_ROWS, CONV_CH), lambda b, c: (row(b, c), 1)),
        pl.BlockSpec((CONV_HALO, CONV_CH),
                     lambda b, c: (jnp.maximum(b * per + cidx(c) * (SSD_ROWS // CONV_HALO) - 1, 0), 1)),
        pl.BlockSpec((SSD_ROWS, GROUP_W), lambda b, c: (row(b, c), 1)),
        pl.BlockSpec((SSD_ROWS, GROUP_W), lambda b, c: (row(b, c), 2)),
        pl.BlockSpec((SSD_ROWS, 128), lambda b, c: (row(b, c), OFF_DT // 128)),
    ]
    return specs, row, cidx, ns


def _const_spec(shape):
    return pl.BlockSpec(shape, lambda b, c: (0,) * len(shape))


def _ssd_fwd(proj, conv_w, conv_b, dt_bias, a_log, dskip_e, g_ssd, nb, seq):
    specs, row, cidx, ns = _ssd_specs(nb, seq, reverse=False)

    def body(uxbc_ref, halo_ref, z0_ref, z1_ref, udt_ref, cw_ref, cb_ref, dtb_ref, alog_ref, dsk_ref, gs_ref,
             yssd_ref, yssm_ref, hprev_ref, pre_ref, h_ref, yd_ref):
        c = pl.program_id(1)

        @pl.when(c == 0)
        def _():
            h_ref[...] = jnp.zeros_like(h_ref)

        for sub in range(SSD_SUB):
            rows = slice(sub * CHUNK, (sub + 1) * CHUNK)
            if sub == 0:
                halo, first = halo_ref[...], c == 0
            else:
                halo, first = uxbc_ref[sub * CHUNK - CONV_HALO:sub * CHUNK, :], False
            pre = _conv_pre(uxbc_ref[rows, :], halo, cw_ref[...], cb_ref[...], first)
            pre_ref[rows, :] = pre
            r = _chunk_terms(pre, udt_ref[rows, :], dtb_ref[...], alog_ref[...])
            xbc = r["xbc"]
            xs = xbc[:, :SSD_INNER]
            xdt = xs * r["dt_e"]
            xdt_b = xdt.astype(BF)
            xdo_b = (xdt * r["d_out"]).astype(BF)
            hprev_ref[0, sub] = h_ref[...]
            for g in range(2):
                gs = slice(g * GROUP_W, (g + 1) * GROUP_W)
                bg = xbc[:, SSD_INNER + g * SSD_STATE:SSD_INNER + (g + 1) * SSD_STATE].astype(BF)
                cg = xbc[:, SSD_INNER + (2 + g) * SSD_STATE:SSD_INNER + (3 + g) * SSD_STATE].astype(BF)
                scores = _nt(cg, bg)
                hg = h_ref[g]
                y_off = _nn(cg, hg.astype(BF)) * r["e_a"][:, gs]
                for hh in range(8):
                    h = g * 8 + hh
                    hs = slice(h * SSD_HEAD_DIM, (h + 1) * SSD_HEAD_DIM)
                    m = (scores * _head_decay(r, h)).astype(BF)
                    yd_ref[sub, :, hs] = _nn(m, xdt_b[:, hs])
                h_ref[g] = hg * r["c_dec"][:, gs] + _tn(bg, xdo_b[:, gs])
                y = yd_ref[sub, :, gs] + y_off + dsk_ref[:, gs] * xs[:, gs]
                yssm_ref[rows, gs] = y
                zg = (z0_ref if g == 0 else z1_ref)[rows, :]
                yg = y * (zg * _sigmoid(zg))
                rg = lax.rsqrt(jnp.mean(yg * yg, -1, keepdims=True) + EPS)
                yssd_ref[rows, gs] = (yg * rg * gs_ref[:, gs]).astype(BF)

    t = nb * seq
    return pl.pallas_call(
        body, name="ssd_fwd", grid=(nb, ns),
        in_specs=specs + [_const_spec((4, CONV_CH)), _const_spec((1, CONV_CH)), _const_spec((1, SSD_HEADS)),
                          _const_spec((1, SSD_HEADS)), _const_spec((1, SSD_INNER)), _const_spec((1, SSD_INNER))],
        out_specs=[pl.BlockSpec((SSD_ROWS, SSD_INNER), lambda b, c: (row(b, c), 0)),
                   pl.BlockSpec((SSD_ROWS, SSD_INNER), lambda b, c: (row(b, c), 0)),
                   pl.BlockSpec((1, SSD_SUB, 2, SSD_STATE, GROUP_W), lambda b, c: (b, c, 0, 0, 0)),
                   pl.BlockSpec((SSD_ROWS, CONV_CH), lambda b, c: (row(b, c), 0))],
        out_shape=[_out((t, SSD_INNER), BF), _out((t, SSD_INNER), F32),
                   _out((nb, seq // CHUNK, 2, SSD_STATE, GROUP_W), F32), _out((t, CONV_CH), F32)],
        scratch_shapes=[pltpu.VMEM((2, SSD_STATE, GROUP_W), F32), pltpu.VMEM((SSD_SUB, CHUNK, SSD_INNER), F32)],
        compiler_params=_cp(("arbitrary", "arbitrary"), 56),
    )(*_pin(proj, proj, proj, proj, proj, conv_w, conv_b, dt_bias, a_log, dskip_e, g_ssd))


def _out_proj(y_pool, y_ssd, w_out, x, mod3, g_mlp, seq):
    t, d = x.shape
    tm = 512
    tps = seq // tm if seq >= tm else 1
    tm = min(tm, seq)

    def body(yp_ref, ys_ref, w_ref, x_ref, mod_ref, g_ref, h1_ref, o_ref, u2_ref):
        for rows in _sub_rows(tm):
            o = _nn(jnp.concatenate([yp_ref[rows, :], ys_ref[rows, :]], 1), w_ref[...])
            o_ref[rows, :] = o.astype(BF)
            h1 = x_ref[rows, :] + mod_ref[0, 2:3, :] * o
            h1_ref[rows, :] = h1
            r = lax.rsqrt(jnp.mean(h1 * h1, -1, keepdims=True) + EPS)
            u2_ref[rows, :] = ((h1 * r * g_ref[...]) * (1.0 + mod_ref[0, 4:5, :]) + mod_ref[0, 3:4, :]).astype(BF)

    row = lambda i: (i, 0)
    return pl.pallas_call(
        body, name="out_proj", grid=(t // tm,),
        in_specs=[pl.BlockSpec((tm, POOL_WIDTH), row), pl.BlockSpec((tm, SSD_INNER), row),
                  _RESIDENT, pl.BlockSpec((tm, d), row),
                  pl.BlockSpec((1, N_MOD, d), lambda i: (i // tps, 0, 0)), pl.BlockSpec((1, d), lambda i: (0, 0))],
        out_specs=[pl.BlockSpec((tm, d), row)] * 3,
        out_shape=[_out((t, d), F32), _out((t, d), BF), _out((t, d), BF)],
        compiler_params=_cp(("parallel",), 48))(*_pin(y_pool, y_ssd), w_out, *_pin(x, mod3, g_mlp))


def _mlp_up(u2, w_up4):
    t, d = u2.shape
    tm = min(1024, t)
    nk, _, cols = w_up4.shape

    def body(u_ref, w_ref, a_ref):
        a_ref[...] = _nn(u_ref[...], w_ref[pl.program_id(1)]).astype(BF)

    return pl.pallas_call(
        body, name="mlp_up", grid=(t // tm, nk),
        in_specs=[pl.BlockSpec((tm, d), lambda i, k: (i, 0)), _RESIDENT],
        out_specs=pl.BlockSpec((tm, cols), lambda i, k: (i, k)),
        out_shape=_out((t, nk * cols), BF),
        compiler_params=_cp(("parallel", "parallel"), 32))(*_pin(u2), w_up4)


def _mlp_down_loss(a_up, w_down, h1, mod3, g_final, target, seq):
    t, d = h1.shape
    nb = t // seq
    tm = min(ROW_TILE, seq)
    tps = seq // tm

    def body(a_ref, w_ref, h1_ref, mod_ref, g_ref, tg_ref, ddn_ref, dh2_ref, sq_ref, gg_ref, dgf_ref):
        i = pl.program_id(0)

        @pl.when(i == 0)
        def _():
            sq_ref[...] = jnp.zeros_like(sq_ref)
            gg_ref[...] = jnp.zeros_like(gg_ref)

        @pl.when(i % tps == 0)
        def _():
            dgf_ref[...] = jnp.zeros_like(dgf_ref)

        gate = mod_ref[0, 5:6, :]
        sq = gg = dgf = 0.0
        for rows in _sub_rows(tm):
            f = jnp.square(jnp.maximum(a_ref[rows, :], 0))
            dn = _nn(f, w_ref[...])
            h2 = h1_ref[rows, :] + gate * dn
            r = lax.rsqrt(jnp.mean(h2 * h2, -1, keepdims=True) + EPS)
            hh = h2 * r
            err = hh * g_ref[...] - tg_ref[rows, :]
            dy = err * (1.0 / d)
            dhat = dy * g_ref[...]
            dh2 = r * (dhat - hh * jnp.mean(dhat * hh, -1, keepdims=True))
            dh2_ref[rows, :] = dh2
            ddn_ref[rows, :] = (dh2 * gate).astype(BF)
            sq = sq + jnp.sum(err * err, 0, keepdims=True)
            gg = gg + jnp.sum(dy * hh, 0, keepdims=True)
            dgf = dgf + jnp.sum(dh2 * dn, 0, keepdims=True)
        sq_ref[...] += sq
        gg_ref[...] += gg
        dgf_ref[0] += dgf

    row = lambda i: (i, 0)
    vec = pl.BlockSpec((1, d), lambda i: (0, 0))
    return pl.pallas_call(
        body, name="mlp_down_loss", grid=(t // tm,),
        in_specs=[pl.BlockSpec((tm, D_FF), row), _RESIDENT, pl.BlockSpec((tm, d), row),
                  pl.BlockSpec((1, N_MOD, d), lambda i: (i // tps, 0, 0)), vec, pl.BlockSpec((tm, d), row)],
        out_specs=[pl.BlockSpec((tm, d), row), pl.BlockSpec((tm, d), row), vec, vec,
                   pl.BlockSpec((1, 1, d), lambda i: (i // tps, 0, 0))],
        out_shape=[_out((t, d), BF), _out((t, d), F32), _out((1, d), F32),
                   _out((1, d), F32), _out((nb, 1, d), F32)],
        compiler_params=_cp(("arbitrary",), 44))(*_pin(a_up), w_down, *_pin(h1, mod3, g_final, target))


def _tn_matmul(a, b, tk, tn, name, square_relu=False, out3=False):
    t, kdim = a.shape
    ndim = b.shape[1]

    def body(a_ref, b_ref, o_ref):
        av = a_ref[...]
        if square_relu:
            av = jnp.square(jnp.maximum(av, 0))
        res = _tn(av, b_ref[...]).astype(BF)
        if out3:
            o_ref[0] = res
        else:
            o_ref[...] = res

    if out3:
        out_spec = pl.BlockSpec((1, tk, tn), lambda j, i: (j, i, 0))
        out_shape = _out((ndim // tn, kdim, tn), BF)
    else:
        out_spec = pl.BlockSpec((tk, tn), lambda j, i: (i, j))
        out_shape = _out((kdim, ndim), BF)
    return pl.pallas_call(
        body, name=name, grid=(ndim // tn, kdim // tk),
        in_specs=[pl.BlockSpec((t, tk), lambda j, i: (0, i)), pl.BlockSpec((t, tn), lambda j, i: (0, j))],
        out_specs=out_spec, out_shape=out_shape,
        compiler_params=_cp(("parallel", "parallel"), 56))(*_pin(a, b))


def _mlp_down_bwd(d_dn, w_down4, a_up, token):
    t, d = d_dn.shape
    tm = min(1024, t)
    nk, rows, _ = w_down4.shape

    def body(g_ref, w_ref, a_ref, tok_ref, o_ref):
        df = _nt(g_ref[...], w_ref[pl.program_id(1)])
        o_ref[...] = (df * (2.0 * jnp.maximum(a_ref[...], 0).astype(F32))).astype(BF)

    return pl.pallas_call(
        body, name="mlp_down_bwd", grid=(t // tm, nk),
        in_specs=[pl.BlockSpec((tm, d), lambda i, k: (i, 0)), _RESIDENT,
                  pl.BlockSpec((tm, rows), lambda i, k: (i, k)), _token_spec()],
        out_specs=pl.BlockSpec((tm, rows), lambda i, k: (i, k)),
        out_shape=_out((t, nk * rows), BF),
        compiler_params=_cp(("parallel", "parallel"), 32))(*_pin(d_dn), w_down4, *_pin(a_up, token))


def _mlp_up_bwd(d_a, w_up4, h1, dh2, o, mod3, g_mlp, seq, token):
    t, d = h1.shape
    nb = t // seq
    tm = min(ROW_TILE, seq)
    tps = seq // tm
    nk = w_up4.shape[0]
    cols = w_up4.shape[2]

    def body(da_ref, w_ref, h1_ref, dh2_ref, o_ref, mod_ref, g_ref, tok_ref, dh1_ref, do_ref, acc_ref, gg_ref):
        i = pl.program_id(0)

        @pl.when(i == 0)
        def _():
            gg_ref[...] = jnp.zeros_like(gg_ref)

        @pl.when(i % tps == 0)
        def _():
            acc_ref[...] = jnp.zeros_like(acc_ref)

        gg = a_shift = a_scale = a_gate = 0.0
        for rows in _sub_rows(tm):
            du = _nt(da_ref[rows, 0:cols], w_ref[0])
            for k in range(1, nk):
                du = du + _nt(da_ref[rows, k * cols:(k + 1) * cols], w_ref[k])
            h1 = h1_ref[rows, :]
            r = lax.rsqrt(jnp.mean(h1 * h1, -1, keepdims=True) + EPS)
            hh = h1 * r
            n2 = hh * g_ref[...]
            dn2 = du * (1.0 + mod_ref[0, 4:5, :])
            dhat = dn2 * g_ref[...]
            dh1 = dh2_ref[rows, :] + r * (dhat - hh * jnp.mean(dhat * hh, -1, keepdims=True))
            dh1_ref[rows, :] = dh1
            do_ref[rows, :] = (dh1 * mod_ref[0, 2:3, :]).astype(BF)
            gg = gg + jnp.sum(dn2 * hh, 0, keepdims=True)
            a_shift = a_shift + jnp.sum(du, 0, keepdims=True)
            a_scale = a_scale + jnp.sum(du * n2, 0, keepdims=True)
            a_gate = a_gate + jnp.sum(dh1 * o_ref[rows, :].astype(F32), 0, keepdims=True)
        gg_ref[...] += gg
        acc_ref[0, 0:1, :] += a_shift
        acc_ref[0, 1:2, :] += a_scale
        acc_ref[0, 2:3, :] += a_gate

    row = lambda i: (i, 0)
    vec = pl.BlockSpec((1, d), lambda i: (0, 0))
    return pl.pallas_call(
        body, name="mlp_up_bwd", grid=(t // tm,),
        in_specs=[pl.BlockSpec((tm, D_FF), row), _RESIDENT, pl.BlockSpec((tm, d), row),
                  pl.BlockSpec((tm, d), row), pl.BlockSpec((tm, d), row),
                  pl.BlockSpec((1, N_MOD, d), lambda i: (i // tps, 0, 0)), vec, _token_spec()],
        out_specs=[pl.BlockSpec((tm, d), row), pl.BlockSpec((tm, d), row),
                   pl.BlockSpec((1, 8, d), lambda i: (i // tps, 0, 0)), vec],
        out_shape=[_out((t, d), F32), _out((t, d), BF),
                   _out((nb, 8, d), F32), _out((1, d), F32)],
        compiler_params=_cp(("arbitrary",), 44))(*_pin(d_a), w_up4, *_pin(h1, dh2, o, mod3, g_mlp, token))


def _out_proj_bwd(d_o, w_out, token):
    t, d = d_o.shape
    tm = min(512, t)

    def body(g_ref, w_ref, tok_ref, dp_ref, ds_ref):
        gv = g_ref[...]
        dp_ref[...] = _nt(gv, w_ref[0:POOL_WIDTH, :])
        ds_ref[...] = _nt(gv, w_ref[POOL_WIDTH:, :])

    row = lambda i: (i, 0)
    return pl.pallas_call(
        body, name="out_proj_bwd", grid=(t // tm,),
        in_specs=[pl.BlockSpec((tm, d), row), _RESIDENT, _token_spec()],
        out_specs=[pl.BlockSpec((tm, POOL_WIDTH), row), pl.BlockSpec((tm, SSD_INNER), row)],
        out_shape=[_out((t, POOL_WIDTH), F32), _out((t, SSD_INNER), F32)],
        compiler_params=_cp(("parallel",), 32))(*_pin(d_o), w_out, *_pin(token))


def _pool_bwd(d_ypool, p, w_pool, pool_scale, nb, seq):
    ts = _pool_tile(seq)
    nt = seq // ts
    hb = ts // HALO
    last_block = nb * seq // HALO - 1

    def body(dy_ref, halo_ref, p_ref, wp_ref, ps_ref, du_ref, gw_ref, gs_ref):
        b = pl.program_id(0)
        i = pl.program_id(1)

        @pl.when((b == 0) & (i == 0))
        def _():
            gw_ref[...] = jnp.zeros_like(gw_ref)
            gs_ref[...] = jnp.zeros_like(gs_ref)

        halo = jnp.where(i == nt - 1, 0.0, halo_ref[...])
        dy = dy_ref[...]
        ext = jnp.concatenate([dy, halo], 0)
        tpos = i * ts + _iota((ts + HALO, 1), 0)
        n_ext = ts + HALO
        for g, w in enumerate(POOL_WINDOWS):
            gs = slice(g * POOL_GROUP, (g + 1) * POOL_GROUP)
            wg = wp_ref[g].astype(BF)
            pg = p_ref[:, gs]
            pw = _nn(pg, wg)
            gs_ref[:, gs] += jnp.sum(dy[:, gs] * pw, 0, keepdims=True)
            dpw = (ext[:, gs] * ps_ref[:, gs]).astype(BF)
            gw_ref[g] += _tn(pg, dpw[:ts])
            dp = _nt(dpw, wg)
            cnt = jnp.minimum(tpos + 1, w).astype(F32)
            s = dp / cnt
            sh = 1
            while sh < w:
                s = s + pltpu.roll(s, n_ext - sh, 0)
                sh *= 2
            du_ref[:, gs] = (s[:ts] - dp[:ts]).astype(BF)

    return pl.pallas_call(
        body, name="pool_bwd", grid=(nb, nt),
        in_specs=[pl.BlockSpec((ts, POOL_WIDTH), lambda b, i: (b * nt + i, 0)),
                  pl.BlockSpec((HALO, POOL_WIDTH), lambda b, i: (jnp.minimum((b * nt + i + 1) * hb, last_block), 0)),
                  pl.BlockSpec((ts, POOL_WIDTH), lambda b, i: (b * nt + i, 0)),
                  pl.BlockSpec((4, POOL_GROUP, POOL_GROUP), lambda b, i: (0, 0, 0)),
                  pl.BlockSpec((1, POOL_WIDTH), lambda b, i: (0, 0))],
        out_specs=[pl.BlockSpec((ts, POOL_WIDTH), lambda b, i: (b * nt + i, 0)),
                   pl.BlockSpec((4, POOL_GROUP, POOL_GROUP), lambda b, i: (0, 0, 0)),
                   pl.BlockSpec((1, POOL_WIDTH), lambda b, i: (0, 0))],
        out_shape=[_out((nb * seq, POOL_WIDTH), BF), _out((4, POOL_GROUP, POOL_GROUP), F32),
                   _out((1, POOL_WIDTH), F32)],
        compiler_params=_cp(("arbitrary", "arbitrary"), 32))(*_pin(d_ypool, d_ypool, p, w_pool, pool_scale))


def _ssd_bwd(proj, pre, d_yssd, yssm, h_prev, dt_bias, a_log, dskip_e, g_ssd, nb, seq):
    specs, row, cidx, ns = _ssd_specs(nb, seq, reverse=True)
    specs = specs[2:]

    def body(z0_ref, z1_ref, udt_ref, pre_ref, dys_ref, yssm_ref, hprev_ref,
             dtb_ref, alog_ref, dsk_ref, gs_ref,
             dz_ref, dpre_ref, dudt_ref, ggs_ref, gdsk_ref, ga_ref, gdtb_ref,
             g_ref, dxdt_ref, dyv_ref):
        b = pl.program_id(0)
        c = pl.program_id(1)

        @pl.when(c == 0)
        def _():
            g_ref[...] = jnp.zeros_like(g_ref)

        @pl.when((b == 0) & (c == 0))
        def _():
            ggs_ref[...] = jnp.zeros_like(ggs_ref)
            gdsk_ref[...] = jnp.zeros_like(gdsk_ref)
            ga_ref[...] = jnp.zeros_like(ga_ref)
            gdtb_ref[...] = jnp.zeros_like(gdtb_ref)

        for sub in reversed(range(SSD_SUB)):
            chunk(sub, z0_ref, z1_ref, udt_ref, pre_ref, dys_ref, yssm_ref, hprev_ref, dtb_ref, alog_ref, dsk_ref, gs_ref,
                  dz_ref, dpre_ref, dudt_ref, ggs_ref, gdsk_ref, ga_ref, gdtb_ref, g_ref, dxdt_ref.at[sub], dyv_ref.at[sub])

    def chunk(sub, z0_ref, z1_ref, udt_ref, pre_ref, dys_ref, yssm_ref, hprev_ref,
              dtb_ref, alog_ref, dsk_ref, gs_ref,
              dz_ref, dpre_ref, dudt_ref, ggs_ref, gdsk_ref, ga_ref, gdtb_ref,
              g_ref, dxdt_ref, dyv_ref):
        rows = slice(sub * CHUNK, (sub + 1) * CHUNK)
        r = _chunk_terms(pre_ref[rows, :], udt_ref[rows, :], dtb_ref[...], alog_ref[...])
        xbc = r["xbc"]
        xs = xbc[:, :SSD_INNER]
        dt_e = r["dt_e"]
        xdt = xs * dt_e
        xdt_b = xdt.astype(BF)
        reduce_m = _head_reduce_matrix(GROUP_W, 8)

        def head_sums(v):
            return _nn(v.astype(BF), reduce_m)

        onehot16 = lambda h: (_iota((1, SSD_HEADS), 1) == h).astype(F32)
        onecol16 = lambda h: (_iota((SSD_HEADS, 1), 0) == h).astype(F32)

        d_acum = jnp.zeros((CHUNK, SSD_HEADS), F32)
        d_acum_t = jnp.zeros((SSD_HEADS, CHUNK), F32)
        d_alast = jnp.zeros((1, SSD_HEADS), F32)
        place8 = lambda g: (_iota((8, SSD_HEADS), 1) == _iota((8, SSD_HEADS), 0) + 8 * g).astype(BF)
        d_b, d_c = [], []
        for g in range(2):
            gs = slice(g * GROUP_W, (g + 1) * GROUP_W)
            zg = (z0_ref if g == 0 else z1_ref)[rows, :]
            sz = _sigmoid(zg)
            silu_z = zg * sz
            ys = yssm_ref[rows, gs]
            yg = ys * silu_z
            rg = lax.rsqrt(jnp.mean(yg * yg, -1, keepdims=True) + EPS)
            yh = yg * rg
            dys = dys_ref[rows, gs]
            ggs_ref[:, gs] += jnp.sum(dys * yh, 0, keepdims=True)
            dyh = dys * gs_ref[:, gs]
            dyg = rg * (dyh - yh * jnp.mean(dyh * yh, -1, keepdims=True))
            dy = dyg * silu_z
            dz_ref[rows, gs] = (dyg * ys * (sz * (1.0 + zg * (1.0 - sz)))).astype(BF)
            gdsk_ref[:, gs] += jnp.sum(dy * xs[:, gs], 0, keepdims=True)
            dyv_ref[:, gs] = dy
            dy_b = dy.astype(BF)

            bg = xbc[:, SSD_INNER + g * SSD_STATE:SSD_INNER + (g + 1) * SSD_STATE].astype(BF)
            cg = xbc[:, SSD_INNER + (2 + g) * SSD_STATE:SSD_INNER + (3 + g) * SSD_STATE].astype(BF)
            scores = _nt(cg, bg)
            hg = hprev_ref[0, sub, g]
            hg_b = hg.astype(BF)
            gg = g_ref[g]
            gg_b = gg.astype(BF)
            e_a = r["e_a"][:, gs]
            d_out = r["d_out"][:, gs]
            c_dec = r["c_dec"][:, gs]
            zc = _nn(cg, hg_b)
            wv = e_a * dy
            wv_b = wv.astype(BF)
            da_g = head_sums(wv * zc)
            dcg = _nt(wv_b, hg_b)
            d_hprev = _tn(cg, wv_b)
            vg = _nn(bg, gg_b)
            dxdt_g = d_out * vg
            dd_out = head_sums(xdt[:, gs] * vg)
            dbg = _nt((xdt[:, gs] * d_out).astype(BF), gg_b)
            dcd = _exact_nn(jnp.sum(gg * hg, 0, keepdims=True), reduce_m)
            d_out8 = jnp.exp(r["acum"][CHUNK - 1:CHUNK, 8 * g:8 * g + 8] - r["acum"][:, 8 * g:8 * g + 8])
            c_dec8 = jnp.exp(r["acum"][CHUNK - 1:CHUNK, 8 * g:8 * g + 8])
            t8 = dd_out * d_out8
            d_alast = d_alast + _exact_nn(jnp.sum(t8, 0, keepdims=True) + dcd * c_dec8, place8(g))
            d_acum = d_acum + _exact_nn(da_g - t8, place8(g))
            dsc = jnp.zeros((CHUNK, CHUNK), F32)
            for hh in range(8):
                h = g * 8 + hh
                hs = slice(h * SSD_HEAD_DIM, (h + 1) * SSD_HEAD_DIM)
                lam = _head_decay(r, h)
                m = scores * lam
                dyh_b = dy_b[:, hh * SSD_HEAD_DIM:(hh + 1) * SSD_HEAD_DIM]
                dm = _nt(dyh_b, xdt_b[:, hs])
                tm_ = dm * m
                d_acum = d_acum + jnp.sum(tm_, 1, keepdims=True) * onehot16(h)
                d_acum_t = d_acum_t + onecol16(h) * jnp.sum(tm_, 0, keepdims=True)
                dsc = dsc + dm * lam
                dxdt_ref[:, hs] = _tn(m.astype(BF), dyh_b) + dxdt_g[:, hh * SSD_HEAD_DIM:(hh + 1) * SSD_HEAD_DIM]
            dsc_b = dsc.astype(BF)
            d_c.append(dcg + _nn(dsc_b, bg))
            d_b.append(dbg + _tn(dsc_b, cg))
            g_ref[g] = d_hprev + c_dec * gg

        eye = (_iota((CHUNK, CHUNK), 0) == _iota((CHUNK, CHUNK), 1)).astype(BF)
        d_acum = d_acum - _exact_nt_left(eye, d_acum_t)
        is_last = (_iota((CHUNK, 1), 0) == CHUNK - 1).astype(F32)
        d_acum = d_acum + is_last * d_alast
        triu = (_iota((CHUNK, CHUNK), 0) <= _iota((CHUNK, CHUNK), 1)).astype(BF)
        d_da = _exact_nn_left(triu, d_acum)
        dt = r["dt"]
        ga_ref[...] += jnp.sum(d_da * dt, 0, keepdims=True)
        dxdt = dxdt_ref[...]
        reduce16 = _head_reduce_matrix(SSD_INNER, SSD_HEADS)
        d_dt = d_da * r["a"] + _nn((dxdt * xs).astype(BF), reduce16)
        d_udt = d_dt * _sigmoid(r["dtp"])
        gdtb_ref[...] += jnp.sum(d_udt, 0, keepdims=True)
        dudt_ref[rows, :] = jnp.zeros((CHUNK, dudt_ref.shape[1]), BF)
        dudt_ref[rows, 0:SSD_HEADS] = d_udt.astype(BF)
        pre, sg = r["pre"], r["sg"]
        dsilu = sg * (1.0 + pre * (1.0 - sg))
        dpre_ref[rows, 0:SSD_INNER] = (dsk_ref[...] * dyv_ref[...] + dxdt * dt_e) * dsilu[:, 0:SSD_INNER]
        for g in range(2):
            bs = slice(SSD_INNER + g * SSD_STATE, SSD_INNER + (g + 1) * SSD_STATE)
            cs = slice(SSD_INNER + (2 + g) * SSD_STATE, SSD_INNER + (3 + g) * SSD_STATE)
            dpre_ref[rows, bs] = d_b[g] * dsilu[:, bs]
            dpre_ref[rows, cs] = d_c[g] * dsilu[:, cs]

    t = nb * seq
    vec = _const_spec((1, SSD_INNER))
    small = _const_spec((1, SSD_HEADS))
    return pl.pallas_call(
        body, name="ssd_bwd", grid=(nb, ns),
        in_specs=specs + [pl.BlockSpec((SSD_ROWS, CONV_CH), lambda b, c: (row(b, c), 0)),
                          pl.BlockSpec((SSD_ROWS, SSD_INNER), lambda b, c: (row(b, c), 0)),
                          pl.BlockSpec((SSD_ROWS, SSD_INNER), lambda b, c: (row(b, c), 0)),
                          pl.BlockSpec((1, SSD_SUB, 2, SSD_STATE, GROUP_W), lambda b, c: (b, cidx(c), 0, 0, 0)),
                          small, small, vec, vec],
        out_specs=[pl.BlockSpec((SSD_ROWS, SSD_INNER), lambda b, c: (row(b, c), 0)),
                   pl.BlockSpec((SSD_ROWS, CONV_CH), lambda b, c: (row(b, c), 0)),
                   pl.BlockSpec((SSD_ROWS, 128), lambda b, c: (row(b, c), 0)),
                   vec, vec, small, small],
        out_shape=[_out((t, SSD_INNER), BF), _out((t, CONV_CH), F32),
                   _out((t, 128), BF), _out((1, SSD_INNER), F32),
                   _out((1, SSD_INNER), F32), _out((1, SSD_HEADS), F32),
                   _out((1, SSD_HEADS), F32)],
        scratch_shapes=[pltpu.VMEM((2, SSD_STATE, GROUP_W), F32), pltpu.VMEM((SSD_SUB, CHUNK, SSD_INNER), F32),
                        pltpu.VMEM((SSD_SUB, CHUNK, SSD_INNER), F32)],
        compiler_params=_cp(("arbitrary", "arbitrary"), 56),
    )(*_pin(proj, proj, proj, pre, d_yssd, yssm, h_prev, dt_bias, a_log, dskip_e, g_ssd))


def _grad_w_out(y_pool, y_ssd, d_o):
    t, d = d_o.shape
    tk = POOL_WIDTH
    n_s = SSD_INNER // tk

    def body(p_ref, s_ref, g_ref, o_ref):
        i = pl.program_id(0)

        @pl.when(i == 0)
        def _():
            o_ref[...] = _tn(p_ref[...], g_ref[...]).astype(BF)

        @pl.when(i > 0)
        def _():
            o_ref[...] = _tn(s_ref[...], g_ref[...]).astype(BF)

    return pl.pallas_call(
        body, name="grad_w_out", grid=(1 + n_s,),
        in_specs=[pl.BlockSpec((t, tk), lambda i: (0, 0)), pl.BlockSpec((t, tk), lambda i: (0, jnp.maximum(i - 1, 0))),
                  pl.BlockSpec((t, d), lambda i: (0, 0))],
        out_specs=pl.BlockSpec((tk, d), lambda i: (i, 0)),
        out_shape=_out((POOL_WIDTH + SSD_INNER, d), BF),
        compiler_params=_cp(("parallel",), 56))(*_pin(y_pool, y_ssd, d_o))


def _grad_w_in_t(d_upool, d_z, d_uxbc, d_udt, u1):
    t, d = u1.shape
    tk = 512
    n_z, n_x = SSD_INNER // tk, CONV_CH // tk

    def body(p_ref, z_ref, x_ref, dt_ref, u_ref, o_ref):
        i = pl.program_id(0)

        @pl.when(i == 0)
        def _():
            o_ref[...] = _tn(p_ref[...], u_ref[...]).astype(BF)

        @pl.when((i >= 1) & (i < 1 + n_z))
        def _():
            o_ref[...] = _tn(z_ref[...], u_ref[...]).astype(BF)

        @pl.when((i >= 1 + n_z) & (i < 1 + n_z + n_x))
        def _():
            o_ref[...] = _tn(x_ref[...], u_ref[...]).astype(BF)

        @pl.when(i == 1 + n_z + n_x)
        def _():
            o_ref[0:128, :] = _tn(dt_ref[...], u_ref[...]).astype(BF)

    return pl.pallas_call(
        body, name="grad_w_in", grid=(2 + n_z + n_x,),
        in_specs=[pl.BlockSpec((t, tk), lambda i: (0, 0)),
                  pl.BlockSpec((t, tk), lambda i: (0, jnp.clip(i - 1, 0, n_z - 1))),
                  pl.BlockSpec((t, tk), lambda i: (0, jnp.clip(i - 1 - n_z, 0, n_x - 1))),
                  pl.BlockSpec((t, 128), lambda i: (0, 0)), pl.BlockSpec((t, d), lambda i: (0, 0))],
        out_specs=pl.BlockSpec((tk, d), lambda i: (i, 0)),
        out_shape=_out((IN_PAD, d), BF),
        compiler_params=_cp(("parallel",), 56))(*_pin(d_upool, d_z, d_uxbc, d_udt, u1))


def _conv_bwd(d_pre, proj, conv_w, nb, seq):
    ts = min(256, seq)
    nt = seq // ts
    hb = ts // CONV_HALO
    last_block = nb * seq // CONV_HALO - 1
    n_ext = CHUNK + CONV_HALO

    def body(dp_ref, dnext_ref, u_ref, cw_ref, du_ref, gw_ref, gb_ref):
        b = pl.program_id(0)
        i = pl.program_id(1)

        @pl.when((b == 0) & (i == 0))
        def _():
            gw_ref[...] = jnp.zeros_like(gw_ref)
            gb_ref[...] = jnp.zeros_like(gb_ref)

        for c0 in range(0, CONV_CH, 128):
            cs = slice(c0, c0 + 128)
            cw = cw_ref[:, cs]
            gw = [0.0] * 4
            gb = 0.0
            for r0 in range(0, ts, CHUNK):
                dp = dp_ref[r0:r0 + CHUNK, cs]
                u = u_ref[r0:r0 + CHUNK, cs]
                if r0 + CHUNK < ts:
                    below = dp_ref[r0 + CHUNK:r0 + CHUNK + CONV_HALO, cs]
                else:
                    below = jnp.where(i == nt - 1, 0.0, dnext_ref[:, cs])
                ext_d = jnp.concatenate([dp, below], 0)
                du = dp * cw[3:4]
                gw[3] = gw[3] + jnp.sum(dp * u, 0, keepdims=True)
                for k in (2, 1, 0):
                    shifted = pltpu.roll(ext_d, n_ext - (3 - k), 0)[:CHUNK]
                    du = du + shifted * cw[k:k + 1]
                    gw[k] = gw[k] + jnp.sum(shifted * u, 0, keepdims=True)
                gb = gb + jnp.sum(dp, 0, keepdims=True)
                du_ref[r0:r0 + CHUNK, cs] = du.astype(BF)
            for k in range(4):
                gw_ref[k:k + 1, cs] += gw[k]
            gb_ref[:, cs] += gb

    return pl.pallas_call(
        body, name="conv_bwd", grid=(nb, nt),
        in_specs=[pl.BlockSpec((ts, CONV_CH), lambda b, i: (b * nt + i, 0)),
                  pl.BlockSpec((CONV_HALO, CONV_CH), lambda b, i: (jnp.minimum((b * nt + i + 1) * hb, last_block), 0)),
                  pl.BlockSpec((ts, CONV_CH), lambda b, i: (b * nt + i, 1)),
                  pl.BlockSpec((4, CONV_CH), lambda b, i: (0, 0))],
        out_specs=[pl.BlockSpec((ts, CONV_CH), lambda b, i: (b * nt + i, 0)),
                   pl.BlockSpec((8, CONV_CH), lambda b, i: (0, 0)), pl.BlockSpec((1, CONV_CH), lambda b, i: (0, 0))],
        out_shape=[_out((nb * seq, CONV_CH), BF), _out((8, CONV_CH), F32),
                   _out((1, CONV_CH), F32)],
        compiler_params=_cp(("arbitrary", "arbitrary"), 48))(*_pin(d_pre, d_pre, proj, conv_w))


def _in_proj_bwd(d_parts, w_in_t, x, dh1, mod3, g_mix, seq, token):
    t, d = x.shape
    nb = t // seq
    tm = min(ROW_TILE, seq)
    tps = seq // tm

    widths = [p.shape[1] for p in d_parts]

    def body(d0_ref, d1_ref, d2_ref, d3_ref, w_ref, x_ref, dh1_ref, mod_ref, g_ref, tok_ref, gx_ref, acc_ref, gg_ref):
        i = pl.program_id(0)

        @pl.when(i == 0)
        def _():
            gg_ref[...] = jnp.zeros_like(gg_ref)

        @pl.when(i % tps == 0)
        def _():
            acc_ref[...] = jnp.zeros_like(acc_ref)

        gg = a_shift = a_scale = 0.0
        for rows in _sub_rows(tm):
            d_cat = jnp.concatenate([p_ref[rows, :] for p_ref in (d0_ref, d1_ref, d2_ref)], 1)
            du = _nn(d_cat, w_ref[0:OFF_DT, :]) + _nn(d3_ref[rows, 0:IN_WIDTH - OFF_DT], w_ref[OFF_DT:IN_WIDTH, :])
            xv = x_ref[rows, :]
            r = lax.rsqrt(jnp.mean(xv * xv, -1, keepdims=True) + EPS)
            hh = xv * r
            n1 = hh * g_ref[...]
            dn1 = du * (1.0 + mod_ref[0, 1:2, :])
            dhat = dn1 * g_ref[...]
            gx_ref[rows, :] = dh1_ref[rows, :] + r * (dhat - hh * jnp.mean(dhat * hh, -1, keepdims=True))
            gg = gg + jnp.sum(dn1 * hh, 0, keepdims=True)
            a_shift = a_shift + jnp.sum(du, 0, keepdims=True)
            a_scale = a_scale + jnp.sum(du * n1, 0, keepdims=True)
        gg_ref[...] += gg
        acc_ref[0, 0:1, :] += a_shift
        acc_ref[0, 1:2, :] += a_scale

    row = lambda i: (i, 0)
    vec = pl.BlockSpec((1, d), lambda i: (0, 0))
    return pl.pallas_call(
        body, name="in_proj_bwd", grid=(t // tm,),
        in_specs=[pl.BlockSpec((tm, wd), row) for wd in widths] +
                 [_RESIDENT, pl.BlockSpec((tm, d), row),
                  pl.BlockSpec((tm, d), row), pl.BlockSpec((1, N_MOD, d), lambda i: (i // tps, 0, 0)), vec, _token_spec()],
        out_specs=[pl.BlockSpec((tm, d), row), pl.BlockSpec((1, 8, d), lambda i: (i // tps, 0, 0)), vec],
        out_shape=[_out((t, d), F32), _out((nb, 8, d), F32),
                   _out((1, d), F32)],
        compiler_params=_cp(("arbitrary",), 40))(*_pin(*d_parts), w_in_t, *_pin(x, dh1, mod3, g_mix, token))


_VEC_LAYOUT = (("g_mix", 1024), ("conv_b", 1536), ("g_ssd", 1024), ("pool_scale", 512), ("g_mlp", 1024),
               ("g_final", 1024), ("dt_bias", 128), ("a_log", 128), ("d_skip_lanes", 1024), ("sq_err", 1024))
_VEC_OFFSET = {}
_off = 0
for _name, _n in _VEC_LAYOUT:
    _VEC_OFFSET[_name] = _off
    _off += _n
_VEC_LANES = _off
_SMALL_PARAMS = ("b_ada", "g_mix", "conv_w", "conv_b", "dt_bias", "a_log", "d_skip", "g_ssd", "w_pool", "pool_scale",
                 "g_mlp", "g_final")


def _pack_vec(parts):
    cols = []
    for name, n in _VEC_LAYOUT:
        v = parts[name]
        if v.shape[1] < n:
            v = jnp.pad(v, ((0, 0), (0, n - v.shape[1])))
        cols.append(v)
    return jnp.concatenate(cols, 1)


def _small_adam(vec_all, wpool_all, convw_all, dmod_all, params):
    names = _SMALL_PARAMS
    nin = 4 + 3 * len(names)

    def body(*refs):
        vec_ref, wp_ref, cw_ref, dm_ref = refs[:4]
        prm = {n: refs[4 + 3 * i:7 + 3 * i] for i, n in enumerate(names)}
        loss_ref = refs[nin]
        outs = {n: refs[nin + 1 + 4 * i:nin + 5 + 4 * i] for i, n in enumerate(names)}
        vsum = vec_ref[0]
        for s in range(1, N_DEV):
            vsum = vsum + vec_ref[s]

        def lanes(name, n):
            off = _VEC_OFFSET[name]
            return vsum[:, off:off + n]

        grads = {n: lanes(n, prm[n][0].shape[1]) for n in ("g_mix", "conv_b", "g_ssd", "pool_scale", "g_mlp", "g_final", "dt_bias")}
        grads["a_log"] = lanes("a_log", SSD_HEADS) * (-jnp.exp(prm["a_log"][0][...]))
        per_lane = jnp.broadcast_to(lanes("d_skip_lanes", SSD_INNER), (8, SSD_INNER))
        grads["d_skip"] = _exact_nn(per_lane, _head_reduce_matrix(SSD_INNER, SSD_HEADS))[0:1]
        gwp = wp_ref[0].astype(F32)
        gcw = cw_ref[0]
        gb = jnp.sum(dm_ref[0], 0, keepdims=True)
        for s in range(1, N_DEV):
            gwp = gwp + wp_ref[s].astype(F32)
            gcw = gcw + cw_ref[s]
            gb = gb + jnp.sum(dm_ref[s], 0, keepdims=True)
        grads["w_pool"] = gwp
        grads["conv_w"] = gcw[0:4]
        grads["b_ada"] = gb
        total = jnp.sum(lanes("sq_err", D_MODEL), 1, keepdims=True) * (0.5 / D_MODEL)
        loss_ref[...] = jnp.broadcast_to(total, loss_ref.shape)
        for n in names:
            w_ref, m_ref, v_ref = prm[n]
            g = grads[n]
            d, m2, v2 = _adam_math(w_ref[...], g, m_ref[...], v_ref[...])
            g_ref, d_ref, m2_ref, v2_ref = outs[n]
            g_ref[...] = g
            d_ref[...] = d
            m2_ref[...] = m2
            v2_ref[...] = v2

    flat = [vec_all, wpool_all, convw_all, dmod_all]
    out_shape = [jax.ShapeDtypeStruct((1, 128), F32)]
    for n in names:
        flat += list(params[n])
        out_shape += [jax.ShapeDtypeStruct(params[n][0].shape, F32)] * 4
    vm = pl.BlockSpec(memory_space=pltpu.VMEM)
    res = pl.pallas_call(body, name="small_adam", out_shape=out_shape, in_specs=[vm] * len(flat),
                         out_specs=[vm] * len(out_shape), compiler_params=_cp(vmem_mb=48))(*flat)
    return res[0], {n: res[1 + 4 * i:5 + 4 * i] for i, n in enumerate(names)}


_WEIGHTS = ("w_ada", "b_ada", "g_mix", "w_in", "conv_w", "conv_b", "dt_bias", "a_log", "d_skip", "g_ssd", "w_pool",
            "pool_scale", "w_out", "g_mlp", "w_up", "w_down", "g_final")


def _local_step(x2, tg2, mod3, seq, w_in_t, first_token, weights_arrived, weights_later, start_reduce, before_last,
                conv_w_full, sp):
    t, d = x2.shape
    nb = t // seq
    dskip_e = jnp.repeat(sp["d_skip"], SSD_HEAD_DIM, axis=1)
    proj, u1 = _in_proj(x2, mod3, sp["g_mix"], w_in_t, seq, first_token)
    y_ssd, yssm, h_prev, pre = _ssd_fwd(proj, conv_w_full, sp["conv_b"], sp["dt_bias"], sp["a_log"], dskip_e, sp["g_ssd"], nb, seq)
    y_pool, p = _pool_fwd(proj, sp["w_pool"], sp["pool_scale"], nb, seq, weights_arrived(y_ssd))
    w_out_f, w_up4, w_down4 = weights_later(y_pool)
    w_down_f = w_down4.reshape(D_FF, d)
    h1, o, u2 = _out_proj(y_pool, y_ssd, w_out_f, x2, mod3, sp["g_mlp"], seq)
    a_up = _mlp_up(u2, w_up4)
    d_dn, dh2, sq, gg_final, d_gf = _mlp_down_loss(a_up, w_down_f, h1, mod3, sp["g_final"], tg2, seq)

    gw_down = _tn_matmul(a_up, d_dn, 512, d, "grad_w_down", square_relu=True)
    tok = start_reduce("w_down", gw_down.reshape(N_CHIPS, D_FF // N_CHIPS, d))
    d_a = _mlp_down_bwd(d_dn, w_down4, a_up, tok)
    gw_up4 = _tn_matmul(u2, d_a, 512, d, "grad_w_up", out3=True)
    tok = start_reduce("w_up", gw_up4)
    dh1, d_o, accf, gg_mlp = _mlp_up_bwd(d_a, w_up4, h1, dh2, o, mod3, sp["g_mlp"], seq, tok)
    gw_out = _grad_w_out(y_pool, y_ssd, d_o)
    tok = start_reduce("w_out", gw_out.reshape(N_CHIPS, gw_out.shape[0] // N_CHIPS, d))
    d_ypool, d_yssd = _out_proj_bwd(d_o, w_out_f, tok)
    d_upool, gw_pool, g_ps = _pool_bwd(d_ypool, p, sp["w_pool"], sp["pool_scale"], nb, seq)
    d_z, d_pre, d_udt, gg_ssd, gdsk, ga, gdtb = _ssd_bwd(proj, pre, d_yssd, yssm, h_prev, sp["dt_bias"], sp["a_log"],
                                                        dskip_e, sp["g_ssd"], nb, seq)
    d_uxbc, gconvw, gconvb = _conv_bwd(d_pre, proj, conv_w_full, nb, seq)
    gw_in_t = _grad_w_in_t(d_upool, d_z, d_uxbc, d_udt, u1)
    shard_rows = IN_WIDTH // N_CHIPS
    tok = start_reduce("w_in", jnp.stack([gw_in_t[k * shard_rows:(k + 1) * shard_rows] for k in range(N_CHIPS)]))
    gx, accm, gg_mix = _in_proj_bwd([d_upool, d_z, d_uxbc, d_udt], w_in_t, x2, dh1, mod3, sp["g_mix"], seq, before_last(tok))

    d_mod = jnp.concatenate([accm[:, 0], accm[:, 1], accf[:, 2], accf[:, 0], accf[:, 1], d_gf[:, 0]], 1)
    vec = _pack_vec({"g_mix": gg_mix, "conv_b": gconvb, "g_ssd": gg_ssd, "pool_scale": g_ps, "g_mlp": gg_mlp,
                     "g_final": gg_final, "dt_bias": gdtb, "a_log": ga, "d_skip_lanes": gdsk, "sq_err": sq})
    return gx, d_mod, vec, gw_pool, gconvw


def kernel(x, c, w_ada, b_ada, g_mix, w_in, conv_w, conv_b, dt_bias, a_log, d_skip, g_ssd, w_pool, pool_scale, w_out, g_mlp, w_up, w_down, g_final, loss_target, m_w_ada, m_b_ada, m_g_mix, m_w_in, m_conv_w, m_conv_b, m_dt_bias, m_a_log, m_d_skip, m_g_ssd, m_w_pool, m_pool_scale, m_w_out, m_g_mlp, m_w_up, m_w_down, m_g_final, v_w_ada, v_b_ada, v_g_mix, v_w_in, v_conv_w, v_conv_b, v_dt_bias, v_a_log, v_d_skip, v_g_ssd, v_w_pool, v_pool_scale, v_w_out, v_g_mlp, v_w_up, v_w_down, v_g_final):
    nb, seq, d = x.shape
    t = nb * seq
    xi, yi, ci = _mesh_pos()
    chip = 2 * xi + yi
    me = 4 * xi + 2 * yi + ci
    ada_cols = w_ada.shape[2]
    conv_cols = conv_w.shape[2]
    in_cols = w_in.shape[2]
    w_in_s, m_w_in_s, v_w_in_s = w_in[0].T, m_w_in[0].T, v_w_in[0].T

    c_send, c_recv, c_src, c_land, c_token = _exchange_start([c, conv_w[0]], ALL_PEERS, "cond_start")
    w_in_b = w_in_s.astype(BF)
    i_send, i_recv, i_src, i_land, in_token = _ici_start(
        [w_in_b], [jax.ShapeDtypeStruct((N_CHIPS,) + w_in_b.shape, BF)], _gather_sent, _gather_landing, "gather_start_w_in",
        after=c_token)
    c_own, c_got = _exchange_wait(c_send, c_recv, c_src, c_land, ALL_PEERS, in_token, "cond_wait")
    c8, convw8 = [lax.dynamic_update_slice(got, mine[None], (me,) + (0,) * mine.ndim) for got, mine in zip(c_got, c_own)]
    c_all = c8.reshape(N_DEV * nb, d)
    conv_w_full = convw8[0::2].transpose(1, 0, 2).reshape(4, N_CHIPS * conv_cols)
    b_shard = lax.dynamic_slice(b_ada, (0, chip * ada_cols), (1, ada_cols))
    mod_part, c_act = _ada_mod(c_all, w_ada[0], b_shard, in_token)
    mod_rows = mod_part.reshape(N_DEV, nb, ada_cols)
    m_send, m_recv, m_src, m_land, _ = _ici_start(
        [mod_rows], [jax.ShapeDtypeStruct((N_CHIPS, nb, ada_cols), F32)], _mod_sent, _mod_landing, "mod_start", after=mod_part)

    later = [w_out[0].astype(BF), w_up[0].astype(BF), w_down[0].astype(BF)]
    in_shard, in_land = _ici_wait(i_send, i_recv, i_src, i_land, [m_src[0]] + later, _gather_sent, _gather_landing,
                                  "gather_wait_w_in")
    (w_in4,) = _gather_finish(in_land, in_shard)
    w_in_t = w_in4.reshape(N_CHIPS * in_cols, d)
    mod_mine, mod_land = _ici_wait(m_send, m_recv, m_src, m_land, w_in_t, _mod_sent, _mod_landing, "mod_wait")
    mod_own = lax.dynamic_slice(mod_mine[0], (me, 0, 0), (1, nb, ada_cols))
    mod4 = lax.dynamic_update_slice(mod_land[0], mod_own, (chip, 0, 0))
    mod3 = mod4.transpose(1, 0, 2).reshape(nb, N_MOD, d)
    g_send, g_recv, g_src, g_land, first_token = _ici_start(
        later, [jax.ShapeDtypeStruct((N_CHIPS,) + s.shape, BF) for s in later], _gather_sent, _gather_landing, "gather_start",
        after=w_in4)

    def weights_arrived(after):
        shards, lands = _ici_wait(g_send, g_recv, g_src, g_land, after, _gather_sent, _gather_landing, "gather_wait")
        pending["forward"] = _forward_start(lands, "forward_start") + (shards,)
        return pending["forward"][3]

    def weights_later(after):
        f_send, f_recv, f_land, _, shards = pending["forward"]
        lands = _forward_wait(f_send, f_recv, f_land, after, "forward_wait")
        w_out4, w_up4, w_down4 = [lax.dynamic_update_slice(land, shard[None], (chip, 0, 0)) for land, shard in zip(lands, shards)]
        return w_out4.reshape(N_CHIPS * w_out.shape[1], d), w_up4, w_down4

    pending = {}

    def start_reduce(name, grad4):
        pending[name] = _reduce_start(grad4, "reduce_start_" + name)
        return pending[name][4]

    pos = jnp.stack([ci, chip, me]).astype(jnp.int32)
    early = ("w_out", "w_up", "w_down")

    def summed_half(name, after):
        r_send, r_recv, r_src, r_land, _ = pending[name]
        own, recv = _reduce_wait(r_send, r_recv, r_src, r_land, after, "reduce_wait_" + name)
        return _sum_eight(recv, own, pos)

    def before_last(token):
        pending["early_halves"] = _exchange_start([summed_half(n, token) for n in early], SIBLING, "halves_start_early")
        return pending["early_halves"][4]

    sp = dict(g_mix=g_mix, conv_b=conv_b, dt_bias=dt_bias, a_log=a_log, d_skip=d_skip, g_ssd=g_ssd,
              w_pool=w_pool[0], pool_scale=pool_scale, g_mlp=g_mlp, g_final=g_final.reshape(1, d))
    gx, d_mod, vec, gw_pool, gconvw = _local_step(
        x.reshape(t, d), loss_target.reshape(t, d), mod3, seq, w_in_t, first_token, weights_arrived, weights_later, start_reduce,
        before_last, conv_w_full, sp)

    small_parts = [vec, gw_pool.reshape(4 * POOL_GROUP, POOL_GROUP).astype(BF), gconvw, d_mod]
    s_send, s_recv, s_src, s_land, s_token = _exchange_start(small_parts, ALL_PEERS, "small_start")
    h_send, h_recv, h_src, h_land, h_token = _exchange_start([summed_half("w_in", s_token)], SIBLING, "halves_start")
    e_send, e_recv, e_src, e_land, _ = pending["early_halves"]
    e_own, e_got = _exchange_wait(e_send, e_recv, e_src, e_land, SIBLING, h_token, "halves_wait_early")
    res = {}
    for i, (n, w, m, v) in enumerate((("w_out", w_out, m_w_out, v_w_out), ("w_up", w_up, m_w_up, v_w_up),
                                      ("w_down", w_down, m_w_down, v_w_down))):
        g, dl, m2, v2 = _adam_big(e_own[i], e_got[i], w[0], m[0], v[0], pos)
        res[n] = (g[None], dl[None], m2[None], v2[None])

    s_own, s_got = _exchange_wait(s_send, s_recv, s_src, s_land, ALL_PEERS, res["w_down"][1], "small_wait")
    vec8, wpool8, convw8g, dmod8 = [lax.dynamic_update_slice(got, mine[None], (me,) + (0,) * mine.ndim)
                                    for got, mine in zip(s_got, s_own)]
    convw8s = lax.dynamic_slice(convw8g, (0, 0, chip * conv_cols), (N_DEV, 8, conv_cols))
    m_in = dict(b_ada=m_b_ada, g_mix=m_g_mix, conv_w=m_conv_w[0], conv_b=m_conv_b, dt_bias=m_dt_bias, a_log=m_a_log,
                d_skip=m_d_skip, g_ssd=m_g_ssd, w_pool=m_w_pool.reshape(4 * POOL_GROUP, POOL_GROUP), pool_scale=m_pool_scale,
                g_mlp=m_g_mlp, g_final=m_g_final.reshape(1, d))
    v_in = dict(b_ada=v_b_ada, g_mix=v_g_mix, conv_w=v_conv_w[0], conv_b=v_conv_b, dt_bias=v_dt_bias, a_log=v_a_log,
                d_skip=v_d_skip, g_ssd=v_g_ssd, w_pool=v_w_pool.reshape(4 * POOL_GROUP, POOL_GROUP), pool_scale=v_pool_scale,
                g_mlp=v_g_mlp, g_final=v_g_final.reshape(1, d))
    w_small = dict(sp, b_ada=b_ada, conv_w=conv_w[0], w_pool=w_pool.reshape(4 * POOL_GROUP, POOL_GROUP))
    loss_row, small = _small_adam(vec8, wpool8, convw8s, dmod8, {n: (w_small[n], m_in[n], v_in[n]) for n in _SMALL_PARAMS})

    dmod_all = dmod8.reshape(N_DEV * nb, N_CHIPS * ada_cols)
    dmod_cols = lax.dynamic_slice(dmod_all, (0, chip * ada_cols), (N_DEV * nb, ada_cols))
    res.update({n: tuple(r.reshape(w.shape) for r in small[n])
                for n, w in (("b_ada", b_ada), ("g_mix", g_mix), ("conv_w", conv_w), ("conv_b", conv_b), ("dt_bias", dt_bias),
                             ("a_log", a_log), ("d_skip", d_skip), ("g_ssd", g_ssd), ("w_pool", w_pool),
                             ("pool_scale", pool_scale), ("g_mlp", g_mlp), ("g_final", g_final))})
    g_ada, d_ada, m_ada, v_ada = _adam_ada(c_act.T.astype(BF), dmod_cols, w_ada[0], m_w_ada[0], v_w_ada[0])
    res["w_ada"] = (g_ada[None], d_ada[None], m_ada[None], v_ada[None])
    h_own, h_got = _exchange_wait(h_send, h_recv, h_src, h_land, SIBLING, g_ada, "halves_wait")
    rows3 = lambda a: jnp.transpose(a, (2, 0, 1))
    res["w_in"] = tuple(jnp.transpose(r, (1, 2, 0))
                        for r in _adam_rows(h_own[0], h_got[0], rows3(w_in), rows3(m_w_in), rows3(v_w_in), pos))

    loss = loss_row[0, 0]
    return (loss, gx.reshape(nb, seq, d), *[res[n][0] for n in _WEIGHTS], *[res[n][1] for n in _WEIGHTS],
            *[res[n][2] for n in _WEIGHTS], *[res[n][3] for n in _WEIGHTS])
```

```python
import jax
import jax.numpy as jnp
from jax import lax
from jax.experimental import pallas as pl
from jax.experimental.pallas import tpu as pltpu

F32 = jnp.float32
BF = jnp.bfloat16
MESH = pl.DeviceIdType.MESH

EPS = 1e-5
D_MODEL = 1024
POOL_WIDTH = 512
POOL_WINDOWS = (2, 4, 8, 16)
POOL_GROUP = 128
SSD_INNER = 1024
SSD_HEADS = 16
SSD_HEAD_DIM = 64
SSD_STATE = 128
GROUP_W = 512
CHUNK = 128
CONV_CH = 1536
OFF_DT = 3072
IN_WIDTH = 3088
IN_PAD = 3200
D_FF = 4096
N_MOD = 6
N_CHIPS = 4
N_DEV = 8
HALO = 16
CONV_HALO = 8

ADAM_LR = 0.001
ADAM_B1 = 0.9
ADAM_B2 = 0.999
ADAM_EPS = 1e-08
ADAM_WD = 0.01
ADAM_STEP = 10

VMEM_BYTES_V7X = 64 * 1024 * 1024


def _cp(semantics=None, vmem_mb=48, **kw):
    assert vmem_mb * 1024 * 1024 < VMEM_BYTES_V7X
    args = dict(vmem_limit_bytes=vmem_mb * 1024 * 1024, **kw)
    if semantics is not None:
        args["dimension_semantics"] = semantics
    return pltpu.CompilerParams(**args)


def _out(shape, dtype):
    return pltpu.HBM(shape, dtype)


def _pin(*arrays):
    return [pltpu.with_memory_space_constraint(a, pltpu.HBM) for a in arrays]


TOKEN_SHAPE = (8, 128)


def _order_operand(a):
    if a.shape == TOKEN_SHAPE:
        return pl.BlockSpec(memory_space=pltpu.VMEM), a
    return pl.BlockSpec(memory_space=pltpu.HBM), pltpu.with_memory_space_constraint(a, pltpu.HBM)


def _nn(a, b):
    return jnp.dot(a, b, preferred_element_type=F32)


def _nt(a, b):
    return lax.dot_general(a, b, (((1,), (1,)), ((), ())), preferred_element_type=F32)


def _tn(a, b):
    return lax.dot_general(a, b, (((0,), (0,)), ((), ())), preferred_element_type=F32)


def _split3(v):
    hi = v.astype(BF)
    r1 = v - hi.astype(F32)
    mid = r1.astype(BF)
    lo = (r1 - mid.astype(F32)).astype(BF)
    return hi, mid, lo


def _exact_nn(v, m01):
    hi, mid, lo = _split3(v)
    return _nn(hi, m01) + _nn(mid, m01) + _nn(lo, m01)


def _exact_nn_left(m01, v):
    hi, mid, lo = _split3(v)
    return _nn(m01, hi) + _nn(m01, mid) + _nn(m01, lo)


def _exact_nt_left(m01, v):
    hi, mid, lo = _split3(v)
    return _nt(m01, hi) + _nt(m01, mid) + _nt(m01, lo)


def _sigmoid(v):
    return 1.0 / (1.0 + jnp.exp(-v))


def _iota(shape, dim):
    return lax.broadcasted_iota(jnp.int32, shape, dim)


def _head_expand_matrix(heads, width):
    return (_iota((heads, width), 1) // SSD_HEAD_DIM == _iota((heads, width), 0)).astype(BF)


def _head_reduce_matrix(width, heads):
    return (_iota((width, heads), 0) // SSD_HEAD_DIM == _iota((width, heads), 1)).astype(BF)


def _mesh_pos():
    return lax.axis_index("x"), lax.axis_index("y"), lax.axis_index("c")


def _flip(v, bit):
    return v + bit - 2 * bit * v


_HBM = pl.BlockSpec(memory_space=pltpu.HBM)
_SEM = pl.BlockSpec(memory_space=pltpu.SEMAPHORE)
_DATAFLOW = pltpu.SideEffectType.DATAFLOW_SIDE_EFFECTING


def _peer_chip(x, y, j):
    return _flip(x, (j >> 1) & 1), _flip(y, j & 1)


def _ici_start(srcs, land_shapes, sent, landing, name, after):
    n = len(srcs)

    def body(*refs):
        src_refs, land_refs = refs[:n], refs[n:2 * n]
        send_sems, recv_sems = refs[2 * n + 1], refs[2 * n + 2]
        token = refs[-1]
        x, y, c = _mesh_pos()
        for j in range(1, N_CHIPS):
            px, py = _peer_chip(x, y, j)
            for a in range(n):
                pltpu.make_async_remote_copy(
                    src_ref=sent(src_refs[a], c, 2 * px + py), dst_ref=landing(land_refs[a], c, 2 * x + y),
                    send_sem=send_sems.at[a * (N_CHIPS - 1) + j - 1], recv_sem=recv_sems.at[a * (N_CHIPS - 1) + j - 1],
                    device_id=(px, py, c), device_id_type=MESH).start()
        token[...] = jnp.zeros_like(token)

    sems = pltpu.SemaphoreType.DMA((n * (N_CHIPS - 1),))
    after_spec, after = _order_operand(after)
    lands = [pltpu.with_memory_space_constraint(lax.empty(s.shape, s.dtype), pltpu.HBM) for s in land_shapes]
    outs = pl.pallas_call(
        body, name=name,
        out_shape=(sems, sems, *[pltpu.HBM(s.shape, s.dtype) for s in srcs],
                   *[pltpu.HBM(s.shape, s.dtype) for s in land_shapes], jax.ShapeDtypeStruct((8, 128), F32)),
        in_specs=[_HBM] * (2 * n) + [after_spec],
        out_specs=[_SEM, _SEM] + [_HBM] * (2 * n) + [pl.BlockSpec(memory_space=pltpu.VMEM)],
        input_output_aliases={i: 2 + i for i in range(2 * n)},
        compiler_params=pltpu.CompilerParams(has_side_effects=_DATAFLOW),
    )(*_pin(*srcs), *lands, after)
    return outs[0], outs[1], outs[2:2 + n], outs[2 + n:2 + 2 * n], outs[-1]


def _ici_wait(send_sems, recv_sems, src_thru, land_thru, after, sent, landing, name):
    n = len(src_thru)
    afters = [_order_operand(a) for a in (after if isinstance(after, (list, tuple)) else [after])]

    def body(*refs):
        src_refs, land_refs = refs[:n], refs[n:2 * n]
        send_sems, recv_sems = refs[2 * n], refs[2 * n + 1]
        x, y, c = _mesh_pos()
        for j in range(1, N_CHIPS):
            px, py = _peer_chip(x, y, j)
            for a in range(n):
                cp = pltpu.make_async_remote_copy(
                    src_ref=sent(src_refs[a], c, 2 * px + py), dst_ref=landing(land_refs[a], c, 2 * px + py),
                    send_sem=send_sems.at[a * (N_CHIPS - 1) + j - 1], recv_sem=recv_sems.at[a * (N_CHIPS - 1) + j - 1],
                    device_id=(px, py, c), device_id_type=MESH)
                cp.wait_send()
                cp.wait_recv()

    outs = pl.pallas_call(
        body, name=name,
        out_shape=tuple(pltpu.HBM(s.shape, s.dtype) for s in (*src_thru, *land_thru)),
        in_specs=[_HBM] * (2 * n) + [_SEM, _SEM] + [s for s, _ in afters], out_specs=[_HBM] * (2 * n),
        input_output_aliases={i: i for i in range(2 * n)},
        compiler_params=pltpu.CompilerParams(has_side_effects=_DATAFLOW),
    )(*src_thru, *land_thru, send_sems, recv_sems, *[a for _, a in afters])
    return outs[:n], outs[n:]


def _col_half(ref, which, lead=()):
    hc = ref.shape[-1] // 2
    return ref.at[(*lead, slice(None), pl.ds(pl.multiple_of(which * hc, 128), hc))]


def _gather_sent(ref, c, dst_chip):
    return _col_half(ref, c)


def _gather_landing(ref, c, src_chip):
    return _col_half(ref, c, lead=(src_chip,))


def _mod_sent(ref, c, dst_chip):
    return ref.at[2 * dst_chip + c]


def _mod_landing(ref, c, src_chip):
    return ref.at[src_chip]


def _reduce_copy(src_ref, land_ref, send_sems, recv_sems, k, receiving):
    x, y, c = _mesh_pos()
    px, py, pc = _flip(x, (k >> 2) & 1), _flip(y, (k >> 1) & 1), _flip(c, k & 1)
    hc = src_ref.shape[2] // 2
    src = src_ref.at[2 * px + py, :, pl.ds(pl.multiple_of(pc * hc, 128), hc)]
    slot = (4 * px + 2 * py + pc) if receiving else (4 * x + 2 * y + c)
    return pltpu.make_async_remote_copy(
        src_ref=src, dst_ref=land_ref.at[slot], send_sem=send_sems.at[k - 1], recv_sem=recv_sems.at[k - 1],
        device_id=(px, py, pc), device_id_type=MESH)


def _reduce_start(grad4, name):
    k4, r, cols = grad4.shape

    def body(src_ref, land_ref, send_sems, recv_sems, src_thru, land_thru, token):
        for k in range(1, N_DEV):
            _reduce_copy(src_ref, land_ref, send_sems, recv_sems, k, receiving=False).start()
        token[...] = jnp.zeros_like(token)

    sems = pltpu.SemaphoreType.DMA((N_DEV - 1,))
    land = pltpu.with_memory_space_constraint(lax.empty((N_DEV, r, cols // 2), grad4.dtype), pltpu.HBM)
    return pl.pallas_call(
        body, name=name,
        out_shape=(sems, sems, pltpu.HBM(grad4.shape, grad4.dtype), pltpu.HBM(land.shape, land.dtype),
                   jax.ShapeDtypeStruct((8, 128), F32)),
        in_specs=[_HBM, _HBM], out_specs=[_SEM, _SEM, _HBM, _HBM, pl.BlockSpec(memory_space=pltpu.VMEM)],
        input_output_aliases={0: 2, 1: 3},
        compiler_params=pltpu.CompilerParams(has_side_effects=_DATAFLOW),
    )(*_pin(grad4), land)


def _reduce_wait(send_sems, recv_sems, src_thru, land_thru, after, name):
    def body(src_ref, land_ref, send_sems, recv_sems, after_ref, src_out, land_out):
        for k in range(1, N_DEV):
            cp = _reduce_copy(src_ref, land_ref, send_sems, recv_sems, k, receiving=True)
            cp.wait_send()
            cp.wait_recv()

    return pl.pallas_call(
        body, name=name,
        out_shape=(pltpu.HBM(src_thru.shape, src_thru.dtype), pltpu.HBM(land_thru.shape, land_thru.dtype)),
        in_specs=[_HBM, _HBM, _SEM, _SEM, _order_operand(after)[0]], out_specs=[_HBM, _HBM],
        input_output_aliases={0: 0, 1: 1},
        compiler_params=pltpu.CompilerParams(has_side_effects=_DATAFLOW),
    )(src_thru, land_thru, send_sems, recv_sems, _order_operand(after)[1])


def _peer_copy(src_ref, land_ref, send_sems, recv_sems, idx, k, receiving):
    x, y, c = _mesh_pos()
    px, py, pc = _flip(x, (k >> 2) & 1), _flip(y, (k >> 1) & 1), _flip(c, k & 1)
    if land_ref.shape[0] == N_DEV:
        slot = (4 * px + 2 * py + pc) if receiving else (4 * x + 2 * y + c)
    else:
        slot = pc if receiving else c
    return pltpu.make_async_remote_copy(
        src_ref=src_ref, dst_ref=land_ref.at[slot], send_sem=send_sems.at[idx], recv_sem=recv_sems.at[idx],
        device_id=(px, py, pc), device_id_type=MESH)


def _exchange_start(arrays, peers, name):
    n = len(arrays)

    def body(*refs):
        src_refs, land_refs = refs[:n], refs[n:2 * n]
        send_sems, recv_sems = refs[2 * n], refs[2 * n + 1]
        token = refs[-1]
        for j, k in enumerate(peers):
            for a in range(n):
                _peer_copy(src_refs[a], land_refs[a], send_sems, recv_sems, a * len(peers) + j, k, receiving=False).start()
        token[...] = jnp.zeros_like(token)

    sems = pltpu.SemaphoreType.DMA((n * len(peers),))
    n_slots = N_DEV if len(peers) > 1 else 2
    lands = [pltpu.with_memory_space_constraint(lax.empty((n_slots,) + a.shape, a.dtype), pltpu.HBM) for a in arrays]
    outs = pl.pallas_call(
        body, name=name,
        out_shape=(sems, sems, *[pltpu.HBM(a.shape, a.dtype) for a in arrays], *[pltpu.HBM(l.shape, l.dtype) for l in lands],
                   jax.ShapeDtypeStruct((8, 128), F32)),
        in_specs=[_HBM] * (2 * n), out_specs=[_SEM, _SEM] + [_HBM] * (2 * n) + [pl.BlockSpec(memory_space=pltpu.VMEM)],
        input_output_aliases={i: 2 + i for i in range(2 * n)},
        compiler_params=pltpu.CompilerParams(has_side_effects=_DATAFLOW),
    )(*_pin(*arrays), *lands)
    return outs[0], outs[1], outs[2:2 + n], outs[2 + n:2 + 2 * n], outs[-1]


def _exchange_wait(send_sems, recv_sems, src_thru, land_thru, peers, after, name):
    n = len(src_thru)

    def body(*refs):
        src_refs, land_refs = refs[:n], refs[n:2 * n]
        send_sems, recv_sems = refs[2 * n], refs[2 * n + 1]
        for j, k in enumerate(peers):
            for a in range(n):
                cp = _peer_copy(src_refs[a], land_refs[a], send_sems, recv_sems, a * len(peers) + j, k, receiving=True)
                cp.wait_send()
                cp.wait_recv()

    outs = pl.pallas_call(
        body, name=name,
        out_shape=tuple(pltpu.HBM(s.shape, s.dtype) for s in (*src_thru, *land_thru)),
        in_specs=[_HBM] * (2 * n) + [_SEM, _SEM, _order_operand(after)[0]], out_specs=[_HBM] * (2 * n),
        input_output_aliases={i: i for i in range(2 * n)},
        compiler_params=pltpu.CompilerParams(has_side_effects=_DATAFLOW),
    )(*src_thru, *land_thru, send_sems, recv_sems, _order_operand(after)[1])
    return outs[:n], outs[n:]


ALL_PEERS = tuple(range(1, N_DEV))
SIBLING = (1,)


def _sum_eight(recv, grad4, pos):
    n, r, hc = recv.shape
    steps = 2
    tc = hc // steps

    def body(pos_ref, r_ref, g_ref, o_ref):
        me = pos_ref[2]
        o_ref[...] = jnp.zeros_like(o_ref)
        for s in range(n):
            @pl.when(me == s)
            def _():
                o_ref[...] += g_ref[0].astype(F32)

            @pl.when(me != s)
            def _():
                o_ref[...] += r_ref[s].astype(F32)

    grid_spec = pltpu.PrefetchScalarGridSpec(
        num_scalar_prefetch=1, grid=(steps,),
        in_specs=[pl.BlockSpec((n, r, tc), lambda i, pos: (0, 0, i)),
                  pl.BlockSpec((1, r, tc), lambda i, pos: (pos[1], 0, pos[0] * steps + i))],
        out_specs=pl.BlockSpec((r, tc), lambda i, pos: (0, i)))
    return pl.pallas_call(body, name="sum_eight", grid_spec=grid_spec, out_shape=_out((r, hc), F32),
                          compiler_params=_cp(("parallel",), 32))(pos, *_pin(recv, grad4))


def _forward_copy(land_ref, send_sems, recv_sems, idx, j, receiving):
    x, y, c = _mesh_pos()
    px, py = _peer_chip(x, y, j)
    mine = _col_half(land_ref, c, lead=(2 * px + py,))
    theirs = _col_half(land_ref, 1 - c, lead=(2 * px + py,))
    return pltpu.make_async_remote_copy(
        src_ref=mine, dst_ref=theirs if receiving else mine, send_sem=send_sems.at[idx], recv_sem=recv_sems.at[idx],
        device_id=(x, y, 1 - c), device_id_type=MESH)


def _forward_start(lands, name):
    n = len(lands)

    def body(*refs):
        land_refs, send_sems, recv_sems, token = refs[:n], refs[n], refs[n + 1], refs[-1]
        for j in range(1, N_CHIPS):
            for a in range(n):
                _forward_copy(land_refs[a], send_sems, recv_sems, a * (N_CHIPS - 1) + j - 1, j, receiving=False).start()
        token[...] = jnp.zeros_like(token)

    sems = pltpu.SemaphoreType.DMA((n * (N_CHIPS - 1),))
    outs = pl.pallas_call(
        body, name=name,
        out_shape=(sems, sems, *[pltpu.HBM(l.shape, l.dtype) for l in lands], jax.ShapeDtypeStruct((8, 128), F32)),
        in_specs=[_HBM] * n, out_specs=[_SEM, _SEM] + [_HBM] * n + [pl.BlockSpec(memory_space=pltpu.VMEM)],
        input_output_aliases={i: 2 + i for i in range(n)},
        compiler_params=pltpu.CompilerParams(has_side_effects=_DATAFLOW),
    )(*lands)
    return outs[0], outs[1], outs[2:2 + n], outs[-1]


def _forward_wait(send_sems, recv_sems, lands_thru, after, name):
    n = len(lands_thru)

    def body(*refs):
        land_refs, send_sems, recv_sems = refs[:n], refs[n], refs[n + 1]
        for j in range(1, N_CHIPS):
            for a in range(n):
                cp = _forward_copy(land_refs[a], send_sems, recv_sems, a * (N_CHIPS - 1) + j - 1, j, receiving=True)
                cp.wait_send()
                cp.wait_recv()

    return pl.pallas_call(
        body, name=name,
        out_shape=tuple(pltpu.HBM(l.shape, l.dtype) for l in lands_thru),
        in_specs=[_HBM] * n + [_SEM, _SEM, _order_operand(after)[0]], out_specs=[_HBM] * n,
        input_output_aliases={i: i for i in range(n)},
        compiler_params=pltpu.CompilerParams(has_side_effects=_DATAFLOW),
    )(*lands_thru, send_sems, recv_sems, _order_operand(after)[1])


def _gather_finish(lands, shards):
    n = len(lands)
    any_spec = _HBM

    def body(*refs):
        shard_refs, out_refs = refs[n:2 * n], refs[2 * n:3 * n]
        send_sems, recv_sems, local_sems = refs[3 * n:]
        x, y, c = _mesh_pos()
        chip = 2 * x + y
        local, sends = [], []
        for a in range(n):
            cp = pltpu.make_async_copy(shard_refs[a], out_refs[a].at[chip], local_sems.at[a])
            cp.start()
            local.append(cp)
        for j in range(1, N_CHIPS):
            px, py = _peer_chip(x, y, j)
            for a in range(n):
                landed = _col_half(out_refs[a], c, lead=(2 * px + py,))
                cp = pltpu.make_async_remote_copy(
                    src_ref=landed, dst_ref=landed, send_sem=send_sems.at[a, j], recv_sem=recv_sems.at[a, j],
                    device_id=(x, y, 1 - c), device_id_type=MESH)
                cp.start()
                sends.append(cp)
        for j in range(1, N_CHIPS):
            px, py = _peer_chip(x, y, j)
            for a in range(n):
                other = _col_half(out_refs[a], 1 - c, lead=(2 * px + py,))
                pltpu.make_async_remote_copy(
                    src_ref=other, dst_ref=other, send_sem=send_sems.at[a, j], recv_sem=recv_sems.at[a, j],
                    device_id=(x, y, 1 - c), device_id_type=MESH).wait_recv()
        for cp in sends:
            cp.wait_send()
        for cp in local:
            cp.wait()

    return pl.pallas_call(
        body, name="gather_finish",
        out_shape=[_out(l.shape, l.dtype) for l in lands],
        in_specs=[any_spec] * (2 * n), out_specs=[any_spec] * n,
        input_output_aliases={i: i for i in range(n)},
        scratch_shapes=[pltpu.SemaphoreType.DMA((n, N_CHIPS))] * 2 + [pltpu.SemaphoreType.DMA((n,))],
        compiler_params=_cp(vmem_mb=16),
    )(*lands, *shards)


def _adam_math(w, g, m, v):
    m2 = ADAM_B1 * m + (1.0 - ADAM_B1) * g
    v2 = ADAM_B2 * v + (1.0 - ADAM_B2) * (g * g)
    m_hat = m2 / (1.0 - ADAM_B1 ** ADAM_STEP)
    v_hat = v2 / (1.0 - ADAM_B2 ** ADAM_STEP)
    delta = -ADAM_LR * (m_hat / (jnp.sqrt(v_hat) + ADAM_EPS) + ADAM_WD * w)
    return delta, m2, v2


def _adam_big(g_own, g_pair, w, m, v, pos):
    r, c = w.shape
    per_half = 2
    tc = c // (2 * per_half)

    def body(pos_ref, go_ref, gp_ref, w_ref, m_ref, v_ref, g_ref, d_ref, m2_ref, v2_ref):
        half = pl.program_id(0) // per_half
        g = jnp.where(half == pos_ref[0], go_ref[...], gp_ref[0])
        d, m2, v2 = _adam_math(w_ref[...], g, m_ref[...], v_ref[...])
        g_ref[...] = g
        d_ref[...] = d
        m2_ref[...] = m2
        v2_ref[...] = v2

    spec = pl.BlockSpec((r, tc), lambda i, pos: (0, i))
    grid_spec = pltpu.PrefetchScalarGridSpec(
        num_scalar_prefetch=1, grid=(2 * per_half,),
        in_specs=[pl.BlockSpec((r, tc), lambda i, pos: (0, i % per_half)),
                  pl.BlockSpec((1, r, tc), lambda i, pos: (1 - pos[0], 0, i % per_half)), spec, spec, spec],
        out_specs=[spec] * 4)
    sh = _out((r, c), F32)
    return pl.pallas_call(body, name="adam_big", grid_spec=grid_spec, out_shape=[sh] * 4,
                          compiler_params=_cp(("parallel",), 32))(pos, *_pin(g_own, g_pair, w, m, v))


def _adam_rows(g_own, g_pair, w3, m3, v3, pos):
    r, _, c = w3.shape
    tr = 128

    def body(pos_ref, go_ref, gp_ref, w_ref, m_ref, v_ref, g_ref, d_ref, m2_ref, v2_ref):
        core = pos_ref[0]
        g = jnp.concatenate([jnp.where(core == 0, go_ref[...], gp_ref[0]), jnp.where(core == 1, go_ref[...], gp_ref[0])], 1)
        d, m2, v2 = _adam_math(w_ref[:, 0, :], g, m_ref[:, 0, :], v_ref[:, 0, :])
        g_ref[:, 0, :] = g
        d_ref[:, 0, :] = d
        m2_ref[:, 0, :] = m2
        v2_ref[:, 0, :] = v2

    spec = pl.BlockSpec((tr, 1, c), lambda i, pos: (i, 0, 0))
    grid_spec = pltpu.PrefetchScalarGridSpec(
        num_scalar_prefetch=1, grid=(pl.cdiv(r, tr),),
        in_specs=[pl.BlockSpec((tr, c // 2), lambda i, pos: (i, 0)),
                  pl.BlockSpec((1, tr, c // 2), lambda i, pos: (1 - pos[0], i, 0)), spec, spec, spec],
        out_specs=[spec] * 4)
    sh = _out(w3.shape, F32)
    return pl.pallas_call(body, name="adam_rows", grid_spec=grid_spec, out_shape=[sh] * 4,
                          compiler_params=_cp(("parallel",), 32))(pos, *_pin(g_own, g_pair, w3, m3, v3))


def _adam_ada(c_act_t, dmod_cols, w, m, v):
    r, c = w.shape
    tc = 512

    def body(ct_ref, dm_ref, w_ref, m_ref, v_ref, g_ref, d_ref, m2_ref, v2_ref):
        g = _nn(ct_ref[...], dm_ref[...].astype(BF))
        d, m2, v2 = _adam_math(w_ref[...], g, m_ref[...], v_ref[...])
        g_ref[...] = g
        d_ref[...] = d
        m2_ref[...] = m2
        v2_ref[...] = v2

    spec = pl.BlockSpec((r, tc), lambda i: (0, i))
    sh = _out((r, c), F32)
    return pl.pallas_call(
        body, name="adam_ada", grid=(c // tc,),
        in_specs=[pl.BlockSpec(c_act_t.shape, lambda i: (0, 0)), pl.BlockSpec((dmod_cols.shape[0], tc), lambda i: (0, i)),
                  spec, spec, spec],
        out_specs=[spec] * 4, out_shape=[sh] * 4, compiler_params=_cp(("parallel",), 48))(*_pin(c_act_t, dmod_cols, w, m, v))


def _ada_mod(c_all, w_shard, b_shard, token):
    nb, d = c_all.shape
    cols = w_shard.shape[1]
    tc = 512

    def body(c_ref, w_ref, b_ref, tok_ref, mod_ref, act_ref):
        cv = c_ref[...]
        act = cv * _sigmoid(cv)
        act_ref[...] = act
        mod_ref[...] = _nn(act.astype(BF), w_ref[...].astype(BF)) + b_ref[...]

    return pl.pallas_call(
        body, name="ada_mod", grid=(cols // tc,),
        in_specs=[pl.BlockSpec((nb, d), lambda i: (0, 0)), pl.BlockSpec((d, tc), lambda i: (0, i)),
                  pl.BlockSpec((1, tc), lambda i: (0, i)), _token_spec()],
        out_specs=[pl.BlockSpec((nb, tc), lambda i: (0, i)), pl.BlockSpec((nb, d), lambda i: (0, 0))],
        out_shape=[_out((nb, cols), F32), _out((nb, d), F32)],
        compiler_params=_cp(("arbitrary",), 32))(*_pin(c_all, w_shard, b_shard, token))


SUB_ROWS = 256
ROW_TILE = 512


def _sub_rows(tm):
    return [slice(s, s + SUB_ROWS) for s in range(0, tm, SUB_ROWS)] if tm > SUB_ROWS else [slice(0, tm)]


_RESIDENT = pl.BlockSpec(memory_space=pltpu.VMEM)


def _token_spec():
    return pl.BlockSpec((8, 128), lambda *_: (0, 0))


def _in_proj(x, mod3, g_mix, w_in_t, seq, token):
    t, d = x.shape
    tm = min(ROW_TILE, seq)
    tps = seq // tm

    def body(x_ref, mod_ref, g_ref, w_ref, tok_ref, proj_ref, u1_ref):
        for rows in _sub_rows(tm):
            xv = x_ref[rows, :]
            r = lax.rsqrt(jnp.mean(xv * xv, -1, keepdims=True) + EPS)
            u = (xv * r * g_ref[...]) * (1.0 + mod_ref[0, 1:2, :]) + mod_ref[0, 0:1, :]
            ub = u.astype(BF)
            u1_ref[rows, :] = ub
            proj_ref[rows, 0:OFF_DT] = _nt(ub, w_ref[0:OFF_DT, :])
            proj_ref[rows, OFF_DT:IN_PAD] = jnp.zeros((rows.stop - rows.start, IN_PAD - OFF_DT), F32)
            proj_ref[rows, OFF_DT:IN_WIDTH] = _nt(ub, w_ref[OFF_DT:IN_WIDTH, :])

    return pl.pallas_call(
        body, name="in_proj", grid=(t // tm,),
        in_specs=[pl.BlockSpec((tm, d), lambda i: (i, 0)), pl.BlockSpec((1, N_MOD, d), lambda i: (i // tps, 0, 0)),
                  pl.BlockSpec((1, d), lambda i: (0, 0)), _RESIDENT, _token_spec()],
        out_specs=[pl.BlockSpec((tm, IN_PAD), lambda i: (i, 0)), pl.BlockSpec((tm, d), lambda i: (i, 0))],
        out_shape=[_out((t, IN_PAD), F32), _out((t, d), BF)],
        compiler_params=_cp(("parallel",), 40))(*_pin(x, mod3, g_mix), w_in_t, *_pin(token))


def _pool_tile(seq):
    return min(1024, seq)


def _pool_fwd(proj, w_pool, pool_scale, nb, seq, token):
    ts = _pool_tile(seq)
    nt = seq // ts

    def body(u_ref, halo_ref, wp_ref, ps_ref, tok_ref, yp_ref, p_ref):
        i = pl.program_id(1)
        halo = jnp.where(i == 0, 0.0, halo_ref[...])
        u = u_ref[...]
        ext = jnp.concatenate([halo, u], 0)
        tpos = i * ts + _iota((ts, 1), 0)
        for g, w in enumerate(POOL_WINDOWS):
            gs = slice(g * POOL_GROUP, (g + 1) * POOL_GROUP)
            s = ext[:, gs]
            sh = 1
            while sh < w:
                s = s + pltpu.roll(s, sh, 0)
                sh *= 2
            cnt = jnp.minimum(tpos + 1, w).astype(F32)
            pb = (s[HALO:] / cnt - u[:, gs]).astype(BF)
            p_ref[:, gs] = pb
            yp_ref[:, gs] = (_nn(pb, wp_ref[g].astype(BF)) * ps_ref[:, gs]).astype(BF)

    hb = ts // HALO
    return pl.pallas_call(
        body, name="pool_fwd", grid=(nb, nt),
        in_specs=[pl.BlockSpec((ts, POOL_WIDTH), lambda b, i: (b * nt + i, 0)),
                  pl.BlockSpec((HALO, POOL_WIDTH), lambda b, i: (jnp.maximum((b * nt + i) * hb - 1, 0), 0)),
                  pl.BlockSpec((4, POOL_GROUP, POOL_GROUP), lambda b, i: (0, 0, 0)),
                  pl.BlockSpec((1, POOL_WIDTH), lambda b, i: (0, 0)), _token_spec()],
        out_specs=[pl.BlockSpec((ts, POOL_WIDTH), lambda b, i: (b * nt + i, 0))] * 2,
        out_shape=[_out((nb * seq, POOL_WIDTH), BF)] * 2,
        compiler_params=_cp(("parallel", "parallel"), 32))(*_pin(proj, proj, w_pool, pool_scale, token))


def _conv_pre(uxbc, halo, cw, cb, first):
    halo = jnp.where(first, 0.0, halo)
    ext = jnp.concatenate([halo, uxbc], 0)
    pre = cb + uxbc * cw[3:4]
    for k in (2, 1, 0):
        pre = pre + pltpu.roll(ext, 3 - k, 0)[CONV_HALO:] * cw[k:k + 1]
    return pre


def _chunk_terms(pre, udt, dtb, alog):
    sg = _sigmoid(pre)
    xbc = pre * sg
    dtp = udt[:, :SSD_HEADS] + dtb
    dt = jnp.maximum(dtp, 0.0) + jnp.log(1.0 + jnp.exp(-jnp.abs(dtp)))
    a = -jnp.exp(alog)
    da = dt * a
    tril = (_iota((CHUNK, CHUNK), 0) >= _iota((CHUNK, CHUNK), 1))
    acum = _exact_nn_left(tril.astype(BF), da)
    eye = (_iota((SSD_HEADS, SSD_HEADS), 0) == _iota((SSD_HEADS, SSD_HEADS), 1)).astype(BF)
    acum_t = _exact_nt_left(eye, acum)
    expand = _head_expand_matrix(SSD_HEADS, SSD_INNER)
    acum_e = _exact_nn(acum, expand)
    dt_e = _exact_nn(dt, expand)
    last_e = acum_e[CHUNK - 1:CHUNK]
    return dict(pre=pre, sg=sg, xbc=xbc, dtp=dtp, dt=dt, a=a, acum=acum, acum_t=acum_t, tril=tril,
                dt_e=dt_e, e_a=jnp.exp(acum_e), d_out=jnp.exp(last_e - acum_e), c_dec=jnp.exp(last_e))


def _head_decay(r, h):
    seg = r["acum"][:, h:h + 1] - r["acum_t"][h:h + 1, :]
    return jnp.where(r["tril"], jnp.exp(jnp.minimum(seg, 0.0)), 0.0)


SSD_SUB = 4
SSD_ROWS = SSD_SUB * CHUNK


def _ssd_specs(nb, seq, reverse):
    ns = seq // SSD_ROWS
    per = seq // CONV_HALO

    def cidx(c):
        return (ns - 1 - c) if reverse else c

    def row(b, c):
        return b * ns + cidx(c)

    specs = [
        pl.BlockSpec((SSD_ROWS, CONV_CH), lambda b, c: (row(b, c), 1)),
        pl.BlockSpec((CONV_HALO, CONV_CH),
                     lambda b, c: (jnp.maximum(b * per + cidx(c) * (SSD_ROWS // CONV_HALO) - 1, 0), 1)),
        pl.BlockSpec((SSD_ROWS, GROUP_W), lambda b, c: (row(b, c), 1)),
        pl.BlockSpec((SSD_ROWS, GROUP_W), lambda b, c: (row(b, c), 2)),
        pl.BlockSpec((SSD_ROWS, 128), lambda b, c: (row(b, c), OFF_DT // 128)),
    ]
    return specs, row, cidx, ns


def _const_spec(shape):
    return pl.BlockSpec(shape, lambda b, c: (0,) * len(shape))


def _ssd_fwd(proj, conv_w, conv_b, dt_bias, a_log, dskip_e, g_ssd, nb, seq):
    specs, row, cidx, ns = _ssd_specs(nb, seq, reverse=False)

    def body(uxbc_ref, halo_ref, z0_ref, z1_ref, udt_ref, cw_ref, cb_ref, dtb_ref, alog_ref, dsk_ref, gs_ref,
             yssd_ref, yssm_ref, hprev_ref, pre_ref, h_ref, yd_ref):
        c = pl.program_id(1)

        @pl.when(c == 0)
        def _():
            h_ref[...] = jnp.zeros_like(h_ref)

        for sub in range(SSD_SUB):
            rows = slice(sub * CHUNK, (sub + 1) * CHUNK)
            if sub == 0:
                halo, first = halo_ref[...], c == 0
            else:
                halo, first = uxbc_ref[sub * CHUNK - CONV_HALO:sub * CHUNK, :], False
            pre = _conv_pre(uxbc_ref[rows, :], halo, cw_ref[...], cb_ref[...], first)
            pre_ref[rows, :] = pre
            r = _chunk_terms(pre, udt_ref[rows, :], dtb_ref[...], alog_ref[...])
            xbc = r["xbc"]
            xs = xbc[:, :SSD_INNER]
            xdt = xs * r["dt_e"]
            xdt_b = xdt.astype(BF)
            xdo_b = (xdt * r["d_out"]).astype(BF)
            hprev_ref[0, sub] = h_ref[...]
            for g in range(2):
                gs = slice(g * GROUP_W, (g + 1) * GROUP_W)
                bg = xbc[:, SSD_INNER + g * SSD_STATE:SSD_INNER + (g + 1) * SSD_STATE].astype(BF)
                cg = xbc[:, SSD_INNER + (2 + g) * SSD_STATE:SSD_INNER + (3 + g) * SSD_STATE].astype(BF)
                scores = _nt(cg, bg)
                hg = h_ref[g]
                y_off = _nn(cg, hg.astype(BF)) * r["e_a"][:, gs]
                for hh in range(8):
                    h = g * 8 + hh
                    hs = slice(h * SSD_HEAD_DIM, (h + 1) * SSD_HEAD_DIM)
                    m = (scores * _head_decay(r, h)).astype(BF)
                    yd_ref[sub, :, hs] = _nn(m, xdt_b[:, hs])
                h_ref[g] = hg * r["c_dec"][:, gs] + _tn(bg, xdo_b[:, gs])
                y = yd_ref[sub, :, gs] + y_off + dsk_ref[:, gs] * xs[:, gs]
                yssm_ref[rows, gs] = y
                zg = (z0_ref if g == 0 else z1_ref)[rows, :]
                yg = y * (zg * _sigmoid(zg))
                rg = lax.rsqrt(jnp.mean(yg * yg, -1, keepdims=True) + EPS)
                yssd_ref[rows, gs] = (yg * rg * gs_ref[:, gs]).astype(BF)

    t = nb * seq
    return pl.pallas_call(
        body, name="ssd_fwd", grid=(nb, ns),
        in_specs=specs + [_const_spec((4, CONV_CH)), _const_spec((1, CONV_CH)), _const_spec((1, SSD_HEADS)),
                          _const_spec((1, SSD_HEADS)), _const_spec((1, SSD_INNER)), _const_spec((1, SSD_INNER))],
        out_specs=[pl.BlockSpec((SSD_ROWS, SSD_INNER), lambda b, c: (row(b, c), 0)),
                   pl.BlockSpec((SSD_ROWS, SSD_INNER), lambda b, c: (row(b, c), 0)),
                   pl.BlockSpec((1, SSD_SUB, 2, SSD_STATE, GROUP_W), lambda b, c: (b, c, 0, 0, 0)),
                   pl.BlockSpec((SSD_ROWS, CONV_CH), lambda b, c: (row(b, c), 0))],
        out_shape=[_out((t, SSD_INNER), BF), _out((t, SSD_INNER), F32),
                   _out((nb, seq // CHUNK, 2, SSD_STATE, GROUP_W), F32), _out((t, CONV_CH), F32)],
        scratch_shapes=[pltpu.VMEM((2, SSD_STATE, GROUP_W), F32), pltpu.VMEM((SSD_SUB, CHUNK, SSD_INNER), F32)],
        compiler_params=_cp(("arbitrary", "arbitrary"), 56),
    )(*_pin(proj, proj, proj, proj, proj, conv_w, conv_b, dt_bias, a_log, dskip_e, g_ssd))


def _out_proj(y_pool, y_ssd, w_out, x, mod3, g_mlp, seq):
    t, d = x.shape
    tm = 512
    tps = seq // tm if seq >= tm else 1
    tm = min(tm, seq)

    def body(yp_ref, ys_ref, w_ref, x_ref, mod_ref, g_ref, h1_ref, o_ref, u2_ref):
        for rows in _sub_rows(tm):
            o = _nn(jnp.concatenate([yp_ref[rows, :], ys_ref[rows, :]], 1), w_ref[...])
            o_ref[rows, :] = o.astype(BF)
            h1 = x_ref[rows, :] + mod_ref[0, 2:3, :] * o
            h1_ref[rows, :] = h1
            r = lax.rsqrt(jnp.mean(h1 * h1, -1, keepdims=True) + EPS)
            u2_ref[rows, :] = ((h1 * r * g_ref[...]) * (1.0 + mod_ref[0, 4:5, :]) + mod_ref[0, 3:4, :]).astype(BF)

    row = lambda i: (i, 0)
    return pl.pallas_call(
        body, name="out_proj", grid=(t // tm,),
        in_specs=[pl.BlockSpec((tm, POOL_WIDTH), row), pl.BlockSpec((tm, SSD_INNER), row),
                  _RESIDENT, pl.BlockSpec((tm, d), row),
                  pl.BlockSpec((1, N_MOD, d), lambda i: (i // tps, 0, 0)), pl.BlockSpec((1, d), lambda i: (0, 0))],
        out_specs=[pl.BlockSpec((tm, d), row)] * 3,
        out_shape=[_out((t, d), F32), _out((t, d), BF), _out((t, d), BF)],
        compiler_params=_cp(("parallel",), 48))(*_pin(y_pool, y_ssd), w_out, *_pin(x, mod3, g_mlp))


def _mlp_up(u2, w_up4):
    t, d = u2.shape
    tm = min(1024, t)
    nk, _, cols = w_up4.shape

    def body(u_ref, w_ref, a_ref):
        a_ref[...] = _nn(u_ref[...], w_ref[pl.program_id(1)]).astype(BF)

    return pl.pallas_call(
        body, name="mlp_up", grid=(t // tm, nk),
        in_specs=[pl.BlockSpec((tm, d), lambda i, k: (i, 0)), _RESIDENT],
        out_specs=pl.BlockSpec((tm, cols), lambda i, k: (i, k)),
        out_shape=_out((t, nk * cols), BF),
        compiler_params=_cp(("parallel", "parallel"), 32))(*_pin(u2), w_up4)


def _mlp_down_loss(a_up, w_down, h1, mod3, g_final, target, seq):
    t, d = h1.shape
    nb = t // seq
    tm = min(ROW_TILE, seq)
    tps = seq // tm

    def body(a_ref, w_ref, h1_ref, mod_ref, g_ref, tg_ref, ddn_ref, dh2_ref, sq_ref, gg_ref, dgf_ref):
        i = pl.program_id(0)

        @pl.when(i == 0)
        def _():
            sq_ref[...] = jnp.zeros_like(sq_ref)
            gg_ref[...] = jnp.zeros_like(gg_ref)

        @pl.when(i % tps == 0)
        def _():
            dgf_ref[...] = jnp.zeros_like(dgf_ref)

        gate = mod_ref[0, 5:6, :]
        sq = gg = dgf = 0.0
        for rows in _sub_rows(tm):
            f = jnp.square(jnp.maximum(a_ref[rows, :], 0))
            dn = _nn(f, w_ref[...])
            h2 = h1_ref[rows, :] + gate * dn
            r = lax.rsqrt(jnp.mean(h2 * h2, -1, keepdims=True) + EPS)
            hh = h2 * r
            err = hh * g_ref[...] - tg_ref[rows, :]
            dy = err * (1.0 / d)
            dhat = dy * g_ref[...]
            dh2 = r * (dhat - hh * jnp.mean(dhat * hh, -1, keepdims=True))
            dh2_ref[rows, :] = dh2
            ddn_ref[rows, :] = (dh2 * gate).astype(BF)
            sq = sq + jnp.sum(err * err, 0, keepdims=True)
            gg = gg + jnp.sum(dy * hh, 0, keepdims=True)
            dgf = dgf + jnp.sum(dh2 * dn, 0, keepdims=True)
        sq_ref[...] += sq
        gg_ref[...] += gg
        dgf_ref[0] += dgf

    row = lambda i: (i, 0)
    vec = pl.BlockSpec((1, d), lambda i: (0, 0))
    return pl.pallas_call(
        body, name="mlp_down_loss", grid=(t // tm,),
        in_specs=[pl.BlockSpec((tm, D_FF), row), _RESIDENT, pl.BlockSpec((tm, d), row),
                  pl.BlockSpec((1, N_MOD, d), lambda i: (i // tps, 0, 0)), vec, pl.BlockSpec((tm, d), row)],
        out_specs=[pl.BlockSpec((tm, d), row), pl.BlockSpec((tm, d), row), vec, vec,
                   pl.BlockSpec((1, 1, d), lambda i: (i // tps, 0, 0))],
        out_shape=[_out((t, d), BF), _out((t, d), F32), _out((1, d), F32),
                   _out((1, d), F32), _out((nb, 1, d), F32)],
        compiler_params=_cp(("arbitrary",), 44))(*_pin(a_up), w_down, *_pin(h1, mod3, g_final, target))


def _tn_matmul(a, b, tk, tn, name, square_relu=False, out3=False):
    t, kdim = a.shape
    ndim = b.shape[1]

    def body(a_ref, b_ref, o_ref):
        av = a_ref[...]
        if square_relu:
            av = jnp.square(jnp.maximum(av, 0))
        res = _tn(av, b_ref[...]).astype(BF)
        if out3:
            o_ref[0] = res
        else:
            o_ref[...] = res

    if out3:
        out_spec = pl.BlockSpec((1, tk, tn), lambda j, i: (j, i, 0))
        out_shape = _out((ndim // tn, kdim, tn), BF)
    else:
        out_spec = pl.BlockSpec((tk, tn), lambda j, i: (i, j))
        out_shape = _out((kdim, ndim), BF)
    return pl.pallas_call(
        body, name=name, grid=(ndim // tn, kdim // tk),
        in_specs=[pl.BlockSpec((t, tk), lambda j, i: (0, i)), pl.BlockSpec((t, tn), lambda j, i: (0, j))],
        out_specs=out_spec, out_shape=out_shape,
        compiler_params=_cp(("parallel", "parallel"), 56))(*_pin(a, b))


def _mlp_down_bwd(d_dn, w_down4, a_up, token):
    t, d = d_dn.shape
    tm = min(1024, t)
    nk, rows, _ = w_down4.shape

    def body(g_ref, w_ref, a_ref, tok_ref, o_ref):
        df = _nt(g_ref[...], w_ref[pl.program_id(1)])
        o_ref[...] = (df * (2.0 * jnp.maximum(a_ref[...], 0).astype(F32))).astype(BF)

    return pl.pallas_call(
        body, name="mlp_down_bwd", grid=(t // tm, nk),
        in_specs=[pl.BlockSpec((tm, d), lambda i, k: (i, 0)), _RESIDENT,
                  pl.BlockSpec((tm, rows), lambda i, k: (i, k)), _token_spec()],
        out_specs=pl.BlockSpec((tm, rows), lambda i, k: (i, k)),
        out_shape=_out((t, nk * rows), BF),
        compiler_params=_cp(("parallel", "parallel"), 32))(*_pin(d_dn), w_down4, *_pin(a_up, token))


def _mlp_up_bwd(d_a, w_up4, h1, dh2, o, mod3, g_mlp, seq, token):
    t, d = h1.shape
    nb = t // seq
    tm = min(ROW_TILE, seq)
    tps = seq // tm
    nk = w_up4.shape[0]
    cols = w_up4.shape[2]

    def body(da_ref, w_ref, h1_ref, dh2_ref, o_ref, mod_ref, g_ref, tok_ref, dh1_ref, do_ref, acc_ref, gg_ref):
        i = pl.program_id(0)

        @pl.when(i == 0)
        def _():
            gg_ref[...] = jnp.zeros_like(gg_ref)

        @pl.when(i % tps == 0)
        def _():
            acc_ref[...] = jnp.zeros_like(acc_ref)

        gg = a_shift = a_scale = a_gate = 0.0
        for rows in _sub_rows(tm):
            du = _nt(da_ref[rows, 0:cols], w_ref[0])
            for k in range(1, nk):
                du = du + _nt(da_ref[rows, k * cols:(k + 1) * cols], w_ref[k])
            h1 = h1_ref[rows, :]
            r = lax.rsqrt(jnp.mean(h1 * h1, -1, keepdims=True) + EPS)
            hh = h1 * r
            n2 = hh * g_ref[...]
            dn2 = du * (1.0 + mod_ref[0, 4:5, :])
            dhat = dn2 * g_ref[...]
            dh1 = dh2_ref[rows, :] + r * (dhat - hh * jnp.mean(dhat * hh, -1, keepdims=True))
            dh1_ref[rows, :] = dh1
            do_ref[rows, :] = (dh1 * mod_ref[0, 2:3, :]).astype(BF)
            gg = gg + jnp.sum(dn2 * hh, 0, keepdims=True)
            a_shift = a_shift + jnp.sum(du, 0, keepdims=True)
            a_scale = a_scale + jnp.sum(du * n2, 0, keepdims=True)
            a_gate = a_gate + jnp.sum(dh1 * o_ref[rows, :].astype(F32), 0, keepdims=True)
        gg_ref[...] += gg
        acc_ref[0, 0:1, :] += a_shift
        acc_ref[0, 1:2, :] += a_scale
        acc_ref[0, 2:3, :] += a_gate

    row = lambda i: (i, 0)
    vec = pl.BlockSpec((1, d), lambda i: (0, 0))
    return pl.pallas_call(
        body, name="mlp_up_bwd", grid=(t // tm,),
        in_specs=[pl.BlockSpec((tm, D_FF), row), _RESIDENT, pl.BlockSpec((tm, d), row),
                  pl.BlockSpec((tm, d), row), pl.BlockSpec((tm, d), row),
                  pl.BlockSpec((1, N_MOD, d), lambda i: (i // tps, 0, 0)), vec, _token_spec()],
        out_specs=[pl.BlockSpec((tm, d), row), pl.BlockSpec((tm, d), row),
                   pl.BlockSpec((1, 8, d), lambda i: (i // tps, 0, 0)), vec],
        out_shape=[_out((t, d), F32), _out((t, d), BF),
                   _out((nb, 8, d), F32), _out((1, d), F32)],
        compiler_params=_cp(("arbitrary",), 44))(*_pin(d_a), w_up4, *_pin(h1, dh2, o, mod3, g_mlp, token))


def _out_proj_bwd(d_o, w_out, token):
    t, d = d_o.shape
    tm = min(1024, t)

    def body(g_ref, w_ref, tok_ref, dp_ref, ds_ref):
        gv = g_ref[...]
        dp_ref[...] = _nt(gv, w_ref[0:POOL_WIDTH, :])
        ds_ref[...] = _nt(gv, w_ref[POOL_WIDTH:, :])

    row = lambda i: (i, 0)
    return pl.pallas_call(
        body, name="out_proj_bwd", grid=(t // tm,),
        in_specs=[pl.BlockSpec((tm, d), row), _RESIDENT, _token_spec()],
        out_specs=[pl.BlockSpec((tm, POOL_WIDTH), row), pl.BlockSpec((tm, SSD_INNER), row)],
        out_shape=[_out((t, POOL_WIDTH), F32), _out((t, SSD_INNER), F32)],
        compiler_params=_cp(("parallel",), 32))(*_pin(d_o), w_out, *_pin(token))


def _pool_bwd(d_ypool, p, w_pool, pool_scale, nb, seq):
    ts = _pool_tile(seq)
    nt = seq // ts
    hb = ts // HALO
    last_block = nb * seq // HALO - 1

    def body(dy_ref, halo_ref, p_ref, wp_ref, ps_ref, du_ref, gw_ref, gs_ref):
        b = pl.program_id(0)
        i = pl.program_id(1)

        @pl.when((b == 0) & (i == 0))
        def _():
            gw_ref[...] = jnp.zeros_like(gw_ref)
            gs_ref[...] = jnp.zeros_like(gs_ref)

        halo = jnp.where(i == nt - 1, 0.0, halo_ref[...])
        dy = dy_ref[...]
        ext = jnp.concatenate([dy, halo], 0)
        tpos = i * ts + _iota((ts + HALO, 1), 0)
        n_ext = ts + HALO
        for g, w in enumerate(POOL_WINDOWS):
            gs = slice(g * POOL_GROUP, (g + 1) * POOL_GROUP)
            wg = wp_ref[g].astype(BF)
            pg = p_ref[:, gs]
            pw = _nn(pg, wg)
            gs_ref[:, gs] += jnp.sum(dy[:, gs] * pw, 0, keepdims=True)
            dpw = (ext[:, gs] * ps_ref[:, gs]).astype(BF)
            gw_ref[g] += _tn(pg, dpw[:ts])
            dp = _nt(dpw, wg)
            cnt = jnp.minimum(tpos + 1, w).astype(F32)
            s = dp / cnt
            sh = 1
            while sh < w:
                s = s + pltpu.roll(s, n_ext - sh, 0)
                sh *= 2
            du_ref[:, gs] = (s[:ts] - dp[:ts]).astype(BF)

    return pl.pallas_call(
        body, name="pool_bwd", grid=(nb, nt),
        in_specs=[pl.BlockSpec((ts, POOL_WIDTH), lambda b, i: (b * nt + i, 0)),
                  pl.BlockSpec((HALO, POOL_WIDTH), lambda b, i: (jnp.minimum((b * nt + i + 1) * hb, last_block), 0)),
                  pl.BlockSpec((ts, POOL_WIDTH), lambda b, i: (b * nt + i, 0)),
                  pl.BlockSpec((4, POOL_GROUP, POOL_GROUP), lambda b, i: (0, 0, 0)),
                  pl.BlockSpec((1, POOL_WIDTH), lambda b, i: (0, 0))],
        out_specs=[pl.BlockSpec((ts, POOL_WIDTH), lambda b, i: (b * nt + i, 0)),
                   pl.BlockSpec((4, POOL_GROUP, POOL_GROUP), lambda b, i: (0, 0, 0)),
                   pl.BlockSpec((1, POOL_WIDTH), lambda b, i: (0, 0))],
        out_shape=[_out((nb * seq, POOL_WIDTH), BF), _out((4, POOL_GROUP, POOL_GROUP), F32),
                   _out((1, POOL_WIDTH), F32)],
        compiler_params=_cp(("arbitrary", "arbitrary"), 32))(*_pin(d_ypool, d_ypool, p, w_pool, pool_scale))


def _ssd_bwd(proj, pre, d_yssd, yssm, h_prev, dt_bias, a_log, dskip_e, g_ssd, nb, seq):
    specs, row, cidx, ns = _ssd_specs(nb, seq, reverse=True)
    specs = specs[2:]

    def body(z0_ref, z1_ref, udt_ref, pre_ref, dys_ref, yssm_ref, hprev_ref,
             dtb_ref, alog_ref, dsk_ref, gs_ref,
             dz_ref, dpre_ref, dudt_ref, ggs_ref, gdsk_ref, ga_ref, gdtb_ref,
             g_ref, dxdt_ref, dyv_ref):
        b = pl.program_id(0)
        c = pl.program_id(1)

        @pl.when(c == 0)
        def _():
            g_ref[...] = jnp.zeros_like(g_ref)

        @pl.when((b == 0) & (c == 0))
        def _():
            ggs_ref[...] = jnp.zeros_like(ggs_ref)
            gdsk_ref[...] = jnp.zeros_like(gdsk_ref)
            ga_ref[...] = jnp.zeros_like(ga_ref)
            gdtb_ref[...] = jnp.zeros_like(gdtb_ref)

        for sub in reversed(range(SSD_SUB)):
            chunk(sub, z0_ref, z1_ref, udt_ref, pre_ref, dys_ref, yssm_ref, hprev_ref, dtb_ref, alog_ref, dsk_ref, gs_ref,
                  dz_ref, dpre_ref, dudt_ref, ggs_ref, gdsk_ref, ga_ref, gdtb_ref, g_ref, dxdt_ref.at[sub], dyv_ref.at[sub])

    def chunk(sub, z0_ref, z1_ref, udt_ref, pre_ref, dys_ref, yssm_ref, hprev_ref,
              dtb_ref, alog_ref, dsk_ref, gs_ref,
              dz_ref, dpre_ref, dudt_ref, ggs_ref, gdsk_ref, ga_ref, gdtb_ref,
              g_ref, dxdt_ref, dyv_ref):
        rows = slice(sub * CHUNK, (sub + 1) * CHUNK)
        r = _chunk_terms(pre_ref[rows, :], udt_ref[rows, :], dtb_ref[...], alog_ref[...])
        xbc = r["xbc"]
        xs = xbc[:, :SSD_INNER]
        dt_e = r["dt_e"]
        xdt = xs * dt_e
        xdt_b = xdt.astype(BF)
        reduce_m = _head_reduce_matrix(GROUP_W, 8)

        def head_sums(v):
            return _nn(v.astype(BF), reduce_m)

        onehot16 = lambda h: (_iota((1, SSD_HEADS), 1) == h).astype(F32)
        onecol16 = lambda h: (_iota((SSD_HEADS, 1), 0) == h).astype(F32)

        d_acum = jnp.zeros((CHUNK, SSD_HEADS), F32)
        d_acum_t = jnp.zeros((SSD_HEADS, CHUNK), F32)
        d_alast = jnp.zeros((1, SSD_HEADS), F32)
        place8 = lambda g: (_iota((8, SSD_HEADS), 1) == _iota((8, SSD_HEADS), 0) + 8 * g).astype(BF)
        d_b, d_c = [], []
        for g in range(2):
            gs = slice(g * GROUP_W, (g + 1) * GROUP_W)
            zg = (z0_ref if g == 0 else z1_ref)[rows, :]
            sz = _sigmoid(zg)
            silu_z = zg * sz
            ys = yssm_ref[rows, gs]
            yg = ys * silu_z
            rg = lax.rsqrt(jnp.mean(yg * yg, -1, keepdims=True) + EPS)
            yh = yg * rg
            dys = dys_ref[rows, gs]
            ggs_ref[:, gs] += jnp.sum(dys * yh, 0, keepdims=True)
            dyh = dys * gs_ref[:, gs]
            dyg = rg * (dyh - yh * jnp.mean(dyh * yh, -1, keepdims=True))
            dy = dyg * silu_z
            dz_ref[rows, gs] = (dyg * ys * (sz * (1.0 + zg * (1.0 - sz)))).astype(BF)
            gdsk_ref[:, gs] += jnp.sum(dy * xs[:, gs], 0, keepdims=True)
            dyv_ref[:, gs] = dy
            dy_b = dy.astype(BF)

            bg = xbc[:, SSD_INNER + g * SSD_STATE:SSD_INNER + (g + 1) * SSD_STATE].astype(BF)
            cg = xbc[:, SSD_INNER + (2 + g) * SSD_STATE:SSD_INNER + (3 + g) * SSD_STATE].astype(BF)
            scores = _nt(cg, bg)
            hg = hprev_ref[0, sub, g]
            hg_b = hg.astype(BF)
            gg = g_ref[g]
            gg_b = gg.astype(BF)
            e_a = r["e_a"][:, gs]
            d_out = r["d_out"][:, gs]
            c_dec = r["c_dec"][:, gs]
            zc = _nn(cg, hg_b)
            wv = e_a * dy
            wv_b = wv.astype(BF)
            da_g = head_sums(wv * zc)
            dcg = _nt(wv_b, hg_b)
            d_hprev = _tn(cg, wv_b)
            vg = _nn(bg, gg_b)
            dxdt_g = d_out * vg
            dd_out = head_sums(xdt[:, gs] * vg)
            dbg = _nt((xdt[:, gs] * d_out).astype(BF), gg_b)
            dcd = _exact_nn(jnp.sum(gg * hg, 0, keepdims=True), reduce_m)
            d_out8 = jnp.exp(r["acum"][CHUNK - 1:CHUNK, 8 * g:8 * g + 8] - r["acum"][:, 8 * g:8 * g + 8])
            c_dec8 = jnp.exp(r["acum"][CHUNK - 1:CHUNK, 8 * g:8 * g + 8])
            t8 = dd_out * d_out8
            d_alast = d_alast + _exact_nn(jnp.sum(t8, 0, keepdims=True) + dcd * c_dec8, place8(g))
            d_acum = d_acum + _exact_nn(da_g - t8, place8(g))
            dsc = jnp.zeros((CHUNK, CHUNK), F32)
            for hh in range(8):
                h = g * 8 + hh
                hs = slice(h * SSD_HEAD_DIM, (h + 1) * SSD_HEAD_DIM)
                lam = _head_decay(r, h)
                m = scores * lam
                dyh_b = dy_b[:, hh * SSD_HEAD_DIM:(hh + 1) * SSD_HEAD_DIM]
                dm = _nt(dyh_b, xdt_b[:, hs])
                tm_ = dm * m
                d_acum = d_acum + jnp.sum(tm_, 1, keepdims=True) * onehot16(h)
                d_acum_t = d_acum_t + onecol16(h) * jnp.sum(tm_, 0, keepdims=True)
                dsc = dsc + dm * lam
                dxdt_ref[:, hs] = _tn(m.astype(BF), dyh_b) + dxdt_g[:, hh * SSD_HEAD_DIM:(hh + 1) * SSD_HEAD_DIM]
            dsc_b = dsc.astype(BF)
            d_c.append(dcg + _nn(dsc_b, bg))
            d_b.append(dbg + _tn(dsc_b, cg))
            g_ref[g] = d_hprev + c_dec * gg

        eye = (_iota((CHUNK, CHUNK), 0) == _iota((CHUNK, CHUNK), 1)).astype(BF)
        d_acum = d_acum - _exact_nt_left(eye, d_acum_t)
        is_last = (_iota((CHUNK, 1), 0) == CHUNK - 1).astype(F32)
        d_acum = d_acum + is_last * d_alast
        triu = (_iota((CHUNK, CHUNK), 0) <= _iota((CHUNK, CHUNK), 1)).astype(BF)
        d_da = _exact_nn_left(triu, d_acum)
        dt = r["dt"]
        ga_ref[...] += jnp.sum(d_da * dt, 0, keepdims=True)
        dxdt = dxdt_ref[...]
        reduce16 = _head_reduce_matrix(SSD_INNER, SSD_HEADS)
        d_dt = d_da * r["a"] + _nn((dxdt * xs).astype(BF), reduce16)
        d_udt = d_dt * _sigmoid(r["dtp"])
        gdtb_ref[...] += jnp.sum(d_udt, 0, keepdims=True)
        dudt_ref[rows, :] = jnp.zeros((CHUNK, dudt_ref.shape[1]), BF)
        dudt_ref[rows, 0:SSD_HEADS] = d_udt.astype(BF)
        pre, sg = r["pre"], r["sg"]
        dsilu = sg * (1.0 + pre * (1.0 - sg))
        dpre_ref[rows, 0:SSD_INNER] = (dsk_ref[...] * dyv_ref[...] + dxdt * dt_e) * dsilu[:, 0:SSD_INNER]
        for g in range(2):
            bs = slice(SSD_INNER + g * SSD_STATE, SSD_INNER + (g + 1) * SSD_STATE)
            cs = slice(SSD_INNER + (2 + g) * SSD_STATE, SSD_INNER + (3 + g) * SSD_STATE)
            dpre_ref[rows, bs] = d_b[g] * dsilu[:, bs]
            dpre_ref[rows, cs] = d_c[g] * dsilu[:, cs]

    t = nb * seq
    vec = _const_spec((1, SSD_INNER))
    small = _const_spec((1, SSD_HEADS))
    return pl.pallas_call(
        body, name="ssd_bwd", grid=(nb, ns),
        in_specs=specs + [pl.BlockSpec((SSD_ROWS, CONV_CH), lambda b, c: (row(b, c), 0)),
                          pl.BlockSpec((SSD_ROWS, SSD_INNER), lambda b, c: (row(b, c), 0)),
                          pl.BlockSpec((SSD_ROWS, SSD_INNER), lambda b, c: (row(b, c), 0)),
                          pl.BlockSpec((1, SSD_SUB, 2, SSD_STATE, GROUP_W), lambda b, c: (b, cidx(c), 0, 0, 0)),
                          small, small, vec, vec],
        out_specs=[pl.BlockSpec((SSD_ROWS, SSD_INNER), lambda b, c: (row(b, c), 0)),
                   pl.BlockSpec((SSD_ROWS, CONV_CH), lambda b, c: (row(b, c), 0)),
                   pl.BlockSpec((SSD_ROWS, 128), lambda b, c: (row(b, c), 0)),
                   vec, vec, small, small],
        out_shape=[_out((t, SSD_INNER), BF), _out((t, CONV_CH), F32),
                   _out((t, 128), BF), _out((1, SSD_INNER), F32),
                   _out((1, SSD_INNER), F32), _out((1, SSD_HEADS), F32),
                   _out((1, SSD_HEADS), F32)],
        scratch_shapes=[pltpu.VMEM((2, SSD_STATE, GROUP_W), F32), pltpu.VMEM((SSD_SUB, CHUNK, SSD_INNER), F32),
                        pltpu.VMEM((SSD_SUB, CHUNK, SSD_INNER), F32)],
        compiler_params=_cp(("arbitrary", "arbitrary"), 56),
    )(*_pin(proj, proj, proj, pre, d_yssd, yssm, h_prev, dt_bias, a_log, dskip_e, g_ssd))


def _grad_w_out(y_pool, y_ssd, d_o):
    t, d = d_o.shape
    tk = POOL_WIDTH
    n_s = SSD_INNER // tk

    def body(p_ref, s_ref, g_ref, o_ref):
        i = pl.program_id(0)

        @pl.when(i == 0)
        def _():
            o_ref[...] = _tn(p_ref[...], g_ref[...]).astype(BF)

        @pl.when(i > 0)
        def _():
            o_ref[...] = _tn(s_ref[...], g_ref[...]).astype(BF)

    return pl.pallas_call(
        body, name="grad_w_out", grid=(1 + n_s,),
        in_specs=[pl.BlockSpec((t, tk), lambda i: (0, 0)), pl.BlockSpec((t, tk), lambda i: (0, jnp.maximum(i - 1, 0))),
                  pl.BlockSpec((t, d), lambda i: (0, 0))],
        out_specs=pl.BlockSpec((tk, d), lambda i: (i, 0)),
        out_shape=_out((POOL_WIDTH + SSD_INNER, d), BF),
        compiler_params=_cp(("parallel",), 56))(*_pin(y_pool, y_ssd, d_o))


def _grad_w_in_t(d_upool, d_z, d_uxbc, d_udt, u1):
    t, d = u1.shape
    tk = 512
    n_z, n_x = SSD_INNER // tk, CONV_CH // tk

    def body(p_ref, z_ref, x_ref, dt_ref, u_ref, o_ref):
        i = pl.program_id(0)

        @pl.when(i == 0)
        def _():
            o_ref[...] = _tn(p_ref[...], u_ref[...]).astype(BF)

        @pl.when((i >= 1) & (i < 1 + n_z))
        def _():
            o_ref[...] = _tn(z_ref[...], u_ref[...]).astype(BF)

        @pl.when((i >= 1 + n_z) & (i < 1 + n_z + n_x))
        def _():
            o_ref[...] = _tn(x_ref[...], u_ref[...]).astype(BF)

        @pl.when(i == 1 + n_z + n_x)
        def _():
            o_ref[0:128, :] = _tn(dt_ref[...], u_ref[...]).astype(BF)

    return pl.pallas_call(
        body, name="grad_w_in", grid=(2 + n_z + n_x,),
        in_specs=[pl.BlockSpec((t, tk), lambda i: (0, 0)),
                  pl.BlockSpec((t, tk), lambda i: (0, jnp.clip(i - 1, 0, n_z - 1))),
                  pl.BlockSpec((t, tk), lambda i: (0, jnp.clip(i - 1 - n_z, 0, n_x - 1))),
                  pl.BlockSpec((t, 128), lambda i: (0, 0)), pl.BlockSpec((t, d), lambda i: (0, 0))],
        out_specs=pl.BlockSpec((tk, d), lambda i: (i, 0)),
        out_shape=_out((IN_PAD, d), BF),
        compiler_params=_cp(("parallel",), 56))(*_pin(d_upool, d_z, d_uxbc, d_udt, u1))


def _conv_bwd(d_pre, proj, conv_w, nb, seq):
    ts = min(256, seq)
    nt = seq // ts
    hb = ts // CONV_HALO
    last_block = nb * seq // CONV_HALO - 1
    n_ext = CHUNK + CONV_HALO

    def body(dp_ref, dnext_ref, u_ref, cw_ref, du_ref, gw_ref, gb_ref):
        b = pl.program_id(0)
        i = pl.program_id(1)

        @pl.when((b == 0) & (i == 0))
        def _():
            gw_ref[...] = jnp.zeros_like(gw_ref)
            gb_ref[...] = jnp.zeros_like(gb_ref)

        for c0 in range(0, CONV_CH, 128):
            cs = slice(c0, c0 + 128)
            cw = cw_ref[:, cs]
            gw = [0.0] * 4
            gb = 0.0
            for r0 in range(0, ts, CHUNK):
                dp = dp_ref[r0:r0 + CHUNK, cs]
                u = u_ref[r0:r0 + CHUNK, cs]
                if r0 + CHUNK < ts:
                    below = dp_ref[r0 + CHUNK:r0 + CHUNK + CONV_HALO, cs]
                else:
                    below = jnp.where(i == nt - 1, 0.0, dnext_ref[:, cs])
                ext_d = jnp.concatenate([dp, below], 0)
                du = dp * cw[3:4]
                gw[3] = gw[3] + jnp.sum(dp * u, 0, keepdims=True)
                for k in (2, 1, 0):
                    shifted = pltpu.roll(ext_d, n_ext - (3 - k), 0)[:CHUNK]
                    du = du + shifted * cw[k:k + 1]
                    gw[k] = gw[k] + jnp.sum(shifted * u, 0, keepdims=True)
                gb = gb + jnp.sum(dp, 0, keepdims=True)
                du_ref[r0:r0 + CHUNK, cs] = du.astype(BF)
            for k in range(4):
                gw_ref[k:k + 1, cs] += gw[k]
            gb_ref[:, cs] += gb

    return pl.pallas_call(
        body, name="conv_bwd", grid=(nb, nt),
        in_specs=[pl.BlockSpec((ts, CONV_CH), lambda b, i: (b * nt + i, 0)),
                  pl.BlockSpec((CONV_HALO, CONV_CH), lambda b, i: (jnp.minimum((b * nt + i + 1) * hb, last_block), 0)),
                  pl.BlockSpec((ts, CONV_CH), lambda b, i: (b * nt + i, 1)),
                  pl.BlockSpec((4, CONV_CH), lambda b, i: (0, 0))],
        out_specs=[pl.BlockSpec((ts, CONV_CH), lambda b, i: (b * nt + i, 0)),
                   pl.BlockSpec((8, CONV_CH), lambda b, i: (0, 0)), pl.BlockSpec((1, CONV_CH), lambda b, i: (0, 0))],
        out_shape=[_out((nb * seq, CONV_CH), BF), _out((8, CONV_CH), F32),
                   _out((1, CONV_CH), F32)],
        compiler_params=_cp(("arbitrary", "arbitrary"), 48))(*_pin(d_pre, d_pre, proj, conv_w))


def _in_proj_bwd(d_parts, w_in_t, x, dh1, mod3, g_mix, seq, token):
    t, d = x.shape
    nb = t // seq
    tm = min(ROW_TILE, seq)
    tps = seq // tm

    widths = [p.shape[1] for p in d_parts]

    def body(d0_ref, d1_ref, d2_ref, d3_ref, w_ref, x_ref, dh1_ref, mod_ref, g_ref, tok_ref, gx_ref, acc_ref, gg_ref):
        i = pl.program_id(0)

        @pl.when(i == 0)
        def _():
            gg_ref[...] = jnp.zeros_like(gg_ref)

        @pl.when(i % tps == 0)
        def _():
            acc_ref[...] = jnp.zeros_like(acc_ref)

        gg = a_shift = a_scale = 0.0
        for rows in _sub_rows(tm):
            d_cat = jnp.concatenate([p_ref[rows, :] for p_ref in (d0_ref, d1_ref, d2_ref)], 1)
            du = _nn(d_cat, w_ref[0:OFF_DT, :]) + _nn(d3_ref[rows, 0:IN_WIDTH - OFF_DT], w_ref[OFF_DT:IN_WIDTH, :])
            xv = x_ref[rows, :]
            r = lax.rsqrt(jnp.mean(xv * xv, -1, keepdims=True) + EPS)
            hh = xv * r
            n1 = hh * g_ref[...]
            dn1 = du * (1.0 + mod_ref[0, 1:2, :])
            dhat = dn1 * g_ref[...]
            gx_ref[rows, :] = dh1_ref[rows, :] + r * (dhat - hh * jnp.mean(dhat * hh, -1, keepdims=True))
            gg = gg + jnp.sum(dn1 * hh, 0, keepdims=True)
            a_shift = a_shift + jnp.sum(du, 0, keepdims=True)
            a_scale = a_scale + jnp.sum(du * n1, 0, keepdims=True)
        gg_ref[...] += gg
        acc_ref[0, 0:1, :] += a_shift
        acc_ref[0, 1:2, :] += a_scale

    row = lambda i: (i, 0)
    vec = pl.BlockSpec((1, d), lambda i: (0, 0))
    return pl.pallas_call(
        body, name="in_proj_bwd", grid=(t // tm,),
        in_specs=[pl.BlockSpec((tm, wd), row) for wd in widths] +
                 [_RESIDENT, pl.BlockSpec((tm, d), row),
                  pl.BlockSpec((tm, d), row), pl.BlockSpec((1, N_MOD, d), lambda i: (i // tps, 0, 0)), vec, _token_spec()],
        out_specs=[pl.BlockSpec((tm, d), row), pl.BlockSpec((1, 8, d), lambda i: (i // tps, 0, 0)), vec],
        out_shape=[_out((t, d), F32), _out((nb, 8, d), F32),
                   _out((1, d), F32)],
        compiler_params=_cp(("arbitrary",), 40))(*_pin(*d_parts), w_in_t, *_pin(x, dh1, mod3, g_mix, token))


_VEC_LAYOUT = (("g_mix", 1024), ("conv_b", 1536), ("g_ssd", 1024), ("pool_scale", 512), ("g_mlp", 1024),
               ("g_final", 1024), ("dt_bias", 128), ("a_log", 128), ("d_skip_lanes", 1024), ("sq_err", 1024))
_VEC_OFFSET = {}
_off = 0
for _name, _n in _VEC_LAYOUT:
    _VEC_OFFSET[_name] = _off
    _off += _n
_SMALL_PARAMS = ("b_ada", "g_mix", "conv_w", "conv_b", "dt_bias", "a_log", "d_skip", "g_ssd", "w_pool", "pool_scale",
                 "g_mlp", "g_final")


def _pack_vec(parts):
    cols = []
    for name, n in _VEC_LAYOUT:
        v = parts[name]
        if v.shape[1] < n:
            v = jnp.pad(v, ((0, 0), (0, n - v.shape[1])))
        cols.append(v)
    return jnp.concatenate(cols, 1)


def _small_adam(vec_all, wpool_all, convw_all, dmod_all, params):
    names = _SMALL_PARAMS
    nin = 4 + 3 * len(names)

    def body(*refs):
        vec_ref, wp_ref, cw_ref, dm_ref = refs[:4]
        prm = {n: refs[4 + 3 * i:7 + 3 * i] for i, n in enumerate(names)}
        loss_ref = refs[nin]
        outs = {n: refs[nin + 1 + 4 * i:nin + 5 + 4 * i] for i, n in enumerate(names)}
        vsum = vec_ref[0]
        for s in range(1, N_DEV):
            vsum = vsum + vec_ref[s]

        def lanes(name, n):
            off = _VEC_OFFSET[name]
            return vsum[:, off:off + n]

        grads = {n: lanes(n, prm[n][0].shape[1]) for n in ("g_mix", "conv_b", "g_ssd", "pool_scale", "g_mlp", "g_final", "dt_bias")}
        grads["a_log"] = lanes("a_log", SSD_HEADS) * (-jnp.exp(prm["a_log"][0][...]))
        per_lane = jnp.broadcast_to(lanes("d_skip_lanes", SSD_INNER), (8, SSD_INNER))
        grads["d_skip"] = _exact_nn(per_lane, _head_reduce_matrix(SSD_INNER, SSD_HEADS))[0:1]
        gwp = wp_ref[0].astype(F32)
        gcw = cw_ref[0]
        gb = jnp.sum(dm_ref[0], 0, keepdims=True)
        for s in range(1, N_DEV):
            gwp = gwp + wp_ref[s].astype(F32)
            gcw = gcw + cw_ref[s]
            gb = gb + jnp.sum(dm_ref[s], 0, keepdims=True)
        grads["w_pool"] = gwp
        grads["conv_w"] = gcw[0:4]
        grads["b_ada"] = gb
        total = jnp.sum(lanes("sq_err", D_MODEL), 1, keepdims=True) * (0.5 / D_MODEL)
        loss_ref[...] = jnp.broadcast_to(total, loss_ref.shape)
        for n in names:
            w_ref, m_ref, v_ref = prm[n]
            g = grads[n]
            d, m2, v2 = _adam_math(w_ref[...], g, m_ref[...], v_ref[...])
            g_ref, d_ref, m2_ref, v2_ref = outs[n]
            g_ref[...] = g
            d_ref[...] = d
            m2_ref[...] = m2
            v2_ref[...] = v2

    flat = [vec_all, wpool_all, convw_all, dmod_all]
    out_shape = [jax.ShapeDtypeStruct((1, 128), F32)]
    for n in names:
        flat += list(params[n])
        out_shape += [jax.ShapeDtypeStruct(params[n][0].shape, F32)] * 4
    vm = pl.BlockSpec(memory_space=pltpu.VMEM)
    res = pl.pallas_call(body, name="small_adam", out_shape=out_shape, in_specs=[vm] * len(flat),
                         out_specs=[vm] * len(out_shape), compiler_params=_cp(vmem_mb=48))(*flat)
    return res[0], {n: res[1 + 4 * i:5 + 4 * i] for i, n in enumerate(names)}


_WEIGHTS = ("w_ada", "b_ada", "g_mix", "w_in", "conv_w", "conv_b", "dt_bias", "a_log", "d_skip", "g_ssd", "w_pool",
            "pool_scale", "w_out", "g_mlp", "w_up", "w_down", "g_final")


def _local_step(x2, tg2, mod3, seq, w_in_t, first_token, weights_arrived, weights_later, start_reduce, before_last,
                conv_w_full, sp):
    t, d = x2.shape
    nb = t // seq
    dskip_e = jnp.repeat(sp["d_skip"], SSD_HEAD_DIM, axis=1)
    proj, u1 = _in_proj(x2, mod3, sp["g_mix"], w_in_t, seq, first_token)
    y_ssd, yssm, h_prev, pre = _ssd_fwd(proj, conv_w_full, sp["conv_b"], sp["dt_bias"], sp["a_log"], dskip_e, sp["g_ssd"], nb, seq)
    y_pool, p = _pool_fwd(proj, sp["w_pool"], sp["pool_scale"], nb, seq, weights_arrived(y_ssd))
    w_out_f, w_up4, w_down4 = weights_later(y_pool)
    w_down_f = w_down4.reshape(D_FF, d)
    h1, o, u2 = _out_proj(y_pool, y_ssd, w_out_f, x2, mod3, sp["g_mlp"], seq)
    a_up = _mlp_up(u2, w_up4)
    d_dn, dh2, sq, gg_final, d_gf = _mlp_down_loss(a_up, w_down_f, h1, mod3, sp["g_final"], tg2, seq)

    gw_down = _tn_matmul(a_up, d_dn, 512, d, "grad_w_down", square_relu=True)
    tok = start_reduce("w_down", gw_down.reshape(N_CHIPS, D_FF // N_CHIPS, d))
    d_a = _mlp_down_bwd(d_dn, w_down4, a_up, tok)
    gw_up4 = _tn_matmul(u2, d_a, 512, d, "grad_w_up", out3=True)
    tok = start_reduce("w_up", gw_up4)
    dh1, d_o, accf, gg_mlp = _mlp_up_bwd(d_a, w_up4, h1, dh2, o, mod3, sp["g_mlp"], seq, tok)
    gw_out = _grad_w_out(y_pool, y_ssd, d_o)
    tok = start_reduce("w_out", gw_out.reshape(N_CHIPS, gw_out.shape[0] // N_CHIPS, d))
    d_ypool, d_yssd = _out_proj_bwd(d_o, w_out_f, tok)
    d_upool, gw_pool, g_ps = _pool_bwd(d_ypool, p, sp["w_pool"], sp["pool_scale"], nb, seq)
    d_z, d_pre, d_udt, gg_ssd, gdsk, ga, gdtb = _ssd_bwd(proj, pre, d_yssd, yssm, h_prev, sp["dt_bias"], sp["a_log"],
                                                        dskip_e, sp["g_ssd"], nb, seq)
    d_uxbc, gconvw, gconvb = _conv_bwd(d_pre, proj, conv_w_full, nb, seq)
    gw_in_t = _grad_w_in_t(d_upool, d_z, d_uxbc, d_udt, u1)
    shard_rows = IN_WIDTH // N_CHIPS
    tok = start_reduce("w_in", jnp.stack([gw_in_t[k * shard_rows:(k + 1) * shard_rows] for k in range(N_CHIPS)]))
    gx, accm, gg_mix = _in_proj_bwd([d_upool, d_z, d_uxbc, d_udt], w_in_t, x2, dh1, mod3, sp["g_mix"], seq, before_last(tok))

    d_mod = jnp.concatenate([accm[:, 0], accm[:, 1], accf[:, 2], accf[:, 0], accf[:, 1], d_gf[:, 0]], 1)
    vec = _pack_vec({"g_mix": gg_mix, "conv_b": gconvb, "g_ssd": gg_ssd, "pool_scale": g_ps, "g_mlp": gg_mlp,
                     "g_final": gg_final, "dt_bias": gdtb, "a_log": ga, "d_skip_lanes": gdsk, "sq_err": sq})
    return gx, d_mod, vec, gw_pool, gconvw


def kernel(x, c, w_ada, b_ada, g_mix, w_in, conv_w, conv_b, dt_bias, a_log, d_skip, g_ssd, w_pool, pool_scale, w_out, g_mlp, w_up, w_down, g_final, loss_target, m_w_ada, m_b_ada, m_g_mix, m_w_in, m_conv_w, m_conv_b, m_dt_bias, m_a_log, m_d_skip, m_g_ssd, m_w_pool, m_pool_scale, m_w_out, m_g_mlp, m_w_up, m_w_down, m_g_final, v_w_ada, v_b_ada, v_g_mix, v_w_in, v_conv_w, v_conv_b, v_dt_bias, v_a_log, v_d_skip, v_g_ssd, v_w_pool, v_pool_scale, v_w_out, v_g_mlp, v_w_up, v_w_down, v_g_final):
    nb, seq, d = x.shape
    t = nb * seq
    xi, yi, ci = _mesh_pos()
    chip = 2 * xi + yi
    me = 4 * xi + 2 * yi + ci
    ada_cols = w_ada.shape[2]
    conv_cols = conv_w.shape[2]
    in_cols = w_in.shape[2]
    w_in_s, m_w_in_s, v_w_in_s = w_in[0].T, m_w_in[0].T, v_w_in[0].T

    c_send, c_recv, c_src, c_land, c_token = _exchange_start([c, conv_w[0]], ALL_PEERS, "cond_start")
    w_in_b = w_in_s.astype(BF)
    i_send, i_recv, i_src, i_land, in_token = _ici_start(
        [w_in_b], [jax.ShapeDtypeStruct((N_CHIPS,) + w_in_b.shape, BF)], _gather_sent, _gather_landing, "gather_start_w_in",
        after=c_token)
    c_own, c_got = _exchange_wait(c_send, c_recv, c_src, c_land, ALL_PEERS, in_token, "cond_wait")
    c8, convw8 = [lax.dynamic_update_slice(got, mine[None], (me,) + (0,) * mine.ndim) for got, mine in zip(c_got, c_own)]
    c_all = c8.reshape(N_DEV * nb, d)
    conv_w_full = convw8[0::2].transpose(1, 0, 2).reshape(4, N_CHIPS * conv_cols)
    b_shard = lax.dynamic_slice(b_ada, (0, chip * ada_cols), (1, ada_cols))
    mod_part, c_act = _ada_mod(c_all, w_ada[0], b_shard, in_token)
    mod_rows = mod_part.reshape(N_DEV, nb, ada_cols)
    m_send, m_recv, m_src, m_land, _ = _ici_start(
        [mod_rows], [jax.ShapeDtypeStruct((N_CHIPS, nb, ada_cols), F32)], _mod_sent, _mod_landing, "mod_start", after=mod_part)

    later = [w_out[0].astype(BF), w_up[0].astype(BF), w_down[0].astype(BF)]
    in_shard, in_land = _ici_wait(i_send, i_recv, i_src, i_land, [m_src[0]] + later, _gather_sent, _gather_landing,
                                  "gather_wait_w_in")
    (w_in4,) = _gather_finish(in_land, in_shard)
    w_in_t = w_in4.reshape(N_CHIPS * in_cols, d)
    mod_mine, mod_land = _ici_wait(m_send, m_recv, m_src, m_land, w_in_t, _mod_sent, _mod_landing, "mod_wait")
    mod_own = lax.dynamic_slice(mod_mine[0], (me, 0, 0), (1, nb, ada_cols))
    mod4 = lax.dynamic_update_slice(mod_land[0], mod_own, (chip, 0, 0))
    mod3 = mod4.transpose(1, 0, 2).reshape(nb, N_MOD, d)
    g_send, g_recv, g_src, g_land, first_token = _ici_start(
        later, [jax.ShapeDtypeStruct((N_CHIPS,) + s.shape, BF) for s in later], _gather_sent, _gather_landing, "gather_start",
        after=w_in4)

    def weights_arrived(after):
        shards, lands = _ici_wait(g_send, g_recv, g_src, g_land, after, _gather_sent, _gather_landing, "gather_wait")
        pending["forward"] = _forward_start(lands, "forward_start") + (shards,)
        return pending["forward"][3]

    def weights_later(after):
        f_send, f_recv, f_land, _, shards = pending["forward"]
        lands = _forward_wait(f_send, f_recv, f_land, after, "forward_wait")
        w_out4, w_up4, w_down4 = [lax.dynamic_update_slice(land, shard[None], (chip, 0, 0)) for land, shard in zip(lands, shards)]
        return w_out4.reshape(N_CHIPS * w_out.shape[1], d), w_up4, w_down4

    pending = {}

    def start_reduce(name, grad4):
        pending[name] = _reduce_start(grad4, "reduce_start_" + name)
        return pending[name][4]

    pos = jnp.stack([ci, chip, me]).astype(jnp.int32)
    early = ("w_out", "w_up", "w_down")

    def summed_half(name, after):
        r_send, r_recv, r_src, r_land, _ = pending[name]
        own, recv = _reduce_wait(r_send, r_recv, r_src, r_land, after, "reduce_wait_" + name)
        return _sum_eight(recv, own, pos)

    def before_last(token):
        pending["early_halves"] = _exchange_start([summed_half(n, token) for n in early], SIBLING, "halves_start_early")
        return pending["early_halves"][4]

    sp = dict(g_mix=g_mix, conv_b=conv_b, dt_bias=dt_bias, a_log=a_log, d_skip=d_skip, g_ssd=g_ssd,
              w_pool=w_pool[0], pool_scale=pool_scale, g_mlp=g_mlp, g_final=g_final.reshape(1, d))
    gx, d_mod, vec, gw_pool, gconvw = _local_step(
        x.reshape(t, d), loss_target.reshape(t, d), mod3, seq, w_in_t, first_token, weights_arrived, weights_later, start_reduce,
        before_last, conv_w_full, sp)

    small_parts = [vec, gw_pool.reshape(4 * POOL_GROUP, POOL_GROUP).astype(BF), gconvw, d_mod]
    s_send, s_recv, s_src, s_land, s_token = _exchange_start(small_parts, ALL_PEERS, "small_start")
    h_send, h_recv, h_src, h_land, h_token = _exchange_start([summed_half("w_in", s_token)], SIBLING, "halves_start")
    e_send, e_recv, e_src, e_land, _ = pending["early_halves"]
    e_own, e_got = _exchange_wait(e_send, e_recv, e_src, e_land, SIBLING, h_token, "halves_wait_early")
    res = {}
    for i, (n, w, m, v) in enumerate((("w_out", w_out, m_w_out, v_w_out), ("w_up", w_up, m_w_up, v_w_up),
                                      ("w_down", w_down, m_w_down, v_w_down))):
        g, dl, m2, v2 = _adam_big(e_own[i], e_got[i], w[0], m[0], v[0], pos)
        res[n] = (g[None], dl[None], m2[None], v2[None])

    s_own, s_got = _exchange_wait(s_send, s_recv, s_src, s_land, ALL_PEERS, res["w_down"][1], "small_wait")
    vec8, wpool8, convw8g, dmod8 = [lax.dynamic_update_slice(got, mine[None], (me,) + (0,) * mine.ndim)
                                    for got, mine in zip(s_got, s_own)]
    convw8s = lax.dynamic_slice(convw8g, (0, 0, chip * conv_cols), (N_DEV, 8, conv_cols))
    m_in = dict(b_ada=m_b_ada, g_mix=m_g_mix, conv_w=m_conv_w[0], conv_b=m_conv_b, dt_bias=m_dt_bias, a_log=m_a_log,
                d_skip=m_d_skip, g_ssd=m_g_ssd, w_pool=m_w_pool.reshape(4 * POOL_GROUP, POOL_GROUP), pool_scale=m_pool_scale,
                g_mlp=m_g_mlp, g_final=m_g_final.reshape(1, d))
    v_in = dict(b_ada=v_b_ada, g_mix=v_g_mix, conv_w=v_conv_w[0], conv_b=v_conv_b, dt_bias=v_dt_bias, a_log=v_a_log,
                d_skip=v_d_skip, g_ssd=v_g_ssd, w_pool=v_w_pool.reshape(4 * POOL_GROUP, POOL_GROUP), pool_scale=v_pool_scale,
                g_mlp=v_g_mlp, g_final=v_g_final.reshape(1, d))
    w_small = dict(sp, b_ada=b_ada, conv_w=conv_w[0], w_pool=w_pool.reshape(4 * POOL_GROUP, POOL_GROUP))
    loss_row, small = _small_adam(vec8, wpool8, convw8s, dmod8, {n: (w_small[n], m_in[n], v_in[n]) for n in _SMALL_PARAMS})

    dmod_all = dmod8.reshape(N_DEV * nb, N_CHIPS * ada_cols)
    dmod_cols = lax.dynamic_slice(dmod_all, (0, chip * ada_cols), (N_DEV * nb, ada_cols))
    res.update({n: tuple(r.reshape(w.shape) for r in small[n])
                for n, w in (("b_ada", b_ada), ("g_mix", g_mix), ("conv_w", conv_w), ("conv_b", conv_b), ("dt_bias", dt_bias),
                             ("a_log", a_log), ("d_skip", d_skip), ("g_ssd", g_ssd), ("w_pool", w_pool),
                             ("pool_scale", pool_scale), ("g_mlp", g_mlp), ("g_final", g_final))})
    g_ada, d_ada, m_ada, v_ada = _adam_ada(c_act.T.astype(BF), dmod_cols, w_ada[0], m_w_ada[0], v_w_ada[0])
    res["w_ada"] = (g_ada[None], d_ada[None], m_ada[None], v_ada[None])
    h_own, h_got = _exchange_wait(h_send, h_recv, h_src, h_land, SIBLING, g_ada, "halves_wait")
    rows3 = lambda a: jnp.transpose(a, (2, 0, 1))
    res["w_in"] = tuple(jnp.transpose(r, (1, 2, 0))
                        for r in _adam_rows(h_own[0], h_got[0], rows3(w_in), rows3(m_w_in), rows3(v_w_in), pos))

    loss = loss_row[0, 0]
    return (loss, gx.reshape(nb, seq, d), *[res[n][0] for n in _WEIGHTS], *[res[n][1] for n in _WEIGHTS],
            *[res[n][2] for n in _WEIGHTS], *[res[n][3] for n in _WEIGHTS])
```

```python
import jax
import jax.numpy as jnp
from jax import lax
from jax.experimental import pallas as pl
from jax.experimental.pallas import tpu as pltpu

F32 = jnp.float32
BF = jnp.bfloat16
MESH = pl.DeviceIdType.MESH

EPS = 1e-5
D_MODEL = 1024
POOL_WIDTH = 512
POOL_WINDOWS = (2, 4, 8, 16)
POOL_GROUP = 128
SSD_INNER = 1024
SSD_HEADS = 16
SSD_HEAD_DIM = 64
SSD_STATE = 128
GROUP_W = 512
CHUNK = 128
CONV_CH = 1536
OFF_DT = 3072
IN_WIDTH = 3088
IN_PAD = 3200
D_FF = 4096
N_MOD = 6
N_CHIPS = 4
N_DEV = 8
HALO = 16
CONV_HALO = 8

ADAM_LR = 0.001
ADAM_B1 = 0.9
ADAM_B2 = 0.999
ADAM_EPS = 1e-08
ADAM_WD = 0.01
ADAM_STEP = 10

VMEM_BYTES_V7X = 64 * 1024 * 1024


def _cp(semantics=None, vmem_mb=48, **kw):
    assert vmem_mb * 1024 * 1024 < VMEM_BYTES_V7X
    args = dict(vmem_limit_bytes=vmem_mb * 1024 * 1024, **kw)
    if semantics is not None:
        args["dimension_semantics"] = semantics
    return pltpu.CompilerParams(**args)


def _out(shape, dtype):
    return pltpu.HBM(shape, dtype)


def _pin(*arrays):
    return [pltpu.with_memory_space_constraint(a, pltpu.HBM) for a in arrays]


TOKEN_SHAPE = (8, 128)


def _order_operand(a):
    if a.shape == TOKEN_SHAPE:
        return pl.BlockSpec(memory_space=pltpu.VMEM), a
    return pl.BlockSpec(memory_space=pltpu.HBM), pltpu.with_memory_space_constraint(a, pltpu.HBM)


def _nn(a, b):
    return jnp.dot(a, b, preferred_element_type=F32)


def _nt(a, b):
    return lax.dot_general(a, b, (((1,), (1,)), ((), ())), preferred_element_type=F32)


def _tn(a, b):
    return lax.dot_general(a, b, (((0,), (0,)), ((), ())), preferred_element_type=F32)


def _split3(v):
    hi = v.astype(BF)
    r1 = v - hi.astype(F32)
    mid = r1.astype(BF)
    lo = (r1 - mid.astype(F32)).astype(BF)
    return hi, mid, lo


def _exact_nn(v, m01):
    hi, mid, lo = _split3(v)
    return _nn(hi, m01) + _nn(mid, m01) + _nn(lo, m01)


def _exact_nn_left(m01, v):
    hi, mid, lo = _split3(v)
    return _nn(m01, hi) + _nn(m01, mid) + _nn(m01, lo)


def _exact_nt_left(m01, v):
    hi, mid, lo = _split3(v)
    return _nt(m01, hi) + _nt(m01, mid) + _nt(m01, lo)


def _sigmoid(v):
    return 1.0 / (1.0 + jnp.exp(-v))


def _iota(shape, dim):
    return lax.broadcasted_iota(jnp.int32, shape, dim)


def _head_expand_matrix(heads, width):
    return (_iota((heads, width), 1) // SSD_HEAD_DIM == _iota((heads, width), 0)).astype(BF)


def _head_reduce_matrix(width, heads):
    return (_iota((width, heads), 0) // SSD_HEAD_DIM == _iota((width, heads), 1)).astype(BF)


def _mesh_pos():
    return lax.axis_index("x"), lax.axis_index("y"), lax.axis_index("c")


def _flip(v, bit):
    return v + bit - 2 * bit * v


_HBM = pl.BlockSpec(memory_space=pltpu.HBM)
_SEM = pl.BlockSpec(memory_space=pltpu.SEMAPHORE)
_DATAFLOW = pltpu.SideEffectType.DATAFLOW_SIDE_EFFECTING


def _peer_chip(x, y, j):
    return _flip(x, (j >> 1) & 1), _flip(y, j & 1)


def _ici_start(srcs, land_shapes, sent, landing, name, after):
    n = len(srcs)

    def body(*refs):
        src_refs, land_refs = refs[:n], refs[n:2 * n]
        send_sems, recv_sems = refs[2 * n + 1], refs[2 * n + 2]
        token = refs[-1]
        x, y, c = _mesh_pos()
        for j in range(1, N_CHIPS):
            px, py = _peer_chip(x, y, j)
            for a in range(n):
                pltpu.make_async_remote_copy(
                    src_ref=sent(src_refs[a], c, 2 * px + py), dst_ref=landing(land_refs[a], c, 2 * x + y),
                    send_sem=send_sems.at[a * (N_CHIPS - 1) + j - 1], recv_sem=recv_sems.at[a * (N_CHIPS - 1) + j - 1],
                    device_id=(px, py, c), device_id_type=MESH).start()
        token[...] = jnp.zeros_like(token)

    sems = pltpu.SemaphoreType.DMA((n * (N_CHIPS - 1),))
    after_spec, after = _order_operand(after)
    lands = [pltpu.with_memory_space_constraint(lax.empty(s.shape, s.dtype), pltpu.HBM) for s in land_shapes]
    outs = pl.pallas_call(
        body, name=name,
        out_shape=(sems, sems, *[pltpu.HBM(s.shape, s.dtype) for s in srcs],
                   *[pltpu.HBM(s.shape, s.dtype) for s in land_shapes], jax.ShapeDtypeStruct((8, 128), F32)),
        in_specs=[_HBM] * (2 * n) + [after_spec],
        out_specs=[_SEM, _SEM] + [_HBM] * (2 * n) + [pl.BlockSpec(memory_space=pltpu.VMEM)],
        input_output_aliases={i: 2 + i for i in range(2 * n)},
        compiler_params=pltpu.CompilerParams(has_side_effects=_DATAFLOW),
    )(*_pin(*srcs), *lands, after)
    return outs[0], outs[1], outs[2:2 + n], outs[2 + n:2 + 2 * n], outs[-1]


def _ici_wait(send_sems, recv_sems, src_thru, land_thru, after, sent, landing, name):
    n = len(src_thru)
    afters = [_order_operand(a) for a in (after if isinstance(after, (list, tuple)) else [after])]

    def body(*refs):
        src_refs, land_refs = refs[:n], refs[n:2 * n]
        send_sems, recv_sems = refs[2 * n], refs[2 * n + 1]
        x, y, c = _mesh_pos()
        for j in range(1, N_CHIPS):
            px, py = _peer_chip(x, y, j)
            for a in range(n):
                cp = pltpu.make_async_remote_copy(
                    src_ref=sent(src_refs[a], c, 2 * px + py), dst_ref=landing(land_refs[a], c, 2 * px + py),
                    send_sem=send_sems.at[a * (N_CHIPS - 1) + j - 1], recv_sem=recv_sems.at[a * (N_CHIPS - 1) + j - 1],
                    device_id=(px, py, c), device_id_type=MESH)
                cp.wait_send()
                cp.wait_recv()

    outs = pl.pallas_call(
        body, name=name,
        out_shape=tuple(pltpu.HBM(s.shape, s.dtype) for s in (*src_thru, *land_thru)),
        in_specs=[_HBM] * (2 * n) + [_SEM, _SEM] + [s for s, _ in afters], out_specs=[_HBM] * (2 * n),
        input_output_aliases={i: i for i in range(2 * n)},
        compiler_params=pltpu.CompilerParams(has_side_effects=_DATAFLOW),
    )(*src_thru, *land_thru, send_sems, recv_sems, *[a for _, a in afters])
    return outs[:n], outs[n:]


def _col_half(ref, which, lead=()):
    hc = ref.shape[-1] // 2
    return ref.at[(*lead, slice(None), pl.ds(pl.multiple_of(which * hc, 128), hc))]


def _gather_sent(ref, c, dst_chip):
    return _col_half(ref, c)


def _gather_landing(ref, c, src_chip):
    return _col_half(ref, c, lead=(src_chip,))


def _mod_sent(ref, c, dst_chip):
    return ref.at[2 * dst_chip + c]


def _mod_landing(ref, c, src_chip):
    return ref.at[src_chip]


def _reduce_copy(src_ref, land_ref, send_sems, recv_sems, k, receiving):
    x, y, c = _mesh_pos()
    px, py, pc = _flip(x, (k >> 2) & 1), _flip(y, (k >> 1) & 1), _flip(c, k & 1)
    hc = src_ref.shape[2] // 2
    src = src_ref.at[2 * px + py, :, pl.ds(pl.multiple_of(pc * hc, 128), hc)]
    slot = (4 * px + 2 * py + pc) if receiving else (4 * x + 2 * y + c)
    return pltpu.make_async_remote_copy(
        src_ref=src, dst_ref=land_ref.at[slot], send_sem=send_sems.at[k - 1], recv_sem=recv_sems.at[k - 1],
        device_id=(px, py, pc), device_id_type=MESH)


def _reduce_start(grad4, name):
    k4, r, cols = grad4.shape

    def body(src_ref, land_ref, send_sems, recv_sems, src_thru, land_thru, token):
        for k in range(1, N_DEV):
            _reduce_copy(src_ref, land_ref, send_sems, recv_sems, k, receiving=False).start()
        token[...] = jnp.zeros_like(token)

    sems = pltpu.SemaphoreType.DMA((N_DEV - 1,))
    land = pltpu.with_memory_space_constraint(lax.empty((N_DEV, r, cols // 2), grad4.dtype), pltpu.HBM)
    return pl.pallas_call(
        body, name=name,
        out_shape=(sems, sems, pltpu.HBM(grad4.shape, grad4.dtype), pltpu.HBM(land.shape, land.dtype),
                   jax.ShapeDtypeStruct((8, 128), F32)),
        in_specs=[_HBM, _HBM], out_specs=[_SEM, _SEM, _HBM, _HBM, pl.BlockSpec(memory_space=pltpu.VMEM)],
        input_output_aliases={0: 2, 1: 3},
        compiler_params=pltpu.CompilerParams(has_side_effects=_DATAFLOW),
    )(*_pin(grad4), land)


def _reduce_wait(send_sems, recv_sems, src_thru, land_thru, after, name):
    def body(src_ref, land_ref, send_sems, recv_sems, after_ref, src_out, land_out):
        for k in range(1, N_DEV):
            cp = _reduce_copy(src_ref, land_ref, send_sems, recv_sems, k, receiving=True)
            cp.wait_send()
            cp.wait_recv()

    return pl.pallas_call(
        body, name=name,
        out_shape=(pltpu.HBM(src_thru.shape, src_thru.dtype), pltpu.HBM(land_thru.shape, land_thru.dtype)),
        in_specs=[_HBM, _HBM, _SEM, _SEM, _order_operand(after)[0]], out_specs=[_HBM, _HBM],
        input_output_aliases={0: 0, 1: 1},
        compiler_params=pltpu.CompilerParams(has_side_effects=_DATAFLOW),
    )(src_thru, land_thru, send_sems, recv_sems, _order_operand(after)[1])


def _peer_copy(src_ref, land_ref, send_sems, recv_sems, idx, k, receiving):
    x, y, c = _mesh_pos()
    px, py, pc = _flip(x, (k >> 2) & 1), _flip(y, (k >> 1) & 1), _flip(c, k & 1)
    if land_ref.shape[0] == N_DEV:
        slot = (4 * px + 2 * py + pc) if receiving else (4 * x + 2 * y + c)
    else:
        slot = pc if receiving else c
    return pltpu.make_async_remote_copy(
        src_ref=src_ref, dst_ref=land_ref.at[slot], send_sem=send_sems.at[idx], recv_sem=recv_sems.at[idx],
        device_id=(px, py, pc), device_id_type=MESH)


def _exchange_start(arrays, peers, name):
    n = len(arrays)

    def body(*refs):
        src_refs, land_refs = refs[:n], refs[n:2 * n]
        send_sems, recv_sems = refs[2 * n], refs[2 * n + 1]
        token = refs[-1]
        for j, k in enumerate(peers):
            for a in range(n):
                _peer_copy(src_refs[a], land_refs[a], send_sems, recv_sems, a * len(peers) + j, k, receiving=False).start()
        token[...] = jnp.zeros_like(token)

    sems = pltpu.SemaphoreType.DMA((n * len(peers),))
    n_slots = N_DEV if len(peers) > 1 else 2
    lands = [pltpu.with_memory_space_constraint(lax.empty((n_slots,) + a.shape, a.dtype), pltpu.HBM) for a in arrays]
    outs = pl.pallas_call(
        body, name=name,
        out_shape=(sems, sems, *[pltpu.HBM(a.shape, a.dtype) for a in arrays], *[pltpu.HBM(l.shape, l.dtype) for l in lands],
                   jax.ShapeDtypeStruct((8, 128), F32)),
        in_specs=[_HBM] * (2 * n), out_specs=[_SEM, _SEM] + [_HBM] * (2 * n) + [pl.BlockSpec(memory_space=pltpu.VMEM)],
        input_output_aliases={i: 2 + i for i in range(2 * n)},
        compiler_params=pltpu.CompilerParams(has_side_effects=_DATAFLOW),
    )(*_pin(*arrays), *lands)
    return outs[0], outs[1], outs[2:2 + n], outs[2 + n:2 + 2 * n], outs[-1]


def _exchange_wait(send_sems, recv_sems, src_thru, land_thru, peers, after, name):
    n = len(src_thru)

    def body(*refs):
        src_refs, land_refs = refs[:n], refs[n:2 * n]
        send_sems, recv_sems = refs[2 * n], refs[2 * n + 1]
        for j, k in enumerate(peers):
            for a in range(n):
                cp = _peer_copy(src_refs[a], land_refs[a], send_sems, recv_sems, a * len(peers) + j, k, receiving=True)
                cp.wait_send()
                cp.wait_recv()

    outs = pl.pallas_call(
        body, name=name,
        out_shape=tuple(pltpu.HBM(s.shape, s.dtype) for s in (*src_thru, *land_thru)),
        in_specs=[_HBM] * (2 * n) + [_SEM, _SEM, _order_operand(after)[0]], out_specs=[_HBM] * (2 * n),
        input_output_aliases={i: i for i in range(2 * n)},
        compiler_params=pltpu.CompilerParams(has_side_effects=_DATAFLOW),
    )(*src_thru, *land_thru, send_sems, recv_sems, _order_operand(after)[1])
    return outs[:n], outs[n:]


ALL_PEERS = tuple(range(1, N_DEV))
SIBLING = (1,)


def _sum_eight(recv, grad4, pos):
    n, r, hc = recv.shape
    steps = 2
    tc = hc // steps

    def body(pos_ref, r_ref, g_ref, o_ref):
        me = pos_ref[2]
        o_ref[...] = jnp.zeros_like(o_ref)
        for s in range(n):
            @pl.when(me == s)
            def _():
                o_ref[...] += g_ref[0].astype(F32)

            @pl.when(me != s)
            def _():
                o_ref[...] += r_ref[s].astype(F32)

    grid_spec = pltpu.PrefetchScalarGridSpec(
        num_scalar_prefetch=1, grid=(steps,),
        in_specs=[pl.BlockSpec((n, r, tc), lambda i, pos: (0, 0, i)),
                  pl.BlockSpec((1, r, tc), lambda i, pos: (pos[1], 0, pos[0] * steps + i))],
        out_specs=pl.BlockSpec((r, tc), lambda i, pos: (0, i)))
    return pl.pallas_call(body, name="sum_eight", grid_spec=grid_spec, out_shape=_out((r, hc), F32),
                          compiler_params=_cp(("parallel",), 32))(pos, *_pin(recv, grad4))


def _forward_copy(land_ref, send_sems, recv_sems, idx, j, receiving):
    x, y, c = _mesh_pos()
    px, py = _peer_chip(x, y, j)
    mine = _col_half(land_ref, c, lead=(2 * px + py,))
    theirs = _col_half(land_ref, 1 - c, lead=(2 * px + py,))
    return pltpu.make_async_remote_copy(
        src_ref=mine, dst_ref=theirs if receiving else mine, send_sem=send_sems.at[idx], recv_sem=recv_sems.at[idx],
        device_id=(x, y, 1 - c), device_id_type=MESH)


def _forward_start(lands, name):
    n = len(lands)

    def body(*refs):
        land_refs, send_sems, recv_sems, token = refs[:n], refs[n], refs[n + 1], refs[-1]
        for j in range(1, N_CHIPS):
            for a in range(n):
                _forward_copy(land_refs[a], send_sems, recv_sems, a * (N_CHIPS - 1) + j - 1, j, receiving=False).start()
        token[...] = jnp.zeros_like(token)

    sems = pltpu.SemaphoreType.DMA((n * (N_CHIPS - 1),))
    outs = pl.pallas_call(
        body, name=name,
        out_shape=(sems, sems, *[pltpu.HBM(l.shape, l.dtype) for l in lands], jax.ShapeDtypeStruct((8, 128), F32)),
        in_specs=[_HBM] * n, out_specs=[_SEM, _SEM] + [_HBM] * n + [pl.BlockSpec(memory_space=pltpu.VMEM)],
        input_output_aliases={i: 2 + i for i in range(n)},
        compiler_params=pltpu.CompilerParams(has_side_effects=_DATAFLOW),
    )(*lands)
    return outs[0], outs[1], outs[2:2 + n], outs[-1]


def _forward_wait(send_sems, recv_sems, lands_thru, after, name):
    n = len(lands_thru)

    def body(*refs):
        land_refs, send_sems, recv_sems = refs[:n], refs[n], refs[n + 1]
        for j in range(1, N_CHIPS):
            for a in range(n):
                cp = _forward_copy(land_refs[a], send_sems, recv_sems, a * (N_CHIPS - 1) + j - 1, j, receiving=True)
                cp.wait_send()
                cp.wait_recv()

    return pl.pallas_call(
        body, name=name,
        out_shape=tuple(pltpu.HBM(l.shape, l.dtype) for l in lands_thru),
        in_specs=[_HBM] * n + [_SEM, _SEM, _order_operand(after)[0]], out_specs=[_HBM] * n,
        input_output_aliases={i: i for i in range(n)},
        compiler_params=pltpu.CompilerParams(has_side_effects=_DATAFLOW),
    )(*lands_thru, send_sems, recv_sems, _order_operand(after)[1])


def _gather_finish(lands, shards):
    n = len(lands)
    any_spec = _HBM

    def body(*refs):
        shard_refs, out_refs = refs[n:2 * n], refs[2 * n:3 * n]
        send_sems, recv_sems, local_sems = refs[3 * n:]
        x, y, c = _mesh_pos()
        chip = 2 * x + y
        local, sends = [], []
        for a in range(n):
            cp = pltpu.make_async_copy(shard_refs[a], out_refs[a].at[chip], local_sems.at[a])
            cp.start()
            local.append(cp)
        for j in range(1, N_CHIPS):
            px, py = _peer_chip(x, y, j)
            for a in range(n):
                landed = _col_half(out_refs[a], c, lead=(2 * px + py,))
                cp = pltpu.make_async_remote_copy(
                    src_ref=landed, dst_ref=landed, send_sem=send_sems.at[a, j], recv_sem=recv_sems.at[a, j],
                    device_id=(x, y, 1 - c), device_id_type=MESH)
                cp.start()
                sends.append(cp)
        for j in range(1, N_CHIPS):
            px, py = _peer_chip(x, y, j)
            for a in range(n):
                other = _col_half(out_refs[a], 1 - c, lead=(2 * px + py,))
                pltpu.make_async_remote_copy(
                    src_ref=other, dst_ref=other, send_sem=send_sems.at[a, j], recv_sem=recv_sems.at[a, j],
                    device_id=(x, y, 1 - c), device_id_type=MESH).wait_recv()
        for cp in sends:
            cp.wait_send()
        for cp in local:
            cp.wait()

    return pl.pallas_call(
        body, name="gather_finish",
        out_shape=[_out(l.shape, l.dtype) for l in lands],
        in_specs=[any_spec] * (2 * n), out_specs=[any_spec] * n,
        input_output_aliases={i: i for i in range(n)},
        scratch_shapes=[pltpu.SemaphoreType.DMA((n, N_CHIPS))] * 2 + [pltpu.SemaphoreType.DMA((n,))],
        compiler_params=_cp(vmem_mb=16),
    )(*lands, *shards)


def _adam_math(w, g, m, v):
    m2 = ADAM_B1 * m + (1.0 - ADAM_B1) * g
    v2 = ADAM_B2 * v + (1.0 - ADAM_B2) * (g * g)
    m_hat = m2 / (1.0 - ADAM_B1 ** ADAM_STEP)
    v_hat = v2 / (1.0 - ADAM_B2 ** ADAM_STEP)
    delta = -ADAM_LR * (m_hat / (jnp.sqrt(v_hat) + ADAM_EPS) + ADAM_WD * w)
    return delta, m2, v2


def _adam_big(g_own, g_pair, w, m, v, pos):
    r, c = w.shape
    per_half = 2
    tc = c // (2 * per_half)

    def body(pos_ref, go_ref, gp_ref, w_ref, m_ref, v_ref, g_ref, d_ref, m2_ref, v2_ref):
        half = pl.program_id(0) // per_half
        g = jnp.where(half == pos_ref[0], go_ref[...], gp_ref[0])
        d, m2, v2 = _adam_math(w_ref[...], g, m_ref[...], v_ref[...])
        g_ref[...] = g
        d_ref[...] = d
        m2_ref[...] = m2
        v2_ref[...] = v2

    spec = pl.BlockSpec((r, tc), lambda i, pos: (0, i))
    grid_spec = pltpu.PrefetchScalarGridSpec(
        num_scalar_prefetch=1, grid=(2 * per_half,),
        in_specs=[pl.BlockSpec((r, tc), lambda i, pos: (0, i % per_half)),
                  pl.BlockSpec((1, r, tc), lambda i, pos: (1 - pos[0], 0, i % per_half)), spec, spec, spec],
        out_specs=[spec] * 4)
    sh = _out((r, c), F32)
    return pl.pallas_call(body, name="adam_big", grid_spec=grid_spec, out_shape=[sh] * 4,
                          compiler_params=_cp(("parallel",), 32))(pos, *_pin(g_own, g_pair, w, m, v))


def _adam_rows(g_own, g_pair, w3, m3, v3, pos):
    r, _, c = w3.shape
    tr = 128

    def body(pos_ref, go_ref, gp_ref, w_ref, m_ref, v_ref, g_ref, d_ref, m2_ref, v2_ref):
        core = pos_ref[0]
        g = jnp.concatenate([jnp.where(core == 0, go_ref[...], gp_ref[0]), jnp.where(core == 1, go_ref[...], gp_ref[0])], 1)
        d, m2, v2 = _adam_math(w_ref[:, 0, :], g, m_ref[:, 0, :], v_ref[:, 0, :])
        g_ref[:, 0, :] = g
        d_ref[:, 0, :] = d
        m2_ref[:, 0, :] = m2
        v2_ref[:, 0, :] = v2

    spec = pl.BlockSpec((tr, 1, c), lambda i, pos: (i, 0, 0))
    grid_spec = pltpu.PrefetchScalarGridSpec(
        num_scalar_prefetch=1, grid=(pl.cdiv(r, tr),),
        in_specs=[pl.BlockSpec((tr, c // 2), lambda i, pos: (i, 0)),
                  pl.BlockSpec((1, tr, c // 2), lambda i, pos: (1 - pos[0], i, 0)), spec, spec, spec],
        out_specs=[spec] * 4)
    sh = _out(w3.shape, F32)
    return pl.pallas_call(body, name="adam_rows", grid_spec=grid_spec, out_shape=[sh] * 4,
                          compiler_params=_cp(("parallel",), 32))(pos, *_pin(g_own, g_pair, w3, m3, v3))


def _adam_ada(c_act_t, dmod_cols, w, m, v):
    r, c = w.shape
    tc = 512

    def body(ct_ref, dm_ref, w_ref, m_ref, v_ref, g_ref, d_ref, m2_ref, v2_ref):
        g = _nn(ct_ref[...], dm_ref[...].astype(BF))
        d, m2, v2 = _adam_math(w_ref[...], g, m_ref[...], v_ref[...])
        g_ref[...] = g
        d_ref[...] = d
        m2_ref[...] = m2
        v2_ref[...] = v2

    spec = pl.BlockSpec((r, tc), lambda i: (0, i))
    sh = _out((r, c), F32)
    return pl.pallas_call(
        body, name="adam_ada", grid=(c // tc,),
        in_specs=[pl.BlockSpec(c_act_t.shape, lambda i: (0, 0)), pl.BlockSpec((dmod_cols.shape[0], tc), lambda i: (0, i)),
                  spec, spec, spec],
        out_specs=[spec] * 4, out_shape=[sh] * 4, compiler_params=_cp(("parallel",), 48))(*_pin(c_act_t, dmod_cols, w, m, v))


def _ada_mod(c_all, w_shard, b_shard, token):
    nb, d = c_all.shape
    cols = w_shard.shape[1]
    tc = 512

    def body(c_ref, w_ref, b_ref, tok_ref, mod_ref, act_ref):
        cv = c_ref[...]
        act = cv * _sigmoid(cv)
        act_ref[...] = act
        mod_ref[...] = _nn(act.astype(BF), w_ref[...].astype(BF)) + b_ref[...]

    return pl.pallas_call(
        body, name="ada_mod", grid=(cols // tc,),
        in_specs=[pl.BlockSpec((nb, d), lambda i: (0, 0)), pl.BlockSpec((d, tc), lambda i: (0, i)),
                  pl.BlockSpec((1, tc), lambda i: (0, i)), _token_spec()],
        out_specs=[pl.BlockSpec((nb, tc), lambda i: (0, i)), pl.BlockSpec((nb, d), lambda i: (0, 0))],
        out_shape=[_out((nb, cols), F32), _out((nb, d), F32)],
        compiler_params=_cp(("arbitrary",), 32))(*_pin(c_all, w_shard, b_shard, token))


SUB_ROWS = 256
ROW_TILE = 512


def _sub_rows(tm):
    return [slice(s, s + SUB_ROWS) for s in range(0, tm, SUB_ROWS)] if tm > SUB_ROWS else [slice(0, tm)]


_RESIDENT = pl.BlockSpec(memory_space=pltpu.VMEM)


def _token_spec():
    return pl.BlockSpec((8, 128), lambda *_: (0, 0))


def _in_proj(x, mod3, g_mix, w_in_t, seq, token):
    t, d = x.shape
    tm = min(ROW_TILE, seq)
    tps = seq // tm

    def body(x_ref, mod_ref, g_ref, w_ref, tok_ref, proj_ref, u1_ref):
        for rows in _sub_rows(tm):
            xv = x_ref[rows, :]
            r = lax.rsqrt(jnp.mean(xv * xv, -1, keepdims=True) + EPS)
            u = (xv * r * g_ref[...]) * (1.0 + mod_ref[0, 1:2, :]) + mod_ref[0, 0:1, :]
            ub = u.astype(BF)
            u1_ref[rows, :] = ub
            proj_ref[rows, 0:OFF_DT] = _nt(ub, w_ref[0:OFF_DT, :])
            proj_ref[rows, OFF_DT:IN_PAD] = jnp.zeros((rows.stop - rows.start, IN_PAD - OFF_DT), F32)
            proj_ref[rows, OFF_DT:IN_WIDTH] = _nt(ub, w_ref[OFF_DT:IN_WIDTH, :])

    return pl.pallas_call(
        body, name="in_proj", grid=(t // tm,),
        in_specs=[pl.BlockSpec((tm, d), lambda i: (i, 0)), pl.BlockSpec((1, N_MOD, d), lambda i: (i // tps, 0, 0)),
                  pl.BlockSpec((1, d), lambda i: (0, 0)), _RESIDENT, _token_spec()],
        out_specs=[pl.BlockSpec((tm, IN_PAD), lambda i: (i, 0)), pl.BlockSpec((tm, d), lambda i: (i, 0))],
        out_shape=[_out((t, IN_PAD), F32), _out((t, d), BF)],
        compiler_params=_cp(("parallel",), 40))(*_pin(x, mod3, g_mix), w_in_t, *_pin(token))


def _pool_tile(seq):
    return min(1024, seq)


def _pool_fwd(proj, w_pool, pool_scale, nb, seq, token):
    ts = _pool_tile(seq)
    nt = seq // ts

    def body(u_ref, halo_ref, wp_ref, ps_ref, tok_ref, yp_ref, p_ref):
        i = pl.program_id(1)
        halo = jnp.where(i == 0, 0.0, halo_ref[...])
        u = u_ref[...]
        ext = jnp.concatenate([halo, u], 0)
        tpos = i * ts + _iota((ts, 1), 0)
        for g, w in enumerate(POOL_WINDOWS):
            gs = slice(g * POOL_GROUP, (g + 1) * POOL_GROUP)
            s = ext[:, gs]
            sh = 1
            while sh < w:
                s = s + pltpu.roll(s, sh, 0)
                sh *= 2
            cnt = jnp.minimum(tpos + 1, w).astype(F32)
            pb = (s[HALO:] / cnt - u[:, gs]).astype(BF)
            p_ref[:, gs] = pb
            yp_ref[:, gs] = (_nn(pb, wp_ref[g].astype(BF)) * ps_ref[:, gs]).astype(BF)

    hb = ts // HALO
    return pl.pallas_call(
        body, name="pool_fwd", grid=(nb, nt),
        in_specs=[pl.BlockSpec((ts, POOL_WIDTH), lambda b, i: (b * nt + i, 0)),
                  pl.BlockSpec((HALO, POOL_WIDTH), lambda b, i: (jnp.maximum((b * nt + i) * hb - 1, 0), 0)),
                  pl.BlockSpec((4, POOL_GROUP, POOL_GROUP), lambda b, i: (0, 0, 0)),
                  pl.BlockSpec((1, POOL_WIDTH), lambda b, i: (0, 0)), _token_spec()],
        out_specs=[pl.BlockSpec((ts, POOL_WIDTH), lambda b, i: (b * nt + i, 0))] * 2,
        out_shape=[_out((nb * seq, POOL_WIDTH), BF)] * 2,
        compiler_params=_cp(("parallel", "parallel"), 32))(*_pin(proj, proj, w_pool, pool_scale, token))


def _conv_pre(uxbc, halo, cw, cb, first):
    halo = jnp.where(first, 0.0, halo)
    ext = jnp.concatenate([halo, uxbc], 0)
    pre = cb + uxbc * cw[3:4]
    for k in (2, 1, 0):
        pre = pre + pltpu.roll(ext, 3 - k, 0)[CONV_HALO:] * cw[k:k + 1]
    return pre


def _chunk_terms(pre, udt, dtb, alog):
    sg = _sigmoid(pre)
    xbc = pre * sg
    dtp = udt[:, :SSD_HEADS] + dtb
    dt = jnp.maximum(dtp, 0.0) + jnp.log(1.0 + jnp.exp(-jnp.abs(dtp)))
    a = -jnp.exp(alog)
    da = dt * a
    tril = (_iota((CHUNK, CHUNK), 0) >= _iota((CHUNK, CHUNK), 1))
    acum = _exact_nn_left(tril.astype(BF), da)
    eye = (_iota((SSD_HEADS, SSD_HEADS), 0) == _iota((SSD_HEADS, SSD_HEADS), 1)).astype(BF)
    acum_t = _exact_nt_left(eye, acum)
    expand = _head_expand_matrix(SSD_HEADS, SSD_INNER)
    acum_e = _exact_nn(acum, expand)
    dt_e = _exact_nn(dt, expand)
    last_e = acum_e[CHUNK - 1:CHUNK]
    return dict(pre=pre, sg=sg, xbc=xbc, dtp=dtp, dt=dt, a=a, acum=acum, acum_t=acum_t, tril=tril,
                dt_e=dt_e, e_a=jnp.exp(acum_e), d_out=jnp.exp(last_e - acum_e), c_dec=jnp.exp(last_e))


def _head_decay(r, h):
    seg = r["acum"][:, h:h + 1] - r["acum_t"][h:h + 1, :]
    return jnp.where(r["tril"], jnp.exp(jnp.minimum(seg, 0.0)), 0.0)


SSD_SUB = 4
SSD_ROWS = SSD_SUB * CHUNK


def _ssd_specs(nb, seq, reverse):
    ns = seq // SSD_ROWS
    per = seq // CONV_HALO

    def cidx(c):
        return (ns - 1 - c) if reverse else c

    def row(b, c):
        return b * ns + cidx(c)

    specs = [
        pl.BlockSpec((SSD_ROWS, CONV_CH), lambda b, c: (row(b, c), 1)),
        pl.BlockSpec((CONV_HALO, CONV_CH),
                     lambda b, c: (jnp.maximum(b * per + cidx(c) * (SSD_ROWS // CONV_HALO) - 1, 0), 1)),
        pl.BlockSpec((SSD_ROWS, GROUP_W), lambda b, c: (row(b, c), 1)),
        pl.BlockSpec((SSD_ROWS, GROUP_W), lambda b, c: (row(b, c), 2)),
        pl.BlockSpec((SSD_ROWS, 128), lambda b, c: (row(b, c), OFF_DT // 128)),
    ]
    return specs, row, cidx, ns


def _const_spec(shape):
    return pl.BlockSpec(shape, lambda b, c: (0,) * len(shape))


def _ssd_fwd(proj, conv_w, conv_b, dt_bias, a_log, dskip_e, g_ssd, nb, seq):
    specs, row, cidx, ns = _ssd_specs(nb, seq, reverse=False)

    def body(uxbc_ref, halo_ref, z0_ref, z1_ref, udt_ref, cw_ref, cb_ref, dtb_ref, alog_ref, dsk_ref, gs_ref,
             yssd_ref, yssm_ref, hprev_ref, pre_ref, h_ref, yd_ref):
        c = pl.program_id(1)

        @pl.when(c == 0)
        def _():
            h_ref[...] = jnp.zeros_like(h_ref)

        for sub in range(SSD_SUB):
            rows = slice(sub * CHUNK, (sub + 1) * CHUNK)
            if sub == 0:
                halo, first = halo_ref[...], c == 0
            else:
                halo, first = uxbc_ref[sub * CHUNK - CONV_HALO:sub * CHUNK, :], False
            pre = _conv_pre(uxbc_ref[rows, :], halo, cw_ref[...], cb_ref[...], first)
            pre_ref[rows, :] = pre
            r = _chunk_terms(pre, udt_ref[rows, :], dtb_ref[...], alog_ref[...])
            xbc = r["xbc"]
            xs = xbc[:, :SSD_INNER]
            xdt = xs * r["dt_e"]
            xdt_b = xdt.astype(BF)
            xdo_b = (xdt * r["d_out"]).astype(BF)
            hprev_ref[0, sub] = h_ref[...]
            for g in range(2):
                gs = slice(g * GROUP_W, (g + 1) * GROUP_W)
                bg = xbc[:, SSD_INNER + g * SSD_STATE:SSD_INNER + (g + 1) * SSD_STATE].astype(BF)
                cg = xbc[:, SSD_INNER + (2 + g) * SSD_STATE:SSD_INNER + (3 + g) * SSD_STATE].astype(BF)
                scores = _nt(cg, bg)
                hg = h_ref[g]
                y_off = _nn(cg, hg.astype(BF)) * r["e_a"][:, gs]
                for hh in range(8):
                    h = g * 8 + hh
                    hs = slice(h * SSD_HEAD_DIM, (h + 1) * SSD_HEAD_DIM)
                    m = (scores * _head_decay(r, h)).astype(BF)
                    yd_ref[sub, :, hs] = _nn(m, xdt_b[:, hs])
                h_ref[g] = hg * r["c_dec"][:, gs] + _tn(bg, xdo_b[:, gs])
                y = yd_ref[sub, :, gs] + y_off + dsk_ref[:, gs] * xs[:, gs]
                yssm_ref[rows, gs] = y
                zg = (z0_ref if g == 0 else z1_ref)[rows, :]
                yg = y * (zg * _sigmoid(zg))
                rg = lax.rsqrt(jnp.mean(yg * yg, -1, keepdims=True) + EPS)
                yssd_ref[rows, gs] = (yg * rg * gs_ref[:, gs]).astype(BF)

    t = nb * seq
    return pl.pallas_call(
        body, name="ssd_fwd", grid=(nb, ns),
        in_specs=specs + [_const_spec((4, CONV_CH)), _const_spec((1, CONV_CH)), _const_spec((1, SSD_HEADS)),
                          _const_spec((1, SSD_HEADS)), _const_spec((1, SSD_INNER)), _const_spec((1, SSD_INNER))],
        out_specs=[pl.BlockSpec((SSD_ROWS, SSD_INNER), lambda b, c: (row(b, c), 0)),
                   pl.BlockSpec((SSD_ROWS, SSD_INNER), lambda b, c: (row(b, c), 0)),
                   pl.BlockSpec((1, SSD_SUB, 2, SSD_STATE, GROUP_W), lambda b, c: (b, c, 0, 0, 0)),
                   pl.BlockSpec((SSD_ROWS, CONV_CH), lambda b, c: (row(b, c), 0))],
        out_shape=[_out((t, SSD_INNER), BF), _out((t, SSD_INNER), F32),
                   _out((nb, seq // CHUNK, 2, SSD_STATE, GROUP_W), F32), _out((t, CONV_CH), F32)],
        scratch_shapes=[pltpu.VMEM((2, SSD_STATE, GROUP_W), F32), pltpu.VMEM((SSD_SUB, CHUNK, SSD_INNER), F32)],
        compiler_params=_cp(("arbitrary", "arbitrary"), 56),
    )(*_pin(proj, proj, proj, proj, proj, conv_w, conv_b, dt_bias, a_log, dskip_e, g_ssd))


def _out_proj(y_pool, y_ssd, w_out, x, mod3, g_mlp, seq):
    t, d = x.shape
    tm = 512
    tps = seq // tm if seq >= tm else 1
    tm = min(tm, seq)

    def body(yp_ref, ys_ref, w_ref, x_ref, mod_ref, g_ref, h1_ref, o_ref, u2_ref):
        for rows in _sub_rows(tm):
            o = _nn(jnp.concatenate([yp_ref[rows, :], ys_ref[rows, :]], 1), w_ref[...])
            o_ref[rows, :] = o.astype(BF)
            h1 = x_ref[rows, :] + mod_ref[0, 2:3, :] * o
            h1_ref[rows, :] = h1
            r = lax.rsqrt(jnp.mean(h1 * h1, -1, keepdims=True) + EPS)
            u2_ref[rows, :] = ((h1 * r * g_ref[...]) * (1.0 + mod_ref[0, 4:5, :]) + mod_ref[0, 3:4, :]).astype(BF)

    row = lambda i: (i, 0)
    return pl.pallas_call(
        body, name="out_proj", grid=(t // tm,),
        in_specs=[pl.BlockSpec((tm, POOL_WIDTH), row), pl.BlockSpec((tm, SSD_INNER), row),
                  _RESIDENT, pl.BlockSpec((tm, d), row),
                  pl.BlockSpec((1, N_MOD, d), lambda i: (i // tps, 0, 0)), pl.BlockSpec((1, d), lambda i: (0, 0))],
        out_specs=[pl.BlockSpec((tm, d), row)] * 3,
        out_shape=[_out((t, d), F32), _out((t, d), BF), _out((t, d), BF)],
        compiler_params=_cp(("parallel",), 48))(*_pin(y_pool, y_ssd), w_out, *_pin(x, mod3, g_mlp))


def _mlp_up(u2, w_up4):
    t, d = u2.shape
    tm = min(1024, t)
    nk, _, cols = w_up4.shape

    def body(u_ref, w_ref, a_ref):
        a_ref[...] = _nn(u_ref[...], w_ref[pl.program_id(1)]).astype(BF)

    return pl.pallas_call(
        body, name="mlp_up", grid=(t // tm, nk),
        in_specs=[pl.BlockSpec((tm, d), lambda i, k: (i, 0)), _RESIDENT],
        out_specs=pl.BlockSpec((tm, cols), lambda i, k: (i, k)),
        out_shape=_out((t, nk * cols), BF),
        compiler_params=_cp(("parallel", "parallel"), 32))(*_pin(u2), w_up4)


def _mlp_down_loss(a_up, w_down, h1, mod3, g_final, target, seq):
    t, d = h1.shape
    nb = t // seq
    tm = min(ROW_TILE, seq)
    tps = seq // tm

    def body(a_ref, w_ref, h1_ref, mod_ref, g_ref, tg_ref, ddn_ref, dh2_ref, sq_ref, gg_ref, dgf_ref):
        i = pl.program_id(0)

        @pl.when(i == 0)
        def _():
            sq_ref[...] = jnp.zeros_like(sq_ref)
            gg_ref[...] = jnp.zeros_like(gg_ref)

        @pl.when(i % tps == 0)
        def _():
            dgf_ref[...] = jnp.zeros_like(dgf_ref)

        gate = mod_ref[0, 5:6, :]
        sq = gg = dgf = 0.0
        for rows in _sub_rows(tm):
            f = jnp.square(jnp.maximum(a_ref[rows, :], 0))
            dn = _nn(f, w_ref[...])
            h2 = h1_ref[rows, :] + gate * dn
            r = lax.rsqrt(jnp.mean(h2 * h2, -1, keepdims=True) + EPS)
            hh = h2 * r
            err = hh * g_ref[...] - tg_ref[rows, :]
            dy = err * (1.0 / d)
            dhat = dy * g_ref[...]
            dh2 = r * (dhat - hh * jnp.mean(dhat * hh, -1, keepdims=True))
            dh2_ref[rows, :] = dh2
            ddn_ref[rows, :] = (dh2 * gate).astype(BF)
            sq = sq + jnp.sum(err * err, 0, keepdims=True)
            gg = gg + jnp.sum(dy * hh, 0, keepdims=True)
            dgf = dgf + jnp.sum(dh2 * dn, 0, keepdims=True)
        sq_ref[...] += sq
        gg_ref[...] += gg
        dgf_ref[0] += dgf

    row = lambda i: (i, 0)
    vec = pl.BlockSpec((1, d), lambda i: (0, 0))
    return pl.pallas_call(
        body, name="mlp_down_loss", grid=(t // tm,),
        in_specs=[pl.BlockSpec((tm, D_FF), row), _RESIDENT, pl.BlockSpec((tm, d), row),
                  pl.BlockSpec((1, N_MOD, d), lambda i: (i // tps, 0, 0)), vec, pl.BlockSpec((tm, d), row)],
        out_specs=[pl.BlockSpec((tm, d), row), pl.BlockSpec((tm, d), row), vec, vec,
                   pl.BlockSpec((1, 1, d), lambda i: (i // tps, 0, 0))],
        out_shape=[_out((t, d), BF), _out((t, d), F32), _out((1, d), F32),
                   _out((1, d), F32), _out((nb, 1, d), F32)],
        compiler_params=_cp(("arbitrary",), 44))(*_pin(a_up), w_down, *_pin(h1, mod3, g_final, target))


def _tn_matmul(a, b, tk, tn, name, square_relu=False, out3=False):
    t, kdim = a.shape
    ndim = b.shape[1]

    def body(a_ref, b_ref, o_ref):
        av = a_ref[...]
        if square_relu:
            av = jnp.square(jnp.maximum(av, 0))
        res = _tn(av, b_ref[...]).astype(BF)
        if out3:
            o_ref[0] = res
        else:
            o_ref[...] = res

    if out3:
        out_spec = pl.BlockSpec((1, tk, tn), lambda j, i: (j, i, 0))
        out_shape = _out((ndim // tn, kdim, tn), BF)
    else:
        out_spec = pl.BlockSpec((tk, tn), lambda j, i: (i, j))
        out_shape = _out((kdim, ndim), BF)
    return pl.pallas_call(
        body, name=name, grid=(ndim // tn, kdim // tk),
        in_specs=[pl.BlockSpec((t, tk), lambda j, i: (0, i)), pl.BlockSpec((t, tn), lambda j, i: (0, j))],
        out_specs=out_spec, out_shape=out_shape,
        compiler_params=_cp(("parallel", "parallel"), 36))(*_pin(a, b))


def _mlp_down_bwd(d_dn, w_down4, a_up, token):
    t, d = d_dn.shape
    tm = min(1024, t)
    nk, rows, _ = w_down4.shape

    def body(g_ref, w_ref, a_ref, tok_ref, o_ref):
        df = _nt(g_ref[...], w_ref[pl.program_id(1)])
        o_ref[...] = (df * (2.0 * jnp.maximum(a_ref[...], 0).astype(F32))).astype(BF)

    return pl.pallas_call(
        body, name="mlp_down_bwd", grid=(t // tm, nk),
        in_specs=[pl.BlockSpec((tm, d), lambda i, k: (i, 0)), _RESIDENT,
                  pl.BlockSpec((tm, rows), lambda i, k: (i, k)), _token_spec()],
        out_specs=pl.BlockSpec((tm, rows), lambda i, k: (i, k)),
        out_shape=_out((t, nk * rows), BF),
        compiler_params=_cp(("parallel", "parallel"), 32))(*_pin(d_dn), w_down4, *_pin(a_up, token))


def _mlp_up_bwd(d_a, w_up4, h1, dh2, o, mod3, g_mlp, seq, token):
    t, d = h1.shape
    nb = t // seq
    tm = min(ROW_TILE, seq)
    tps = seq // tm
    nk = w_up4.shape[0]
    cols = w_up4.shape[2]

    def body(da_ref, w_ref, h1_ref, dh2_ref, o_ref, mod_ref, g_ref, tok_ref, dh1_ref, do_ref, acc_ref, gg_ref):
        i = pl.program_id(0)

        @pl.when(i == 0)
        def _():
            gg_ref[...] = jnp.zeros_like(gg_ref)

        @pl.when(i % tps == 0)
        def _():
            acc_ref[...] = jnp.zeros_like(acc_ref)

        gg = a_shift = a_scale = a_gate = 0.0
        for rows in _sub_rows(tm):
            du = _nt(da_ref[rows, 0:cols], w_ref[0])
            for k in range(1, nk):
                du = du + _nt(da_ref[rows, k * cols:(k + 1) * cols], w_ref[k])
            h1 = h1_ref[rows, :]
            r = lax.rsqrt(jnp.mean(h1 * h1, -1, keepdims=True) + EPS)
            hh = h1 * r
            n2 = hh * g_ref[...]
            dn2 = du * (1.0 + mod_ref[0, 4:5, :])
            dhat = dn2 * g_ref[...]
            dh1 = dh2_ref[rows, :] + r * (dhat - hh * jnp.mean(dhat * hh, -1, keepdims=True))
            dh1_ref[rows, :] = dh1
            do_ref[rows, :] = (dh1 * mod_ref[0, 2:3, :]).astype(BF)
            gg = gg + jnp.sum(dn2 * hh, 0, keepdims=True)
            a_shift = a_shift + jnp.sum(du, 0, keepdims=True)
            a_scale = a_scale + jnp.sum(du * n2, 0, keepdims=True)
            a_gate = a_gate + jnp.sum(dh1 * o_ref[rows, :].astype(F32), 0, keepdims=True)
        gg_ref[...] += gg
        acc_ref[0, 0:1, :] += a_shift
        acc_ref[0, 1:2, :] += a_scale
        acc_ref[0, 2:3, :] += a_gate

    row = lambda i: (i, 0)
    vec = pl.BlockSpec((1, d), lambda i: (0, 0))
    return pl.pallas_call(
        body, name="mlp_up_bwd", grid=(t // tm,),
        in_specs=[pl.BlockSpec((tm, D_FF), row), _RESIDENT, pl.BlockSpec((tm, d), row),
                  pl.BlockSpec((tm, d), row), pl.BlockSpec((tm, d), row),
                  pl.BlockSpec((1, N_MOD, d), lambda i: (i // tps, 0, 0)), vec, _token_spec()],
        out_specs=[pl.BlockSpec((tm, d), row), pl.BlockSpec((tm, d), row),
                   pl.BlockSpec((1, 8, d), lambda i: (i // tps, 0, 0)), vec],
        out_shape=[_out((t, d), F32), _out((t, d), BF),
                   _out((nb, 8, d), F32), _out((1, d), F32)],
        compiler_params=_cp(("arbitrary",), 44))(*_pin(d_a), w_up4, *_pin(h1, dh2, o, mod3, g_mlp, token))


def _out_proj_bwd(d_o, w_out, token):
    t, d = d_o.shape
    tm = min(1024, t)

    def body(g_ref, w_ref, tok_ref, dp_ref, ds_ref):
        gv = g_ref[...]
        dp_ref[...] = _nt(gv, w_ref[0:POOL_WIDTH, :])
        ds_ref[...] = _nt(gv, w_ref[POOL_WIDTH:, :])

    row = lambda i: (i, 0)
    return pl.pallas_call(
        body, name="out_proj_bwd", grid=(t // tm,),
        in_specs=[pl.BlockSpec((tm, d), row), _RESIDENT, _token_spec()],
        out_specs=[pl.BlockSpec((tm, POOL_WIDTH), row), pl.BlockSpec((tm, SSD_INNER), row)],
        out_shape=[_out((t, POOL_WIDTH), F32), _out((t, SSD_INNER), F32)],
        compiler_params=_cp(("parallel",), 32))(*_pin(d_o), w_out, *_pin(token))


def _pool_bwd(d_ypool, p, w_pool, pool_scale, nb, seq):
    ts = _pool_tile(seq)
    nt = seq // ts
    hb = ts // HALO
    last_block = nb * seq // HALO - 1

    def body(dy_ref, halo_ref, p_ref, wp_ref, ps_ref, du_ref, gw_ref, gs_ref):
        b = pl.program_id(0)
        i = pl.program_id(1)

        @pl.when((b == 0) & (i == 0))
        def _():
            gw_ref[...] = jnp.zeros_like(gw_ref)
            gs_ref[...] = jnp.zeros_like(gs_ref)

        halo = jnp.where(i == nt - 1, 0.0, halo_ref[...])
        dy = dy_ref[...]
        ext = jnp.concatenate([dy, halo], 0)
        tpos = i * ts + _iota((ts + HALO, 1), 0)
        n_ext = ts + HALO
        for g, w in enumerate(POOL_WINDOWS):
            gs = slice(g * POOL_GROUP, (g + 1) * POOL_GROUP)
            wg = wp_ref[g].astype(BF)
            pg = p_ref[:, gs]
            pw = _nn(pg, wg)
            gs_ref[:, gs] += jnp.sum(dy[:, gs] * pw, 0, keepdims=True)
            dpw = (ext[:, gs] * ps_ref[:, gs]).astype(BF)
            gw_ref[g] += _tn(pg, dpw[:ts])
            dp = _nt(dpw, wg)
            cnt = jnp.minimum(tpos + 1, w).astype(F32)
            s = dp / cnt
            sh = 1
            while sh < w:
                s = s + pltpu.roll(s, n_ext - sh, 0)
                sh *= 2
            du_ref[:, gs] = (s[:ts] - dp[:ts]).astype(BF)

    return pl.pallas_call(
        body, name="pool_bwd", grid=(nb, nt),
        in_specs=[pl.BlockSpec((ts, POOL_WIDTH), lambda b, i: (b * nt + i, 0)),
                  pl.BlockSpec((HALO, POOL_WIDTH), lambda b, i: (jnp.minimum((b * nt + i + 1) * hb, last_block), 0)),
                  pl.BlockSpec((ts, POOL_WIDTH), lambda b, i: (b * nt + i, 0)),
                  pl.BlockSpec((4, POOL_GROUP, POOL_GROUP), lambda b, i: (0, 0, 0)),
                  pl.BlockSpec((1, POOL_WIDTH), lambda b, i: (0, 0))],
        out_specs=[pl.BlockSpec((ts, POOL_WIDTH), lambda b, i: (b * nt + i, 0)),
                   pl.BlockSpec((4, POOL_GROUP, POOL_GROUP), lambda b, i: (0, 0, 0)),
                   pl.BlockSpec((1, POOL_WIDTH), lambda b, i: (0, 0))],
        out_shape=[_out((nb * seq, POOL_WIDTH), BF), _out((4, POOL_GROUP, POOL_GROUP), F32),
                   _out((1, POOL_WIDTH), F32)],
        compiler_params=_cp(("arbitrary", "arbitrary"), 32))(*_pin(d_ypool, d_ypool, p, w_pool, pool_scale))


def _ssd_bwd(proj, pre, d_yssd, yssm, h_prev, dt_bias, a_log, dskip_e, g_ssd, nb, seq):
    specs, row, cidx, ns = _ssd_specs(nb, seq, reverse=True)
    specs = specs[2:]

    def body(z0_ref, z1_ref, udt_ref, pre_ref, dys_ref, yssm_ref, hprev_ref,
             dtb_ref, alog_ref, dsk_ref, gs_ref,
             dz_ref, dpre_ref, dudt_ref, ggs_ref, gdsk_ref, ga_ref, gdtb_ref,
             g_ref, dxdt_ref, dyv_ref):
        b = pl.program_id(0)
        c = pl.program_id(1)

        @pl.when(c == 0)
        def _():
            g_ref[...] = jnp.zeros_like(g_ref)

        @pl.when((b == 0) & (c == 0))
        def _():
            ggs_ref[...] = jnp.zeros_like(ggs_ref)
            gdsk_ref[...] = jnp.zeros_like(gdsk_ref)
            ga_ref[...] = jnp.zeros_like(ga_ref)
            gdtb_ref[...] = jnp.zeros_like(gdtb_ref)

        for sub in reversed(range(SSD_SUB)):
            chunk(sub, z0_ref, z1_ref, udt_ref, pre_ref, dys_ref, yssm_ref, hprev_ref, dtb_ref, alog_ref, dsk_ref, gs_ref,
                  dz_ref, dpre_ref, dudt_ref, ggs_ref, gdsk_ref, ga_ref, gdtb_ref, g_ref, dxdt_ref.at[sub], dyv_ref.at[sub])

    def chunk(sub, z0_ref, z1_ref, udt_ref, pre_ref, dys_ref, yssm_ref, hprev_ref,
              dtb_ref, alog_ref, dsk_ref, gs_ref,
              dz_ref, dpre_ref, dudt_ref, ggs_ref, gdsk_ref, ga_ref, gdtb_ref,
              g_ref, dxdt_ref, dyv_ref):
        rows = slice(sub * CHUNK, (sub + 1) * CHUNK)
        r = _chunk_terms(pre_ref[rows, :], udt_ref[rows, :], dtb_ref[...], alog_ref[...])
        xbc = r["xbc"]
        xs = xbc[:, :SSD_INNER]
        dt_e = r["dt_e"]
        xdt = xs * dt_e
        xdt_b = xdt.astype(BF)
        reduce_m = _head_reduce_matrix(GROUP_W, 8)

        def head_sums(v):
            return _nn(v.astype(BF), reduce_m)

        onehot16 = lambda h: (_iota((1, SSD_HEADS), 1) == h).astype(F32)
        onecol16 = lambda h: (_iota((SSD_HEADS, 1), 0) == h).astype(F32)

        d_acum = jnp.zeros((CHUNK, SSD_HEADS), F32)
        d_acum_t = jnp.zeros((SSD_HEADS, CHUNK), F32)
        d_alast = jnp.zeros((1, SSD_HEADS), F32)
        place8 = lambda g: (_iota((8, SSD_HEADS), 1) == _iota((8, SSD_HEADS), 0) + 8 * g).astype(BF)
        d_b, d_c = [], []
        for g in range(2):
            gs = slice(g * GROUP_W, (g + 1) * GROUP_W)
            zg = (z0_ref if g == 0 else z1_ref)[rows, :]
            sz = _sigmoid(zg)
            silu_z = zg * sz
            ys = yssm_ref[rows, gs]
            yg = ys * silu_z
            rg = lax.rsqrt(jnp.mean(yg * yg, -1, keepdims=True) + EPS)
            yh = yg * rg
            dys = dys_ref[rows, gs]
            ggs_ref[:, gs] += jnp.sum(dys * yh, 0, keepdims=True)
            dyh = dys * gs_ref[:, gs]
            dyg = rg * (dyh - yh * jnp.mean(dyh * yh, -1, keepdims=True))
            dy = dyg * silu_z
            dz_ref[rows, gs] = (dyg * ys * (sz * (1.0 + zg * (1.0 - sz)))).astype(BF)
            gdsk_ref[:, gs] += jnp.sum(dy * xs[:, gs], 0, keepdims=True)
            dyv_ref[:, gs] = dy
            dy_b = dy.astype(BF)

            bg = xbc[:, SSD_INNER + g * SSD_STATE:SSD_INNER + (g + 1) * SSD_STATE].astype(BF)
            cg = xbc[:, SSD_INNER + (2 + g) * SSD_STATE:SSD_INNER + (3 + g) * SSD_STATE].astype(BF)
            scores = _nt(cg, bg)
            hg = hprev_ref[0, sub, g]
            hg_b = hg.astype(BF)
            gg = g_ref[g]
            gg_b = gg.astype(BF)
            e_a = r["e_a"][:, gs]
            d_out = r["d_out"][:, gs]
            c_dec = r["c_dec"][:, gs]
            zc = _nn(cg, hg_b)
            wv = e_a * dy
            wv_b = wv.astype(BF)
            da_g = head_sums(wv * zc)
            dcg = _nt(wv_b, hg_b)
            d_hprev = _tn(cg, wv_b)
            vg = _nn(bg, gg_b)
            dxdt_g = d_out * vg
            dd_out = head_sums(xdt[:, gs] * vg)
            dbg = _nt((xdt[:, gs] * d_out).astype(BF), gg_b)
            dcd = _exact_nn(jnp.sum(gg * hg, 0, keepdims=True), reduce_m)
            d_out8 = jnp.exp(r["acum"][CHUNK - 1:CHUNK, 8 * g:8 * g + 8] - r["acum"][:, 8 * g:8 * g + 8])
            c_dec8 = jnp.exp(r["acum"][CHUNK - 1:CHUNK, 8 * g:8 * g + 8])
            t8 = dd_out * d_out8
            d_alast = d_alast + _exact_nn(jnp.sum(t8, 0, keepdims=True) + dcd * c_dec8, place8(g))
            d_acum = d_acum + _exact_nn(da_g - t8, place8(g))
            dsc = jnp.zeros((CHUNK, CHUNK), F32)
            for hh in range(8):
                h = g * 8 + hh
                hs = slice(h * SSD_HEAD_DIM, (h + 1) * SSD_HEAD_DIM)
                lam = _head_decay(r, h)
                m = scores * lam
                dyh_b = dy_b[:, hh * SSD_HEAD_DIM:(hh + 1) * SSD_HEAD_DIM]
                dm = _nt(dyh_b, xdt_b[:, hs])
                tm_ = dm * m
                d_acum = d_acum + jnp.sum(tm_, 1, keepdims=True) * onehot16(h)
                d_acum_t = d_acum_t + onecol16(h) * jnp.sum(tm_, 0, keepdims=True)
                dsc = dsc + dm * lam
                dxdt_ref[:, hs] = _tn(m.astype(BF), dyh_b) + dxdt_g[:, hh * SSD_HEAD_DIM:(hh + 1) * SSD_HEAD_DIM]
            dsc_b = dsc.astype(BF)
            d_c.append(dcg + _nn(dsc_b, bg))
            d_b.append(dbg + _tn(dsc_b, cg))
            g_ref[g] = d_hprev + c_dec * gg

        eye = (_iota((CHUNK, CHUNK), 0) == _iota((CHUNK, CHUNK), 1)).astype(BF)
        d_acum = d_acum - _exact_nt_left(eye, d_acum_t)
        is_last = (_iota((CHUNK, 1), 0) == CHUNK - 1).astype(F32)
        d_acum = d_acum + is_last * d_alast
        triu = (_iota((CHUNK, CHUNK), 0) <= _iota((CHUNK, CHUNK), 1)).astype(BF)
        d_da = _exact_nn_left(triu, d_acum)
        dt = r["dt"]
        ga_ref[...] += jnp.sum(d_da * dt, 0, keepdims=True)
        dxdt = dxdt_ref[...]
        reduce16 = _head_reduce_matrix(SSD_INNER, SSD_HEADS)
        d_dt = d_da * r["a"] + _nn((dxdt * xs).astype(BF), reduce16)
        d_udt = d_dt * _sigmoid(r["dtp"])
        gdtb_ref[...] += jnp.sum(d_udt, 0, keepdims=True)
        dudt_ref[rows, :] = jnp.zeros((CHUNK, dudt_ref.shape[1]), BF)
        dudt_ref[rows, 0:SSD_HEADS] = d_udt.astype(BF)
        pre, sg = r["pre"], r["sg"]
        dsilu = sg * (1.0 + pre * (1.0 - sg))
        dpre_ref[rows, 0:SSD_INNER] = (dsk_ref[...] * dyv_ref[...] + dxdt * dt_e) * dsilu[:, 0:SSD_INNER]
        for g in range(2):
            bs = slice(SSD_INNER + g * SSD_STATE, SSD_INNER + (g + 1) * SSD_STATE)
            cs = slice(SSD_INNER + (2 + g) * SSD_STATE, SSD_INNER + (3 + g) * SSD_STATE)
            dpre_ref[rows, bs] = d_b[g] * dsilu[:, bs]
            dpre_ref[rows, cs] = d_c[g] * dsilu[:, cs]

    t = nb * seq
    vec = _const_spec((1, SSD_INNER))
    small = _const_spec((1, SSD_HEADS))
    return pl.pallas_call(
        body, name="ssd_bwd", grid=(nb, ns),
        in_specs=specs + [pl.BlockSpec((SSD_ROWS, CONV_CH), lambda b, c: (row(b, c), 0)),
                          pl.BlockSpec((SSD_ROWS, SSD_INNER), lambda b, c: (row(b, c), 0)),
                          pl.BlockSpec((SSD_ROWS, SSD_INNER), lambda b, c: (row(b, c), 0)),
                          pl.BlockSpec((1, SSD_SUB, 2, SSD_STATE, GROUP_W), lambda b, c: (b, cidx(c), 0, 0, 0)),
                          small, small, vec, vec],
        out_specs=[pl.BlockSpec((SSD_ROWS, SSD_INNER), lambda b, c: (row(b, c), 0)),
                   pl.BlockSpec((SSD_ROWS, CONV_CH), lambda b, c: (row(b, c), 0)),
                   pl.BlockSpec((SSD_ROWS, 128), lambda b, c: (row(b, c), 0)),
                   vec, vec, small, small],
        out_shape=[_out((t, SSD_INNER), BF), _out((t, CONV_CH), F32),
                   _out((t, 128), BF), _out((1, SSD_INNER), F32),
                   _out((1, SSD_INNER), F32), _out((1, SSD_HEADS), F32),
                   _out((1, SSD_HEADS), F32)],
        scratch_shapes=[pltpu.VMEM((2, SSD_STATE, GROUP_W), F32), pltpu.VMEM((SSD_SUB, CHUNK, SSD_INNER), F32),
                        pltpu.VMEM((SSD_SUB, CHUNK, SSD_INNER), F32)],
        compiler_params=_cp(("arbitrary", "arbitrary"), 56),
    )(*_pin(proj, proj, proj, pre, d_yssd, yssm, h_prev, dt_bias, a_log, dskip_e, g_ssd))


def _grad_w_out(y_pool, y_ssd, d_o):
    t, d = d_o.shape
    tk = POOL_WIDTH
    n_s = SSD_INNER // tk

    def body(p_ref, s_ref, g_ref, o_ref):
        i = pl.program_id(0)

        @pl.when(i == 0)
        def _():
            o_ref[...] = _tn(p_ref[...], g_ref[...]).astype(BF)

        @pl.when(i > 0)
        def _():
            o_ref[...] = _tn(s_ref[...], g_ref[...]).astype(BF)

    return pl.pallas_call(
        body, name="grad_w_out", grid=(1 + n_s,),
        in_specs=[pl.BlockSpec((t, tk), lambda i: (0, 0)), pl.BlockSpec((t, tk), lambda i: (0, jnp.maximum(i - 1, 0))),
                  pl.BlockSpec((t, d), lambda i: (0, 0))],
        out_specs=pl.BlockSpec((tk, d), lambda i: (i, 0)),
        out_shape=_out((POOL_WIDTH + SSD_INNER, d), BF),
        compiler_params=_cp(("parallel",), 40))(*_pin(y_pool, y_ssd, d_o))


def _grad_w_in_t(d_upool, d_z, d_uxbc, d_udt, u1):
    t, d = u1.shape
    tk = 512
    n_z, n_x = SSD_INNER // tk, CONV_CH // tk

    def body(p_ref, z_ref, x_ref, dt_ref, u_ref, o_ref):
        i = pl.program_id(0)

        @pl.when(i == 0)
        def _():
            o_ref[...] = _tn(p_ref[...], u_ref[...]).astype(BF)

        @pl.when((i >= 1) & (i < 1 + n_z))
        def _():
            o_ref[...] = _tn(z_ref[...], u_ref[...]).astype(BF)

        @pl.when((i >= 1 + n_z) & (i < 1 + n_z + n_x))
        def _():
            o_ref[...] = _tn(x_ref[...], u_ref[...]).astype(BF)

        @pl.when(i == 1 + n_z + n_x)
        def _():
            o_ref[0:128, :] = _tn(dt_ref[...], u_ref[...]).astype(BF)

    return pl.pallas_call(
        body, name="grad_w_in", grid=(2 + n_z + n_x,),
        in_specs=[pl.BlockSpec((t, tk), lambda i: (0, 0)),
                  pl.BlockSpec((t, tk), lambda i: (0, jnp.clip(i - 1, 0, n_z - 1))),
                  pl.BlockSpec((t, tk), lambda i: (0, jnp.clip(i - 1 - n_z, 0, n_x - 1))),
                  pl.BlockSpec((t, 128), lambda i: (0, 0)), pl.BlockSpec((t, d), lambda i: (0, 0))],
        out_specs=pl.BlockSpec((tk, d), lambda i: (i, 0)),
        out_shape=_out((IN_PAD, d), BF),
        compiler_params=_cp(("parallel",), 56))(*_pin(d_upool, d_z, d_uxbc, d_udt, u1))


def _conv_bwd(d_pre, proj, conv_w, nb, seq):
    ts = min(256, seq)
    nt = seq // ts
    hb = ts // CONV_HALO
    last_block = nb * seq // CONV_HALO - 1
    n_ext = CHUNK + CONV_HALO

    def body(dp_ref, dnext_ref, u_ref, cw_ref, du_ref, gw_ref, gb_ref):
        b = pl.program_id(0)
        i = pl.program_id(1)

        @pl.when((b == 0) & (i == 0))
        def _():
            gw_ref[...] = jnp.zeros_like(gw_ref)
            gb_ref[...] = jnp.zeros_like(gb_ref)

        for c0 in range(0, CONV_CH, 128):
            cs = slice(c0, c0 + 128)
            cw = cw_ref[:, cs]
            gw = [0.0] * 4
            gb = 0.0
            for r0 in range(0, ts, CHUNK):
                dp = dp_ref[r0:r0 + CHUNK, cs]
                u = u_ref[r0:r0 + CHUNK, cs]
                if r0 + CHUNK < ts:
                    below = dp_ref[r0 + CHUNK:r0 + CHUNK + CONV_HALO, cs]
                else:
                    below = jnp.where(i == nt - 1, 0.0, dnext_ref[:, cs])
                ext_d = jnp.concatenate([dp, below], 0)
                du = dp * cw[3:4]
                gw[3] = gw[3] + jnp.sum(dp * u, 0, keepdims=True)
                for k in (2, 1, 0):
                    shifted = pltpu.roll(ext_d, n_ext - (3 - k), 0)[:CHUNK]
                    du = du + shifted * cw[k:k + 1]
                    gw[k] = gw[k] + jnp.sum(shifted * u, 0, keepdims=True)
                gb = gb + jnp.sum(dp, 0, keepdims=True)
                du_ref[r0:r0 + CHUNK, cs] = du.astype(BF)
            for k in range(4):
                gw_ref[k:k + 1, cs] += gw[k]
            gb_ref[:, cs] += gb

    return pl.pallas_call(
        body, name="conv_bwd", grid=(nb, nt),
        in_specs=[pl.BlockSpec((ts, CONV_CH), lambda b, i: (b * nt + i, 0)),
                  pl.BlockSpec((CONV_HALO, CONV_CH), lambda b, i: (jnp.minimum((b * nt + i + 1) * hb, last_block), 0)),
                  pl.BlockSpec((ts, CONV_CH), lambda b, i: (b * nt + i, 1)),
                  pl.BlockSpec((4, CONV_CH), lambda b, i: (0, 0))],
        out_specs=[pl.BlockSpec((ts, CONV_CH), lambda b, i: (b * nt + i, 0)),
                   pl.BlockSpec((8, CONV_CH), lambda b, i: (0, 0)), pl.BlockSpec((1, CONV_CH), lambda b, i: (0, 0))],
        out_shape=[_out((nb * seq, CONV_CH), BF), _out((8, CONV_CH), F32),
                   _out((1, CONV_CH), F32)],
        compiler_params=_cp(("arbitrary", "arbitrary"), 48))(*_pin(d_pre, d_pre, proj, conv_w))


def _in_proj_bwd(d_parts, w_in_t, x, dh1, mod3, g_mix, seq, token):
    t, d = x.shape
    nb = t // seq
    tm = min(ROW_TILE, seq)
    tps = seq // tm

    widths = [p.shape[1] for p in d_parts]

    def body(d0_ref, d1_ref, d2_ref, d3_ref, w_ref, x_ref, dh1_ref, mod_ref, g_ref, tok_ref, gx_ref, acc_ref, gg_ref):
        i = pl.program_id(0)

        @pl.when(i == 0)
        def _():
            gg_ref[...] = jnp.zeros_like(gg_ref)

        @pl.when(i % tps == 0)
        def _():
            acc_ref[...] = jnp.zeros_like(acc_ref)

        gg = a_shift = a_scale = 0.0
        for rows in _sub_rows(tm):
            d_cat = jnp.concatenate([p_ref[rows, :] for p_ref in (d0_ref, d1_ref, d2_ref)], 1)
            du = _nn(d_cat, w_ref[0:OFF_DT, :]) + _nn(d3_ref[rows, 0:IN_WIDTH - OFF_DT], w_ref[OFF_DT:IN_WIDTH, :])
            xv = x_ref[rows, :]
            r = lax.rsqrt(jnp.mean(xv * xv, -1, keepdims=True) + EPS)
            hh = xv * r
            n1 = hh * g_ref[...]
            dn1 = du * (1.0 + mod_ref[0, 1:2, :])
            dhat = dn1 * g_ref[...]
            gx_ref[rows, :] = dh1_ref[rows, :] + r * (dhat - hh * jnp.mean(dhat * hh, -1, keepdims=True))
            gg = gg + jnp.sum(dn1 * hh, 0, keepdims=True)
            a_shift = a_shift + jnp.sum(du, 0, keepdims=True)
            a_scale = a_scale + jnp.sum(du * n1, 0, keepdims=True)
        gg_ref[...] += gg
        acc_ref[0, 0:1, :] += a_shift
        acc_ref[0, 1:2, :] += a_scale

    row = lambda i: (i, 0)
    vec = pl.BlockSpec((1, d), lambda i: (0, 0))
    return pl.pallas_call(
        body, name="in_proj_bwd", grid=(t // tm,),
        in_specs=[pl.BlockSpec((tm, wd), row) for wd in widths] +
                 [_RESIDENT, pl.BlockSpec((tm, d), row),
                  pl.BlockSpec((tm, d), row), pl.BlockSpec((1, N_MOD, d), lambda i: (i // tps, 0, 0)), vec, _token_spec()],
        out_specs=[pl.BlockSpec((tm, d), row), pl.BlockSpec((1, 8, d), lambda i: (i // tps, 0, 0)), vec],
        out_shape=[_out((t, d), F32), _out((nb, 8, d), F32),
                   _out((1, d), F32)],
        compiler_params=_cp(("arbitrary",), 40))(*_pin(*d_parts), w_in_t, *_pin(x, dh1, mod3, g_mix, token))


_VEC_LAYOUT = (("g_mix", 1024), ("conv_b", 1536), ("g_ssd", 1024), ("pool_scale", 512), ("g_mlp", 1024),
               ("g_final", 1024), ("dt_bias", 128), ("a_log", 128), ("d_skip_lanes", 1024), ("sq_err", 1024))
_VEC_OFFSET = {}
_off = 0
for _name, _n in _VEC_LAYOUT:
    _VEC_OFFSET[_name] = _off
    _off += _n
_SMALL_PARAMS = ("b_ada", "g_mix", "conv_w", "conv_b", "dt_bias", "a_log", "d_skip", "g_ssd", "w_pool", "pool_scale",
                 "g_mlp", "g_final")


def _pack_vec(parts):
    cols = []
    for name, n in _VEC_LAYOUT:
        v = parts[name]
        if v.shape[1] < n:
            v = jnp.pad(v, ((0, 0), (0, n - v.shape[1])))
        cols.append(v)
    return jnp.concatenate(cols, 1)


def _small_adam(vec_all, wpool_all, convw_all, dmod_all, params):
    names = _SMALL_PARAMS
    nin = 4 + 3 * len(names)

    def body(*refs):
        vec_ref, wp_ref, cw_ref, dm_ref = refs[:4]
        prm = {n: refs[4 + 3 * i:7 + 3 * i] for i, n in enumerate(names)}
        loss_ref = refs[nin]
        outs = {n: refs[nin + 1 + 4 * i:nin + 5 + 4 * i] for i, n in enumerate(names)}
        vsum = vec_ref[0]
        for s in range(1, N_DEV):
            vsum = vsum + vec_ref[s]

        def lanes(name, n):
            off = _VEC_OFFSET[name]
            return vsum[:, off:off + n]

        grads = {n: lanes(n, prm[n][0].shape[1]) for n in ("g_mix", "conv_b", "g_ssd", "pool_scale", "g_mlp", "g_final", "dt_bias")}
        grads["a_log"] = lanes("a_log", SSD_HEADS) * (-jnp.exp(prm["a_log"][0][...]))
        per_lane = jnp.broadcast_to(lanes("d_skip_lanes", SSD_INNER), (8, SSD_INNER))
        grads["d_skip"] = _exact_nn(per_lane, _head_reduce_matrix(SSD_INNER, SSD_HEADS))[0:1]
        gwp = wp_ref[0].astype(F32)
        gcw = cw_ref[0]
        gb = jnp.sum(dm_ref[0], 0, keepdims=True)
        for s in range(1, N_DEV):
            gwp = gwp + wp_ref[s].astype(F32)
            gcw = gcw + cw_ref[s]
            gb = gb + jnp.sum(dm_ref[s], 0, keepdims=True)
        grads["w_pool"] = gwp
        grads["conv_w"] = gcw[0:4]
        grads["b_ada"] = gb
        total = jnp.sum(lanes("sq_err", D_MODEL), 1, keepdims=True) * (0.5 / D_MODEL)
        loss_ref[...] = jnp.broadcast_to(total, loss_ref.shape)
        for n in names:
            w_ref, m_ref, v_ref = prm[n]
            g = grads[n]
            d, m2, v2 = _adam_math(w_ref[...], g, m_ref[...], v_ref[...])
            g_ref, d_ref, m2_ref, v2_ref = outs[n]
            g_ref[...] = g
            d_ref[...] = d
            m2_ref[...] = m2
            v2_ref[...] = v2

    flat = [vec_all, wpool_all, convw_all, dmod_all]
    out_shape = [jax.ShapeDtypeStruct((1, 128), F32)]
    for n in names:
        flat += list(params[n])
        out_shape += [jax.ShapeDtypeStruct(params[n][0].shape, F32)] * 4
    vm = pl.BlockSpec(memory_space=pltpu.VMEM)
    res = pl.pallas_call(body, name="small_adam", out_shape=out_shape, in_specs=[vm] * len(flat),
                         out_specs=[vm] * len(out_shape), compiler_params=_cp(vmem_mb=48))(*flat)
    return res[0], {n: res[1 + 4 * i:5 + 4 * i] for i, n in enumerate(names)}


_WEIGHTS = ("w_ada", "b_ada", "g_mix", "w_in", "conv_w", "conv_b", "dt_bias", "a_log", "d_skip", "g_ssd", "w_pool",
            "pool_scale", "w_out", "g_mlp", "w_up", "w_down", "g_final")


def _local_step(x2, tg2, mod3, seq, w_in_t, first_token, weights_arrived, weights_later, start_reduce, before_last,
                conv_w_full, sp):
    t, d = x2.shape
    nb = t // seq
    dskip_e = jnp.repeat(sp["d_skip"], SSD_HEAD_DIM, axis=1)
    proj, u1 = _in_proj(x2, mod3, sp["g_mix"], w_in_t, seq, first_token)
    y_ssd, yssm, h_prev, pre = _ssd_fwd(proj, conv_w_full, sp["conv_b"], sp["dt_bias"], sp["a_log"], dskip_e, sp["g_ssd"], nb, seq)
    y_pool, p = _pool_fwd(proj, sp["w_pool"], sp["pool_scale"], nb, seq, weights_arrived(y_ssd))
    w_out_f, w_up4, w_down4 = weights_later(y_pool)
    w_down_f = w_down4.reshape(D_FF, d)
    h1, o, u2 = _out_proj(y_pool, y_ssd, w_out_f, x2, mod3, sp["g_mlp"], seq)
    a_up = _mlp_up(u2, w_up4)
    d_dn, dh2, sq, gg_final, d_gf = _mlp_down_loss(a_up, w_down_f, h1, mod3, sp["g_final"], tg2, seq)

    gw_down = _tn_matmul(a_up, d_dn, 512, d, "grad_w_down", square_relu=True)
    tok = start_reduce("w_down", gw_down.reshape(N_CHIPS, D_FF // N_CHIPS, d))
    d_a = _mlp_down_bwd(d_dn, w_down4, a_up, tok)
    gw_up4 = _tn_matmul(u2, d_a, 512, d, "grad_w_up", out3=True)
    tok = start_reduce("w_up", gw_up4)
    dh1, d_o, accf, gg_mlp = _mlp_up_bwd(d_a, w_up4, h1, dh2, o, mod3, sp["g_mlp"], seq, tok)
    gw_out = _grad_w_out(y_pool, y_ssd, d_o)
    tok = start_reduce("w_out", gw_out.reshape(N_CHIPS, gw_out.shape[0] // N_CHIPS, d))
    d_ypool, d_yssd = _out_proj_bwd(d_o, w_out_f, tok)
    d_upool, gw_pool, g_ps = _pool_bwd(d_ypool, p, sp["w_pool"], sp["pool_scale"], nb, seq)
    d_z, d_pre, d_udt, gg_ssd, gdsk, ga, gdtb = _ssd_bwd(proj, pre, d_yssd, yssm, h_prev, sp["dt_bias"], sp["a_log"],
                                                        dskip_e, sp["g_ssd"], nb, seq)
    d_uxbc, gconvw, gconvb = _conv_bwd(d_pre, proj, conv_w_full, nb, seq)
    gw_in_t = _grad_w_in_t(d_upool, d_z, d_uxbc, d_udt, u1)
    shard_rows = IN_WIDTH // N_CHIPS
    tok = start_reduce("w_in", jnp.stack([gw_in_t[k * shard_rows:(k + 1) * shard_rows] for k in range(N_CHIPS)]))
    gx, accm, gg_mix = _in_proj_bwd([d_upool, d_z, d_uxbc, d_udt], w_in_t, x2, dh1, mod3, sp["g_mix"], seq, before_last(tok))

    d_mod = jnp.concatenate([accm[:, 0], accm[:, 1], accf[:, 2], accf[:, 0], accf[:, 1], d_gf[:, 0]], 1)
    vec = _pack_vec({"g_mix": gg_mix, "conv_b": gconvb, "g_ssd": gg_ssd, "pool_scale": g_ps, "g_mlp": gg_mlp,
                     "g_final": gg_final, "dt_bias": gdtb, "a_log": ga, "d_skip_lanes": gdsk, "sq_err": sq})
    return gx, d_mod, vec, gw_pool, gconvw


def kernel(x, c, w_ada, b_ada, g_mix, w_in, conv_w, conv_b, dt_bias, a_log, d_skip, g_ssd, w_pool, pool_scale, w_out, g_mlp, w_up, w_down, g_final, loss_target, m_w_ada, m_b_ada, m_g_mix, m_w_in, m_conv_w, m_conv_b, m_dt_bias, m_a_log, m_d_skip, m_g_ssd, m_w_pool, m_pool_scale, m_w_out, m_g_mlp, m_w_up, m_w_down, m_g_final, v_w_ada, v_b_ada, v_g_mix, v_w_in, v_conv_w, v_conv_b, v_dt_bias, v_a_log, v_d_skip, v_g_ssd, v_w_pool, v_pool_scale, v_w_out, v_g_mlp, v_w_up, v_w_down, v_g_final):
    nb, seq, d = x.shape
    t = nb * seq
    xi, yi, ci = _mesh_pos()
    chip = 2 * xi + yi
    me = 4 * xi + 2 * yi + ci
    ada_cols = w_ada.shape[2]
    conv_cols = conv_w.shape[2]
    in_cols = w_in.shape[2]
    w_in_s, m_w_in_s, v_w_in_s = w_in[0].T, m_w_in[0].T, v_w_in[0].T

    c_send, c_recv, c_src, c_land, c_token = _exchange_start([c, conv_w[0]], ALL_PEERS, "cond_start")
    w_in_b = w_in_s.astype(BF)
    i_send, i_recv, i_src, i_land, in_token = _ici_start(
        [w_in_b], [jax.ShapeDtypeStruct((N_CHIPS,) + w_in_b.shape, BF)], _gather_sent, _gather_landing, "gather_start_w_in",
        after=c_token)
    c_own, c_got = _exchange_wait(c_send, c_recv, c_src, c_land, ALL_PEERS, in_token, "cond_wait")
    c8, convw8 = [lax.dynamic_update_slice(got, mine[None], (me,) + (0,) * mine.ndim) for got, mine in zip(c_got, c_own)]
    c_all = c8.reshape(N_DEV * nb, d)
    conv_w_full = convw8[0::2].transpose(1, 0, 2).reshape(4, N_CHIPS * conv_cols)
    b_shard = lax.dynamic_slice(b_ada, (0, chip * ada_cols), (1, ada_cols))
    mod_part, c_act = _ada_mod(c_all, w_ada[0], b_shard, in_token)
    mod_rows = mod_part.reshape(N_DEV, nb, ada_cols)
    m_send, m_recv, m_src, m_land, _ = _ici_start(
        [mod_rows], [jax.ShapeDtypeStruct((N_CHIPS, nb, ada_cols), F32)], _mod_sent, _mod_landing, "mod_start", after=mod_part)

    later = [w_out[0].astype(BF), w_up[0].astype(BF), w_down[0].astype(BF)]
    in_shard, in_land = _ici_wait(i_send, i_recv, i_src, i_land, [m_src[0]] + later, _gather_sent, _gather_landing,
                                  "gather_wait_w_in")
    (w_in4,) = _gather_finish(in_land, in_shard)
    w_in_t = w_in4.reshape(N_CHIPS * in_cols, d)
    mod_mine, mod_land = _ici_wait(m_send, m_recv, m_src, m_land, w_in_t, _mod_sent, _mod_landing, "mod_wait")
    mod_own = lax.dynamic_slice(mod_mine[0], (me, 0, 0), (1, nb, ada_cols))
    mod4 = lax.dynamic_update_slice(mod_land[0], mod_own, (chip, 0, 0))
    mod3 = mod4.transpose(1, 0, 2).reshape(nb, N_MOD, d)
    g_send, g_recv, g_src, g_land, first_token = _ici_start(
        later, [jax.ShapeDtypeStruct((N_CHIPS,) + s.shape, BF) for s in later], _gather_sent, _gather_landing, "gather_start",
        after=w_in4)

    def weights_arrived(after):
        shards, lands = _ici_wait(g_send, g_recv, g_src, g_land, after, _gather_sent, _gather_landing, "gather_wait")
        pending["forward"] = _forward_start(lands, "forward_start") + (shards,)
        return pending["forward"][3]

    def weights_later(after):
        f_send, f_recv, f_land, _, shards = pending["forward"]
        lands = _forward_wait(f_send, f_recv, f_land, after, "forward_wait")
        w_out4, w_up4, w_down4 = [lax.dynamic_update_slice(land, shard[None], (chip, 0, 0)) for land, shard in zip(lands, shards)]
        return w_out4.reshape(N_CHIPS * w_out.shape[1], d), w_up4, w_down4

    pending = {}

    def start_reduce(name, grad4):
        pending[name] = _reduce_start(grad4, "reduce_start_" + name)
        return pending[name][4]

    pos = jnp.stack([ci, chip, me]).astype(jnp.int32)
    early = ("w_out", "w_up", "w_down")

    def summed_half(name, after):
        r_send, r_recv, r_src, r_land, _ = pending[name]
        own, recv = _reduce_wait(r_send, r_recv, r_src, r_land, after, "reduce_wait_" + name)
        return _sum_eight(recv, own, pos)

    def before_last(token):
        pending["early_halves"] = _exchange_start([summed_half(n, token) for n in early], SIBLING, "halves_start_early")
        return pending["early_halves"][4]

    sp = dict(g_mix=g_mix, conv_b=conv_b, dt_bias=dt_bias, a_log=a_log, d_skip=d_skip, g_ssd=g_ssd,
              w_pool=w_pool[0], pool_scale=pool_scale, g_mlp=g_mlp, g_final=g_final.reshape(1, d))
    gx, d_mod, vec, gw_pool, gconvw = _local_step(
        x.reshape(t, d), loss_target.reshape(t, d), mod3, seq, w_in_t, first_token, weights_arrived, weights_later, start_reduce,
        before_last, conv_w_full, sp)

    small_parts = [vec, gw_pool.reshape(4 * POOL_GROUP, POOL_GROUP).astype(BF), gconvw, d_mod]
    s_send, s_recv, s_src, s_land, s_token = _exchange_start(small_parts, ALL_PEERS, "small_start")
    h_send, h_recv, h_src, h_land, h_token = _exchange_start([summed_half("w_in", s_token)], SIBLING, "halves_start")
    e_send, e_recv, e_src, e_land, _ = pending["early_halves"]
    e_own, e_got = _exchange_wait(e_send, e_recv, e_src, e_land, SIBLING, h_token, "halves_wait_early")
    res = {}
    for i, (n, w, m, v) in enumerate((("w_out", w_out, m_w_out, v_w_out), ("w_up", w_up, m_w_up, v_w_up),
                                      ("w_down", w_down, m_w_down, v_w_down))):
        g, dl, m2, v2 = _adam_big(e_own[i], e_got[i], w[0], m[0], v[0], pos)
        res[n] = (g[None], dl[None], m2[None], v2[None])

    s_own, s_got = _exchange_wait(s_send, s_recv, s_src, s_land, ALL_PEERS, res["w_down"][1], "small_wait")
    vec8, wpool8, convw8g, dmod8 = [lax.dynamic_update_slice(got, mine[None], (me,) + (0,) * mine.ndim)
                                    for got, mine in zip(s_got, s_own)]
    convw8s = lax.dynamic_slice(convw8g, (0, 0, chip * conv_cols), (N_DEV, 8, conv_cols))
    m_in = dict(b_ada=m_b_ada, g_mix=m_g_mix, conv_w=m_conv_w[0], conv_b=m_conv_b, dt_bias=m_dt_bias, a_log=m_a_log,
                d_skip=m_d_skip, g_ssd=m_g_ssd, w_pool=m_w_pool.reshape(4 * POOL_GROUP, POOL_GROUP), pool_scale=m_pool_scale,
                g_mlp=m_g_mlp, g_final=m_g_final.reshape(1, d))
    v_in = dict(b_ada=v_b_ada, g_mix=v_g_mix, conv_w=v_conv_w[0], conv_b=v_conv_b, dt_bias=v_dt_bias, a_log=v_a_log,
                d_skip=v_d_skip, g_ssd=v_g_ssd, w_pool=v_w_pool.reshape(4 * POOL_GROUP, POOL_GROUP), pool_scale=v_pool_scale,
                g_mlp=v_g_mlp, g_final=v_g_final.reshape(1, d))
    w_small = dict(sp, b_ada=b_ada, conv_w=conv_w[0], w_pool=w_pool.reshape(4 * POOL_GROUP, POOL_GROUP))
    loss_row, small = _small_adam(vec8, wpool8, convw8s, dmod8, {n: (w_small[n], m_in[n], v_in[n]) for n in _SMALL_PARAMS})

    dmod_all = dmod8.reshape(N_DEV * nb, N_CHIPS * ada_cols)
    dmod_cols = lax.dynamic_slice(dmod_all, (0, chip * ada_cols), (N_DEV * nb, ada_cols))
    res.update({n: tuple(r.reshape(w.shape) for r in small[n])
                for n, w in (("b_ada", b_ada), ("g_mix", g_mix), ("conv_w", conv_w), ("conv_b", conv_b), ("dt_bias", dt_bias),
                             ("a_log", a_log), ("d_skip", d_skip), ("g_ssd", g_ssd), ("w_pool", w_pool),
                             ("pool_scale", pool_scale), ("g_mlp", g_mlp), ("g_final", g_final))})
    g_ada, d_ada, m_ada, v_ada = _adam_ada(c_act.T.astype(BF), dmod_cols, w_ada[0], m_w_ada[0], v_w_ada[0])
    res["w_ada"] = (g_ada[None], d_ada[None], m_ada[None], v_ada[None])
    h_own, h_got = _exchange_wait(h_send, h_recv, h_src, h_land, SIBLING, g_ada, "halves_wait")
    rows3 = lambda a: jnp.transpose(a, (2, 0, 1))
    res["w_in"] = tuple(jnp.transpose(r, (1, 2, 0))
                        for r in _adam_rows(h_own[0], h_got[0], rows3(w_in), rows3(m_w_in), rows3(v_w_in), pos))

    loss = loss_row[0, 0]
    return (loss, gx.reshape(nb, seq, d), *[res[n][0] for n in _WEIGHTS], *[res[n][1] for n in _WEIGHTS],
            *[res[n][2] for n in _WEIGHTS], *[res[n][3] for n in _WEIGHTS])
```

```python
import jax
import jax.numpy as jnp
from jax import lax
from jax.experimental import pallas as pl
from jax.experimental.pallas import tpu as pltpu

F32 = jnp.float32
BF = jnp.bfloat16
MESH = pl.DeviceIdType.MESH

EPS = 1e-5
D_MODEL = 1024
POOL_WIDTH = 512
POOL_WINDOWS = (2, 4, 8, 16)
POOL_GROUP = 128
SSD_INNER = 1024
SSD_HEADS = 16
SSD_HEAD_DIM = 64
SSD_STATE = 128
GROUP_W = 512
CHUNK = 128
CONV_CH = 1536
OFF_DT = 3072
IN_WIDTH = 3088
IN_PAD = 3200
D_FF = 4096
N_MOD = 6
N_CHIPS = 4
N_DEV = 8
HALO = 16
CONV_HALO = 8

ADAM_LR = 0.001
ADAM_B1 = 0.9
ADAM_B2 = 0.999
ADAM_EPS = 1e-08
ADAM_WD = 0.01
ADAM_STEP = 10

VMEM_BYTES_V7X = 64 * 1024 * 1024


def _cp(semantics=None, vmem_mb=48, **kw):
    assert vmem_mb * 1024 * 1024 < VMEM_BYTES_V7X
    args = dict(vmem_limit_bytes=vmem_mb * 1024 * 1024, **kw)
    if semantics is not None:
        args["dimension_semantics"] = semantics
    return pltpu.CompilerParams(**args)


def _out(shape, dtype):
    return pltpu.HBM(shape, dtype)


def _pin(*arrays):
    return [pltpu.with_memory_space_constraint(a, pltpu.HBM) for a in arrays]


TOKEN_SHAPE = (8, 128)


def _order_operand(a):
    if a.shape == TOKEN_SHAPE:
        return pl.BlockSpec(memory_space=pltpu.VMEM), a
    return pl.BlockSpec(memory_space=pltpu.HBM), pltpu.with_memory_space_constraint(a, pltpu.HBM)


def _nn(a, b):
    return jnp.dot(a, b, preferred_element_type=F32)


def _nt(a, b):
    return lax.dot_general(a, b, (((1,), (1,)), ((), ())), preferred_element_type=F32)


def _tn(a, b):
    return lax.dot_general(a, b, (((0,), (0,)), ((), ())), preferred_element_type=F32)


def _split3(v):
    hi = v.astype(BF)
    r1 = v - hi.astype(F32)
    mid = r1.astype(BF)
    lo = (r1 - mid.astype(F32)).astype(BF)
    return hi, mid, lo


def _exact_nn(v, m01):
    hi, mid, lo = _split3(v)
    return _nn(hi, m01) + _nn(mid, m01) + _nn(lo, m01)


def _exact_nn_left(m01, v):
    hi, mid, lo = _split3(v)
    return _nn(m01, hi) + _nn(m01, mid) + _nn(m01, lo)


def _exact_nt_left(m01, v):
    hi, mid, lo = _split3(v)
    return _nt(m01, hi) + _nt(m01, mid) + _nt(m01, lo)


def _sigmoid(v):
    return 1.0 / (1.0 + jnp.exp(-v))


def _iota(shape, dim):
    return lax.broadcasted_iota(jnp.int32, shape, dim)


def _head_expand_matrix(heads, width):
    return (_iota((heads, width), 1) // SSD_HEAD_DIM == _iota((heads, width), 0)).astype(BF)


def _head_reduce_matrix(width, heads):
    return (_iota((width, heads), 0) // SSD_HEAD_DIM == _iota((width, heads), 1)).astype(BF)


def _mesh_pos():
    return lax.axis_index("x"), lax.axis_index("y"), lax.axis_index("c")


def _flip(v, bit):
    return v + bit - 2 * bit * v


_HBM = pl.BlockSpec(memory_space=pltpu.HBM)
_SEM = pl.BlockSpec(memory_space=pltpu.SEMAPHORE)
_DATAFLOW = pltpu.SideEffectType.DATAFLOW_SIDE_EFFECTING


def _peer_chip(x, y, j):
    return _flip(x, (j >> 1) & 1), _flip(y, j & 1)


def _ici_start(srcs, land_shapes, sent, landing, name, after):
    n = len(srcs)

    def body(*refs):
        src_refs, land_refs = refs[:n], refs[n:2 * n]
        send_sems, recv_sems = refs[2 * n + 1], refs[2 * n + 2]
        token = refs[-1]
        x, y, c = _mesh_pos()
        for j in range(1, N_CHIPS):
            px, py = _peer_chip(x, y, j)
            for a in range(n):
                pltpu.make_async_remote_copy(
                    src_ref=sent(src_refs[a], c, 2 * px + py), dst_ref=landing(land_refs[a], c, 2 * x + y),
                    send_sem=send_sems.at[a * (N_CHIPS - 1) + j - 1], recv_sem=recv_sems.at[a * (N_CHIPS - 1) + j - 1],
                    device_id=(px, py, c), device_id_type=MESH).start()
        token[...] = jnp.zeros_like(token)

    sems = pltpu.SemaphoreType.DMA((n * (N_CHIPS - 1),))
    after_spec, after = _order_operand(after)
    lands = [pltpu.with_memory_space_constraint(lax.empty(s.shape, s.dtype), pltpu.HBM) for s in land_shapes]
    outs = pl.pallas_call(
        body, name=name,
        out_shape=(sems, sems, *[pltpu.HBM(s.shape, s.dtype) for s in srcs],
                   *[pltpu.HBM(s.shape, s.dtype) for s in land_shapes], jax.ShapeDtypeStruct((8, 128), F32)),
        in_specs=[_HBM] * (2 * n) + [after_spec],
        out_specs=[_SEM, _SEM] + [_HBM] * (2 * n) + [pl.BlockSpec(memory_space=pltpu.VMEM)],
        input_output_aliases={i: 2 + i for i in range(2 * n)},
        compiler_params=pltpu.CompilerParams(has_side_effects=_DATAFLOW),
    )(*_pin(*srcs), *lands, after)
    return outs[0], outs[1], outs[2:2 + n], outs[2 + n:2 + 2 * n], outs[-1]


def _ici_wait(send_sems, recv_sems, src_thru, land_thru, after, sent, landing, name):
    n = len(src_thru)
    afters = [_order_operand(a) for a in (after if isinstance(after, (list, tuple)) else [after])]

    def body(*refs):
        src_refs, land_refs = refs[:n], refs[n:2 * n]
        send_sems, recv_sems = refs[2 * n], refs[2 * n + 1]
        x, y, c = _mesh_pos()
        for j in range(1, N_CHIPS):
            px, py = _peer_chip(x, y, j)
            for a in range(n):
                cp = pltpu.make_async_remote_copy(
                    src_ref=sent(src_refs[a], c, 2 * px + py), dst_ref=landing(land_refs[a], c, 2 * px + py),
                    send_sem=send_sems.at[a * (N_CHIPS - 1) + j - 1], recv_sem=recv_sems.at[a * (N_CHIPS - 1) + j - 1],
                    device_id=(px, py, c), device_id_type=MESH)
                cp.wait_send()
                cp.wait_recv()

    outs = pl.pallas_call(
        body, name=name,
        out_shape=tuple(pltpu.HBM(s.shape, s.dtype) for s in (*src_thru, *land_thru)),
        in_specs=[_HBM] * (2 * n) + [_SEM, _SEM] + [s for s, _ in afters], out_specs=[_HBM] * (2 * n),
        input_output_aliases={i: i for i in range(2 * n)},
        compiler_params=pltpu.CompilerParams(has_side_effects=_DATAFLOW),
    )(*src_thru, *land_thru, send_sems, recv_sems, *[a for _, a in afters])
    return outs[:n], outs[n:]


def _col_half(ref, which, lead=()):
    hc = ref.shape[-1] // 2
    return ref.at[(*lead, slice(None), pl.ds(pl.multiple_of(which * hc, 128), hc))]


def _gather_sent(ref, c, dst_chip):
    return _col_half(ref, c)


def _gather_landing(ref, c, src_chip):
    return _col_half(ref, c, lead=(src_chip,))


def _mod_sent(ref, c, dst_chip):
    return ref.at[2 * dst_chip + c]


def _mod_landing(ref, c, src_chip):
    return ref.at[src_chip]


def _reduce_copy(src_ref, land_ref, send_sems, recv_sems, k, receiving):
    x, y, c = _mesh_pos()
    px, py, pc = _flip(x, (k >> 2) & 1), _flip(y, (k >> 1) & 1), _flip(c, k & 1)
    hc = src_ref.shape[2] // 2
    src = src_ref.at[2 * px + py, :, pl.ds(pl.multiple_of(pc * hc, 128), hc)]
    slot = (4 * px + 2 * py + pc) if receiving else (4 * x + 2 * y + c)
    return pltpu.make_async_remote_copy(
        src_ref=src, dst_ref=land_ref.at[slot], send_sem=send_sems.at[k - 1], recv_sem=recv_sems.at[k - 1],
        device_id=(px, py, pc), device_id_type=MESH)


def _reduce_start(grad4, name):
    k4, r, cols = grad4.shape

    def body(src_ref, land_ref, send_sems, recv_sems, src_thru, land_thru, token):
        for k in range(1, N_DEV):
            _reduce_copy(src_ref, land_ref, send_sems, recv_sems, k, receiving=False).start()
        token[...] = jnp.zeros_like(token)

    sems = pltpu.SemaphoreType.DMA((N_DEV - 1,))
    land = pltpu.with_memory_space_constraint(lax.empty((N_DEV, r, cols // 2), grad4.dtype), pltpu.HBM)
    return pl.pallas_call(
        body, name=name,
        out_shape=(sems, sems, pltpu.HBM(grad4.shape, grad4.dtype), pltpu.HBM(land.shape, land.dtype),
                   jax.ShapeDtypeStruct((8, 128), F32)),
        in_specs=[_HBM, _HBM], out_specs=[_SEM, _SEM, _HBM, _HBM, pl.BlockSpec(memory_space=pltpu.VMEM)],
        input_output_aliases={0: 2, 1: 3},
        compiler_params=pltpu.CompilerParams(has_side_effects=_DATAFLOW),
    )(*_pin(grad4), land)


def _reduce_wait(send_sems, recv_sems, src_thru, land_thru, after, name):
    def body(src_ref, land_ref, send_sems, recv_sems, after_ref, src_out, land_out):
        for k in range(1, N_DEV):
            cp = _reduce_copy(src_ref, land_ref, send_sems, recv_sems, k, receiving=True)
            cp.wait_send()
            cp.wait_recv()

    return pl.pallas_call(
        body, name=name,
        out_shape=(pltpu.HBM(src_thru.shape, src_thru.dtype), pltpu.HBM(land_thru.shape, land_thru.dtype)),
        in_specs=[_HBM, _HBM, _SEM, _SEM, _order_operand(after)[0]], out_specs=[_HBM, _HBM],
        input_output_aliases={0: 0, 1: 1},
        compiler_params=pltpu.CompilerParams(has_side_effects=_DATAFLOW),
    )(src_thru, land_thru, send_sems, recv_sems, _order_operand(after)[1])


SIBLING_COLLECTIVE_ID = 0
_SIBLING_PARAMS = pltpu.CompilerParams(has_side_effects=_DATAFLOW, collective_id=SIBLING_COLLECTIVE_ID)


def _sibling_handshake():
    x, y, c = _mesh_pos()
    barrier = pltpu.get_barrier_semaphore()
    pl.semaphore_signal(barrier, inc=1, device_id=(x, y, 1 - c), device_id_type=MESH)
    pl.semaphore_wait(barrier, 1)


def _peer_copy(src_ref, land_ref, send_sems, recv_sems, idx, k, receiving):
    x, y, c = _mesh_pos()
    px, py, pc = _flip(x, (k >> 2) & 1), _flip(y, (k >> 1) & 1), _flip(c, k & 1)
    if land_ref.shape[0] == N_DEV:
        slot = (4 * px + 2 * py + pc) if receiving else (4 * x + 2 * y + c)
    else:
        slot = pc if receiving else c
    return pltpu.make_async_remote_copy(
        src_ref=src_ref, dst_ref=land_ref.at[slot], send_sem=send_sems.at[idx], recv_sem=recv_sems.at[idx],
        device_id=(px, py, pc), device_id_type=MESH)


def _exchange_start(arrays, peers, name):
    n = len(arrays)

    def body(*refs):
        src_refs, land_refs = refs[:n], refs[n:2 * n]
        send_sems, recv_sems = refs[2 * n], refs[2 * n + 1]
        token = refs[-1]
        if peers == SIBLING:
            _sibling_handshake()
        for j, k in enumerate(peers):
            for a in range(n):
                _peer_copy(src_refs[a], land_refs[a], send_sems, recv_sems, a * len(peers) + j, k, receiving=False).start()
        token[...] = jnp.zeros_like(token)

    sems = pltpu.SemaphoreType.DMA((n * len(peers),))
    n_slots = N_DEV if len(peers) > 1 else 2
    lands = [pltpu.with_memory_space_constraint(lax.empty((n_slots,) + a.shape, a.dtype), pltpu.HBM) for a in arrays]
    outs = pl.pallas_call(
        body, name=name,
        out_shape=(sems, sems, *[pltpu.HBM(a.shape, a.dtype) for a in arrays], *[pltpu.HBM(l.shape, l.dtype) for l in lands],
                   jax.ShapeDtypeStruct((8, 128), F32)),
        in_specs=[_HBM] * (2 * n), out_specs=[_SEM, _SEM] + [_HBM] * (2 * n) + [pl.BlockSpec(memory_space=pltpu.VMEM)],
        input_output_aliases={i: 2 + i for i in range(2 * n)},
        compiler_params=_SIBLING_PARAMS if peers == SIBLING else pltpu.CompilerParams(has_side_effects=_DATAFLOW),
    )(*_pin(*arrays), *lands)
    return outs[0], outs[1], outs[2:2 + n], outs[2 + n:2 + 2 * n], outs[-1]


def _exchange_wait(send_sems, recv_sems, src_thru, land_thru, peers, after, name):
    n = len(src_thru)

    def body(*refs):
        src_refs, land_refs = refs[:n], refs[n:2 * n]
        send_sems, recv_sems = refs[2 * n], refs[2 * n + 1]
        for j, k in enumerate(peers):
            for a in range(n):
                cp = _peer_copy(src_refs[a], land_refs[a], send_sems, recv_sems, a * len(peers) + j, k, receiving=True)
                cp.wait_send()
                cp.wait_recv()

    outs = pl.pallas_call(
        body, name=name,
        out_shape=tuple(pltpu.HBM(s.shape, s.dtype) for s in (*src_thru, *land_thru)),
        in_specs=[_HBM] * (2 * n) + [_SEM, _SEM, _order_operand(after)[0]], out_specs=[_HBM] * (2 * n),
        input_output_aliases={i: i for i in range(2 * n)},
        compiler_params=pltpu.CompilerParams(has_side_effects=_DATAFLOW),
    )(*src_thru, *land_thru, send_sems, recv_sems, _order_operand(after)[1])
    return outs[:n], outs[n:]


ALL_PEERS = tuple(range(1, N_DEV))
SIBLING = (1,)


def _sum_eight(recv, grad4, pos):
    n, r, hc = recv.shape
    steps = 2
    tc = hc // steps

    def body(pos_ref, r_ref, g_ref, o_ref):
        me = pos_ref[2]
        o_ref[...] = jnp.zeros_like(o_ref)
        for s in range(n):
            @pl.when(me == s)
            def _():
                o_ref[...] += g_ref[0].astype(F32)

            @pl.when(me != s)
            def _():
                o_ref[...] += r_ref[s].astype(F32)

    grid_spec = pltpu.PrefetchScalarGridSpec(
        num_scalar_prefetch=1, grid=(steps,),
        in_specs=[pl.BlockSpec((n, r, tc), lambda i, pos: (0, 0, i)),
                  pl.BlockSpec((1, r, tc), lambda i, pos: (pos[1], 0, pos[0] * steps + i))],
        out_specs=pl.BlockSpec((r, tc), lambda i, pos: (0, i)))
    return pl.pallas_call(body, name="sum_eight", grid_spec=grid_spec, out_shape=_out((r, hc), F32),
                          compiler_params=_cp(("parallel",), 32))(pos, *_pin(recv, grad4))


def _forward_copy(land_ref, send_sems, recv_sems, idx, j, receiving):
    x, y, c = _mesh_pos()
    px, py = _peer_chip(x, y, j)
    mine = _col_half(land_ref, c, lead=(2 * px + py,))
    theirs = _col_half(land_ref, 1 - c, lead=(2 * px + py,))
    return pltpu.make_async_remote_copy(
        src_ref=mine, dst_ref=theirs if receiving else mine, send_sem=send_sems.at[idx], recv_sem=recv_sems.at[idx],
        device_id=(x, y, 1 - c), device_id_type=MESH)


def _forward_start(lands, name):
    n = len(lands)

    def body(*refs):
        land_refs, send_sems, recv_sems, token = refs[:n], refs[n], refs[n + 1], refs[-1]
        _sibling_handshake()
        for j in range(1, N_CHIPS):
            for a in range(n):
                _forward_copy(land_refs[a], send_sems, recv_sems, a * (N_CHIPS - 1) + j - 1, j, receiving=False).start()
        token[...] = jnp.zeros_like(token)

    sems = pltpu.SemaphoreType.DMA((n * (N_CHIPS - 1),))
    outs = pl.pallas_call(
        body, name=name,
        out_shape=(sems, sems, *[pltpu.HBM(l.shape, l.dtype) for l in lands], jax.ShapeDtypeStruct((8, 128), F32)),
        in_specs=[_HBM] * n, out_specs=[_SEM, _SEM] + [_HBM] * n + [pl.BlockSpec(memory_space=pltpu.VMEM)],
        input_output_aliases={i: 2 + i for i in range(n)},
        compiler_params=_SIBLING_PARAMS,
    )(*lands)
    return outs[0], outs[1], outs[2:2 + n], outs[-1]


def _forward_wait(send_sems, recv_sems, lands_thru, after, name):
    n = len(lands_thru)

    def body(*refs):
        land_refs, send_sems, recv_sems = refs[:n], refs[n], refs[n + 1]
        for j in range(1, N_CHIPS):
            for a in range(n):
                cp = _forward_copy(land_refs[a], send_sems, recv_sems, a * (N_CHIPS - 1) + j - 1, j, receiving=True)
                cp.wait_send()
                cp.wait_recv()

    return pl.pallas_call(
        body, name=name,
        out_shape=tuple(pltpu.HBM(l.shape, l.dtype) for l in lands_thru),
        in_specs=[_HBM] * n + [_SEM, _SEM, _order_operand(after)[0]], out_specs=[_HBM] * n,
        input_output_aliases={i: i for i in range(n)},
        compiler_params=pltpu.CompilerParams(has_side_effects=_DATAFLOW),
    )(*lands_thru, send_sems, recv_sems, _order_operand(after)[1])


def _gather_finish(lands, shards):
    n = len(lands)
    any_spec = _HBM

    def body(*refs):
        shard_refs, out_refs = refs[n:2 * n], refs[2 * n:3 * n]
        send_sems, recv_sems, local_sems = refs[3 * n:]
        x, y, c = _mesh_pos()
        chip = 2 * x + y
        local, sends = [], []
        for a in range(n):
            cp = pltpu.make_async_copy(shard_refs[a], out_refs[a].at[chip], local_sems.at[a])
            cp.start()
            local.append(cp)
        for j in range(1, N_CHIPS):
            px, py = _peer_chip(x, y, j)
            for a in range(n):
                landed = _col_half(out_refs[a], c, lead=(2 * px + py,))
                cp = pltpu.make_async_remote_copy(
                    src_ref=landed, dst_ref=landed, send_sem=send_sems.at[a, j], recv_sem=recv_sems.at[a, j],
                    device_id=(x, y, 1 - c), device_id_type=MESH)
                cp.start()
                sends.append(cp)
        for j in range(1, N_CHIPS):
            px, py = _peer_chip(x, y, j)
            for a in range(n):
                other = _col_half(out_refs[a], 1 - c, lead=(2 * px + py,))
                pltpu.make_async_remote_copy(
                    src_ref=other, dst_ref=other, send_sem=send_sems.at[a, j], recv_sem=recv_sems.at[a, j],
                    device_id=(x, y, 1 - c), device_id_type=MESH).wait_recv()
        for cp in sends:
            cp.wait_send()
        for cp in local:
            cp.wait()

    return pl.pallas_call(
        body, name="gather_finish",
        out_shape=[_out(l.shape, l.dtype) for l in lands],
        in_specs=[any_spec] * (2 * n), out_specs=[any_spec] * n,
        input_output_aliases={i: i for i in range(n)},
        scratch_shapes=[pltpu.SemaphoreType.DMA((n, N_CHIPS))] * 2 + [pltpu.SemaphoreType.DMA((n,))],
        compiler_params=_cp(vmem_mb=16),
    )(*lands, *shards)


def _adam_math(w, g, m, v):
    m2 = ADAM_B1 * m + (1.0 - ADAM_B1) * g
    v2 = ADAM_B2 * v + (1.0 - ADAM_B2) * (g * g)
    m_hat = m2 / (1.0 - ADAM_B1 ** ADAM_STEP)
    v_hat = v2 / (1.0 - ADAM_B2 ** ADAM_STEP)
    delta = -ADAM_LR * (m_hat / (jnp.sqrt(v_hat) + ADAM_EPS) + ADAM_WD * w)
    return delta, m2, v2


def _adam_big(g_own, g_pair, w, m, v, pos):
    r, c = w.shape
    per_half = 2
    tc = c // (2 * per_half)

    def body(pos_ref, go_ref, gp_ref, w_ref, m_ref, v_ref, g_ref, d_ref, m2_ref, v2_ref):
        half = pl.program_id(0) // per_half
        g = jnp.where(half == pos_ref[0], go_ref[...], gp_ref[0])
        d, m2, v2 = _adam_math(w_ref[...], g, m_ref[...], v_ref[...])
        g_ref[...] = g
        d_ref[...] = d
        m2_ref[...] = m2
        v2_ref[...] = v2

    spec = pl.BlockSpec((r, tc), lambda i, pos: (0, i))
    grid_spec = pltpu.PrefetchScalarGridSpec(
        num_scalar_prefetch=1, grid=(2 * per_half,),
        in_specs=[pl.BlockSpec((r, tc), lambda i, pos: (0, i % per_half)),
                  pl.BlockSpec((1, r, tc), lambda i, pos: (1 - pos[0], 0, i % per_half)), spec, spec, spec],
        out_specs=[spec] * 4)
    sh = _out((r, c), F32)
    return pl.pallas_call(body, name="adam_big", grid_spec=grid_spec, out_shape=[sh] * 4,
                          compiler_params=_cp(("parallel",), 32))(pos, *_pin(g_own, g_pair, w, m, v))


def _adam_rows(g_own, g_pair, w3, m3, v3, pos):
    r, _, c = w3.shape
    tr = 128

    def body(pos_ref, go_ref, gp_ref, w_ref, m_ref, v_ref, g_ref, d_ref, m2_ref, v2_ref):
        core = pos_ref[0]
        g = jnp.concatenate([jnp.where(core == 0, go_ref[...], gp_ref[0]), jnp.where(core == 1, go_ref[...], gp_ref[0])], 1)
        d, m2, v2 = _adam_math(w_ref[:, 0, :], g, m_ref[:, 0, :], v_ref[:, 0, :])
        g_ref[:, 0, :] = g
        d_ref[:, 0, :] = d
        m2_ref[:, 0, :] = m2
        v2_ref[:, 0, :] = v2

    spec = pl.BlockSpec((tr, 1, c), lambda i, pos: (i, 0, 0))
    grid_spec = pltpu.PrefetchScalarGridSpec(
        num_scalar_prefetch=1, grid=(pl.cdiv(r, tr),),
        in_specs=[pl.BlockSpec((tr, c // 2), lambda i, pos: (i, 0)),
                  pl.BlockSpec((1, tr, c // 2), lambda i, pos: (1 - pos[0], i, 0)), spec, spec, spec],
        out_specs=[spec] * 4)
    sh = _out(w3.shape, F32)
    return pl.pallas_call(body, name="adam_rows", grid_spec=grid_spec, out_shape=[sh] * 4,
                          compiler_params=_cp(("parallel",), 32))(pos, *_pin(g_own, g_pair, w3, m3, v3))


def _adam_ada(c_act_t, dmod_cols, w, m, v):
    r, c = w.shape
    tc = 512

    def body(ct_ref, dm_ref, w_ref, m_ref, v_ref, g_ref, d_ref, m2_ref, v2_ref):
        g = _nn(ct_ref[...], dm_ref[...].astype(BF))
        d, m2, v2 = _adam_math(w_ref[...], g, m_ref[...], v_ref[...])
        g_ref[...] = g
        d_ref[...] = d
        m2_ref[...] = m2
        v2_ref[...] = v2

    spec = pl.BlockSpec((r, tc), lambda i: (0, i))
    sh = _out((r, c), F32)
    return pl.pallas_call(
        body, name="adam_ada", grid=(c // tc,),
        in_specs=[pl.BlockSpec(c_act_t.shape, lambda i: (0, 0)), pl.BlockSpec((dmod_cols.shape[0], tc), lambda i: (0, i)),
                  spec, spec, spec],
        out_specs=[spec] * 4, out_shape=[sh] * 4, compiler_params=_cp(("parallel",), 48))(*_pin(c_act_t, dmod_cols, w, m, v))


def _ada_mod(c_all, w_shard, b_shard, token):
    nb, d = c_all.shape
    cols = w_shard.shape[1]
    tc = 512

    def body(c_ref, w_ref, b_ref, tok_ref, mod_ref, act_ref):
        cv = c_ref[...]
        act = cv * _sigmoid(cv)
        act_ref[...] = act
        mod_ref[...] = _nn(act.astype(BF), w_ref[...].astype(BF)) + b_ref[...]

    return pl.pallas_call(
        body, name="ada_mod", grid=(cols // tc,),
        in_specs=[pl.BlockSpec((nb, d), lambda i: (0, 0)), pl.BlockSpec((d, tc), lambda i: (0, i)),
                  pl.BlockSpec((1, tc), lambda i: (0, i)), _token_spec()],
        out_specs=[pl.BlockSpec((nb, tc), lambda i: (0, i)), pl.BlockSpec((nb, d), lambda i: (0, 0))],
        out_shape=[_out((nb, cols), F32), _out((nb, d), F32)],
        compiler_params=_cp(("arbitrary",), 32))(*_pin(c_all, w_shard, b_shard, token))


SUB_ROWS = 256
ROW_TILE = 512


def _sub_rows(tm):
    return [slice(s, s + SUB_ROWS) for s in range(0, tm, SUB_ROWS)] if tm > SUB_ROWS else [slice(0, tm)]


_RESIDENT = pl.BlockSpec(memory_space=pltpu.VMEM)


def _token_spec():
    return pl.BlockSpec((8, 128), lambda *_: (0, 0))


def _in_proj(x, mod3, g_mix, w_in_t, seq, token):
    t, d = x.shape
    tm = min(ROW_TILE, seq)
    tps = seq // tm

    def body(x_ref, mod_ref, g_ref, w_ref, tok_ref, proj_ref, u1_ref):
        for rows in _sub_rows(tm):
            xv = x_ref[rows, :]
            r = lax.rsqrt(jnp.mean(xv * xv, -1, keepdims=True) + EPS)
            u = (xv * r * g_ref[...]) * (1.0 + mod_ref[0, 1:2, :]) + mod_ref[0, 0:1, :]
            ub = u.astype(BF)
            u1_ref[rows, :] = ub
            proj_ref[rows, 0:OFF_DT] = _nt(ub, w_ref[0:OFF_DT, :])
            proj_ref[rows, OFF_DT:IN_PAD] = jnp.zeros((rows.stop - rows.start, IN_PAD - OFF_DT), F32)
            proj_ref[rows, OFF_DT:IN_WIDTH] = _nt(ub, w_ref[OFF_DT:IN_WIDTH, :])

    return pl.pallas_call(
        body, name="in_proj", grid=(t // tm,),
        in_specs=[pl.BlockSpec((tm, d), lambda i: (i, 0)), pl.BlockSpec((1, N_MOD, d), lambda i: (i // tps, 0, 0)),
                  pl.BlockSpec((1, d), lambda i: (0, 0)), _RESIDENT, _token_spec()],
        out_specs=[pl.BlockSpec((tm, IN_PAD), lambda i: (i, 0)), pl.BlockSpec((tm, d), lambda i: (i, 0))],
        out_shape=[_out((t, IN_PAD), F32), _out((t, d), BF)],
        compiler_params=_cp(("parallel",), 40))(*_pin(x, mod3, g_mix), w_in_t, *_pin(token))


def _pool_tile(seq):
    return min(1024, seq)


def _pool_fwd(proj, w_pool, pool_scale, nb, seq, token):
    ts = _pool_tile(seq)
    nt = seq // ts

    def body(u_ref, halo_ref, wp_ref, ps_ref, tok_ref, yp_ref, p_ref):
        i = pl.program_id(1)
        halo = jnp.where(i == 0, 0.0, halo_ref[...])
        u = u_ref[...]
        ext = jnp.concatenate([halo, u], 0)
        tpos = i * ts + _iota((ts, 1), 0)
        for g, w in enumerate(POOL_WINDOWS):
            gs = slice(g * POOL_GROUP, (g + 1) * POOL_GROUP)
            s = ext[:, gs]
            sh = 1
            while sh < w:
                s = s + pltpu.roll(s, sh, 0)
                sh *= 2
            cnt = jnp.minimum(tpos + 1, w).astype(F32)
            pb = (s[HALO:] / cnt - u[:, gs]).astype(BF)
            p_ref[:, gs] = pb
            yp_ref[:, gs] = (_nn(pb, wp_ref[g].astype(BF)) * ps_ref[:, gs]).astype(BF)

    hb = ts // HALO
    return pl.pallas_call(
        body, name="pool_fwd", grid=(nb, nt),
        in_specs=[pl.BlockSpec((ts, POOL_WIDTH), lambda b, i: (b * nt + i, 0)),
                  pl.BlockSpec((HALO, POOL_WIDTH), lambda b, i: (jnp.maximum((b * nt + i) * hb - 1, 0), 0)),
                  pl.BlockSpec((4, POOL_GROUP, POOL_GROUP), lambda b, i: (0, 0, 0)),
                  pl.BlockSpec((1, POOL_WIDTH), lambda b, i: (0, 0)), _token_spec()],
        out_specs=[pl.BlockSpec((ts, POOL_WIDTH), lambda b, i: (b * nt + i, 0))] * 2,
        out_shape=[_out((nb * seq, POOL_WIDTH), BF)] * 2,
        compiler_params=_cp(("parallel", "parallel"), 32))(*_pin(proj, proj, w_pool, pool_scale, token))


def _conv_pre(uxbc, halo, cw, cb, first):
    halo = jnp.where(first, 0.0, halo)
    ext = jnp.concatenate([halo, uxbc], 0)
    pre = cb + uxbc * cw[3:4]
    for k in (2, 1, 0):
        pre = pre + pltpu.roll(ext, 3 - k, 0)[CONV_HALO:] * cw[k:k + 1]
    return pre


def _chunk_terms(pre, udt, dtb, alog):
    sg = _sigmoid(pre)
    xbc = pre * sg
    dtp = udt[:, :SSD_HEADS] + dtb
    dt = jnp.maximum(dtp, 0.0) + jnp.log(1.0 + jnp.exp(-jnp.abs(dtp)))
    a = -jnp.exp(alog)
    da = dt * a
    tril = (_iota((CHUNK, CHUNK), 0) >= _iota((CHUNK, CHUNK), 1))
    acum = _exact_nn_left(tril.astype(BF), da)
    eye = (_iota((SSD_HEADS, SSD_HEADS), 0) == _iota((SSD_HEADS, SSD_HEADS), 1)).astype(BF)
    acum_t = _exact_nt_left(eye, acum)
    expand = _head_expand_matrix(SSD_HEADS, SSD_INNER)
    acum_e = _exact_nn(acum, expand)
    dt_e = _exact_nn(dt, expand)
    last_e = acum_e[CHUNK - 1:CHUNK]
    return dict(pre=pre, sg=sg, xbc=xbc, dtp=dtp, dt=dt, a=a, acum=acum, acum_t=acum_t, tril=tril,
                dt_e=dt_e, e_a=jnp.exp(acum_e), d_out=jnp.exp(last_e - acum_e), c_dec=jnp.exp(last_e))


def _head_decay(r, h):
    seg = r["acum"][:, h:h + 1] - r["acum_t"][h:h + 1, :]
    return jnp.where(r["tril"], jnp.exp(jnp.minimum(seg, 0.0)), 0.0)


SSD_SUB = 4
SSD_ROWS = SSD_SUB * CHUNK


def _ssd_specs(nb, seq, reverse):
    ns = seq // SSD_ROWS
    per = seq // CONV_HALO

    def cidx(c):
        return (ns - 1 - c) if reverse else c

    def row(b, c):
        return b * ns + cidx(c)

    specs = [
        pl.BlockSpec((SSD_ROWS, CONV_CH), lambda b, c: (row(b, c), 1)),
        pl.BlockSpec((CONV_HALO, CONV_CH),
                     lambda b, c: (jnp.maximum(b * per + cidx(c) * (SSD_ROWS // CONV_HALO) - 1, 0), 1)),
        pl.BlockSpec((SSD_ROWS, GROUP_W), lambda b, c: (row(b, c), 1)),
        pl.BlockSpec((SSD_ROWS, GROUP_W), lambda b, c: (row(b, c), 2)),
        pl.BlockSpec((SSD_ROWS, 128), lambda b, c: (row(b, c), OFF_DT // 128)),
    ]
    return specs, row, cidx, ns


def _const_spec(shape):
    return pl.BlockSpec(shape, lambda b, c: (0,) * len(shape))


def _ssd_fwd(proj, conv_w, conv_b, dt_bias, a_log, dskip_e, g_ssd, nb, seq):
    specs, row, cidx, ns = _ssd_specs(nb, seq, reverse=False)

    def body(uxbc_ref, halo_ref, z0_ref, z1_ref, udt_ref, cw_ref, cb_ref, dtb_ref, alog_ref, dsk_ref, gs_ref,
             yssd_ref, yssm_ref, hprev_ref, pre_ref, h_ref, yd_ref):
        c = pl.program_id(1)

        @pl.when(c == 0)
        def _():
            h_ref[...] = jnp.zeros_like(h_ref)

        for sub in range(SSD_SUB):
            rows = slice(sub * CHUNK, (sub + 1) * CHUNK)
            if sub == 0:
                halo, first = halo_ref[...], c == 0
            else:
                halo, first = uxbc_ref[sub * CHUNK - CONV_HALO:sub * CHUNK, :], False
            pre = _conv_pre(uxbc_ref[rows, :], halo, cw_ref[...], cb_ref[...], first)
            pre_ref[rows, :] = pre
            r = _chunk_terms(pre, udt_ref[rows, :], dtb_ref[...], alog_ref[...])
            xbc = r["xbc"]
            xs = xbc[:, :SSD_INNER]
            xdt = xs * r["dt_e"]
            xdt_b = xdt.astype(BF)
            xdo_b = (xdt * r["d_out"]).astype(BF)
            hprev_ref[0, sub] = h_ref[...]
            for g in range(2):
                gs = slice(g * GROUP_W, (g + 1) * GROUP_W)
                bg = xbc[:, SSD_INNER + g * SSD_STATE:SSD_INNER + (g + 1) * SSD_STATE].astype(BF)
                cg = xbc[:, SSD_INNER + (2 + g) * SSD_STATE:SSD_INNER + (3 + g) * SSD_STATE].astype(BF)
                scores = _nt(cg, bg)
                hg = h_ref[g]
                y_off = _nn(cg, hg.astype(BF)) * r["e_a"][:, gs]
                for hh in range(8):
                    h = g * 8 + hh
                    hs = slice(h * SSD_HEAD_DIM, (h + 1) * SSD_HEAD_DIM)
                    m = (scores * _head_decay(r, h)).astype(BF)
                    yd_ref[sub, :, hs] = _nn(m, xdt_b[:, hs])
                h_ref[g] = hg * r["c_dec"][:, gs] + _tn(bg, xdo_b[:, gs])
                y = yd_ref[sub, :, gs] + y_off + dsk_ref[:, gs] * xs[:, gs]
                yssm_ref[rows, gs] = y
                zg = (z0_ref if g == 0 else z1_ref)[rows, :]
                yg = y * (zg * _sigmoid(zg))
                rg = lax.rsqrt(jnp.mean(yg * yg, -1, keepdims=True) + EPS)
                yssd_ref[rows, gs] = (yg * rg * gs_ref[:, gs]).astype(BF)

    t = nb * seq
    return pl.pallas_call(
        body, name="ssd_fwd", grid=(nb, ns),
        in_specs=specs + [_const_spec((4, CONV_CH)), _const_spec((1, CONV_CH)), _const_spec((1, SSD_HEADS)),
                          _const_spec((1, SSD_HEADS)), _const_spec((1, SSD_INNER)), _const_spec((1, SSD_INNER))],
        out_specs=[pl.BlockSpec((SSD_ROWS, SSD_INNER), lambda b, c: (row(b, c), 0)),
                   pl.BlockSpec((SSD_ROWS, SSD_INNER), lambda b, c: (row(b, c), 0)),
                   pl.BlockSpec((1, SSD_SUB, 2, SSD_STATE, GROUP_W), lambda b, c: (b, c, 0, 0, 0)),
                   pl.BlockSpec((SSD_ROWS, CONV_CH), lambda b, c: (row(b, c), 0))],
        out_shape=[_out((t, SSD_INNER), BF), _out((t, SSD_INNER), F32),
                   _out((nb, seq // CHUNK, 2, SSD_STATE, GROUP_W), F32), _out((t, CONV_CH), F32)],
        scratch_shapes=[pltpu.VMEM((2, SSD_STATE, GROUP_W), F32), pltpu.VMEM((SSD_SUB, CHUNK, SSD_INNER), F32)],
        compiler_params=_cp(("arbitrary", "arbitrary"), 56),
    )(*_pin(proj, proj, proj, proj, proj, conv_w, conv_b, dt_bias, a_log, dskip_e, g_ssd))


def _out_proj(y_pool, y_ssd, w_out, x, mod3, g_mlp, seq):
    t, d = x.shape
    tm = 512
    tps = seq // tm if seq >= tm else 1
    tm = min(tm, seq)

    def body(yp_ref, ys_ref, w_ref, x_ref, mod_ref, g_ref, h1_ref, o_ref, u2_ref):
        for rows in _sub_rows(tm):
            o = _nn(jnp.concatenate([yp_ref[rows, :], ys_ref[rows, :]], 1), w_ref[...])
            o_ref[rows, :] = o.astype(BF)
            h1 = x_ref[rows, :] + mod_ref[0, 2:3, :] * o
            h1_ref[rows, :] = h1
            r = lax.rsqrt(jnp.mean(h1 * h1, -1, keepdims=True) + EPS)
            u2_ref[rows, :] = ((h1 * r * g_ref[...]) * (1.0 + mod_ref[0, 4:5, :]) + mod_ref[0, 3:4, :]).astype(BF)

    row = lambda i: (i, 0)
    return pl.pallas_call(
        body, name="out_proj", grid=(t // tm,),
        in_specs=[pl.BlockSpec((tm, POOL_WIDTH), row), pl.BlockSpec((tm, SSD_INNER), row),
                  _RESIDENT, pl.BlockSpec((tm, d), row),
                  pl.BlockSpec((1, N_MOD, d), lambda i: (i // tps, 0, 0)), pl.BlockSpec((1, d), lambda i: (0, 0))],
        out_specs=[pl.BlockSpec((tm, d), row)] * 3,
        out_shape=[_out((t, d), F32), _out((t, d), BF), _out((t, d), BF)],
        compiler_params=_cp(("parallel",), 48))(*_pin(y_pool, y_ssd), w_out, *_pin(x, mod3, g_mlp))


def _mlp_up(u2, w_up4):
    t, d = u2.shape
    tm = min(1024, t)
    nk, _, cols = w_up4.shape

    def body(u_ref, w_ref, a_ref):
        a_ref[...] = _nn(u_ref[...], w_ref[pl.program_id(1)]).astype(BF)

    return pl.pallas_call(
        body, name="mlp_up", grid=(t // tm, nk),
        in_specs=[pl.BlockSpec((tm, d), lambda i, k: (i, 0)), _RESIDENT],
        out_specs=pl.BlockSpec((tm, cols), lambda i, k: (i, k)),
        out_shape=_out((t, nk * cols), BF),
        compiler_params=_cp(("parallel", "parallel"), 32))(*_pin(u2), w_up4)


def _mlp_down_loss(a_up, w_down, h1, mod3, g_final, target, seq):
    t, d = h1.shape
    nb = t // seq
    tm = min(ROW_TILE, seq)
    tps = seq // tm

    def body(a_ref, w_ref, h1_ref, mod_ref, g_ref, tg_ref, ddn_ref, dh2_ref, sq_ref, gg_ref, dgf_ref):
        i = pl.program_id(0)

        @pl.when(i == 0)
        def _():
            sq_ref[...] = jnp.zeros_like(sq_ref)
            gg_ref[...] = jnp.zeros_like(gg_ref)

        @pl.when(i % tps == 0)
        def _():
            dgf_ref[...] = jnp.zeros_like(dgf_ref)

        gate = mod_ref[0, 5:6, :]
        sq = gg = dgf = 0.0
        for rows in _sub_rows(tm):
            f = jnp.square(jnp.maximum(a_ref[rows, :], 0))
            dn = _nn(f, w_ref[...])
            h2 = h1_ref[rows, :] + gate * dn
            r = lax.rsqrt(jnp.mean(h2 * h2, -1, keepdims=True) + EPS)
            hh = h2 * r
            err = hh * g_ref[...] - tg_ref[rows, :]
            dy = err * (1.0 / d)
            dhat = dy * g_ref[...]
            dh2 = r * (dhat - hh * jnp.mean(dhat * hh, -1, keepdims=True))
            dh2_ref[rows, :] = dh2
            ddn_ref[rows, :] = (dh2 * gate).astype(BF)
            sq = sq + jnp.sum(err * err, 0, keepdims=True)
            gg = gg + jnp.sum(dy * hh, 0, keepdims=True)
            dgf = dgf + jnp.sum(dh2 * dn, 0, keepdims=True)
        sq_ref[...] += sq
        gg_ref[...] += gg
        dgf_ref[0] += dgf

    row = lambda i: (i, 0)
    vec = pl.BlockSpec((1, d), lambda i: (0, 0))
    return pl.pallas_call(
        body, name="mlp_down_loss", grid=(t // tm,),
        in_specs=[pl.BlockSpec((tm, D_FF), row), _RESIDENT, pl.BlockSpec((tm, d), row),
                  pl.BlockSpec((1, N_MOD, d), lambda i: (i // tps, 0, 0)), vec, pl.BlockSpec((tm, d), row)],
        out_specs=[pl.BlockSpec((tm, d), row), pl.BlockSpec((tm, d), row), vec, vec,
                   pl.BlockSpec((1, 1, d), lambda i: (i // tps, 0, 0))],
        out_shape=[_out((t, d), BF), _out((t, d), F32), _out((1, d), F32),
                   _out((1, d), F32), _out((nb, 1, d), F32)],
        compiler_params=_cp(("arbitrary",), 44))(*_pin(a_up), w_down, *_pin(h1, mod3, g_final, target))


def _tn_matmul(a, b, tk, tn, name, square_relu=False, out3=False):
    t, kdim = a.shape
    ndim = b.shape[1]

    def body(a_ref, b_ref, o_ref):
        av = a_ref[...]
        if square_relu:
            av = jnp.square(jnp.maximum(av, 0))
        res = _tn(av, b_ref[...]).astype(BF)
        if out3:
            o_ref[0] = res
        else:
            o_ref[...] = res

    if out3:
        out_spec = pl.BlockSpec((1, tk, tn), lambda j, i: (j, i, 0))
        out_shape = _out((ndim // tn, kdim, tn), BF)
    else:
        out_spec = pl.BlockSpec((tk, tn), lambda j, i: (i, j))
        out_shape = _out((kdim, ndim), BF)
    return pl.pallas_call(
        body, name=name, grid=(ndim // tn, kdim // tk),
        in_specs=[pl.BlockSpec((t, tk), lambda j, i: (0, i)), pl.BlockSpec((t, tn), lambda j, i: (0, j))],
        out_specs=out_spec, out_shape=out_shape,
        compiler_params=_cp(("parallel", "parallel"), 56))(*_pin(a, b))


def _mlp_down_bwd(d_dn, w_down4, a_up, token):
    t, d = d_dn.shape
    tm = min(1024, t)
    nk, rows, _ = w_down4.shape

    def body(g_ref, w_ref, a_ref, tok_ref, o_ref):
        df = _nt(g_ref[...], w_ref[pl.program_id(1)])
        o_ref[...] = (df * (2.0 * jnp.maximum(a_ref[...], 0).astype(F32))).astype(BF)

    return pl.pallas_call(
        body, name="mlp_down_bwd", grid=(t // tm, nk),
        in_specs=[pl.BlockSpec((tm, d), lambda i, k: (i, 0)), _RESIDENT,
                  pl.BlockSpec((tm, rows), lambda i, k: (i, k)), _token_spec()],
        out_specs=pl.BlockSpec((tm, rows), lambda i, k: (i, k)),
        out_shape=_out((t, nk * rows), BF),
        compiler_params=_cp(("parallel", "parallel"), 32))(*_pin(d_dn), w_down4, *_pin(a_up, token))


def _mlp_up_bwd(d_a, w_up4, h1, dh2, o, mod3, g_mlp, seq, token):
    t, d = h1.shape
    nb = t // seq
    tm = min(ROW_TILE, seq)
    tps = seq // tm
    nk = w_up4.shape[0]
    cols = w_up4.shape[2]

    def body(da_ref, w_ref, h1_ref, dh2_ref, o_ref, mod_ref, g_ref, tok_ref, dh1_ref, do_ref, acc_ref, gg_ref):
        i = pl.program_id(0)

        @pl.when(i == 0)
        def _():
            gg_ref[...] = jnp.zeros_like(gg_ref)

        @pl.when(i % tps == 0)
        def _():
            acc_ref[...] = jnp.zeros_like(acc_ref)

        gg = a_shift = a_scale = a_gate = 0.0
        for rows in _sub_rows(tm):
            du = _nt(da_ref[rows, 0:cols], w_ref[0])
            for k in range(1, nk):
                du = du + _nt(da_ref[rows, k * cols:(k + 1) * cols], w_ref[k])
            h1 = h1_ref[rows, :]
            r = lax.rsqrt(jnp.mean(h1 * h1, -1, keepdims=True) + EPS)
            hh = h1 * r
            n2 = hh * g_ref[...]
            dn2 = du * (1.0 + mod_ref[0, 4:5, :])
            dhat = dn2 * g_ref[...]
            dh1 = dh2_ref[rows, :] + r * (dhat - hh * jnp.mean(dhat * hh, -1, keepdims=True))
            dh1_ref[rows, :] = dh1
            do_ref[rows, :] = (dh1 * mod_ref[0, 2:3, :]).astype(BF)
            gg = gg + jnp.sum(dn2 * hh, 0, keepdims=True)
            a_shift = a_shift + jnp.sum(du, 0, keepdims=True)
            a_scale = a_scale + jnp.sum(du * n2, 0, keepdims=True)
            a_gate = a_gate + jnp.sum(dh1 * o_ref[rows, :].astype(F32), 0, keepdims=True)
        gg_ref[...] += gg
        acc_ref[0, 0:1, :] += a_shift
        acc_ref[0, 1:2, :] += a_scale
        acc_ref[0, 2:3, :] += a_gate

    row = lambda i: (i, 0)
    vec = pl.BlockSpec((1, d), lambda i: (0, 0))
    return pl.pallas_call(
        body, name="mlp_up_bwd", grid=(t // tm,),
        in_specs=[pl.BlockSpec((tm, D_FF), row), _RESIDENT, pl.BlockSpec((tm, d), row),
                  pl.BlockSpec((tm, d), row), pl.BlockSpec((tm, d), row),
                  pl.BlockSpec((1, N_MOD, d), lambda i: (i // tps, 0, 0)), vec, _token_spec()],
        out_specs=[pl.BlockSpec((tm, d), row), pl.BlockSpec((tm, d), row),
                   pl.BlockSpec((1, 8, d), lambda i: (i // tps, 0, 0)), vec],
        out_shape=[_out((t, d), F32), _out((t, d), BF),
                   _out((nb, 8, d), F32), _out((1, d), F32)],
        compiler_params=_cp(("arbitrary",), 44))(*_pin(d_a), w_up4, *_pin(h1, dh2, o, mod3, g_mlp, token))


def _out_proj_bwd(d_o, w_out, token):
    t, d = d_o.shape
    tm = min(1024, t)

    def body(g_ref, w_ref, tok_ref, dp_ref, ds_ref):
        gv = g_ref[...]
        dp_ref[...] = _nt(gv, w_ref[0:POOL_WIDTH, :])
        ds_ref[...] = _nt(gv, w_ref[POOL_WIDTH:, :])

    row = lambda i: (i, 0)
    return pl.pallas_call(
        body, name="out_proj_bwd", grid=(t // tm,),
        in_specs=[pl.BlockSpec((tm, d), row), _RESIDENT, _token_spec()],
        out_specs=[pl.BlockSpec((tm, POOL_WIDTH), row), pl.BlockSpec((tm, SSD_INNER), row)],
        out_shape=[_out((t, POOL_WIDTH), F32), _out((t, SSD_INNER), F32)],
        compiler_params=_cp(("parallel",), 32))(*_pin(d_o), w_out, *_pin(token))


def _pool_bwd(d_ypool, p, w_pool, pool_scale, nb, seq):
    ts = _pool_tile(seq)
    nt = seq // ts
    hb = ts // HALO
    last_block = nb * seq // HALO - 1

    def body(dy_ref, halo_ref, p_ref, wp_ref, ps_ref, du_ref, gw_ref, gs_ref):
        b = pl.program_id(0)
        i = pl.program_id(1)

        @pl.when((b == 0) & (i == 0))
        def _():
            gw_ref[...] = jnp.zeros_like(gw_ref)
            gs_ref[...] = jnp.zeros_like(gs_ref)

        halo = jnp.where(i == nt - 1, 0.0, halo_ref[...])
        dy = dy_ref[...]
        ext = jnp.concatenate([dy, halo], 0)
        tpos = i * ts + _iota((ts + HALO, 1), 0)
        n_ext = ts + HALO
        for g, w in enumerate(POOL_WINDOWS):
            gs = slice(g * POOL_GROUP, (g + 1) * POOL_GROUP)
            wg = wp_ref[g].astype(BF)
            pg = p_ref[:, gs]
            pw = _nn(pg, wg)
            gs_ref[:, gs] += jnp.sum(dy[:, gs] * pw, 0, keepdims=True)
            dpw = (ext[:, gs] * ps_ref[:, gs]).astype(BF)
            gw_ref[g] += _tn(pg, dpw[:ts])
            dp = _nt(dpw, wg)
            cnt = jnp.minimum(tpos + 1, w).astype(F32)
            s = dp / cnt
            sh = 1
            while sh < w:
                s = s + pltpu.roll(s, n_ext - sh, 0)
                sh *= 2
            du_ref[:, gs] = (s[:ts] - dp[:ts]).astype(BF)

    return pl.pallas_call(
        body, name="pool_bwd", grid=(nb, nt),
        in_specs=[pl.BlockSpec((ts, POOL_WIDTH), lambda b, i: (b * nt + i, 0)),
                  pl.BlockSpec((HALO, POOL_WIDTH), lambda b, i: (jnp.minimum((b * nt + i + 1) * hb, last_block), 0)),
                  pl.BlockSpec((ts, POOL_WIDTH), lambda b, i: (b * nt + i, 0)),
                  pl.BlockSpec((4, POOL_GROUP, POOL_GROUP), lambda b, i: (0, 0, 0)),
                  pl.BlockSpec((1, POOL_WIDTH), lambda b, i: (0, 0))],
        out_specs=[pl.BlockSpec((ts, POOL_WIDTH), lambda b, i: (b * nt + i, 0)),
                   pl.BlockSpec((4, POOL_GROUP, POOL_GROUP), lambda b, i: (0, 0, 0)),
                   pl.BlockSpec((1, POOL_WIDTH), lambda b, i: (0, 0))],
        out_shape=[_out((nb * seq, POOL_WIDTH), BF), _out((4, POOL_GROUP, POOL_GROUP), F32),
                   _out((1, POOL_WIDTH), F32)],
        compiler_params=_cp(("arbitrary", "arbitrary"), 32))(*_pin(d_ypool, d_ypool, p, w_pool, pool_scale))


def _ssd_bwd(proj, pre, d_yssd, yssm, h_prev, dt_bias, a_log, dskip_e, g_ssd, nb, seq):
    specs, row, cidx, ns = _ssd_specs(nb, seq, reverse=True)
    specs = specs[2:]

    def body(z0_ref, z1_ref, udt_ref, pre_ref, dys_ref, yssm_ref, hprev_ref,
             dtb_ref, alog_ref, dsk_ref, gs_ref,
             dz_ref, dpre_ref, dudt_ref, ggs_ref, gdsk_ref, ga_ref, gdtb_ref,
             g_ref, dxdt_ref, dyv_ref):
        b = pl.program_id(0)
        c = pl.program_id(1)

        @pl.when(c == 0)
        def _():
            g_ref[...] = jnp.zeros_like(g_ref)

        @pl.when((b == 0) & (c == 0))
        def _():
            ggs_ref[...] = jnp.zeros_like(ggs_ref)
            gdsk_ref[...] = jnp.zeros_like(gdsk_ref)
            ga_ref[...] = jnp.zeros_like(ga_ref)
            gdtb_ref[...] = jnp.zeros_like(gdtb_ref)

        for sub in reversed(range(SSD_SUB)):
            chunk(sub, z0_ref, z1_ref, udt_ref, pre_ref, dys_ref, yssm_ref, hprev_ref, dtb_ref, alog_ref, dsk_ref, gs_ref,
                  dz_ref, dpre_ref, dudt_ref, ggs_ref, gdsk_ref, ga_ref, gdtb_ref, g_ref, dxdt_ref.at[sub], dyv_ref.at[sub])

    def chunk(sub, z0_ref, z1_ref, udt_ref, pre_ref, dys_ref, yssm_ref, hprev_ref,
              dtb_ref, alog_ref, dsk_ref, gs_ref,
              dz_ref, dpre_ref, dudt_ref, ggs_ref, gdsk_ref, ga_ref, gdtb_ref,
              g_ref, dxdt_ref, dyv_ref):
        rows = slice(sub * CHUNK, (sub + 1) * CHUNK)
        r = _chunk_terms(pre_ref[rows, :], udt_ref[rows, :], dtb_ref[...], alog_ref[...])
        xbc = r["xbc"]
        xs = xbc[:, :SSD_INNER]
        dt_e = r["dt_e"]
        xdt = xs * dt_e
        xdt_b = xdt.astype(BF)
        reduce_m = _head_reduce_matrix(GROUP_W, 8)

        def head_sums(v):
            return _nn(v.astype(BF), reduce_m)

        onehot16 = lambda h: (_iota((1, SSD_HEADS), 1) == h).astype(F32)
        onecol16 = lambda h: (_iota((SSD_HEADS, 1), 0) == h).astype(F32)

        d_acum = jnp.zeros((CHUNK, SSD_HEADS), F32)
        d_acum_t = jnp.zeros((SSD_HEADS, CHUNK), F32)
        d_alast = jnp.zeros((1, SSD_HEADS), F32)
        place8 = lambda g: (_iota((8, SSD_HEADS), 1) == _iota((8, SSD_HEADS), 0) + 8 * g).astype(BF)
        d_b, d_c = [], []
        for g in range(2):
            gs = slice(g * GROUP_W, (g + 1) * GROUP_W)
            zg = (z0_ref if g == 0 else z1_ref)[rows, :]
            sz = _sigmoid(zg)
            silu_z = zg * sz
            ys = yssm_ref[rows, gs]
            yg = ys * silu_z
            rg = lax.rsqrt(jnp.mean(yg * yg, -1, keepdims=True) + EPS)
            yh = yg * rg
            dys = dys_ref[rows, gs]
            ggs_ref[:, gs] += jnp.sum(dys * yh, 0, keepdims=True)
            dyh = dys * gs_ref[:, gs]
            dyg = rg * (dyh - yh * jnp.mean(dyh * yh, -1, keepdims=True))
            dy = dyg * silu_z
            dz_ref[rows, gs] = (dyg * ys * (sz * (1.0 + zg * (1.0 - sz)))).astype(BF)
            gdsk_ref[:, gs] += jnp.sum(dy * xs[:, gs], 0, keepdims=True)
            dyv_ref[:, gs] = dy
            dy_b = dy.astype(BF)

            bg = xbc[:, SSD_INNER + g * SSD_STATE:SSD_INNER + (g + 1) * SSD_STATE].astype(BF)
            cg = xbc[:, SSD_INNER + (2 + g) * SSD_STATE:SSD_INNER + (3 + g) * SSD_STATE].astype(BF)
            scores = _nt(cg, bg)
            hg = hprev_ref[0, sub, g]
            hg_b = hg.astype(BF)
            gg = g_ref[g]
            gg_b = gg.astype(BF)
            e_a = r["e_a"][:, gs]
            d_out = r["d_out"][:, gs]
            c_dec = r["c_dec"][:, gs]
            zc = _nn(cg, hg_b)
            wv = e_a * dy
            wv_b = wv.astype(BF)
            da_g = head_sums(wv * zc)
            dcg = _nt(wv_b, hg_b)
            d_hprev = _tn(cg, wv_b)
            vg = _nn(bg, gg_b)
            dxdt_g = d_out * vg
            dd_out = head_sums(xdt[:, gs] * vg)
            dbg = _nt((xdt[:, gs] * d_out).astype(BF), gg_b)
            dcd = _exact_nn(jnp.sum(gg * hg, 0, keepdims=True), reduce_m)
            d_out8 = jnp.exp(r["acum"][CHUNK - 1:CHUNK, 8 * g:8 * g + 8] - r["acum"][:, 8 * g:8 * g + 8])
            c_dec8 = jnp.exp(r["acum"][CHUNK - 1:CHUNK, 8 * g:8 * g + 8])
            t8 = dd_out * d_out8
            d_alast = d_alast + _exact_nn(jnp.sum(t8, 0, keepdims=True) + dcd * c_dec8, place8(g))
            d_acum = d_acum + _exact_nn(da_g - t8, place8(g))
            dsc = jnp.zeros((CHUNK, CHUNK), F32)
            for hh in range(8):
                h = g * 8 + hh
                hs = slice(h * SSD_HEAD_DIM, (h + 1) * SSD_HEAD_DIM)
                lam = _head_decay(r, h)
                m = scores * lam
                dyh_b = dy_b[:, hh * SSD_HEAD_DIM:(hh + 1) * SSD_HEAD_DIM]
                dm = _nt(dyh_b, xdt_b[:, hs])
                tm_ = dm * m
                d_acum = d_acum + jnp.sum(tm_, 1, keepdims=True) * onehot16(h)
                d_acum_t = d_acum_t + onecol16(h) * jnp.sum(tm_, 0, keepdims=True)
                dsc = dsc + dm * lam
                dxdt_ref[:, hs] = _tn(m.astype(BF), dyh_b) + dxdt_g[:, hh * SSD_HEAD_DIM:(hh + 1) * SSD_HEAD_DIM]
            dsc_b = dsc.astype(BF)
            d_c.append(dcg + _nn(dsc_b, bg))
            d_b.append(dbg + _tn(dsc_b, cg))
            g_ref[g] = d_hprev + c_dec * gg

        eye = (_iota((CHUNK, CHUNK), 0) == _iota((CHUNK, CHUNK), 1)).astype(BF)
        d_acum = d_acum - _exact_nt_left(eye, d_acum_t)
        is_last = (_iota((CHUNK, 1), 0) == CHUNK - 1).astype(F32)
        d_acum = d_acum + is_last * d_alast
        triu = (_iota((CHUNK, CHUNK), 0) <= _iota((CHUNK, CHUNK), 1)).astype(BF)
        d_da = _exact_nn_left(triu, d_acum)
        dt = r["dt"]
        ga_ref[...] += jnp.sum(d_da * dt, 0, keepdims=True)
        dxdt = dxdt_ref[...]
        reduce16 = _head_reduce_matrix(SSD_INNER, SSD_HEADS)
        d_dt = d_da * r["a"] + _nn((dxdt * xs).astype(BF), reduce16)
        d_udt = d_dt * _sigmoid(r["dtp"])
        gdtb_ref[...] += jnp.sum(d_udt, 0, keepdims=True)
        dudt_ref[rows, :] = jnp.zeros((CHUNK, dudt_ref.shape[1]), BF)
        dudt_ref[rows, 0:SSD_HEADS] = d_udt.astype(BF)
        pre, sg = r["pre"], r["sg"]
        dsilu = sg * (1.0 + pre * (1.0 - sg))
        dpre_ref[rows, 0:SSD_INNER] = (dsk_ref[...] * dyv_ref[...] + dxdt * dt_e) * dsilu[:, 0:SSD_INNER]
        for g in range(2):
            bs = slice(SSD_INNER + g * SSD_STATE, SSD_INNER + (g + 1) * SSD_STATE)
            cs = slice(SSD_INNER + (2 + g) * SSD_STATE, SSD_INNER + (3 + g) * SSD_STATE)
            dpre_ref[rows, bs] = d_b[g] * dsilu[:, bs]
            dpre_ref[rows, cs] = d_c[g] * dsilu[:, cs]

    t = nb * seq
    vec = _const_spec((1, SSD_INNER))
    small = _const_spec((1, SSD_HEADS))
    return pl.pallas_call(
        body, name="ssd_bwd", grid=(nb, ns),
        in_specs=specs + [pl.BlockSpec((SSD_ROWS, CONV_CH), lambda b, c: (row(b, c), 0)),
                          pl.BlockSpec((SSD_ROWS, SSD_INNER), lambda b, c: (row(b, c), 0)),
                          pl.BlockSpec((SSD_ROWS, SSD_INNER), lambda b, c: (row(b, c), 0)),
                          pl.BlockSpec((1, SSD_SUB, 2, SSD_STATE, GROUP_W), lambda b, c: (b, cidx(c), 0, 0, 0)),
                          small, small, vec, vec],
        out_specs=[pl.BlockSpec((SSD_ROWS, SSD_INNER), lambda b, c: (row(b, c), 0)),
                   pl.BlockSpec((SSD_ROWS, CONV_CH), lambda b, c: (row(b, c), 0)),
                   pl.BlockSpec((SSD_ROWS, 128), lambda b, c: (row(b, c), 0)),
                   vec, vec, small, small],
        out_shape=[_out((t, SSD_INNER), BF), _out((t, CONV_CH), F32),
                   _out((t, 128), BF), _out((1, SSD_INNER), F32),
                   _out((1, SSD_INNER), F32), _out((1, SSD_HEADS), F32),
                   _out((1, SSD_HEADS), F32)],
        scratch_shapes=[pltpu.VMEM((2, SSD_STATE, GROUP_W), F32), pltpu.VMEM((SSD_SUB, CHUNK, SSD_INNER), F32),
                        pltpu.VMEM((SSD_SUB, CHUNK, SSD_INNER), F32)],
        compiler_params=_cp(("arbitrary", "arbitrary"), 56),
    )(*_pin(proj, proj, proj, pre, d_yssd, yssm, h_prev, dt_bias, a_log, dskip_e, g_ssd))


def _grad_w_out(y_pool, y_ssd, d_o):
    t, d = d_o.shape
    tk = POOL_WIDTH
    n_s = SSD_INNER // tk

    def body(p_ref, s_ref, g_ref, o_ref):
        i = pl.program_id(0)

        @pl.when(i == 0)
        def _():
            o_ref[...] = _tn(p_ref[...], g_ref[...]).astype(BF)

        @pl.when(i > 0)
        def _():
            o_ref[...] = _tn(s_ref[...], g_ref[...]).astype(BF)

    return pl.pallas_call(
        body, name="grad_w_out", grid=(1 + n_s,),
        in_specs=[pl.BlockSpec((t, tk), lambda i: (0, 0)), pl.BlockSpec((t, tk), lambda i: (0, jnp.maximum(i - 1, 0))),
                  pl.BlockSpec((t, d), lambda i: (0, 0))],
        out_specs=pl.BlockSpec((tk, d), lambda i: (i, 0)),
        out_shape=_out((POOL_WIDTH + SSD_INNER, d), BF),
        compiler_params=_cp(("parallel",), 56))(*_pin(y_pool, y_ssd, d_o))


def _grad_w_in_t(d_upool, d_z, d_uxbc, d_udt, u1):
    t, d = u1.shape
    tk = 512
    n_z, n_x = SSD_INNER // tk, CONV_CH // tk

    def body(p_ref, z_ref, x_ref, dt_ref, u_ref, o_ref):
        i = pl.program_id(0)

        @pl.when(i == 0)
        def _():
            o_ref[...] = _tn(p_ref[...], u_ref[...]).astype(BF)

        @pl.when((i >= 1) & (i < 1 + n_z))
        def _():
            o_ref[...] = _tn(z_ref[...], u_ref[...]).astype(BF)

        @pl.when((i >= 1 + n_z) & (i < 1 + n_z + n_x))
        def _():
            o_ref[...] = _tn(x_ref[...], u_ref[...]).astype(BF)

        @pl.when(i == 1 + n_z + n_x)
        def _():
            o_ref[0:128, :] = _tn(dt_ref[...], u_ref[...]).astype(BF)

    return pl.pallas_call(
        body, name="grad_w_in", grid=(2 + n_z + n_x,),
        in_specs=[pl.BlockSpec((t, tk), lambda i: (0, 0)),
                  pl.BlockSpec((t, tk), lambda i: (0, jnp.clip(i - 1, 0, n_z - 1))),
                  pl.BlockSpec((t, tk), lambda i: (0, jnp.clip(i - 1 - n_z, 0, n_x - 1))),
                  pl.BlockSpec((t, 128), lambda i: (0, 0)), pl.BlockSpec((t, d), lambda i: (0, 0))],
        out_specs=pl.BlockSpec((tk, d), lambda i: (i, 0)),
        out_shape=_out((IN_PAD, d), BF),
        compiler_params=_cp(("parallel",), 56))(*_pin(d_upool, d_z, d_uxbc, d_udt, u1))


def _conv_bwd(d_pre, proj, conv_w, nb, seq):
    ts = min(256, seq)
    nt = seq // ts
    hb = ts // CONV_HALO
    last_block = nb * seq // CONV_HALO - 1
    n_ext = CHUNK + CONV_HALO

    def body(dp_ref, dnext_ref, u_ref, cw_ref, du_ref, gw_ref, gb_ref):
        b = pl.program_id(0)
        i = pl.program_id(1)

        @pl.when((b == 0) & (i == 0))
        def _():
            gw_ref[...] = jnp.zeros_like(gw_ref)
            gb_ref[...] = jnp.zeros_like(gb_ref)

        for c0 in range(0, CONV_CH, 128):
            cs = slice(c0, c0 + 128)
            cw = cw_ref[:, cs]
            gw = [0.0] * 4
            gb = 0.0
            for r0 in range(0, ts, CHUNK):
                dp = dp_ref[r0:r0 + CHUNK, cs]
                u = u_ref[r0:r0 + CHUNK, cs]
                if r0 + CHUNK < ts:
                    below = dp_ref[r0 + CHUNK:r0 + CHUNK + CONV_HALO, cs]
                else:
                    below = jnp.where(i == nt - 1, 0.0, dnext_ref[:, cs])
                ext_d = jnp.concatenate([dp, below], 0)
                du = dp * cw[3:4]
                gw[3] = gw[3] + jnp.sum(dp * u, 0, keepdims=True)
                for k in (2, 1, 0):
                    shifted = pltpu.roll(ext_d, n_ext - (3 - k), 0)[:CHUNK]
                    du = du + shifted * cw[k:k + 1]
                    gw[k] = gw[k] + jnp.sum(shifted * u, 0, keepdims=True)
                gb = gb + jnp.sum(dp, 0, keepdims=True)
                du_ref[r0:r0 + CHUNK, cs] = du.astype(BF)
            for k in range(4):
                gw_ref[k:k + 1, cs] += gw[k]
            gb_ref[:, cs] += gb

    return pl.pallas_call(
        body, name="conv_bwd", grid=(nb, nt),
        in_specs=[pl.BlockSpec((ts, CONV_CH), lambda b, i: (b * nt + i, 0)),
                  pl.BlockSpec((CONV_HALO, CONV_CH), lambda b, i: (jnp.minimum((b * nt + i + 1) * hb, last_block), 0)),
                  pl.BlockSpec((ts, CONV_CH), lambda b, i: (b * nt + i, 1)),
                  pl.BlockSpec((4, CONV_CH), lambda b, i: (0, 0))],
        out_specs=[pl.BlockSpec((ts, CONV_CH), lambda b, i: (b * nt + i, 0)),
                   pl.BlockSpec((8, CONV_CH), lambda b, i: (0, 0)), pl.BlockSpec((1, CONV_CH), lambda b, i: (0, 0))],
        out_shape=[_out((nb * seq, CONV_CH), BF), _out((8, CONV_CH), F32),
                   _out((1, CONV_CH), F32)],
        compiler_params=_cp(("arbitrary", "arbitrary"), 48))(*_pin(d_pre, d_pre, proj, conv_w))


def _in_proj_bwd(d_parts, w_in_t, x, dh1, mod3, g_mix, seq, token):
    t, d = x.shape
    nb = t // seq
    tm = min(ROW_TILE, seq)
    tps = seq // tm

    widths = [p.shape[1] for p in d_parts]

    def body(d0_ref, d1_ref, d2_ref, d3_ref, w_ref, x_ref, dh1_ref, mod_ref, g_ref, tok_ref, gx_ref, acc_ref, gg_ref):
        i = pl.program_id(0)

        @pl.when(i == 0)
        def _():
            gg_ref[...] = jnp.zeros_like(gg_ref)

        @pl.when(i % tps == 0)
        def _():
            acc_ref[...] = jnp.zeros_like(acc_ref)

        gg = a_shift = a_scale = 0.0
        for rows in _sub_rows(tm):
            d_cat = jnp.concatenate([p_ref[rows, :] for p_ref in (d0_ref, d1_ref, d2_ref)], 1)
            du = _nn(d_cat, w_ref[0:OFF_DT, :]) + _nn(d3_ref[rows, 0:IN_WIDTH - OFF_DT], w_ref[OFF_DT:IN_WIDTH, :])
            xv = x_ref[rows, :]
            r = lax.rsqrt(jnp.mean(xv * xv, -1, keepdims=True) + EPS)
            hh = xv * r
            n1 = hh * g_ref[...]
            dn1 = du * (1.0 + mod_ref[0, 1:2, :])
            dhat = dn1 * g_ref[...]
            gx_ref[rows, :] = dh1_ref[rows, :] + r * (dhat - hh * jnp.mean(dhat * hh, -1, keepdims=True))
            gg = gg + jnp.sum(dn1 * hh, 0, keepdims=True)
            a_shift = a_shift + jnp.sum(du, 0, keepdims=True)
            a_scale = a_scale + jnp.sum(du * n1, 0, keepdims=True)
        gg_ref[...] += gg
        acc_ref[0, 0:1, :] += a_shift
        acc_ref[0, 1:2, :] += a_scale

    row = lambda i: (i, 0)
    vec = pl.BlockSpec((1, d), lambda i: (0, 0))
    return pl.pallas_call(
        body, name="in_proj_bwd", grid=(t // tm,),
        in_specs=[pl.BlockSpec((tm, wd), row) for wd in widths] +
                 [_RESIDENT, pl.BlockSpec((tm, d), row),
                  pl.BlockSpec((tm, d), row), pl.BlockSpec((1, N_MOD, d), lambda i: (i // tps, 0, 0)), vec, _token_spec()],
        out_specs=[pl.BlockSpec((tm, d), row), pl.BlockSpec((1, 8, d), lambda i: (i // tps, 0, 0)), vec],
        out_shape=[_out((t, d), F32), _out((nb, 8, d), F32),
                   _out((1, d), F32)],
        compiler_params=_cp(("arbitrary",), 40))(*_pin(*d_parts), w_in_t, *_pin(x, dh1, mod3, g_mix, token))


_VEC_LAYOUT = (("g_mix", 1024), ("conv_b", 1536), ("g_ssd", 1024), ("pool_scale", 512), ("g_mlp", 1024),
               ("g_final", 1024), ("dt_bias", 128), ("a_log", 128), ("d_skip_lanes", 1024), ("sq_err", 1024))
_VEC_OFFSET = {}
_off = 0
for _name, _n in _VEC_LAYOUT:
    _VEC_OFFSET[_name] = _off
    _off += _n
_SMALL_PARAMS = ("b_ada", "g_mix", "conv_w", "conv_b", "dt_bias", "a_log", "d_skip", "g_ssd", "w_pool", "pool_scale",
                 "g_mlp", "g_final")


def _pack_vec(parts):
    cols = []
    for name, n in _VEC_LAYOUT:
        v = parts[name]
        if v.shape[1] < n:
            v = jnp.pad(v, ((0, 0), (0, n - v.shape[1])))
        cols.append(v)
    return jnp.concatenate(cols, 1)


def _small_adam(vec_all, wpool_all, convw_all, dmod_all, params):
    names = _SMALL_PARAMS
    nin = 4 + 3 * len(names)

    def body(*refs):
        vec_ref, wp_ref, cw_ref, dm_ref = refs[:4]
        prm = {n: refs[4 + 3 * i:7 + 3 * i] for i, n in enumerate(names)}
        loss_ref = refs[nin]
        outs = {n: refs[nin + 1 + 4 * i:nin + 5 + 4 * i] for i, n in enumerate(names)}
        vsum = vec_ref[0]
        for s in range(1, N_DEV):
            vsum = vsum + vec_ref[s]

        def lanes(name, n):
            off = _VEC_OFFSET[name]
            return vsum[:, off:off + n]

        grads = {n: lanes(n, prm[n][0].shape[1]) for n in ("g_mix", "conv_b", "g_ssd", "pool_scale", "g_mlp", "g_final", "dt_bias")}
        grads["a_log"] = lanes("a_log", SSD_HEADS) * (-jnp.exp(prm["a_log"][0][...]))
        per_lane = jnp.broadcast_to(lanes("d_skip_lanes", SSD_INNER), (8, SSD_INNER))
        grads["d_skip"] = _exact_nn(per_lane, _head_reduce_matrix(SSD_INNER, SSD_HEADS))[0:1]
        gwp = wp_ref[0].astype(F32)
        gcw = cw_ref[0]
        gb = jnp.sum(dm_ref[0], 0, keepdims=True)
        for s in range(1, N_DEV):
            gwp = gwp + wp_ref[s].astype(F32)
            gcw = gcw + cw_ref[s]
            gb = gb + jnp.sum(dm_ref[s], 0, keepdims=True)
        grads["w_pool"] = gwp
        grads["conv_w"] = gcw[0:4]
        grads["b_ada"] = gb
        total = jnp.sum(lanes("sq_err", D_MODEL), 1, keepdims=True) * (0.5 / D_MODEL)
        loss_ref[...] = jnp.broadcast_to(total, loss_ref.shape)
        for n in names:
            w_ref, m_ref, v_ref = prm[n]
            g = grads[n]
            d, m2, v2 = _adam_math(w_ref[...], g, m_ref[...], v_ref[...])
            g_ref, d_ref, m2_ref, v2_ref = outs[n]
            g_ref[...] = g
            d_ref[...] = d
            m2_ref[...] = m2
            v2_ref[...] = v2

    flat = [vec_all, wpool_all, convw_all, dmod_all]
    out_shape = [jax.ShapeDtypeStruct((1, 128), F32)]
    for n in names:
        flat += list(params[n])
        out_shape += [jax.ShapeDtypeStruct(params[n][0].shape, F32)] * 4
    vm = pl.BlockSpec(memory_space=pltpu.VMEM)
    res = pl.pallas_call(body, name="small_adam", out_shape=out_shape, in_specs=[vm] * len(flat),
                         out_specs=[vm] * len(out_shape), compiler_params=_cp(vmem_mb=48))(*flat)
    return res[0], {n: res[1 + 4 * i:5 + 4 * i] for i, n in enumerate(names)}


_WEIGHTS = ("w_ada", "b_ada", "g_mix", "w_in", "conv_w", "conv_b", "dt_bias", "a_log", "d_skip", "g_ssd", "w_pool",
            "pool_scale", "w_out", "g_mlp", "w_up", "w_down", "g_final")


def _local_step(x2, tg2, mod3, seq, w_in_t, first_token, weights_arrived, weights_later, start_reduce, before_last,
                conv_w_full, sp):
    t, d = x2.shape
    nb = t // seq
    dskip_e = jnp.repeat(sp["d_skip"], SSD_HEAD_DIM, axis=1)
    proj, u1 = _in_proj(x2, mod3, sp["g_mix"], w_in_t, seq, first_token)
    y_ssd, yssm, h_prev, pre = _ssd_fwd(proj, conv_w_full, sp["conv_b"], sp["dt_bias"], sp["a_log"], dskip_e, sp["g_ssd"], nb, seq)
    y_pool, p = _pool_fwd(proj, sp["w_pool"], sp["pool_scale"], nb, seq, weights_arrived(y_ssd))
    w_out_f, w_up4, w_down4 = weights_later(y_pool)
    w_down_f = w_down4.reshape(D_FF, d)
    h1, o, u2 = _out_proj(y_pool, y_ssd, w_out_f, x2, mod3, sp["g_mlp"], seq)
    a_up = _mlp_up(u2, w_up4)
    d_dn, dh2, sq, gg_final, d_gf = _mlp_down_loss(a_up, w_down_f, h1, mod3, sp["g_final"], tg2, seq)

    gw_down = _tn_matmul(a_up, d_dn, 512, d, "grad_w_down", square_relu=True)
    tok = start_reduce("w_down", gw_down.reshape(N_CHIPS, D_FF // N_CHIPS, d))
    d_a = _mlp_down_bwd(d_dn, w_down4, a_up, tok)
    gw_up4 = _tn_matmul(u2, d_a, 512, d, "grad_w_up", out3=True)
    tok = start_reduce("w_up", gw_up4)
    dh1, d_o, accf, gg_mlp = _mlp_up_bwd(d_a, w_up4, h1, dh2, o, mod3, sp["g_mlp"], seq, tok)
    gw_out = _grad_w_out(y_pool, y_ssd, d_o)
    tok = start_reduce("w_out", gw_out.reshape(N_CHIPS, gw_out.shape[0] // N_CHIPS, d))
    d_ypool, d_yssd = _out_proj_bwd(d_o, w_out_f, tok)
    d_upool, gw_pool, g_ps = _pool_bwd(d_ypool, p, sp["w_pool"], sp["pool_scale"], nb, seq)
    d_z, d_pre, d_udt, gg_ssd, gdsk, ga, gdtb = _ssd_bwd(proj, pre, d_yssd, yssm, h_prev, sp["dt_bias"], sp["a_log"],
                                                        dskip_e, sp["g_ssd"], nb, seq)
    d_uxbc, gconvw, gconvb = _conv_bwd(d_pre, proj, conv_w_full, nb, seq)
    gw_in_t = _grad_w_in_t(d_upool, d_z, d_uxbc, d_udt, u1)
    shard_rows = IN_WIDTH // N_CHIPS
    tok = start_reduce("w_in", jnp.stack([gw_in_t[k * shard_rows:(k + 1) * shard_rows] for k in range(N_CHIPS)]))
    gx, accm, gg_mix = _in_proj_bwd([d_upool, d_z, d_uxbc, d_udt], w_in_t, x2, dh1, mod3, sp["g_mix"], seq, before_last(tok))

    d_mod = jnp.concatenate([accm[:, 0], accm[:, 1], accf[:, 2], accf[:, 0], accf[:, 1], d_gf[:, 0]], 1)
    vec = _pack_vec({"g_mix": gg_mix, "conv_b": gconvb, "g_ssd": gg_ssd, "pool_scale": g_ps, "g_mlp": gg_mlp,
                     "g_final": gg_final, "dt_bias": gdtb, "a_log": ga, "d_skip_lanes": gdsk, "sq_err": sq})
    return gx, d_mod, vec, gw_pool, gconvw


def kernel(x, c, w_ada, b_ada, g_mix, w_in, conv_w, conv_b, dt_bias, a_log, d_skip, g_ssd, w_pool, pool_scale, w_out, g_mlp, w_up, w_down, g_final, loss_target, m_w_ada, m_b_ada, m_g_mix, m_w_in, m_conv_w, m_conv_b, m_dt_bias, m_a_log, m_d_skip, m_g_ssd, m_w_pool, m_pool_scale, m_w_out, m_g_mlp, m_w_up, m_w_down, m_g_final, v_w_ada, v_b_ada, v_g_mix, v_w_in, v_conv_w, v_conv_b, v_dt_bias, v_a_log, v_d_skip, v_g_ssd, v_w_pool, v_pool_scale, v_w_out, v_g_mlp, v_w_up, v_w_down, v_g_final):
    nb, seq, d = x.shape
    t = nb * seq
    xi, yi, ci = _mesh_pos()
    chip = 2 * xi + yi
    me = 4 * xi + 2 * yi + ci
    ada_cols = w_ada.shape[2]
    conv_cols = conv_w.shape[2]
    in_cols = w_in.shape[2]
    w_in_s, m_w_in_s, v_w_in_s = w_in[0].T, m_w_in[0].T, v_w_in[0].T

    c_send, c_recv, c_src, c_land, c_token = _exchange_start([c, conv_w[0]], ALL_PEERS, "cond_start")
    w_in_b = w_in_s.astype(BF)
    i_send, i_recv, i_src, i_land, in_token = _ici_start(
        [w_in_b], [jax.ShapeDtypeStruct((N_CHIPS,) + w_in_b.shape, BF)], _gather_sent, _gather_landing, "gather_start_w_in",
        after=c_token)
    c_own, c_got = _exchange_wait(c_send, c_recv, c_src, c_land, ALL_PEERS, in_token, "cond_wait")
    c8, convw8 = [lax.dynamic_update_slice(got, mine[None], (me,) + (0,) * mine.ndim) for got, mine in zip(c_got, c_own)]
    c_all = c8.reshape(N_DEV * nb, d)
    conv_w_full = convw8[0::2].transpose(1, 0, 2).reshape(4, N_CHIPS * conv_cols)
    b_shard = lax.dynamic_slice(b_ada, (0, chip * ada_cols), (1, ada_cols))
    mod_part, c_act = _ada_mod(c_all, w_ada[0], b_shard, in_token)
    mod_rows = mod_part.reshape(N_DEV, nb, ada_cols)
    m_send, m_recv, m_src, m_land, _ = _ici_start(
        [mod_rows], [jax.ShapeDtypeStruct((N_CHIPS, nb, ada_cols), F32)], _mod_sent, _mod_landing, "mod_start", after=mod_part)

    later = [w_out[0].astype(BF), w_up[0].astype(BF), w_down[0].astype(BF)]
    in_shard, in_land = _ici_wait(i_send, i_recv, i_src, i_land, [m_src[0]] + later, _gather_sent, _gather_landing,
                                  "gather_wait_w_in")
    (w_in4,) = _gather_finish(in_land, in_shard)
    w_in_t = w_in4.reshape(N_CHIPS * in_cols, d)
    mod_mine, mod_land = _ici_wait(m_send, m_recv, m_src, m_land, w_in_t, _mod_sent, _mod_landing, "mod_wait")
    mod_own = lax.dynamic_slice(mod_mine[0], (me, 0, 0), (1, nb, ada_cols))
    mod4 = lax.dynamic_update_slice(mod_land[0], mod_own, (chip, 0, 0))
    mod3 = mod4.transpose(1, 0, 2).reshape(nb, N_MOD, d)
    g_send, g_recv, g_src, g_land, first_token = _ici_start(
        later, [jax.ShapeDtypeStruct((N_CHIPS,) + s.shape, BF) for s in later], _gather_sent, _gather_landing, "gather_start",
        after=w_in4)

    def weights_arrived(after):
        shards, lands = _ici_wait(g_send, g_recv, g_src, g_land, after, _gather_sent, _gather_landing, "gather_wait")
        pending["forward"] = _forward_start(lands, "forward_start") + (shards,)
        return pending["forward"][3]

    def weights_later(after):
        f_send, f_recv, f_land, _, shards = pending["forward"]
        lands = _forward_wait(f_send, f_recv, f_land, after, "forward_wait")
        w_out4, w_up4, w_down4 = [lax.dynamic_update_slice(land, shard[None], (chip, 0, 0)) for land, shard in zip(lands, shards)]
        return w_out4.reshape(N_CHIPS * w_out.shape[1], d), w_up4, w_down4

    pending = {}

    def start_reduce(name, grad4):
        pending[name] = _reduce_start(grad4, "reduce_start_" + name)
        return pending[name][4]

    pos = jnp.stack([ci, chip, me]).astype(jnp.int32)
    early = ("w_out", "w_up", "w_down")

    def summed_half(name, after):
        r_send, r_recv, r_src, r_land, _ = pending[name]
        own, recv = _reduce_wait(r_send, r_recv, r_src, r_land, after, "reduce_wait_" + name)
        return _sum_eight(recv, own, pos)

    def before_last(token):
        pending["early_halves"] = _exchange_start([summed_half(n, token) for n in early], SIBLING, "halves_start_early")
        return pending["early_halves"][4]

    sp = dict(g_mix=g_mix, conv_b=conv_b, dt_bias=dt_bias, a_log=a_log, d_skip=d_skip, g_ssd=g_ssd,
              w_pool=w_pool[0], pool_scale=pool_scale, g_mlp=g_mlp, g_final=g_final.reshape(1, d))
    gx, d_mod, vec, gw_pool, gconvw = _local_step(
        x.reshape(t, d), loss_target.reshape(t, d), mod3, seq, w_in_t, first_token, weights_arrived, weights_later, start_reduce,
        before_last, conv_w_full, sp)

    small_parts = [vec, gw_pool.reshape(4 * POOL_GROUP, POOL_GROUP).astype(BF), gconvw, d_mod]
    s_send, s_recv, s_src, s_land, s_token = _exchange_start(small_parts, ALL_PEERS, "small_start")
    h_send, h_recv, h_src, h_land, h_token = _exchange_start([summed_half("w_in", s_token)], SIBLING, "halves_start")
    e_send, e_recv, e_src, e_land, _ = pending["early_halves"]
    e_own, e_got = _exchange_wait(e_send, e_recv, e_src, e_land, SIBLING, h_token, "halves_wait_early")
    res = {}
    for i, (n, w, m, v) in enumerate((("w_out", w_out, m_w_out, v_w_out), ("w_up", w_up, m_w_up, v_w_up),
                                      ("w_down", w_down, m_w_down, v_w_down))):
        g, dl, m2, v2 = _adam_big(e_own[i], e_got[i], w[0], m[0], v[0], pos)
        res[n] = (g[None], dl[None], m2[None], v2[None])

    s_own, s_got = _exchange_wait(s_send, s_recv, s_src, s_land, ALL_PEERS, res["w_down"][1], "small_wait")
    vec8, wpool8, convw8g, dmod8 = [lax.dynamic_update_slice(got, mine[None], (me,) + (0,) * mine.ndim)
                                    for got, mine in zip(s_got, s_own)]
    convw8s = lax.dynamic_slice(convw8g, (0, 0, chip * conv_cols), (N_DEV, 8, conv_cols))
    m_in = dict(b_ada=m_b_ada, g_mix=m_g_mix, conv_w=m_conv_w[0], conv_b=m_conv_b, dt_bias=m_dt_bias, a_log=m_a_log,
                d_skip=m_d_skip, g_ssd=m_g_ssd, w_pool=m_w_pool.reshape(4 * POOL_GROUP, POOL_GROUP), pool_scale=m_pool_scale,
                g_mlp=m_g_mlp, g_final=m_g_final.reshape(1, d))
    v_in = dict(b_ada=v_b_ada, g_mix=v_g_mix, conv_w=v_conv_w[0], conv_b=v_conv_b, dt_bias=v_dt_bias, a_log=v_a_log,
                d_skip=v_d_skip, g_ssd=v_g_ssd, w_pool=v_w_pool.reshape(4 * POOL_GROUP, POOL_GROUP), pool_scale=v_pool_scale,
                g_mlp=v_g_mlp, g_final=v_g_final.reshape(1, d))
    w_small = dict(sp, b_ada=b_ada, conv_w=conv_w[0], w_pool=w_pool.reshape(4 * POOL_GROUP, POOL_GROUP))
    loss_row, small = _small_adam(vec8, wpool8, convw8s, dmod8, {n: (w_small[n], m_in[n], v_in[n]) for n in _SMALL_PARAMS})

    dmod_all = dmod8.reshape(N_DEV * nb, N_CHIPS * ada_cols)
    dmod_cols = lax.dynamic_slice(dmod_all, (0, chip * ada_cols), (N_DEV * nb, ada_cols))
    res.update({n: tuple(r.reshape(w.shape) for r in small[n])
                for n, w in (("b_ada", b_ada), ("g_mix", g_mix), ("conv_w", conv_w), ("conv_b", conv_b), ("dt_bias", dt_bias),
                             ("a_log", a_log), ("d_skip", d_skip), ("g_ssd", g_ssd), ("w_pool", w_pool),
                             ("pool_scale", pool_scale), ("g_mlp", g_mlp), ("g_final", g_final))})
    g_ada, d_ada, m_ada, v_ada = _adam_ada(c_act.T.astype(BF), dmod_cols, w_ada[0], m_w_ada[0], v_w_ada[0])
    res["w_ada"] = (g_ada[None], d_ada[None], m_ada[None], v_ada[None])
    h_own, h_got = _exchange_wait(h_send, h_recv, h_src, h_land, SIBLING, g_ada, "halves_wait")
    rows3 = lambda a: jnp.transpose(a, (2, 0, 1))
    res["w_in"] = tuple(jnp.transpose(r, (1, 2, 0))
                        for r in _adam_rows(h_own[0], h_got[0], rows3(w_in), rows3(m_w_in), rows3(v_w_in), pos))

    loss = loss_row[0, 0]
    return (loss, gx.reshape(nb, seq, d), *[res[n][0] for n in _WEIGHTS], *[res[n][1] for n in _WEIGHTS],
            *[res[n][2] for n in _WEIGHTS], *[res[n][3] for n in _WEIGHTS])
```

```python
import jax
import jax.numpy as jnp
from jax import lax
from jax.experimental import pallas as pl
from jax.experimental.pallas import tpu as pltpu

F32 = jnp.float32
BF = jnp.bfloat16
MESH = pl.DeviceIdType.MESH

EPS = 1e-5
D_MODEL = 1024
POOL_WIDTH = 512
POOL_WINDOWS = (2, 4, 8, 16)
POOL_GROUP = 128
SSD_INNER = 1024
SSD_HEADS = 16
SSD_HEAD_DIM = 64
SSD_STATE = 128
GROUP_W = 512
CHUNK = 128
CONV_CH = 1536
OFF_DT = 3072
IN_WIDTH = 3088
IN_PAD = 3200
D_FF = 4096
N_MOD = 6
N_CHIPS = 4
N_DEV = 8
HALO = 16
CONV_HALO = 8

ADAM_LR = 0.001
ADAM_B1 = 0.9
ADAM_B2 = 0.999
ADAM_EPS = 1e-08
ADAM_WD = 0.01
ADAM_STEP = 10

VMEM_BYTES_V7X = 64 * 1024 * 1024


def _cp(semantics=None, vmem_mb=48, **kw):
    assert vmem_mb * 1024 * 1024 < VMEM_BYTES_V7X
    args = dict(vmem_limit_bytes=vmem_mb * 1024 * 1024, **kw)
    if semantics is not None:
        args["dimension_semantics"] = semantics
    return pltpu.CompilerParams(**args)


def _out(shape, dtype):
    return pltpu.HBM(shape, dtype)


def _pin(*arrays):
    return [pltpu.with_memory_space_constraint(a, pltpu.HBM) for a in arrays]


TOKEN_SHAPE = (8, 128)


def _order_operand(a):
    if a.shape == TOKEN_SHAPE:
        return pl.BlockSpec(memory_space=pltpu.VMEM), a
    return pl.BlockSpec(memory_space=pltpu.HBM), pltpu.with_memory_space_constraint(a, pltpu.HBM)


def _nn(a, b):
    return jnp.dot(a, b, preferred_element_type=F32)


def _nt(a, b):
    return lax.dot_general(a, b, (((1,), (1,)), ((), ())), preferred_element_type=F32)


def _tn(a, b):
    return lax.dot_general(a, b, (((0,), (0,)), ((), ())), preferred_element_type=F32)


def _split3(v):
    hi = v.astype(BF)
    r1 = v - hi.astype(F32)
    mid = r1.astype(BF)
    lo = (r1 - mid.astype(F32)).astype(BF)
    return hi, mid, lo


def _exact_nn(v, m01):
    hi, mid, lo = _split3(v)
    return _nn(hi, m01) + _nn(mid, m01) + _nn(lo, m01)


def _exact_nn_left(m01, v):
    hi, mid, lo = _split3(v)
    return _nn(m01, hi) + _nn(m01, mid) + _nn(m01, lo)


def _exact_nt_left(m01, v):
    hi, mid, lo = _split3(v)
    return _nt(m01, hi) + _nt(m01, mid) + _nt(m01, lo)


def _sigmoid(v):
    return 1.0 / (1.0 + jnp.exp(-v))


def _iota(shape, dim):
    return lax.broadcasted_iota(jnp.int32, shape, dim)


def _head_expand_matrix(heads, width):
    return (_iota((heads, width), 1) // SSD_HEAD_DIM == _iota((heads, width), 0)).astype(BF)


def _head_reduce_matrix(width, heads):
    return (_iota((width, heads), 0) // SSD_HEAD_DIM == _iota((width, heads), 1)).astype(BF)


def _mesh_pos():
    return lax.axis_index("x"), lax.axis_index("y"), lax.axis_index("c")


def _flip(v, bit):
    return v + bit - 2 * bit * v


_HBM = pl.BlockSpec(memory_space=pltpu.HBM)
_SEM = pl.BlockSpec(memory_space=pltpu.SEMAPHORE)
_DATAFLOW = pltpu.SideEffectType.DATAFLOW_SIDE_EFFECTING


def _peer_chip(x, y, j):
    return _flip(x, (j >> 1) & 1), _flip(y, j & 1)


def _all_peers():
    x, y, c = _mesh_pos()
    return [(_flip(x, (k >> 2) & 1), _flip(y, (k >> 1) & 1), _flip(c, k & 1)) for k in range(1, N_DEV)]


_START_IDS = ("sibling", "gather_start_w_in", "mod_start", "gather_start", "reduce_start_w_down", "reduce_start_w_up",
              "reduce_start_w_out", "reduce_start_w_in", "small_start", "cond_start")


def _start_params(name):
    return pltpu.CompilerParams(has_side_effects=_DATAFLOW, collective_id=_START_IDS.index(name))


def _handshake(peers):
    barrier = pltpu.get_barrier_semaphore()
    for peer in peers:
        pl.semaphore_signal(barrier, inc=1, device_id=peer, device_id_type=MESH)
    pl.semaphore_wait(barrier, len(peers))


def _ici_start(srcs, land_shapes, sent, landing, name, after):
    n = len(srcs)

    def body(*refs):
        src_refs, land_refs = refs[:n], refs[n:2 * n]
        send_sems, recv_sems = refs[2 * n + 1], refs[2 * n + 2]
        token = refs[-1]
        x, y, c = _mesh_pos()
        _handshake([(*_peer_chip(x, y, j), c) for j in range(1, N_CHIPS)])
        for j in range(1, N_CHIPS):
            px, py = _peer_chip(x, y, j)
            for a in range(n):
                pltpu.make_async_remote_copy(
                    src_ref=sent(src_refs[a], c, 2 * px + py), dst_ref=landing(land_refs[a], c, 2 * x + y),
                    send_sem=send_sems.at[a * (N_CHIPS - 1) + j - 1], recv_sem=recv_sems.at[a * (N_CHIPS - 1) + j - 1],
                    device_id=(px, py, c), device_id_type=MESH).start()
        token[...] = jnp.zeros_like(token)

    sems = pltpu.SemaphoreType.DMA((n * (N_CHIPS - 1),))
    after_spec, after = _order_operand(after)
    lands = [pltpu.with_memory_space_constraint(lax.empty(s.shape, s.dtype), pltpu.HBM) for s in land_shapes]
    outs = pl.pallas_call(
        body, name=name,
        out_shape=(sems, sems, *[pltpu.HBM(s.shape, s.dtype) for s in srcs],
                   *[pltpu.HBM(s.shape, s.dtype) for s in land_shapes], jax.ShapeDtypeStruct((8, 128), F32)),
        in_specs=[_HBM] * (2 * n) + [after_spec],
        out_specs=[_SEM, _SEM] + [_HBM] * (2 * n) + [pl.BlockSpec(memory_space=pltpu.VMEM)],
        input_output_aliases={i: 2 + i for i in range(2 * n)},
        compiler_params=_start_params(name),
    )(*_pin(*srcs), *lands, after)
    return outs[0], outs[1], outs[2:2 + n], outs[2 + n:2 + 2 * n], outs[-1]


def _ici_wait(send_sems, recv_sems, src_thru, land_thru, after, sent, landing, name):
    n = len(src_thru)
    afters = [_order_operand(a) for a in (after if isinstance(after, (list, tuple)) else [after])]

    def body(*refs):
        src_refs, land_refs = refs[:n], refs[n:2 * n]
        send_sems, recv_sems = refs[2 * n], refs[2 * n + 1]
        x, y, c = _mesh_pos()
        for j in range(1, N_CHIPS):
            px, py = _peer_chip(x, y, j)
            for a in range(n):
                cp = pltpu.make_async_remote_copy(
                    src_ref=sent(src_refs[a], c, 2 * px + py), dst_ref=landing(land_refs[a], c, 2 * px + py),
                    send_sem=send_sems.at[a * (N_CHIPS - 1) + j - 1], recv_sem=recv_sems.at[a * (N_CHIPS - 1) + j - 1],
                    device_id=(px, py, c), device_id_type=MESH)
                cp.wait_send()
                cp.wait_recv()

    outs = pl.pallas_call(
        body, name=name,
        out_shape=tuple(pltpu.HBM(s.shape, s.dtype) for s in (*src_thru, *land_thru)),
        in_specs=[_HBM] * (2 * n) + [_SEM, _SEM] + [s for s, _ in afters], out_specs=[_HBM] * (2 * n),
        input_output_aliases={i: i for i in range(2 * n)},
        compiler_params=pltpu.CompilerParams(has_side_effects=_DATAFLOW),
    )(*src_thru, *land_thru, send_sems, recv_sems, *[a for _, a in afters])
    return outs[:n], outs[n:]


def _col_half(ref, which, lead=()):
    hc = ref.shape[-1] // 2
    return ref.at[(*lead, slice(None), pl.ds(pl.multiple_of(which * hc, 128), hc))]


def _gather_sent(ref, c, dst_chip):
    return _col_half(ref, c)


def _gather_landing(ref, c, src_chip):
    return _col_half(ref, c, lead=(src_chip,))


def _mod_sent(ref, c, dst_chip):
    return ref.at[2 * dst_chip + c]


def _mod_landing(ref, c, src_chip):
    return ref.at[src_chip]


def _reduce_copy(src_ref, land_ref, send_sems, recv_sems, k, receiving):
    x, y, c = _mesh_pos()
    px, py, pc = _flip(x, (k >> 2) & 1), _flip(y, (k >> 1) & 1), _flip(c, k & 1)
    hc = src_ref.shape[2] // 2
    src = src_ref.at[2 * px + py, :, pl.ds(pl.multiple_of(pc * hc, 128), hc)]
    slot = (4 * px + 2 * py + pc) if receiving else (4 * x + 2 * y + c)
    return pltpu.make_async_remote_copy(
        src_ref=src, dst_ref=land_ref.at[slot], send_sem=send_sems.at[k - 1], recv_sem=recv_sems.at[k - 1],
        device_id=(px, py, pc), device_id_type=MESH)


def _reduce_start(grad4, name):
    k4, r, cols = grad4.shape

    def body(src_ref, land_ref, send_sems, recv_sems, src_thru, land_thru, token):
        _handshake(_all_peers())
        for k in range(1, N_DEV):
            _reduce_copy(src_ref, land_ref, send_sems, recv_sems, k, receiving=False).start()
        token[...] = jnp.zeros_like(token)

    sems = pltpu.SemaphoreType.DMA((N_DEV - 1,))
    land = pltpu.with_memory_space_constraint(lax.empty((N_DEV, r, cols // 2), grad4.dtype), pltpu.HBM)
    return pl.pallas_call(
        body, name=name,
        out_shape=(sems, sems, pltpu.HBM(grad4.shape, grad4.dtype), pltpu.HBM(land.shape, land.dtype),
                   jax.ShapeDtypeStruct((8, 128), F32)),
        in_specs=[_HBM, _HBM], out_specs=[_SEM, _SEM, _HBM, _HBM, pl.BlockSpec(memory_space=pltpu.VMEM)],
        input_output_aliases={0: 2, 1: 3},
        compiler_params=_start_params(name),
    )(*_pin(grad4), land)


def _reduce_wait(send_sems, recv_sems, src_thru, land_thru, after, name):
    def body(src_ref, land_ref, send_sems, recv_sems, after_ref, src_out, land_out):
        for k in range(1, N_DEV):
            cp = _reduce_copy(src_ref, land_ref, send_sems, recv_sems, k, receiving=True)
            cp.wait_send()
            cp.wait_recv()

    return pl.pallas_call(
        body, name=name,
        out_shape=(pltpu.HBM(src_thru.shape, src_thru.dtype), pltpu.HBM(land_thru.shape, land_thru.dtype)),
        in_specs=[_HBM, _HBM, _SEM, _SEM, _order_operand(after)[0]], out_specs=[_HBM, _HBM],
        input_output_aliases={0: 0, 1: 1},
        compiler_params=pltpu.CompilerParams(has_side_effects=_DATAFLOW),
    )(src_thru, land_thru, send_sems, recv_sems, _order_operand(after)[1])


SIBLING_COLLECTIVE_ID = 0
_SIBLING_PARAMS = pltpu.CompilerParams(has_side_effects=_DATAFLOW, collective_id=SIBLING_COLLECTIVE_ID)


def _sibling_handshake():
    x, y, c = _mesh_pos()
    barrier = pltpu.get_barrier_semaphore()
    pl.semaphore_signal(barrier, inc=1, device_id=(x, y, 1 - c), device_id_type=MESH)
    pl.semaphore_wait(barrier, 1)


def _peer_copy(src_ref, land_ref, send_sems, recv_sems, idx, k, receiving):
    x, y, c = _mesh_pos()
    px, py, pc = _flip(x, (k >> 2) & 1), _flip(y, (k >> 1) & 1), _flip(c, k & 1)
    if land_ref.shape[0] == N_DEV:
        slot = (4 * px + 2 * py + pc) if receiving else (4 * x + 2 * y + c)
    else:
        slot = pc if receiving else c
    return pltpu.make_async_remote_copy(
        src_ref=src_ref, dst_ref=land_ref.at[slot], send_sem=send_sems.at[idx], recv_sem=recv_sems.at[idx],
        device_id=(px, py, pc), device_id_type=MESH)


def _exchange_start(arrays, peers, name):
    n = len(arrays)

    def body(*refs):
        src_refs, land_refs = refs[:n], refs[n:2 * n]
        send_sems, recv_sems = refs[2 * n], refs[2 * n + 1]
        token = refs[-1]
        if peers == SIBLING:
            _sibling_handshake()
        else:
            _handshake(_all_peers())
        for j, k in enumerate(peers):
            for a in range(n):
                _peer_copy(src_refs[a], land_refs[a], send_sems, recv_sems, a * len(peers) + j, k, receiving=False).start()
        token[...] = jnp.zeros_like(token)

    sems = pltpu.SemaphoreType.DMA((n * len(peers),))
    n_slots = N_DEV if len(peers) > 1 else 2
    lands = [pltpu.with_memory_space_constraint(lax.empty((n_slots,) + a.shape, a.dtype), pltpu.HBM) for a in arrays]
    outs = pl.pallas_call(
        body, name=name,
        out_shape=(sems, sems, *[pltpu.HBM(a.shape, a.dtype) for a in arrays], *[pltpu.HBM(l.shape, l.dtype) for l in lands],
                   jax.ShapeDtypeStruct((8, 128), F32)),
        in_specs=[_HBM] * (2 * n), out_specs=[_SEM, _SEM] + [_HBM] * (2 * n) + [pl.BlockSpec(memory_space=pltpu.VMEM)],
        input_output_aliases={i: 2 + i for i in range(2 * n)},
        compiler_params=_SIBLING_PARAMS if peers == SIBLING else _start_params(name),
    )(*_pin(*arrays), *lands)
    return outs[0], outs[1], outs[2:2 + n], outs[2 + n:2 + 2 * n], outs[-1]


def _exchange_wait(send_sems, recv_sems, src_thru, land_thru, peers, after, name):
    n = len(src_thru)

    def body(*refs):
        src_refs, land_refs = refs[:n], refs[n:2 * n]
        send_sems, recv_sems = refs[2 * n], refs[2 * n + 1]
        for j, k in enumerate(peers):
            for a in range(n):
                cp = _peer_copy(src_refs[a], land_refs[a], send_sems, recv_sems, a * len(peers) + j, k, receiving=True)
                cp.wait_send()
                cp.wait_recv()

    outs = pl.pallas_call(
        body, name=name,
        out_shape=tuple(pltpu.HBM(s.shape, s.dtype) for s in (*src_thru, *land_thru)),
        in_specs=[_HBM] * (2 * n) + [_SEM, _SEM, _order_operand(after)[0]], out_specs=[_HBM] * (2 * n),
        input_output_aliases={i: i for i in range(2 * n)},
        compiler_params=pltpu.CompilerParams(has_side_effects=_DATAFLOW),
    )(*src_thru, *land_thru, send_sems, recv_sems, _order_operand(after)[1])
    return outs[:n], outs[n:]


ALL_PEERS = tuple(range(1, N_DEV))
SIBLING = (1,)


def _sum_eight(recv, grad4, pos):
    n, r, hc = recv.shape
    steps = 2
    tc = hc // steps

    def body(pos_ref, r_ref, g_ref, o_ref):
        me = pos_ref[2]
        o_ref[...] = jnp.zeros_like(o_ref)
        for s in range(n):
            @pl.when(me == s)
            def _():
                o_ref[...] += g_ref[0].astype(F32)

            @pl.when(me != s)
            def _():
                o_ref[...] += r_ref[s].astype(F32)

    grid_spec = pltpu.PrefetchScalarGridSpec(
        num_scalar_prefetch=1, grid=(steps,),
        in_specs=[pl.BlockSpec((n, r, tc), lambda i, pos: (0, 0, i)),
                  pl.BlockSpec((1, r, tc), lambda i, pos: (pos[1], 0, pos[0] * steps + i))],
        out_specs=pl.BlockSpec((r, tc), lambda i, pos: (0, i)))
    return pl.pallas_call(body, name="sum_eight", grid_spec=grid_spec, out_shape=_out((r, hc), F32),
                          compiler_params=_cp(("parallel",), 32))(pos, *_pin(recv, grad4))


def _forward_copy(land_ref, send_sems, recv_sems, idx, j, receiving):
    x, y, c = _mesh_pos()
    px, py = _peer_chip(x, y, j)
    mine = _col_half(land_ref, c, lead=(2 * px + py,))
    theirs = _col_half(land_ref, 1 - c, lead=(2 * px + py,))
    return pltpu.make_async_remote_copy(
        src_ref=mine, dst_ref=theirs if receiving else mine, send_sem=send_sems.at[idx], recv_sem=recv_sems.at[idx],
        device_id=(x, y, 1 - c), device_id_type=MESH)


def _forward_start(lands, name):
    n = len(lands)

    def body(*refs):
        land_refs, send_sems, recv_sems, token = refs[:n], refs[n], refs[n + 1], refs[-1]
        _sibling_handshake()
        for j in range(1, N_CHIPS):
            for a in range(n):
                _forward_copy(land_refs[a], send_sems, recv_sems, a * (N_CHIPS - 1) + j - 1, j, receiving=False).start()
        token[...] = jnp.zeros_like(token)

    sems = pltpu.SemaphoreType.DMA((n * (N_CHIPS - 1),))
    outs = pl.pallas_call(
        body, name=name,
        out_shape=(sems, sems, *[pltpu.HBM(l.shape, l.dtype) for l in lands], jax.ShapeDtypeStruct((8, 128), F32)),
        in_specs=[_HBM] * n, out_specs=[_SEM, _SEM] + [_HBM] * n + [pl.BlockSpec(memory_space=pltpu.VMEM)],
        input_output_aliases={i: 2 + i for i in range(n)},
        compiler_params=_SIBLING_PARAMS,
    )(*lands)
    return outs[0], outs[1], outs[2:2 + n], outs[-1]


def _forward_wait(send_sems, recv_sems, lands_thru, after, name):
    n = len(lands_thru)

    def body(*refs):
        land_refs, send_sems, recv_sems = refs[:n], refs[n], refs[n + 1]
        for j in range(1, N_CHIPS):
            for a in range(n):
                cp = _forward_copy(land_refs[a], send_sems, recv_sems, a * (N_CHIPS - 1) + j - 1, j, receiving=True)
                cp.wait_send()
                cp.wait_recv()

    return pl.pallas_call(
        body, name=name,
        out_shape=tuple(pltpu.HBM(l.shape, l.dtype) for l in lands_thru),
        in_specs=[_HBM] * n + [_SEM, _SEM, _order_operand(after)[0]], out_specs=[_HBM] * n,
        input_output_aliases={i: i for i in range(n)},
        compiler_params=pltpu.CompilerParams(has_side_effects=_DATAFLOW),
    )(*lands_thru, send_sems, recv_sems, _order_operand(after)[1])


def _gather_finish(lands, shards):
    n = len(lands)
    any_spec = _HBM

    def body(*refs):
        shard_refs, out_refs = refs[n:2 * n], refs[2 * n:3 * n]
        send_sems, recv_sems, local_sems = refs[3 * n:]
        x, y, c = _mesh_pos()
        chip = 2 * x + y
        local, sends = [], []
        for a in range(n):
            cp = pltpu.make_async_copy(shard_refs[a], out_refs[a].at[chip], local_sems.at[a])
            cp.start()
            local.append(cp)
        for j in range(1, N_CHIPS):
            px, py = _peer_chip(x, y, j)
            for a in range(n):
                landed = _col_half(out_refs[a], c, lead=(2 * px + py,))
                cp = pltpu.make_async_remote_copy(
                    src_ref=landed, dst_ref=landed, send_sem=send_sems.at[a, j], recv_sem=recv_sems.at[a, j],
                    device_id=(x, y, 1 - c), device_id_type=MESH)
                cp.start()
                sends.append(cp)
        for j in range(1, N_CHIPS):
            px, py = _peer_chip(x, y, j)
            for a in range(n):
                other = _col_half(out_refs[a], 1 - c, lead=(2 * px + py,))
                pltpu.make_async_remote_copy(
                    src_ref=other, dst_ref=other, send_sem=send_sems.at[a, j], recv_sem=recv_sems.at[a, j],
                    device_id=(x, y, 1 - c), device_id_type=MESH).wait_recv()
        for cp in sends:
            cp.wait_send()
        for cp in local:
            cp.wait()

    return pl.pallas_call(
        body, name="gather_finish",
        out_shape=[_out(l.shape, l.dtype) for l in lands],
        in_specs=[any_spec] * (2 * n), out_specs=[any_spec] * n,
        input_output_aliases={i: i for i in range(n)},
        scratch_shapes=[pltpu.SemaphoreType.DMA((n, N_CHIPS))] * 2 + [pltpu.SemaphoreType.DMA((n,))],
        compiler_params=_cp(vmem_mb=16),
    )(*lands, *shards)


def _adam_math(w, g, m, v):
    m2 = ADAM_B1 * m + (1.0 - ADAM_B1) * g
    v2 = ADAM_B2 * v + (1.0 - ADAM_B2) * (g * g)
    m_hat = m2 / (1.0 - ADAM_B1 ** ADAM_STEP)
    v_hat = v2 / (1.0 - ADAM_B2 ** ADAM_STEP)
    delta = -ADAM_LR * (m_hat / (jnp.sqrt(v_hat) + ADAM_EPS) + ADAM_WD * w)
    return delta, m2, v2


def _adam_big(g_own, g_pair, w, m, v, pos):
    r, c = w.shape
    per_half = 2
    tc = c // (2 * per_half)

    def body(pos_ref, go_ref, gp_ref, w_ref, m_ref, v_ref, g_ref, d_ref, m2_ref, v2_ref):
        half = pl.program_id(0) // per_half
        g = jnp.where(half == pos_ref[0], go_ref[...], gp_ref[0])
        d, m2, v2 = _adam_math(w_ref[...], g, m_ref[...], v_ref[...])
        g_ref[...] = g
        d_ref[...] = d
        m2_ref[...] = m2
        v2_ref[...] = v2

    spec = pl.BlockSpec((r, tc), lambda i, pos: (0, i))
    grid_spec = pltpu.PrefetchScalarGridSpec(
        num_scalar_prefetch=1, grid=(2 * per_half,),
        in_specs=[pl.BlockSpec((r, tc), lambda i, pos: (0, i % per_half)),
                  pl.BlockSpec((1, r, tc), lambda i, pos: (1 - pos[0], 0, i % per_half)), spec, spec, spec],
        out_specs=[spec] * 4)
    sh = _out((r, c), F32)
    return pl.pallas_call(body, name="adam_big", grid_spec=grid_spec, out_shape=[sh] * 4,
                          compiler_params=_cp(("parallel",), 32))(pos, *_pin(g_own, g_pair, w, m, v))


def _adam_rows(g_own, g_pair, w3, m3, v3, pos):
    r, _, c = w3.shape
    tr = 128

    def body(pos_ref, go_ref, gp_ref, w_ref, m_ref, v_ref, g_ref, d_ref, m2_ref, v2_ref):
        core = pos_ref[0]
        g = jnp.concatenate([jnp.where(core == 0, go_ref[...], gp_ref[0]), jnp.where(core == 1, go_ref[...], gp_ref[0])], 1)
        d, m2, v2 = _adam_math(w_ref[:, 0, :], g, m_ref[:, 0, :], v_ref[:, 0, :])
        g_ref[:, 0, :] = g
        d_ref[:, 0, :] = d
        m2_ref[:, 0, :] = m2
        v2_ref[:, 0, :] = v2

    spec = pl.BlockSpec((tr, 1, c), lambda i, pos: (i, 0, 0))
    grid_spec = pltpu.PrefetchScalarGridSpec(
        num_scalar_prefetch=1, grid=(pl.cdiv(r, tr),),
        in_specs=[pl.BlockSpec((tr, c // 2), lambda i, pos: (i, 0)),
                  pl.BlockSpec((1, tr, c // 2), lambda i, pos: (1 - pos[0], i, 0)), spec, spec, spec],
        out_specs=[spec] * 4)
    sh = _out(w3.shape, F32)
    return pl.pallas_call(body, name="adam_rows", grid_spec=grid_spec, out_shape=[sh] * 4,
                          compiler_params=_cp(("parallel",), 32))(pos, *_pin(g_own, g_pair, w3, m3, v3))


def _adam_ada(c_act_t, dmod_cols, w, m, v):
    r, c = w.shape
    tc = 512

    def body(ct_ref, dm_ref, w_ref, m_ref, v_ref, g_ref, d_ref, m2_ref, v2_ref):
        g = _nn(ct_ref[...], dm_ref[...].astype(BF))
        d, m2, v2 = _adam_math(w_ref[...], g, m_ref[...], v_ref[...])
        g_ref[...] = g
        d_ref[...] = d
        m2_ref[...] = m2
        v2_ref[...] = v2

    spec = pl.BlockSpec((r, tc), lambda i: (0, i))
    sh = _out((r, c), F32)
    return pl.pallas_call(
        body, name="adam_ada", grid=(c // tc,),
        in_specs=[pl.BlockSpec(c_act_t.shape, lambda i: (0, 0)), pl.BlockSpec((dmod_cols.shape[0], tc), lambda i: (0, i)),
                  spec, spec, spec],
        out_specs=[spec] * 4, out_shape=[sh] * 4, compiler_params=_cp(("parallel",), 48))(*_pin(c_act_t, dmod_cols, w, m, v))


def _ada_mod(c_all, w_shard, b_shard, token):
    nb, d = c_all.shape
    cols = w_shard.shape[1]
    tc = 512

    def body(c_ref, w_ref, b_ref, tok_ref, mod_ref, act_ref):
        cv = c_ref[...]
        act = cv * _sigmoid(cv)
        act_ref[...] = act
        mod_ref[...] = _nn(act.astype(BF), w_ref[...].astype(BF)) + b_ref[...]

    return pl.pallas_call(
        body, name="ada_mod", grid=(cols // tc,),
        in_specs=[pl.BlockSpec((nb, d), lambda i: (0, 0)), pl.BlockSpec((d, tc), lambda i: (0, i)),
                  pl.BlockSpec((1, tc), lambda i: (0, i)), _token_spec()],
        out_specs=[pl.BlockSpec((nb, tc), lambda i: (0, i)), pl.BlockSpec((nb, d), lambda i: (0, 0))],
        out_shape=[_out((nb, cols), F32), _out((nb, d), F32)],
        compiler_params=_cp(("arbitrary",), 32))(*_pin(c_all, w_shard, b_shard, token))


SUB_ROWS = 256
ROW_TILE = 512


def _sub_rows(tm):
    return [slice(s, s + SUB_ROWS) for s in range(0, tm, SUB_ROWS)] if tm > SUB_ROWS else [slice(0, tm)]


_RESIDENT = pl.BlockSpec(memory_space=pltpu.VMEM)


def _token_spec():
    return pl.BlockSpec((8, 128), lambda *_: (0, 0))


def _in_proj(x, mod3, g_mix, w_in_t, seq, token):
    t, d = x.shape
    tm = min(ROW_TILE, seq)
    tps = seq // tm

    def body(x_ref, mod_ref, g_ref, w_ref, tok_ref, proj_ref, u1_ref):
        for rows in _sub_rows(tm):
            xv = x_ref[rows, :]
            r = lax.rsqrt(jnp.mean(xv * xv, -1, keepdims=True) + EPS)
            u = (xv * r * g_ref[...]) * (1.0 + mod_ref[0, 1:2, :]) + mod_ref[0, 0:1, :]
            ub = u.astype(BF)
            u1_ref[rows, :] = ub
            proj_ref[rows, 0:OFF_DT] = _nt(ub, w_ref[0:OFF_DT, :])
            proj_ref[rows, OFF_DT:IN_PAD] = jnp.zeros((rows.stop - rows.start, IN_PAD - OFF_DT), F32)
            proj_ref[rows, OFF_DT:IN_WIDTH] = _nt(ub, w_ref[OFF_DT:IN_WIDTH, :])

    return pl.pallas_call(
        body, name="in_proj", grid=(t // tm,),
        in_specs=[pl.BlockSpec((tm, d), lambda i: (i, 0)), pl.BlockSpec((1, N_MOD, d), lambda i: (i // tps, 0, 0)),
                  pl.BlockSpec((1, d), lambda i: (0, 0)), _RESIDENT, _token_spec()],
        out_specs=[pl.BlockSpec((tm, IN_PAD), lambda i: (i, 0)), pl.BlockSpec((tm, d), lambda i: (i, 0))],
        out_shape=[_out((t, IN_PAD), F32), _out((t, d), BF)],
        compiler_params=_cp(("parallel",), 40))(*_pin(x, mod3, g_mix), w_in_t, *_pin(token))


def _pool_tile(seq):
    return min(1024, seq)


def _pool_fwd(proj, w_pool, pool_scale, nb, seq, token):
    ts = _pool_tile(seq)
    nt = seq // ts

    def body(u_ref, halo_ref, wp_ref, ps_ref, tok_ref, yp_ref, p_ref):
        i = pl.program_id(1)
        halo = jnp.where(i == 0, 0.0, halo_ref[...])
        u = u_ref[...]
        ext = jnp.concatenate([halo, u], 0)
        tpos = i * ts + _iota((ts, 1), 0)
        for g, w in enumerate(POOL_WINDOWS):
            gs = slice(g * POOL_GROUP, (g + 1) * POOL_GROUP)
            s = ext[:, gs]
            sh = 1
            while sh < w:
                s = s + pltpu.roll(s, sh, 0)
                sh *= 2
            cnt = jnp.minimum(tpos + 1, w).astype(F32)
            pb = (s[HALO:] / cnt - u[:, gs]).astype(BF)
            p_ref[:, gs] = pb
            yp_ref[:, gs] = (_nn(pb, wp_ref[g].astype(BF)) * ps_ref[:, gs]).astype(BF)

    hb = ts // HALO
    return pl.pallas_call(
        body, name="pool_fwd", grid=(nb, nt),
        in_specs=[pl.BlockSpec((ts, POOL_WIDTH), lambda b, i: (b * nt + i, 0)),
                  pl.BlockSpec((HALO, POOL_WIDTH), lambda b, i: (jnp.maximum((b * nt + i) * hb - 1, 0), 0)),
                  pl.BlockSpec((4, POOL_GROUP, POOL_GROUP), lambda b, i: (0, 0, 0)),
                  pl.BlockSpec((1, POOL_WIDTH), lambda b, i: (0, 0)), _token_spec()],
        out_specs=[pl.BlockSpec((ts, POOL_WIDTH), lambda b, i: (b * nt + i, 0))] * 2,
        out_shape=[_out((nb * seq, POOL_WIDTH), BF)] * 2,
        compiler_params=_cp(("parallel", "parallel"), 32))(*_pin(proj, proj, w_pool, pool_scale, token))


def _conv_pre(uxbc, halo, cw, cb, first):
    halo = jnp.where(first, 0.0, halo)
    ext = jnp.concatenate([halo, uxbc], 0)
    pre = cb + uxbc * cw[3:4]
    for k in (2, 1, 0):
        pre = pre + pltpu.roll(ext, 3 - k, 0)[CONV_HALO:] * cw[k:k + 1]
    return pre


def _chunk_terms(pre, udt, dtb, alog):
    sg = _sigmoid(pre)
    xbc = pre * sg
    dtp = udt[:, :SSD_HEADS] + dtb
    dt = jnp.maximum(dtp, 0.0) + jnp.log(1.0 + jnp.exp(-jnp.abs(dtp)))
    a = -jnp.exp(alog)
    da = dt * a
    tril = (_iota((CHUNK, CHUNK), 0) >= _iota((CHUNK, CHUNK), 1))
    acum = _exact_nn_left(tril.astype(BF), da)
    eye = (_iota((SSD_HEADS, SSD_HEADS), 0) == _iota((SSD_HEADS, SSD_HEADS), 1)).astype(BF)
    acum_t = _exact_nt_left(eye, acum)
    expand = _head_expand_matrix(SSD_HEADS, SSD_INNER)
    acum_e = _exact_nn(acum, expand)
    dt_e = _exact_nn(dt, expand)
    last_e = acum_e[CHUNK - 1:CHUNK]
    return dict(pre=pre, sg=sg, xbc=xbc, dtp=dtp, dt=dt, a=a, acum=acum, acum_t=acum_t, tril=tril,
                dt_e=dt_e, e_a=jnp.exp(acum_e), d_out=jnp.exp(last_e - acum_e), c_dec=jnp.exp(last_e))


def _head_decay(r, h):
    seg = r["acum"][:, h:h + 1] - r["acum_t"][h:h + 1, :]
    return jnp.where(r["tril"], jnp.exp(jnp.minimum(seg, 0.0)), 0.0)


SSD_SUB = 4
SSD_ROWS = SSD_SUB * CHUNK


def _ssd_specs(nb, seq, reverse):
    ns = seq // SSD_ROWS
    per = seq // CONV_HALO

    def cidx(c):
        return (ns - 1 - c) if reverse else c

    def row(b, c):
        return b * ns + cidx(c)

    specs = [
        pl.BlockSpec((SSD_ROWS, CONV_CH), lambda b, c: (row(b, c), 1)),
        pl.BlockSpec((CONV_HALO, CONV_CH),
                     lambda b, c: (jnp.maximum(b * per + cidx(c) * (SSD_ROWS // CONV_HALO) - 1, 0), 1)),
        pl.BlockSpec((SSD_ROWS, GROUP_W), lambda b, c: (row(b, c), 1)),
        pl.BlockSpec((SSD_ROWS, GROUP_W), lambda b, c: (row(b, c), 2)),
        pl.BlockSpec((SSD_ROWS, 128), lambda b, c: (row(b, c), OFF_DT // 128)),
    ]
    return specs, row, cidx, ns


def _const_spec(shape):
    return pl.BlockSpec(shape, lambda b, c: (0,) * len(shape))


def _ssd_fwd(proj, conv_w, conv_b, dt_bias, a_log, dskip_e, g_ssd, nb, seq):
    specs, row, cidx, ns = _ssd_specs(nb, seq, reverse=False)

    def body(uxbc_ref, halo_ref, z0_ref, z1_ref, udt_ref, cw_ref, cb_ref, dtb_ref, alog_ref, dsk_ref, gs_ref,
             yssd_ref, yssm_ref, hprev_ref, pre_ref, h_ref, yd_ref):
        c = pl.program_id(1)

        @pl.when(c == 0)
        def _():
            h_ref[...] = jnp.zeros_like(h_ref)

        for sub in range(SSD_SUB):
            rows = slice(sub * CHUNK, (sub + 1) * CHUNK)
            if sub == 0:
                halo, first = halo_ref[...], c == 0
            else:
                halo, first = uxbc_ref[sub * CHUNK - CONV_HALO:sub * CHUNK, :], False
            pre = _conv_pre(uxbc_ref[rows, :], halo, cw_ref[...], cb_ref[...], first)
            pre_ref[rows, :] = pre
            r = _chunk_terms(pre, udt_ref[rows, :], dtb_ref[...], alog_ref[...])
            xbc = r["xbc"]
            xs = xbc[:, :SSD_INNER]
            xdt = xs * r["dt_e"]
            xdt_b = xdt.astype(BF)
            xdo_b = (xdt * r["d_out"]).astype(BF)
            hprev_ref[0, sub] = h_ref[...]
            for g in range(2):
                gs = slice(g * GROUP_W, (g + 1) * GROUP_W)
                bg = xbc[:, SSD_INNER + g * SSD_STATE:SSD_INNER + (g + 1) * SSD_STATE].astype(BF)
                cg = xbc[:, SSD_INNER + (2 + g) * SSD_STATE:SSD_INNER + (3 + g) * SSD_STATE].astype(BF)
                scores = _nt(cg, bg)
                hg = h_ref[g]
                y_off = _nn(cg, hg.astype(BF)) * r["e_a"][:, gs]
                for hh in range(8):
                    h = g * 8 + hh
                    hs = slice(h * SSD_HEAD_DIM, (h + 1) * SSD_HEAD_DIM)
                    m = (scores * _head_decay(r, h)).astype(BF)
                    yd_ref[sub, :, hs] = _nn(m, xdt_b[:, hs])
                h_ref[g] = hg * r["c_dec"][:, gs] + _tn(bg, xdo_b[:, gs])
                y = yd_ref[sub, :, gs] + y_off + dsk_ref[:, gs] * xs[:, gs]
                yssm_ref[rows, gs] = y
                zg = (z0_ref if g == 0 else z1_ref)[rows, :]
                yg = y * (zg * _sigmoid(zg))
                rg = lax.rsqrt(jnp.mean(yg * yg, -1, keepdims=True) + EPS)
                yssd_ref[rows, gs] = (yg * rg * gs_ref[:, gs]).astype(BF)

    t = nb * seq
    return pl.pallas_call(
        body, name="ssd_fwd", grid=(nb, ns),
        in_specs=specs + [_const_spec((4, CONV_CH)), _const_spec((1, CONV_CH)), _const_spec((1, SSD_HEADS)),
                          _const_spec((1, SSD_HEADS)), _const_spec((1, SSD_INNER)), _const_spec((1, SSD_INNER))],
        out_specs=[pl.BlockSpec((SSD_ROWS, SSD_INNER), lambda b, c: (row(b, c), 0)),
                   pl.BlockSpec((SSD_ROWS, SSD_INNER), lambda b, c: (row(b, c), 0)),
                   pl.BlockSpec((1, SSD_SUB, 2, SSD_STATE, GROUP_W), lambda b, c: (b, c, 0, 0, 0)),
                   pl.BlockSpec((SSD_ROWS, CONV_CH), lambda b, c: (row(b, c), 0))],
        out_shape=[_out((t, SSD_INNER), BF), _out((t, SSD_INNER), F32),
                   _out((nb, seq // CHUNK, 2, SSD_STATE, GROUP_W), F32), _out((t, CONV_CH), F32)],
        scratch_shapes=[pltpu.VMEM((2, SSD_STATE, GROUP_W), F32), pltpu.VMEM((SSD_SUB, CHUNK, SSD_INNER), F32)],
        compiler_params=_cp(("arbitrary", "arbitrary"), 56),
    )(*_pin(proj, proj, proj, proj, proj, conv_w, conv_b, dt_bias, a_log, dskip_e, g_ssd))


def _out_proj(y_pool, y_ssd, w_out, x, mod3, g_mlp, seq):
    t, d = x.shape
    tm = 512
    tps = seq // tm if seq >= tm else 1
    tm = min(tm, seq)

    def body(yp_ref, ys_ref, w_ref, x_ref, mod_ref, g_ref, h1_ref, o_ref, u2_ref):
        for rows in _sub_rows(tm):
            o = _nn(jnp.concatenate([yp_ref[rows, :], ys_ref[rows, :]], 1), w_ref[...])
            o_ref[rows, :] = o.astype(BF)
            h1 = x_ref[rows, :] + mod_ref[0, 2:3, :] * o
            h1_ref[rows, :] = h1
            r = lax.rsqrt(jnp.mean(h1 * h1, -1, keepdims=True) + EPS)
            u2_ref[rows, :] = ((h1 * r * g_ref[...]) * (1.0 + mod_ref[0, 4:5, :]) + mod_ref[0, 3:4, :]).astype(BF)

    row = lambda i: (i, 0)
    return pl.pallas_call(
        body, name="out_proj", grid=(t // tm,),
        in_specs=[pl.BlockSpec((tm, POOL_WIDTH), row), pl.BlockSpec((tm, SSD_INNER), row),
                  _RESIDENT, pl.BlockSpec((tm, d), row),
                  pl.BlockSpec((1, N_MOD, d), lambda i: (i // tps, 0, 0)), pl.BlockSpec((1, d), lambda i: (0, 0))],
        out_specs=[pl.BlockSpec((tm, d), row)] * 3,
        out_shape=[_out((t, d), F32), _out((t, d), BF), _out((t, d), BF)],
        compiler_params=_cp(("parallel",), 48))(*_pin(y_pool, y_ssd), w_out, *_pin(x, mod3, g_mlp))


def _mlp_up(u2, w_up4):
    t, d = u2.shape
    tm = min(1024, t)
    nk, _, cols = w_up4.shape

    def body(u_ref, w_ref, a_ref):
        a_ref[...] = _nn(u_ref[...], w_ref[pl.program_id(1)]).astype(BF)

    return pl.pallas_call(
        body, name="mlp_up", grid=(t // tm, nk),
        in_specs=[pl.BlockSpec((tm, d), lambda i, k: (i, 0)), _RESIDENT],
        out_specs=pl.BlockSpec((tm, cols), lambda i, k: (i, k)),
        out_shape=_out((t, nk * cols), BF),
        compiler_params=_cp(("parallel", "parallel"), 32))(*_pin(u2), w_up4)


def _mlp_down_loss(a_up, w_down, h1, mod3, g_final, target, seq):
    t, d = h1.shape
    nb = t // seq
    tm = min(ROW_TILE, seq)
    tps = seq // tm

    def body(a_ref, w_ref, h1_ref, mod_ref, g_ref, tg_ref, ddn_ref, dh2_ref, sq_ref, gg_ref, dgf_ref):
        i = pl.program_id(0)

        @pl.when(i == 0)
        def _():
            sq_ref[...] = jnp.zeros_like(sq_ref)
            gg_ref[...] = jnp.zeros_like(gg_ref)

        @pl.when(i % tps == 0)
        def _():
            dgf_ref[...] = jnp.zeros_like(dgf_ref)

        gate = mod_ref[0, 5:6, :]
        sq = gg = dgf = 0.0
        for rows in _sub_rows(tm):
            f = jnp.square(jnp.maximum(a_ref[rows, :], 0))
            dn = _nn(f, w_ref[...])
            h2 = h1_ref[rows, :] + gate * dn
            r = lax.rsqrt(jnp.mean(h2 * h2, -1, keepdims=True) + EPS)
            hh = h2 * r
            err = hh * g_ref[...] - tg_ref[rows, :]
            dy = err * (1.0 / d)
            dhat = dy * g_ref[...]
            dh2 = r * (dhat - hh * jnp.mean(dhat * hh, -1, keepdims=True))
            dh2_ref[rows, :] = dh2
            ddn_ref[rows, :] = (dh2 * gate).astype(BF)
            sq = sq + jnp.sum(err * err, 0, keepdims=True)
            gg = gg + jnp.sum(dy * hh, 0, keepdims=True)
            dgf = dgf + jnp.sum(dh2 * dn, 0, keepdims=True)
        sq_ref[...] += sq
        gg_ref[...] += gg
        dgf_ref[0] += dgf

    row = lambda i: (i, 0)
    vec = pl.BlockSpec((1, d), lambda i: (0, 0))
    return pl.pallas_call(
        body, name="mlp_down_loss", grid=(t // tm,),
        in_specs=[pl.BlockSpec((tm, D_FF), row), _RESIDENT, pl.BlockSpec((tm, d), row),
                  pl.BlockSpec((1, N_MOD, d), lambda i: (i // tps, 0, 0)), vec, pl.BlockSpec((tm, d), row)],
        out_specs=[pl.BlockSpec((tm, d), row), pl.BlockSpec((tm, d), row), vec, vec,
                   pl.BlockSpec((1, 1, d), lambda i: (i // tps, 0, 0))],
        out_shape=[_out((t, d), BF), _out((t, d), F32), _out((1, d), F32),
                   _out((1, d), F32), _out((nb, 1, d), F32)],
        compiler_params=_cp(("arbitrary",), 44))(*_pin(a_up), w_down, *_pin(h1, mod3, g_final, target))


def _tn_matmul(a, b, tk, tn, name, square_relu=False, out3=False):
    t, kdim = a.shape
    ndim = b.shape[1]

    def body(a_ref, b_ref, o_ref):
        av = a_ref[...]
        if square_relu:
            av = jnp.square(jnp.maximum(av, 0))
        res = _tn(av, b_ref[...]).astype(BF)
        if out3:
            o_ref[0] = res
        else:
            o_ref[...] = res

    if out3:
        out_spec = pl.BlockSpec((1, tk, tn), lambda j, i: (j, i, 0))
        out_shape = _out((ndim // tn, kdim, tn), BF)
    else:
        out_spec = pl.BlockSpec((tk, tn), lambda j, i: (i, j))
        out_shape = _out((kdim, ndim), BF)
    return pl.pallas_call(
        body, name=name, grid=(ndim // tn, kdim // tk),
        in_specs=[pl.BlockSpec((t, tk), lambda j, i: (0, i)), pl.BlockSpec((t, tn), lambda j, i: (0, j))],
        out_specs=out_spec, out_shape=out_shape,
        compiler_params=_cp(("parallel", "parallel"), 56))(*_pin(a, b))


def _mlp_down_bwd(d_dn, w_down4, a_up, token):
    t, d = d_dn.shape
    tm = min(1024, t)
    nk, rows, _ = w_down4.shape

    def body(g_ref, w_ref, a_ref, tok_ref, o_ref):
        df = _nt(g_ref[...], w_ref[pl.program_id(1)])
        o_ref[...] = (df * (2.0 * jnp.maximum(a_ref[...], 0).astype(F32))).astype(BF)

    return pl.pallas_call(
        body, name="mlp_down_bwd", grid=(t // tm, nk),
        in_specs=[pl.BlockSpec((tm, d), lambda i, k: (i, 0)), _RESIDENT,
                  pl.BlockSpec((tm, rows), lambda i, k: (i, k)), _token_spec()],
        out_specs=pl.BlockSpec((tm, rows), lambda i, k: (i, k)),
        out_shape=_out((t, nk * rows), BF),
        compiler_params=_cp(("parallel", "parallel"), 32))(*_pin(d_dn), w_down4, *_pin(a_up, token))


def _mlp_up_bwd(d_a, w_up4, h1, dh2, o, mod3, g_mlp, seq, token):
    t, d = h1.shape
    nb = t // seq
    tm = min(ROW_TILE, seq)
    tps = seq // tm
    nk = w_up4.shape[0]
    cols = w_up4.shape[2]

    def body(da_ref, w_ref, h1_ref, dh2_ref, o_ref, mod_ref, g_ref, tok_ref, dh1_ref, do_ref, acc_ref, gg_ref):
        i = pl.program_id(0)

        @pl.when(i == 0)
        def _():
            gg_ref[...] = jnp.zeros_like(gg_ref)

        @pl.when(i % tps == 0)
        def _():
            acc_ref[...] = jnp.zeros_like(acc_ref)

        gg = a_shift = a_scale = a_gate = 0.0
        for rows in _sub_rows(tm):
            du = _nt(da_ref[rows, 0:cols], w_ref[0])
            for k in range(1, nk):
                du = du + _nt(da_ref[rows, k * cols:(k + 1) * cols], w_ref[k])
            h1 = h1_ref[rows, :]
            r = lax.rsqrt(jnp.mean(h1 * h1, -1, keepdims=True) + EPS)
            hh = h1 * r
            n2 = hh * g_ref[...]
            dn2 = du * (1.0 + mod_ref[0, 4:5, :])
            dhat = dn2 * g_ref[...]
            dh1 = dh2_ref[rows, :] + r * (dhat - hh * jnp.mean(dhat * hh, -1, keepdims=True))
            dh1_ref[rows, :] = dh1
            do_ref[rows, :] = (dh1 * mod_ref[0, 2:3, :]).astype(BF)
            gg = gg + jnp.sum(dn2 * hh, 0, keepdims=True)
            a_shift = a_shift + jnp.sum(du, 0, keepdims=True)
            a_scale = a_scale + jnp.sum(du * n2, 0, keepdims=True)
            a_gate = a_gate + jnp.sum(dh1 * o_ref[rows, :].astype(F32), 0, keepdims=True)
        gg_ref[...] += gg
        acc_ref[0, 0:1, :] += a_shift
        acc_ref[0, 1:2, :] += a_scale
        acc_ref[0, 2:3, :] += a_gate

    row = lambda i: (i, 0)
    vec = pl.BlockSpec((1, d), lambda i: (0, 0))
    return pl.pallas_call(
        body, name="mlp_up_bwd", grid=(t // tm,),
        in_specs=[pl.BlockSpec((tm, D_FF), row), _RESIDENT, pl.BlockSpec((tm, d), row),
                  pl.BlockSpec((tm, d), row), pl.BlockSpec((tm, d), row),
                  pl.BlockSpec((1, N_MOD, d), lambda i: (i // tps, 0, 0)), vec, _token_spec()],
        out_specs=[pl.BlockSpec((tm, d), row), pl.BlockSpec((tm, d), row),
                   pl.BlockSpec((1, 8, d), lambda i: (i // tps, 0, 0)), vec],
        out_shape=[_out((t, d), F32), _out((t, d), BF),
                   _out((nb, 8, d), F32), _out((1, d), F32)],
        compiler_params=_cp(("arbitrary",), 44))(*_pin(d_a), w_up4, *_pin(h1, dh2, o, mod3, g_mlp, token))


def _out_proj_bwd(d_o, w_out, token):
    t, d = d_o.shape
    tm = min(1024, t)

    def body(g_ref, w_ref, tok_ref, dp_ref, ds_ref):
        gv = g_ref[...]
        dp_ref[...] = _nt(gv, w_ref[0:POOL_WIDTH, :])
        ds_ref[...] = _nt(gv, w_ref[POOL_WIDTH:, :])

    row = lambda i: (i, 0)
    return pl.pallas_call(
        body, name="out_proj_bwd", grid=(t // tm,),
        in_specs=[pl.BlockSpec((tm, d), row), _RESIDENT, _token_spec()],
        out_specs=[pl.BlockSpec((tm, POOL_WIDTH), row), pl.BlockSpec((tm, SSD_INNER), row)],
        out_shape=[_out((t, POOL_WIDTH), F32), _out((t, SSD_INNER), F32)],
        compiler_params=_cp(("parallel",), 32))(*_pin(d_o), w_out, *_pin(token))


def _pool_bwd(d_ypool, p, w_pool, pool_scale, nb, seq):
    ts = _pool_tile(seq)
    nt = seq // ts
    hb = ts // HALO
    last_block = nb * seq // HALO - 1

    def body(dy_ref, halo_ref, p_ref, wp_ref, ps_ref, du_ref, gw_ref, gs_ref):
        b = pl.program_id(0)
        i = pl.program_id(1)

        @pl.when((b == 0) & (i == 0))
        def _():
            gw_ref[...] = jnp.zeros_like(gw_ref)
            gs_ref[...] = jnp.zeros_like(gs_ref)

        halo = jnp.where(i == nt - 1, 0.0, halo_ref[...])
        dy = dy_ref[...]
        ext = jnp.concatenate([dy, halo], 0)
        tpos = i * ts + _iota((ts + HALO, 1), 0)
        n_ext = ts + HALO
        for g, w in enumerate(POOL_WINDOWS):
            gs = slice(g * POOL_GROUP, (g + 1) * POOL_GROUP)
            wg = wp_ref[g].astype(BF)
            pg = p_ref[:, gs]
            pw = _nn(pg, wg)
            gs_ref[:, gs] += jnp.sum(dy[:, gs] * pw, 0, keepdims=True)
            dpw = (ext[:, gs] * ps_ref[:, gs]).astype(BF)
            gw_ref[g] += _tn(pg, dpw[:ts])
            dp = _nt(dpw, wg)
            cnt = jnp.minimum(tpos + 1, w).astype(F32)
            s = dp / cnt
            sh = 1
            while sh < w:
                s = s + pltpu.roll(s, n_ext - sh, 0)
                sh *= 2
            du_ref[:, gs] = (s[:ts] - dp[:ts]).astype(BF)

    return pl.pallas_call(
        body, name="pool_bwd", grid=(nb, nt),
        in_specs=[pl.BlockSpec((ts, POOL_WIDTH), lambda b, i: (b * nt + i, 0)),
                  pl.BlockSpec((HALO, POOL_WIDTH), lambda b, i: (jnp.minimum((b * nt + i + 1) * hb, last_block), 0)),
                  pl.BlockSpec((ts, POOL_WIDTH), lambda b, i: (b * nt + i, 0)),
                  pl.BlockSpec((4, POOL_GROUP, POOL_GROUP), lambda b, i: (0, 0, 0)),
                  pl.BlockSpec((1, POOL_WIDTH), lambda b, i: (0, 0))],
        out_specs=[pl.BlockSpec((ts, POOL_WIDTH), lambda b, i: (b * nt + i, 0)),
                   pl.BlockSpec((4, POOL_GROUP, POOL_GROUP), lambda b, i: (0, 0, 0)),
                   pl.BlockSpec((1, POOL_WIDTH), lambda b, i: (0, 0))],
        out_shape=[_out((nb * seq, POOL_WIDTH), BF), _out((4, POOL_GROUP, POOL_GROUP), F32),
                   _out((1, POOL_WIDTH), F32)],
        compiler_params=_cp(("arbitrary", "arbitrary"), 32))(*_pin(d_ypool, d_ypool, p, w_pool, pool_scale))


def _ssd_bwd(proj, pre, d_yssd, yssm, h_prev, dt_bias, a_log, dskip_e, g_ssd, nb, seq):
    specs, row, cidx, ns = _ssd_specs(nb, seq, reverse=True)
    specs = specs[2:]

    def body(z0_ref, z1_ref, udt_ref, pre_ref, dys_ref, yssm_ref, hprev_ref,
             dtb_ref, alog_ref, dsk_ref, gs_ref,
             dz_ref, dpre_ref, dudt_ref, ggs_ref, gdsk_ref, ga_ref, gdtb_ref,
             g_ref, dxdt_ref, dyv_ref):
        b = pl.program_id(0)
        c = pl.program_id(1)

        @pl.when(c == 0)
        def _():
            g_ref[...] = jnp.zeros_like(g_ref)

        @pl.when((b == 0) & (c == 0))
        def _():
            ggs_ref[...] = jnp.zeros_like(ggs_ref)
            gdsk_ref[...] = jnp.zeros_like(gdsk_ref)
            ga_ref[...] = jnp.zeros_like(ga_ref)
            gdtb_ref[...] = jnp.zeros_like(gdtb_ref)

        for sub in reversed(range(SSD_SUB)):
            chunk(sub, z0_ref, z1_ref, udt_ref, pre_ref, dys_ref, yssm_ref, hprev_ref, dtb_ref, alog_ref, dsk_ref, gs_ref,
                  dz_ref, dpre_ref, dudt_ref, ggs_ref, gdsk_ref, ga_ref, gdtb_ref, g_ref, dxdt_ref.at[sub], dyv_ref.at[sub])

    def chunk(sub, z0_ref, z1_ref, udt_ref, pre_ref, dys_ref, yssm_ref, hprev_ref,
              dtb_ref, alog_ref, dsk_ref, gs_ref,
              dz_ref, dpre_ref, dudt_ref, ggs_ref, gdsk_ref, ga_ref, gdtb_ref,
              g_ref, dxdt_ref, dyv_ref):
        rows = slice(sub * CHUNK, (sub + 1) * CHUNK)
        r = _chunk_terms(pre_ref[rows, :], udt_ref[rows, :], dtb_ref[...], alog_ref[...])
        xbc = r["xbc"]
        xs = xbc[:, :SSD_INNER]
        dt_e = r["dt_e"]
        xdt = xs * dt_e
        xdt_b = xdt.astype(BF)
        reduce_m = _head_reduce_matrix(GROUP_W, 8)

        def head_sums(v):
            return _nn(v.astype(BF), reduce_m)

        onehot16 = lambda h: (_iota((1, SSD_HEADS), 1) == h).astype(F32)
        onecol16 = lambda h: (_iota((SSD_HEADS, 1), 0) == h).astype(F32)

        d_acum = jnp.zeros((CHUNK, SSD_HEADS), F32)
        d_acum_t = jnp.zeros((SSD_HEADS, CHUNK), F32)
        d_alast = jnp.zeros((1, SSD_HEADS), F32)
        place8 = lambda g: (_iota((8, SSD_HEADS), 1) == _iota((8, SSD_HEADS), 0) + 8 * g).astype(BF)
        d_b, d_c = [], []
        for g in range(2):
            gs = slice(g * GROUP_W, (g + 1) * GROUP_W)
            zg = (z0_ref if g == 0 else z1_ref)[rows, :]
            sz = _sigmoid(zg)
            silu_z = zg * sz
            ys = yssm_ref[rows, gs]
            yg = ys * silu_z
            rg = lax.rsqrt(jnp.mean(yg * yg, -1, keepdims=True) + EPS)
            yh = yg * rg
            dys = dys_ref[rows, gs]
            ggs_ref[:, gs] += jnp.sum(dys * yh, 0, keepdims=True)
            dyh = dys * gs_ref[:, gs]
            dyg = rg * (dyh - yh * jnp.mean(dyh * yh, -1, keepdims=True))
            dy = dyg * silu_z
            dz_ref[rows, gs] = (dyg * ys * (sz * (1.0 + zg * (1.0 - sz)))).astype(BF)
            gdsk_ref[:, gs] += jnp.sum(dy * xs[:, gs], 0, keepdims=True)
            dyv_ref[:, gs] = dy
            dy_b = dy.astype(BF)

            bg = xbc[:, SSD_INNER + g * SSD_STATE:SSD_INNER + (g + 1) * SSD_STATE].astype(BF)
            cg = xbc[:, SSD_INNER + (2 + g) * SSD_STATE:SSD_INNER + (3 + g) * SSD_STATE].astype(BF)
            scores = _nt(cg, bg)
            hg = hprev_ref[0, sub, g]
            hg_b = hg.astype(BF)
            gg = g_ref[g]
            gg_b = gg.astype(BF)
            e_a = r["e_a"][:, gs]
            d_out = r["d_out"][:, gs]
            c_dec = r["c_dec"][:, gs]
            zc = _nn(cg, hg_b)
            wv = e_a * dy
            wv_b = wv.astype(BF)
            da_g = head_sums(wv * zc)
            dcg = _nt(wv_b, hg_b)
            d_hprev = _tn(cg, wv_b)
            vg = _nn(bg, gg_b)
            dxdt_g = d_out * vg
            dd_out = head_sums(xdt[:, gs] * vg)
            dbg = _nt((xdt[:, gs] * d_out).astype(BF), gg_b)
            dcd = _exact_nn(jnp.sum(gg * hg, 0, keepdims=True), reduce_m)
            d_out8 = jnp.exp(r["acum"][CHUNK - 1:CHUNK, 8 * g:8 * g + 8] - r["acum"][:, 8 * g:8 * g + 8])
            c_dec8 = jnp.exp(r["acum"][CHUNK - 1:CHUNK, 8 * g:8 * g + 8])
            t8 = dd_out * d_out8
            d_alast = d_alast + _exact_nn(jnp.sum(t8, 0, keepdims=True) + dcd * c_dec8, place8(g))
            d_acum = d_acum + _exact_nn(da_g - t8, place8(g))
            dsc = jnp.zeros((CHUNK, CHUNK), F32)
            for hh in range(8):
                h = g * 8 + hh
                hs = slice(h * SSD_HEAD_DIM, (h + 1) * SSD_HEAD_DIM)
                lam = _head_decay(r, h)
                m = scores * lam
                dyh_b = dy_b[:, hh * SSD_HEAD_DIM:(hh + 1) * SSD_HEAD_DIM]
                dm = _nt(dyh_b, xdt_b[:, hs])
                tm_ = dm * m
                d_acum = d_acum + jnp.sum(tm_, 1, keepdims=True) * onehot16(h)
                d_acum_t = d_acum_t + onecol16(h) * jnp.sum(tm_, 0, keepdims=True)
                dsc = dsc + dm * lam
                dxdt_ref[:, hs] = _tn(m.astype(BF), dyh_b) + dxdt_g[:, hh * SSD_HEAD_DIM:(hh + 1) * SSD_HEAD_DIM]
            dsc_b = dsc.astype(BF)
            d_c.append(dcg + _nn(dsc_b, bg))
            d_b.append(dbg + _tn(dsc_b, cg))
            g_ref[g] = d_hprev + c_dec * gg

        eye = (_iota((CHUNK, CHUNK), 0) == _iota((CHUNK, CHUNK), 1)).astype(BF)
        d_acum = d_acum - _exact_nt_left(eye, d_acum_t)
        is_last = (_iota((CHUNK, 1), 0) == CHUNK - 1).astype(F32)
        d_acum = d_acum + is_last * d_alast
        triu = (_iota((CHUNK, CHUNK), 0) <= _iota((CHUNK, CHUNK), 1)).astype(BF)
        d_da = _exact_nn_left(triu, d_acum)
        dt = r["dt"]
        ga_ref[...] += jnp.sum(d_da * dt, 0, keepdims=True)
        dxdt = dxdt_ref[...]
        reduce16 = _head_reduce_matrix(SSD_INNER, SSD_HEADS)
        d_dt = d_da * r["a"] + _nn((dxdt * xs).astype(BF), reduce16)
        d_udt = d_dt * _sigmoid(r["dtp"])
        gdtb_ref[...] += jnp.sum(d_udt, 0, keepdims=True)
        dudt_ref[rows, :] = jnp.zeros((CHUNK, dudt_ref.shape[1]), BF)
        dudt_ref[rows, 0:SSD_HEADS] = d_udt.astype(BF)
        pre, sg = r["pre"], r["sg"]
        dsilu = sg * (1.0 + pre * (1.0 - sg))
        dpre_ref[rows, 0:SSD_INNER] = (dsk_ref[...] * dyv_ref[...] + dxdt * dt_e) * dsilu[:, 0:SSD_INNER]
        for g in range(2):
            bs = slice(SSD_INNER + g * SSD_STATE, SSD_INNER + (g + 1) * SSD_STATE)
            cs = slice(SSD_INNER + (2 + g) * SSD_STATE, SSD_INNER + (3 + g) * SSD_STATE)
            dpre_ref[rows, bs] = d_b[g] * dsilu[:, bs]
            dpre_ref[rows, cs] = d_c[g] * dsilu[:, cs]

    t = nb * seq
    vec = _const_spec((1, SSD_INNER))
    small = _const_spec((1, SSD_HEADS))
    return pl.pallas_call(
        body, name="ssd_bwd", grid=(nb, ns),
        in_specs=specs + [pl.BlockSpec((SSD_ROWS, CONV_CH), lambda b, c: (row(b, c), 0)),
                          pl.BlockSpec((SSD_ROWS, SSD_INNER), lambda b, c: (row(b, c), 0)),
                          pl.BlockSpec((SSD_ROWS, SSD_INNER), lambda b, c: (row(b, c), 0)),
                          pl.BlockSpec((1, SSD_SUB, 2, SSD_STATE, GROUP_W), lambda b, c: (b, cidx(c), 0, 0, 0)),
                          small, small, vec, vec],
        out_specs=[pl.BlockSpec((SSD_ROWS, SSD_INNER), lambda b, c: (row(b, c), 0)),
                   pl.BlockSpec((SSD_ROWS, CONV_CH), lambda b, c: (row(b, c), 0)),
                   pl.BlockSpec((SSD_ROWS, 128), lambda b, c: (row(b, c), 0)),
                   vec, vec, small, small],
        out_shape=[_out((t, SSD_INNER), BF), _out((t, CONV_CH), F32),
                   _out((t, 128), BF), _out((1, SSD_INNER), F32),
                   _out((1, SSD_INNER), F32), _out((1, SSD_HEADS), F32),
                   _out((1, SSD_HEADS), F32)],
        scratch_shapes=[pltpu.VMEM((2, SSD_STATE, GROUP_W), F32), pltpu.VMEM((SSD_SUB, CHUNK, SSD_INNER), F32),
                        pltpu.VMEM((SSD_SUB, CHUNK, SSD_INNER), F32)],
        compiler_params=_cp(("arbitrary", "arbitrary"), 56),
    )(*_pin(proj, proj, proj, pre, d_yssd, yssm, h_prev, dt_bias, a_log, dskip_e, g_ssd))


def _grad_w_out(y_pool, y_ssd, d_o):
    t, d = d_o.shape
    tk = POOL_WIDTH
    n_s = SSD_INNER // tk

    def body(p_ref, s_ref, g_ref, o_ref):
        i = pl.program_id(0)

        @pl.when(i == 0)
        def _():
            o_ref[...] = _tn(p_ref[...], g_ref[...]).astype(BF)

        @pl.when(i > 0)
        def _():
            o_ref[...] = _tn(s_ref[...], g_ref[...]).astype(BF)

    return pl.pallas_call(
        body, name="grad_w_out", grid=(1 + n_s,),
        in_specs=[pl.BlockSpec((t, tk), lambda i: (0, 0)), pl.BlockSpec((t, tk), lambda i: (0, jnp.maximum(i - 1, 0))),
                  pl.BlockSpec((t, d), lambda i: (0, 0))],
        out_specs=pl.BlockSpec((tk, d), lambda i: (i, 0)),
        out_shape=_out((POOL_WIDTH + SSD_INNER, d), BF),
        compiler_params=_cp(("parallel",), 56))(*_pin(y_pool, y_ssd, d_o))


def _grad_w_in_t(d_upool, d_z, d_uxbc, d_udt, u1):
    t, d = u1.shape
    tk = 512
    n_z, n_x = SSD_INNER // tk, CONV_CH // tk

    def body(p_ref, z_ref, x_ref, dt_ref, u_ref, o_ref):
        i = pl.program_id(0)

        @pl.when(i == 0)
        def _():
            o_ref[...] = _tn(p_ref[...], u_ref[...]).astype(BF)

        @pl.when((i >= 1) & (i < 1 + n_z))
        def _():
            o_ref[...] = _tn(z_ref[...], u_ref[...]).astype(BF)

        @pl.when((i >= 1 + n_z) & (i < 1 + n_z + n_x))
        def _():
            o_ref[...] = _tn(x_ref[...], u_ref[...]).astype(BF)

        @pl.when(i == 1 + n_z + n_x)
        def _():
            o_ref[0:128, :] = _tn(dt_ref[...], u_ref[...]).astype(BF)

    return pl.pallas_call(
        body, name="grad_w_in", grid=(2 + n_z + n_x,),
        in_specs=[pl.BlockSpec((t, tk), lambda i: (0, 0)),
                  pl.BlockSpec((t, tk), lambda i: (0, jnp.clip(i - 1, 0, n_z - 1))),
                  pl.BlockSpec((t, tk), lambda i: (0, jnp.clip(i - 1 - n_z, 0, n_x - 1))),
                  pl.BlockSpec((t, 128), lambda i: (0, 0)), pl.BlockSpec((t, d), lambda i: (0, 0))],
        out_specs=pl.BlockSpec((tk, d), lambda i: (i, 0)),
        out_shape=_out((IN_PAD, d), BF),
        compiler_params=_cp(("parallel",), 56))(*_pin(d_upool, d_z, d_uxbc, d_udt, u1))


def _conv_bwd(d_pre, proj, conv_w, nb, seq):
    ts = min(256, seq)
    nt = seq // ts
    hb = ts // CONV_HALO
    last_block = nb * seq // CONV_HALO - 1
    n_ext = CHUNK + CONV_HALO

    def body(dp_ref, dnext_ref, u_ref, cw_ref, du_ref, gw_ref, gb_ref):
        b = pl.program_id(0)
        i = pl.program_id(1)

        @pl.when((b == 0) & (i == 0))
        def _():
            gw_ref[...] = jnp.zeros_like(gw_ref)
            gb_ref[...] = jnp.zeros_like(gb_ref)

        for c0 in range(0, CONV_CH, 128):
            cs = slice(c0, c0 + 128)
            cw = cw_ref[:, cs]
            gw = [0.0] * 4
            gb = 0.0
            for r0 in range(0, ts, CHUNK):
                dp = dp_ref[r0:r0 + CHUNK, cs]
                u = u_ref[r0:r0 + CHUNK, cs]
                if r0 + CHUNK < ts:
                    below = dp_ref[r0 + CHUNK:r0 + CHUNK + CONV_HALO, cs]
                else:
                    below = jnp.where(i == nt - 1, 0.0, dnext_ref[:, cs])
                ext_d = jnp.concatenate([dp, below], 0)
                du = dp * cw[3:4]
                gw[3] = gw[3] + jnp.sum(dp * u, 0, keepdims=True)
                for k in (2, 1, 0):
                    shifted = pltpu.roll(ext_d, n_ext - (3 - k), 0)[:CHUNK]
                    du = du + shifted * cw[k:k + 1]
                    gw[k] = gw[k] + jnp.sum(shifted * u, 0, keepdims=True)
                gb = gb + jnp.sum(dp, 0, keepdims=True)
                du_ref[r0:r0 + CHUNK, cs] = du.astype(BF)
            for k in range(4):
                gw_ref[k:k + 1, cs] += gw[k]
            gb_ref[:, cs] += gb

    return pl.pallas_call(
        body, name="conv_bwd", grid=(nb, nt),
        in_specs=[pl.BlockSpec((ts, CONV_CH), lambda b, i: (b * nt + i, 0)),
                  pl.BlockSpec((CONV_HALO, CONV_CH), lambda b, i: (jnp.minimum((b * nt + i + 1) * hb, last_block), 0)),
                  pl.BlockSpec((ts, CONV_CH), lambda b, i: (b * nt + i, 1)),
                  pl.BlockSpec((4, CONV_CH), lambda b, i: (0, 0))],
        out_specs=[pl.BlockSpec((ts, CONV_CH), lambda b, i: (b * nt + i, 0)),
                   pl.BlockSpec((8, CONV_CH), lambda b, i: (0, 0)), pl.BlockSpec((1, CONV_CH), lambda b, i: (0, 0))],
        out_shape=[_out((nb * seq, CONV_CH), BF), _out((8, CONV_CH), F32),
                   _out((1, CONV_CH), F32)],
        compiler_params=_cp(("arbitrary", "arbitrary"), 48))(*_pin(d_pre, d_pre, proj, conv_w))


def _in_proj_bwd(d_parts, w_in_t, x, dh1, mod3, g_mix, seq, token):
    t, d = x.shape
    nb = t // seq
    tm = min(ROW_TILE, seq)
    tps = seq // tm

    widths = [p.shape[1] for p in d_parts]

    def body(d0_ref, d1_ref, d2_ref, d3_ref, w_ref, x_ref, dh1_ref, mod_ref, g_ref, tok_ref, gx_ref, acc_ref, gg_ref):
        i = pl.program_id(0)

        @pl.when(i == 0)
        def _():
            gg_ref[...] = jnp.zeros_like(gg_ref)

        @pl.when(i % tps == 0)
        def _():
            acc_ref[...] = jnp.zeros_like(acc_ref)

        gg = a_shift = a_scale = 0.0
        for rows in _sub_rows(tm):
            d_cat = jnp.concatenate([p_ref[rows, :] for p_ref in (d0_ref, d1_ref, d2_ref)], 1)
            du = _nn(d_cat, w_ref[0:OFF_DT, :]) + _nn(d3_ref[rows, 0:IN_WIDTH - OFF_DT], w_ref[OFF_DT:IN_WIDTH, :])
            xv = x_ref[rows, :]
            r = lax.rsqrt(jnp.mean(xv * xv, -1, keepdims=True) + EPS)
            hh = xv * r
            n1 = hh * g_ref[...]
            dn1 = du * (1.0 + mod_ref[0, 1:2, :])
            dhat = dn1 * g_ref[...]
            gx_ref[rows, :] = dh1_ref[rows, :] + r * (dhat - hh * jnp.mean(dhat * hh, -1, keepdims=True))
            gg = gg + jnp.sum(dn1 * hh, 0, keepdims=True)
            a_shift = a_shift + jnp.sum(du, 0, keepdims=True)
            a_scale = a_scale + jnp.sum(du * n1, 0, keepdims=True)
        gg_ref[...] += gg
        acc_ref[0, 0:1, :] += a_shift
        acc_ref[0, 1:2, :] += a_scale

    row = lambda i: (i, 0)
    vec = pl.BlockSpec((1, d), lambda i: (0, 0))
    return pl.pallas_call(
        body, name="in_proj_bwd", grid=(t // tm,),
        in_specs=[pl.BlockSpec((tm, wd), row) for wd in widths] +
                 [_RESIDENT, pl.BlockSpec((tm, d), row),
                  pl.BlockSpec((tm, d), row), pl.BlockSpec((1, N_MOD, d), lambda i: (i // tps, 0, 0)), vec, _token_spec()],
        out_specs=[pl.BlockSpec((tm, d), row), pl.BlockSpec((1, 8, d), lambda i: (i // tps, 0, 0)), vec],
        out_shape=[_out((t, d), F32), _out((nb, 8, d), F32),
                   _out((1, d), F32)],
        compiler_params=_cp(("arbitrary",), 40))(*_pin(*d_parts), w_in_t, *_pin(x, dh1, mod3, g_mix, token))


_VEC_LAYOUT = (("g_mix", 1024), ("conv_b", 1536), ("g_ssd", 1024), ("pool_scale", 512), ("g_mlp", 1024),
               ("g_final", 1024), ("dt_bias", 128), ("a_log", 128), ("d_skip_lanes", 1024), ("sq_err", 1024))
_VEC_OFFSET = {}
_off = 0
for _name, _n in _VEC_LAYOUT:
    _VEC_OFFSET[_name] = _off
    _off += _n
_SMALL_PARAMS = ("b_ada", "g_mix", "conv_w", "conv_b", "dt_bias", "a_log", "d_skip", "g_ssd", "w_pool", "pool_scale",
                 "g_mlp", "g_final")


def _pack_vec(parts):
    cols = []
    for name, n in _VEC_LAYOUT:
        v = parts[name]
        if v.shape[1] < n:
            v = jnp.pad(v, ((0, 0), (0, n - v.shape[1])))
        cols.append(v)
    return jnp.concatenate(cols, 1)


def _small_adam(vec_all, wpool_all, convw_all, dmod_all, params):
    names = _SMALL_PARAMS
    nin = 4 + 3 * len(names)

    def body(*refs):
        vec_ref, wp_ref, cw_ref, dm_ref = refs[:4]
        prm = {n: refs[4 + 3 * i:7 + 3 * i] for i, n in enumerate(names)}
        loss_ref = refs[nin]
        outs = {n: refs[nin + 1 + 4 * i:nin + 5 + 4 * i] for i, n in enumerate(names)}
        vsum = vec_ref[0]
        for s in range(1, N_DEV):
            vsum = vsum + vec_ref[s]

        def lanes(name, n):
            off = _VEC_OFFSET[name]
            return vsum[:, off:off + n]

        grads = {n: lanes(n, prm[n][0].shape[1]) for n in ("g_mix", "conv_b", "g_ssd", "pool_scale", "g_mlp", "g_final", "dt_bias")}
        grads["a_log"] = lanes("a_log", SSD_HEADS) * (-jnp.exp(prm["a_log"][0][...]))
        per_lane = jnp.broadcast_to(lanes("d_skip_lanes", SSD_INNER), (8, SSD_INNER))
        grads["d_skip"] = _exact_nn(per_lane, _head_reduce_matrix(SSD_INNER, SSD_HEADS))[0:1]
        gwp = wp_ref[0].astype(F32)
        gcw = cw_ref[0]
        gb = jnp.sum(dm_ref[0], 0, keepdims=True)
        for s in range(1, N_DEV):
            gwp = gwp + wp_ref[s].astype(F32)
            gcw = gcw + cw_ref[s]
            gb = gb + jnp.sum(dm_ref[s], 0, keepdims=True)
        grads["w_pool"] = gwp
        grads["conv_w"] = gcw[0:4]
        grads["b_ada"] = gb
        total = jnp.sum(lanes("sq_err", D_MODEL), 1, keepdims=True) * (0.5 / D_MODEL)
        loss_ref[...] = jnp.broadcast_to(total, loss_ref.shape)
        for n in names:
            w_ref, m_ref, v_ref = prm[n]
            g = grads[n]
            d, m2, v2 = _adam_math(w_ref[...], g, m_ref[...], v_ref[...])
            g_ref, d_ref, m2_ref, v2_ref = outs[n]
            g_ref[...] = g
            d_ref[...] = d
            m2_ref[...] = m2
            v2_ref[...] = v2

    flat = [vec_all, wpool_all, convw_all, dmod_all]
    out_shape = [jax.ShapeDtypeStruct((1, 128), F32)]
    for n in names:
        flat += list(params[n])
        out_shape += [jax.ShapeDtypeStruct(params[n][0].shape, F32)] * 4
    vm = pl.BlockSpec(memory_space=pltpu.VMEM)
    res = pl.pallas_call(body, name="small_adam", out_shape=out_shape, in_specs=[vm] * len(flat),
                         out_specs=[vm] * len(out_shape), compiler_params=_cp(vmem_mb=48))(*flat)
    return res[0], {n: res[1 + 4 * i:5 + 4 * i] for i, n in enumerate(names)}


_WEIGHTS = ("w_ada", "b_ada", "g_mix", "w_in", "conv_w", "conv_b", "dt_bias", "a_log", "d_skip", "g_ssd", "w_pool",
            "pool_scale", "w_out", "g_mlp", "w_up", "w_down", "g_final")


def _local_step(x2, tg2, mod3, seq, w_in_t, first_token, weights_arrived, weights_later, start_reduce, before_last,
                conv_w_full, sp):
    t, d = x2.shape
    nb = t // seq
    dskip_e = jnp.repeat(sp["d_skip"], SSD_HEAD_DIM, axis=1)
    proj, u1 = _in_proj(x2, mod3, sp["g_mix"], w_in_t, seq, first_token)
    y_ssd, yssm, h_prev, pre = _ssd_fwd(proj, conv_w_full, sp["conv_b"], sp["dt_bias"], sp["a_log"], dskip_e, sp["g_ssd"], nb, seq)
    y_pool, p = _pool_fwd(proj, sp["w_pool"], sp["pool_scale"], nb, seq, weights_arrived(y_ssd))
    w_out_f, w_up4, w_down4 = weights_later(y_pool)
    w_down_f = w_down4.reshape(D_FF, d)
    h1, o, u2 = _out_proj(y_pool, y_ssd, w_out_f, x2, mod3, sp["g_mlp"], seq)
    a_up = _mlp_up(u2, w_up4)
    d_dn, dh2, sq, gg_final, d_gf = _mlp_down_loss(a_up, w_down_f, h1, mod3, sp["g_final"], tg2, seq)

    gw_down = _tn_matmul(a_up, d_dn, 512, d, "grad_w_down", square_relu=True)
    tok = start_reduce("w_down", gw_down.reshape(N_CHIPS, D_FF // N_CHIPS, d))
    d_a = _mlp_down_bwd(d_dn, w_down4, a_up, tok)
    gw_up4 = _tn_matmul(u2, d_a, 512, d, "grad_w_up", out3=True)
    tok = start_reduce("w_up", gw_up4)
    dh1, d_o, accf, gg_mlp = _mlp_up_bwd(d_a, w_up4, h1, dh2, o, mod3, sp["g_mlp"], seq, tok)
    gw_out = _grad_w_out(y_pool, y_ssd, d_o)
    tok = start_reduce("w_out", gw_out.reshape(N_CHIPS, gw_out.shape[0] // N_CHIPS, d))
    d_ypool, d_yssd = _out_proj_bwd(d_o, w_out_f, tok)
    d_upool, gw_pool, g_ps = _pool_bwd(d_ypool, p, sp["w_pool"], sp["pool_scale"], nb, seq)
    d_z, d_pre, d_udt, gg_ssd, gdsk, ga, gdtb = _ssd_bwd(proj, pre, d_yssd, yssm, h_prev, sp["dt_bias"], sp["a_log"],
                                                        dskip_e, sp["g_ssd"], nb, seq)
    d_uxbc, gconvw, gconvb = _conv_bwd(d_pre, proj, conv_w_full, nb, seq)
    gw_in_t = _grad_w_in_t(d_upool, d_z, d_uxbc, d_udt, u1)
    shard_rows = IN_WIDTH // N_CHIPS
    tok = start_reduce("w_in", jnp.stack([gw_in_t[k * shard_rows:(k + 1) * shard_rows] for k in range(N_CHIPS)]))
    gx, accm, gg_mix = _in_proj_bwd([d_upool, d_z, d_uxbc, d_udt], w_in_t, x2, dh1, mod3, sp["g_mix"], seq, before_last(tok))

    d_mod = jnp.concatenate([accm[:, 0], accm[:, 1], accf[:, 2], accf[:, 0], accf[:, 1], d_gf[:, 0]], 1)
    vec = _pack_vec({"g_mix": gg_mix, "conv_b": gconvb, "g_ssd": gg_ssd, "pool_scale": g_ps, "g_mlp": gg_mlp,
                     "g_final": gg_final, "dt_bias": gdtb, "a_log": ga, "d_skip_lanes": gdsk, "sq_err": sq})
    return gx, d_mod, vec, gw_pool, gconvw


def kernel(x, c, w_ada, b_ada, g_mix, w_in, conv_w, conv_b, dt_bias, a_log, d_skip, g_ssd, w_pool, pool_scale, w_out, g_mlp, w_up, w_down, g_final, loss_target, m_w_ada, m_b_ada, m_g_mix, m_w_in, m_conv_w, m_conv_b, m_dt_bias, m_a_log, m_d_skip, m_g_ssd, m_w_pool, m_pool_scale, m_w_out, m_g_mlp, m_w_up, m_w_down, m_g_final, v_w_ada, v_b_ada, v_g_mix, v_w_in, v_conv_w, v_conv_b, v_dt_bias, v_a_log, v_d_skip, v_g_ssd, v_w_pool, v_pool_scale, v_w_out, v_g_mlp, v_w_up, v_w_down, v_g_final):
    nb, seq, d = x.shape
    t = nb * seq
    xi, yi, ci = _mesh_pos()
    chip = 2 * xi + yi
    me = 4 * xi + 2 * yi + ci
    ada_cols = w_ada.shape[2]
    conv_cols = conv_w.shape[2]
    in_cols = w_in.shape[2]
    w_in_s, m_w_in_s, v_w_in_s = w_in[0].T, m_w_in[0].T, v_w_in[0].T

    c_send, c_recv, c_src, c_land, c_token = _exchange_start([c, conv_w[0]], ALL_PEERS, "cond_start")
    w_in_b = w_in_s.astype(BF)
    i_send, i_recv, i_src, i_land, in_token = _ici_start(
        [w_in_b], [jax.ShapeDtypeStruct((N_CHIPS,) + w_in_b.shape, BF)], _gather_sent, _gather_landing, "gather_start_w_in",
        after=c_token)
    c_own, c_got = _exchange_wait(c_send, c_recv, c_src, c_land, ALL_PEERS, in_token, "cond_wait")
    c8, convw8 = [lax.dynamic_update_slice(got, mine[None], (me,) + (0,) * mine.ndim) for got, mine in zip(c_got, c_own)]
    c_all = c8.reshape(N_DEV * nb, d)
    conv_w_full = convw8[0::2].transpose(1, 0, 2).reshape(4, N_CHIPS * conv_cols)
    b_shard = lax.dynamic_slice(b_ada, (0, chip * ada_cols), (1, ada_cols))
    mod_part, c_act = _ada_mod(c_all, w_ada[0], b_shard, in_token)
    mod_rows = mod_part.reshape(N_DEV, nb, ada_cols)
    m_send, m_recv, m_src, m_land, _ = _ici_start(
        [mod_rows], [jax.ShapeDtypeStruct((N_CHIPS, nb, ada_cols), F32)], _mod_sent, _mod_landing, "mod_start", after=mod_part)

    later = [w_out[0].astype(BF), w_up[0].astype(BF), w_down[0].astype(BF)]
    in_shard, in_land = _ici_wait(i_send, i_recv, i_src, i_land, [m_src[0]] + later, _gather_sent, _gather_landing,
                                  "gather_wait_w_in")
    (w_in4,) = _gather_finish(in_land, in_shard)
    w_in_t = w_in4.reshape(N_CHIPS * in_cols, d)
    mod_mine, mod_land = _ici_wait(m_send, m_recv, m_src, m_land, w_in_t, _mod_sent, _mod_landing, "mod_wait")
    mod_own = lax.dynamic_slice(mod_mine[0], (me, 0, 0), (1, nb, ada_cols))
    mod4 = lax.dynamic_update_slice(mod_land[0], mod_own, (chip, 0, 0))
    mod3 = mod4.transpose(1, 0, 2).reshape(nb, N_MOD, d)
    g_send, g_recv, g_src, g_land, first_token = _ici_start(
        later, [jax.ShapeDtypeStruct((N_CHIPS,) + s.shape, BF) for s in later], _gather_sent, _gather_landing, "gather_start",
        after=w_in4)

    def weights_arrived(after):
        shards, lands = _ici_wait(g_send, g_recv, g_src, g_land, after, _gather_sent, _gather_landing, "gather_wait")
        pending["forward"] = _forward_start(lands, "forward_start") + (shards,)
        return pending["forward"][3]

    def weights_later(after):
        f_send, f_recv, f_land, _, shards = pending["forward"]
        lands = _forward_wait(f_send, f_recv, f_land, after, "forward_wait")
        w_out4, w_up4, w_down4 = [lax.dynamic_update_slice(land, shard[None], (chip, 0, 0)) for land, shard in zip(lands, shards)]
        return w_out4.reshape(N_CHIPS * w_out.shape[1], d), w_up4, w_down4

    pending = {}

    def start_reduce(name, grad4):
        pending[name] = _reduce_start(grad4, "reduce_start_" + name)
        return pending[name][4]

    pos = jnp.stack([ci, chip, me]).astype(jnp.int32)
    early = ("w_out", "w_up", "w_down")

    def summed_half(name, after):
        r_send, r_recv, r_src, r_land, _ = pending[name]
        own, recv = _reduce_wait(r_send, r_recv, r_src, r_land, after, "reduce_wait_" + name)
        return _sum_eight(recv, own, pos)

    def before_last(token):
        pending["early_halves"] = _exchange_start([summed_half(n, token) for n in early], SIBLING, "halves_start_early")
        return pending["early_halves"][4]

    sp = dict(g_mix=g_mix, conv_b=conv_b, dt_bias=dt_bias, a_log=a_log, d_skip=d_skip, g_ssd=g_ssd,
              w_pool=w_pool[0], pool_scale=pool_scale, g_mlp=g_mlp, g_final=g_final.reshape(1, d))
    gx, d_mod, vec, gw_pool, gconvw = _local_step(
        x.reshape(t, d), loss_target.reshape(t, d), mod3, seq, w_in_t, first_token, weights_arrived, weights_later, start_reduce,
        before_last, conv_w_full, sp)

    small_parts = [vec, gw_pool.reshape(4 * POOL_GROUP, POOL_GROUP).astype(BF), gconvw, d_mod]
    s_send, s_recv, s_src, s_land, s_token = _exchange_start(small_parts, ALL_PEERS, "small_start")
    h_send, h_recv, h_src, h_land, h_token = _exchange_start([summed_half("w_in", s_token)], SIBLING, "halves_start")
    e_send, e_recv, e_src, e_land, _ = pending["early_halves"]
    e_own, e_got = _exchange_wait(e_send, e_recv, e_src, e_land, SIBLING, h_token, "halves_wait_early")
    res = {}
    for i, (n, w, m, v) in enumerate((("w_out", w_out, m_w_out, v_w_out), ("w_up", w_up, m_w_up, v_w_up),
                                      ("w_down", w_down, m_w_down, v_w_down))):
        g, dl, m2, v2 = _adam_big(e_own[i], e_got[i], w[0], m[0], v[0], pos)
        res[n] = (g[None], dl[None], m2[None], v2[None])

    s_own, s_got = _exchange_wait(s_send, s_recv, s_src, s_land, ALL_PEERS, res["w_down"][1], "small_wait")
    vec8, wpool8, convw8g, dmod8 = [lax.dynamic_update_slice(got, mine[None], (me,) + (0,) * mine.ndim)
                                    for got, mine in zip(s_got, s_own)]
    convw8s = lax.dynamic_slice(convw8g, (0, 0, chip * conv_cols), (N_DEV, 8, conv_cols))
    m_in = dict(b_ada=m_b_ada, g_mix=m_g_mix, conv_w=m_conv_w[0], conv_b=m_conv_b, dt_bias=m_dt_bias, a_log=m_a_log,
                d_skip=m_d_skip, g_ssd=m_g_ssd, w_pool=m_w_pool.reshape(4 * POOL_GROUP, POOL_GROUP), pool_scale=m_pool_scale,
                g_mlp=m_g_mlp, g_final=m_g_final.reshape(1, d))
    v_in = dict(b_ada=v_b_ada, g_mix=v_g_mix, conv_w=v_conv_w[0], conv_b=v_conv_b, dt_bias=v_dt_bias, a_log=v_a_log,
                d_skip=v_d_skip, g_ssd=v_g_ssd, w_pool=v_w_pool.reshape(4 * POOL_GROUP, POOL_GROUP), pool_scale=v_pool_scale,
                g_mlp=v_g_mlp, g_final=v_g_final.reshape(1, d))
    w_small = dict(sp, b_ada=b_ada, conv_w=conv_w[0], w_pool=w_pool.reshape(4 * POOL_GROUP, POOL_GROUP))
    loss_row, small = _small_adam(vec8, wpool8, convw8s, dmod8, {n: (w_small[n], m_in[n], v_in[n]) for n in _SMALL_PARAMS})

    dmod_all = dmod8.reshape(N_DEV * nb, N_CHIPS * ada_cols)
    dmod_cols = lax.dynamic_slice(dmod_all, (0, chip * ada_cols), (N_DEV * nb, ada_cols))
    res.update({n: tuple(r.reshape(w.shape) for r in small[n])
                for n, w in (("b_ada", b_ada), ("g_mix", g_mix), ("conv_w", conv_w), ("conv_b", conv_b), ("dt_bias", dt_bias),
                             ("a_log", a_log), ("d_skip", d_skip), ("g_ssd", g_ssd), ("w_pool", w_pool),
                             ("pool_scale", pool_scale), ("g_mlp", g_mlp), ("g_final", g_final))})
    g_ada, d_ada, m_ada, v_ada = _adam_ada(c_act.T.astype(BF), dmod_cols, w_ada[0], m_w_ada[0], v_w_ada[0])
    res["w_ada"] = (g_ada[None], d_ada[None], m_ada[None], v_ada[None])
    h_own, h_got = _exchange_wait(h_send, h_recv, h_src, h_land, SIBLING, g_ada, "halves_wait")
    rows3 = lambda a: jnp.transpose(a, (2, 0, 1))
    res["w_in"] = tuple(jnp.transpose(r, (1, 2, 0))
                        for r in _adam_rows(h_own[0], h_got[0], rows3(w_in), rows3(m_w_in), rows3(v_w_in), pos))

    loss = loss_row[0, 0]
    return (loss, gx.reshape(nb, seq, d), *[res[n][0] for n in _WEIGHTS], *[res[n][1] for n in _WEIGHTS],
            *[res[n][2] for n in _WEIGHTS], *[res[n][3] for n in _WEIGHTS])
```

```python
import jax
import jax.numpy as jnp
from jax import lax
from jax.experimental import pallas as pl
from jax.experimental.pallas import tpu as pltpu

F32 = jnp.float32
BF = jnp.bfloat16
MESH = pl.DeviceIdType.MESH

EPS = 1e-5
D_MODEL = 1024
POOL_WIDTH = 512
POOL_WINDOWS = (2, 4, 8, 16)
POOL_GROUP = 128
SSD_INNER = 1024
SSD_HEADS = 16
SSD_HEAD_DIM = 64
SSD_STATE = 128
GROUP_W = 512
CHUNK = 128
CONV_CH = 1536
OFF_DT = 3072
IN_WIDTH = 3088
IN_PAD = 3200
D_FF = 4096
N_MOD = 6
N_CHIPS = 4
N_DEV = 8
HALO = 16
CONV_HALO = 8

ADAM_LR = 0.001
ADAM_B1 = 0.9
ADAM_B2 = 0.999
ADAM_EPS = 1e-08
ADAM_WD = 0.01
ADAM_STEP = 10

VMEM_BYTES_V7X = 64 * 1024 * 1024


def _cp(semantics=None, vmem_mb=48, **kw):
    assert vmem_mb * 1024 * 1024 < VMEM_BYTES_V7X
    args = dict(vmem_limit_bytes=vmem_mb * 1024 * 1024, **kw)
    if semantics is not None:
        args["dimension_semantics"] = semantics
    return pltpu.CompilerParams(**args)


def _out(shape, dtype):
    return pltpu.HBM(shape, dtype)


def _pin(*arrays):
    return [pltpu.with_memory_space_constraint(a, pltpu.HBM) for a in arrays]


TOKEN_SHAPE = (8, 128)


def _order_operand(a):
    if a.shape == TOKEN_SHAPE:
        return pl.BlockSpec(memory_space=pltpu.VMEM), a
    return pl.BlockSpec(memory_space=pltpu.HBM), pltpu.with_memory_space_constraint(a, pltpu.HBM)


def _nn(a, b):
    return jnp.dot(a, b, preferred_element_type=F32)


def _nt(a, b):
    return lax.dot_general(a, b, (((1,), (1,)), ((), ())), preferred_element_type=F32)


def _tn(a, b):
    return lax.dot_general(a, b, (((0,), (0,)), ((), ())), preferred_element_type=F32)


def _split3(v):
    hi = v.astype(BF)
    r1 = v - hi.astype(F32)
    mid = r1.astype(BF)
    lo = (r1 - mid.astype(F32)).astype(BF)
    return hi, mid, lo


def _exact_nn(v, m01):
    hi, mid, lo = _split3(v)
    return _nn(hi, m01) + _nn(mid, m01) + _nn(lo, m01)


def _exact_nn_left(m01, v):
    hi, mid, lo = _split3(v)
    return _nn(m01, hi) + _nn(m01, mid) + _nn(m01, lo)


def _exact_nt_left(m01, v):
    hi, mid, lo = _split3(v)
    return _nt(m01, hi) + _nt(m01, mid) + _nt(m01, lo)


def _sigmoid(v):
    return 1.0 / (1.0 + jnp.exp(-v))


def _iota(shape, dim):
    return lax.broadcasted_iota(jnp.int32, shape, dim)


def _head_expand_matrix(heads, width):
    return (_iota((heads, width), 1) // SSD_HEAD_DIM == _iota((heads, width), 0)).astype(BF)


def _head_reduce_matrix(width, heads):
    return (_iota((width, heads), 0) // SSD_HEAD_DIM == _iota((width, heads), 1)).astype(BF)


def _mesh_pos():
    return lax.axis_index("x"), lax.axis_index("y"), lax.axis_index("c")


def _flip(v, bit):
    return v + bit - 2 * bit * v


_HBM = pl.BlockSpec(memory_space=pltpu.HBM)
_SEM = pl.BlockSpec(memory_space=pltpu.SEMAPHORE)
_DATAFLOW = pltpu.SideEffectType.DATAFLOW_SIDE_EFFECTING


def _peer_chip(x, y, j):
    return _flip(x, (j >> 1) & 1), _flip(y, j & 1)


def _all_peers():
    x, y, c = _mesh_pos()
    return [(_flip(x, (k >> 2) & 1), _flip(y, (k >> 1) & 1), _flip(c, k & 1)) for k in range(1, N_DEV)]


_START_IDS = ("sibling", "gather_start_w_in", "mod_start", "gather_start", "reduce_start_w_down", "reduce_start_w_up",
              "reduce_start_w_out", "reduce_start_w_in", "small_start", "cond_start")


def _start_params(name):
    return pltpu.CompilerParams(has_side_effects=_DATAFLOW, collective_id=_START_IDS.index(name))


def _handshake(peers):
    barrier = pltpu.get_barrier_semaphore()
    for peer in peers:
        pl.semaphore_signal(barrier, inc=1, device_id=peer, device_id_type=MESH)
    pl.semaphore_wait(barrier, len(peers))


def _ici_start(srcs, land_shapes, sent, landing, name, after):
    n = len(srcs)

    def body(*refs):
        src_refs, land_refs = refs[:n], refs[n:2 * n]
        send_sems, recv_sems = refs[2 * n + 1], refs[2 * n + 2]
        token = refs[-1]
        x, y, c = _mesh_pos()
        _handshake([(*_peer_chip(x, y, j), c) for j in range(1, N_CHIPS)])
        for j in range(1, N_CHIPS):
            px, py = _peer_chip(x, y, j)
            for a in range(n):
                pltpu.make_async_remote_copy(
                    src_ref=sent(src_refs[a], c, 2 * px + py), dst_ref=landing(land_refs[a], c, 2 * x + y),
                    send_sem=send_sems.at[a * (N_CHIPS - 1) + j - 1], recv_sem=recv_sems.at[a * (N_CHIPS - 1) + j - 1],
                    device_id=(px, py, c), device_id_type=MESH).start()
        token[...] = jnp.zeros_like(token)

    sems = pltpu.SemaphoreType.DMA((n * (N_CHIPS - 1),))
    after_spec, after = _order_operand(after)
    lands = [pltpu.with_memory_space_constraint(lax.empty(s.shape, s.dtype), pltpu.HBM) for s in land_shapes]
    outs = pl.pallas_call(
        body, name=name,
        out_shape=(sems, sems, *[pltpu.HBM(s.shape, s.dtype) for s in srcs],
                   *[pltpu.HBM(s.shape, s.dtype) for s in land_shapes], jax.ShapeDtypeStruct((8, 128), F32)),
        in_specs=[_HBM] * (2 * n) + [after_spec],
        out_specs=[_SEM, _SEM] + [_HBM] * (2 * n) + [pl.BlockSpec(memory_space=pltpu.VMEM)],
        input_output_aliases={i: 2 + i for i in range(2 * n)},
        compiler_params=_start_params(name),
    )(*_pin(*srcs), *lands, after)
    return outs[0], outs[1], outs[2:2 + n], outs[2 + n:2 + 2 * n], outs[-1]


def _ici_wait(send_sems, recv_sems, src_thru, land_thru, after, sent, landing, name):
    n = len(src_thru)
    afters = [_order_operand(a) for a in (after if isinstance(after, (list, tuple)) else [after])]

    def body(*refs):
        src_refs, land_refs = refs[:n], refs[n:2 * n]
        send_sems, recv_sems = refs[2 * n], refs[2 * n + 1]
        x, y, c = _mesh_pos()
        for j in range(1, N_CHIPS):
            px, py = _peer_chip(x, y, j)
            for a in range(n):
                cp = pltpu.make_async_remote_copy(
                    src_ref=sent(src_refs[a], c, 2 * px + py), dst_ref=landing(land_refs[a], c, 2 * px + py),
                    send_sem=send_sems.at[a * (N_CHIPS - 1) + j - 1], recv_sem=recv_sems.at[a * (N_CHIPS - 1) + j - 1],
                    device_id=(px, py, c), device_id_type=MESH)
                cp.wait_send()
                cp.wait_recv()

    outs = pl.pallas_call(
        body, name=name,
        out_shape=tuple(pltpu.HBM(s.shape, s.dtype) for s in (*src_thru, *land_thru)),
        in_specs=[_HBM] * (2 * n) + [_SEM, _SEM] + [s for s, _ in afters], out_specs=[_HBM] * (2 * n),
        input_output_aliases={i: i for i in range(2 * n)},
        compiler_params=pltpu.CompilerParams(has_side_effects=_DATAFLOW),
    )(*src_thru, *land_thru, send_sems, recv_sems, *[a for _, a in afters])
    return outs[:n], outs[n:]


def _col_half(ref, which, lead=()):
    hc = ref.shape[-1] // 2
    return ref.at[(*lead, slice(None), pl.ds(pl.multiple_of(which * hc, 128), hc))]


def _gather_sent(ref, c, dst_chip):
    return _col_half(ref, c)


def _gather_landing(ref, c, src_chip):
    return _col_half(ref, c, lead=(src_chip,))


def _mod_sent(ref, c, dst_chip):
    return ref.at[2 * dst_chip + c]


def _mod_landing(ref, c, src_chip):
    return ref.at[src_chip]


def _reduce_copy(src_ref, land_ref, send_sems, recv_sems, k, receiving):
    x, y, c = _mesh_pos()
    px, py, pc = _flip(x, (k >> 2) & 1), _flip(y, (k >> 1) & 1), _flip(c, k & 1)
    hc = src_ref.shape[2] // 2
    src = src_ref.at[2 * px + py, :, pl.ds(pl.multiple_of(pc * hc, 128), hc)]
    slot = (4 * px + 2 * py + pc) if receiving else (4 * x + 2 * y + c)
    return pltpu.make_async_remote_copy(
        src_ref=src, dst_ref=land_ref.at[slot], send_sem=send_sems.at[k - 1], recv_sem=recv_sems.at[k - 1],
        device_id=(px, py, pc), device_id_type=MESH)


def _reduce_start(grad4, name):
    k4, r, cols = grad4.shape

    def body(src_ref, land_ref, send_sems, recv_sems, src_thru, land_thru, token):
        _handshake(_all_peers())
        for k in range(1, N_DEV):
            _reduce_copy(src_ref, land_ref, send_sems, recv_sems, k, receiving=False).start()
        token[...] = jnp.zeros_like(token)

    sems = pltpu.SemaphoreType.DMA((N_DEV - 1,))
    land = pltpu.with_memory_space_constraint(lax.empty((N_DEV, r, cols // 2), grad4.dtype), pltpu.HBM)
    return pl.pallas_call(
        body, name=name,
        out_shape=(sems, sems, pltpu.HBM(grad4.shape, grad4.dtype), pltpu.HBM(land.shape, land.dtype),
                   jax.ShapeDtypeStruct((8, 128), F32)),
        in_specs=[_HBM, _HBM], out_specs=[_SEM, _SEM, _HBM, _HBM, pl.BlockSpec(memory_space=pltpu.VMEM)],
        input_output_aliases={0: 2, 1: 3},
        compiler_params=_start_params(name),
    )(*_pin(grad4), land)


def _reduce_wait(send_sems, recv_sems, src_thru, land_thru, after, name):
    def body(src_ref, land_ref, send_sems, recv_sems, after_ref, src_out, land_out):
        for k in range(1, N_DEV):
            cp = _reduce_copy(src_ref, land_ref, send_sems, recv_sems, k, receiving=True)
            cp.wait_send()
            cp.wait_recv()

    return pl.pallas_call(
        body, name=name,
        out_shape=(pltpu.HBM(src_thru.shape, src_thru.dtype), pltpu.HBM(land_thru.shape, land_thru.dtype)),
        in_specs=[_HBM, _HBM, _SEM, _SEM, _order_operand(after)[0]], out_specs=[_HBM, _HBM],
        input_output_aliases={0: 0, 1: 1},
        compiler_params=pltpu.CompilerParams(has_side_effects=_DATAFLOW),
    )(src_thru, land_thru, send_sems, recv_sems, _order_operand(after)[1])


SIBLING_COLLECTIVE_ID = 0
_SIBLING_PARAMS = pltpu.CompilerParams(has_side_effects=_DATAFLOW, collective_id=SIBLING_COLLECTIVE_ID)


def _sibling_handshake():
    x, y, c = _mesh_pos()
    barrier = pltpu.get_barrier_semaphore()
    pl.semaphore_signal(barrier, inc=1, device_id=(x, y, 1 - c), device_id_type=MESH)
    pl.semaphore_wait(barrier, 1)


def _peer_copy(src_ref, land_ref, send_sems, recv_sems, idx, k, receiving):
    x, y, c = _mesh_pos()
    px, py, pc = _flip(x, (k >> 2) & 1), _flip(y, (k >> 1) & 1), _flip(c, k & 1)
    if land_ref.shape[0] == N_DEV:
        slot = (4 * px + 2 * py + pc) if receiving else (4 * x + 2 * y + c)
    else:
        slot = pc if receiving else c
    return pltpu.make_async_remote_copy(
        src_ref=src_ref, dst_ref=land_ref.at[slot], send_sem=send_sems.at[idx], recv_sem=recv_sems.at[idx],
        device_id=(px, py, pc), device_id_type=MESH)


def _exchange_start(arrays, peers, name):
    n = len(arrays)

    def body(*refs):
        src_refs, land_refs = refs[:n], refs[n:2 * n]
        send_sems, recv_sems = refs[2 * n], refs[2 * n + 1]
        token = refs[-1]
        if peers == SIBLING:
            _sibling_handshake()
        else:
            _handshake(_all_peers())
        for j, k in enumerate(peers):
            for a in range(n):
                _peer_copy(src_refs[a], land_refs[a], send_sems, recv_sems, a * len(peers) + j, k, receiving=False).start()
        token[...] = jnp.zeros_like(token)

    sems = pltpu.SemaphoreType.DMA((n * len(peers),))
    n_slots = N_DEV if len(peers) > 1 else 2
    lands = [pltpu.with_memory_space_constraint(lax.empty((n_slots,) + a.shape, a.dtype), pltpu.HBM) for a in arrays]
    outs = pl.pallas_call(
        body, name=name,
        out_shape=(sems, sems, *[pltpu.HBM(a.shape, a.dtype) for a in arrays], *[pltpu.HBM(l.shape, l.dtype) for l in lands],
                   jax.ShapeDtypeStruct((8, 128), F32)),
        in_specs=[_HBM] * (2 * n), out_specs=[_SEM, _SEM] + [_HBM] * (2 * n) + [pl.BlockSpec(memory_space=pltpu.VMEM)],
        input_output_aliases={i: 2 + i for i in range(2 * n)},
        compiler_params=_SIBLING_PARAMS if peers == SIBLING else _start_params(name),
    )(*_pin(*arrays), *lands)
    return outs[0], outs[1], outs[2:2 + n], outs[2 + n:2 + 2 * n], outs[-1]


def _exchange_wait(send_sems, recv_sems, src_thru, land_thru, peers, after, name):
    n = len(src_thru)

    def body(*refs):
        src_refs, land_refs = refs[:n], refs[n:2 * n]
        send_sems, recv_sems = refs[2 * n], refs[2 * n + 1]
        for j, k in enumerate(peers):
            for a in range(n):
                cp = _peer_copy(src_refs[a], land_refs[a], send_sems, recv_sems, a * len(peers) + j, k, receiving=True)
                cp.wait_send()
                cp.wait_recv()

    outs = pl.pallas_call(
        body, name=name,
        out_shape=tuple(pltpu.HBM(s.shape, s.dtype) for s in (*src_thru, *land_thru)),
        in_specs=[_HBM] * (2 * n) + [_SEM, _SEM, _order_operand(after)[0]], out_specs=[_HBM] * (2 * n),
        input_output_aliases={i: i for i in range(2 * n)},
        compiler_params=pltpu.CompilerParams(has_side_effects=_DATAFLOW),
    )(*src_thru, *land_thru, send_sems, recv_sems, _order_operand(after)[1])
    return outs[:n], outs[n:]


ALL_PEERS = tuple(range(1, N_DEV))
SIBLING = (1,)


def _sum_eight(recv, grad4, pos):
    n, r, hc = recv.shape
    steps = 2
    tc = hc // steps

    def body(pos_ref, r_ref, g_ref, o_ref):
        me = pos_ref[2]
        o_ref[...] = jnp.zeros_like(o_ref)
        for s in range(n):
            @pl.when(me == s)
            def _():
                o_ref[...] += g_ref[0].astype(F32)

            @pl.when(me != s)
            def _():
                o_ref[...] += r_ref[s].astype(F32)

    grid_spec = pltpu.PrefetchScalarGridSpec(
        num_scalar_prefetch=1, grid=(steps,),
        in_specs=[pl.BlockSpec((n, r, tc), lambda i, pos: (0, 0, i)),
                  pl.BlockSpec((1, r, tc), lambda i, pos: (pos[1], 0, pos[0] * steps + i))],
        out_specs=pl.BlockSpec((r, tc), lambda i, pos: (0, i)))
    return pl.pallas_call(body, name="sum_eight", grid_spec=grid_spec, out_shape=_out((r, hc), F32),
                          compiler_params=_cp(("parallel",), 32))(pos, *_pin(recv, grad4))


def _forward_copy(land_ref, send_sems, recv_sems, idx, j, receiving):
    x, y, c = _mesh_pos()
    px, py = _peer_chip(x, y, j)
    mine = _col_half(land_ref, c, lead=(2 * px + py,))
    theirs = _col_half(land_ref, 1 - c, lead=(2 * px + py,))
    return pltpu.make_async_remote_copy(
        src_ref=mine, dst_ref=theirs if receiving else mine, send_sem=send_sems.at[idx], recv_sem=recv_sems.at[idx],
        device_id=(x, y, 1 - c), device_id_type=MESH)


def _forward_start(lands, name):
    n = len(lands)

    def body(*refs):
        land_refs, send_sems, recv_sems, token = refs[:n], refs[n], refs[n + 1], refs[-1]
        _sibling_handshake()
        for j in range(1, N_CHIPS):
            for a in range(n):
                _forward_copy(land_refs[a], send_sems, recv_sems, a * (N_CHIPS - 1) + j - 1, j, receiving=False).start()
        token[...] = jnp.zeros_like(token)

    sems = pltpu.SemaphoreType.DMA((n * (N_CHIPS - 1),))
    outs = pl.pallas_call(
        body, name=name,
        out_shape=(sems, sems, *[pltpu.HBM(l.shape, l.dtype) for l in lands], jax.ShapeDtypeStruct((8, 128), F32)),
        in_specs=[_HBM] * n, out_specs=[_SEM, _SEM] + [_HBM] * n + [pl.BlockSpec(memory_space=pltpu.VMEM)],
        input_output_aliases={i: 2 + i for i in range(n)},
        compiler_params=_SIBLING_PARAMS,
    )(*lands)
    return outs[0], outs[1], outs[2:2 + n], outs[-1]


def _forward_wait(send_sems, recv_sems, lands_thru, after, name):
    n = len(lands_thru)

    def body(*refs):
        land_refs, send_sems, recv_sems = refs[:n], refs[n], refs[n + 1]
        for j in range(1, N_CHIPS):
            for a in range(n):
                cp = _forward_copy(land_refs[a], send_sems, recv_sems, a * (N_CHIPS - 1) + j - 1, j, receiving=True)
                cp.wait_send()
                cp.wait_recv()

    return pl.pallas_call(
        body, name=name,
        out_shape=tuple(pltpu.HBM(l.shape, l.dtype) for l in lands_thru),
        in_specs=[_HBM] * n + [_SEM, _SEM, _order_operand(after)[0]], out_specs=[_HBM] * n,
        input_output_aliases={i: i for i in range(n)},
        compiler_params=pltpu.CompilerParams(has_side_effects=_DATAFLOW),
    )(*lands_thru, send_sems, recv_sems, _order_operand(after)[1])


def _gather_finish(lands, shards):
    n = len(lands)
    any_spec = _HBM

    def body(*refs):
        shard_refs, out_refs = refs[n:2 * n], refs[2 * n:3 * n]
        send_sems, recv_sems, local_sems = refs[3 * n:]
        x, y, c = _mesh_pos()
        chip = 2 * x + y
        local, sends = [], []
        for a in range(n):
            cp = pltpu.make_async_copy(shard_refs[a], out_refs[a].at[chip], local_sems.at[a])
            cp.start()
            local.append(cp)
        _sibling_handshake()
        for j in range(1, N_CHIPS):
            px, py = _peer_chip(x, y, j)
            for a in range(n):
                landed = _col_half(out_refs[a], c, lead=(2 * px + py,))
                cp = pltpu.make_async_remote_copy(
                    src_ref=landed, dst_ref=landed, send_sem=send_sems.at[a, j], recv_sem=recv_sems.at[a, j],
                    device_id=(x, y, 1 - c), device_id_type=MESH)
                cp.start()
                sends.append(cp)
        for j in range(1, N_CHIPS):
            px, py = _peer_chip(x, y, j)
            for a in range(n):
                other = _col_half(out_refs[a], 1 - c, lead=(2 * px + py,))
                pltpu.make_async_remote_copy(
                    src_ref=other, dst_ref=other, send_sem=send_sems.at[a, j], recv_sem=recv_sems.at[a, j],
                    device_id=(x, y, 1 - c), device_id_type=MESH).wait_recv()
        for cp in sends:
            cp.wait_send()
        for cp in local:
            cp.wait()

    return pl.pallas_call(
        body, name="gather_finish",
        out_shape=[_out(l.shape, l.dtype) for l in lands],
        in_specs=[any_spec] * (2 * n), out_specs=[any_spec] * n,
        input_output_aliases={i: i for i in range(n)},
        scratch_shapes=[pltpu.SemaphoreType.DMA((n, N_CHIPS))] * 2 + [pltpu.SemaphoreType.DMA((n,))],
        compiler_params=_cp(vmem_mb=16, collective_id=SIBLING_COLLECTIVE_ID),
    )(*lands, *shards)


def _adam_math(w, g, m, v):
    m2 = ADAM_B1 * m + (1.0 - ADAM_B1) * g
    v2 = ADAM_B2 * v + (1.0 - ADAM_B2) * (g * g)
    m_hat = m2 / (1.0 - ADAM_B1 ** ADAM_STEP)
    v_hat = v2 / (1.0 - ADAM_B2 ** ADAM_STEP)
    delta = -ADAM_LR * (m_hat / (jnp.sqrt(v_hat) + ADAM_EPS) + ADAM_WD * w)
    return delta, m2, v2


def _adam_big(g_own, g_pair, w, m, v, pos):
    r, c = w.shape
    per_half = 2
    tc = c // (2 * per_half)

    def body(pos_ref, go_ref, gp_ref, w_ref, m_ref, v_ref, g_ref, d_ref, m2_ref, v2_ref):
        half = pl.program_id(0) // per_half
        g = jnp.where(half == pos_ref[0], go_ref[...], gp_ref[0])
        d, m2, v2 = _adam_math(w_ref[...], g, m_ref[...], v_ref[...])
        g_ref[...] = g
        d_ref[...] = d
        m2_ref[...] = m2
        v2_ref[...] = v2

    spec = pl.BlockSpec((r, tc), lambda i, pos: (0, i))
    grid_spec = pltpu.PrefetchScalarGridSpec(
        num_scalar_prefetch=1, grid=(2 * per_half,),
        in_specs=[pl.BlockSpec((r, tc), lambda i, pos: (0, i % per_half)),
                  pl.BlockSpec((1, r, tc), lambda i, pos: (1 - pos[0], 0, i % per_half)), spec, spec, spec],
        out_specs=[spec] * 4)
    sh = _out((r, c), F32)
    return pl.pallas_call(body, name="adam_big", grid_spec=grid_spec, out_shape=[sh] * 4,
                          compiler_params=_cp(("parallel",), 32))(pos, *_pin(g_own, g_pair, w, m, v))


def _adam_rows(g_own, g_pair, w3, m3, v3, pos):
    r, _, c = w3.shape
    tr = 128

    def body(pos_ref, go_ref, gp_ref, w_ref, m_ref, v_ref, g_ref, d_ref, m2_ref, v2_ref):
        core = pos_ref[0]
        g = jnp.concatenate([jnp.where(core == 0, go_ref[...], gp_ref[0]), jnp.where(core == 1, go_ref[...], gp_ref[0])], 1)
        d, m2, v2 = _adam_math(w_ref[:, 0, :], g, m_ref[:, 0, :], v_ref[:, 0, :])
        g_ref[:, 0, :] = g
        d_ref[:, 0, :] = d
        m2_ref[:, 0, :] = m2
        v2_ref[:, 0, :] = v2

    spec = pl.BlockSpec((tr, 1, c), lambda i, pos: (i, 0, 0))
    grid_spec = pltpu.PrefetchScalarGridSpec(
        num_scalar_prefetch=1, grid=(pl.cdiv(r, tr),),
        in_specs=[pl.BlockSpec((tr, c // 2), lambda i, pos: (i, 0)),
                  pl.BlockSpec((1, tr, c // 2), lambda i, pos: (1 - pos[0], i, 0)), spec, spec, spec],
        out_specs=[spec] * 4)
    sh = _out(w3.shape, F32)
    return pl.pallas_call(body, name="adam_rows", grid_spec=grid_spec, out_shape=[sh] * 4,
                          compiler_params=_cp(("parallel",), 32))(pos, *_pin(g_own, g_pair, w3, m3, v3))


def _adam_ada(c_act_t, dmod_cols, w, m, v):
    r, c = w.shape
    tc = 512

    def body(ct_ref, dm_ref, w_ref, m_ref, v_ref, g_ref, d_ref, m2_ref, v2_ref):
        g = _nn(ct_ref[...], dm_ref[...].astype(BF))
        d, m2, v2 = _adam_math(w_ref[...], g, m_ref[...], v_ref[...])
        g_ref[...] = g
        d_ref[...] = d
        m2_ref[...] = m2
        v2_ref[...] = v2

    spec = pl.BlockSpec((r, tc), lambda i: (0, i))
    sh = _out((r, c), F32)
    return pl.pallas_call(
        body, name="adam_ada", grid=(c // tc,),
        in_specs=[pl.BlockSpec(c_act_t.shape, lambda i: (0, 0)), pl.BlockSpec((dmod_cols.shape[0], tc), lambda i: (0, i)),
                  spec, spec, spec],
        out_specs=[spec] * 4, out_shape=[sh] * 4, compiler_params=_cp(("parallel",), 48))(*_pin(c_act_t, dmod_cols, w, m, v))


def _ada_mod(c_all, w_shard, b_shard, token):
    nb, d = c_all.shape
    cols = w_shard.shape[1]
    tc = 512

    def body(c_ref, w_ref, b_ref, tok_ref, mod_ref, act_ref):
        cv = c_ref[...]
        act = cv * _sigmoid(cv)
        act_ref[...] = act
        mod_ref[...] = _nn(act.astype(BF), w_ref[...].astype(BF)) + b_ref[...]

    return pl.pallas_call(
        body, name="ada_mod", grid=(cols // tc,),
        in_specs=[pl.BlockSpec((nb, d), lambda i: (0, 0)), pl.BlockSpec((d, tc), lambda i: (0, i)),
                  pl.BlockSpec((1, tc), lambda i: (0, i)), _token_spec()],
        out_specs=[pl.BlockSpec((nb, tc), lambda i: (0, i)), pl.BlockSpec((nb, d), lambda i: (0, 0))],
        out_shape=[_out((nb, cols), F32), _out((nb, d), F32)],
        compiler_params=_cp(("arbitrary",), 32))(*_pin(c_all, w_shard, b_shard, token))


SUB_ROWS = 256
ROW_TILE = 512


def _sub_rows(tm):
    return [slice(s, s + SUB_ROWS) for s in range(0, tm, SUB_ROWS)] if tm > SUB_ROWS else [slice(0, tm)]


_RESIDENT = pl.BlockSpec(memory_space=pltpu.VMEM)


def _token_spec():
    return pl.BlockSpec((8, 128), lambda *_: (0, 0))


def _in_proj(x, mod3, g_mix, w_in_t, seq, token):
    t, d = x.shape
    tm = min(ROW_TILE, seq)
    tps = seq // tm

    def body(x_ref, mod_ref, g_ref, w_ref, tok_ref, proj_ref, u1_ref):
        for rows in _sub_rows(tm):
            xv = x_ref[rows, :]
            r = lax.rsqrt(jnp.mean(xv * xv, -1, keepdims=True) + EPS)
            u = (xv * r * g_ref[...]) * (1.0 + mod_ref[0, 1:2, :]) + mod_ref[0, 0:1, :]
            ub = u.astype(BF)
            u1_ref[rows, :] = ub
            proj_ref[rows, 0:OFF_DT] = _nt(ub, w_ref[0:OFF_DT, :])
            proj_ref[rows, OFF_DT:IN_PAD] = jnp.zeros((rows.stop - rows.start, IN_PAD - OFF_DT), F32)
            proj_ref[rows, OFF_DT:IN_WIDTH] = _nt(ub, w_ref[OFF_DT:IN_WIDTH, :])

    return pl.pallas_call(
        body, name="in_proj", grid=(t // tm,),
        in_specs=[pl.BlockSpec((tm, d), lambda i: (i, 0)), pl.BlockSpec((1, N_MOD, d), lambda i: (i // tps, 0, 0)),
                  pl.BlockSpec((1, d), lambda i: (0, 0)), _RESIDENT, _token_spec()],
        out_specs=[pl.BlockSpec((tm, IN_PAD), lambda i: (i, 0)), pl.BlockSpec((tm, d), lambda i: (i, 0))],
        out_shape=[_out((t, IN_PAD), F32), _out((t, d), BF)],
        compiler_params=_cp(("parallel",), 40))(*_pin(x, mod3, g_mix), w_in_t, *_pin(token))


def _pool_tile(seq):
    return min(1024, seq)


def _pool_fwd(proj, w_pool, pool_scale, nb, seq, token):
    ts = _pool_tile(seq)
    nt = seq // ts

    def body(u_ref, halo_ref, wp_ref, ps_ref, tok_ref, yp_ref, p_ref):
        i = pl.program_id(1)
        halo = jnp.where(i == 0, 0.0, halo_ref[...])
        u = u_ref[...]
        ext = jnp.concatenate([halo, u], 0)
        tpos = i * ts + _iota((ts, 1), 0)
        for g, w in enumerate(POOL_WINDOWS):
            gs = slice(g * POOL_GROUP, (g + 1) * POOL_GROUP)
            s = ext[:, gs]
            sh = 1
            while sh < w:
                s = s + pltpu.roll(s, sh, 0)
                sh *= 2
            cnt = jnp.minimum(tpos + 1, w).astype(F32)
            pb = (s[HALO:] / cnt - u[:, gs]).astype(BF)
            p_ref[:, gs] = pb
            yp_ref[:, gs] = (_nn(pb, wp_ref[g].astype(BF)) * ps_ref[:, gs]).astype(BF)

    hb = ts // HALO
    return pl.pallas_call(
        body, name="pool_fwd", grid=(nb, nt),
        in_specs=[pl.BlockSpec((ts, POOL_WIDTH), lambda b, i: (b * nt + i, 0)),
                  pl.BlockSpec((HALO, POOL_WIDTH), lambda b, i: (jnp.maximum((b * nt + i) * hb - 1, 0), 0)),
                  pl.BlockSpec((4, POOL_GROUP, POOL_GROUP), lambda b, i: (0, 0, 0)),
                  pl.BlockSpec((1, POOL_WIDTH), lambda b, i: (0, 0)), _token_spec()],
        out_specs=[pl.BlockSpec((ts, POOL_WIDTH), lambda b, i: (b * nt + i, 0))] * 2,
        out_shape=[_out((nb * seq, POOL_WIDTH), BF)] * 2,
        compiler_params=_cp(("parallel", "parallel"), 32))(*_pin(proj, proj, w_pool, pool_scale, token))


def _conv_pre(uxbc, halo, cw, cb, first):
    halo = jnp.where(first, 0.0, halo)
    ext = jnp.concatenate([halo, uxbc], 0)
    pre = cb + uxbc * cw[3:4]
    for k in (2, 1, 0):
        pre = pre + pltpu.roll(ext, 3 - k, 0)[CONV_HALO:] * cw[k:k + 1]
    return pre


def _chunk_terms(pre, udt, dtb, alog):
    sg = _sigmoid(pre)
    xbc = pre * sg
    dtp = udt[:, :SSD_HEADS] + dtb
    dt = jnp.maximum(dtp, 0.0) + jnp.log(1.0 + jnp.exp(-jnp.abs(dtp)))
    a = -jnp.exp(alog)
    da = dt * a
    tril = (_iota((CHUNK, CHUNK), 0) >= _iota((CHUNK, CHUNK), 1))
    acum = _exact_nn_left(tril.astype(BF), da)
    eye = (_iota((SSD_HEADS, SSD_HEADS), 0) == _iota((SSD_HEADS, SSD_HEADS), 1)).astype(BF)
    acum_t = _exact_nt_left(eye, acum)
    expand = _head_expand_matrix(SSD_HEADS, SSD_INNER)
    acum_e = _exact_nn(acum, expand)
    dt_e = _exact_nn(dt, expand)
    last_e = acum_e[CHUNK - 1:CHUNK]
    return dict(pre=pre, sg=sg, xbc=xbc, dtp=dtp, dt=dt, a=a, acum=acum, acum_t=acum_t, tril=tril,
                dt_e=dt_e, e_a=jnp.exp(acum_e), d_out=jnp.exp(last_e - acum_e), c_dec=jnp.exp(last_e))


def _head_decay(r, h):
    seg = r["acum"][:, h:h + 1] - r["acum_t"][h:h + 1, :]
    return jnp.where(r["tril"], jnp.exp(jnp.minimum(seg, 0.0)), 0.0)


SSD_SUB = 4
SSD_ROWS = SSD_SUB * CHUNK


def _ssd_specs(nb, seq, reverse):
    ns = seq // SSD_ROWS
    per = seq // CONV_HALO

    def cidx(c):
        return (ns - 1 - c) if reverse else c

    def row(b, c):
        return b * ns + cidx(c)

    specs = [
        pl.BlockSpec((SSD_ROWS, CONV_CH), lambda b, c: (row(b, c), 1)),
        pl.BlockSpec((CONV_HALO, CONV_CH),
                     lambda b, c: (jnp.maximum(b * per + cidx(c) * (SSD_ROWS // CONV_HALO) - 1, 0), 1)),
        pl.BlockSpec((SSD_ROWS, GROUP_W), lambda b, c: (row(b, c), 1)),
        pl.BlockSpec((SSD_ROWS, GROUP_W), lambda b, c: (row(b, c), 2)),
        pl.BlockSpec((SSD_ROWS, 128), lambda b, c: (row(b, c), OFF_DT // 128)),
    ]
    return specs, row, cidx, ns


def _const_spec(shape):
    return pl.BlockSpec(shape, lambda b, c: (0,) * len(shape))


def _ssd_fwd(proj, conv_w, conv_b, dt_bias, a_log, dskip_e, g_ssd, nb, seq):
    specs, row, cidx, ns = _ssd_specs(nb, seq, reverse=False)

    def body(uxbc_ref, halo_ref, z0_ref, z1_ref, udt_ref, cw_ref, cb_ref, dtb_ref, alog_ref, dsk_ref, gs_ref,
             yssd_ref, yssm_ref, hprev_ref, pre_ref, h_ref, yd_ref):
        c = pl.program_id(1)

        @pl.when(c == 0)
        def _():
            h_ref[...] = jnp.zeros_like(h_ref)

        for sub in range(SSD_SUB):
            rows = slice(sub * CHUNK, (sub + 1) * CHUNK)
            if sub == 0:
                halo, first = halo_ref[...], c == 0
            else:
                halo, first = uxbc_ref[sub * CHUNK - CONV_HALO:sub * CHUNK, :], False
            pre = _conv_pre(uxbc_ref[rows, :], halo, cw_ref[...], cb_ref[...], first)
            pre_ref[rows, :] = pre
            r = _chunk_terms(pre, udt_ref[rows, :], dtb_ref[...], alog_ref[...])
            xbc = r["xbc"]
            xs = xbc[:, :SSD_INNER]
            xdt = xs * r["dt_e"]
            xdt_b = xdt.astype(BF)
            xdo_b = (xdt * r["d_out"]).astype(BF)
            hprev_ref[0, sub] = h_ref[...]
            for g in range(2):
                gs = slice(g * GROUP_W, (g + 1) * GROUP_W)
                bg = xbc[:, SSD_INNER + g * SSD_STATE:SSD_INNER + (g + 1) * SSD_STATE].astype(BF)
                cg = xbc[:, SSD_INNER + (2 + g) * SSD_STATE:SSD_INNER + (3 + g) * SSD_STATE].astype(BF)
                scores = _nt(cg, bg)
                hg = h_ref[g]
                y_off = _nn(cg, hg.astype(BF)) * r["e_a"][:, gs]
                for hh in range(8):
                    h = g * 8 + hh
                    hs = slice(h * SSD_HEAD_DIM, (h + 1) * SSD_HEAD_DIM)
                    m = (scores * _head_decay(r, h)).astype(BF)
                    yd_ref[sub, :, hs] = _nn(m, xdt_b[:, hs])
                h_ref[g] = hg * r["c_dec"][:, gs] + _tn(bg, xdo_b[:, gs])
                y = yd_ref[sub, :, gs] + y_off + dsk_ref[:, gs] * xs[:, gs]
                yssm_ref[rows, gs] = y
                zg = (z0_ref if g == 0 else z1_ref)[rows, :]
                yg = y * (zg * _sigmoid(zg))
                rg = lax.rsqrt(jnp.mean(yg * yg, -1, keepdims=True) + EPS)
                yssd_ref[rows, gs] = (yg * rg * gs_ref[:, gs]).astype(BF)

    t = nb * seq
    return pl.pallas_call(
        body, name="ssd_fwd", grid=(nb, ns),
        in_specs=specs + [_const_spec((4, CONV_CH)), _const_spec((1, CONV_CH)), _const_spec((1, SSD_HEADS)),
                          _const_spec((1, SSD_HEADS)), _const_spec((1, SSD_INNER)), _const_spec((1, SSD_INNER))],
        out_specs=[pl.BlockSpec((SSD_ROWS, SSD_INNER), lambda b, c: (row(b, c), 0)),
                   pl.BlockSpec((SSD_ROWS, SSD_INNER), lambda b, c: (row(b, c), 0)),
                   pl.BlockSpec((1, SSD_SUB, 2, SSD_STATE, GROUP_W), lambda b, c: (b, c, 0, 0, 0)),
                   pl.BlockSpec((SSD_ROWS, CONV_CH), lambda b, c: (row(b, c), 0))],
        out_shape=[_out((t, SSD_INNER), BF), _out((t, SSD_INNER), F32),
                   _out((nb, seq // CHUNK, 2, SSD_STATE, GROUP_W), F32), _out((t, CONV_CH), F32)],
        scratch_shapes=[pltpu.VMEM((2, SSD_STATE, GROUP_W), F32), pltpu.VMEM((SSD_SUB, CHUNK, SSD_INNER), F32)],
        compiler_params=_cp(("arbitrary", "arbitrary"), 56),
    )(*_pin(proj, proj, proj, proj, proj, conv_w, conv_b, dt_bias, a_log, dskip_e, g_ssd))


def _out_proj(y_pool, y_ssd, w_out, x, mod3, g_mlp, seq):
    t, d = x.shape
    tm = 512
    tps = seq // tm if seq >= tm else 1
    tm = min(tm, seq)

    def body(yp_ref, ys_ref, w_ref, x_ref, mod_ref, g_ref, h1_ref, o_ref, u2_ref):
        for rows in _sub_rows(tm):
            o = _nn(jnp.concatenate([yp_ref[rows, :], ys_ref[rows, :]], 1), w_ref[...])
            o_ref[rows, :] = o.astype(BF)
            h1 = x_ref[rows, :] + mod_ref[0, 2:3, :] * o
            h1_ref[rows, :] = h1
            r = lax.rsqrt(jnp.mean(h1 * h1, -1, keepdims=True) + EPS)
            u2_ref[rows, :] = ((h1 * r * g_ref[...]) * (1.0 + mod_ref[0, 4:5, :]) + mod_ref[0, 3:4, :]).astype(BF)

    row = lambda i: (i, 0)
    return pl.pallas_call(
        body, name="out_proj", grid=(t // tm,),
        in_specs=[pl.BlockSpec((tm, POOL_WIDTH), row), pl.BlockSpec((tm, SSD_INNER), row),
                  _RESIDENT, pl.BlockSpec((tm, d), row),
                  pl.BlockSpec((1, N_MOD, d), lambda i: (i // tps, 0, 0)), pl.BlockSpec((1, d), lambda i: (0, 0))],
        out_specs=[pl.BlockSpec((tm, d), row)] * 3,
        out_shape=[_out((t, d), F32), _out((t, d), BF), _out((t, d), BF)],
        compiler_params=_cp(("parallel",), 48))(*_pin(y_pool, y_ssd), w_out, *_pin(x, mod3, g_mlp))


def _mlp_up(u2, w_up4):
    t, d = u2.shape
    tm = min(1024, t)
    nk, _, cols = w_up4.shape

    def body(u_ref, w_ref, a_ref):
        a_ref[...] = _nn(u_ref[...], w_ref[pl.program_id(1)]).astype(BF)

    return pl.pallas_call(
        body, name="mlp_up", grid=(t // tm, nk),
        in_specs=[pl.BlockSpec((tm, d), lambda i, k: (i, 0)), _RESIDENT],
        out_specs=pl.BlockSpec((tm, cols), lambda i, k: (i, k)),
        out_shape=_out((t, nk * cols), BF),
        compiler_params=_cp(("parallel", "parallel"), 32))(*_pin(u2), w_up4)


def _mlp_down_loss(a_up, w_down, h1, mod3, g_final, target, seq):
    t, d = h1.shape
    nb = t // seq
    tm = min(ROW_TILE, seq)
    tps = seq // tm

    def body(a_ref, w_ref, h1_ref, mod_ref, g_ref, tg_ref, ddn_ref, dh2_ref, sq_ref, gg_ref, dgf_ref):
        i = pl.program_id(0)

        @pl.when(i == 0)
        def _():
            sq_ref[...] = jnp.zeros_like(sq_ref)
            gg_ref[...] = jnp.zeros_like(gg_ref)

        @pl.when(i % tps == 0)
        def _():
            dgf_ref[...] = jnp.zeros_like(dgf_ref)

        gate = mod_ref[0, 5:6, :]
        sq = gg = dgf = 0.0
        for rows in _sub_rows(tm):
            f = jnp.square(jnp.maximum(a_ref[rows, :], 0))
            dn = _nn(f, w_ref[...])
            h2 = h1_ref[rows, :] + gate * dn
            r = lax.rsqrt(jnp.mean(h2 * h2, -1, keepdims=True) + EPS)
            hh = h2 * r
            err = hh * g_ref[...] - tg_ref[rows, :]
            dy = err * (1.0 / d)
            dhat = dy * g_ref[...]
            dh2 = r * (dhat - hh * jnp.mean(dhat * hh, -1, keepdims=True))
            dh2_ref[rows, :] = dh2
            ddn_ref[rows, :] = (dh2 * gate).astype(BF)
            sq = sq + jnp.sum(err * err, 0, keepdims=True)
            gg = gg + jnp.sum(dy * hh, 0, keepdims=True)
            dgf = dgf + jnp.sum(dh2 * dn, 0, keepdims=True)
        sq_ref[...] += sq
        gg_ref[...] += gg
        dgf_ref[0] += dgf

    row = lambda i: (i, 0)
    vec = pl.BlockSpec((1, d), lambda i: (0, 0))
    return pl.pallas_call(
        body, name="mlp_down_loss", grid=(t // tm,),
        in_specs=[pl.BlockSpec((tm, D_FF), row), _RESIDENT, pl.BlockSpec((tm, d), row),
                  pl.BlockSpec((1, N_MOD, d), lambda i: (i // tps, 0, 0)), vec, pl.BlockSpec((tm, d), row)],
        out_specs=[pl.BlockSpec((tm, d), row), pl.BlockSpec((tm, d), row), vec, vec,
                   pl.BlockSpec((1, 1, d), lambda i: (i // tps, 0, 0))],
        out_shape=[_out((t, d), BF), _out((t, d), F32), _out((1, d), F32),
                   _out((1, d), F32), _out((nb, 1, d), F32)],
        compiler_params=_cp(("arbitrary",), 44))(*_pin(a_up), w_down, *_pin(h1, mod3, g_final, target))


def _tn_matmul(a, b, tk, tn, name, square_relu=False, out3=False):
    t, kdim = a.shape
    ndim = b.shape[1]

    def body(a_ref, b_ref, o_ref):
        av = a_ref[...]
        if square_relu:
            av = jnp.square(jnp.maximum(av, 0))
        res = _tn(av, b_ref[...]).astype(BF)
        if out3:
            o_ref[0] = res
        else:
            o_ref[...] = res

    if out3:
        out_spec = pl.BlockSpec((1, tk, tn), lambda j, i: (j, i, 0))
        out_shape = _out((ndim // tn, kdim, tn), BF)
    else:
        out_spec = pl.BlockSpec((tk, tn), lambda j, i: (i, j))
        out_shape = _out((kdim, ndim), BF)
    return pl.pallas_call(
        body, name=name, grid=(ndim // tn, kdim // tk),
        in_specs=[pl.BlockSpec((t, tk), lambda j, i: (0, i)), pl.BlockSpec((t, tn), lambda j, i: (0, j))],
        out_specs=out_spec, out_shape=out_shape,
        compiler_params=_cp(("parallel", "parallel"), 56))(*_pin(a, b))


def _mlp_down_bwd(d_dn, w_down4, a_up, token):
    t, d = d_dn.shape
    tm = min(1024, t)
    nk, rows, _ = w_down4.shape

    def body(g_ref, w_ref, a_ref, tok_ref, o_ref):
        df = _nt(g_ref[...], w_ref[pl.program_id(1)])
        o_ref[...] = (df * (2.0 * jnp.maximum(a_ref[...], 0).astype(F32))).astype(BF)

    return pl.pallas_call(
        body, name="mlp_down_bwd", grid=(t // tm, nk),
        in_specs=[pl.BlockSpec((tm, d), lambda i, k: (i, 0)), _RESIDENT,
                  pl.BlockSpec((tm, rows), lambda i, k: (i, k)), _token_spec()],
        out_specs=pl.BlockSpec((tm, rows), lambda i, k: (i, k)),
        out_shape=_out((t, nk * rows), BF),
        compiler_params=_cp(("parallel", "parallel"), 32))(*_pin(d_dn), w_down4, *_pin(a_up, token))


def _mlp_up_bwd(d_a, w_up4, h1, dh2, o, mod3, g_mlp, seq, token):
    t, d = h1.shape
    nb = t // seq
    tm = min(ROW_TILE, seq)
    tps = seq // tm
    nk = w_up4.shape[0]
    cols = w_up4.shape[2]

    def body(da_ref, w_ref, h1_ref, dh2_ref, o_ref, mod_ref, g_ref, tok_ref, dh1_ref, do_ref, acc_ref, gg_ref):
        i = pl.program_id(0)

        @pl.when(i == 0)
        def _():
            gg_ref[...] = jnp.zeros_like(gg_ref)

        @pl.when(i % tps == 0)
        def _():
            acc_ref[...] = jnp.zeros_like(acc_ref)

        gg = a_shift = a_scale = a_gate = 0.0
        for rows in _sub_rows(tm):
            du = _nt(da_ref[rows, 0:cols], w_ref[0])
            for k in range(1, nk):
                du = du + _nt(da_ref[rows, k * cols:(k + 1) * cols], w_ref[k])
            h1 = h1_ref[rows, :]
            r = lax.rsqrt(jnp.mean(h1 * h1, -1, keepdims=True) + EPS)
            hh = h1 * r
            n2 = hh * g_ref[...]
            dn2 = du * (1.0 + mod_ref[0, 4:5, :])
            dhat = dn2 * g_ref[...]
            dh1 = dh2_ref[rows, :] + r * (dhat - hh * jnp.mean(dhat * hh, -1, keepdims=True))
            dh1_ref[rows, :] = dh1
            do_ref[rows, :] = (dh1 * mod_ref[0, 2:3, :]).astype(BF)
            gg = gg + jnp.sum(dn2 * hh, 0, keepdims=True)
            a_shift = a_shift + jnp.sum(du, 0, keepdims=True)
            a_scale = a_scale + jnp.sum(du * n2, 0, keepdims=True)
            a_gate = a_gate + jnp.sum(dh1 * o_ref[rows, :].astype(F32), 0, keepdims=True)
        gg_ref[...] += gg
        acc_ref[0, 0:1, :] += a_shift
        acc_ref[0, 1:2, :] += a_scale
        acc_ref[0, 2:3, :] += a_gate

    row = lambda i: (i, 0)
    vec = pl.BlockSpec((1, d), lambda i: (0, 0))
    return pl.pallas_call(
        body, name="mlp_up_bwd", grid=(t // tm,),
        in_specs=[pl.BlockSpec((tm, D_FF), row), _RESIDENT, pl.BlockSpec((tm, d), row),
                  pl.BlockSpec((tm, d), row), pl.BlockSpec((tm, d), row),
                  pl.BlockSpec((1, N_MOD, d), lambda i: (i // tps, 0, 0)), vec, _token_spec()],
        out_specs=[pl.BlockSpec((tm, d), row), pl.BlockSpec((tm, d), row),
                   pl.BlockSpec((1, 8, d), lambda i: (i // tps, 0, 0)), vec],
        out_shape=[_out((t, d), F32), _out((t, d), BF),
                   _out((nb, 8, d), F32), _out((1, d), F32)],
        compiler_params=_cp(("arbitrary",), 44))(*_pin(d_a), w_up4, *_pin(h1, dh2, o, mod3, g_mlp, token))


def _out_proj_bwd(d_o, w_out, token):
    t, d = d_o.shape
    tm = min(1024, t)

    def body(g_ref, w_ref, tok_ref, dp_ref, ds_ref):
        gv = g_ref[...]
        dp_ref[...] = _nt(gv, w_ref[0:POOL_WIDTH, :])
        ds_ref[...] = _nt(gv, w_ref[POOL_WIDTH:, :])

    row = lambda i: (i, 0)
    return pl.pallas_call(
        body, name="out_proj_bwd", grid=(t // tm,),
        in_specs=[pl.BlockSpec((tm, d), row), _RESIDENT, _token_spec()],
        out_specs=[pl.BlockSpec((tm, POOL_WIDTH), row), pl.BlockSpec((tm, SSD_INNER), row)],
        out_shape=[_out((t, POOL_WIDTH), F32), _out((t, SSD_INNER), F32)],
        compiler_params=_cp(("parallel",), 32))(*_pin(d_o), w_out, *_pin(token))


def _pool_bwd(d_ypool, p, w_pool, pool_scale, nb, seq):
    ts = _pool_tile(seq)
    nt = seq // ts
    hb = ts // HALO
    last_block = nb * seq // HALO - 1

    def body(dy_ref, halo_ref, p_ref, wp_ref, ps_ref, du_ref, gw_ref, gs_ref):
        b = pl.program_id(0)
        i = pl.program_id(1)

        @pl.when((b == 0) & (i == 0))
        def _():
            gw_ref[...] = jnp.zeros_like(gw_ref)
            gs_ref[...] = jnp.zeros_like(gs_ref)

        halo = jnp.where(i == nt - 1, 0.0, halo_ref[...])
        dy = dy_ref[...]
        ext = jnp.concatenate([dy, halo], 0)
        tpos = i * ts + _iota((ts + HALO, 1), 0)
        n_ext = ts + HALO
        for g, w in enumerate(POOL_WINDOWS):
            gs = slice(g * POOL_GROUP, (g + 1) * POOL_GROUP)
            wg = wp_ref[g].astype(BF)
            pg = p_ref[:, gs]
            pw = _nn(pg, wg)
            gs_ref[:, gs] += jnp.sum(dy[:, gs] * pw, 0, keepdims=True)
            dpw = (ext[:, gs] * ps_ref[:, gs]).astype(BF)
            gw_ref[g] += _tn(pg, dpw[:ts])
            dp = _nt(dpw, wg)
            cnt = jnp.minimum(tpos + 1, w).astype(F32)
            s = dp / cnt
            sh = 1
            while sh < w:
                s = s + pltpu.roll(s, n_ext - sh, 0)
                sh *= 2
            du_ref[:, gs] = (s[:ts] - dp[:ts]).astype(BF)

    return pl.pallas_call(
        body, name="pool_bwd", grid=(nb, nt),
        in_specs=[pl.BlockSpec((ts, POOL_WIDTH), lambda b, i: (b * nt + i, 0)),
                  pl.BlockSpec((HALO, POOL_WIDTH), lambda b, i: (jnp.minimum((b * nt + i + 1) * hb, last_block), 0)),
                  pl.BlockSpec((ts, POOL_WIDTH), lambda b, i: (b * nt + i, 0)),
                  pl.BlockSpec((4, POOL_GROUP, POOL_GROUP), lambda b, i: (0, 0, 0)),
                  pl.BlockSpec((1, POOL_WIDTH), lambda b, i: (0, 0))],
        out_specs=[pl.BlockSpec((ts, POOL_WIDTH), lambda b, i: (b * nt + i, 0)),
                   pl.BlockSpec((4, POOL_GROUP, POOL_GROUP), lambda b, i: (0, 0, 0)),
                   pl.BlockSpec((1, POOL_WIDTH), lambda b, i: (0, 0))],
        out_shape=[_out((nb * seq, POOL_WIDTH), BF), _out((4, POOL_GROUP, POOL_GROUP), F32),
                   _out((1, POOL_WIDTH), F32)],
        compiler_params=_cp(("arbitrary", "arbitrary"), 32))(*_pin(d_ypool, d_ypool, p, w_pool, pool_scale))


def _ssd_bwd(proj, pre, d_yssd, yssm, h_prev, dt_bias, a_log, dskip_e, g_ssd, nb, seq):
    specs, row, cidx, ns = _ssd_specs(nb, seq, reverse=True)
    specs = specs[2:]

    def body(z0_ref, z1_ref, udt_ref, pre_ref, dys_ref, yssm_ref, hprev_ref,
             dtb_ref, alog_ref, dsk_ref, gs_ref,
             dz_ref, dpre_ref, dudt_ref, ggs_ref, gdsk_ref, ga_ref, gdtb_ref,
             g_ref, dxdt_ref, dyv_ref):
        b = pl.program_id(0)
        c = pl.program_id(1)

        @pl.when(c == 0)
        def _():
            g_ref[...] = jnp.zeros_like(g_ref)

        @pl.when((b == 0) & (c == 0))
        def _():
            ggs_ref[...] = jnp.zeros_like(ggs_ref)
            gdsk_ref[...] = jnp.zeros_like(gdsk_ref)
            ga_ref[...] = jnp.zeros_like(ga_ref)
            gdtb_ref[...] = jnp.zeros_like(gdtb_ref)

        for sub in reversed(range(SSD_SUB)):
            chunk(sub, z0_ref, z1_ref, udt_ref, pre_ref, dys_ref, yssm_ref, hprev_ref, dtb_ref, alog_ref, dsk_ref, gs_ref,
                  dz_ref, dpre_ref, dudt_ref, ggs_ref, gdsk_ref, ga_ref, gdtb_ref, g_ref, dxdt_ref.at[sub], dyv_ref.at[sub])

    def chunk(sub, z0_ref, z1_ref, udt_ref, pre_ref, dys_ref, yssm_ref, hprev_ref,
              dtb_ref, alog_ref, dsk_ref, gs_ref,
              dz_ref, dpre_ref, dudt_ref, ggs_ref, gdsk_ref, ga_ref, gdtb_ref,
              g_ref, dxdt_ref, dyv_ref):
        rows = slice(sub * CHUNK, (sub + 1) * CHUNK)
        r = _chunk_terms(pre_ref[rows, :], udt_ref[rows, :], dtb_ref[...], alog_ref[...])
        xbc = r["xbc"]
        xs = xbc[:, :SSD_INNER]
        dt_e = r["dt_e"]
        xdt = xs * dt_e
        xdt_b = xdt.astype(BF)
        reduce_m = _head_reduce_matrix(GROUP_W, 8)

        def head_sums(v):
            return _nn(v.astype(BF), reduce_m)

        onehot16 = lambda h: (_iota((1, SSD_HEADS), 1) == h).astype(F32)
        onecol16 = lambda h: (_iota((SSD_HEADS, 1), 0) == h).astype(F32)

        d_acum = jnp.zeros((CHUNK, SSD_HEADS), F32)
        d_acum_t = jnp.zeros((SSD_HEADS, CHUNK), F32)
        d_alast = jnp.zeros((1, SSD_HEADS), F32)
        place8 = lambda g: (_iota((8, SSD_HEADS), 1) == _iota((8, SSD_HEADS), 0) + 8 * g).astype(BF)
        d_b, d_c = [], []
        for g in range(2):
            gs = slice(g * GROUP_W, (g + 1) * GROUP_W)
            zg = (z0_ref if g == 0 else z1_ref)[rows, :]
            sz = _sigmoid(zg)
            silu_z = zg * sz
            ys = yssm_ref[rows, gs]
            yg = ys * silu_z
            rg = lax.rsqrt(jnp.mean(yg * yg, -1, keepdims=True) + EPS)
            yh = yg * rg
            dys = dys_ref[rows, gs]
            ggs_ref[:, gs] += jnp.sum(dys * yh, 0, keepdims=True)
            dyh = dys * gs_ref[:, gs]
            dyg = rg * (dyh - yh * jnp.mean(dyh * yh, -1, keepdims=True))
            dy = dyg * silu_z
            dz_ref[rows, gs] = (dyg * ys * (sz * (1.0 + zg * (1.0 - sz)))).astype(BF)
            gdsk_ref[:, gs] += jnp.sum(dy * xs[:, gs], 0, keepdims=True)
            dyv_ref[:, gs] = dy
            dy_b = dy.astype(BF)

            bg = xbc[:, SSD_INNER + g * SSD_STATE:SSD_INNER + (g + 1) * SSD_STATE].astype(BF)
            cg = xbc[:, SSD_INNER + (2 + g) * SSD_STATE:SSD_INNER + (3 + g) * SSD_STATE].astype(BF)
            scores = _nt(cg, bg)
            hg = hprev_ref[0, sub, g]
            hg_b = hg.astype(BF)
            gg = g_ref[g]
            gg_b = gg.astype(BF)
            e_a = r["e_a"][:, gs]
            d_out = r["d_out"][:, gs]
            c_dec = r["c_dec"][:, gs]
            zc = _nn(cg, hg_b)
            wv = e_a * dy
            wv_b = wv.astype(BF)
            da_g = head_sums(wv * zc)
            dcg = _nt(wv_b, hg_b)
            d_hprev = _tn(cg, wv_b)
            vg = _nn(bg, gg_b)
            dxdt_g = d_out * vg
            dd_out = head_sums(xdt[:, gs] * vg)
            dbg = _nt((xdt[:, gs] * d_out).astype(BF), gg_b)
            dcd = _exact_nn(jnp.sum(gg * hg, 0, keepdims=True), reduce_m)
            d_out8 = jnp.exp(r["acum"][CHUNK - 1:CHUNK, 8 * g:8 * g + 8] - r["acum"][:, 8 * g:8 * g + 8])
            c_dec8 = jnp.exp(r["acum"][CHUNK - 1:CHUNK, 8 * g:8 * g + 8])
            t8 = dd_out * d_out8
            d_alast = d_alast + _exact_nn(jnp.sum(t8, 0, keepdims=True) + dcd * c_dec8, place8(g))
            d_acum = d_acum + _exact_nn(da_g - t8, place8(g))
            dsc = jnp.zeros((CHUNK, CHUNK), F32)
            for hh in range(8):
                h = g * 8 + hh
                hs = slice(h * SSD_HEAD_DIM, (h + 1) * SSD_HEAD_DIM)
                lam = _head_decay(r, h)
                m = scores * lam
                dyh_b = dy_b[:, hh * SSD_HEAD_DIM:(hh + 1) * SSD_HEAD_DIM]
                dm = _nt(dyh_b, xdt_b[:, hs])
                tm_ = dm * m
                d_acum = d_acum + jnp.sum(tm_, 1, keepdims=True) * onehot16(h)
                d_acum_t = d_acum_t + onecol16(h) * jnp.sum(tm_, 0, keepdims=True)
                dsc = dsc + dm * lam
                dxdt_ref[:, hs] = _tn(m.astype(BF), dyh_b) + dxdt_g[:, hh * SSD_HEAD_DIM:(hh + 1) * SSD_HEAD_DIM]
            dsc_b = dsc.astype(BF)
            d_c.append(dcg + _nn(dsc_b, bg))
            d_b.append(dbg + _tn(dsc_b, cg))
            g_ref[g] = d_hprev + c_dec * gg

        eye = (_iota((CHUNK, CHUNK), 0) == _iota((CHUNK, CHUNK), 1)).astype(BF)
        d_acum = d_acum - _exact_nt_left(eye, d_acum_t)
        is_last = (_iota((CHUNK, 1), 0) == CHUNK - 1).astype(F32)
        d_acum = d_acum + is_last * d_alast
        triu = (_iota((CHUNK, CHUNK), 0) <= _iota((CHUNK, CHUNK), 1)).astype(BF)
        d_da = _exact_nn_left(triu, d_acum)
        dt = r["dt"]
        ga_ref[...] += jnp.sum(d_da * dt, 0, keepdims=True)
        dxdt = dxdt_ref[...]
        reduce16 = _head_reduce_matrix(SSD_INNER, SSD_HEADS)
        d_dt = d_da * r["a"] + _nn((dxdt * xs).astype(BF), reduce16)
        d_udt = d_dt * _sigmoid(r["dtp"])
        gdtb_ref[...] += jnp.sum(d_udt, 0, keepdims=True)
        dudt_ref[rows, :] = jnp.zeros((CHUNK, dudt_ref.shape[1]), BF)
        dudt_ref[rows, 0:SSD_HEADS] = d_udt.astype(BF)
        pre, sg = r["pre"], r["sg"]
        dsilu = sg * (1.0 + pre * (1.0 - sg))
        dpre_ref[rows, 0:SSD_INNER] = (dsk_ref[...] * dyv_ref[...] + dxdt * dt_e) * dsilu[:, 0:SSD_INNER]
        for g in range(2):
            bs = slice(SSD_INNER + g * SSD_STATE, SSD_INNER + (g + 1) * SSD_STATE)
            cs = slice(SSD_INNER + (2 + g) * SSD_STATE, SSD_INNER + (3 + g) * SSD_STATE)
            dpre_ref[rows, bs] = d_b[g] * dsilu[:, bs]
            dpre_ref[rows, cs] = d_c[g] * dsilu[:, cs]

    t = nb * seq
    vec = _const_spec((1, SSD_INNER))
    small = _const_spec((1, SSD_HEADS))
    return pl.pallas_call(
        body, name="ssd_bwd", grid=(nb, ns),
        in_specs=specs + [pl.BlockSpec((SSD_ROWS, CONV_CH), lambda b, c: (row(b, c), 0)),
                          pl.BlockSpec((SSD_ROWS, SSD_INNER), lambda b, c: (row(b, c), 0)),
                          pl.BlockSpec((SSD_ROWS, SSD_INNER), lambda b, c: (row(b, c), 0)),
                          pl.BlockSpec((1, SSD_SUB, 2, SSD_STATE, GROUP_W), lambda b, c: (b, cidx(c), 0, 0, 0)),
                          small, small, vec, vec],
        out_specs=[pl.BlockSpec((SSD_ROWS, SSD_INNER), lambda b, c: (row(b, c), 0)),
                   pl.BlockSpec((SSD_ROWS, CONV_CH), lambda b, c: (row(b, c), 0)),
                   pl.BlockSpec((SSD_ROWS, 128), lambda b, c: (row(b, c), 0)),
                   vec, vec, small, small],
        out_shape=[_out((t, SSD_INNER), BF), _out((t, CONV_CH), F32),
                   _out((t, 128), BF), _out((1, SSD_INNER), F32),
                   _out((1, SSD_INNER), F32), _out((1, SSD_HEADS), F32),
                   _out((1, SSD_HEADS), F32)],
        scratch_shapes=[pltpu.VMEM((2, SSD_STATE, GROUP_W), F32), pltpu.VMEM((SSD_SUB, CHUNK, SSD_INNER), F32),
                        pltpu.VMEM((SSD_SUB, CHUNK, SSD_INNER), F32)],
        compiler_params=_cp(("arbitrary", "arbitrary"), 56),
    )(*_pin(proj, proj, proj, pre, d_yssd, yssm, h_prev, dt_bias, a_log, dskip_e, g_ssd))


def _grad_w_out(y_pool, y_ssd, d_o):
    t, d = d_o.shape
    tk = POOL_WIDTH
    n_s = SSD_INNER // tk

    def body(p_ref, s_ref, g_ref, o_ref):
        i = pl.program_id(0)

        @pl.when(i == 0)
        def _():
            o_ref[...] = _tn(p_ref[...], g_ref[...]).astype(BF)

        @pl.when(i > 0)
        def _():
            o_ref[...] = _tn(s_ref[...], g_ref[...]).astype(BF)

    return pl.pallas_call(
        body, name="grad_w_out", grid=(1 + n_s,),
        in_specs=[pl.BlockSpec((t, tk), lambda i: (0, 0)), pl.BlockSpec((t, tk), lambda i: (0, jnp.maximum(i - 1, 0))),
                  pl.BlockSpec((t, d), lambda i: (0, 0))],
        out_specs=pl.BlockSpec((tk, d), lambda i: (i, 0)),
        out_shape=_out((POOL_WIDTH + SSD_INNER, d), BF),
        compiler_params=_cp(("parallel",), 56))(*_pin(y_pool, y_ssd, d_o))


def _grad_w_in_t(d_upool, d_z, d_uxbc, d_udt, u1):
    t, d = u1.shape
    tk = 512
    n_z, n_x = SSD_INNER // tk, CONV_CH // tk

    def body(p_ref, z_ref, x_ref, dt_ref, u_ref, o_ref):
        i = pl.program_id(0)

        @pl.when(i == 0)
        def _():
            o_ref[...] = _tn(p_ref[...], u_ref[...]).astype(BF)

        @pl.when((i >= 1) & (i < 1 + n_z))
        def _():
            o_ref[...] = _tn(z_ref[...], u_ref[...]).astype(BF)

        @pl.when((i >= 1 + n_z) & (i < 1 + n_z + n_x))
        def _():
            o_ref[...] = _tn(x_ref[...], u_ref[...]).astype(BF)

        @pl.when(i == 1 + n_z + n_x)
        def _():
            o_ref[0:128, :] = _tn(dt_ref[...], u_ref[...]).astype(BF)

    return pl.pallas_call(
        body, name="grad_w_in", grid=(2 + n_z + n_x,),
        in_specs=[pl.BlockSpec((t, tk), lambda i: (0, 0)),
                  pl.BlockSpec((t, tk), lambda i: (0, jnp.clip(i - 1, 0, n_z - 1))),
                  pl.BlockSpec((t, tk), lambda i: (0, jnp.clip(i - 1 - n_z, 0, n_x - 1))),
                  pl.BlockSpec((t, 128), lambda i: (0, 0)), pl.BlockSpec((t, d), lambda i: (0, 0))],
        out_specs=pl.BlockSpec((tk, d), lambda i: (i, 0)),
        out_shape=_out((IN_PAD, d), BF),
        compiler_params=_cp(("parallel",), 56))(*_pin(d_upool, d_z, d_uxbc, d_udt, u1))


def _conv_bwd(d_pre, proj, conv_w, nb, seq):
    ts = min(256, seq)
    nt = seq // ts
    hb = ts // CONV_HALO
    last_block = nb * seq // CONV_HALO - 1
    n_ext = CHUNK + CONV_HALO

    def body(dp_ref, dnext_ref, u_ref, cw_ref, du_ref, gw_ref, gb_ref):
        b = pl.program_id(0)
        i = pl.program_id(1)

        @pl.when((b == 0) & (i == 0))
        def _():
            gw_ref[...] = jnp.zeros_like(gw_ref)
            gb_ref[...] = jnp.zeros_like(gb_ref)

        for c0 in range(0, CONV_CH, 128):
            cs = slice(c0, c0 + 128)
            cw = cw_ref[:, cs]
            gw = [0.0] * 4
            gb = 0.0
            for r0 in range(0, ts, CHUNK):
                dp = dp_ref[r0:r0 + CHUNK, cs]
                u = u_ref[r0:r0 + CHUNK, cs]
                if r0 + CHUNK < ts:
                    below = dp_ref[r0 + CHUNK:r0 + CHUNK + CONV_HALO, cs]
                else:
                    below = jnp.where(i == nt - 1, 0.0, dnext_ref[:, cs])
                ext_d = jnp.concatenate([dp, below], 0)
                du = dp * cw[3:4]
                gw[3] = gw[3] + jnp.sum(dp * u, 0, keepdims=True)
                for k in (2, 1, 0):
                    shifted = pltpu.roll(ext_d, n_ext - (3 - k), 0)[:CHUNK]
                    du = du + shifted * cw[k:k + 1]
                    gw[k] = gw[k] + jnp.sum(shifted * u, 0, keepdims=True)
                gb = gb + jnp.sum(dp, 0, keepdims=True)
                du_ref[r0:r0 + CHUNK, cs] = du.astype(BF)
            for k in range(4):
                gw_ref[k:k + 1, cs] += gw[k]
            gb_ref[:, cs] += gb

    return pl.pallas_call(
        body, name="conv_bwd", grid=(nb, nt),
        in_specs=[pl.BlockSpec((ts, CONV_CH), lambda b, i: (b * nt + i, 0)),
                  pl.BlockSpec((CONV_HALO, CONV_CH), lambda b, i: (jnp.minimum((b * nt + i + 1) * hb, last_block), 0)),
                  pl.BlockSpec((ts, CONV_CH), lambda b, i: (b * nt + i, 1)),
                  pl.BlockSpec((4, CONV_CH), lambda b, i: (0, 0))],
        out_specs=[pl.BlockSpec((ts, CONV_CH), lambda b, i: (b * nt + i, 0)),
                   pl.BlockSpec((8, CONV_CH), lambda b, i: (0, 0)), pl.BlockSpec((1, CONV_CH), lambda b, i: (0, 0))],
        out_shape=[_out((nb * seq, CONV_CH), BF), _out((8, CONV_CH), F32),
                   _out((1, CONV_CH), F32)],
        compiler_params=_cp(("arbitrary", "arbitrary"), 48))(*_pin(d_pre, d_pre, proj, conv_w))


def _in_proj_bwd(d_parts, w_in_t, x, dh1, mod3, g_mix, seq, token):
    t, d = x.shape
    nb = t // seq
    tm = min(ROW_TILE, seq)
    tps = seq // tm

    widths = [p.shape[1] for p in d_parts]

    def body(d0_ref, d1_ref, d2_ref, d3_ref, w_ref, x_ref, dh1_ref, mod_ref, g_ref, tok_ref, gx_ref, acc_ref, gg_ref):
        i = pl.program_id(0)

        @pl.when(i == 0)
        def _():
            gg_ref[...] = jnp.zeros_like(gg_ref)

        @pl.when(i % tps == 0)
        def _():
            acc_ref[...] = jnp.zeros_like(acc_ref)

        gg = a_shift = a_scale = 0.0
        for rows in _sub_rows(tm):
            d_cat = jnp.concatenate([p_ref[rows, :] for p_ref in (d0_ref, d1_ref, d2_ref)], 1)
            du = _nn(d_cat, w_ref[0:OFF_DT, :]) + _nn(d3_ref[rows, 0:IN_WIDTH - OFF_DT], w_ref[OFF_DT:IN_WIDTH, :])
            xv = x_ref[rows, :]
            r = lax.rsqrt(jnp.mean(xv * xv, -1, keepdims=True) + EPS)
            hh = xv * r
            n1 = hh * g_ref[...]
            dn1 = du * (1.0 + mod_ref[0, 1:2, :])
            dhat = dn1 * g_ref[...]
            gx_ref[rows, :] = dh1_ref[rows, :] + r * (dhat - hh * jnp.mean(dhat * hh, -1, keepdims=True))
            gg = gg + jnp.sum(dn1 * hh, 0, keepdims=True)
            a_shift = a_shift + jnp.sum(du, 0, keepdims=True)
            a_scale = a_scale + jnp.sum(du * n1, 0, keepdims=True)
        gg_ref[...] += gg
        acc_ref[0, 0:1, :] += a_shift
        acc_ref[0, 1:2, :] += a_scale

    row = lambda i: (i, 0)
    vec = pl.BlockSpec((1, d), lambda i: (0, 0))
    return pl.pallas_call(
        body, name="in_proj_bwd", grid=(t // tm,),
        in_specs=[pl.BlockSpec((tm, wd), row) for wd in widths] +
                 [_RESIDENT, pl.BlockSpec((tm, d), row),
                  pl.BlockSpec((tm, d), row), pl.BlockSpec((1, N_MOD, d), lambda i: (i // tps, 0, 0)), vec, _token_spec()],
        out_specs=[pl.BlockSpec((tm, d), row), pl.BlockSpec((1, 8, d), lambda i: (i // tps, 0, 0)), vec],
        out_shape=[_out((t, d), F32), _out((nb, 8, d), F32),
                   _out((1, d), F32)],
        compiler_params=_cp(("arbitrary",), 40))(*_pin(*d_parts), w_in_t, *_pin(x, dh1, mod3, g_mix, token))


_VEC_LAYOUT = (("g_mix", 1024), ("conv_b", 1536), ("g_ssd", 1024), ("pool_scale", 512), ("g_mlp", 1024),
               ("g_final", 1024), ("dt_bias", 128), ("a_log", 128), ("d_skip_lanes", 1024), ("sq_err", 1024))
_VEC_OFFSET = {}
_off = 0
for _name, _n in _VEC_LAYOUT:
    _VEC_OFFSET[_name] = _off
    _off += _n
_SMALL_PARAMS = ("b_ada", "g_mix", "conv_w", "conv_b", "dt_bias", "a_log", "d_skip", "g_ssd", "w_pool", "pool_scale",
                 "g_mlp", "g_final")


def _pack_vec(parts):
    cols = []
    for name, n in _VEC_LAYOUT:
        v = parts[name]
        if v.shape[1] < n:
            v = jnp.pad(v, ((0, 0), (0, n - v.shape[1])))
        cols.append(v)
    return jnp.concatenate(cols, 1)


def _small_adam(vec_all, wpool_all, convw_all, dmod_all, params):
    names = _SMALL_PARAMS
    nin = 4 + 3 * len(names)

    def body(*refs):
        vec_ref, wp_ref, cw_ref, dm_ref = refs[:4]
        prm = {n: refs[4 + 3 * i:7 + 3 * i] for i, n in enumerate(names)}
        loss_ref = refs[nin]
        outs = {n: refs[nin + 1 + 4 * i:nin + 5 + 4 * i] for i, n in enumerate(names)}
        vsum = vec_ref[0]
        for s in range(1, N_DEV):
            vsum = vsum + vec_ref[s]

        def lanes(name, n):
            off = _VEC_OFFSET[name]
            return vsum[:, off:off + n]

        grads = {n: lanes(n, prm[n][0].shape[1]) for n in ("g_mix", "conv_b", "g_ssd", "pool_scale", "g_mlp", "g_final", "dt_bias")}
        grads["a_log"] = lanes("a_log", SSD_HEADS) * (-jnp.exp(prm["a_log"][0][...]))
        per_lane = jnp.broadcast_to(lanes("d_skip_lanes", SSD_INNER), (8, SSD_INNER))
        grads["d_skip"] = _exact_nn(per_lane, _head_reduce_matrix(SSD_INNER, SSD_HEADS))[0:1]
        gwp = wp_ref[0].astype(F32)
        gcw = cw_ref[0]
        gb = jnp.sum(dm_ref[0], 0, keepdims=True)
        for s in range(1, N_DEV):
            gwp = gwp + wp_ref[s].astype(F32)
            gcw = gcw + cw_ref[s]
            gb = gb + jnp.sum(dm_ref[s], 0, keepdims=True)
        grads["w_pool"] = gwp
        grads["conv_w"] = gcw[0:4]
        grads["b_ada"] = gb
        total = jnp.sum(lanes("sq_err", D_MODEL), 1, keepdims=True) * (0.5 / D_MODEL)
        loss_ref[...] = jnp.broadcast_to(total, loss_ref.shape)
        for n in names:
            w_ref, m_ref, v_ref = prm[n]
            g = grads[n]
            d, m2, v2 = _adam_math(w_ref[...], g, m_ref[...], v_ref[...])
            g_ref, d_ref, m2_ref, v2_ref = outs[n]
            g_ref[...] = g
            d_ref[...] = d
            m2_ref[...] = m2
            v2_ref[...] = v2

    flat = [vec_all, wpool_all, convw_all, dmod_all]
    out_shape = [jax.ShapeDtypeStruct((1, 128), F32)]
    for n in names:
        flat += list(params[n])
        out_shape += [jax.ShapeDtypeStruct(params[n][0].shape, F32)] * 4
    vm = pl.BlockSpec(memory_space=pltpu.VMEM)
    res = pl.pallas_call(body, name="small_adam", out_shape=out_shape, in_specs=[vm] * len(flat),
                         out_specs=[vm] * len(out_shape), compiler_params=_cp(vmem_mb=48))(*flat)
    return res[0], {n: res[1 + 4 * i:5 + 4 * i] for i, n in enumerate(names)}


_WEIGHTS = ("w_ada", "b_ada", "g_mix", "w_in", "conv_w", "conv_b", "dt_bias", "a_log", "d_skip", "g_ssd", "w_pool",
            "pool_scale", "w_out", "g_mlp", "w_up", "w_down", "g_final")


def _local_step(x2, tg2, mod3, seq, w_in_t, first_token, weights_arrived, weights_later, start_reduce, before_last,
                conv_w_full, sp):
    t, d = x2.shape
    nb = t // seq
    dskip_e = jnp.repeat(sp["d_skip"], SSD_HEAD_DIM, axis=1)
    proj, u1 = _in_proj(x2, mod3, sp["g_mix"], w_in_t, seq, first_token)
    y_ssd, yssm, h_prev, pre = _ssd_fwd(proj, conv_w_full, sp["conv_b"], sp["dt_bias"], sp["a_log"], dskip_e, sp["g_ssd"], nb, seq)
    y_pool, p = _pool_fwd(proj, sp["w_pool"], sp["pool_scale"], nb, seq, weights_arrived(y_ssd))
    w_out_f, w_up4, w_down4 = weights_later(y_pool)
    w_down_f = w_down4.reshape(D_FF, d)
    h1, o, u2 = _out_proj(y_pool, y_ssd, w_out_f, x2, mod3, sp["g_mlp"], seq)
    a_up = _mlp_up(u2, w_up4)
    d_dn, dh2, sq, gg_final, d_gf = _mlp_down_loss(a_up, w_down_f, h1, mod3, sp["g_final"], tg2, seq)

    gw_down = _tn_matmul(a_up, d_dn, 512, d, "grad_w_down", square_relu=True)
    tok = start_reduce("w_down", gw_down.reshape(N_CHIPS, D_FF // N_CHIPS, d))
    d_a = _mlp_down_bwd(d_dn, w_down4, a_up, tok)
    gw_up4 = _tn_matmul(u2, d_a, 512, d, "grad_w_up", out3=True)
    tok = start_reduce("w_up", gw_up4)
    dh1, d_o, accf, gg_mlp = _mlp_up_bwd(d_a, w_up4, h1, dh2, o, mod3, sp["g_mlp"], seq, tok)
    gw_out = _grad_w_out(y_pool, y_ssd, d_o)
    tok = start_reduce("w_out", gw_out.reshape(N_CHIPS, gw_out.shape[0] // N_CHIPS, d))
    d_ypool, d_yssd = _out_proj_bwd(d_o, w_out_f, tok)
    d_upool, gw_pool, g_ps = _pool_bwd(d_ypool, p, sp["w_pool"], sp["pool_scale"], nb, seq)
    d_z, d_pre, d_udt, gg_ssd, gdsk, ga, gdtb = _ssd_bwd(proj, pre, d_yssd, yssm, h_prev, sp["dt_bias"], sp["a_log"],
                                                        dskip_e, sp["g_ssd"], nb, seq)
    d_uxbc, gconvw, gconvb = _conv_bwd(d_pre, proj, conv_w_full, nb, seq)
    gw_in_t = _grad_w_in_t(d_upool, d_z, d_uxbc, d_udt, u1)
    shard_rows = IN_WIDTH // N_CHIPS
    tok = start_reduce("w_in", jnp.stack([gw_in_t[k * shard_rows:(k + 1) * shard_rows] for k in range(N_CHIPS)]))
    gx, accm, gg_mix = _in_proj_bwd([d_upool, d_z, d_uxbc, d_udt], w_in_t, x2, dh1, mod3, sp["g_mix"], seq, before_last(tok))

    d_mod = jnp.concatenate([accm[:, 0], accm[:, 1], accf[:, 2], accf[:, 0], accf[:, 1], d_gf[:, 0]], 1)
    vec = _pack_vec({"g_mix": gg_mix, "conv_b": gconvb, "g_ssd": gg_ssd, "pool_scale": g_ps, "g_mlp": gg_mlp,
                     "g_final": gg_final, "dt_bias": gdtb, "a_log": ga, "d_skip_lanes": gdsk, "sq_err": sq})
    return gx, d_mod, vec, gw_pool, gconvw


def kernel(x, c, w_ada, b_ada, g_mix, w_in, conv_w, conv_b, dt_bias, a_log, d_skip, g_ssd, w_pool, pool_scale, w_out, g_mlp, w_up, w_down, g_final, loss_target, m_w_ada, m_b_ada, m_g_mix, m_w_in, m_conv_w, m_conv_b, m_dt_bias, m_a_log, m_d_skip, m_g_ssd, m_w_pool, m_pool_scale, m_w_out, m_g_mlp, m_w_up, m_w_down, m_g_final, v_w_ada, v_b_ada, v_g_mix, v_w_in, v_conv_w, v_conv_b, v_dt_bias, v_a_log, v_d_skip, v_g_ssd, v_w_pool, v_pool_scale, v_w_out, v_g_mlp, v_w_up, v_w_down, v_g_final):
    nb, seq, d = x.shape
    t = nb * seq
    xi, yi, ci = _mesh_pos()
    chip = 2 * xi + yi
    me = 4 * xi + 2 * yi + ci
    ada_cols = w_ada.shape[2]
    conv_cols = conv_w.shape[2]
    in_cols = w_in.shape[2]
    w_in_s, m_w_in_s, v_w_in_s = w_in[0].T, m_w_in[0].T, v_w_in[0].T

    c_send, c_recv, c_src, c_land, c_token = _exchange_start([c, conv_w[0]], ALL_PEERS, "cond_start")
    w_in_b = w_in_s.astype(BF)
    i_send, i_recv, i_src, i_land, in_token = _ici_start(
        [w_in_b], [jax.ShapeDtypeStruct((N_CHIPS,) + w_in_b.shape, BF)], _gather_sent, _gather_landing, "gather_start_w_in",
        after=c_token)
    c_own, c_got = _exchange_wait(c_send, c_recv, c_src, c_land, ALL_PEERS, in_token, "cond_wait")
    c8, convw8 = [lax.dynamic_update_slice(got, mine[None], (me,) + (0,) * mine.ndim) for got, mine in zip(c_got, c_own)]
    c_all = c8.reshape(N_DEV * nb, d)
    conv_w_full = convw8[0::2].transpose(1, 0, 2).reshape(4, N_CHIPS * conv_cols)
    b_shard = lax.dynamic_slice(b_ada, (0, chip * ada_cols), (1, ada_cols))
    mod_part, c_act = _ada_mod(c_all, w_ada[0], b_shard, in_token)
    mod_rows = mod_part.reshape(N_DEV, nb, ada_cols)
    m_send, m_recv, m_src, m_land, _ = _ici_start(
        [mod_rows], [jax.ShapeDtypeStruct((N_CHIPS, nb, ada_cols), F32)], _mod_sent, _mod_landing, "mod_start", after=mod_part)

    later = [w_out[0].astype(BF), w_up[0].astype(BF), w_down[0].astype(BF)]
    in_shard, in_land = _ici_wait(i_send, i_recv, i_src, i_land, [m_src[0]] + later, _gather_sent, _gather_landing,
                                  "gather_wait_w_in")
    (w_in4,) = _gather_finish(in_land, in_shard)
    w_in_t = w_in4.reshape(N_CHIPS * in_cols, d)
    mod_mine, mod_land = _ici_wait(m_send, m_recv, m_src, m_land, w_in_t, _mod_sent, _mod_landing, "mod_wait")
    mod_own = lax.dynamic_slice(mod_mine[0], (me, 0, 0), (1, nb, ada_cols))
    mod4 = lax.dynamic_update_slice(mod_land[0], mod_own, (chip, 0, 0))
    mod3 = mod4.transpose(1, 0, 2).reshape(nb, N_MOD, d)
    g_send, g_recv, g_src, g_land, first_token = _ici_start(
        later, [jax.ShapeDtypeStruct((N_CHIPS,) + s.shape, BF) for s in later], _gather_sent, _gather_landing, "gather_start",
        after=w_in4)

    def weights_arrived(after):
        shards, lands = _ici_wait(g_send, g_recv, g_src, g_land, after, _gather_sent, _gather_landing, "gather_wait")
        pending["forward"] = _forward_start(lands, "forward_start") + (shards,)
        return pending["forward"][3]

    def weights_later(after):
        f_send, f_recv, f_land, _, shards = pending["forward"]
        lands = _forward_wait(f_send, f_recv, f_land, after, "forward_wait")
        w_out4, w_up4, w_down4 = [lax.dynamic_update_slice(land, shard[None], (chip, 0, 0)) for land, shard in zip(lands, shards)]
        return w_out4.reshape(N_CHIPS * w_out.shape[1], d), w_up4, w_down4

    pending = {}

    def start_reduce(name, grad4):
        pending[name] = _reduce_start(grad4, "reduce_start_" + name)
        return pending[name][4]

    pos = jnp.stack([ci, chip, me]).astype(jnp.int32)
    early = ("w_out", "w_up", "w_down")

    def summed_half(name, after):
        r_send, r_recv, r_src, r_land, _ = pending[name]
        own, recv = _reduce_wait(r_send, r_recv, r_src, r_land, after, "reduce_wait_" + name)
        return _sum_eight(recv, own, pos)

    def before_last(token):
        pending["early_halves"] = _exchange_start([summed_half(n, token) for n in early], SIBLING, "halves_start_early")
        return pending["early_halves"][4]

    sp = dict(g_mix=g_mix, conv_b=conv_b, dt_bias=dt_bias, a_log=a_log, d_skip=d_skip, g_ssd=g_ssd,
              w_pool=w_pool[0], pool_scale=pool_scale, g_mlp=g_mlp, g_final=g_final.reshape(1, d))
    gx, d_mod, vec, gw_pool, gconvw = _local_step(
        x.reshape(t, d), loss_target.reshape(t, d), mod3, seq, w_in_t, first_token, weights_arrived, weights_later, start_reduce,
        before_last, conv_w_full, sp)

    small_parts = [vec, gw_pool.reshape(4 * POOL_GROUP, POOL_GROUP).astype(BF), gconvw, d_mod]
    s_send, s_recv, s_src, s_land, s_token = _exchange_start(small_parts, ALL_PEERS, "small_start")
    h_send, h_recv, h_src, h_land, h_token = _exchange_start([summed_half("w_in", s_token)], SIBLING, "halves_start")
    e_send, e_recv, e_src, e_land, _ = pending["early_halves"]
    e_own, e_got = _exchange_wait(e_send, e_recv, e_src, e_land, SIBLING, h_token, "halves_wait_early")
    res = {}
    for i, (n, w, m, v) in enumerate((("w_out", w_out, m_w_out, v_w_out), ("w_up", w_up, m_w_up, v_w_up),
                                      ("w_down", w_down, m_w_down, v_w_down))):
        g, dl, m2, v2 = _adam_big(e_own[i], e_got[i], w[0], m[0], v[0], pos)
        res[n] = (g[None], dl[None], m2[None], v2[None])

    s_own, s_got = _exchange_wait(s_send, s_recv, s_src, s_land, ALL_PEERS, res["w_down"][1], "small_wait")
    vec8, wpool8, convw8g, dmod8 = [lax.dynamic_update_slice(got, mine[None], (me,) + (0,) * mine.ndim)
                                    for got, mine in zip(s_got, s_own)]
    convw8s = lax.dynamic_slice(convw8g, (0, 0, chip * conv_cols), (N_DEV, 8, conv_cols))
    m_in = dict(b_ada=m_b_ada, g_mix=m_g_mix, conv_w=m_conv_w[0], conv_b=m_conv_b, dt_bias=m_dt_bias, a_log=m_a_log,
                d_skip=m_d_skip, g_ssd=m_g_ssd, w_pool=m_w_pool.reshape(4 * POOL_GROUP, POOL_GROUP), pool_scale=m_pool_scale,
                g_mlp=m_g_mlp, g_final=m_g_final.reshape(1, d))
    v_in = dict(b_ada=v_b_ada, g_mix=v_g_mix, conv_w=v_conv_w[0], conv_b=v_conv_b, dt_bias=v_dt_bias, a_log=v_a_log,
                d_skip=v_d_skip, g_ssd=v_g_ssd, w_pool=v_w_pool.reshape(4 * POOL_GROUP, POOL_GROUP), pool_scale=v_pool_scale,
                g_mlp=v_g_mlp, g_final=v_g_final.reshape(1, d))
    w_small = dict(sp, b_ada=b_ada, conv_w=conv_w[0], w_pool=w_pool.reshape(4 * POOL_GROUP, POOL_GROUP))
    loss_row, small = _small_adam(vec8, wpool8, convw8s, dmod8, {n: (w_small[n], m_in[n], v_in[n]) for n in _SMALL_PARAMS})

    dmod_all = dmod8.reshape(N_DEV * nb, N_CHIPS * ada_cols)
    dmod_cols = lax.dynamic_slice(dmod_all, (0, chip * ada_cols), (N_DEV * nb, ada_cols))
    res.update({n: tuple(r.reshape(w.shape) for r in small[n])
                for n, w in (("b_ada", b_ada), ("g_mix", g_mix), ("conv_w", conv_w), ("conv_b", conv_b), ("dt_bias", dt_bias),
                             ("a_log", a_log), ("d_skip", d_skip), ("g_ssd", g_ssd), ("w_pool", w_pool),
                             ("pool_scale", pool_scale), ("g_mlp", g_mlp), ("g_final", g_final))})
    g_ada, d_ada, m_ada, v_ada = _adam_ada(c_act.T.astype(BF), dmod_cols, w_ada[0], m_w_ada[0], v_w_ada[0])
    res["w_ada"] = (g_ada[None], d_ada[None], m_ada[None], v_ada[None])
    h_own, h_got = _exchange_wait(h_send, h_recv, h_src, h_land, SIBLING, g_ada, "halves_wait")
    rows3 = lambda a: jnp.transpose(a, (2, 0, 1))
    res["w_in"] = tuple(jnp.transpose(r, (1, 2, 0))
                        for r in _adam_rows(h_own[0], h_got[0], rows3(w_in), rows3(m_w_in), rows3(v_w_in), pos))

    loss = loss_row[0, 0]
    return (loss, gx.reshape(nb, seq, d), *[res[n][0] for n in _WEIGHTS], *[res[n][1] for n in _WEIGHTS],
            *[res[n][2] for n in _WEIGHTS], *[res[n][3] for n in _WEIGHTS])
```

```python
import jax
import jax.numpy as jnp
from jax import lax
from jax.experimental import pallas as pl
from jax.experimental.pallas import tpu as pltpu

F32 = jnp.float32
BF = jnp.bfloat16
MESH = pl.DeviceIdType.MESH

EPS = 1e-5
D_MODEL = 1024
POOL_WIDTH = 512
POOL_WINDOWS = (2, 4, 8, 16)
POOL_GROUP = 128
SSD_INNER = 1024
SSD_HEADS = 16
SSD_HEAD_DIM = 64
SSD_STATE = 128
GROUP_W = 512
CHUNK = 128
CONV_CH = 1536
OFF_DT = 3072
IN_WIDTH = 3088
IN_PAD = 3200
D_FF = 4096
N_MOD = 6
N_CHIPS = 4
N_DEV = 8
HALO = 16
CONV_HALO = 8

ADAM_LR = 0.001
ADAM_B1 = 0.9
ADAM_B2 = 0.999
ADAM_EPS = 1e-08
ADAM_WD = 0.01
ADAM_STEP = 10

VMEM_BYTES_V7X = 64 * 1024 * 1024


def _cp(semantics=None, vmem_mb=48, **kw):
    assert vmem_mb * 1024 * 1024 < VMEM_BYTES_V7X
    args = dict(vmem_limit_bytes=vmem_mb * 1024 * 1024, **kw)
    if semantics is not None:
        args["dimension_semantics"] = semantics
    return pltpu.CompilerParams(**args)


def _out(shape, dtype):
    return pltpu.HBM(shape, dtype)


def _pin(*arrays):
    return [pltpu.with_memory_space_constraint(a, pltpu.HBM) for a in arrays]


TOKEN_SHAPE = (8, 128)


def _order_operand(a):
    if a.shape == TOKEN_SHAPE:
        return pl.BlockSpec(memory_space=pltpu.VMEM), a
    return pl.BlockSpec(memory_space=pltpu.HBM), pltpu.with_memory_space_constraint(a, pltpu.HBM)


def _nn(a, b):
    return jnp.dot(a, b, preferred_element_type=F32)


def _nt(a, b):
    return lax.dot_general(a, b, (((1,), (1,)), ((), ())), preferred_element_type=F32)


def _tn(a, b):
    return lax.dot_general(a, b, (((0,), (0,)), ((), ())), preferred_element_type=F32)


def _split3(v):
    hi = v.astype(BF)
    r1 = v - hi.astype(F32)
    mid = r1.astype(BF)
    lo = (r1 - mid.astype(F32)).astype(BF)
    return hi, mid, lo


def _exact_nn(v, m01):
    hi, mid, lo = _split3(v)
    return _nn(hi, m01) + _nn(mid, m01) + _nn(lo, m01)


def _exact_nn_left(m01, v):
    hi, mid, lo = _split3(v)
    return _nn(m01, hi) + _nn(m01, mid) + _nn(m01, lo)


def _exact_nt_left(m01, v):
    hi, mid, lo = _split3(v)
    return _nt(m01, hi) + _nt(m01, mid) + _nt(m01, lo)


def _sigmoid(v):
    return 1.0 / (1.0 + jnp.exp(-v))


def _iota(shape, dim):
    return lax.broadcasted_iota(jnp.int32, shape, dim)


def _head_expand_matrix(heads, width):
    return (_iota((heads, width), 1) // SSD_HEAD_DIM == _iota((heads, width), 0)).astype(BF)


def _head_reduce_matrix(width, heads):
    return (_iota((width, heads), 0) // SSD_HEAD_DIM == _iota((width, heads), 1)).astype(BF)


def _mesh_pos():
    return lax.axis_index("x"), lax.axis_index("y"), lax.axis_index("c")


def _flip(v, bit):
    return v + bit - 2 * bit * v


_HBM = pl.BlockSpec(memory_space=pltpu.HBM)
_SEM = pl.BlockSpec(memory_space=pltpu.SEMAPHORE)
_DATAFLOW = pltpu.SideEffectType.DATAFLOW_SIDE_EFFECTING


def _peer_chip(x, y, j):
    return _flip(x, (j >> 1) & 1), _flip(y, j & 1)


def _all_peers():
    x, y, c = _mesh_pos()
    return [(_flip(x, (k >> 2) & 1), _flip(y, (k >> 1) & 1), _flip(c, k & 1)) for k in range(1, N_DEV)]


_START_IDS = ("sibling", "gather_start_w_in", "mod_start", "gather_start", "reduce_start_w_down", "reduce_start_w_up",
              "reduce_start_w_out", "reduce_start_w_in", "small_start", "cond_start")


def _start_params(name):
    return pltpu.CompilerParams(has_side_effects=_DATAFLOW, collective_id=_START_IDS.index(name))


def _handshake(peers):
    barrier = pltpu.get_barrier_semaphore()
    for peer in peers:
        pl.semaphore_signal(barrier, inc=1, device_id=peer, device_id_type=MESH)
    pl.semaphore_wait(barrier, len(peers))


def _ici_start(srcs, land_shapes, sent, landing, name, after):
    n = len(srcs)

    def body(*refs):
        src_refs, land_refs = refs[:n], refs[n:2 * n]
        send_sems, recv_sems = refs[2 * n + 1], refs[2 * n + 2]
        token = refs[-1]
        x, y, c = _mesh_pos()
        _handshake([(*_peer_chip(x, y, j), c) for j in range(1, N_CHIPS)])
        for j in range(1, N_CHIPS):
            px, py = _peer_chip(x, y, j)
            for a in range(n):
                pltpu.make_async_remote_copy(
                    src_ref=sent(src_refs[a], c, 2 * px + py), dst_ref=landing(land_refs[a], c, 2 * x + y),
                    send_sem=send_sems.at[a * (N_CHIPS - 1) + j - 1], recv_sem=recv_sems.at[a * (N_CHIPS - 1) + j - 1],
                    device_id=(px, py, c), device_id_type=MESH).start()
        token[...] = jnp.zeros_like(token)

    sems = pltpu.SemaphoreType.DMA((n * (N_CHIPS - 1),))
    after_spec, after = _order_operand(after)
    lands = [pltpu.with_memory_space_constraint(lax.empty(s.shape, s.dtype), pltpu.HBM) for s in land_shapes]
    outs = pl.pallas_call(
        body, name=name,
        out_shape=(sems, sems, *[pltpu.HBM(s.shape, s.dtype) for s in srcs],
                   *[pltpu.HBM(s.shape, s.dtype) for s in land_shapes], jax.ShapeDtypeStruct((8, 128), F32)),
        in_specs=[_HBM] * (2 * n) + [after_spec],
        out_specs=[_SEM, _SEM] + [_HBM] * (2 * n) + [pl.BlockSpec(memory_space=pltpu.VMEM)],
        input_output_aliases={i: 2 + i for i in range(2 * n)},
        compiler_params=_start_params(name),
    )(*_pin(*srcs), *lands, after)
    return outs[0], outs[1], outs[2:2 + n], outs[2 + n:2 + 2 * n], outs[-1]


def _ici_wait(send_sems, recv_sems, src_thru, land_thru, after, sent, landing, name):
    n = len(src_thru)
    afters = [_order_operand(a) for a in (after if isinstance(after, (list, tuple)) else [after])]

    def body(*refs):
        src_refs, land_refs = refs[:n], refs[n:2 * n]
        send_sems, recv_sems = refs[2 * n], refs[2 * n + 1]
        x, y, c = _mesh_pos()
        for j in range(1, N_CHIPS):
            px, py = _peer_chip(x, y, j)
            for a in range(n):
                cp = pltpu.make_async_remote_copy(
                    src_ref=sent(src_refs[a], c, 2 * px + py), dst_ref=landing(land_refs[a], c, 2 * px + py),
                    send_sem=send_sems.at[a * (N_CHIPS - 1) + j - 1], recv_sem=recv_sems.at[a * (N_CHIPS - 1) + j - 1],
                    device_id=(px, py, c), device_id_type=MESH)
                cp.wait_send()
                cp.wait_recv()

    outs = pl.pallas_call(
        body, name=name,
        out_shape=tuple(pltpu.HBM(s.shape, s.dtype) for s in (*src_thru, *land_thru)),
        in_specs=[_HBM] * (2 * n) + [_SEM, _SEM] + [s for s, _ in afters], out_specs=[_HBM] * (2 * n),
        input_output_aliases={i: i for i in range(2 * n)},
        compiler_params=pltpu.CompilerParams(has_side_effects=_DATAFLOW),
    )(*src_thru, *land_thru, send_sems, recv_sems, *[a for _, a in afters])
    return outs[:n], outs[n:]


def _col_half(ref, which, lead=()):
    hc = ref.shape[-1] // 2
    return ref.at[(*lead, slice(None), pl.ds(pl.multiple_of(which * hc, 128), hc))]


def _gather_sent(ref, c, dst_chip):
    return _col_half(ref, c)


def _gather_landing(ref, c, src_chip):
    return _col_half(ref, c, lead=(src_chip,))


def _mod_sent(ref, c, dst_chip):
    return ref.at[2 * dst_chip + c]


def _mod_landing(ref, c, src_chip):
    return ref.at[src_chip]


def _reduce_copy(src_ref, land_ref, send_sems, recv_sems, k, receiving):
    x, y, c = _mesh_pos()
    px, py, pc = _flip(x, (k >> 2) & 1), _flip(y, (k >> 1) & 1), _flip(c, k & 1)
    hc = src_ref.shape[2] // 2
    src = src_ref.at[2 * px + py, :, pl.ds(pl.multiple_of(pc * hc, 128), hc)]
    slot = (4 * px + 2 * py + pc) if receiving else (4 * x + 2 * y + c)
    return pltpu.make_async_remote_copy(
        src_ref=src, dst_ref=land_ref.at[slot], send_sem=send_sems.at[k - 1], recv_sem=recv_sems.at[k - 1],
        device_id=(px, py, pc), device_id_type=MESH)


def _reduce_start(grad4, name):
    k4, r, cols = grad4.shape

    def body(src_ref, land_ref, send_sems, recv_sems, src_thru, land_thru, token):
        _handshake(_all_peers())
        for k in range(1, N_DEV):
            _reduce_copy(src_ref, land_ref, send_sems, recv_sems, k, receiving=False).start()
        token[...] = jnp.zeros_like(token)

    sems = pltpu.SemaphoreType.DMA((N_DEV - 1,))
    land = pltpu.with_memory_space_constraint(lax.empty((N_DEV, r, cols // 2), grad4.dtype), pltpu.HBM)
    return pl.pallas_call(
        body, name=name,
        out_shape=(sems, sems, pltpu.HBM(grad4.shape, grad4.dtype), pltpu.HBM(land.shape, land.dtype),
                   jax.ShapeDtypeStruct((8, 128), F32)),
        in_specs=[_HBM, _HBM], out_specs=[_SEM, _SEM, _HBM, _HBM, pl.BlockSpec(memory_space=pltpu.VMEM)],
        input_output_aliases={0: 2, 1: 3},
        compiler_params=_start_params(name),
    )(*_pin(grad4), land)


def _reduce_wait(send_sems, recv_sems, src_thru, land_thru, after, name):
    def body(src_ref, land_ref, send_sems, recv_sems, after_ref, src_out, land_out):
        for k in range(1, N_DEV):
            cp = _reduce_copy(src_ref, land_ref, send_sems, recv_sems, k, receiving=True)
            cp.wait_send()
            cp.wait_recv()

    return pl.pallas_call(
        body, name=name,
        out_shape=(pltpu.HBM(src_thru.shape, src_thru.dtype), pltpu.HBM(land_thru.shape, land_thru.dtype)),
        in_specs=[_HBM, _HBM, _SEM, _SEM, _order_operand(after)[0]], out_specs=[_HBM, _HBM],
        input_output_aliases={0: 0, 1: 1},
        compiler_params=pltpu.CompilerParams(has_side_effects=_DATAFLOW),
    )(src_thru, land_thru, send_sems, recv_sems, _order_operand(after)[1])


SIBLING_COLLECTIVE_ID = 0
_SIBLING_PARAMS = pltpu.CompilerParams(has_side_effects=_DATAFLOW, collective_id=SIBLING_COLLECTIVE_ID)


def _sibling_handshake():
    x, y, c = _mesh_pos()
    barrier = pltpu.get_barrier_semaphore()
    pl.semaphore_signal(barrier, inc=1, device_id=(x, y, 1 - c), device_id_type=MESH)
    pl.semaphore_wait(barrier, 1)


def _peer_copy(src_ref, land_ref, send_sems, recv_sems, idx, k, receiving):
    x, y, c = _mesh_pos()
    px, py, pc = _flip(x, (k >> 2) & 1), _flip(y, (k >> 1) & 1), _flip(c, k & 1)
    if land_ref.shape[0] == N_DEV:
        slot = (4 * px + 2 * py + pc) if receiving else (4 * x + 2 * y + c)
    else:
        slot = pc if receiving else c
    return pltpu.make_async_remote_copy(
        src_ref=src_ref, dst_ref=land_ref.at[slot], send_sem=send_sems.at[idx], recv_sem=recv_sems.at[idx],
        device_id=(px, py, pc), device_id_type=MESH)


def _exchange_start(arrays, peers, name):
    n = len(arrays)

    def body(*refs):
        src_refs, land_refs = refs[:n], refs[n:2 * n]
        send_sems, recv_sems = refs[2 * n], refs[2 * n + 1]
        token = refs[-1]
        if peers == SIBLING:
            _sibling_handshake()
        else:
            _handshake(_all_peers())
        for j, k in enumerate(peers):
            for a in range(n):
                _peer_copy(src_refs[a], land_refs[a], send_sems, recv_sems, a * len(peers) + j, k, receiving=False).start()
        token[...] = jnp.zeros_like(token)

    sems = pltpu.SemaphoreType.DMA((n * len(peers),))
    n_slots = N_DEV if len(peers) > 1 else 2
    lands = [pltpu.with_memory_space_constraint(lax.empty((n_slots,) + a.shape, a.dtype), pltpu.HBM) for a in arrays]
    outs = pl.pallas_call(
        body, name=name,
        out_shape=(sems, sems, *[pltpu.HBM(a.shape, a.dtype) for a in arrays], *[pltpu.HBM(l.shape, l.dtype) for l in lands],
                   jax.ShapeDtypeStruct((8, 128), F32)),
        in_specs=[_HBM] * (2 * n), out_specs=[_SEM, _SEM] + [_HBM] * (2 * n) + [pl.BlockSpec(memory_space=pltpu.VMEM)],
        input_output_aliases={i: 2 + i for i in range(2 * n)},
        compiler_params=_SIBLING_PARAMS if peers == SIBLING else _start_params(name),
    )(*_pin(*arrays), *lands)
    return outs[0], outs[1], outs[2:2 + n], outs[2 + n:2 + 2 * n], outs[-1]


def _exchange_wait(send_sems, recv_sems, src_thru, land_thru, peers, after, name):
    n = len(src_thru)

    def body(*refs):
        src_refs, land_refs = refs[:n], refs[n:2 * n]
        send_sems, recv_sems = refs[2 * n], refs[2 * n + 1]
        for j, k in enumerate(peers):
            for a in range(n):
                cp = _peer_copy(src_refs[a], land_refs[a], send_sems, recv_sems, a * len(peers) + j, k, receiving=True)
                cp.wait_send()
                cp.wait_recv()

    outs = pl.pallas_call(
        body, name=name,
        out_shape=tuple(pltpu.HBM(s.shape, s.dtype) for s in (*src_thru, *land_thru)),
        in_specs=[_HBM] * (2 * n) + [_SEM, _SEM, _order_operand(after)[0]], out_specs=[_HBM] * (2 * n),
        input_output_aliases={i: i for i in range(2 * n)},
        compiler_params=pltpu.CompilerParams(has_side_effects=_DATAFLOW),
    )(*src_thru, *land_thru, send_sems, recv_sems, _order_operand(after)[1])
    return outs[:n], outs[n:]


ALL_PEERS = tuple(range(1, N_DEV))
SIBLING = (1,)


def _sum_eight(recv, grad4, pos):
    n, r, hc = recv.shape
    steps = 2
    tc = hc // steps

    def body(pos_ref, r_ref, g_ref, o_ref):
        me = pos_ref[2]
        o_ref[...] = jnp.zeros_like(o_ref)
        for s in range(n):
            @pl.when(me == s)
            def _():
                o_ref[...] += g_ref[0].astype(F32)

            @pl.when(me != s)
            def _():
                o_ref[...] += r_ref[s].astype(F32)

    grid_spec = pltpu.PrefetchScalarGridSpec(
        num_scalar_prefetch=1, grid=(steps,),
        in_specs=[pl.BlockSpec((n, r, tc), lambda i, pos: (0, 0, i)),
                  pl.BlockSpec((1, r, tc), lambda i, pos: (pos[1], 0, pos[0] * steps + i))],
        out_specs=pl.BlockSpec((r, tc), lambda i, pos: (0, i)))
    return pl.pallas_call(body, name="sum_eight", grid_spec=grid_spec, out_shape=_out((r, hc), F32),
                          compiler_params=_cp(("parallel",), 32))(pos, *_pin(recv, grad4))


def _forward_copy(land_ref, send_sems, recv_sems, idx, j, receiving):
    x, y, c = _mesh_pos()
    px, py = _peer_chip(x, y, j)
    mine = _col_half(land_ref, c, lead=(2 * px + py,))
    theirs = _col_half(land_ref, 1 - c, lead=(2 * px + py,))
    return pltpu.make_async_remote_copy(
        src_ref=mine, dst_ref=theirs if receiving else mine, send_sem=send_sems.at[idx], recv_sem=recv_sems.at[idx],
        device_id=(x, y, 1 - c), device_id_type=MESH)


def _forward_start(lands, name):
    n = len(lands)

    def body(*refs):
        land_refs, send_sems, recv_sems, token = refs[:n], refs[n], refs[n + 1], refs[-1]
        _sibling_handshake()
        for j in range(1, N_CHIPS):
            for a in range(n):
                _forward_copy(land_refs[a], send_sems, recv_sems, a * (N_CHIPS - 1) + j - 1, j, receiving=False).start()
        token[...] = jnp.zeros_like(token)

    sems = pltpu.SemaphoreType.DMA((n * (N_CHIPS - 1),))
    outs = pl.pallas_call(
        body, name=name,
        out_shape=(sems, sems, *[pltpu.HBM(l.shape, l.dtype) for l in lands], jax.ShapeDtypeStruct((8, 128), F32)),
        in_specs=[_HBM] * n, out_specs=[_SEM, _SEM] + [_HBM] * n + [pl.BlockSpec(memory_space=pltpu.VMEM)],
        input_output_aliases={i: 2 + i for i in range(n)},
        compiler_params=_SIBLING_PARAMS,
    )(*lands)
    return outs[0], outs[1], outs[2:2 + n], outs[-1]


def _forward_wait(send_sems, recv_sems, lands_thru, after, name):
    n = len(lands_thru)

    def body(*refs):
        land_refs, send_sems, recv_sems = refs[:n], refs[n], refs[n + 1]
        for j in range(1, N_CHIPS):
            for a in range(n):
                cp = _forward_copy(land_refs[a], send_sems, recv_sems, a * (N_CHIPS - 1) + j - 1, j, receiving=True)
                cp.wait_send()
                cp.wait_recv()

    return pl.pallas_call(
        body, name=name,
        out_shape=tuple(pltpu.HBM(l.shape, l.dtype) for l in lands_thru),
        in_specs=[_HBM] * n + [_SEM, _SEM, _order_operand(after)[0]], out_specs=[_HBM] * n,
        input_output_aliases={i: i for i in range(n)},
        compiler_params=pltpu.CompilerParams(has_side_effects=_DATAFLOW),
    )(*lands_thru, send_sems, recv_sems, _order_operand(after)[1])


def _gather_finish(lands, shards):
    n = len(lands)
    any_spec = _HBM

    def body(*refs):
        shard_refs, out_refs = refs[n:2 * n], refs[2 * n:3 * n]
        send_sems, recv_sems, local_sems = refs[3 * n:]
        x, y, c = _mesh_pos()
        chip = 2 * x + y
        local, sends = [], []
        for a in range(n):
            cp = pltpu.make_async_copy(shard_refs[a], out_refs[a].at[chip], local_sems.at[a])
            cp.start()
            local.append(cp)
        _sibling_handshake()
        for j in range(1, N_CHIPS):
            px, py = _peer_chip(x, y, j)
            for a in range(n):
                landed = _col_half(out_refs[a], c, lead=(2 * px + py,))
                cp = pltpu.make_async_remote_copy(
                    src_ref=landed, dst_ref=landed, send_sem=send_sems.at[a, j], recv_sem=recv_sems.at[a, j],
                    device_id=(x, y, 1 - c), device_id_type=MESH)
                cp.start()
                sends.append(cp)
        for j in range(1, N_CHIPS):
            px, py = _peer_chip(x, y, j)
            for a in range(n):
                other = _col_half(out_refs[a], 1 - c, lead=(2 * px + py,))
                pltpu.make_async_remote_copy(
                    src_ref=other, dst_ref=other, send_sem=send_sems.at[a, j], recv_sem=recv_sems.at[a, j],
                    device_id=(x, y, 1 - c), device_id_type=MESH).wait_recv()
        for cp in sends:
            cp.wait_send()
        for cp in local:
            cp.wait()

    return pl.pallas_call(
        body, name="gather_finish",
        out_shape=[_out(l.shape, l.dtype) for l in lands],
        in_specs=[any_spec] * (2 * n), out_specs=[any_spec] * n,
        input_output_aliases={i: i for i in range(n)},
        scratch_shapes=[pltpu.SemaphoreType.DMA((n, N_CHIPS))] * 2 + [pltpu.SemaphoreType.DMA((n,))],
        compiler_params=_cp(vmem_mb=16, collective_id=SIBLING_COLLECTIVE_ID),
    )(*lands, *shards)


def _adam_math(w, g, m, v):
    m2 = ADAM_B1 * m + (1.0 - ADAM_B1) * g
    v2 = ADAM_B2 * v + (1.0 - ADAM_B2) * (g * g)
    m_hat = m2 / (1.0 - ADAM_B1 ** ADAM_STEP)
    v_hat = v2 / (1.0 - ADAM_B2 ** ADAM_STEP)
    delta = -ADAM_LR * (m_hat / (jnp.sqrt(v_hat) + ADAM_EPS) + ADAM_WD * w)
    return delta, m2, v2


def _adam_big(g_own, g_pair, w, m, v, pos):
    r, c = w.shape
    steps = 4
    tr = r // steps
    assert tr * steps == r and tr % 8 == 0

    def body(pos_ref, go_ref, gp_ref, w_ref, m_ref, v_ref, g_ref, d_ref, m2_ref, v2_ref):
        core = pos_ref[0]
        g = jnp.concatenate([jnp.where(core == 0, go_ref[...], gp_ref[0]), jnp.where(core == 1, go_ref[...], gp_ref[0])], 1)
        d, m2, v2 = _adam_math(w_ref[...], g, m_ref[...], v_ref[...])
        g_ref[...] = g
        d_ref[...] = d
        m2_ref[...] = m2
        v2_ref[...] = v2

    spec = pl.BlockSpec((tr, c), lambda i, pos: (i, 0))
    grid_spec = pltpu.PrefetchScalarGridSpec(
        num_scalar_prefetch=1, grid=(steps,),
        in_specs=[pl.BlockSpec((tr, c // 2), lambda i, pos: (i, 0)),
                  pl.BlockSpec((1, tr, c // 2), lambda i, pos: (1 - pos[0], i, 0)), spec, spec, spec],
        out_specs=[spec] * 4)
    sh = _out((r, c), F32)
    return pl.pallas_call(body, name="adam_big", grid_spec=grid_spec, out_shape=[sh] * 4,
                          compiler_params=_cp(("parallel",), 32))(pos, *_pin(g_own, g_pair, w, m, v))


def _adam_rows(g_own, g_pair, w3, m3, v3, pos):
    r, _, c = w3.shape
    tr = 128

    def body(pos_ref, go_ref, gp_ref, w_ref, m_ref, v_ref, g_ref, d_ref, m2_ref, v2_ref):
        core = pos_ref[0]
        g = jnp.concatenate([jnp.where(core == 0, go_ref[...], gp_ref[0]), jnp.where(core == 1, go_ref[...], gp_ref[0])], 1)
        d, m2, v2 = _adam_math(w_ref[:, 0, :], g, m_ref[:, 0, :], v_ref[:, 0, :])
        g_ref[:, 0, :] = g
        d_ref[:, 0, :] = d
        m2_ref[:, 0, :] = m2
        v2_ref[:, 0, :] = v2

    spec = pl.BlockSpec((tr, 1, c), lambda i, pos: (i, 0, 0))
    grid_spec = pltpu.PrefetchScalarGridSpec(
        num_scalar_prefetch=1, grid=(pl.cdiv(r, tr),),
        in_specs=[pl.BlockSpec((tr, c // 2), lambda i, pos: (i, 0)),
                  pl.BlockSpec((1, tr, c // 2), lambda i, pos: (1 - pos[0], i, 0)), spec, spec, spec],
        out_specs=[spec] * 4)
    sh = _out(w3.shape, F32)
    return pl.pallas_call(body, name="adam_rows", grid_spec=grid_spec, out_shape=[sh] * 4,
                          compiler_params=_cp(("parallel",), 32))(pos, *_pin(g_own, g_pair, w3, m3, v3))


def _adam_ada(c_act_t, dmod_cols, w, m, v):
    r, c = w.shape
    tr = 256
    assert r % tr == 0

    def body(ct_ref, dm_ref, w_ref, m_ref, v_ref, g_ref, d_ref, m2_ref, v2_ref):
        g = _nn(ct_ref[...], dm_ref[...].astype(BF))
        d, m2, v2 = _adam_math(w_ref[...], g, m_ref[...], v_ref[...])
        g_ref[...] = g
        d_ref[...] = d
        m2_ref[...] = m2
        v2_ref[...] = v2

    spec = pl.BlockSpec((tr, c), lambda i: (i, 0))
    sh = _out((r, c), F32)
    return pl.pallas_call(
        body, name="adam_ada", grid=(r // tr,),
        in_specs=[pl.BlockSpec((tr, c_act_t.shape[1]), lambda i: (i, 0)), pl.BlockSpec(dmod_cols.shape, lambda i: (0, 0)),
                  spec, spec, spec],
        out_specs=[spec] * 4, out_shape=[sh] * 4, compiler_params=_cp(("parallel",), 48))(*_pin(c_act_t, dmod_cols, w, m, v))


def _ada_mod(c_all, w_shard, b_shard, token):
    nb, d = c_all.shape
    cols = w_shard.shape[1]
    tc = 512

    def body(c_ref, w_ref, b_ref, tok_ref, mod_ref, act_ref):
        cv = c_ref[...]
        act = cv * _sigmoid(cv)
        act_ref[...] = act
        mod_ref[...] = _nn(act.astype(BF), w_ref[...].astype(BF)) + b_ref[...]

    return pl.pallas_call(
        body, name="ada_mod", grid=(cols // tc,),
        in_specs=[pl.BlockSpec((nb, d), lambda i: (0, 0)), pl.BlockSpec((d, tc), lambda i: (0, i)),
                  pl.BlockSpec((1, tc), lambda i: (0, i)), _token_spec()],
        out_specs=[pl.BlockSpec((nb, tc), lambda i: (0, i)), pl.BlockSpec((nb, d), lambda i: (0, 0))],
        out_shape=[_out((nb, cols), F32), _out((nb, d), F32)],
        compiler_params=_cp(("arbitrary",), 32))(*_pin(c_all, w_shard, b_shard, token))


SUB_ROWS = 256
ROW_TILE = 512


def _sub_rows(tm):
    return [slice(s, s + SUB_ROWS) for s in range(0, tm, SUB_ROWS)] if tm > SUB_ROWS else [slice(0, tm)]


_RESIDENT = pl.BlockSpec(memory_space=pltpu.VMEM)


def _token_spec():
    return pl.BlockSpec((8, 128), lambda *_: (0, 0))


def _in_proj(x, mod3, g_mix, w_in_t, seq, token):
    t, d = x.shape
    tm = min(ROW_TILE, seq)
    tps = seq // tm

    def body(x_ref, mod_ref, g_ref, w_ref, tok_ref, proj_ref, u1_ref):
        for rows in _sub_rows(tm):
            xv = x_ref[rows, :]
            r = lax.rsqrt(jnp.mean(xv * xv, -1, keepdims=True) + EPS)
            u = (xv * r * g_ref[...]) * (1.0 + mod_ref[0, 1:2, :]) + mod_ref[0, 0:1, :]
            ub = u.astype(BF)
            u1_ref[rows, :] = ub
            proj_ref[rows, 0:OFF_DT] = _nt(ub, w_ref[0:OFF_DT, :])
            proj_ref[rows, OFF_DT:IN_PAD] = jnp.zeros((rows.stop - rows.start, IN_PAD - OFF_DT), F32)
            proj_ref[rows, OFF_DT:IN_WIDTH] = _nt(ub, w_ref[OFF_DT:IN_WIDTH, :])

    return pl.pallas_call(
        body, name="in_proj", grid=(t // tm,),
        in_specs=[pl.BlockSpec((tm, d), lambda i: (i, 0)), pl.BlockSpec((1, N_MOD, d), lambda i: (i // tps, 0, 0)),
                  pl.BlockSpec((1, d), lambda i: (0, 0)), _RESIDENT, _token_spec()],
        out_specs=[pl.BlockSpec((tm, IN_PAD), lambda i: (i, 0)), pl.BlockSpec((tm, d), lambda i: (i, 0))],
        out_shape=[_out((t, IN_PAD), F32), _out((t, d), BF)],
        compiler_params=_cp(("parallel",), 40))(*_pin(x, mod3, g_mix), w_in_t, *_pin(token))


def _pool_tile(seq):
    return min(1024, seq)


def _pool_fwd(proj, w_pool, pool_scale, nb, seq, token):
    ts = _pool_tile(seq)
    nt = seq // ts

    def body(u_ref, halo_ref, wp_ref, ps_ref, tok_ref, yp_ref, p_ref):
        i = pl.program_id(1)
        halo = jnp.where(i == 0, 0.0, halo_ref[...])
        u = u_ref[...]
        ext = jnp.concatenate([halo, u], 0)
        tpos = i * ts + _iota((ts, 1), 0)
        for g, w in enumerate(POOL_WINDOWS):
            gs = slice(g * POOL_GROUP, (g + 1) * POOL_GROUP)
            s = ext[:, gs]
            sh = 1
            while sh < w:
                s = s + pltpu.roll(s, sh, 0)
                sh *= 2
            cnt = jnp.minimum(tpos + 1, w).astype(F32)
            pb = (s[HALO:] / cnt - u[:, gs]).astype(BF)
            p_ref[:, gs] = pb
            yp_ref[:, gs] = (_nn(pb, wp_ref[g].astype(BF)) * ps_ref[:, gs]).astype(BF)

    hb = ts // HALO
    return pl.pallas_call(
        body, name="pool_fwd", grid=(nb, nt),
        in_specs=[pl.BlockSpec((ts, POOL_WIDTH), lambda b, i: (b * nt + i, 0)),
                  pl.BlockSpec((HALO, POOL_WIDTH), lambda b, i: (jnp.maximum((b * nt + i) * hb - 1, 0), 0)),
                  pl.BlockSpec((4, POOL_GROUP, POOL_GROUP), lambda b, i: (0, 0, 0)),
                  pl.BlockSpec((1, POOL_WIDTH), lambda b, i: (0, 0)), _token_spec()],
        out_specs=[pl.BlockSpec((ts, POOL_WIDTH), lambda b, i: (b * nt + i, 0))] * 2,
        out_shape=[_out((nb * seq, POOL_WIDTH), BF)] * 2,
        compiler_params=_cp(("parallel", "parallel"), 32))(*_pin(proj, proj, w_pool, pool_scale, token))


def _conv_pre(uxbc, halo, cw, cb, first):
    halo = jnp.where(first, 0.0, halo)
    ext = jnp.concatenate([halo, uxbc], 0)
    pre = cb + uxbc * cw[3:4]
    for k in (2, 1, 0):
        pre = pre + pltpu.roll(ext, 3 - k, 0)[CONV_HALO:] * cw[k:k + 1]
    return pre


def _chunk_terms(pre, udt, dtb, alog):
    sg = _sigmoid(pre)
    xbc = pre * sg
    dtp = udt[:, :SSD_HEADS] + dtb
    dt = jnp.maximum(dtp, 0.0) + jnp.log(1.0 + jnp.exp(-jnp.abs(dtp)))
    a = -jnp.exp(alog)
    da = dt * a
    tril = (_iota((CHUNK, CHUNK), 0) >= _iota((CHUNK, CHUNK), 1))
    acum = _exact_nn_left(tril.astype(BF), da)
    eye = (_iota((SSD_HEADS, SSD_HEADS), 0) == _iota((SSD_HEADS, SSD_HEADS), 1)).astype(BF)
    acum_t = _exact_nt_left(eye, acum)
    expand = _head_expand_matrix(SSD_HEADS, SSD_INNER)
    acum_e = _exact_nn(acum, expand)
    dt_e = _exact_nn(dt, expand)
    last_e = acum_e[CHUNK - 1:CHUNK]
    return dict(pre=pre, sg=sg, xbc=xbc, dtp=dtp, dt=dt, a=a, acum=acum, acum_t=acum_t, tril=tril,
                dt_e=dt_e, e_a=jnp.exp(acum_e), d_out=jnp.exp(last_e - acum_e), c_dec=jnp.exp(last_e))


def _head_decay(r, h):
    seg = r["acum"][:, h:h + 1] - r["acum_t"][h:h + 1, :]
    return jnp.where(r["tril"], jnp.exp(jnp.minimum(seg, 0.0)), 0.0)


SSD_SUB = 4
SSD_ROWS = SSD_SUB * CHUNK


def _ssd_specs(nb, seq, reverse):
    ns = seq // SSD_ROWS
    per = seq // CONV_HALO

    def cidx(c):
        return (ns - 1 - c) if reverse else c

    def row(b, c):
        return b * ns + cidx(c)

    specs = [
        pl.BlockSpec((SSD_ROWS, CONV_CH), lambda b, c: (row(b, c), 1)),
        pl.BlockSpec((CONV_HALO, CONV_CH),
                     lambda b, c: (jnp.maximum(b * per + cidx(c) * (SSD_ROWS // CONV_HALO) - 1, 0), 1)),
        pl.BlockSpec((SSD_ROWS, GROUP_W), lambda b, c: (row(b, c), 1)),
        pl.BlockSpec((SSD_ROWS, GROUP_W), lambda b, c: (row(b, c), 2)),
        pl.BlockSpec((SSD_ROWS, 128), lambda b, c: (row(b, c), OFF_DT // 128)),
    ]
    return specs, row, cidx, ns


def _const_spec(shape):
    return pl.BlockSpec(shape, lambda b, c: (0,) * len(shape))


def _ssd_fwd(proj, conv_w, conv_b, dt_bias, a_log, dskip_e, g_ssd, nb, seq):
    specs, row, cidx, ns = _ssd_specs(nb, seq, reverse=False)

    def body(uxbc_ref, halo_ref, z0_ref, z1_ref, udt_ref, cw_ref, cb_ref, dtb_ref, alog_ref, dsk_ref, gs_ref,
             yssd_ref, yssm_ref, hprev_ref, pre_ref, h_ref, yd_ref):
        c = pl.program_id(1)

        @pl.when(c == 0)
        def _():
            h_ref[...] = jnp.zeros_like(h_ref)

        for sub in range(SSD_SUB):
            rows = slice(sub * CHUNK, (sub + 1) * CHUNK)
            if sub == 0:
                halo, first = halo_ref[...], c == 0
            else:
                halo, first = uxbc_ref[sub * CHUNK - CONV_HALO:sub * CHUNK, :], False
            pre = _conv_pre(uxbc_ref[rows, :], halo, cw_ref[...], cb_ref[...], first)
            pre_ref[rows, :] = pre
            r = _chunk_terms(pre, udt_ref[rows, :], dtb_ref[...], alog_ref[...])
            xbc = r["xbc"]
            xs = xbc[:, :SSD_INNER]
            xdt = xs * r["dt_e"]
            xdt_b = xdt.astype(BF)
            xdo_b = (xdt * r["d_out"]).astype(BF)
            hprev_ref[0, sub] = h_ref[...]
            for g in range(2):
                gs = slice(g * GROUP_W, (g + 1) * GROUP_W)
                bg = xbc[:, SSD_INNER + g * SSD_STATE:SSD_INNER + (g + 1) * SSD_STATE].astype(BF)
                cg = xbc[:, SSD_INNER + (2 + g) * SSD_STATE:SSD_INNER + (3 + g) * SSD_STATE].astype(BF)
                scores = _nt(cg, bg)
                hg = h_ref[g]
                y_off = _nn(cg, hg.astype(BF)) * r["e_a"][:, gs]
                for hh in range(8):
                    h = g * 8 + hh
                    hs = slice(h * SSD_HEAD_DIM, (h + 1) * SSD_HEAD_DIM)
                    m = (scores * _head_decay(r, h)).astype(BF)
                    yd_ref[sub, :, hs] = _nn(m, xdt_b[:, hs])
                h_ref[g] = hg * r["c_dec"][:, gs] + _tn(bg, xdo_b[:, gs])
                y = yd_ref[sub, :, gs] + y_off + dsk_ref[:, gs] * xs[:, gs]
                yssm_ref[rows, gs] = y
                zg = (z0_ref if g == 0 else z1_ref)[rows, :]
                yg = y * (zg * _sigmoid(zg))
                rg = lax.rsqrt(jnp.mean(yg * yg, -1, keepdims=True) + EPS)
                yssd_ref[rows, gs] = (yg * rg * gs_ref[:, gs]).astype(BF)

    t = nb * seq
    return pl.pallas_call(
        body, name="ssd_fwd", grid=(nb, ns),
        in_specs=specs + [_const_spec((4, CONV_CH)), _const_spec((1, CONV_CH)), _const_spec((1, SSD_HEADS)),
                          _const_spec((1, SSD_HEADS)), _const_spec((1, SSD_INNER)), _const_spec((1, SSD_INNER))],
        out_specs=[pl.BlockSpec((SSD_ROWS, SSD_INNER), lambda b, c: (row(b, c), 0)),
                   pl.BlockSpec((SSD_ROWS, SSD_INNER), lambda b, c: (row(b, c), 0)),
                   pl.BlockSpec((1, SSD_SUB, 2, SSD_STATE, GROUP_W), lambda b, c: (b, c, 0, 0, 0)),
                   pl.BlockSpec((SSD_ROWS, CONV_CH), lambda b, c: (row(b, c), 0))],
        out_shape=[_out((t, SSD_INNER), BF), _out((t, SSD_INNER), F32),
                   _out((nb, seq // CHUNK, 2, SSD_STATE, GROUP_W), F32), _out((t, CONV_CH), F32)],
        scratch_shapes=[pltpu.VMEM((2, SSD_STATE, GROUP_W), F32), pltpu.VMEM((SSD_SUB, CHUNK, SSD_INNER), F32)],
        compiler_params=_cp(("arbitrary", "arbitrary"), 56),
    )(*_pin(proj, proj, proj, proj, proj, conv_w, conv_b, dt_bias, a_log, dskip_e, g_ssd))


def _out_proj(y_pool, y_ssd, w_out, x, mod3, g_mlp, seq):
    t, d = x.shape
    tm = 512
    tps = seq // tm if seq >= tm else 1
    tm = min(tm, seq)

    def body(yp_ref, ys_ref, w_ref, x_ref, mod_ref, g_ref, h1_ref, o_ref, u2_ref):
        for rows in _sub_rows(tm):
            o = _nn(jnp.concatenate([yp_ref[rows, :], ys_ref[rows, :]], 1), w_ref[...])
            o_ref[rows, :] = o.astype(BF)
            h1 = x_ref[rows, :] + mod_ref[0, 2:3, :] * o
            h1_ref[rows, :] = h1
            r = lax.rsqrt(jnp.mean(h1 * h1, -1, keepdims=True) + EPS)
            u2_ref[rows, :] = ((h1 * r * g_ref[...]) * (1.0 + mod_ref[0, 4:5, :]) + mod_ref[0, 3:4, :]).astype(BF)

    row = lambda i: (i, 0)
    return pl.pallas_call(
        body, name="out_proj", grid=(t // tm,),
        in_specs=[pl.BlockSpec((tm, POOL_WIDTH), row), pl.BlockSpec((tm, SSD_INNER), row),
                  _RESIDENT, pl.BlockSpec((tm, d), row),
                  pl.BlockSpec((1, N_MOD, d), lambda i: (i // tps, 0, 0)), pl.BlockSpec((1, d), lambda i: (0, 0))],
        out_specs=[pl.BlockSpec((tm, d), row)] * 3,
        out_shape=[_out((t, d), F32), _out((t, d), BF), _out((t, d), BF)],
        compiler_params=_cp(("parallel",), 48))(*_pin(y_pool, y_ssd), w_out, *_pin(x, mod3, g_mlp))


def _mlp_up(u2, w_up4):
    t, d = u2.shape
    tm = min(1024, t)
    nk, _, cols = w_up4.shape

    def body(u_ref, w_ref, a_ref):
        a_ref[...] = _nn(u_ref[...], w_ref[pl.program_id(1)]).astype(BF)

    return pl.pallas_call(
        body, name="mlp_up", grid=(t // tm, nk),
        in_specs=[pl.BlockSpec((tm, d), lambda i, k: (i, 0)), _RESIDENT],
        out_specs=pl.BlockSpec((tm, cols), lambda i, k: (i, k)),
        out_shape=_out((t, nk * cols), BF),
        compiler_params=_cp(("parallel", "parallel"), 32))(*_pin(u2), w_up4)


def _mlp_down_loss(a_up, w_down, h1, mod3, g_final, target, seq):
    t, d = h1.shape
    nb = t // seq
    tm = min(ROW_TILE, seq)
    tps = seq // tm

    def body(a_ref, w_ref, h1_ref, mod_ref, g_ref, tg_ref, ddn_ref, dh2_ref, sq_ref, gg_ref, dgf_ref):
        i = pl.program_id(0)

        @pl.when(i == 0)
        def _():
            sq_ref[...] = jnp.zeros_like(sq_ref)
            gg_ref[...] = jnp.zeros_like(gg_ref)

        @pl.when(i % tps == 0)
        def _():
            dgf_ref[...] = jnp.zeros_like(dgf_ref)

        gate = mod_ref[0, 5:6, :]
        sq = gg = dgf = 0.0
        for rows in _sub_rows(tm):
            f = jnp.square(jnp.maximum(a_ref[rows, :], 0))
            dn = _nn(f, w_ref[...])
            h2 = h1_ref[rows, :] + gate * dn
            r = lax.rsqrt(jnp.mean(h2 * h2, -1, keepdims=True) + EPS)
            hh = h2 * r
            err = hh * g_ref[...] - tg_ref[rows, :]
            dy = err * (1.0 / d)
            dhat = dy * g_ref[...]
            dh2 = r * (dhat - hh * jnp.mean(dhat * hh, -1, keepdims=True))
            dh2_ref[rows, :] = dh2
            ddn_ref[rows, :] = (dh2 * gate).astype(BF)
            sq = sq + jnp.sum(err * err, 0, keepdims=True)
            gg = gg + jnp.sum(dy * hh, 0, keepdims=True)
            dgf = dgf + jnp.sum(dh2 * dn, 0, keepdims=True)
        sq_ref[...] += sq
        gg_ref[...] += gg
        dgf_ref[0] += dgf

    row = lambda i: (i, 0)
    vec = pl.BlockSpec((1, d), lambda i: (0, 0))
    return pl.pallas_call(
        body, name="mlp_down_loss", grid=(t // tm,),
        in_specs=[pl.BlockSpec((tm, D_FF), row), _RESIDENT, pl.BlockSpec((tm, d), row),
                  pl.BlockSpec((1, N_MOD, d), lambda i: (i // tps, 0, 0)), vec, pl.BlockSpec((tm, d), row)],
        out_specs=[pl.BlockSpec((tm, d), row), pl.BlockSpec((tm, d), row), vec, vec,
                   pl.BlockSpec((1, 1, d), lambda i: (i // tps, 0, 0))],
        out_shape=[_out((t, d), BF), _out((t, d), F32), _out((1, d), F32),
                   _out((1, d), F32), _out((nb, 1, d), F32)],
        compiler_params=_cp(("arbitrary",), 44))(*_pin(a_up), w_down, *_pin(h1, mod3, g_final, target))


def _tn_matmul(a, b, tk, tn, name, square_relu=False, out3=False):
    t, kdim = a.shape
    ndim = b.shape[1]

    def body(a_ref, b_ref, o_ref):
        av = a_ref[...]
        if square_relu:
            av = jnp.square(jnp.maximum(av, 0))
        res = _tn(av, b_ref[...]).astype(BF)
        if out3:
            o_ref[0] = res
        else:
            o_ref[...] = res

    if out3:
        out_spec = pl.BlockSpec((1, tk, tn), lambda j, i: (j, i, 0))
        out_shape = _out((ndim // tn, kdim, tn), BF)
    else:
        out_spec = pl.BlockSpec((tk, tn), lambda j, i: (i, j))
        out_shape = _out((kdim, ndim), BF)
    return pl.pallas_call(
        body, name=name, grid=(ndim // tn, kdim // tk),
        in_specs=[pl.BlockSpec((t, tk), lambda j, i: (0, i)), pl.BlockSpec((t, tn), lambda j, i: (0, j))],
        out_specs=out_spec, out_shape=out_shape,
        compiler_params=_cp(("parallel", "parallel"), 56))(*_pin(a, b))


def _mlp_down_bwd(d_dn, w_down4, a_up, token):
    t, d = d_dn.shape
    tm = min(1024, t)
    nk, rows, _ = w_down4.shape

    def body(g_ref, w_ref, a_ref, tok_ref, o_ref):
        df = _nt(g_ref[...], w_ref[pl.program_id(1)])
        o_ref[...] = (df * (2.0 * jnp.maximum(a_ref[...], 0).astype(F32))).astype(BF)

    return pl.pallas_call(
        body, name="mlp_down_bwd", grid=(t // tm, nk),
        in_specs=[pl.BlockSpec((tm, d), lambda i, k: (i, 0)), _RESIDENT,
                  pl.BlockSpec((tm, rows), lambda i, k: (i, k)), _token_spec()],
        out_specs=pl.BlockSpec((tm, rows), lambda i, k: (i, k)),
        out_shape=_out((t, nk * rows), BF),
        compiler_params=_cp(("parallel", "parallel"), 32))(*_pin(d_dn), w_down4, *_pin(a_up, token))


def _mlp_up_bwd(d_a, w_up4, h1, dh2, o, mod3, g_mlp, seq, token):
    t, d = h1.shape
    nb = t // seq
    tm = min(ROW_TILE, seq)
    tps = seq // tm
    nk = w_up4.shape[0]
    cols = w_up4.shape[2]

    def body(da_ref, w_ref, h1_ref, dh2_ref, o_ref, mod_ref, g_ref, tok_ref, dh1_ref, do_ref, acc_ref, gg_ref):
        i = pl.program_id(0)

        @pl.when(i == 0)
        def _():
            gg_ref[...] = jnp.zeros_like(gg_ref)

        @pl.when(i % tps == 0)
        def _():
            acc_ref[...] = jnp.zeros_like(acc_ref)

        gg = a_shift = a_scale = a_gate = 0.0
        for rows in _sub_rows(tm):
            du = _nt(da_ref[rows, 0:cols], w_ref[0])
            for k in range(1, nk):
                du = du + _nt(da_ref[rows, k * cols:(k + 1) * cols], w_ref[k])
            h1 = h1_ref[rows, :]
            r = lax.rsqrt(jnp.mean(h1 * h1, -1, keepdims=True) + EPS)
            hh = h1 * r
            n2 = hh * g_ref[...]
            dn2 = du * (1.0 + mod_ref[0, 4:5, :])
            dhat = dn2 * g_ref[...]
            dh1 = dh2_ref[rows, :] + r * (dhat - hh * jnp.mean(dhat * hh, -1, keepdims=True))
            dh1_ref[rows, :] = dh1
            do_ref[rows, :] = (dh1 * mod_ref[0, 2:3, :]).astype(BF)
            gg = gg + jnp.sum(dn2 * hh, 0, keepdims=True)
            a_shift = a_shift + jnp.sum(du, 0, keepdims=True)
            a_scale = a_scale + jnp.sum(du * n2, 0, keepdims=True)
            a_gate = a_gate + jnp.sum(dh1 * o_ref[rows, :].astype(F32), 0, keepdims=True)
        gg_ref[...] += gg
        acc_ref[0, 0:1, :] += a_shift
        acc_ref[0, 1:2, :] += a_scale
        acc_ref[0, 2:3, :] += a_gate

    row = lambda i: (i, 0)
    vec = pl.BlockSpec((1, d), lambda i: (0, 0))
    return pl.pallas_call(
        body, name="mlp_up_bwd", grid=(t // tm,),
        in_specs=[pl.BlockSpec((tm, D_FF), row), _RESIDENT, pl.BlockSpec((tm, d), row),
                  pl.BlockSpec((tm, d), row), pl.BlockSpec((tm, d), row),
                  pl.BlockSpec((1, N_MOD, d), lambda i: (i // tps, 0, 0)), vec, _token_spec()],
        out_specs=[pl.BlockSpec((tm, d), row), pl.BlockSpec((tm, d), row),
                   pl.BlockSpec((1, 8, d), lambda i: (i // tps, 0, 0)), vec],
        out_shape=[_out((t, d), F32), _out((t, d), BF),
                   _out((nb, 8, d), F32), _out((1, d), F32)],
        compiler_params=_cp(("arbitrary",), 44))(*_pin(d_a), w_up4, *_pin(h1, dh2, o, mod3, g_mlp, token))


def _out_proj_bwd(d_o, w_out, token):
    t, d = d_o.shape
    tm = min(1024, t)

    def body(g_ref, w_ref, tok_ref, dp_ref, ds_ref):
        gv = g_ref[...]
        dp_ref[...] = _nt(gv, w_ref[0:POOL_WIDTH, :])
        ds_ref[...] = _nt(gv, w_ref[POOL_WIDTH:, :])

    row = lambda i: (i, 0)
    return pl.pallas_call(
        body, name="out_proj_bwd", grid=(t // tm,),
        in_specs=[pl.BlockSpec((tm, d), row), _RESIDENT, _token_spec()],
        out_specs=[pl.BlockSpec((tm, POOL_WIDTH), row), pl.BlockSpec((tm, SSD_INNER), row)],
        out_shape=[_out((t, POOL_WIDTH), F32), _out((t, SSD_INNER), F32)],
        compiler_params=_cp(("parallel",), 32))(*_pin(d_o), w_out, *_pin(token))


def _pool_bwd(d_ypool, p, w_pool, pool_scale, nb, seq):
    ts = _pool_tile(seq)
    nt = seq // ts
    hb = ts // HALO
    last_block = nb * seq // HALO - 1

    def body(dy_ref, halo_ref, p_ref, wp_ref, ps_ref, du_ref, gw_ref, gs_ref):
        b = pl.program_id(0)
        i = pl.program_id(1)

        @pl.when((b == 0) & (i == 0))
        def _():
            gw_ref[...] = jnp.zeros_like(gw_ref)
            gs_ref[...] = jnp.zeros_like(gs_ref)

        halo = jnp.where(i == nt - 1, 0.0, halo_ref[...])
        dy = dy_ref[...]
        ext = jnp.concatenate([dy, halo], 0)
        tpos = i * ts + _iota((ts + HALO, 1), 0)
        n_ext = ts + HALO
        for g, w in enumerate(POOL_WINDOWS):
            gs = slice(g * POOL_GROUP, (g + 1) * POOL_GROUP)
            wg = wp_ref[g].astype(BF)
            pg = p_ref[:, gs]
            pw = _nn(pg, wg)
            gs_ref[:, gs] += jnp.sum(dy[:, gs] * pw, 0, keepdims=True)
            dpw = (ext[:, gs] * ps_ref[:, gs]).astype(BF)
            gw_ref[g] += _tn(pg, dpw[:ts])
            dp = _nt(dpw, wg)
            cnt = jnp.minimum(tpos + 1, w).astype(F32)
            s = dp / cnt
            sh = 1
            while sh < w:
                s = s + pltpu.roll(s, n_ext - sh, 0)
                sh *= 2
            du_ref[:, gs] = (s[:ts] - dp[:ts]).astype(BF)

    return pl.pallas_call(
        body, name="pool_bwd", grid=(nb, nt),
        in_specs=[pl.BlockSpec((ts, POOL_WIDTH), lambda b, i: (b * nt + i, 0)),
                  pl.BlockSpec((HALO, POOL_WIDTH), lambda b, i: (jnp.minimum((b * nt + i + 1) * hb, last_block), 0)),
                  pl.BlockSpec((ts, POOL_WIDTH), lambda b, i: (b * nt + i, 0)),
                  pl.BlockSpec((4, POOL_GROUP, POOL_GROUP), lambda b, i: (0, 0, 0)),
                  pl.BlockSpec((1, POOL_WIDTH), lambda b, i: (0, 0))],
        out_specs=[pl.BlockSpec((ts, POOL_WIDTH), lambda b, i: (b * nt + i, 0)),
                   pl.BlockSpec((4, POOL_GROUP, POOL_GROUP), lambda b, i: (0, 0, 0)),
                   pl.BlockSpec((1, POOL_WIDTH), lambda b, i: (0, 0))],
        out_shape=[_out((nb * seq, POOL_WIDTH), BF), _out((4, POOL_GROUP, POOL_GROUP), F32),
                   _out((1, POOL_WIDTH), F32)],
        compiler_params=_cp(("arbitrary", "arbitrary"), 32))(*_pin(d_ypool, d_ypool, p, w_pool, pool_scale))


def _ssd_bwd(proj, pre, d_yssd, yssm, h_prev, dt_bias, a_log, dskip_e, g_ssd, nb, seq):
    specs, row, cidx, ns = _ssd_specs(nb, seq, reverse=True)
    specs = specs[2:]

    def body(z0_ref, z1_ref, udt_ref, pre_ref, dys_ref, yssm_ref, hprev_ref,
             dtb_ref, alog_ref, dsk_ref, gs_ref,
             dz_ref, dpre_ref, dudt_ref, ggs_ref, gdsk_ref, ga_ref, gdtb_ref,
             g_ref, dxdt_ref, dyv_ref):
        b = pl.program_id(0)
        c = pl.program_id(1)

        @pl.when(c == 0)
        def _():
            g_ref[...] = jnp.zeros_like(g_ref)

        @pl.when((b == 0) & (c == 0))
        def _():
            ggs_ref[...] = jnp.zeros_like(ggs_ref)
            gdsk_ref[...] = jnp.zeros_like(gdsk_ref)
            ga_ref[...] = jnp.zeros_like(ga_ref)
            gdtb_ref[...] = jnp.zeros_like(gdtb_ref)

        for sub in reversed(range(SSD_SUB)):
            chunk(sub, z0_ref, z1_ref, udt_ref, pre_ref, dys_ref, yssm_ref, hprev_ref, dtb_ref, alog_ref, dsk_ref, gs_ref,
                  dz_ref, dpre_ref, dudt_ref, ggs_ref, gdsk_ref, ga_ref, gdtb_ref, g_ref, dxdt_ref.at[sub], dyv_ref.at[sub])

    def chunk(sub, z0_ref, z1_ref, udt_ref, pre_ref, dys_ref, yssm_ref, hprev_ref,
              dtb_ref, alog_ref, dsk_ref, gs_ref,
              dz_ref, dpre_ref, dudt_ref, ggs_ref, gdsk_ref, ga_ref, gdtb_ref,
              g_ref, dxdt_ref, dyv_ref):
        rows = slice(sub * CHUNK, (sub + 1) * CHUNK)
        r = _chunk_terms(pre_ref[rows, :], udt_ref[rows, :], dtb_ref[...], alog_ref[...])
        xbc = r["xbc"]
        xs = xbc[:, :SSD_INNER]
        dt_e = r["dt_e"]
        xdt = xs * dt_e
        xdt_b = xdt.astype(BF)
        reduce_m = _head_reduce_matrix(GROUP_W, 8)

        def head_sums(v):
            return _nn(v.astype(BF), reduce_m)

        onehot16 = lambda h: (_iota((1, SSD_HEADS), 1) == h).astype(F32)
        onecol16 = lambda h: (_iota((SSD_HEADS, 1), 0) == h).astype(F32)

        d_acum = jnp.zeros((CHUNK, SSD_HEADS), F32)
        d_acum_t = jnp.zeros((SSD_HEADS, CHUNK), F32)
        d_alast = jnp.zeros((1, SSD_HEADS), F32)
        place8 = lambda g: (_iota((8, SSD_HEADS), 1) == _iota((8, SSD_HEADS), 0) + 8 * g).astype(BF)
        d_b, d_c = [], []
        for g in range(2):
            gs = slice(g * GROUP_W, (g + 1) * GROUP_W)
            zg = (z0_ref if g == 0 else z1_ref)[rows, :]
            sz = _sigmoid(zg)
            silu_z = zg * sz
            ys = yssm_ref[rows, gs]
            yg = ys * silu_z
            rg = lax.rsqrt(jnp.mean(yg * yg, -1, keepdims=True) + EPS)
            yh = yg * rg
            dys = dys_ref[rows, gs]
            ggs_ref[:, gs] += jnp.sum(dys * yh, 0, keepdims=True)
            dyh = dys * gs_ref[:, gs]
            dyg = rg * (dyh - yh * jnp.mean(dyh * yh, -1, keepdims=True))
            dy = dyg * silu_z
            dz_ref[rows, gs] = (dyg * ys * (sz * (1.0 + zg * (1.0 - sz)))).astype(BF)
            gdsk_ref[:, gs] += jnp.sum(dy * xs[:, gs], 0, keepdims=True)
            dyv_ref[:, gs] = dy
            dy_b = dy.astype(BF)

            bg = xbc[:, SSD_INNER + g * SSD_STATE:SSD_INNER + (g + 1) * SSD_STATE].astype(BF)
            cg = xbc[:, SSD_INNER + (2 + g) * SSD_STATE:SSD_INNER + (3 + g) * SSD_STATE].astype(BF)
            scores = _nt(cg, bg)
            hg = hprev_ref[0, sub, g]
            hg_b = hg.astype(BF)
            gg = g_ref[g]
            gg_b = gg.astype(BF)
            e_a = r["e_a"][:, gs]
            d_out = r["d_out"][:, gs]
            c_dec = r["c_dec"][:, gs]
            zc = _nn(cg, hg_b)
            wv = e_a * dy
            wv_b = wv.astype(BF)
            da_g = head_sums(wv * zc)
            dcg = _nt(wv_b, hg_b)
            d_hprev = _tn(cg, wv_b)
            vg = _nn(bg, gg_b)
            dxdt_g = d_out * vg
            dd_out = head_sums(xdt[:, gs] * vg)
            dbg = _nt((xdt[:, gs] * d_out).astype(BF), gg_b)
            dcd = _exact_nn(jnp.sum(gg * hg, 0, keepdims=True), reduce_m)
            d_out8 = jnp.exp(r["acum"][CHUNK - 1:CHUNK, 8 * g:8 * g + 8] - r["acum"][:, 8 * g:8 * g + 8])
            c_dec8 = jnp.exp(r["acum"][CHUNK - 1:CHUNK, 8 * g:8 * g + 8])
            t8 = dd_out * d_out8
            d_alast = d_alast + _exact_nn(jnp.sum(t8, 0, keepdims=True) + dcd * c_dec8, place8(g))
            d_acum = d_acum + _exact_nn(da_g - t8, place8(g))
            dsc = jnp.zeros((CHUNK, CHUNK), F32)
            for hh in range(8):
                h = g * 8 + hh
                hs = slice(h * SSD_HEAD_DIM, (h + 1) * SSD_HEAD_DIM)
                lam = _head_decay(r, h)
                m = scores * lam
                dyh_b = dy_b[:, hh * SSD_HEAD_DIM:(hh + 1) * SSD_HEAD_DIM]
                dm = _nt(dyh_b, xdt_b[:, hs])
                tm_ = dm * m
                d_acum = d_acum + jnp.sum(tm_, 1, keepdims=True) * onehot16(h)
                d_acum_t = d_acum_t + onecol16(h) * jnp.sum(tm_, 0, keepdims=True)
                dsc = dsc + dm * lam
                dxdt_ref[:, hs] = _tn(m.astype(BF), dyh_b) + dxdt_g[:, hh * SSD_HEAD_DIM:(hh + 1) * SSD_HEAD_DIM]
            dsc_b = dsc.astype(BF)
            d_c.append(dcg + _nn(dsc_b, bg))
            d_b.append(dbg + _tn(dsc_b, cg))
            g_ref[g] = d_hprev + c_dec * gg

        eye = (_iota((CHUNK, CHUNK), 0) == _iota((CHUNK, CHUNK), 1)).astype(BF)
        d_acum = d_acum - _exact_nt_left(eye, d_acum_t)
        is_last = (_iota((CHUNK, 1), 0) == CHUNK - 1).astype(F32)
        d_acum = d_acum + is_last * d_alast
        triu = (_iota((CHUNK, CHUNK), 0) <= _iota((CHUNK, CHUNK), 1)).astype(BF)
        d_da = _exact_nn_left(triu, d_acum)
        dt = r["dt"]
        ga_ref[...] += jnp.sum(d_da * dt, 0, keepdims=True)
        dxdt = dxdt_ref[...]
        reduce16 = _head_reduce_matrix(SSD_INNER, SSD_HEADS)
        d_dt = d_da * r["a"] + _nn((dxdt * xs).astype(BF), reduce16)
        d_udt = d_dt * _sigmoid(r["dtp"])
        gdtb_ref[...] += jnp.sum(d_udt, 0, keepdims=True)
        dudt_ref[rows, :] = jnp.zeros((CHUNK, dudt_ref.shape[1]), BF)
        dudt_ref[rows, 0:SSD_HEADS] = d_udt.astype(BF)
        pre, sg = r["pre"], r["sg"]
        dsilu = sg * (1.0 + pre * (1.0 - sg))
        dpre_ref[rows, 0:SSD_INNER] = (dsk_ref[...] * dyv_ref[...] + dxdt * dt_e) * dsilu[:, 0:SSD_INNER]
        for g in range(2):
            bs = slice(SSD_INNER + g * SSD_STATE, SSD_INNER + (g + 1) * SSD_STATE)
            cs = slice(SSD_INNER + (2 + g) * SSD_STATE, SSD_INNER + (3 + g) * SSD_STATE)
            dpre_ref[rows, bs] = d_b[g] * dsilu[:, bs]
            dpre_ref[rows, cs] = d_c[g] * dsilu[:, cs]

    t = nb * seq
    vec = _const_spec((1, SSD_INNER))
    small = _const_spec((1, SSD_HEADS))
    return pl.pallas_call(
        body, name="ssd_bwd", grid=(nb, ns),
        in_specs=specs + [pl.BlockSpec((SSD_ROWS, CONV_CH), lambda b, c: (row(b, c), 0)),
                          pl.BlockSpec((SSD_ROWS, SSD_INNER), lambda b, c: (row(b, c), 0)),
                          pl.BlockSpec((SSD_ROWS, SSD_INNER), lambda b, c: (row(b, c), 0)),
                          pl.BlockSpec((1, SSD_SUB, 2, SSD_STATE, GROUP_W), lambda b, c: (b, cidx(c), 0, 0, 0)),
                          small, small, vec, vec],
        out_specs=[pl.BlockSpec((SSD_ROWS, SSD_INNER), lambda b, c: (row(b, c), 0)),
                   pl.BlockSpec((SSD_ROWS, CONV_CH), lambda b, c: (row(b, c), 0)),
                   pl.BlockSpec((SSD_ROWS, 128), lambda b, c: (row(b, c), 0)),
                   vec, vec, small, small],
        out_shape=[_out((t, SSD_INNER), BF), _out((t, CONV_CH), F32),
                   _out((t, 128), BF), _out((1, SSD_INNER), F32),
                   _out((1, SSD_INNER), F32), _out((1, SSD_HEADS), F32),
                   _out((1, SSD_HEADS), F32)],
        scratch_shapes=[pltpu.VMEM((2, SSD_STATE, GROUP_W), F32), pltpu.VMEM((SSD_SUB, CHUNK, SSD_INNER), F32),
                        pltpu.VMEM((SSD_SUB, CHUNK, SSD_INNER), F32)],
        compiler_params=_cp(("arbitrary", "arbitrary"), 56),
    )(*_pin(proj, proj, proj, pre, d_yssd, yssm, h_prev, dt_bias, a_log, dskip_e, g_ssd))


def _grad_w_out(y_pool, y_ssd, d_o):
    t, d = d_o.shape
    tk = POOL_WIDTH
    n_s = SSD_INNER // tk

    def body(p_ref, s_ref, g_ref, o_ref):
        i = pl.program_id(0)

        @pl.when(i == 0)
        def _():
            o_ref[...] = _tn(p_ref[...], g_ref[...]).astype(BF)

        @pl.when(i > 0)
        def _():
            o_ref[...] = _tn(s_ref[...], g_ref[...]).astype(BF)

    return pl.pallas_call(
        body, name="grad_w_out", grid=(1 + n_s,),
        in_specs=[pl.BlockSpec((t, tk), lambda i: (0, 0)), pl.BlockSpec((t, tk), lambda i: (0, jnp.maximum(i - 1, 0))),
                  pl.BlockSpec((t, d), lambda i: (0, 0))],
        out_specs=pl.BlockSpec((tk, d), lambda i: (i, 0)),
        out_shape=_out((POOL_WIDTH + SSD_INNER, d), BF),
        compiler_params=_cp(("parallel",), 56))(*_pin(y_pool, y_ssd, d_o))


def _grad_w_in_t(d_upool, d_z, d_uxbc, d_udt, u1):
    t, d = u1.shape
    tk = 512
    n_z, n_x = SSD_INNER // tk, CONV_CH // tk

    def body(p_ref, z_ref, x_ref, dt_ref, u_ref, o_ref):
        i = pl.program_id(0)

        @pl.when(i == 0)
        def _():
            o_ref[...] = _tn(p_ref[...], u_ref[...]).astype(BF)

        @pl.when((i >= 1) & (i < 1 + n_z))
        def _():
            o_ref[...] = _tn(z_ref[...], u_ref[...]).astype(BF)

        @pl.when((i >= 1 + n_z) & (i < 1 + n_z + n_x))
        def _():
            o_ref[...] = _tn(x_ref[...], u_ref[...]).astype(BF)

        @pl.when(i == 1 + n_z + n_x)
        def _():
            o_ref[0:128, :] = _tn(dt_ref[...], u_ref[...]).astype(BF)

    return pl.pallas_call(
        body, name="grad_w_in", grid=(2 + n_z + n_x,),
        in_specs=[pl.BlockSpec((t, tk), lambda i: (0, 0)),
                  pl.BlockSpec((t, tk), lambda i: (0, jnp.clip(i - 1, 0, n_z - 1))),
                  pl.BlockSpec((t, tk), lambda i: (0, jnp.clip(i - 1 - n_z, 0, n_x - 1))),
                  pl.BlockSpec((t, 128), lambda i: (0, 0)), pl.BlockSpec((t, d), lambda i: (0, 0))],
        out_specs=pl.BlockSpec((tk, d), lambda i: (i, 0)),
        out_shape=_out((IN_PAD, d), BF),
        compiler_params=_cp(("parallel",), 56))(*_pin(d_upool, d_z, d_uxbc, d_udt, u1))


def _conv_bwd(d_pre, proj, conv_w, nb, seq):
    ts = min(256, seq)
    nt = seq // ts
    hb = ts // CONV_HALO
    last_block = nb * seq // CONV_HALO - 1
    n_ext = CHUNK + CONV_HALO

    def body(dp_ref, dnext_ref, u_ref, cw_ref, du_ref, gw_ref, gb_ref):
        b = pl.program_id(0)
        i = pl.program_id(1)

        @pl.when((b == 0) & (i == 0))
        def _():
            gw_ref[...] = jnp.zeros_like(gw_ref)
            gb_ref[...] = jnp.zeros_like(gb_ref)

        for c0 in range(0, CONV_CH, 128):
            cs = slice(c0, c0 + 128)
            cw = cw_ref[:, cs]
            gw = [0.0] * 4
            gb = 0.0
            for r0 in range(0, ts, CHUNK):
                dp = dp_ref[r0:r0 + CHUNK, cs]
                u = u_ref[r0:r0 + CHUNK, cs]
                if r0 + CHUNK < ts:
                    below = dp_ref[r0 + CHUNK:r0 + CHUNK + CONV_HALO, cs]
                else:
                    below = jnp.where(i == nt - 1, 0.0, dnext_ref[:, cs])
                ext_d = jnp.concatenate([dp, below], 0)
                du = dp * cw[3:4]
                gw[3] = gw[3] + jnp.sum(dp * u, 0, keepdims=True)
                for k in (2, 1, 0):
                    shifted = pltpu.roll(ext_d, n_ext - (3 - k), 0)[:CHUNK]
                    du = du + shifted * cw[k:k + 1]
                    gw[k] = gw[k] + jnp.sum(shifted * u, 0, keepdims=True)
                gb = gb + jnp.sum(dp, 0, keepdims=True)
                du_ref[r0:r0 + CHUNK, cs] = du.astype(BF)
            for k in range(4):
                gw_ref[k:k + 1, cs] += gw[k]
            gb_ref[:, cs] += gb

    return pl.pallas_call(
        body, name="conv_bwd", grid=(nb, nt),
        in_specs=[pl.BlockSpec((ts, CONV_CH), lambda b, i: (b * nt + i, 0)),
                  pl.BlockSpec((CONV_HALO, CONV_CH), lambda b, i: (jnp.minimum((b * nt + i + 1) * hb, last_block), 0)),
                  pl.BlockSpec((ts, CONV_CH), lambda b, i: (b * nt + i, 1)),
                  pl.BlockSpec((4, CONV_CH), lambda b, i: (0, 0))],
        out_specs=[pl.BlockSpec((ts, CONV_CH), lambda b, i: (b * nt + i, 0)),
                   pl.BlockSpec((8, CONV_CH), lambda b, i: (0, 0)), pl.BlockSpec((1, CONV_CH), lambda b, i: (0, 0))],
        out_shape=[_out((nb * seq, CONV_CH), BF), _out((8, CONV_CH), F32),
                   _out((1, CONV_CH), F32)],
        compiler_params=_cp(("arbitrary", "arbitrary"), 48))(*_pin(d_pre, d_pre, proj, conv_w))


def _in_proj_bwd(d_parts, w_in_t, x, dh1, mod3, g_mix, seq, token):
    t, d = x.shape
    nb = t // seq
    tm = min(ROW_TILE, seq)
    tps = seq // tm

    widths = [p.shape[1] for p in d_parts]

    def body(d0_ref, d1_ref, d2_ref, d3_ref, w_ref, x_ref, dh1_ref, mod_ref, g_ref, tok_ref, gx_ref, acc_ref, gg_ref):
        i = pl.program_id(0)

        @pl.when(i == 0)
        def _():
            gg_ref[...] = jnp.zeros_like(gg_ref)

        @pl.when(i % tps == 0)
        def _():
            acc_ref[...] = jnp.zeros_like(acc_ref)

        gg = a_shift = a_scale = 0.0
        for rows in _sub_rows(tm):
            d_cat = jnp.concatenate([p_ref[rows, :] for p_ref in (d0_ref, d1_ref, d2_ref)], 1)
            du = _nn(d_cat, w_ref[0:OFF_DT, :]) + _nn(d3_ref[rows, 0:IN_WIDTH - OFF_DT], w_ref[OFF_DT:IN_WIDTH, :])
            xv = x_ref[rows, :]
            r = lax.rsqrt(jnp.mean(xv * xv, -1, keepdims=True) + EPS)
            hh = xv * r
            n1 = hh * g_ref[...]
            dn1 = du * (1.0 + mod_ref[0, 1:2, :])
            dhat = dn1 * g_ref[...]
            gx_ref[rows, :] = dh1_ref[rows, :] + r * (dhat - hh * jnp.mean(dhat * hh, -1, keepdims=True))
            gg = gg + jnp.sum(dn1 * hh, 0, keepdims=True)
            a_shift = a_shift + jnp.sum(du, 0, keepdims=True)
            a_scale = a_scale + jnp.sum(du * n1, 0, keepdims=True)
        gg_ref[...] += gg
        acc_ref[0, 0:1, :] += a_shift
        acc_ref[0, 1:2, :] += a_scale

    row = lambda i: (i, 0)
    vec = pl.BlockSpec((1, d), lambda i: (0, 0))
    return pl.pallas_call(
        body, name="in_proj_bwd", grid=(t // tm,),
        in_specs=[pl.BlockSpec((tm, wd), row) for wd in widths] +
                 [_RESIDENT, pl.BlockSpec((tm, d), row),
                  pl.BlockSpec((tm, d), row), pl.BlockSpec((1, N_MOD, d), lambda i: (i // tps, 0, 0)), vec, _token_spec()],
        out_specs=[pl.BlockSpec((tm, d), row), pl.BlockSpec((1, 8, d), lambda i: (i // tps, 0, 0)), vec],
        out_shape=[_out((t, d), F32), _out((nb, 8, d), F32),
                   _out((1, d), F32)],
        compiler_params=_cp(("arbitrary",), 40))(*_pin(*d_parts), w_in_t, *_pin(x, dh1, mod3, g_mix, token))


_VEC_LAYOUT = (("g_mix", 1024), ("conv_b", 1536), ("g_ssd", 1024), ("pool_scale", 512), ("g_mlp", 1024),
               ("g_final", 1024), ("dt_bias", 128), ("a_log", 128), ("d_skip_lanes", 1024), ("sq_err", 1024))
_VEC_OFFSET = {}
_off = 0
for _name, _n in _VEC_LAYOUT:
    _VEC_OFFSET[_name] = _off
    _off += _n
_SMALL_PARAMS = ("b_ada", "g_mix", "conv_w", "conv_b", "dt_bias", "a_log", "d_skip", "g_ssd", "w_pool", "pool_scale",
                 "g_mlp", "g_final")


def _pack_vec(parts):
    cols = []
    for name, n in _VEC_LAYOUT:
        v = parts[name]
        if v.shape[1] < n:
            v = jnp.pad(v, ((0, 0), (0, n - v.shape[1])))
        cols.append(v)
    return jnp.concatenate(cols, 1)


def _small_adam(vec_all, wpool_all, convw_all, dmod_all, params):
    names = _SMALL_PARAMS
    nin = 4 + 3 * len(names)

    def body(*refs):
        vec_ref, wp_ref, cw_ref, dm_ref = refs[:4]
        prm = {n: refs[4 + 3 * i:7 + 3 * i] for i, n in enumerate(names)}
        loss_ref = refs[nin]
        outs = {n: refs[nin + 1 + 4 * i:nin + 5 + 4 * i] for i, n in enumerate(names)}
        vsum = vec_ref[0]
        for s in range(1, N_DEV):
            vsum = vsum + vec_ref[s]

        def lanes(name, n):
            off = _VEC_OFFSET[name]
            return vsum[:, off:off + n]

        grads = {n: lanes(n, prm[n][0].shape[1]) for n in ("g_mix", "conv_b", "g_ssd", "pool_scale", "g_mlp", "g_final", "dt_bias")}
        grads["a_log"] = lanes("a_log", SSD_HEADS) * (-jnp.exp(prm["a_log"][0][...]))
        per_lane = jnp.broadcast_to(lanes("d_skip_lanes", SSD_INNER), (8, SSD_INNER))
        grads["d_skip"] = _exact_nn(per_lane, _head_reduce_matrix(SSD_INNER, SSD_HEADS))[0:1]
        gwp = wp_ref[0].astype(F32)
        gcw = cw_ref[0]
        gb = jnp.sum(dm_ref[0], 0, keepdims=True)
        for s in range(1, N_DEV):
            gwp = gwp + wp_ref[s].astype(F32)
            gcw = gcw + cw_ref[s]
            gb = gb + jnp.sum(dm_ref[s], 0, keepdims=True)
        grads["w_pool"] = gwp
        grads["conv_w"] = gcw[0:4]
        grads["b_ada"] = gb
        total = jnp.sum(lanes("sq_err", D_MODEL), 1, keepdims=True) * (0.5 / D_MODEL)
        loss_ref[...] = jnp.broadcast_to(total, loss_ref.shape)
        for n in names:
            w_ref, m_ref, v_ref = prm[n]
            g = grads[n]
            d, m2, v2 = _adam_math(w_ref[...], g, m_ref[...], v_ref[...])
            g_ref, d_ref, m2_ref, v2_ref = outs[n]
            g_ref[...] = g
            d_ref[...] = d
            m2_ref[...] = m2
            v2_ref[...] = v2

    flat = [vec_all, wpool_all, convw_all, dmod_all]
    out_shape = [jax.ShapeDtypeStruct((1, 128), F32)]
    for n in names:
        flat += list(params[n])
        out_shape += [jax.ShapeDtypeStruct(params[n][0].shape, F32)] * 4
    vm = pl.BlockSpec(memory_space=pltpu.VMEM)
    res = pl.pallas_call(body, name="small_adam", out_shape=out_shape, in_specs=[vm] * len(flat),
                         out_specs=[vm] * len(out_shape), compiler_params=_cp(vmem_mb=48))(*flat)
    return res[0], {n: res[1 + 4 * i:5 + 4 * i] for i, n in enumerate(names)}


_WEIGHTS = ("w_ada", "b_ada", "g_mix", "w_in", "conv_w", "conv_b", "dt_bias", "a_log", "d_skip", "g_ssd", "w_pool",
            "pool_scale", "w_out", "g_mlp", "w_up", "w_down", "g_final")


def _local_step(x2, tg2, mod3, seq, w_in_t, first_token, weights_arrived, weights_later, start_reduce, before_last,
                conv_w_full, sp):
    t, d = x2.shape
    nb = t // seq
    dskip_e = jnp.repeat(sp["d_skip"], SSD_HEAD_DIM, axis=1)
    proj, u1 = _in_proj(x2, mod3, sp["g_mix"], w_in_t, seq, first_token)
    y_ssd, yssm, h_prev, pre = _ssd_fwd(proj, conv_w_full, sp["conv_b"], sp["dt_bias"], sp["a_log"], dskip_e, sp["g_ssd"], nb, seq)
    y_pool, p = _pool_fwd(proj, sp["w_pool"], sp["pool_scale"], nb, seq, weights_arrived(y_ssd))
    w_out_f, w_up4, w_down4 = weights_later(y_pool)
    w_down_f = w_down4.reshape(D_FF, d)
    h1, o, u2 = _out_proj(y_pool, y_ssd, w_out_f, x2, mod3, sp["g_mlp"], seq)
    a_up = _mlp_up(u2, w_up4)
    d_dn, dh2, sq, gg_final, d_gf = _mlp_down_loss(a_up, w_down_f, h1, mod3, sp["g_final"], tg2, seq)

    gw_down = _tn_matmul(a_up, d_dn, 512, d, "grad_w_down", square_relu=True)
    tok = start_reduce("w_down", gw_down.reshape(N_CHIPS, D_FF // N_CHIPS, d))
    d_a = _mlp_down_bwd(d_dn, w_down4, a_up, tok)
    gw_up4 = _tn_matmul(u2, d_a, 512, d, "grad_w_up", out3=True)
    tok = start_reduce("w_up", gw_up4)
    dh1, d_o, accf, gg_mlp = _mlp_up_bwd(d_a, w_up4, h1, dh2, o, mod3, sp["g_mlp"], seq, tok)
    gw_out = _grad_w_out(y_pool, y_ssd, d_o)
    tok = start_reduce("w_out", gw_out.reshape(N_CHIPS, gw_out.shape[0] // N_CHIPS, d))
    d_ypool, d_yssd = _out_proj_bwd(d_o, w_out_f, tok)
    d_upool, gw_pool, g_ps = _pool_bwd(d_ypool, p, sp["w_pool"], sp["pool_scale"], nb, seq)
    d_z, d_pre, d_udt, gg_ssd, gdsk, ga, gdtb = _ssd_bwd(proj, pre, d_yssd, yssm, h_prev, sp["dt_bias"], sp["a_log"],
                                                        dskip_e, sp["g_ssd"], nb, seq)
    d_uxbc, gconvw, gconvb = _conv_bwd(d_pre, proj, conv_w_full, nb, seq)
    gw_in_t = _grad_w_in_t(d_upool, d_z, d_uxbc, d_udt, u1)
    shard_rows = IN_WIDTH // N_CHIPS
    tok = start_reduce("w_in", jnp.stack([gw_in_t[k * shard_rows:(k + 1) * shard_rows] for k in range(N_CHIPS)]))
    gx, accm, gg_mix = _in_proj_bwd([d_upool, d_z, d_uxbc, d_udt], w_in_t, x2, dh1, mod3, sp["g_mix"], seq, before_last(tok))

    d_mod = jnp.concatenate([accm[:, 0], accm[:, 1], accf[:, 2], accf[:, 0], accf[:, 1], d_gf[:, 0]], 1)
    vec = _pack_vec({"g_mix": gg_mix, "conv_b": gconvb, "g_ssd": gg_ssd, "pool_scale": g_ps, "g_mlp": gg_mlp,
                     "g_final": gg_final, "dt_bias": gdtb, "a_log": ga, "d_skip_lanes": gdsk, "sq_err": sq})
    return gx, d_mod, vec, gw_pool, gconvw


def kernel(x, c, w_ada, b_ada, g_mix, w_in, conv_w, conv_b, dt_bias, a_log, d_skip, g_ssd, w_pool, pool_scale, w_out, g_mlp, w_up, w_down, g_final, loss_target, m_w_ada, m_b_ada, m_g_mix, m_w_in, m_conv_w, m_conv_b, m_dt_bias, m_a_log, m_d_skip, m_g_ssd, m_w_pool, m_pool_scale, m_w_out, m_g_mlp, m_w_up, m_w_down, m_g_final, v_w_ada, v_b_ada, v_g_mix, v_w_in, v_conv_w, v_conv_b, v_dt_bias, v_a_log, v_d_skip, v_g_ssd, v_w_pool, v_pool_scale, v_w_out, v_g_mlp, v_w_up, v_w_down, v_g_final):
    nb, seq, d = x.shape
    t = nb * seq
    xi, yi, ci = _mesh_pos()
    chip = 2 * xi + yi
    me = 4 * xi + 2 * yi + ci
    ada_cols = w_ada.shape[2]
    conv_cols = conv_w.shape[2]
    in_cols = w_in.shape[2]
    w_in_s, m_w_in_s, v_w_in_s = w_in[0].T, m_w_in[0].T, v_w_in[0].T

    c_send, c_recv, c_src, c_land, c_token = _exchange_start([c, conv_w[0]], ALL_PEERS, "cond_start")
    w_in_b = w_in_s.astype(BF)
    i_send, i_recv, i_src, i_land, in_token = _ici_start(
        [w_in_b], [jax.ShapeDtypeStruct((N_CHIPS,) + w_in_b.shape, BF)], _gather_sent, _gather_landing, "gather_start_w_in",
        after=c_token)
    c_own, c_got = _exchange_wait(c_send, c_recv, c_src, c_land, ALL_PEERS, in_token, "cond_wait")
    c8, convw8 = [lax.dynamic_update_slice(got, mine[None], (me,) + (0,) * mine.ndim) for got, mine in zip(c_got, c_own)]
    c_all = c8.reshape(N_DEV * nb, d)
    conv_w_full = convw8[0::2].transpose(1, 0, 2).reshape(4, N_CHIPS * conv_cols)
    b_shard = lax.dynamic_slice(b_ada, (0, chip * ada_cols), (1, ada_cols))
    mod_part, c_act = _ada_mod(c_all, w_ada[0], b_shard, in_token)
    mod_rows = mod_part.reshape(N_DEV, nb, ada_cols)
    m_send, m_recv, m_src, m_land, _ = _ici_start(
        [mod_rows], [jax.ShapeDtypeStruct((N_CHIPS, nb, ada_cols), F32)], _mod_sent, _mod_landing, "mod_start", after=mod_part)

    later = [w_out[0].astype(BF), w_up[0].astype(BF), w_down[0].astype(BF)]
    in_shard, in_land = _ici_wait(i_send, i_recv, i_src, i_land, [m_src[0]] + later, _gather_sent, _gather_landing,
                                  "gather_wait_w_in")
    (w_in4,) = _gather_finish(in_land, in_shard)
    w_in_t = w_in4.reshape(N_CHIPS * in_cols, d)
    mod_mine, mod_land = _ici_wait(m_send, m_recv, m_src, m_land, w_in_t, _mod_sent, _mod_landing, "mod_wait")
    mod_own = lax.dynamic_slice(mod_mine[0], (me, 0, 0), (1, nb, ada_cols))
    mod4 = lax.dynamic_update_slice(mod_land[0], mod_own, (chip, 0, 0))
    mod3 = mod4.transpose(1, 0, 2).reshape(nb, N_MOD, d)
    g_send, g_recv, g_src, g_land, first_token = _ici_start(
        later, [jax.ShapeDtypeStruct((N_CHIPS,) + s.shape, BF) for s in later], _gather_sent, _gather_landing, "gather_start",
        after=w_in4)

    def weights_arrived(after):
        shards, lands = _ici_wait(g_send, g_recv, g_src, g_land, after, _gather_sent, _gather_landing, "gather_wait")
        pending["forward"] = _forward_start(lands, "forward_start") + (shards,)
        return pending["forward"][3]

    def weights_later(after):
        f_send, f_recv, f_land, _, shards = pending["forward"]
        lands = _forward_wait(f_send, f_recv, f_land, after, "forward_wait")
        w_out4, w_up4, w_down4 = [lax.dynamic_update_slice(land, shard[None], (chip, 0, 0)) for land, shard in zip(lands, shards)]
        return w_out4.reshape(N_CHIPS * w_out.shape[1], d), w_up4, w_down4

    pending = {}

    def start_reduce(name, grad4):
        pending[name] = _reduce_start(grad4, "reduce_start_" + name)
        return pending[name][4]

    pos = jnp.stack([ci, chip, me]).astype(jnp.int32)
    early = ("w_out", "w_up", "w_down")

    def summed_half(name, after):
        r_send, r_recv, r_src, r_land, _ = pending[name]
        own, recv = _reduce_wait(r_send, r_recv, r_src, r_land, after, "reduce_wait_" + name)
        return _sum_eight(recv, own, pos)

    def before_last(token):
        pending["early_halves"] = _exchange_start([summed_half(n, token) for n in early], SIBLING, "halves_start_early")
        return pending["early_halves"][4]

    sp = dict(g_mix=g_mix, conv_b=conv_b, dt_bias=dt_bias, a_log=a_log, d_skip=d_skip, g_ssd=g_ssd,
              w_pool=w_pool[0], pool_scale=pool_scale, g_mlp=g_mlp, g_final=g_final.reshape(1, d))
    gx, d_mod, vec, gw_pool, gconvw = _local_step(
        x.reshape(t, d), loss_target.reshape(t, d), mod3, seq, w_in_t, first_token, weights_arrived, weights_later, start_reduce,
        before_last, conv_w_full, sp)

    small_parts = [vec, gw_pool.reshape(4 * POOL_GROUP, POOL_GROUP).astype(BF), gconvw, d_mod]
    s_send, s_recv, s_src, s_land, s_token = _exchange_start(small_parts, ALL_PEERS, "small_start")
    h_send, h_recv, h_src, h_land, h_token = _exchange_start([summed_half("w_in", s_token)], SIBLING, "halves_start")
    e_send, e_recv, e_src, e_land, _ = pending["early_halves"]
    e_own, e_got = _exchange_wait(e_send, e_recv, e_src, e_land, SIBLING, h_token, "halves_wait_early")
    res = {}
    for i, (n, w, m, v) in enumerate((("w_out", w_out, m_w_out, v_w_out), ("w_up", w_up, m_w_up, v_w_up),
                                      ("w_down", w_down, m_w_down, v_w_down))):
        g, dl, m2, v2 = _adam_big(e_own[i], e_got[i], w[0], m[0], v[0], pos)
        res[n] = (g[None], dl[None], m2[None], v2[None])

    s_own, s_got = _exchange_wait(s_send, s_recv, s_src, s_land, ALL_PEERS, res["w_down"][1], "small_wait")
    vec8, wpool8, convw8g, dmod8 = [lax.dynamic_update_slice(got, mine[None], (me,) + (0,) * mine.ndim)
                                    for got, mine in zip(s_got, s_own)]
    convw8s = lax.dynamic_slice(convw8g, (0, 0, chip * conv_cols), (N_DEV, 8, conv_cols))
    m_in = dict(b_ada=m_b_ada, g_mix=m_g_mix, conv_w=m_conv_w[0], conv_b=m_conv_b, dt_bias=m_dt_bias, a_log=m_a_log,
                d_skip=m_d_skip, g_ssd=m_g_ssd, w_pool=m_w_pool.reshape(4 * POOL_GROUP, POOL_GROUP), pool_scale=m_pool_scale,
                g_mlp=m_g_mlp, g_final=m_g_final.reshape(1, d))
    v_in = dict(b_ada=v_b_ada, g_mix=v_g_mix, conv_w=v_conv_w[0], conv_b=v_conv_b, dt_bias=v_dt_bias, a_log=v_a_log,
                d_skip=v_d_skip, g_ssd=v_g_ssd, w_pool=v_w_pool.reshape(4 * POOL_GROUP, POOL_GROUP), pool_scale=v_pool_scale,
                g_mlp=v_g_mlp, g_final=v_g_final.reshape(1, d))
    w_small = dict(sp, b_ada=b_ada, conv_w=conv_w[0], w_pool=w_pool.reshape(4 * POOL_GROUP, POOL_GROUP))
    loss_row, small = _small_adam(vec8, wpool8, convw8s, dmod8, {n: (w_small[n], m_in[n], v_in[n]) for n in _SMALL_PARAMS})

    dmod_all = dmod8.reshape(N_DEV * nb, N_CHIPS * ada_cols)
    dmod_cols = lax.dynamic_slice(dmod_all, (0, chip * ada_cols), (N_DEV * nb, ada_cols))
    res.update({n: tuple(r.reshape(w.shape) for r in small[n])
                for n, w in (("b_ada", b_ada), ("g_mix", g_mix), ("conv_w", conv_w), ("conv_b", conv_b), ("dt_bias", dt_bias),
                             ("a_log", a_log), ("d_skip", d_skip), ("g_ssd", g_ssd), ("w_pool", w_pool),
                             ("pool_scale", pool_scale), ("g_mlp", g_mlp), ("g_final", g_final))})
    g_ada, d_ada, m_ada, v_ada = _adam_ada(c_act.T.astype(BF), dmod_cols, w_ada[0], m_w_ada[0], v_w_ada[0])
    res["w_ada"] = (g_ada[None], d_ada[None], m_ada[None], v_ada[None])
    h_own, h_got = _exchange_wait(h_send, h_recv, h_src, h_land, SIBLING, g_ada, "halves_wait")
    rows3 = lambda a: jnp.transpose(a, (2, 0, 1))
    res["w_in"] = tuple(jnp.transpose(r, (1, 2, 0))
                        for r in _adam_rows(h_own[0], h_got[0], rows3(w_in), rows3(m_w_in), rows3(v_w_in), pos))

    loss = loss_row[0, 0]
    return (loss, gx.reshape(nb, seq, d), *[res[n][0] for n in _WEIGHTS], *[res[n][1] for n in _WEIGHTS],
            *[res[n][2] for n in _WEIGHTS], *[res[n][3] for n in _WEIGHTS])
```

```python
import jax
import jax.numpy as jnp
from jax import lax
from jax.experimental import pallas as pl
from jax.experimental.pallas import tpu as pltpu

F32 = jnp.float32
BF = jnp.bfloat16
MESH = pl.DeviceIdType.MESH

EPS = 1e-5
D_MODEL = 1024
POOL_WIDTH = 512
POOL_WINDOWS = (2, 4, 8, 16)
POOL_GROUP = 128
SSD_INNER = 1024
SSD_HEADS = 16
SSD_HEAD_DIM = 64
SSD_STATE = 128
GROUP_W = 512
CHUNK = 128
CONV_CH = 1536
OFF_DT = 3072
IN_WIDTH = 3088
IN_PAD = 3200
D_FF = 4096
N_MOD = 6
N_CHIPS = 4
N_DEV = 8
HALO = 16
CONV_HALO = 8

ADAM_LR = 0.001
ADAM_B1 = 0.9
ADAM_B2 = 0.999
ADAM_EPS = 1e-08
ADAM_WD = 0.01
ADAM_STEP = 10

VMEM_BYTES_V7X = 64 * 1024 * 1024


def _cp(semantics=None, vmem_mb=48, **kw):
    assert vmem_mb * 1024 * 1024 < VMEM_BYTES_V7X
    args = dict(vmem_limit_bytes=vmem_mb * 1024 * 1024, **kw)
    if semantics is not None:
        args["dimension_semantics"] = semantics
    return pltpu.CompilerParams(**args)


def _out(shape, dtype):
    return pltpu.HBM(shape, dtype)


def _pin(*arrays):
    return [pltpu.with_memory_space_constraint(a, pltpu.HBM) for a in arrays]


TOKEN_SHAPE = (8, 128)


def _order_operand(a):
    if a.shape == TOKEN_SHAPE:
        return pl.BlockSpec(memory_space=pltpu.VMEM), a
    return pl.BlockSpec(memory_space=pltpu.HBM), pltpu.with_memory_space_constraint(a, pltpu.HBM)


def _nn(a, b):
    return jnp.dot(a, b, preferred_element_type=F32)


def _nt(a, b):
    return lax.dot_general(a, b, (((1,), (1,)), ((), ())), preferred_element_type=F32)


def _tn(a, b):
    return lax.dot_general(a, b, (((0,), (0,)), ((), ())), preferred_element_type=F32)


def _split3(v):
    hi = v.astype(BF)
    r1 = v - hi.astype(F32)
    mid = r1.astype(BF)
    lo = (r1 - mid.astype(F32)).astype(BF)
    return hi, mid, lo


def _exact_nn(v, m01):
    hi, mid, lo = _split3(v)
    return _nn(hi, m01) + _nn(mid, m01) + _nn(lo, m01)


def _exact_nn_left(m01, v):
    hi, mid, lo = _split3(v)
    return _nn(m01, hi) + _nn(m01, mid) + _nn(m01, lo)


def _exact_nt_left(m01, v):
    hi, mid, lo = _split3(v)
    return _nt(m01, hi) + _nt(m01, mid) + _nt(m01, lo)


def _sigmoid(v):
    return 1.0 / (1.0 + jnp.exp(-v))


def _iota(shape, dim):
    return lax.broadcasted_iota(jnp.int32, shape, dim)


def _head_expand_matrix(heads, width):
    return (_iota((heads, width), 1) // SSD_HEAD_DIM == _iota((heads, width), 0)).astype(BF)


def _head_reduce_matrix(width, heads):
    return (_iota((width, heads), 0) // SSD_HEAD_DIM == _iota((width, heads), 1)).astype(BF)


def _mesh_pos():
    return lax.axis_index("x"), lax.axis_index("y"), lax.axis_index("c")


def _flip(v, bit):
    return v + bit - 2 * bit * v


_HBM = pl.BlockSpec(memory_space=pltpu.HBM)
_SEM = pl.BlockSpec(memory_space=pltpu.SEMAPHORE)
_DATAFLOW = pltpu.SideEffectType.DATAFLOW_SIDE_EFFECTING


def _peer_chip(x, y, j):
    return _flip(x, (j >> 1) & 1), _flip(y, j & 1)


def _all_peers():
    x, y, c = _mesh_pos()
    return [(_flip(x, (k >> 2) & 1), _flip(y, (k >> 1) & 1), _flip(c, k & 1)) for k in range(1, N_DEV)]


_START_IDS = ("sibling", "gather_start_w_in", "mod_start", "gather_start", "reduce_start_w_down", "reduce_start_w_up",
              "reduce_start_w_out", "reduce_start_w_in", "small_start", "cond_start")


def _start_params(name):
    return pltpu.CompilerParams(has_side_effects=_DATAFLOW, collective_id=_START_IDS.index(name))


def _handshake(peers):
    barrier = pltpu.get_barrier_semaphore()
    for peer in peers:
        pl.semaphore_signal(barrier, inc=1, device_id=peer, device_id_type=MESH)
    pl.semaphore_wait(barrier, len(peers))


def _ici_start(srcs, land_shapes, sent, landing, name, after):
    n = len(srcs)

    def body(*refs):
        src_refs, land_refs = refs[:n], refs[n:2 * n]
        send_sems, recv_sems = refs[2 * n + 1], refs[2 * n + 2]
        token = refs[-1]
        x, y, c = _mesh_pos()
        _handshake([(*_peer_chip(x, y, j), c) for j in range(1, N_CHIPS)])
        for j in range(1, N_CHIPS):
            px, py = _peer_chip(x, y, j)
            for a in range(n):
                pltpu.make_async_remote_copy(
                    src_ref=sent(src_refs[a], c, 2 * px + py), dst_ref=landing(land_refs[a], c, 2 * x + y),
                    send_sem=send_sems.at[a * (N_CHIPS - 1) + j - 1], recv_sem=recv_sems.at[a * (N_CHIPS - 1) + j - 1],
                    device_id=(px, py, c), device_id_type=MESH).start()
        token[...] = jnp.zeros_like(token)

    sems = pltpu.SemaphoreType.DMA((n * (N_CHIPS - 1),))
    after_spec, after = _order_operand(after)
    lands = [pltpu.with_memory_space_constraint(lax.empty(s.shape, s.dtype), pltpu.HBM) for s in land_shapes]
    outs = pl.pallas_call(
        body, name=name,
        out_shape=(sems, sems, *[pltpu.HBM(s.shape, s.dtype) for s in srcs],
                   *[pltpu.HBM(s.shape, s.dtype) for s in land_shapes], jax.ShapeDtypeStruct((8, 128), F32)),
        in_specs=[_HBM] * (2 * n) + [after_spec],
        out_specs=[_SEM, _SEM] + [_HBM] * (2 * n) + [pl.BlockSpec(memory_space=pltpu.VMEM)],
        input_output_aliases={i: 2 + i for i in range(2 * n)},
        compiler_params=_start_params(name),
    )(*_pin(*srcs), *lands, after)
    return outs[0], outs[1], outs[2:2 + n], outs[2 + n:2 + 2 * n], outs[-1]


def _ici_wait(send_sems, recv_sems, src_thru, land_thru, after, sent, landing, name):
    n = len(src_thru)
    afters = [_order_operand(a) for a in (after if isinstance(after, (list, tuple)) else [after])]

    def body(*refs):
        src_refs, land_refs = refs[:n], refs[n:2 * n]
        send_sems, recv_sems = refs[2 * n], refs[2 * n + 1]
        x, y, c = _mesh_pos()
        for j in range(1, N_CHIPS):
            px, py = _peer_chip(x, y, j)
            for a in range(n):
                cp = pltpu.make_async_remote_copy(
                    src_ref=sent(src_refs[a], c, 2 * px + py), dst_ref=landing(land_refs[a], c, 2 * px + py),
                    send_sem=send_sems.at[a * (N_CHIPS - 1) + j - 1], recv_sem=recv_sems.at[a * (N_CHIPS - 1) + j - 1],
                    device_id=(px, py, c), device_id_type=MESH)
                cp.wait_send()
                cp.wait_recv()

    outs = pl.pallas_call(
        body, name=name,
        out_shape=tuple(pltpu.HBM(s.shape, s.dtype) for s in (*src_thru, *land_thru)),
        in_specs=[_HBM] * (2 * n) + [_SEM, _SEM] + [s for s, _ in afters], out_specs=[_HBM] * (2 * n),
        input_output_aliases={i: i for i in range(2 * n)},
        compiler_params=pltpu.CompilerParams(has_side_effects=_DATAFLOW),
    )(*src_thru, *land_thru, send_sems, recv_sems, *[a for _, a in afters])
    return outs[:n], outs[n:]


def _col_half(ref, which, lead=()):
    hc = ref.shape[-1] // 2
    return ref.at[(*lead, slice(None), pl.ds(pl.multiple_of(which * hc, 128), hc))]


def _gather_sent(ref, c, dst_chip):
    return _col_half(ref, c)


def _gather_landing(ref, c, src_chip):
    return _col_half(ref, c, lead=(src_chip,))


def _mod_sent(ref, c, dst_chip):
    return ref.at[2 * dst_chip + c]


def _mod_landing(ref, c, src_chip):
    return ref.at[src_chip]


def _reduce_copy(src_ref, land_ref, send_sems, recv_sems, k, receiving):
    x, y, c = _mesh_pos()
    px, py, pc = _flip(x, (k >> 2) & 1), _flip(y, (k >> 1) & 1), _flip(c, k & 1)
    hc = src_ref.shape[2] // 2
    src = src_ref.at[2 * px + py, :, pl.ds(pl.multiple_of(pc * hc, 128), hc)]
    slot = (4 * px + 2 * py + pc) if receiving else (4 * x + 2 * y + c)
    return pltpu.make_async_remote_copy(
        src_ref=src, dst_ref=land_ref.at[slot], send_sem=send_sems.at[k - 1], recv_sem=recv_sems.at[k - 1],
        device_id=(px, py, pc), device_id_type=MESH)


def _reduce_start(grad4, name):
    k4, r, cols = grad4.shape

    def body(src_ref, land_ref, send_sems, recv_sems, src_thru, land_thru, token):
        _handshake(_all_peers())
        for k in range(1, N_DEV):
            _reduce_copy(src_ref, land_ref, send_sems, recv_sems, k, receiving=False).start()
        token[...] = jnp.zeros_like(token)

    sems = pltpu.SemaphoreType.DMA((N_DEV - 1,))
    land = pltpu.with_memory_space_constraint(lax.empty((N_DEV, r, cols // 2), grad4.dtype), pltpu.HBM)
    return pl.pallas_call(
        body, name=name,
        out_shape=(sems, sems, pltpu.HBM(grad4.shape, grad4.dtype), pltpu.HBM(land.shape, land.dtype),
                   jax.ShapeDtypeStruct((8, 128), F32)),
        in_specs=[_HBM, _HBM], out_specs=[_SEM, _SEM, _HBM, _HBM, pl.BlockSpec(memory_space=pltpu.VMEM)],
        input_output_aliases={0: 2, 1: 3},
        compiler_params=_start_params(name),
    )(*_pin(grad4), land)


def _reduce_wait(send_sems, recv_sems, src_thru, land_thru, after, name):
    def body(src_ref, land_ref, send_sems, recv_sems, after_ref, src_out, land_out):
        for k in range(1, N_DEV):
            cp = _reduce_copy(src_ref, land_ref, send_sems, recv_sems, k, receiving=True)
            cp.wait_send()
            cp.wait_recv()

    return pl.pallas_call(
        body, name=name,
        out_shape=(pltpu.HBM(src_thru.shape, src_thru.dtype), pltpu.HBM(land_thru.shape, land_thru.dtype)),
        in_specs=[_HBM, _HBM, _SEM, _SEM, _order_operand(after)[0]], out_specs=[_HBM, _HBM],
        input_output_aliases={0: 0, 1: 1},
        compiler_params=pltpu.CompilerParams(has_side_effects=_DATAFLOW),
    )(src_thru, land_thru, send_sems, recv_sems, _order_operand(after)[1])


SIBLING_COLLECTIVE_ID = 0
_SIBLING_PARAMS = pltpu.CompilerParams(has_side_effects=_DATAFLOW, collective_id=SIBLING_COLLECTIVE_ID)


def _sibling_handshake():
    x, y, c = _mesh_pos()
    barrier = pltpu.get_barrier_semaphore()
    pl.semaphore_signal(barrier, inc=1, device_id=(x, y, 1 - c), device_id_type=MESH)
    pl.semaphore_wait(barrier, 1)


def _peer_copy(src_ref, land_ref, send_sems, recv_sems, idx, k, receiving):
    x, y, c = _mesh_pos()
    px, py, pc = _flip(x, (k >> 2) & 1), _flip(y, (k >> 1) & 1), _flip(c, k & 1)
    if land_ref.shape[0] == N_DEV:
        slot = (4 * px + 2 * py + pc) if receiving else (4 * x + 2 * y + c)
    else:
        slot = pc if receiving else c
    return pltpu.make_async_remote_copy(
        src_ref=src_ref, dst_ref=land_ref.at[slot], send_sem=send_sems.at[idx], recv_sem=recv_sems.at[idx],
        device_id=(px, py, pc), device_id_type=MESH)


def _exchange_start(arrays, peers, name):
    n = len(arrays)

    def body(*refs):
        src_refs, land_refs = refs[:n], refs[n:2 * n]
        send_sems, recv_sems = refs[2 * n], refs[2 * n + 1]
        token = refs[-1]
        if peers == SIBLING:
            _sibling_handshake()
        else:
            _handshake(_all_peers())
        for j, k in enumerate(peers):
            for a in range(n):
                _peer_copy(src_refs[a], land_refs[a], send_sems, recv_sems, a * len(peers) + j, k, receiving=False).start()
        token[...] = jnp.zeros_like(token)

    sems = pltpu.SemaphoreType.DMA((n * len(peers),))
    n_slots = N_DEV if len(peers) > 1 else 2
    lands = [pltpu.with_memory_space_constraint(lax.empty((n_slots,) + a.shape, a.dtype), pltpu.HBM) for a in arrays]
    outs = pl.pallas_call(
        body, name=name,
        out_shape=(sems, sems, *[pltpu.HBM(a.shape, a.dtype) for a in arrays], *[pltpu.HBM(l.shape, l.dtype) for l in lands],
                   jax.ShapeDtypeStruct((8, 128), F32)),
        in_specs=[_HBM] * (2 * n), out_specs=[_SEM, _SEM] + [_HBM] * (2 * n) + [pl.BlockSpec(memory_space=pltpu.VMEM)],
        input_output_aliases={i: 2 + i for i in range(2 * n)},
        compiler_params=_SIBLING_PARAMS if peers == SIBLING else _start_params(name),
    )(*_pin(*arrays), *lands)
    return outs[0], outs[1], outs[2:2 + n], outs[2 + n:2 + 2 * n], outs[-1]


def _exchange_wait(send_sems, recv_sems, src_thru, land_thru, peers, after, name):
    n = len(src_thru)

    def body(*refs):
        src_refs, land_refs = refs[:n], refs[n:2 * n]
        send_sems, recv_sems = refs[2 * n], refs[2 * n + 1]
        for j, k in enumerate(peers):
            for a in range(n):
                cp = _peer_copy(src_refs[a], land_refs[a], send_sems, recv_sems, a * len(peers) + j, k, receiving=True)
                cp.wait_send()
                cp.wait_recv()

    outs = pl.pallas_call(
        body, name=name,
        out_shape=tuple(pltpu.HBM(s.shape, s.dtype) for s in (*src_thru, *land_thru)),
        in_specs=[_HBM] * (2 * n) + [_SEM, _SEM, _order_operand(after)[0]], out_specs=[_HBM] * (2 * n),
        input_output_aliases={i: i for i in range(2 * n)},
        compiler_params=pltpu.CompilerParams(has_side_effects=_DATAFLOW),
    )(*src_thru, *land_thru, send_sems, recv_sems, _order_operand(after)[1])
    return outs[:n], outs[n:]


ALL_PEERS = tuple(range(1, N_DEV))
SIBLING = (1,)


def _sum_eight(recv, grad4, pos):
    n, r, hc = recv.shape
    steps = 2
    tc = hc // steps

    def body(pos_ref, r_ref, g_ref, o_ref):
        me = pos_ref[2]
        o_ref[...] = jnp.zeros_like(o_ref)
        for s in range(n):
            @pl.when(me == s)
            def _():
                o_ref[...] += g_ref[0].astype(F32)

            @pl.when(me != s)
            def _():
                o_ref[...] += r_ref[s].astype(F32)

    grid_spec = pltpu.PrefetchScalarGridSpec(
        num_scalar_prefetch=1, grid=(steps,),
        in_specs=[pl.BlockSpec((n, r, tc), lambda i, pos: (0, 0, i)),
                  pl.BlockSpec((1, r, tc), lambda i, pos: (pos[1], 0, pos[0] * steps + i))],
        out_specs=pl.BlockSpec((r, tc), lambda i, pos: (0, i)))
    return pl.pallas_call(body, name="sum_eight", grid_spec=grid_spec, out_shape=_out((r, hc), F32),
                          compiler_params=_cp(("parallel",), 32))(pos, *_pin(recv, grad4))


def _forward_copy(land_ref, send_sems, recv_sems, idx, j, receiving):
    x, y, c = _mesh_pos()
    px, py = _peer_chip(x, y, j)
    mine = _col_half(land_ref, c, lead=(2 * px + py,))
    theirs = _col_half(land_ref, 1 - c, lead=(2 * px + py,))
    return pltpu.make_async_remote_copy(
        src_ref=mine, dst_ref=theirs if receiving else mine, send_sem=send_sems.at[idx], recv_sem=recv_sems.at[idx],
        device_id=(x, y, 1 - c), device_id_type=MESH)


def _forward_start(lands, name):
    n = len(lands)

    def body(*refs):
        land_refs, send_sems, recv_sems, token = refs[:n], refs[n], refs[n + 1], refs[-1]
        _sibling_handshake()
        for j in range(1, N_CHIPS):
            for a in range(n):
                _forward_copy(land_refs[a], send_sems, recv_sems, a * (N_CHIPS - 1) + j - 1, j, receiving=False).start()
        token[...] = jnp.zeros_like(token)

    sems = pltpu.SemaphoreType.DMA((n * (N_CHIPS - 1),))
    outs = pl.pallas_call(
        body, name=name,
        out_shape=(sems, sems, *[pltpu.HBM(l.shape, l.dtype) for l in lands], jax.ShapeDtypeStruct((8, 128), F32)),
        in_specs=[_HBM] * n, out_specs=[_SEM, _SEM] + [_HBM] * n + [pl.BlockSpec(memory_space=pltpu.VMEM)],
        input_output_aliases={i: 2 + i for i in range(n)},
        compiler_params=_SIBLING_PARAMS,
    )(*lands)
    return outs[0], outs[1], outs[2:2 + n], outs[-1]


def _forward_wait(send_sems, recv_sems, lands_thru, after, name):
    n = len(lands_thru)

    def body(*refs):
        land_refs, send_sems, recv_sems = refs[:n], refs[n], refs[n + 1]
        for j in range(1, N_CHIPS):
            for a in range(n):
                cp = _forward_copy(land_refs[a], send_sems, recv_sems, a * (N_CHIPS - 1) + j - 1, j, receiving=True)
                cp.wait_send()
                cp.wait_recv()

    return pl.pallas_call(
        body, name=name,
        out_shape=tuple(pltpu.HBM(l.shape, l.dtype) for l in lands_thru),
        in_specs=[_HBM] * n + [_SEM, _SEM, _order_operand(after)[0]], out_specs=[_HBM] * n,
        input_output_aliases={i: i for i in range(n)},
        compiler_params=pltpu.CompilerParams(has_side_effects=_DATAFLOW),
    )(*lands_thru, send_sems, recv_sems, _order_operand(after)[1])


def _gather_finish(lands, shards):
    n = len(lands)
    any_spec = _HBM

    def body(*refs):
        shard_refs, out_refs = refs[n:2 * n], refs[2 * n:3 * n]
        send_sems, recv_sems, local_sems = refs[3 * n:]
        x, y, c = _mesh_pos()
        chip = 2 * x + y
        local, sends = [], []
        for a in range(n):
            cp = pltpu.make_async_copy(shard_refs[a], out_refs[a].at[chip], local_sems.at[a])
            cp.start()
            local.append(cp)
        _sibling_handshake()
        for j in range(1, N_CHIPS):
            px, py = _peer_chip(x, y, j)
            for a in range(n):
                landed = _col_half(out_refs[a], c, lead=(2 * px + py,))
                cp = pltpu.make_async_remote_copy(
                    src_ref=landed, dst_ref=landed, send_sem=send_sems.at[a, j], recv_sem=recv_sems.at[a, j],
                    device_id=(x, y, 1 - c), device_id_type=MESH)
                cp.start()
                sends.append(cp)
        for j in range(1, N_CHIPS):
            px, py = _peer_chip(x, y, j)
            for a in range(n):
                other = _col_half(out_refs[a], 1 - c, lead=(2 * px + py,))
                pltpu.make_async_remote_copy(
                    src_ref=other, dst_ref=other, send_sem=send_sems.at[a, j], recv_sem=recv_sems.at[a, j],
                    device_id=(x, y, 1 - c), device_id_type=MESH).wait_recv()
        for cp in sends:
            cp.wait_send()
        for cp in local:
            cp.wait()

    return pl.pallas_call(
        body, name="gather_finish",
        out_shape=[_out(l.shape, l.dtype) for l in lands],
        in_specs=[any_spec] * (2 * n), out_specs=[any_spec] * n,
        input_output_aliases={i: i for i in range(n)},
        scratch_shapes=[pltpu.SemaphoreType.DMA((n, N_CHIPS))] * 2 + [pltpu.SemaphoreType.DMA((n,))],
        compiler_params=_cp(vmem_mb=16, collective_id=SIBLING_COLLECTIVE_ID),
    )(*lands, *shards)


def _adam_math(w, g, m, v):
    m2 = ADAM_B1 * m + (1.0 - ADAM_B1) * g
    v2 = ADAM_B2 * v + (1.0 - ADAM_B2) * (g * g)
    m_hat = m2 / (1.0 - ADAM_B1 ** ADAM_STEP)
    v_hat = v2 / (1.0 - ADAM_B2 ** ADAM_STEP)
    delta = -ADAM_LR * (m_hat / (jnp.sqrt(v_hat) + ADAM_EPS) + ADAM_WD * w)
    return delta, m2, v2


def _adam_big(g_own, g_pair, w, m, v, pos):
    r, c = w.shape
    steps = 8
    tr = r // steps
    assert tr * steps == r and tr % 8 == 0

    def body(pos_ref, go_hbm, gp_hbm, w_hbm, m_hbm, v_hbm, g_hbm, d_hbm, m2_hbm, v2_hbm):
        core = pos_ref[0]

        def rows(go_ref, gp_ref, w_ref, m_ref, v_ref, g_ref, d_ref, m2_ref, v2_ref):
            own, pair = go_ref[...], gp_ref[...]
            g = jnp.concatenate([jnp.where(core == 0, own, pair), jnp.where(core == 1, own, pair)], 1)
            d, m2, v2 = _adam_math(w_ref[...], g, m_ref[...], v_ref[...])
            g_ref[...] = g
            d_ref[...] = d
            m2_ref[...] = m2
            v2_ref[...] = v2

        def spec(cols, **kw):
            return pl.BlockSpec((tr, cols), lambda i: (i, 0), **kw)

        ahead = dict(pipeline_mode=pl.Buffered(3))
        pltpu.emit_pipeline(
            rows, grid=(steps,),
            in_specs=[spec(c // 2, **ahead), spec(c // 2, **ahead), spec(c, **ahead), spec(c, **ahead), spec(c, **ahead)],
            out_specs=[spec(c)] * 4,
        )(go_hbm, gp_hbm.at[1 - core], w_hbm, m_hbm, v_hbm, g_hbm, d_hbm, m2_hbm, v2_hbm)

    sh = _out((r, c), F32)
    return pl.pallas_call(body, name="adam_big", out_shape=[sh] * 4,
                          in_specs=[pl.BlockSpec(memory_space=pltpu.SMEM)] + [_HBM] * 5, out_specs=[_HBM] * 4,
                          compiler_params=_cp(vmem_mb=32))(pos, *_pin(g_own, g_pair, w, m, v))


def _adam_rows(g_own, g_pair, w3, m3, v3, pos):
    r, _, c = w3.shape
    tr = 128

    def body(pos_ref, go_ref, gp_ref, w_ref, m_ref, v_ref, g_ref, d_ref, m2_ref, v2_ref):
        core = pos_ref[0]
        g = jnp.concatenate([jnp.where(core == 0, go_ref[...], gp_ref[0]), jnp.where(core == 1, go_ref[...], gp_ref[0])], 1)
        d, m2, v2 = _adam_math(w_ref[:, 0, :], g, m_ref[:, 0, :], v_ref[:, 0, :])
        g_ref[:, 0, :] = g
        d_ref[:, 0, :] = d
        m2_ref[:, 0, :] = m2
        v2_ref[:, 0, :] = v2

    spec = pl.BlockSpec((tr, 1, c), lambda i, pos: (i, 0, 0))
    grid_spec = pltpu.PrefetchScalarGridSpec(
        num_scalar_prefetch=1, grid=(pl.cdiv(r, tr),),
        in_specs=[pl.BlockSpec((tr, c // 2), lambda i, pos: (i, 0)),
                  pl.BlockSpec((1, tr, c // 2), lambda i, pos: (1 - pos[0], i, 0)), spec, spec, spec],
        out_specs=[spec] * 4)
    sh = _out(w3.shape, F32)
    return pl.pallas_call(body, name="adam_rows", grid_spec=grid_spec, out_shape=[sh] * 4,
                          compiler_params=_cp(("parallel",), 32))(pos, *_pin(g_own, g_pair, w3, m3, v3))


def _adam_ada(c_act_t, dmod_cols, w, m, v):
    r, c = w.shape
    tr = 256
    assert r % tr == 0

    def body(ct_ref, dm_ref, w_ref, m_ref, v_ref, g_ref, d_ref, m2_ref, v2_ref):
        g = _nn(ct_ref[...], dm_ref[...].astype(BF))
        d, m2, v2 = _adam_math(w_ref[...], g, m_ref[...], v_ref[...])
        g_ref[...] = g
        d_ref[...] = d
        m2_ref[...] = m2
        v2_ref[...] = v2

    spec = pl.BlockSpec((tr, c), lambda i: (i, 0))
    sh = _out((r, c), F32)
    return pl.pallas_call(
        body, name="adam_ada", grid=(r // tr,),
        in_specs=[pl.BlockSpec((tr, c_act_t.shape[1]), lambda i: (i, 0)), pl.BlockSpec(dmod_cols.shape, lambda i: (0, 0)),
                  spec, spec, spec],
        out_specs=[spec] * 4, out_shape=[sh] * 4, compiler_params=_cp(("parallel",), 48))(*_pin(c_act_t, dmod_cols, w, m, v))


def _ada_mod(c_all, w_shard, b_shard, token):
    nb, d = c_all.shape
    cols = w_shard.shape[1]
    tc = 512

    def body(c_ref, w_ref, b_ref, tok_ref, mod_ref, act_ref):
        cv = c_ref[...]
        act = cv * _sigmoid(cv)
        act_ref[...] = act
        mod_ref[...] = _nn(act.astype(BF), w_ref[...].astype(BF)) + b_ref[...]

    return pl.pallas_call(
        body, name="ada_mod", grid=(cols // tc,),
        in_specs=[pl.BlockSpec((nb, d), lambda i: (0, 0)), pl.BlockSpec((d, tc), lambda i: (0, i)),
                  pl.BlockSpec((1, tc), lambda i: (0, i)), _token_spec()],
        out_specs=[pl.BlockSpec((nb, tc), lambda i: (0, i)), pl.BlockSpec((nb, d), lambda i: (0, 0))],
        out_shape=[_out((nb, cols), F32), _out((nb, d), F32)],
        compiler_params=_cp(("arbitrary",), 32))(*_pin(c_all, w_shard, b_shard, token))


SUB_ROWS = 256
ROW_TILE = 512


def _sub_rows(tm):
    return [slice(s, s + SUB_ROWS) for s in range(0, tm, SUB_ROWS)] if tm > SUB_ROWS else [slice(0, tm)]


_RESIDENT = pl.BlockSpec(memory_space=pltpu.VMEM)


def _token_spec():
    return pl.BlockSpec((8, 128), lambda *_: (0, 0))


def _in_proj(x, mod3, g_mix, w_in_t, seq, token):
    t, d = x.shape
    tm = min(ROW_TILE, seq)
    tps = seq // tm

    def body(x_ref, mod_ref, g_ref, w_ref, tok_ref, proj_ref, u1_ref):
        for rows in _sub_rows(tm):
            xv = x_ref[rows, :]
            r = lax.rsqrt(jnp.mean(xv * xv, -1, keepdims=True) + EPS)
            u = (xv * r * g_ref[...]) * (1.0 + mod_ref[0, 1:2, :]) + mod_ref[0, 0:1, :]
            ub = u.astype(BF)
            u1_ref[rows, :] = ub
            proj_ref[rows, 0:OFF_DT] = _nt(ub, w_ref[0:OFF_DT, :])
            proj_ref[rows, OFF_DT:IN_PAD] = jnp.zeros((rows.stop - rows.start, IN_PAD - OFF_DT), F32)
            proj_ref[rows, OFF_DT:IN_WIDTH] = _nt(ub, w_ref[OFF_DT:IN_WIDTH, :])

    return pl.pallas_call(
        body, name="in_proj", grid=(t // tm,),
        in_specs=[pl.BlockSpec((tm, d), lambda i: (i, 0)), pl.BlockSpec((1, N_MOD, d), lambda i: (i // tps, 0, 0)),
                  pl.BlockSpec((1, d), lambda i: (0, 0)), _RESIDENT, _token_spec()],
        out_specs=[pl.BlockSpec((tm, IN_PAD), lambda i: (i, 0)), pl.BlockSpec((tm, d), lambda i: (i, 0))],
        out_shape=[_out((t, IN_PAD), F32), _out((t, d), BF)],
        compiler_params=_cp(("parallel",), 40))(*_pin(x, mod3, g_mix), w_in_t, *_pin(token))


def _pool_tile(seq):
    return min(1024, seq)


def _pool_fwd(proj, w_pool, pool_scale, nb, seq, token):
    ts = _pool_tile(seq)
    nt = seq // ts

    def body(u_ref, halo_ref, wp_ref, ps_ref, tok_ref, yp_ref, p_ref):
        i = pl.program_id(1)
        halo = jnp.where(i == 0, 0.0, halo_ref[...])
        u = u_ref[...]
        ext = jnp.concatenate([halo, u], 0)
        tpos = i * ts + _iota((ts, 1), 0)
        for g, w in enumerate(POOL_WINDOWS):
            gs = slice(g * POOL_GROUP, (g + 1) * POOL_GROUP)
            s = ext[:, gs]
            sh = 1
            while sh < w:
                s = s + pltpu.roll(s, sh, 0)
                sh *= 2
            cnt = jnp.minimum(tpos + 1, w).astype(F32)
            pb = (s[HALO:] / cnt - u[:, gs]).astype(BF)
            p_ref[:, gs] = pb
            yp_ref[:, gs] = (_nn(pb, wp_ref[g].astype(BF)) * ps_ref[:, gs]).astype(BF)

    hb = ts // HALO
    return pl.pallas_call(
        body, name="pool_fwd", grid=(nb, nt),
        in_specs=[pl.BlockSpec((ts, POOL_WIDTH), lambda b, i: (b * nt + i, 0)),
                  pl.BlockSpec((HALO, POOL_WIDTH), lambda b, i: (jnp.maximum((b * nt + i) * hb - 1, 0), 0)),
                  pl.BlockSpec((4, POOL_GROUP, POOL_GROUP), lambda b, i: (0, 0, 0)),
                  pl.BlockSpec((1, POOL_WIDTH), lambda b, i: (0, 0)), _token_spec()],
        out_specs=[pl.BlockSpec((ts, POOL_WIDTH), lambda b, i: (b * nt + i, 0))] * 2,
        out_shape=[_out((nb * seq, POOL_WIDTH), BF)] * 2,
        compiler_params=_cp(("parallel", "parallel"), 32))(*_pin(proj, proj, w_pool, pool_scale, token))


def _conv_pre(uxbc, halo, cw, cb, first):
    halo = jnp.where(first, 0.0, halo)
    ext = jnp.concatenate([halo, uxbc], 0)
    pre = cb + uxbc * cw[3:4]
    for k in (2, 1, 0):
        pre = pre + pltpu.roll(ext, 3 - k, 0)[CONV_HALO:] * cw[k:k + 1]
    return pre


def _chunk_terms(pre, udt, dtb, alog):
    sg = _sigmoid(pre)
    xbc = pre * sg
    dtp = udt[:, :SSD_HEADS] + dtb
    dt = jnp.maximum(dtp, 0.0) + jnp.log(1.0 + jnp.exp(-jnp.abs(dtp)))
    a = -jnp.exp(alog)
    da = dt * a
    tril = (_iota((CHUNK, CHUNK), 0) >= _iota((CHUNK, CHUNK), 1))
    acum = _exact_nn_left(tril.astype(BF), da)
    eye = (_iota((SSD_HEADS, SSD_HEADS), 0) == _iota((SSD_HEADS, SSD_HEADS), 1)).astype(BF)
    acum_t = _exact_nt_left(eye, acum)
    expand = _head_expand_matrix(SSD_HEADS, SSD_INNER)
    acum_e = _exact_nn(acum, expand)
    dt_e = _exact_nn(dt, expand)
    last_e = acum_e[CHUNK - 1:CHUNK]
    return dict(pre=pre, sg=sg, xbc=xbc, dtp=dtp, dt=dt, a=a, acum=acum, acum_t=acum_t, tril=tril,
                dt_e=dt_e, e_a=jnp.exp(acum_e), d_out=jnp.exp(last_e - acum_e), c_dec=jnp.exp(last_e))


def _head_decay(r, h):
    seg = r["acum"][:, h:h + 1] - r["acum_t"][h:h + 1, :]
    return jnp.where(r["tril"], jnp.exp(jnp.minimum(seg, 0.0)), 0.0)


SSD_SUB = 4
SSD_ROWS = SSD_SUB * CHUNK


def _ssd_specs(nb, seq, reverse):
    ns = seq // SSD_ROWS
    per = seq // CONV_HALO

    def cidx(c):
        return (ns - 1 - c) if reverse else c

    def row(b, c):
        return b * ns + cidx(c)

    specs = [
        pl.BlockSpec((SSD_ROWS, CONV_CH), lambda b, c: (row(b, c), 1)),
        pl.BlockSpec((CONV_HALO, CONV_CH),
                     lambda b, c: (jnp.maximum(b * per + cidx(c) * (SSD_ROWS // CONV_HALO) - 1, 0), 1)),
        pl.BlockSpec((SSD_ROWS, GROUP_W), lambda b, c: (row(b, c), 1)),
        pl.BlockSpec((SSD_ROWS, GROUP_W), lambda b, c: (row(b, c), 2)),
        pl.BlockSpec((SSD_ROWS, 128), lambda b, c: (row(b, c), OFF_DT // 128)),
    ]
    return specs, row, cidx, ns


def _const_spec(shape):
    return pl.BlockSpec(shape, lambda b, c: (0,) * len(shape))


def _ssd_fwd(proj, conv_w, conv_b, dt_bias, a_log, dskip_e, g_ssd, nb, seq):
    specs, row, cidx, ns = _ssd_specs(nb, seq, reverse=False)

    def body(uxbc_ref, halo_ref, z0_ref, z1_ref, udt_ref, cw_ref, cb_ref, dtb_ref, alog_ref, dsk_ref, gs_ref,
             yssd_ref, yssm_ref, hprev_ref, pre_ref, h_ref, yd_ref):
        c = pl.program_id(1)

        @pl.when(c == 0)
        def _():
            h_ref[...] = jnp.zeros_like(h_ref)

        for sub in range(SSD_SUB):
            rows = slice(sub * CHUNK, (sub + 1) * CHUNK)
            if sub == 0:
                halo, first = halo_ref[...], c == 0
            else:
                halo, first = uxbc_ref[sub * CHUNK - CONV_HALO:sub * CHUNK, :], False
            pre = _conv_pre(uxbc_ref[rows, :], halo, cw_ref[...], cb_ref[...], first)
            pre_ref[rows, :] = pre
            r = _chunk_terms(pre, udt_ref[rows, :], dtb_ref[...], alog_ref[...])
            xbc = r["xbc"]
            xs = xbc[:, :SSD_INNER]
            xdt = xs * r["dt_e"]
            xdt_b = xdt.astype(BF)
            xdo_b = (xdt * r["d_out"]).astype(BF)
            hprev_ref[0, sub] = h_ref[...]
            for g in range(2):
                gs = slice(g * GROUP_W, (g + 1) * GROUP_W)
                bg = xbc[:, SSD_INNER + g * SSD_STATE:SSD_INNER + (g + 1) * SSD_STATE].astype(BF)
                cg = xbc[:, SSD_INNER + (2 + g) * SSD_STATE:SSD_INNER + (3 + g) * SSD_STATE].astype(BF)
                scores = _nt(cg, bg)
                hg = h_ref[g]
                y_off = _nn(cg, hg.astype(BF)) * r["e_a"][:, gs]
                for hh in range(8):
                    h = g * 8 + hh
                    hs = slice(h * SSD_HEAD_DIM, (h + 1) * SSD_HEAD_DIM)
                    m = (scores * _head_decay(r, h)).astype(BF)
                    yd_ref[sub, :, hs] = _nn(m, xdt_b[:, hs])
                h_ref[g] = hg * r["c_dec"][:, gs] + _tn(bg, xdo_b[:, gs])
                y = yd_ref[sub, :, gs] + y_off + dsk_ref[:, gs] * xs[:, gs]
                yssm_ref[rows, gs] = y
                zg = (z0_ref if g == 0 else z1_ref)[rows, :]
                yg = y * (zg * _sigmoid(zg))
                rg = lax.rsqrt(jnp.mean(yg * yg, -1, keepdims=True) + EPS)
                yssd_ref[rows, gs] = (yg * rg * gs_ref[:, gs]).astype(BF)

    t = nb * seq
    return pl.pallas_call(
        body, name="ssd_fwd", grid=(nb, ns),
        in_specs=specs + [_const_spec((4, CONV_CH)), _const_spec((1, CONV_CH)), _const_spec((1, SSD_HEADS)),
                          _const_spec((1, SSD_HEADS)), _const_spec((1, SSD_INNER)), _const_spec((1, SSD_INNER))],
        out_specs=[pl.BlockSpec((SSD_ROWS, SSD_INNER), lambda b, c: (row(b, c), 0)),
                   pl.BlockSpec((SSD_ROWS, SSD_INNER), lambda b, c: (row(b, c), 0)),
                   pl.BlockSpec((1, SSD_SUB, 2, SSD_STATE, GROUP_W), lambda b, c: (b, c, 0, 0, 0)),
                   pl.BlockSpec((SSD_ROWS, CONV_CH), lambda b, c: (row(b, c), 0))],
        out_shape=[_out((t, SSD_INNER), BF), _out((t, SSD_INNER), F32),
                   _out((nb, seq // CHUNK, 2, SSD_STATE, GROUP_W), F32), _out((t, CONV_CH), F32)],
        scratch_shapes=[pltpu.VMEM((2, SSD_STATE, GROUP_W), F32), pltpu.VMEM((SSD_SUB, CHUNK, SSD_INNER), F32)],
        compiler_params=_cp(("arbitrary", "arbitrary"), 56),
    )(*_pin(proj, proj, proj, proj, proj, conv_w, conv_b, dt_bias, a_log, dskip_e, g_ssd))


def _out_proj(y_pool, y_ssd, w_out, x, mod3, g_mlp, seq):
    t, d = x.shape
    tm = 512
    tps = seq // tm if seq >= tm else 1
    tm = min(tm, seq)

    def body(yp_ref, ys_ref, w_ref, x_ref, mod_ref, g_ref, h1_ref, o_ref, u2_ref):
        for rows in _sub_rows(tm):
            o = _nn(jnp.concatenate([yp_ref[rows, :], ys_ref[rows, :]], 1), w_ref[...])
            o_ref[rows, :] = o.astype(BF)
            h1 = x_ref[rows, :] + mod_ref[0, 2:3, :] * o
            h1_ref[rows, :] = h1
            r = lax.rsqrt(jnp.mean(h1 * h1, -1, keepdims=True) + EPS)
            u2_ref[rows, :] = ((h1 * r * g_ref[...]) * (1.0 + mod_ref[0, 4:5, :]) + mod_ref[0, 3:4, :]).astype(BF)

    row = lambda i: (i, 0)
    return pl.pallas_call(
        body, name="out_proj", grid=(t // tm,),
        in_specs=[pl.BlockSpec((tm, POOL_WIDTH), row), pl.BlockSpec((tm, SSD_INNER), row),
                  _RESIDENT, pl.BlockSpec((tm, d), row),
                  pl.BlockSpec((1, N_MOD, d), lambda i: (i // tps, 0, 0)), pl.BlockSpec((1, d), lambda i: (0, 0))],
        out_specs=[pl.BlockSpec((tm, d), row)] * 3,
        out_shape=[_out((t, d), F32), _out((t, d), BF), _out((t, d), BF)],
        compiler_params=_cp(("parallel",), 48))(*_pin(y_pool, y_ssd), w_out, *_pin(x, mod3, g_mlp))


def _mlp_up(u2, w_up4):
    t, d = u2.shape
    tm = min(1024, t)
    nk, _, cols = w_up4.shape

    def body(u_ref, w_ref, a_ref):
        a_ref[...] = _nn(u_ref[...], w_ref[pl.program_id(1)]).astype(BF)

    return pl.pallas_call(
        body, name="mlp_up", grid=(t // tm, nk),
        in_specs=[pl.BlockSpec((tm, d), lambda i, k: (i, 0)), _RESIDENT],
        out_specs=pl.BlockSpec((tm, cols), lambda i, k: (i, k)),
        out_shape=_out((t, nk * cols), BF),
        compiler_params=_cp(("parallel", "parallel"), 32))(*_pin(u2), w_up4)


def _mlp_down_loss(a_up, w_down, h1, mod3, g_final, target, seq):
    t, d = h1.shape
    nb = t // seq
    tm = min(ROW_TILE, seq)
    tps = seq // tm

    def body(a_ref, w_ref, h1_ref, mod_ref, g_ref, tg_ref, ddn_ref, dh2_ref, sq_ref, gg_ref, dgf_ref):
        i = pl.program_id(0)

        @pl.when(i == 0)
        def _():
            sq_ref[...] = jnp.zeros_like(sq_ref)
            gg_ref[...] = jnp.zeros_like(gg_ref)

        @pl.when(i % tps == 0)
        def _():
            dgf_ref[...] = jnp.zeros_like(dgf_ref)

        gate = mod_ref[0, 5:6, :]
        sq = gg = dgf = 0.0
        for rows in _sub_rows(tm):
            f = jnp.square(jnp.maximum(a_ref[rows, :], 0))
            dn = _nn(f, w_ref[...])
            h2 = h1_ref[rows, :] + gate * dn
            r = lax.rsqrt(jnp.mean(h2 * h2, -1, keepdims=True) + EPS)
            hh = h2 * r
            err = hh * g_ref[...] - tg_ref[rows, :]
            dy = err * (1.0 / d)
            dhat = dy * g_ref[...]
            dh2 = r * (dhat - hh * jnp.mean(dhat * hh, -1, keepdims=True))
            dh2_ref[rows, :] = dh2
            ddn_ref[rows, :] = (dh2 * gate).astype(BF)
            sq = sq + jnp.sum(err * err, 0, keepdims=True)
            gg = gg + jnp.sum(dy * hh, 0, keepdims=True)
            dgf = dgf + jnp.sum(dh2 * dn, 0, keepdims=True)
        sq_ref[...] += sq
        gg_ref[...] += gg
        dgf_ref[0] += dgf

    row = lambda i: (i, 0)
    vec = pl.BlockSpec((1, d), lambda i: (0, 0))
    return pl.pallas_call(
        body, name="mlp_down_loss", grid=(t // tm,),
        in_specs=[pl.BlockSpec((tm, D_FF), row), _RESIDENT, pl.BlockSpec((tm, d), row),
                  pl.BlockSpec((1, N_MOD, d), lambda i: (i // tps, 0, 0)), vec, pl.BlockSpec((tm, d), row)],
        out_specs=[pl.BlockSpec((tm, d), row), pl.BlockSpec((tm, d), row), vec, vec,
                   pl.BlockSpec((1, 1, d), lambda i: (i // tps, 0, 0))],
        out_shape=[_out((t, d), BF), _out((t, d), F32), _out((1, d), F32),
                   _out((1, d), F32), _out((nb, 1, d), F32)],
        compiler_params=_cp(("arbitrary",), 44))(*_pin(a_up), w_down, *_pin(h1, mod3, g_final, target))


def _tn_matmul(a, b, tk, tn, name, square_relu=False, out3=False):
    t, kdim = a.shape
    ndim = b.shape[1]

    def body(a_ref, b_ref, o_ref):
        av = a_ref[...]
        if square_relu:
            av = jnp.square(jnp.maximum(av, 0))
        res = _tn(av, b_ref[...]).astype(BF)
        if out3:
            o_ref[0] = res
        else:
            o_ref[...] = res

    if out3:
        out_spec = pl.BlockSpec((1, tk, tn), lambda j, i: (j, i, 0))
        out_shape = _out((ndim // tn, kdim, tn), BF)
    else:
        out_spec = pl.BlockSpec((tk, tn), lambda j, i: (i, j))
        out_shape = _out((kdim, ndim), BF)
    return pl.pallas_call(
        body, name=name, grid=(ndim // tn, kdim // tk),
        in_specs=[pl.BlockSpec((t, tk), lambda j, i: (0, i)), pl.BlockSpec((t, tn), lambda j, i: (0, j))],
        out_specs=out_spec, out_shape=out_shape,
        compiler_params=_cp(("parallel", "parallel"), 56))(*_pin(a, b))


def _mlp_down_bwd(d_dn, w_down4, a_up, token):
    t, d = d_dn.shape
    tm = min(1024, t)
    nk, rows, _ = w_down4.shape

    def body(g_ref, w_ref, a_ref, tok_ref, o_ref):
        df = _nt(g_ref[...], w_ref[pl.program_id(1)])
        o_ref[...] = (df * (2.0 * jnp.maximum(a_ref[...], 0).astype(F32))).astype(BF)

    return pl.pallas_call(
        body, name="mlp_down_bwd", grid=(t // tm, nk),
        in_specs=[pl.BlockSpec((tm, d), lambda i, k: (i, 0)), _RESIDENT,
                  pl.BlockSpec((tm, rows), lambda i, k: (i, k)), _token_spec()],
        out_specs=pl.BlockSpec((tm, rows), lambda i, k: (i, k)),
        out_shape=_out((t, nk * rows), BF),
        compiler_params=_cp(("parallel", "parallel"), 32))(*_pin(d_dn), w_down4, *_pin(a_up, token))


def _mlp_up_bwd(d_a, w_up4, h1, dh2, o, mod3, g_mlp, seq, token):
    t, d = h1.shape
    nb = t // seq
    tm = min(ROW_TILE, seq)
    tps = seq // tm
    nk = w_up4.shape[0]
    cols = w_up4.shape[2]

    def body(da_ref, w_ref, h1_ref, dh2_ref, o_ref, mod_ref, g_ref, tok_ref, dh1_ref, do_ref, acc_ref, gg_ref):
        i = pl.program_id(0)

        @pl.when(i == 0)
        def _():
            gg_ref[...] = jnp.zeros_like(gg_ref)

        @pl.when(i % tps == 0)
        def _():
            acc_ref[...] = jnp.zeros_like(acc_ref)

        gg = a_shift = a_scale = a_gate = 0.0
        for rows in _sub_rows(tm):
            du = _nt(da_ref[rows, 0:cols], w_ref[0])
            for k in range(1, nk):
                du = du + _nt(da_ref[rows, k * cols:(k + 1) * cols], w_ref[k])
            h1 = h1_ref[rows, :]
            r = lax.rsqrt(jnp.mean(h1 * h1, -1, keepdims=True) + EPS)
            hh = h1 * r
            n2 = hh * g_ref[...]
            dn2 = du * (1.0 + mod_ref[0, 4:5, :])
            dhat = dn2 * g_ref[...]
            dh1 = dh2_ref[rows, :] + r * (dhat - hh * jnp.mean(dhat * hh, -1, keepdims=True))
            dh1_ref[rows, :] = dh1
            do_ref[rows, :] = (dh1 * mod_ref[0, 2:3, :]).astype(BF)
            gg = gg + jnp.sum(dn2 * hh, 0, keepdims=True)
            a_shift = a_shift + jnp.sum(du, 0, keepdims=True)
            a_scale = a_scale + jnp.sum(du * n2, 0, keepdims=True)
            a_gate = a_gate + jnp.sum(dh1 * o_ref[rows, :].astype(F32), 0, keepdims=True)
        gg_ref[...] += gg
        acc_ref[0, 0:1, :] += a_shift
        acc_ref[0, 1:2, :] += a_scale
        acc_ref[0, 2:3, :] += a_gate

    row = lambda i: (i, 0)
    vec = pl.BlockSpec((1, d), lambda i: (0, 0))
    return pl.pallas_call(
        body, name="mlp_up_bwd", grid=(t // tm,),
        in_specs=[pl.BlockSpec((tm, D_FF), row), _RESIDENT, pl.BlockSpec((tm, d), row),
                  pl.BlockSpec((tm, d), row), pl.BlockSpec((tm, d), row),
                  pl.BlockSpec((1, N_MOD, d), lambda i: (i // tps, 0, 0)), vec, _token_spec()],
        out_specs=[pl.BlockSpec((tm, d), row), pl.BlockSpec((tm, d), row),
                   pl.BlockSpec((1, 8, d), lambda i: (i // tps, 0, 0)), vec],
        out_shape=[_out((t, d), F32), _out((t, d), BF),
                   _out((nb, 8, d), F32), _out((1, d), F32)],
        compiler_params=_cp(("arbitrary",), 44))(*_pin(d_a), w_up4, *_pin(h1, dh2, o, mod3, g_mlp, token))


def _out_proj_bwd(d_o, w_out, token):
    t, d = d_o.shape
    tm = min(1024, t)

    def body(g_ref, w_ref, tok_ref, dp_ref, ds_ref):
        gv = g_ref[...]
        dp_ref[...] = _nt(gv, w_ref[0:POOL_WIDTH, :])
        ds_ref[...] = _nt(gv, w_ref[POOL_WIDTH:, :])

    row = lambda i: (i, 0)
    return pl.pallas_call(
        body, name="out_proj_bwd", grid=(t // tm,),
        in_specs=[pl.BlockSpec((tm, d), row), _RESIDENT, _token_spec()],
        out_specs=[pl.BlockSpec((tm, POOL_WIDTH), row), pl.BlockSpec((tm, SSD_INNER), row)],
        out_shape=[_out((t, POOL_WIDTH), F32), _out((t, SSD_INNER), F32)],
        compiler_params=_cp(("parallel",), 32))(*_pin(d_o), w_out, *_pin(token))


def _pool_bwd(d_ypool, p, w_pool, pool_scale, nb, seq):
    ts = _pool_tile(seq)
    nt = seq // ts
    hb = ts // HALO
    last_block = nb * seq // HALO - 1

    def body(dy_ref, halo_ref, p_ref, wp_ref, ps_ref, du_ref, gw_ref, gs_ref):
        b = pl.program_id(0)
        i = pl.program_id(1)

        @pl.when((b == 0) & (i == 0))
        def _():
            gw_ref[...] = jnp.zeros_like(gw_ref)
            gs_ref[...] = jnp.zeros_like(gs_ref)

        halo = jnp.where(i == nt - 1, 0.0, halo_ref[...])
        dy = dy_ref[...]
        ext = jnp.concatenate([dy, halo], 0)
        tpos = i * ts + _iota((ts + HALO, 1), 0)
        n_ext = ts + HALO
        for g, w in enumerate(POOL_WINDOWS):
            gs = slice(g * POOL_GROUP, (g + 1) * POOL_GROUP)
            wg = wp_ref[g].astype(BF)
            pg = p_ref[:, gs]
            pw = _nn(pg, wg)
            gs_ref[:, gs] += jnp.sum(dy[:, gs] * pw, 0, keepdims=True)
            dpw = (ext[:, gs] * ps_ref[:, gs]).astype(BF)
            gw_ref[g] += _tn(pg, dpw[:ts])
            dp = _nt(dpw, wg)
            cnt = jnp.minimum(tpos + 1, w).astype(F32)
            s = dp / cnt
            sh = 1
            while sh < w:
                s = s + pltpu.roll(s, n_ext - sh, 0)
                sh *= 2
            du_ref[:, gs] = (s[:ts] - dp[:ts]).astype(BF)

    return pl.pallas_call(
        body, name="pool_bwd", grid=(nb, nt),
        in_specs=[pl.BlockSpec((ts, POOL_WIDTH), lambda b, i: (b * nt + i, 0)),
                  pl.BlockSpec((HALO, POOL_WIDTH), lambda b, i: (jnp.minimum((b * nt + i + 1) * hb, last_block), 0)),
                  pl.BlockSpec((ts, POOL_WIDTH), lambda b, i: (b * nt + i, 0)),
                  pl.BlockSpec((4, POOL_GROUP, POOL_GROUP), lambda b, i: (0, 0, 0)),
                  pl.BlockSpec((1, POOL_WIDTH), lambda b, i: (0, 0))],
        out_specs=[pl.BlockSpec((ts, POOL_WIDTH), lambda b, i: (b * nt + i, 0)),
                   pl.BlockSpec((4, POOL_GROUP, POOL_GROUP), lambda b, i: (0, 0, 0)),
                   pl.BlockSpec((1, POOL_WIDTH), lambda b, i: (0, 0))],
        out_shape=[_out((nb * seq, POOL_WIDTH), BF), _out((4, POOL_GROUP, POOL_GROUP), F32),
                   _out((1, POOL_WIDTH), F32)],
        compiler_params=_cp(("arbitrary", "arbitrary"), 32))(*_pin(d_ypool, d_ypool, p, w_pool, pool_scale))


def _ssd_bwd(proj, pre, d_yssd, yssm, h_prev, dt_bias, a_log, dskip_e, g_ssd, nb, seq):
    specs, row, cidx, ns = _ssd_specs(nb, seq, reverse=True)
    specs = specs[2:]

    def body(z0_ref, z1_ref, udt_ref, pre_ref, dys_ref, yssm_ref, hprev_ref,
             dtb_ref, alog_ref, dsk_ref, gs_ref,
             dz_ref, dpre_ref, dudt_ref, ggs_ref, gdsk_ref, ga_ref, gdtb_ref,
             g_ref, dxdt_ref, dyv_ref):
        b = pl.program_id(0)
        c = pl.program_id(1)

        @pl.when(c == 0)
        def _():
            g_ref[...] = jnp.zeros_like(g_ref)

        @pl.when((b == 0) & (c == 0))
        def _():
            ggs_ref[...] = jnp.zeros_like(ggs_ref)
            gdsk_ref[...] = jnp.zeros_like(gdsk_ref)
            ga_ref[...] = jnp.zeros_like(ga_ref)
            gdtb_ref[...] = jnp.zeros_like(gdtb_ref)

        for sub in reversed(range(SSD_SUB)):
            chunk(sub, z0_ref, z1_ref, udt_ref, pre_ref, dys_ref, yssm_ref, hprev_ref, dtb_ref, alog_ref, dsk_ref, gs_ref,
                  dz_ref, dpre_ref, dudt_ref, ggs_ref, gdsk_ref, ga_ref, gdtb_ref, g_ref, dxdt_ref.at[sub], dyv_ref.at[sub])

    def chunk(sub, z0_ref, z1_ref, udt_ref, pre_ref, dys_ref, yssm_ref, hprev_ref,
              dtb_ref, alog_ref, dsk_ref, gs_ref,
              dz_ref, dpre_ref, dudt_ref, ggs_ref, gdsk_ref, ga_ref, gdtb_ref,
              g_ref, dxdt_ref, dyv_ref):
        rows = slice(sub * CHUNK, (sub + 1) * CHUNK)
        r = _chunk_terms(pre_ref[rows, :], udt_ref[rows, :], dtb_ref[...], alog_ref[...])
        xbc = r["xbc"]
        xs = xbc[:, :SSD_INNER]
        dt_e = r["dt_e"]
        xdt = xs * dt_e
        xdt_b = xdt.astype(BF)
        reduce_m = _head_reduce_matrix(GROUP_W, 8)

        def head_sums(v):
            return _nn(v.astype(BF), reduce_m)

        onehot16 = lambda h: (_iota((1, SSD_HEADS), 1) == h).astype(F32)
        onecol16 = lambda h: (_iota((SSD_HEADS, 1), 0) == h).astype(F32)

        d_acum = jnp.zeros((CHUNK, SSD_HEADS), F32)
        d_acum_t = jnp.zeros((SSD_HEADS, CHUNK), F32)
        d_alast = jnp.zeros((1, SSD_HEADS), F32)
        place8 = lambda g: (_iota((8, SSD_HEADS), 1) == _iota((8, SSD_HEADS), 0) + 8 * g).astype(BF)
        d_b, d_c = [], []
        for g in range(2):
            gs = slice(g * GROUP_W, (g + 1) * GROUP_W)
            zg = (z0_ref if g == 0 else z1_ref)[rows, :]
            sz = _sigmoid(zg)
            silu_z = zg * sz
            ys = yssm_ref[rows, gs]
            yg = ys * silu_z
            rg = lax.rsqrt(jnp.mean(yg * yg, -1, keepdims=True) + EPS)
            yh = yg * rg
            dys = dys_ref[rows, gs]
            ggs_ref[:, gs] += jnp.sum(dys * yh, 0, keepdims=True)
            dyh = dys * gs_ref[:, gs]
            dyg = rg * (dyh - yh * jnp.mean(dyh * yh, -1, keepdims=True))
            dy = dyg * silu_z
            dz_ref[rows, gs] = (dyg * ys * (sz * (1.0 + zg * (1.0 - sz)))).astype(BF)
            gdsk_ref[:, gs] += jnp.sum(dy * xs[:, gs], 0, keepdims=True)
            dyv_ref[:, gs] = dy
            dy_b = dy.astype(BF)

            bg = xbc[:, SSD_INNER + g * SSD_STATE:SSD_INNER + (g + 1) * SSD_STATE].astype(BF)
            cg = xbc[:, SSD_INNER + (2 + g) * SSD_STATE:SSD_INNER + (3 + g) * SSD_STATE].astype(BF)
            scores = _nt(cg, bg)
            hg = hprev_ref[0, sub, g]
            hg_b = hg.astype(BF)
            gg = g_ref[g]
            gg_b = gg.astype(BF)
            e_a = r["e_a"][:, gs]
            d_out = r["d_out"][:, gs]
            c_dec = r["c_dec"][:, gs]
            zc = _nn(cg, hg_b)
            wv = e_a * dy
            wv_b = wv.astype(BF)
            da_g = head_sums(wv * zc)
            dcg = _nt(wv_b, hg_b)
            d_hprev = _tn(cg, wv_b)
            vg = _nn(bg, gg_b)
            dxdt_g = d_out * vg
            dd_out = head_sums(xdt[:, gs] * vg)
            dbg = _nt((xdt[:, gs] * d_out).astype(BF), gg_b)
            dcd = _exact_nn(jnp.sum(gg * hg, 0, keepdims=True), reduce_m)
            d_out8 = jnp.exp(r["acum"][CHUNK - 1:CHUNK, 8 * g:8 * g + 8] - r["acum"][:, 8 * g:8 * g + 8])
            c_dec8 = jnp.exp(r["acum"][CHUNK - 1:CHUNK, 8 * g:8 * g + 8])
            t8 = dd_out * d_out8
            d_alast = d_alast + _exact_nn(jnp.sum(t8, 0, keepdims=True) + dcd * c_dec8, place8(g))
            d_acum = d_acum + _exact_nn(da_g - t8, place8(g))
            dsc = jnp.zeros((CHUNK, CHUNK), F32)
            for hh in range(8):
                h = g * 8 + hh
                hs = slice(h * SSD_HEAD_DIM, (h + 1) * SSD_HEAD_DIM)
                lam = _head_decay(r, h)
                m = scores * lam
                dyh_b = dy_b[:, hh * SSD_HEAD_DIM:(hh + 1) * SSD_HEAD_DIM]
                dm = _nt(dyh_b, xdt_b[:, hs])
                tm_ = dm * m
                d_acum = d_acum + jnp.sum(tm_, 1, keepdims=True) * onehot16(h)
                d_acum_t = d_acum_t + onecol16(h) * jnp.sum(tm_, 0, keepdims=True)
                dsc = dsc + dm * lam
                dxdt_ref[:, hs] = _tn(m.astype(BF), dyh_b) + dxdt_g[:, hh * SSD_HEAD_DIM:(hh + 1) * SSD_HEAD_DIM]
            dsc_b = dsc.astype(BF)
            d_c.append(dcg + _nn(dsc_b, bg))
            d_b.append(dbg + _tn(dsc_b, cg))
            g_ref[g] = d_hprev + c_dec * gg

        eye = (_iota((CHUNK, CHUNK), 0) == _iota((CHUNK, CHUNK), 1)).astype(BF)
        d_acum = d_acum - _exact_nt_left(eye, d_acum_t)
        is_last = (_iota((CHUNK, 1), 0) == CHUNK - 1).astype(F32)
        d_acum = d_acum + is_last * d_alast
        triu = (_iota((CHUNK, CHUNK), 0) <= _iota((CHUNK, CHUNK), 1)).astype(BF)
        d_da = _exact_nn_left(triu, d_acum)
        dt = r["dt"]
        ga_ref[...] += jnp.sum(d_da * dt, 0, keepdims=True)
        dxdt = dxdt_ref[...]
        reduce16 = _head_reduce_matrix(SSD_INNER, SSD_HEADS)
        d_dt = d_da * r["a"] + _nn((dxdt * xs).astype(BF), reduce16)
        d_udt = d_dt * _sigmoid(r["dtp"])
        gdtb_ref[...] += jnp.sum(d_udt, 0, keepdims=True)
        dudt_ref[rows, :] = jnp.zeros((CHUNK, dudt_ref.shape[1]), BF)
        dudt_ref[rows, 0:SSD_HEADS] = d_udt.astype(BF)
        pre, sg = r["pre"], r["sg"]
        dsilu = sg * (1.0 + pre * (1.0 - sg))
        dpre_ref[rows, 0:SSD_INNER] = (dsk_ref[...] * dyv_ref[...] + dxdt * dt_e) * dsilu[:, 0:SSD_INNER]
        for g in range(2):
            bs = slice(SSD_INNER + g * SSD_STATE, SSD_INNER + (g + 1) * SSD_STATE)
            cs = slice(SSD_INNER + (2 + g) * SSD_STATE, SSD_INNER + (3 + g) * SSD_STATE)
            dpre_ref[rows, bs] = d_b[g] * dsilu[:, bs]
            dpre_ref[rows, cs] = d_c[g] * dsilu[:, cs]

    t = nb * seq
    vec = _const_spec((1, SSD_INNER))
    small = _const_spec((1, SSD_HEADS))
    return pl.pallas_call(
        body, name="ssd_bwd", grid=(nb, ns),
        in_specs=specs + [pl.BlockSpec((SSD_ROWS, CONV_CH), lambda b, c: (row(b, c), 0)),
                          pl.BlockSpec((SSD_ROWS, SSD_INNER), lambda b, c: (row(b, c), 0)),
                          pl.BlockSpec((SSD_ROWS, SSD_INNER), lambda b, c: (row(b, c), 0)),
                          pl.BlockSpec((1, SSD_SUB, 2, SSD_STATE, GROUP_W), lambda b, c: (b, cidx(c), 0, 0, 0)),
                          small, small, vec, vec],
        out_specs=[pl.BlockSpec((SSD_ROWS, SSD_INNER), lambda b, c: (row(b, c), 0)),
                   pl.BlockSpec((SSD_ROWS, CONV_CH), lambda b, c: (row(b, c), 0)),
                   pl.BlockSpec((SSD_ROWS, 128), lambda b, c: (row(b, c), 0)),
                   vec, vec, small, small],
        out_shape=[_out((t, SSD_INNER), BF), _out((t, CONV_CH), F32),
                   _out((t, 128), BF), _out((1, SSD_INNER), F32),
                   _out((1, SSD_INNER), F32), _out((1, SSD_HEADS), F32),
                   _out((1, SSD_HEADS), F32)],
        scratch_shapes=[pltpu.VMEM((2, SSD_STATE, GROUP_W), F32), pltpu.VMEM((SSD_SUB, CHUNK, SSD_INNER), F32),
                        pltpu.VMEM((SSD_SUB, CHUNK, SSD_INNER), F32)],
        compiler_params=_cp(("arbitrary", "arbitrary"), 56),
    )(*_pin(proj, proj, proj, pre, d_yssd, yssm, h_prev, dt_bias, a_log, dskip_e, g_ssd))


def _grad_w_out(y_pool, y_ssd, d_o):
    t, d = d_o.shape
    tk = POOL_WIDTH
    n_s = SSD_INNER // tk

    def body(p_ref, s_ref, g_ref, o_ref):
        i = pl.program_id(0)

        @pl.when(i == 0)
        def _():
            o_ref[...] = _tn(p_ref[...], g_ref[...]).astype(BF)

        @pl.when(i > 0)
        def _():
            o_ref[...] = _tn(s_ref[...], g_ref[...]).astype(BF)

    return pl.pallas_call(
        body, name="grad_w_out", grid=(1 + n_s,),
        in_specs=[pl.BlockSpec((t, tk), lambda i: (0, 0)), pl.BlockSpec((t, tk), lambda i: (0, jnp.maximum(i - 1, 0))),
                  pl.BlockSpec((t, d), lambda i: (0, 0))],
        out_specs=pl.BlockSpec((tk, d), lambda i: (i, 0)),
        out_shape=_out((POOL_WIDTH + SSD_INNER, d), BF),
        compiler_params=_cp(("parallel",), 56))(*_pin(y_pool, y_ssd, d_o))


def _grad_w_in_t(d_upool, d_z, d_uxbc, d_udt, u1):
    t, d = u1.shape
    tk = 512
    n_z, n_x = SSD_INNER // tk, CONV_CH // tk

    def body(p_ref, z_ref, x_ref, dt_ref, u_ref, o_ref):
        i = pl.program_id(0)

        @pl.when(i == 0)
        def _():
            o_ref[...] = _tn(p_ref[...], u_ref[...]).astype(BF)

        @pl.when((i >= 1) & (i < 1 + n_z))
        def _():
            o_ref[...] = _tn(z_ref[...], u_ref[...]).astype(BF)

        @pl.when((i >= 1 + n_z) & (i < 1 + n_z + n_x))
        def _():
            o_ref[...] = _tn(x_ref[...], u_ref[...]).astype(BF)

        @pl.when(i == 1 + n_z + n_x)
        def _():
            o_ref[0:128, :] = _tn(dt_ref[...], u_ref[...]).astype(BF)

    return pl.pallas_call(
        body, name="grad_w_in", grid=(2 + n_z + n_x,),
        in_specs=[pl.BlockSpec((t, tk), lambda i: (0, 0)),
                  pl.BlockSpec((t, tk), lambda i: (0, jnp.clip(i - 1, 0, n_z - 1))),
                  pl.BlockSpec((t, tk), lambda i: (0, jnp.clip(i - 1 - n_z, 0, n_x - 1))),
                  pl.BlockSpec((t, 128), lambda i: (0, 0)), pl.BlockSpec((t, d), lambda i: (0, 0))],
        out_specs=pl.BlockSpec((tk, d), lambda i: (i, 0)),
        out_shape=_out((IN_PAD, d), BF),
        compiler_params=_cp(("parallel",), 56))(*_pin(d_upool, d_z, d_uxbc, d_udt, u1))


def _conv_bwd(d_pre, proj, conv_w, nb, seq):
    ts = min(256, seq)
    nt = seq // ts
    hb = ts // CONV_HALO
    last_block = nb * seq // CONV_HALO - 1
    n_ext = CHUNK + CONV_HALO

    def body(dp_ref, dnext_ref, u_ref, cw_ref, du_ref, gw_ref, gb_ref):
        b = pl.program_id(0)
        i = pl.program_id(1)

        @pl.when((b == 0) & (i == 0))
        def _():
            gw_ref[...] = jnp.zeros_like(gw_ref)
            gb_ref[...] = jnp.zeros_like(gb_ref)

        for c0 in range(0, CONV_CH, 128):
            cs = slice(c0, c0 + 128)
            cw = cw_ref[:, cs]
            gw = [0.0] * 4
            gb = 0.0
            for r0 in range(0, ts, CHUNK):
                dp = dp_ref[r0:r0 + CHUNK, cs]
                u = u_ref[r0:r0 + CHUNK, cs]
                if r0 + CHUNK < ts:
                    below = dp_ref[r0 + CHUNK:r0 + CHUNK + CONV_HALO, cs]
                else:
                    below = jnp.where(i == nt - 1, 0.0, dnext_ref[:, cs])
                ext_d = jnp.concatenate([dp, below], 0)
                du = dp * cw[3:4]
                gw[3] = gw[3] + jnp.sum(dp * u, 0, keepdims=True)
                for k in (2, 1, 0):
                    shifted = pltpu.roll(ext_d, n_ext - (3 - k), 0)[:CHUNK]
                    du = du + shifted * cw[k:k + 1]
                    gw[k] = gw[k] + jnp.sum(shifted * u, 0, keepdims=True)
                gb = gb + jnp.sum(dp, 0, keepdims=True)
                du_ref[r0:r0 + CHUNK, cs] = du.astype(BF)
            for k in range(4):
                gw_ref[k:k + 1, cs] += gw[k]
            gb_ref[:, cs] += gb

    return pl.pallas_call(
        body, name="conv_bwd", grid=(nb, nt),
        in_specs=[pl.BlockSpec((ts, CONV_CH), lambda b, i: (b * nt + i, 0)),
                  pl.BlockSpec((CONV_HALO, CONV_CH), lambda b, i: (jnp.minimum((b * nt + i + 1) * hb, last_block), 0)),
                  pl.BlockSpec((ts, CONV_CH), lambda b, i: (b * nt + i, 1)),
                  pl.BlockSpec((4, CONV_CH), lambda b, i: (0, 0))],
        out_specs=[pl.BlockSpec((ts, CONV_CH), lambda b, i: (b * nt + i, 0)),
                   pl.BlockSpec((8, CONV_CH), lambda b, i: (0, 0)), pl.BlockSpec((1, CONV_CH), lambda b, i: (0, 0))],
        out_shape=[_out((nb * seq, CONV_CH), BF), _out((8, CONV_CH), F32),
                   _out((1, CONV_CH), F32)],
        compiler_params=_cp(("arbitrary", "arbitrary"), 48))(*_pin(d_pre, d_pre, proj, conv_w))


def _in_proj_bwd(d_parts, w_in_t, x, dh1, mod3, g_mix, seq, token):
    t, d = x.shape
    nb = t // seq
    tm = min(ROW_TILE, seq)
    tps = seq // tm

    widths = [p.shape[1] for p in d_parts]

    def body(d0_ref, d1_ref, d2_ref, d3_ref, w_ref, x_ref, dh1_ref, mod_ref, g_ref, tok_ref, gx_ref, acc_ref, gg_ref):
        i = pl.program_id(0)

        @pl.when(i == 0)
        def _():
            gg_ref[...] = jnp.zeros_like(gg_ref)

        @pl.when(i % tps == 0)
        def _():
            acc_ref[...] = jnp.zeros_like(acc_ref)

        gg = a_shift = a_scale = 0.0
        for rows in _sub_rows(tm):
            d_cat = jnp.concatenate([p_ref[rows, :] for p_ref in (d0_ref, d1_ref, d2_ref)], 1)
            du = _nn(d_cat, w_ref[0:OFF_DT, :]) + _nn(d3_ref[rows, 0:IN_WIDTH - OFF_DT], w_ref[OFF_DT:IN_WIDTH, :])
            xv = x_ref[rows, :]
            r = lax.rsqrt(jnp.mean(xv * xv, -1, keepdims=True) + EPS)
            hh = xv * r
            n1 = hh * g_ref[...]
            dn1 = du * (1.0 + mod_ref[0, 1:2, :])
            dhat = dn1 * g_ref[...]
            gx_ref[rows, :] = dh1_ref[rows, :] + r * (dhat - hh * jnp.mean(dhat * hh, -1, keepdims=True))
            gg = gg + jnp.sum(dn1 * hh, 0, keepdims=True)
            a_shift = a_shift + jnp.sum(du, 0, keepdims=True)
            a_scale = a_scale + jnp.sum(du * n1, 0, keepdims=True)
        gg_ref[...] += gg
        acc_ref[0, 0:1, :] += a_shift
        acc_ref[0, 1:2, :] += a_scale

    row = lambda i: (i, 0)
    vec = pl.BlockSpec((1, d), lambda i: (0, 0))
    return pl.pallas_call(
        body, name="in_proj_bwd", grid=(t // tm,),
        in_specs=[pl.BlockSpec((tm, wd), row) for wd in widths] +
                 [_RESIDENT, pl.BlockSpec((tm, d), row),
                  pl.BlockSpec((tm, d), row), pl.BlockSpec((1, N_MOD, d), lambda i: (i // tps, 0, 0)), vec, _token_spec()],
        out_specs=[pl.BlockSpec((tm, d), row), pl.BlockSpec((1, 8, d), lambda i: (i // tps, 0, 0)), vec],
        out_shape=[_out((t, d), F32), _out((nb, 8, d), F32),
                   _out((1, d), F32)],
        compiler_params=_cp(("arbitrary",), 40))(*_pin(*d_parts), w_in_t, *_pin(x, dh1, mod3, g_mix, token))


_VEC_LAYOUT = (("g_mix", 1024), ("conv_b", 1536), ("g_ssd", 1024), ("pool_scale", 512), ("g_mlp", 1024),
               ("g_final", 1024), ("dt_bias", 128), ("a_log", 128), ("d_skip_lanes", 1024), ("sq_err", 1024))
_VEC_OFFSET = {}
_off = 0
for _name, _n in _VEC_LAYOUT:
    _VEC_OFFSET[_name] = _off
    _off += _n
_SMALL_PARAMS = ("b_ada", "g_mix", "conv_w", "conv_b", "dt_bias", "a_log", "d_skip", "g_ssd", "w_pool", "pool_scale",
                 "g_mlp", "g_final")


def _pack_vec(parts):
    cols = []
    for name, n in _VEC_LAYOUT:
        v = parts[name]
        if v.shape[1] < n:
            v = jnp.pad(v, ((0, 0), (0, n - v.shape[1])))
        cols.append(v)
    return jnp.concatenate(cols, 1)


def _small_adam(vec_all, wpool_all, convw_all, dmod_all, params):
    names = _SMALL_PARAMS
    nin = 4 + 3 * len(names)

    def body(*refs):
        vec_ref, wp_ref, cw_ref, dm_ref = refs[:4]
        prm = {n: refs[4 + 3 * i:7 + 3 * i] for i, n in enumerate(names)}
        loss_ref = refs[nin]
        outs = {n: refs[nin + 1 + 4 * i:nin + 5 + 4 * i] for i, n in enumerate(names)}
        vsum = vec_ref[0]
        for s in range(1, N_DEV):
            vsum = vsum + vec_ref[s]

        def lanes(name, n):
            off = _VEC_OFFSET[name]
            return vsum[:, off:off + n]

        grads = {n: lanes(n, prm[n][0].shape[1]) for n in ("g_mix", "conv_b", "g_ssd", "pool_scale", "g_mlp", "g_final", "dt_bias")}
        grads["a_log"] = lanes("a_log", SSD_HEADS) * (-jnp.exp(prm["a_log"][0][...]))
        per_lane = jnp.broadcast_to(lanes("d_skip_lanes", SSD_INNER), (8, SSD_INNER))
        grads["d_skip"] = _exact_nn(per_lane, _head_reduce_matrix(SSD_INNER, SSD_HEADS))[0:1]
        gwp = wp_ref[0].astype(F32)
        gcw = cw_ref[0]
        gb = jnp.sum(dm_ref[0], 0, keepdims=True)
        for s in range(1, N_DEV):
            gwp = gwp + wp_ref[s].astype(F32)
            gcw = gcw + cw_ref[s]
            gb = gb + jnp.sum(dm_ref[s], 0, keepdims=True)
        grads["w_pool"] = gwp
        grads["conv_w"] = gcw[0:4]
        grads["b_ada"] = gb
        total = jnp.sum(lanes("sq_err", D_MODEL), 1, keepdims=True) * (0.5 / D_MODEL)
        loss_ref[...] = jnp.broadcast_to(total, loss_ref.shape)
        for n in names:
            w_ref, m_ref, v_ref = prm[n]
            g = grads[n]
            d, m2, v2 = _adam_math(w_ref[...], g, m_ref[...], v_ref[...])
            g_ref, d_ref, m2_ref, v2_ref = outs[n]
            g_ref[...] = g
            d_ref[...] = d
            m2_ref[...] = m2
            v2_ref[...] = v2

    flat = [vec_all, wpool_all, convw_all, dmod_all]
    out_shape = [jax.ShapeDtypeStruct((1, 128), F32)]
    for n in names:
        flat += list(params[n])
        out_shape += [jax.ShapeDtypeStruct(params[n][0].shape, F32)] * 4
    vm = pl.BlockSpec(memory_space=pltpu.VMEM)
    res = pl.pallas_call(body, name="small_adam", out_shape=out_shape, in_specs=[vm] * len(flat),
                         out_specs=[vm] * len(out_shape), compiler_params=_cp(vmem_mb=48))(*flat)
    return res[0], {n: res[1 + 4 * i:5 + 4 * i] for i, n in enumerate(names)}


_WEIGHTS = ("w_ada", "b_ada", "g_mix", "w_in", "conv_w", "conv_b", "dt_bias", "a_log", "d_skip", "g_ssd", "w_pool",
            "pool_scale", "w_out", "g_mlp", "w_up", "w_down", "g_final")


def _local_step(x2, tg2, mod3, seq, w_in_t, first_token, weights_arrived, weights_later, start_reduce, before_last,
                conv_w_full, sp):
    t, d = x2.shape
    nb = t // seq
    dskip_e = jnp.repeat(sp["d_skip"], SSD_HEAD_DIM, axis=1)
    proj, u1 = _in_proj(x2, mod3, sp["g_mix"], w_in_t, seq, first_token)
    y_ssd, yssm, h_prev, pre = _ssd_fwd(proj, conv_w_full, sp["conv_b"], sp["dt_bias"], sp["a_log"], dskip_e, sp["g_ssd"], nb, seq)
    y_pool, p = _pool_fwd(proj, sp["w_pool"], sp["pool_scale"], nb, seq, weights_arrived(y_ssd))
    w_out_f, w_up4, w_down4 = weights_later(y_pool)
    w_down_f = w_down4.reshape(D_FF, d)
    h1, o, u2 = _out_proj(y_pool, y_ssd, w_out_f, x2, mod3, sp["g_mlp"], seq)
    a_up = _mlp_up(u2, w_up4)
    d_dn, dh2, sq, gg_final, d_gf = _mlp_down_loss(a_up, w_down_f, h1, mod3, sp["g_final"], tg2, seq)

    gw_down = _tn_matmul(a_up, d_dn, 512, d, "grad_w_down", square_relu=True)
    tok = start_reduce("w_down", gw_down.reshape(N_CHIPS, D_FF // N_CHIPS, d))
    d_a = _mlp_down_bwd(d_dn, w_down4, a_up, tok)
    gw_up4 = _tn_matmul(u2, d_a, 512, d, "grad_w_up", out3=True)
    tok = start_reduce("w_up", gw_up4)
    dh1, d_o, accf, gg_mlp = _mlp_up_bwd(d_a, w_up4, h1, dh2, o, mod3, sp["g_mlp"], seq, tok)
    gw_out = _grad_w_out(y_pool, y_ssd, d_o)
    tok = start_reduce("w_out", gw_out.reshape(N_CHIPS, gw_out.shape[0] // N_CHIPS, d))
    d_ypool, d_yssd = _out_proj_bwd(d_o, w_out_f, tok)
    d_upool, gw_pool, g_ps = _pool_bwd(d_ypool, p, sp["w_pool"], sp["pool_scale"], nb, seq)
    d_z, d_pre, d_udt, gg_ssd, gdsk, ga, gdtb = _ssd_bwd(proj, pre, d_yssd, yssm, h_prev, sp["dt_bias"], sp["a_log"],
                                                        dskip_e, sp["g_ssd"], nb, seq)
    d_uxbc, gconvw, gconvb = _conv_bwd(d_pre, proj, conv_w_full, nb, seq)
    gw_in_t = _grad_w_in_t(d_upool, d_z, d_uxbc, d_udt, u1)
    shard_rows = IN_WIDTH // N_CHIPS
    tok = start_reduce("w_in", jnp.stack([gw_in_t[k * shard_rows:(k + 1) * shard_rows] for k in range(N_CHIPS)]))
    gx, accm, gg_mix = _in_proj_bwd([d_upool, d_z, d_uxbc, d_udt], w_in_t, x2, dh1, mod3, sp["g_mix"], seq, before_last(tok))

    d_mod = jnp.concatenate([accm[:, 0], accm[:, 1], accf[:, 2], accf[:, 0], accf[:, 1], d_gf[:, 0]], 1)
    vec = _pack_vec({"g_mix": gg_mix, "conv_b": gconvb, "g_ssd": gg_ssd, "pool_scale": g_ps, "g_mlp": gg_mlp,
                     "g_final": gg_final, "dt_bias": gdtb, "a_log": ga, "d_skip_lanes": gdsk, "sq_err": sq})
    return gx, d_mod, vec, gw_pool, gconvw


def kernel(x, c, w_ada, b_ada, g_mix, w_in, conv_w, conv_b, dt_bias, a_log, d_skip, g_ssd, w_pool, pool_scale, w_out, g_mlp, w_up, w_down, g_final, loss_target, m_w_ada, m_b_ada, m_g_mix, m_w_in, m_conv_w, m_conv_b, m_dt_bias, m_a_log, m_d_skip, m_g_ssd, m_w_pool, m_pool_scale, m_w_out, m_g_mlp, m_w_up, m_w_down, m_g_final, v_w_ada, v_b_ada, v_g_mix, v_w_in, v_conv_w, v_conv_b, v_dt_bias, v_a_log, v_d_skip, v_g_ssd, v_w_pool, v_pool_scale, v_w_out, v_g_mlp, v_w_up, v_w_down, v_g_final):
    nb, seq, d = x.shape
    t = nb * seq
    xi, yi, ci = _mesh_pos()
    chip = 2 * xi + yi
    me = 4 * xi + 2 * yi + ci
    ada_cols = w_ada.shape[2]
    conv_cols = conv_w.shape[2]
    in_cols = w_in.shape[2]
    w_in_s, m_w_in_s, v_w_in_s = w_in[0].T, m_w_in[0].T, v_w_in[0].T

    c_send, c_recv, c_src, c_land, c_token = _exchange_start([c, conv_w[0]], ALL_PEERS, "cond_start")
    w_in_b = w_in_s.astype(BF)
    i_send, i_recv, i_src, i_land, in_token = _ici_start(
        [w_in_b], [jax.ShapeDtypeStruct((N_CHIPS,) + w_in_b.shape, BF)], _gather_sent, _gather_landing, "gather_start_w_in",
        after=c_token)
    c_own, c_got = _exchange_wait(c_send, c_recv, c_src, c_land, ALL_PEERS, in_token, "cond_wait")
    c8, convw8 = [lax.dynamic_update_slice(got, mine[None], (me,) + (0,) * mine.ndim) for got, mine in zip(c_got, c_own)]
    c_all = c8.reshape(N_DEV * nb, d)
    conv_w_full = convw8[0::2].transpose(1, 0, 2).reshape(4, N_CHIPS * conv_cols)
    b_shard = lax.dynamic_slice(b_ada, (0, chip * ada_cols), (1, ada_cols))
    mod_part, c_act = _ada_mod(c_all, w_ada[0], b_shard, in_token)
    mod_rows = mod_part.reshape(N_DEV, nb, ada_cols)
    m_send, m_recv, m_src, m_land, _ = _ici_start(
        [mod_rows], [jax.ShapeDtypeStruct((N_CHIPS, nb, ada_cols), F32)], _mod_sent, _mod_landing, "mod_start", after=mod_part)

    later = [w_out[0].astype(BF), w_up[0].astype(BF), w_down[0].astype(BF)]
    in_shard, in_land = _ici_wait(i_send, i_recv, i_src, i_land, [m_src[0]] + later, _gather_sent, _gather_landing,
                                  "gather_wait_w_in")
    (w_in4,) = _gather_finish(in_land, in_shard)
    w_in_t = w_in4.reshape(N_CHIPS * in_cols, d)
    mod_mine, mod_land = _ici_wait(m_send, m_recv, m_src, m_land, w_in_t, _mod_sent, _mod_landing, "mod_wait")
    mod_own = lax.dynamic_slice(mod_mine[0], (me, 0, 0), (1, nb, ada_cols))
    mod4 = lax.dynamic_update_slice(mod_land[0], mod_own, (chip, 0, 0))
    mod3 = mod4.transpose(1, 0, 2).reshape(nb, N_MOD, d)
    g_send, g_recv, g_src, g_land, first_token = _ici_start(
        later, [jax.ShapeDtypeStruct((N_CHIPS,) + s.shape, BF) for s in later], _gather_sent, _gather_landing, "gather_start",
        after=w_in4)

    def weights_arrived(after):
        shards, lands = _ici_wait(g_send, g_recv, g_src, g_land, after, _gather_sent, _gather_landing, "gather_wait")
        pending["forward"] = _forward_start(lands, "forward_start") + (shards,)
        return pending["forward"][3]

    def weights_later(after):
        f_send, f_recv, f_land, _, shards = pending["forward"]
        lands = _forward_wait(f_send, f_recv, f_land, after, "forward_wait")
        w_out4, w_up4, w_down4 = [lax.dynamic_update_slice(land, shard[None], (chip, 0, 0)) for land, shard in zip(lands, shards)]
        return w_out4.reshape(N_CHIPS * w_out.shape[1], d), w_up4, w_down4

    pending = {}

    def start_reduce(name, grad4):
        pending[name] = _reduce_start(grad4, "reduce_start_" + name)
        return pending[name][4]

    pos = jnp.stack([ci, chip, me]).astype(jnp.int32)
    early = ("w_out", "w_up", "w_down")

    def summed_half(name, after):
        r_send, r_recv, r_src, r_land, _ = pending[name]
        own, recv = _reduce_wait(r_send, r_recv, r_src, r_land, after, "reduce_wait_" + name)
        return _sum_eight(recv, own, pos)

    def before_last(token):
        pending["early_halves"] = _exchange_start([summed_half(n, token) for n in early], SIBLING, "halves_start_early")
        return pending["early_halves"][4]

    sp = dict(g_mix=g_mix, conv_b=conv_b, dt_bias=dt_bias, a_log=a_log, d_skip=d_skip, g_ssd=g_ssd,
              w_pool=w_pool[0], pool_scale=pool_scale, g_mlp=g_mlp, g_final=g_final.reshape(1, d))
    gx, d_mod, vec, gw_pool, gconvw = _local_step(
        x.reshape(t, d), loss_target.reshape(t, d), mod3, seq, w_in_t, first_token, weights_arrived, weights_later, start_reduce,
        before_last, conv_w_full, sp)

    small_parts = [vec, gw_pool.reshape(4 * POOL_GROUP, POOL_GROUP).astype(BF), gconvw, d_mod]
    s_send, s_recv, s_src, s_land, s_token = _exchange_start(small_parts, ALL_PEERS, "small_start")
    h_send, h_recv, h_src, h_land, h_token = _exchange_start([summed_half("w_in", s_token)], SIBLING, "halves_start")
    e_send, e_recv, e_src, e_land, _ = pending["early_halves"]
    e_own, e_got = _exchange_wait(e_send, e_recv, e_src, e_land, SIBLING, h_token, "halves_wait_early")
    res = {}
    for i, (n, w, m, v) in enumerate((("w_out", w_out, m_w_out, v_w_out), ("w_up", w_up, m_w_up, v_w_up),
                                      ("w_down", w_down, m_w_down, v_w_down))):
        g, dl, m2, v2 = _adam_big(e_own[i], e_got[i], w[0], m[0], v[0], pos)
        res[n] = (g[None], dl[None], m2[None], v2[None])

    s_own, s_got = _exchange_wait(s_send, s_recv, s_src, s_land, ALL_PEERS, res["w_down"][1], "small_wait")
    vec8, wpool8, convw8g, dmod8 = [lax.dynamic_update_slice(got, mine[None], (me,) + (0,) * mine.ndim)
                                    for got, mine in zip(s_got, s_own)]
    convw8s = lax.dynamic_slice(convw8g, (0, 0, chip * conv_cols), (N_DEV, 8, conv_cols))
    m_in = dict(b_ada=m_b_ada, g_mix=m_g_mix, conv_w=m_conv_w[0], conv_b=m_conv_b, dt_bias=m_dt_bias, a_log=m_a_log,
                d_skip=m_d_skip, g_ssd=m_g_ssd, w_pool=m_w_pool.reshape(4 * POOL_GROUP, POOL_GROUP), pool_scale=m_pool_scale,
                g_mlp=m_g_mlp, g_final=m_g_final.reshape(1, d))
    v_in = dict(b_ada=v_b_ada, g_mix=v_g_mix, conv_w=v_conv_w[0], conv_b=v_conv_b, dt_bias=v_dt_bias, a_log=v_a_log,
                d_skip=v_d_skip, g_ssd=v_g_ssd, w_pool=v_w_pool.reshape(4 * POOL_GROUP, POOL_GROUP), pool_scale=v_pool_scale,
                g_mlp=v_g_mlp, g_final=v_g_final.reshape(1, d))
    w_small = dict(sp, b_ada=b_ada, conv_w=conv_w[0], w_pool=w_pool.reshape(4 * POOL_GROUP, POOL_GROUP))
    loss_row, small = _small_adam(vec8, wpool8, convw8s, dmod8, {n: (w_small[n], m_in[n], v_in[n]) for n in _SMALL_PARAMS})

    dmod_all = dmod8.reshape(N_DEV * nb, N_CHIPS * ada_cols)
    dmod_cols = lax.dynamic_slice(dmod_all, (0, chip * ada_cols), (N_DEV * nb, ada_cols))
    res.update({n: tuple(r.reshape(w.shape) for r in small[n])
                for n, w in (("b_ada", b_ada), ("g_mix", g_mix), ("conv_w", conv_w), ("conv_b", conv_b), ("dt_bias", dt_bias),
                             ("a_log", a_log), ("d_skip", d_skip), ("g_ssd", g_ssd), ("w_pool", w_pool),
                             ("pool_scale", pool_scale), ("g_mlp", g_mlp), ("g_final", g_final))})
    g_ada, d_ada, m_ada, v_ada = _adam_ada(c_act.T.astype(BF), dmod_cols, w_ada[0], m_w_ada[0], v_w_ada[0])
    res["w_ada"] = (g_ada[None], d_ada[None], m_ada[None], v_ada[None])
    h_own, h_got = _exchange_wait(h_send, h_recv, h_src, h_land, SIBLING, g_ada, "halves_wait")
    rows3 = lambda a: jnp.transpose(a, (2, 0, 1))
    res["w_in"] = tuple(jnp.transpose(r, (1, 2, 0))
                        for r in _adam_rows(h_own[0], h_got[0], rows3(w_in), rows3(m_w_in), rows3(v_w_in), pos))

    loss = loss_row[0, 0]
    return (loss, gx.reshape(nb, seq, d), *[res[n][0] for n in _WEIGHTS], *[res[n][1] for n in _WEIGHTS],
            *[res[n][2] for n in _WEIGHTS], *[res[n][3] for n in _WEIGHTS])
```

```python
import jax
import jax.numpy as jnp
from jax import lax
from jax.experimental import pallas as pl
from jax.experimental.pallas import tpu as pltpu

F32 = jnp.float32
BF = jnp.bfloat16
MESH = pl.DeviceIdType.MESH

EPS = 1e-5
D_MODEL = 1024
POOL_WIDTH = 512
POOL_WINDOWS = (2, 4, 8, 16)
POOL_GROUP = 128
SSD_INNER = 1024
SSD_HEADS = 16
SSD_HEAD_DIM = 64
SSD_STATE = 128
GROUP_W = 512
CHUNK = 128
CONV_CH = 1536
OFF_DT = 3072
IN_WIDTH = 3088
IN_PAD = 3200
D_FF = 4096
N_MOD = 6
N_CHIPS = 4
N_DEV = 8
HALO = 16
CONV_HALO = 8

ADAM_LR = 0.001
ADAM_B1 = 0.9
ADAM_B2 = 0.999
ADAM_EPS = 1e-08
ADAM_WD = 0.01
ADAM_STEP = 10

VMEM_BYTES_V7X = 64 * 1024 * 1024


def _cp(semantics=None, vmem_mb=48, **kw):
    assert vmem_mb * 1024 * 1024 < VMEM_BYTES_V7X
    args = dict(vmem_limit_bytes=vmem_mb * 1024 * 1024, **kw)
    if semantics is not None:
        args["dimension_semantics"] = semantics
    return pltpu.CompilerParams(**args)


def _out(shape, dtype):
    return pltpu.HBM(shape, dtype)


def _pin(*arrays):
    return [pltpu.with_memory_space_constraint(a, pltpu.HBM) for a in arrays]


TOKEN_SHAPE = (8, 128)


def _order_operand(a):
    if a.shape == TOKEN_SHAPE:
        return pl.BlockSpec(memory_space=pltpu.VMEM), a
    return pl.BlockSpec(memory_space=pltpu.HBM), pltpu.with_memory_space_constraint(a, pltpu.HBM)


def _nn(a, b):
    return jnp.dot(a, b, preferred_element_type=F32)


def _nt(a, b):
    return lax.dot_general(a, b, (((1,), (1,)), ((), ())), preferred_element_type=F32)


def _tn(a, b):
    return lax.dot_general(a, b, (((0,), (0,)), ((), ())), preferred_element_type=F32)


def _split3(v):
    hi = v.astype(BF)
    r1 = v - hi.astype(F32)
    mid = r1.astype(BF)
    lo = (r1 - mid.astype(F32)).astype(BF)
    return hi, mid, lo


def _exact_nn(v, m01):
    hi, mid, lo = _split3(v)
    return _nn(hi, m01) + _nn(mid, m01) + _nn(lo, m01)


def _exact_nn_left(m01, v):
    hi, mid, lo = _split3(v)
    return _nn(m01, hi) + _nn(m01, mid) + _nn(m01, lo)


def _exact_nt_left(m01, v):
    hi, mid, lo = _split3(v)
    return _nt(m01, hi) + _nt(m01, mid) + _nt(m01, lo)


def _sigmoid(v):
    return 1.0 / (1.0 + jnp.exp(-v))


def _iota(shape, dim):
    return lax.broadcasted_iota(jnp.int32, shape, dim)


def _head_expand_matrix(heads, width):
    return (_iota((heads, width), 1) // SSD_HEAD_DIM == _iota((heads, width), 0)).astype(BF)


def _head_reduce_matrix(width, heads):
    return (_iota((width, heads), 0) // SSD_HEAD_DIM == _iota((width, heads), 1)).astype(BF)


def _mesh_pos():
    return lax.axis_index("x"), lax.axis_index("y"), lax.axis_index("c")


def _flip(v, bit):
    return v + bit - 2 * bit * v


_HBM = pl.BlockSpec(memory_space=pltpu.HBM)
_SEM = pl.BlockSpec(memory_space=pltpu.SEMAPHORE)
_DATAFLOW = pltpu.SideEffectType.DATAFLOW_SIDE_EFFECTING


def _peer_chip(x, y, j):
    return _flip(x, (j >> 1) & 1), _flip(y, j & 1)


def _all_peers():
    x, y, c = _mesh_pos()
    return [(_flip(x, (k >> 2) & 1), _flip(y, (k >> 1) & 1), _flip(c, k & 1)) for k in range(1, N_DEV)]


_START_IDS = ("sibling", "gather_start_w_in", "mod_start", "gather_start", "reduce_start_w_down", "reduce_start_w_up",
              "reduce_start_w_out", "reduce_start_w_in", "small_start", "cond_start")


def _start_params(name):
    return pltpu.CompilerParams(has_side_effects=_DATAFLOW, collective_id=_START_IDS.index(name))


def _handshake(peers):
    barrier = pltpu.get_barrier_semaphore()
    for peer in peers:
        pl.semaphore_signal(barrier, inc=1, device_id=peer, device_id_type=MESH)
    pl.semaphore_wait(barrier, len(peers))


def _ici_start(srcs, land_shapes, sent, landing, name, after):
    n = len(srcs)

    def body(*refs):
        src_refs, land_refs = refs[:n], refs[n:2 * n]
        send_sems, recv_sems = refs[2 * n + 1], refs[2 * n + 2]
        token = refs[-1]
        x, y, c = _mesh_pos()
        _handshake([(*_peer_chip(x, y, j), c) for j in range(1, N_CHIPS)])
        for j in range(1, N_CHIPS):
            px, py = _peer_chip(x, y, j)
            for a in range(n):
                pltpu.make_async_remote_copy(
                    src_ref=sent(src_refs[a], c, 2 * px + py), dst_ref=landing(land_refs[a], c, 2 * x + y),
                    send_sem=send_sems.at[a * (N_CHIPS - 1) + j - 1], recv_sem=recv_sems.at[a * (N_CHIPS - 1) + j - 1],
                    device_id=(px, py, c), device_id_type=MESH).start()
        token[...] = jnp.zeros_like(token)

    sems = pltpu.SemaphoreType.DMA((n * (N_CHIPS - 1),))
    after_spec, after = _order_operand(after)
    lands = [pltpu.with_memory_space_constraint(lax.empty(s.shape, s.dtype), pltpu.HBM) for s in land_shapes]
    outs = pl.pallas_call(
        body, name=name,
        out_shape=(sems, sems, *[pltpu.HBM(s.shape, s.dtype) for s in srcs],
                   *[pltpu.HBM(s.shape, s.dtype) for s in land_shapes], jax.ShapeDtypeStruct((8, 128), F32)),
        in_specs=[_HBM] * (2 * n) + [after_spec],
        out_specs=[_SEM, _SEM] + [_HBM] * (2 * n) + [pl.BlockSpec(memory_space=pltpu.VMEM)],
        input_output_aliases={i: 2 + i for i in range(2 * n)},
        compiler_params=_start_params(name),
    )(*_pin(*srcs), *lands, after)
    return outs[0], outs[1], outs[2:2 + n], outs[2 + n:2 + 2 * n], outs[-1]


def _ici_wait(send_sems, recv_sems, src_thru, land_thru, after, sent, landing, name):
    n = len(src_thru)
    afters = [_order_operand(a) for a in (after if isinstance(after, (list, tuple)) else [after])]

    def body(*refs):
        src_refs, land_refs = refs[:n], refs[n:2 * n]
        send_sems, recv_sems = refs[2 * n], refs[2 * n + 1]
        x, y, c = _mesh_pos()
        for j in range(1, N_CHIPS):
            px, py = _peer_chip(x, y, j)
            for a in range(n):
                cp = pltpu.make_async_remote_copy(
                    src_ref=sent(src_refs[a], c, 2 * px + py), dst_ref=landing(land_refs[a], c, 2 * px + py),
                    send_sem=send_sems.at[a * (N_CHIPS - 1) + j - 1], recv_sem=recv_sems.at[a * (N_CHIPS - 1) + j - 1],
                    device_id=(px, py, c), device_id_type=MESH)
                cp.wait_send()
                cp.wait_recv()

    outs = pl.pallas_call(
        body, name=name,
        out_shape=tuple(pltpu.HBM(s.shape, s.dtype) for s in (*src_thru, *land_thru)),
        in_specs=[_HBM] * (2 * n) + [_SEM, _SEM] + [s for s, _ in afters], out_specs=[_HBM] * (2 * n),
        input_output_aliases={i: i for i in range(2 * n)},
        compiler_params=pltpu.CompilerParams(has_side_effects=_DATAFLOW),
    )(*src_thru, *land_thru, send_sems, recv_sems, *[a for _, a in afters])
    return outs[:n], outs[n:]


def _col_half(ref, which, lead=()):
    hc = ref.shape[-1] // 2
    return ref.at[(*lead, slice(None), pl.ds(pl.multiple_of(which * hc, 128), hc))]


def _gather_sent(ref, c, dst_chip):
    return _col_half(ref, c)


def _gather_landing(ref, c, src_chip):
    return _col_half(ref, c, lead=(src_chip,))


def _mod_sent(ref, c, dst_chip):
    return ref.at[2 * dst_chip + c]


def _mod_landing(ref, c, src_chip):
    return ref.at[src_chip]


def _reduce_copy(src_ref, land_ref, send_sems, recv_sems, k, receiving):
    x, y, c = _mesh_pos()
    px, py, pc = _flip(x, (k >> 2) & 1), _flip(y, (k >> 1) & 1), _flip(c, k & 1)
    hc = src_ref.shape[2] // 2
    src = src_ref.at[2 * px + py, :, pl.ds(pl.multiple_of(pc * hc, 128), hc)]
    slot = (4 * px + 2 * py + pc) if receiving else (4 * x + 2 * y + c)
    return pltpu.make_async_remote_copy(
        src_ref=src, dst_ref=land_ref.at[slot], send_sem=send_sems.at[k - 1], recv_sem=recv_sems.at[k - 1],
        device_id=(px, py, pc), device_id_type=MESH)


def _reduce_start(grad4, name):
    k4, r, cols = grad4.shape

    def body(src_ref, land_ref, send_sems, recv_sems, src_thru, land_thru, token):
        _handshake(_all_peers())
        for k in range(1, N_DEV):
            _reduce_copy(src_ref, land_ref, send_sems, recv_sems, k, receiving=False).start()
        token[...] = jnp.zeros_like(token)

    sems = pltpu.SemaphoreType.DMA((N_DEV - 1,))
    land = pltpu.with_memory_space_constraint(lax.empty((N_DEV, r, cols // 2), grad4.dtype), pltpu.HBM)
    return pl.pallas_call(
        body, name=name,
        out_shape=(sems, sems, pltpu.HBM(grad4.shape, grad4.dtype), pltpu.HBM(land.shape, land.dtype),
                   jax.ShapeDtypeStruct((8, 128), F32)),
        in_specs=[_HBM, _HBM], out_specs=[_SEM, _SEM, _HBM, _HBM, pl.BlockSpec(memory_space=pltpu.VMEM)],
        input_output_aliases={0: 2, 1: 3},
        compiler_params=_start_params(name),
    )(*_pin(grad4), land)


def _reduce_wait(send_sems, recv_sems, src_thru, land_thru, after, name):
    def body(src_ref, land_ref, send_sems, recv_sems, after_ref, src_out, land_out):
        for k in range(1, N_DEV):
            cp = _reduce_copy(src_ref, land_ref, send_sems, recv_sems, k, receiving=True)
            cp.wait_send()
            cp.wait_recv()

    return pl.pallas_call(
        body, name=name,
        out_shape=(pltpu.HBM(src_thru.shape, src_thru.dtype), pltpu.HBM(land_thru.shape, land_thru.dtype)),
        in_specs=[_HBM, _HBM, _SEM, _SEM, _order_operand(after)[0]], out_specs=[_HBM, _HBM],
        input_output_aliases={0: 0, 1: 1},
        compiler_params=pltpu.CompilerParams(has_side_effects=_DATAFLOW),
    )(src_thru, land_thru, send_sems, recv_sems, _order_operand(after)[1])


SIBLING_COLLECTIVE_ID = 0
_SIBLING_PARAMS = pltpu.CompilerParams(has_side_effects=_DATAFLOW, collective_id=SIBLING_COLLECTIVE_ID)


def _sibling_handshake():
    x, y, c = _mesh_pos()
    barrier = pltpu.get_barrier_semaphore()
    pl.semaphore_signal(barrier, inc=1, device_id=(x, y, 1 - c), device_id_type=MESH)
    pl.semaphore_wait(barrier, 1)


def _peer_copy(src_ref, land_ref, send_sems, recv_sems, idx, k, receiving):
    x, y, c = _mesh_pos()
    px, py, pc = _flip(x, (k >> 2) & 1), _flip(y, (k >> 1) & 1), _flip(c, k & 1)
    if land_ref.shape[0] == N_DEV:
        slot = (4 * px + 2 * py + pc) if receiving else (4 * x + 2 * y + c)
    else:
        slot = pc if receiving else c
    return pltpu.make_async_remote_copy(
        src_ref=src_ref, dst_ref=land_ref.at[slot], send_sem=send_sems.at[idx], recv_sem=recv_sems.at[idx],
        device_id=(px, py, pc), device_id_type=MESH)


def _exchange_start(arrays, peers, name):
    n = len(arrays)

    def body(*refs):
        src_refs, land_refs = refs[:n], refs[n:2 * n]
        send_sems, recv_sems = refs[2 * n], refs[2 * n + 1]
        token = refs[-1]
        if peers == SIBLING:
            _sibling_handshake()
        else:
            _handshake(_all_peers())
        for j, k in enumerate(peers):
            for a in range(n):
                _peer_copy(src_refs[a], land_refs[a], send_sems, recv_sems, a * len(peers) + j, k, receiving=False).start()
        token[...] = jnp.zeros_like(token)

    sems = pltpu.SemaphoreType.DMA((n * len(peers),))
    n_slots = N_DEV if len(peers) > 1 else 2
    lands = [pltpu.with_memory_space_constraint(lax.empty((n_slots,) + a.shape, a.dtype), pltpu.HBM) for a in arrays]
    outs = pl.pallas_call(
        body, name=name,
        out_shape=(sems, sems, *[pltpu.HBM(a.shape, a.dtype) for a in arrays], *[pltpu.HBM(l.shape, l.dtype) for l in lands],
                   jax.ShapeDtypeStruct((8, 128), F32)),
        in_specs=[_HBM] * (2 * n), out_specs=[_SEM, _SEM] + [_HBM] * (2 * n) + [pl.BlockSpec(memory_space=pltpu.VMEM)],
        input_output_aliases={i: 2 + i for i in range(2 * n)},
        compiler_params=_SIBLING_PARAMS if peers == SIBLING else _start_params(name),
    )(*_pin(*arrays), *lands)
    return outs[0], outs[1], outs[2:2 + n], outs[2 + n:2 + 2 * n], outs[-1]


def _exchange_wait(send_sems, recv_sems, src_thru, land_thru, peers, after, name):
    n = len(src_thru)

    def body(*refs):
        src_refs, land_refs = refs[:n], refs[n:2 * n]
        send_sems, recv_sems = refs[2 * n], refs[2 * n + 1]
        for j, k in enumerate(peers):
            for a in range(n):
                cp = _peer_copy(src_refs[a], land_refs[a], send_sems, recv_sems, a * len(peers) + j, k, receiving=True)
                cp.wait_send()
                cp.wait_recv()

    outs = pl.pallas_call(
        body, name=name,
        out_shape=tuple(pltpu.HBM(s.shape, s.dtype) for s in (*src_thru, *land_thru)),
        in_specs=[_HBM] * (2 * n) + [_SEM, _SEM, _order_operand(after)[0]], out_specs=[_HBM] * (2 * n),
        input_output_aliases={i: i for i in range(2 * n)},
        compiler_params=pltpu.CompilerParams(has_side_effects=_DATAFLOW),
    )(*src_thru, *land_thru, send_sems, recv_sems, _order_operand(after)[1])
    return outs[:n], outs[n:]


ALL_PEERS = tuple(range(1, N_DEV))
SIBLING = (1,)


def _sum_eight(recv, grad4, pos):
    n, r, hc = recv.shape
    steps = 2
    tc = hc // steps

    def body(pos_ref, r_ref, g_ref, o_ref):
        me = pos_ref[2]
        o_ref[...] = jnp.zeros_like(o_ref)
        for s in range(n):
            @pl.when(me == s)
            def _():
                o_ref[...] += g_ref[0].astype(F32)

            @pl.when(me != s)
            def _():
                o_ref[...] += r_ref[s].astype(F32)

    grid_spec = pltpu.PrefetchScalarGridSpec(
        num_scalar_prefetch=1, grid=(steps,),
        in_specs=[pl.BlockSpec((n, r, tc), lambda i, pos: (0, 0, i)),
                  pl.BlockSpec((1, r, tc), lambda i, pos: (pos[1], 0, pos[0] * steps + i))],
        out_specs=pl.BlockSpec((r, tc), lambda i, pos: (0, i)))
    return pl.pallas_call(body, name="sum_eight", grid_spec=grid_spec, out_shape=_out((r, hc), F32),
                          compiler_params=_cp(("parallel",), 32))(pos, *_pin(recv, grad4))


def _forward_copy(land_ref, send_sems, recv_sems, idx, j, receiving):
    x, y, c = _mesh_pos()
    px, py = _peer_chip(x, y, j)
    mine = _col_half(land_ref, c, lead=(2 * px + py,))
    theirs = _col_half(land_ref, 1 - c, lead=(2 * px + py,))
    return pltpu.make_async_remote_copy(
        src_ref=mine, dst_ref=theirs if receiving else mine, send_sem=send_sems.at[idx], recv_sem=recv_sems.at[idx],
        device_id=(x, y, 1 - c), device_id_type=MESH)


def _forward_start(lands, name):
    n = len(lands)

    def body(*refs):
        land_refs, send_sems, recv_sems, token = refs[:n], refs[n], refs[n + 1], refs[-1]
        _sibling_handshake()
        for j in range(1, N_CHIPS):
            for a in range(n):
                _forward_copy(land_refs[a], send_sems, recv_sems, a * (N_CHIPS - 1) + j - 1, j, receiving=False).start()
        token[...] = jnp.zeros_like(token)

    sems = pltpu.SemaphoreType.DMA((n * (N_CHIPS - 1),))
    outs = pl.pallas_call(
        body, name=name,
        out_shape=(sems, sems, *[pltpu.HBM(l.shape, l.dtype) for l in lands], jax.ShapeDtypeStruct((8, 128), F32)),
        in_specs=[_HBM] * n, out_specs=[_SEM, _SEM] + [_HBM] * n + [pl.BlockSpec(memory_space=pltpu.VMEM)],
        input_output_aliases={i: 2 + i for i in range(n)},
        compiler_params=_SIBLING_PARAMS,
    )(*lands)
    return outs[0], outs[1], outs[2:2 + n], outs[-1]


def _forward_wait(send_sems, recv_sems, lands_thru, after, name):
    n = len(lands_thru)

    def body(*refs):
        land_refs, send_sems, recv_sems = refs[:n], refs[n], refs[n + 1]
        for j in range(1, N_CHIPS):
            for a in range(n):
                cp = _forward_copy(land_refs[a], send_sems, recv_sems, a * (N_CHIPS - 1) + j - 1, j, receiving=True)
                cp.wait_send()
                cp.wait_recv()

    return pl.pallas_call(
        body, name=name,
        out_shape=tuple(pltpu.HBM(l.shape, l.dtype) for l in lands_thru),
        in_specs=[_HBM] * n + [_SEM, _SEM, _order_operand(after)[0]], out_specs=[_HBM] * n,
        input_output_aliases={i: i for i in range(n)},
        compiler_params=pltpu.CompilerParams(has_side_effects=_DATAFLOW),
    )(*lands_thru, send_sems, recv_sems, _order_operand(after)[1])


def _gather_finish(lands, shards):
    n = len(lands)
    any_spec = _HBM

    def body(*refs):
        shard_refs, out_refs = refs[n:2 * n], refs[2 * n:3 * n]
        send_sems, recv_sems, local_sems = refs[3 * n:]
        x, y, c = _mesh_pos()
        chip = 2 * x + y
        local, sends = [], []
        for a in range(n):
            cp = pltpu.make_async_copy(shard_refs[a], out_refs[a].at[chip], local_sems.at[a])
            cp.start()
            local.append(cp)
        _sibling_handshake()
        for j in range(1, N_CHIPS):
            px, py = _peer_chip(x, y, j)
            for a in range(n):
                landed = _col_half(out_refs[a], c, lead=(2 * px + py,))
                cp = pltpu.make_async_remote_copy(
                    src_ref=landed, dst_ref=landed, send_sem=send_sems.at[a, j], recv_sem=recv_sems.at[a, j],
                    device_id=(x, y, 1 - c), device_id_type=MESH)
                cp.start()
                sends.append(cp)
        for j in range(1, N_CHIPS):
            px, py = _peer_chip(x, y, j)
            for a in range(n):
                other = _col_half(out_refs[a], 1 - c, lead=(2 * px + py,))
                pltpu.make_async_remote_copy(
                    src_ref=other, dst_ref=other, send_sem=send_sems.at[a, j], recv_sem=recv_sems.at[a, j],
                    device_id=(x, y, 1 - c), device_id_type=MESH).wait_recv()
        for cp in sends:
            cp.wait_send()
        for cp in local:
            cp.wait()

    return pl.pallas_call(
        body, name="gather_finish",
        out_shape=[_out(l.shape, l.dtype) for l in lands],
        in_specs=[any_spec] * (2 * n), out_specs=[any_spec] * n,
        input_output_aliases={i: i for i in range(n)},
        scratch_shapes=[pltpu.SemaphoreType.DMA((n, N_CHIPS))] * 2 + [pltpu.SemaphoreType.DMA((n,))],
        compiler_params=_cp(vmem_mb=16, collective_id=SIBLING_COLLECTIVE_ID),
    )(*lands, *shards)


def _adam_math(w, g, m, v):
    m2 = ADAM_B1 * m + (1.0 - ADAM_B1) * g
    v2 = ADAM_B2 * v + (1.0 - ADAM_B2) * (g * g)
    m_hat = m2 / (1.0 - ADAM_B1 ** ADAM_STEP)
    v_hat = v2 / (1.0 - ADAM_B2 ** ADAM_STEP)
    delta = -ADAM_LR * (m_hat / (jnp.sqrt(v_hat) + ADAM_EPS) + ADAM_WD * w)
    return delta, m2, v2


def _adam_big(g_own, g_pair, w, m, v, pos):
    r, c = w.shape
    steps = 8
    tr = r // steps
    assert tr * steps == r and tr % 8 == 0

    def body(pos_ref, go_hbm, gp_hbm, w_hbm, m_hbm, v_hbm, g_hbm, d_hbm, m2_hbm, v2_hbm):
        core = pos_ref[0]

        def rows(go_ref, gp_ref, w_ref, m_ref, v_ref, g_ref, d_ref, m2_ref, v2_ref):
            own, pair = go_ref[...], gp_ref[...]
            g = jnp.concatenate([jnp.where(core == 0, own, pair), jnp.where(core == 1, own, pair)], 1)
            d, m2, v2 = _adam_math(w_ref[...], g, m_ref[...], v_ref[...])
            g_ref[...] = g
            d_ref[...] = d
            m2_ref[...] = m2
            v2_ref[...] = v2

        def spec(cols, **kw):
            return pl.BlockSpec((tr, cols), lambda i: (i, 0), **kw)

        ahead = dict(pipeline_mode=pl.Buffered(3))
        pltpu.emit_pipeline(
            rows, grid=(steps,),
            in_specs=[spec(c // 2, **ahead), spec(c // 2, **ahead), spec(c, **ahead), spec(c, **ahead), spec(c, **ahead)],
            out_specs=[spec(c)] * 4,
        )(go_hbm, gp_hbm.at[1 - core], w_hbm, m_hbm, v_hbm, g_hbm, d_hbm, m2_hbm, v2_hbm)

    sh = _out((r, c), F32)
    return pl.pallas_call(body, name="adam_big", out_shape=[sh] * 4,
                          in_specs=[pl.BlockSpec(memory_space=pltpu.SMEM)] + [_HBM] * 5, out_specs=[_HBM] * 4,
                          compiler_params=_cp(vmem_mb=32))(pos, *_pin(g_own, g_pair, w, m, v))


def _adam_rows(g_own, g_pair, w3, m3, v3, pos):
    r, _, c = w3.shape
    tr = 128

    def body(pos_ref, go_ref, gp_ref, w_ref, m_ref, v_ref, g_ref, d_ref, m2_ref, v2_ref):
        core = pos_ref[0]
        g = jnp.concatenate([jnp.where(core == 0, go_ref[...], gp_ref[0]), jnp.where(core == 1, go_ref[...], gp_ref[0])], 1)
        d, m2, v2 = _adam_math(w_ref[:, 0, :], g, m_ref[:, 0, :], v_ref[:, 0, :])
        g_ref[:, 0, :] = g
        d_ref[:, 0, :] = d
        m2_ref[:, 0, :] = m2
        v2_ref[:, 0, :] = v2

    spec = pl.BlockSpec((tr, 1, c), lambda i, pos: (i, 0, 0))
    grid_spec = pltpu.PrefetchScalarGridSpec(
        num_scalar_prefetch=1, grid=(pl.cdiv(r, tr),),
        in_specs=[pl.BlockSpec((tr, c // 2), lambda i, pos: (i, 0)),
                  pl.BlockSpec((1, tr, c // 2), lambda i, pos: (1 - pos[0], i, 0)), spec, spec, spec],
        out_specs=[spec] * 4)
    sh = _out(w3.shape, F32)
    return pl.pallas_call(body, name="adam_rows", grid_spec=grid_spec, out_shape=[sh] * 4,
                          compiler_params=_cp(("parallel",), 32))(pos, *_pin(g_own, g_pair, w3, m3, v3))


def _adam_ada(c_act_t, dmod_cols, w, m, v):
    r, c = w.shape
    tr = 128
    assert r % tr == 0

    def rows(ct_ref, dm_ref, w_ref, m_ref, v_ref, g_ref, d_ref, m2_ref, v2_ref):
        g = _nn(ct_ref[...], dm_ref[...].astype(BF))
        d, m2, v2 = _adam_math(w_ref[...], g, m_ref[...], v_ref[...])
        g_ref[...] = g
        d_ref[...] = d
        m2_ref[...] = m2
        v2_ref[...] = v2

    def body(ct_hbm, dm_hbm, w_hbm, m_hbm, v_hbm, g_hbm, d_hbm, m2_hbm, v2_hbm):
        def spec(**kw):
            return pl.BlockSpec((tr, c), lambda i: (i, 0), **kw)

        ahead = dict(pipeline_mode=pl.Buffered(3))
        pltpu.emit_pipeline(
            rows, grid=(r // tr,),
            in_specs=[pl.BlockSpec((tr, c_act_t.shape[1]), lambda i: (i, 0)), pl.BlockSpec(dmod_cols.shape, lambda i: (0, 0)),
                      spec(**ahead), spec(**ahead), spec(**ahead)],
            out_specs=[spec()] * 4,
        )(ct_hbm, dm_hbm, w_hbm, m_hbm, v_hbm, g_hbm, d_hbm, m2_hbm, v2_hbm)

    sh = _out((r, c), F32)
    return pl.pallas_call(
        body, name="adam_ada", in_specs=[_HBM] * 5, out_specs=[_HBM] * 4, out_shape=[sh] * 4,
        compiler_params=_cp(vmem_mb=48))(*_pin(c_act_t, dmod_cols, w, m, v))


def _ada_mod(c_all, w_shard, b_shard, token):
    nb, d = c_all.shape
    cols = w_shard.shape[1]
    tc = 512

    def body(c_ref, w_ref, b_ref, tok_ref, mod_ref, act_ref):
        cv = c_ref[...]
        act = cv * _sigmoid(cv)
        act_ref[...] = act
        mod_ref[...] = _nn(act.astype(BF), w_ref[...].astype(BF)) + b_ref[...]

    return pl.pallas_call(
        body, name="ada_mod", grid=(cols // tc,),
        in_specs=[pl.BlockSpec((nb, d), lambda i: (0, 0)), pl.BlockSpec((d, tc), lambda i: (0, i)),
                  pl.BlockSpec((1, tc), lambda i: (0, i)), _token_spec()],
        out_specs=[pl.BlockSpec((nb, tc), lambda i: (0, i)), pl.BlockSpec((nb, d), lambda i: (0, 0))],
        out_shape=[_out((nb, cols), F32), _out((nb, d), F32)],
        compiler_params=_cp(("arbitrary",), 32))(*_pin(c_all, w_shard, b_shard, token))


SUB_ROWS = 256
ROW_TILE = 512


def _sub_rows(tm):
    return [slice(s, s + SUB_ROWS) for s in range(0, tm, SUB_ROWS)] if tm > SUB_ROWS else [slice(0, tm)]


_RESIDENT = pl.BlockSpec(memory_space=pltpu.VMEM)


def _token_spec():
    return pl.BlockSpec((8, 128), lambda *_: (0, 0))


def _in_proj(x, mod3, g_mix, w_in_t, seq, token):
    t, d = x.shape
    tm = min(ROW_TILE, seq)
    tps = seq // tm

    def body(x_ref, mod_ref, g_ref, w_ref, tok_ref, proj_ref, u1_ref):
        for rows in _sub_rows(tm):
            xv = x_ref[rows, :]
            r = lax.rsqrt(jnp.mean(xv * xv, -1, keepdims=True) + EPS)
            u = (xv * r * g_ref[...]) * (1.0 + mod_ref[0, 1:2, :]) + mod_ref[0, 0:1, :]
            ub = u.astype(BF)
            u1_ref[rows, :] = ub
            proj_ref[rows, 0:OFF_DT] = _nt(ub, w_ref[0:OFF_DT, :])
            proj_ref[rows, OFF_DT:IN_PAD] = jnp.zeros((rows.stop - rows.start, IN_PAD - OFF_DT), F32)
            proj_ref[rows, OFF_DT:IN_WIDTH] = _nt(ub, w_ref[OFF_DT:IN_WIDTH, :])

    return pl.pallas_call(
        body, name="in_proj", grid=(t // tm,),
        in_specs=[pl.BlockSpec((tm, d), lambda i: (i, 0)), pl.BlockSpec((1, N_MOD, d), lambda i: (i // tps, 0, 0)),
                  pl.BlockSpec((1, d), lambda i: (0, 0)), _RESIDENT, _token_spec()],
        out_specs=[pl.BlockSpec((tm, IN_PAD), lambda i: (i, 0)), pl.BlockSpec((tm, d), lambda i: (i, 0))],
        out_shape=[_out((t, IN_PAD), F32), _out((t, d), BF)],
        compiler_params=_cp(("parallel",), 40))(*_pin(x, mod3, g_mix), w_in_t, *_pin(token))


def _pool_tile(seq):
    return min(1024, seq)


def _pool_fwd(proj, w_pool, pool_scale, nb, seq, token):
    ts = _pool_tile(seq)
    nt = seq // ts

    def body(u_ref, halo_ref, wp_ref, ps_ref, tok_ref, yp_ref, p_ref):
        i = pl.program_id(1)
        halo = jnp.where(i == 0, 0.0, halo_ref[...])
        u = u_ref[...]
        ext = jnp.concatenate([halo, u], 0)
        tpos = i * ts + _iota((ts, 1), 0)
        for g, w in enumerate(POOL_WINDOWS):
            gs = slice(g * POOL_GROUP, (g + 1) * POOL_GROUP)
            s = ext[:, gs]
            sh = 1
            while sh < w:
                s = s + pltpu.roll(s, sh, 0)
                sh *= 2
            cnt = jnp.minimum(tpos + 1, w).astype(F32)
            pb = (s[HALO:] / cnt - u[:, gs]).astype(BF)
            p_ref[:, gs] = pb
            yp_ref[:, gs] = (_nn(pb, wp_ref[g].astype(BF)) * ps_ref[:, gs]).astype(BF)

    hb = ts // HALO
    return pl.pallas_call(
        body, name="pool_fwd", grid=(nb, nt),
        in_specs=[pl.BlockSpec((ts, POOL_WIDTH), lambda b, i: (b * nt + i, 0)),
                  pl.BlockSpec((HALO, POOL_WIDTH), lambda b, i: (jnp.maximum((b * nt + i) * hb - 1, 0), 0)),
                  pl.BlockSpec((4, POOL_GROUP, POOL_GROUP), lambda b, i: (0, 0, 0)),
                  pl.BlockSpec((1, POOL_WIDTH), lambda b, i: (0, 0)), _token_spec()],
        out_specs=[pl.BlockSpec((ts, POOL_WIDTH), lambda b, i: (b * nt + i, 0))] * 2,
        out_shape=[_out((nb * seq, POOL_WIDTH), BF)] * 2,
        compiler_params=_cp(("parallel", "parallel"), 32))(*_pin(proj, proj, w_pool, pool_scale, token))


def _conv_pre(uxbc, halo, cw, cb, first):
    halo = jnp.where(first, 0.0, halo)
    ext = jnp.concatenate([halo, uxbc], 0)
    pre = cb + uxbc * cw[3:4]
    for k in (2, 1, 0):
        pre = pre + pltpu.roll(ext, 3 - k, 0)[CONV_HALO:] * cw[k:k + 1]
    return pre


def _chunk_terms(pre, udt, dtb, alog):
    sg = _sigmoid(pre)
    xbc = pre * sg
    dtp = udt[:, :SSD_HEADS] + dtb
    dt = jnp.maximum(dtp, 0.0) + jnp.log(1.0 + jnp.exp(-jnp.abs(dtp)))
    a = -jnp.exp(alog)
    da = dt * a
    tril = (_iota((CHUNK, CHUNK), 0) >= _iota((CHUNK, CHUNK), 1))
    acum = _exact_nn_left(tril.astype(BF), da)
    eye = (_iota((SSD_HEADS, SSD_HEADS), 0) == _iota((SSD_HEADS, SSD_HEADS), 1)).astype(BF)
    acum_t = _exact_nt_left(eye, acum)
    expand = _head_expand_matrix(SSD_HEADS, SSD_INNER)
    acum_e = _exact_nn(acum, expand)
    dt_e = _exact_nn(dt, expand)
    last_e = acum_e[CHUNK - 1:CHUNK]
    return dict(pre=pre, sg=sg, xbc=xbc, dtp=dtp, dt=dt, a=a, acum=acum, acum_t=acum_t, tril=tril,
                dt_e=dt_e, e_a=jnp.exp(acum_e), d_out=jnp.exp(last_e - acum_e), c_dec=jnp.exp(last_e))


def _head_decay(r, h):
    seg = r["acum"][:, h:h + 1] - r["acum_t"][h:h + 1, :]
    return jnp.where(r["tril"], jnp.exp(jnp.minimum(seg, 0.0)), 0.0)


SSD_SUB = 4
SSD_ROWS = SSD_SUB * CHUNK


def _ssd_specs(nb, seq, reverse):
    ns = seq // SSD_ROWS
    per = seq // CONV_HALO

    def cidx(c):
        return (ns - 1 - c) if reverse else c

    def row(b, c):
        return b * ns + cidx(c)

    specs = [
        pl.BlockSpec((SSD_ROWS, CONV_CH), lambda b, c: (row(b, c), 1)),
        pl.BlockSpec((CONV_HALO, CONV_CH),
                     lambda b, c: (jnp.maximum(b * per + cidx(c) * (SSD_ROWS // CONV_HALO) - 1, 0), 1)),
        pl.BlockSpec((SSD_ROWS, GROUP_W), lambda b, c: (row(b, c), 1)),
        pl.BlockSpec((SSD_ROWS, GROUP_W), lambda b, c: (row(b, c), 2)),
        pl.BlockSpec((SSD_ROWS, 128), lambda b, c: (row(b, c), OFF_DT // 128)),
    ]
    return specs, row, cidx, ns


def _const_spec(shape):
    return pl.BlockSpec(shape, lambda b, c: (0,) * len(shape))


def _ssd_fwd(proj, conv_w, conv_b, dt_bias, a_log, dskip_e, g_ssd, nb, seq):
    specs, row, cidx, ns = _ssd_specs(nb, seq, reverse=False)

    def body(uxbc_ref, halo_ref, z0_ref, z1_ref, udt_ref, cw_ref, cb_ref, dtb_ref, alog_ref, dsk_ref, gs_ref,
             yssd_ref, yssm_ref, hprev_ref, pre_ref, h_ref, yd_ref):
        c = pl.program_id(1)

        @pl.when(c == 0)
        def _():
            h_ref[...] = jnp.zeros_like(h_ref)

        for sub in range(SSD_SUB):
            rows = slice(sub * CHUNK, (sub + 1) * CHUNK)
            if sub == 0:
                halo, first = halo_ref[...], c == 0
            else:
                halo, first = uxbc_ref[sub * CHUNK - CONV_HALO:sub * CHUNK, :], False
            pre = _conv_pre(uxbc_ref[rows, :], halo, cw_ref[...], cb_ref[...], first)
            pre_ref[rows, :] = pre
            r = _chunk_terms(pre, udt_ref[rows, :], dtb_ref[...], alog_ref[...])
            xbc = r["xbc"]
            xs = xbc[:, :SSD_INNER]
            xdt = xs * r["dt_e"]
            xdt_b = xdt.astype(BF)
            xdo_b = (xdt * r["d_out"]).astype(BF)
            hprev_ref[0, sub] = h_ref[...]
            for g in range(2):
                gs = slice(g * GROUP_W, (g + 1) * GROUP_W)
                bg = xbc[:, SSD_INNER + g * SSD_STATE:SSD_INNER + (g + 1) * SSD_STATE].astype(BF)
                cg = xbc[:, SSD_INNER + (2 + g) * SSD_STATE:SSD_INNER + (3 + g) * SSD_STATE].astype(BF)
                scores = _nt(cg, bg)
                hg = h_ref[g]
                y_off = _nn(cg, hg.astype(BF)) * r["e_a"][:, gs]
                for hh in range(8):
                    h = g * 8 + hh
                    hs = slice(h * SSD_HEAD_DIM, (h + 1) * SSD_HEAD_DIM)
                    m = (scores * _head_decay(r, h)).astype(BF)
                    yd_ref[sub, :, hs] = _nn(m, xdt_b[:, hs])
                h_ref[g] = hg * r["c_dec"][:, gs] + _tn(bg, xdo_b[:, gs])
                y = yd_ref[sub, :, gs] + y_off + dsk_ref[:, gs] * xs[:, gs]
                yssm_ref[rows, gs] = y
                zg = (z0_ref if g == 0 else z1_ref)[rows, :]
                yg = y * (zg * _sigmoid(zg))
                rg = lax.rsqrt(jnp.mean(yg * yg, -1, keepdims=True) + EPS)
                yssd_ref[rows, gs] = (yg * rg * gs_ref[:, gs]).astype(BF)

    t = nb * seq
    return pl.pallas_call(
        body, name="ssd_fwd", grid=(nb, ns),
        in_specs=specs + [_const_spec((4, CONV_CH)), _const_spec((1, CONV_CH)), _const_spec((1, SSD_HEADS)),
                          _const_spec((1, SSD_HEADS)), _const_spec((1, SSD_INNER)), _const_spec((1, SSD_INNER))],
        out_specs=[pl.BlockSpec((SSD_ROWS, SSD_INNER), lambda b, c: (row(b, c), 0)),
                   pl.BlockSpec((SSD_ROWS, SSD_INNER), lambda b, c: (row(b, c), 0)),
                   pl.BlockSpec((1, SSD_SUB, 2, SSD_STATE, GROUP_W), lambda b, c: (b, c, 0, 0, 0)),
                   pl.BlockSpec((SSD_ROWS, CONV_CH), lambda b, c: (row(b, c), 0))],
        out_shape=[_out((t, SSD_INNER), BF), _out((t, SSD_INNER), F32),
                   _out((nb, seq // CHUNK, 2, SSD_STATE, GROUP_W), F32), _out((t, CONV_CH), F32)],
        scratch_shapes=[pltpu.VMEM((2, SSD_STATE, GROUP_W), F32), pltpu.VMEM((SSD_SUB, CHUNK, SSD_INNER), F32)],
        compiler_params=_cp(("arbitrary", "arbitrary"), 56),
    )(*_pin(proj, proj, proj, proj, proj, conv_w, conv_b, dt_bias, a_log, dskip_e, g_ssd))


def _out_proj(y_pool, y_ssd, w_out, x, mod3, g_mlp, seq):
    t, d = x.shape
    tm = 512
    tps = seq // tm if seq >= tm else 1
    tm = min(tm, seq)

    def body(yp_ref, ys_ref, w_ref, x_ref, mod_ref, g_ref, h1_ref, o_ref, u2_ref):
        for rows in _sub_rows(tm):
            o = _nn(jnp.concatenate([yp_ref[rows, :], ys_ref[rows, :]], 1), w_ref[...])
            o_ref[rows, :] = o.astype(BF)
            h1 = x_ref[rows, :] + mod_ref[0, 2:3, :] * o
            h1_ref[rows, :] = h1
            r = lax.rsqrt(jnp.mean(h1 * h1, -1, keepdims=True) + EPS)
            u2_ref[rows, :] = ((h1 * r * g_ref[...]) * (1.0 + mod_ref[0, 4:5, :]) + mod_ref[0, 3:4, :]).astype(BF)

    row = lambda i: (i, 0)
    return pl.pallas_call(
        body, name="out_proj", grid=(t // tm,),
        in_specs=[pl.BlockSpec((tm, POOL_WIDTH), row), pl.BlockSpec((tm, SSD_INNER), row),
                  _RESIDENT, pl.BlockSpec((tm, d), row),
                  pl.BlockSpec((1, N_MOD, d), lambda i: (i // tps, 0, 0)), pl.BlockSpec((1, d), lambda i: (0, 0))],
        out_specs=[pl.BlockSpec((tm, d), row)] * 3,
        out_shape=[_out((t, d), F32), _out((t, d), BF), _out((t, d), BF)],
        compiler_params=_cp(("parallel",), 48))(*_pin(y_pool, y_ssd), w_out, *_pin(x, mod3, g_mlp))


def _mlp_up(u2, w_up4):
    t, d = u2.shape
    tm = min(1024, t)
    nk, _, cols = w_up4.shape

    def body(u_ref, w_ref, a_ref):
        a_ref[...] = _nn(u_ref[...], w_ref[pl.program_id(1)]).astype(BF)

    return pl.pallas_call(
        body, name="mlp_up", grid=(t // tm, nk),
        in_specs=[pl.BlockSpec((tm, d), lambda i, k: (i, 0)), _RESIDENT],
        out_specs=pl.BlockSpec((tm, cols), lambda i, k: (i, k)),
        out_shape=_out((t, nk * cols), BF),
        compiler_params=_cp(("parallel", "parallel"), 32))(*_pin(u2), w_up4)


def _mlp_down_loss(a_up, w_down, h1, mod3, g_final, target, seq):
    t, d = h1.shape
    nb = t // seq
    tm = min(ROW_TILE, seq)
    tps = seq // tm

    def body(a_ref, w_ref, h1_ref, mod_ref, g_ref, tg_ref, ddn_ref, dh2_ref, sq_ref, gg_ref, dgf_ref):
        i = pl.program_id(0)

        @pl.when(i == 0)
        def _():
            sq_ref[...] = jnp.zeros_like(sq_ref)
            gg_ref[...] = jnp.zeros_like(gg_ref)

        @pl.when(i % tps == 0)
        def _():
            dgf_ref[...] = jnp.zeros_like(dgf_ref)

        gate = mod_ref[0, 5:6, :]
        sq = gg = dgf = 0.0
        for rows in _sub_rows(tm):
            f = jnp.square(jnp.maximum(a_ref[rows, :], 0))
            dn = _nn(f, w_ref[...])
            h2 = h1_ref[rows, :] + gate * dn
            r = lax.rsqrt(jnp.mean(h2 * h2, -1, keepdims=True) + EPS)
            hh = h2 * r
            err = hh * g_ref[...] - tg_ref[rows, :]
            dy = err * (1.0 / d)
            dhat = dy * g_ref[...]
            dh2 = r * (dhat - hh * jnp.mean(dhat * hh, -1, keepdims=True))
            dh2_ref[rows, :] = dh2
            ddn_ref[rows, :] = (dh2 * gate).astype(BF)
            sq = sq + jnp.sum(err * err, 0, keepdims=True)
            gg = gg + jnp.sum(dy * hh, 0, keepdims=True)
            dgf = dgf + jnp.sum(dh2 * dn, 0, keepdims=True)
        sq_ref[...] += sq
        gg_ref[...] += gg
        dgf_ref[0] += dgf

    row = lambda i: (i, 0)
    vec = pl.BlockSpec((1, d), lambda i: (0, 0))
    return pl.pallas_call(
        body, name="mlp_down_loss", grid=(t // tm,),
        in_specs=[pl.BlockSpec((tm, D_FF), row), _RESIDENT, pl.BlockSpec((tm, d), row),
                  pl.BlockSpec((1, N_MOD, d), lambda i: (i // tps, 0, 0)), vec, pl.BlockSpec((tm, d), row)],
        out_specs=[pl.BlockSpec((tm, d), row), pl.BlockSpec((tm, d), row), vec, vec,
                   pl.BlockSpec((1, 1, d), lambda i: (i // tps, 0, 0))],
        out_shape=[_out((t, d), BF), _out((t, d), F32), _out((1, d), F32),
                   _out((1, d), F32), _out((nb, 1, d), F32)],
        compiler_params=_cp(("arbitrary",), 44))(*_pin(a_up), w_down, *_pin(h1, mod3, g_final, target))


def _tn_matmul(a, b, tk, tn, name, square_relu=False, out3=False):
    t, kdim = a.shape
    ndim = b.shape[1]

    def body(a_ref, b_ref, o_ref):
        av = a_ref[...]
        if square_relu:
            av = jnp.square(jnp.maximum(av, 0))
        res = _tn(av, b_ref[...]).astype(BF)
        if out3:
            o_ref[0] = res
        else:
            o_ref[...] = res

    if out3:
        out_spec = pl.BlockSpec((1, tk, tn), lambda j, i: (j, i, 0))
        out_shape = _out((ndim // tn, kdim, tn), BF)
    else:
        out_spec = pl.BlockSpec((tk, tn), lambda j, i: (i, j))
        out_shape = _out((kdim, ndim), BF)
    return pl.pallas_call(
        body, name=name, grid=(ndim // tn, kdim // tk),
        in_specs=[pl.BlockSpec((t, tk), lambda j, i: (0, i)), pl.BlockSpec((t, tn), lambda j, i: (0, j))],
        out_specs=out_spec, out_shape=out_shape,
        compiler_params=_cp(("parallel", "parallel"), 56))(*_pin(a, b))


def _mlp_down_bwd(d_dn, w_down4, a_up, token):
    t, d = d_dn.shape
    tm = min(1024, t)
    nk, rows, _ = w_down4.shape

    def body(g_ref, w_ref, a_ref, tok_ref, o_ref):
        df = _nt(g_ref[...], w_ref[pl.program_id(1)])
        o_ref[...] = (df * (2.0 * jnp.maximum(a_ref[...], 0).astype(F32))).astype(BF)

    return pl.pallas_call(
        body, name="mlp_down_bwd", grid=(t // tm, nk),
        in_specs=[pl.BlockSpec((tm, d), lambda i, k: (i, 0)), _RESIDENT,
                  pl.BlockSpec((tm, rows), lambda i, k: (i, k)), _token_spec()],
        out_specs=pl.BlockSpec((tm, rows), lambda i, k: (i, k)),
        out_shape=_out((t, nk * rows), BF),
        compiler_params=_cp(("parallel", "parallel"), 32))(*_pin(d_dn), w_down4, *_pin(a_up, token))


def _mlp_up_bwd(d_a, w_up4, h1, dh2, o, mod3, g_mlp, seq, token):
    t, d = h1.shape
    nb = t // seq
    tm = min(ROW_TILE, seq)
    tps = seq // tm
    nk = w_up4.shape[0]
    cols = w_up4.shape[2]

    def body(da_ref, w_ref, h1_ref, dh2_ref, o_ref, mod_ref, g_ref, tok_ref, dh1_ref, do_ref, acc_ref, gg_ref):
        i = pl.program_id(0)

        @pl.when(i == 0)
        def _():
            gg_ref[...] = jnp.zeros_like(gg_ref)

        @pl.when(i % tps == 0)
        def _():
            acc_ref[...] = jnp.zeros_like(acc_ref)

        gg = a_shift = a_scale = a_gate = 0.0
        for rows in _sub_rows(tm):
            du = _nt(da_ref[rows, 0:cols], w_ref[0])
            for k in range(1, nk):
                du = du + _nt(da_ref[rows, k * cols:(k + 1) * cols], w_ref[k])
            h1 = h1_ref[rows, :]
            r = lax.rsqrt(jnp.mean(h1 * h1, -1, keepdims=True) + EPS)
            hh = h1 * r
            n2 = hh * g_ref[...]
            dn2 = du * (1.0 + mod_ref[0, 4:5, :])
            dhat = dn2 * g_ref[...]
            dh1 = dh2_ref[rows, :] + r * (dhat - hh * jnp.mean(dhat * hh, -1, keepdims=True))
            dh1_ref[rows, :] = dh1
            do_ref[rows, :] = (dh1 * mod_ref[0, 2:3, :]).astype(BF)
            gg = gg + jnp.sum(dn2 * hh, 0, keepdims=True)
            a_shift = a_shift + jnp.sum(du, 0, keepdims=True)
            a_scale = a_scale + jnp.sum(du * n2, 0, keepdims=True)
            a_gate = a_gate + jnp.sum(dh1 * o_ref[rows, :].astype(F32), 0, keepdims=True)
        gg_ref[...] += gg
        acc_ref[0, 0:1, :] += a_shift
        acc_ref[0, 1:2, :] += a_scale
        acc_ref[0, 2:3, :] += a_gate

    row = lambda i: (i, 0)
    vec = pl.BlockSpec((1, d), lambda i: (0, 0))
    return pl.pallas_call(
        body, name="mlp_up_bwd", grid=(t // tm,),
        in_specs=[pl.BlockSpec((tm, D_FF), row), _RESIDENT, pl.BlockSpec((tm, d), row),
                  pl.BlockSpec((tm, d), row), pl.BlockSpec((tm, d), row),
                  pl.BlockSpec((1, N_MOD, d), lambda i: (i // tps, 0, 0)), vec, _token_spec()],
        out_specs=[pl.BlockSpec((tm, d), row), pl.BlockSpec((tm, d), row),
                   pl.BlockSpec((1, 8, d), lambda i: (i // tps, 0, 0)), vec],
        out_shape=[_out((t, d), F32), _out((t, d), BF),
                   _out((nb, 8, d), F32), _out((1, d), F32)],
        compiler_params=_cp(("arbitrary",), 44))(*_pin(d_a), w_up4, *_pin(h1, dh2, o, mod3, g_mlp, token))


def _out_proj_bwd(d_o, w_out, token):
    t, d = d_o.shape
    tm = min(1024, t)

    def body(g_ref, w_ref, tok_ref, dp_ref, ds_ref):
        gv = g_ref[...]
        dp_ref[...] = _nt(gv, w_ref[0:POOL_WIDTH, :])
        ds_ref[...] = _nt(gv, w_ref[POOL_WIDTH:, :])

    row = lambda i: (i, 0)
    return pl.pallas_call(
        body, name="out_proj_bwd", grid=(t // tm,),
        in_specs=[pl.BlockSpec((tm, d), row), _RESIDENT, _token_spec()],
        out_specs=[pl.BlockSpec((tm, POOL_WIDTH), row), pl.BlockSpec((tm, SSD_INNER), row)],
        out_shape=[_out((t, POOL_WIDTH), F32), _out((t, SSD_INNER), F32)],
        compiler_params=_cp(("parallel",), 32))(*_pin(d_o), w_out, *_pin(token))


def _pool_bwd(d_ypool, p, w_pool, pool_scale, nb, seq):
    ts = _pool_tile(seq)
    nt = seq // ts
    hb = ts // HALO
    last_block = nb * seq // HALO - 1

    def body(dy_ref, halo_ref, p_ref, wp_ref, ps_ref, du_ref, gw_ref, gs_ref):
        b = pl.program_id(0)
        i = pl.program_id(1)

        @pl.when((b == 0) & (i == 0))
        def _():
            gw_ref[...] = jnp.zeros_like(gw_ref)
            gs_ref[...] = jnp.zeros_like(gs_ref)

        halo = jnp.where(i == nt - 1, 0.0, halo_ref[...])
        dy = dy_ref[...]
        ext = jnp.concatenate([dy, halo], 0)
        tpos = i * ts + _iota((ts + HALO, 1), 0)
        n_ext = ts + HALO
        for g, w in enumerate(POOL_WINDOWS):
            gs = slice(g * POOL_GROUP, (g + 1) * POOL_GROUP)
            wg = wp_ref[g].astype(BF)
            pg = p_ref[:, gs]
            pw = _nn(pg, wg)
            gs_ref[:, gs] += jnp.sum(dy[:, gs] * pw, 0, keepdims=True)
            dpw = (ext[:, gs] * ps_ref[:, gs]).astype(BF)
            gw_ref[g] += _tn(pg, dpw[:ts])
            dp = _nt(dpw, wg)
            cnt = jnp.minimum(tpos + 1, w).astype(F32)
            s = dp / cnt
            sh = 1
            while sh < w:
                s = s + pltpu.roll(s, n_ext - sh, 0)
                sh *= 2
            du_ref[:, gs] = (s[:ts] - dp[:ts]).astype(BF)

    return pl.pallas_call(
        body, name="pool_bwd", grid=(nb, nt),
        in_specs=[pl.BlockSpec((ts, POOL_WIDTH), lambda b, i: (b * nt + i, 0)),
                  pl.BlockSpec((HALO, POOL_WIDTH), lambda b, i: (jnp.minimum((b * nt + i + 1) * hb, last_block), 0)),
                  pl.BlockSpec((ts, POOL_WIDTH), lambda b, i: (b * nt + i, 0)),
                  pl.BlockSpec((4, POOL_GROUP, POOL_GROUP), lambda b, i: (0, 0, 0)),
                  pl.BlockSpec((1, POOL_WIDTH), lambda b, i: (0, 0))],
        out_specs=[pl.BlockSpec((ts, POOL_WIDTH), lambda b, i: (b * nt + i, 0)),
                   pl.BlockSpec((4, POOL_GROUP, POOL_GROUP), lambda b, i: (0, 0, 0)),
                   pl.BlockSpec((1, POOL_WIDTH), lambda b, i: (0, 0))],
        out_shape=[_out((nb * seq, POOL_WIDTH), BF), _out((4, POOL_GROUP, POOL_GROUP), F32),
                   _out((1, POOL_WIDTH), F32)],
        compiler_params=_cp(("arbitrary", "arbitrary"), 32))(*_pin(d_ypool, d_ypool, p, w_pool, pool_scale))


def _ssd_bwd(proj, pre, d_yssd, yssm, h_prev, dt_bias, a_log, dskip_e, g_ssd, nb, seq):
    specs, row, cidx, ns = _ssd_specs(nb, seq, reverse=True)
    specs = specs[2:]

    def body(z0_ref, z1_ref, udt_ref, pre_ref, dys_ref, yssm_ref, hprev_ref,
             dtb_ref, alog_ref, dsk_ref, gs_ref,
             dz_ref, dpre_ref, dudt_ref, ggs_ref, gdsk_ref, ga_ref, gdtb_ref,
             g_ref, dxdt_ref, dyv_ref):
        b = pl.program_id(0)
        c = pl.program_id(1)

        @pl.when(c == 0)
        def _():
            g_ref[...] = jnp.zeros_like(g_ref)

        @pl.when((b == 0) & (c == 0))
        def _():
            ggs_ref[...] = jnp.zeros_like(ggs_ref)
            gdsk_ref[...] = jnp.zeros_like(gdsk_ref)
            ga_ref[...] = jnp.zeros_like(ga_ref)
            gdtb_ref[...] = jnp.zeros_like(gdtb_ref)

        for sub in reversed(range(SSD_SUB)):
            chunk(sub, z0_ref, z1_ref, udt_ref, pre_ref, dys_ref, yssm_ref, hprev_ref, dtb_ref, alog_ref, dsk_ref, gs_ref,
                  dz_ref, dpre_ref, dudt_ref, ggs_ref, gdsk_ref, ga_ref, gdtb_ref, g_ref, dxdt_ref.at[sub], dyv_ref.at[sub])

    def chunk(sub, z0_ref, z1_ref, udt_ref, pre_ref, dys_ref, yssm_ref, hprev_ref,
              dtb_ref, alog_ref, dsk_ref, gs_ref,
              dz_ref, dpre_ref, dudt_ref, ggs_ref, gdsk_ref, ga_ref, gdtb_ref,
              g_ref, dxdt_ref, dyv_ref):
        rows = slice(sub * CHUNK, (sub + 1) * CHUNK)
        r = _chunk_terms(pre_ref[rows, :], udt_ref[rows, :], dtb_ref[...], alog_ref[...])
        xbc = r["xbc"]
        xs = xbc[:, :SSD_INNER]
        dt_e = r["dt_e"]
        xdt = xs * dt_e
        xdt_b = xdt.astype(BF)
        reduce_m = _head_reduce_matrix(GROUP_W, 8)

        def head_sums(v):
            return _nn(v.astype(BF), reduce_m)

        onehot16 = lambda h: (_iota((1, SSD_HEADS), 1) == h).astype(F32)
        onecol16 = lambda h: (_iota((SSD_HEADS, 1), 0) == h).astype(F32)

        d_acum = jnp.zeros((CHUNK, SSD_HEADS), F32)
        d_acum_t = jnp.zeros((SSD_HEADS, CHUNK), F32)
        d_alast = jnp.zeros((1, SSD_HEADS), F32)
        place8 = lambda g: (_iota((8, SSD_HEADS), 1) == _iota((8, SSD_HEADS), 0) + 8 * g).astype(BF)
        d_b, d_c = [], []
        for g in range(2):
            gs = slice(g * GROUP_W, (g + 1) * GROUP_W)
            zg = (z0_ref if g == 0 else z1_ref)[rows, :]
            sz = _sigmoid(zg)
            silu_z = zg * sz
            ys = yssm_ref[rows, gs]
            yg = ys * silu_z
            rg = lax.rsqrt(jnp.mean(yg * yg, -1, keepdims=True) + EPS)
            yh = yg * rg
            dys = dys_ref[rows, gs]
            ggs_ref[:, gs] += jnp.sum(dys * yh, 0, keepdims=True)
            dyh = dys * gs_ref[:, gs]
            dyg = rg * (dyh - yh * jnp.mean(dyh * yh, -1, keepdims=True))
            dy = dyg * silu_z
            dz_ref[rows, gs] = (dyg * ys * (sz * (1.0 + zg * (1.0 - sz)))).astype(BF)
            gdsk_ref[:, gs] += jnp.sum(dy * xs[:, gs], 0, keepdims=True)
            dyv_ref[:, gs] = dy
            dy_b = dy.astype(BF)

            bg = xbc[:, SSD_INNER + g * SSD_STATE:SSD_INNER + (g + 1) * SSD_STATE].astype(BF)
            cg = xbc[:, SSD_INNER + (2 + g) * SSD_STATE:SSD_INNER + (3 + g) * SSD_STATE].astype(BF)
            scores = _nt(cg, bg)
            hg = hprev_ref[0, sub, g]
            hg_b = hg.astype(BF)
            gg = g_ref[g]
            gg_b = gg.astype(BF)
            e_a = r["e_a"][:, gs]
            d_out = r["d_out"][:, gs]
            c_dec = r["c_dec"][:, gs]
            zc = _nn(cg, hg_b)
            wv = e_a * dy
            wv_b = wv.astype(BF)
            da_g = head_sums(wv * zc)
            dcg = _nt(wv_b, hg_b)
            d_hprev = _tn(cg, wv_b)
            vg = _nn(bg, gg_b)
            dxdt_g = d_out * vg
            dd_out = head_sums(xdt[:, gs] * vg)
            dbg = _nt((xdt[:, gs] * d_out).astype(BF), gg_b)
            dcd = _exact_nn(jnp.sum(gg * hg, 0, keepdims=True), reduce_m)
            d_out8 = jnp.exp(r["acum"][CHUNK - 1:CHUNK, 8 * g:8 * g + 8] - r["acum"][:, 8 * g:8 * g + 8])
            c_dec8 = jnp.exp(r["acum"][CHUNK - 1:CHUNK, 8 * g:8 * g + 8])
            t8 = dd_out * d_out8
            d_alast = d_alast + _exact_nn(jnp.sum(t8, 0, keepdims=True) + dcd * c_dec8, place8(g))
            d_acum = d_acum + _exact_nn(da_g - t8, place8(g))
            dsc = jnp.zeros((CHUNK, CHUNK), F32)
            for hh in range(8):
                h = g * 8 + hh
                hs = slice(h * SSD_HEAD_DIM, (h + 1) * SSD_HEAD_DIM)
                lam = _head_decay(r, h)
                m = scores * lam
                dyh_b = dy_b[:, hh * SSD_HEAD_DIM:(hh + 1) * SSD_HEAD_DIM]
                dm = _nt(dyh_b, xdt_b[:, hs])
                tm_ = dm * m
                d_acum = d_acum + jnp.sum(tm_, 1, keepdims=True) * onehot16(h)
                d_acum_t = d_acum_t + onecol16(h) * jnp.sum(tm_, 0, keepdims=True)
                dsc = dsc + dm * lam
                dxdt_ref[:, hs] = _tn(m.astype(BF), dyh_b) + dxdt_g[:, hh * SSD_HEAD_DIM:(hh + 1) * SSD_HEAD_DIM]
            dsc_b = dsc.astype(BF)
            d_c.append(dcg + _nn(dsc_b, bg))
            d_b.append(dbg + _tn(dsc_b, cg))
            g_ref[g] = d_hprev + c_dec * gg

        eye = (_iota((CHUNK, CHUNK), 0) == _iota((CHUNK, CHUNK), 1)).astype(BF)
        d_acum = d_acum - _exact_nt_left(eye, d_acum_t)
        is_last = (_iota((CHUNK, 1), 0) == CHUNK - 1).astype(F32)
        d_acum = d_acum + is_last * d_alast
        triu = (_iota((CHUNK, CHUNK), 0) <= _iota((CHUNK, CHUNK), 1)).astype(BF)
        d_da = _exact_nn_left(triu, d_acum)
        dt = r["dt"]
        ga_ref[...] += jnp.sum(d_da * dt, 0, keepdims=True)
        dxdt = dxdt_ref[...]
        reduce16 = _head_reduce_matrix(SSD_INNER, SSD_HEADS)
        d_dt = d_da * r["a"] + _nn((dxdt * xs).astype(BF), reduce16)
        d_udt = d_dt * _sigmoid(r["dtp"])
        gdtb_ref[...] += jnp.sum(d_udt, 0, keepdims=True)
        dudt_ref[rows, :] = jnp.zeros((CHUNK, dudt_ref.shape[1]), BF)
        dudt_ref[rows, 0:SSD_HEADS] = d_udt.astype(BF)
        pre, sg = r["pre"], r["sg"]
        dsilu = sg * (1.0 + pre * (1.0 - sg))
        dpre_ref[rows, 0:SSD_INNER] = (dsk_ref[...] * dyv_ref[...] + dxdt * dt_e) * dsilu[:, 0:SSD_INNER]
        for g in range(2):
            bs = slice(SSD_INNER + g * SSD_STATE, SSD_INNER + (g + 1) * SSD_STATE)
            cs = slice(SSD_INNER + (2 + g) * SSD_STATE, SSD_INNER + (3 + g) * SSD_STATE)
            dpre_ref[rows, bs] = d_b[g] * dsilu[:, bs]
            dpre_ref[rows, cs] = d_c[g] * dsilu[:, cs]

    t = nb * seq
    vec = _const_spec((1, SSD_INNER))
    small = _const_spec((1, SSD_HEADS))
    return pl.pallas_call(
        body, name="ssd_bwd", grid=(nb, ns),
        in_specs=specs + [pl.BlockSpec((SSD_ROWS, CONV_CH), lambda b, c: (row(b, c), 0)),
                          pl.BlockSpec((SSD_ROWS, SSD_INNER), lambda b, c: (row(b, c), 0)),
                          pl.BlockSpec((SSD_ROWS, SSD_INNER), lambda b, c: (row(b, c), 0)),
                          pl.BlockSpec((1, SSD_SUB, 2, SSD_STATE, GROUP_W), lambda b, c: (b, cidx(c), 0, 0, 0)),
                          small, small, vec, vec],
        out_specs=[pl.BlockSpec((SSD_ROWS, SSD_INNER), lambda b, c: (row(b, c), 0)),
                   pl.BlockSpec((SSD_ROWS, CONV_CH), lambda b, c: (row(b, c), 0)),
                   pl.BlockSpec((SSD_ROWS, 128), lambda b, c: (row(b, c), 0)),
                   vec, vec, small, small],
        out_shape=[_out((t, SSD_INNER), BF), _out((t, CONV_CH), F32),
                   _out((t, 128), BF), _out((1, SSD_INNER), F32),
                   _out((1, SSD_INNER), F32), _out((1, SSD_HEADS), F32),
                   _out((1, SSD_HEADS), F32)],
        scratch_shapes=[pltpu.VMEM((2, SSD_STATE, GROUP_W), F32), pltpu.VMEM((SSD_SUB, CHUNK, SSD_INNER), F32),
                        pltpu.VMEM((SSD_SUB, CHUNK, SSD_INNER), F32)],
        compiler_params=_cp(("arbitrary", "arbitrary"), 56),
    )(*_pin(proj, proj, proj, pre, d_yssd, yssm, h_prev, dt_bias, a_log, dskip_e, g_ssd))


def _grad_w_out(y_pool, y_ssd, d_o):
    t, d = d_o.shape
    tk = POOL_WIDTH
    n_s = SSD_INNER // tk

    def body(p_ref, s_ref, g_ref, o_ref):
        i = pl.program_id(0)

        @pl.when(i == 0)
        def _():
            o_ref[...] = _tn(p_ref[...], g_ref[...]).astype(BF)

        @pl.when(i > 0)
        def _():
            o_ref[...] = _tn(s_ref[...], g_ref[...]).astype(BF)

    return pl.pallas_call(
        body, name="grad_w_out", grid=(1 + n_s,),
        in_specs=[pl.BlockSpec((t, tk), lambda i: (0, 0)), pl.BlockSpec((t, tk), lambda i: (0, jnp.maximum(i - 1, 0))),
                  pl.BlockSpec((t, d), lambda i: (0, 0))],
        out_specs=pl.BlockSpec((tk, d), lambda i: (i, 0)),
        out_shape=_out((POOL_WIDTH + SSD_INNER, d), BF),
        compiler_params=_cp(("parallel",), 56))(*_pin(y_pool, y_ssd, d_o))


def _grad_w_in_t(d_upool, d_z, d_uxbc, d_udt, u1):
    t, d = u1.shape
    tk = 512
    n_z, n_x = SSD_INNER // tk, CONV_CH // tk

    def body(p_ref, z_ref, x_ref, dt_ref, u_ref, o_ref):
        i = pl.program_id(0)

        @pl.when(i == 0)
        def _():
            o_ref[...] = _tn(p_ref[...], u_ref[...]).astype(BF)

        @pl.when((i >= 1) & (i < 1 + n_z))
        def _():
            o_ref[...] = _tn(z_ref[...], u_ref[...]).astype(BF)

        @pl.when((i >= 1 + n_z) & (i < 1 + n_z + n_x))
        def _():
            o_ref[...] = _tn(x_ref[...], u_ref[...]).astype(BF)

        @pl.when(i == 1 + n_z + n_x)
        def _():
            o_ref[0:128, :] = _tn(dt_ref[...], u_ref[...]).astype(BF)

    return pl.pallas_call(
        body, name="grad_w_in", grid=(2 + n_z + n_x,),
        in_specs=[pl.BlockSpec((t, tk), lambda i: (0, 0)),
                  pl.BlockSpec((t, tk), lambda i: (0, jnp.clip(i - 1, 0, n_z - 1))),
                  pl.BlockSpec((t, tk), lambda i: (0, jnp.clip(i - 1 - n_z, 0, n_x - 1))),
                  pl.BlockSpec((t, 128), lambda i: (0, 0)), pl.BlockSpec((t, d), lambda i: (0, 0))],
        out_specs=pl.BlockSpec((tk, d), lambda i: (i, 0)),
        out_shape=_out((IN_PAD, d), BF),
        compiler_params=_cp(("parallel",), 56))(*_pin(d_upool, d_z, d_uxbc, d_udt, u1))


def _conv_bwd(d_pre, proj, conv_w, nb, seq):
    ts = min(256, seq)
    nt = seq // ts
    hb = ts // CONV_HALO
    last_block = nb * seq // CONV_HALO - 1
    n_ext = CHUNK + CONV_HALO

    def body(dp_ref, dnext_ref, u_ref, cw_ref, du_ref, gw_ref, gb_ref):
        b = pl.program_id(0)
        i = pl.program_id(1)

        @pl.when((b == 0) & (i == 0))
        def _():
            gw_ref[...] = jnp.zeros_like(gw_ref)
            gb_ref[...] = jnp.zeros_like(gb_ref)

        for c0 in range(0, CONV_CH, 128):
            cs = slice(c0, c0 + 128)
            cw = cw_ref[:, cs]
            gw = [0.0] * 4
            gb = 0.0
            for r0 in range(0, ts, CHUNK):
                dp = dp_ref[r0:r0 + CHUNK, cs]
                u = u_ref[r0:r0 + CHUNK, cs]
                if r0 + CHUNK < ts:
                    below = dp_ref[r0 + CHUNK:r0 + CHUNK + CONV_HALO, cs]
                else:
                    below = jnp.where(i == nt - 1, 0.0, dnext_ref[:, cs])
                ext_d = jnp.concatenate([dp, below], 0)
                du = dp * cw[3:4]
                gw[3] = gw[3] + jnp.sum(dp * u, 0, keepdims=True)
                for k in (2, 1, 0):
                    shifted = pltpu.roll(ext_d, n_ext - (3 - k), 0)[:CHUNK]
                    du = du + shifted * cw[k:k + 1]
                    gw[k] = gw[k] + jnp.sum(shifted * u, 0, keepdims=True)
                gb = gb + jnp.sum(dp, 0, keepdims=True)
                du_ref[r0:r0 + CHUNK, cs] = du.astype(BF)
            for k in range(4):
                gw_ref[k:k + 1, cs] += gw[k]
            gb_ref[:, cs] += gb

    return pl.pallas_call(
        body, name="conv_bwd", grid=(nb, nt),
        in_specs=[pl.BlockSpec((ts, CONV_CH), lambda b, i: (b * nt + i, 0)),
                  pl.BlockSpec((CONV_HALO, CONV_CH), lambda b, i: (jnp.minimum((b * nt + i + 1) * hb, last_block), 0)),
                  pl.BlockSpec((ts, CONV_CH), lambda b, i: (b * nt + i, 1)),
                  pl.BlockSpec((4, CONV_CH), lambda b, i: (0, 0))],
        out_specs=[pl.BlockSpec((ts, CONV_CH), lambda b, i: (b * nt + i, 0)),
                   pl.BlockSpec((8, CONV_CH), lambda b, i: (0, 0)), pl.BlockSpec((1, CONV_CH), lambda b, i: (0, 0))],
        out_shape=[_out((nb * seq, CONV_CH), BF), _out((8, CONV_CH), F32),
                   _out((1, CONV_CH), F32)],
        compiler_params=_cp(("arbitrary", "arbitrary"), 48))(*_pin(d_pre, d_pre, proj, conv_w))


def _in_proj_bwd(d_parts, w_in_t, x, dh1, mod3, g_mix, seq, token):
    t, d = x.shape
    nb = t // seq
    tm = min(ROW_TILE, seq)
    tps = seq // tm

    widths = [p.shape[1] for p in d_parts]

    def body(d0_ref, d1_ref, d2_ref, d3_ref, w_ref, x_ref, dh1_ref, mod_ref, g_ref, tok_ref, gx_ref, acc_ref, gg_ref):
        i = pl.program_id(0)

        @pl.when(i == 0)
        def _():
            gg_ref[...] = jnp.zeros_like(gg_ref)

        @pl.when(i % tps == 0)
        def _():
            acc_ref[...] = jnp.zeros_like(acc_ref)

        gg = a_shift = a_scale = 0.0
        for rows in _sub_rows(tm):
            d_cat = jnp.concatenate([p_ref[rows, :] for p_ref in (d0_ref, d1_ref, d2_ref)], 1)
            du = _nn(d_cat, w_ref[0:OFF_DT, :]) + _nn(d3_ref[rows, 0:IN_WIDTH - OFF_DT], w_ref[OFF_DT:IN_WIDTH, :])
            xv = x_ref[rows, :]
            r = lax.rsqrt(jnp.mean(xv * xv, -1, keepdims=True) + EPS)
            hh = xv * r
            n1 = hh * g_ref[...]
            dn1 = du * (1.0 + mod_ref[0, 1:2, :])
            dhat = dn1 * g_ref[...]
            gx_ref[rows, :] = dh1_ref[rows, :] + r * (dhat - hh * jnp.mean(dhat * hh, -1, keepdims=True))
            gg = gg + jnp.sum(dn1 * hh, 0, keepdims=True)
            a_shift = a_shift + jnp.sum(du, 0, keepdims=True)
            a_scale = a_scale + jnp.sum(du * n1, 0, keepdims=True)
        gg_ref[...] += gg
        acc_ref[0, 0:1, :] += a_shift
        acc_ref[0, 1:2, :] += a_scale

    row = lambda i: (i, 0)
    vec = pl.BlockSpec((1, d), lambda i: (0, 0))
    return pl.pallas_call(
        body, name="in_proj_bwd", grid=(t // tm,),
        in_specs=[pl.BlockSpec((tm, wd), row) for wd in widths] +
                 [_RESIDENT, pl.BlockSpec((tm, d), row),
                  pl.BlockSpec((tm, d), row), pl.BlockSpec((1, N_MOD, d), lambda i: (i // tps, 0, 0)), vec, _token_spec()],
        out_specs=[pl.BlockSpec((tm, d), row), pl.BlockSpec((1, 8, d), lambda i: (i // tps, 0, 0)), vec],
        out_shape=[_out((t, d), F32), _out((nb, 8, d), F32),
                   _out((1, d), F32)],
        compiler_params=_cp(("arbitrary",), 40))(*_pin(*d_parts), w_in_t, *_pin(x, dh1, mod3, g_mix, token))


_VEC_LAYOUT = (("g_mix", 1024), ("conv_b", 1536), ("g_ssd", 1024), ("pool_scale", 512), ("g_mlp", 1024),
               ("g_final", 1024), ("dt_bias", 128), ("a_log", 128), ("d_skip_lanes", 1024), ("sq_err", 1024))
_VEC_OFFSET = {}
_off = 0
for _name, _n in _VEC_LAYOUT:
    _VEC_OFFSET[_name] = _off
    _off += _n
_SMALL_PARAMS = ("b_ada", "g_mix", "conv_w", "conv_b", "dt_bias", "a_log", "d_skip", "g_ssd", "w_pool", "pool_scale",
                 "g_mlp", "g_final")


def _pack_vec(parts):
    cols = []
    for name, n in _VEC_LAYOUT:
        v = parts[name]
        if v.shape[1] < n:
            v = jnp.pad(v, ((0, 0), (0, n - v.shape[1])))
        cols.append(v)
    return jnp.concatenate(cols, 1)


def _small_adam(vec_all, wpool_all, convw_all, dmod_all, params):
    names = _SMALL_PARAMS
    nin = 4 + 3 * len(names)

    def body(*refs):
        vec_ref, wp_ref, cw_ref, dm_ref = refs[:4]
        prm = {n: refs[4 + 3 * i:7 + 3 * i] for i, n in enumerate(names)}
        loss_ref = refs[nin]
        outs = {n: refs[nin + 1 + 4 * i:nin + 5 + 4 * i] for i, n in enumerate(names)}
        vsum = vec_ref[0]
        for s in range(1, N_DEV):
            vsum = vsum + vec_ref[s]

        def lanes(name, n):
            off = _VEC_OFFSET[name]
            return vsum[:, off:off + n]

        grads = {n: lanes(n, prm[n][0].shape[1]) for n in ("g_mix", "conv_b", "g_ssd", "pool_scale", "g_mlp", "g_final", "dt_bias")}
        grads["a_log"] = lanes("a_log", SSD_HEADS) * (-jnp.exp(prm["a_log"][0][...]))
        per_lane = jnp.broadcast_to(lanes("d_skip_lanes", SSD_INNER), (8, SSD_INNER))
        grads["d_skip"] = _exact_nn(per_lane, _head_reduce_matrix(SSD_INNER, SSD_HEADS))[0:1]
        gwp = wp_ref[0].astype(F32)
        gcw = cw_ref[0]
        gb = jnp.sum(dm_ref[0], 0, keepdims=True)
        for s in range(1, N_DEV):
            gwp = gwp + wp_ref[s].astype(F32)
            gcw = gcw + cw_ref[s]
            gb = gb + jnp.sum(dm_ref[s], 0, keepdims=True)
        grads["w_pool"] = gwp
        grads["conv_w"] = gcw[0:4]
        grads["b_ada"] = gb
        total = jnp.sum(lanes("sq_err", D_MODEL), 1, keepdims=True) * (0.5 / D_MODEL)
        loss_ref[...] = jnp.broadcast_to(total, loss_ref.shape)
        for n in names:
            w_ref, m_ref, v_ref = prm[n]
            g = grads[n]
            d, m2, v2 = _adam_math(w_ref[...], g, m_ref[...], v_ref[...])
            g_ref, d_ref, m2_ref, v2_ref = outs[n]
            g_ref[...] = g
            d_ref[...] = d
            m2_ref[...] = m2
            v2_ref[...] = v2

    flat = [vec_all, wpool_all, convw_all, dmod_all]
    out_shape = [jax.ShapeDtypeStruct((1, 128), F32)]
    for n in names:
        flat += list(params[n])
        out_shape += [jax.ShapeDtypeStruct(params[n][0].shape, F32)] * 4
    vm = pl.BlockSpec(memory_space=pltpu.VMEM)
    res = pl.pallas_call(body, name="small_adam", out_shape=out_shape, in_specs=[vm] * len(flat),
                         out_specs=[vm] * len(out_shape), compiler_params=_cp(vmem_mb=48))(*flat)
    return res[0], {n: res[1 + 4 * i:5 + 4 * i] for i, n in enumerate(names)}


_WEIGHTS = ("w_ada", "b_ada", "g_mix", "w_in", "conv_w", "conv_b", "dt_bias", "a_log", "d_skip", "g_ssd", "w_pool",
            "pool_scale", "w_out", "g_mlp", "w_up", "w_down", "g_final")


def _local_step(x2, tg2, mod3, seq, w_in_t, first_token, weights_arrived, weights_later, start_reduce, before_last,
                conv_w_full, sp):
    t, d = x2.shape
    nb = t // seq
    dskip_e = jnp.repeat(sp["d_skip"], SSD_HEAD_DIM, axis=1)
    proj, u1 = _in_proj(x2, mod3, sp["g_mix"], w_in_t, seq, first_token)
    y_ssd, yssm, h_prev, pre = _ssd_fwd(proj, conv_w_full, sp["conv_b"], sp["dt_bias"], sp["a_log"], dskip_e, sp["g_ssd"], nb, seq)
    y_pool, p = _pool_fwd(proj, sp["w_pool"], sp["pool_scale"], nb, seq, weights_arrived(y_ssd))
    w_out_f, w_up4, w_down4 = weights_later(y_pool)
    w_down_f = w_down4.reshape(D_FF, d)
    h1, o, u2 = _out_proj(y_pool, y_ssd, w_out_f, x2, mod3, sp["g_mlp"], seq)
    a_up = _mlp_up(u2, w_up4)
    d_dn, dh2, sq, gg_final, d_gf = _mlp_down_loss(a_up, w_down_f, h1, mod3, sp["g_final"], tg2, seq)

    gw_down = _tn_matmul(a_up, d_dn, 512, d, "grad_w_down", square_relu=True)
    tok = start_reduce("w_down", gw_down.reshape(N_CHIPS, D_FF // N_CHIPS, d))
    d_a = _mlp_down_bwd(d_dn, w_down4, a_up, tok)
    gw_up4 = _tn_matmul(u2, d_a, 512, d, "grad_w_up", out3=True)
    tok = start_reduce("w_up", gw_up4)
    dh1, d_o, accf, gg_mlp = _mlp_up_bwd(d_a, w_up4, h1, dh2, o, mod3, sp["g_mlp"], seq, tok)
    gw_out = _grad_w_out(y_pool, y_ssd, d_o)
    tok = start_reduce("w_out", gw_out.reshape(N_CHIPS, gw_out.shape[0] // N_CHIPS, d))
    d_ypool, d_yssd = _out_proj_bwd(d_o, w_out_f, tok)
    d_upool, gw_pool, g_ps = _pool_bwd(d_ypool, p, sp["w_pool"], sp["pool_scale"], nb, seq)
    d_z, d_pre, d_udt, gg_ssd, gdsk, ga, gdtb = _ssd_bwd(proj, pre, d_yssd, yssm, h_prev, sp["dt_bias"], sp["a_log"],
                                                        dskip_e, sp["g_ssd"], nb, seq)
    d_uxbc, gconvw, gconvb = _conv_bwd(d_pre, proj, conv_w_full, nb, seq)
    gw_in_t = _grad_w_in_t(d_upool, d_z, d_uxbc, d_udt, u1)
    shard_rows = IN_WIDTH // N_CHIPS
    tok = start_reduce("w_in", jnp.stack([gw_in_t[k * shard_rows:(k + 1) * shard_rows] for k in range(N_CHIPS)]))
    gx, accm, gg_mix = _in_proj_bwd([d_upool, d_z, d_uxbc, d_udt], w_in_t, x2, dh1, mod3, sp["g_mix"], seq, before_last(tok))

    d_mod = jnp.concatenate([accm[:, 0], accm[:, 1], accf[:, 2], accf[:, 0], accf[:, 1], d_gf[:, 0]], 1)
    vec = _pack_vec({"g_mix": gg_mix, "conv_b": gconvb, "g_ssd": gg_ssd, "pool_scale": g_ps, "g_mlp": gg_mlp,
                     "g_final": gg_final, "dt_bias": gdtb, "a_log": ga, "d_skip_lanes": gdsk, "sq_err": sq})
    return gx, d_mod, vec, gw_pool, gconvw


def kernel(x, c, w_ada, b_ada, g_mix, w_in, conv_w, conv_b, dt_bias, a_log, d_skip, g_ssd, w_pool, pool_scale, w_out, g_mlp, w_up, w_down, g_final, loss_target, m_w_ada, m_b_ada, m_g_mix, m_w_in, m_conv_w, m_conv_b, m_dt_bias, m_a_log, m_d_skip, m_g_ssd, m_w_pool, m_pool_scale, m_w_out, m_g_mlp, m_w_up, m_w_down, m_g_final, v_w_ada, v_b_ada, v_g_mix, v_w_in, v_conv_w, v_conv_b, v_dt_bias, v_a_log, v_d_skip, v_g_ssd, v_w_pool, v_pool_scale, v_w_out, v_g_mlp, v_w_up, v_w_down, v_g_final):
    nb, seq, d = x.shape
    t = nb * seq
    xi, yi, ci = _mesh_pos()
    chip = 2 * xi + yi
    me = 4 * xi + 2 * yi + ci
    ada_cols = w_ada.shape[2]
    conv_cols = conv_w.shape[2]
    in_cols = w_in.shape[2]
    w_in_s, m_w_in_s, v_w_in_s = w_in[0].T, m_w_in[0].T, v_w_in[0].T

    c_send, c_recv, c_src, c_land, c_token = _exchange_start([c, conv_w[0]], ALL_PEERS, "cond_start")
    w_in_b = w_in_s.astype(BF)
    i_send, i_recv, i_src, i_land, in_token = _ici_start(
        [w_in_b], [jax.ShapeDtypeStruct((N_CHIPS,) + w_in_b.shape, BF)], _gather_sent, _gather_landing, "gather_start_w_in",
        after=c_token)
    c_own, c_got = _exchange_wait(c_send, c_recv, c_src, c_land, ALL_PEERS, in_token, "cond_wait")
    c8, convw8 = [lax.dynamic_update_slice(got, mine[None], (me,) + (0,) * mine.ndim) for got, mine in zip(c_got, c_own)]
    c_all = c8.reshape(N_DEV * nb, d)
    conv_w_full = convw8[0::2].transpose(1, 0, 2).reshape(4, N_CHIPS * conv_cols)
    b_shard = lax.dynamic_slice(b_ada, (0, chip * ada_cols), (1, ada_cols))
    mod_part, c_act = _ada_mod(c_all, w_ada[0], b_shard, in_token)
    mod_rows = mod_part.reshape(N_DEV, nb, ada_cols)
    m_send, m_recv, m_src, m_land, _ = _ici_start(
        [mod_rows], [jax.ShapeDtypeStruct((N_CHIPS, nb, ada_cols), F32)], _mod_sent, _mod_landing, "mod_start", after=mod_part)

    later = [w_out[0].astype(BF), w_up[0].astype(BF), w_down[0].astype(BF)]
    in_shard, in_land = _ici_wait(i_send, i_recv, i_src, i_land, [m_src[0]] + later, _gather_sent, _gather_landing,
                                  "gather_wait_w_in")
    (w_in4,) = _gather_finish(in_land, in_shard)
    w_in_t = w_in4.reshape(N_CHIPS * in_cols, d)
    mod_mine, mod_land = _ici_wait(m_send, m_recv, m_src, m_land, w_in_t, _mod_sent, _mod_landing, "mod_wait")
    mod_own = lax.dynamic_slice(mod_mine[0], (me, 0, 0), (1, nb, ada_cols))
    mod4 = lax.dynamic_update_slice(mod_land[0], mod_own, (chip, 0, 0))
    mod3 = mod4.transpose(1, 0, 2).reshape(nb, N_MOD, d)
    g_send, g_recv, g_src, g_land, first_token = _ici_start(
        later, [jax.ShapeDtypeStruct((N_CHIPS,) + s.shape, BF) for s in later], _gather_sent, _gather_landing, "gather_start",
        after=w_in4)

    def weights_arrived(after):
        shards, lands = _ici_wait(g_send, g_recv, g_src, g_land, after, _gather_sent, _gather_landing, "gather_wait")
        pending["forward"] = _forward_start(lands, "forward_start") + (shards,)
        return pending["forward"][3]

    def weights_later(after):
        f_send, f_recv, f_land, _, shards = pending["forward"]
        lands = _forward_wait(f_send, f_recv, f_land, after, "forward_wait")
        w_out4, w_up4, w_down4 = [lax.dynamic_update_slice(land, shard[None], (chip, 0, 0)) for land, shard in zip(lands, shards)]
        return w_out4.reshape(N_CHIPS * w_out.shape[1], d), w_up4, w_down4

    pending = {}

    def start_reduce(name, grad4):
        pending[name] = _reduce_start(grad4, "reduce_start_" + name)
        return pending[name][4]

    pos = jnp.stack([ci, chip, me]).astype(jnp.int32)
    early = ("w_out", "w_up", "w_down")

    def summed_half(name, after):
        r_send, r_recv, r_src, r_land, _ = pending[name]
        own, recv = _reduce_wait(r_send, r_recv, r_src, r_land, after, "reduce_wait_" + name)
        return _sum_eight(recv, own, pos)

    def before_last(token):
        pending["early_halves"] = _exchange_start([summed_half(n, token) for n in early], SIBLING, "halves_start_early")
        return pending["early_halves"][4]

    sp = dict(g_mix=g_mix, conv_b=conv_b, dt_bias=dt_bias, a_log=a_log, d_skip=d_skip, g_ssd=g_ssd,
              w_pool=w_pool[0], pool_scale=pool_scale, g_mlp=g_mlp, g_final=g_final.reshape(1, d))
    gx, d_mod, vec, gw_pool, gconvw = _local_step(
        x.reshape(t, d), loss_target.reshape(t, d), mod3, seq, w_in_t, first_token, weights_arrived, weights_later, start_reduce,
        before_last, conv_w_full, sp)

    small_parts = [vec, gw_pool.reshape(4 * POOL_GROUP, POOL_GROUP).astype(BF), gconvw, d_mod]
    s_send, s_recv, s_src, s_land, s_token = _exchange_start(small_parts, ALL_PEERS, "small_start")
    h_send, h_recv, h_src, h_land, h_token = _exchange_start([summed_half("w_in", s_token)], SIBLING, "halves_start")
    e_send, e_recv, e_src, e_land, _ = pending["early_halves"]
    e_own, e_got = _exchange_wait(e_send, e_recv, e_src, e_land, SIBLING, h_token, "halves_wait_early")
    res = {}
    for i, (n, w, m, v) in enumerate((("w_out", w_out, m_w_out, v_w_out), ("w_up", w_up, m_w_up, v_w_up),
                                      ("w_down", w_down, m_w_down, v_w_down))):
        g, dl, m2, v2 = _adam_big(e_own[i], e_got[i], w[0], m[0], v[0], pos)
        res[n] = (g[None], dl[None], m2[None], v2[None])

    s_own, s_got = _exchange_wait(s_send, s_recv, s_src, s_land, ALL_PEERS, res["w_down"][1], "small_wait")
    vec8, wpool8, convw8g, dmod8 = [lax.dynamic_update_slice(got, mine[None], (me,) + (0,) * mine.ndim)
                                    for got, mine in zip(s_got, s_own)]
    convw8s = lax.dynamic_slice(convw8g, (0, 0, chip * conv_cols), (N_DEV, 8, conv_cols))
    m_in = dict(b_ada=m_b_ada, g_mix=m_g_mix, conv_w=m_conv_w[0], conv_b=m_conv_b, dt_bias=m_dt_bias, a_log=m_a_log,
                d_skip=m_d_skip, g_ssd=m_g_ssd, w_pool=m_w_pool.reshape(4 * POOL_GROUP, POOL_GROUP), pool_scale=m_pool_scale,
                g_mlp=m_g_mlp, g_final=m_g_final.reshape(1, d))
    v_in = dict(b_ada=v_b_ada, g_mix=v_g_mix, conv_w=v_conv_w[0], conv_b=v_conv_b, dt_bias=v_dt_bias, a_log=v_a_log,
                d_skip=v_d_skip, g_ssd=v_g_ssd, w_pool=v_w_pool.reshape(4 * POOL_GROUP, POOL_GROUP), pool_scale=v_pool_scale,
                g_mlp=v_g_mlp, g_final=v_g_final.reshape(1, d))
    w_small = dict(sp, b_ada=b_ada, conv_w=conv_w[0], w_pool=w_pool.reshape(4 * POOL_GROUP, POOL_GROUP))
    loss_row, small = _small_adam(vec8, wpool8, convw8s, dmod8, {n: (w_small[n], m_in[n], v_in[n]) for n in _SMALL_PARAMS})

    dmod_all = dmod8.reshape(N_DEV * nb, N_CHIPS * ada_cols)
    dmod_cols = lax.dynamic_slice(dmod_all, (0, chip * ada_cols), (N_DEV * nb, ada_cols))
    res.update({n: tuple(r.reshape(w.shape) for r in small[n])
                for n, w in (("b_ada", b_ada), ("g_mix", g_mix), ("conv_w", conv_w), ("conv_b", conv_b), ("dt_bias", dt_bias),
                             ("a_log", a_log), ("d_skip", d_skip), ("g_ssd", g_ssd), ("w_pool", w_pool),
                             ("pool_scale", pool_scale), ("g_mlp", g_mlp), ("g_final", g_final))})
    g_ada, d_ada, m_ada, v_ada = _adam_ada(c_act.T.astype(BF), dmod_cols, w_ada[0], m_w_ada[0], v_w_ada[0])
    res["w_ada"] = (g_ada[None], d_ada[None], m_ada[None], v_ada[None])
    h_own, h_got = _exchange_wait(h_send, h_recv, h_src, h_land, SIBLING, g_ada, "halves_wait")
    rows3 = lambda a: jnp.transpose(a, (2, 0, 1))
    res["w_in"] = tuple(jnp.transpose(r, (1, 2, 0))
                        for r in _adam_rows(h_own[0], h_got[0], rows3(w_in), rows3(m_w_in), rows3(v_w_in), pos))

    loss = loss_row[0, 0]
    return (loss, gx.reshape(nb, seq, d), *[res[n][0] for n in _WEIGHTS], *[res[n][1] for n in _WEIGHTS],
            *[res[n][2] for n in _WEIGHTS], *[res[n][3] for n in _WEIGHTS])
```
